```python
import jax
import jax.numpy as jnp
from jax import lax
import numpy as np

D_MODEL = 1024
BATCH = 8
SEQ = 4096
DEPTH = 1

N_META = 16
HEAD_DIM = 64
ATTN_Q_HEADS = 8
ATTN_KV_HEADS = 2
ATTN_GROUP = ATTN_Q_HEADS // ATTN_KV_HEADS
WINDOW = 128
BLOCK = 128
ROPE_THETA = 500000.0
ROPE_DIM = HEAD_DIM // 4
RWKV_HEADS = 8
RWKV_HEAD = 64
RWKV_DIM = RWKV_HEADS * RWKV_HEAD
DECAY_LORA = 64
AAA_LORA = 64
GATE_LORA = 160
RWKV_LN_EPS = 64e-5
D_FF = -(-8 * D_MODEL // (3 * 256)) * 256
Q_W = ATTN_Q_HEADS * HEAD_DIM
KV_W = ATTN_KV_HEADS * HEAD_DIM
ATTN_PROJ = Q_W + 2 * KV_W
RWKV_PROJ = 3 * RWKV_DIM + DECAY_LORA + AAA_LORA + GATE_LORA
D_IN = ATTN_PROJ + RWKV_PROJ + 2 * D_MODEL
RMS_EPS = 1e-6
NEG_INF = -1e30

kernel_name = 'hybrid_swa_sink_rwkv7_gated_block'


def rms_norm(x, g):
    xf = x.astype(jnp.float32)
    y = xf * lax.rsqrt(jnp.mean(xf * xf, axis=-1, keepdims=True) + RMS_EPS)
    return (y * g.astype(jnp.float32)).astype(x.dtype)


def partial_rope(t, pos):
    half = ROPE_DIM // 2
    inv_freq = jnp.power(jnp.float32(ROPE_THETA), -jnp.arange(half, dtype=jnp.float32) * (2.0 / ROPE_DIM))
    ang = pos.astype(jnp.float32)[:, None] * inv_freq[None, :]
    cos = jnp.cos(ang)[None, :, None, :]
    sin = jnp.sin(ang)[None, :, None, :]
    tf = t.astype(jnp.float32)
    t1 = tf[..., :half]
    t2 = tf[..., half:ROPE_DIM]
    out = jnp.concatenate([t1 * cos - t2 * sin, t2 * cos + t1 * sin, tf[..., ROPE_DIM:]], axis=-1)
    return out.astype(t.dtype)


def sliding_window_sink_attention(q, k, v, sinks):
    B, L = q.shape[0], q.shape[1]
    pad = BLOCK - N_META
    n_blk = (L + pad) // BLOCK

    def blockify(t):
        t = jnp.pad(t, ((0, 0), (pad, 0), (0, 0), (0, 0)))
        return t.reshape(B, n_blk, BLOCK, t.shape[2], t.shape[3])

    def prev_block(t):
        return jnp.pad(t, ((0, 0), (1, 0), (0, 0), (0, 0), (0, 0)))[:, :-1]

    qb = blockify(q).reshape(B, n_blk, BLOCK, ATTN_KV_HEADS, ATTN_GROUP, HEAD_DIM)
    kb = blockify(k)
    vb = blockify(v)
    k_band = jnp.concatenate([prev_block(kb), kb], axis=2)
    v_band = jnp.concatenate([prev_block(vb), vb], axis=2)
    k_meta = k[:, :N_META]
    v_meta = v[:, :N_META]

    scale = HEAD_DIM ** -0.5
    s_band = jnp.einsum('bnqhgd,bnkhd->bhgnqk', qb, k_band).astype(jnp.float32) * scale
    s_meta = jnp.einsum('bnqhgd,bmhd->bhgnqm', qb, k_meta).astype(jnp.float32) * scale

    q_pos = jnp.arange(n_blk * BLOCK).reshape(n_blk, BLOCK) - pad
    k_pos = (jnp.arange(n_blk)[:, None] - 1) * BLOCK + jnp.arange(2 * BLOCK)[None, :] - pad
    qp = q_pos[:, :, None]
    kp = k_pos[:, None, :]
    band_ok = (kp >= N_META) & (kp <= qp) & (qp - kp < WINDOW)
    meta_ok = jnp.arange(N_META)[None, None, :] <= qp
    s_band = jnp.where(band_ok[None, None, None], s_band, NEG_INF)
    s_meta = jnp.where(meta_ok[None, None, None], s_meta, NEG_INF)
    sink = sinks.astype(jnp.float32).reshape(ATTN_KV_HEADS, ATTN_GROUP)[None, :, :, None, None, None]
    sink = jnp.broadcast_to(sink, s_band.shape[:-1] + (1,))

    probs = jax.nn.softmax(jnp.concatenate([s_meta, s_band, sink], axis=-1), axis=-1)
    p_meta = probs[..., :N_META].astype(v.dtype)
    p_band = probs[..., N_META:N_META + 2 * BLOCK].astype(v.dtype)
    out = (jnp.einsum('bhgnqm,bmhd->bnqhgd', p_meta, v_meta)
           + jnp.einsum('bhgnqk,bnkhd->bnqhgd', p_band, v_band))
    return out.reshape(B, n_blk * BLOCK, Q_W)[:, pad:]


def token_shift(t):
    return jnp.pad(t, ((0, 0), (1, 0), (0, 0)))[:, :-1]


def wkv7_scan(r, decay, k, v, aa, bb):
    B, L, H, N = r.shape

    def step(S, inp):
        r_t, w_t, k_t, v_t, a_t, b_t = inp
        sa = jnp.einsum('bhvk,bhk->bhv', S, a_t)
        S = S * w_t[:, :, None, :] + sa[..., None] * b_t[:, :, None, :] + v_t[..., None] * k_t[:, :, None, :]
        y = jnp.einsum('bhvk,bhk->bhv', S, r_t)
        return S, y

    xs = (jnp.moveaxis(r, 1, 0), jnp.moveaxis(decay, 1, 0), jnp.moveaxis(k, 1, 0),
          jnp.moveaxis(v, 1, 0), jnp.moveaxis(aa, 1, 0), jnp.moveaxis(bb, 1, 0))
    S0 = jnp.zeros((B, H, N, N), jnp.float32)
    _, ys = lax.scan(step, S0, xs)
    return jnp.moveaxis(ys, 0, 1)


def rwkv7_time_mix(p, mix, w0, w2, a0, a2, g2, k_k, k_a, r_k, ln_w, ln_b):
    B, L = p.shape[0], p.shape[1]
    f32 = jnp.float32
    pf = p.astype(f32)
    pf = pf + (token_shift(pf) - pf) * mix.astype(f32)
    o1, o2, o3 = RWKV_DIM, 2 * RWKV_DIM, 3 * RWKV_DIM
    o4 = o3 + DECAY_LORA
    o5 = o4 + AAA_LORA
    r = pf[..., :o1]
    k = pf[..., o1:o2]
    v = pf[..., o2:o3]
    dw = pf[..., o3:o4]
    da = pf[..., o4:o5]
    dg = pf[..., o5:]
    w = -jax.nn.softplus(-(w0.astype(f32) + jnp.tanh(dw) @ w2.astype(f32))) - 0.5
    a = jax.nn.sigmoid(a0.astype(f32) + da @ a2.astype(f32))
    g = jax.nn.sigmoid(dg) @ g2.astype(f32)
    hs = (B, L, RWKV_HEADS, RWKV_HEAD)
    kk = (k * k_k.astype(f32)).reshape(hs)
    kk = kk / jnp.maximum(jnp.sqrt(jnp.sum(kk * kk, axis=-1, keepdims=True)), 1e-12)
    k = k * (1.0 + (a - 1.0) * k_a.astype(f32))
    r = r.reshape(hs)
    k = k.reshape(hs)
    v = v.reshape(hs)
    a = a.reshape(hs)
    decay = jnp.exp(-jnp.exp(w)).reshape(hs)
    y = wkv7_scan(r, decay, k, v, -kk, kk * a)
    mean = jnp.mean(y, axis=-1, keepdims=True)
    var = jnp.mean(jnp.square(y - mean), axis=-1, keepdims=True)
    y = ((y - mean) * lax.rsqrt(var + RWKV_LN_EPS) * ln_w.astype(f32).reshape(RWKV_HEADS, RWKV_HEAD)
         + ln_b.astype(f32).reshape(RWKV_HEADS, RWKV_HEAD))
    y = y + jnp.sum(r * k * r_k.astype(f32), axis=-1, keepdims=True) * v
    return (y.reshape(B, L, RWKV_DIM) * g).astype(p.dtype)


def _fwd_setup_inputs(seed: int = 0) -> dict:
    key = jax.random.key(seed)
    ks = jax.random.split(key, 25)
    f32 = jnp.float32

    def nrm(k, shape, scale):
        return jax.random.normal(k, shape, f32) * scale

    def unif(k, shape, lo, hi):
        return jax.random.uniform(k, shape, f32, lo, hi)

    Dp = DEPTH
    return {
        'x': nrm(ks[0], (BATCH, SEQ, D_MODEL), 1.0),
        'meta_tokens': nrm(ks[1], (N_META, D_MODEL), 1.0),
        'norm_mix_g': 1.0 + nrm(ks[2], (Dp, D_MODEL), 0.02),
        'w_in': nrm(ks[3], (Dp, D_MODEL, D_IN), D_MODEL ** -0.5),
        'b_in': nrm(ks[4], (Dp, D_IN), 0.02),
        'attn_sinks': nrm(ks[5], (Dp, ATTN_Q_HEADS), 1.0),
        'rwkv_mix': unif(ks[6], (Dp, RWKV_PROJ), 0.0, 1.0),
        'rwkv_w0': unif(ks[7], (Dp, RWKV_DIM), -6.0, -1.0),
        'rwkv_w2': nrm(ks[8], (Dp, DECAY_LORA, RWKV_DIM), 0.1 * DECAY_LORA ** -0.5),
        'rwkv_a0': nrm(ks[9], (Dp, RWKV_DIM), 0.1),
        'rwkv_a2': nrm(ks[10], (Dp, AAA_LORA, RWKV_DIM), 0.5 * AAA_LORA ** -0.5),
        'rwkv_g2': nrm(ks[11], (Dp, GATE_LORA, RWKV_DIM), GATE_LORA ** -0.5),
        'rwkv_k_k': 0.85 + nrm(ks[12], (Dp, RWKV_DIM), 0.02),
        'rwkv_k_a': 1.0 + nrm(ks[13], (Dp, RWKV_DIM), 0.02),
        'rwkv_r_k': -0.04 + nrm(ks[14], (Dp, RWKV_HEADS, RWKV_HEAD), 0.02),
        'rwkv_ln_w': 1.0 + nrm(ks[15], (Dp, RWKV_DIM), 0.02),
        'rwkv_ln_b': nrm(ks[16], (Dp, RWKV_DIM), 0.02),
        'w_br_attn': nrm(ks[17], (Dp, Q_W, D_MODEL), Q_W ** -0.5),
        'w_br_rwkv': nrm(ks[18], (Dp, RWKV_DIM, D_MODEL), RWKV_DIM ** -0.5),
        'w_o': nrm(ks[19], (Dp, D_MODEL, D_MODEL), D_MODEL ** -0.5),
        'norm_ffn_g': 1.0 + nrm(ks[20], (Dp, D_MODEL), 0.02),
        'w_ffn_gate': nrm(ks[21], (Dp, D_MODEL, D_FF), D_MODEL ** -0.5),
        'w_ffn_up': nrm(ks[22], (Dp, D_MODEL, D_FF), D_MODEL ** -0.5),
        'w_ffn_down': nrm(ks[23], (Dp, D_FF, D_MODEL), D_FF ** -0.5),
        'norm_final_g': 1.0 + nrm(ks[24], (D_MODEL,), 0.02),
    }


def _fwd_reference(x, meta_tokens, norm_mix_g, w_in, b_in, attn_sinks, rwkv_mix, rwkv_w0, rwkv_w2,
              rwkv_a0, rwkv_a2, rwkv_g2, rwkv_k_k, rwkv_k_a, rwkv_r_k, rwkv_ln_w, rwkv_ln_b,
              w_br_attn, w_br_rwkv, w_o, norm_ffn_g, w_ffn_gate, w_ffn_up, w_ffn_down,
              norm_final_g):
    B = x.shape[0]
    meta = jnp.broadcast_to(meta_tokens.astype(x.dtype)[None], (B, N_META, D_MODEL))
    h = jnp.concatenate([meta, x], axis=1)
    L = h.shape[1]
    pos = jnp.arange(L, dtype=jnp.int32)
    for layer in range(DEPTH):
        u = rms_norm(h, norm_mix_g[layer])
        proj = u @ w_in[layer] + b_in[layer]
        p_attn = proj[..., :ATTN_PROJ]
        p_rwkv = proj[..., ATTN_PROJ:ATTN_PROJ + RWKV_PROJ]
        gates = jax.nn.sigmoid(proj[..., ATTN_PROJ + RWKV_PROJ:].astype(jnp.float32)).astype(h.dtype)
        q = p_attn[..., :Q_W].reshape(B, L, ATTN_Q_HEADS, HEAD_DIM)
        k = p_attn[..., Q_W:Q_W + KV_W].reshape(B, L, ATTN_KV_HEADS, HEAD_DIM)
        v = p_attn[..., Q_W + KV_W:].reshape(B, L, ATTN_KV_HEADS, HEAD_DIM)
        q = partial_rope(q, pos)
        k = partial_rope(k, pos)
        y_attn = sliding_window_sink_attention(q, k, v, attn_sinks[layer])
        y_rwkv = rwkv7_time_mix(p_rwkv, rwkv_mix[layer], rwkv_w0[layer], rwkv_w2[layer],
                                rwkv_a0[layer], rwkv_a2[layer], rwkv_g2[layer], rwkv_k_k[layer],
                                rwkv_k_a[layer], rwkv_r_k[layer], rwkv_ln_w[layer],
                                rwkv_ln_b[layer])
        merged = (gates[..., :D_MODEL] * (y_attn @ w_br_attn[layer])
                  + gates[..., D_MODEL:] * (y_rwkv @ w_br_rwkv[layer]))
        h = h + merged @ w_o[layer]
        f = rms_norm(h, norm_ffn_g[layer])
        h = h + (jax.nn.silu(f @ w_ffn_gate[layer]) * (f @ w_ffn_up[layer])) @ w_ffn_down[layer]
    return rms_norm(h, norm_final_g)[:, N_META:]


import jax as _jax
import jax.numpy as _jnp

TWIN_FORMAT = 'train_step'
FWD_PARAMS = ['x', 'meta_tokens', 'norm_mix_g', 'w_in', 'b_in', 'attn_sinks', 'rwkv_mix', 'rwkv_w0', 'rwkv_w2', 'rwkv_a0', 'rwkv_a2', 'rwkv_g2', 'rwkv_k_k', 'rwkv_k_a', 'rwkv_r_k', 'rwkv_ln_w', 'rwkv_ln_b', 'w_br_attn', 'w_br_rwkv', 'w_o', 'norm_ffn_g', 'w_ffn_gate', 'w_ffn_up', 'w_ffn_down', 'norm_final_g']
TWIN_WEIGHTS = ['meta_tokens', 'norm_mix_g', 'w_in', 'b_in', 'attn_sinks', 'rwkv_mix', 'rwkv_w0', 'rwkv_w2', 'rwkv_a0', 'rwkv_a2', 'rwkv_g2', 'rwkv_k_k', 'rwkv_k_a', 'rwkv_r_k', 'rwkv_ln_w', 'rwkv_ln_b', 'w_br_attn', 'w_br_rwkv', 'w_o', 'norm_ffn_g', 'w_ffn_gate', 'w_ffn_up', 'w_ffn_down', 'norm_final_g']
TWIN_DIFF_INPUT = 'x'
TWIN_INPUTS = ['x', 'meta_tokens', 'norm_mix_g', 'w_in', 'b_in', 'attn_sinks', 'rwkv_mix', 'rwkv_w0', 'rwkv_w2', 'rwkv_a0', 'rwkv_a2', 'rwkv_g2', 'rwkv_k_k', 'rwkv_k_a', 'rwkv_r_k', 'rwkv_ln_w', 'rwkv_ln_b', 'w_br_attn', 'w_br_rwkv', 'w_o', 'norm_ffn_g', 'w_ffn_gate', 'w_ffn_up', 'w_ffn_down', 'norm_final_g', 'loss_target', 'm_meta_tokens', 'm_norm_mix_g', 'm_w_in', 'm_b_in', 'm_attn_sinks', 'm_rwkv_mix', 'm_rwkv_w0', 'm_rwkv_w2', 'm_rwkv_a0', 'm_rwkv_a2', 'm_rwkv_g2', 'm_rwkv_k_k', 'm_rwkv_k_a', 'm_rwkv_r_k', 'm_rwkv_ln_w', 'm_rwkv_ln_b', 'm_w_br_attn', 'm_w_br_rwkv', 'm_w_o', 'm_norm_ffn_g', 'm_w_ffn_gate', 'm_w_ffn_up', 'm_w_ffn_down', 'm_norm_final_g', 'v_meta_tokens', 'v_norm_mix_g', 'v_w_in', 'v_b_in', 'v_attn_sinks', 'v_rwkv_mix', 'v_rwkv_w0', 'v_rwkv_w2', 'v_rwkv_a0', 'v_rwkv_a2', 'v_rwkv_g2', 'v_rwkv_k_k', 'v_rwkv_k_a', 'v_rwkv_r_k', 'v_rwkv_ln_w', 'v_rwkv_ln_b', 'v_w_br_attn', 'v_w_br_rwkv', 'v_w_o', 'v_norm_ffn_g', 'v_w_ffn_gate', 'v_w_ffn_up', 'v_w_ffn_down', 'v_norm_final_g']
TWIN_OUTPUTS = ['loss', 'grad_x', 'grad_meta_tokens', 'grad_norm_mix_g', 'grad_w_in', 'grad_b_in', 'grad_attn_sinks', 'grad_rwkv_mix', 'grad_rwkv_w0', 'grad_rwkv_w2', 'grad_rwkv_a0', 'grad_rwkv_a2', 'grad_rwkv_g2', 'grad_rwkv_k_k', 'grad_rwkv_k_a', 'grad_rwkv_r_k', 'grad_rwkv_ln_w', 'grad_rwkv_ln_b', 'grad_w_br_attn', 'grad_w_br_rwkv', 'grad_w_o', 'grad_norm_ffn_g', 'grad_w_ffn_gate', 'grad_w_ffn_up', 'grad_w_ffn_down', 'grad_norm_final_g', 'delta_meta_tokens', 'delta_norm_mix_g', 'delta_w_in', 'delta_b_in', 'delta_attn_sinks', 'delta_rwkv_mix', 'delta_rwkv_w0', 'delta_rwkv_w2', 'delta_rwkv_a0', 'delta_rwkv_a2', 'delta_rwkv_g2', 'delta_rwkv_k_k', 'delta_rwkv_k_a', 'delta_rwkv_r_k', 'delta_rwkv_ln_w', 'delta_rwkv_ln_b', 'delta_w_br_attn', 'delta_w_br_rwkv', 'delta_w_o', 'delta_norm_ffn_g', 'delta_w_ffn_gate', 'delta_w_ffn_up', 'delta_w_ffn_down', 'delta_norm_final_g', 'new_m_meta_tokens', 'new_m_norm_mix_g', 'new_m_w_in', 'new_m_b_in', 'new_m_attn_sinks', 'new_m_rwkv_mix', 'new_m_rwkv_w0', 'new_m_rwkv_w2', 'new_m_rwkv_a0', 'new_m_rwkv_a2', 'new_m_rwkv_g2', 'new_m_rwkv_k_k', 'new_m_rwkv_k_a', 'new_m_rwkv_r_k', 'new_m_rwkv_ln_w', 'new_m_rwkv_ln_b', 'new_m_w_br_attn', 'new_m_w_br_rwkv', 'new_m_w_o', 'new_m_norm_ffn_g', 'new_m_w_ffn_gate', 'new_m_w_ffn_up', 'new_m_w_ffn_down', 'new_m_norm_final_g', 'new_v_meta_tokens', 'new_v_norm_mix_g', 'new_v_w_in', 'new_v_b_in', 'new_v_attn_sinks', 'new_v_rwkv_mix', 'new_v_rwkv_w0', 'new_v_rwkv_w2', 'new_v_rwkv_a0', 'new_v_rwkv_a2', 'new_v_rwkv_g2', 'new_v_rwkv_k_k', 'new_v_rwkv_k_a', 'new_v_rwkv_r_k', 'new_v_rwkv_ln_w', 'new_v_rwkv_ln_b', 'new_v_w_br_attn', 'new_v_w_br_rwkv', 'new_v_w_o', 'new_v_norm_ffn_g', 'new_v_w_ffn_gate', 'new_v_w_ffn_up', 'new_v_w_ffn_down', 'new_v_norm_final_g']
TWIN_LEAF_KINDS = {'loss': 'loss', 'grad_x': 'grad_x', 'grad_meta_tokens': 'grad_w', 'grad_norm_mix_g': 'grad_w', 'grad_w_in': 'grad_w', 'grad_b_in': 'grad_w', 'grad_attn_sinks': 'grad_w', 'grad_rwkv_mix': 'grad_w', 'grad_rwkv_w0': 'grad_w', 'grad_rwkv_w2': 'grad_w', 'grad_rwkv_a0': 'grad_w', 'grad_rwkv_a2': 'grad_w', 'grad_rwkv_g2': 'grad_w', 'grad_rwkv_k_k': 'grad_w', 'grad_rwkv_k_a': 'grad_w', 'grad_rwkv_r_k': 'grad_w', 'grad_rwkv_ln_w': 'grad_w', 'grad_rwkv_ln_b': 'grad_w', 'grad_w_br_attn': 'grad_w', 'grad_w_br_rwkv': 'grad_w', 'grad_w_o': 'grad_w', 'grad_norm_ffn_g': 'grad_w', 'grad_w_ffn_gate': 'grad_w', 'grad_w_ffn_up': 'grad_w', 'grad_w_ffn_down': 'grad_w', 'grad_norm_final_g': 'grad_w', 'delta_meta_tokens': 'delta_w', 'delta_norm_mix_g': 'delta_w', 'delta_w_in': 'delta_w', 'delta_b_in': 'delta_w', 'delta_attn_sinks': 'delta_w', 'delta_rwkv_mix': 'delta_w', 'delta_rwkv_w0': 'delta_w', 'delta_rwkv_w2': 'delta_w', 'delta_rwkv_a0': 'delta_w', 'delta_rwkv_a2': 'delta_w', 'delta_rwkv_g2': 'delta_w', 'delta_rwkv_k_k': 'delta_w', 'delta_rwkv_k_a': 'delta_w', 'delta_rwkv_r_k': 'delta_w', 'delta_rwkv_ln_w': 'delta_w', 'delta_rwkv_ln_b': 'delta_w', 'delta_w_br_attn': 'delta_w', 'delta_w_br_rwkv': 'delta_w', 'delta_w_o': 'delta_w', 'delta_norm_ffn_g': 'delta_w', 'delta_w_ffn_gate': 'delta_w', 'delta_w_ffn_up': 'delta_w', 'delta_w_ffn_down': 'delta_w', 'delta_norm_final_g': 'delta_w', 'new_m_meta_tokens': 'new_m', 'new_m_norm_mix_g': 'new_m', 'new_m_w_in': 'new_m', 'new_m_b_in': 'new_m', 'new_m_attn_sinks': 'new_m', 'new_m_rwkv_mix': 'new_m', 'new_m_rwkv_w0': 'new_m', 'new_m_rwkv_w2': 'new_m', 'new_m_rwkv_a0': 'new_m', 'new_m_rwkv_a2': 'new_m', 'new_m_rwkv_g2': 'new_m', 'new_m_rwkv_k_k': 'new_m', 'new_m_rwkv_k_a': 'new_m', 'new_m_rwkv_r_k': 'new_m', 'new_m_rwkv_ln_w': 'new_m', 'new_m_rwkv_ln_b': 'new_m', 'new_m_w_br_attn': 'new_m', 'new_m_w_br_rwkv': 'new_m', 'new_m_w_o': 'new_m', 'new_m_norm_ffn_g': 'new_m', 'new_m_w_ffn_gate': 'new_m', 'new_m_w_ffn_up': 'new_m', 'new_m_w_ffn_down': 'new_m', 'new_m_norm_final_g': 'new_m', 'new_v_meta_tokens': 'new_v', 'new_v_norm_mix_g': 'new_v', 'new_v_w_in': 'new_v', 'new_v_b_in': 'new_v', 'new_v_attn_sinks': 'new_v', 'new_v_rwkv_mix': 'new_v', 'new_v_rwkv_w0': 'new_v', 'new_v_rwkv_w2': 'new_v', 'new_v_rwkv_a0': 'new_v', 'new_v_rwkv_a2': 'new_v', 'new_v_rwkv_g2': 'new_v', 'new_v_rwkv_k_k': 'new_v', 'new_v_rwkv_k_a': 'new_v', 'new_v_rwkv_r_k': 'new_v', 'new_v_rwkv_ln_w': 'new_v', 'new_v_rwkv_ln_b': 'new_v', 'new_v_w_br_attn': 'new_v', 'new_v_w_br_rwkv': 'new_v', 'new_v_w_o': 'new_v', 'new_v_norm_ffn_g': 'new_v', 'new_v_w_ffn_gate': 'new_v', 'new_v_w_ffn_up': 'new_v', 'new_v_w_ffn_down': 'new_v', 'new_v_norm_final_g': 'new_v'}


def _forward(args):
    return _fwd_reference(*[args[k] for k in FWD_PARAMS])


def _output_shape():
    out = _jax.eval_shape(lambda: _forward(_fwd_setup_inputs(0)))
    return out.shape, out.dtype

N_MICROBATCH = 1
ADAM_LR = 0.001
ADAM_B1 = 0.9
ADAM_B2 = 0.999
ADAM_EPS = 1e-08
ADAM_WD = 0.01
ADAM_STEP = 10
PER_EXAMPLE_BATCH_AXIS = {'x': 0, 'loss_target': 0}
SHARED_INPUTS = []
_WEIGHT_DTYPES = {'meta_tokens': _jnp.float32, 'norm_mix_g': _jnp.float32, 'w_in': _jnp.float32, 'b_in': _jnp.float32, 'attn_sinks': _jnp.float32, 'rwkv_mix': _jnp.float32, 'rwkv_w0': _jnp.float32, 'rwkv_w2': _jnp.float32, 'rwkv_a0': _jnp.float32, 'rwkv_a2': _jnp.float32, 'rwkv_g2': _jnp.float32, 'rwkv_k_k': _jnp.float32, 'rwkv_k_a': _jnp.float32, 'rwkv_r_k': _jnp.float32, 'rwkv_ln_w': _jnp.float32, 'rwkv_ln_b': _jnp.float32, 'w_br_attn': _jnp.float32, 'w_br_rwkv': _jnp.float32, 'w_o': _jnp.float32, 'norm_ffn_g': _jnp.float32, 'w_ffn_gate': _jnp.float32, 'w_ffn_up': _jnp.float32, 'w_ffn_down': _jnp.float32, 'norm_final_g': _jnp.float32}
MOMENT_SCALE = {'meta_tokens': 6.574496e-03, 'norm_mix_g': 1.115560e-01, 'w_in': 5.022348e-02, 'b_in': 7.759106e-02, 'attn_sinks': 4.129134e-03, 'rwkv_mix': 1.210959e-01, 'rwkv_w0': 3.136847e-02, 'rwkv_w2': 3.481202e-03, 'rwkv_a0': 3.298211e-02, 'rwkv_a2': 3.064119e-02, 'rwkv_g2': 7.903286e-02, 'rwkv_k_k': 1.127347e-01, 'rwkv_k_a': 8.576206e-02, 'rwkv_r_k': 1.739529e-01, 'rwkv_ln_w': 7.677464e-02, 'rwkv_ln_b': 7.362360e-02, 'w_br_attn': 1.732809e-02, 'w_br_rwkv': 5.403220e-02, 'w_o': 5.685608e-02, 'norm_ffn_g': 1.298560e-01, 'w_ffn_gate': 5.489151e-02, 'w_ffn_up': 5.305953e-02, 'w_ffn_down': 8.840413e-02, 'norm_final_g': 3.201866e+01}


def _to_microbatches(a, axis):
    t = _jnp.moveaxis(a, axis, 0)
    t = t.reshape((N_MICROBATCH, t.shape[0] // N_MICROBATCH) + t.shape[1:])
    return _jnp.moveaxis(t, 1, axis + 1)


def setup_inputs(seed: int = 0) -> dict:
    inp = _fwd_setup_inputs(seed)
    key = _jax.random.fold_in(_jax.random.key(seed), 7919)
    shape, _ = _output_shape()
    out = dict(inp)
    out["loss_target"] = _jax.random.normal(_jax.random.fold_in(key, 0), shape, _jnp.float32)
    for i, name in enumerate(TWIN_WEIGHTS):
        w = inp[name].astype(_jnp.float32)
        if MOMENT_SCALE is None:
            s = _jnp.sqrt(_jnp.mean(_jnp.square(w)) + 1e-30)
        else:
            s = MOMENT_SCALE[name]
        km, kv = _jax.random.split(_jax.random.fold_in(key, i + 1))
        out[name] = w
        out["m_" + name] = s * _jax.random.normal(km, w.shape, _jnp.float32)
        out["v_" + name] = (s * s) * _jax.random.uniform(kv, w.shape, _jnp.float32, 0.5, 1.5)
    if N_MICROBATCH > 1:
        for name, axis in PER_EXAMPLE_BATCH_AXIS.items():
            out[name] = _to_microbatches(out[name], axis)
    return {'x': out['x'], 'meta_tokens': out['meta_tokens'], 'norm_mix_g': out['norm_mix_g'], 'w_in': out['w_in'], 'b_in': out['b_in'], 'attn_sinks': out['attn_sinks'], 'rwkv_mix': out['rwkv_mix'], 'rwkv_w0': out['rwkv_w0'], 'rwkv_w2': out['rwkv_w2'], 'rwkv_a0': out['rwkv_a0'], 'rwkv_a2': out['rwkv_a2'], 'rwkv_g2': out['rwkv_g2'], 'rwkv_k_k': out['rwkv_k_k'], 'rwkv_k_a': out['rwkv_k_a'], 'rwkv_r_k': out['rwkv_r_k'], 'rwkv_ln_w': out['rwkv_ln_w'], 'rwkv_ln_b': out['rwkv_ln_b'], 'w_br_attn': out['w_br_attn'], 'w_br_rwkv': out['w_br_rwkv'], 'w_o': out['w_o'], 'norm_ffn_g': out['norm_ffn_g'], 'w_ffn_gate': out['w_ffn_gate'], 'w_ffn_up': out['w_ffn_up'], 'w_ffn_down': out['w_ffn_down'], 'norm_final_g': out['norm_final_g'], 'loss_target': out['loss_target'], 'm_meta_tokens': out['m_meta_tokens'], 'm_norm_mix_g': out['m_norm_mix_g'], 'm_w_in': out['m_w_in'], 'm_b_in': out['m_b_in'], 'm_attn_sinks': out['m_attn_sinks'], 'm_rwkv_mix': out['m_rwkv_mix'], 'm_rwkv_w0': out['m_rwkv_w0'], 'm_rwkv_w2': out['m_rwkv_w2'], 'm_rwkv_a0': out['m_rwkv_a0'], 'm_rwkv_a2': out['m_rwkv_a2'], 'm_rwkv_g2': out['m_rwkv_g2'], 'm_rwkv_k_k': out['m_rwkv_k_k'], 'm_rwkv_k_a': out['m_rwkv_k_a'], 'm_rwkv_r_k': out['m_rwkv_r_k'], 'm_rwkv_ln_w': out['m_rwkv_ln_w'], 'm_rwkv_ln_b': out['m_rwkv_ln_b'], 'm_w_br_attn': out['m_w_br_attn'], 'm_w_br_rwkv': out['m_w_br_rwkv'], 'm_w_o': out['m_w_o'], 'm_norm_ffn_g': out['m_norm_ffn_g'], 'm_w_ffn_gate': out['m_w_ffn_gate'], 'm_w_ffn_up': out['m_w_ffn_up'], 'm_w_ffn_down': out['m_w_ffn_down'], 'm_norm_final_g': out['m_norm_final_g'], 'v_meta_tokens': out['v_meta_tokens'], 'v_norm_mix_g': out['v_norm_mix_g'], 'v_w_in': out['v_w_in'], 'v_b_in': out['v_b_in'], 'v_attn_sinks': out['v_attn_sinks'], 'v_rwkv_mix': out['v_rwkv_mix'], 'v_rwkv_w0': out['v_rwkv_w0'], 'v_rwkv_w2': out['v_rwkv_w2'], 'v_rwkv_a0': out['v_rwkv_a0'], 'v_rwkv_a2': out['v_rwkv_a2'], 'v_rwkv_g2': out['v_rwkv_g2'], 'v_rwkv_k_k': out['v_rwkv_k_k'], 'v_rwkv_k_a': out['v_rwkv_k_a'], 'v_rwkv_r_k': out['v_rwkv_r_k'], 'v_rwkv_ln_w': out['v_rwkv_ln_w'], 'v_rwkv_ln_b': out['v_rwkv_ln_b'], 'v_w_br_attn': out['v_w_br_attn'], 'v_w_br_rwkv': out['v_w_br_rwkv'], 'v_w_o': out['v_w_o'], 'v_norm_ffn_g': out['v_norm_ffn_g'], 'v_w_ffn_gate': out['v_w_ffn_gate'], 'v_w_ffn_up': out['v_w_ffn_up'], 'v_w_ffn_down': out['v_w_ffn_down'], 'v_norm_final_g': out['v_norm_final_g']}


def _loss(weights, diff, rest, loss_target):
    with _jax.named_scope("forward"):
        args = {**rest, TWIN_DIFF_INPUT: diff, **{k: w.astype(_WEIGHT_DTYPES[k]) for k, w in weights.items()}}
        y = _forward(args)
    with _jax.named_scope("loss_head"):
        err = _jnp.square(y.astype(_jnp.float32) - loss_target)
        return 0.5 * _jnp.sum(_jnp.mean(err, axis=-1)) if err.ndim else 0.5 * err


def _adamw(w, g, m, v):
    m = ADAM_B1 * m + (1.0 - ADAM_B1) * g
    v = ADAM_B2 * v + (1.0 - ADAM_B2) * _jnp.square(g)
    m_hat = m / (1.0 - ADAM_B1 ** ADAM_STEP)
    v_hat = v / (1.0 - ADAM_B2 ** ADAM_STEP)
    delta = -ADAM_LR * (m_hat / (_jnp.sqrt(v_hat) + ADAM_EPS) + ADAM_WD * w)
    return delta, m, v


def reference(x, meta_tokens, norm_mix_g, w_in, b_in, attn_sinks, rwkv_mix, rwkv_w0, rwkv_w2, rwkv_a0, rwkv_a2, rwkv_g2, rwkv_k_k, rwkv_k_a, rwkv_r_k, rwkv_ln_w, rwkv_ln_b, w_br_attn, w_br_rwkv, w_o, norm_ffn_g, w_ffn_gate, w_ffn_up, w_ffn_down, norm_final_g, loss_target, m_meta_tokens, m_norm_mix_g, m_w_in, m_b_in, m_attn_sinks, m_rwkv_mix, m_rwkv_w0, m_rwkv_w2, m_rwkv_a0, m_rwkv_a2, m_rwkv_g2, m_rwkv_k_k, m_rwkv_k_a, m_rwkv_r_k, m_rwkv_ln_w, m_rwkv_ln_b, m_w_br_attn, m_w_br_rwkv, m_w_o, m_norm_ffn_g, m_w_ffn_gate, m_w_ffn_up, m_w_ffn_down, m_norm_final_g, v_meta_tokens, v_norm_mix_g, v_w_in, v_b_in, v_attn_sinks, v_rwkv_mix, v_rwkv_w0, v_rwkv_w2, v_rwkv_a0, v_rwkv_a2, v_rwkv_g2, v_rwkv_k_k, v_rwkv_k_a, v_rwkv_r_k, v_rwkv_ln_w, v_rwkv_ln_b, v_w_br_attn, v_w_br_rwkv, v_w_o, v_norm_ffn_g, v_w_ffn_gate, v_w_ffn_up, v_w_ffn_down, v_norm_final_g):
    given = dict(x=x, meta_tokens=meta_tokens, norm_mix_g=norm_mix_g, w_in=w_in, b_in=b_in, attn_sinks=attn_sinks, rwkv_mix=rwkv_mix, rwkv_w0=rwkv_w0, rwkv_w2=rwkv_w2, rwkv_a0=rwkv_a0, rwkv_a2=rwkv_a2, rwkv_g2=rwkv_g2, rwkv_k_k=rwkv_k_k, rwkv_k_a=rwkv_k_a, rwkv_r_k=rwkv_r_k, rwkv_ln_w=rwkv_ln_w, rwkv_ln_b=rwkv_ln_b, w_br_attn=w_br_attn, w_br_rwkv=w_br_rwkv, w_o=w_o, norm_ffn_g=norm_ffn_g, w_ffn_gate=w_ffn_gate, w_ffn_up=w_ffn_up, w_ffn_down=w_ffn_down, norm_final_g=norm_final_g, loss_target=loss_target, m_meta_tokens=m_meta_tokens, m_norm_mix_g=m_norm_mix_g, m_w_in=m_w_in, m_b_in=m_b_in, m_attn_sinks=m_attn_sinks, m_rwkv_mix=m_rwkv_mix, m_rwkv_w0=m_rwkv_w0, m_rwkv_w2=m_rwkv_w2, m_rwkv_a0=m_rwkv_a0, m_rwkv_a2=m_rwkv_a2, m_rwkv_g2=m_rwkv_g2, m_rwkv_k_k=m_rwkv_k_k, m_rwkv_k_a=m_rwkv_k_a, m_rwkv_r_k=m_rwkv_r_k, m_rwkv_ln_w=m_rwkv_ln_w, m_rwkv_ln_b=m_rwkv_ln_b, m_w_br_attn=m_w_br_attn, m_w_br_rwkv=m_w_br_rwkv, m_w_o=m_w_o, m_norm_ffn_g=m_norm_ffn_g, m_w_ffn_gate=m_w_ffn_gate, m_w_ffn_up=m_w_ffn_up, m_w_ffn_down=m_w_ffn_down, m_norm_final_g=m_norm_final_g, v_meta_tokens=v_meta_tokens, v_norm_mix_g=v_norm_mix_g, v_w_in=v_w_in, v_b_in=v_b_in, v_attn_sinks=v_attn_sinks, v_rwkv_mix=v_rwkv_mix, v_rwkv_w0=v_rwkv_w0, v_rwkv_w2=v_rwkv_w2, v_rwkv_a0=v_rwkv_a0, v_rwkv_a2=v_rwkv_a2, v_rwkv_g2=v_rwkv_g2, v_rwkv_k_k=v_rwkv_k_k, v_rwkv_k_a=v_rwkv_k_a, v_rwkv_r_k=v_rwkv_r_k, v_rwkv_ln_w=v_rwkv_ln_w, v_rwkv_ln_b=v_rwkv_ln_b, v_w_br_attn=v_w_br_attn, v_w_br_rwkv=v_w_br_rwkv, v_w_o=v_w_o, v_norm_ffn_g=v_norm_ffn_g, v_w_ffn_gate=v_w_ffn_gate, v_w_ffn_up=v_w_ffn_up, v_w_ffn_down=v_w_ffn_down, v_norm_final_g=v_norm_final_g)
    weights = {n: given[n] for n in TWIN_WEIGHTS}
    shared = {n: given[n] for n in SHARED_INPUTS}
    per_example = {n: given[n] for n in ['x']}
    grad_fn = _jax.value_and_grad(_loss, argnums=(0, 1))

    def one_microbatch(ex, loss_target):
        ex = dict(ex)
        diff = ex.pop(TWIN_DIFF_INPUT)
        return grad_fn(weights, diff, {**shared, **ex}, loss_target)

    if N_MICROBATCH == 1:
        loss, (grad_w, grad_x) = one_microbatch(per_example, given["loss_target"])
    else:
        def body(carry, xs):
            loss_sum, grad_sum = carry
            l_k, (gw_k, gx_k) = one_microbatch(xs[0], xs[1])
            with _jax.named_scope("update"):
                return (loss_sum + l_k, _jax.tree.map(_jnp.add, grad_sum, gw_k)), gx_k

        init = (_jnp.zeros((), _jnp.float32), _jax.tree.map(_jnp.zeros_like, weights))
        (loss, grad_w), grad_x = _jax.lax.scan(body, init, (per_example, given["loss_target"]))
    with _jax.named_scope("update"):
        delta_w, new_m, new_v = {}, {}, {}
        for n in TWIN_WEIGHTS:
            delta_w[n], new_m[n], new_v[n] = _adamw(weights[n], grad_w[n], given["m_" + n], given["v_" + n])
    return (loss, grad_x, *[grad_w[n] for n in TWIN_WEIGHTS], *[delta_w[n] for n in TWIN_WEIGHTS],
            *[new_m[n] for n in TWIN_WEIGHTS], *[new_v[n] for n in TWIN_WEIGHTS])
```

```python
import functools
import math

import jax
import jax.numpy as jnp
from jax import lax
from jax.experimental import pallas as pl
from jax.experimental.pallas import tpu as pltpu

f32 = jnp.float32
bf16 = jnp.bfloat16

D_MODEL = 1024
N_META = 16
HEAD_DIM = 64
Q_HEADS = 8
KV_HEADS = 2
GROUP = Q_HEADS // KV_HEADS
WINDOW = 128
BLOCK = 128
ROPE_THETA = 500000.0
ROPE_DIM = HEAD_DIM // 4
RWKV_HEADS = 8
RWKV_HEAD = 64
RWKV_DIM = RWKV_HEADS * RWKV_HEAD
DECAY_LORA = 64
AAA_LORA = 64
GATE_LORA = 160
LORA_W = DECAY_LORA + AAA_LORA + GATE_LORA
RWKV_LN_EPS = 64e-5
D_FF = 2816
Q_W = Q_HEADS * HEAD_DIM
KV_W = KV_HEADS * HEAD_DIM
ATTN_PROJ = Q_W + 2 * KV_W
RKV_W = 3 * RWKV_DIM
RWKV_PROJ = RKV_W + LORA_W
D_IN = ATTN_PROJ + RWKV_PROJ + 2 * D_MODEL
RMS_EPS = 1e-6
NEG_INF = -1e30
PAD = BLOCK - N_META
FRONT = PAD + N_META

ADAM_LR = 0.001
ADAM_B1 = 0.9
ADAM_B2 = 0.999
ADAM_EPS = 1e-08
ADAM_WD = 0.01
ADAM_STEP = 10

N_CHIPS = 4
N_DEV = 8
CHUNK = 64
VMEM_LIMIT = 56 * 1024 * 1024
PACK_W = 1024
MESH = pl.DeviceIdType.MESH
HIGHEST = lax.Precision.HIGHEST


def _tile(m, pref=384):
    for t in (pref, 256, 128, 64, 32, 16, 8):
        if t <= m and m % t == 0:
            return t
    return m


def _params(sem=None):
    return pltpu.CompilerParams(dimension_semantics=sem, vmem_limit_bytes=VMEM_LIMIT)


def _full(shape):
    nd = len(shape)
    return pl.BlockSpec(shape, lambda *_: (0,) * nd)


def _dot(a, b, dims="nn", exact=False):
    dn = {"nn": (((1,), (0,)), ((), ())), "nt": (((1,), (1,)), ((), ())), "tn": (((0,), (0,)), ((), ()))}[dims]
    if exact:
        return lax.dot_general(a.astype(f32), b.astype(f32), dn, precision=HIGHEST, preferred_element_type=f32)
    return lax.dot_general(a.astype(bf16), b.astype(bf16), dn, preferred_element_type=f32)


def _mm(a, b, mode, *, name, out_dtype=f32, bias=None, add=None, zero_rows_below=0):
    m, _ = a.shape
    n = b.shape[1] if mode == "nn" else b.shape[0]
    tm = _tile(m)
    has_bias, has_add = bias is not None, add is not None

    def body(*refs):
        a_ref, b_ref = refs[0], refs[1]
        o_ref = refs[-1]
        acc = _dot(a_ref[...], b_ref[...], mode)
        k = 2
        if has_bias:
            acc = acc + refs[k][...]
            k += 1
        if zero_rows_below:
            rows = pl.program_id(0) * tm + lax.broadcasted_iota(jnp.int32, acc.shape, 0)
            acc = jnp.where(rows >= zero_rows_below, acc, 0.0)
        if has_add:
            acc = acc + refs[k][...].astype(f32)
        o_ref[...] = acc.astype(out_dtype)

    ins = [a, b]
    in_specs = [pl.BlockSpec((tm, a.shape[1]), lambda i: (i, 0)), _full(b.shape)]
    if has_bias:
        ins.append(bias)
        in_specs.append(_full(bias.shape))
    if has_add:
        ins.append(add)
        in_specs.append(pl.BlockSpec((tm, n), lambda i: (i, 0)))
    return pl.pallas_call(
        body, name=name, grid=(m // tm,), in_specs=in_specs,
        out_specs=pl.BlockSpec((tm, n), lambda i: (i, 0)),
        out_shape=jax.ShapeDtypeStruct((m, n), out_dtype),
        compiler_params=_params(("parallel",)),
    )(*ins)


def _mm_tn(a, b, *, name, colsum=False):
    r, m = a.shape
    n = b.shape[1]
    tr = _tile(r)
    tmo = m
    for cand in (1408, 1024, 768, 512):
        if m > 1024 and m % cand == 0:
            tmo = cand
            break

    def body(a_ref, b_ref, o_ref, *cs):
        i = pl.program_id(1)

        @pl.when(i == 0)
        def _():
            o_ref[...] = jnp.zeros_like(o_ref)
            if colsum:
                cs[0][...] = jnp.zeros_like(cs[0])

        o_ref[...] += _dot(a_ref[...], b_ref[...], "tn")
        if colsum:
            cs[0][...] += jnp.sum(a_ref[...].astype(f32), axis=0, keepdims=True)

    out_shape = [jax.ShapeDtypeStruct((m, n), f32)]
    out_specs = [pl.BlockSpec((tmo, n), lambda j, i: (j, 0))]
    if colsum:
        out_shape.append(jax.ShapeDtypeStruct((1, m), f32))
        out_specs.append(pl.BlockSpec((1, tmo), lambda j, i: (0, j)))
    res = pl.pallas_call(
        body, name=name, grid=(m // tmo, r // tr),
        in_specs=[pl.BlockSpec((tr, tmo), lambda j, i: (i, j)), pl.BlockSpec((tr, n), lambda j, i: (i, 0))],
        out_specs=out_specs, out_shape=out_shape,
        compiler_params=_params(("parallel", "arbitrary")),
    )(a, b)
    return res if colsum else res[0]


def _rowwise(fn, rows, params, outs, *, name, tm=None, with_row0=False):
    m = rows[0].shape[0]
    tm = tm or _tile(m)
    nr, npar = len(rows), len(params)

    def body(*refs):
        vals = [r[...] for r in refs[:nr + npar]]
        kw = dict(row0=pl.program_id(0) * tm) if with_row0 else {}
        res = fn(*vals, **kw)
        for o_ref, v in zip(refs[nr + npar:], res):
            o_ref[...] = v.astype(o_ref.dtype)

    return pl.pallas_call(
        body, name=name, grid=(m // tm,),
        in_specs=[pl.BlockSpec((tm, r.shape[1]), lambda i: (i, 0)) for r in rows] + [_full(p.shape) for p in params],
        out_specs=[pl.BlockSpec((tm, w), lambda i: (i, 0)) for w, _ in outs],
        out_shape=[jax.ShapeDtypeStruct((m, w), dt) for w, dt in outs],
        compiler_params=_params(("parallel",)),
    )(*rows, *params)


def _rowwise_bwd(fn, rows, params, cts, *, name, diff_rows, diff_params, tm=None, with_row0=False, zero_rows_below=0):
    m = rows[0].shape[0]
    tm = tm or _tile(m)
    nr, npar = len(rows), len(params)
    d_idx = [i for i in range(nr) if diff_rows[i]]
    p_idx = [i for i in range(npar) if diff_params[i]]
    flat_cts = [c for group in cts for c in group]
    n_ct = len(flat_cts)

    def body(*refs):
        vals = [r[...] for r in refs[:nr + npar]]
        ct_refs = refs[nr + npar:nr + npar + n_ct]
        out_refs = refs[nr + npar + n_ct:]
        kw = dict(row0=pl.program_id(0) * tm) if with_row0 else {}
        ct_vals, k = [], 0
        for group in cts:
            acc = ct_refs[k][...].astype(f32)
            for extra in range(1, len(group)):
                acc = acc + ct_refs[k + extra][...].astype(f32)
            k += len(group)
            if zero_rows_below:
                rr = pl.program_id(0) * tm + lax.broadcasted_iota(jnp.int32, acc.shape, 0)
                acc = jnp.where(rr >= zero_rows_below, acc, 0.0)
            ct_vals.append(acc)

        def g(*dargs):
            full = list(vals)
            for pos, i in enumerate(d_idx):
                full[i] = dargs[pos]
            for pos, i in enumerate(p_idx):
                full[nr + i] = dargs[len(d_idx) + pos]
            return tuple(fn(*full, **kw))

        _, vjp = jax.vjp(g, *[vals[i] for i in d_idx], *[vals[nr + i] for i in p_idx])
        grads = vjp(tuple(ct_vals))
        for pos in range(len(d_idx)):
            out_refs[pos][...] = grads[pos].astype(out_refs[pos].dtype)
        first = pl.program_id(0) == 0
        for pos in range(len(p_idx)):
            o_ref = out_refs[len(d_idx) + pos]

            @pl.when(first)
            def _(o_ref=o_ref):
                o_ref[...] = jnp.zeros_like(o_ref)

            o_ref[...] += grads[len(d_idx) + pos]

    return pl.pallas_call(
        body, name=name, grid=(m // tm,),
        in_specs=[pl.BlockSpec((tm, r.shape[1]), lambda i: (i, 0)) for r in rows] + [_full(p.shape) for p in params]
        + [pl.BlockSpec((tm, c.shape[1]), lambda i: (i, 0)) for c in flat_cts],
        out_specs=[pl.BlockSpec((tm, rows[i].shape[1]), lambda i_: (i_, 0)) for i in d_idx]
        + [_full(params[i].shape) for i in p_idx],
        out_shape=[jax.ShapeDtypeStruct(rows[i].shape, f32) for i in d_idx]
        + [jax.ShapeDtypeStruct(params[i].shape, f32) for i in p_idx],
        compiler_params=_params(("arbitrary",)),
    )(*rows, *params, *flat_cts)


def _rms(x, g):
    return x * lax.rsqrt(jnp.mean(x * x, axis=-1, keepdims=True) + RMS_EPS) * g


def _head_sum_matrix(width, head):
    idx = jnp.arange(width) // head
    return (idx[:, None] == idx[None, :]).astype(f32)


def _rope_tables(lp):
    half = ROPE_DIM // 2
    pos = (jnp.arange(lp) - PAD).astype(f32)
    inv_freq = jnp.power(jnp.float32(ROPE_THETA), -jnp.arange(half, dtype=f32) * (2.0 / ROPE_DIM))
    ang = pos[:, None] * inv_freq[None, :]
    cos, sin = jnp.cos(ang), jnp.sin(ang)
    ones = jnp.ones((lp, HEAD_DIM - ROPE_DIM), f32)
    zeros = jnp.zeros((lp, HEAD_DIM - ROPE_DIM), f32)
    cos_t = jnp.concatenate([cos, cos, ones], axis=1)
    sin_t = jnp.concatenate([-sin, sin, zeros], axis=1)
    i = jnp.arange(HEAD_DIM)
    src = jnp.where(i < half, i + half, jnp.where(i < ROPE_DIM, i - half, i))
    swap = ((i[:, None] == src[None, :]) & (i[None, :] < ROPE_DIM)).astype(f32)
    return cos_t, sin_t, swap


def _attn_prep(qkv, cos_t, sin_t, swap):
    outs = []
    for h in range(Q_HEADS + KV_HEADS):
        t = qkv[:, h * HEAD_DIM:(h + 1) * HEAD_DIM]
        outs.append(t * cos_t + _dot(t, swap, exact=True) * sin_t)
    q = jnp.concatenate(outs[:Q_HEADS], axis=1)
    k = jnp.concatenate(outs[Q_HEADS:], axis=1)
    return q, k, qkv[:, Q_W + KV_W:]


def _softplus(z):
    return jnp.maximum(z, 0.0) + jnp.log1p(jnp.exp(-jnp.abs(z)))


def _rwkv_prep(rkv, lora, w0, w2, a0, a2, g2, k_k, k_a, hsum):
    r = rkv[:, :RWKV_DIM]
    k = rkv[:, RWKV_DIM:2 * RWKV_DIM]
    v = rkv[:, 2 * RWKV_DIM:]
    dw = lora[:, :DECAY_LORA]
    da = lora[:, DECAY_LORA:DECAY_LORA + AAA_LORA]
    dg = lora[:, DECAY_LORA + AAA_LORA:]
    w = -_softplus(-(w0 + _dot(jnp.tanh(dw), w2))) - 0.5
    a = jax.nn.sigmoid(a0 + _dot(da, a2))
    g = _dot(jax.nn.sigmoid(dg), g2)
    kk = k * k_k
    kk = kk * lax.rsqrt(jnp.maximum(_dot(kk * kk, hsum, exact=True), 1e-24))
    k = k * (1.0 + (a - 1.0) * k_a)
    log_decay = -jnp.exp(w)
    return r, log_decay, k, v, -kk, kk * a, g


def _rwkv_post(y, r, k, v, g, ln_w, ln_b, r_k, hmean):
    hsum = hmean * RWKV_HEAD
    mean = _dot(y, hmean, exact=True)
    yc = y - mean
    var = _dot(yc * yc, hmean, exact=True)
    yn = yc * lax.rsqrt(var + RWKV_LN_EPS) * ln_w + ln_b
    bonus = _dot(r * k * r_k, hsum, exact=True) * v
    return ((yn + bonus) * g,)


def _merge(gates, br_a, br_r):
    sg = jax.nn.sigmoid(gates)
    return (sg[:, :D_MODEL] * br_a + sg[:, D_MODEL:] * br_r,)


def _swiglu(gate, up):
    return (jax.nn.silu(gate) * up,)


def _token_shift(p, mix, *, name):
    m, c = p.shape
    tm = _tile(m)
    sub = tm // 8

    def body(p_ref, prev_ref, mix_ref, o_ref):
        x = p_ref[...]
        rows = lax.broadcasted_iota(jnp.int32, x.shape, 0)
        last = jnp.where(pl.program_id(0) == 0, 0.0, prev_ref[7:8, :])
        xp = jnp.where(rows == 0, last, pltpu.roll(x, 1, axis=0))
        o_ref[...] = x + (xp - x) * mix_ref[...]

    return pl.pallas_call(
        body, name=name, grid=(m // tm,),
        in_specs=[pl.BlockSpec((tm, c), lambda i: (i, 0)),
                  pl.BlockSpec((8, c), lambda i: (jnp.maximum(i * sub - 1, 0), 0)),
                  _full(mix.shape)],
        out_specs=pl.BlockSpec((tm, c), lambda i: (i, 0)),
        out_shape=jax.ShapeDtypeStruct((m, c), f32),
        compiler_params=_params(("parallel",)),
    )(p, p, mix)


def _token_shift_bwd(p, mix, dpf, *, name):
    m, c = p.shape
    tm = _tile(m)
    sub = tm // 8
    n_tiles = m // tm

    def body(p_ref, prev_ref, mix_ref, d_ref, nxt_ref, dp_ref, dmix_ref):
        i = pl.program_id(0)
        x = p_ref[...]
        d = d_ref[...]
        mixv = mix_ref[...]
        rows = lax.broadcasted_iota(jnp.int32, x.shape, 0)
        last = jnp.where(i == 0, 0.0, prev_ref[7:8, :])
        xp = jnp.where(rows == 0, last, pltpu.roll(x, 1, axis=0))
        dm = d * mixv
        first_next = jnp.where(i == n_tiles - 1, 0.0, nxt_ref[0:1, :] * mixv)
        dm_next = jnp.where(rows == tm - 1, first_next, pltpu.roll(dm, tm - 1, axis=0))
        dp = d - dm + dm_next
        dp_ref[...] = jnp.where(i * tm + rows >= PAD, dp, 0.0)

        @pl.when(i == 0)
        def _():
            dmix_ref[...] = jnp.zeros_like(dmix_ref)

        dmix_ref[...] += jnp.sum(d * (xp - x), axis=0, keepdims=True)

    return pl.pallas_call(
        body, name=name, grid=(n_tiles,),
        in_specs=[pl.BlockSpec((tm, c), lambda i: (i, 0)),
                  pl.BlockSpec((8, c), lambda i: (jnp.maximum(i * sub - 1, 0), 0)),
                  _full(mix.shape),
                  pl.BlockSpec((tm, c), lambda i: (i, 0)),
                  pl.BlockSpec((8, c), lambda i: (jnp.minimum((i + 1) * sub, m // 8 - 1), 0))],
        out_specs=[pl.BlockSpec((tm, c), lambda i: (i, 0)), _full(mix.shape)],
        out_shape=[jax.ShapeDtypeStruct((m, c), f32), jax.ShapeDtypeStruct(mix.shape, f32)],
        compiler_params=_params(("arbitrary",)),
    )(p, p, mix, dpf, dpf)


def _attn_head(q, kp, kc, km, vp, vc, vm, sink, blk):
    scale = HEAD_DIM ** -0.5
    qi = lax.broadcasted_iota(jnp.int32, (BLOCK, BLOCK), 0)
    ki = lax.broadcasted_iota(jnp.int32, (BLOCK, BLOCK), 1)
    qpos = blk * BLOCK + qi - PAD
    kpos_c = blk * BLOCK + ki - PAD
    kpos_p = kpos_c - BLOCK
    kpos_m = ki - PAD

    def band(kpos):
        return (kpos >= N_META) & (kpos <= qpos) & (qpos - kpos < WINDOW)

    s_p = jnp.where(band(kpos_p), _dot(q, kp, "nt") * scale, NEG_INF)
    s_c = jnp.where(band(kpos_c), _dot(q, kc, "nt") * scale, NEG_INF)
    s_m = jnp.where((kpos_m >= 0) & (kpos_m <= qpos), _dot(q, km, "nt") * scale, NEG_INF)
    mx = jnp.maximum(jnp.maximum(jnp.max(s_p, -1, keepdims=True), jnp.max(s_c, -1, keepdims=True)),
                     jnp.maximum(jnp.max(s_m, -1, keepdims=True), sink))
    e_p, e_c, e_m = jnp.exp(s_p - mx), jnp.exp(s_c - mx), jnp.exp(s_m - mx)
    den = (jnp.sum(e_p, -1, keepdims=True) + jnp.sum(e_c, -1, keepdims=True)
           + jnp.sum(e_m, -1, keepdims=True) + jnp.exp(sink - mx))
    inv = 1.0 / den
    return _dot(e_p * inv, vp) + _dot(e_c * inv, vc) + _dot(e_m * inv, vm)


def _attn_specs():
    prev = lambda i: (jnp.maximum(i - 1, 0), 0)
    cur = lambda i: (i, 0)
    meta = lambda i: (0, 0)
    return [pl.BlockSpec((BLOCK, Q_W), cur),
            pl.BlockSpec((BLOCK, KV_W), prev), pl.BlockSpec((BLOCK, KV_W), cur), pl.BlockSpec((BLOCK, KV_W), meta),
            pl.BlockSpec((BLOCK, KV_W), prev), pl.BlockSpec((BLOCK, KV_W), cur), pl.BlockSpec((BLOCK, KV_W), meta),
            _full((1, Q_HEADS))]


def _head_slices(refs, i):
    q_ref, kp_ref, kc_ref, km_ref, vp_ref, vc_ref, vm_ref, s_ref = refs
    h = i // GROUP
    hs = slice(h * HEAD_DIM, (h + 1) * HEAD_DIM)
    return (q_ref[:, i * HEAD_DIM:(i + 1) * HEAD_DIM], kp_ref[:, hs], kc_ref[:, hs], km_ref[:, hs],
            vp_ref[:, hs], vc_ref[:, hs], vm_ref[:, hs], s_ref[:, i:i + 1])


def _attention(q, k, v, sinks, *, name):
    lp = q.shape[0]

    def body(*refs):
        o_ref = refs[-1]
        blk = pl.program_id(0)
        for i in range(Q_HEADS):
            o_ref[:, i * HEAD_DIM:(i + 1) * HEAD_DIM] = _attn_head(*_head_slices(refs[:-1], i), blk).astype(o_ref.dtype)

    return pl.pallas_call(
        body, name=name, grid=(lp // BLOCK,), in_specs=_attn_specs(),
        out_specs=pl.BlockSpec((BLOCK, Q_W), lambda i: (i, 0)),
        out_shape=jax.ShapeDtypeStruct((lp, Q_W), f32),
        compiler_params=_params(("parallel",)),
    )(q, k, k, k, v, v, v, sinks)


def _attention_bwd(q, k, v, sinks, do, *, name):
    lp = q.shape[0]
    nb = lp // BLOCK

    def body(*refs):
        ins, do_ref = refs[:8], refs[8]
        dq_ref, dkp_ref, dkc_ref, dkm_ref, dvp_ref, dvc_ref, dvm_ref, ds_ref = refs[9:]
        blk = pl.program_id(0)

        @pl.when(blk == 0)
        def _():
            dkm_ref[...] = jnp.zeros_like(dkm_ref)
            dvm_ref[...] = jnp.zeros_like(dvm_ref)
            ds_ref[...] = jnp.zeros_like(ds_ref)

        for h in range(KV_HEADS):
            hs = slice(h * HEAD_DIM, (h + 1) * HEAD_DIM)
            acc = None
            for i in range(h * GROUP, (h + 1) * GROUP):
                args = [a.astype(f32) for a in _head_slices(ins, i)]
                _, vjp = jax.vjp(functools.partial(_attn_head, blk=blk), *args)
                g = vjp(do_ref[:, i * HEAD_DIM:(i + 1) * HEAD_DIM].astype(f32))
                dq_ref[:, i * HEAD_DIM:(i + 1) * HEAD_DIM] = g[0]
                ds_ref[:, i:i + 1] += g[7]
                acc = list(g[1:7]) if acc is None else [x + y for x, y in zip(acc, g[1:7])]
            dkp_ref[:, hs] = acc[0]
            dkc_ref[:, hs] = acc[1]
            dkm_ref[:, hs] += acc[2]
            dvp_ref[:, hs] = acc[3]
            dvc_ref[:, hs] = acc[4]
            dvm_ref[:, hs] += acc[5]

    blk_spec = pl.BlockSpec((BLOCK, KV_W), lambda i: (i, 0))
    one_spec = pl.BlockSpec((BLOCK, KV_W), lambda i: (0, 0))
    kv_shape = jax.ShapeDtypeStruct((lp, KV_W), f32)
    one_shape = jax.ShapeDtypeStruct((BLOCK, KV_W), f32)
    return pl.pallas_call(
        body, name=name, grid=(nb,),
        in_specs=_attn_specs() + [pl.BlockSpec((BLOCK, Q_W), lambda i: (i, 0))],
        out_specs=[pl.BlockSpec((BLOCK, Q_W), lambda i: (i, 0)), blk_spec, blk_spec, one_spec,
                   blk_spec, blk_spec, one_spec, _full((1, Q_HEADS))],
        out_shape=[jax.ShapeDtypeStruct((lp, Q_W), f32), kv_shape, kv_shape, one_shape,
                   kv_shape, kv_shape, one_shape, jax.ShapeDtypeStruct((1, Q_HEADS), f32)],
        compiler_params=_params(("arbitrary",)),
    )(q, k, k, k, v, v, v, sinks, do)


def _kv_combine(d_prev, d_cur, d_meta, *, name):
    lp = d_cur.shape[0]
    nb = lp // BLOCK

    def body(p_ref, c_ref, m_ref, o_ref):
        j = pl.program_id(0)
        acc = c_ref[...] + jnp.where(j < nb - 1, p_ref[...], 0.0)
        o_ref[...] = acc + jnp.where(j == 0, m_ref[...], 0.0)

    return pl.pallas_call(
        body, name=name, grid=(nb,),
        in_specs=[pl.BlockSpec((BLOCK, KV_W), lambda j: (jnp.minimum(j + 1, nb - 1), 0)),
                  pl.BlockSpec((BLOCK, KV_W), lambda j: (j, 0)),
                  pl.BlockSpec((BLOCK, KV_W), lambda j: (0, 0))],
        out_specs=pl.BlockSpec((BLOCK, KV_W), lambda j: (j, 0)),
        out_shape=jax.ShapeDtypeStruct((lp, KV_W), f32),
        compiler_params=_params(("parallel",)),
    )(d_prev, d_cur, d_meta)


def _scan_chunk(s0, r, lw, k, v, a, b):
    t = r.shape[0]
    ii = lax.broadcasted_iota(jnp.int32, (t, t), 0)
    jj = lax.broadcasted_iota(jnp.int32, (t, t), 1)
    incl = jj <= ii
    strict = jj < ii
    cl = _dot(incl.astype(f32), lw, exact=True)
    e_neg = jnp.exp(-cl)
    rt = r * jnp.exp(cl)
    at = a * jnp.exp(cl - lw)
    bt = b * e_neg
    kt = k * e_neg
    l_ab = jnp.where(strict, _dot(at, bt, "nt", exact=True), 0.0)
    l_ak = jnp.where(strict, _dot(at, kt, "nt"), 0.0)
    r_b = jnp.where(incl, _dot(rt, bt, "nt"), 0.0)
    r_k = jnp.where(incl, _dot(rt, kt, "nt"), 0.0)
    inv = jnp.where(ii == jj, 1.0, 0.0) + l_ab
    pw = l_ab
    for _ in range(int(math.log2(t)) - 1):
        pw = _dot(pw, pw, exact=True)
        inv = inv + _dot(inv, pw, exact=True)
    u = _dot(inv, _dot(at, s0, "nt") + _dot(l_ak, v), exact=True)
    y = _dot(rt, s0, "nt") + _dot(r_b, u) + _dot(r_k, v)
    s1 = (s0 + _dot(u, bt, "tn") + _dot(v, kt, "tn")) * jnp.exp(cl[t - 1:t, :])
    return y, s1


def _scan_inputs(ref_list, h):
    hs = slice(h * RWKV_HEAD, (h + 1) * RWKV_HEAD)
    return [ref[:, hs] for ref in ref_list]


def _scan(r, lw, k, v, a, b, *, name):
    lp = r.shape[0]
    nc = lp // CHUNK
    row = pl.BlockSpec((CHUNK, RWKV_DIM), lambda c: (c, 0))

    def body(r_ref, lw_ref, k_ref, v_ref, a_ref, b_ref, y_ref, s_ref, state):
        @pl.when(pl.program_id(0) == 0)
        def _():
            state[...] = jnp.zeros_like(state)

        s_ref[...] = state[...]
        for h in range(RWKV_HEADS):
            rows = slice(h * RWKV_HEAD, (h + 1) * RWKV_HEAD)
            y, s1 = _scan_chunk(state[rows, :], *_scan_inputs((r_ref, lw_ref, k_ref, v_ref, a_ref, b_ref), h))
            y_ref[:, rows] = y
            state[rows, :] = s1

    return pl.pallas_call(
        body, name=name, grid=(nc,), in_specs=[row] * 6,
        out_specs=[row, pl.BlockSpec((RWKV_DIM, RWKV_HEAD), lambda c: (c, 0))],
        out_shape=[jax.ShapeDtypeStruct((lp, RWKV_DIM), f32), jax.ShapeDtypeStruct((nc * RWKV_DIM, RWKV_HEAD), f32)],
        scratch_shapes=[pltpu.VMEM((RWKV_DIM, RWKV_HEAD), f32)],
        compiler_params=_params(("arbitrary",)),
    )(r, lw, k, v, a, b)


def _scan_bwd(r, lw, k, v, a, b, states, dy, *, name):
    lp = r.shape[0]
    nc = lp // CHUNK
    row = pl.BlockSpec((CHUNK, RWKV_DIM), lambda c: (nc - 1 - c, 0))

    def body(r_ref, lw_ref, k_ref, v_ref, a_ref, b_ref, s_ref, dy_ref,
             dr_ref, dlw_ref, dk_ref, dv_ref, da_ref, db_ref, dstate):
        @pl.when(pl.program_id(0) == 0)
        def _():
            dstate[...] = jnp.zeros_like(dstate)

        outs = (dr_ref, dlw_ref, dk_ref, dv_ref, da_ref, db_ref)
        for h in range(RWKV_HEADS):
            rows = slice(h * RWKV_HEAD, (h + 1) * RWKV_HEAD)
            _, vjp = jax.vjp(_scan_chunk, s_ref[rows, :],
                             *_scan_inputs((r_ref, lw_ref, k_ref, v_ref, a_ref, b_ref), h))
            g = vjp((dy_ref[:, rows], dstate[rows, :]))
            dstate[rows, :] = g[0]
            for o_ref, gv in zip(outs, g[1:]):
                o_ref[:, rows] = gv

    shape = jax.ShapeDtypeStruct((lp, RWKV_DIM), f32)
    return pl.pallas_call(
        body, name=name, grid=(nc,),
        in_specs=[row] * 6 + [pl.BlockSpec((RWKV_DIM, RWKV_HEAD), lambda c: (nc - 1 - c, 0)), row],
        out_specs=[row] * 6, out_shape=[shape] * 6,
        scratch_shapes=[pltpu.VMEM((RWKV_DIM, RWKV_HEAD), f32)],
        compiler_params=_params(("arbitrary",)),
    )(r, lw, k, v, a, b, states, dy)


def _loss_head(h2, target, g_final, *, name):
    lp = h2.shape[0]
    tm = BLOCK
    front_tiles = FRONT // tm

    def body(h_ref, t_ref, g_ref, loss_ref, dh_ref, dg_ref):
        i = pl.program_id(0)
        real = i >= front_tiles

        def tile_loss(hv, gv):
            err = _rms(hv, gv) - t_ref[...]
            return jnp.where(real, 0.5 * jnp.sum(jnp.mean(err * err, axis=-1, keepdims=True)), 0.0)

        loss, (dh, dg) = jax.value_and_grad(tile_loss, argnums=(0, 1))(h_ref[...], g_ref[...])

        @pl.when(i == 0)
        def _():
            loss_ref[...] = jnp.zeros_like(loss_ref)
            dg_ref[...] = jnp.zeros_like(dg_ref)

        loss_ref[...] += jnp.full(loss_ref.shape, loss, f32)
        dg_ref[...] += dg
        dh_ref[...] = dh

    return pl.pallas_call(
        body, name=name, grid=(lp // tm,),
        in_specs=[pl.BlockSpec((tm, D_MODEL), lambda i: (i, 0)),
                  pl.BlockSpec((tm, D_MODEL), lambda i: (jnp.maximum(i - front_tiles, 0), 0)),
                  _full(g_final.shape)],
        out_specs=[_full((8, 128)), pl.BlockSpec((tm, D_MODEL), lambda i: (i, 0)), _full(g_final.shape)],
        out_shape=[jax.ShapeDtypeStruct((8, 128), f32), jax.ShapeDtypeStruct((lp, D_MODEL), f32),
                   jax.ShapeDtypeStruct(g_final.shape, f32)],
        compiler_params=_params(("arbitrary",)),
    )(h2, target, g_final)


def _local_step(x, target, meta, p):
    seq = x.shape[0]
    lp = seq + FRONT
    h0 = jnp.concatenate([jnp.zeros((PAD, D_MODEL), f32), meta, x], axis=0)
    cos_t, sin_t, swap = _rope_tables(lp)
    hsum = _head_sum_matrix(RWKV_DIM, RWKV_HEAD)
    hmean = hsum / RWKV_HEAD
    w_qkv_t, w_rkv_t = p["w_in_t"][:ATTN_PROJ], p["w_in_t"][ATTN_PROJ:ATTN_PROJ + RKV_W]
    w_lora_t, w_gates_t = p["w_in_t"][ATTN_PROJ + RKV_W:ATTN_PROJ + RWKV_PROJ], p["w_in_t"][ATTN_PROJ + RWKV_PROJ:]
    b_qkv, b_rkv = p["b_in"][:, :ATTN_PROJ], p["b_in"][:, ATTN_PROJ:ATTN_PROJ + RKV_W]
    b_lora, b_gates = p["b_in"][:, ATTN_PROJ + RKV_W:ATTN_PROJ + RWKV_PROJ], p["b_in"][:, ATTN_PROJ + RWKV_PROJ:]
    prep_params = [p["w0"], p["w2"], p["a0"], p["a2"], p["g2"], p["k_k"], p["k_a"], hsum]
    post_params = [p["ln_w"], p["ln_b"], p["r_k"], hmean]

    (u,) = _rowwise(lambda hv, g: (_rms(hv, g),), [h0], [p["norm_mix_g"]], [(D_MODEL, bf16)], name="norm_mix")
    qkv = _mm(u, w_qkv_t, "nt", name="proj_qkv", bias=b_qkv, zero_rows_below=PAD)
    p_rkv = _mm(u, w_rkv_t, "nt", name="proj_rkv", bias=b_rkv, zero_rows_below=PAD)
    p_lora = _mm(u, w_lora_t, "nt", name="proj_lora", bias=b_lora, zero_rows_below=PAD)
    gates = _mm(u, w_gates_t, "nt", name="proj_gates", bias=b_gates, zero_rows_below=PAD)

    q, k, v = _rowwise(_attn_prep, [qkv, cos_t, sin_t], [swap], [(Q_W, bf16), (KV_W, bf16), (KV_W, bf16)],
                       name="attn_prep")
    y_attn = _attention(q, k, v, p["sinks"], name="attention")

    mix_rkv, mix_lora = p["mix"][:, :RKV_W], p["mix"][:, RKV_W:]
    pf_rkv = _token_shift(p_rkv, mix_rkv, name="shift_rkv")
    pf_lora = _token_shift(p_lora, mix_lora, name="shift_lora")
    wide = [(RWKV_DIM, f32)] * 7
    r_, lw_, k_, v_, a_, b_, g_ = _rowwise(_rwkv_prep, [pf_rkv, pf_lora], prep_params, wide, name="rwkv_prep")
    y_scan, states = _scan(r_, lw_, k_, v_, a_, b_, name="wkv_scan")
    (y_rwkv,) = _rowwise(_rwkv_post, [y_scan, r_, k_, v_, g_], post_params, [(RWKV_DIM, f32)], name="rwkv_post")

    br_a = _mm(y_attn, p["w_br_attn_t"], "nt", name="branch_attn")
    br_r = _mm(y_rwkv, p["w_br_rwkv_t"], "nt", name="branch_rwkv")
    (merged,) = _rowwise(_merge, [gates, br_a, br_r], [], [(D_MODEL, bf16)], name="merge")
    h1 = _mm(merged, p["w_o"], "nn", name="out_proj", add=h0)
    (f,) = _rowwise(lambda hv, g: (_rms(hv, g),), [h1], [p["norm_ffn_g"]], [(D_MODEL, bf16)], name="norm_ffn")
    gate = _mm(f, p["w_gate_t"], "nt", name="ffn_gate")
    up = _mm(f, p["w_up_t"], "nt", name="ffn_up")
    (act,) = _rowwise(_swiglu, [gate, up], [], [(D_FF, bf16)], name="swiglu")
    h2 = _mm(act, p["w_down"], "nn", name="ffn_down", add=h1)

    loss8, dh2, d_final_g = _loss_head(h2, target, p["norm_final_g"], name="loss_head")
    dact = _mm(dh2, p["w_down"], "nt", name="d_act")
    d_w_down = _mm_tn(act, dh2, name="dw_down")
    dgate, dup = _rowwise_bwd(_swiglu, [gate, up], [], [[dact]], name="swiglu_bwd",
                              diff_rows=[True, True], diff_params=[])
    d_w_gate_t = _mm_tn(dgate, f, name="dw_gate")
    d_w_up_t = _mm_tn(dup, f, name="dw_up")
    df = _mm(dgate, p["w_gate_t"], "nn", name="d_f_gate")
    df = _mm(dup, p["w_up_t"], "nn", name="d_f_up", add=df)
    dh1_n, d_ffn_g = _rowwise_bwd(lambda hv, g: (_rms(hv, g),), [h1], [p["norm_ffn_g"]], [[df]], name="norm_ffn_bwd",
                                  diff_rows=[True], diff_params=[True])
    (dh1,) = _rowwise(lambda x1, x2: (x1 + x2,), [dh2, dh1_n], [], [(D_MODEL, f32)], name="dh1_sum")
    dmerged = _mm(dh1, p["w_o"], "nt", name="d_merged")
    d_w_o = _mm_tn(merged, dh1, name="dw_o")
    dgates, dbr_a, dbr_r = _rowwise_bwd(_merge, [gates, br_a, br_r], [], [[dmerged]], name="merge_bwd",
                                        diff_rows=[True, True, True], diff_params=[])
    dy_attn = _mm(dbr_a, p["w_br_attn_t"], "nn", name="d_y_attn")
    dy_rwkv = _mm(dbr_r, p["w_br_rwkv_t"], "nn", name="d_y_rwkv")
    d_w_br_attn_t = _mm_tn(dbr_a, y_attn, name="dw_br_attn")
    d_w_br_rwkv_t = _mm_tn(dbr_r, y_rwkv, name="dw_br_rwkv")

    res = _rowwise_bwd(_rwkv_post, [y_scan, r_, k_, v_, g_], post_params, [[dy_rwkv]], name="rwkv_post_bwd",
                       diff_rows=[True] * 5, diff_params=[True, True, True, False])
    dy_scan, dr_p, dk_p, dv_p, dg_p, d_ln_w, d_ln_b, d_r_k = res
    dr_s, dlw_s, dk_s, dv_s, da_s, db_s = _scan_bwd(r_, lw_, k_, v_, a_, b_, states, dy_scan, name="wkv_scan_bwd")
    res = _rowwise_bwd(_rwkv_prep, [pf_rkv, pf_lora], prep_params,
                       [[dr_s, dr_p], [dlw_s], [dk_s, dk_p], [dv_s, dv_p], [da_s], [db_s], [dg_p]],
                       name="rwkv_prep_bwd", diff_rows=[True, True], diff_params=[True] * 7 + [False],
                       zero_rows_below=PAD)
    dpf_rkv, dpf_lora, d_w0, d_w2, d_a0, d_a2, d_g2, d_k_k, d_k_a = res
    dp_rkv, d_mix_rkv = _token_shift_bwd(p_rkv, mix_rkv, dpf_rkv, name="shift_rkv_bwd")
    dp_lora, d_mix_lora = _token_shift_bwd(p_lora, mix_lora, dpf_lora, name="shift_lora_bwd")

    dq, dkp, dkc, dkm, dvp, dvc, dvm, d_sinks = _attention_bwd(q, k, v, p["sinks"], dy_attn, name="attention_bwd")
    dk = _kv_combine(dkp, dkc, dkm, name="dk_sum")
    dv = _kv_combine(dvp, dvc, dvm, name="dv_sum")
    (dqkv,) = _rowwise_bwd(_attn_prep, [qkv, cos_t, sin_t], [swap], [[dq], [dk], [dv]], name="attn_prep_bwd",
                           diff_rows=[True, False, False], diff_params=[False])

    d_w_qkv_t, db_qkv = _mm_tn(dqkv, u, name="dw_qkv", colsum=True)
    d_w_rkv_t, db_rkv = _mm_tn(dp_rkv, u, name="dw_rkv", colsum=True)
    d_w_lora_t, db_lora = _mm_tn(dp_lora, u, name="dw_lora", colsum=True)
    d_w_gates_t, db_gates = _mm_tn(dgates, u, name="dw_gates", colsum=True)
    du = _mm(dqkv, w_qkv_t, "nn", name="d_u_qkv")
    du = _mm(dp_rkv, w_rkv_t, "nn", name="d_u_rkv", add=du)
    du = _mm(dp_lora, w_lora_t, "nn", name="d_u_lora", add=du)
    du = _mm(dgates, w_gates_t, "nn", name="d_u_gates", add=du)
    dh0_n, d_mix_g = _rowwise_bwd(lambda hv, g: (_rms(hv, g),), [h0], [p["norm_mix_g"]], [[du]], name="norm_mix_bwd",
                                  diff_rows=[True], diff_params=[True])
    (dh0,) = _rowwise(lambda x1, x2: (x1 + x2,), [dh1, dh0_n], [], [(D_MODEL, f32)], name="dh0_sum")

    grads = dict(
        w_in_t=jnp.concatenate([d_w_qkv_t, d_w_rkv_t, d_w_lora_t, d_w_gates_t], axis=0),
        b_in=jnp.concatenate([db_qkv, db_rkv, db_lora, db_gates], axis=1),
        mix=jnp.concatenate([d_mix_rkv, d_mix_lora], axis=1),
        norm_mix_g=d_mix_g, sinks=d_sinks, w0=d_w0, w2=d_w2, a0=d_a0, a2=d_a2, g2=d_g2, k_k=d_k_k, k_a=d_k_a,
        r_k=d_r_k, ln_w=d_ln_w, ln_b=d_ln_b, w_br_attn_t=d_w_br_attn_t, w_br_rwkv_t=d_w_br_rwkv_t, w_o=d_w_o,
        norm_ffn_g=d_ffn_g, w_gate_t=d_w_gate_t, w_up_t=d_w_up_t, w_down=d_w_down, norm_final_g=d_final_g,
        meta=dh0[PAD:FRONT],
    )
    return loss8[0, 0], dh0[FRONT:], grads


def _position():
    return lax.axis_index("x"), lax.axis_index("y"), lax.axis_index("c")


def _other_chips(x, y):
    return [(1 - x, y), (x, 1 - y), (1 - x, 1 - y)]


def _gather_chips(shard, *, name):
    r, w = shard.shape

    def body(x_ref, o_ref, send_sems, recv_sems, local_sem):
        x, y, c = _position()
        local = pltpu.make_async_copy(x_ref, o_ref.at[2 * x + y], local_sem)
        local.start()
        chips = _other_chips(x, y)

        def copy(j, px, py, slot):
            return pltpu.make_async_remote_copy(
                src_ref=x_ref, dst_ref=o_ref.at[slot], send_sem=send_sems.at[j], recv_sem=recv_sems.at[j],
                device_id=(px, py, c), device_id_type=MESH)

        sends = [copy(j, px, py, 2 * x + y) for j, (px, py) in enumerate(chips)]
        for cp in sends:
            cp.start()
        for j, (px, py) in enumerate(chips):
            copy(j, px, py, 2 * px + py).wait_recv()
        for cp in sends:
            cp.wait_send()
        local.wait()

    return pl.pallas_call(
        body, name=name,
        in_specs=[pl.BlockSpec(memory_space=pl.ANY)],
        out_specs=pl.BlockSpec(memory_space=pl.ANY),
        out_shape=jax.ShapeDtypeStruct((N_CHIPS, r, w), shard.dtype),
        scratch_shapes=[pltpu.SemaphoreType.DMA((3,)), pltpu.SemaphoreType.DMA((3,)), pltpu.SemaphoreType.DMA],
    )(shard)


def _scatter_chips(g, *, name):
    _, r, w = g.shape

    def body(g_ref, o_ref, send_sems, recv_sems):
        x, y, c = _position()
        chips = _other_chips(x, y)

        def copy(j, px, py):
            return pltpu.make_async_remote_copy(
                src_ref=g_ref.at[2 * px + py], dst_ref=o_ref.at[j], send_sem=send_sems.at[j],
                recv_sem=recv_sems.at[j], device_id=(px, py, c), device_id_type=MESH)

        sends = [copy(j, px, py) for j, (px, py) in enumerate(chips)]
        for cp in sends:
            cp.start()
        for cp in sends:
            cp.wait_recv()
        for cp in sends:
            cp.wait_send()

    return pl.pallas_call(
        body, name=name,
        in_specs=[pl.BlockSpec(memory_space=pl.ANY)],
        out_specs=pl.BlockSpec(memory_space=pl.ANY),
        out_shape=jax.ShapeDtypeStruct((3, r, w), g.dtype),
        scratch_shapes=[pltpu.SemaphoreType.DMA((3,)), pltpu.SemaphoreType.DMA((3,))],
    )(g)


def _sum_own_and_received(g, recv, *, name):
    _, r, w = g.shape
    tm = _tile(r)
    x, y, _ = _position()
    me = jnp.reshape(2 * x + y, (1,)).astype(jnp.int32)

    def body(me_ref, g_ref, r_ref, o_ref):
        o_ref[...] = g_ref[0] + r_ref[0] + r_ref[1] + r_ref[2]

    return pl.pallas_call(
        body, name=name,
        grid_spec=pltpu.PrefetchScalarGridSpec(
            num_scalar_prefetch=1, grid=(r // tm,),
            in_specs=[pl.BlockSpec((1, tm, w), lambda i, me_ref: (me_ref[0], i, 0)),
                      pl.BlockSpec((3, tm, w), lambda i, me_ref: (0, i, 0))],
            out_specs=pl.BlockSpec((tm, w), lambda i, me_ref: (i, 0))),
        out_shape=jax.ShapeDtypeStruct((r, w), f32),
        compiler_params=_params(("parallel",)),
    )(me, g, recv)


def _swap_cores(a, *, name):
    def body(a_ref, o_ref, send_sem, recv_sem):
        x, y, c = _position()
        cp = pltpu.make_async_remote_copy(src_ref=a_ref, dst_ref=o_ref, send_sem=send_sem, recv_sem=recv_sem,
                                          device_id=(x, y, 1 - c), device_id_type=MESH)
        cp.start()
        cp.wait_recv()
        cp.wait_send()

    return pl.pallas_call(
        body, name=name,
        in_specs=[pl.BlockSpec(memory_space=pl.ANY)],
        out_specs=pl.BlockSpec(memory_space=pl.ANY),
        out_shape=jax.ShapeDtypeStruct(a.shape, a.dtype),
        scratch_shapes=[pltpu.SemaphoreType.DMA, pltpu.SemaphoreType.DMA],
    )(a)


def _all_reduce_small(a, *, name):
    rows, w = a.shape

    def body(a_ref, o_ref, buf, send_sems, recv_sems):
        x, y, c = _position()
        me = 4 * x + 2 * y + c
        buf[0] = a_ref[...]
        sends = []
        for rel in range(1, N_DEV):
            peer = ((1 - x) if rel & 4 else x, (1 - y) if rel & 2 else y, (1 - c) if rel & 1 else c)
            cp = pltpu.make_async_remote_copy(
                src_ref=a_ref, dst_ref=buf.at[rel], send_sem=send_sems.at[rel - 1], recv_sem=recv_sems.at[rel - 1],
                device_id=peer, device_id_type=MESH)
            cp.start()
            sends.append(cp)
        for cp in sends:
            cp.wait_recv()
        for cp in sends:
            cp.wait_send()
        acc = buf[jnp.bitwise_xor(me, 0)]
        for d in range(1, N_DEV):
            acc = acc + buf[jnp.bitwise_xor(me, d)]
        o_ref[...] = acc

    return pl.pallas_call(
        body, name=name,
        in_specs=[pl.BlockSpec(memory_space=pltpu.VMEM)],
        out_specs=pl.BlockSpec(memory_space=pltpu.VMEM),
        out_shape=jax.ShapeDtypeStruct((rows, w), f32),
        scratch_shapes=[pltpu.VMEM((N_DEV, rows, w), f32), pltpu.SemaphoreType.DMA((N_DEV - 1,)),
                        pltpu.SemaphoreType.DMA((N_DEV - 1,))],
    )(a)


def _adamw(w, g, m, v, *, name):
    rows, cols = w.shape
    tm = _tile(rows, 256)

    def body(w_ref, g_ref, m_ref, v_ref, d_ref, nm_ref, nv_ref):
        gv = g_ref[...]
        nm = ADAM_B1 * m_ref[...] + (1.0 - ADAM_B1) * gv
        nv = ADAM_B2 * v_ref[...] + (1.0 - ADAM_B2) * (gv * gv)
        m_hat = nm / (1.0 - ADAM_B1 ** ADAM_STEP)
        v_hat = nv / (1.0 - ADAM_B2 ** ADAM_STEP)
        d_ref[...] = -ADAM_LR * (m_hat / (jnp.sqrt(v_hat) + ADAM_EPS) + ADAM_WD * w_ref[...])
        nm_ref[...] = nm
        nv_ref[...] = nv

    spec = pl.BlockSpec((tm, cols), lambda i: (i, 0))
    shape = jax.ShapeDtypeStruct((rows, cols), f32)
    return pl.pallas_call(
        body, name=name, grid=(rows // tm,), in_specs=[spec] * 4, out_specs=[spec] * 3, out_shape=[shape] * 3,
        compiler_params=_params(("parallel",)),
    )(w, g, m, v)


def _rows_of(a):
    return a.reshape(-1, PACK_W)


def _pad_rows(a, rows):
    return jnp.concatenate([a, jnp.zeros((rows - a.shape[0], a.shape[1]), a.dtype)], axis=0) if rows > a.shape[0] else a


_SHARD_SECTIONS = (("w_in_t", D_IN // N_CHIPS, D_MODEL), ("w_gate_t", D_FF // N_CHIPS, D_MODEL),
                   ("w_up_t", D_FF // N_CHIPS, D_MODEL), ("w_down", D_FF // N_CHIPS, D_MODEL),
                   ("w_o", D_MODEL // N_CHIPS, D_MODEL), ("w_br_attn_t", D_MODEL // N_CHIPS, Q_W),
                   ("w_br_rwkv_t", D_MODEL // N_CHIPS, RWKV_DIM), ("g2_t", RWKV_DIM // N_CHIPS, GATE_LORA),
                   ("w2_t", RWKV_DIM // N_CHIPS, DECAY_LORA), ("a2_t", RWKV_DIM // N_CHIPS, AAA_LORA))
_SMALL = (("norm_mix_g", D_MODEL), ("b_in", D_IN), ("sinks", Q_HEADS), ("mix", RWKV_PROJ), ("w0", RWKV_DIM),
          ("a0", RWKV_DIM), ("k_k", RWKV_DIM), ("k_a", RWKV_DIM), ("r_k", RWKV_DIM), ("ln_w", RWKV_DIM),
          ("ln_b", RWKV_DIM), ("norm_ffn_g", D_MODEL), ("norm_final_g", D_MODEL))


def _section_rows(rows, cols):
    return rows * cols // PACK_W


def _pack_sections(d, dtype):
    return jnp.concatenate([_rows_of(d[n].astype(dtype)) for n, _, _ in _SHARD_SECTIONS], axis=0)


def _unpack_sections(pack):
    out, off = {}, 0
    lead = pack.shape[:-2]
    for n, rows, cols in _SHARD_SECTIONS:
        k = _section_rows(rows, cols)
        out[n] = pack[..., off:off + k, :].reshape(*lead, rows, cols)
        off += k
    return out


def _pack_small(d):
    flat = jnp.concatenate([d[n].reshape(-1).astype(f32) for n, _ in _SMALL])
    return flat


def _unpack_small(flat):
    out, off = {}, 0
    for n, size in _SMALL:
        out[n] = flat[off:off + size]
        off += size
    return out


_SMALL_TOTAL = sum(s for _, s in _SMALL)


def kernel(x, meta_tokens, norm_mix_g, w_in, b_in, attn_sinks, rwkv_mix, rwkv_w0, rwkv_w2, rwkv_a0, rwkv_a2, rwkv_g2, rwkv_k_k, rwkv_k_a, rwkv_r_k, rwkv_ln_w, rwkv_ln_b, w_br_attn, w_br_rwkv, w_o, norm_ffn_g, w_ffn_gate, w_ffn_up, w_ffn_down, norm_final_g, loss_target, m_meta_tokens, m_norm_mix_g, m_w_in, m_b_in, m_attn_sinks, m_rwkv_mix, m_rwkv_w0, m_rwkv_w2, m_rwkv_a0, m_rwkv_a2, m_rwkv_g2, m_rwkv_k_k, m_rwkv_k_a, m_rwkv_r_k, m_rwkv_ln_w, m_rwkv_ln_b, m_w_br_attn, m_w_br_rwkv, m_w_o, m_norm_ffn_g, m_w_ffn_gate, m_w_ffn_up, m_w_ffn_down, m_norm_final_g, v_meta_tokens, v_norm_mix_g, v_w_in, v_b_in, v_attn_sinks, v_rwkv_mix, v_rwkv_w0, v_rwkv_w2, v_rwkv_a0, v_rwkv_a2, v_rwkv_g2, v_rwkv_k_k, v_rwkv_k_a, v_rwkv_r_k, v_rwkv_ln_w, v_rwkv_ln_b, v_w_br_attn, v_w_br_rwkv, v_w_o, v_norm_ffn_g, v_w_ffn_gate, v_w_ffn_up, v_w_ffn_down, v_norm_final_g):
    names = ("meta_tokens", "norm_mix_g", "w_in", "b_in", "attn_sinks", "rwkv_mix", "rwkv_w0", "rwkv_w2", "rwkv_a0",
             "rwkv_a2", "rwkv_g2", "rwkv_k_k", "rwkv_k_a", "rwkv_r_k", "rwkv_ln_w", "rwkv_ln_b", "w_br_attn",
             "w_br_rwkv", "w_o", "norm_ffn_g", "w_ffn_gate", "w_ffn_up", "w_ffn_down", "norm_final_g")
    w_all = dict(zip(names, (meta_tokens, norm_mix_g, w_in, b_in, attn_sinks, rwkv_mix, rwkv_w0, rwkv_w2, rwkv_a0,
                             rwkv_a2, rwkv_g2, rwkv_k_k, rwkv_k_a, rwkv_r_k, rwkv_ln_w, rwkv_ln_b, w_br_attn,
                             w_br_rwkv, w_o, norm_ffn_g, w_ffn_gate, w_ffn_up, w_ffn_down, norm_final_g)))
    m_all = dict(zip(names, (m_meta_tokens, m_norm_mix_g, m_w_in, m_b_in, m_attn_sinks, m_rwkv_mix, m_rwkv_w0,
                             m_rwkv_w2, m_rwkv_a0, m_rwkv_a2, m_rwkv_g2, m_rwkv_k_k, m_rwkv_k_a, m_rwkv_r_k,
                             m_rwkv_ln_w, m_rwkv_ln_b, m_w_br_attn, m_w_br_rwkv, m_w_o, m_norm_ffn_g, m_w_ffn_gate,
                             m_w_ffn_up, m_w_ffn_down, m_norm_final_g)))
    v_all = dict(zip(names, (v_meta_tokens, v_norm_mix_g, v_w_in, v_b_in, v_attn_sinks, v_rwkv_mix, v_rwkv_w0,
                             v_rwkv_w2, v_rwkv_a0, v_rwkv_a2, v_rwkv_g2, v_rwkv_k_k, v_rwkv_k_a, v_rwkv_r_k,
                             v_rwkv_ln_w, v_rwkv_ln_b, v_w_br_attn, v_w_br_rwkv, v_w_o, v_norm_ffn_g, v_w_ffn_gate,
                             v_w_ffn_up, v_w_ffn_down, v_norm_final_g)))
    cx, cy, _ = _position()
    chip = 2 * cx + cy

    t_of = dict(w_in_t="w_in", w_gate_t="w_ffn_gate", w_up_t="w_ffn_up", w_br_attn_t="w_br_attn",
                w_br_rwkv_t="w_br_rwkv", g2_t="rwkv_g2", w2_t="rwkv_w2", a2_t="rwkv_a2")
    plain_of = dict(w_down="w_ffn_down", w_o="w_o")

    def shard_views(src):
        d = {k: src[n][0].T for k, n in t_of.items()}
        d.update({k: src[n][0] for k, n in plain_of.items()})
        return d

    meta_bits = lax.bitcast_convert_type(meta_tokens.T, bf16).reshape(-1, PACK_W)
    pack = jnp.concatenate([_pack_sections(shard_views(w_all), bf16), meta_bits], axis=0)
    pack_rows = -(-pack.shape[0] // BLOCK) * BLOCK
    gathered = _gather_chips(_pad_rows(pack, pack_rows), name="gather_weights")
    full = _unpack_sections(gathered)
    meta_off = sum(_section_rows(r, c) for _, r, c in _SHARD_SECTIONS)
    meta_rows = meta_tokens.size * 2 // PACK_W
    meta_cols = meta_tokens.shape[1]
    meta_full = lax.bitcast_convert_type(
        gathered[:, meta_off:meta_off + meta_rows, :].reshape(N_CHIPS, meta_cols, N_META, 2), f32)
    meta_full = meta_full.reshape(N_CHIPS * meta_cols, N_META).T

    def cat(k):
        return full[k].reshape(-1, full[k].shape[-1])

    p = dict(
        w_in_t=cat("w_in_t"), w_gate_t=cat("w_gate_t"), w_up_t=cat("w_up_t"), w_down=cat("w_down"), w_o=cat("w_o"),
        w_br_attn_t=cat("w_br_attn_t"), w_br_rwkv_t=cat("w_br_rwkv_t"),
        g2=cat("g2_t").T.astype(f32), w2=cat("w2_t").T.astype(f32), a2=cat("a2_t").T.astype(f32),
        b_in=b_in, sinks=attn_sinks, mix=rwkv_mix, w0=rwkv_w0, a0=rwkv_a0, k_k=rwkv_k_k, k_a=rwkv_k_a,
        r_k=rwkv_r_k.reshape(1, RWKV_DIM), ln_w=rwkv_ln_w, ln_b=rwkv_ln_b, norm_mix_g=norm_mix_g,
        norm_ffn_g=norm_ffn_g, norm_final_g=norm_final_g.reshape(1, D_MODEL),
    )

    loss, dx, g = _local_step(x[0], loss_target[0], meta_full, p)

    g_views = dict(g)
    g_views.update(g2_t=g["g2"].T, w2_t=g["w2"].T, a2_t=g["a2"].T)
    g_pack = jnp.concatenate(
        [g_views[n].reshape(N_CHIPS, _section_rows(r, c), PACK_W) for n, r, c in _SHARD_SECTIONS], axis=1)
    g_rows = -(-g_pack.shape[1] // BLOCK) * BLOCK
    g_pack = jnp.concatenate([g_pack, jnp.zeros((N_CHIPS, g_rows - g_pack.shape[1], PACK_W), f32)], axis=1)
    received = _scatter_chips(g_pack, name="scatter_grads")
    part = _sum_own_and_received(g_pack, received, name="sum_chips")
    other = _swap_cores(part, name="swap_cores")
    (g_shard,) = _rowwise(lambda a, b: (a + b,), [part, other], [], [(PACK_W, f32)], name="sum_cores")
    gs = _unpack_sections(g_shard)

    small = jnp.concatenate([_pack_small(g), loss.reshape(1)])
    small_rows = -(-small.shape[0] // PACK_W)
    small = jnp.concatenate([small, jnp.zeros((small_rows * PACK_W - small.shape[0],), f32)]).reshape(small_rows, PACK_W)
    small_rows8 = -(-(small_rows + N_META) // 8) * 8
    reduced = _all_reduce_small(_pad_rows(jnp.concatenate([g["meta"], small], axis=0), small_rows8),
                                name="reduce_small")
    g_meta = lax.dynamic_slice_in_dim(reduced[:N_META], chip * meta_cols, meta_cols, axis=1)
    flat = reduced[N_META:N_META + small_rows].reshape(-1)
    g_small = _unpack_small(flat)
    loss_total = flat[_SMALL_TOTAL]

    small_of = dict(norm_mix_g="norm_mix_g", b_in="b_in", attn_sinks="sinks", rwkv_mix="mix", rwkv_w0="w0",
                    rwkv_a0="a0", rwkv_k_k="k_k", rwkv_k_a="k_a", rwkv_r_k="r_k", rwkv_ln_w="ln_w",
                    rwkv_ln_b="ln_b", norm_ffn_g="norm_ffn_g", norm_final_g="norm_final_g")
    grads = {"meta_tokens": g_meta}
    for n, k in small_of.items():
        grads[n] = g_small[k].reshape(w_all[n].shape)
    for k, n in t_of.items():
        grads[n] = gs[k].T.reshape(w_all[n].shape)
    for k, n in plain_of.items():
        grads[n] = gs[k].reshape(w_all[n].shape)

    big = ("w_in", "w_ffn_gate", "w_ffn_up", "w_ffn_down", "w_o", "w_br_attn", "w_br_rwkv", "rwkv_g2", "rwkv_w2",
           "rwkv_a2")
    delta, new_m, new_v = {}, {}, {}
    for n in big:
        shape2 = w_all[n].shape[1:]
        d_, m_, v_ = _adamw(w_all[n].reshape(shape2), grads[n].reshape(shape2), m_all[n].reshape(shape2),
                            v_all[n].reshape(shape2), name="adamw_" + n)
        delta[n], new_m[n], new_v[n] = (t.reshape(w_all[n].shape) for t in (d_, m_, v_))
    rest = [n for n in names if n not in big]

    def pack_rest(src):
        flat_ = jnp.concatenate([src[n].reshape(-1) for n in rest])
        rows_ = -(-flat_.shape[0] // (8 * PACK_W)) * 8
        return jnp.concatenate([flat_, jnp.ones((rows_ * PACK_W - flat_.shape[0],), f32)]).reshape(rows_, PACK_W)

    d_, m_, v_ = _adamw(pack_rest(w_all), pack_rest(grads), pack_rest(m_all), pack_rest(v_all), name="adamw_small")
    off = 0
    for n in rest:
        size = w_all[n].size
        for dst, src in ((delta, d_), (new_m, m_), (new_v, v_)):
            dst[n] = src.reshape(-1)[off:off + size].reshape(w_all[n].shape)
        off += size

    return (loss_total, dx.reshape(x.shape), *[grads[n] for n in names], *[delta[n] for n in names],
            *[new_m[n] for n in names], *[new_v[n] for n in names])
```

```python
import functools
import math

import jax
import jax.numpy as jnp
from jax import lax
from jax.experimental import pallas as pl
from jax.experimental.pallas import tpu as pltpu

f32 = jnp.float32
bf16 = jnp.bfloat16

D_MODEL = 1024
N_META = 16
HEAD_DIM = 64
Q_HEADS = 8
KV_HEADS = 2
GROUP = Q_HEADS // KV_HEADS
WINDOW = 128
BLOCK = 128
ROPE_THETA = 500000.0
ROPE_DIM = HEAD_DIM // 4
RWKV_HEADS = 8
RWKV_HEAD = 64
RWKV_DIM = RWKV_HEADS * RWKV_HEAD
DECAY_LORA = 64
AAA_LORA = 64
GATE_LORA = 160
LORA_W = DECAY_LORA + AAA_LORA + GATE_LORA
RWKV_LN_EPS = 64e-5
D_FF = 2816
Q_W = Q_HEADS * HEAD_DIM
KV_W = KV_HEADS * HEAD_DIM
ATTN_PROJ = Q_W + 2 * KV_W
RKV_W = 3 * RWKV_DIM
RWKV_PROJ = RKV_W + LORA_W
D_IN = ATTN_PROJ + RWKV_PROJ + 2 * D_MODEL
RMS_EPS = 1e-6
NEG_INF = -1e30
PAD = BLOCK - N_META
FRONT = PAD + N_META

ADAM_LR = 0.001
ADAM_B1 = 0.9
ADAM_B2 = 0.999
ADAM_EPS = 1e-08
ADAM_WD = 0.01
ADAM_STEP = 10

N_CHIPS = 4
N_DEV = 8
CHUNK = 64
VMEM_LIMIT = 56 * 1024 * 1024
PACK_W = 1024
MESH = pl.DeviceIdType.MESH
HIGHEST = lax.Precision.HIGHEST


def _tile(m, pref=384):
    for t in (pref, 256, 128, 64, 32, 16, 8):
        if t <= m and m % t == 0:
            return t
    return m


def _params(sem=None):
    return pltpu.CompilerParams(dimension_semantics=sem, vmem_limit_bytes=VMEM_LIMIT)


def _full(shape):
    nd = len(shape)
    return pl.BlockSpec(shape, lambda *_: (0,) * nd)


def _dot(a, b, dims="nn", exact=False):
    dn = {"nn": (((1,), (0,)), ((), ())), "nt": (((1,), (1,)), ((), ())), "tn": (((0,), (0,)), ((), ()))}[dims]
    if exact:
        return lax.dot_general(a.astype(f32), b.astype(f32), dn, precision=HIGHEST, preferred_element_type=f32)
    return lax.dot_general(a.astype(bf16), b.astype(bf16), dn, preferred_element_type=f32)


def _dot3(a, b):
    a_hi, b_hi = a.astype(bf16), b.astype(bf16)
    a_lo = (a - a_hi.astype(f32)).astype(bf16)
    b_lo = (b - b_hi.astype(f32)).astype(bf16)
    return _dot(a_hi, b_hi) + (_dot(a_hi, b_lo) + _dot(a_lo, b_hi))


def _mm(a, b, mode, *, name, out_dtype=f32, bias=None, add=None, zero_rows_below=0):
    m, _ = a.shape
    n = b.shape[1] if mode == "nn" else b.shape[0]
    tm = _tile(m)
    has_bias, has_add = bias is not None, add is not None

    def body(*refs):
        a_ref, b_ref = refs[0], refs[1]
        o_ref = refs[-1]
        acc = _dot(a_ref[...], b_ref[...], mode)
        k = 2
        if has_bias:
            acc = acc + refs[k][...]
            k += 1
        if zero_rows_below:
            rows = pl.program_id(0) * tm + lax.broadcasted_iota(jnp.int32, acc.shape, 0)
            acc = jnp.where(rows >= zero_rows_below, acc, 0.0)
        if has_add:
            acc = acc + refs[k][...].astype(f32)
        o_ref[...] = acc.astype(out_dtype)

    ins = [a, b]
    in_specs = [pl.BlockSpec((tm, a.shape[1]), lambda i: (i, 0)), _full(b.shape)]
    if has_bias:
        ins.append(bias)
        in_specs.append(_full(bias.shape))
    if has_add:
        ins.append(add)
        in_specs.append(pl.BlockSpec((tm, n), lambda i: (i, 0)))
    return pl.pallas_call(
        body, name=name, grid=(m // tm,), in_specs=in_specs,
        out_specs=pl.BlockSpec((tm, n), lambda i: (i, 0)),
        out_shape=jax.ShapeDtypeStruct((m, n), out_dtype),
        compiler_params=_params(("parallel",)),
    )(*ins)


def _mm_tn(a, b, *, name, colsum=False):
    r, m = a.shape
    n = b.shape[1]
    tr = _tile(r)
    tmo = m
    for cand in (1408, 1024, 768, 512):
        if m > 1024 and m % cand == 0:
            tmo = cand
            break

    def body(a_ref, b_ref, o_ref, *cs):
        i = pl.program_id(1)

        @pl.when(i == 0)
        def _():
            o_ref[...] = jnp.zeros_like(o_ref)
            if colsum:
                cs[0][...] = jnp.zeros_like(cs[0])

        o_ref[...] += _dot(a_ref[...], b_ref[...], "tn")
        if colsum:
            cs[0][...] += jnp.sum(a_ref[...].astype(f32), axis=0, keepdims=True)

    out_shape = [jax.ShapeDtypeStruct((m, n), f32)]
    out_specs = [pl.BlockSpec((tmo, n), lambda j, i: (j, 0))]
    if colsum:
        out_shape.append(jax.ShapeDtypeStruct((1, m), f32))
        out_specs.append(pl.BlockSpec((1, tmo), lambda j, i: (0, j)))
    res = pl.pallas_call(
        body, name=name, grid=(m // tmo, r // tr),
        in_specs=[pl.BlockSpec((tr, tmo), lambda j, i: (i, j)), pl.BlockSpec((tr, n), lambda j, i: (i, 0))],
        out_specs=out_specs, out_shape=out_shape,
        compiler_params=_params(("parallel", "arbitrary")),
    )(a, b)
    return res if colsum else res[0]


def _rowwise(fn, rows, params, outs, *, name, tm=None, with_row0=False):
    m = rows[0].shape[0]
    tm = tm or _tile(m)
    nr, npar = len(rows), len(params)

    def body(*refs):
        vals = [r[...] for r in refs[:nr + npar]]
        kw = dict(row0=pl.program_id(0) * tm) if with_row0 else {}
        res = fn(*vals, **kw)
        for o_ref, v in zip(refs[nr + npar:], res):
            o_ref[...] = v.astype(o_ref.dtype)

    return pl.pallas_call(
        body, name=name, grid=(m // tm,),
        in_specs=[pl.BlockSpec((tm, r.shape[1]), lambda i: (i, 0)) for r in rows] + [_full(p.shape) for p in params],
        out_specs=[pl.BlockSpec((tm, w), lambda i: (i, 0)) for w, _ in outs],
        out_shape=[jax.ShapeDtypeStruct((m, w), dt) for w, dt in outs],
        compiler_params=_params(("parallel",)),
    )(*rows, *params)


def _rowwise_bwd(fn, rows, params, cts, *, name, diff_rows, diff_params, tm=None, with_row0=False, zero_rows_below=0):
    m = rows[0].shape[0]
    tm = tm or _tile(m)
    nr, npar = len(rows), len(params)
    d_idx = [i for i in range(nr) if diff_rows[i]]
    p_idx = [i for i in range(npar) if diff_params[i]]
    flat_cts = [c for group in cts for c in group]
    n_ct = len(flat_cts)

    def body(*refs):
        vals = [r[...] for r in refs[:nr + npar]]
        ct_refs = refs[nr + npar:nr + npar + n_ct]
        out_refs = refs[nr + npar + n_ct:]
        kw = dict(row0=pl.program_id(0) * tm) if with_row0 else {}
        ct_vals, k = [], 0
        for group in cts:
            acc = ct_refs[k][...].astype(f32)
            for extra in range(1, len(group)):
                acc = acc + ct_refs[k + extra][...].astype(f32)
            k += len(group)
            if zero_rows_below:
                rr = pl.program_id(0) * tm + lax.broadcasted_iota(jnp.int32, acc.shape, 0)
                acc = jnp.where(rr >= zero_rows_below, acc, 0.0)
            ct_vals.append(acc)

        def g(*dargs):
            full = list(vals)
            for pos, i in enumerate(d_idx):
                full[i] = dargs[pos]
            for pos, i in enumerate(p_idx):
                full[nr + i] = dargs[len(d_idx) + pos]
            return tuple(fn(*full, **kw))

        _, vjp = jax.vjp(g, *[vals[i] for i in d_idx], *[vals[nr + i] for i in p_idx])
        grads = vjp(tuple(ct_vals))
        for pos in range(len(d_idx)):
            out_refs[pos][...] = grads[pos].astype(out_refs[pos].dtype)
        first = pl.program_id(0) == 0
        for pos in range(len(p_idx)):
            o_ref = out_refs[len(d_idx) + pos]

            @pl.when(first)
            def _(o_ref=o_ref):
                o_ref[...] = jnp.zeros_like(o_ref)

            o_ref[...] += grads[len(d_idx) + pos]

    return pl.pallas_call(
        body, name=name, grid=(m // tm,),
        in_specs=[pl.BlockSpec((tm, r.shape[1]), lambda i: (i, 0)) for r in rows] + [_full(p.shape) for p in params]
        + [pl.BlockSpec((tm, c.shape[1]), lambda i: (i, 0)) for c in flat_cts],
        out_specs=[pl.BlockSpec((tm, rows[i].shape[1]), lambda i_: (i_, 0)) for i in d_idx]
        + [_full(params[i].shape) for i in p_idx],
        out_shape=[jax.ShapeDtypeStruct(rows[i].shape, f32) for i in d_idx]
        + [jax.ShapeDtypeStruct(params[i].shape, f32) for i in p_idx],
        compiler_params=_params(("arbitrary",)),
    )(*rows, *params, *flat_cts)


def _rms(x, g):
    return x * lax.rsqrt(jnp.mean(x * x, axis=-1, keepdims=True) + RMS_EPS) * g


def _head_sum_matrix(width, head):
    idx = jnp.arange(width) // head
    return (idx[:, None] == idx[None, :]).astype(f32)


def _rope_tables(lp):
    half = ROPE_DIM // 2
    pos = (jnp.arange(lp) - PAD).astype(f32)
    inv_freq = jnp.power(jnp.float32(ROPE_THETA), -jnp.arange(half, dtype=f32) * (2.0 / ROPE_DIM))
    ang = pos[:, None] * inv_freq[None, :]
    cos, sin = jnp.cos(ang), jnp.sin(ang)
    ones = jnp.ones((lp, HEAD_DIM - ROPE_DIM), f32)
    zeros = jnp.zeros((lp, HEAD_DIM - ROPE_DIM), f32)
    cos_t = jnp.concatenate([cos, cos, ones], axis=1)
    sin_t = jnp.concatenate([-sin, sin, zeros], axis=1)
    i = jnp.arange(HEAD_DIM)
    src = jnp.where(i < half, i + half, jnp.where(i < ROPE_DIM, i - half, i))
    swap = ((i[:, None] == src[None, :]) & (i[None, :] < ROPE_DIM)).astype(f32)
    return cos_t, sin_t, swap


def _attn_prep(qkv, cos_t, sin_t, swap):
    outs = []
    for h in range(Q_HEADS + KV_HEADS):
        t = qkv[:, h * HEAD_DIM:(h + 1) * HEAD_DIM]
        outs.append(t * cos_t + _dot(t, swap, exact=True) * sin_t)
    q = jnp.concatenate(outs[:Q_HEADS], axis=1)
    k = jnp.concatenate(outs[Q_HEADS:], axis=1)
    return q, k, qkv[:, Q_W + KV_W:]


def _softplus(z):
    return jnp.maximum(z, 0.0) + jnp.log1p(jnp.exp(-jnp.abs(z)))


def _rwkv_prep(rkv, lora, w0, w2, a0, a2, g2, k_k, k_a, hsum):
    r = rkv[:, :RWKV_DIM]
    k = rkv[:, RWKV_DIM:2 * RWKV_DIM]
    v = rkv[:, 2 * RWKV_DIM:]
    dw = lora[:, :DECAY_LORA]
    da = lora[:, DECAY_LORA:DECAY_LORA + AAA_LORA]
    dg = lora[:, DECAY_LORA + AAA_LORA:]
    w = -_softplus(-(w0 + _dot(jnp.tanh(dw), w2))) - 0.5
    a = jax.nn.sigmoid(a0 + _dot(da, a2))
    g = _dot(jax.nn.sigmoid(dg), g2)
    kk = k * k_k
    kk = kk * lax.rsqrt(jnp.maximum(_dot(kk * kk, hsum, exact=True), 1e-24))
    k = k * (1.0 + (a - 1.0) * k_a)
    log_decay = -jnp.exp(w)
    return r, log_decay, k, v, -kk, kk * a, g


def _rwkv_post(y, r, k, v, g, ln_w, ln_b, r_k, hmean):
    hsum = hmean * RWKV_HEAD
    mean = _dot(y, hmean, exact=True)
    yc = y - mean
    var = _dot(yc * yc, hmean, exact=True)
    yn = yc * lax.rsqrt(var + RWKV_LN_EPS) * ln_w + ln_b
    bonus = _dot(r * k * r_k, hsum, exact=True) * v
    return ((yn + bonus) * g,)


def _merge(gates, br_a, br_r):
    sg = jax.nn.sigmoid(gates)
    return (sg[:, :D_MODEL] * br_a + sg[:, D_MODEL:] * br_r,)


def _swiglu(gate, up):
    return (jax.nn.silu(gate) * up,)


def _token_shift(p, mix, *, name):
    m, c = p.shape
    tm = _tile(m)
    sub = tm // 8

    def body(p_ref, prev_ref, mix_ref, o_ref):
        x = p_ref[...]
        rows = lax.broadcasted_iota(jnp.int32, x.shape, 0)
        last = jnp.where(pl.program_id(0) == 0, 0.0, prev_ref[7:8, :])
        xp = jnp.where(rows == 0, last, pltpu.roll(x, 1, axis=0))
        o_ref[...] = x + (xp - x) * mix_ref[...]

    return pl.pallas_call(
        body, name=name, grid=(m // tm,),
        in_specs=[pl.BlockSpec((tm, c), lambda i: (i, 0)),
                  pl.BlockSpec((8, c), lambda i: (jnp.maximum(i * sub - 1, 0), 0)),
                  _full(mix.shape)],
        out_specs=pl.BlockSpec((tm, c), lambda i: (i, 0)),
        out_shape=jax.ShapeDtypeStruct((m, c), f32),
        compiler_params=_params(("parallel",)),
    )(p, p, mix)


def _token_shift_bwd(p, mix, dpf, *, name):
    m, c = p.shape
    tm = _tile(m)
    sub = tm // 8
    n_tiles = m // tm

    def body(p_ref, prev_ref, mix_ref, d_ref, nxt_ref, dp_ref, dmix_ref):
        i = pl.program_id(0)
        x = p_ref[...]
        d = d_ref[...]
        mixv = mix_ref[...]
        rows = lax.broadcasted_iota(jnp.int32, x.shape, 0)
        last = jnp.where(i == 0, 0.0, prev_ref[7:8, :])
        xp = jnp.where(rows == 0, last, pltpu.roll(x, 1, axis=0))
        dm = d * mixv
        first_next = jnp.where(i == n_tiles - 1, 0.0, nxt_ref[0:1, :] * mixv)
        dm_next = jnp.where(rows == tm - 1, first_next, pltpu.roll(dm, tm - 1, axis=0))
        dp = d - dm + dm_next
        dp_ref[...] = jnp.where(i * tm + rows >= PAD, dp, 0.0)

        @pl.when(i == 0)
        def _():
            dmix_ref[...] = jnp.zeros_like(dmix_ref)

        dmix_ref[...] += jnp.sum(d * (xp - x), axis=0, keepdims=True)

    return pl.pallas_call(
        body, name=name, grid=(n_tiles,),
        in_specs=[pl.BlockSpec((tm, c), lambda i: (i, 0)),
                  pl.BlockSpec((8, c), lambda i: (jnp.maximum(i * sub - 1, 0), 0)),
                  _full(mix.shape),
                  pl.BlockSpec((tm, c), lambda i: (i, 0)),
                  pl.BlockSpec((8, c), lambda i: (jnp.minimum((i + 1) * sub, m // 8 - 1), 0))],
        out_specs=[pl.BlockSpec((tm, c), lambda i: (i, 0)), _full(mix.shape)],
        out_shape=[jax.ShapeDtypeStruct((m, c), f32), jax.ShapeDtypeStruct(mix.shape, f32)],
        compiler_params=_params(("arbitrary",)),
    )(p, p, mix, dpf, dpf)


def _attn_heads(q, kp, kc, km, vp, vc, vm, sink, blk):
    scale = HEAD_DIM ** -0.5
    heads = range(len(q))
    qi = lax.broadcasted_iota(jnp.int32, (BLOCK, BLOCK), 0)
    ki = lax.broadcasted_iota(jnp.int32, (BLOCK, BLOCK), 1)
    qpos = blk * BLOCK + qi - PAD
    kpos_c = blk * BLOCK + ki - PAD
    kpos_p = kpos_c - BLOCK
    kpos_m = ki - PAD

    def band(kpos):
        return (kpos >= N_META) & (kpos <= qpos) & (qpos - kpos < WINDOW)

    ok_p, ok_c, ok_m = band(kpos_p), band(kpos_c), (kpos_m >= 0) & (kpos_m <= qpos)
    s_p = [jnp.where(ok_p, _dot(q[i], kp[i // GROUP], "nt") * scale, NEG_INF) for i in heads]
    s_c = [jnp.where(ok_c, _dot(q[i], kc[i // GROUP], "nt") * scale, NEG_INF) for i in heads]
    s_m = [jnp.where(ok_m, _dot(q[i], km[i // GROUP], "nt") * scale, NEG_INF) for i in heads]
    mx = [jnp.maximum(jnp.maximum(jnp.max(s_p[i], -1, keepdims=True), jnp.max(s_c[i], -1, keepdims=True)),
                      jnp.maximum(jnp.max(s_m[i], -1, keepdims=True), sink[i])) for i in heads]
    e_p = [jnp.exp(s_p[i] - mx[i]) for i in heads]
    e_c = [jnp.exp(s_c[i] - mx[i]) for i in heads]
    e_m = [jnp.exp(s_m[i] - mx[i]) for i in heads]
    inv = [1.0 / (jnp.sum(e_p[i], -1, keepdims=True) + jnp.sum(e_c[i], -1, keepdims=True)
                  + jnp.sum(e_m[i], -1, keepdims=True) + jnp.exp(sink[i] - mx[i])) for i in heads]
    return [_dot(e_p[i] * inv[i], vp[i // GROUP]) + _dot(e_c[i] * inv[i], vc[i // GROUP])
            + _dot(e_m[i] * inv[i], vm[i // GROUP]) for i in heads]


def _attn_specs():
    prev = lambda i: (jnp.maximum(i - 1, 0), 0)
    cur = lambda i: (i, 0)
    meta = lambda i: (0, 0)
    return [pl.BlockSpec((BLOCK, Q_W), cur),
            pl.BlockSpec((BLOCK, KV_W), prev), pl.BlockSpec((BLOCK, KV_W), cur), pl.BlockSpec((BLOCK, KV_W), meta),
            pl.BlockSpec((BLOCK, KV_W), prev), pl.BlockSpec((BLOCK, KV_W), cur), pl.BlockSpec((BLOCK, KV_W), meta),
            _full((1, Q_HEADS))]


def _head_cols(i):
    return slice(i * HEAD_DIM, (i + 1) * HEAD_DIM)


def _attn_args(refs):
    q_ref, s_ref = refs[0], refs[7]
    args = [[q_ref[:, _head_cols(i)].astype(f32) for i in range(Q_HEADS)]]
    args += [[ref[:, _head_cols(h)].astype(f32) for h in range(KV_HEADS)] for ref in refs[1:7]]
    return args + [[s_ref[:, i:i + 1] for i in range(Q_HEADS)]]


def _attention(q, k, v, sinks, *, name):
    lp = q.shape[0]

    def body(*refs):
        o_ref = refs[-1]
        out = _attn_heads(*_attn_args(refs[:-1]), pl.program_id(0))
        for i in range(Q_HEADS):
            o_ref[:, _head_cols(i)] = out[i].astype(o_ref.dtype)

    return pl.pallas_call(
        body, name=name, grid=(lp // BLOCK,), in_specs=_attn_specs(),
        out_specs=pl.BlockSpec((BLOCK, Q_W), lambda i: (i, 0)),
        out_shape=jax.ShapeDtypeStruct((lp, Q_W), f32),
        compiler_params=_params(("parallel",)),
    )(q, k, k, k, v, v, v, sinks)


def _attention_bwd(q, k, v, sinks, do, *, name):
    lp = q.shape[0]
    nb = lp // BLOCK

    def body(*refs):
        ins, do_ref = refs[:8], refs[8]
        dq_ref, dkp_ref, dkc_ref, dkm_ref, dvp_ref, dvc_ref, dvm_ref, ds_ref = refs[9:]
        blk = pl.program_id(0)

        @pl.when(blk == 0)
        def _():
            dkm_ref[...] = jnp.zeros_like(dkm_ref)
            dvm_ref[...] = jnp.zeros_like(dvm_ref)
            ds_ref[...] = jnp.zeros_like(ds_ref)

        _, vjp = jax.vjp(functools.partial(_attn_heads, blk=blk), *_attn_args(ins))
        g = vjp([do_ref[:, _head_cols(i)].astype(f32) for i in range(Q_HEADS)])
        for i in range(Q_HEADS):
            dq_ref[:, _head_cols(i)] = g[0][i]
            ds_ref[:, i:i + 1] += g[7][i]
        for h in range(KV_HEADS):
            hs = _head_cols(h)
            dkp_ref[:, hs] = g[1][h]
            dkc_ref[:, hs] = g[2][h]
            dkm_ref[:, hs] += g[3][h]
            dvp_ref[:, hs] = g[4][h]
            dvc_ref[:, hs] = g[5][h]
            dvm_ref[:, hs] += g[6][h]

    blk_spec = pl.BlockSpec((BLOCK, KV_W), lambda i: (i, 0))
    one_spec = pl.BlockSpec((BLOCK, KV_W), lambda i: (0, 0))
    kv_shape = jax.ShapeDtypeStruct((lp, KV_W), f32)
    one_shape = jax.ShapeDtypeStruct((BLOCK, KV_W), f32)
    return pl.pallas_call(
        body, name=name, grid=(nb,),
        in_specs=_attn_specs() + [pl.BlockSpec((BLOCK, Q_W), lambda i: (i, 0))],
        out_specs=[pl.BlockSpec((BLOCK, Q_W), lambda i: (i, 0)), blk_spec, blk_spec, one_spec,
                   blk_spec, blk_spec, one_spec, _full((1, Q_HEADS))],
        out_shape=[jax.ShapeDtypeStruct((lp, Q_W), f32), kv_shape, kv_shape, one_shape,
                   kv_shape, kv_shape, one_shape, jax.ShapeDtypeStruct((1, Q_HEADS), f32)],
        compiler_params=_params(("arbitrary",)),
    )(q, k, k, k, v, v, v, sinks, do)


def _kv_combine(d_prev, d_cur, d_meta, *, name):
    lp = d_cur.shape[0]
    nb = lp // BLOCK

    def body(p_ref, c_ref, m_ref, o_ref):
        j = pl.program_id(0)
        acc = c_ref[...] + jnp.where(j < nb - 1, p_ref[...], 0.0)
        o_ref[...] = acc + jnp.where(j == 0, m_ref[...], 0.0)

    return pl.pallas_call(
        body, name=name, grid=(nb,),
        in_specs=[pl.BlockSpec((BLOCK, KV_W), lambda j: (jnp.minimum(j + 1, nb - 1), 0)),
                  pl.BlockSpec((BLOCK, KV_W), lambda j: (j, 0)),
                  pl.BlockSpec((BLOCK, KV_W), lambda j: (0, 0))],
        out_specs=pl.BlockSpec((BLOCK, KV_W), lambda j: (j, 0)),
        out_shape=jax.ShapeDtypeStruct((lp, KV_W), f32),
        compiler_params=_params(("parallel",)),
    )(d_prev, d_cur, d_meta)


def _scan_chunk(s0, r, lw, k, v, a, b):
    t = r[0].shape[0]
    ii = lax.broadcasted_iota(jnp.int32, (t, t), 0)
    jj = lax.broadcasted_iota(jnp.int32, (t, t), 1)
    incl = jj <= ii
    strict = jj < ii
    tri = incl.astype(f32)
    eye = jnp.where(ii == jj, 1.0, 0.0)
    cl = [_dot3(tri, x) for x in lw]
    e_pos = [jnp.exp(c) for c in cl]
    e_neg = [jnp.exp(-c) for c in cl]
    e_prev = [jnp.exp(c - x) for c, x in zip(cl, lw)]
    rt = [x * e for x, e in zip(r, e_pos)]
    at = [x * e for x, e in zip(a, e_prev)]
    bt = [x * e for x, e in zip(b, e_neg)]
    kt = [x * e for x, e in zip(k, e_neg)]
    l_ab = [jnp.where(strict, _dot(x, y, "nt"), 0.0) for x, y in zip(at, bt)]
    l_ak = [jnp.where(strict, _dot(x, y, "nt"), 0.0) for x, y in zip(at, kt)]
    r_b = [jnp.where(incl, _dot(x, y, "nt"), 0.0) for x, y in zip(rt, bt)]
    r_k = [jnp.where(incl, _dot(x, y, "nt"), 0.0) for x, y in zip(rt, kt)]
    inv = [eye + x for x in l_ab]
    pw = l_ab
    for _ in range(int(math.log2(t)) - 1):
        pw = [_dot(x, x) for x in pw]
        inv = [x + _dot(x, y) for x, y in zip(inv, pw)]
    rhs = [_dot(x, s, "nt") + _dot(m, y) for x, s, m, y in zip(at, s0, l_ak, v)]
    u = [_dot(x, y) for x, y in zip(inv, rhs)]
    y_s = [_dot(x, s, "nt") for x, s in zip(rt, s0)]
    y = [ys + _dot(m, uu) + _dot(n, vv) for ys, m, uu, n, vv in zip(y_s, r_b, u, r_k, v)]
    grow = [s + _dot(uu, x, "tn") + _dot(vv, z, "tn") for s, uu, x, vv, z in zip(s0, u, bt, v, kt)]
    s1 = [g * e[t - 1:t, :] for g, e in zip(grow, e_pos)]
    return y, s1


def _head_rows(h):
    return slice(h * RWKV_HEAD, (h + 1) * RWKV_HEAD)


def _per_head(ref):
    return [ref[:, _head_rows(h)] for h in range(RWKV_HEADS)]


def _scan(r, lw, k, v, a, b, *, name):
    lp = r.shape[0]
    nc = lp // CHUNK
    row = pl.BlockSpec((CHUNK, RWKV_DIM), lambda c: (c, 0))

    def body(r_ref, lw_ref, k_ref, v_ref, a_ref, b_ref, y_ref, s_ref, state):
        @pl.when(pl.program_id(0) == 0)
        def _():
            state[...] = jnp.zeros_like(state)

        s_ref[...] = state[...]
        s0 = [state[_head_rows(h), :] for h in range(RWKV_HEADS)]
        y, s1 = _scan_chunk(s0, *[_per_head(ref) for ref in (r_ref, lw_ref, k_ref, v_ref, a_ref, b_ref)])
        for h in range(RWKV_HEADS):
            y_ref[:, _head_rows(h)] = y[h]
            state[_head_rows(h), :] = s1[h]

    return pl.pallas_call(
        body, name=name, grid=(nc,), in_specs=[row] * 6,
        out_specs=[row, pl.BlockSpec((RWKV_DIM, RWKV_HEAD), lambda c: (c, 0))],
        out_shape=[jax.ShapeDtypeStruct((lp, RWKV_DIM), f32), jax.ShapeDtypeStruct((nc * RWKV_DIM, RWKV_HEAD), f32)],
        scratch_shapes=[pltpu.VMEM((RWKV_DIM, RWKV_HEAD), f32)],
        compiler_params=_params(("arbitrary",)),
    )(r, lw, k, v, a, b)


def _scan_bwd(r, lw, k, v, a, b, states, dy, *, name):
    lp = r.shape[0]
    nc = lp // CHUNK
    row = pl.BlockSpec((CHUNK, RWKV_DIM), lambda c: (nc - 1 - c, 0))

    def body(r_ref, lw_ref, k_ref, v_ref, a_ref, b_ref, s_ref, dy_ref,
             dr_ref, dlw_ref, dk_ref, dv_ref, da_ref, db_ref, dstate):
        @pl.when(pl.program_id(0) == 0)
        def _():
            dstate[...] = jnp.zeros_like(dstate)

        outs = (dr_ref, dlw_ref, dk_ref, dv_ref, da_ref, db_ref)
        s0 = [s_ref[_head_rows(h), :] for h in range(RWKV_HEADS)]
        _, vjp = jax.vjp(_scan_chunk, s0, *[_per_head(ref) for ref in (r_ref, lw_ref, k_ref, v_ref, a_ref, b_ref)])
        g = vjp((_per_head(dy_ref), [dstate[_head_rows(h), :] for h in range(RWKV_HEADS)]))
        for h in range(RWKV_HEADS):
            dstate[_head_rows(h), :] = g[0][h]
            for o_ref, gv in zip(outs, g[1:]):
                o_ref[:, _head_rows(h)] = gv[h]

    shape = jax.ShapeDtypeStruct((lp, RWKV_DIM), f32)
    return pl.pallas_call(
        body, name=name, grid=(nc,),
        in_specs=[row] * 6 + [pl.BlockSpec((RWKV_DIM, RWKV_HEAD), lambda c: (nc - 1 - c, 0)), row],
        out_specs=[row] * 6, out_shape=[shape] * 6,
        scratch_shapes=[pltpu.VMEM((RWKV_DIM, RWKV_HEAD), f32)],
        compiler_params=_params(("arbitrary",)),
    )(r, lw, k, v, a, b, states, dy)


def _loss_head(h2, target, g_final, *, name):
    lp = h2.shape[0]
    tm = BLOCK
    front_tiles = FRONT // tm

    def body(h_ref, t_ref, g_ref, loss_ref, dh_ref, dg_ref):
        i = pl.program_id(0)
        real = i >= front_tiles

        def tile_loss(hv, gv):
            err = _rms(hv, gv) - t_ref[...]
            return jnp.where(real, 0.5 * jnp.sum(jnp.mean(err * err, axis=-1, keepdims=True)), 0.0)

        loss, (dh, dg) = jax.value_and_grad(tile_loss, argnums=(0, 1))(h_ref[...], g_ref[...])

        @pl.when(i == 0)
        def _():
            loss_ref[...] = jnp.zeros_like(loss_ref)
            dg_ref[...] = jnp.zeros_like(dg_ref)

        loss_ref[...] += jnp.full(loss_ref.shape, loss, f32)
        dg_ref[...] += dg
        dh_ref[...] = dh

    return pl.pallas_call(
        body, name=name, grid=(lp // tm,),
        in_specs=[pl.BlockSpec((tm, D_MODEL), lambda i: (i, 0)),
                  pl.BlockSpec((tm, D_MODEL), lambda i: (jnp.maximum(i - front_tiles, 0), 0)),
                  _full(g_final.shape)],
        out_specs=[_full((8, 128)), pl.BlockSpec((tm, D_MODEL), lambda i: (i, 0)), _full(g_final.shape)],
        out_shape=[jax.ShapeDtypeStruct((8, 128), f32), jax.ShapeDtypeStruct((lp, D_MODEL), f32),
                   jax.ShapeDtypeStruct(g_final.shape, f32)],
        compiler_params=_params(("arbitrary",)),
    )(h2, target, g_final)


def _local_step(x, target, meta, p):
    seq = x.shape[0]
    lp = seq + FRONT
    h0 = jnp.concatenate([jnp.zeros((PAD, D_MODEL), f32), meta, x], axis=0)
    cos_t, sin_t, swap = _rope_tables(lp)
    hsum = _head_sum_matrix(RWKV_DIM, RWKV_HEAD)
    hmean = hsum / RWKV_HEAD
    w_qkv_t, w_rkv_t = p["w_in_t"][:ATTN_PROJ], p["w_in_t"][ATTN_PROJ:ATTN_PROJ + RKV_W]
    w_lora_t, w_gates_t = p["w_in_t"][ATTN_PROJ + RKV_W:ATTN_PROJ + RWKV_PROJ], p["w_in_t"][ATTN_PROJ + RWKV_PROJ:]
    b_qkv, b_rkv = p["b_in"][:, :ATTN_PROJ], p["b_in"][:, ATTN_PROJ:ATTN_PROJ + RKV_W]
    b_lora, b_gates = p["b_in"][:, ATTN_PROJ + RKV_W:ATTN_PROJ + RWKV_PROJ], p["b_in"][:, ATTN_PROJ + RWKV_PROJ:]
    prep_params = [p["w0"], p["w2"], p["a0"], p["a2"], p["g2"], p["k_k"], p["k_a"], hsum]
    post_params = [p["ln_w"], p["ln_b"], p["r_k"], hmean]

    (u,) = _rowwise(lambda hv, g: (_rms(hv, g),), [h0], [p["norm_mix_g"]], [(D_MODEL, bf16)], name="norm_mix")
    qkv = _mm(u, w_qkv_t, "nt", name="proj_qkv", bias=b_qkv, zero_rows_below=PAD)
    p_rkv = _mm(u, w_rkv_t, "nt", name="proj_rkv", bias=b_rkv, zero_rows_below=PAD)
    p_lora = _mm(u, w_lora_t, "nt", name="proj_lora", bias=b_lora, zero_rows_below=PAD)
    gates = _mm(u, w_gates_t, "nt", name="proj_gates", bias=b_gates, zero_rows_below=PAD)

    q, k, v = _rowwise(_attn_prep, [qkv, cos_t, sin_t], [swap], [(Q_W, bf16), (KV_W, bf16), (KV_W, bf16)],
                       name="attn_prep")
    y_attn = _attention(q, k, v, p["sinks"], name="attention")

    mix_rkv, mix_lora = p["mix"][:, :RKV_W], p["mix"][:, RKV_W:]
    pf_rkv = _token_shift(p_rkv, mix_rkv, name="shift_rkv")
    pf_lora = _token_shift(p_lora, mix_lora, name="shift_lora")
    wide = [(RWKV_DIM, f32)] * 7
    r_, lw_, k_, v_, a_, b_, g_ = _rowwise(_rwkv_prep, [pf_rkv, pf_lora], prep_params, wide, name="rwkv_prep")
    y_scan, states = _scan(r_, lw_, k_, v_, a_, b_, name="wkv_scan")
    (y_rwkv,) = _rowwise(_rwkv_post, [y_scan, r_, k_, v_, g_], post_params, [(RWKV_DIM, f32)], name="rwkv_post")

    br_a = _mm(y_attn, p["w_br_attn_t"], "nt", name="branch_attn")
    br_r = _mm(y_rwkv, p["w_br_rwkv_t"], "nt", name="branch_rwkv")
    (merged,) = _rowwise(_merge, [gates, br_a, br_r], [], [(D_MODEL, bf16)], name="merge")
    h1 = _mm(merged, p["w_o"], "nn", name="out_proj", add=h0)
    (f,) = _rowwise(lambda hv, g: (_rms(hv, g),), [h1], [p["norm_ffn_g"]], [(D_MODEL, bf16)], name="norm_ffn")
    gate = _mm(f, p["w_gate_t"], "nt", name="ffn_gate")
    up = _mm(f, p["w_up_t"], "nt", name="ffn_up")
    (act,) = _rowwise(_swiglu, [gate, up], [], [(D_FF, bf16)], name="swiglu")
    h2 = _mm(act, p["w_down"], "nn", name="ffn_down", add=h1)

    loss8, dh2, d_final_g = _loss_head(h2, target, p["norm_final_g"], name="loss_head")
    dact = _mm(dh2, p["w_down"], "nt", name="d_act")
    d_w_down = _mm_tn(act, dh2, name="dw_down")
    dgate, dup = _rowwise_bwd(_swiglu, [gate, up], [], [[dact]], name="swiglu_bwd",
                              diff_rows=[True, True], diff_params=[])
    d_w_gate_t = _mm_tn(dgate, f, name="dw_gate")
    d_w_up_t = _mm_tn(dup, f, name="dw_up")
    df = _mm(dgate, p["w_gate_t"], "nn", name="d_f_gate")
    df = _mm(dup, p["w_up_t"], "nn", name="d_f_up", add=df)
    dh1_n, d_ffn_g = _rowwise_bwd(lambda hv, g: (_rms(hv, g),), [h1], [p["norm_ffn_g"]], [[df]], name="norm_ffn_bwd",
                                  diff_rows=[True], diff_params=[True])
    (dh1,) = _rowwise(lambda x1, x2: (x1 + x2,), [dh2, dh1_n], [], [(D_MODEL, f32)], name="dh1_sum")
    dmerged = _mm(dh1, p["w_o"], "nt", name="d_merged")
    d_w_o = _mm_tn(merged, dh1, name="dw_o")
    dgates, dbr_a, dbr_r = _rowwise_bwd(_merge, [gates, br_a, br_r], [], [[dmerged]], name="merge_bwd",
                                        diff_rows=[True, True, True], diff_params=[])
    dy_attn = _mm(dbr_a, p["w_br_attn_t"], "nn", name="d_y_attn")
    dy_rwkv = _mm(dbr_r, p["w_br_rwkv_t"], "nn", name="d_y_rwkv")
    d_w_br_attn_t = _mm_tn(dbr_a, y_attn, name="dw_br_attn")
    d_w_br_rwkv_t = _mm_tn(dbr_r, y_rwkv, name="dw_br_rwkv")

    res = _rowwise_bwd(_rwkv_post, [y_scan, r_, k_, v_, g_], post_params, [[dy_rwkv]], name="rwkv_post_bwd",
                       diff_rows=[True] * 5, diff_params=[True, True, True, False])
    dy_scan, dr_p, dk_p, dv_p, dg_p, d_ln_w, d_ln_b, d_r_k = res
    dr_s, dlw_s, dk_s, dv_s, da_s, db_s = _scan_bwd(r_, lw_, k_, v_, a_, b_, states, dy_scan, name="wkv_scan_bwd")
    res = _rowwise_bwd(_rwkv_prep, [pf_rkv, pf_lora], prep_params,
                       [[dr_s, dr_p], [dlw_s], [dk_s, dk_p], [dv_s, dv_p], [da_s], [db_s], [dg_p]],
                       name="rwkv_prep_bwd", diff_rows=[True, True], diff_params=[True] * 7 + [False],
                       zero_rows_below=PAD)
    dpf_rkv, dpf_lora, d_w0, d_w2, d_a0, d_a2, d_g2, d_k_k, d_k_a = res
    dp_rkv, d_mix_rkv = _token_shift_bwd(p_rkv, mix_rkv, dpf_rkv, name="shift_rkv_bwd")
    dp_lora, d_mix_lora = _token_shift_bwd(p_lora, mix_lora, dpf_lora, name="shift_lora_bwd")

    dq, dkp, dkc, dkm, dvp, dvc, dvm, d_sinks = _attention_bwd(q, k, v, p["sinks"], dy_attn, name="attention_bwd")
    dk = _kv_combine(dkp, dkc, dkm, name="dk_sum")
    dv = _kv_combine(dvp, dvc, dvm, name="dv_sum")
    (dqkv,) = _rowwise_bwd(_attn_prep, [qkv, cos_t, sin_t], [swap], [[dq], [dk], [dv]], name="attn_prep_bwd",
                           diff_rows=[True, False, False], diff_params=[False])

    d_w_qkv_t, db_qkv = _mm_tn(dqkv, u, name="dw_qkv", colsum=True)
    d_w_rkv_t, db_rkv = _mm_tn(dp_rkv, u, name="dw_rkv", colsum=True)
    d_w_lora_t, db_lora = _mm_tn(dp_lora, u, name="dw_lora", colsum=True)
    d_w_gates_t, db_gates = _mm_tn(dgates, u, name="dw_gates", colsum=True)
    du = _mm(dqkv, w_qkv_t, "nn", name="d_u_qkv")
    du = _mm(dp_rkv, w_rkv_t, "nn", name="d_u_rkv", add=du)
    du = _mm(dp_lora, w_lora_t, "nn", name="d_u_lora", add=du)
    du = _mm(dgates, w_gates_t, "nn", name="d_u_gates", add=du)
    dh0_n, d_mix_g = _rowwise_bwd(lambda hv, g: (_rms(hv, g),), [h0], [p["norm_mix_g"]], [[du]], name="norm_mix_bwd",
                                  diff_rows=[True], diff_params=[True])
    (dh0,) = _rowwise(lambda x1, x2: (x1 + x2,), [dh1, dh0_n], [], [(D_MODEL, f32)], name="dh0_sum")

    grads = dict(
        w_in_t=jnp.concatenate([d_w_qkv_t, d_w_rkv_t, d_w_lora_t, d_w_gates_t], axis=0),
        b_in=jnp.concatenate([db_qkv, db_rkv, db_lora, db_gates], axis=1),
        mix=jnp.concatenate([d_mix_rkv, d_mix_lora], axis=1),
        norm_mix_g=d_mix_g, sinks=d_sinks, w0=d_w0, w2=d_w2, a0=d_a0, a2=d_a2, g2=d_g2, k_k=d_k_k, k_a=d_k_a,
        r_k=d_r_k, ln_w=d_ln_w, ln_b=d_ln_b, w_br_attn_t=d_w_br_attn_t, w_br_rwkv_t=d_w_br_rwkv_t, w_o=d_w_o,
        norm_ffn_g=d_ffn_g, w_gate_t=d_w_gate_t, w_up_t=d_w_up_t, w_down=d_w_down, norm_final_g=d_final_g,
        meta=dh0[PAD:FRONT],
    )
    return loss8[0, 0], dh0[FRONT:], grads


def _position():
    return lax.axis_index("x"), lax.axis_index("y"), lax.axis_index("c")


def _other_chips(x, y):
    return [(1 - x, y), (x, 1 - y), (1 - x, 1 - y)]


def _gather_chips(shard, *, name):
    r, w = shard.shape

    def body(x_ref, o_ref, send_sems, recv_sems, local_sem):
        x, y, c = _position()
        local = pltpu.make_async_copy(x_ref, o_ref.at[2 * x + y], local_sem)
        local.start()
        chips = _other_chips(x, y)

        def copy(j, px, py, slot):
            return pltpu.make_async_remote_copy(
                src_ref=x_ref, dst_ref=o_ref.at[slot], send_sem=send_sems.at[j], recv_sem=recv_sems.at[j],
                device_id=(px, py, c), device_id_type=MESH)

        sends = [copy(j, px, py, 2 * x + y) for j, (px, py) in enumerate(chips)]
        for cp in sends:
            cp.start()
        for j, (px, py) in enumerate(chips):
            copy(j, px, py, 2 * px + py).wait_recv()
        for cp in sends:
            cp.wait_send()
        local.wait()

    return pl.pallas_call(
        body, name=name,
        in_specs=[pl.BlockSpec(memory_space=pl.ANY)],
        out_specs=pl.BlockSpec(memory_space=pl.ANY),
        out_shape=jax.ShapeDtypeStruct((N_CHIPS, r, w), shard.dtype),
        scratch_shapes=[pltpu.SemaphoreType.DMA((3,)), pltpu.SemaphoreType.DMA((3,)), pltpu.SemaphoreType.DMA],
    )(shard)


def _scatter_chips(g, *, name):
    _, r, w = g.shape

    def body(g_ref, o_ref, send_sems, recv_sems):
        x, y, c = _position()
        chips = _other_chips(x, y)

        def copy(j, px, py):
            return pltpu.make_async_remote_copy(
                src_ref=g_ref.at[2 * px + py], dst_ref=o_ref.at[j], send_sem=send_sems.at[j],
                recv_sem=recv_sems.at[j], device_id=(px, py, c), device_id_type=MESH)

        sends = [copy(j, px, py) for j, (px, py) in enumerate(chips)]
        for cp in sends:
            cp.start()
        for cp in sends:
            cp.wait_recv()
        for cp in sends:
            cp.wait_send()

    return pl.pallas_call(
        body, name=name,
        in_specs=[pl.BlockSpec(memory_space=pl.ANY)],
        out_specs=pl.BlockSpec(memory_space=pl.ANY),
        out_shape=jax.ShapeDtypeStruct((3, r, w), g.dtype),
        scratch_shapes=[pltpu.SemaphoreType.DMA((3,)), pltpu.SemaphoreType.DMA((3,))],
    )(g)


def _sum_own_and_received(g, recv, *, name):
    _, r, w = g.shape
    tm = _tile(r)
    x, y, _ = _position()
    me = jnp.reshape(2 * x + y, (1,)).astype(jnp.int32)

    def body(me_ref, g_ref, r_ref, o_ref):
        o_ref[...] = g_ref[0] + r_ref[0] + r_ref[1] + r_ref[2]

    return pl.pallas_call(
        body, name=name,
        grid_spec=pltpu.PrefetchScalarGridSpec(
            num_scalar_prefetch=1, grid=(r // tm,),
            in_specs=[pl.BlockSpec((1, tm, w), lambda i, me_ref: (me_ref[0], i, 0)),
                      pl.BlockSpec((3, tm, w), lambda i, me_ref: (0, i, 0))],
            out_specs=pl.BlockSpec((tm, w), lambda i, me_ref: (i, 0))),
        out_shape=jax.ShapeDtypeStruct((r, w), f32),
        compiler_params=_params(("parallel",)),
    )(me, g, recv)


def _swap_cores(a, *, name):
    def body(a_ref, o_ref, send_sem, recv_sem):
        x, y, c = _position()
        cp = pltpu.make_async_remote_copy(src_ref=a_ref, dst_ref=o_ref, send_sem=send_sem, recv_sem=recv_sem,
                                          device_id=(x, y, 1 - c), device_id_type=MESH)
        cp.start()
        cp.wait_recv()
        cp.wait_send()

    return pl.pallas_call(
        body, name=name,
        in_specs=[pl.BlockSpec(memory_space=pl.ANY)],
        out_specs=pl.BlockSpec(memory_space=pl.ANY),
        out_shape=jax.ShapeDtypeStruct(a.shape, a.dtype),
        scratch_shapes=[pltpu.SemaphoreType.DMA, pltpu.SemaphoreType.DMA],
    )(a)


def _all_reduce_small(a, *, name):
    rows, w = a.shape

    def body(a_ref, o_ref, buf, send_sems, recv_sems):
        x, y, c = _position()
        me = 4 * x + 2 * y + c
        buf[0] = a_ref[...]
        sends = []
        for rel in range(1, N_DEV):
            peer = ((1 - x) if rel & 4 else x, (1 - y) if rel & 2 else y, (1 - c) if rel & 1 else c)
            cp = pltpu.make_async_remote_copy(
                src_ref=a_ref, dst_ref=buf.at[rel], send_sem=send_sems.at[rel - 1], recv_sem=recv_sems.at[rel - 1],
                device_id=peer, device_id_type=MESH)
            cp.start()
            sends.append(cp)
        for cp in sends:
            cp.wait_recv()
        for cp in sends:
            cp.wait_send()
        acc = buf[jnp.bitwise_xor(me, 0)]
        for d in range(1, N_DEV):
            acc = acc + buf[jnp.bitwise_xor(me, d)]
        o_ref[...] = acc

    return pl.pallas_call(
        body, name=name,
        in_specs=[pl.BlockSpec(memory_space=pltpu.VMEM)],
        out_specs=pl.BlockSpec(memory_space=pltpu.VMEM),
        out_shape=jax.ShapeDtypeStruct((rows, w), f32),
        scratch_shapes=[pltpu.VMEM((N_DEV, rows, w), f32), pltpu.SemaphoreType.DMA((N_DEV - 1,)),
                        pltpu.SemaphoreType.DMA((N_DEV - 1,))],
    )(a)


def _adamw(w, g, m, v, *, name):
    rows, cols = w.shape
    tm = _tile(rows, 256)

    def body(w_ref, g_ref, m_ref, v_ref, d_ref, nm_ref, nv_ref):
        gv = g_ref[...]
        nm = ADAM_B1 * m_ref[...] + (1.0 - ADAM_B1) * gv
        nv = ADAM_B2 * v_ref[...] + (1.0 - ADAM_B2) * (gv * gv)
        m_hat = nm / (1.0 - ADAM_B1 ** ADAM_STEP)
        v_hat = nv / (1.0 - ADAM_B2 ** ADAM_STEP)
        d_ref[...] = -ADAM_LR * (m_hat / (jnp.sqrt(v_hat) + ADAM_EPS) + ADAM_WD * w_ref[...])
        nm_ref[...] = nm
        nv_ref[...] = nv

    spec = pl.BlockSpec((tm, cols), lambda i: (i, 0))
    shape = jax.ShapeDtypeStruct((rows, cols), f32)
    return pl.pallas_call(
        body, name=name, grid=(rows // tm,), in_specs=[spec] * 4, out_specs=[spec] * 3, out_shape=[shape] * 3,
        compiler_params=_params(("parallel",)),
    )(w, g, m, v)


def _rows_of(a):
    return a.reshape(-1, PACK_W)


def _pad_rows(a, rows):
    return jnp.concatenate([a, jnp.zeros((rows - a.shape[0], a.shape[1]), a.dtype)], axis=0) if rows > a.shape[0] else a


_SHARD_SECTIONS = (("w_in_t", D_IN // N_CHIPS, D_MODEL), ("w_gate_t", D_FF // N_CHIPS, D_MODEL),
                   ("w_up_t", D_FF // N_CHIPS, D_MODEL), ("w_down", D_FF // N_CHIPS, D_MODEL),
                   ("w_o", D_MODEL // N_CHIPS, D_MODEL), ("w_br_attn_t", D_MODEL // N_CHIPS, Q_W),
                   ("w_br_rwkv_t", D_MODEL // N_CHIPS, RWKV_DIM), ("g2_t", RWKV_DIM // N_CHIPS, GATE_LORA),
                   ("w2_t", RWKV_DIM // N_CHIPS, DECAY_LORA), ("a2_t", RWKV_DIM // N_CHIPS, AAA_LORA))
_SMALL = (("norm_mix_g", D_MODEL), ("b_in", D_IN), ("sinks", Q_HEADS), ("mix", RWKV_PROJ), ("w0", RWKV_DIM),
          ("a0", RWKV_DIM), ("k_k", RWKV_DIM), ("k_a", RWKV_DIM), ("r_k", RWKV_DIM), ("ln_w", RWKV_DIM),
          ("ln_b", RWKV_DIM), ("norm_ffn_g", D_MODEL), ("norm_final_g", D_MODEL))


def _section_rows(rows, cols):
    return rows * cols // PACK_W


def _pack_sections(d, dtype):
    return jnp.concatenate([_rows_of(d[n].astype(dtype)) for n, _, _ in _SHARD_SECTIONS], axis=0)


def _unpack_sections(pack):
    out, off = {}, 0
    lead = pack.shape[:-2]
    for n, rows, cols in _SHARD_SECTIONS:
        k = _section_rows(rows, cols)
        out[n] = pack[..., off:off + k, :].reshape(*lead, rows, cols)
        off += k
    return out


def _pack_small(d):
    flat = jnp.concatenate([d[n].reshape(-1).astype(f32) for n, _ in _SMALL])
    return flat


def _unpack_small(flat):
    out, off = {}, 0
    for n, size in _SMALL:
        out[n] = flat[off:off + size]
        off += size
    return out


_SMALL_TOTAL = sum(s for _, s in _SMALL)


def kernel(x, meta_tokens, norm_mix_g, w_in, b_in, attn_sinks, rwkv_mix, rwkv_w0, rwkv_w2, rwkv_a0, rwkv_a2, rwkv_g2, rwkv_k_k, rwkv_k_a, rwkv_r_k, rwkv_ln_w, rwkv_ln_b, w_br_attn, w_br_rwkv, w_o, norm_ffn_g, w_ffn_gate, w_ffn_up, w_ffn_down, norm_final_g, loss_target, m_meta_tokens, m_norm_mix_g, m_w_in, m_b_in, m_attn_sinks, m_rwkv_mix, m_rwkv_w0, m_rwkv_w2, m_rwkv_a0, m_rwkv_a2, m_rwkv_g2, m_rwkv_k_k, m_rwkv_k_a, m_rwkv_r_k, m_rwkv_ln_w, m_rwkv_ln_b, m_w_br_attn, m_w_br_rwkv, m_w_o, m_norm_ffn_g, m_w_ffn_gate, m_w_ffn_up, m_w_ffn_down, m_norm_final_g, v_meta_tokens, v_norm_mix_g, v_w_in, v_b_in, v_attn_sinks, v_rwkv_mix, v_rwkv_w0, v_rwkv_w2, v_rwkv_a0, v_rwkv_a2, v_rwkv_g2, v_rwkv_k_k, v_rwkv_k_a, v_rwkv_r_k, v_rwkv_ln_w, v_rwkv_ln_b, v_w_br_attn, v_w_br_rwkv, v_w_o, v_norm_ffn_g, v_w_ffn_gate, v_w_ffn_up, v_w_ffn_down, v_norm_final_g):
    names = ("meta_tokens", "norm_mix_g", "w_in", "b_in", "attn_sinks", "rwkv_mix", "rwkv_w0", "rwkv_w2", "rwkv_a0",
             "rwkv_a2", "rwkv_g2", "rwkv_k_k", "rwkv_k_a", "rwkv_r_k", "rwkv_ln_w", "rwkv_ln_b", "w_br_attn",
             "w_br_rwkv", "w_o", "norm_ffn_g", "w_ffn_gate", "w_ffn_up", "w_ffn_down", "norm_final_g")
    w_all = dict(zip(names, (meta_tokens, norm_mix_g, w_in, b_in, attn_sinks, rwkv_mix, rwkv_w0, rwkv_w2, rwkv_a0,
                             rwkv_a2, rwkv_g2, rwkv_k_k, rwkv_k_a, rwkv_r_k, rwkv_ln_w, rwkv_ln_b, w_br_attn,
                             w_br_rwkv, w_o, norm_ffn_g, w_ffn_gate, w_ffn_up, w_ffn_down, norm_final_g)))
    m_all = dict(zip(names, (m_meta_tokens, m_norm_mix_g, m_w_in, m_b_in, m_attn_sinks, m_rwkv_mix, m_rwkv_w0,
                             m_rwkv_w2, m_rwkv_a0, m_rwkv_a2, m_rwkv_g2, m_rwkv_k_k, m_rwkv_k_a, m_rwkv_r_k,
                             m_rwkv_ln_w, m_rwkv_ln_b, m_w_br_attn, m_w_br_rwkv, m_w_o, m_norm_ffn_g, m_w_ffn_gate,
                             m_w_ffn_up, m_w_ffn_down, m_norm_final_g)))
    v_all = dict(zip(names, (v_meta_tokens, v_norm_mix_g, v_w_in, v_b_in, v_attn_sinks, v_rwkv_mix, v_rwkv_w0,
                             v_rwkv_w2, v_rwkv_a0, v_rwkv_a2, v_rwkv_g2, v_rwkv_k_k, v_rwkv_k_a, v_rwkv_r_k,
                             v_rwkv_ln_w, v_rwkv_ln_b, v_w_br_attn, v_w_br_rwkv, v_w_o, v_norm_ffn_g, v_w_ffn_gate,
                             v_w_ffn_up, v_w_ffn_down, v_norm_final_g)))
    cx, cy, _ = _position()
    chip = 2 * cx + cy

    t_of = dict(w_in_t="w_in", w_gate_t="w_ffn_gate", w_up_t="w_ffn_up", w_br_attn_t="w_br_attn",
                w_br_rwkv_t="w_br_rwkv", g2_t="rwkv_g2", w2_t="rwkv_w2", a2_t="rwkv_a2")
    plain_of = dict(w_down="w_ffn_down", w_o="w_o")

    def shard_views(src):
        d = {k: src[n][0].T for k, n in t_of.items()}
        d.update({k: src[n][0] for k, n in plain_of.items()})
        return d

    meta_bits = lax.bitcast_convert_type(meta_tokens.T, bf16).reshape(-1, PACK_W)
    pack = jnp.concatenate([_pack_sections(shard_views(w_all), bf16), meta_bits], axis=0)
    pack_rows = -(-pack.shape[0] // BLOCK) * BLOCK
    gathered = _gather_chips(_pad_rows(pack, pack_rows), name="gather_weights")
    full = _unpack_sections(gathered)
    meta_off = sum(_section_rows(r, c) for _, r, c in _SHARD_SECTIONS)
    meta_rows = meta_tokens.size * 2 // PACK_W
    meta_cols = meta_tokens.shape[1]
    meta_full = lax.bitcast_convert_type(
        gathered[:, meta_off:meta_off + meta_rows, :].reshape(N_CHIPS, meta_cols, N_META, 2), f32)
    meta_full = meta_full.reshape(N_CHIPS * meta_cols, N_META).T

    def cat(k):
        return full[k].reshape(-1, full[k].shape[-1])

    p = dict(
        w_in_t=cat("w_in_t"), w_gate_t=cat("w_gate_t"), w_up_t=cat("w_up_t"), w_down=cat("w_down"), w_o=cat("w_o"),
        w_br_attn_t=cat("w_br_attn_t"), w_br_rwkv_t=cat("w_br_rwkv_t"),
        g2=cat("g2_t").T.astype(f32), w2=cat("w2_t").T.astype(f32), a2=cat("a2_t").T.astype(f32),
        b_in=b_in, sinks=attn_sinks, mix=rwkv_mix, w0=rwkv_w0, a0=rwkv_a0, k_k=rwkv_k_k, k_a=rwkv_k_a,
        r_k=rwkv_r_k.reshape(1, RWKV_DIM), ln_w=rwkv_ln_w, ln_b=rwkv_ln_b, norm_mix_g=norm_mix_g,
        norm_ffn_g=norm_ffn_g, norm_final_g=norm_final_g.reshape(1, D_MODEL),
    )

    loss, dx, g = _local_step(x[0], loss_target[0], meta_full, p)

    g_views = dict(g)
    g_views.update(g2_t=g["g2"].T, w2_t=g["w2"].T, a2_t=g["a2"].T)
    g_pack = jnp.concatenate(
        [g_views[n].reshape(N_CHIPS, _section_rows(r, c), PACK_W) for n, r, c in _SHARD_SECTIONS], axis=1)
    g_rows = -(-g_pack.shape[1] // BLOCK) * BLOCK
    g_pack = jnp.concatenate([g_pack, jnp.zeros((N_CHIPS, g_rows - g_pack.shape[1], PACK_W), f32)], axis=1)
    received = _scatter_chips(g_pack, name="scatter_grads")
    part = _sum_own_and_received(g_pack, received, name="sum_chips")
    other = _swap_cores(part, name="swap_cores")
    (g_shard,) = _rowwise(lambda a, b: (a + b,), [part, other], [], [(PACK_W, f32)], name="sum_cores")
    gs = _unpack_sections(g_shard)

    small = jnp.concatenate([_pack_small(g), loss.reshape(1)])
    small_rows = -(-small.shape[0] // PACK_W)
    small = jnp.concatenate([small, jnp.zeros((small_rows * PACK_W - small.shape[0],), f32)]).reshape(small_rows, PACK_W)
    small_rows8 = -(-(small_rows + N_META) // 8) * 8
    reduced = _all_reduce_small(_pad_rows(jnp.concatenate([g["meta"], small], axis=0), small_rows8),
                                name="reduce_small")
    g_meta = lax.dynamic_slice_in_dim(reduced[:N_META], chip * meta_cols, meta_cols, axis=1)
    flat = reduced[N_META:N_META + small_rows].reshape(-1)
    g_small = _unpack_small(flat)
    loss_total = flat[_SMALL_TOTAL]

    small_of = dict(norm_mix_g="norm_mix_g", b_in="b_in", attn_sinks="sinks", rwkv_mix="mix", rwkv_w0="w0",
                    rwkv_a0="a0", rwkv_k_k="k_k", rwkv_k_a="k_a", rwkv_r_k="r_k", rwkv_ln_w="ln_w",
                    rwkv_ln_b="ln_b", norm_ffn_g="norm_ffn_g", norm_final_g="norm_final_g")
    grads = {"meta_tokens": g_meta}
    for n, k in small_of.items():
        grads[n] = g_small[k].reshape(w_all[n].shape)
    for k, n in t_of.items():
        grads[n] = gs[k].T.reshape(w_all[n].shape)
    for k, n in plain_of.items():
        grads[n] = gs[k].reshape(w_all[n].shape)

    big = ("w_in", "w_ffn_gate", "w_ffn_up", "w_ffn_down", "w_o", "w_br_attn", "w_br_rwkv", "rwkv_g2", "rwkv_w2",
           "rwkv_a2")
    delta, new_m, new_v = {}, {}, {}
    for n in big:
        shape2 = w_all[n].shape[1:]
        d_, m_, v_ = _adamw(w_all[n].reshape(shape2), grads[n].reshape(shape2), m_all[n].reshape(shape2),
                            v_all[n].reshape(shape2), name="adamw_" + n)
        delta[n], new_m[n], new_v[n] = (t.reshape(w_all[n].shape) for t in (d_, m_, v_))
    rest = [n for n in names if n not in big]

    def pack_rest(src):
        flat_ = jnp.concatenate([src[n].reshape(-1) for n in rest])
        rows_ = -(-flat_.shape[0] // (8 * PACK_W)) * 8
        return jnp.concatenate([flat_, jnp.ones((rows_ * PACK_W - flat_.shape[0],), f32)]).reshape(rows_, PACK_W)

    d_, m_, v_ = _adamw(pack_rest(w_all), pack_rest(grads), pack_rest(m_all), pack_rest(v_all), name="adamw_small")
    off = 0
    for n in rest:
        size = w_all[n].size
        for dst, src in ((delta, d_), (new_m, m_), (new_v, v_)):
            dst[n] = src.reshape(-1)[off:off + size].reshape(w_all[n].shape)
        off += size

    return (loss_total, dx.reshape(x.shape), *[grads[n] for n in names], *[delta[n] for n in names],
            *[new_m[n] for n in names], *[new_v[n] for n in names])
```

```python
import functools
import math

import jax
import jax.numpy as jnp
from jax import lax
from jax.experimental import pallas as pl
from jax.experimental.pallas import tpu as pltpu

f32 = jnp.float32
bf16 = jnp.bfloat16

D_MODEL = 1024
N_META = 16
HEAD_DIM = 64
Q_HEADS = 8
KV_HEADS = 2
GROUP = Q_HEADS // KV_HEADS
WINDOW = 128
BLOCK = 128
ROPE_THETA = 500000.0
ROPE_DIM = HEAD_DIM // 4
RWKV_HEADS = 8
RWKV_HEAD = 64
RWKV_DIM = RWKV_HEADS * RWKV_HEAD
DECAY_LORA = 64
AAA_LORA = 64
GATE_LORA = 160
LORA_W = DECAY_LORA + AAA_LORA + GATE_LORA
RWKV_LN_EPS = 64e-5
D_FF = 2816
Q_W = Q_HEADS * HEAD_DIM
KV_W = KV_HEADS * HEAD_DIM
ATTN_PROJ = Q_W + 2 * KV_W
RKV_W = 3 * RWKV_DIM
RWKV_PROJ = RKV_W + LORA_W
D_IN = ATTN_PROJ + RWKV_PROJ + 2 * D_MODEL
RMS_EPS = 1e-6
NEG_INF = -1e30
PAD = BLOCK - N_META
FRONT = PAD + N_META

ADAM_LR = 0.001
ADAM_B1 = 0.9
ADAM_B2 = 0.999
ADAM_EPS = 1e-08
ADAM_WD = 0.01
ADAM_STEP = 10

N_CHIPS = 4
N_DEV = 8
CHUNK = 64
VMEM_LIMIT = 56 * 1024 * 1024
PACK_W = 1024
MESH = pl.DeviceIdType.MESH
HIGHEST = lax.Precision.HIGHEST


def _tile(m, pref=384):
    for step in (16, 8):
        for t in range(min(m, pref) // step * step, 0, -step):
            if m % t == 0:
                return t
    return m


def _params(sem=None):
    return pltpu.CompilerParams(dimension_semantics=sem, vmem_limit_bytes=VMEM_LIMIT)


def _full(shape):
    nd = len(shape)
    return pl.BlockSpec(shape, lambda *_: (0,) * nd)


def _dot(a, b, dims="nn", exact=False):
    dn = {"nn": (((1,), (0,)), ((), ())), "nt": (((1,), (1,)), ((), ())), "tn": (((0,), (0,)), ((), ()))}[dims]
    if exact:
        return lax.dot_general(a.astype(f32), b.astype(f32), dn, precision=HIGHEST, preferred_element_type=f32)
    return lax.dot_general(a.astype(bf16), b.astype(bf16), dn, preferred_element_type=f32)


def _dot3(a, b):
    a_hi, b_hi = a.astype(bf16), b.astype(bf16)
    a_lo = (a - a_hi.astype(f32)).astype(bf16)
    b_lo = (b - b_hi.astype(f32)).astype(bf16)
    return _dot(a_hi, b_hi) + (_dot(a_hi, b_lo) + _dot(a_lo, b_hi))


def _mm(a, b, mode, *, name, out_dtype=f32, bias=None, add=None, zero_rows_below=0):
    m, _ = a.shape
    n = b.shape[1] if mode == "nn" else b.shape[0]
    tm = _tile(m)
    has_bias, has_add = bias is not None, add is not None

    def body(*refs):
        a_ref, b_ref = refs[0], refs[1]
        o_ref = refs[-1]
        acc = _dot(a_ref[...], b_ref[...], mode)
        k = 2
        if has_bias:
            acc = acc + refs[k][...]
            k += 1
        if zero_rows_below:
            rows = pl.program_id(0) * tm + lax.broadcasted_iota(jnp.int32, acc.shape, 0)
            acc = jnp.where(rows >= zero_rows_below, acc, 0.0)
        if has_add:
            acc = acc + refs[k][...].astype(f32)
        o_ref[...] = acc.astype(out_dtype)

    ins = [a, b]
    in_specs = [pl.BlockSpec((tm, a.shape[1]), lambda i: (i, 0)), _full(b.shape)]
    if has_bias:
        ins.append(bias)
        in_specs.append(_full(bias.shape))
    if has_add:
        ins.append(add)
        in_specs.append(pl.BlockSpec((tm, n), lambda i: (i, 0)))
    return pl.pallas_call(
        body, name=name, grid=(m // tm,), in_specs=in_specs,
        out_specs=pl.BlockSpec((tm, n), lambda i: (i, 0)),
        out_shape=jax.ShapeDtypeStruct((m, n), out_dtype),
        compiler_params=_params(("parallel",)),
    )(*ins)


def _mm_tn(a, b, *, name, colsum=False):
    r, m = a.shape
    n = b.shape[1]
    tr = _tile(r)
    tmo = m
    for cand in (1408, 1024, 768, 512):
        if m > 1024 and m % cand == 0:
            tmo = cand
            break

    def body(a_ref, b_ref, o_ref, *cs):
        i = pl.program_id(1)

        @pl.when(i == 0)
        def _():
            o_ref[...] = jnp.zeros_like(o_ref)
            if colsum:
                cs[0][...] = jnp.zeros_like(cs[0])

        o_ref[...] += _dot(a_ref[...], b_ref[...], "tn")
        if colsum:
            cs[0][...] += jnp.sum(a_ref[...].astype(f32), axis=0, keepdims=True)

    out_shape = [jax.ShapeDtypeStruct((m, n), f32)]
    out_specs = [pl.BlockSpec((tmo, n), lambda j, i: (j, 0))]
    if colsum:
        out_shape.append(jax.ShapeDtypeStruct((1, m), f32))
        out_specs.append(pl.BlockSpec((1, tmo), lambda j, i: (0, j)))
    res = pl.pallas_call(
        body, name=name, grid=(m // tmo, r // tr),
        in_specs=[pl.BlockSpec((tr, tmo), lambda j, i: (i, j)), pl.BlockSpec((tr, n), lambda j, i: (i, 0))],
        out_specs=out_specs, out_shape=out_shape,
        compiler_params=_params(("parallel", "arbitrary")),
    )(a, b)
    return res if colsum else res[0]


def _rowwise(fn, rows, params, outs, *, name, tm=None, with_row0=False):
    m = rows[0].shape[0]
    tm = tm or _tile(m)
    nr, npar = len(rows), len(params)

    def body(*refs):
        vals = [r[...] for r in refs[:nr + npar]]
        kw = dict(row0=pl.program_id(0) * tm) if with_row0 else {}
        res = fn(*vals, **kw)
        for o_ref, v in zip(refs[nr + npar:], res):
            o_ref[...] = v.astype(o_ref.dtype)

    return pl.pallas_call(
        body, name=name, grid=(m // tm,),
        in_specs=[pl.BlockSpec((tm, r.shape[1]), lambda i: (i, 0)) for r in rows] + [_full(p.shape) for p in params],
        out_specs=[pl.BlockSpec((tm, w), lambda i: (i, 0)) for w, _ in outs],
        out_shape=[jax.ShapeDtypeStruct((m, w), dt) for w, dt in outs],
        compiler_params=_params(("parallel",)),
    )(*rows, *params)


def _rowwise_bwd(fn, rows, params, cts, *, name, diff_rows, diff_params, tm=None, with_row0=False, zero_rows_below=0):
    m = rows[0].shape[0]
    tm = tm or _tile(m)
    nr, npar = len(rows), len(params)
    d_idx = [i for i in range(nr) if diff_rows[i]]
    p_idx = [i for i in range(npar) if diff_params[i]]
    flat_cts = [c for group in cts for c in group]
    n_ct = len(flat_cts)

    def body(*refs):
        vals = [r[...] for r in refs[:nr + npar]]
        ct_refs = refs[nr + npar:nr + npar + n_ct]
        out_refs = refs[nr + npar + n_ct:]
        kw = dict(row0=pl.program_id(0) * tm) if with_row0 else {}
        ct_vals, k = [], 0
        for group in cts:
            acc = ct_refs[k][...].astype(f32)
            for extra in range(1, len(group)):
                acc = acc + ct_refs[k + extra][...].astype(f32)
            k += len(group)
            if zero_rows_below:
                rr = pl.program_id(0) * tm + lax.broadcasted_iota(jnp.int32, acc.shape, 0)
                acc = jnp.where(rr >= zero_rows_below, acc, 0.0)
            ct_vals.append(acc)

        def g(*dargs):
            full = list(vals)
            for pos, i in enumerate(d_idx):
                full[i] = dargs[pos]
            for pos, i in enumerate(p_idx):
                full[nr + i] = dargs[len(d_idx) + pos]
            return tuple(fn(*full, **kw))

        _, vjp = jax.vjp(g, *[vals[i] for i in d_idx], *[vals[nr + i] for i in p_idx])
        grads = vjp(tuple(ct_vals))
        for pos in range(len(d_idx)):
            out_refs[pos][...] = grads[pos].astype(out_refs[pos].dtype)
        first = pl.program_id(0) == 0
        for pos in range(len(p_idx)):
            o_ref = out_refs[len(d_idx) + pos]

            @pl.when(first)
            def _(o_ref=o_ref):
                o_ref[...] = jnp.zeros_like(o_ref)

            o_ref[...] += grads[len(d_idx) + pos]

    return pl.pallas_call(
        body, name=name, grid=(m // tm,),
        in_specs=[pl.BlockSpec((tm, r.shape[1]), lambda i: (i, 0)) for r in rows] + [_full(p.shape) for p in params]
        + [pl.BlockSpec((tm, c.shape[1]), lambda i: (i, 0)) for c in flat_cts],
        out_specs=[pl.BlockSpec((tm, rows[i].shape[1]), lambda i_: (i_, 0)) for i in d_idx]
        + [_full(params[i].shape) for i in p_idx],
        out_shape=[jax.ShapeDtypeStruct(rows[i].shape, f32) for i in d_idx]
        + [jax.ShapeDtypeStruct(params[i].shape, f32) for i in p_idx],
        compiler_params=_params(("arbitrary",)),
    )(*rows, *params, *flat_cts)


def _rms(x, g):
    return x * lax.rsqrt(jnp.mean(x * x, axis=-1, keepdims=True) + RMS_EPS) * g


def _head_sum_matrix(width, head):
    idx = jnp.arange(width) // head
    return (idx[:, None] == idx[None, :]).astype(f32)


def _rope_tables(lp):
    half = ROPE_DIM // 2
    pos = (jnp.arange(lp) - PAD).astype(f32)
    inv_freq = jnp.power(jnp.float32(ROPE_THETA), -jnp.arange(half, dtype=f32) * (2.0 / ROPE_DIM))
    ang = pos[:, None] * inv_freq[None, :]
    cos, sin = jnp.cos(ang), jnp.sin(ang)
    ones = jnp.ones((lp, HEAD_DIM - ROPE_DIM), f32)
    zeros = jnp.zeros((lp, HEAD_DIM - ROPE_DIM), f32)
    cos_t = jnp.concatenate([cos, cos, ones], axis=1)
    sin_t = jnp.concatenate([-sin, sin, zeros], axis=1)
    i = jnp.arange(HEAD_DIM)
    src = jnp.where(i < half, i + half, jnp.where(i < ROPE_DIM, i - half, i))
    swap = ((i[:, None] == src[None, :]) & (i[None, :] < ROPE_DIM)).astype(f32)
    return cos_t, sin_t, swap


def _attn_prep(qkv, cos_t, sin_t, swap):
    outs = []
    for h in range(Q_HEADS + KV_HEADS):
        t = qkv[:, h * HEAD_DIM:(h + 1) * HEAD_DIM]
        outs.append(t * cos_t + _dot(t, swap, exact=True) * sin_t)
    q = jnp.concatenate(outs[:Q_HEADS], axis=1)
    k = jnp.concatenate(outs[Q_HEADS:], axis=1)
    return q, k, qkv[:, Q_W + KV_W:]


def _softplus(z):
    return jnp.maximum(z, 0.0) + jnp.log1p(jnp.exp(-jnp.abs(z)))


def _rwkv_prep(rkv, lora, w0, w2, a0, a2, g2, k_k, k_a, hsum):
    r = rkv[:, :RWKV_DIM]
    k = rkv[:, RWKV_DIM:2 * RWKV_DIM]
    v = rkv[:, 2 * RWKV_DIM:]
    dw = lora[:, :DECAY_LORA]
    da = lora[:, DECAY_LORA:DECAY_LORA + AAA_LORA]
    dg = lora[:, DECAY_LORA + AAA_LORA:]
    w = -_softplus(-(w0 + _dot(jnp.tanh(dw), w2))) - 0.5
    a = jax.nn.sigmoid(a0 + _dot(da, a2))
    g = _dot(jax.nn.sigmoid(dg), g2)
    kk = k * k_k
    kk = kk * lax.rsqrt(jnp.maximum(_dot(kk * kk, hsum, exact=True), 1e-24))
    k = k * (1.0 + (a - 1.0) * k_a)
    log_decay = -jnp.exp(w)
    return r, log_decay, k, v, -kk, kk * a, g


def _rwkv_post(y, r, k, v, g, ln_w, ln_b, r_k, hmean):
    hsum = hmean * RWKV_HEAD
    mean = _dot(y, hmean, exact=True)
    yc = y - mean
    var = _dot(yc * yc, hmean, exact=True)
    yn = yc * lax.rsqrt(var + RWKV_LN_EPS) * ln_w + ln_b
    bonus = _dot(r * k * r_k, hsum, exact=True) * v
    return ((yn + bonus) * g,)


def _merge(gates, br_a, br_r):
    sg = jax.nn.sigmoid(gates)
    return (sg[:, :D_MODEL] * br_a + sg[:, D_MODEL:] * br_r,)


def _swiglu(gate, up):
    return (jax.nn.silu(gate) * up,)


def _token_shift(p, mix, *, name):
    m, c = p.shape
    tm = _tile(m)
    sub = tm // 8

    def body(p_ref, prev_ref, mix_ref, o_ref):
        x = p_ref[...]
        rows = lax.broadcasted_iota(jnp.int32, x.shape, 0)
        last = jnp.where(pl.program_id(0) == 0, 0.0, prev_ref[7:8, :])
        xp = jnp.where(rows == 0, last, pltpu.roll(x, 1, axis=0))
        o_ref[...] = x + (xp - x) * mix_ref[...]

    return pl.pallas_call(
        body, name=name, grid=(m // tm,),
        in_specs=[pl.BlockSpec((tm, c), lambda i: (i, 0)),
                  pl.BlockSpec((8, c), lambda i: (jnp.maximum(i * sub - 1, 0), 0)),
                  _full(mix.shape)],
        out_specs=pl.BlockSpec((tm, c), lambda i: (i, 0)),
        out_shape=jax.ShapeDtypeStruct((m, c), f32),
        compiler_params=_params(("parallel",)),
    )(p, p, mix)


def _token_shift_bwd(p, mix, dpf, *, name):
    m, c = p.shape
    tm = _tile(m)
    sub = tm // 8
    n_tiles = m // tm

    def body(p_ref, prev_ref, mix_ref, d_ref, nxt_ref, dp_ref, dmix_ref):
        i = pl.program_id(0)
        x = p_ref[...]
        d = d_ref[...]
        mixv = mix_ref[...]
        rows = lax.broadcasted_iota(jnp.int32, x.shape, 0)
        last = jnp.where(i == 0, 0.0, prev_ref[7:8, :])
        xp = jnp.where(rows == 0, last, pltpu.roll(x, 1, axis=0))
        dm = d * mixv
        first_next = jnp.where(i == n_tiles - 1, 0.0, nxt_ref[0:1, :] * mixv)
        dm_next = jnp.where(rows == tm - 1, first_next, pltpu.roll(dm, tm - 1, axis=0))
        dp = d - dm + dm_next
        dp_ref[...] = jnp.where(i * tm + rows >= PAD, dp, 0.0)

        @pl.when(i == 0)
        def _():
            dmix_ref[...] = jnp.zeros_like(dmix_ref)

        dmix_ref[...] += jnp.sum(d * (xp - x), axis=0, keepdims=True)

    return pl.pallas_call(
        body, name=name, grid=(n_tiles,),
        in_specs=[pl.BlockSpec((tm, c), lambda i: (i, 0)),
                  pl.BlockSpec((8, c), lambda i: (jnp.maximum(i * sub - 1, 0), 0)),
                  _full(mix.shape),
                  pl.BlockSpec((tm, c), lambda i: (i, 0)),
                  pl.BlockSpec((8, c), lambda i: (jnp.minimum((i + 1) * sub, m // 8 - 1), 0))],
        out_specs=[pl.BlockSpec((tm, c), lambda i: (i, 0)), _full(mix.shape)],
        out_shape=[jax.ShapeDtypeStruct((m, c), f32), jax.ShapeDtypeStruct(mix.shape, f32)],
        compiler_params=_params(("arbitrary",)),
    )(p, p, mix, dpf, dpf)


def _attn_heads(q, kp, kc, km, vp, vc, vm, sink, blk):
    scale = HEAD_DIM ** -0.5
    heads = range(len(q))
    qi = lax.broadcasted_iota(jnp.int32, (BLOCK, BLOCK), 0)
    ki = lax.broadcasted_iota(jnp.int32, (BLOCK, BLOCK), 1)
    qpos = blk * BLOCK + qi - PAD
    kpos_c = blk * BLOCK + ki - PAD
    kpos_p = kpos_c - BLOCK
    kpos_m = ki - PAD

    def band(kpos):
        return (kpos >= N_META) & (kpos <= qpos) & (qpos - kpos < WINDOW)

    ok_p, ok_c, ok_m = band(kpos_p), band(kpos_c), (kpos_m >= 0) & (kpos_m <= qpos)
    s_p = [jnp.where(ok_p, _dot(q[i], kp[i // GROUP], "nt") * scale, NEG_INF) for i in heads]
    s_c = [jnp.where(ok_c, _dot(q[i], kc[i // GROUP], "nt") * scale, NEG_INF) for i in heads]
    s_m = [jnp.where(ok_m, _dot(q[i], km[i // GROUP], "nt") * scale, NEG_INF) for i in heads]
    mx = [jnp.maximum(jnp.maximum(jnp.max(s_p[i], -1, keepdims=True), jnp.max(s_c[i], -1, keepdims=True)),
                      jnp.maximum(jnp.max(s_m[i], -1, keepdims=True), sink[i])) for i in heads]
    e_p = [jnp.exp(s_p[i] - mx[i]) for i in heads]
    e_c = [jnp.exp(s_c[i] - mx[i]) for i in heads]
    e_m = [jnp.exp(s_m[i] - mx[i]) for i in heads]
    inv = [1.0 / (jnp.sum(e_p[i], -1, keepdims=True) + jnp.sum(e_c[i], -1, keepdims=True)
                  + jnp.sum(e_m[i], -1, keepdims=True) + jnp.exp(sink[i] - mx[i])) for i in heads]
    return [_dot(e_p[i] * inv[i], vp[i // GROUP]) + _dot(e_c[i] * inv[i], vc[i // GROUP])
            + _dot(e_m[i] * inv[i], vm[i // GROUP]) for i in heads]


def _attn_specs():
    prev = lambda i: (jnp.maximum(i - 1, 0), 0)
    cur = lambda i: (i, 0)
    meta = lambda i: (0, 0)
    return [pl.BlockSpec((BLOCK, Q_W), cur),
            pl.BlockSpec((BLOCK, KV_W), prev), pl.BlockSpec((BLOCK, KV_W), cur), pl.BlockSpec((BLOCK, KV_W), meta),
            pl.BlockSpec((BLOCK, KV_W), prev), pl.BlockSpec((BLOCK, KV_W), cur), pl.BlockSpec((BLOCK, KV_W), meta),
            _full((1, Q_HEADS))]


def _head_cols(i):
    return slice(i * HEAD_DIM, (i + 1) * HEAD_DIM)


def _attn_args(refs):
    q_ref, s_ref = refs[0], refs[7]
    args = [[q_ref[:, _head_cols(i)].astype(f32) for i in range(Q_HEADS)]]
    args += [[ref[:, _head_cols(h)].astype(f32) for h in range(KV_HEADS)] for ref in refs[1:7]]
    return args + [[s_ref[:, i:i + 1] for i in range(Q_HEADS)]]


def _attention(q, k, v, sinks, *, name):
    lp = q.shape[0]

    def body(*refs):
        o_ref = refs[-1]
        out = _attn_heads(*_attn_args(refs[:-1]), pl.program_id(0))
        for i in range(Q_HEADS):
            o_ref[:, _head_cols(i)] = out[i].astype(o_ref.dtype)

    return pl.pallas_call(
        body, name=name, grid=(lp // BLOCK,), in_specs=_attn_specs(),
        out_specs=pl.BlockSpec((BLOCK, Q_W), lambda i: (i, 0)),
        out_shape=jax.ShapeDtypeStruct((lp, Q_W), f32),
        compiler_params=_params(("parallel",)),
    )(q, k, k, k, v, v, v, sinks)


def _attention_bwd(q, k, v, sinks, do, *, name):
    lp = q.shape[0]
    nb = lp // BLOCK

    def body(*refs):
        ins, do_ref = refs[:8], refs[8]
        dq_ref, dkp_ref, dkc_ref, dkm_ref, dvp_ref, dvc_ref, dvm_ref, ds_ref = refs[9:]
        blk = pl.program_id(0)

        @pl.when(blk == 0)
        def _():
            dkm_ref[...] = jnp.zeros_like(dkm_ref)
            dvm_ref[...] = jnp.zeros_like(dvm_ref)
            ds_ref[...] = jnp.zeros_like(ds_ref)

        _, vjp = jax.vjp(functools.partial(_attn_heads, blk=blk), *_attn_args(ins))
        g = vjp([do_ref[:, _head_cols(i)].astype(f32) for i in range(Q_HEADS)])
        for i in range(Q_HEADS):
            dq_ref[:, _head_cols(i)] = g[0][i]
            ds_ref[:, i:i + 1] += g[7][i]
        for h in range(KV_HEADS):
            hs = _head_cols(h)
            dkp_ref[:, hs] = g[1][h]
            dkc_ref[:, hs] = g[2][h]
            dkm_ref[:, hs] += g[3][h]
            dvp_ref[:, hs] = g[4][h]
            dvc_ref[:, hs] = g[5][h]
            dvm_ref[:, hs] += g[6][h]

    blk_spec = pl.BlockSpec((BLOCK, KV_W), lambda i: (i, 0))
    one_spec = pl.BlockSpec((BLOCK, KV_W), lambda i: (0, 0))
    kv_shape = jax.ShapeDtypeStruct((lp, KV_W), f32)
    one_shape = jax.ShapeDtypeStruct((BLOCK, KV_W), f32)
    return pl.pallas_call(
        body, name=name, grid=(nb,),
        in_specs=_attn_specs() + [pl.BlockSpec((BLOCK, Q_W), lambda i: (i, 0))],
        out_specs=[pl.BlockSpec((BLOCK, Q_W), lambda i: (i, 0)), blk_spec, blk_spec, one_spec,
                   blk_spec, blk_spec, one_spec, _full((1, Q_HEADS))],
        out_shape=[jax.ShapeDtypeStruct((lp, Q_W), f32), kv_shape, kv_shape, one_shape,
                   kv_shape, kv_shape, one_shape, jax.ShapeDtypeStruct((1, Q_HEADS), f32)],
        compiler_params=_params(("arbitrary",)),
    )(q, k, k, k, v, v, v, sinks, do)


def _kv_combine(d_prev, d_cur, d_meta, *, name):
    lp = d_cur.shape[0]
    nb = lp // BLOCK

    def body(p_ref, c_ref, m_ref, o_ref):
        j = pl.program_id(0)
        acc = c_ref[...] + jnp.where(j < nb - 1, p_ref[...], 0.0)
        o_ref[...] = acc + jnp.where(j == 0, m_ref[...], 0.0)

    return pl.pallas_call(
        body, name=name, grid=(nb,),
        in_specs=[pl.BlockSpec((BLOCK, KV_W), lambda j: (jnp.minimum(j + 1, nb - 1), 0)),
                  pl.BlockSpec((BLOCK, KV_W), lambda j: (j, 0)),
                  pl.BlockSpec((BLOCK, KV_W), lambda j: (0, 0))],
        out_specs=pl.BlockSpec((BLOCK, KV_W), lambda j: (j, 0)),
        out_shape=jax.ShapeDtypeStruct((lp, KV_W), f32),
        compiler_params=_params(("parallel",)),
    )(d_prev, d_cur, d_meta)


def _scan_chunk(s0, r, lw, k, v, a, b):
    t = r[0].shape[0]
    ii = lax.broadcasted_iota(jnp.int32, (t, t), 0)
    jj = lax.broadcasted_iota(jnp.int32, (t, t), 1)
    incl = jj <= ii
    strict = jj < ii
    tri = incl.astype(f32)
    eye = jnp.where(ii == jj, 1.0, 0.0)
    cl = [_dot3(tri, x) for x in lw]
    e_pos = [jnp.exp(c) for c in cl]
    e_neg = [jnp.exp(-c) for c in cl]
    e_prev = [jnp.exp(c - x) for c, x in zip(cl, lw)]
    rt = [x * e for x, e in zip(r, e_pos)]
    at = [x * e for x, e in zip(a, e_prev)]
    bt = [x * e for x, e in zip(b, e_neg)]
    kt = [x * e for x, e in zip(k, e_neg)]
    l_ab = [jnp.where(strict, _dot(x, y, "nt"), 0.0) for x, y in zip(at, bt)]
    l_ak = [jnp.where(strict, _dot(x, y, "nt"), 0.0) for x, y in zip(at, kt)]
    r_b = [jnp.where(incl, _dot(x, y, "nt"), 0.0) for x, y in zip(rt, bt)]
    r_k = [jnp.where(incl, _dot(x, y, "nt"), 0.0) for x, y in zip(rt, kt)]
    inv = [eye + x for x in l_ab]
    pw = l_ab
    for _ in range(int(math.log2(t)) - 1):
        pw = [_dot(x, x) for x in pw]
        inv = [x + _dot(x, y) for x, y in zip(inv, pw)]
    rhs = [_dot(x, s, "nt") + _dot(m, y) for x, s, m, y in zip(at, s0, l_ak, v)]
    u = [_dot(x, y) for x, y in zip(inv, rhs)]
    y_s = [_dot(x, s, "nt") for x, s in zip(rt, s0)]
    y = [ys + _dot(m, uu) + _dot(n, vv) for ys, m, uu, n, vv in zip(y_s, r_b, u, r_k, v)]
    grow = [s + _dot(uu, x, "tn") + _dot(vv, z, "tn") for s, uu, x, vv, z in zip(s0, u, bt, v, kt)]
    s1 = [g * e[t - 1:t, :] for g, e in zip(grow, e_pos)]
    return y, s1


def _head_rows(h):
    return slice(h * RWKV_HEAD, (h + 1) * RWKV_HEAD)


def _per_head(ref):
    return [ref[:, _head_rows(h)] for h in range(RWKV_HEADS)]


def _scan(r, lw, k, v, a, b, *, name):
    lp = r.shape[0]
    nc = lp // CHUNK
    row = pl.BlockSpec((CHUNK, RWKV_DIM), lambda c: (c, 0))

    def body(r_ref, lw_ref, k_ref, v_ref, a_ref, b_ref, y_ref, s_ref, state):
        @pl.when(pl.program_id(0) == 0)
        def _():
            state[...] = jnp.zeros_like(state)

        s_ref[...] = state[...]
        s0 = [state[_head_rows(h), :] for h in range(RWKV_HEADS)]
        y, s1 = _scan_chunk(s0, *[_per_head(ref) for ref in (r_ref, lw_ref, k_ref, v_ref, a_ref, b_ref)])
        for h in range(RWKV_HEADS):
            y_ref[:, _head_rows(h)] = y[h]
            state[_head_rows(h), :] = s1[h]

    return pl.pallas_call(
        body, name=name, grid=(nc,), in_specs=[row] * 6,
        out_specs=[row, pl.BlockSpec((RWKV_DIM, RWKV_HEAD), lambda c: (c, 0))],
        out_shape=[jax.ShapeDtypeStruct((lp, RWKV_DIM), f32), jax.ShapeDtypeStruct((nc * RWKV_DIM, RWKV_HEAD), f32)],
        scratch_shapes=[pltpu.VMEM((RWKV_DIM, RWKV_HEAD), f32)],
        compiler_params=_params(("arbitrary",)),
    )(r, lw, k, v, a, b)


def _scan_bwd(r, lw, k, v, a, b, states, dy, *, name):
    lp = r.shape[0]
    nc = lp // CHUNK
    row = pl.BlockSpec((CHUNK, RWKV_DIM), lambda c: (nc - 1 - c, 0))

    def body(r_ref, lw_ref, k_ref, v_ref, a_ref, b_ref, s_ref, dy_ref,
             dr_ref, dlw_ref, dk_ref, dv_ref, da_ref, db_ref, dstate):
        @pl.when(pl.program_id(0) == 0)
        def _():
            dstate[...] = jnp.zeros_like(dstate)

        outs = (dr_ref, dlw_ref, dk_ref, dv_ref, da_ref, db_ref)
        s0 = [s_ref[_head_rows(h), :] for h in range(RWKV_HEADS)]
        _, vjp = jax.vjp(_scan_chunk, s0, *[_per_head(ref) for ref in (r_ref, lw_ref, k_ref, v_ref, a_ref, b_ref)])
        g = vjp((_per_head(dy_ref), [dstate[_head_rows(h), :] for h in range(RWKV_HEADS)]))
        for h in range(RWKV_HEADS):
            dstate[_head_rows(h), :] = g[0][h]
            for o_ref, gv in zip(outs, g[1:]):
                o_ref[:, _head_rows(h)] = gv[h]

    shape = jax.ShapeDtypeStruct((lp, RWKV_DIM), f32)
    return pl.pallas_call(
        body, name=name, grid=(nc,),
        in_specs=[row] * 6 + [pl.BlockSpec((RWKV_DIM, RWKV_HEAD), lambda c: (nc - 1 - c, 0)), row],
        out_specs=[row] * 6, out_shape=[shape] * 6,
        scratch_shapes=[pltpu.VMEM((RWKV_DIM, RWKV_HEAD), f32)],
        compiler_params=_params(("arbitrary",)),
    )(r, lw, k, v, a, b, states, dy)


def _loss_head(h2, target, g_final, *, name):
    lp = h2.shape[0]
    tm = BLOCK
    front_tiles = FRONT // tm

    def body(h_ref, t_ref, g_ref, loss_ref, dh_ref, dg_ref):
        i = pl.program_id(0)
        real = i >= front_tiles

        def tile_loss(hv, gv):
            err = _rms(hv, gv) - t_ref[...]
            return jnp.where(real, 0.5 * jnp.sum(jnp.mean(err * err, axis=-1, keepdims=True)), 0.0)

        loss, (dh, dg) = jax.value_and_grad(tile_loss, argnums=(0, 1))(h_ref[...], g_ref[...])

        @pl.when(i == 0)
        def _():
            loss_ref[...] = jnp.zeros_like(loss_ref)
            dg_ref[...] = jnp.zeros_like(dg_ref)

        loss_ref[...] += jnp.full(loss_ref.shape, loss, f32)
        dg_ref[...] += dg
        dh_ref[...] = dh

    return pl.pallas_call(
        body, name=name, grid=(lp // tm,),
        in_specs=[pl.BlockSpec((tm, D_MODEL), lambda i: (i, 0)),
                  pl.BlockSpec((tm, D_MODEL), lambda i: (jnp.maximum(i - front_tiles, 0), 0)),
                  _full(g_final.shape)],
        out_specs=[_full((8, 128)), pl.BlockSpec((tm, D_MODEL), lambda i: (i, 0)), _full(g_final.shape)],
        out_shape=[jax.ShapeDtypeStruct((8, 128), f32), jax.ShapeDtypeStruct((lp, D_MODEL), f32),
                   jax.ShapeDtypeStruct(g_final.shape, f32)],
        compiler_params=_params(("arbitrary",)),
    )(h2, target, g_final)


def _local_step(x, target, meta, p, late_weights=None, emit=None):
    emit = emit or (lambda group, grads: 0.0)
    seq = x.shape[0]
    lp = seq + FRONT
    h0 = jnp.concatenate([jnp.zeros((PAD, D_MODEL), f32), meta, x], axis=0)
    cos_t, sin_t, swap = _rope_tables(lp)
    hsum = _head_sum_matrix(RWKV_DIM, RWKV_HEAD)
    hmean = hsum / RWKV_HEAD
    w_qkv_t, w_rkv_t = p["w_in_t"][:ATTN_PROJ], p["w_in_t"][ATTN_PROJ:ATTN_PROJ + RKV_W]
    w_lora_t, w_gates_t = p["w_in_t"][ATTN_PROJ + RKV_W:ATTN_PROJ + RWKV_PROJ], p["w_in_t"][ATTN_PROJ + RWKV_PROJ:]
    b_qkv, b_rkv = p["b_in"][:, :ATTN_PROJ], p["b_in"][:, ATTN_PROJ:ATTN_PROJ + RKV_W]
    b_lora, b_gates = p["b_in"][:, ATTN_PROJ + RKV_W:ATTN_PROJ + RWKV_PROJ], p["b_in"][:, ATTN_PROJ + RWKV_PROJ:]
    prep_params = [p["w0"], p["w2"], p["a0"], p["a2"], p["g2"], p["k_k"], p["k_a"], hsum]
    post_params = [p["ln_w"], p["ln_b"], p["r_k"], hmean]

    (u,) = _rowwise(lambda hv, g: (_rms(hv, g),), [h0], [p["norm_mix_g"]], [(D_MODEL, bf16)], name="norm_mix")
    qkv = _mm(u, w_qkv_t, "nt", name="proj_qkv", bias=b_qkv, zero_rows_below=PAD)
    p_rkv = _mm(u, w_rkv_t, "nt", name="proj_rkv", bias=b_rkv, zero_rows_below=PAD)
    p_lora = _mm(u, w_lora_t, "nt", name="proj_lora", bias=b_lora, zero_rows_below=PAD)
    gates = _mm(u, w_gates_t, "nt", name="proj_gates", bias=b_gates, zero_rows_below=PAD)

    q, k, v = _rowwise(_attn_prep, [qkv, cos_t, sin_t], [swap], [(Q_W, bf16), (KV_W, bf16), (KV_W, bf16)],
                       name="attn_prep")
    y_attn = _attention(q, k, v, p["sinks"], name="attention")

    mix_rkv, mix_lora = p["mix"][:, :RKV_W], p["mix"][:, RKV_W:]
    pf_rkv = _token_shift(p_rkv, mix_rkv, name="shift_rkv")
    pf_lora = _token_shift(p_lora, mix_lora, name="shift_lora")
    wide = [(RWKV_DIM, f32)] * 7
    r_, lw_, k_, v_, a_, b_, g_ = _rowwise(_rwkv_prep, [pf_rkv, pf_lora], prep_params, wide, name="rwkv_prep")
    y_scan, states = _scan(r_, lw_, k_, v_, a_, b_, name="wkv_scan")
    (y_rwkv,) = _rowwise(_rwkv_post, [y_scan, r_, k_, v_, g_], post_params, [(RWKV_DIM, f32)], name="rwkv_post")

    if late_weights is not None:
        p = {**p, **late_weights(y_rwkv)}
    br_a = _mm(y_attn, p["w_br_attn_t"], "nt", name="branch_attn")
    br_r = _mm(y_rwkv, p["w_br_rwkv_t"], "nt", name="branch_rwkv")
    (merged,) = _rowwise(_merge, [gates, br_a, br_r], [], [(D_MODEL, bf16)], name="merge")
    h1 = _mm(merged, p["w_o"], "nn", name="out_proj", add=h0)
    (f,) = _rowwise(lambda hv, g: (_rms(hv, g),), [h1], [p["norm_ffn_g"]], [(D_MODEL, bf16)], name="norm_ffn")
    gate = _mm(f, p["w_gate_t"], "nt", name="ffn_gate")
    up = _mm(f, p["w_up_t"], "nt", name="ffn_up")
    (act,) = _rowwise(_swiglu, [gate, up], [], [(D_FF, bf16)], name="swiglu")
    h2 = _mm(act, p["w_down"], "nn", name="ffn_down", add=h1)

    loss8, dh2, d_final_g = _loss_head(h2, target, p["norm_final_g"], name="loss_head")
    dact = _mm(dh2, p["w_down"], "nt", name="d_act")
    d_w_down = _mm_tn(act, dh2, name="dw_down")
    dgate, dup = _rowwise_bwd(_swiglu, [gate, up], [], [[dact]], name="swiglu_bwd",
                              diff_rows=[True, True], diff_params=[])
    d_w_gate_t = _mm_tn(dgate, f, name="dw_gate")
    d_w_up_t = _mm_tn(dup, f, name="dw_up")
    zero = emit("ffn", dict(w_down=d_w_down, w_gate_t=d_w_gate_t, w_up_t=d_w_up_t))
    df = _mm(dgate, p["w_gate_t"], "nn", name="d_f_gate")
    df = _mm(dup, p["w_up_t"], "nn", name="d_f_up", add=df)
    dh1_n, d_ffn_g = _rowwise_bwd(lambda hv, g: (_rms(hv, g),), [h1], [p["norm_ffn_g"] + zero], [[df]],
                                  name="norm_ffn_bwd", diff_rows=[True], diff_params=[True])
    (dh1,) = _rowwise(lambda x1, x2: (x1 + x2,), [dh2, dh1_n], [], [(D_MODEL, f32)], name="dh1_sum")
    dmerged = _mm(dh1, p["w_o"], "nt", name="d_merged")
    d_w_o = _mm_tn(merged, dh1, name="dw_o")
    dgates, dbr_a, dbr_r = _rowwise_bwd(_merge, [gates, br_a, br_r], [], [[dmerged]], name="merge_bwd",
                                        diff_rows=[True, True, True], diff_params=[])
    d_w_br_attn_t = _mm_tn(dbr_a, y_attn, name="dw_br_attn")
    d_w_br_rwkv_t = _mm_tn(dbr_r, y_rwkv, name="dw_br_rwkv")
    zero = emit("branch", dict(w_o=d_w_o, w_br_attn_t=d_w_br_attn_t, w_br_rwkv_t=d_w_br_rwkv_t))
    dy_attn = _mm(dbr_a, p["w_br_attn_t"], "nn", name="d_y_attn")
    dy_rwkv = _mm(dbr_r, p["w_br_rwkv_t"], "nn", name="d_y_rwkv")

    post_params = [p["ln_w"] + zero, p["ln_b"], p["r_k"], hmean]
    res = _rowwise_bwd(_rwkv_post, [y_scan, r_, k_, v_, g_], post_params, [[dy_rwkv]], name="rwkv_post_bwd",
                       diff_rows=[True] * 5, diff_params=[True, True, True, False])
    dy_scan, dr_p, dk_p, dv_p, dg_p, d_ln_w, d_ln_b, d_r_k = res
    dr_s, dlw_s, dk_s, dv_s, da_s, db_s = _scan_bwd(r_, lw_, k_, v_, a_, b_, states, dy_scan, name="wkv_scan_bwd")
    res = _rowwise_bwd(_rwkv_prep, [pf_rkv, pf_lora], prep_params,
                       [[dr_s, dr_p], [dlw_s], [dk_s, dk_p], [dv_s, dv_p], [da_s], [db_s], [dg_p]],
                       name="rwkv_prep_bwd", diff_rows=[True, True], diff_params=[True] * 7 + [False],
                       zero_rows_below=PAD)
    dpf_rkv, dpf_lora, d_w0, d_w2, d_a0, d_a2, d_g2, d_k_k, d_k_a = res
    dp_rkv, d_mix_rkv = _token_shift_bwd(p_rkv, mix_rkv, dpf_rkv, name="shift_rkv_bwd")
    dp_lora, d_mix_lora = _token_shift_bwd(p_lora, mix_lora, dpf_lora, name="shift_lora_bwd")

    dq, dkp, dkc, dkm, dvp, dvc, dvm, d_sinks = _attention_bwd(q, k, v, p["sinks"], dy_attn, name="attention_bwd")
    dk = _kv_combine(dkp, dkc, dkm, name="dk_sum")
    dv = _kv_combine(dvp, dvc, dvm, name="dv_sum")
    (dqkv,) = _rowwise_bwd(_attn_prep, [qkv, cos_t, sin_t], [swap], [[dq], [dk], [dv]], name="attn_prep_bwd",
                           diff_rows=[True, False, False], diff_params=[False])

    d_w_qkv_t, db_qkv = _mm_tn(dqkv, u, name="dw_qkv", colsum=True)
    d_w_rkv_t, db_rkv = _mm_tn(dp_rkv, u, name="dw_rkv", colsum=True)
    d_w_lora_t, db_lora = _mm_tn(dp_lora, u, name="dw_lora", colsum=True)
    d_w_gates_t, db_gates = _mm_tn(dgates, u, name="dw_gates", colsum=True)
    d_w_in_t = jnp.concatenate([d_w_qkv_t, d_w_rkv_t, d_w_lora_t, d_w_gates_t], axis=0)
    zero = emit("input", dict(w_in_t=d_w_in_t, g2=d_g2, w2=d_w2, a2=d_a2))
    du = _mm(dqkv, w_qkv_t, "nn", name="d_u_qkv")
    du = _mm(dp_rkv, w_rkv_t, "nn", name="d_u_rkv", add=du)
    du = _mm(dp_lora, w_lora_t, "nn", name="d_u_lora", add=du)
    du = _mm(dgates, w_gates_t, "nn", name="d_u_gates", add=du)
    dh0_n, d_mix_g = _rowwise_bwd(lambda hv, g: (_rms(hv, g),), [h0], [p["norm_mix_g"] + zero], [[du]],
                                  name="norm_mix_bwd", diff_rows=[True], diff_params=[True])
    (dh0,) = _rowwise(lambda x1, x2: (x1 + x2,), [dh1, dh0_n], [], [(D_MODEL, f32)], name="dh0_sum")

    grads = dict(
        w_in_t=d_w_in_t,
        b_in=jnp.concatenate([db_qkv, db_rkv, db_lora, db_gates], axis=1),
        mix=jnp.concatenate([d_mix_rkv, d_mix_lora], axis=1),
        norm_mix_g=d_mix_g, sinks=d_sinks, w0=d_w0, w2=d_w2, a0=d_a0, a2=d_a2, g2=d_g2, k_k=d_k_k, k_a=d_k_a,
        r_k=d_r_k, ln_w=d_ln_w, ln_b=d_ln_b, w_br_attn_t=d_w_br_attn_t, w_br_rwkv_t=d_w_br_rwkv_t, w_o=d_w_o,
        norm_ffn_g=d_ffn_g, w_gate_t=d_w_gate_t, w_up_t=d_w_up_t, w_down=d_w_down, norm_final_g=d_final_g,
        meta=dh0[PAD:FRONT],
    )
    return loss8[0, 0], dh0[FRONT:], grads


def _position():
    return lax.axis_index("x"), lax.axis_index("y"), lax.axis_index("c")


def _other_chips(x, y):
    return [(1 - x, y), (x, 1 - y), (1 - x, 1 - y)]


_HBM = pl.BlockSpec(memory_space=pltpu.HBM)
_SEM = pl.BlockSpec(memory_space=pltpu.SEMAPHORE)
_EFFECT = pltpu.SideEffectType.DATAFLOW_SIDE_EFFECTING


def _chip_copies(src_refs, land_refs, send_sems, recv_sems, gather):
    x, y, c = _position()
    copies = []
    for a, (src, land) in enumerate(zip(src_refs, land_refs)):
        for j, (px, py) in enumerate(_other_chips(x, y)):
            copies.append(pltpu.make_async_remote_copy(
                src_ref=src if gather else src.at[2 * px + py],
                dst_ref=land.at[2 * x + y] if gather else land.at[j],
                send_sem=send_sems.at[3 * a + j], recv_sem=recv_sems.at[3 * a + j],
                device_id=(px, py, c), device_id_type=MESH))
    return copies


def _exchange_start(srcs, *, gather, name):
    n = len(srcs)
    lands = [lax.empty((N_CHIPS,) + s.shape if gather else (3,) + s.shape[1:], s.dtype) for s in srcs]

    def body(*refs):
        for cp in _chip_copies(refs[:n], refs[n:2 * n], refs[2 * n], refs[2 * n + 1], gather):
            cp.start()
        refs[-1][...] = jnp.zeros_like(refs[-1])

    res = pl.pallas_call(
        body, name=name,
        out_shape=(pltpu.SemaphoreType.DMA((3 * n,)), pltpu.SemaphoreType.DMA((3 * n,)),
                   *[pltpu.HBM(a.shape, a.dtype) for a in srcs + lands], jax.ShapeDtypeStruct((8, 128), f32)),
        in_specs=[_HBM] * (2 * n),
        out_specs=(_SEM, _SEM, *[_HBM] * (2 * n), pl.BlockSpec(memory_space=pltpu.VMEM)),
        input_output_aliases={i: 2 + i for i in range(2 * n)},
        compiler_params=pltpu.CompilerParams(has_side_effects=_EFFECT),
    )(*[pltpu.with_memory_space_constraint(a, pltpu.HBM) for a in srcs + lands])
    return res[0], res[1], list(res[2:2 + n]), list(res[2 + n:2 + 2 * n]), res[-1]


def _exchange_wait(handle, after, *, gather, name):
    send_sems, recv_sems, srcs, lands, _ = handle
    n = len(srcs)

    def body(*refs):
        for cp in _chip_copies(refs[:n], refs[n:2 * n], refs[2 * n], refs[2 * n + 1], gather):
            cp.wait_send()
            cp.wait_recv()

    res = pl.pallas_call(
        body, name=name,
        out_shape=tuple(pltpu.HBM(a.shape, a.dtype) for a in srcs + lands),
        in_specs=[_HBM] * (2 * n) + [_SEM, _SEM, pl.BlockSpec(memory_space=pl.ANY)],
        out_specs=tuple([_HBM] * (2 * n)),
        input_output_aliases={i: i for i in range(2 * n)},
        compiler_params=pltpu.CompilerParams(has_side_effects=_EFFECT),
    )(*srcs, *lands, send_sems, recv_sems, after)
    return list(res[:n]), list(res[n:])


def _sum_own_and_received(g, recv, *, name):
    _, r, w = g.shape
    tm = _tile(r)
    x, y, _ = _position()
    me = jnp.reshape(2 * x + y, (1,)).astype(jnp.int32)

    def body(me_ref, g_ref, r_ref, o_ref):
        o_ref[...] = g_ref[0] + r_ref[0] + r_ref[1] + r_ref[2]

    return pl.pallas_call(
        body, name=name,
        grid_spec=pltpu.PrefetchScalarGridSpec(
            num_scalar_prefetch=1, grid=(r // tm,),
            in_specs=[pl.BlockSpec((1, tm, w), lambda i, me_ref: (me_ref[0], i, 0)),
                      pl.BlockSpec((3, tm, w), lambda i, me_ref: (0, i, 0))],
            out_specs=pl.BlockSpec((tm, w), lambda i, me_ref: (i, 0))),
        out_shape=jax.ShapeDtypeStruct((r, w), f32),
        compiler_params=_params(("parallel",)),
    )(me, g, recv)


def _swap_cores(arrs, *, name):
    n = len(arrs)

    def body(*refs):
        x, y, c = _position()
        copies = [pltpu.make_async_remote_copy(
            src_ref=refs[i], dst_ref=refs[n + i], send_sem=refs[2 * n].at[i], recv_sem=refs[2 * n + 1].at[i],
            device_id=(x, y, 1 - c), device_id_type=MESH) for i in range(n)]
        for cp in copies:
            cp.start()
        for cp in copies:
            cp.wait_recv()
        for cp in copies:
            cp.wait_send()

    return pl.pallas_call(
        body, name=name,
        in_specs=[pl.BlockSpec(memory_space=pl.ANY)] * n,
        out_specs=[pl.BlockSpec(memory_space=pl.ANY)] * n,
        out_shape=[jax.ShapeDtypeStruct(a.shape, a.dtype) for a in arrs],
        scratch_shapes=[pltpu.SemaphoreType.DMA((n,)), pltpu.SemaphoreType.DMA((n,))],
    )(*arrs)


def _all_reduce_small(a, *, name):
    rows, w = a.shape

    def body(a_ref, o_ref, buf, send_sems, recv_sems):
        x, y, c = _position()
        me = 4 * x + 2 * y + c
        buf[0] = a_ref[...]
        sends = []
        for rel in range(1, N_DEV):
            peer = ((1 - x) if rel & 4 else x, (1 - y) if rel & 2 else y, (1 - c) if rel & 1 else c)
            cp = pltpu.make_async_remote_copy(
                src_ref=a_ref, dst_ref=buf.at[rel], send_sem=send_sems.at[rel - 1], recv_sem=recv_sems.at[rel - 1],
                device_id=peer, device_id_type=MESH)
            cp.start()
            sends.append(cp)
        for cp in sends:
            cp.wait_recv()
        for cp in sends:
            cp.wait_send()
        acc = buf[jnp.bitwise_xor(me, 0)]
        for d in range(1, N_DEV):
            acc = acc + buf[jnp.bitwise_xor(me, d)]
        o_ref[...] = acc

    return pl.pallas_call(
        body, name=name,
        in_specs=[pl.BlockSpec(memory_space=pltpu.VMEM)],
        out_specs=pl.BlockSpec(memory_space=pltpu.VMEM),
        out_shape=jax.ShapeDtypeStruct((rows, w), f32),
        scratch_shapes=[pltpu.VMEM((N_DEV, rows, w), f32), pltpu.SemaphoreType.DMA((N_DEV - 1,)),
                        pltpu.SemaphoreType.DMA((N_DEV - 1,))],
    )(a)


def _adamw(w, g, m, v, *, name):
    rows, cols = w.shape
    tm = _tile(rows, 256)

    def body(w_ref, g_ref, m_ref, v_ref, d_ref, nm_ref, nv_ref):
        gv = g_ref[...]
        nm = ADAM_B1 * m_ref[...] + (1.0 - ADAM_B1) * gv
        nv = ADAM_B2 * v_ref[...] + (1.0 - ADAM_B2) * (gv * gv)
        m_hat = nm / (1.0 - ADAM_B1 ** ADAM_STEP)
        v_hat = nv / (1.0 - ADAM_B2 ** ADAM_STEP)
        d_ref[...] = -ADAM_LR * (m_hat / (jnp.sqrt(v_hat) + ADAM_EPS) + ADAM_WD * w_ref[...])
        nm_ref[...] = nm
        nv_ref[...] = nv

    spec = pl.BlockSpec((tm, cols), lambda i: (i, 0))
    shape = jax.ShapeDtypeStruct((rows, cols), f32)
    return pl.pallas_call(
        body, name=name, grid=(rows // tm,), in_specs=[spec] * 4, out_specs=[spec] * 3, out_shape=[shape] * 3,
        compiler_params=_params(("parallel",)),
    )(w, g, m, v)


def _pad_rows(a, rows):
    return jnp.concatenate([a, jnp.zeros((rows - a.shape[0], a.shape[1]), a.dtype)], axis=0) if rows > a.shape[0] else a


_SMALL = (("norm_mix_g", D_MODEL), ("b_in", D_IN), ("sinks", Q_HEADS), ("mix", RWKV_PROJ), ("w0", RWKV_DIM),
          ("a0", RWKV_DIM), ("k_k", RWKV_DIM), ("k_a", RWKV_DIM), ("r_k", RWKV_DIM), ("ln_w", RWKV_DIM),
          ("ln_b", RWKV_DIM), ("norm_ffn_g", D_MODEL), ("norm_final_g", D_MODEL))


def _pack_small(d):
    flat = jnp.concatenate([d[n].reshape(-1).astype(f32) for n, _ in _SMALL])
    return flat


def _unpack_small(flat):
    out, off = {}, 0
    for n, size in _SMALL:
        out[n] = flat[off:off + size]
        off += size
    return out


_SMALL_TOTAL = sum(s for _, s in _SMALL)


def kernel(x, meta_tokens, norm_mix_g, w_in, b_in, attn_sinks, rwkv_mix, rwkv_w0, rwkv_w2, rwkv_a0, rwkv_a2, rwkv_g2, rwkv_k_k, rwkv_k_a, rwkv_r_k, rwkv_ln_w, rwkv_ln_b, w_br_attn, w_br_rwkv, w_o, norm_ffn_g, w_ffn_gate, w_ffn_up, w_ffn_down, norm_final_g, loss_target, m_meta_tokens, m_norm_mix_g, m_w_in, m_b_in, m_attn_sinks, m_rwkv_mix, m_rwkv_w0, m_rwkv_w2, m_rwkv_a0, m_rwkv_a2, m_rwkv_g2, m_rwkv_k_k, m_rwkv_k_a, m_rwkv_r_k, m_rwkv_ln_w, m_rwkv_ln_b, m_w_br_attn, m_w_br_rwkv, m_w_o, m_norm_ffn_g, m_w_ffn_gate, m_w_ffn_up, m_w_ffn_down, m_norm_final_g, v_meta_tokens, v_norm_mix_g, v_w_in, v_b_in, v_attn_sinks, v_rwkv_mix, v_rwkv_w0, v_rwkv_w2, v_rwkv_a0, v_rwkv_a2, v_rwkv_g2, v_rwkv_k_k, v_rwkv_k_a, v_rwkv_r_k, v_rwkv_ln_w, v_rwkv_ln_b, v_w_br_attn, v_w_br_rwkv, v_w_o, v_norm_ffn_g, v_w_ffn_gate, v_w_ffn_up, v_w_ffn_down, v_norm_final_g):
    names = ("meta_tokens", "norm_mix_g", "w_in", "b_in", "attn_sinks", "rwkv_mix", "rwkv_w0", "rwkv_w2", "rwkv_a0",
             "rwkv_a2", "rwkv_g2", "rwkv_k_k", "rwkv_k_a", "rwkv_r_k", "rwkv_ln_w", "rwkv_ln_b", "w_br_attn",
             "w_br_rwkv", "w_o", "norm_ffn_g", "w_ffn_gate", "w_ffn_up", "w_ffn_down", "norm_final_g")
    w_all = dict(zip(names, (meta_tokens, norm_mix_g, w_in, b_in, attn_sinks, rwkv_mix, rwkv_w0, rwkv_w2, rwkv_a0,
                             rwkv_a2, rwkv_g2, rwkv_k_k, rwkv_k_a, rwkv_r_k, rwkv_ln_w, rwkv_ln_b, w_br_attn,
                             w_br_rwkv, w_o, norm_ffn_g, w_ffn_gate, w_ffn_up, w_ffn_down, norm_final_g)))
    m_all = dict(zip(names, (m_meta_tokens, m_norm_mix_g, m_w_in, m_b_in, m_attn_sinks, m_rwkv_mix, m_rwkv_w0,
                             m_rwkv_w2, m_rwkv_a0, m_rwkv_a2, m_rwkv_g2, m_rwkv_k_k, m_rwkv_k_a, m_rwkv_r_k,
                             m_rwkv_ln_w, m_rwkv_ln_b, m_w_br_attn, m_w_br_rwkv, m_w_o, m_norm_ffn_g, m_w_ffn_gate,
                             m_w_ffn_up, m_w_ffn_down, m_norm_final_g)))
    v_all = dict(zip(names, (v_meta_tokens, v_norm_mix_g, v_w_in, v_b_in, v_attn_sinks, v_rwkv_mix, v_rwkv_w0,
                             v_rwkv_w2, v_rwkv_a0, v_rwkv_a2, v_rwkv_g2, v_rwkv_k_k, v_rwkv_k_a, v_rwkv_r_k,
                             v_rwkv_ln_w, v_rwkv_ln_b, v_w_br_attn, v_w_br_rwkv, v_w_o, v_norm_ffn_g, v_w_ffn_gate,
                             v_w_ffn_up, v_w_ffn_down, v_norm_final_g)))
    cx, cy, _ = _position()
    chip = 2 * cx + cy

    t_of = dict(w_in_t="w_in", w_gate_t="w_ffn_gate", w_up_t="w_ffn_up", w_br_attn_t="w_br_attn",
                w_br_rwkv_t="w_br_rwkv", g2_t="rwkv_g2", w2_t="rwkv_w2", a2_t="rwkv_a2")
    plain_of = dict(w_down="w_ffn_down", w_o="w_o")
    meta_cols = meta_tokens.shape[1]

    def shard(k):
        return (w_all[t_of[k]][0].T if k in t_of else w_all[plain_of[k]][0]).astype(bf16)

    def whole(zone, own):
        return lax.dynamic_update_slice_in_dim(zone, own[None], chip, axis=0).reshape(-1, own.shape[-1])

    early = ("w_in_t", "g2_t", "w2_t", "a2_t")
    late = ("w_gate_t", "w_up_t", "w_down", "w_o", "w_br_attn_t", "w_br_rwkv_t")
    early_h = _exchange_start([shard(k) for k in early] + [meta_tokens], gather=True, name="gather_early_start")
    late_h = _exchange_start([shard(k) for k in late], gather=True, name="gather_late_start")
    own, zones = _exchange_wait(early_h, late_h[4], gather=True, name="gather_early_wait")
    got = {k: whole(z, o) for k, z, o in zip(early, zones, own)}
    meta_full = whole(zones[-1], own[-1]).reshape(N_CHIPS, N_META, meta_cols).transpose(1, 0, 2).reshape(N_META, -1)
    p = dict(
        w_in_t=got["w_in_t"], g2=got["g2_t"].T.astype(f32), w2=got["w2_t"].T.astype(f32),
        a2=got["a2_t"].T.astype(f32),
        b_in=b_in, sinks=attn_sinks, mix=rwkv_mix, w0=rwkv_w0, a0=rwkv_a0, k_k=rwkv_k_k, k_a=rwkv_k_a,
        r_k=rwkv_r_k.reshape(1, RWKV_DIM), ln_w=rwkv_ln_w, ln_b=rwkv_ln_b, norm_mix_g=norm_mix_g,
        norm_ffn_g=norm_ffn_g, norm_final_g=norm_final_g.reshape(1, D_MODEL),
    )

    def late_weights(after):
        own_l, zones_l = _exchange_wait(late_h, after, gather=True, name="gather_late_wait")
        return {k: whole(z, o) for k, z, o in zip(late, zones_l, own_l)}

    started = {}

    def emit(group, grads_):
        keys = list(grads_)
        slabs = []
        for k in keys:
            a = grads_[k].T if k in ("g2", "w2", "a2") else grads_[k]
            slabs.append(a.reshape(N_CHIPS, a.shape[0] // N_CHIPS, a.shape[1]))
        started[group] = (keys, _exchange_start(slabs, gather=False, name="scatter_" + group + "_start"))
        return started[group][1][4][0, 0]

    loss, dx, g = _local_step(x[0], loss_target[0], meta_full, p, late_weights, emit)

    def partial_sums(group, after):
        keys, handle = started[group]
        slabs, lands = _exchange_wait(handle, after, gather=False, name="scatter_" + group + "_wait")
        return {k: _sum_own_and_received(s, l, name="sum_chips_" + k) for k, s, l in zip(keys, slabs, lands)}

    parts = partial_sums("ffn", dx)
    parts.update(partial_sums("branch", parts["w_down"]))
    parts.update(partial_sums("input", parts["w_o"]))
    keys = list(parts)
    others = _swap_cores([parts[k] for k in keys], name="swap_cores")
    gs = {}
    for k, other in zip(keys, others):
        (gs[k],) = _rowwise(lambda a, b: (a + b,), [parts[k], other], [], [(other.shape[1], f32)],
                            name="sum_cores_" + k)
    gs.update(g2_t=gs.pop("g2"), w2_t=gs.pop("w2"), a2_t=gs.pop("a2"))

    small = jnp.concatenate([_pack_small(g), loss.reshape(1)])
    small_rows = -(-small.shape[0] // PACK_W)
    small = jnp.concatenate([small, jnp.zeros((small_rows * PACK_W - small.shape[0],), f32)]).reshape(small_rows, PACK_W)
    small_rows8 = -(-(small_rows + N_META) // 8) * 8
    reduced = _all_reduce_small(_pad_rows(jnp.concatenate([g["meta"], small], axis=0), small_rows8),
                                name="reduce_small")
    g_meta = lax.dynamic_slice_in_dim(reduced[:N_META], chip * meta_cols, meta_cols, axis=1)
    flat = reduced[N_META:N_META + small_rows].reshape(-1)
    g_small = _unpack_small(flat)
    loss_total = flat[_SMALL_TOTAL]

    small_of = dict(norm_mix_g="norm_mix_g", b_in="b_in", attn_sinks="sinks", rwkv_mix="mix", rwkv_w0="w0",
                    rwkv_a0="a0", rwkv_k_k="k_k", rwkv_k_a="k_a", rwkv_r_k="r_k", rwkv_ln_w="ln_w",
                    rwkv_ln_b="ln_b", norm_ffn_g="norm_ffn_g", norm_final_g="norm_final_g")
    grads = {"meta_tokens": g_meta}
    for n, k in small_of.items():
        grads[n] = g_small[k].reshape(w_all[n].shape)
    for k, n in t_of.items():
        grads[n] = gs[k].T.reshape(w_all[n].shape)
    for k, n in plain_of.items():
        grads[n] = gs[k].reshape(w_all[n].shape)

    big = ("w_in", "w_ffn_gate", "w_ffn_up", "w_ffn_down", "w_o", "w_br_attn", "w_br_rwkv", "rwkv_g2", "rwkv_w2",
           "rwkv_a2")
    delta, new_m, new_v = {}, {}, {}
    for n in big:
        shape2 = w_all[n].shape[1:]
        d_, m_, v_ = _adamw(w_all[n].reshape(shape2), grads[n].reshape(shape2), m_all[n].reshape(shape2),
                            v_all[n].reshape(shape2), name="adamw_" + n)
        delta[n], new_m[n], new_v[n] = (t.reshape(w_all[n].shape) for t in (d_, m_, v_))
    rest = [n for n in names if n not in big]

    def pack_rest(src):
        flat_ = jnp.concatenate([src[n].reshape(-1) for n in rest])
        rows_ = -(-flat_.shape[0] // (8 * PACK_W)) * 8
        return jnp.concatenate([flat_, jnp.ones((rows_ * PACK_W - flat_.shape[0],), f32)]).reshape(rows_, PACK_W)

    d_, m_, v_ = _adamw(pack_rest(w_all), pack_rest(grads), pack_rest(m_all), pack_rest(v_all), name="adamw_small")
    off = 0
    for n in rest:
        size = w_all[n].size
        for dst, src in ((delta, d_), (new_m, m_), (new_v, v_)):
            dst[n] = src.reshape(-1)[off:off + size].reshape(w_all[n].shape)
        off += size

    return (loss_total, dx.reshape(x.shape), *[grads[n] for n in names], *[delta[n] for n in names],
            *[new_m[n] for n in names], *[new_v[n] for n in names])
```

```python
import functools
import math

import jax
import jax.numpy as jnp
from jax import lax
from jax.experimental import pallas as pl
from jax.experimental.pallas import tpu as pltpu

f32 = jnp.float32
bf16 = jnp.bfloat16

D_MODEL = 1024
N_META = 16
HEAD_DIM = 64
Q_HEADS = 8
KV_HEADS = 2
GROUP = Q_HEADS // KV_HEADS
WINDOW = 128
BLOCK = 128
ROPE_THETA = 500000.0
ROPE_DIM = HEAD_DIM // 4
RWKV_HEADS = 8
RWKV_HEAD = 64
RWKV_DIM = RWKV_HEADS * RWKV_HEAD
DECAY_LORA = 64
AAA_LORA = 64
GATE_LORA = 160
LORA_W = DECAY_LORA + AAA_LORA + GATE_LORA
RWKV_LN_EPS = 64e-5
D_FF = 2816
Q_W = Q_HEADS * HEAD_DIM
KV_W = KV_HEADS * HEAD_DIM
ATTN_PROJ = Q_W + 2 * KV_W
RKV_W = 3 * RWKV_DIM
RWKV_PROJ = RKV_W + LORA_W
D_IN = ATTN_PROJ + RWKV_PROJ + 2 * D_MODEL
RMS_EPS = 1e-6
NEG_INF = -1e30
PAD = BLOCK - N_META
FRONT = PAD + N_META

ADAM_LR = 0.001
ADAM_B1 = 0.9
ADAM_B2 = 0.999
ADAM_EPS = 1e-08
ADAM_WD = 0.01
ADAM_STEP = 10

N_CHIPS = 4
N_DEV = 8
CHUNK = 64
VMEM_LIMIT = 56 * 1024 * 1024
PACK_W = 1024
MESH = pl.DeviceIdType.MESH
HIGHEST = lax.Precision.HIGHEST


def _tile(m, pref=384):
    for step in (16, 8):
        for t in range(min(m, pref) // step * step, 0, -step):
            if m % t == 0:
                return t
    return m


def _params(sem=None):
    return pltpu.CompilerParams(dimension_semantics=sem, vmem_limit_bytes=VMEM_LIMIT)


def _full(shape):
    nd = len(shape)
    return pl.BlockSpec(shape, lambda *_: (0,) * nd)


def _dot(a, b, dims="nn", exact=False):
    dn = {"nn": (((1,), (0,)), ((), ())), "nt": (((1,), (1,)), ((), ())), "tn": (((0,), (0,)), ((), ()))}[dims]
    if exact:
        return lax.dot_general(a.astype(f32), b.astype(f32), dn, precision=HIGHEST, preferred_element_type=f32)
    return lax.dot_general(a.astype(bf16), b.astype(bf16), dn, preferred_element_type=f32)


def _dot3(a, b):
    a_hi, b_hi = a.astype(bf16), b.astype(bf16)
    a_lo = (a - a_hi.astype(f32)).astype(bf16)
    b_lo = (b - b_hi.astype(f32)).astype(bf16)
    return _dot(a_hi, b_hi) + (_dot(a_hi, b_lo) + _dot(a_lo, b_hi))


def _two_pass(x, m):
    x_hi = x.astype(bf16)
    x_lo = (x - x_hi.astype(f32)).astype(bf16)
    return _dot(x_hi, m) + _dot(x_lo, m)


@jax.custom_vjp
def _dot_const(x, m):
    return _two_pass(x, m)


def _dot_const_fwd(x, m):
    return _two_pass(x, m), m


def _dot_const_bwd(m, ct):
    return _two_pass(ct, m.T), jnp.zeros_like(m)


_dot_const.defvjp(_dot_const_fwd, _dot_const_bwd)


def _mm(a, b, mode, *, name, out_dtype=f32, bias=None, add=None, zero_rows_below=0):
    m, _ = a.shape
    n = b.shape[1] if mode == "nn" else b.shape[0]
    tm = _tile(m)
    has_bias, has_add = bias is not None, add is not None

    def body(*refs):
        a_ref, b_ref = refs[0], refs[1]
        o_ref = refs[-1]
        acc = _dot(a_ref[...], b_ref[...], mode)
        k = 2
        if has_bias:
            acc = acc + refs[k][...]
            k += 1
        if zero_rows_below:
            rows = pl.program_id(0) * tm + lax.broadcasted_iota(jnp.int32, acc.shape, 0)
            acc = jnp.where(rows >= zero_rows_below, acc, 0.0)
        if has_add:
            acc = acc + refs[k][...].astype(f32)
        o_ref[...] = acc.astype(out_dtype)

    ins = [a, b]
    in_specs = [pl.BlockSpec((tm, a.shape[1]), lambda i: (i, 0)), _full(b.shape)]
    if has_bias:
        ins.append(bias)
        in_specs.append(_full(bias.shape))
    if has_add:
        ins.append(add)
        in_specs.append(pl.BlockSpec((tm, n), lambda i: (i, 0)))
    return pl.pallas_call(
        body, name=name, grid=(m // tm,), in_specs=in_specs,
        out_specs=pl.BlockSpec((tm, n), lambda i: (i, 0)),
        out_shape=jax.ShapeDtypeStruct((m, n), out_dtype),
        compiler_params=_params(("parallel",)),
    )(*ins)


def _mm_tn(a, b, *, name, colsum=False, out_dtype=bf16):
    r, m = a.shape
    n = b.shape[1]
    tr = _tile(r)
    tmo = m
    for cand in (1408, 1024, 768, 512):
        if m > 1024 and m % cand == 0:
            tmo = cand
            break
    steps = r // tr

    def body(a_ref, b_ref, o_ref, *rest):
        acc = rest[-1]
        i = pl.program_id(1)

        @pl.when(i == 0)
        def _():
            acc[...] = jnp.zeros_like(acc)
            if colsum:
                rest[0][...] = jnp.zeros_like(rest[0])

        acc[...] += _dot(a_ref[...], b_ref[...], "tn")
        if colsum:
            rest[0][...] += jnp.sum(a_ref[...].astype(f32), axis=0, keepdims=True)

        @pl.when(i == steps - 1)
        def _():
            o_ref[...] = acc[...].astype(out_dtype)

    out_shape = [jax.ShapeDtypeStruct((m, n), out_dtype)]
    out_specs = [pl.BlockSpec((tmo, n), lambda j, i: (j, 0))]
    if colsum:
        out_shape.append(jax.ShapeDtypeStruct((1, m), f32))
        out_specs.append(pl.BlockSpec((1, tmo), lambda j, i: (0, j)))
    res = pl.pallas_call(
        body, name=name, grid=(m // tmo, steps),
        in_specs=[pl.BlockSpec((tr, tmo), lambda j, i: (i, j)), pl.BlockSpec((tr, n), lambda j, i: (i, 0))],
        out_specs=out_specs, out_shape=out_shape,
        scratch_shapes=[pltpu.VMEM((tmo, n), f32)],
        compiler_params=_params(("parallel", "arbitrary")),
    )(a, b)
    return res if colsum else res[0]


def _rowwise(fn, rows, params, outs, *, name, tm=None, with_row0=False):
    m = rows[0].shape[0]
    tm = tm or _tile(m)
    nr, npar = len(rows), len(params)

    def body(*refs):
        vals = [r[...] for r in refs[:nr + npar]]
        kw = dict(row0=pl.program_id(0) * tm) if with_row0 else {}
        res = fn(*vals, **kw)
        for o_ref, v in zip(refs[nr + npar:], res):
            o_ref[...] = v.astype(o_ref.dtype)

    return pl.pallas_call(
        body, name=name, grid=(m // tm,),
        in_specs=[pl.BlockSpec((tm, r.shape[1]), lambda i: (i, 0)) for r in rows] + [_full(p.shape) for p in params],
        out_specs=[pl.BlockSpec((tm, w), lambda i: (i, 0)) for w, _ in outs],
        out_shape=[jax.ShapeDtypeStruct((m, w), dt) for w, dt in outs],
        compiler_params=_params(("parallel",)),
    )(*rows, *params)


def _rowwise_bwd(fn, rows, params, cts, *, name, diff_rows, diff_params, tm=None, with_row0=False, zero_rows_below=0,
                 out_dtypes=None):
    m = rows[0].shape[0]
    tm = tm or _tile(m)
    nr, npar = len(rows), len(params)
    d_idx = [i for i in range(nr) if diff_rows[i]]
    p_idx = [i for i in range(npar) if diff_params[i]]
    out_dtypes = out_dtypes or [f32] * len(d_idx)
    flat_cts = [c for group in cts for c in group]
    n_ct = len(flat_cts)

    def body(*refs):
        vals = [r[...] for r in refs[:nr + npar]]
        ct_refs = refs[nr + npar:nr + npar + n_ct]
        out_refs = refs[nr + npar + n_ct:]
        kw = dict(row0=pl.program_id(0) * tm) if with_row0 else {}
        ct_vals, k = [], 0
        for group in cts:
            acc = ct_refs[k][...].astype(f32)
            for extra in range(1, len(group)):
                acc = acc + ct_refs[k + extra][...].astype(f32)
            k += len(group)
            if zero_rows_below:
                rr = pl.program_id(0) * tm + lax.broadcasted_iota(jnp.int32, acc.shape, 0)
                acc = jnp.where(rr >= zero_rows_below, acc, 0.0)
            ct_vals.append(acc)

        def g(*dargs):
            full = list(vals)
            for pos, i in enumerate(d_idx):
                full[i] = dargs[pos]
            for pos, i in enumerate(p_idx):
                full[nr + i] = dargs[len(d_idx) + pos]
            return tuple(fn(*full, **kw))

        _, vjp = jax.vjp(g, *[vals[i].astype(f32) for i in d_idx], *[vals[nr + i] for i in p_idx])
        grads = vjp(tuple(ct_vals))
        for pos in range(len(d_idx)):
            out_refs[pos][...] = grads[pos].astype(out_refs[pos].dtype)
        first = pl.program_id(0) == 0
        for pos in range(len(p_idx)):
            o_ref = out_refs[len(d_idx) + pos]

            @pl.when(first)
            def _(o_ref=o_ref):
                o_ref[...] = jnp.zeros_like(o_ref)

            o_ref[...] += grads[len(d_idx) + pos]

    return pl.pallas_call(
        body, name=name, grid=(m // tm,),
        in_specs=[pl.BlockSpec((tm, r.shape[1]), lambda i: (i, 0)) for r in rows] + [_full(p.shape) for p in params]
        + [pl.BlockSpec((tm, c.shape[1]), lambda i: (i, 0)) for c in flat_cts],
        out_specs=[pl.BlockSpec((tm, rows[i].shape[1]), lambda i_: (i_, 0)) for i in d_idx]
        + [_full(params[i].shape) for i in p_idx],
        out_shape=[jax.ShapeDtypeStruct(rows[i].shape, dt) for i, dt in zip(d_idx, out_dtypes)]
        + [jax.ShapeDtypeStruct(params[i].shape, f32) for i in p_idx],
        compiler_params=_params(("arbitrary",)),
    )(*rows, *params, *flat_cts)


def _rms(x, g):
    return x * lax.rsqrt(jnp.mean(x * x, axis=-1, keepdims=True) + RMS_EPS) * g


def _head_sum_matrix(width, head):
    idx = jnp.arange(width) // head
    return (idx[:, None] == idx[None, :]).astype(f32)


def _rope_tables(lp):
    half = ROPE_DIM // 2
    pos = (jnp.arange(lp) - PAD).astype(f32)
    inv_freq = jnp.power(jnp.float32(ROPE_THETA), -jnp.arange(half, dtype=f32) * (2.0 / ROPE_DIM))
    ang = pos[:, None] * inv_freq[None, :]
    cos, sin = jnp.cos(ang), jnp.sin(ang)
    ones = jnp.ones((lp, HEAD_DIM - ROPE_DIM), f32)
    zeros = jnp.zeros((lp, HEAD_DIM - ROPE_DIM), f32)
    cos_t = jnp.concatenate([cos, cos, ones], axis=1)
    sin_t = jnp.concatenate([-sin, sin, zeros], axis=1)
    i = jnp.arange(HEAD_DIM)
    src = jnp.where(i < half, i + half, jnp.where(i < ROPE_DIM, i - half, i))
    swap = ((i[:, None] == src[None, :]) & (i[None, :] < ROPE_DIM)).astype(f32)
    return cos_t, sin_t, swap


def _attn_prep(qkv, cos_t, sin_t, swap):
    outs = []
    for h in range(Q_HEADS + KV_HEADS):
        t = qkv[:, h * HEAD_DIM:(h + 1) * HEAD_DIM]
        outs.append(t * cos_t + _dot_const(t, swap) * sin_t)
    q = jnp.concatenate(outs[:Q_HEADS], axis=1)
    k = jnp.concatenate(outs[Q_HEADS:], axis=1)
    return q, k, qkv[:, Q_W + KV_W:]


def _softplus(z):
    return jnp.maximum(z, 0.0) + jnp.log1p(jnp.exp(-jnp.abs(z)))


def _rwkv_prep(rkv, lora, w0, w2, a0, a2, g2, k_k, k_a, hsum):
    r = rkv[:, :RWKV_DIM]
    k = rkv[:, RWKV_DIM:2 * RWKV_DIM]
    v = rkv[:, 2 * RWKV_DIM:]
    dw = lora[:, :DECAY_LORA]
    da = lora[:, DECAY_LORA:DECAY_LORA + AAA_LORA]
    dg = lora[:, DECAY_LORA + AAA_LORA:]
    w = -_softplus(-(w0 + _dot(jnp.tanh(dw), w2))) - 0.5
    a = jax.nn.sigmoid(a0 + _dot(da, a2))
    g = _dot(jax.nn.sigmoid(dg), g2)
    kk = k * k_k
    kk = kk * lax.rsqrt(jnp.maximum(_dot_const(kk * kk, hsum), 1e-24))
    k = k * (1.0 + (a - 1.0) * k_a)
    log_decay = -jnp.exp(w)
    return r, log_decay, k, v, -kk, kk * a, g


def _rwkv_post(y, r, k, v, g, ln_w, ln_b, r_k, hmean):
    hsum = hmean * RWKV_HEAD
    mean = _dot_const(y, hmean)
    yc = y - mean
    var = _dot_const(yc * yc, hmean)
    yn = yc * lax.rsqrt(var + RWKV_LN_EPS) * ln_w + ln_b
    bonus = _dot_const(r * k * r_k, hsum) * v
    return ((yn + bonus) * g,)


def _merge(gates, br_a, br_r):
    sg = jax.nn.sigmoid(gates)
    return (sg[:, :D_MODEL] * br_a + sg[:, D_MODEL:] * br_r,)


def _swiglu(gate, up):
    return (jax.nn.silu(gate) * up,)


def _token_shift(p, mix, *, name):
    m, c = p.shape
    tm = _tile(m)
    sub = tm // 8

    def body(p_ref, prev_ref, mix_ref, o_ref):
        x = p_ref[...]
        rows = lax.broadcasted_iota(jnp.int32, x.shape, 0)
        last = jnp.where(pl.program_id(0) == 0, 0.0, prev_ref[7:8, :])
        xp = jnp.where(rows == 0, last, pltpu.roll(x, 1, axis=0))
        o_ref[...] = x + (xp - x) * mix_ref[...]

    return pl.pallas_call(
        body, name=name, grid=(m // tm,),
        in_specs=[pl.BlockSpec((tm, c), lambda i: (i, 0)),
                  pl.BlockSpec((8, c), lambda i: (jnp.maximum(i * sub - 1, 0), 0)),
                  _full(mix.shape)],
        out_specs=pl.BlockSpec((tm, c), lambda i: (i, 0)),
        out_shape=jax.ShapeDtypeStruct((m, c), f32),
        compiler_params=_params(("parallel",)),
    )(p, p, mix)


def _token_shift_bwd(p, mix, dpf, *, name):
    m, c = p.shape
    tm = _tile(m)
    sub = tm // 8
    n_tiles = m // tm

    def body(p_ref, prev_ref, mix_ref, d_ref, nxt_ref, dp_ref, dmix_ref):
        i = pl.program_id(0)
        x = p_ref[...]
        d = d_ref[...]
        mixv = mix_ref[...]
        rows = lax.broadcasted_iota(jnp.int32, x.shape, 0)
        last = jnp.where(i == 0, 0.0, prev_ref[7:8, :])
        xp = jnp.where(rows == 0, last, pltpu.roll(x, 1, axis=0))
        dm = d * mixv
        first_next = jnp.where(i == n_tiles - 1, 0.0, nxt_ref[0:1, :] * mixv)
        dm_next = jnp.where(rows == tm - 1, first_next, pltpu.roll(dm, tm - 1, axis=0))
        dp = d - dm + dm_next
        dp_ref[...] = jnp.where(i * tm + rows >= PAD, dp, 0.0).astype(dp_ref.dtype)

        @pl.when(i == 0)
        def _():
            dmix_ref[...] = jnp.zeros_like(dmix_ref)

        dmix_ref[...] += jnp.sum(d * (xp - x), axis=0, keepdims=True)

    return pl.pallas_call(
        body, name=name, grid=(n_tiles,),
        in_specs=[pl.BlockSpec((tm, c), lambda i: (i, 0)),
                  pl.BlockSpec((8, c), lambda i: (jnp.maximum(i * sub - 1, 0), 0)),
                  _full(mix.shape),
                  pl.BlockSpec((tm, c), lambda i: (i, 0)),
                  pl.BlockSpec((8, c), lambda i: (jnp.minimum((i + 1) * sub, m // 8 - 1), 0))],
        out_specs=[pl.BlockSpec((tm, c), lambda i: (i, 0)), _full(mix.shape)],
        out_shape=[jax.ShapeDtypeStruct((m, c), bf16), jax.ShapeDtypeStruct(mix.shape, f32)],
        compiler_params=_params(("arbitrary",)),
    )(p, p, mix, dpf, dpf)


def _attn_heads(q, kp, kc, km, vp, vc, vm, sink, blk):
    scale = HEAD_DIM ** -0.5
    heads = range(len(q))
    qi = lax.broadcasted_iota(jnp.int32, (BLOCK, BLOCK), 0)
    ki = lax.broadcasted_iota(jnp.int32, (BLOCK, BLOCK), 1)
    qpos = blk * BLOCK + qi - PAD
    kpos_c = blk * BLOCK + ki - PAD
    kpos_p = kpos_c - BLOCK
    kpos_m = ki - PAD

    def band(kpos):
        return (kpos >= N_META) & (kpos <= qpos) & (qpos - kpos < WINDOW)

    ok_p, ok_c, ok_m = band(kpos_p), band(kpos_c), (kpos_m >= 0) & (kpos_m <= qpos)
    s_p = [jnp.where(ok_p, _dot(q[i], kp[i // GROUP], "nt") * scale, NEG_INF) for i in heads]
    s_c = [jnp.where(ok_c, _dot(q[i], kc[i // GROUP], "nt") * scale, NEG_INF) for i in heads]
    s_m = [jnp.where(ok_m, _dot(q[i], km[i // GROUP], "nt") * scale, NEG_INF) for i in heads]
    mx = [jnp.maximum(jnp.maximum(jnp.max(s_p[i], -1, keepdims=True), jnp.max(s_c[i], -1, keepdims=True)),
                      jnp.maximum(jnp.max(s_m[i], -1, keepdims=True), sink[i])) for i in heads]
    e_p = [jnp.exp(s_p[i] - mx[i]) for i in heads]
    e_c = [jnp.exp(s_c[i] - mx[i]) for i in heads]
    e_m = [jnp.exp(s_m[i] - mx[i]) for i in heads]
    inv = [1.0 / (jnp.sum(e_p[i], -1, keepdims=True) + jnp.sum(e_c[i], -1, keepdims=True)
                  + jnp.sum(e_m[i], -1, keepdims=True) + jnp.exp(sink[i] - mx[i])) for i in heads]
    return [_dot(e_p[i] * inv[i], vp[i // GROUP]) + _dot(e_c[i] * inv[i], vc[i // GROUP])
            + _dot(e_m[i] * inv[i], vm[i // GROUP]) for i in heads]


def _attn_specs():
    prev = lambda i: (jnp.maximum(i - 1, 0), 0)
    cur = lambda i: (i, 0)
    meta = lambda i: (0, 0)
    return [pl.BlockSpec((BLOCK, Q_W), cur),
            pl.BlockSpec((BLOCK, KV_W), prev), pl.BlockSpec((BLOCK, KV_W), cur), pl.BlockSpec((BLOCK, KV_W), meta),
            pl.BlockSpec((BLOCK, KV_W), prev), pl.BlockSpec((BLOCK, KV_W), cur), pl.BlockSpec((BLOCK, KV_W), meta),
            _full((1, Q_HEADS))]


def _head_cols(i):
    return slice(i * HEAD_DIM, (i + 1) * HEAD_DIM)


def _attn_args(refs):
    q_ref, s_ref = refs[0], refs[7]
    args = [[q_ref[:, _head_cols(i)].astype(f32) for i in range(Q_HEADS)]]
    args += [[ref[:, _head_cols(h)].astype(f32) for h in range(KV_HEADS)] for ref in refs[1:7]]
    return args + [[s_ref[:, i:i + 1] for i in range(Q_HEADS)]]


def _attention(q, k, v, sinks, *, name):
    lp = q.shape[0]

    def body(*refs):
        o_ref = refs[-1]
        out = _attn_heads(*_attn_args(refs[:-1]), pl.program_id(0))
        for i in range(Q_HEADS):
            o_ref[:, _head_cols(i)] = out[i].astype(o_ref.dtype)

    return pl.pallas_call(
        body, name=name, grid=(lp // BLOCK,), in_specs=_attn_specs(),
        out_specs=pl.BlockSpec((BLOCK, Q_W), lambda i: (i, 0)),
        out_shape=jax.ShapeDtypeStruct((lp, Q_W), bf16),
        compiler_params=_params(("parallel",)),
    )(q, k, k, k, v, v, v, sinks)


def _attention_bwd(q, k, v, sinks, do, *, name):
    lp = q.shape[0]
    nb = lp // BLOCK

    def body(*refs):
        ins, do_ref = refs[:8], refs[8]
        dq_ref, dkp_ref, dkc_ref, dkm_ref, dvp_ref, dvc_ref, dvm_ref, ds_ref = refs[9:]
        blk = pl.program_id(0)

        @pl.when(blk == 0)
        def _():
            dkm_ref[...] = jnp.zeros_like(dkm_ref)
            dvm_ref[...] = jnp.zeros_like(dvm_ref)
            ds_ref[...] = jnp.zeros_like(ds_ref)

        _, vjp = jax.vjp(functools.partial(_attn_heads, blk=blk), *_attn_args(ins))
        g = vjp([do_ref[:, _head_cols(i)].astype(f32) for i in range(Q_HEADS)])
        for i in range(Q_HEADS):
            dq_ref[:, _head_cols(i)] = g[0][i]
            ds_ref[:, i:i + 1] += g[7][i]
        for h in range(KV_HEADS):
            hs = _head_cols(h)
            dkp_ref[:, hs] = g[1][h]
            dkc_ref[:, hs] = g[2][h]
            dkm_ref[:, hs] += g[3][h]
            dvp_ref[:, hs] = g[4][h]
            dvc_ref[:, hs] = g[5][h]
            dvm_ref[:, hs] += g[6][h]

    blk_spec = pl.BlockSpec((BLOCK, KV_W), lambda i: (i, 0))
    one_spec = pl.BlockSpec((BLOCK, KV_W), lambda i: (0, 0))
    kv_shape = jax.ShapeDtypeStruct((lp, KV_W), f32)
    one_shape = jax.ShapeDtypeStruct((BLOCK, KV_W), f32)
    return pl.pallas_call(
        body, name=name, grid=(nb,),
        in_specs=_attn_specs() + [pl.BlockSpec((BLOCK, Q_W), lambda i: (i, 0))],
        out_specs=[pl.BlockSpec((BLOCK, Q_W), lambda i: (i, 0)), blk_spec, blk_spec, one_spec,
                   blk_spec, blk_spec, one_spec, _full((1, Q_HEADS))],
        out_shape=[jax.ShapeDtypeStruct((lp, Q_W), f32), kv_shape, kv_shape, one_shape,
                   kv_shape, kv_shape, one_shape, jax.ShapeDtypeStruct((1, Q_HEADS), f32)],
        compiler_params=_params(("arbitrary",)),
    )(q, k, k, k, v, v, v, sinks, do)


def _kv_combine(d_prev, d_cur, d_meta, *, name):
    lp = d_cur.shape[0]
    nb = lp // BLOCK

    def body(p_ref, c_ref, m_ref, o_ref):
        j = pl.program_id(0)
        acc = c_ref[...] + jnp.where(j < nb - 1, p_ref[...], 0.0)
        o_ref[...] = acc + jnp.where(j == 0, m_ref[...], 0.0)

    return pl.pallas_call(
        body, name=name, grid=(nb,),
        in_specs=[pl.BlockSpec((BLOCK, KV_W), lambda j: (jnp.minimum(j + 1, nb - 1), 0)),
                  pl.BlockSpec((BLOCK, KV_W), lambda j: (j, 0)),
                  pl.BlockSpec((BLOCK, KV_W), lambda j: (0, 0))],
        out_specs=pl.BlockSpec((BLOCK, KV_W), lambda j: (j, 0)),
        out_shape=jax.ShapeDtypeStruct((lp, KV_W), f32),
        compiler_params=_params(("parallel",)),
    )(d_prev, d_cur, d_meta)


def _scan_chunk(s0, r, lw, k, v, a, b):
    t = r[0].shape[0]
    ii = lax.broadcasted_iota(jnp.int32, (t, t), 0)
    jj = lax.broadcasted_iota(jnp.int32, (t, t), 1)
    incl = jj <= ii
    strict = jj < ii
    tri = incl.astype(f32)
    eye = jnp.where(ii == jj, 1.0, 0.0)
    cl = [_dot3(tri, x) for x in lw]
    e_pos = [jnp.exp(c) for c in cl]
    e_neg = [jnp.exp(-c) for c in cl]
    e_prev = [jnp.exp(c - x) for c, x in zip(cl, lw)]
    rt = [x * e for x, e in zip(r, e_pos)]
    at = [x * e for x, e in zip(a, e_prev)]
    bt = [x * e for x, e in zip(b, e_neg)]
    kt = [x * e for x, e in zip(k, e_neg)]
    l_ab = [jnp.where(strict, _dot(x, y, "nt"), 0.0) for x, y in zip(at, bt)]
    l_ak = [jnp.where(strict, _dot(x, y, "nt"), 0.0) for x, y in zip(at, kt)]
    r_b = [jnp.where(incl, _dot(x, y, "nt"), 0.0) for x, y in zip(rt, bt)]
    r_k = [jnp.where(incl, _dot(x, y, "nt"), 0.0) for x, y in zip(rt, kt)]
    inv = [eye + x for x in l_ab]
    pw = l_ab
    for _ in range(int(math.log2(t)) - 1):
        pw = [_dot(x, x) for x in pw]
        inv = [x + _dot(x, y) for x, y in zip(inv, pw)]
    rhs = [_dot(x, s, "nt") + _dot(m, y) for x, s, m, y in zip(at, s0, l_ak, v)]
    u = [_dot(x, y) for x, y in zip(inv, rhs)]
    y_s = [_dot(x, s, "nt") for x, s in zip(rt, s0)]
    y = [ys + _dot(m, uu) + _dot(n, vv) for ys, m, uu, n, vv in zip(y_s, r_b, u, r_k, v)]
    grow = [s + _dot(uu, x, "tn") + _dot(vv, z, "tn") for s, uu, x, vv, z in zip(s0, u, bt, v, kt)]
    s1 = [g * e[t - 1:t, :] for g, e in zip(grow, e_pos)]
    return y, s1


def _head_rows(h):
    return slice(h * RWKV_HEAD, (h + 1) * RWKV_HEAD)


def _per_head(ref):
    return [ref[:, _head_rows(h)] for h in range(RWKV_HEADS)]


def _scan(r, lw, k, v, a, b, *, name):
    lp = r.shape[0]
    nc = lp // CHUNK
    row = pl.BlockSpec((CHUNK, RWKV_DIM), lambda c: (c, 0))

    def body(r_ref, lw_ref, k_ref, v_ref, a_ref, b_ref, y_ref, s_ref, state):
        @pl.when(pl.program_id(0) == 0)
        def _():
            state[...] = jnp.zeros_like(state)

        s_ref[...] = state[...]
        s0 = [state[_head_rows(h), :] for h in range(RWKV_HEADS)]
        y, s1 = _scan_chunk(s0, *[_per_head(ref) for ref in (r_ref, lw_ref, k_ref, v_ref, a_ref, b_ref)])
        for h in range(RWKV_HEADS):
            y_ref[:, _head_rows(h)] = y[h]
            state[_head_rows(h), :] = s1[h]

    return pl.pallas_call(
        body, name=name, grid=(nc,), in_specs=[row] * 6,
        out_specs=[row, pl.BlockSpec((RWKV_DIM, RWKV_HEAD), lambda c: (c, 0))],
        out_shape=[jax.ShapeDtypeStruct((lp, RWKV_DIM), f32), jax.ShapeDtypeStruct((nc * RWKV_DIM, RWKV_HEAD), f32)],
        scratch_shapes=[pltpu.VMEM((RWKV_DIM, RWKV_HEAD), f32)],
        compiler_params=_params(("arbitrary",)),
    )(r, lw, k, v, a, b)


def _scan_bwd(r, lw, k, v, a, b, states, dy, *, name):
    lp = r.shape[0]
    nc = lp // CHUNK
    row = pl.BlockSpec((CHUNK, RWKV_DIM), lambda c: (nc - 1 - c, 0))

    def body(r_ref, lw_ref, k_ref, v_ref, a_ref, b_ref, s_ref, dy_ref,
             dr_ref, dlw_ref, dk_ref, dv_ref, da_ref, db_ref, dstate):
        @pl.when(pl.program_id(0) == 0)
        def _():
            dstate[...] = jnp.zeros_like(dstate)

        outs = (dr_ref, dlw_ref, dk_ref, dv_ref, da_ref, db_ref)
        s0 = [s_ref[_head_rows(h), :] for h in range(RWKV_HEADS)]
        _, vjp = jax.vjp(_scan_chunk, s0, *[_per_head(ref) for ref in (r_ref, lw_ref, k_ref, v_ref, a_ref, b_ref)])
        g = vjp((_per_head(dy_ref), [dstate[_head_rows(h), :] for h in range(RWKV_HEADS)]))
        for h in range(RWKV_HEADS):
            dstate[_head_rows(h), :] = g[0][h]
            for o_ref, gv in zip(outs, g[1:]):
                o_ref[:, _head_rows(h)] = gv[h]

    shape = jax.ShapeDtypeStruct((lp, RWKV_DIM), f32)
    return pl.pallas_call(
        body, name=name, grid=(nc,),
        in_specs=[row] * 6 + [pl.BlockSpec((RWKV_DIM, RWKV_HEAD), lambda c: (nc - 1 - c, 0)), row],
        out_specs=[row] * 6, out_shape=[shape] * 6,
        scratch_shapes=[pltpu.VMEM((RWKV_DIM, RWKV_HEAD), f32)],
        compiler_params=_params(("arbitrary",)),
    )(r, lw, k, v, a, b, states, dy)


def _loss_head(h2, target, g_final, *, name):
    lp = h2.shape[0]
    tm = BLOCK
    front_tiles = FRONT // tm

    def body(h_ref, t_ref, g_ref, loss_ref, dh_ref, dg_ref):
        i = pl.program_id(0)
        real = i >= front_tiles

        def tile_loss(hv, gv):
            err = _rms(hv, gv) - t_ref[...]
            return jnp.where(real, 0.5 * jnp.sum(jnp.mean(err * err, axis=-1, keepdims=True)), 0.0)

        loss, (dh, dg) = jax.value_and_grad(tile_loss, argnums=(0, 1))(h_ref[...], g_ref[...])

        @pl.when(i == 0)
        def _():
            loss_ref[...] = jnp.zeros_like(loss_ref)
            dg_ref[...] = jnp.zeros_like(dg_ref)

        loss_ref[...] += jnp.full(loss_ref.shape, loss, f32)
        dg_ref[...] += dg
        dh_ref[...] = dh

    return pl.pallas_call(
        body, name=name, grid=(lp // tm,),
        in_specs=[pl.BlockSpec((tm, D_MODEL), lambda i: (i, 0)),
                  pl.BlockSpec((tm, D_MODEL), lambda i: (jnp.maximum(i - front_tiles, 0), 0)),
                  _full(g_final.shape)],
        out_specs=[_full((8, 128)), pl.BlockSpec((tm, D_MODEL), lambda i: (i, 0)), _full(g_final.shape)],
        out_shape=[jax.ShapeDtypeStruct((8, 128), f32), jax.ShapeDtypeStruct((lp, D_MODEL), f32),
                   jax.ShapeDtypeStruct(g_final.shape, f32)],
        compiler_params=_params(("arbitrary",)),
    )(h2, target, g_final)


def _local_step(x, target, meta, p, late_weights=None, emit=None):
    emit = emit or (lambda group, grads: 0.0)
    seq = x.shape[0]
    lp = seq + FRONT
    h0 = jnp.concatenate([jnp.zeros((PAD, D_MODEL), f32), meta, x], axis=0)
    cos_t, sin_t, swap = _rope_tables(lp)
    hsum = _head_sum_matrix(RWKV_DIM, RWKV_HEAD)
    hmean = hsum / RWKV_HEAD
    w_qkv_t, w_rkv_t = p["w_in_t"][:ATTN_PROJ], p["w_in_t"][ATTN_PROJ:ATTN_PROJ + RKV_W]
    w_lora_t, w_gates_t = p["w_in_t"][ATTN_PROJ + RKV_W:ATTN_PROJ + RWKV_PROJ], p["w_in_t"][ATTN_PROJ + RWKV_PROJ:]
    b_qkv, b_rkv = p["b_in"][:, :ATTN_PROJ], p["b_in"][:, ATTN_PROJ:ATTN_PROJ + RKV_W]
    b_lora, b_gates = p["b_in"][:, ATTN_PROJ + RKV_W:ATTN_PROJ + RWKV_PROJ], p["b_in"][:, ATTN_PROJ + RWKV_PROJ:]
    prep_params = [p["w0"], p["w2"], p["a0"], p["a2"], p["g2"], p["k_k"], p["k_a"], hsum]
    post_params = [p["ln_w"], p["ln_b"], p["r_k"], hmean]

    (u,) = _rowwise(lambda hv, g: (_rms(hv, g),), [h0], [p["norm_mix_g"]], [(D_MODEL, bf16)], name="norm_mix")
    qkv = _mm(u, w_qkv_t, "nt", name="proj_qkv", bias=b_qkv, zero_rows_below=PAD)
    p_rkv = _mm(u, w_rkv_t, "nt", name="proj_rkv", bias=b_rkv, zero_rows_below=PAD)
    p_lora = _mm(u, w_lora_t, "nt", name="proj_lora", bias=b_lora, zero_rows_below=PAD)
    gates = _mm(u, w_gates_t, "nt", name="proj_gates", bias=b_gates, zero_rows_below=PAD)

    q, k, v = _rowwise(_attn_prep, [qkv, cos_t, sin_t], [swap], [(Q_W, bf16), (KV_W, bf16), (KV_W, bf16)],
                       name="attn_prep")
    y_attn = _attention(q, k, v, p["sinks"], name="attention")

    mix_rkv, mix_lora = p["mix"][:, :RKV_W], p["mix"][:, RKV_W:]
    pf_rkv = _token_shift(p_rkv, mix_rkv, name="shift_rkv")
    pf_lora = _token_shift(p_lora, mix_lora, name="shift_lora")
    wide = [(RWKV_DIM, f32)] * 7
    r_, lw_, k_, v_, a_, b_, g_ = _rowwise(_rwkv_prep, [pf_rkv, pf_lora], prep_params, wide, name="rwkv_prep")
    y_scan, states = _scan(r_, lw_, k_, v_, a_, b_, name="wkv_scan")
    (y_rwkv,) = _rowwise(_rwkv_post, [y_scan, r_, k_, v_, g_], post_params, [(RWKV_DIM, bf16)], name="rwkv_post")

    if late_weights is not None:
        p = {**p, **late_weights(y_rwkv)}
    br_a = _mm(y_attn, p["w_br_attn_t"], "nt", name="branch_attn")
    br_r = _mm(y_rwkv, p["w_br_rwkv_t"], "nt", name="branch_rwkv")
    (merged,) = _rowwise(_merge, [gates, br_a, br_r], [], [(D_MODEL, bf16)], name="merge")
    h1 = _mm(merged, p["w_o"], "nn", name="out_proj", add=h0)
    (f,) = _rowwise(lambda hv, g: (_rms(hv, g),), [h1], [p["norm_ffn_g"]], [(D_MODEL, bf16)], name="norm_ffn")
    gate = _mm(f, p["w_gate_t"], "nt", name="ffn_gate")
    up = _mm(f, p["w_up_t"], "nt", name="ffn_up")
    (act,) = _rowwise(_swiglu, [gate, up], [], [(D_FF, bf16)], name="swiglu")
    h2 = _mm(act, p["w_down"], "nn", name="ffn_down", add=h1)

    loss8, dh2, d_final_g = _loss_head(h2, target, p["norm_final_g"], name="loss_head")
    dact = _mm(dh2, p["w_down"], "nt", name="d_act")
    d_w_down = _mm_tn(act, dh2, name="dw_down")
    dgate, dup = _rowwise_bwd(_swiglu, [gate, up], [], [[dact]], name="swiglu_bwd",
                              diff_rows=[True, True], diff_params=[], out_dtypes=[bf16, bf16])
    d_w_gate_t = _mm_tn(dgate, f, name="dw_gate")
    d_w_up_t = _mm_tn(dup, f, name="dw_up")
    zero = emit("ffn", dict(w_down=d_w_down, w_gate_t=d_w_gate_t, w_up_t=d_w_up_t))
    df = _mm(dgate, p["w_gate_t"], "nn", name="d_f_gate")
    df = _mm(dup, p["w_up_t"], "nn", name="d_f_up", add=df)
    dh1_n, d_ffn_g = _rowwise_bwd(lambda hv, g: (_rms(hv, g),), [h1], [p["norm_ffn_g"] + zero], [[df]],
                                  name="norm_ffn_bwd", diff_rows=[True], diff_params=[True])
    (dh1,) = _rowwise(lambda x1, x2: (x1 + x2,), [dh2, dh1_n], [], [(D_MODEL, f32)], name="dh1_sum")
    dmerged = _mm(dh1, p["w_o"], "nt", name="d_merged")
    d_w_o = _mm_tn(merged, dh1, name="dw_o")
    dgates, dbr_a, dbr_r = _rowwise_bwd(_merge, [gates, br_a, br_r], [], [[dmerged]], name="merge_bwd",
                                        diff_rows=[True, True, True], diff_params=[], out_dtypes=[bf16] * 3)
    d_w_br_attn_t = _mm_tn(dbr_a, y_attn, name="dw_br_attn")
    d_w_br_rwkv_t = _mm_tn(dbr_r, y_rwkv, name="dw_br_rwkv")
    zero = emit("branch", dict(w_o=d_w_o, w_br_attn_t=d_w_br_attn_t, w_br_rwkv_t=d_w_br_rwkv_t))
    dy_attn = _mm(dbr_a, p["w_br_attn_t"], "nn", name="d_y_attn")
    dy_rwkv = _mm(dbr_r, p["w_br_rwkv_t"], "nn", name="d_y_rwkv")

    post_params = [p["ln_w"] + zero, p["ln_b"], p["r_k"], hmean]
    res = _rowwise_bwd(_rwkv_post, [y_scan, r_, k_, v_, g_], post_params, [[dy_rwkv]], name="rwkv_post_bwd",
                       diff_rows=[True] * 5, diff_params=[True, True, True, False])
    dy_scan, dr_p, dk_p, dv_p, dg_p, d_ln_w, d_ln_b, d_r_k = res
    dr_s, dlw_s, dk_s, dv_s, da_s, db_s = _scan_bwd(r_, lw_, k_, v_, a_, b_, states, dy_scan, name="wkv_scan_bwd")
    res = _rowwise_bwd(_rwkv_prep, [pf_rkv, pf_lora], prep_params,
                       [[dr_s, dr_p], [dlw_s], [dk_s, dk_p], [dv_s, dv_p], [da_s], [db_s], [dg_p]],
                       name="rwkv_prep_bwd", diff_rows=[True, True], diff_params=[True] * 7 + [False],
                       zero_rows_below=PAD)
    dpf_rkv, dpf_lora, d_w0, d_w2, d_a0, d_a2, d_g2, d_k_k, d_k_a = res
    dp_rkv, d_mix_rkv = _token_shift_bwd(p_rkv, mix_rkv, dpf_rkv, name="shift_rkv_bwd")
    dp_lora, d_mix_lora = _token_shift_bwd(p_lora, mix_lora, dpf_lora, name="shift_lora_bwd")

    dq, dkp, dkc, dkm, dvp, dvc, dvm, d_sinks = _attention_bwd(q, k, v, p["sinks"], dy_attn, name="attention_bwd")
    dk = _kv_combine(dkp, dkc, dkm, name="dk_sum")
    dv = _kv_combine(dvp, dvc, dvm, name="dv_sum")
    (dqkv,) = _rowwise_bwd(_attn_prep, [qkv, cos_t, sin_t], [swap], [[dq], [dk], [dv]], name="attn_prep_bwd",
                           diff_rows=[True, False, False], diff_params=[False], out_dtypes=[bf16])

    d_w_qkv_t, db_qkv = _mm_tn(dqkv, u, name="dw_qkv", colsum=True)
    d_w_rkv_t, db_rkv = _mm_tn(dp_rkv, u, name="dw_rkv", colsum=True)
    d_w_lora_t, db_lora = _mm_tn(dp_lora, u, name="dw_lora", colsum=True)
    d_w_gates_t, db_gates = _mm_tn(dgates, u, name="dw_gates", colsum=True)
    d_w_in_t = jnp.concatenate([d_w_qkv_t, d_w_rkv_t, d_w_lora_t, d_w_gates_t], axis=0)
    zero = emit("input", dict(w_in_t=d_w_in_t, g2=d_g2, w2=d_w2, a2=d_a2))
    du = _mm(dqkv, w_qkv_t, "nn", name="d_u_qkv")
    du = _mm(dp_rkv, w_rkv_t, "nn", name="d_u_rkv", add=du)
    du = _mm(dp_lora, w_lora_t, "nn", name="d_u_lora", add=du)
    du = _mm(dgates, w_gates_t, "nn", name="d_u_gates", add=du)
    dh0_n, d_mix_g = _rowwise_bwd(lambda hv, g: (_rms(hv, g),), [h0], [p["norm_mix_g"] + zero], [[du]],
                                  name="norm_mix_bwd", diff_rows=[True], diff_params=[True])
    (dh0,) = _rowwise(lambda x1, x2: (x1 + x2,), [dh1, dh0_n], [], [(D_MODEL, f32)], name="dh0_sum")

    grads = dict(
        w_in_t=d_w_in_t,
        b_in=jnp.concatenate([db_qkv, db_rkv, db_lora, db_gates], axis=1),
        mix=jnp.concatenate([d_mix_rkv, d_mix_lora], axis=1),
        norm_mix_g=d_mix_g, sinks=d_sinks, w0=d_w0, w2=d_w2, a0=d_a0, a2=d_a2, g2=d_g2, k_k=d_k_k, k_a=d_k_a,
        r_k=d_r_k, ln_w=d_ln_w, ln_b=d_ln_b, w_br_attn_t=d_w_br_attn_t, w_br_rwkv_t=d_w_br_rwkv_t, w_o=d_w_o,
        norm_ffn_g=d_ffn_g, w_gate_t=d_w_gate_t, w_up_t=d_w_up_t, w_down=d_w_down, norm_final_g=d_final_g,
        meta=dh0[PAD:FRONT],
    )
    return loss8[0, 0], dh0[FRONT:], grads


def _position():
    return lax.axis_index("x"), lax.axis_index("y"), lax.axis_index("c")


def _other_chips(x, y):
    return [(1 - x, y), (x, 1 - y), (1 - x, 1 - y)]


_HBM = pl.BlockSpec(memory_space=pltpu.HBM)
_SEM = pl.BlockSpec(memory_space=pltpu.SEMAPHORE)
_EFFECT = pltpu.SideEffectType.DATAFLOW_SIDE_EFFECTING


def _chip_copies(src_refs, land_refs, send_sems, recv_sems, gather):
    x, y, c = _position()
    copies = []
    for a, (src, land) in enumerate(zip(src_refs, land_refs)):
        for j, (px, py) in enumerate(_other_chips(x, y)):
            copies.append(pltpu.make_async_remote_copy(
                src_ref=src if gather else src.at[2 * px + py],
                dst_ref=land.at[2 * x + y] if gather else land.at[j],
                send_sem=send_sems.at[3 * a + j], recv_sem=recv_sems.at[3 * a + j],
                device_id=(px, py, c), device_id_type=MESH))
    return copies


def _exchange_start(srcs, *, gather, name):
    n = len(srcs)
    lands = [lax.empty((N_CHIPS,) + s.shape if gather else (3,) + s.shape[1:], s.dtype) for s in srcs]

    def body(*refs):
        for cp in _chip_copies(refs[:n], refs[n:2 * n], refs[2 * n], refs[2 * n + 1], gather):
            cp.start()
        refs[-1][...] = jnp.zeros_like(refs[-1])

    res = pl.pallas_call(
        body, name=name,
        out_shape=(pltpu.SemaphoreType.DMA((3 * n,)), pltpu.SemaphoreType.DMA((3 * n,)),
                   *[pltpu.HBM(a.shape, a.dtype) for a in srcs + lands], jax.ShapeDtypeStruct((8, 128), f32)),
        in_specs=[_HBM] * (2 * n),
        out_specs=(_SEM, _SEM, *[_HBM] * (2 * n), pl.BlockSpec(memory_space=pltpu.VMEM)),
        input_output_aliases={i: 2 + i for i in range(2 * n)},
        compiler_params=pltpu.CompilerParams(has_side_effects=_EFFECT),
    )(*[pltpu.with_memory_space_constraint(a, pltpu.HBM) for a in srcs + lands])
    return res[0], res[1], list(res[2:2 + n]), list(res[2 + n:2 + 2 * n]), res[-1]


def _exchange_wait(handle, after, *, gather, name):
    send_sems, recv_sems, srcs, lands, _ = handle
    n = len(srcs)

    def body(*refs):
        for cp in _chip_copies(refs[:n], refs[n:2 * n], refs[2 * n], refs[2 * n + 1], gather):
            cp.wait_send()
            cp.wait_recv()

    res = pl.pallas_call(
        body, name=name,
        out_shape=tuple(pltpu.HBM(a.shape, a.dtype) for a in srcs + lands),
        in_specs=[_HBM] * (2 * n) + [_SEM, _SEM, pl.BlockSpec(memory_space=pl.ANY)],
        out_specs=tuple([_HBM] * (2 * n)),
        input_output_aliases={i: i for i in range(2 * n)},
        compiler_params=pltpu.CompilerParams(has_side_effects=_EFFECT),
    )(*srcs, *lands, send_sems, recv_sems, after)
    return list(res[:n]), list(res[n:])


def _sum_own_and_received(g, recv, *, name):
    _, r, w = g.shape
    tm = _tile(r)
    if g.dtype == bf16 and tm % 16:
        tm = r
    x, y, _ = _position()
    me = jnp.reshape(2 * x + y, (1,)).astype(jnp.int32)

    def body(me_ref, g_ref, r_ref, o_ref):
        o_ref[...] = (g_ref[0].astype(f32) + r_ref[0].astype(f32)) + (r_ref[1].astype(f32) + r_ref[2].astype(f32))

    return pl.pallas_call(
        body, name=name,
        grid_spec=pltpu.PrefetchScalarGridSpec(
            num_scalar_prefetch=1, grid=(r // tm,),
            in_specs=[pl.BlockSpec((1, tm, w), lambda i, me_ref: (me_ref[0], i, 0)),
                      pl.BlockSpec((3, tm, w), lambda i, me_ref: (0, i, 0))],
            out_specs=pl.BlockSpec((tm, w), lambda i, me_ref: (i, 0))),
        out_shape=jax.ShapeDtypeStruct((r, w), f32),
        compiler_params=_params(("parallel",)),
    )(me, g, recv)


def _swap_cores(arrs, *, name):
    n = len(arrs)

    def body(*refs):
        x, y, c = _position()
        copies = [pltpu.make_async_remote_copy(
            src_ref=refs[i], dst_ref=refs[n + i], send_sem=refs[2 * n].at[i], recv_sem=refs[2 * n + 1].at[i],
            device_id=(x, y, 1 - c), device_id_type=MESH) for i in range(n)]
        for cp in copies:
            cp.start()
        for cp in copies:
            cp.wait_recv()
        for cp in copies:
            cp.wait_send()

    return pl.pallas_call(
        body, name=name,
        in_specs=[pl.BlockSpec(memory_space=pl.ANY)] * n,
        out_specs=[pl.BlockSpec(memory_space=pl.ANY)] * n,
        out_shape=[jax.ShapeDtypeStruct(a.shape, a.dtype) for a in arrs],
        scratch_shapes=[pltpu.SemaphoreType.DMA((n,)), pltpu.SemaphoreType.DMA((n,))],
    )(*arrs)


def _all_reduce_small(a, *, name):
    rows, w = a.shape

    def body(a_ref, o_ref, buf, send_sems, recv_sems):
        x, y, c = _position()
        me = 4 * x + 2 * y + c
        buf[0] = a_ref[...]
        sends = []
        for rel in range(1, N_DEV):
            peer = ((1 - x) if rel & 4 else x, (1 - y) if rel & 2 else y, (1 - c) if rel & 1 else c)
            cp = pltpu.make_async_remote_copy(
                src_ref=a_ref, dst_ref=buf.at[rel], send_sem=send_sems.at[rel - 1], recv_sem=recv_sems.at[rel - 1],
                device_id=peer, device_id_type=MESH)
            cp.start()
            sends.append(cp)
        for cp in sends:
            cp.wait_recv()
        for cp in sends:
            cp.wait_send()
        acc = buf[jnp.bitwise_xor(me, 0)]
        for d in range(1, N_DEV):
            acc = acc + buf[jnp.bitwise_xor(me, d)]
        o_ref[...] = acc

    return pl.pallas_call(
        body, name=name,
        in_specs=[pl.BlockSpec(memory_space=pltpu.VMEM)],
        out_specs=pl.BlockSpec(memory_space=pltpu.VMEM),
        out_shape=jax.ShapeDtypeStruct((rows, w), f32),
        scratch_shapes=[pltpu.VMEM((N_DEV, rows, w), f32), pltpu.SemaphoreType.DMA((N_DEV - 1,)),
                        pltpu.SemaphoreType.DMA((N_DEV - 1,))],
    )(a)


def _adamw(w, g, m, v, *, name):
    rows, cols = w.shape
    tm = _tile(rows, 256)

    def body(w_ref, g_ref, m_ref, v_ref, d_ref, nm_ref, nv_ref):
        gv = g_ref[...]
        nm = ADAM_B1 * m_ref[...] + (1.0 - ADAM_B1) * gv
        nv = ADAM_B2 * v_ref[...] + (1.0 - ADAM_B2) * (gv * gv)
        m_hat = nm / (1.0 - ADAM_B1 ** ADAM_STEP)
        v_hat = nv / (1.0 - ADAM_B2 ** ADAM_STEP)
        d_ref[...] = -ADAM_LR * (m_hat / (jnp.sqrt(v_hat) + ADAM_EPS) + ADAM_WD * w_ref[...])
        nm_ref[...] = nm
        nv_ref[...] = nv

    spec = pl.BlockSpec((tm, cols), lambda i: (i, 0))
    shape = jax.ShapeDtypeStruct((rows, cols), f32)
    return pl.pallas_call(
        body, name=name, grid=(rows // tm,), in_specs=[spec] * 4, out_specs=[spec] * 3, out_shape=[shape] * 3,
        compiler_params=_params(("parallel",)),
    )(w, g, m, v)


def _pad_rows(a, rows):
    return jnp.concatenate([a, jnp.zeros((rows - a.shape[0], a.shape[1]), a.dtype)], axis=0) if rows > a.shape[0] else a


_SMALL = (("norm_mix_g", D_MODEL), ("b_in", D_IN), ("sinks", Q_HEADS), ("mix", RWKV_PROJ), ("w0", RWKV_DIM),
          ("a0", RWKV_DIM), ("k_k", RWKV_DIM), ("k_a", RWKV_DIM), ("r_k", RWKV_DIM), ("ln_w", RWKV_DIM),
          ("ln_b", RWKV_DIM), ("norm_ffn_g", D_MODEL), ("norm_final_g", D_MODEL))


def _pack_small(d):
    flat = jnp.concatenate([d[n].reshape(-1).astype(f32) for n, _ in _SMALL])
    return flat


def _unpack_small(flat):
    out, off = {}, 0
    for n, size in _SMALL:
        out[n] = flat[off:off + size]
        off += size
    return out


_SMALL_TOTAL = sum(s for _, s in _SMALL)


def kernel(x, meta_tokens, norm_mix_g, w_in, b_in, attn_sinks, rwkv_mix, rwkv_w0, rwkv_w2, rwkv_a0, rwkv_a2, rwkv_g2, rwkv_k_k, rwkv_k_a, rwkv_r_k, rwkv_ln_w, rwkv_ln_b, w_br_attn, w_br_rwkv, w_o, norm_ffn_g, w_ffn_gate, w_ffn_up, w_ffn_down, norm_final_g, loss_target, m_meta_tokens, m_norm_mix_g, m_w_in, m_b_in, m_attn_sinks, m_rwkv_mix, m_rwkv_w0, m_rwkv_w2, m_rwkv_a0, m_rwkv_a2, m_rwkv_g2, m_rwkv_k_k, m_rwkv_k_a, m_rwkv_r_k, m_rwkv_ln_w, m_rwkv_ln_b, m_w_br_attn, m_w_br_rwkv, m_w_o, m_norm_ffn_g, m_w_ffn_gate, m_w_ffn_up, m_w_ffn_down, m_norm_final_g, v_meta_tokens, v_norm_mix_g, v_w_in, v_b_in, v_attn_sinks, v_rwkv_mix, v_rwkv_w0, v_rwkv_w2, v_rwkv_a0, v_rwkv_a2, v_rwkv_g2, v_rwkv_k_k, v_rwkv_k_a, v_rwkv_r_k, v_rwkv_ln_w, v_rwkv_ln_b, v_w_br_attn, v_w_br_rwkv, v_w_o, v_norm_ffn_g, v_w_ffn_gate, v_w_ffn_up, v_w_ffn_down, v_norm_final_g):
    names = ("meta_tokens", "norm_mix_g", "w_in", "b_in", "attn_sinks", "rwkv_mix", "rwkv_w0", "rwkv_w2", "rwkv_a0",
             "rwkv_a2", "rwkv_g2", "rwkv_k_k", "rwkv_k_a", "rwkv_r_k", "rwkv_ln_w", "rwkv_ln_b", "w_br_attn",
             "w_br_rwkv", "w_o", "norm_ffn_g", "w_ffn_gate", "w_ffn_up", "w_ffn_down", "norm_final_g")
    w_all = dict(zip(names, (meta_tokens, norm_mix_g, w_in, b_in, attn_sinks, rwkv_mix, rwkv_w0, rwkv_w2, rwkv_a0,
                             rwkv_a2, rwkv_g2, rwkv_k_k, rwkv_k_a, rwkv_r_k, rwkv_ln_w, rwkv_ln_b, w_br_attn,
                             w_br_rwkv, w_o, norm_ffn_g, w_ffn_gate, w_ffn_up, w_ffn_down, norm_final_g)))
    m_all = dict(zip(names, (m_meta_tokens, m_norm_mix_g, m_w_in, m_b_in, m_attn_sinks, m_rwkv_mix, m_rwkv_w0,
                             m_rwkv_w2, m_rwkv_a0, m_rwkv_a2, m_rwkv_g2, m_rwkv_k_k, m_rwkv_k_a, m_rwkv_r_k,
                             m_rwkv_ln_w, m_rwkv_ln_b, m_w_br_attn, m_w_br_rwkv, m_w_o, m_norm_ffn_g, m_w_ffn_gate,
                             m_w_ffn_up, m_w_ffn_down, m_norm_final_g)))
    v_all = dict(zip(names, (v_meta_tokens, v_norm_mix_g, v_w_in, v_b_in, v_attn_sinks, v_rwkv_mix, v_rwkv_w0,
                             v_rwkv_w2, v_rwkv_a0, v_rwkv_a2, v_rwkv_g2, v_rwkv_k_k, v_rwkv_k_a, v_rwkv_r_k,
                             v_rwkv_ln_w, v_rwkv_ln_b, v_w_br_attn, v_w_br_rwkv, v_w_o, v_norm_ffn_g, v_w_ffn_gate,
                             v_w_ffn_up, v_w_ffn_down, v_norm_final_g)))
    cx, cy, _ = _position()
    chip = 2 * cx + cy

    t_of = dict(w_in_t="w_in", w_gate_t="w_ffn_gate", w_up_t="w_ffn_up", w_br_attn_t="w_br_attn",
                w_br_rwkv_t="w_br_rwkv", g2_t="rwkv_g2", w2_t="rwkv_w2", a2_t="rwkv_a2")
    plain_of = dict(w_down="w_ffn_down", w_o="w_o")
    meta_cols = meta_tokens.shape[1]

    def shard(k):
        return (w_all[t_of[k]][0].T if k in t_of else w_all[plain_of[k]][0]).astype(bf16)

    def whole(zone, own):
        return lax.dynamic_update_slice_in_dim(zone, own[None], chip, axis=0).reshape(-1, own.shape[-1])

    early = ("w_in_t", "g2_t", "w2_t", "a2_t")
    late = ("w_gate_t", "w_up_t", "w_down", "w_o", "w_br_attn_t", "w_br_rwkv_t")
    early_h = _exchange_start([shard(k) for k in early] + [meta_tokens], gather=True, name="gather_early_start")
    late_h = _exchange_start([shard(k) for k in late], gather=True, name="gather_late_start")
    own, zones = _exchange_wait(early_h, late_h[4], gather=True, name="gather_early_wait")
    got = {k: whole(z, o) for k, z, o in zip(early, zones, own)}
    meta_full = whole(zones[-1], own[-1]).reshape(N_CHIPS, N_META, meta_cols).transpose(1, 0, 2).reshape(N_META, -1)
    p = dict(
        w_in_t=got["w_in_t"], g2=got["g2_t"].T.astype(f32), w2=got["w2_t"].T.astype(f32),
        a2=got["a2_t"].T.astype(f32),
        b_in=b_in, sinks=attn_sinks, mix=rwkv_mix, w0=rwkv_w0, a0=rwkv_a0, k_k=rwkv_k_k, k_a=rwkv_k_a,
        r_k=rwkv_r_k.reshape(1, RWKV_DIM), ln_w=rwkv_ln_w, ln_b=rwkv_ln_b, norm_mix_g=norm_mix_g,
        norm_ffn_g=norm_ffn_g, norm_final_g=norm_final_g.reshape(1, D_MODEL),
    )

    def late_weights(after):
        own_l, zones_l = _exchange_wait(late_h, after, gather=True, name="gather_late_wait")
        return {k: whole(z, o) for k, z, o in zip(late, zones_l, own_l)}

    started = {}

    def emit(group, grads_):
        keys = list(grads_)
        slabs = []
        for k in keys:
            a = grads_[k].T if k in ("g2", "w2", "a2") else grads_[k]
            slabs.append(a.reshape(N_CHIPS, a.shape[0] // N_CHIPS, a.shape[1]))
        started[group] = (keys, _exchange_start(slabs, gather=False, name="scatter_" + group + "_start"))
        return started[group][1][4][0, 0]

    loss, dx, g = _local_step(x[0], loss_target[0], meta_full, p, late_weights, emit)

    def partial_sums(group, after):
        keys, handle = started[group]
        slabs, lands = _exchange_wait(handle, after, gather=False, name="scatter_" + group + "_wait")
        return {k: _sum_own_and_received(s, l, name="sum_chips_" + k) for k, s, l in zip(keys, slabs, lands)}

    parts = partial_sums("ffn", dx)
    parts.update(partial_sums("branch", parts["w_down"]))
    parts.update(partial_sums("input", parts["w_o"]))
    keys = list(parts)
    others = _swap_cores([parts[k] for k in keys], name="swap_cores")
    gs = {}
    for k, other in zip(keys, others):
        (gs[k],) = _rowwise(lambda a, b: (a + b,), [parts[k], other], [], [(other.shape[1], f32)],
                            name="sum_cores_" + k)
    gs.update(g2_t=gs.pop("g2"), w2_t=gs.pop("w2"), a2_t=gs.pop("a2"))

    small = jnp.concatenate([_pack_small(g), loss.reshape(1)])
    small_rows = -(-small.shape[0] // PACK_W)
    small = jnp.concatenate([small, jnp.zeros((small_rows * PACK_W - small.shape[0],), f32)]).reshape(small_rows, PACK_W)
    small_rows8 = -(-(small_rows + N_META) // 8) * 8
    reduced = _all_reduce_small(_pad_rows(jnp.concatenate([g["meta"], small], axis=0), small_rows8),
                                name="reduce_small")
    g_meta = lax.dynamic_slice_in_dim(reduced[:N_META], chip * meta_cols, meta_cols, axis=1)
    flat = reduced[N_META:N_META + small_rows].reshape(-1)
    g_small = _unpack_small(flat)
    loss_total = flat[_SMALL_TOTAL]

    small_of = dict(norm_mix_g="norm_mix_g", b_in="b_in", attn_sinks="sinks", rwkv_mix="mix", rwkv_w0="w0",
                    rwkv_a0="a0", rwkv_k_k="k_k", rwkv_k_a="k_a", rwkv_r_k="r_k", rwkv_ln_w="ln_w",
                    rwkv_ln_b="ln_b", norm_ffn_g="norm_ffn_g", norm_final_g="norm_final_g")
    grads = {"meta_tokens": g_meta}
    for n, k in small_of.items():
        grads[n] = g_small[k].reshape(w_all[n].shape)
    for k, n in t_of.items():
        grads[n] = gs[k].T.reshape(w_all[n].shape)
    for k, n in plain_of.items():
        grads[n] = gs[k].reshape(w_all[n].shape)

    big = ("w_in", "w_ffn_gate", "w_ffn_up", "w_ffn_down", "w_o", "w_br_attn", "w_br_rwkv", "rwkv_g2", "rwkv_w2",
           "rwkv_a2")
    delta, new_m, new_v = {}, {}, {}
    for n in big:
        shape2 = w_all[n].shape[1:]
        d_, m_, v_ = _adamw(w_all[n].reshape(shape2), grads[n].reshape(shape2), m_all[n].reshape(shape2),
                            v_all[n].reshape(shape2), name="adamw_" + n)
        delta[n], new_m[n], new_v[n] = (t.reshape(w_all[n].shape) for t in (d_, m_, v_))
    rest = [n for n in names if n not in big]

    def pack_rest(src):
        flat_ = jnp.concatenate([src[n].reshape(-1) for n in rest])
        rows_ = -(-flat_.shape[0] // (8 * PACK_W)) * 8
        return jnp.concatenate([flat_, jnp.ones((rows_ * PACK_W - flat_.shape[0],), f32)]).reshape(rows_, PACK_W)

    d_, m_, v_ = _adamw(pack_rest(w_all), pack_rest(grads), pack_rest(m_all), pack_rest(v_all), name="adamw_small")
    off = 0
    for n in rest:
        size = w_all[n].size
        for dst, src in ((delta, d_), (new_m, m_), (new_v, v_)):
            dst[n] = src.reshape(-1)[off:off + size].reshape(w_all[n].shape)
        off += size

    return (loss_total, dx.reshape(x.shape), *[grads[n] for n in names], *[delta[n] for n in names],
            *[new_m[n] for n in names], *[new_v[n] for n in names])
```

```python
import functools
import math

import jax
import jax.numpy as jnp
from jax import lax
from jax.experimental import pallas as pl
from jax.experimental.pallas import tpu as pltpu

f32 = jnp.float32
bf16 = jnp.bfloat16

D_MODEL = 1024
N_META = 16
HEAD_DIM = 64
Q_HEADS = 8
KV_HEADS = 2
GROUP = Q_HEADS // KV_HEADS
WINDOW = 128
BLOCK = 128
ROPE_THETA = 500000.0
ROPE_DIM = HEAD_DIM // 4
RWKV_HEADS = 8
RWKV_HEAD = 64
RWKV_DIM = RWKV_HEADS * RWKV_HEAD
DECAY_LORA = 64
AAA_LORA = 64
GATE_LORA = 160
LORA_W = DECAY_LORA + AAA_LORA + GATE_LORA
RWKV_LN_EPS = 64e-5
D_FF = 2816
Q_W = Q_HEADS * HEAD_DIM
KV_W = KV_HEADS * HEAD_DIM
ATTN_PROJ = Q_W + 2 * KV_W
RKV_W = 3 * RWKV_DIM
RWKV_PROJ = RKV_W + LORA_W
D_IN = ATTN_PROJ + RWKV_PROJ + 2 * D_MODEL
RMS_EPS = 1e-6
NEG_INF = -1e30
PAD = BLOCK - N_META
FRONT = PAD + N_META

ADAM_LR = 0.001
ADAM_B1 = 0.9
ADAM_B2 = 0.999
ADAM_EPS = 1e-08
ADAM_WD = 0.01
ADAM_STEP = 10

N_CHIPS = 4
N_DEV = 8
CHUNK = 64
VMEM_LIMIT = 56 * 1024 * 1024
PACK_W = 1024
MESH = pl.DeviceIdType.MESH
HIGHEST = lax.Precision.HIGHEST


def _tile(m, pref=384):
    for step in (16, 8):
        for t in range(min(m, pref) // step * step, 0, -step):
            if m % t == 0:
                return t
    return m


def _params(sem=None):
    return pltpu.CompilerParams(dimension_semantics=sem, vmem_limit_bytes=VMEM_LIMIT)


def _full(shape):
    nd = len(shape)
    return pl.BlockSpec(shape, lambda *_: (0,) * nd)


def _dot(a, b, dims="nn", exact=False):
    dn = {"nn": (((1,), (0,)), ((), ())), "nt": (((1,), (1,)), ((), ())), "tn": (((0,), (0,)), ((), ()))}[dims]
    if exact:
        return lax.dot_general(a.astype(f32), b.astype(f32), dn, precision=HIGHEST, preferred_element_type=f32)
    return lax.dot_general(a.astype(bf16), b.astype(bf16), dn, preferred_element_type=f32)


def _dot3(a, b):
    a_hi, b_hi = a.astype(bf16), b.astype(bf16)
    a_lo = (a - a_hi.astype(f32)).astype(bf16)
    b_lo = (b - b_hi.astype(f32)).astype(bf16)
    return _dot(a_hi, b_hi) + (_dot(a_hi, b_lo) + _dot(a_lo, b_hi))


def _two_pass(x, m):
    x_hi = x.astype(bf16)
    x_lo = (x - x_hi.astype(f32)).astype(bf16)
    return _dot(x_hi, m) + _dot(x_lo, m)


@jax.custom_vjp
def _dot_const(x, m):
    return _two_pass(x, m)


def _dot_const_fwd(x, m):
    return _two_pass(x, m), m


def _dot_const_bwd(m, ct):
    return _two_pass(ct, m.T), jnp.zeros_like(m)


_dot_const.defvjp(_dot_const_fwd, _dot_const_bwd)


def _mm(a, b, mode, *, name, out_dtype=f32, bias=None, add=None, zero_rows_below=0):
    m, _ = a.shape
    n = b.shape[1] if mode == "nn" else b.shape[0]
    tm = _tile(m)
    has_bias, has_add = bias is not None, add is not None

    def body(*refs):
        a_ref, b_ref = refs[0], refs[1]
        o_ref = refs[-1]
        acc = _dot(a_ref[...], b_ref[...], mode)
        k = 2
        if has_bias:
            acc = acc + refs[k][...]
            k += 1
        if zero_rows_below:
            rows = pl.program_id(0) * tm + lax.broadcasted_iota(jnp.int32, acc.shape, 0)
            acc = jnp.where(rows >= zero_rows_below, acc, 0.0)
        if has_add:
            acc = acc + refs[k][...].astype(f32)
        o_ref[...] = acc.astype(out_dtype)

    ins = [a, b]
    in_specs = [pl.BlockSpec((tm, a.shape[1]), lambda i: (i, 0)), _full(b.shape)]
    if has_bias:
        ins.append(bias)
        in_specs.append(_full(bias.shape))
    if has_add:
        ins.append(add)
        in_specs.append(pl.BlockSpec((tm, n), lambda i: (i, 0)))
    return pl.pallas_call(
        body, name=name, grid=(m // tm,), in_specs=in_specs,
        out_specs=pl.BlockSpec((tm, n), lambda i: (i, 0)),
        out_shape=jax.ShapeDtypeStruct((m, n), out_dtype),
        compiler_params=_params(("parallel",)),
    )(*ins)


def _mm_tn(a, b, *, name, colsum=False, out_dtype=bf16):
    r, m = a.shape
    n = b.shape[1]
    tr = _tile(r, 1408)
    tmo = m
    for cand in (1408, 1024, 768, 512):
        if m > 1024 and m % cand == 0:
            tmo = cand
            break
    steps = r // tr

    def body(a_ref, b_ref, o_ref, *rest):
        acc = rest[-1]
        i = pl.program_id(1)

        @pl.when(i == 0)
        def _():
            acc[...] = jnp.zeros_like(acc)
            if colsum:
                rest[0][...] = jnp.zeros_like(rest[0])

        acc[...] += _dot(a_ref[...], b_ref[...], "tn")
        if colsum:
            rest[0][...] += jnp.sum(a_ref[...].astype(f32), axis=0, keepdims=True)

        @pl.when(i == steps - 1)
        def _():
            o_ref[...] = acc[...].astype(out_dtype)

    out_shape = [jax.ShapeDtypeStruct((m, n), out_dtype)]
    out_specs = [pl.BlockSpec((tmo, n), lambda j, i: (j, 0))]
    if colsum:
        out_shape.append(jax.ShapeDtypeStruct((1, m), f32))
        out_specs.append(pl.BlockSpec((1, tmo), lambda j, i: (0, j)))
    res = pl.pallas_call(
        body, name=name, grid=(m // tmo, steps),
        in_specs=[pl.BlockSpec((tr, tmo), lambda j, i: (i, j)), pl.BlockSpec((tr, n), lambda j, i: (i, 0))],
        out_specs=out_specs, out_shape=out_shape,
        scratch_shapes=[pltpu.VMEM((tmo, n), f32)],
        compiler_params=_params(("parallel", "arbitrary")),
    )(a, b)
    return res if colsum else res[0]


def _rowwise(fn, rows, params, outs, *, name, tm=None, with_row0=False):
    m = rows[0].shape[0]
    tm = tm or _tile(m)
    nr, npar = len(rows), len(params)

    def body(*refs):
        vals = [r[...] for r in refs[:nr + npar]]
        kw = dict(row0=pl.program_id(0) * tm) if with_row0 else {}
        res = fn(*vals, **kw)
        for o_ref, v in zip(refs[nr + npar:], res):
            o_ref[...] = v.astype(o_ref.dtype)

    return pl.pallas_call(
        body, name=name, grid=(m // tm,),
        in_specs=[pl.BlockSpec((tm, r.shape[1]), lambda i: (i, 0)) for r in rows] + [_full(p.shape) for p in params],
        out_specs=[pl.BlockSpec((tm, w), lambda i: (i, 0)) for w, _ in outs],
        out_shape=[jax.ShapeDtypeStruct((m, w), dt) for w, dt in outs],
        compiler_params=_params(("parallel",)),
    )(*rows, *params)


def _rowwise_bwd(fn, rows, params, cts, *, name, diff_rows, diff_params, tm=None, with_row0=False, zero_rows_below=0,
                 out_dtypes=None):
    m = rows[0].shape[0]
    tm = tm or _tile(m)
    nr, npar = len(rows), len(params)
    d_idx = [i for i in range(nr) if diff_rows[i]]
    p_idx = [i for i in range(npar) if diff_params[i]]
    out_dtypes = out_dtypes or [f32] * len(d_idx)
    flat_cts = [c for group in cts for c in group]
    n_ct = len(flat_cts)

    def body(*refs):
        vals = [r[...] for r in refs[:nr + npar]]
        ct_refs = refs[nr + npar:nr + npar + n_ct]
        out_refs = refs[nr + npar + n_ct:]
        kw = dict(row0=pl.program_id(0) * tm) if with_row0 else {}
        ct_vals, k = [], 0
        for group in cts:
            acc = ct_refs[k][...].astype(f32)
            for extra in range(1, len(group)):
                acc = acc + ct_refs[k + extra][...].astype(f32)
            k += len(group)
            if zero_rows_below:
                rr = pl.program_id(0) * tm + lax.broadcasted_iota(jnp.int32, acc.shape, 0)
                acc = jnp.where(rr >= zero_rows_below, acc, 0.0)
            ct_vals.append(acc)

        def g(*dargs):
            full = list(vals)
            for pos, i in enumerate(d_idx):
                full[i] = dargs[pos]
            for pos, i in enumerate(p_idx):
                full[nr + i] = dargs[len(d_idx) + pos]
            return tuple(fn(*full, **kw))

        _, vjp = jax.vjp(g, *[vals[i].astype(f32) for i in d_idx], *[vals[nr + i] for i in p_idx])
        grads = vjp(tuple(ct_vals))
        for pos in range(len(d_idx)):
            out_refs[pos][...] = grads[pos].astype(out_refs[pos].dtype)
        first = pl.program_id(0) == 0
        for pos in range(len(p_idx)):
            o_ref = out_refs[len(d_idx) + pos]

            @pl.when(first)
            def _(o_ref=o_ref):
                o_ref[...] = jnp.zeros_like(o_ref)

            o_ref[...] += grads[len(d_idx) + pos]

    return pl.pallas_call(
        body, name=name, grid=(m // tm,),
        in_specs=[pl.BlockSpec((tm, r.shape[1]), lambda i: (i, 0)) for r in rows] + [_full(p.shape) for p in params]
        + [pl.BlockSpec((tm, c.shape[1]), lambda i: (i, 0)) for c in flat_cts],
        out_specs=[pl.BlockSpec((tm, rows[i].shape[1]), lambda i_: (i_, 0)) for i in d_idx]
        + [_full(params[i].shape) for i in p_idx],
        out_shape=[jax.ShapeDtypeStruct(rows[i].shape, dt) for i, dt in zip(d_idx, out_dtypes)]
        + [jax.ShapeDtypeStruct(params[i].shape, f32) for i in p_idx],
        compiler_params=_params(("arbitrary",)),
    )(*rows, *params, *flat_cts)


def _rms(x, g):
    return x * lax.rsqrt(jnp.mean(x * x, axis=-1, keepdims=True) + RMS_EPS) * g


def _head_sum_matrix(width, head):
    idx = jnp.arange(width) // head
    return (idx[:, None] == idx[None, :]).astype(f32)


def _rope_tables(lp):
    half = ROPE_DIM // 2
    pos = (jnp.arange(lp) - PAD).astype(f32)
    inv_freq = jnp.power(jnp.float32(ROPE_THETA), -jnp.arange(half, dtype=f32) * (2.0 / ROPE_DIM))
    ang = pos[:, None] * inv_freq[None, :]
    cos, sin = jnp.cos(ang), jnp.sin(ang)
    ones = jnp.ones((lp, HEAD_DIM - ROPE_DIM), f32)
    zeros = jnp.zeros((lp, HEAD_DIM - ROPE_DIM), f32)
    cos_t = jnp.concatenate([cos, cos, ones], axis=1)
    sin_t = jnp.concatenate([-sin, sin, zeros], axis=1)
    i = jnp.arange(HEAD_DIM)
    src = jnp.where(i < half, i + half, jnp.where(i < ROPE_DIM, i - half, i))
    swap = ((i[:, None] == src[None, :]) & (i[None, :] < ROPE_DIM)).astype(f32)
    return cos_t, sin_t, swap


def _attn_prep(qkv, cos_t, sin_t, swap):
    outs = []
    for h in range(Q_HEADS + KV_HEADS):
        t = qkv[:, h * HEAD_DIM:(h + 1) * HEAD_DIM]
        outs.append(t * cos_t + _dot_const(t, swap) * sin_t)
    q = jnp.concatenate(outs[:Q_HEADS], axis=1)
    k = jnp.concatenate(outs[Q_HEADS:], axis=1)
    return q, k, qkv[:, Q_W + KV_W:]


def _softplus(z):
    return jnp.maximum(z, 0.0) + jnp.log1p(jnp.exp(-jnp.abs(z)))


def _rwkv_prep(rkv, lora, w0, w2, a0, a2, g2, k_k, k_a, hsum):
    r = rkv[:, :RWKV_DIM]
    k = rkv[:, RWKV_DIM:2 * RWKV_DIM]
    v = rkv[:, 2 * RWKV_DIM:]
    dw = lora[:, :DECAY_LORA]
    da = lora[:, DECAY_LORA:DECAY_LORA + AAA_LORA]
    dg = lora[:, DECAY_LORA + AAA_LORA:]
    w = -_softplus(-(w0 + _dot(jnp.tanh(dw), w2))) - 0.5
    a = jax.nn.sigmoid(a0 + _dot(da, a2))
    g = _dot(jax.nn.sigmoid(dg), g2)
    kk = k * k_k
    kk = kk * lax.rsqrt(jnp.maximum(_dot_const(kk * kk, hsum), 1e-24))
    k = k * (1.0 + (a - 1.0) * k_a)
    log_decay = -jnp.exp(w)
    return r, log_decay, k, v, -kk, kk * a, g


def _rwkv_post(y, r, k, v, g, ln_w, ln_b, r_k, hmean):
    hsum = hmean * RWKV_HEAD
    mean = _dot_const(y, hmean)
    yc = y - mean
    var = _dot_const(yc * yc, hmean)
    yn = yc * lax.rsqrt(var + RWKV_LN_EPS) * ln_w + ln_b
    bonus = _dot_const(r * k * r_k, hsum) * v
    return ((yn + bonus) * g,)


def _merge(gates, br_a, br_r):
    sg = jax.nn.sigmoid(gates)
    return (sg[:, :D_MODEL] * br_a + sg[:, D_MODEL:] * br_r,)


def _swiglu(gate, up):
    return (jax.nn.silu(gate) * up,)


def _token_shift(p, mix, *, name):
    m, c = p.shape
    tm = _tile(m)
    sub = tm // 8

    def body(p_ref, prev_ref, mix_ref, o_ref):
        x = p_ref[...]
        rows = lax.broadcasted_iota(jnp.int32, x.shape, 0)
        last = jnp.where(pl.program_id(0) == 0, 0.0, prev_ref[7:8, :])
        xp = jnp.where(rows == 0, last, pltpu.roll(x, 1, axis=0))
        o_ref[...] = x + (xp - x) * mix_ref[...]

    return pl.pallas_call(
        body, name=name, grid=(m // tm,),
        in_specs=[pl.BlockSpec((tm, c), lambda i: (i, 0)),
                  pl.BlockSpec((8, c), lambda i: (jnp.maximum(i * sub - 1, 0), 0)),
                  _full(mix.shape)],
        out_specs=pl.BlockSpec((tm, c), lambda i: (i, 0)),
        out_shape=jax.ShapeDtypeStruct((m, c), f32),
        compiler_params=_params(("parallel",)),
    )(p, p, mix)


def _token_shift_bwd(p, mix, dpf, *, name):
    m, c = p.shape
    tm = _tile(m)
    sub = tm // 8
    n_tiles = m // tm

    def body(p_ref, prev_ref, mix_ref, d_ref, nxt_ref, dp_ref, dmix_ref):
        i = pl.program_id(0)
        x = p_ref[...]
        d = d_ref[...]
        mixv = mix_ref[...]
        rows = lax.broadcasted_iota(jnp.int32, x.shape, 0)
        last = jnp.where(i == 0, 0.0, prev_ref[7:8, :])
        xp = jnp.where(rows == 0, last, pltpu.roll(x, 1, axis=0))
        dm = d * mixv
        first_next = jnp.where(i == n_tiles - 1, 0.0, nxt_ref[0:1, :] * mixv)
        dm_next = jnp.where(rows == tm - 1, first_next, pltpu.roll(dm, tm - 1, axis=0))
        dp = d - dm + dm_next
        dp_ref[...] = jnp.where(i * tm + rows >= PAD, dp, 0.0).astype(dp_ref.dtype)

        @pl.when(i == 0)
        def _():
            dmix_ref[...] = jnp.zeros_like(dmix_ref)

        dmix_ref[...] += jnp.sum(d * (xp - x), axis=0, keepdims=True)

    return pl.pallas_call(
        body, name=name, grid=(n_tiles,),
        in_specs=[pl.BlockSpec((tm, c), lambda i: (i, 0)),
                  pl.BlockSpec((8, c), lambda i: (jnp.maximum(i * sub - 1, 0), 0)),
                  _full(mix.shape),
                  pl.BlockSpec((tm, c), lambda i: (i, 0)),
                  pl.BlockSpec((8, c), lambda i: (jnp.minimum((i + 1) * sub, m // 8 - 1), 0))],
        out_specs=[pl.BlockSpec((tm, c), lambda i: (i, 0)), _full(mix.shape)],
        out_shape=[jax.ShapeDtypeStruct((m, c), bf16), jax.ShapeDtypeStruct(mix.shape, f32)],
        compiler_params=_params(("arbitrary",)),
    )(p, p, mix, dpf, dpf)


def _attn_masks(blk):
    qi = lax.broadcasted_iota(jnp.int32, (BLOCK, BLOCK), 0)
    ki = lax.broadcasted_iota(jnp.int32, (BLOCK, BLOCK), 1)
    qpos = blk * BLOCK + qi - PAD
    kpos_c = blk * BLOCK + ki - PAD
    kpos_p = kpos_c - BLOCK
    kpos_m = ki - PAD

    def band(kpos):
        return (kpos >= N_META) & (kpos <= qpos) & (qpos - kpos < WINDOW)

    return band(kpos_p), band(kpos_c), (kpos_m >= 0) & (kpos_m <= qpos)


def _attn_probs(qs, k3s, sink, oks):
    s = [[jnp.where(ok, _dot(qh, kx, "nt"), NEG_INF) for kx, ok in zip(k3, oks)] for qh, k3 in zip(qs, k3s)]
    mx = [jnp.maximum(jnp.maximum(jnp.max(t[0], -1, keepdims=True), jnp.max(t[1], -1, keepdims=True)),
                      jnp.maximum(jnp.max(t[2], -1, keepdims=True), sk)) for t, sk in zip(s, sink)]
    e = [[jnp.exp(tx - m) for tx in t] for t, m in zip(s, mx)]
    e_sink = [jnp.exp(sk - m) for sk, m in zip(sink, mx)]
    inv = [1.0 / (jnp.sum(t[0], -1, keepdims=True) + jnp.sum(t[1], -1, keepdims=True)
                  + jnp.sum(t[2], -1, keepdims=True) + es) for t, es in zip(e, e_sink)]
    return [[tx * i for tx in t] for t, i in zip(e, inv)], [es * i for es, i in zip(e_sink, inv)]


def _head_cols(i):
    return slice(i * HEAD_DIM, (i + 1) * HEAD_DIM)


def _attn_operands(refs):
    q_ref, kp_ref, kc_ref, km_ref, vp_ref, vc_ref, vm_ref, s_ref = refs
    qs = [q_ref[:, _head_cols(i)] * (HEAD_DIM ** -0.5) for i in range(Q_HEADS)]
    k3 = [[ref[:, _head_cols(h)] for ref in (kp_ref, kc_ref, km_ref)] for h in range(KV_HEADS)]
    v3 = [[ref[:, _head_cols(h)] for ref in (vp_ref, vc_ref, vm_ref)] for h in range(KV_HEADS)]
    return (qs, [k3[i // GROUP] for i in range(Q_HEADS)], [v3[i // GROUP] for i in range(Q_HEADS)],
            [s_ref[:, i:i + 1] for i in range(Q_HEADS)])


def _attention(q, k, v, sinks, *, name):
    lp = q.shape[0]
    nb = lp // BLOCK
    prev = lambda i: (jnp.maximum(i - 1, 0), 0)
    cur = lambda i: (i, 0)
    meta = lambda i: (0, 0)
    kv = lambda index: pl.BlockSpec((BLOCK, KV_W), index)

    def body(*refs):
        o_ref = refs[-1]
        qs, k3s, v3s, sink = _attn_operands(refs[:-1])
        p, _ = _attn_probs(qs, k3s, sink, _attn_masks(pl.program_id(0)))
        out = [_dot(ph[0], v3[0]) + _dot(ph[1], v3[1]) + _dot(ph[2], v3[2]) for ph, v3 in zip(p, v3s)]
        for i in range(Q_HEADS):
            o_ref[:, _head_cols(i)] = out[i].astype(o_ref.dtype)

    return pl.pallas_call(
        body, name=name, grid=(nb,),
        in_specs=[pl.BlockSpec((BLOCK, Q_W), cur), kv(prev), kv(cur), kv(meta), kv(prev), kv(cur), kv(meta),
                  _full((1, Q_HEADS))],
        out_specs=pl.BlockSpec((BLOCK, Q_W), cur),
        out_shape=jax.ShapeDtypeStruct((lp, Q_W), bf16),
        compiler_params=_params(("parallel",)),
    )(q, k, k, k, v, v, v, sinks)


def _attention_bwd(q, k, v, sinks, do, *, name):
    lp = q.shape[0]
    nb = lp // BLOCK
    cur = lambda n: (jnp.minimum(n, nb - 1), 0)
    prev = lambda n: (jnp.maximum(jnp.minimum(n, nb - 1) - 1, 0), 0)
    behind = lambda n: (jnp.maximum(n - 1, 0), 0)
    meta = lambda n: (0, 0)
    kv = lambda index: pl.BlockSpec((BLOCK, KV_W), index)
    scale = HEAD_DIM ** -0.5

    def body(*refs):
        ins, do_ref = refs[:8], refs[8]
        dq_ref, dk_ref, dv_ref, dkm_ref, dvm_ref, ds_ref, carry_k, carry_v = refs[9:]
        n = pl.program_id(0)

        @pl.when(n == 0)
        def _():
            for ref in (dkm_ref, dvm_ref, ds_ref, carry_k, carry_v):
                ref[...] = jnp.zeros_like(ref)

        @pl.when(n < nb)
        def _():
            qs, k3s, v3s, sink = _attn_operands(ins)
            do = [do_ref[:, _head_cols(i)] for i in range(Q_HEADS)]
            p, p_sink = _attn_probs(qs, k3s, sink, _attn_masks(n))
            out = [_dot(ph[0], v3[0]) + _dot(ph[1], v3[1]) + _dot(ph[2], v3[2]) for ph, v3 in zip(p, v3s)]
            delta = [jnp.sum(d * o, -1, keepdims=True) for d, o in zip(do, out)]
            dp = [[_dot(d, vx, "nt") for vx in v3] for d, v3 in zip(do, v3s)]
            ds = [[px * (dx - dl) for px, dx in zip(ph, dh)] for ph, dh, dl in zip(p, dp, delta)]
            dq = [_dot(dsh[0], k3[0]) + _dot(dsh[1], k3[1]) + _dot(dsh[2], k3[2]) for dsh, k3 in zip(ds, k3s)]
            for i in range(Q_HEADS):
                dq_ref[:, _head_cols(i)] = dq[i] * scale
                ds_ref[:, i:i + 1] -= jnp.sum(p_sink[i] * delta[i], axis=0, keepdims=True)
            for h in range(KV_HEADS):
                group = slice(h * GROUP, (h + 1) * GROUP)
                q_all = jnp.concatenate(qs[group], axis=0)
                do_all = jnp.concatenate(do[group], axis=0)
                dk3 = [_dot(jnp.concatenate([dsh[x] for dsh in ds[group]], axis=0), q_all, "tn") for x in range(3)]
                dv3 = [_dot(jnp.concatenate([ph[x] for ph in p[group]], axis=0), do_all, "tn") for x in range(3)]
                hs = _head_cols(h)
                for out_ref, carry, meta_ref, d3 in ((dk_ref, carry_k, dkm_ref, dk3),
                                                     (dv_ref, carry_v, dvm_ref, dv3)):
                    out_ref[:, hs] = carry[:, hs] + d3[0]
                    carry[:, hs] = d3[1]
                    meta_ref[:, hs] += d3[2]

        @pl.when(n == nb)
        def _():
            dk_ref[...] = carry_k[...]
            dv_ref[...] = carry_v[...]

    kv_shape = jax.ShapeDtypeStruct((lp, KV_W), f32)
    one_shape = jax.ShapeDtypeStruct((BLOCK, KV_W), f32)
    return pl.pallas_call(
        body, name=name, grid=(nb + 1,),
        in_specs=[pl.BlockSpec((BLOCK, Q_W), cur), kv(prev), kv(cur), kv(meta), kv(prev), kv(cur), kv(meta),
                  _full((1, Q_HEADS)), pl.BlockSpec((BLOCK, Q_W), cur)],
        out_specs=[pl.BlockSpec((BLOCK, Q_W), cur), kv(behind), kv(behind), kv(meta), kv(meta),
                   _full((1, Q_HEADS))],
        out_shape=[jax.ShapeDtypeStruct((lp, Q_W), f32), kv_shape, kv_shape, one_shape, one_shape,
                   jax.ShapeDtypeStruct((1, Q_HEADS), f32)],
        scratch_shapes=[pltpu.VMEM((BLOCK, KV_W), f32), pltpu.VMEM((BLOCK, KV_W), f32)],
        compiler_params=_params(("arbitrary",)),
    )(q, k, k, k, v, v, v, sinks, do)


def _scan_chunk(s0, r, lw, k, v, a, b):
    t = r[0].shape[0]
    ii = lax.broadcasted_iota(jnp.int32, (t, t), 0)
    jj = lax.broadcasted_iota(jnp.int32, (t, t), 1)
    incl = jj <= ii
    strict = jj < ii
    tri = incl.astype(f32)
    eye = jnp.where(ii == jj, 1.0, 0.0)
    cl = [_dot3(tri, x) for x in lw]
    e_pos = [jnp.exp(c) for c in cl]
    e_neg = [jnp.exp(-c) for c in cl]
    e_prev = [jnp.exp(c - x) for c, x in zip(cl, lw)]
    rt = [x * e for x, e in zip(r, e_pos)]
    at = [x * e for x, e in zip(a, e_prev)]
    bt = [x * e for x, e in zip(b, e_neg)]
    kt = [x * e for x, e in zip(k, e_neg)]
    l_ab = [jnp.where(strict, _dot(x, y, "nt"), 0.0) for x, y in zip(at, bt)]
    l_ak = [jnp.where(strict, _dot(x, y, "nt"), 0.0) for x, y in zip(at, kt)]
    r_b = [jnp.where(incl, _dot(x, y, "nt"), 0.0) for x, y in zip(rt, bt)]
    r_k = [jnp.where(incl, _dot(x, y, "nt"), 0.0) for x, y in zip(rt, kt)]
    inv = [eye + x for x in l_ab]
    pw = l_ab
    for _ in range(int(math.log2(t)) - 1):
        pw = [_dot(x, x) for x in pw]
        inv = [x + _dot(x, y) for x, y in zip(inv, pw)]
    rhs = [_dot(x, s, "nt") + _dot(m, y) for x, s, m, y in zip(at, s0, l_ak, v)]
    u = [_dot(x, y) for x, y in zip(inv, rhs)]
    y_s = [_dot(x, s, "nt") for x, s in zip(rt, s0)]
    y = [ys + _dot(m, uu) + _dot(n, vv) for ys, m, uu, n, vv in zip(y_s, r_b, u, r_k, v)]
    grow = [s + _dot(uu, x, "tn") + _dot(vv, z, "tn") for s, uu, x, vv, z in zip(s0, u, bt, v, kt)]
    s1 = [g * e[t - 1:t, :] for g, e in zip(grow, e_pos)]
    return y, s1


def _head_rows(h):
    return slice(h * RWKV_HEAD, (h + 1) * RWKV_HEAD)


def _per_head(ref):
    return [ref[:, _head_rows(h)] for h in range(RWKV_HEADS)]


def _scan(r, lw, k, v, a, b, *, name):
    lp = r.shape[0]
    nc = lp // CHUNK
    row = pl.BlockSpec((CHUNK, RWKV_DIM), lambda c: (c, 0))

    def body(r_ref, lw_ref, k_ref, v_ref, a_ref, b_ref, y_ref, s_ref, state):
        @pl.when(pl.program_id(0) == 0)
        def _():
            state[...] = jnp.zeros_like(state)

        s_ref[...] = state[...]
        s0 = [state[_head_rows(h), :] for h in range(RWKV_HEADS)]
        y, s1 = _scan_chunk(s0, *[_per_head(ref) for ref in (r_ref, lw_ref, k_ref, v_ref, a_ref, b_ref)])
        for h in range(RWKV_HEADS):
            y_ref[:, _head_rows(h)] = y[h]
            state[_head_rows(h), :] = s1[h]

    return pl.pallas_call(
        body, name=name, grid=(nc,), in_specs=[row] * 6,
        out_specs=[row, pl.BlockSpec((RWKV_DIM, RWKV_HEAD), lambda c: (c, 0))],
        out_shape=[jax.ShapeDtypeStruct((lp, RWKV_DIM), f32), jax.ShapeDtypeStruct((nc * RWKV_DIM, RWKV_HEAD), f32)],
        scratch_shapes=[pltpu.VMEM((RWKV_DIM, RWKV_HEAD), f32)],
        compiler_params=_params(("arbitrary",)),
    )(r, lw, k, v, a, b)


def _scan_bwd(r, lw, k, v, a, b, states, dy, *, name):
    lp = r.shape[0]
    nc = lp // CHUNK
    row = pl.BlockSpec((CHUNK, RWKV_DIM), lambda c: (nc - 1 - c, 0))

    def body(r_ref, lw_ref, k_ref, v_ref, a_ref, b_ref, s_ref, dy_ref,
             dr_ref, dlw_ref, dk_ref, dv_ref, da_ref, db_ref, dstate):
        @pl.when(pl.program_id(0) == 0)
        def _():
            dstate[...] = jnp.zeros_like(dstate)

        outs = (dr_ref, dlw_ref, dk_ref, dv_ref, da_ref, db_ref)
        s0 = [s_ref[_head_rows(h), :] for h in range(RWKV_HEADS)]
        _, vjp = jax.vjp(_scan_chunk, s0, *[_per_head(ref) for ref in (r_ref, lw_ref, k_ref, v_ref, a_ref, b_ref)])
        g = vjp((_per_head(dy_ref), [dstate[_head_rows(h), :] for h in range(RWKV_HEADS)]))
        for h in range(RWKV_HEADS):
            dstate[_head_rows(h), :] = g[0][h]
            for o_ref, gv in zip(outs, g[1:]):
                o_ref[:, _head_rows(h)] = gv[h]

    shape = jax.ShapeDtypeStruct((lp, RWKV_DIM), f32)
    return pl.pallas_call(
        body, name=name, grid=(nc,),
        in_specs=[row] * 6 + [pl.BlockSpec((RWKV_DIM, RWKV_HEAD), lambda c: (nc - 1 - c, 0)), row],
        out_specs=[row] * 6, out_shape=[shape] * 6,
        scratch_shapes=[pltpu.VMEM((RWKV_DIM, RWKV_HEAD), f32)],
        compiler_params=_params(("arbitrary",)),
    )(r, lw, k, v, a, b, states, dy)


def _loss_head(h2, target, g_final, *, name):
    lp = h2.shape[0]
    tm = BLOCK
    front_tiles = FRONT // tm

    def body(h_ref, t_ref, g_ref, loss_ref, dh_ref, dg_ref):
        i = pl.program_id(0)
        real = i >= front_tiles

        def tile_loss(hv, gv):
            err = _rms(hv, gv) - t_ref[...]
            return jnp.where(real, 0.5 * jnp.sum(jnp.mean(err * err, axis=-1, keepdims=True)), 0.0)

        loss, (dh, dg) = jax.value_and_grad(tile_loss, argnums=(0, 1))(h_ref[...], g_ref[...])

        @pl.when(i == 0)
        def _():
            loss_ref[...] = jnp.zeros_like(loss_ref)
            dg_ref[...] = jnp.zeros_like(dg_ref)

        loss_ref[...] += jnp.full(loss_ref.shape, loss, f32)
        dg_ref[...] += dg
        dh_ref[...] = dh

    return pl.pallas_call(
        body, name=name, grid=(lp // tm,),
        in_specs=[pl.BlockSpec((tm, D_MODEL), lambda i: (i, 0)),
                  pl.BlockSpec((tm, D_MODEL), lambda i: (jnp.maximum(i - front_tiles, 0), 0)),
                  _full(g_final.shape)],
        out_specs=[_full((8, 128)), pl.BlockSpec((tm, D_MODEL), lambda i: (i, 0)), _full(g_final.shape)],
        out_shape=[jax.ShapeDtypeStruct((8, 128), f32), jax.ShapeDtypeStruct((lp, D_MODEL), f32),
                   jax.ShapeDtypeStruct(g_final.shape, f32)],
        compiler_params=_params(("arbitrary",)),
    )(h2, target, g_final)


def _local_step(x, target, meta, p, late_weights=None, emit=None):
    emit = emit or (lambda group, grads: 0.0)
    seq = x.shape[0]
    lp = seq + FRONT
    h0 = jnp.concatenate([jnp.zeros((PAD, D_MODEL), f32), meta, x], axis=0)
    cos_t, sin_t, swap = _rope_tables(lp)
    hsum = _head_sum_matrix(RWKV_DIM, RWKV_HEAD)
    hmean = hsum / RWKV_HEAD
    w_qkv_t, w_rkv_t = p["w_in_t"][:ATTN_PROJ], p["w_in_t"][ATTN_PROJ:ATTN_PROJ + RKV_W]
    w_lora_t, w_gates_t = p["w_in_t"][ATTN_PROJ + RKV_W:ATTN_PROJ + RWKV_PROJ], p["w_in_t"][ATTN_PROJ + RWKV_PROJ:]
    b_qkv, b_rkv = p["b_in"][:, :ATTN_PROJ], p["b_in"][:, ATTN_PROJ:ATTN_PROJ + RKV_W]
    b_lora, b_gates = p["b_in"][:, ATTN_PROJ + RKV_W:ATTN_PROJ + RWKV_PROJ], p["b_in"][:, ATTN_PROJ + RWKV_PROJ:]
    prep_params = [p["w0"], p["w2"], p["a0"], p["a2"], p["g2"], p["k_k"], p["k_a"], hsum]
    post_params = [p["ln_w"], p["ln_b"], p["r_k"], hmean]

    (u,) = _rowwise(lambda hv, g: (_rms(hv, g),), [h0], [p["norm_mix_g"]], [(D_MODEL, bf16)], name="norm_mix")
    qkv = _mm(u, w_qkv_t, "nt", name="proj_qkv", bias=b_qkv, zero_rows_below=PAD)
    p_rkv = _mm(u, w_rkv_t, "nt", name="proj_rkv", bias=b_rkv, zero_rows_below=PAD)
    p_lora = _mm(u, w_lora_t, "nt", name="proj_lora", bias=b_lora, zero_rows_below=PAD)
    gates = _mm(u, w_gates_t, "nt", name="proj_gates", bias=b_gates, zero_rows_below=PAD)

    q, k, v = _rowwise(_attn_prep, [qkv, cos_t, sin_t], [swap], [(Q_W, bf16), (KV_W, bf16), (KV_W, bf16)],
                       name="attn_prep")
    y_attn = _attention(q, k, v, p["sinks"], name="attention")

    mix_rkv, mix_lora = p["mix"][:, :RKV_W], p["mix"][:, RKV_W:]
    pf_rkv = _token_shift(p_rkv, mix_rkv, name="shift_rkv")
    pf_lora = _token_shift(p_lora, mix_lora, name="shift_lora")
    wide = [(RWKV_DIM, f32)] * 7
    r_, lw_, k_, v_, a_, b_, g_ = _rowwise(_rwkv_prep, [pf_rkv, pf_lora], prep_params, wide, name="rwkv_prep")
    y_scan, states = _scan(r_, lw_, k_, v_, a_, b_, name="wkv_scan")
    (y_rwkv,) = _rowwise(_rwkv_post, [y_scan, r_, k_, v_, g_], post_params, [(RWKV_DIM, bf16)], name="rwkv_post")

    if late_weights is not None:
        p = {**p, **late_weights(y_rwkv)}
    br_a = _mm(y_attn, p["w_br_attn_t"], "nt", name="branch_attn")
    br_r = _mm(y_rwkv, p["w_br_rwkv_t"], "nt", name="branch_rwkv")
    (merged,) = _rowwise(_merge, [gates, br_a, br_r], [], [(D_MODEL, bf16)], name="merge")
    h1 = _mm(merged, p["w_o"], "nn", name="out_proj", add=h0)
    (f,) = _rowwise(lambda hv, g: (_rms(hv, g),), [h1], [p["norm_ffn_g"]], [(D_MODEL, bf16)], name="norm_ffn")
    gate = _mm(f, p["w_gate_t"], "nt", name="ffn_gate")
    up = _mm(f, p["w_up_t"], "nt", name="ffn_up")
    (act,) = _rowwise(_swiglu, [gate, up], [], [(D_FF, bf16)], name="swiglu")
    h2 = _mm(act, p["w_down"], "nn", name="ffn_down", add=h1)

    loss8, dh2, d_final_g = _loss_head(h2, target, p["norm_final_g"], name="loss_head")
    dact = _mm(dh2, p["w_down"], "nt", name="d_act")
    d_w_down = _mm_tn(act, dh2, name="dw_down")
    dgate, dup = _rowwise_bwd(_swiglu, [gate, up], [], [[dact]], name="swiglu_bwd",
                              diff_rows=[True, True], diff_params=[], out_dtypes=[bf16, bf16])
    d_w_gate_t = _mm_tn(dgate, f, name="dw_gate")
    d_w_up_t = _mm_tn(dup, f, name="dw_up")
    zero = emit("ffn", dict(w_down=d_w_down, w_gate_t=d_w_gate_t, w_up_t=d_w_up_t))
    df = _mm(dgate, p["w_gate_t"], "nn", name="d_f_gate")
    df = _mm(dup, p["w_up_t"], "nn", name="d_f_up", add=df)
    dh1, d_ffn_g = _rowwise_bwd(lambda hv, g: (_rms(hv, g), hv), [h1], [p["norm_ffn_g"] + zero], [[df], [dh2]],
                                name="norm_ffn_bwd", diff_rows=[True], diff_params=[True])
    dmerged = _mm(dh1, p["w_o"], "nt", name="d_merged")
    d_w_o = _mm_tn(merged, dh1, name="dw_o")
    dgates, dbr_a, dbr_r = _rowwise_bwd(_merge, [gates, br_a, br_r], [], [[dmerged]], name="merge_bwd",
                                        diff_rows=[True, True, True], diff_params=[], out_dtypes=[bf16] * 3)
    d_w_br_attn_t = _mm_tn(dbr_a, y_attn, name="dw_br_attn")
    d_w_br_rwkv_t = _mm_tn(dbr_r, y_rwkv, name="dw_br_rwkv")
    zero = emit("branch", dict(w_o=d_w_o, w_br_attn_t=d_w_br_attn_t, w_br_rwkv_t=d_w_br_rwkv_t))
    dy_attn = _mm(dbr_a, p["w_br_attn_t"], "nn", name="d_y_attn")
    dy_rwkv = _mm(dbr_r, p["w_br_rwkv_t"], "nn", name="d_y_rwkv")

    post_params = [p["ln_w"] + zero, p["ln_b"], p["r_k"], hmean]
    res = _rowwise_bwd(_rwkv_post, [y_scan, r_, k_, v_, g_], post_params, [[dy_rwkv]], name="rwkv_post_bwd",
                       diff_rows=[True] * 5, diff_params=[True, True, True, False])
    dy_scan, dr_p, dk_p, dv_p, dg_p, d_ln_w, d_ln_b, d_r_k = res
    dr_s, dlw_s, dk_s, dv_s, da_s, db_s = _scan_bwd(r_, lw_, k_, v_, a_, b_, states, dy_scan, name="wkv_scan_bwd")
    res = _rowwise_bwd(_rwkv_prep, [pf_rkv, pf_lora], prep_params,
                       [[dr_s, dr_p], [dlw_s], [dk_s, dk_p], [dv_s, dv_p], [da_s], [db_s], [dg_p]],
                       name="rwkv_prep_bwd", diff_rows=[True, True], diff_params=[True] * 7 + [False],
                       zero_rows_below=PAD)
    dpf_rkv, dpf_lora, d_w0, d_w2, d_a0, d_a2, d_g2, d_k_k, d_k_a = res
    dp_rkv, d_mix_rkv = _token_shift_bwd(p_rkv, mix_rkv, dpf_rkv, name="shift_rkv_bwd")
    dp_lora, d_mix_lora = _token_shift_bwd(p_lora, mix_lora, dpf_lora, name="shift_lora_bwd")

    dq, dk, dv, dkm, dvm, d_sinks = _attention_bwd(q, k, v, p["sinks"], dy_attn, name="attention_bwd")
    rest = jnp.zeros((lp - BLOCK, KV_W), f32)
    dkm, dvm = jnp.concatenate([dkm, rest], axis=0), jnp.concatenate([dvm, rest], axis=0)
    (dqkv,) = _rowwise_bwd(_attn_prep, [qkv, cos_t, sin_t], [swap], [[dq], [dk, dkm], [dv, dvm]], name="attn_prep_bwd",
                           diff_rows=[True, False, False], diff_params=[False], out_dtypes=[bf16])

    d_w_qkv_t, db_qkv = _mm_tn(dqkv, u, name="dw_qkv", colsum=True)
    d_w_rkv_t, db_rkv = _mm_tn(dp_rkv, u, name="dw_rkv", colsum=True)
    d_w_lora_t, db_lora = _mm_tn(dp_lora, u, name="dw_lora", colsum=True)
    d_w_gates_t, db_gates = _mm_tn(dgates, u, name="dw_gates", colsum=True)
    d_w_in_t = jnp.concatenate([d_w_qkv_t, d_w_rkv_t, d_w_lora_t, d_w_gates_t], axis=0)
    zero = emit("input", dict(w_in_t=d_w_in_t, g2=d_g2, w2=d_w2, a2=d_a2))
    du = _mm(dqkv, w_qkv_t, "nn", name="d_u_qkv")
    du = _mm(dp_rkv, w_rkv_t, "nn", name="d_u_rkv", add=du)
    du = _mm(dp_lora, w_lora_t, "nn", name="d_u_lora", add=du)
    du = _mm(dgates, w_gates_t, "nn", name="d_u_gates", add=du)
    dh0, d_mix_g = _rowwise_bwd(lambda hv, g: (_rms(hv, g), hv), [h0], [p["norm_mix_g"] + zero], [[du], [dh1]],
                                name="norm_mix_bwd", diff_rows=[True], diff_params=[True])

    grads = dict(
        w_in_t=d_w_in_t,
        b_in=jnp.concatenate([db_qkv, db_rkv, db_lora, db_gates], axis=1),
        mix=jnp.concatenate([d_mix_rkv, d_mix_lora], axis=1),
        norm_mix_g=d_mix_g, sinks=d_sinks, w0=d_w0, w2=d_w2, a0=d_a0, a2=d_a2, g2=d_g2, k_k=d_k_k, k_a=d_k_a,
        r_k=d_r_k, ln_w=d_ln_w, ln_b=d_ln_b, w_br_attn_t=d_w_br_attn_t, w_br_rwkv_t=d_w_br_rwkv_t, w_o=d_w_o,
        norm_ffn_g=d_ffn_g, w_gate_t=d_w_gate_t, w_up_t=d_w_up_t, w_down=d_w_down, norm_final_g=d_final_g,
        meta=dh0[PAD:FRONT],
    )
    return loss8[0, 0], dh0[FRONT:], grads


def _position():
    return lax.axis_index("x"), lax.axis_index("y"), lax.axis_index("c")


def _other_chips(x, y):
    return [(1 - x, y), (x, 1 - y), (1 - x, 1 - y)]


_HBM = pl.BlockSpec(memory_space=pltpu.HBM)
_SEM = pl.BlockSpec(memory_space=pltpu.SEMAPHORE)
_EFFECT = pltpu.SideEffectType.DATAFLOW_SIDE_EFFECTING


def _chip_copies(src_refs, land_refs, send_sems, recv_sems, gather):
    x, y, c = _position()
    copies = []
    for a, (src, land) in enumerate(zip(src_refs, land_refs)):
        for j, (px, py) in enumerate(_other_chips(x, y)):
            copies.append(pltpu.make_async_remote_copy(
                src_ref=src if gather else src.at[2 * px + py],
                dst_ref=land.at[2 * x + y] if gather else land.at[j],
                send_sem=send_sems.at[3 * a + j], recv_sem=recv_sems.at[3 * a + j],
                device_id=(px, py, c), device_id_type=MESH))
    return copies


def _exchange_start(srcs, *, gather, name):
    n = len(srcs)
    lands = [lax.empty((N_CHIPS,) + s.shape if gather else (3,) + s.shape[1:], s.dtype) for s in srcs]

    def body(*refs):
        for cp in _chip_copies(refs[:n], refs[n:2 * n], refs[2 * n], refs[2 * n + 1], gather):
            cp.start()
        refs[-1][...] = jnp.zeros_like(refs[-1])

    res = pl.pallas_call(
        body, name=name,
        out_shape=(pltpu.SemaphoreType.DMA((3 * n,)), pltpu.SemaphoreType.DMA((3 * n,)),
                   *[pltpu.HBM(a.shape, a.dtype) for a in srcs + lands], jax.ShapeDtypeStruct((8, 128), f32)),
        in_specs=[_HBM] * (2 * n),
        out_specs=(_SEM, _SEM, *[_HBM] * (2 * n), pl.BlockSpec(memory_space=pltpu.VMEM)),
        input_output_aliases={i: 2 + i for i in range(2 * n)},
        compiler_params=pltpu.CompilerParams(has_side_effects=_EFFECT),
    )(*[pltpu.with_memory_space_constraint(a, pltpu.HBM) for a in srcs + lands])
    return res[0], res[1], list(res[2:2 + n]), list(res[2 + n:2 + 2 * n]), res[-1]


def _exchange_wait(handle, after, *, gather, name):
    send_sems, recv_sems, srcs, lands, _ = handle
    n = len(srcs)

    def body(*refs):
        for cp in _chip_copies(refs[:n], refs[n:2 * n], refs[2 * n], refs[2 * n + 1], gather):
            cp.wait_send()
            cp.wait_recv()

    res = pl.pallas_call(
        body, name=name,
        out_shape=tuple(pltpu.HBM(a.shape, a.dtype) for a in srcs + lands),
        in_specs=[_HBM] * (2 * n) + [_SEM, _SEM, pl.BlockSpec(memory_space=pl.ANY)],
        out_specs=tuple([_HBM] * (2 * n)),
        input_output_aliases={i: i for i in range(2 * n)},
        compiler_params=pltpu.CompilerParams(has_side_effects=_EFFECT),
    )(*srcs, *lands, send_sems, recv_sems, after)
    return list(res[:n]), list(res[n:])


def _sum_own_and_received(g, recv, *, name):
    _, r, w = g.shape
    tm = _tile(r)
    if g.dtype == bf16 and tm % 16:
        tm = r
    x, y, _ = _position()
    me = jnp.reshape(2 * x + y, (1,)).astype(jnp.int32)

    def body(me_ref, g_ref, r_ref, o_ref):
        o_ref[...] = (g_ref[0].astype(f32) + r_ref[0].astype(f32)) + (r_ref[1].astype(f32) + r_ref[2].astype(f32))

    return pl.pallas_call(
        body, name=name,
        grid_spec=pltpu.PrefetchScalarGridSpec(
            num_scalar_prefetch=1, grid=(r // tm,),
            in_specs=[pl.BlockSpec((1, tm, w), lambda i, me_ref: (me_ref[0], i, 0)),
                      pl.BlockSpec((3, tm, w), lambda i, me_ref: (0, i, 0))],
            out_specs=pl.BlockSpec((tm, w), lambda i, me_ref: (i, 0))),
        out_shape=jax.ShapeDtypeStruct((r, w), f32),
        compiler_params=_params(("parallel",)),
    )(me, g, recv)


def _swap_cores(arrs, *, name):
    n = len(arrs)

    def body(*refs):
        x, y, c = _position()
        copies = [pltpu.make_async_remote_copy(
            src_ref=refs[i], dst_ref=refs[n + i], send_sem=refs[2 * n].at[i], recv_sem=refs[2 * n + 1].at[i],
            device_id=(x, y, 1 - c), device_id_type=MESH) for i in range(n)]
        for cp in copies:
            cp.start()
        for cp in copies:
            cp.wait_recv()
        for cp in copies:
            cp.wait_send()

    return pl.pallas_call(
        body, name=name,
        in_specs=[pl.BlockSpec(memory_space=pl.ANY)] * n,
        out_specs=[pl.BlockSpec(memory_space=pl.ANY)] * n,
        out_shape=[jax.ShapeDtypeStruct(a.shape, a.dtype) for a in arrs],
        scratch_shapes=[pltpu.SemaphoreType.DMA((n,)), pltpu.SemaphoreType.DMA((n,))],
    )(*arrs)


def _all_reduce_small(a, *, name):
    rows, w = a.shape

    def body(a_ref, o_ref, buf, send_sems, recv_sems):
        x, y, c = _position()
        me = 4 * x + 2 * y + c
        buf[0] = a_ref[...]
        sends = []
        for rel in range(1, N_DEV):
            peer = ((1 - x) if rel & 4 else x, (1 - y) if rel & 2 else y, (1 - c) if rel & 1 else c)
            cp = pltpu.make_async_remote_copy(
                src_ref=a_ref, dst_ref=buf.at[rel], send_sem=send_sems.at[rel - 1], recv_sem=recv_sems.at[rel - 1],
                device_id=peer, device_id_type=MESH)
            cp.start()
            sends.append(cp)
        for cp in sends:
            cp.wait_recv()
        for cp in sends:
            cp.wait_send()
        acc = buf[jnp.bitwise_xor(me, 0)]
        for d in range(1, N_DEV):
            acc = acc + buf[jnp.bitwise_xor(me, d)]
        o_ref[...] = acc

    return pl.pallas_call(
        body, name=name,
        in_specs=[pl.BlockSpec(memory_space=pltpu.VMEM)],
        out_specs=pl.BlockSpec(memory_space=pltpu.VMEM),
        out_shape=jax.ShapeDtypeStruct((rows, w), f32),
        scratch_shapes=[pltpu.VMEM((N_DEV, rows, w), f32), pltpu.SemaphoreType.DMA((N_DEV - 1,)),
                        pltpu.SemaphoreType.DMA((N_DEV - 1,))],
    )(a)


def _adamw(w, g, m, v, *, name):
    rows, cols = w.shape
    tm = _tile(rows, 256)

    def body(w_ref, g_ref, m_ref, v_ref, d_ref, nm_ref, nv_ref):
        gv = g_ref[...]
        nm = ADAM_B1 * m_ref[...] + (1.0 - ADAM_B1) * gv
        nv = ADAM_B2 * v_ref[...] + (1.0 - ADAM_B2) * (gv * gv)
        m_hat = nm / (1.0 - ADAM_B1 ** ADAM_STEP)
        v_hat = nv / (1.0 - ADAM_B2 ** ADAM_STEP)
        d_ref[...] = -ADAM_LR * (m_hat / (jnp.sqrt(v_hat) + ADAM_EPS) + ADAM_WD * w_ref[...])
        nm_ref[...] = nm
        nv_ref[...] = nv

    spec = pl.BlockSpec((tm, cols), lambda i: (i, 0))
    shape = jax.ShapeDtypeStruct((rows, cols), f32)
    return pl.pallas_call(
        body, name=name, grid=(rows // tm,), in_specs=[spec] * 4, out_specs=[spec] * 3, out_shape=[shape] * 3,
        compiler_params=_params(("parallel",)),
    )(w, g, m, v)


def _pad_rows(a, rows):
    return jnp.concatenate([a, jnp.zeros((rows - a.shape[0], a.shape[1]), a.dtype)], axis=0) if rows > a.shape[0] else a


_SMALL = (("norm_mix_g", D_MODEL), ("b_in", D_IN), ("sinks", Q_HEADS), ("mix", RWKV_PROJ), ("w0", RWKV_DIM),
          ("a0", RWKV_DIM), ("k_k", RWKV_DIM), ("k_a", RWKV_DIM), ("r_k", RWKV_DIM), ("ln_w", RWKV_DIM),
          ("ln_b", RWKV_DIM), ("norm_ffn_g", D_MODEL), ("norm_final_g", D_MODEL))


def _pack_small(d):
    flat = jnp.concatenate([d[n].reshape(-1).astype(f32) for n, _ in _SMALL])
    return flat


def _unpack_small(flat):
    out, off = {}, 0
    for n, size in _SMALL:
        out[n] = flat[off:off + size]
        off += size
    return out


_SMALL_TOTAL = sum(s for _, s in _SMALL)


def kernel(x, meta_tokens, norm_mix_g, w_in, b_in, attn_sinks, rwkv_mix, rwkv_w0, rwkv_w2, rwkv_a0, rwkv_a2, rwkv_g2, rwkv_k_k, rwkv_k_a, rwkv_r_k, rwkv_ln_w, rwkv_ln_b, w_br_attn, w_br_rwkv, w_o, norm_ffn_g, w_ffn_gate, w_ffn_up, w_ffn_down, norm_final_g, loss_target, m_meta_tokens, m_norm_mix_g, m_w_in, m_b_in, m_attn_sinks, m_rwkv_mix, m_rwkv_w0, m_rwkv_w2, m_rwkv_a0, m_rwkv_a2, m_rwkv_g2, m_rwkv_k_k, m_rwkv_k_a, m_rwkv_r_k, m_rwkv_ln_w, m_rwkv_ln_b, m_w_br_attn, m_w_br_rwkv, m_w_o, m_norm_ffn_g, m_w_ffn_gate, m_w_ffn_up, m_w_ffn_down, m_norm_final_g, v_meta_tokens, v_norm_mix_g, v_w_in, v_b_in, v_attn_sinks, v_rwkv_mix, v_rwkv_w0, v_rwkv_w2, v_rwkv_a0, v_rwkv_a2, v_rwkv_g2, v_rwkv_k_k, v_rwkv_k_a, v_rwkv_r_k, v_rwkv_ln_w, v_rwkv_ln_b, v_w_br_attn, v_w_br_rwkv, v_w_o, v_norm_ffn_g, v_w_ffn_gate, v_w_ffn_up, v_w_ffn_down, v_norm_final_g):
    names = ("meta_tokens", "norm_mix_g", "w_in", "b_in", "attn_sinks", "rwkv_mix", "rwkv_w0", "rwkv_w2", "rwkv_a0",
             "rwkv_a2", "rwkv_g2", "rwkv_k_k", "rwkv_k_a", "rwkv_r_k", "rwkv_ln_w", "rwkv_ln_b", "w_br_attn",
             "w_br_rwkv", "w_o", "norm_ffn_g", "w_ffn_gate", "w_ffn_up", "w_ffn_down", "norm_final_g")
    w_all = dict(zip(names, (meta_tokens, norm_mix_g, w_in, b_in, attn_sinks, rwkv_mix, rwkv_w0, rwkv_w2, rwkv_a0,
                             rwkv_a2, rwkv_g2, rwkv_k_k, rwkv_k_a, rwkv_r_k, rwkv_ln_w, rwkv_ln_b, w_br_attn,
                             w_br_rwkv, w_o, norm_ffn_g, w_ffn_gate, w_ffn_up, w_ffn_down, norm_final_g)))
    m_all = dict(zip(names, (m_meta_tokens, m_norm_mix_g, m_w_in, m_b_in, m_attn_sinks, m_rwkv_mix, m_rwkv_w0,
                             m_rwkv_w2, m_rwkv_a0, m_rwkv_a2, m_rwkv_g2, m_rwkv_k_k, m_rwkv_k_a, m_rwkv_r_k,
                             m_rwkv_ln_w, m_rwkv_ln_b, m_w_br_attn, m_w_br_rwkv, m_w_o, m_norm_ffn_g, m_w_ffn_gate,
                             m_w_ffn_up, m_w_ffn_down, m_norm_final_g)))
    v_all = dict(zip(names, (v_meta_tokens, v_norm_mix_g, v_w_in, v_b_in, v_attn_sinks, v_rwkv_mix, v_rwkv_w0,
                             v_rwkv_w2, v_rwkv_a0, v_rwkv_a2, v_rwkv_g2, v_rwkv_k_k, v_rwkv_k_a, v_rwkv_r_k,
                             v_rwkv_ln_w, v_rwkv_ln_b, v_w_br_attn, v_w_br_rwkv, v_w_o, v_norm_ffn_g, v_w_ffn_gate,
                             v_w_ffn_up, v_w_ffn_down, v_norm_final_g)))
    cx, cy, _ = _position()
    chip = 2 * cx + cy

    t_of = dict(w_in_t="w_in", w_gate_t="w_ffn_gate", w_up_t="w_ffn_up", w_br_attn_t="w_br_attn",
                w_br_rwkv_t="w_br_rwkv", g2_t="rwkv_g2", w2_t="rwkv_w2", a2_t="rwkv_a2")
    plain_of = dict(w_down="w_ffn_down", w_o="w_o")
    meta_cols = meta_tokens.shape[1]

    def shard(k):
        return (w_all[t_of[k]][0].T if k in t_of else w_all[plain_of[k]][0]).astype(bf16)

    def whole(zone, own):
        return lax.dynamic_update_slice_in_dim(zone, own[None], chip, axis=0).reshape(-1, own.shape[-1])

    early = ("w_in_t", "g2_t", "w2_t", "a2_t")
    late = ("w_gate_t", "w_up_t", "w_down", "w_o", "w_br_attn_t", "w_br_rwkv_t")
    early_h = _exchange_start([shard(k) for k in early] + [meta_tokens], gather=True, name="gather_early_start")
    late_h = _exchange_start([shard(k) for k in late], gather=True, name="gather_late_start")
    own, zones = _exchange_wait(early_h, late_h[4], gather=True, name="gather_early_wait")
    got = {k: whole(z, o) for k, z, o in zip(early, zones, own)}
    meta_full = whole(zones[-1], own[-1]).reshape(N_CHIPS, N_META, meta_cols).transpose(1, 0, 2).reshape(N_META, -1)
    p = dict(
        w_in_t=got["w_in_t"], g2=got["g2_t"].T.astype(f32), w2=got["w2_t"].T.astype(f32),
        a2=got["a2_t"].T.astype(f32),
        b_in=b_in, sinks=attn_sinks, mix=rwkv_mix, w0=rwkv_w0, a0=rwkv_a0, k_k=rwkv_k_k, k_a=rwkv_k_a,
        r_k=rwkv_r_k.reshape(1, RWKV_DIM), ln_w=rwkv_ln_w, ln_b=rwkv_ln_b, norm_mix_g=norm_mix_g,
        norm_ffn_g=norm_ffn_g, norm_final_g=norm_final_g.reshape(1, D_MODEL),
    )

    def late_weights(after):
        own_l, zones_l = _exchange_wait(late_h, after, gather=True, name="gather_late_wait")
        return {k: whole(z, o) for k, z, o in zip(late, zones_l, own_l)}

    started = {}

    def emit(group, grads_):
        keys = list(grads_)
        slabs = []
        for k in keys:
            a = grads_[k].T if k in ("g2", "w2", "a2") else grads_[k]
            slabs.append(a.reshape(N_CHIPS, a.shape[0] // N_CHIPS, a.shape[1]))
        started[group] = (keys, _exchange_start(slabs, gather=False, name="scatter_" + group + "_start"))
        return started[group][1][4][0, 0]

    loss, dx, g = _local_step(x[0], loss_target[0], meta_full, p, late_weights, emit)

    def partial_sums(group, after):
        keys, handle = started[group]
        slabs, lands = _exchange_wait(handle, after, gather=False, name="scatter_" + group + "_wait")
        return {k: _sum_own_and_received(s, l, name="sum_chips_" + k) for k, s, l in zip(keys, slabs, lands)}

    parts = partial_sums("ffn", dx)
    parts.update(partial_sums("branch", parts["w_down"]))
    parts.update(partial_sums("input", parts["w_o"]))
    keys = list(parts)
    others = _swap_cores([parts[k] for k in keys], name="swap_cores")
    gs = {}
    for k, other in zip(keys, others):
        (gs[k],) = _rowwise(lambda a, b: (a + b,), [parts[k], other], [], [(other.shape[1], f32)],
                            name="sum_cores_" + k)
    gs.update(g2_t=gs.pop("g2"), w2_t=gs.pop("w2"), a2_t=gs.pop("a2"))

    small = jnp.concatenate([_pack_small(g), loss.reshape(1)])
    small_rows = -(-small.shape[0] // PACK_W)
    small = jnp.concatenate([small, jnp.zeros((small_rows * PACK_W - small.shape[0],), f32)]).reshape(small_rows, PACK_W)
    small_rows8 = -(-(small_rows + N_META) // 8) * 8
    reduced = _all_reduce_small(_pad_rows(jnp.concatenate([g["meta"], small], axis=0), small_rows8),
                                name="reduce_small")
    g_meta = lax.dynamic_slice_in_dim(reduced[:N_META], chip * meta_cols, meta_cols, axis=1)
    flat = reduced[N_META:N_META + small_rows].reshape(-1)
    g_small = _unpack_small(flat)
    loss_total = flat[_SMALL_TOTAL]

    small_of = dict(norm_mix_g="norm_mix_g", b_in="b_in", attn_sinks="sinks", rwkv_mix="mix", rwkv_w0="w0",
                    rwkv_a0="a0", rwkv_k_k="k_k", rwkv_k_a="k_a", rwkv_r_k="r_k", rwkv_ln_w="ln_w",
                    rwkv_ln_b="ln_b", norm_ffn_g="norm_ffn_g", norm_final_g="norm_final_g")
    grads = {"meta_tokens": g_meta}
    for n, k in small_of.items():
        grads[n] = g_small[k].reshape(w_all[n].shape)
    for k, n in t_of.items():
        grads[n] = gs[k].T.reshape(w_all[n].shape)
    for k, n in plain_of.items():
        grads[n] = gs[k].reshape(w_all[n].shape)

    big = ("w_in", "w_ffn_gate", "w_ffn_up", "w_ffn_down", "w_o", "w_br_attn", "w_br_rwkv", "rwkv_g2", "rwkv_w2",
           "rwkv_a2")
    delta, new_m, new_v = {}, {}, {}
    for n in big:
        shape2 = w_all[n].shape[1:]
        d_, m_, v_ = _adamw(w_all[n].reshape(shape2), grads[n].reshape(shape2), m_all[n].reshape(shape2),
                            v_all[n].reshape(shape2), name="adamw_" + n)
        delta[n], new_m[n], new_v[n] = (t.reshape(w_all[n].shape) for t in (d_, m_, v_))
    rest = [n for n in names if n not in big]

    def pack_rest(src):
        flat_ = jnp.concatenate([src[n].reshape(-1) for n in rest])
        rows_ = -(-flat_.shape[0] // (8 * PACK_W)) * 8
        return jnp.concatenate([flat_, jnp.ones((rows_ * PACK_W - flat_.shape[0],), f32)]).reshape(rows_, PACK_W)

    d_, m_, v_ = _adamw(pack_rest(w_all), pack_rest(grads), pack_rest(m_all), pack_rest(v_all), name="adamw_small")
    off = 0
    for n in rest:
        size = w_all[n].size
        for dst, src in ((delta, d_), (new_m, m_), (new_v, v_)):
            dst[n] = src.reshape(-1)[off:off + size].reshape(w_all[n].shape)
        off += size

    return (loss_total, dx.reshape(x.shape), *[grads[n] for n in names], *[delta[n] for n in names],
            *[new_m[n] for n in names], *[new_v[n] for n in names])
```

```python
import functools
import math

import jax
import jax.numpy as jnp
from jax import lax
from jax.experimental import pallas as pl
from jax.experimental.pallas import tpu as pltpu

f32 = jnp.float32
bf16 = jnp.bfloat16

D_MODEL = 1024
N_META = 16
HEAD_DIM = 64
Q_HEADS = 8
KV_HEADS = 2
GROUP = Q_HEADS // KV_HEADS
WINDOW = 128
BLOCK = 128
ROPE_THETA = 500000.0
ROPE_DIM = HEAD_DIM // 4
RWKV_HEADS = 8
RWKV_HEAD = 64
RWKV_DIM = RWKV_HEADS * RWKV_HEAD
DECAY_LORA = 64
AAA_LORA = 64
GATE_LORA = 160
LORA_W = DECAY_LORA + AAA_LORA + GATE_LORA
RWKV_LN_EPS = 64e-5
D_FF = 2816
Q_W = Q_HEADS * HEAD_DIM
KV_W = KV_HEADS * HEAD_DIM
ATTN_PROJ = Q_W + 2 * KV_W
RKV_W = 3 * RWKV_DIM
RWKV_PROJ = RKV_W + LORA_W
D_IN = ATTN_PROJ + RWKV_PROJ + 2 * D_MODEL
RMS_EPS = 1e-6
NEG_INF = -1e30
PAD = BLOCK - N_META
FRONT = PAD + N_META

ADAM_LR = 0.001
ADAM_B1 = 0.9
ADAM_B2 = 0.999
ADAM_EPS = 1e-08
ADAM_WD = 0.01
ADAM_STEP = 10

N_CHIPS = 4
N_DEV = 8
CHUNK = 64
VMEM_LIMIT = 56 * 1024 * 1024
PACK_W = 1024
MESH = pl.DeviceIdType.MESH
HIGHEST = lax.Precision.HIGHEST


def _tile(m, pref=384):
    for step in (16, 8):
        for t in range(min(m, pref) // step * step, 0, -step):
            if m % t == 0:
                return t
    return m


def _params(sem=None):
    return pltpu.CompilerParams(dimension_semantics=sem, vmem_limit_bytes=VMEM_LIMIT)


def _full(shape):
    nd = len(shape)
    return pl.BlockSpec(shape, lambda *_: (0,) * nd)


def _dot(a, b, dims="nn", exact=False):
    dn = {"nn": (((1,), (0,)), ((), ())), "nt": (((1,), (1,)), ((), ())), "tn": (((0,), (0,)), ((), ()))}[dims]
    if exact:
        return lax.dot_general(a.astype(f32), b.astype(f32), dn, precision=HIGHEST, preferred_element_type=f32)
    return lax.dot_general(a.astype(bf16), b.astype(bf16), dn, preferred_element_type=f32)


def _dot3(a, b):
    a_hi, b_hi = a.astype(bf16), b.astype(bf16)
    a_lo = (a - a_hi.astype(f32)).astype(bf16)
    b_lo = (b - b_hi.astype(f32)).astype(bf16)
    return _dot(a_hi, b_hi) + (_dot(a_hi, b_lo) + _dot(a_lo, b_hi))


def _two_pass(x, m):
    x_hi = x.astype(bf16)
    x_lo = (x - x_hi.astype(f32)).astype(bf16)
    return _dot(x_hi, m) + _dot(x_lo, m)


@jax.custom_vjp
def _dot_const(x, m):
    return _two_pass(x, m)


def _dot_const_fwd(x, m):
    return _two_pass(x, m), m


def _dot_const_bwd(m, ct):
    return _two_pass(ct, m.T), jnp.zeros_like(m)


_dot_const.defvjp(_dot_const_fwd, _dot_const_bwd)


def _mm(a, b, mode, *, name, out_dtype=f32, bias=None, add=None, zero_rows_below=0):
    m, _ = a.shape
    n = b.shape[1] if mode == "nn" else b.shape[0]
    tm = _tile(m)
    has_bias, has_add = bias is not None, add is not None

    def body(*refs):
        a_ref, b_ref = refs[0], refs[1]
        o_ref = refs[-1]
        acc = _dot(a_ref[...], b_ref[...], mode)
        k = 2
        if has_bias:
            acc = acc + refs[k][...]
            k += 1
        if zero_rows_below:
            rows = pl.program_id(0) * tm + lax.broadcasted_iota(jnp.int32, acc.shape, 0)
            acc = jnp.where(rows >= zero_rows_below, acc, 0.0)
        if has_add:
            acc = acc + refs[k][...].astype(f32)
        o_ref[...] = acc.astype(out_dtype)

    ins = [a, b]
    in_specs = [pl.BlockSpec((tm, a.shape[1]), lambda i: (i, 0)), _full(b.shape)]
    if has_bias:
        ins.append(bias)
        in_specs.append(_full(bias.shape))
    if has_add:
        ins.append(add)
        in_specs.append(pl.BlockSpec((tm, n), lambda i: (i, 0)))
    return pl.pallas_call(
        body, name=name, grid=(m // tm,), in_specs=in_specs,
        out_specs=pl.BlockSpec((tm, n), lambda i: (i, 0)),
        out_shape=jax.ShapeDtypeStruct((m, n), out_dtype),
        compiler_params=_params(("parallel",)),
    )(*ins)


def _mm_tn(a, b, *, name, colsum=False, out_dtype=bf16):
    r, m = a.shape
    n = b.shape[1]
    tr = _tile(r, 1408)
    tmo = m
    for cand in (1408, 1024, 768, 512):
        if m > 1024 and m % cand == 0:
            tmo = cand
            break
    steps = r // tr

    def body(a_ref, b_ref, o_ref, *rest):
        acc = rest[-1]
        i = pl.program_id(1)

        @pl.when(i == 0)
        def _():
            acc[...] = jnp.zeros_like(acc)
            if colsum:
                rest[0][...] = jnp.zeros_like(rest[0])

        acc[...] += _dot(a_ref[...], b_ref[...], "tn")
        if colsum:
            rest[0][...] += jnp.sum(a_ref[...].astype(f32), axis=0, keepdims=True)

        @pl.when(i == steps - 1)
        def _():
            o_ref[...] = acc[...].astype(out_dtype)

    out_shape = [jax.ShapeDtypeStruct((m, n), out_dtype)]
    out_specs = [pl.BlockSpec((tmo, n), lambda j, i: (j, 0))]
    if colsum:
        out_shape.append(jax.ShapeDtypeStruct((1, m), f32))
        out_specs.append(pl.BlockSpec((1, tmo), lambda j, i: (0, j)))
    res = pl.pallas_call(
        body, name=name, grid=(m // tmo, steps),
        in_specs=[pl.BlockSpec((tr, tmo), lambda j, i: (i, j)), pl.BlockSpec((tr, n), lambda j, i: (i, 0))],
        out_specs=out_specs, out_shape=out_shape,
        scratch_shapes=[pltpu.VMEM((tmo, n), f32)],
        compiler_params=_params(("parallel", "arbitrary")),
    )(a, b)
    return res if colsum else res[0]


def _rowwise(fn, rows, params, outs, *, name, tm=None, with_row0=False):
    m = rows[0].shape[0]
    tm = tm or _tile(m)
    nr, npar = len(rows), len(params)

    def body(*refs):
        vals = [r[...] for r in refs[:nr + npar]]
        kw = dict(row0=pl.program_id(0) * tm) if with_row0 else {}
        res = fn(*vals, **kw)
        for o_ref, v in zip(refs[nr + npar:], res):
            o_ref[...] = v.astype(o_ref.dtype)

    return pl.pallas_call(
        body, name=name, grid=(m // tm,),
        in_specs=[pl.BlockSpec((tm, r.shape[1]), lambda i: (i, 0)) for r in rows] + [_full(p.shape) for p in params],
        out_specs=[pl.BlockSpec((tm, w), lambda i: (i, 0)) for w, _ in outs],
        out_shape=[jax.ShapeDtypeStruct((m, w), dt) for w, dt in outs],
        compiler_params=_params(("parallel",)),
    )(*rows, *params)


def _rowwise_bwd(fn, rows, params, cts, *, name, diff_rows, diff_params, tm=None, with_row0=False, zero_rows_below=0,
                 out_dtypes=None):
    m = rows[0].shape[0]
    tm = tm or _tile(m)
    nr, npar = len(rows), len(params)
    d_idx = [i for i in range(nr) if diff_rows[i]]
    p_idx = [i for i in range(npar) if diff_params[i]]
    out_dtypes = out_dtypes or [f32] * len(d_idx)
    flat_cts = [c for group in cts for c in group]
    n_ct = len(flat_cts)

    def body(*refs):
        vals = [r[...] for r in refs[:nr + npar]]
        ct_refs = refs[nr + npar:nr + npar + n_ct]
        out_refs = refs[nr + npar + n_ct:]
        kw = dict(row0=pl.program_id(0) * tm) if with_row0 else {}
        ct_vals, k = [], 0
        for group in cts:
            acc = ct_refs[k][...].astype(f32)
            for extra in range(1, len(group)):
                acc = acc + ct_refs[k + extra][...].astype(f32)
            k += len(group)
            if zero_rows_below:
                rr = pl.program_id(0) * tm + lax.broadcasted_iota(jnp.int32, acc.shape, 0)
                acc = jnp.where(rr >= zero_rows_below, acc, 0.0)
            ct_vals.append(acc)

        def g(*dargs):
            full = list(vals)
            for pos, i in enumerate(d_idx):
                full[i] = dargs[pos]
            for pos, i in enumerate(p_idx):
                full[nr + i] = dargs[len(d_idx) + pos]
            return tuple(fn(*full, **kw))

        _, vjp = jax.vjp(g, *[vals[i].astype(f32) for i in d_idx], *[vals[nr + i] for i in p_idx])
        grads = vjp(tuple(ct_vals))
        for pos in range(len(d_idx)):
            out_refs[pos][...] = grads[pos].astype(out_refs[pos].dtype)
        first = pl.program_id(0) == 0
        for pos in range(len(p_idx)):
            o_ref = out_refs[len(d_idx) + pos]

            @pl.when(first)
            def _(o_ref=o_ref):
                o_ref[...] = jnp.zeros_like(o_ref)

            o_ref[...] += grads[len(d_idx) + pos]

    return pl.pallas_call(
        body, name=name, grid=(m // tm,),
        in_specs=[pl.BlockSpec((tm, r.shape[1]), lambda i: (i, 0)) for r in rows] + [_full(p.shape) for p in params]
        + [pl.BlockSpec((tm, c.shape[1]), lambda i: (i, 0)) for c in flat_cts],
        out_specs=[pl.BlockSpec((tm, rows[i].shape[1]), lambda i_: (i_, 0)) for i in d_idx]
        + [_full(params[i].shape) for i in p_idx],
        out_shape=[jax.ShapeDtypeStruct(rows[i].shape, dt) for i, dt in zip(d_idx, out_dtypes)]
        + [jax.ShapeDtypeStruct(params[i].shape, f32) for i in p_idx],
        compiler_params=_params(("arbitrary",)),
    )(*rows, *params, *flat_cts)


def _rms(x, g):
    return x * lax.rsqrt(jnp.mean(x * x, axis=-1, keepdims=True) + RMS_EPS) * g


def _head_sum_matrix(width, head):
    idx = jnp.arange(width) // head
    return (idx[:, None] == idx[None, :]).astype(f32)


def _rope_tables(lp):
    half = ROPE_DIM // 2
    pos = (jnp.arange(lp) - PAD).astype(f32)
    inv_freq = jnp.power(jnp.float32(ROPE_THETA), -jnp.arange(half, dtype=f32) * (2.0 / ROPE_DIM))
    ang = pos[:, None] * inv_freq[None, :]
    cos, sin = jnp.cos(ang), jnp.sin(ang)
    ones = jnp.ones((lp, HEAD_DIM - ROPE_DIM), f32)
    zeros = jnp.zeros((lp, HEAD_DIM - ROPE_DIM), f32)
    cos_t = jnp.concatenate([cos, cos, ones], axis=1)
    sin_t = jnp.concatenate([-sin, sin, zeros], axis=1)
    i = jnp.arange(HEAD_DIM)
    src = jnp.where(i < half, i + half, jnp.where(i < ROPE_DIM, i - half, i))
    swap = ((i[:, None] == src[None, :]) & (i[None, :] < ROPE_DIM)).astype(f32)
    return cos_t, sin_t, swap


def _attn_prep(qkv, cos_t, sin_t, swap):
    outs = []
    for h in range(Q_HEADS + KV_HEADS):
        t = qkv[:, h * HEAD_DIM:(h + 1) * HEAD_DIM]
        outs.append(t * cos_t + _dot_const(t, swap) * sin_t)
    q = jnp.concatenate(outs[:Q_HEADS], axis=1)
    k = jnp.concatenate(outs[Q_HEADS:], axis=1)
    return q, k, qkv[:, Q_W + KV_W:]


def _softplus(z):
    return jnp.maximum(z, 0.0) + jnp.log1p(jnp.exp(-jnp.abs(z)))


def _rwkv_prep(rkv, lora, w0, w2, a0, a2, g2, k_k, k_a, hsum):
    r = rkv[:, :RWKV_DIM]
    k = rkv[:, RWKV_DIM:2 * RWKV_DIM]
    v = rkv[:, 2 * RWKV_DIM:]
    dw = lora[:, :DECAY_LORA]
    da = lora[:, DECAY_LORA:DECAY_LORA + AAA_LORA]
    dg = lora[:, DECAY_LORA + AAA_LORA:]
    w = -_softplus(-(w0 + _dot(jnp.tanh(dw), w2))) - 0.5
    a = jax.nn.sigmoid(a0 + _dot(da, a2))
    g = _dot(jax.nn.sigmoid(dg), g2)
    kk = k * k_k
    kk = kk * lax.rsqrt(jnp.maximum(_dot_const(kk * kk, hsum), 1e-24))
    k = k * (1.0 + (a - 1.0) * k_a)
    log_decay = -jnp.exp(w)
    return r, log_decay, k, v, -kk, kk * a, g


def _rwkv_post(y, r, k, v, g, ln_w, ln_b, r_k, hmean):
    hsum = hmean * RWKV_HEAD
    mean = _dot_const(y, hmean)
    yc = y - mean
    var = _dot_const(yc * yc, hmean)
    yn = yc * lax.rsqrt(var + RWKV_LN_EPS) * ln_w + ln_b
    bonus = _dot_const(r * k * r_k, hsum) * v
    return ((yn + bonus) * g,)


def _merge(gates, br_a, br_r):
    sg = jax.nn.sigmoid(gates)
    return (sg[:, :D_MODEL] * br_a + sg[:, D_MODEL:] * br_r,)


def _swiglu(gate, up):
    return (jax.nn.silu(gate) * up,)


def _token_shift(p, mix, *, name):
    m, c = p.shape
    tm = _tile(m)
    sub = tm // 8

    def body(p_ref, prev_ref, mix_ref, o_ref):
        x = p_ref[...]
        rows = lax.broadcasted_iota(jnp.int32, x.shape, 0)
        last = jnp.where(pl.program_id(0) == 0, 0.0, prev_ref[7:8, :])
        xp = jnp.where(rows == 0, last, pltpu.roll(x, 1, axis=0))
        o_ref[...] = x + (xp - x) * mix_ref[...]

    return pl.pallas_call(
        body, name=name, grid=(m // tm,),
        in_specs=[pl.BlockSpec((tm, c), lambda i: (i, 0)),
                  pl.BlockSpec((8, c), lambda i: (jnp.maximum(i * sub - 1, 0), 0)),
                  _full(mix.shape)],
        out_specs=pl.BlockSpec((tm, c), lambda i: (i, 0)),
        out_shape=jax.ShapeDtypeStruct((m, c), f32),
        compiler_params=_params(("parallel",)),
    )(p, p, mix)


def _token_shift_bwd(p, mix, dpf, *, name):
    m, c = p.shape
    tm = _tile(m)
    sub = tm // 8
    n_tiles = m // tm

    def body(p_ref, prev_ref, mix_ref, d_ref, nxt_ref, dp_ref, dmix_ref):
        i = pl.program_id(0)
        x = p_ref[...]
        d = d_ref[...]
        mixv = mix_ref[...]
        rows = lax.broadcasted_iota(jnp.int32, x.shape, 0)
        last = jnp.where(i == 0, 0.0, prev_ref[7:8, :])
        xp = jnp.where(rows == 0, last, pltpu.roll(x, 1, axis=0))
        dm = d * mixv
        first_next = jnp.where(i == n_tiles - 1, 0.0, nxt_ref[0:1, :] * mixv)
        dm_next = jnp.where(rows == tm - 1, first_next, pltpu.roll(dm, tm - 1, axis=0))
        dp = d - dm + dm_next
        dp_ref[...] = jnp.where(i * tm + rows >= PAD, dp, 0.0).astype(dp_ref.dtype)

        @pl.when(i == 0)
        def _():
            dmix_ref[...] = jnp.zeros_like(dmix_ref)

        dmix_ref[...] += jnp.sum(d * (xp - x), axis=0, keepdims=True)

    return pl.pallas_call(
        body, name=name, grid=(n_tiles,),
        in_specs=[pl.BlockSpec((tm, c), lambda i: (i, 0)),
                  pl.BlockSpec((8, c), lambda i: (jnp.maximum(i * sub - 1, 0), 0)),
                  _full(mix.shape),
                  pl.BlockSpec((tm, c), lambda i: (i, 0)),
                  pl.BlockSpec((8, c), lambda i: (jnp.minimum((i + 1) * sub, m // 8 - 1), 0))],
        out_specs=[pl.BlockSpec((tm, c), lambda i: (i, 0)), _full(mix.shape)],
        out_shape=[jax.ShapeDtypeStruct((m, c), bf16), jax.ShapeDtypeStruct(mix.shape, f32)],
        compiler_params=_params(("arbitrary",)),
    )(p, p, mix, dpf, dpf)


def _attn_masks(blk):
    qi = lax.broadcasted_iota(jnp.int32, (BLOCK, BLOCK), 0)
    ki = lax.broadcasted_iota(jnp.int32, (BLOCK, BLOCK), 1)
    qpos = blk * BLOCK + qi - PAD
    kpos_c = blk * BLOCK + ki - PAD
    kpos_p = kpos_c - BLOCK
    kpos_m = ki - PAD

    def band(kpos):
        return (kpos >= N_META) & (kpos <= qpos) & (qpos - kpos < WINDOW)

    return band(kpos_p), band(kpos_c), (kpos_m >= 0) & (kpos_m <= qpos)


def _attn_probs(qs, k3s, sink, oks):
    s = [[jnp.where(ok, _dot(qh, kx, "nt"), NEG_INF) for kx, ok in zip(k3, oks)] for qh, k3 in zip(qs, k3s)]
    mx = [jnp.maximum(jnp.maximum(jnp.max(t[0], -1, keepdims=True), jnp.max(t[1], -1, keepdims=True)),
                      jnp.maximum(jnp.max(t[2], -1, keepdims=True), sk)) for t, sk in zip(s, sink)]
    e = [[jnp.exp(tx - m) for tx in t] for t, m in zip(s, mx)]
    e_sink = [jnp.exp(sk - m) for sk, m in zip(sink, mx)]
    inv = [1.0 / (jnp.sum(t[0], -1, keepdims=True) + jnp.sum(t[1], -1, keepdims=True)
                  + jnp.sum(t[2], -1, keepdims=True) + es) for t, es in zip(e, e_sink)]
    return [[tx * i for tx in t] for t, i in zip(e, inv)], [es * i for es, i in zip(e_sink, inv)]


def _head_cols(i):
    return slice(i * HEAD_DIM, (i + 1) * HEAD_DIM)


def _attn_operands(refs):
    q_ref, kp_ref, kc_ref, km_ref, vp_ref, vc_ref, vm_ref, s_ref = refs
    qs = [q_ref[:, _head_cols(i)] * (HEAD_DIM ** -0.5) for i in range(Q_HEADS)]
    k3 = [[ref[:, _head_cols(h)] for ref in (kp_ref, kc_ref, km_ref)] for h in range(KV_HEADS)]
    v3 = [[ref[:, _head_cols(h)] for ref in (vp_ref, vc_ref, vm_ref)] for h in range(KV_HEADS)]
    return (qs, [k3[i // GROUP] for i in range(Q_HEADS)], [v3[i // GROUP] for i in range(Q_HEADS)],
            [s_ref[:, i:i + 1] for i in range(Q_HEADS)])


def _attention(q, k, v, sinks, *, name):
    lp = q.shape[0]
    nb = lp // BLOCK
    prev = lambda i: (jnp.maximum(i - 1, 0), 0)
    cur = lambda i: (i, 0)
    meta = lambda i: (0, 0)
    kv = lambda index: pl.BlockSpec((BLOCK, KV_W), index)

    def body(*refs):
        o_ref = refs[-1]
        qs, k3s, v3s, sink = _attn_operands(refs[:-1])
        p, _ = _attn_probs(qs, k3s, sink, _attn_masks(pl.program_id(0)))
        out = [_dot(ph[0], v3[0]) + _dot(ph[1], v3[1]) + _dot(ph[2], v3[2]) for ph, v3 in zip(p, v3s)]
        for i in range(Q_HEADS):
            o_ref[:, _head_cols(i)] = out[i].astype(o_ref.dtype)

    return pl.pallas_call(
        body, name=name, grid=(nb,),
        in_specs=[pl.BlockSpec((BLOCK, Q_W), cur), kv(prev), kv(cur), kv(meta), kv(prev), kv(cur), kv(meta),
                  _full((1, Q_HEADS))],
        out_specs=pl.BlockSpec((BLOCK, Q_W), cur),
        out_shape=jax.ShapeDtypeStruct((lp, Q_W), bf16),
        compiler_params=_params(("parallel",)),
    )(q, k, k, k, v, v, v, sinks)


def _attention_bwd(q, k, v, sinks, do, *, name):
    lp = q.shape[0]
    nb = lp // BLOCK
    cur = lambda n: (jnp.minimum(n, nb - 1), 0)
    prev = lambda n: (jnp.maximum(jnp.minimum(n, nb - 1) - 1, 0), 0)
    behind = lambda n: (jnp.maximum(n - 1, 0), 0)
    meta = lambda n: (0, 0)
    kv = lambda index: pl.BlockSpec((BLOCK, KV_W), index)
    scale = HEAD_DIM ** -0.5

    def body(*refs):
        ins, do_ref = refs[:8], refs[8]
        dq_ref, dk_ref, dv_ref, dkm_ref, dvm_ref, ds_ref, carry_k, carry_v = refs[9:]
        n = pl.program_id(0)

        @pl.when(n == 0)
        def _():
            for ref in (dkm_ref, dvm_ref, ds_ref, carry_k, carry_v):
                ref[...] = jnp.zeros_like(ref)

        @pl.when(n < nb)
        def _():
            qs, k3s, v3s, sink = _attn_operands(ins)
            do = [do_ref[:, _head_cols(i)] for i in range(Q_HEADS)]
            p, p_sink = _attn_probs(qs, k3s, sink, _attn_masks(n))
            out = [_dot(ph[0], v3[0]) + _dot(ph[1], v3[1]) + _dot(ph[2], v3[2]) for ph, v3 in zip(p, v3s)]
            delta = [jnp.sum(d * o, -1, keepdims=True) for d, o in zip(do, out)]
            dp = [[_dot(d, vx, "nt") for vx in v3] for d, v3 in zip(do, v3s)]
            ds = [[px * (dx - dl) for px, dx in zip(ph, dh)] for ph, dh, dl in zip(p, dp, delta)]
            dq = [_dot(dsh[0], k3[0]) + _dot(dsh[1], k3[1]) + _dot(dsh[2], k3[2]) for dsh, k3 in zip(ds, k3s)]
            for i in range(Q_HEADS):
                dq_ref[:, _head_cols(i)] = dq[i] * scale
                ds_ref[:, i:i + 1] -= jnp.sum(p_sink[i] * delta[i], axis=0, keepdims=True)
            for h in range(KV_HEADS):
                group = slice(h * GROUP, (h + 1) * GROUP)
                q_all = jnp.concatenate(qs[group], axis=0)
                do_all = jnp.concatenate(do[group], axis=0)
                dk3 = [_dot(jnp.concatenate([dsh[x] for dsh in ds[group]], axis=0), q_all, "tn") for x in range(3)]
                dv3 = [_dot(jnp.concatenate([ph[x] for ph in p[group]], axis=0), do_all, "tn") for x in range(3)]
                hs = _head_cols(h)
                for out_ref, carry, meta_ref, d3 in ((dk_ref, carry_k, dkm_ref, dk3),
                                                     (dv_ref, carry_v, dvm_ref, dv3)):
                    out_ref[:, hs] = carry[:, hs] + d3[0]
                    carry[:, hs] = d3[1]
                    meta_ref[:, hs] += d3[2]

        @pl.when(n == nb)
        def _():
            dk_ref[...] = carry_k[...]
            dv_ref[...] = carry_v[...]

    kv_shape = jax.ShapeDtypeStruct((lp, KV_W), f32)
    one_shape = jax.ShapeDtypeStruct((BLOCK, KV_W), f32)
    return pl.pallas_call(
        body, name=name, grid=(nb + 1,),
        in_specs=[pl.BlockSpec((BLOCK, Q_W), cur), kv(prev), kv(cur), kv(meta), kv(prev), kv(cur), kv(meta),
                  _full((1, Q_HEADS)), pl.BlockSpec((BLOCK, Q_W), cur)],
        out_specs=[pl.BlockSpec((BLOCK, Q_W), cur), kv(behind), kv(behind), kv(meta), kv(meta),
                   _full((1, Q_HEADS))],
        out_shape=[jax.ShapeDtypeStruct((lp, Q_W), f32), kv_shape, kv_shape, one_shape, one_shape,
                   jax.ShapeDtypeStruct((1, Q_HEADS), f32)],
        scratch_shapes=[pltpu.VMEM((BLOCK, KV_W), f32), pltpu.VMEM((BLOCK, KV_W), f32)],
        compiler_params=_params(("arbitrary",)),
    )(q, k, k, k, v, v, v, sinks, do)


@jax.custom_vjp
def _known_inverse(l, x):
    return x


def _known_inverse_fwd(l, x):
    return x, x


def _known_inverse_bwd(x, ct):
    return _dot(_dot(x, ct, "tn"), x, "nt"), jnp.zeros_like(x)


_known_inverse.defvjp(_known_inverse_fwd, _known_inverse_bwd)


def _scan_chunk(s0, r, lw, k, v, a, b, inv=None):
    t = r[0].shape[0]
    ii = lax.broadcasted_iota(jnp.int32, (t, t), 0)
    jj = lax.broadcasted_iota(jnp.int32, (t, t), 1)
    incl = jj <= ii
    strict = jj < ii
    tri = incl.astype(f32)
    eye = jnp.where(ii == jj, 1.0, 0.0)
    cl = [_dot3(tri, x) for x in lw]
    e_pos = [jnp.exp(c) for c in cl]
    e_neg = [jnp.exp(-c) for c in cl]
    e_prev = [jnp.exp(c - x) for c, x in zip(cl, lw)]
    rt = [x * e for x, e in zip(r, e_pos)]
    at = [x * e for x, e in zip(a, e_prev)]
    bt = [x * e for x, e in zip(b, e_neg)]
    kt = [x * e for x, e in zip(k, e_neg)]
    l_ab = [jnp.where(strict, _dot(x, y, "nt"), 0.0) for x, y in zip(at, bt)]
    l_ak = [jnp.where(strict, _dot(x, y, "nt"), 0.0) for x, y in zip(at, kt)]
    r_b = [jnp.where(incl, _dot(x, y, "nt"), 0.0) for x, y in zip(rt, bt)]
    r_k = [jnp.where(incl, _dot(x, y, "nt"), 0.0) for x, y in zip(rt, kt)]
    if inv is None:
        inv = [eye + x for x in l_ab]
        pw = l_ab
        for _ in range(int(math.log2(t)) - 1):
            pw = [_dot(x, x) for x in pw]
            inv = [x + _dot(x, y) for x, y in zip(inv, pw)]
    else:
        inv = [_known_inverse(x, y) for x, y in zip(l_ab, inv)]
    rhs = [_dot(x, s, "nt") + _dot(m, y) for x, s, m, y in zip(at, s0, l_ak, v)]
    u = [_dot(x, y) for x, y in zip(inv, rhs)]
    y_s = [_dot(x, s, "nt") for x, s in zip(rt, s0)]
    y = [ys + _dot(m, uu) + _dot(n, vv) for ys, m, uu, n, vv in zip(y_s, r_b, u, r_k, v)]
    grow = [s + _dot(uu, x, "tn") + _dot(vv, z, "tn") for s, uu, x, vv, z in zip(s0, u, bt, v, kt)]
    s1 = [g * e[t - 1:t, :] for g, e in zip(grow, e_pos)]
    return y, s1, inv


def _head_rows(h):
    return slice(h * RWKV_HEAD, (h + 1) * RWKV_HEAD)


def _per_head(ref):
    return [ref[:, _head_rows(h)] for h in range(RWKV_HEADS)]


def _scan(r, lw, k, v, a, b, *, name):
    lp = r.shape[0]
    nc = lp // CHUNK
    row = pl.BlockSpec((CHUNK, RWKV_DIM), lambda c: (c, 0))

    def body(r_ref, lw_ref, k_ref, v_ref, a_ref, b_ref, y_ref, s_ref, inv_ref, state):
        @pl.when(pl.program_id(0) == 0)
        def _():
            state[...] = jnp.zeros_like(state)

        s_ref[...] = state[...]
        s0 = [state[_head_rows(h), :] for h in range(RWKV_HEADS)]
        y, s1, inv = _scan_chunk(s0, *[_per_head(ref) for ref in (r_ref, lw_ref, k_ref, v_ref, a_ref, b_ref)])
        for h in range(RWKV_HEADS):
            y_ref[:, _head_rows(h)] = y[h]
            state[_head_rows(h), :] = s1[h]
            inv_ref[h * CHUNK:(h + 1) * CHUNK, :] = inv[h].astype(inv_ref.dtype)

    return pl.pallas_call(
        body, name=name, grid=(nc,), in_specs=[row] * 6,
        out_specs=[row, pl.BlockSpec((RWKV_DIM, RWKV_HEAD), lambda c: (c, 0)),
                   pl.BlockSpec((RWKV_HEADS * CHUNK, CHUNK), lambda c: (c, 0))],
        out_shape=[jax.ShapeDtypeStruct((lp, RWKV_DIM), f32), jax.ShapeDtypeStruct((nc * RWKV_DIM, RWKV_HEAD), f32),
                   jax.ShapeDtypeStruct((nc * RWKV_HEADS * CHUNK, CHUNK), bf16)],
        scratch_shapes=[pltpu.VMEM((RWKV_DIM, RWKV_HEAD), f32)],
        compiler_params=_params(("arbitrary",)),
    )(r, lw, k, v, a, b)


def _scan_bwd(r, lw, k, v, a, b, states, inverses, dy, *, name):
    lp = r.shape[0]
    nc = lp // CHUNK
    back = lambda c: (nc - 1 - c, 0)
    row = pl.BlockSpec((CHUNK, RWKV_DIM), back)

    def body(r_ref, lw_ref, k_ref, v_ref, a_ref, b_ref, s_ref, inv_ref, dy_ref,
             dr_ref, dlw_ref, dk_ref, dv_ref, da_ref, db_ref, dstate):
        @pl.when(pl.program_id(0) == 0)
        def _():
            dstate[...] = jnp.zeros_like(dstate)

        outs = (dr_ref, dlw_ref, dk_ref, dv_ref, da_ref, db_ref)
        s0 = [s_ref[_head_rows(h), :] for h in range(RWKV_HEADS)]
        inv = [inv_ref[h * CHUNK:(h + 1) * CHUNK, :].astype(f32) for h in range(RWKV_HEADS)]
        _, vjp = jax.vjp(lambda *args: _scan_chunk(*args, inv=inv)[:2], s0,
                         *[_per_head(ref) for ref in (r_ref, lw_ref, k_ref, v_ref, a_ref, b_ref)])
        g = vjp((_per_head(dy_ref), [dstate[_head_rows(h), :] for h in range(RWKV_HEADS)]))
        for h in range(RWKV_HEADS):
            dstate[_head_rows(h), :] = g[0][h]
            for o_ref, gv in zip(outs, g[1:]):
                o_ref[:, _head_rows(h)] = gv[h]

    shape = jax.ShapeDtypeStruct((lp, RWKV_DIM), f32)
    return pl.pallas_call(
        body, name=name, grid=(nc,),
        in_specs=[row] * 6 + [pl.BlockSpec((RWKV_DIM, RWKV_HEAD), back),
                              pl.BlockSpec((RWKV_HEADS * CHUNK, CHUNK), back), row],
        out_specs=[row] * 6, out_shape=[shape] * 6,
        scratch_shapes=[pltpu.VMEM((RWKV_DIM, RWKV_HEAD), f32)],
        compiler_params=_params(("arbitrary",)),
    )(r, lw, k, v, a, b, states, inverses, dy)


def _loss_head(h2, target, g_final, *, name):
    lp = h2.shape[0]
    tm = BLOCK
    front_tiles = FRONT // tm

    def body(h_ref, t_ref, g_ref, loss_ref, dh_ref, dg_ref):
        i = pl.program_id(0)
        real = i >= front_tiles

        def tile_loss(hv, gv):
            err = _rms(hv, gv) - t_ref[...]
            return jnp.where(real, 0.5 * jnp.sum(jnp.mean(err * err, axis=-1, keepdims=True)), 0.0)

        loss, (dh, dg) = jax.value_and_grad(tile_loss, argnums=(0, 1))(h_ref[...], g_ref[...])

        @pl.when(i == 0)
        def _():
            loss_ref[...] = jnp.zeros_like(loss_ref)
            dg_ref[...] = jnp.zeros_like(dg_ref)

        loss_ref[...] += jnp.full(loss_ref.shape, loss, f32)
        dg_ref[...] += dg
        dh_ref[...] = dh

    return pl.pallas_call(
        body, name=name, grid=(lp // tm,),
        in_specs=[pl.BlockSpec((tm, D_MODEL), lambda i: (i, 0)),
                  pl.BlockSpec((tm, D_MODEL), lambda i: (jnp.maximum(i - front_tiles, 0), 0)),
                  _full(g_final.shape)],
        out_specs=[_full((8, 128)), pl.BlockSpec((tm, D_MODEL), lambda i: (i, 0)), _full(g_final.shape)],
        out_shape=[jax.ShapeDtypeStruct((8, 128), f32), jax.ShapeDtypeStruct((lp, D_MODEL), f32),
                   jax.ShapeDtypeStruct(g_final.shape, f32)],
        compiler_params=_params(("arbitrary",)),
    )(h2, target, g_final)


def _local_step(x, target, meta, p, late_weights=None, emit=None):
    emit = emit or (lambda group, grads: 0.0)
    seq = x.shape[0]
    lp = seq + FRONT
    h0 = jnp.concatenate([jnp.zeros((PAD, D_MODEL), f32), meta, x], axis=0)
    cos_t, sin_t, swap = _rope_tables(lp)
    hsum = _head_sum_matrix(RWKV_DIM, RWKV_HEAD)
    hmean = hsum / RWKV_HEAD
    w_qkv_t, w_rkv_t = p["w_in_t"][:ATTN_PROJ], p["w_in_t"][ATTN_PROJ:ATTN_PROJ + RKV_W]
    w_lora_t, w_gates_t = p["w_in_t"][ATTN_PROJ + RKV_W:ATTN_PROJ + RWKV_PROJ], p["w_in_t"][ATTN_PROJ + RWKV_PROJ:]
    b_qkv, b_rkv = p["b_in"][:, :ATTN_PROJ], p["b_in"][:, ATTN_PROJ:ATTN_PROJ + RKV_W]
    b_lora, b_gates = p["b_in"][:, ATTN_PROJ + RKV_W:ATTN_PROJ + RWKV_PROJ], p["b_in"][:, ATTN_PROJ + RWKV_PROJ:]
    prep_params = [p["w0"], p["w2"], p["a0"], p["a2"], p["g2"], p["k_k"], p["k_a"], hsum]
    post_params = [p["ln_w"], p["ln_b"], p["r_k"], hmean]

    (u,) = _rowwise(lambda hv, g: (_rms(hv, g),), [h0], [p["norm_mix_g"]], [(D_MODEL, bf16)], name="norm_mix")
    qkv = _mm(u, w_qkv_t, "nt", name="proj_qkv", bias=b_qkv, zero_rows_below=PAD)
    p_rkv = _mm(u, w_rkv_t, "nt", name="proj_rkv", bias=b_rkv, zero_rows_below=PAD)
    p_lora = _mm(u, w_lora_t, "nt", name="proj_lora", bias=b_lora, zero_rows_below=PAD)
    gates = _mm(u, w_gates_t, "nt", name="proj_gates", bias=b_gates, zero_rows_below=PAD)

    q, k, v = _rowwise(_attn_prep, [qkv, cos_t, sin_t], [swap], [(Q_W, bf16), (KV_W, bf16), (KV_W, bf16)],
                       name="attn_prep")
    y_attn = _attention(q, k, v, p["sinks"], name="attention")

    mix_rkv, mix_lora = p["mix"][:, :RKV_W], p["mix"][:, RKV_W:]
    pf_rkv = _token_shift(p_rkv, mix_rkv, name="shift_rkv")
    pf_lora = _token_shift(p_lora, mix_lora, name="shift_lora")
    wide = [(RWKV_DIM, f32)] * 7
    r_, lw_, k_, v_, a_, b_, g_ = _rowwise(_rwkv_prep, [pf_rkv, pf_lora], prep_params, wide, name="rwkv_prep")
    y_scan, states, inverses = _scan(r_, lw_, k_, v_, a_, b_, name="wkv_scan")
    (y_rwkv,) = _rowwise(_rwkv_post, [y_scan, r_, k_, v_, g_], post_params, [(RWKV_DIM, bf16)], name="rwkv_post")

    if late_weights is not None:
        p = {**p, **late_weights(y_rwkv)}
    br_a = _mm(y_attn, p["w_br_attn_t"], "nt", name="branch_attn")
    br_r = _mm(y_rwkv, p["w_br_rwkv_t"], "nt", name="branch_rwkv")
    (merged,) = _rowwise(_merge, [gates, br_a, br_r], [], [(D_MODEL, bf16)], name="merge")
    h1 = _mm(merged, p["w_o"], "nn", name="out_proj", add=h0)
    (f,) = _rowwise(lambda hv, g: (_rms(hv, g),), [h1], [p["norm_ffn_g"]], [(D_MODEL, bf16)], name="norm_ffn")
    gate = _mm(f, p["w_gate_t"], "nt", name="ffn_gate")
    up = _mm(f, p["w_up_t"], "nt", name="ffn_up")
    (act,) = _rowwise(_swiglu, [gate, up], [], [(D_FF, bf16)], name="swiglu")
    h2 = _mm(act, p["w_down"], "nn", name="ffn_down", add=h1)

    loss8, dh2, d_final_g = _loss_head(h2, target, p["norm_final_g"], name="loss_head")
    dact = _mm(dh2, p["w_down"], "nt", name="d_act")
    d_w_down = _mm_tn(act, dh2, name="dw_down")
    dgate, dup = _rowwise_bwd(_swiglu, [gate, up], [], [[dact]], name="swiglu_bwd",
                              diff_rows=[True, True], diff_params=[], out_dtypes=[bf16, bf16])
    d_w_gate_t = _mm_tn(dgate, f, name="dw_gate")
    d_w_up_t = _mm_tn(dup, f, name="dw_up")
    zero = emit("ffn", dict(w_down=d_w_down, w_gate_t=d_w_gate_t, w_up_t=d_w_up_t))
    df = _mm(dgate, p["w_gate_t"], "nn", name="d_f_gate")
    df = _mm(dup, p["w_up_t"], "nn", name="d_f_up", add=df)
    dh1, d_ffn_g = _rowwise_bwd(lambda hv, g: (_rms(hv, g), hv), [h1], [p["norm_ffn_g"] + zero], [[df], [dh2]],
                                name="norm_ffn_bwd", diff_rows=[True], diff_params=[True])
    dmerged = _mm(dh1, p["w_o"], "nt", name="d_merged")
    d_w_o = _mm_tn(merged, dh1, name="dw_o")
    dgates, dbr_a, dbr_r = _rowwise_bwd(_merge, [gates, br_a, br_r], [], [[dmerged]], name="merge_bwd",
                                        diff_rows=[True, True, True], diff_params=[], out_dtypes=[bf16] * 3)
    d_w_br_attn_t = _mm_tn(dbr_a, y_attn, name="dw_br_attn")
    d_w_br_rwkv_t = _mm_tn(dbr_r, y_rwkv, name="dw_br_rwkv")
    zero = emit("branch", dict(w_o=d_w_o, w_br_attn_t=d_w_br_attn_t, w_br_rwkv_t=d_w_br_rwkv_t))
    dy_attn = _mm(dbr_a, p["w_br_attn_t"], "nn", name="d_y_attn")
    dy_rwkv = _mm(dbr_r, p["w_br_rwkv_t"], "nn", name="d_y_rwkv")

    post_params = [p["ln_w"] + zero, p["ln_b"], p["r_k"], hmean]
    res = _rowwise_bwd(_rwkv_post, [y_scan, r_, k_, v_, g_], post_params, [[dy_rwkv]], name="rwkv_post_bwd",
                       diff_rows=[True] * 5, diff_params=[True, True, True, False])
    dy_scan, dr_p, dk_p, dv_p, dg_p, d_ln_w, d_ln_b, d_r_k = res
    dr_s, dlw_s, dk_s, dv_s, da_s, db_s = _scan_bwd(r_, lw_, k_, v_, a_, b_, states, inverses, dy_scan,
                                                    name="wkv_scan_bwd")
    res = _rowwise_bwd(_rwkv_prep, [pf_rkv, pf_lora], prep_params,
                       [[dr_s, dr_p], [dlw_s], [dk_s, dk_p], [dv_s, dv_p], [da_s], [db_s], [dg_p]],
                       name="rwkv_prep_bwd", diff_rows=[True, True], diff_params=[True] * 7 + [False],
                       zero_rows_below=PAD)
    dpf_rkv, dpf_lora, d_w0, d_w2, d_a0, d_a2, d_g2, d_k_k, d_k_a = res
    dp_rkv, d_mix_rkv = _token_shift_bwd(p_rkv, mix_rkv, dpf_rkv, name="shift_rkv_bwd")
    dp_lora, d_mix_lora = _token_shift_bwd(p_lora, mix_lora, dpf_lora, name="shift_lora_bwd")

    dq, dk, dv, dkm, dvm, d_sinks = _attention_bwd(q, k, v, p["sinks"], dy_attn, name="attention_bwd")
    rest = jnp.zeros((lp - BLOCK, KV_W), f32)
    dkm, dvm = jnp.concatenate([dkm, rest], axis=0), jnp.concatenate([dvm, rest], axis=0)
    (dqkv,) = _rowwise_bwd(_attn_prep, [qkv, cos_t, sin_t], [swap], [[dq], [dk, dkm], [dv, dvm]], name="attn_prep_bwd",
                           diff_rows=[True, False, False], diff_params=[False], out_dtypes=[bf16])

    d_w_qkv_t, db_qkv = _mm_tn(dqkv, u, name="dw_qkv", colsum=True)
    d_w_rkv_t, db_rkv = _mm_tn(dp_rkv, u, name="dw_rkv", colsum=True)
    d_w_lora_t, db_lora = _mm_tn(dp_lora, u, name="dw_lora", colsum=True)
    d_w_gates_t, db_gates = _mm_tn(dgates, u, name="dw_gates", colsum=True)
    d_w_in_t = jnp.concatenate([d_w_qkv_t, d_w_rkv_t, d_w_lora_t, d_w_gates_t], axis=0)
    zero = emit("input", dict(w_in_t=d_w_in_t, g2=d_g2, w2=d_w2, a2=d_a2))
    du = _mm(dqkv, w_qkv_t, "nn", name="d_u_qkv")
    du = _mm(dp_rkv, w_rkv_t, "nn", name="d_u_rkv", add=du)
    du = _mm(dp_lora, w_lora_t, "nn", name="d_u_lora", add=du)
    du = _mm(dgates, w_gates_t, "nn", name="d_u_gates", add=du)
    dh0, d_mix_g = _rowwise_bwd(lambda hv, g: (_rms(hv, g), hv), [h0], [p["norm_mix_g"] + zero], [[du], [dh1]],
                                name="norm_mix_bwd", diff_rows=[True], diff_params=[True])

    grads = dict(
        w_in_t=d_w_in_t,
        b_in=jnp.concatenate([db_qkv, db_rkv, db_lora, db_gates], axis=1),
        mix=jnp.concatenate([d_mix_rkv, d_mix_lora], axis=1),
        norm_mix_g=d_mix_g, sinks=d_sinks, w0=d_w0, w2=d_w2, a0=d_a0, a2=d_a2, g2=d_g2, k_k=d_k_k, k_a=d_k_a,
        r_k=d_r_k, ln_w=d_ln_w, ln_b=d_ln_b, w_br_attn_t=d_w_br_attn_t, w_br_rwkv_t=d_w_br_rwkv_t, w_o=d_w_o,
        norm_ffn_g=d_ffn_g, w_gate_t=d_w_gate_t, w_up_t=d_w_up_t, w_down=d_w_down, norm_final_g=d_final_g,
        meta=dh0[PAD:FRONT],
    )
    return loss8[0, 0], dh0[FRONT:], grads


def _position():
    return lax.axis_index("x"), lax.axis_index("y"), lax.axis_index("c")


def _other_chips(x, y):
    return [(1 - x, y), (x, 1 - y), (1 - x, 1 - y)]


_HBM = pl.BlockSpec(memory_space=pltpu.HBM)
_SEM = pl.BlockSpec(memory_space=pltpu.SEMAPHORE)
_EFFECT = pltpu.SideEffectType.DATAFLOW_SIDE_EFFECTING


def _chip_copies(src_refs, land_refs, send_sems, recv_sems, gather):
    x, y, c = _position()
    copies = []
    for a, (src, land) in enumerate(zip(src_refs, land_refs)):
        for j, (px, py) in enumerate(_other_chips(x, y)):
            copies.append(pltpu.make_async_remote_copy(
                src_ref=src if gather else src.at[2 * px + py],
                dst_ref=land.at[2 * x + y] if gather else land.at[j],
                send_sem=send_sems.at[3 * a + j], recv_sem=recv_sems.at[3 * a + j],
                device_id=(px, py, c), device_id_type=MESH))
    return copies


def _exchange_start(srcs, *, gather, name):
    n = len(srcs)
    lands = [lax.empty((N_CHIPS,) + s.shape if gather else (3,) + s.shape[1:], s.dtype) for s in srcs]

    def body(*refs):
        for cp in _chip_copies(refs[:n], refs[n:2 * n], refs[2 * n], refs[2 * n + 1], gather):
            cp.start()
        refs[-1][...] = jnp.zeros_like(refs[-1])

    res = pl.pallas_call(
        body, name=name,
        out_shape=(pltpu.SemaphoreType.DMA((3 * n,)), pltpu.SemaphoreType.DMA((3 * n,)),
                   *[pltpu.HBM(a.shape, a.dtype) for a in srcs + lands], jax.ShapeDtypeStruct((8, 128), f32)),
        in_specs=[_HBM] * (2 * n),
        out_specs=(_SEM, _SEM, *[_HBM] * (2 * n), pl.BlockSpec(memory_space=pltpu.VMEM)),
        input_output_aliases={i: 2 + i for i in range(2 * n)},
        compiler_params=pltpu.CompilerParams(has_side_effects=_EFFECT),
    )(*[pltpu.with_memory_space_constraint(a, pltpu.HBM) for a in srcs + lands])
    return res[0], res[1], list(res[2:2 + n]), list(res[2 + n:2 + 2 * n]), res[-1]


def _exchange_wait(handle, after, *, gather, name):
    send_sems, recv_sems, srcs, lands, _ = handle
    n = len(srcs)

    def body(*refs):
        for cp in _chip_copies(refs[:n], refs[n:2 * n], refs[2 * n], refs[2 * n + 1], gather):
            cp.wait_send()
            cp.wait_recv()

    res = pl.pallas_call(
        body, name=name,
        out_shape=tuple(pltpu.HBM(a.shape, a.dtype) for a in srcs + lands),
        in_specs=[_HBM] * (2 * n) + [_SEM, _SEM, pl.BlockSpec(memory_space=pl.ANY)],
        out_specs=tuple([_HBM] * (2 * n)),
        input_output_aliases={i: i for i in range(2 * n)},
        compiler_params=pltpu.CompilerParams(has_side_effects=_EFFECT),
    )(*srcs, *lands, send_sems, recv_sems, after)
    return list(res[:n]), list(res[n:])


def _sum_own_and_received(g, recv, *, name):
    _, r, w = g.shape
    tm = _tile(r)
    if g.dtype == bf16 and tm % 16:
        tm = r
    x, y, _ = _position()
    me = jnp.reshape(2 * x + y, (1,)).astype(jnp.int32)

    def body(me_ref, g_ref, r_ref, o_ref):
        o_ref[...] = (g_ref[0].astype(f32) + r_ref[0].astype(f32)) + (r_ref[1].astype(f32) + r_ref[2].astype(f32))

    return pl.pallas_call(
        body, name=name,
        grid_spec=pltpu.PrefetchScalarGridSpec(
            num_scalar_prefetch=1, grid=(r // tm,),
            in_specs=[pl.BlockSpec((1, tm, w), lambda i, me_ref: (me_ref[0], i, 0)),
                      pl.BlockSpec((3, tm, w), lambda i, me_ref: (0, i, 0))],
            out_specs=pl.BlockSpec((tm, w), lambda i, me_ref: (i, 0))),
        out_shape=jax.ShapeDtypeStruct((r, w), f32),
        compiler_params=_params(("parallel",)),
    )(me, g, recv)


def _swap_cores(arrs, *, name):
    n = len(arrs)

    def body(*refs):
        x, y, c = _position()
        copies = [pltpu.make_async_remote_copy(
            src_ref=refs[i], dst_ref=refs[n + i], send_sem=refs[2 * n].at[i], recv_sem=refs[2 * n + 1].at[i],
            device_id=(x, y, 1 - c), device_id_type=MESH) for i in range(n)]
        for cp in copies:
            cp.start()
        for cp in copies:
            cp.wait_recv()
        for cp in copies:
            cp.wait_send()

    return pl.pallas_call(
        body, name=name,
        in_specs=[pl.BlockSpec(memory_space=pl.ANY)] * n,
        out_specs=[pl.BlockSpec(memory_space=pl.ANY)] * n,
        out_shape=[jax.ShapeDtypeStruct(a.shape, a.dtype) for a in arrs],
        scratch_shapes=[pltpu.SemaphoreType.DMA((n,)), pltpu.SemaphoreType.DMA((n,))],
    )(*arrs)


def _all_reduce_small(a, *, name):
    rows, w = a.shape

    def body(a_ref, o_ref, buf, send_sems, recv_sems):
        x, y, c = _position()
        me = 4 * x + 2 * y + c
        buf[0] = a_ref[...]
        sends = []
        for rel in range(1, N_DEV):
            peer = ((1 - x) if rel & 4 else x, (1 - y) if rel & 2 else y, (1 - c) if rel & 1 else c)
            cp = pltpu.make_async_remote_copy(
                src_ref=a_ref, dst_ref=buf.at[rel], send_sem=send_sems.at[rel - 1], recv_sem=recv_sems.at[rel - 1],
                device_id=peer, device_id_type=MESH)
            cp.start()
            sends.append(cp)
        for cp in sends:
            cp.wait_recv()
        for cp in sends:
            cp.wait_send()
        acc = buf[jnp.bitwise_xor(me, 0)]
        for d in range(1, N_DEV):
            acc = acc + buf[jnp.bitwise_xor(me, d)]
        o_ref[...] = acc

    return pl.pallas_call(
        body, name=name,
        in_specs=[pl.BlockSpec(memory_space=pltpu.VMEM)],
        out_specs=pl.BlockSpec(memory_space=pltpu.VMEM),
        out_shape=jax.ShapeDtypeStruct((rows, w), f32),
        scratch_shapes=[pltpu.VMEM((N_DEV, rows, w), f32), pltpu.SemaphoreType.DMA((N_DEV - 1,)),
                        pltpu.SemaphoreType.DMA((N_DEV - 1,))],
    )(a)


def _adamw(w, g, m, v, *, name):
    rows, cols = w.shape
    tm = _tile(rows, 256)

    def body(w_ref, g_ref, m_ref, v_ref, d_ref, nm_ref, nv_ref):
        gv = g_ref[...]
        nm = ADAM_B1 * m_ref[...] + (1.0 - ADAM_B1) * gv
        nv = ADAM_B2 * v_ref[...] + (1.0 - ADAM_B2) * (gv * gv)
        m_hat = nm / (1.0 - ADAM_B1 ** ADAM_STEP)
        v_hat = nv / (1.0 - ADAM_B2 ** ADAM_STEP)
        d_ref[...] = -ADAM_LR * (m_hat / (jnp.sqrt(v_hat) + ADAM_EPS) + ADAM_WD * w_ref[...])
        nm_ref[...] = nm
        nv_ref[...] = nv

    spec = pl.BlockSpec((tm, cols), lambda i: (i, 0))
    shape = jax.ShapeDtypeStruct((rows, cols), f32)
    return pl.pallas_call(
        body, name=name, grid=(rows // tm,), in_specs=[spec] * 4, out_specs=[spec] * 3, out_shape=[shape] * 3,
        compiler_params=_params(("parallel",)),
    )(w, g, m, v)


def _pad_rows(a, rows):
    return jnp.concatenate([a, jnp.zeros((rows - a.shape[0], a.shape[1]), a.dtype)], axis=0) if rows > a.shape[0] else a


_SMALL = (("norm_mix_g", D_MODEL), ("b_in", D_IN), ("sinks", Q_HEADS), ("mix", RWKV_PROJ), ("w0", RWKV_DIM),
          ("a0", RWKV_DIM), ("k_k", RWKV_DIM), ("k_a", RWKV_DIM), ("r_k", RWKV_DIM), ("ln_w", RWKV_DIM),
          ("ln_b", RWKV_DIM), ("norm_ffn_g", D_MODEL), ("norm_final_g", D_MODEL))


def _pack_small(d):
    flat = jnp.concatenate([d[n].reshape(-1).astype(f32) for n, _ in _SMALL])
    return flat


def _unpack_small(flat):
    out, off = {}, 0
    for n, size in _SMALL:
        out[n] = flat[off:off + size]
        off += size
    return out


_SMALL_TOTAL = sum(s for _, s in _SMALL)


def kernel(x, meta_tokens, norm_mix_g, w_in, b_in, attn_sinks, rwkv_mix, rwkv_w0, rwkv_w2, rwkv_a0, rwkv_a2, rwkv_g2, rwkv_k_k, rwkv_k_a, rwkv_r_k, rwkv_ln_w, rwkv_ln_b, w_br_attn, w_br_rwkv, w_o, norm_ffn_g, w_ffn_gate, w_ffn_up, w_ffn_down, norm_final_g, loss_target, m_meta_tokens, m_norm_mix_g, m_w_in, m_b_in, m_attn_sinks, m_rwkv_mix, m_rwkv_w0, m_rwkv_w2, m_rwkv_a0, m_rwkv_a2, m_rwkv_g2, m_rwkv_k_k, m_rwkv_k_a, m_rwkv_r_k, m_rwkv_ln_w, m_rwkv_ln_b, m_w_br_attn, m_w_br_rwkv, m_w_o, m_norm_ffn_g, m_w_ffn_gate, m_w_ffn_up, m_w_ffn_down, m_norm_final_g, v_meta_tokens, v_norm_mix_g, v_w_in, v_b_in, v_attn_sinks, v_rwkv_mix, v_rwkv_w0, v_rwkv_w2, v_rwkv_a0, v_rwkv_a2, v_rwkv_g2, v_rwkv_k_k, v_rwkv_k_a, v_rwkv_r_k, v_rwkv_ln_w, v_rwkv_ln_b, v_w_br_attn, v_w_br_rwkv, v_w_o, v_norm_ffn_g, v_w_ffn_gate, v_w_ffn_up, v_w_ffn_down, v_norm_final_g):
    names = ("meta_tokens", "norm_mix_g", "w_in", "b_in", "attn_sinks", "rwkv_mix", "rwkv_w0", "rwkv_w2", "rwkv_a0",
             "rwkv_a2", "rwkv_g2", "rwkv_k_k", "rwkv_k_a", "rwkv_r_k", "rwkv_ln_w", "rwkv_ln_b", "w_br_attn",
             "w_br_rwkv", "w_o", "norm_ffn_g", "w_ffn_gate", "w_ffn_up", "w_ffn_down", "norm_final_g")
    w_all = dict(zip(names, (meta_tokens, norm_mix_g, w_in, b_in, attn_sinks, rwkv_mix, rwkv_w0, rwkv_w2, rwkv_a0,
                             rwkv_a2, rwkv_g2, rwkv_k_k, rwkv_k_a, rwkv_r_k, rwkv_ln_w, rwkv_ln_b, w_br_attn,
                             w_br_rwkv, w_o, norm_ffn_g, w_ffn_gate, w_ffn_up, w_ffn_down, norm_final_g)))
    m_all = dict(zip(names, (m_meta_tokens, m_norm_mix_g, m_w_in, m_b_in, m_attn_sinks, m_rwkv_mix, m_rwkv_w0,
                             m_rwkv_w2, m_rwkv_a0, m_rwkv_a2, m_rwkv_g2, m_rwkv_k_k, m_rwkv_k_a, m_rwkv_r_k,
                             m_rwkv_ln_w, m_rwkv_ln_b, m_w_br_attn, m_w_br_rwkv, m_w_o, m_norm_ffn_g, m_w_ffn_gate,
                             m_w_ffn_up, m_w_ffn_down, m_norm_final_g)))
    v_all = dict(zip(names, (v_meta_tokens, v_norm_mix_g, v_w_in, v_b_in, v_attn_sinks, v_rwkv_mix, v_rwkv_w0,
                             v_rwkv_w2, v_rwkv_a0, v_rwkv_a2, v_rwkv_g2, v_rwkv_k_k, v_rwkv_k_a, v_rwkv_r_k,
                             v_rwkv_ln_w, v_rwkv_ln_b, v_w_br_attn, v_w_br_rwkv, v_w_o, v_norm_ffn_g, v_w_ffn_gate,
                             v_w_ffn_up, v_w_ffn_down, v_norm_final_g)))
    cx, cy, _ = _position()
    chip = 2 * cx + cy

    t_of = dict(w_in_t="w_in", w_gate_t="w_ffn_gate", w_up_t="w_ffn_up", w_br_attn_t="w_br_attn",
                w_br_rwkv_t="w_br_rwkv", g2_t="rwkv_g2", w2_t="rwkv_w2", a2_t="rwkv_a2")
    plain_of = dict(w_down="w_ffn_down", w_o="w_o")
    meta_cols = meta_tokens.shape[1]

    def shard(k):
        return (w_all[t_of[k]][0].T if k in t_of else w_all[plain_of[k]][0]).astype(bf16)

    def whole(zone, own):
        return lax.dynamic_update_slice_in_dim(zone, own[None], chip, axis=0).reshape(-1, own.shape[-1])

    early = ("w_in_t", "g2_t", "w2_t", "a2_t")
    late = ("w_gate_t", "w_up_t", "w_down", "w_o", "w_br_attn_t", "w_br_rwkv_t")
    early_h = _exchange_start([shard(k) for k in early] + [meta_tokens], gather=True, name="gather_early_start")
    late_h = _exchange_start([shard(k) for k in late], gather=True, name="gather_late_start")
    own, zones = _exchange_wait(early_h, late_h[4], gather=True, name="gather_early_wait")
    got = {k: whole(z, o) for k, z, o in zip(early, zones, own)}
    meta_full = whole(zones[-1], own[-1]).reshape(N_CHIPS, N_META, meta_cols).transpose(1, 0, 2).reshape(N_META, -1)
    p = dict(
        w_in_t=got["w_in_t"], g2=got["g2_t"].T.astype(f32), w2=got["w2_t"].T.astype(f32),
        a2=got["a2_t"].T.astype(f32),
        b_in=b_in, sinks=attn_sinks, mix=rwkv_mix, w0=rwkv_w0, a0=rwkv_a0, k_k=rwkv_k_k, k_a=rwkv_k_a,
        r_k=rwkv_r_k.reshape(1, RWKV_DIM), ln_w=rwkv_ln_w, ln_b=rwkv_ln_b, norm_mix_g=norm_mix_g,
        norm_ffn_g=norm_ffn_g, norm_final_g=norm_final_g.reshape(1, D_MODEL),
    )

    def late_weights(after):
        own_l, zones_l = _exchange_wait(late_h, after, gather=True, name="gather_late_wait")
        return {k: whole(z, o) for k, z, o in zip(late, zones_l, own_l)}

    started = {}

    def emit(group, grads_):
        keys = list(grads_)
        slabs = []
        for k in keys:
            a = grads_[k].T if k in ("g2", "w2", "a2") else grads_[k]
            slabs.append(a.reshape(N_CHIPS, a.shape[0] // N_CHIPS, a.shape[1]))
        started[group] = (keys, _exchange_start(slabs, gather=False, name="scatter_" + group + "_start"))
        return started[group][1][4][0, 0]

    loss, dx, g = _local_step(x[0], loss_target[0], meta_full, p, late_weights, emit)

    def partial_sums(group, after):
        keys, handle = started[group]
        slabs, lands = _exchange_wait(handle, after, gather=False, name="scatter_" + group + "_wait")
        return {k: _sum_own_and_received(s, l, name="sum_chips_" + k) for k, s, l in zip(keys, slabs, lands)}

    parts = partial_sums("ffn", dx)
    parts.update(partial_sums("branch", parts["w_down"]))
    parts.update(partial_sums("input", parts["w_o"]))
    keys = list(parts)
    others = _swap_cores([parts[k] for k in keys], name="swap_cores")
    gs = {}
    for k, other in zip(keys, others):
        (gs[k],) = _rowwise(lambda a, b: (a + b,), [parts[k], other], [], [(other.shape[1], f32)],
                            name="sum_cores_" + k)
    gs.update(g2_t=gs.pop("g2"), w2_t=gs.pop("w2"), a2_t=gs.pop("a2"))

    small = jnp.concatenate([_pack_small(g), loss.reshape(1)])
    small_rows = -(-small.shape[0] // PACK_W)
    small = jnp.concatenate([small, jnp.zeros((small_rows * PACK_W - small.shape[0],), f32)]).reshape(small_rows, PACK_W)
    small_rows8 = -(-(small_rows + N_META) // 8) * 8
    reduced = _all_reduce_small(_pad_rows(jnp.concatenate([g["meta"], small], axis=0), small_rows8),
                                name="reduce_small")
    g_meta = lax.dynamic_slice_in_dim(reduced[:N_META], chip * meta_cols, meta_cols, axis=1)
    flat = reduced[N_META:N_META + small_rows].reshape(-1)
    g_small = _unpack_small(flat)
    loss_total = flat[_SMALL_TOTAL]

    small_of = dict(norm_mix_g="norm_mix_g", b_in="b_in", attn_sinks="sinks", rwkv_mix="mix", rwkv_w0="w0",
                    rwkv_a0="a0", rwkv_k_k="k_k", rwkv_k_a="k_a", rwkv_r_k="r_k", rwkv_ln_w="ln_w",
                    rwkv_ln_b="ln_b", norm_ffn_g="norm_ffn_g", norm_final_g="norm_final_g")
    grads = {"meta_tokens": g_meta}
    for n, k in small_of.items():
        grads[n] = g_small[k].reshape(w_all[n].shape)
    for k, n in t_of.items():
        grads[n] = gs[k].T.reshape(w_all[n].shape)
    for k, n in plain_of.items():
        grads[n] = gs[k].reshape(w_all[n].shape)

    big = ("w_in", "w_ffn_gate", "w_ffn_up", "w_ffn_down", "w_o", "w_br_attn", "w_br_rwkv", "rwkv_g2", "rwkv_w2",
           "rwkv_a2")
    delta, new_m, new_v = {}, {}, {}
    for n in big:
        shape2 = w_all[n].shape[1:]
        d_, m_, v_ = _adamw(w_all[n].reshape(shape2), grads[n].reshape(shape2), m_all[n].reshape(shape2),
                            v_all[n].reshape(shape2), name="adamw_" + n)
        delta[n], new_m[n], new_v[n] = (t.reshape(w_all[n].shape) for t in (d_, m_, v_))
    rest = [n for n in names if n not in big]

    def pack_rest(src):
        flat_ = jnp.concatenate([src[n].reshape(-1) for n in rest])
        rows_ = -(-flat_.shape[0] // (8 * PACK_W)) * 8
        return jnp.concatenate([flat_, jnp.ones((rows_ * PACK_W - flat_.shape[0],), f32)]).reshape(rows_, PACK_W)

    d_, m_, v_ = _adamw(pack_rest(w_all), pack_rest(grads), pack_rest(m_all), pack_rest(v_all), name="adamw_small")
    off = 0
    for n in rest:
        size = w_all[n].size
        for dst, src in ((delta, d_), (new_m, m_), (new_v, v_)):
            dst[n] = src.reshape(-1)[off:off + size].reshape(w_all[n].shape)
        off += size

    return (loss_total, dx.reshape(x.shape), *[grads[n] for n in names], *[delta[n] for n in names],
            *[new_m[n] for n in names], *[new_v[n] for n in names])
```

```python
import functools
import math

import jax
import jax.numpy as jnp
from jax import lax
from jax.experimental import pallas as pl
from jax.experimental.pallas import tpu as pltpu

f32 = jnp.float32
bf16 = jnp.bfloat16

D_MODEL = 1024
N_META = 16
HEAD_DIM = 64
Q_HEADS = 8
KV_HEADS = 2
GROUP = Q_HEADS // KV_HEADS
WINDOW = 128
BLOCK = 128
ROPE_THETA = 500000.0
ROPE_DIM = HEAD_DIM // 4
RWKV_HEADS = 8
RWKV_HEAD = 64
RWKV_DIM = RWKV_HEADS * RWKV_HEAD
DECAY_LORA = 64
AAA_LORA = 64
GATE_LORA = 160
LORA_W = DECAY_LORA + AAA_LORA + GATE_LORA
RWKV_LN_EPS = 64e-5
D_FF = 2816
Q_W = Q_HEADS * HEAD_DIM
KV_W = KV_HEADS * HEAD_DIM
ATTN_PROJ = Q_W + 2 * KV_W
RKV_W = 3 * RWKV_DIM
RWKV_PROJ = RKV_W + LORA_W
D_IN = ATTN_PROJ + RWKV_PROJ + 2 * D_MODEL
RMS_EPS = 1e-6
NEG_INF = -1e30
PAD = BLOCK - N_META
FRONT = PAD + N_META

ADAM_LR = 0.001
ADAM_B1 = 0.9
ADAM_B2 = 0.999
ADAM_EPS = 1e-08
ADAM_WD = 0.01
ADAM_STEP = 10

N_CHIPS = 4
N_DEV = 8
CHUNK = 64
VMEM_LIMIT = 56 * 1024 * 1024
PACK_W = 1024
MESH = pl.DeviceIdType.MESH
HIGHEST = lax.Precision.HIGHEST


def _tile(m, pref=384):
    for step in (16, 8):
        for t in range(min(m, pref) // step * step, 0, -step):
            if m % t == 0:
                return t
    return m


def _params(sem=None):
    return pltpu.CompilerParams(dimension_semantics=sem, vmem_limit_bytes=VMEM_LIMIT)


def _full(shape):
    nd = len(shape)
    return pl.BlockSpec(shape, lambda *_: (0,) * nd)


def _dot(a, b, dims="nn", exact=False):
    dn = {"nn": (((1,), (0,)), ((), ())), "nt": (((1,), (1,)), ((), ())), "tn": (((0,), (0,)), ((), ()))}[dims]
    if exact:
        return lax.dot_general(a.astype(f32), b.astype(f32), dn, precision=HIGHEST, preferred_element_type=f32)
    return lax.dot_general(a.astype(bf16), b.astype(bf16), dn, preferred_element_type=f32)


def _dot3(a, b):
    a_hi, b_hi = a.astype(bf16), b.astype(bf16)
    a_lo = (a - a_hi.astype(f32)).astype(bf16)
    b_lo = (b - b_hi.astype(f32)).astype(bf16)
    return _dot(a_hi, b_hi) + (_dot(a_hi, b_lo) + _dot(a_lo, b_hi))


def _two_pass(x, m):
    x_hi = x.astype(bf16)
    x_lo = (x - x_hi.astype(f32)).astype(bf16)
    return _dot(x_hi, m) + _dot(x_lo, m)


@jax.custom_vjp
def _dot_const(x, m):
    return _two_pass(x, m)


def _dot_const_fwd(x, m):
    return _two_pass(x, m), m


def _dot_const_bwd(m, ct):
    return _two_pass(ct, m.T), jnp.zeros_like(m)


_dot_const.defvjp(_dot_const_fwd, _dot_const_bwd)


def _mm(a, b, mode, *, name, out_dtype=f32, bias=None, add=None, zero_rows_below=0):
    m, _ = a.shape
    n = b.shape[1] if mode == "nn" else b.shape[0]
    tm = _tile(m)
    has_bias, has_add = bias is not None, add is not None

    def body(*refs):
        a_ref, b_ref = refs[0], refs[1]
        o_ref = refs[-1]
        acc = _dot(a_ref[...], b_ref[...], mode)
        k = 2
        if has_bias:
            acc = acc + refs[k][...]
            k += 1
        if zero_rows_below:
            rows = pl.program_id(0) * tm + lax.broadcasted_iota(jnp.int32, acc.shape, 0)
            acc = jnp.where(rows >= zero_rows_below, acc, 0.0)
        if has_add:
            acc = acc + refs[k][...].astype(f32)
        o_ref[...] = acc.astype(out_dtype)

    ins = [a, b]
    in_specs = [pl.BlockSpec((tm, a.shape[1]), lambda i: (i, 0)), _full(b.shape)]
    if has_bias:
        ins.append(bias)
        in_specs.append(_full(bias.shape))
    if has_add:
        ins.append(add)
        in_specs.append(pl.BlockSpec((tm, n), lambda i: (i, 0)))
    return pl.pallas_call(
        body, name=name, grid=(m // tm,), in_specs=in_specs,
        out_specs=pl.BlockSpec((tm, n), lambda i: (i, 0)),
        out_shape=jax.ShapeDtypeStruct((m, n), out_dtype),
        compiler_params=_params(("parallel",)),
    )(*ins)


def _mm_tn(a, b, *, name, colsum=False, out_dtype=bf16):
    r, m = a.shape
    n = b.shape[1]
    tr = _tile(r, 1408)
    tmo = m
    for cand in (1408, 1024, 768, 512):
        if m > 1024 and m % cand == 0:
            tmo = cand
            break
    steps = r // tr

    def body(a_ref, b_ref, o_ref, *rest):
        acc = rest[-1]
        i = pl.program_id(1)

        @pl.when(i == 0)
        def _():
            acc[...] = jnp.zeros_like(acc)
            if colsum:
                rest[0][...] = jnp.zeros_like(rest[0])

        acc[...] += _dot(a_ref[...], b_ref[...], "tn")
        if colsum:
            rest[0][...] += jnp.sum(a_ref[...].astype(f32), axis=0, keepdims=True)

        @pl.when(i == steps - 1)
        def _():
            o_ref[...] = acc[...].astype(out_dtype)

    out_shape = [jax.ShapeDtypeStruct((m, n), out_dtype)]
    out_specs = [pl.BlockSpec((tmo, n), lambda j, i: (j, 0))]
    if colsum:
        out_shape.append(jax.ShapeDtypeStruct((1, m), f32))
        out_specs.append(pl.BlockSpec((1, tmo), lambda j, i: (0, j)))
    res = pl.pallas_call(
        body, name=name, grid=(m // tmo, steps),
        in_specs=[pl.BlockSpec((tr, tmo), lambda j, i: (i, j)), pl.BlockSpec((tr, n), lambda j, i: (i, 0))],
        out_specs=out_specs, out_shape=out_shape,
        scratch_shapes=[pltpu.VMEM((tmo, n), f32)],
        compiler_params=_params(("parallel", "arbitrary")),
    )(a, b)
    return res if colsum else res[0]


def _rowwise(fn, rows, params, outs, *, name, tm=None, with_row0=False):
    m = rows[0].shape[0]
    tm = tm or _tile(m)
    nr, npar = len(rows), len(params)

    def body(*refs):
        vals = [r[...] for r in refs[:nr + npar]]
        kw = dict(row0=pl.program_id(0) * tm) if with_row0 else {}
        res = fn(*vals, **kw)
        for o_ref, v in zip(refs[nr + npar:], res):
            o_ref[...] = v.astype(o_ref.dtype)

    return pl.pallas_call(
        body, name=name, grid=(m // tm,),
        in_specs=[pl.BlockSpec((tm, r.shape[1]), lambda i: (i, 0)) for r in rows] + [_full(p.shape) for p in params],
        out_specs=[pl.BlockSpec((tm, w), lambda i: (i, 0)) for w, _ in outs],
        out_shape=[jax.ShapeDtypeStruct((m, w), dt) for w, dt in outs],
        compiler_params=_params(("parallel",)),
    )(*rows, *params)


def _rowwise_bwd(fn, rows, params, cts, *, name, diff_rows, diff_params, tm=None, with_row0=False, zero_rows_below=0,
                 out_dtypes=None):
    m = rows[0].shape[0]
    tm = tm or _tile(m)
    nr, npar = len(rows), len(params)
    d_idx = [i for i in range(nr) if diff_rows[i]]
    p_idx = [i for i in range(npar) if diff_params[i]]
    out_dtypes = out_dtypes or [f32] * len(d_idx)
    flat_cts = [c for group in cts for c in group]
    n_ct = len(flat_cts)

    def body(*refs):
        vals = [r[...] for r in refs[:nr + npar]]
        ct_refs = refs[nr + npar:nr + npar + n_ct]
        out_refs = refs[nr + npar + n_ct:]
        kw = dict(row0=pl.program_id(0) * tm) if with_row0 else {}
        ct_vals, k = [], 0
        for group in cts:
            acc = ct_refs[k][...].astype(f32)
            for extra in range(1, len(group)):
                acc = acc + ct_refs[k + extra][...].astype(f32)
            k += len(group)
            if zero_rows_below:
                rr = pl.program_id(0) * tm + lax.broadcasted_iota(jnp.int32, acc.shape, 0)
                acc = jnp.where(rr >= zero_rows_below, acc, 0.0)
            ct_vals.append(acc)

        def g(*dargs):
            full = list(vals)
            for pos, i in enumerate(d_idx):
                full[i] = dargs[pos]
            for pos, i in enumerate(p_idx):
                full[nr + i] = dargs[len(d_idx) + pos]
            return tuple(fn(*full, **kw))

        _, vjp = jax.vjp(g, *[vals[i].astype(f32) for i in d_idx], *[vals[nr + i] for i in p_idx])
        grads = vjp(tuple(ct_vals))
        for pos in range(len(d_idx)):
            out_refs[pos][...] = grads[pos].astype(out_refs[pos].dtype)
        first = pl.program_id(0) == 0
        for pos in range(len(p_idx)):
            o_ref = out_refs[len(d_idx) + pos]

            @pl.when(first)
            def _(o_ref=o_ref):
                o_ref[...] = jnp.zeros_like(o_ref)

            o_ref[...] += grads[len(d_idx) + pos]

    return pl.pallas_call(
        body, name=name, grid=(m // tm,),
        in_specs=[pl.BlockSpec((tm, r.shape[1]), lambda i: (i, 0)) for r in rows] + [_full(p.shape) for p in params]
        + [pl.BlockSpec((tm, c.shape[1]), lambda i: (i, 0)) for c in flat_cts],
        out_specs=[pl.BlockSpec((tm, rows[i].shape[1]), lambda i_: (i_, 0)) for i in d_idx]
        + [_full(params[i].shape) for i in p_idx],
        out_shape=[jax.ShapeDtypeStruct(rows[i].shape, dt) for i, dt in zip(d_idx, out_dtypes)]
        + [jax.ShapeDtypeStruct(params[i].shape, f32) for i in p_idx],
        compiler_params=_params(("arbitrary",)),
    )(*rows, *params, *flat_cts)


def _rms(x, g):
    return x * lax.rsqrt(jnp.mean(x * x, axis=-1, keepdims=True) + RMS_EPS) * g


def _head_sum_matrix(width, head):
    idx = jnp.arange(width) // head
    return (idx[:, None] == idx[None, :]).astype(f32)


def _rope_tables(lp):
    half = ROPE_DIM // 2
    pos = (jnp.arange(lp) - PAD).astype(f32)
    inv_freq = jnp.power(jnp.float32(ROPE_THETA), -jnp.arange(half, dtype=f32) * (2.0 / ROPE_DIM))
    ang = pos[:, None] * inv_freq[None, :]
    cos, sin = jnp.cos(ang), jnp.sin(ang)
    ones = jnp.ones((lp, HEAD_DIM - ROPE_DIM), f32)
    zeros = jnp.zeros((lp, HEAD_DIM - ROPE_DIM), f32)
    cos_t = jnp.concatenate([cos, cos, ones], axis=1)
    sin_t = jnp.concatenate([-sin, sin, zeros], axis=1)
    i = jnp.arange(HEAD_DIM)
    src = jnp.where(i < half, i + half, jnp.where(i < ROPE_DIM, i - half, i))
    swap = ((i[:, None] == src[None, :]) & (i[None, :] < ROPE_DIM)).astype(f32)
    return cos_t, sin_t, swap


def _attn_prep(qkv, cos_t, sin_t, swap):
    outs = []
    for h in range(Q_HEADS + KV_HEADS):
        t = qkv[:, h * HEAD_DIM:(h + 1) * HEAD_DIM]
        outs.append(t * cos_t + _dot_const(t, swap) * sin_t)
    q = jnp.concatenate(outs[:Q_HEADS], axis=1)
    k = jnp.concatenate(outs[Q_HEADS:], axis=1)
    return q, k, qkv[:, Q_W + KV_W:]


def _softplus(z):
    return jnp.maximum(z, 0.0) + jnp.log1p(jnp.exp(-jnp.abs(z)))


def _rwkv_prep(rkv, lora, w0, w2, a0, a2, g2, k_k, k_a, hsum):
    r = rkv[:, :RWKV_DIM]
    k = rkv[:, RWKV_DIM:2 * RWKV_DIM]
    v = rkv[:, 2 * RWKV_DIM:]
    dw = lora[:, :DECAY_LORA]
    da = lora[:, DECAY_LORA:DECAY_LORA + AAA_LORA]
    dg = lora[:, DECAY_LORA + AAA_LORA:]
    w = -_softplus(-(w0 + _dot(jnp.tanh(dw), w2))) - 0.5
    a = jax.nn.sigmoid(a0 + _dot(da, a2))
    g = _dot(jax.nn.sigmoid(dg), g2)
    kk = k * k_k
    kk = kk * lax.rsqrt(jnp.maximum(_dot_const(kk * kk, hsum), 1e-24))
    k = k * (1.0 + (a - 1.0) * k_a)
    log_decay = -jnp.exp(w)
    return r, log_decay, k, v, -kk, kk * a, g


def _rwkv_post(y, r, k, v, g, ln_w, ln_b, r_k, hmean):
    hsum = hmean * RWKV_HEAD
    mean = _dot_const(y, hmean)
    yc = y - mean
    var = _dot_const(yc * yc, hmean)
    yn = yc * lax.rsqrt(var + RWKV_LN_EPS) * ln_w + ln_b
    bonus = _dot_const(r * k * r_k, hsum) * v
    return ((yn + bonus) * g,)


def _merge(gates, br_a, br_r):
    sg = jax.nn.sigmoid(gates)
    return (sg[:, :D_MODEL] * br_a + sg[:, D_MODEL:] * br_r,)


def _swiglu(gate, up):
    return (jax.nn.silu(gate) * up,)


def _ffn_in(f, w_gate_t, w_up_t, *, name):
    m, d = f.shape
    n = w_gate_t.shape[0]
    tm = _tile(m)

    def body(f_ref, wg_ref, wu_ref, g_ref, u_ref, a_ref):
        g = _dot(f_ref[...], wg_ref[...], "nt")
        u = _dot(f_ref[...], wu_ref[...], "nt")
        g_ref[...] = g.astype(g_ref.dtype)
        u_ref[...] = u.astype(u_ref.dtype)
        a_ref[...] = _swiglu(g, u)[0].astype(a_ref.dtype)

    spec = pl.BlockSpec((tm, n), lambda i: (i, 0))
    return pl.pallas_call(
        body, name=name, grid=(m // tm,),
        in_specs=[pl.BlockSpec((tm, d), lambda i: (i, 0)), _full(w_gate_t.shape), _full(w_up_t.shape)],
        out_specs=[spec] * 3, out_shape=[jax.ShapeDtypeStruct((m, n), bf16)] * 3,
        compiler_params=_params(("parallel",)),
    )(f, w_gate_t, w_up_t)


def _branch_merge(y_attn, y_rwkv, w_attn_t, w_rwkv_t, gates, *, name):
    m = y_attn.shape[0]
    tm = _tile(m)

    def body(ya_ref, yr_ref, wa_ref, wr_ref, g_ref, a_ref, r_ref, o_ref):
        br_a = _dot(ya_ref[...], wa_ref[...], "nt")
        br_r = _dot(yr_ref[...], wr_ref[...], "nt")
        a_ref[...] = br_a.astype(a_ref.dtype)
        r_ref[...] = br_r.astype(r_ref.dtype)
        o_ref[...] = _merge(g_ref[...], br_a, br_r)[0].astype(o_ref.dtype)

    rows = lambda a: pl.BlockSpec((tm, a.shape[1]), lambda i: (i, 0))
    spec = pl.BlockSpec((tm, D_MODEL), lambda i: (i, 0))
    return pl.pallas_call(
        body, name=name, grid=(m // tm,),
        in_specs=[rows(y_attn), rows(y_rwkv), _full(w_attn_t.shape), _full(w_rwkv_t.shape), rows(gates)],
        out_specs=[spec] * 3, out_shape=[jax.ShapeDtypeStruct((m, D_MODEL), bf16)] * 3,
        compiler_params=_params(("parallel",)),
    )(y_attn, y_rwkv, w_attn_t, w_rwkv_t, gates)


def _branch_merge_bwd(dh, w_o, gates, br_a, br_r, *, name):
    m = dh.shape[0]
    tm = _tile(m)

    def body(dh_ref, w_ref, g_ref, a_ref, r_ref, dg_ref, da_ref, dr_ref):
        dmerged = _dot(dh_ref[...], w_ref[...], "nt")
        _, vjp = jax.vjp(lambda g, a, r: _merge(g, a, r)[0], g_ref[...], a_ref[...].astype(f32),
                         r_ref[...].astype(f32))
        dg, da, dr = vjp(dmerged)
        dg_ref[...] = dg.astype(dg_ref.dtype)
        da_ref[...] = da.astype(da_ref.dtype)
        dr_ref[...] = dr.astype(dr_ref.dtype)

    rows = lambda a: pl.BlockSpec((tm, a.shape[1]), lambda i: (i, 0))
    return pl.pallas_call(
        body, name=name, grid=(m // tm,),
        in_specs=[rows(dh), _full(w_o.shape), rows(gates), rows(br_a), rows(br_r)],
        out_specs=[rows(gates), rows(br_a), rows(br_r)],
        out_shape=[jax.ShapeDtypeStruct(gates.shape, bf16), jax.ShapeDtypeStruct(br_a.shape, bf16),
                   jax.ShapeDtypeStruct(br_r.shape, bf16)],
        compiler_params=_params(("parallel",)),
    )(dh, w_o, gates, br_a, br_r)


def _ffn_in_bwd(dh, w_down, gate, up, *, name):
    m, d = dh.shape
    n = w_down.shape[0]
    tm = _tile(m)

    def body(dh_ref, w_ref, g_ref, u_ref, dg_ref, du_ref):
        dact = _dot(dh_ref[...], w_ref[...], "nt")
        _, vjp = jax.vjp(lambda a, b: _swiglu(a, b)[0], g_ref[...].astype(f32), u_ref[...].astype(f32))
        dg, du = vjp(dact)
        dg_ref[...] = dg.astype(dg_ref.dtype)
        du_ref[...] = du.astype(du_ref.dtype)

    spec = pl.BlockSpec((tm, n), lambda i: (i, 0))
    return pl.pallas_call(
        body, name=name, grid=(m // tm,),
        in_specs=[pl.BlockSpec((tm, d), lambda i: (i, 0)), _full(w_down.shape), spec, spec],
        out_specs=[spec] * 2, out_shape=[jax.ShapeDtypeStruct((m, n), bf16)] * 2,
        compiler_params=_params(("parallel",)),
    )(dh, w_down, gate, up)


def _token_shift(p, mix, *, name):
    m, c = p.shape
    tm = _tile(m)
    sub = tm // 8

    def body(p_ref, prev_ref, mix_ref, o_ref):
        x = p_ref[...]
        rows = lax.broadcasted_iota(jnp.int32, x.shape, 0)
        last = jnp.where(pl.program_id(0) == 0, 0.0, prev_ref[7:8, :])
        xp = jnp.where(rows == 0, last, pltpu.roll(x, 1, axis=0))
        o_ref[...] = x + (xp - x) * mix_ref[...]

    return pl.pallas_call(
        body, name=name, grid=(m // tm,),
        in_specs=[pl.BlockSpec((tm, c), lambda i: (i, 0)),
                  pl.BlockSpec((8, c), lambda i: (jnp.maximum(i * sub - 1, 0), 0)),
                  _full(mix.shape)],
        out_specs=pl.BlockSpec((tm, c), lambda i: (i, 0)),
        out_shape=jax.ShapeDtypeStruct((m, c), f32),
        compiler_params=_params(("parallel",)),
    )(p, p, mix)


def _token_shift_bwd(p, mix, dpf, *, name):
    m, c = p.shape
    tm = _tile(m)
    sub = tm // 8
    n_tiles = m // tm

    def body(p_ref, prev_ref, mix_ref, d_ref, nxt_ref, dp_ref, dmix_ref):
        i = pl.program_id(0)
        x = p_ref[...]
        d = d_ref[...]
        mixv = mix_ref[...]
        rows = lax.broadcasted_iota(jnp.int32, x.shape, 0)
        last = jnp.where(i == 0, 0.0, prev_ref[7:8, :])
        xp = jnp.where(rows == 0, last, pltpu.roll(x, 1, axis=0))
        dm = d * mixv
        first_next = jnp.where(i == n_tiles - 1, 0.0, nxt_ref[0:1, :] * mixv)
        dm_next = jnp.where(rows == tm - 1, first_next, pltpu.roll(dm, tm - 1, axis=0))
        dp = d - dm + dm_next
        dp_ref[...] = jnp.where(i * tm + rows >= PAD, dp, 0.0).astype(dp_ref.dtype)

        @pl.when(i == 0)
        def _():
            dmix_ref[...] = jnp.zeros_like(dmix_ref)

        dmix_ref[...] += jnp.sum(d * (xp - x), axis=0, keepdims=True)

    return pl.pallas_call(
        body, name=name, grid=(n_tiles,),
        in_specs=[pl.BlockSpec((tm, c), lambda i: (i, 0)),
                  pl.BlockSpec((8, c), lambda i: (jnp.maximum(i * sub - 1, 0), 0)),
                  _full(mix.shape),
                  pl.BlockSpec((tm, c), lambda i: (i, 0)),
                  pl.BlockSpec((8, c), lambda i: (jnp.minimum((i + 1) * sub, m // 8 - 1), 0))],
        out_specs=[pl.BlockSpec((tm, c), lambda i: (i, 0)), _full(mix.shape)],
        out_shape=[jax.ShapeDtypeStruct((m, c), bf16), jax.ShapeDtypeStruct(mix.shape, f32)],
        compiler_params=_params(("arbitrary",)),
    )(p, p, mix, dpf, dpf)


def _attn_masks(blk):
    qi = lax.broadcasted_iota(jnp.int32, (BLOCK, BLOCK), 0)
    ki = lax.broadcasted_iota(jnp.int32, (BLOCK, BLOCK), 1)
    qpos = blk * BLOCK + qi - PAD
    kpos_c = blk * BLOCK + ki - PAD
    kpos_p = kpos_c - BLOCK
    kpos_m = ki - PAD

    def band(kpos):
        return (kpos >= N_META) & (kpos <= qpos) & (qpos - kpos < WINDOW)

    return band(kpos_p), band(kpos_c), (kpos_m >= 0) & (kpos_m <= qpos)


def _attn_probs(qs, k3s, sink, oks):
    s = [[jnp.where(ok, _dot(qh, kx, "nt"), NEG_INF) for kx, ok in zip(k3, oks)] for qh, k3 in zip(qs, k3s)]
    mx = [jnp.maximum(jnp.maximum(jnp.max(t[0], -1, keepdims=True), jnp.max(t[1], -1, keepdims=True)),
                      jnp.maximum(jnp.max(t[2], -1, keepdims=True), sk)) for t, sk in zip(s, sink)]
    e = [[jnp.exp(tx - m) for tx in t] for t, m in zip(s, mx)]
    e_sink = [jnp.exp(sk - m) for sk, m in zip(sink, mx)]
    inv = [1.0 / (jnp.sum(t[0], -1, keepdims=True) + jnp.sum(t[1], -1, keepdims=True)
                  + jnp.sum(t[2], -1, keepdims=True) + es) for t, es in zip(e, e_sink)]
    return [[tx * i for tx in t] for t, i in zip(e, inv)], [es * i for es, i in zip(e_sink, inv)]


def _head_cols(i):
    return slice(i * HEAD_DIM, (i + 1) * HEAD_DIM)


def _attn_operands(refs):
    q_ref, kp_ref, kc_ref, km_ref, vp_ref, vc_ref, vm_ref, s_ref = refs
    qs = [q_ref[:, _head_cols(i)] * (HEAD_DIM ** -0.5) for i in range(Q_HEADS)]
    k3 = [[ref[:, _head_cols(h)] for ref in (kp_ref, kc_ref, km_ref)] for h in range(KV_HEADS)]
    v3 = [[ref[:, _head_cols(h)] for ref in (vp_ref, vc_ref, vm_ref)] for h in range(KV_HEADS)]
    return (qs, [k3[i // GROUP] for i in range(Q_HEADS)], [v3[i // GROUP] for i in range(Q_HEADS)],
            [s_ref[:, i:i + 1] for i in range(Q_HEADS)])


def _attention(q, k, v, sinks, *, name):
    lp = q.shape[0]
    nb = lp // BLOCK
    prev = lambda i: (jnp.maximum(i - 1, 0), 0)
    cur = lambda i: (i, 0)
    meta = lambda i: (0, 0)
    kv = lambda index: pl.BlockSpec((BLOCK, KV_W), index)

    def body(*refs):
        o_ref = refs[-1]
        qs, k3s, v3s, sink = _attn_operands(refs[:-1])
        p, _ = _attn_probs(qs, k3s, sink, _attn_masks(pl.program_id(0)))
        out = [_dot(ph[0], v3[0]) + _dot(ph[1], v3[1]) + _dot(ph[2], v3[2]) for ph, v3 in zip(p, v3s)]
        for i in range(Q_HEADS):
            o_ref[:, _head_cols(i)] = out[i].astype(o_ref.dtype)

    return pl.pallas_call(
        body, name=name, grid=(nb,),
        in_specs=[pl.BlockSpec((BLOCK, Q_W), cur), kv(prev), kv(cur), kv(meta), kv(prev), kv(cur), kv(meta),
                  _full((1, Q_HEADS))],
        out_specs=pl.BlockSpec((BLOCK, Q_W), cur),
        out_shape=jax.ShapeDtypeStruct((lp, Q_W), bf16),
        compiler_params=_params(("parallel",)),
    )(q, k, k, k, v, v, v, sinks)


def _attention_bwd(q, k, v, sinks, do, *, name):
    lp = q.shape[0]
    nb = lp // BLOCK
    cur = lambda n: (jnp.minimum(n, nb - 1), 0)
    prev = lambda n: (jnp.maximum(jnp.minimum(n, nb - 1) - 1, 0), 0)
    behind = lambda n: (jnp.maximum(n - 1, 0), 0)
    meta = lambda n: (0, 0)
    kv = lambda index: pl.BlockSpec((BLOCK, KV_W), index)
    scale = HEAD_DIM ** -0.5

    def body(*refs):
        ins, do_ref = refs[:8], refs[8]
        dq_ref, dk_ref, dv_ref, dkm_ref, dvm_ref, ds_ref, carry_k, carry_v = refs[9:]
        n = pl.program_id(0)

        @pl.when(n == 0)
        def _():
            for ref in (dkm_ref, dvm_ref, ds_ref, carry_k, carry_v):
                ref[...] = jnp.zeros_like(ref)

        @pl.when(n < nb)
        def _():
            qs, k3s, v3s, sink = _attn_operands(ins)
            do = [do_ref[:, _head_cols(i)] for i in range(Q_HEADS)]
            p, p_sink = _attn_probs(qs, k3s, sink, _attn_masks(n))
            out = [_dot(ph[0], v3[0]) + _dot(ph[1], v3[1]) + _dot(ph[2], v3[2]) for ph, v3 in zip(p, v3s)]
            delta = [jnp.sum(d * o, -1, keepdims=True) for d, o in zip(do, out)]
            dp = [[_dot(d, vx, "nt") for vx in v3] for d, v3 in zip(do, v3s)]
            ds = [[px * (dx - dl) for px, dx in zip(ph, dh)] for ph, dh, dl in zip(p, dp, delta)]
            dq = [_dot(dsh[0], k3[0]) + _dot(dsh[1], k3[1]) + _dot(dsh[2], k3[2]) for dsh, k3 in zip(ds, k3s)]
            for i in range(Q_HEADS):
                dq_ref[:, _head_cols(i)] = dq[i] * scale
                ds_ref[:, i:i + 1] -= jnp.sum(p_sink[i] * delta[i], axis=0, keepdims=True)
            for h in range(KV_HEADS):
                group = slice(h * GROUP, (h + 1) * GROUP)
                q_all = jnp.concatenate(qs[group], axis=0)
                do_all = jnp.concatenate(do[group], axis=0)
                dk3 = [_dot(jnp.concatenate([dsh[x] for dsh in ds[group]], axis=0), q_all, "tn") for x in range(3)]
                dv3 = [_dot(jnp.concatenate([ph[x] for ph in p[group]], axis=0), do_all, "tn") for x in range(3)]
                hs = _head_cols(h)
                for out_ref, carry, meta_ref, d3 in ((dk_ref, carry_k, dkm_ref, dk3),
                                                     (dv_ref, carry_v, dvm_ref, dv3)):
                    out_ref[:, hs] = carry[:, hs] + d3[0]
                    carry[:, hs] = d3[1]
                    meta_ref[:, hs] += d3[2]

        @pl.when(n == nb)
        def _():
            dk_ref[...] = carry_k[...]
            dv_ref[...] = carry_v[...]

    kv_shape = jax.ShapeDtypeStruct((lp, KV_W), f32)
    one_shape = jax.ShapeDtypeStruct((BLOCK, KV_W), f32)
    return pl.pallas_call(
        body, name=name, grid=(nb + 1,),
        in_specs=[pl.BlockSpec((BLOCK, Q_W), cur), kv(prev), kv(cur), kv(meta), kv(prev), kv(cur), kv(meta),
                  _full((1, Q_HEADS)), pl.BlockSpec((BLOCK, Q_W), cur)],
        out_specs=[pl.BlockSpec((BLOCK, Q_W), cur), kv(behind), kv(behind), kv(meta), kv(meta),
                   _full((1, Q_HEADS))],
        out_shape=[jax.ShapeDtypeStruct((lp, Q_W), f32), kv_shape, kv_shape, one_shape, one_shape,
                   jax.ShapeDtypeStruct((1, Q_HEADS), f32)],
        scratch_shapes=[pltpu.VMEM((BLOCK, KV_W), f32), pltpu.VMEM((BLOCK, KV_W), f32)],
        compiler_params=_params(("arbitrary",)),
    )(q, k, k, k, v, v, v, sinks, do)


@jax.custom_vjp
def _known_inverse(l, x):
    return x


def _known_inverse_fwd(l, x):
    return x, x


def _known_inverse_bwd(x, ct):
    return _dot(_dot(x, ct, "tn"), x, "nt"), jnp.zeros_like(x)


_known_inverse.defvjp(_known_inverse_fwd, _known_inverse_bwd)


def _scan_chunk(s0, r, lw, k, v, a, b, inv=None):
    t = r[0].shape[0]
    ii = lax.broadcasted_iota(jnp.int32, (t, t), 0)
    jj = lax.broadcasted_iota(jnp.int32, (t, t), 1)
    incl = jj <= ii
    strict = jj < ii
    tri = incl.astype(f32)
    eye = jnp.where(ii == jj, 1.0, 0.0)
    cl = [_dot3(tri, x) for x in lw]
    e_pos = [jnp.exp(c) for c in cl]
    e_neg = [jnp.exp(-c) for c in cl]
    e_prev = [jnp.exp(c - x) for c, x in zip(cl, lw)]
    rt = [x * e for x, e in zip(r, e_pos)]
    at = [x * e for x, e in zip(a, e_prev)]
    bt = [x * e for x, e in zip(b, e_neg)]
    kt = [x * e for x, e in zip(k, e_neg)]
    l_ab = [jnp.where(strict, _dot(x, y, "nt"), 0.0) for x, y in zip(at, bt)]
    l_ak = [jnp.where(strict, _dot(x, y, "nt"), 0.0) for x, y in zip(at, kt)]
    r_b = [jnp.where(incl, _dot(x, y, "nt"), 0.0) for x, y in zip(rt, bt)]
    r_k = [jnp.where(incl, _dot(x, y, "nt"), 0.0) for x, y in zip(rt, kt)]
    if inv is None:
        inv = [eye + x for x in l_ab]
        pw = l_ab
        for _ in range(int(math.log2(t)) - 1):
            pw = [_dot(x, x) for x in pw]
            inv = [x + _dot(x, y) for x, y in zip(inv, pw)]
    else:
        inv = [_known_inverse(x, y) for x, y in zip(l_ab, inv)]
    rhs = [_dot(x, s, "nt") + _dot(m, y) for x, s, m, y in zip(at, s0, l_ak, v)]
    u = [_dot(x, y) for x, y in zip(inv, rhs)]
    y_s = [_dot(x, s, "nt") for x, s in zip(rt, s0)]
    y = [ys + _dot(m, uu) + _dot(n, vv) for ys, m, uu, n, vv in zip(y_s, r_b, u, r_k, v)]
    grow = [s + _dot(uu, x, "tn") + _dot(vv, z, "tn") for s, uu, x, vv, z in zip(s0, u, bt, v, kt)]
    s1 = [g * e[t - 1:t, :] for g, e in zip(grow, e_pos)]
    return y, s1, inv


def _head_rows(h):
    return slice(h * RWKV_HEAD, (h + 1) * RWKV_HEAD)


def _per_head(ref):
    return [ref[:, _head_rows(h)] for h in range(RWKV_HEADS)]


def _scan(r, lw, k, v, a, b, *, name):
    lp = r.shape[0]
    nc = lp // CHUNK
    row = pl.BlockSpec((CHUNK, RWKV_DIM), lambda c: (c, 0))

    def body(r_ref, lw_ref, k_ref, v_ref, a_ref, b_ref, y_ref, s_ref, inv_ref, state):
        @pl.when(pl.program_id(0) == 0)
        def _():
            state[...] = jnp.zeros_like(state)

        s_ref[...] = state[...]
        s0 = [state[_head_rows(h), :] for h in range(RWKV_HEADS)]
        y, s1, inv = _scan_chunk(s0, *[_per_head(ref) for ref in (r_ref, lw_ref, k_ref, v_ref, a_ref, b_ref)])
        for h in range(RWKV_HEADS):
            y_ref[:, _head_rows(h)] = y[h]
            state[_head_rows(h), :] = s1[h]
            inv_ref[h * CHUNK:(h + 1) * CHUNK, :] = inv[h].astype(inv_ref.dtype)

    return pl.pallas_call(
        body, name=name, grid=(nc,), in_specs=[row] * 6,
        out_specs=[row, pl.BlockSpec((RWKV_DIM, RWKV_HEAD), lambda c: (c, 0)),
                   pl.BlockSpec((RWKV_HEADS * CHUNK, CHUNK), lambda c: (c, 0))],
        out_shape=[jax.ShapeDtypeStruct((lp, RWKV_DIM), f32), jax.ShapeDtypeStruct((nc * RWKV_DIM, RWKV_HEAD), f32),
                   jax.ShapeDtypeStruct((nc * RWKV_HEADS * CHUNK, CHUNK), bf16)],
        scratch_shapes=[pltpu.VMEM((RWKV_DIM, RWKV_HEAD), f32)],
        compiler_params=_params(("arbitrary",)),
    )(r, lw, k, v, a, b)


def _scan_bwd(r, lw, k, v, a, b, states, inverses, dy, *, name):
    lp = r.shape[0]
    nc = lp // CHUNK
    back = lambda c: (nc - 1 - c, 0)
    row = pl.BlockSpec((CHUNK, RWKV_DIM), back)

    def body(r_ref, lw_ref, k_ref, v_ref, a_ref, b_ref, s_ref, inv_ref, dy_ref,
             dr_ref, dlw_ref, dk_ref, dv_ref, da_ref, db_ref, dstate):
        @pl.when(pl.program_id(0) == 0)
        def _():
            dstate[...] = jnp.zeros_like(dstate)

        outs = (dr_ref, dlw_ref, dk_ref, dv_ref, da_ref, db_ref)
        s0 = [s_ref[_head_rows(h), :] for h in range(RWKV_HEADS)]
        inv = [inv_ref[h * CHUNK:(h + 1) * CHUNK, :].astype(f32) for h in range(RWKV_HEADS)]
        _, vjp = jax.vjp(lambda *args: _scan_chunk(*args, inv=inv)[:2], s0,
                         *[_per_head(ref) for ref in (r_ref, lw_ref, k_ref, v_ref, a_ref, b_ref)])
        g = vjp((_per_head(dy_ref), [dstate[_head_rows(h), :] for h in range(RWKV_HEADS)]))
        for h in range(RWKV_HEADS):
            dstate[_head_rows(h), :] = g[0][h]
            for o_ref, gv in zip(outs, g[1:]):
                o_ref[:, _head_rows(h)] = gv[h]

    shape = jax.ShapeDtypeStruct((lp, RWKV_DIM), f32)
    return pl.pallas_call(
        body, name=name, grid=(nc,),
        in_specs=[row] * 6 + [pl.BlockSpec((RWKV_DIM, RWKV_HEAD), back),
                              pl.BlockSpec((RWKV_HEADS * CHUNK, CHUNK), back), row],
        out_specs=[row] * 6, out_shape=[shape] * 6,
        scratch_shapes=[pltpu.VMEM((RWKV_DIM, RWKV_HEAD), f32)],
        compiler_params=_params(("arbitrary",)),
    )(r, lw, k, v, a, b, states, inverses, dy)


def _loss_head(h2, target, g_final, *, name):
    lp = h2.shape[0]
    tm = BLOCK
    front_tiles = FRONT // tm

    def body(h_ref, t_ref, g_ref, loss_ref, dh_ref, dg_ref):
        i = pl.program_id(0)
        real = i >= front_tiles

        def tile_loss(hv, gv):
            err = _rms(hv, gv) - t_ref[...]
            return jnp.where(real, 0.5 * jnp.sum(jnp.mean(err * err, axis=-1, keepdims=True)), 0.0)

        loss, (dh, dg) = jax.value_and_grad(tile_loss, argnums=(0, 1))(h_ref[...], g_ref[...])

        @pl.when(i == 0)
        def _():
            loss_ref[...] = jnp.zeros_like(loss_ref)
            dg_ref[...] = jnp.zeros_like(dg_ref)

        loss_ref[...] += jnp.full(loss_ref.shape, loss, f32)
        dg_ref[...] += dg
        dh_ref[...] = dh

    return pl.pallas_call(
        body, name=name, grid=(lp // tm,),
        in_specs=[pl.BlockSpec((tm, D_MODEL), lambda i: (i, 0)),
                  pl.BlockSpec((tm, D_MODEL), lambda i: (jnp.maximum(i - front_tiles, 0), 0)),
                  _full(g_final.shape)],
        out_specs=[_full((8, 128)), pl.BlockSpec((tm, D_MODEL), lambda i: (i, 0)), _full(g_final.shape)],
        out_shape=[jax.ShapeDtypeStruct((8, 128), f32), jax.ShapeDtypeStruct((lp, D_MODEL), f32),
                   jax.ShapeDtypeStruct(g_final.shape, f32)],
        compiler_params=_params(("arbitrary",)),
    )(h2, target, g_final)


def _local_step(x, target, meta, p, late_weights=None, emit=None):
    emit = emit or (lambda group, grads: 0.0)
    seq = x.shape[0]
    lp = seq + FRONT
    h0 = jnp.concatenate([jnp.zeros((PAD, D_MODEL), f32), meta, x], axis=0)
    cos_t, sin_t, swap = _rope_tables(lp)
    hsum = _head_sum_matrix(RWKV_DIM, RWKV_HEAD)
    hmean = hsum / RWKV_HEAD
    w_qkv_t, w_rkv_t = p["w_in_t"][:ATTN_PROJ], p["w_in_t"][ATTN_PROJ:ATTN_PROJ + RKV_W]
    w_lora_t, w_gates_t = p["w_in_t"][ATTN_PROJ + RKV_W:ATTN_PROJ + RWKV_PROJ], p["w_in_t"][ATTN_PROJ + RWKV_PROJ:]
    b_qkv, b_rkv = p["b_in"][:, :ATTN_PROJ], p["b_in"][:, ATTN_PROJ:ATTN_PROJ + RKV_W]
    b_lora, b_gates = p["b_in"][:, ATTN_PROJ + RKV_W:ATTN_PROJ + RWKV_PROJ], p["b_in"][:, ATTN_PROJ + RWKV_PROJ:]
    prep_params = [p["w0"], p["w2"], p["a0"], p["a2"], p["g2"], p["k_k"], p["k_a"], hsum]
    post_params = [p["ln_w"], p["ln_b"], p["r_k"], hmean]

    (u,) = _rowwise(lambda hv, g: (_rms(hv, g),), [h0], [p["norm_mix_g"]], [(D_MODEL, bf16)], name="norm_mix")
    qkv = _mm(u, w_qkv_t, "nt", name="proj_qkv", bias=b_qkv, zero_rows_below=PAD)
    p_rkv = _mm(u, w_rkv_t, "nt", name="proj_rkv", bias=b_rkv, zero_rows_below=PAD)
    p_lora = _mm(u, w_lora_t, "nt", name="proj_lora", bias=b_lora, zero_rows_below=PAD)
    gates = _mm(u, w_gates_t, "nt", name="proj_gates", bias=b_gates, zero_rows_below=PAD)

    q, k, v = _rowwise(_attn_prep, [qkv, cos_t, sin_t], [swap], [(Q_W, bf16), (KV_W, bf16), (KV_W, bf16)],
                       name="attn_prep")
    y_attn = _attention(q, k, v, p["sinks"], name="attention")

    mix_rkv, mix_lora = p["mix"][:, :RKV_W], p["mix"][:, RKV_W:]
    pf_rkv = _token_shift(p_rkv, mix_rkv, name="shift_rkv")
    pf_lora = _token_shift(p_lora, mix_lora, name="shift_lora")
    wide = [(RWKV_DIM, f32)] * 7
    r_, lw_, k_, v_, a_, b_, g_ = _rowwise(_rwkv_prep, [pf_rkv, pf_lora], prep_params, wide, name="rwkv_prep")
    y_scan, states, inverses = _scan(r_, lw_, k_, v_, a_, b_, name="wkv_scan")
    (y_rwkv,) = _rowwise(_rwkv_post, [y_scan, r_, k_, v_, g_], post_params, [(RWKV_DIM, bf16)], name="rwkv_post")

    if late_weights is not None:
        p = {**p, **late_weights(y_rwkv)}
    br_a, br_r, merged = _branch_merge(y_attn, y_rwkv, p["w_br_attn_t"], p["w_br_rwkv_t"], gates, name="branch_merge")
    h1 = _mm(merged, p["w_o"], "nn", name="out_proj", add=h0)
    (f,) = _rowwise(lambda hv, g: (_rms(hv, g),), [h1], [p["norm_ffn_g"]], [(D_MODEL, bf16)], name="norm_ffn")
    gate, up, act = _ffn_in(f, p["w_gate_t"], p["w_up_t"], name="ffn_in")
    h2 = _mm(act, p["w_down"], "nn", name="ffn_down", add=h1)

    loss8, dh2, d_final_g = _loss_head(h2, target, p["norm_final_g"], name="loss_head")
    dgate, dup = _ffn_in_bwd(dh2, p["w_down"], gate, up, name="ffn_in_bwd")
    d_w_down = _mm_tn(act, dh2, name="dw_down")
    d_w_gate_t = _mm_tn(dgate, f, name="dw_gate")
    d_w_up_t = _mm_tn(dup, f, name="dw_up")
    zero = emit("ffn", dict(w_down=d_w_down, w_gate_t=d_w_gate_t, w_up_t=d_w_up_t))
    df = _mm(dgate, p["w_gate_t"], "nn", name="d_f_gate")
    df = _mm(dup, p["w_up_t"], "nn", name="d_f_up", add=df)
    dh1, d_ffn_g = _rowwise_bwd(lambda hv, g: (_rms(hv, g), hv), [h1], [p["norm_ffn_g"] + zero], [[df], [dh2]],
                                name="norm_ffn_bwd", diff_rows=[True], diff_params=[True])
    dgates, dbr_a, dbr_r = _branch_merge_bwd(dh1, p["w_o"], gates, br_a, br_r, name="branch_merge_bwd")
    d_w_o = _mm_tn(merged, dh1, name="dw_o")
    d_w_br_attn_t = _mm_tn(dbr_a, y_attn, name="dw_br_attn")
    d_w_br_rwkv_t = _mm_tn(dbr_r, y_rwkv, name="dw_br_rwkv")
    zero = emit("branch", dict(w_o=d_w_o, w_br_attn_t=d_w_br_attn_t, w_br_rwkv_t=d_w_br_rwkv_t))
    dy_attn = _mm(dbr_a, p["w_br_attn_t"], "nn", name="d_y_attn")
    dy_rwkv = _mm(dbr_r, p["w_br_rwkv_t"], "nn", name="d_y_rwkv")

    post_params = [p["ln_w"] + zero, p["ln_b"], p["r_k"], hmean]
    res = _rowwise_bwd(_rwkv_post, [y_scan, r_, k_, v_, g_], post_params, [[dy_rwkv]], name="rwkv_post_bwd",
                       diff_rows=[True] * 5, diff_params=[True, True, True, False])
    dy_scan, dr_p, dk_p, dv_p, dg_p, d_ln_w, d_ln_b, d_r_k = res
    dr_s, dlw_s, dk_s, dv_s, da_s, db_s = _scan_bwd(r_, lw_, k_, v_, a_, b_, states, inverses, dy_scan,
                                                    name="wkv_scan_bwd")
    res = _rowwise_bwd(_rwkv_prep, [pf_rkv, pf_lora], prep_params,
                       [[dr_s, dr_p], [dlw_s], [dk_s, dk_p], [dv_s, dv_p], [da_s], [db_s], [dg_p]],
                       name="rwkv_prep_bwd", diff_rows=[True, True], diff_params=[True] * 7 + [False],
                       zero_rows_below=PAD)
    dpf_rkv, dpf_lora, d_w0, d_w2, d_a0, d_a2, d_g2, d_k_k, d_k_a = res
    dp_rkv, d_mix_rkv = _token_shift_bwd(p_rkv, mix_rkv, dpf_rkv, name="shift_rkv_bwd")
    dp_lora, d_mix_lora = _token_shift_bwd(p_lora, mix_lora, dpf_lora, name="shift_lora_bwd")

    dq, dk, dv, dkm, dvm, d_sinks = _attention_bwd(q, k, v, p["sinks"], dy_attn, name="attention_bwd")
    rest = jnp.zeros((lp - BLOCK, KV_W), f32)
    dkm, dvm = jnp.concatenate([dkm, rest], axis=0), jnp.concatenate([dvm, rest], axis=0)
    (dqkv,) = _rowwise_bwd(_attn_prep, [qkv, cos_t, sin_t], [swap], [[dq], [dk, dkm], [dv, dvm]], name="attn_prep_bwd",
                           diff_rows=[True, False, False], diff_params=[False], out_dtypes=[bf16])

    d_w_qkv_t, db_qkv = _mm_tn(dqkv, u, name="dw_qkv", colsum=True)
    d_w_rkv_t, db_rkv = _mm_tn(dp_rkv, u, name="dw_rkv", colsum=True)
    d_w_lora_t, db_lora = _mm_tn(dp_lora, u, name="dw_lora", colsum=True)
    d_w_gates_t, db_gates = _mm_tn(dgates, u, name="dw_gates", colsum=True)
    d_w_in_t = jnp.concatenate([d_w_qkv_t, d_w_rkv_t, d_w_lora_t, d_w_gates_t], axis=0)
    zero = emit("input", dict(w_in_t=d_w_in_t, g2=d_g2, w2=d_w2, a2=d_a2))
    du = _mm(dqkv, w_qkv_t, "nn", name="d_u_qkv")
    du = _mm(dp_rkv, w_rkv_t, "nn", name="d_u_rkv", add=du)
    du = _mm(dp_lora, w_lora_t, "nn", name="d_u_lora", add=du)
    du = _mm(dgates, w_gates_t, "nn", name="d_u_gates", add=du)
    dh0, d_mix_g = _rowwise_bwd(lambda hv, g: (_rms(hv, g), hv), [h0], [p["norm_mix_g"] + zero], [[du], [dh1]],
                                name="norm_mix_bwd", diff_rows=[True], diff_params=[True])

    grads = dict(
        w_in_t=d_w_in_t,
        b_in=jnp.concatenate([db_qkv, db_rkv, db_lora, db_gates], axis=1),
        mix=jnp.concatenate([d_mix_rkv, d_mix_lora], axis=1),
        norm_mix_g=d_mix_g, sinks=d_sinks, w0=d_w0, w2=d_w2, a0=d_a0, a2=d_a2, g2=d_g2, k_k=d_k_k, k_a=d_k_a,
        r_k=d_r_k, ln_w=d_ln_w, ln_b=d_ln_b, w_br_attn_t=d_w_br_attn_t, w_br_rwkv_t=d_w_br_rwkv_t, w_o=d_w_o,
        norm_ffn_g=d_ffn_g, w_gate_t=d_w_gate_t, w_up_t=d_w_up_t, w_down=d_w_down, norm_final_g=d_final_g,
        meta=dh0[PAD:FRONT],
    )
    return loss8[0, 0], dh0[FRONT:], grads


def _position():
    return lax.axis_index("x"), lax.axis_index("y"), lax.axis_index("c")


def _other_chips(x, y):
    return [(1 - x, y), (x, 1 - y), (1 - x, 1 - y)]


_HBM = pl.BlockSpec(memory_space=pltpu.HBM)
_SEM = pl.BlockSpec(memory_space=pltpu.SEMAPHORE)
_EFFECT = pltpu.SideEffectType.DATAFLOW_SIDE_EFFECTING


def _chip_copies(src_refs, land_refs, send_sems, recv_sems, gather):
    x, y, c = _position()
    copies = []
    for a, (src, land) in enumerate(zip(src_refs, land_refs)):
        for j, (px, py) in enumerate(_other_chips(x, y)):
            copies.append(pltpu.make_async_remote_copy(
                src_ref=src if gather else src.at[2 * px + py],
                dst_ref=land.at[2 * x + y] if gather else land.at[j],
                send_sem=send_sems.at[3 * a + j], recv_sem=recv_sems.at[3 * a + j],
                device_id=(px, py, c), device_id_type=MESH))
    return copies


def _exchange_start(srcs, *, gather, name):
    n = len(srcs)
    lands = [lax.empty((N_CHIPS,) + s.shape if gather else (3,) + s.shape[1:], s.dtype) for s in srcs]

    def body(*refs):
        for cp in _chip_copies(refs[:n], refs[n:2 * n], refs[2 * n], refs[2 * n + 1], gather):
            cp.start()
        refs[-1][...] = jnp.zeros_like(refs[-1])

    res = pl.pallas_call(
        body, name=name,
        out_shape=(pltpu.SemaphoreType.DMA((3 * n,)), pltpu.SemaphoreType.DMA((3 * n,)),
                   *[pltpu.HBM(a.shape, a.dtype) for a in srcs + lands], jax.ShapeDtypeStruct((8, 128), f32)),
        in_specs=[_HBM] * (2 * n),
        out_specs=(_SEM, _SEM, *[_HBM] * (2 * n), pl.BlockSpec(memory_space=pltpu.VMEM)),
        input_output_aliases={i: 2 + i for i in range(2 * n)},
        compiler_params=pltpu.CompilerParams(has_side_effects=_EFFECT),
    )(*[pltpu.with_memory_space_constraint(a, pltpu.HBM) for a in srcs + lands])
    return res[0], res[1], list(res[2:2 + n]), list(res[2 + n:2 + 2 * n]), res[-1]


def _exchange_wait(handle, after, *, gather, name):
    send_sems, recv_sems, srcs, lands, _ = handle
    n = len(srcs)

    def body(*refs):
        for cp in _chip_copies(refs[:n], refs[n:2 * n], refs[2 * n], refs[2 * n + 1], gather):
            cp.wait_send()
            cp.wait_recv()

    res = pl.pallas_call(
        body, name=name,
        out_shape=tuple(pltpu.HBM(a.shape, a.dtype) for a in srcs + lands),
        in_specs=[_HBM] * (2 * n) + [_SEM, _SEM, pl.BlockSpec(memory_space=pl.ANY)],
        out_specs=tuple([_HBM] * (2 * n)),
        input_output_aliases={i: i for i in range(2 * n)},
        compiler_params=pltpu.CompilerParams(has_side_effects=_EFFECT),
    )(*srcs, *lands, send_sems, recv_sems, after)
    return list(res[:n]), list(res[n:])


def _sum_own_and_received(g, recv, *, name):
    _, r, w = g.shape
    tm = _tile(r)
    if g.dtype == bf16 and tm % 16:
        tm = r
    x, y, _ = _position()
    me = jnp.reshape(2 * x + y, (1,)).astype(jnp.int32)

    def body(me_ref, g_ref, r_ref, o_ref):
        o_ref[...] = (g_ref[0].astype(f32) + r_ref[0].astype(f32)) + (r_ref[1].astype(f32) + r_ref[2].astype(f32))

    return pl.pallas_call(
        body, name=name,
        grid_spec=pltpu.PrefetchScalarGridSpec(
            num_scalar_prefetch=1, grid=(r // tm,),
            in_specs=[pl.BlockSpec((1, tm, w), lambda i, me_ref: (me_ref[0], i, 0)),
                      pl.BlockSpec((3, tm, w), lambda i, me_ref: (0, i, 0))],
            out_specs=pl.BlockSpec((tm, w), lambda i, me_ref: (i, 0))),
        out_shape=jax.ShapeDtypeStruct((r, w), f32),
        compiler_params=_params(("parallel",)),
    )(me, g, recv)


def _swap_cores(arrs, *, name):
    n = len(arrs)

    def body(*refs):
        x, y, c = _position()
        copies = [pltpu.make_async_remote_copy(
            src_ref=refs[i], dst_ref=refs[n + i], send_sem=refs[2 * n].at[i], recv_sem=refs[2 * n + 1].at[i],
            device_id=(x, y, 1 - c), device_id_type=MESH) for i in range(n)]
        for cp in copies:
            cp.start()
        for cp in copies:
            cp.wait_recv()
        for cp in copies:
            cp.wait_send()

    return pl.pallas_call(
        body, name=name,
        in_specs=[pl.BlockSpec(memory_space=pl.ANY)] * n,
        out_specs=[pl.BlockSpec(memory_space=pl.ANY)] * n,
        out_shape=[jax.ShapeDtypeStruct(a.shape, a.dtype) for a in arrs],
        scratch_shapes=[pltpu.SemaphoreType.DMA((n,)), pltpu.SemaphoreType.DMA((n,))],
    )(*arrs)


def _all_reduce_small(a, *, name):
    rows, w = a.shape

    def body(a_ref, o_ref, buf, send_sems, recv_sems):
        x, y, c = _position()
        me = 4 * x + 2 * y + c
        buf[0] = a_ref[...]
        sends = []
        for rel in range(1, N_DEV):
            peer = ((1 - x) if rel & 4 else x, (1 - y) if rel & 2 else y, (1 - c) if rel & 1 else c)
            cp = pltpu.make_async_remote_copy(
                src_ref=a_ref, dst_ref=buf.at[rel], send_sem=send_sems.at[rel - 1], recv_sem=recv_sems.at[rel - 1],
                device_id=peer, device_id_type=MESH)
            cp.start()
            sends.append(cp)
        for cp in sends:
            cp.wait_recv()
        for cp in sends:
            cp.wait_send()
        acc = buf[jnp.bitwise_xor(me, 0)]
        for d in range(1, N_DEV):
            acc = acc + buf[jnp.bitwise_xor(me, d)]
        o_ref[...] = acc

    return pl.pallas_call(
        body, name=name,
        in_specs=[pl.BlockSpec(memory_space=pltpu.VMEM)],
        out_specs=pl.BlockSpec(memory_space=pltpu.VMEM),
        out_shape=jax.ShapeDtypeStruct((rows, w), f32),
        scratch_shapes=[pltpu.VMEM((N_DEV, rows, w), f32), pltpu.SemaphoreType.DMA((N_DEV - 1,)),
                        pltpu.SemaphoreType.DMA((N_DEV - 1,))],
    )(a)


def _adamw(w, g_parts, m, v, *, name, transposed=False):
    rows, cols = w.shape
    if transposed:
        tm = 256 if rows % 256 == 0 else rows
        g_spec = pl.BlockSpec((cols, tm), lambda i: (0, i))
    else:
        tm = _tile(rows, 256)
        g_spec = pl.BlockSpec((tm, cols), lambda i: (i, 0))
    n = len(g_parts)

    def body(*refs):
        w_ref, m_ref, v_ref = refs[0], refs[1 + n], refs[2 + n]
        g_ref, d_ref, nm_ref, nv_ref = refs[3 + n:]
        gv = refs[1][...]
        for part in refs[2:1 + n]:
            gv = gv + part[...]
        if transposed:
            gv = gv.T
        g_ref[...] = gv
        nm = ADAM_B1 * m_ref[...] + (1.0 - ADAM_B1) * gv
        nv = ADAM_B2 * v_ref[...] + (1.0 - ADAM_B2) * (gv * gv)
        m_hat = nm / (1.0 - ADAM_B1 ** ADAM_STEP)
        v_hat = nv / (1.0 - ADAM_B2 ** ADAM_STEP)
        d_ref[...] = -ADAM_LR * (m_hat / (jnp.sqrt(v_hat) + ADAM_EPS) + ADAM_WD * w_ref[...])
        nm_ref[...] = nm
        nv_ref[...] = nv

    spec = pl.BlockSpec((tm, cols), lambda i: (i, 0))
    shape = jax.ShapeDtypeStruct((rows, cols), f32)
    return pl.pallas_call(
        body, name=name, grid=(rows // tm,), in_specs=[spec] + [g_spec] * n + [spec] * 2,
        out_specs=[spec] * 4, out_shape=[shape] * 4,
        compiler_params=_params(("parallel",)),
    )(w, *g_parts, m, v)


def _pad_rows(a, rows):
    return jnp.concatenate([a, jnp.zeros((rows - a.shape[0], a.shape[1]), a.dtype)], axis=0) if rows > a.shape[0] else a


_SMALL = (("norm_mix_g", D_MODEL), ("b_in", D_IN), ("sinks", Q_HEADS), ("mix", RWKV_PROJ), ("w0", RWKV_DIM),
          ("a0", RWKV_DIM), ("k_k", RWKV_DIM), ("k_a", RWKV_DIM), ("r_k", RWKV_DIM), ("ln_w", RWKV_DIM),
          ("ln_b", RWKV_DIM), ("norm_ffn_g", D_MODEL), ("norm_final_g", D_MODEL))


def _pack_small(d):
    flat = jnp.concatenate([d[n].reshape(-1).astype(f32) for n, _ in _SMALL])
    return flat


def _unpack_small(flat):
    out, off = {}, 0
    for n, size in _SMALL:
        out[n] = flat[off:off + size]
        off += size
    return out


_SMALL_TOTAL = sum(s for _, s in _SMALL)


def kernel(x, meta_tokens, norm_mix_g, w_in, b_in, attn_sinks, rwkv_mix, rwkv_w0, rwkv_w2, rwkv_a0, rwkv_a2, rwkv_g2, rwkv_k_k, rwkv_k_a, rwkv_r_k, rwkv_ln_w, rwkv_ln_b, w_br_attn, w_br_rwkv, w_o, norm_ffn_g, w_ffn_gate, w_ffn_up, w_ffn_down, norm_final_g, loss_target, m_meta_tokens, m_norm_mix_g, m_w_in, m_b_in, m_attn_sinks, m_rwkv_mix, m_rwkv_w0, m_rwkv_w2, m_rwkv_a0, m_rwkv_a2, m_rwkv_g2, m_rwkv_k_k, m_rwkv_k_a, m_rwkv_r_k, m_rwkv_ln_w, m_rwkv_ln_b, m_w_br_attn, m_w_br_rwkv, m_w_o, m_norm_ffn_g, m_w_ffn_gate, m_w_ffn_up, m_w_ffn_down, m_norm_final_g, v_meta_tokens, v_norm_mix_g, v_w_in, v_b_in, v_attn_sinks, v_rwkv_mix, v_rwkv_w0, v_rwkv_w2, v_rwkv_a0, v_rwkv_a2, v_rwkv_g2, v_rwkv_k_k, v_rwkv_k_a, v_rwkv_r_k, v_rwkv_ln_w, v_rwkv_ln_b, v_w_br_attn, v_w_br_rwkv, v_w_o, v_norm_ffn_g, v_w_ffn_gate, v_w_ffn_up, v_w_ffn_down, v_norm_final_g):
    names = ("meta_tokens", "norm_mix_g", "w_in", "b_in", "attn_sinks", "rwkv_mix", "rwkv_w0", "rwkv_w2", "rwkv_a0",
             "rwkv_a2", "rwkv_g2", "rwkv_k_k", "rwkv_k_a", "rwkv_r_k", "rwkv_ln_w", "rwkv_ln_b", "w_br_attn",
             "w_br_rwkv", "w_o", "norm_ffn_g", "w_ffn_gate", "w_ffn_up", "w_ffn_down", "norm_final_g")
    w_all = dict(zip(names, (meta_tokens, norm_mix_g, w_in, b_in, attn_sinks, rwkv_mix, rwkv_w0, rwkv_w2, rwkv_a0,
                             rwkv_a2, rwkv_g2, rwkv_k_k, rwkv_k_a, rwkv_r_k, rwkv_ln_w, rwkv_ln_b, w_br_attn,
                             w_br_rwkv, w_o, norm_ffn_g, w_ffn_gate, w_ffn_up, w_ffn_down, norm_final_g)))
    m_all = dict(zip(names, (m_meta_tokens, m_norm_mix_g, m_w_in, m_b_in, m_attn_sinks, m_rwkv_mix, m_rwkv_w0,
                             m_rwkv_w2, m_rwkv_a0, m_rwkv_a2, m_rwkv_g2, m_rwkv_k_k, m_rwkv_k_a, m_rwkv_r_k,
                             m_rwkv_ln_w, m_rwkv_ln_b, m_w_br_attn, m_w_br_rwkv, m_w_o, m_norm_ffn_g, m_w_ffn_gate,
                             m_w_ffn_up, m_w_ffn_down, m_norm_final_g)))
    v_all = dict(zip(names, (v_meta_tokens, v_norm_mix_g, v_w_in, v_b_in, v_attn_sinks, v_rwkv_mix, v_rwkv_w0,
                             v_rwkv_w2, v_rwkv_a0, v_rwkv_a2, v_rwkv_g2, v_rwkv_k_k, v_rwkv_k_a, v_rwkv_r_k,
                             v_rwkv_ln_w, v_rwkv_ln_b, v_w_br_attn, v_w_br_rwkv, v_w_o, v_norm_ffn_g, v_w_ffn_gate,
                             v_w_ffn_up, v_w_ffn_down, v_norm_final_g)))
    cx, cy, _ = _position()
    chip = 2 * cx + cy

    t_of = dict(w_in_t="w_in", w_gate_t="w_ffn_gate", w_up_t="w_ffn_up", w_br_attn_t="w_br_attn",
                w_br_rwkv_t="w_br_rwkv", g2_t="rwkv_g2", w2_t="rwkv_w2", a2_t="rwkv_a2")
    plain_of = dict(w_down="w_ffn_down", w_o="w_o")
    meta_cols = meta_tokens.shape[1]

    def shard(k):
        return (w_all[t_of[k]][0].T if k in t_of else w_all[plain_of[k]][0]).astype(bf16)

    def whole(zone, own):
        return lax.dynamic_update_slice_in_dim(zone, own[None], chip, axis=0).reshape(-1, own.shape[-1])

    early = ("w_in_t", "g2_t", "w2_t", "a2_t")
    late = ("w_gate_t", "w_up_t", "w_down", "w_o", "w_br_attn_t", "w_br_rwkv_t")
    early_h = _exchange_start([shard(k) for k in early] + [meta_tokens], gather=True, name="gather_early_start")
    late_h = _exchange_start([shard(k) for k in late], gather=True, name="gather_late_start")
    own, zones = _exchange_wait(early_h, late_h[4], gather=True, name="gather_early_wait")
    got = {k: whole(z, o) for k, z, o in zip(early, zones, own)}
    meta_full = whole(zones[-1], own[-1]).reshape(N_CHIPS, N_META, meta_cols).transpose(1, 0, 2).reshape(N_META, -1)
    p = dict(
        w_in_t=got["w_in_t"], g2=got["g2_t"].T.astype(f32), w2=got["w2_t"].T.astype(f32),
        a2=got["a2_t"].T.astype(f32),
        b_in=b_in, sinks=attn_sinks, mix=rwkv_mix, w0=rwkv_w0, a0=rwkv_a0, k_k=rwkv_k_k, k_a=rwkv_k_a,
        r_k=rwkv_r_k.reshape(1, RWKV_DIM), ln_w=rwkv_ln_w, ln_b=rwkv_ln_b, norm_mix_g=norm_mix_g,
        norm_ffn_g=norm_ffn_g, norm_final_g=norm_final_g.reshape(1, D_MODEL),
    )

    def late_weights(after):
        own_l, zones_l = _exchange_wait(late_h, after, gather=True, name="gather_late_wait")
        return {k: whole(z, o) for k, z, o in zip(late, zones_l, own_l)}

    started = {}

    def emit(group, grads_):
        keys = list(grads_)
        slabs = []
        for k in keys:
            a = grads_[k].T if k in ("g2", "w2", "a2") else grads_[k]
            slabs.append(a.reshape(N_CHIPS, a.shape[0] // N_CHIPS, a.shape[1]))
        started[group] = (keys, _exchange_start(slabs, gather=False, name="scatter_" + group + "_start"))
        return started[group][1][4][0, 0]

    loss, dx, g = _local_step(x[0], loss_target[0], meta_full, p, late_weights, emit)

    def partial_sums(group, after):
        keys, handle = started[group]
        slabs, lands = _exchange_wait(handle, after, gather=False, name="scatter_" + group + "_wait")
        return {k: _sum_own_and_received(s, l, name="sum_chips_" + k) for k, s, l in zip(keys, slabs, lands)}

    parts = partial_sums("ffn", dx)
    parts.update(partial_sums("branch", parts["w_down"]))
    parts.update(partial_sums("input", parts["w_o"]))
    keys = list(parts)
    others = dict(zip(keys, _swap_cores([parts[k] for k in keys], name="swap_cores")))
    for short in ("g2", "w2", "a2"):
        parts[short + "_t"], others[short + "_t"] = parts.pop(short), others.pop(short)

    small = jnp.concatenate([_pack_small(g), loss.reshape(1)])
    small_rows = -(-small.shape[0] // PACK_W)
    small = jnp.concatenate([small, jnp.zeros((small_rows * PACK_W - small.shape[0],), f32)]).reshape(small_rows, PACK_W)
    small_rows8 = -(-(small_rows + N_META) // 8) * 8
    reduced = _all_reduce_small(_pad_rows(jnp.concatenate([g["meta"], small], axis=0), small_rows8),
                                name="reduce_small")
    g_meta = lax.dynamic_slice_in_dim(reduced[:N_META], chip * meta_cols, meta_cols, axis=1)
    flat = reduced[N_META:N_META + small_rows].reshape(-1)
    g_small = _unpack_small(flat)
    loss_total = flat[_SMALL_TOTAL]

    small_of = dict(norm_mix_g="norm_mix_g", b_in="b_in", attn_sinks="sinks", rwkv_mix="mix", rwkv_w0="w0",
                    rwkv_a0="a0", rwkv_k_k="k_k", rwkv_k_a="k_a", rwkv_r_k="r_k", rwkv_ln_w="ln_w",
                    rwkv_ln_b="ln_b", norm_ffn_g="norm_ffn_g", norm_final_g="norm_final_g")
    grads = {"meta_tokens": g_meta}
    for n, k in small_of.items():
        grads[n] = g_small[k].reshape(w_all[n].shape)

    delta, new_m, new_v = {}, {}, {}
    for k, n in {**t_of, **plain_of}.items():
        shape2 = w_all[n].shape[1:]
        res = _adamw(w_all[n].reshape(shape2), [parts[k], others[k]], m_all[n].reshape(shape2),
                     v_all[n].reshape(shape2), name="adamw_" + n, transposed=k in t_of)
        grads[n], delta[n], new_m[n], new_v[n] = (t.reshape(w_all[n].shape) for t in res)
    rest = [n for n in names if n not in delta]

    def pack_rest(src):
        flat_ = jnp.concatenate([src[n].reshape(-1) for n in rest])
        rows_ = -(-flat_.shape[0] // (8 * PACK_W)) * 8
        return jnp.concatenate([flat_, jnp.ones((rows_ * PACK_W - flat_.shape[0],), f32)]).reshape(rows_, PACK_W)

    _, d_, m_, v_ = _adamw(pack_rest(w_all), [pack_rest(grads)], pack_rest(m_all), pack_rest(v_all),
                           name="adamw_small")
    off = 0
    for n in rest:
        size = w_all[n].size
        for dst, src in ((delta, d_), (new_m, m_), (new_v, v_)):
            dst[n] = src.reshape(-1)[off:off + size].reshape(w_all[n].shape)
        off += size

    return (loss_total, dx.reshape(x.shape), *[grads[n] for n in names], *[delta[n] for n in names],
            *[new_m[n] for n in names], *[new_v[n] for n in names])
```

```python
import math

import jax
import jax.numpy as jnp
from jax import lax
from jax.experimental import pallas as pl
from jax.experimental.pallas import tpu as pltpu

f32 = jnp.float32
bf16 = jnp.bfloat16

D_MODEL = 1024
N_META = 16
HEAD_DIM = 64
Q_HEADS = 8
KV_HEADS = 2
GROUP = Q_HEADS // KV_HEADS
WINDOW = 128
BLOCK = 128
ROPE_THETA = 500000.0
ROPE_DIM = HEAD_DIM // 4
RWKV_HEADS = 8
RWKV_HEAD = 64
RWKV_DIM = RWKV_HEADS * RWKV_HEAD
DECAY_LORA = 64
AAA_LORA = 64
GATE_LORA = 160
LORA_W = DECAY_LORA + AAA_LORA + GATE_LORA
RWKV_LN_EPS = 64e-5
D_FF = 2816
Q_W = Q_HEADS * HEAD_DIM
KV_W = KV_HEADS * HEAD_DIM
ATTN_PROJ = Q_W + 2 * KV_W
RKV_W = 3 * RWKV_DIM
RWKV_PROJ = RKV_W + LORA_W
D_IN = ATTN_PROJ + RWKV_PROJ + 2 * D_MODEL
RMS_EPS = 1e-6
NEG_INF = -1e30
PAD = BLOCK - N_META
FRONT = PAD + N_META

ADAM_LR = 0.001
ADAM_B1 = 0.9
ADAM_B2 = 0.999
ADAM_EPS = 1e-08
ADAM_WD = 0.01
ADAM_STEP = 10

N_CHIPS = 4
N_DEV = 8
CHUNK = 64
VMEM_LIMIT = 56 * 1024 * 1024
PACK_W = 1024
MESH = pl.DeviceIdType.MESH


def _tile(m, pref=384):
    for step in (16, 8):
        for t in range(min(m, pref) // step * step, 0, -step):
            if m % t == 0:
                return t
    return m


def _params(sem=None):
    return pltpu.CompilerParams(dimension_semantics=sem, vmem_limit_bytes=VMEM_LIMIT)


def _full(shape):
    nd = len(shape)
    return pl.BlockSpec(shape, lambda *_: (0,) * nd)


def _dot(a, b, dims="nn"):
    dn = {"nn": (((1,), (0,)), ((), ())), "nt": (((1,), (1,)), ((), ())), "tn": (((0,), (0,)), ((), ()))}[dims]
    return lax.dot_general(a.astype(bf16), b.astype(bf16), dn, preferred_element_type=f32)


def _two_pass(x, m):
    x_hi = x.astype(bf16)
    x_lo = (x - x_hi.astype(f32)).astype(bf16)
    return _dot(x_hi, m) + _dot(x_lo, m)


@jax.custom_vjp
def _dot_const(x, m):
    return _two_pass(x, m)


def _dot_const_fwd(x, m):
    return _two_pass(x, m), m


def _dot_const_bwd(m, ct):
    return _two_pass(ct, m.T), jnp.zeros_like(m)


_dot_const.defvjp(_dot_const_fwd, _dot_const_bwd)


def _two_pass_left(m, x, dims):
    x_hi = x.astype(bf16)
    x_lo = (x - x_hi.astype(f32)).astype(bf16)
    return _dot(m, x_hi, dims) + _dot(m, x_lo, dims)


@jax.custom_vjp
def _const_dot(m, x):
    return _two_pass_left(m, x, "nn")


def _const_dot_fwd(m, x):
    return _two_pass_left(m, x, "nn"), m


def _const_dot_bwd(m, ct):
    return jnp.zeros_like(m), _two_pass_left(m, ct, "tn")


_const_dot.defvjp(_const_dot_fwd, _const_dot_bwd)


def _mm(a, b, mode, *, name, out_dtype=f32, bias=None, add=None, zero_rows_below=0):
    m, _ = a.shape
    n = b.shape[1] if mode == "nn" else b.shape[0]
    tm = _tile(m)
    has_bias, has_add = bias is not None, add is not None

    def body(*refs):
        a_ref, b_ref = refs[0], refs[1]
        o_ref = refs[-1]
        acc = _dot(a_ref[...], b_ref[...], mode)
        k = 2
        if has_bias:
            acc = acc + refs[k][...]
            k += 1
        if zero_rows_below:
            rows = pl.program_id(0) * tm + lax.broadcasted_iota(jnp.int32, acc.shape, 0)
            acc = jnp.where(rows >= zero_rows_below, acc, 0.0)
        if has_add:
            acc = acc + refs[k][...].astype(f32)
        o_ref[...] = acc.astype(out_dtype)

    ins = [a, b]
    in_specs = [pl.BlockSpec((tm, a.shape[1]), lambda i: (i, 0)), _full(b.shape)]
    if has_bias:
        ins.append(bias)
        in_specs.append(_full(bias.shape))
    if has_add:
        ins.append(add)
        in_specs.append(pl.BlockSpec((tm, n), lambda i: (i, 0)))
    return pl.pallas_call(
        body, name=name, grid=(m // tm,), in_specs=in_specs,
        out_specs=pl.BlockSpec((tm, n), lambda i: (i, 0)),
        out_shape=jax.ShapeDtypeStruct((m, n), out_dtype),
        compiler_params=_params(("parallel",)),
    )(*ins)


def _mm_tn(a, b, *, name, colsum=False, out_dtype=bf16):
    r, m = a.shape
    n = b.shape[1]
    tr = _tile(r, 1408)
    tmo = m
    for cand in (1408, 1024, 768, 512):
        if m > 1024 and m % cand == 0:
            tmo = cand
            break
    steps = r // tr

    def body(a_ref, b_ref, o_ref, *rest):
        acc = rest[-1]
        i = pl.program_id(1)

        @pl.when(i == 0)
        def _():
            acc[...] = jnp.zeros_like(acc)
            if colsum:
                rest[0][...] = jnp.zeros_like(rest[0])

        acc[...] += _dot(a_ref[...], b_ref[...], "tn")
        if colsum:
            rest[0][...] += jnp.sum(a_ref[...].astype(f32), axis=0, keepdims=True)

        @pl.when(i == steps - 1)
        def _():
            o_ref[...] = acc[...].astype(out_dtype)

    out_shape = [jax.ShapeDtypeStruct((m, n), out_dtype)]
    out_specs = [pl.BlockSpec((tmo, n), lambda j, i: (j, 0))]
    if colsum:
        out_shape.append(jax.ShapeDtypeStruct((1, m), f32))
        out_specs.append(pl.BlockSpec((1, tmo), lambda j, i: (0, j)))
    res = pl.pallas_call(
        body, name=name, grid=(m // tmo, steps),
        in_specs=[pl.BlockSpec((tr, tmo), lambda j, i: (i, j)), pl.BlockSpec((tr, n), lambda j, i: (i, 0))],
        out_specs=out_specs, out_shape=out_shape,
        scratch_shapes=[pltpu.VMEM((tmo, n), f32)],
        compiler_params=_params(("parallel", "arbitrary")),
    )(a, b)
    return res if colsum else res[0]


def _rowwise(fn, rows, params, outs, *, name, tm=None):
    m = rows[0].shape[0]
    tm = tm or _tile(m)
    nr, npar = len(rows), len(params)

    def body(*refs):
        vals = [r[...] for r in refs[:nr + npar]]
        res = fn(*vals)
        for o_ref, v in zip(refs[nr + npar:], res):
            o_ref[...] = v.astype(o_ref.dtype)

    return pl.pallas_call(
        body, name=name, grid=(m // tm,),
        in_specs=[pl.BlockSpec((tm, r.shape[1]), lambda i: (i, 0)) for r in rows] + [_full(p.shape) for p in params],
        out_specs=[pl.BlockSpec((tm, w), lambda i: (i, 0)) for w, _ in outs],
        out_shape=[jax.ShapeDtypeStruct((m, w), dt) for w, dt in outs],
        compiler_params=_params(("parallel",)),
    )(*rows, *params)


def _rowwise_bwd(fn, rows, params, cts, *, name, diff_rows, diff_params, tm=None, zero_rows_below=0, out_dtypes=None):
    m = rows[0].shape[0]
    tm = tm or _tile(m)
    nr, npar = len(rows), len(params)
    d_idx = [i for i in range(nr) if diff_rows[i]]
    p_idx = [i for i in range(npar) if diff_params[i]]
    out_dtypes = out_dtypes or [f32] * len(d_idx)
    flat_cts = [c for group in cts for c in group]
    n_ct = len(flat_cts)

    def body(*refs):
        vals = [r[...] for r in refs[:nr + npar]]
        ct_refs = refs[nr + npar:nr + npar + n_ct]
        out_refs = refs[nr + npar + n_ct:]
        ct_vals, k = [], 0
        for group in cts:
            acc = ct_refs[k][...].astype(f32)
            for extra in range(1, len(group)):
                acc = acc + ct_refs[k + extra][...].astype(f32)
            k += len(group)
            if zero_rows_below:
                rr = pl.program_id(0) * tm + lax.broadcasted_iota(jnp.int32, acc.shape, 0)
                acc = jnp.where(rr >= zero_rows_below, acc, 0.0)
            ct_vals.append(acc)

        def g(*dargs):
            full = list(vals)
            for pos, i in enumerate(d_idx):
                full[i] = dargs[pos]
            for pos, i in enumerate(p_idx):
                full[nr + i] = dargs[len(d_idx) + pos]
            return tuple(fn(*full))

        _, vjp = jax.vjp(g, *[vals[i].astype(f32) for i in d_idx], *[vals[nr + i] for i in p_idx])
        grads = vjp(tuple(ct_vals))
        for pos in range(len(d_idx)):
            out_refs[pos][...] = grads[pos].astype(out_refs[pos].dtype)
        first = pl.program_id(0) == 0
        for pos in range(len(p_idx)):
            o_ref = out_refs[len(d_idx) + pos]

            @pl.when(first)
            def _(o_ref=o_ref):
                o_ref[...] = jnp.zeros_like(o_ref)

            o_ref[...] += grads[len(d_idx) + pos]

    return pl.pallas_call(
        body, name=name, grid=(m // tm,),
        in_specs=[pl.BlockSpec((tm, r.shape[1]), lambda i: (i, 0)) for r in rows] + [_full(p.shape) for p in params]
        + [pl.BlockSpec((tm, c.shape[1]), lambda i: (i, 0)) for c in flat_cts],
        out_specs=[pl.BlockSpec((tm, rows[i].shape[1]), lambda i_: (i_, 0)) for i in d_idx]
        + [_full(params[i].shape) for i in p_idx],
        out_shape=[jax.ShapeDtypeStruct(rows[i].shape, dt) for i, dt in zip(d_idx, out_dtypes)]
        + [jax.ShapeDtypeStruct(params[i].shape, f32) for i in p_idx],
        compiler_params=_params(("arbitrary",)),
    )(*rows, *params, *flat_cts)


def _rms(x, g):
    return x * lax.rsqrt(jnp.mean(x * x, axis=-1, keepdims=True) + RMS_EPS) * g


def _head_sum_matrix(width, head):
    idx = jnp.arange(width) // head
    return (idx[:, None] == idx[None, :]).astype(f32)


def _rope_tables(lp):
    half = ROPE_DIM // 2
    pos = (jnp.arange(lp) - PAD).astype(f32)
    inv_freq = jnp.power(jnp.float32(ROPE_THETA), -jnp.arange(half, dtype=f32) * (2.0 / ROPE_DIM))
    ang = pos[:, None] * inv_freq[None, :]
    cos, sin = jnp.cos(ang), jnp.sin(ang)
    ones = jnp.ones((lp, HEAD_DIM - ROPE_DIM), f32)
    zeros = jnp.zeros((lp, HEAD_DIM - ROPE_DIM), f32)
    cos_t = jnp.concatenate([cos, cos, ones], axis=1)
    sin_t = jnp.concatenate([-sin, sin, zeros], axis=1)
    i = jnp.arange(HEAD_DIM)
    src = jnp.where(i < half, i + half, jnp.where(i < ROPE_DIM, i - half, i))
    swap = ((i[:, None] == src[None, :]) & (i[None, :] < ROPE_DIM)).astype(f32)
    return cos_t, sin_t, swap


def _attn_prep(qkv, cos_t, sin_t, swap):
    outs = []
    for h in range(Q_HEADS + KV_HEADS):
        t = qkv[:, h * HEAD_DIM:(h + 1) * HEAD_DIM]
        outs.append(t * cos_t + _dot_const(t, swap) * sin_t)
    q = jnp.concatenate(outs[:Q_HEADS], axis=1)
    k = jnp.concatenate(outs[Q_HEADS:], axis=1)
    return q, k, qkv[:, Q_W + KV_W:]


def _softplus(z):
    return jnp.maximum(z, 0.0) + jnp.log1p(jnp.exp(-jnp.abs(z)))


def _rwkv_prep(rkv, lora, w0, w2, a0, a2, g2, k_k, k_a, hsum):
    r = rkv[:, :RWKV_DIM]
    k = rkv[:, RWKV_DIM:2 * RWKV_DIM]
    v = rkv[:, 2 * RWKV_DIM:]
    dw = lora[:, :DECAY_LORA]
    da = lora[:, DECAY_LORA:DECAY_LORA + AAA_LORA]
    dg = lora[:, DECAY_LORA + AAA_LORA:]
    w = -_softplus(-(w0 + _dot(jnp.tanh(dw), w2))) - 0.5
    a = jax.nn.sigmoid(a0 + _dot(da, a2))
    g = _dot(jax.nn.sigmoid(dg), g2)
    kk = k * k_k
    kk = kk * lax.rsqrt(jnp.maximum(_dot_const(kk * kk, hsum), 1e-24))
    k = k * (1.0 + (a - 1.0) * k_a)
    log_decay = -jnp.exp(w)
    return r, log_decay, k, v, -kk, kk * a, g


def _rwkv_post(y, r, k, v, g, ln_w, ln_b, r_k, hmean):
    hsum = hmean * RWKV_HEAD
    mean = _dot_const(y, hmean)
    yc = y - mean
    var = _dot_const(yc * yc, hmean)
    yn = yc * lax.rsqrt(var + RWKV_LN_EPS) * ln_w + ln_b
    bonus = _dot_const(r * k * r_k, hsum) * v
    return ((yn + bonus) * g,)


def _merge(gates, br_a, br_r):
    sg = jax.nn.sigmoid(gates)
    return (sg[:, :D_MODEL] * br_a + sg[:, D_MODEL:] * br_r,)


def _swiglu(gate, up):
    return (jax.nn.silu(gate) * up,)


def _ffn_in(f, w_gate_t, w_up_t, *, name):
    m, d = f.shape
    n = w_gate_t.shape[0]
    tm = _tile(m)

    def body(f_ref, wg_ref, wu_ref, g_ref, u_ref, a_ref):
        g = _dot(f_ref[...], wg_ref[...], "nt")
        u = _dot(f_ref[...], wu_ref[...], "nt")
        g_ref[...] = g.astype(g_ref.dtype)
        u_ref[...] = u.astype(u_ref.dtype)
        a_ref[...] = _swiglu(g, u)[0].astype(a_ref.dtype)

    spec = pl.BlockSpec((tm, n), lambda i: (i, 0))
    return pl.pallas_call(
        body, name=name, grid=(m // tm,),
        in_specs=[pl.BlockSpec((tm, d), lambda i: (i, 0)), _full(w_gate_t.shape), _full(w_up_t.shape)],
        out_specs=[spec] * 3, out_shape=[jax.ShapeDtypeStruct((m, n), bf16)] * 3,
        compiler_params=_params(("parallel",)),
    )(f, w_gate_t, w_up_t)


def _branch_merge(y_attn, y_rwkv, w_attn_t, w_rwkv_t, gates, *, name):
    m = y_attn.shape[0]
    tm = _tile(m)

    def body(ya_ref, yr_ref, wa_ref, wr_ref, g_ref, a_ref, r_ref, o_ref):
        br_a = _dot(ya_ref[...], wa_ref[...], "nt")
        br_r = _dot(yr_ref[...], wr_ref[...], "nt")
        a_ref[...] = br_a.astype(a_ref.dtype)
        r_ref[...] = br_r.astype(r_ref.dtype)
        o_ref[...] = _merge(g_ref[...], br_a, br_r)[0].astype(o_ref.dtype)

    rows = lambda a: pl.BlockSpec((tm, a.shape[1]), lambda i: (i, 0))
    spec = pl.BlockSpec((tm, D_MODEL), lambda i: (i, 0))
    return pl.pallas_call(
        body, name=name, grid=(m // tm,),
        in_specs=[rows(y_attn), rows(y_rwkv), _full(w_attn_t.shape), _full(w_rwkv_t.shape), rows(gates)],
        out_specs=[spec] * 3, out_shape=[jax.ShapeDtypeStruct((m, D_MODEL), bf16)] * 3,
        compiler_params=_params(("parallel",)),
    )(y_attn, y_rwkv, w_attn_t, w_rwkv_t, gates)


def _branch_merge_bwd(dh, w_o, gates, br_a, br_r, *, name):
    m = dh.shape[0]
    tm = _tile(m)

    def body(dh_ref, w_ref, g_ref, a_ref, r_ref, dg_ref, da_ref, dr_ref):
        dmerged = _dot(dh_ref[...], w_ref[...], "nt")
        _, vjp = jax.vjp(lambda g, a, r: _merge(g, a, r)[0], g_ref[...], a_ref[...].astype(f32),
                         r_ref[...].astype(f32))
        dg, da, dr = vjp(dmerged)
        dg_ref[...] = dg.astype(dg_ref.dtype)
        da_ref[...] = da.astype(da_ref.dtype)
        dr_ref[...] = dr.astype(dr_ref.dtype)

    rows = lambda a: pl.BlockSpec((tm, a.shape[1]), lambda i: (i, 0))
    return pl.pallas_call(
        body, name=name, grid=(m // tm,),
        in_specs=[rows(dh), _full(w_o.shape), rows(gates), rows(br_a), rows(br_r)],
        out_specs=[rows(gates), rows(br_a), rows(br_r)],
        out_shape=[jax.ShapeDtypeStruct(gates.shape, bf16), jax.ShapeDtypeStruct(br_a.shape, bf16),
                   jax.ShapeDtypeStruct(br_r.shape, bf16)],
        compiler_params=_params(("parallel",)),
    )(dh, w_o, gates, br_a, br_r)


def _ffn_in_bwd(dh, w_down, gate, up, *, name):
    m, d = dh.shape
    n = w_down.shape[0]
    tm = _tile(m)

    def body(dh_ref, w_ref, g_ref, u_ref, dg_ref, du_ref):
        dact = _dot(dh_ref[...], w_ref[...], "nt")
        _, vjp = jax.vjp(lambda a, b: _swiglu(a, b)[0], g_ref[...].astype(f32), u_ref[...].astype(f32))
        dg, du = vjp(dact)
        dg_ref[...] = dg.astype(dg_ref.dtype)
        du_ref[...] = du.astype(du_ref.dtype)

    spec = pl.BlockSpec((tm, n), lambda i: (i, 0))
    return pl.pallas_call(
        body, name=name, grid=(m // tm,),
        in_specs=[pl.BlockSpec((tm, d), lambda i: (i, 0)), _full(w_down.shape), spec, spec],
        out_specs=[spec] * 2, out_shape=[jax.ShapeDtypeStruct((m, n), bf16)] * 2,
        compiler_params=_params(("parallel",)),
    )(dh, w_down, gate, up)


def _token_shift(p, mix, *, name):
    m, c = p.shape
    tm = _tile(m)
    sub = tm // 8

    def body(p_ref, prev_ref, mix_ref, o_ref):
        x = p_ref[...]
        rows = lax.broadcasted_iota(jnp.int32, x.shape, 0)
        last = jnp.where(pl.program_id(0) == 0, 0.0, prev_ref[7:8, :])
        xp = jnp.where(rows == 0, last, pltpu.roll(x, 1, axis=0))
        o_ref[...] = x + (xp - x) * mix_ref[...]

    return pl.pallas_call(
        body, name=name, grid=(m // tm,),
        in_specs=[pl.BlockSpec((tm, c), lambda i: (i, 0)),
                  pl.BlockSpec((8, c), lambda i: (jnp.maximum(i * sub - 1, 0), 0)),
                  _full(mix.shape)],
        out_specs=pl.BlockSpec((tm, c), lambda i: (i, 0)),
        out_shape=jax.ShapeDtypeStruct((m, c), f32),
        compiler_params=_params(("parallel",)),
    )(p, p, mix)


def _token_shift_bwd(p, mix, dpf, *, name):
    m, c = p.shape
    tm = _tile(m)
    sub = tm // 8
    n_tiles = m // tm

    def body(p_ref, prev_ref, mix_ref, d_ref, nxt_ref, dp_ref, dmix_ref):
        i = pl.program_id(0)
        x = p_ref[...]
        d = d_ref[...]
        mixv = mix_ref[...]
        rows = lax.broadcasted_iota(jnp.int32, x.shape, 0)
        last = jnp.where(i == 0, 0.0, prev_ref[7:8, :])
        xp = jnp.where(rows == 0, last, pltpu.roll(x, 1, axis=0))
        dm = d * mixv
        first_next = jnp.where(i == n_tiles - 1, 0.0, nxt_ref[0:1, :] * mixv)
        dm_next = jnp.where(rows == tm - 1, first_next, pltpu.roll(dm, tm - 1, axis=0))
        dp = d - dm + dm_next
        dp_ref[...] = jnp.where(i * tm + rows >= PAD, dp, 0.0).astype(dp_ref.dtype)

        @pl.when(i == 0)
        def _():
            dmix_ref[...] = jnp.zeros_like(dmix_ref)

        dmix_ref[...] += jnp.sum(d * (xp - x), axis=0, keepdims=True)

    return pl.pallas_call(
        body, name=name, grid=(n_tiles,),
        in_specs=[pl.BlockSpec((tm, c), lambda i: (i, 0)),
                  pl.BlockSpec((8, c), lambda i: (jnp.maximum(i * sub - 1, 0), 0)),
                  _full(mix.shape),
                  pl.BlockSpec((tm, c), lambda i: (i, 0)),
                  pl.BlockSpec((8, c), lambda i: (jnp.minimum((i + 1) * sub, m // 8 - 1), 0))],
        out_specs=[pl.BlockSpec((tm, c), lambda i: (i, 0)), _full(mix.shape)],
        out_shape=[jax.ShapeDtypeStruct((m, c), bf16), jax.ShapeDtypeStruct(mix.shape, f32)],
        compiler_params=_params(("arbitrary",)),
    )(p, p, mix, dpf, dpf)


def _attn_masks(blk):
    qi = lax.broadcasted_iota(jnp.int32, (BLOCK, BLOCK), 0)
    ki = lax.broadcasted_iota(jnp.int32, (BLOCK, BLOCK), 1)
    qpos = blk * BLOCK + qi - PAD
    kpos_c = blk * BLOCK + ki - PAD
    kpos_p = kpos_c - BLOCK
    kpos_m = ki - PAD

    def band(kpos):
        return (kpos >= N_META) & (kpos <= qpos) & (qpos - kpos < WINDOW)

    return band(kpos_p), band(kpos_c), (kpos_m >= 0) & (kpos_m <= qpos)


def _attn_probs(qs, k3s, sink, oks):
    s = [[jnp.where(ok, _dot(qh, kx, "nt"), NEG_INF) for kx, ok in zip(k3, oks)] for qh, k3 in zip(qs, k3s)]
    mx = [jnp.maximum(jnp.maximum(jnp.max(t[0], -1, keepdims=True), jnp.max(t[1], -1, keepdims=True)),
                      jnp.maximum(jnp.max(t[2], -1, keepdims=True), sk)) for t, sk in zip(s, sink)]
    e = [[jnp.exp(tx - m) for tx in t] for t, m in zip(s, mx)]
    e_sink = [jnp.exp(sk - m) for sk, m in zip(sink, mx)]
    inv = [1.0 / (jnp.sum(t[0], -1, keepdims=True) + jnp.sum(t[1], -1, keepdims=True)
                  + jnp.sum(t[2], -1, keepdims=True) + es) for t, es in zip(e, e_sink)]
    return [[tx * i for tx in t] for t, i in zip(e, inv)], [es * i for es, i in zip(e_sink, inv)]


def _head_cols(i):
    return slice(i * HEAD_DIM, (i + 1) * HEAD_DIM)


def _attn_operands(refs):
    q_ref, kp_ref, kc_ref, km_ref, vp_ref, vc_ref, vm_ref, s_ref = refs
    qs = [q_ref[:, _head_cols(i)] * (HEAD_DIM ** -0.5) for i in range(Q_HEADS)]
    k3 = [[ref[:, _head_cols(h)] for ref in (kp_ref, kc_ref, km_ref)] for h in range(KV_HEADS)]
    v3 = [[ref[:, _head_cols(h)] for ref in (vp_ref, vc_ref, vm_ref)] for h in range(KV_HEADS)]
    return (qs, [k3[i // GROUP] for i in range(Q_HEADS)], [v3[i // GROUP] for i in range(Q_HEADS)],
            [s_ref[:, i:i + 1] for i in range(Q_HEADS)])


def _attention(q, k, v, sinks, *, name):
    lp = q.shape[0]
    nb = lp // BLOCK
    prev = lambda i: (jnp.maximum(i - 1, 0), 0)
    cur = lambda i: (i, 0)
    meta = lambda i: (0, 0)
    kv = lambda index: pl.BlockSpec((BLOCK, KV_W), index)

    def body(*refs):
        o_ref = refs[-1]
        qs, k3s, v3s, sink = _attn_operands(refs[:-1])
        p, _ = _attn_probs(qs, k3s, sink, _attn_masks(pl.program_id(0)))
        out = [_dot(ph[0], v3[0]) + _dot(ph[1], v3[1]) + _dot(ph[2], v3[2]) for ph, v3 in zip(p, v3s)]
        for i in range(Q_HEADS):
            o_ref[:, _head_cols(i)] = out[i].astype(o_ref.dtype)

    return pl.pallas_call(
        body, name=name, grid=(nb,),
        in_specs=[pl.BlockSpec((BLOCK, Q_W), cur), kv(prev), kv(cur), kv(meta), kv(prev), kv(cur), kv(meta),
                  _full((1, Q_HEADS))],
        out_specs=pl.BlockSpec((BLOCK, Q_W), cur),
        out_shape=jax.ShapeDtypeStruct((lp, Q_W), bf16),
        compiler_params=_params(("parallel",)),
    )(q, k, k, k, v, v, v, sinks)


def _attention_bwd(q, k, v, sinks, do, *, name):
    lp = q.shape[0]
    nb = lp // BLOCK
    cur = lambda n: (jnp.minimum(n, nb - 1), 0)
    prev = lambda n: (jnp.maximum(jnp.minimum(n, nb - 1) - 1, 0), 0)
    behind = lambda n: (jnp.maximum(n - 1, 0), 0)
    meta = lambda n: (0, 0)
    kv = lambda index: pl.BlockSpec((BLOCK, KV_W), index)
    scale = HEAD_DIM ** -0.5

    def body(*refs):
        ins, do_ref = refs[:8], refs[8]
        dq_ref, dk_ref, dv_ref, dkm_ref, dvm_ref, ds_ref, carry_k, carry_v = refs[9:]
        n = pl.program_id(0)

        @pl.when(n == 0)
        def _():
            for ref in (dkm_ref, dvm_ref, ds_ref, carry_k, carry_v):
                ref[...] = jnp.zeros_like(ref)

        @pl.when(n < nb)
        def _():
            qs, k3s, v3s, sink = _attn_operands(ins)
            do = [do_ref[:, _head_cols(i)] for i in range(Q_HEADS)]
            p, p_sink = _attn_probs(qs, k3s, sink, _attn_masks(n))
            out = [_dot(ph[0], v3[0]) + _dot(ph[1], v3[1]) + _dot(ph[2], v3[2]) for ph, v3 in zip(p, v3s)]
            delta = [jnp.sum(d * o, -1, keepdims=True) for d, o in zip(do, out)]
            dp = [[_dot(d, vx, "nt") for vx in v3] for d, v3 in zip(do, v3s)]
            ds = [[px * (dx - dl) for px, dx in zip(ph, dh)] for ph, dh, dl in zip(p, dp, delta)]
            dq = [_dot(dsh[0], k3[0]) + _dot(dsh[1], k3[1]) + _dot(dsh[2], k3[2]) for dsh, k3 in zip(ds, k3s)]
            for i in range(Q_HEADS):
                dq_ref[:, _head_cols(i)] = dq[i] * scale
                ds_ref[:, i:i + 1] -= jnp.sum(p_sink[i] * delta[i], axis=0, keepdims=True)
            for h in range(KV_HEADS):
                group = slice(h * GROUP, (h + 1) * GROUP)
                q_all = jnp.concatenate(qs[group], axis=0)
                do_all = jnp.concatenate(do[group], axis=0)
                dk3 = [_dot(jnp.concatenate([dsh[x] for dsh in ds[group]], axis=0), q_all, "tn") for x in range(3)]
                dv3 = [_dot(jnp.concatenate([ph[x] for ph in p[group]], axis=0), do_all, "tn") for x in range(3)]
                hs = _head_cols(h)
                for out_ref, carry, meta_ref, d3 in ((dk_ref, carry_k, dkm_ref, dk3),
                                                     (dv_ref, carry_v, dvm_ref, dv3)):
                    out_ref[:, hs] = carry[:, hs] + d3[0]
                    carry[:, hs] = d3[1]
                    meta_ref[:, hs] += d3[2]

        @pl.when(n == nb)
        def _():
            dk_ref[...] = carry_k[...]
            dv_ref[...] = carry_v[...]

    kv_shape = jax.ShapeDtypeStruct((lp, KV_W), f32)
    one_shape = jax.ShapeDtypeStruct((BLOCK, KV_W), f32)
    return pl.pallas_call(
        body, name=name, grid=(nb + 1,),
        in_specs=[pl.BlockSpec((BLOCK, Q_W), cur), kv(prev), kv(cur), kv(meta), kv(prev), kv(cur), kv(meta),
                  _full((1, Q_HEADS)), pl.BlockSpec((BLOCK, Q_W), cur)],
        out_specs=[pl.BlockSpec((BLOCK, Q_W), cur), kv(behind), kv(behind), kv(meta), kv(meta),
                   _full((1, Q_HEADS))],
        out_shape=[jax.ShapeDtypeStruct((lp, Q_W), f32), kv_shape, kv_shape, one_shape, one_shape,
                   jax.ShapeDtypeStruct((1, Q_HEADS), f32)],
        scratch_shapes=[pltpu.VMEM((BLOCK, KV_W), f32), pltpu.VMEM((BLOCK, KV_W), f32)],
        compiler_params=_params(("arbitrary",)),
    )(q, k, k, k, v, v, v, sinks, do)


@jax.custom_vjp
def _known_inverse(l, x):
    return x


def _known_inverse_fwd(l, x):
    return x, x


def _known_inverse_bwd(x, ct):
    return _dot(_dot(x, ct, "tn"), x, "nt"), jnp.zeros_like(x)


_known_inverse.defvjp(_known_inverse_fwd, _known_inverse_bwd)


def _scan_chunk(s0, r, lw, k, v, a, b, inv=None):
    t = r[0].shape[0]
    ii = lax.broadcasted_iota(jnp.int32, (t, t), 0)
    jj = lax.broadcasted_iota(jnp.int32, (t, t), 1)
    incl = jj <= ii
    strict = jj < ii
    tri = incl.astype(f32)
    eye = jnp.where(ii == jj, 1.0, 0.0)
    cl = [_const_dot(tri, x) for x in lw]
    e_pos = [jnp.exp(c) for c in cl]
    e_neg = [jnp.exp(-c) for c in cl]
    e_prev = [jnp.exp(c - x) for c, x in zip(cl, lw)]
    rt = [x * e for x, e in zip(r, e_pos)]
    at = [x * e for x, e in zip(a, e_prev)]
    bt = [x * e for x, e in zip(b, e_neg)]
    kt = [x * e for x, e in zip(k, e_neg)]
    l_ab = [jnp.where(strict, _dot(x, y, "nt"), 0.0) for x, y in zip(at, bt)]
    l_ak = [jnp.where(strict, _dot(x, y, "nt"), 0.0) for x, y in zip(at, kt)]
    r_b = [jnp.where(incl, _dot(x, y, "nt"), 0.0) for x, y in zip(rt, bt)]
    r_k = [jnp.where(incl, _dot(x, y, "nt"), 0.0) for x, y in zip(rt, kt)]
    if inv is None:
        inv = [eye + x for x in l_ab]
        pw = l_ab
        for _ in range(int(math.log2(t)) - 1):
            pw = [_dot(x, x) for x in pw]
            inv = [x + _dot(x, y) for x, y in zip(inv, pw)]
    else:
        inv = [_known_inverse(x, y) for x, y in zip(l_ab, inv)]
    rhs = [_dot(x, s, "nt") + _dot(m, y) for x, s, m, y in zip(at, s0, l_ak, v)]
    u = [_dot(x, y) for x, y in zip(inv, rhs)]
    y_s = [_dot(x, s, "nt") for x, s in zip(rt, s0)]
    y = [ys + _dot(m, uu) + _dot(n, vv) for ys, m, uu, n, vv in zip(y_s, r_b, u, r_k, v)]
    grow = [s + _dot(uu, x, "tn") + _dot(vv, z, "tn") for s, uu, x, vv, z in zip(s0, u, bt, v, kt)]
    s1 = [g * e[t - 1:t, :] for g, e in zip(grow, e_pos)]
    return y, s1, inv


def _head_rows(h):
    return slice(h * RWKV_HEAD, (h + 1) * RWKV_HEAD)


def _per_head(ref):
    return [ref[:, _head_rows(h)] for h in range(RWKV_HEADS)]


def _scan(r, lw, k, v, a, b, *, name):
    lp = r.shape[0]
    nc = lp // CHUNK
    row = pl.BlockSpec((CHUNK, RWKV_DIM), lambda c: (c, 0))

    def body(r_ref, lw_ref, k_ref, v_ref, a_ref, b_ref, y_ref, s_ref, inv_ref, state):
        @pl.when(pl.program_id(0) == 0)
        def _():
            state[...] = jnp.zeros_like(state)

        s_ref[...] = state[...]
        s0 = [state[_head_rows(h), :] for h in range(RWKV_HEADS)]
        y, s1, inv = _scan_chunk(s0, *[_per_head(ref) for ref in (r_ref, lw_ref, k_ref, v_ref, a_ref, b_ref)])
        for h in range(RWKV_HEADS):
            y_ref[:, _head_rows(h)] = y[h]
            state[_head_rows(h), :] = s1[h]
            inv_ref[h * CHUNK:(h + 1) * CHUNK, :] = inv[h].astype(inv_ref.dtype)

    return pl.pallas_call(
        body, name=name, grid=(nc,), in_specs=[row] * 6,
        out_specs=[row, pl.BlockSpec((RWKV_DIM, RWKV_HEAD), lambda c: (c, 0)),
                   pl.BlockSpec((RWKV_HEADS * CHUNK, CHUNK), lambda c: (c, 0))],
        out_shape=[jax.ShapeDtypeStruct((lp, RWKV_DIM), f32), jax.ShapeDtypeStruct((nc * RWKV_DIM, RWKV_HEAD), f32),
                   jax.ShapeDtypeStruct((nc * RWKV_HEADS * CHUNK, CHUNK), bf16)],
        scratch_shapes=[pltpu.VMEM((RWKV_DIM, RWKV_HEAD), f32)],
        compiler_params=_params(("arbitrary",)),
    )(r, lw, k, v, a, b)


def _scan_bwd(r, lw, k, v, a, b, states, inverses, dy, *, name):
    lp = r.shape[0]
    nc = lp // CHUNK
    back = lambda c: (nc - 1 - c, 0)
    row = pl.BlockSpec((CHUNK, RWKV_DIM), back)

    def body(r_ref, lw_ref, k_ref, v_ref, a_ref, b_ref, s_ref, inv_ref, dy_ref,
             dr_ref, dlw_ref, dk_ref, dv_ref, da_ref, db_ref, dstate):
        @pl.when(pl.program_id(0) == 0)
        def _():
            dstate[...] = jnp.zeros_like(dstate)

        outs = (dr_ref, dlw_ref, dk_ref, dv_ref, da_ref, db_ref)
        s0 = [s_ref[_head_rows(h), :] for h in range(RWKV_HEADS)]
        inv = [inv_ref[h * CHUNK:(h + 1) * CHUNK, :].astype(f32) for h in range(RWKV_HEADS)]
        _, vjp = jax.vjp(lambda *args: _scan_chunk(*args, inv=inv)[:2], s0,
                         *[_per_head(ref) for ref in (r_ref, lw_ref, k_ref, v_ref, a_ref, b_ref)])
        g = vjp((_per_head(dy_ref), [dstate[_head_rows(h), :] for h in range(RWKV_HEADS)]))
        for h in range(RWKV_HEADS):
            dstate[_head_rows(h), :] = g[0][h]
            for o_ref, gv in zip(outs, g[1:]):
                o_ref[:, _head_rows(h)] = gv[h]

    shape = jax.ShapeDtypeStruct((lp, RWKV_DIM), f32)
    return pl.pallas_call(
        body, name=name, grid=(nc,),
        in_specs=[row] * 6 + [pl.BlockSpec((RWKV_DIM, RWKV_HEAD), back),
                              pl.BlockSpec((RWKV_HEADS * CHUNK, CHUNK), back), row],
        out_specs=[row] * 6, out_shape=[shape] * 6,
        scratch_shapes=[pltpu.VMEM((RWKV_DIM, RWKV_HEAD), f32)],
        compiler_params=_params(("arbitrary",)),
    )(r, lw, k, v, a, b, states, inverses, dy)


def _loss_head(h2, target, g_final, *, name):
    lp = h2.shape[0]
    tm = BLOCK
    front_tiles = FRONT // tm

    def body(h_ref, t_ref, g_ref, loss_ref, dh_ref, dg_ref):
        i = pl.program_id(0)
        real = i >= front_tiles

        def tile_loss(hv, gv):
            err = _rms(hv, gv) - t_ref[...]
            return jnp.where(real, 0.5 * jnp.sum(jnp.mean(err * err, axis=-1, keepdims=True)), 0.0)

        loss, (dh, dg) = jax.value_and_grad(tile_loss, argnums=(0, 1))(h_ref[...], g_ref[...])

        @pl.when(i == 0)
        def _():
            loss_ref[...] = jnp.zeros_like(loss_ref)
            dg_ref[...] = jnp.zeros_like(dg_ref)

        loss_ref[...] += jnp.full(loss_ref.shape, loss, f32)
        dg_ref[...] += dg
        dh_ref[...] = dh

    return pl.pallas_call(
        body, name=name, grid=(lp // tm,),
        in_specs=[pl.BlockSpec((tm, D_MODEL), lambda i: (i, 0)),
                  pl.BlockSpec((tm, D_MODEL), lambda i: (jnp.maximum(i - front_tiles, 0), 0)),
                  _full(g_final.shape)],
        out_specs=[_full((8, 128)), pl.BlockSpec((tm, D_MODEL), lambda i: (i, 0)), _full(g_final.shape)],
        out_shape=[jax.ShapeDtypeStruct((8, 128), f32), jax.ShapeDtypeStruct((lp, D_MODEL), f32),
                   jax.ShapeDtypeStruct(g_final.shape, f32)],
        compiler_params=_params(("arbitrary",)),
    )(h2, target, g_final)


def _local_step(x, target, meta, p, late_weights=None, emit=None):
    emit = emit or (lambda group, grads: 0.0)
    seq = x.shape[0]
    lp = seq + FRONT
    h0 = jnp.concatenate([jnp.zeros((PAD, D_MODEL), f32), meta, x], axis=0)
    cos_t, sin_t, swap = _rope_tables(lp)
    hsum = _head_sum_matrix(RWKV_DIM, RWKV_HEAD)
    hmean = hsum / RWKV_HEAD
    w_qkv_t, w_rkv_t = p["w_in_t"][:ATTN_PROJ], p["w_in_t"][ATTN_PROJ:ATTN_PROJ + RKV_W]
    w_lora_t, w_gates_t = p["w_in_t"][ATTN_PROJ + RKV_W:ATTN_PROJ + RWKV_PROJ], p["w_in_t"][ATTN_PROJ + RWKV_PROJ:]
    b_qkv, b_rkv = p["b_in"][:, :ATTN_PROJ], p["b_in"][:, ATTN_PROJ:ATTN_PROJ + RKV_W]
    b_lora, b_gates = p["b_in"][:, ATTN_PROJ + RKV_W:ATTN_PROJ + RWKV_PROJ], p["b_in"][:, ATTN_PROJ + RWKV_PROJ:]
    prep_params = [p["w0"], p["w2"], p["a0"], p["a2"], p["g2"], p["k_k"], p["k_a"], hsum]
    post_params = [p["ln_w"], p["ln_b"], p["r_k"], hmean]

    (u,) = _rowwise(lambda hv, g: (_rms(hv, g),), [h0], [p["norm_mix_g"]], [(D_MODEL, bf16)], name="norm_mix")
    qkv = _mm(u, w_qkv_t, "nt", name="proj_qkv", bias=b_qkv, zero_rows_below=PAD)
    p_rkv = _mm(u, w_rkv_t, "nt", name="proj_rkv", bias=b_rkv, zero_rows_below=PAD)
    p_lora = _mm(u, w_lora_t, "nt", name="proj_lora", bias=b_lora, zero_rows_below=PAD)
    gates = _mm(u, w_gates_t, "nt", name="proj_gates", bias=b_gates, zero_rows_below=PAD)

    q, k, v = _rowwise(_attn_prep, [qkv, cos_t, sin_t], [swap], [(Q_W, bf16), (KV_W, bf16), (KV_W, bf16)],
                       name="attn_prep")
    y_attn = _attention(q, k, v, p["sinks"], name="attention")

    mix_rkv, mix_lora = p["mix"][:, :RKV_W], p["mix"][:, RKV_W:]
    pf_rkv = _token_shift(p_rkv, mix_rkv, name="shift_rkv")
    pf_lora = _token_shift(p_lora, mix_lora, name="shift_lora")
    wide = [(RWKV_DIM, f32)] * 7
    r_, lw_, k_, v_, a_, b_, g_ = _rowwise(_rwkv_prep, [pf_rkv, pf_lora], prep_params, wide, name="rwkv_prep")
    y_scan, states, inverses = _scan(r_, lw_, k_, v_, a_, b_, name="wkv_scan")
    (y_rwkv,) = _rowwise(_rwkv_post, [y_scan, r_, k_, v_, g_], post_params, [(RWKV_DIM, bf16)], name="rwkv_post")

    if late_weights is not None:
        p = {**p, **late_weights(y_rwkv)}
    br_a, br_r, merged = _branch_merge(y_attn, y_rwkv, p["w_br_attn_t"], p["w_br_rwkv_t"], gates, name="branch_merge")
    h1 = _mm(merged, p["w_o"], "nn", name="out_proj", add=h0)
    (f,) = _rowwise(lambda hv, g: (_rms(hv, g),), [h1], [p["norm_ffn_g"]], [(D_MODEL, bf16)], name="norm_ffn")
    gate, up, act = _ffn_in(f, p["w_gate_t"], p["w_up_t"], name="ffn_in")
    h2 = _mm(act, p["w_down"], "nn", name="ffn_down", add=h1)

    loss8, dh2, d_final_g = _loss_head(h2, target, p["norm_final_g"], name="loss_head")
    dgate, dup = _ffn_in_bwd(dh2, p["w_down"], gate, up, name="ffn_in_bwd")
    d_w_down = _mm_tn(act, dh2, name="dw_down")
    d_w_gate_t = _mm_tn(dgate, f, name="dw_gate")
    d_w_up_t = _mm_tn(dup, f, name="dw_up")
    zero = emit("ffn", dict(w_down=d_w_down, w_gate_t=d_w_gate_t, w_up_t=d_w_up_t))
    df = _mm(dgate, p["w_gate_t"], "nn", name="d_f_gate")
    df = _mm(dup, p["w_up_t"], "nn", name="d_f_up", add=df)
    dh1, d_ffn_g = _rowwise_bwd(lambda hv, g: (_rms(hv, g), hv), [h1], [p["norm_ffn_g"] + zero], [[df], [dh2]],
                                name="norm_ffn_bwd", diff_rows=[True], diff_params=[True])
    dgates, dbr_a, dbr_r = _branch_merge_bwd(dh1, p["w_o"], gates, br_a, br_r, name="branch_merge_bwd")
    d_w_o = _mm_tn(merged, dh1, name="dw_o")
    d_w_br_attn_t = _mm_tn(dbr_a, y_attn, name="dw_br_attn")
    d_w_br_rwkv_t = _mm_tn(dbr_r, y_rwkv, name="dw_br_rwkv")
    zero = emit("branch", dict(w_o=d_w_o, w_br_attn_t=d_w_br_attn_t, w_br_rwkv_t=d_w_br_rwkv_t))
    dy_attn = _mm(dbr_a, p["w_br_attn_t"], "nn", name="d_y_attn")
    dy_rwkv = _mm(dbr_r, p["w_br_rwkv_t"], "nn", name="d_y_rwkv")

    post_params = [p["ln_w"] + zero, p["ln_b"], p["r_k"], hmean]
    res = _rowwise_bwd(_rwkv_post, [y_scan, r_, k_, v_, g_], post_params, [[dy_rwkv]], name="rwkv_post_bwd",
                       diff_rows=[True] * 5, diff_params=[True, True, True, False])
    dy_scan, dr_p, dk_p, dv_p, dg_p, d_ln_w, d_ln_b, d_r_k = res
    dr_s, dlw_s, dk_s, dv_s, da_s, db_s = _scan_bwd(r_, lw_, k_, v_, a_, b_, states, inverses, dy_scan,
                                                    name="wkv_scan_bwd")
    res = _rowwise_bwd(_rwkv_prep, [pf_rkv, pf_lora], prep_params,
                       [[dr_s, dr_p], [dlw_s], [dk_s, dk_p], [dv_s, dv_p], [da_s], [db_s], [dg_p]],
                       name="rwkv_prep_bwd", diff_rows=[True, True], diff_params=[True] * 7 + [False],
                       zero_rows_below=PAD)
    dpf_rkv, dpf_lora, d_w0, d_w2, d_a0, d_a2, d_g2, d_k_k, d_k_a = res
    dp_rkv, d_mix_rkv = _token_shift_bwd(p_rkv, mix_rkv, dpf_rkv, name="shift_rkv_bwd")
    dp_lora, d_mix_lora = _token_shift_bwd(p_lora, mix_lora, dpf_lora, name="shift_lora_bwd")

    dq, dk, dv, dkm, dvm, d_sinks = _attention_bwd(q, k, v, p["sinks"], dy_attn, name="attention_bwd")
    rest = jnp.zeros((lp - BLOCK, KV_W), f32)
    dkm, dvm = jnp.concatenate([dkm, rest], axis=0), jnp.concatenate([dvm, rest], axis=0)
    (dqkv,) = _rowwise_bwd(_attn_prep, [qkv, cos_t, sin_t], [swap], [[dq], [dk, dkm], [dv, dvm]], name="attn_prep_bwd",
                           diff_rows=[True, False, False], diff_params=[False], out_dtypes=[bf16])

    d_w_qkv_t, db_qkv = _mm_tn(dqkv, u, name="dw_qkv", colsum=True)
    d_w_rkv_t, db_rkv = _mm_tn(dp_rkv, u, name="dw_rkv", colsum=True)
    d_w_lora_t, db_lora = _mm_tn(dp_lora, u, name="dw_lora", colsum=True)
    d_w_gates_t, db_gates = _mm_tn(dgates, u, name="dw_gates", colsum=True)
    d_w_in_t = jnp.concatenate([d_w_qkv_t, d_w_rkv_t, d_w_lora_t, d_w_gates_t], axis=0)
    zero = emit("input", dict(w_in_t=d_w_in_t, g2=d_g2, w2=d_w2, a2=d_a2))
    du = _mm(dqkv, w_qkv_t, "nn", name="d_u_qkv")
    du = _mm(dp_rkv, w_rkv_t, "nn", name="d_u_rkv", add=du)
    du = _mm(dp_lora, w_lora_t, "nn", name="d_u_lora", add=du)
    du = _mm(dgates, w_gates_t, "nn", name="d_u_gates", add=du)
    dh0, d_mix_g = _rowwise_bwd(lambda hv, g: (_rms(hv, g), hv), [h0], [p["norm_mix_g"] + zero], [[du], [dh1]],
                                name="norm_mix_bwd", diff_rows=[True], diff_params=[True])

    grads = dict(
        w_in_t=d_w_in_t,
        b_in=jnp.concatenate([db_qkv, db_rkv, db_lora, db_gates], axis=1),
        mix=jnp.concatenate([d_mix_rkv, d_mix_lora], axis=1),
        norm_mix_g=d_mix_g, sinks=d_sinks, w0=d_w0, w2=d_w2, a0=d_a0, a2=d_a2, g2=d_g2, k_k=d_k_k, k_a=d_k_a,
        r_k=d_r_k, ln_w=d_ln_w, ln_b=d_ln_b, w_br_attn_t=d_w_br_attn_t, w_br_rwkv_t=d_w_br_rwkv_t, w_o=d_w_o,
        norm_ffn_g=d_ffn_g, w_gate_t=d_w_gate_t, w_up_t=d_w_up_t, w_down=d_w_down, norm_final_g=d_final_g,
        meta=dh0[PAD:FRONT],
    )
    return loss8[0, 0], dh0[FRONT:], grads


def _position():
    return lax.axis_index("x"), lax.axis_index("y"), lax.axis_index("c")


def _other_chips(x, y):
    return [(1 - x, y), (x, 1 - y), (1 - x, 1 - y)]


_HBM = pl.BlockSpec(memory_space=pltpu.HBM)
_SEM = pl.BlockSpec(memory_space=pltpu.SEMAPHORE)
_EFFECT = pltpu.SideEffectType.DATAFLOW_SIDE_EFFECTING


def _chip_copies(src_refs, land_refs, send_sems, recv_sems, gather):
    x, y, c = _position()
    copies = []
    for a, (src, land) in enumerate(zip(src_refs, land_refs)):
        for j, (px, py) in enumerate(_other_chips(x, y)):
            copies.append(pltpu.make_async_remote_copy(
                src_ref=src if gather else src.at[2 * px + py],
                dst_ref=land.at[2 * x + y] if gather else land.at[j],
                send_sem=send_sems.at[3 * a + j], recv_sem=recv_sems.at[3 * a + j],
                device_id=(px, py, c), device_id_type=MESH))
    return copies


def _exchange_start(srcs, *, gather, name):
    n = len(srcs)
    lands = [lax.empty((N_CHIPS,) + s.shape if gather else (3,) + s.shape[1:], s.dtype) for s in srcs]

    def body(*refs):
        for cp in _chip_copies(refs[:n], refs[n:2 * n], refs[2 * n], refs[2 * n + 1], gather):
            cp.start()
        refs[-1][...] = jnp.zeros_like(refs[-1])

    res = pl.pallas_call(
        body, name=name,
        out_shape=(pltpu.SemaphoreType.DMA((3 * n,)), pltpu.SemaphoreType.DMA((3 * n,)),
                   *[pltpu.HBM(a.shape, a.dtype) for a in srcs + lands], jax.ShapeDtypeStruct((8, 128), f32)),
        in_specs=[_HBM] * (2 * n),
        out_specs=(_SEM, _SEM, *[_HBM] * (2 * n), pl.BlockSpec(memory_space=pltpu.VMEM)),
        input_output_aliases={i: 2 + i for i in range(2 * n)},
        compiler_params=pltpu.CompilerParams(has_side_effects=_EFFECT),
    )(*[pltpu.with_memory_space_constraint(a, pltpu.HBM) for a in srcs + lands])
    return res[0], res[1], list(res[2:2 + n]), list(res[2 + n:2 + 2 * n]), res[-1]


def _exchange_wait(handle, after, *, gather, name):
    send_sems, recv_sems, srcs, lands, _ = handle
    n = len(srcs)

    def body(*refs):
        for cp in _chip_copies(refs[:n], refs[n:2 * n], refs[2 * n], refs[2 * n + 1], gather):
            cp.wait_send()
            cp.wait_recv()

    res = pl.pallas_call(
        body, name=name,
        out_shape=tuple(pltpu.HBM(a.shape, a.dtype) for a in srcs + lands),
        in_specs=[_HBM] * (2 * n) + [_SEM, _SEM, pl.BlockSpec(memory_space=pl.ANY)],
        out_specs=tuple([_HBM] * (2 * n)),
        input_output_aliases={i: i for i in range(2 * n)},
        compiler_params=pltpu.CompilerParams(has_side_effects=_EFFECT),
    )(*srcs, *lands, send_sems, recv_sems, after)
    return list(res[:n]), list(res[n:])


def _sum_own_and_received(g, recv, *, name):
    _, r, w = g.shape
    tm = _tile(r)
    if g.dtype == bf16 and tm % 16:
        tm = r
    x, y, _ = _position()
    me = jnp.reshape(2 * x + y, (1,)).astype(jnp.int32)

    def body(me_ref, g_ref, r_ref, o_ref):
        o_ref[...] = (g_ref[0].astype(f32) + r_ref[0].astype(f32)) + (r_ref[1].astype(f32) + r_ref[2].astype(f32))

    return pl.pallas_call(
        body, name=name,
        grid_spec=pltpu.PrefetchScalarGridSpec(
            num_scalar_prefetch=1, grid=(r // tm,),
            in_specs=[pl.BlockSpec((1, tm, w), lambda i, me_ref: (me_ref[0], i, 0)),
                      pl.BlockSpec((3, tm, w), lambda i, me_ref: (0, i, 0))],
            out_specs=pl.BlockSpec((tm, w), lambda i, me_ref: (i, 0))),
        out_shape=jax.ShapeDtypeStruct((r, w), f32),
        compiler_params=_params(("parallel",)),
    )(me, g, recv)


def _swap_cores(arrs, *, name):
    n = len(arrs)

    def body(*refs):
        x, y, c = _position()
        copies = [pltpu.make_async_remote_copy(
            src_ref=refs[i], dst_ref=refs[n + i], send_sem=refs[2 * n].at[i], recv_sem=refs[2 * n + 1].at[i],
            device_id=(x, y, 1 - c), device_id_type=MESH) for i in range(n)]
        for cp in copies:
            cp.start()
        for cp in copies:
            cp.wait_recv()
        for cp in copies:
            cp.wait_send()

    return pl.pallas_call(
        body, name=name,
        in_specs=[pl.BlockSpec(memory_space=pl.ANY)] * n,
        out_specs=[pl.BlockSpec(memory_space=pl.ANY)] * n,
        out_shape=[jax.ShapeDtypeStruct(a.shape, a.dtype) for a in arrs],
        scratch_shapes=[pltpu.SemaphoreType.DMA((n,)), pltpu.SemaphoreType.DMA((n,))],
    )(*arrs)


def _all_reduce_small(a, *, name):
    rows, w = a.shape

    def body(a_ref, o_ref, buf, send_sems, recv_sems):
        x, y, c = _position()
        me = 4 * x + 2 * y + c
        buf[0] = a_ref[...]
        sends = []
        for rel in range(1, N_DEV):
            peer = ((1 - x) if rel & 4 else x, (1 - y) if rel & 2 else y, (1 - c) if rel & 1 else c)
            cp = pltpu.make_async_remote_copy(
                src_ref=a_ref, dst_ref=buf.at[rel], send_sem=send_sems.at[rel - 1], recv_sem=recv_sems.at[rel - 1],
                device_id=peer, device_id_type=MESH)
            cp.start()
            sends.append(cp)
        for cp in sends:
            cp.wait_recv()
        for cp in sends:
            cp.wait_send()
        acc = buf[jnp.bitwise_xor(me, 0)]
        for d in range(1, N_DEV):
            acc = acc + buf[jnp.bitwise_xor(me, d)]
        o_ref[...] = acc

    return pl.pallas_call(
        body, name=name,
        in_specs=[pl.BlockSpec(memory_space=pltpu.VMEM)],
        out_specs=pl.BlockSpec(memory_space=pltpu.VMEM),
        out_shape=jax.ShapeDtypeStruct((rows, w), f32),
        scratch_shapes=[pltpu.VMEM((N_DEV, rows, w), f32), pltpu.SemaphoreType.DMA((N_DEV - 1,)),
                        pltpu.SemaphoreType.DMA((N_DEV - 1,))],
    )(a)


def _adamw(w, g_parts, m, v, *, name, transposed=False):
    rows, cols = w.shape
    if transposed:
        tm = 256 if rows % 256 == 0 else rows
        g_spec = pl.BlockSpec((cols, tm), lambda i: (0, i))
    else:
        tm = _tile(rows, 256)
        g_spec = pl.BlockSpec((tm, cols), lambda i: (i, 0))
    n = len(g_parts)

    def body(*refs):
        w_ref, m_ref, v_ref = refs[0], refs[1 + n], refs[2 + n]
        g_ref, d_ref, nm_ref, nv_ref = refs[3 + n:]
        gv = refs[1][...]
        for part in refs[2:1 + n]:
            gv = gv + part[...]
        if transposed:
            gv = gv.T
        g_ref[...] = gv
        nm = ADAM_B1 * m_ref[...] + (1.0 - ADAM_B1) * gv
        nv = ADAM_B2 * v_ref[...] + (1.0 - ADAM_B2) * (gv * gv)
        m_hat = nm / (1.0 - ADAM_B1 ** ADAM_STEP)
        v_hat = nv / (1.0 - ADAM_B2 ** ADAM_STEP)
        d_ref[...] = -ADAM_LR * (m_hat / (jnp.sqrt(v_hat) + ADAM_EPS) + ADAM_WD * w_ref[...])
        nm_ref[...] = nm
        nv_ref[...] = nv

    spec = pl.BlockSpec((tm, cols), lambda i: (i, 0))
    shape = jax.ShapeDtypeStruct((rows, cols), f32)
    return pl.pallas_call(
        body, name=name, grid=(rows // tm,), in_specs=[spec] + [g_spec] * n + [spec] * 2,
        out_specs=[spec] * 4, out_shape=[shape] * 4,
        compiler_params=_params(("parallel",)),
    )(w, *g_parts, m, v)


def _pad_rows(a, rows):
    return jnp.concatenate([a, jnp.zeros((rows - a.shape[0], a.shape[1]), a.dtype)], axis=0) if rows > a.shape[0] else a


_SMALL = (("norm_mix_g", D_MODEL), ("b_in", D_IN), ("sinks", Q_HEADS), ("mix", RWKV_PROJ), ("w0", RWKV_DIM),
          ("a0", RWKV_DIM), ("k_k", RWKV_DIM), ("k_a", RWKV_DIM), ("r_k", RWKV_DIM), ("ln_w", RWKV_DIM),
          ("ln_b", RWKV_DIM), ("norm_ffn_g", D_MODEL), ("norm_final_g", D_MODEL))


def _pack_small(d):
    flat = jnp.concatenate([d[n].reshape(-1).astype(f32) for n, _ in _SMALL])
    return flat


def _unpack_small(flat):
    out, off = {}, 0
    for n, size in _SMALL:
        out[n] = flat[off:off + size]
        off += size
    return out


_SMALL_TOTAL = sum(s for _, s in _SMALL)


def kernel(x, meta_tokens, norm_mix_g, w_in, b_in, attn_sinks, rwkv_mix, rwkv_w0, rwkv_w2, rwkv_a0, rwkv_a2, rwkv_g2, rwkv_k_k, rwkv_k_a, rwkv_r_k, rwkv_ln_w, rwkv_ln_b, w_br_attn, w_br_rwkv, w_o, norm_ffn_g, w_ffn_gate, w_ffn_up, w_ffn_down, norm_final_g, loss_target, m_meta_tokens, m_norm_mix_g, m_w_in, m_b_in, m_attn_sinks, m_rwkv_mix, m_rwkv_w0, m_rwkv_w2, m_rwkv_a0, m_rwkv_a2, m_rwkv_g2, m_rwkv_k_k, m_rwkv_k_a, m_rwkv_r_k, m_rwkv_ln_w, m_rwkv_ln_b, m_w_br_attn, m_w_br_rwkv, m_w_o, m_norm_ffn_g, m_w_ffn_gate, m_w_ffn_up, m_w_ffn_down, m_norm_final_g, v_meta_tokens, v_norm_mix_g, v_w_in, v_b_in, v_attn_sinks, v_rwkv_mix, v_rwkv_w0, v_rwkv_w2, v_rwkv_a0, v_rwkv_a2, v_rwkv_g2, v_rwkv_k_k, v_rwkv_k_a, v_rwkv_r_k, v_rwkv_ln_w, v_rwkv_ln_b, v_w_br_attn, v_w_br_rwkv, v_w_o, v_norm_ffn_g, v_w_ffn_gate, v_w_ffn_up, v_w_ffn_down, v_norm_final_g):
    names = ("meta_tokens", "norm_mix_g", "w_in", "b_in", "attn_sinks", "rwkv_mix", "rwkv_w0", "rwkv_w2", "rwkv_a0",
             "rwkv_a2", "rwkv_g2", "rwkv_k_k", "rwkv_k_a", "rwkv_r_k", "rwkv_ln_w", "rwkv_ln_b", "w_br_attn",
             "w_br_rwkv", "w_o", "norm_ffn_g", "w_ffn_gate", "w_ffn_up", "w_ffn_down", "norm_final_g")
    w_all = dict(zip(names, (meta_tokens, norm_mix_g, w_in, b_in, attn_sinks, rwkv_mix, rwkv_w0, rwkv_w2, rwkv_a0,
                             rwkv_a2, rwkv_g2, rwkv_k_k, rwkv_k_a, rwkv_r_k, rwkv_ln_w, rwkv_ln_b, w_br_attn,
                             w_br_rwkv, w_o, norm_ffn_g, w_ffn_gate, w_ffn_up, w_ffn_down, norm_final_g)))
    m_all = dict(zip(names, (m_meta_tokens, m_norm_mix_g, m_w_in, m_b_in, m_attn_sinks, m_rwkv_mix, m_rwkv_w0,
                             m_rwkv_w2, m_rwkv_a0, m_rwkv_a2, m_rwkv_g2, m_rwkv_k_k, m_rwkv_k_a, m_rwkv_r_k,
                             m_rwkv_ln_w, m_rwkv_ln_b, m_w_br_attn, m_w_br_rwkv, m_w_o, m_norm_ffn_g, m_w_ffn_gate,
                             m_w_ffn_up, m_w_ffn_down, m_norm_final_g)))
    v_all = dict(zip(names, (v_meta_tokens, v_norm_mix_g, v_w_in, v_b_in, v_attn_sinks, v_rwkv_mix, v_rwkv_w0,
                             v_rwkv_w2, v_rwkv_a0, v_rwkv_a2, v_rwkv_g2, v_rwkv_k_k, v_rwkv_k_a, v_rwkv_r_k,
                             v_rwkv_ln_w, v_rwkv_ln_b, v_w_br_attn, v_w_br_rwkv, v_w_o, v_norm_ffn_g, v_w_ffn_gate,
                             v_w_ffn_up, v_w_ffn_down, v_norm_final_g)))
    cx, cy, _ = _position()
    chip = 2 * cx + cy

    t_of = dict(w_in_t="w_in", w_gate_t="w_ffn_gate", w_up_t="w_ffn_up", w_br_attn_t="w_br_attn",
                w_br_rwkv_t="w_br_rwkv", g2_t="rwkv_g2", w2_t="rwkv_w2", a2_t="rwkv_a2")
    plain_of = dict(w_down="w_ffn_down", w_o="w_o")
    meta_cols = meta_tokens.shape[1]

    def shard(k):
        return (w_all[t_of[k]][0].T if k in t_of else w_all[plain_of[k]][0]).astype(bf16)

    def whole(zone, own):
        return lax.dynamic_update_slice_in_dim(zone, own[None], chip, axis=0).reshape(-1, own.shape[-1])

    early = ("w_in_t", "g2_t", "w2_t", "a2_t")
    late = ("w_gate_t", "w_up_t", "w_down", "w_o", "w_br_attn_t", "w_br_rwkv_t")
    early_h = _exchange_start([shard(k) for k in early] + [meta_tokens], gather=True, name="gather_early_start")
    late_h = _exchange_start([shard(k) for k in late], gather=True, name="gather_late_start")
    own, zones = _exchange_wait(early_h, late_h[4], gather=True, name="gather_early_wait")
    got = {k: whole(z, o) for k, z, o in zip(early, zones, own)}
    meta_full = whole(zones[-1], own[-1]).reshape(N_CHIPS, N_META, meta_cols).transpose(1, 0, 2).reshape(N_META, -1)
    p = dict(
        w_in_t=got["w_in_t"], g2=got["g2_t"].T.astype(f32), w2=got["w2_t"].T.astype(f32),
        a2=got["a2_t"].T.astype(f32),
        b_in=b_in, sinks=attn_sinks, mix=rwkv_mix, w0=rwkv_w0, a0=rwkv_a0, k_k=rwkv_k_k, k_a=rwkv_k_a,
        r_k=rwkv_r_k.reshape(1, RWKV_DIM), ln_w=rwkv_ln_w, ln_b=rwkv_ln_b, norm_mix_g=norm_mix_g,
        norm_ffn_g=norm_ffn_g, norm_final_g=norm_final_g.reshape(1, D_MODEL),
    )

    def late_weights(after):
        own_l, zones_l = _exchange_wait(late_h, after, gather=True, name="gather_late_wait")
        return {k: whole(z, o) for k, z, o in zip(late, zones_l, own_l)}

    started = {}

    def emit(group, grads_):
        keys = list(grads_)
        slabs = []
        for k in keys:
            a = grads_[k].T if k in ("g2", "w2", "a2") else grads_[k]
            slabs.append(a.reshape(N_CHIPS, a.shape[0] // N_CHIPS, a.shape[1]))
        started[group] = (keys, _exchange_start(slabs, gather=False, name="scatter_" + group + "_start"))
        return started[group][1][4][0, 0]

    loss, dx, g = _local_step(x[0], loss_target[0], meta_full, p, late_weights, emit)

    def partial_sums(group, after):
        keys, handle = started[group]
        slabs, lands = _exchange_wait(handle, after, gather=False, name="scatter_" + group + "_wait")
        return {k: _sum_own_and_received(s, l, name="sum_chips_" + k) for k, s, l in zip(keys, slabs, lands)}

    parts = partial_sums("ffn", dx)
    parts.update(partial_sums("branch", parts["w_down"]))
    parts.update(partial_sums("input", parts["w_o"]))
    keys = list(parts)
    others = dict(zip(keys, _swap_cores([parts[k] for k in keys], name="swap_cores")))
    for short in ("g2", "w2", "a2"):
        parts[short + "_t"], others[short + "_t"] = parts.pop(short), others.pop(short)

    small = jnp.concatenate([_pack_small(g), loss.reshape(1)])
    small_rows = -(-small.shape[0] // PACK_W)
    small = jnp.concatenate([small, jnp.zeros((small_rows * PACK_W - small.shape[0],), f32)]).reshape(small_rows, PACK_W)
    small_rows8 = -(-(small_rows + N_META) // 8) * 8
    reduced = _all_reduce_small(_pad_rows(jnp.concatenate([g["meta"], small], axis=0), small_rows8),
                                name="reduce_small")
    g_meta = lax.dynamic_slice_in_dim(reduced[:N_META], chip * meta_cols, meta_cols, axis=1)
    flat = reduced[N_META:N_META + small_rows].reshape(-1)
    g_small = _unpack_small(flat)
    loss_total = flat[_SMALL_TOTAL]

    small_of = dict(norm_mix_g="norm_mix_g", b_in="b_in", attn_sinks="sinks", rwkv_mix="mix", rwkv_w0="w0",
                    rwkv_a0="a0", rwkv_k_k="k_k", rwkv_k_a="k_a", rwkv_r_k="r_k", rwkv_ln_w="ln_w",
                    rwkv_ln_b="ln_b", norm_ffn_g="norm_ffn_g", norm_final_g="norm_final_g")
    grads = {"meta_tokens": g_meta}
    for n, k in small_of.items():
        grads[n] = g_small[k].reshape(w_all[n].shape)

    delta, new_m, new_v = {}, {}, {}
    in_grad_layout = ("w_in_t", "w_gate_t", "w_up_t")
    for k, n in {**t_of, **plain_of}.items():
        shape2 = w_all[n].shape[1:]
        w_, m_, v_ = (a.reshape(shape2) for a in (w_all[n], m_all[n], v_all[n]))
        if k in in_grad_layout:
            res = [t.T for t in _adamw(w_.T, [parts[k], others[k]], m_.T, v_.T, name="adamw_" + n)]
        else:
            res = _adamw(w_, [parts[k], others[k]], m_, v_, name="adamw_" + n, transposed=k in t_of)
        grads[n], delta[n], new_m[n], new_v[n] = (t.reshape(w_all[n].shape) for t in res)
    rest = [n for n in names if n not in delta]

    def pack_rest(src):
        flat_ = jnp.concatenate([src[n].reshape(-1) for n in rest])
        rows_ = -(-flat_.shape[0] // (8 * PACK_W)) * 8
        return jnp.concatenate([flat_, jnp.ones((rows_ * PACK_W - flat_.shape[0],), f32)]).reshape(rows_, PACK_W)

    _, d_, m_, v_ = _adamw(pack_rest(w_all), [pack_rest(grads)], pack_rest(m_all), pack_rest(v_all),
                           name="adamw_small")
    off = 0
    for n in rest:
        size = w_all[n].size
        for dst, src in ((delta, d_), (new_m, m_), (new_v, v_)):
            dst[n] = src.reshape(-1)[off:off + size].reshape(w_all[n].shape)
        off += size

    return (loss_total, dx.reshape(x.shape), *[grads[n] for n in names], *[delta[n] for n in names],
            *[new_m[n] for n in names], *[new_v[n] for n in names])
```

```python
import math

import jax
import jax.numpy as jnp
from jax import lax
from jax.experimental import pallas as pl
from jax.experimental.pallas import tpu as pltpu

f32 = jnp.float32
bf16 = jnp.bfloat16

D_MODEL = 1024
N_META = 16
HEAD_DIM = 64
Q_HEADS = 8
KV_HEADS = 2
GROUP = Q_HEADS // KV_HEADS
WINDOW = 128
BLOCK = 128
ROPE_THETA = 500000.0
ROPE_DIM = HEAD_DIM // 4
RWKV_HEADS = 8
RWKV_HEAD = 64
RWKV_DIM = RWKV_HEADS * RWKV_HEAD
DECAY_LORA = 64
AAA_LORA = 64
GATE_LORA = 160
LORA_W = DECAY_LORA + AAA_LORA + GATE_LORA
RWKV_LN_EPS = 64e-5
D_FF = 2816
Q_W = Q_HEADS * HEAD_DIM
KV_W = KV_HEADS * HEAD_DIM
ATTN_PROJ = Q_W + 2 * KV_W
RKV_W = 3 * RWKV_DIM
RWKV_PROJ = RKV_W + LORA_W
D_IN = ATTN_PROJ + RWKV_PROJ + 2 * D_MODEL
RMS_EPS = 1e-6
NEG_INF = -1e30
PAD = BLOCK - N_META
FRONT = PAD + N_META

ADAM_LR = 0.001
ADAM_B1 = 0.9
ADAM_B2 = 0.999
ADAM_EPS = 1e-08
ADAM_WD = 0.01
ADAM_STEP = 10

N_CHIPS = 4
N_DEV = 8
CHUNK = 64
VMEM_LIMIT = 56 * 1024 * 1024
PACK_W = 1024
MESH = pl.DeviceIdType.MESH


def _tile(m, pref=384):
    for step in (16, 8):
        for t in range(min(m, pref) // step * step, 0, -step):
            if m % t == 0:
                return t
    return m


def _params(sem=None):
    return pltpu.CompilerParams(dimension_semantics=sem, vmem_limit_bytes=VMEM_LIMIT)


def _full(shape):
    nd = len(shape)
    return pl.BlockSpec(shape, lambda *_: (0,) * nd)


def _dot(a, b, dims="nn"):
    dn = {"nn": (((1,), (0,)), ((), ())), "nt": (((1,), (1,)), ((), ())), "tn": (((0,), (0,)), ((), ()))}[dims]
    return lax.dot_general(a.astype(bf16), b.astype(bf16), dn, preferred_element_type=f32)


def _two_pass(x, m):
    x_hi = x.astype(bf16)
    x_lo = (x - x_hi.astype(f32)).astype(bf16)
    return _dot(x_hi, m) + _dot(x_lo, m)


@jax.custom_vjp
def _dot_const(x, m):
    return _two_pass(x, m)


def _dot_const_fwd(x, m):
    return _two_pass(x, m), m


def _dot_const_bwd(m, ct):
    return _two_pass(ct, m.T), jnp.zeros_like(m)


_dot_const.defvjp(_dot_const_fwd, _dot_const_bwd)


def _two_pass_left(m, x, dims):
    x_hi = x.astype(bf16)
    x_lo = (x - x_hi.astype(f32)).astype(bf16)
    return _dot(m, x_hi, dims) + _dot(m, x_lo, dims)


@jax.custom_vjp
def _const_dot(m, x):
    return _two_pass_left(m, x, "nn")


def _const_dot_fwd(m, x):
    return _two_pass_left(m, x, "nn"), m


def _const_dot_bwd(m, ct):
    return jnp.zeros_like(m), _two_pass_left(m, ct, "tn")


_const_dot.defvjp(_const_dot_fwd, _const_dot_bwd)


def _mm(a, b, mode, *, name, out_dtype=f32, bias=None, add=None, zero_rows_below=0):
    m, _ = a.shape
    n = b.shape[1] if mode == "nn" else b.shape[0]
    tm = _tile(m)
    has_bias, has_add = bias is not None, add is not None

    def body(*refs):
        a_ref, b_ref = refs[0], refs[1]
        o_ref = refs[-1]
        acc = _dot(a_ref[...], b_ref[...], mode)
        k = 2
        if has_bias:
            acc = acc + refs[k][...]
            k += 1
        if zero_rows_below:
            rows = pl.program_id(0) * tm + lax.broadcasted_iota(jnp.int32, acc.shape, 0)
            acc = jnp.where(rows >= zero_rows_below, acc, 0.0)
        if has_add:
            acc = acc + refs[k][...].astype(f32)
        o_ref[...] = acc.astype(out_dtype)

    ins = [a, b]
    in_specs = [pl.BlockSpec((tm, a.shape[1]), lambda i: (i, 0)), _full(b.shape)]
    if has_bias:
        ins.append(bias)
        in_specs.append(_full(bias.shape))
    if has_add:
        ins.append(add)
        in_specs.append(pl.BlockSpec((tm, n), lambda i: (i, 0)))
    return pl.pallas_call(
        body, name=name, grid=(m // tm,), in_specs=in_specs,
        out_specs=pl.BlockSpec((tm, n), lambda i: (i, 0)),
        out_shape=jax.ShapeDtypeStruct((m, n), out_dtype),
        compiler_params=_params(("parallel",)),
    )(*ins)


def _mm_sum(a_list, b_list, *, name):
    m = a_list[0].shape[0]
    n = b_list[0].shape[1]
    k = len(a_list)
    tm = _tile(m)

    def body(*refs):
        acc = _dot(refs[0][...], refs[k][...])
        for i in range(1, k):
            acc = acc + _dot(refs[i][...], refs[k + i][...])
        refs[-1][...] = acc

    return pl.pallas_call(
        body, name=name, grid=(m // tm,),
        in_specs=[pl.BlockSpec((tm, a.shape[1]), lambda i: (i, 0)) for a in a_list] + [_full(b.shape) for b in b_list],
        out_specs=pl.BlockSpec((tm, n), lambda i: (i, 0)),
        out_shape=jax.ShapeDtypeStruct((m, n), f32),
        compiler_params=_params(("parallel",)),
    )(*a_list, *b_list)


def _mm_fanout(a, b_list, bias_list, *, name, zero_rows_below=0):
    m, kdim = a.shape
    k = len(b_list)
    tm = _tile(m)

    def body(*refs):
        av = refs[0][...]
        for j in range(k):
            acc = _dot(av, refs[1 + j][...], "nt") + refs[1 + k + j][...]
            if zero_rows_below:
                rows = pl.program_id(0) * tm + lax.broadcasted_iota(jnp.int32, acc.shape, 0)
                acc = jnp.where(rows >= zero_rows_below, acc, 0.0)
            refs[1 + 2 * k + j][...] = acc

    return pl.pallas_call(
        body, name=name, grid=(m // tm,),
        in_specs=[pl.BlockSpec((tm, kdim), lambda i: (i, 0))] + [_full(b.shape) for b in b_list]
        + [_full(c.shape) for c in bias_list],
        out_specs=[pl.BlockSpec((tm, b.shape[0]), lambda i: (i, 0)) for b in b_list],
        out_shape=[jax.ShapeDtypeStruct((m, b.shape[0]), f32) for b in b_list],
        compiler_params=_params(("parallel",)),
    )(a, *b_list, *bias_list)


def _mm_tn(a, b, *, name, colsum=False, out_dtype=bf16):
    r, m = a.shape
    n = b.shape[1]
    tr = _tile(r, 1408)
    tmo = m
    for cand in (1408, 1024, 768, 512):
        if m > 1024 and m % cand == 0:
            tmo = cand
            break
    steps = r // tr

    def body(a_ref, b_ref, o_ref, *rest):
        acc = rest[-1]
        i = pl.program_id(1)

        @pl.when(i == 0)
        def _():
            acc[...] = jnp.zeros_like(acc)
            if colsum:
                rest[0][...] = jnp.zeros_like(rest[0])

        acc[...] += _dot(a_ref[...], b_ref[...], "tn")
        if colsum:
            rest[0][...] += jnp.sum(a_ref[...].astype(f32), axis=0, keepdims=True)

        @pl.when(i == steps - 1)
        def _():
            o_ref[...] = acc[...].astype(out_dtype)

    out_shape = [jax.ShapeDtypeStruct((m, n), out_dtype)]
    out_specs = [pl.BlockSpec((tmo, n), lambda j, i: (j, 0))]
    if colsum:
        out_shape.append(jax.ShapeDtypeStruct((1, m), f32))
        out_specs.append(pl.BlockSpec((1, tmo), lambda j, i: (0, j)))
    res = pl.pallas_call(
        body, name=name, grid=(m // tmo, steps),
        in_specs=[pl.BlockSpec((tr, tmo), lambda j, i: (i, j)), pl.BlockSpec((tr, n), lambda j, i: (i, 0))],
        out_specs=out_specs, out_shape=out_shape,
        scratch_shapes=[pltpu.VMEM((tmo, n), f32)],
        compiler_params=_params(("parallel", "arbitrary")),
    )(a, b)
    return res if colsum else res[0]


def _rowwise(fn, rows, params, outs, *, name, tm=None):
    m = rows[0].shape[0]
    tm = tm or _tile(m)
    nr, npar = len(rows), len(params)

    def body(*refs):
        vals = [r[...] for r in refs[:nr + npar]]
        res = fn(*vals)
        for o_ref, v in zip(refs[nr + npar:], res):
            o_ref[...] = v.astype(o_ref.dtype)

    return pl.pallas_call(
        body, name=name, grid=(m // tm,),
        in_specs=[pl.BlockSpec((tm, r.shape[1]), lambda i: (i, 0)) for r in rows] + [_full(p.shape) for p in params],
        out_specs=[pl.BlockSpec((tm, w), lambda i: (i, 0)) for w, _ in outs],
        out_shape=[jax.ShapeDtypeStruct((m, w), dt) for w, dt in outs],
        compiler_params=_params(("parallel",)),
    )(*rows, *params)


def _rowwise_bwd(fn, rows, params, cts, *, name, diff_rows, diff_params, tm=None, zero_rows_below=0, out_dtypes=None):
    m = rows[0].shape[0]
    tm = tm or _tile(m)
    nr, npar = len(rows), len(params)
    d_idx = [i for i in range(nr) if diff_rows[i]]
    p_idx = [i for i in range(npar) if diff_params[i]]
    out_dtypes = out_dtypes or [f32] * len(d_idx)
    flat_cts = [c for group in cts for c in group]
    n_ct = len(flat_cts)

    def body(*refs):
        vals = [r[...] for r in refs[:nr + npar]]
        ct_refs = refs[nr + npar:nr + npar + n_ct]
        out_refs = refs[nr + npar + n_ct:]
        ct_vals, k = [], 0
        for group in cts:
            acc = ct_refs[k][...].astype(f32)
            for extra in range(1, len(group)):
                acc = acc + ct_refs[k + extra][...].astype(f32)
            k += len(group)
            if zero_rows_below:
                rr = pl.program_id(0) * tm + lax.broadcasted_iota(jnp.int32, acc.shape, 0)
                acc = jnp.where(rr >= zero_rows_below, acc, 0.0)
            ct_vals.append(acc)

        def g(*dargs):
            full = list(vals)
            for pos, i in enumerate(d_idx):
                full[i] = dargs[pos]
            for pos, i in enumerate(p_idx):
                full[nr + i] = dargs[len(d_idx) + pos]
            return tuple(fn(*full))

        _, vjp = jax.vjp(g, *[vals[i].astype(f32) for i in d_idx], *[vals[nr + i] for i in p_idx])
        grads = vjp(tuple(ct_vals))
        for pos in range(len(d_idx)):
            out_refs[pos][...] = grads[pos].astype(out_refs[pos].dtype)
        first = pl.program_id(0) == 0
        for pos in range(len(p_idx)):
            o_ref = out_refs[len(d_idx) + pos]

            @pl.when(first)
            def _(o_ref=o_ref):
                o_ref[...] = jnp.zeros_like(o_ref)

            o_ref[...] += grads[len(d_idx) + pos]

    return pl.pallas_call(
        body, name=name, grid=(m // tm,),
        in_specs=[pl.BlockSpec((tm, r.shape[1]), lambda i: (i, 0)) for r in rows] + [_full(p.shape) for p in params]
        + [pl.BlockSpec((tm, c.shape[1]), lambda i: (i, 0)) for c in flat_cts],
        out_specs=[pl.BlockSpec((tm, rows[i].shape[1]), lambda i_: (i_, 0)) for i in d_idx]
        + [_full(params[i].shape) for i in p_idx],
        out_shape=[jax.ShapeDtypeStruct(rows[i].shape, dt) for i, dt in zip(d_idx, out_dtypes)]
        + [jax.ShapeDtypeStruct(params[i].shape, f32) for i in p_idx],
        compiler_params=_params(("arbitrary",)),
    )(*rows, *params, *flat_cts)


def _rms(x, g):
    return x * lax.rsqrt(jnp.mean(x * x, axis=-1, keepdims=True) + RMS_EPS) * g


def _head_sum_matrix(width, head):
    idx = jnp.arange(width) // head
    return (idx[:, None] == idx[None, :]).astype(f32)


def _rope_tables(lp):
    half = ROPE_DIM // 2
    pos = (jnp.arange(lp) - PAD).astype(f32)
    inv_freq = jnp.power(jnp.float32(ROPE_THETA), -jnp.arange(half, dtype=f32) * (2.0 / ROPE_DIM))
    ang = pos[:, None] * inv_freq[None, :]
    cos, sin = jnp.cos(ang), jnp.sin(ang)
    ones = jnp.ones((lp, HEAD_DIM - ROPE_DIM), f32)
    zeros = jnp.zeros((lp, HEAD_DIM - ROPE_DIM), f32)
    cos_t = jnp.concatenate([cos, cos, ones], axis=1)
    sin_t = jnp.concatenate([-sin, sin, zeros], axis=1)
    i = jnp.arange(HEAD_DIM)
    src = jnp.where(i < half, i + half, jnp.where(i < ROPE_DIM, i - half, i))
    swap = ((i[:, None] == src[None, :]) & (i[None, :] < ROPE_DIM)).astype(f32)
    return cos_t, sin_t, swap


def _attn_prep(qkv, cos_t, sin_t, swap):
    outs = []
    for h in range(Q_HEADS + KV_HEADS):
        t = qkv[:, h * HEAD_DIM:(h + 1) * HEAD_DIM]
        outs.append(t * cos_t + _dot_const(t, swap) * sin_t)
    q = jnp.concatenate(outs[:Q_HEADS], axis=1)
    k = jnp.concatenate(outs[Q_HEADS:], axis=1)
    return q, k, qkv[:, Q_W + KV_W:]


def _softplus(z):
    return jnp.maximum(z, 0.0) + jnp.log1p(jnp.exp(-jnp.abs(z)))


def _rwkv_prep(rkv, lora, w0, w2, a0, a2, g2, k_k, k_a, hsum):
    r = rkv[:, :RWKV_DIM]
    k = rkv[:, RWKV_DIM:2 * RWKV_DIM]
    v = rkv[:, 2 * RWKV_DIM:]
    dw = lora[:, :DECAY_LORA]
    da = lora[:, DECAY_LORA:DECAY_LORA + AAA_LORA]
    dg = lora[:, DECAY_LORA + AAA_LORA:]
    w = -_softplus(-(w0 + _dot(jnp.tanh(dw), w2))) - 0.5
    a = jax.nn.sigmoid(a0 + _dot(da, a2))
    g = _dot(jax.nn.sigmoid(dg), g2)
    kk = k * k_k
    kk = kk * lax.rsqrt(jnp.maximum(_dot_const(kk * kk, hsum), 1e-24))
    k = k * (1.0 + (a - 1.0) * k_a)
    log_decay = -jnp.exp(w)
    return r, log_decay, k, v, -kk, kk * a, g


def _rwkv_post(y, r, k, v, g, ln_w, ln_b, r_k, hmean):
    hsum = hmean * RWKV_HEAD
    mean = _dot_const(y, hmean)
    yc = y - mean
    var = _dot_const(yc * yc, hmean)
    yn = yc * lax.rsqrt(var + RWKV_LN_EPS) * ln_w + ln_b
    bonus = _dot_const(r * k * r_k, hsum) * v
    return ((yn + bonus) * g,)


def _merge(gates, br_a, br_r):
    sg = jax.nn.sigmoid(gates)
    return (sg[:, :D_MODEL] * br_a + sg[:, D_MODEL:] * br_r,)


def _swiglu(gate, up):
    return (jax.nn.silu(gate) * up,)


def _ffn_in(f, w_gate_t, w_up_t, *, name):
    m, d = f.shape
    n = w_gate_t.shape[0]
    tm = _tile(m)

    def body(f_ref, wg_ref, wu_ref, g_ref, u_ref, a_ref):
        g = _dot(f_ref[...], wg_ref[...], "nt")
        u = _dot(f_ref[...], wu_ref[...], "nt")
        g_ref[...] = g.astype(g_ref.dtype)
        u_ref[...] = u.astype(u_ref.dtype)
        a_ref[...] = _swiglu(g, u)[0].astype(a_ref.dtype)

    spec = pl.BlockSpec((tm, n), lambda i: (i, 0))
    return pl.pallas_call(
        body, name=name, grid=(m // tm,),
        in_specs=[pl.BlockSpec((tm, d), lambda i: (i, 0)), _full(w_gate_t.shape), _full(w_up_t.shape)],
        out_specs=[spec] * 3, out_shape=[jax.ShapeDtypeStruct((m, n), bf16)] * 3,
        compiler_params=_params(("parallel",)),
    )(f, w_gate_t, w_up_t)


def _branch_merge(y_attn, y_rwkv, w_attn_t, w_rwkv_t, gates, *, name):
    m = y_attn.shape[0]
    tm = _tile(m)

    def body(ya_ref, yr_ref, wa_ref, wr_ref, g_ref, a_ref, r_ref, o_ref):
        br_a = _dot(ya_ref[...], wa_ref[...], "nt")
        br_r = _dot(yr_ref[...], wr_ref[...], "nt")
        a_ref[...] = br_a.astype(a_ref.dtype)
        r_ref[...] = br_r.astype(r_ref.dtype)
        o_ref[...] = _merge(g_ref[...], br_a, br_r)[0].astype(o_ref.dtype)

    rows = lambda a: pl.BlockSpec((tm, a.shape[1]), lambda i: (i, 0))
    spec = pl.BlockSpec((tm, D_MODEL), lambda i: (i, 0))
    return pl.pallas_call(
        body, name=name, grid=(m // tm,),
        in_specs=[rows(y_attn), rows(y_rwkv), _full(w_attn_t.shape), _full(w_rwkv_t.shape), rows(gates)],
        out_specs=[spec] * 3, out_shape=[jax.ShapeDtypeStruct((m, D_MODEL), bf16)] * 3,
        compiler_params=_params(("parallel",)),
    )(y_attn, y_rwkv, w_attn_t, w_rwkv_t, gates)


def _branch_merge_bwd(dh, w_o, gates, br_a, br_r, *, name):
    m = dh.shape[0]
    tm = _tile(m)

    def body(dh_ref, w_ref, g_ref, a_ref, r_ref, dg_ref, da_ref, dr_ref):
        dmerged = _dot(dh_ref[...], w_ref[...], "nt")
        _, vjp = jax.vjp(lambda g, a, r: _merge(g, a, r)[0], g_ref[...], a_ref[...].astype(f32),
                         r_ref[...].astype(f32))
        dg, da, dr = vjp(dmerged)
        dg_ref[...] = dg.astype(dg_ref.dtype)
        da_ref[...] = da.astype(da_ref.dtype)
        dr_ref[...] = dr.astype(dr_ref.dtype)

    rows = lambda a: pl.BlockSpec((tm, a.shape[1]), lambda i: (i, 0))
    return pl.pallas_call(
        body, name=name, grid=(m // tm,),
        in_specs=[rows(dh), _full(w_o.shape), rows(gates), rows(br_a), rows(br_r)],
        out_specs=[rows(gates), rows(br_a), rows(br_r)],
        out_shape=[jax.ShapeDtypeStruct(gates.shape, bf16), jax.ShapeDtypeStruct(br_a.shape, bf16),
                   jax.ShapeDtypeStruct(br_r.shape, bf16)],
        compiler_params=_params(("parallel",)),
    )(dh, w_o, gates, br_a, br_r)


def _ffn_in_bwd(dh, w_down, gate, up, *, name):
    m, d = dh.shape
    n = w_down.shape[0]
    tm = _tile(m)

    def body(dh_ref, w_ref, g_ref, u_ref, dg_ref, du_ref):
        dact = _dot(dh_ref[...], w_ref[...], "nt")
        _, vjp = jax.vjp(lambda a, b: _swiglu(a, b)[0], g_ref[...].astype(f32), u_ref[...].astype(f32))
        dg, du = vjp(dact)
        dg_ref[...] = dg.astype(dg_ref.dtype)
        du_ref[...] = du.astype(du_ref.dtype)

    spec = pl.BlockSpec((tm, n), lambda i: (i, 0))
    return pl.pallas_call(
        body, name=name, grid=(m // tm,),
        in_specs=[pl.BlockSpec((tm, d), lambda i: (i, 0)), _full(w_down.shape), spec, spec],
        out_specs=[spec] * 2, out_shape=[jax.ShapeDtypeStruct((m, n), bf16)] * 2,
        compiler_params=_params(("parallel",)),
    )(dh, w_down, gate, up)


def _previous_rows(x, before_ref, first_tile):
    rows = lax.broadcasted_iota(jnp.int32, x.shape, 0)
    last = jnp.where(first_tile, 0.0, before_ref[7:8, :])
    return jnp.where(rows == 0, last, pltpu.roll(x, 1, axis=0))


def _mixer_inputs(ps, mixes, params, *, name):
    m = ps[0].shape[0]
    tm = _tile(m)
    sub = tm // 8
    n_par = len(params)

    def body(*refs):
        first = pl.program_id(0) == 0
        pf = []
        for k in range(2):
            x = refs[k][...]
            pf.append(x + (_previous_rows(x, refs[2 + k], first) - x) * refs[4 + k][...])
        res = _rwkv_prep(*pf, *[ref[...] for ref in refs[6:6 + n_par]])
        for o_ref, val in zip(refs[6 + n_par:], res):
            o_ref[...] = val

    tile = lambda a: pl.BlockSpec((tm, a.shape[1]), lambda i: (i, 0))
    before = lambda a: pl.BlockSpec((8, a.shape[1]), lambda i: (jnp.maximum(i * sub - 1, 0), 0))
    out = pl.BlockSpec((tm, RWKV_DIM), lambda i: (i, 0))
    return pl.pallas_call(
        body, name=name, grid=(m // tm,),
        in_specs=[tile(a) for a in ps] + [before(a) for a in ps] + [_full(a.shape) for a in mixes + params],
        out_specs=[out] * 7, out_shape=[jax.ShapeDtypeStruct((m, RWKV_DIM), f32)] * 7,
        compiler_params=_params(("parallel",)),
    )(*ps, *ps, *mixes, *params)


def _mixer_inputs_bwd(ps, mixes, params, cts, *, name):
    m = ps[0].shape[0]
    tm = _tile(m)
    sub = tm // 8
    nt = m // tm
    n_par = len(params)
    flat_cts = [c for group in cts for c in group]
    n_ct = len(flat_cts)

    def body(*refs):
        i = pl.program_id(0)
        tile_index = nt - 1 - i
        ct_refs = refs[6 + n_par:6 + n_par + n_ct]
        dp_refs = refs[6 + n_par + n_ct:8 + n_par + n_ct]
        dmix_refs = refs[8 + n_par + n_ct:10 + n_par + n_ct]
        dpar_refs = refs[10 + n_par + n_ct:9 + 2 * n_par + n_ct]
        carries = refs[9 + 2 * n_par + n_ct:]
        rows1 = tile_index * tm + lax.broadcasted_iota(jnp.int32, (tm, 1), 0)
        live = rows1 >= PAD

        @pl.when(i == 0)
        def _():
            for ref in (*dmix_refs, *dpar_refs, *carries):
                ref[...] = jnp.zeros_like(ref)

        xs, prevs, pf = [], [], []
        for k in range(2):
            x = refs[k][...]
            xp = _previous_rows(x, refs[2 + k], tile_index == 0)
            xs.append(x)
            prevs.append(xp)
            pf.append(x + (xp - x) * refs[4 + k][...])
        ct_vals, pos = [], 0
        for group in cts:
            acc = ct_refs[pos][...].astype(f32)
            for extra in range(1, len(group)):
                acc = acc + ct_refs[pos + extra][...].astype(f32)
            pos += len(group)
            ct_vals.append(jnp.where(live, acc, 0.0))
        par_vals = [ref[...] for ref in refs[6:6 + n_par]]
        _, vjp = jax.vjp(lambda *args: _rwkv_prep(*args, par_vals[-1]), *pf, *par_vals[:-1])
        g = vjp(tuple(ct_vals))
        for k in range(2):
            dpf = g[k]
            mixv = refs[4 + k][...]
            dm = dpf * mixv
            rows = lax.broadcasted_iota(jnp.int32, dm.shape, 0)
            dm_next = jnp.where(rows == tm - 1, carries[k][...], pltpu.roll(dm, tm - 1, axis=0))
            dp_refs[k][...] = jnp.where(live, dpf - dm + dm_next, 0.0).astype(dp_refs[k].dtype)
            carries[k][...] = dm[0:1, :]
            dmix_refs[k][...] += jnp.sum(dpf * (prevs[k] - xs[k]), axis=0, keepdims=True)
        for ref, val in zip(dpar_refs, g[2:]):
            ref[...] += val

    tile = lambda a: pl.BlockSpec((tm, a.shape[1]), lambda i: (nt - 1 - i, 0))
    before = lambda a: pl.BlockSpec((8, a.shape[1]), lambda i: (jnp.maximum((nt - 1 - i) * sub - 1, 0), 0))
    return pl.pallas_call(
        body, name=name, grid=(nt,),
        in_specs=[tile(a) for a in ps] + [before(a) for a in ps] + [_full(a.shape) for a in mixes + params]
        + [tile(c) for c in flat_cts],
        out_specs=[tile(a) for a in ps] + [_full(a.shape) for a in mixes + params[:-1]],
        out_shape=[jax.ShapeDtypeStruct(a.shape, bf16) for a in ps]
        + [jax.ShapeDtypeStruct(a.shape, f32) for a in mixes + params[:-1]],
        scratch_shapes=[pltpu.VMEM((1, a.shape[1]), f32) for a in ps],
        compiler_params=_params(("arbitrary",)),
    )(*ps, *ps, *mixes, *params, *flat_cts)


def _attn_masks(blk):
    qi = lax.broadcasted_iota(jnp.int32, (BLOCK, BLOCK), 0)
    ki = lax.broadcasted_iota(jnp.int32, (BLOCK, BLOCK), 1)
    qpos = blk * BLOCK + qi - PAD
    kpos_c = blk * BLOCK + ki - PAD
    kpos_p = kpos_c - BLOCK
    kpos_m = ki - PAD

    def band(kpos):
        return (kpos >= N_META) & (kpos <= qpos) & (qpos - kpos < WINDOW)

    return band(kpos_p), band(kpos_c), (kpos_m >= 0) & (kpos_m <= qpos)


def _attn_probs(qs, k3s, sink, oks):
    s = [[jnp.where(ok, _dot(qh, kx, "nt"), NEG_INF) for kx, ok in zip(k3, oks)] for qh, k3 in zip(qs, k3s)]
    mx = [jnp.maximum(jnp.maximum(jnp.max(t[0], -1, keepdims=True), jnp.max(t[1], -1, keepdims=True)),
                      jnp.maximum(jnp.max(t[2], -1, keepdims=True), sk)) for t, sk in zip(s, sink)]
    e = [[jnp.exp(tx - m) for tx in t] for t, m in zip(s, mx)]
    e_sink = [jnp.exp(sk - m) for sk, m in zip(sink, mx)]
    inv = [1.0 / (jnp.sum(t[0], -1, keepdims=True) + jnp.sum(t[1], -1, keepdims=True)
                  + jnp.sum(t[2], -1, keepdims=True) + es) for t, es in zip(e, e_sink)]
    return [[tx * i for tx in t] for t, i in zip(e, inv)], [es * i for es, i in zip(e_sink, inv)]


def _head_cols(i):
    return slice(i * HEAD_DIM, (i + 1) * HEAD_DIM)


def _attn_operands(refs):
    q_ref, kp_ref, kc_ref, km_ref, vp_ref, vc_ref, vm_ref, s_ref = refs
    qs = [q_ref[:, _head_cols(i)] * (HEAD_DIM ** -0.5) for i in range(Q_HEADS)]
    k3 = [[ref[:, _head_cols(h)] for ref in (kp_ref, kc_ref, km_ref)] for h in range(KV_HEADS)]
    v3 = [[ref[:, _head_cols(h)] for ref in (vp_ref, vc_ref, vm_ref)] for h in range(KV_HEADS)]
    return (qs, [k3[i // GROUP] for i in range(Q_HEADS)], [v3[i // GROUP] for i in range(Q_HEADS)],
            [s_ref[:, i:i + 1] for i in range(Q_HEADS)])


def _attention(q, k, v, sinks, *, name):
    lp = q.shape[0]
    nb = lp // BLOCK
    prev = lambda i: (jnp.maximum(i - 1, 0), 0)
    cur = lambda i: (i, 0)
    meta = lambda i: (0, 0)
    kv = lambda index: pl.BlockSpec((BLOCK, KV_W), index)

    def body(*refs):
        o_ref = refs[-1]
        qs, k3s, v3s, sink = _attn_operands(refs[:-1])
        p, _ = _attn_probs(qs, k3s, sink, _attn_masks(pl.program_id(0)))
        out = [_dot(ph[0], v3[0]) + _dot(ph[1], v3[1]) + _dot(ph[2], v3[2]) for ph, v3 in zip(p, v3s)]
        for i in range(Q_HEADS):
            o_ref[:, _head_cols(i)] = out[i].astype(o_ref.dtype)

    return pl.pallas_call(
        body, name=name, grid=(nb,),
        in_specs=[pl.BlockSpec((BLOCK, Q_W), cur), kv(prev), kv(cur), kv(meta), kv(prev), kv(cur), kv(meta),
                  _full((1, Q_HEADS))],
        out_specs=pl.BlockSpec((BLOCK, Q_W), cur),
        out_shape=jax.ShapeDtypeStruct((lp, Q_W), bf16),
        compiler_params=_params(("parallel",)),
    )(q, k, k, k, v, v, v, sinks)


def _attention_bwd(q, k, v, sinks, do, *, name):
    lp = q.shape[0]
    nb = lp // BLOCK
    cur = lambda n: (jnp.minimum(n, nb - 1), 0)
    prev = lambda n: (jnp.maximum(jnp.minimum(n, nb - 1) - 1, 0), 0)
    behind = lambda n: (jnp.maximum(n - 1, 0), 0)
    meta = lambda n: (0, 0)
    kv = lambda index: pl.BlockSpec((BLOCK, KV_W), index)
    scale = HEAD_DIM ** -0.5

    def body(*refs):
        ins, do_ref = refs[:8], refs[8]
        dq_ref, dk_ref, dv_ref, dkm_ref, dvm_ref, ds_ref, carry_k, carry_v = refs[9:]
        n = pl.program_id(0)

        @pl.when(n == 0)
        def _():
            for ref in (dkm_ref, dvm_ref, ds_ref, carry_k, carry_v):
                ref[...] = jnp.zeros_like(ref)

        @pl.when(n < nb)
        def _():
            qs, k3s, v3s, sink = _attn_operands(ins)
            do = [do_ref[:, _head_cols(i)] for i in range(Q_HEADS)]
            p, p_sink = _attn_probs(qs, k3s, sink, _attn_masks(n))
            out = [_dot(ph[0], v3[0]) + _dot(ph[1], v3[1]) + _dot(ph[2], v3[2]) for ph, v3 in zip(p, v3s)]
            delta = [jnp.sum(d * o, -1, keepdims=True) for d, o in zip(do, out)]
            dp = [[_dot(d, vx, "nt") for vx in v3] for d, v3 in zip(do, v3s)]
            ds = [[px * (dx - dl) for px, dx in zip(ph, dh)] for ph, dh, dl in zip(p, dp, delta)]
            dq = [_dot(dsh[0], k3[0]) + _dot(dsh[1], k3[1]) + _dot(dsh[2], k3[2]) for dsh, k3 in zip(ds, k3s)]
            for i in range(Q_HEADS):
                dq_ref[:, _head_cols(i)] = dq[i] * scale
                ds_ref[:, i:i + 1] -= jnp.sum(p_sink[i] * delta[i], axis=0, keepdims=True)
            for h in range(KV_HEADS):
                group = slice(h * GROUP, (h + 1) * GROUP)
                q_all = jnp.concatenate(qs[group], axis=0)
                do_all = jnp.concatenate(do[group], axis=0)
                dk3 = [_dot(jnp.concatenate([dsh[x] for dsh in ds[group]], axis=0), q_all, "tn") for x in range(3)]
                dv3 = [_dot(jnp.concatenate([ph[x] for ph in p[group]], axis=0), do_all, "tn") for x in range(3)]
                hs = _head_cols(h)
                for out_ref, carry, meta_ref, d3 in ((dk_ref, carry_k, dkm_ref, dk3),
                                                     (dv_ref, carry_v, dvm_ref, dv3)):
                    out_ref[:, hs] = carry[:, hs] + d3[0]
                    carry[:, hs] = d3[1]
                    meta_ref[:, hs] += d3[2]

        @pl.when(n == nb)
        def _():
            dk_ref[...] = carry_k[...]
            dv_ref[...] = carry_v[...]

    kv_shape = jax.ShapeDtypeStruct((lp, KV_W), f32)
    one_shape = jax.ShapeDtypeStruct((BLOCK, KV_W), f32)
    return pl.pallas_call(
        body, name=name, grid=(nb + 1,),
        in_specs=[pl.BlockSpec((BLOCK, Q_W), cur), kv(prev), kv(cur), kv(meta), kv(prev), kv(cur), kv(meta),
                  _full((1, Q_HEADS)), pl.BlockSpec((BLOCK, Q_W), cur)],
        out_specs=[pl.BlockSpec((BLOCK, Q_W), cur), kv(behind), kv(behind), kv(meta), kv(meta),
                   _full((1, Q_HEADS))],
        out_shape=[jax.ShapeDtypeStruct((lp, Q_W), f32), kv_shape, kv_shape, one_shape, one_shape,
                   jax.ShapeDtypeStruct((1, Q_HEADS), f32)],
        scratch_shapes=[pltpu.VMEM((BLOCK, KV_W), f32), pltpu.VMEM((BLOCK, KV_W), f32)],
        compiler_params=_params(("arbitrary",)),
    )(q, k, k, k, v, v, v, sinks, do)


@jax.custom_vjp
def _known_inverse(l, x):
    return x


def _known_inverse_fwd(l, x):
    return x, x


def _known_inverse_bwd(x, ct):
    return _dot(_dot(x, ct, "tn"), x, "nt"), jnp.zeros_like(x)


_known_inverse.defvjp(_known_inverse_fwd, _known_inverse_bwd)


def _scan_chunk(s0, r, lw, k, v, a, b, inv=None):
    t = r[0].shape[0]
    ii = lax.broadcasted_iota(jnp.int32, (t, t), 0)
    jj = lax.broadcasted_iota(jnp.int32, (t, t), 1)
    incl = jj <= ii
    strict = jj < ii
    tri = incl.astype(f32)
    eye = jnp.where(ii == jj, 1.0, 0.0)
    cl = [_const_dot(tri, x) for x in lw]
    e_pos = [jnp.exp(c) for c in cl]
    e_neg = [jnp.exp(-c) for c in cl]
    e_prev = [jnp.exp(c - x) for c, x in zip(cl, lw)]
    rt = [x * e for x, e in zip(r, e_pos)]
    at = [x * e for x, e in zip(a, e_prev)]
    bt = [x * e for x, e in zip(b, e_neg)]
    kt = [x * e for x, e in zip(k, e_neg)]
    l_ab = [jnp.where(strict, _dot(x, y, "nt"), 0.0) for x, y in zip(at, bt)]
    l_ak = [jnp.where(strict, _dot(x, y, "nt"), 0.0) for x, y in zip(at, kt)]
    r_b = [jnp.where(incl, _dot(x, y, "nt"), 0.0) for x, y in zip(rt, bt)]
    r_k = [jnp.where(incl, _dot(x, y, "nt"), 0.0) for x, y in zip(rt, kt)]
    if inv is None:
        inv = [eye + x for x in l_ab]
        pw = l_ab
        for _ in range(int(math.log2(t)) - 1):
            pw = [_dot(x, x) for x in pw]
            inv = [x + _dot(x, y) for x, y in zip(inv, pw)]
    else:
        inv = [_known_inverse(x, y) for x, y in zip(l_ab, inv)]
    rhs = [_dot(x, s, "nt") + _dot(m, y) for x, s, m, y in zip(at, s0, l_ak, v)]
    u = [_dot(x, y) for x, y in zip(inv, rhs)]
    y_s = [_dot(x, s, "nt") for x, s in zip(rt, s0)]
    y = [ys + _dot(m, uu) + _dot(n, vv) for ys, m, uu, n, vv in zip(y_s, r_b, u, r_k, v)]
    grow = [s + _dot(uu, x, "tn") + _dot(vv, z, "tn") for s, uu, x, vv, z in zip(s0, u, bt, v, kt)]
    s1 = [g * e[t - 1:t, :] for g, e in zip(grow, e_pos)]
    return y, s1, inv


def _head_rows(h):
    return slice(h * RWKV_HEAD, (h + 1) * RWKV_HEAD)


def _per_head(ref):
    return [ref[:, _head_rows(h)] for h in range(RWKV_HEADS)]


def _scan(r, lw, k, v, a, b, *, name):
    lp = r.shape[0]
    nc = lp // CHUNK
    row = pl.BlockSpec((CHUNK, RWKV_DIM), lambda c: (c, 0))

    def body(r_ref, lw_ref, k_ref, v_ref, a_ref, b_ref, y_ref, s_ref, inv_ref, state):
        @pl.when(pl.program_id(0) == 0)
        def _():
            state[...] = jnp.zeros_like(state)

        s_ref[...] = state[...]
        s0 = [state[_head_rows(h), :] for h in range(RWKV_HEADS)]
        y, s1, inv = _scan_chunk(s0, *[_per_head(ref) for ref in (r_ref, lw_ref, k_ref, v_ref, a_ref, b_ref)])
        for h in range(RWKV_HEADS):
            y_ref[:, _head_rows(h)] = y[h]
            state[_head_rows(h), :] = s1[h]
            inv_ref[h * CHUNK:(h + 1) * CHUNK, :] = inv[h].astype(inv_ref.dtype)

    return pl.pallas_call(
        body, name=name, grid=(nc,), in_specs=[row] * 6,
        out_specs=[row, pl.BlockSpec((RWKV_DIM, RWKV_HEAD), lambda c: (c, 0)),
                   pl.BlockSpec((RWKV_HEADS * CHUNK, CHUNK), lambda c: (c, 0))],
        out_shape=[jax.ShapeDtypeStruct((lp, RWKV_DIM), f32), jax.ShapeDtypeStruct((nc * RWKV_DIM, RWKV_HEAD), f32),
                   jax.ShapeDtypeStruct((nc * RWKV_HEADS * CHUNK, CHUNK), bf16)],
        scratch_shapes=[pltpu.VMEM((RWKV_DIM, RWKV_HEAD), f32)],
        compiler_params=_params(("arbitrary",)),
    )(r, lw, k, v, a, b)


def _scan_bwd(r, lw, k, v, a, b, states, inverses, dy, *, name):
    lp = r.shape[0]
    nc = lp // CHUNK
    back = lambda c: (nc - 1 - c, 0)
    row = pl.BlockSpec((CHUNK, RWKV_DIM), back)

    def body(r_ref, lw_ref, k_ref, v_ref, a_ref, b_ref, s_ref, inv_ref, dy_ref,
             dr_ref, dlw_ref, dk_ref, dv_ref, da_ref, db_ref, dstate):
        @pl.when(pl.program_id(0) == 0)
        def _():
            dstate[...] = jnp.zeros_like(dstate)

        outs = (dr_ref, dlw_ref, dk_ref, dv_ref, da_ref, db_ref)
        s0 = [s_ref[_head_rows(h), :] for h in range(RWKV_HEADS)]
        inv = [inv_ref[h * CHUNK:(h + 1) * CHUNK, :].astype(f32) for h in range(RWKV_HEADS)]
        _, vjp = jax.vjp(lambda *args: _scan_chunk(*args, inv=inv)[:2], s0,
                         *[_per_head(ref) for ref in (r_ref, lw_ref, k_ref, v_ref, a_ref, b_ref)])
        g = vjp((_per_head(dy_ref), [dstate[_head_rows(h), :] for h in range(RWKV_HEADS)]))
        for h in range(RWKV_HEADS):
            dstate[_head_rows(h), :] = g[0][h]
            for o_ref, gv in zip(outs, g[1:]):
                o_ref[:, _head_rows(h)] = gv[h]

    shape = jax.ShapeDtypeStruct((lp, RWKV_DIM), f32)
    return pl.pallas_call(
        body, name=name, grid=(nc,),
        in_specs=[row] * 6 + [pl.BlockSpec((RWKV_DIM, RWKV_HEAD), back),
                              pl.BlockSpec((RWKV_HEADS * CHUNK, CHUNK), back), row],
        out_specs=[row] * 6, out_shape=[shape] * 6,
        scratch_shapes=[pltpu.VMEM((RWKV_DIM, RWKV_HEAD), f32)],
        compiler_params=_params(("arbitrary",)),
    )(r, lw, k, v, a, b, states, inverses, dy)


def _loss_head(h2, target, g_final, *, name):
    lp = h2.shape[0]
    tm = BLOCK
    front_tiles = FRONT // tm

    def body(h_ref, t_ref, g_ref, loss_ref, dh_ref, dg_ref):
        i = pl.program_id(0)
        real = i >= front_tiles

        def tile_loss(hv, gv):
            err = _rms(hv, gv) - t_ref[...]
            return jnp.where(real, 0.5 * jnp.sum(jnp.mean(err * err, axis=-1, keepdims=True)), 0.0)

        loss, (dh, dg) = jax.value_and_grad(tile_loss, argnums=(0, 1))(h_ref[...], g_ref[...])

        @pl.when(i == 0)
        def _():
            loss_ref[...] = jnp.zeros_like(loss_ref)
            dg_ref[...] = jnp.zeros_like(dg_ref)

        loss_ref[...] += jnp.full(loss_ref.shape, loss, f32)
        dg_ref[...] += dg
        dh_ref[...] = dh

    return pl.pallas_call(
        body, name=name, grid=(lp // tm,),
        in_specs=[pl.BlockSpec((tm, D_MODEL), lambda i: (i, 0)),
                  pl.BlockSpec((tm, D_MODEL), lambda i: (jnp.maximum(i - front_tiles, 0), 0)),
                  _full(g_final.shape)],
        out_specs=[_full((8, 128)), pl.BlockSpec((tm, D_MODEL), lambda i: (i, 0)), _full(g_final.shape)],
        out_shape=[jax.ShapeDtypeStruct((8, 128), f32), jax.ShapeDtypeStruct((lp, D_MODEL), f32),
                   jax.ShapeDtypeStruct(g_final.shape, f32)],
        compiler_params=_params(("arbitrary",)),
    )(h2, target, g_final)


def _local_step(x, target, meta, p, late_weights=None, emit=None):
    emit = emit or (lambda group, grads: 0.0)
    seq = x.shape[0]
    lp = seq + FRONT
    h0 = jnp.concatenate([jnp.zeros((PAD, D_MODEL), f32), meta, x], axis=0)
    cos_t, sin_t, swap = _rope_tables(lp)
    hsum = _head_sum_matrix(RWKV_DIM, RWKV_HEAD)
    hmean = hsum / RWKV_HEAD
    w_qkv_t, w_rkv_t = p["w_in_t"][:ATTN_PROJ], p["w_in_t"][ATTN_PROJ:ATTN_PROJ + RKV_W]
    w_lora_t, w_gates_t = p["w_in_t"][ATTN_PROJ + RKV_W:ATTN_PROJ + RWKV_PROJ], p["w_in_t"][ATTN_PROJ + RWKV_PROJ:]
    b_qkv, b_rkv = p["b_in"][:, :ATTN_PROJ], p["b_in"][:, ATTN_PROJ:ATTN_PROJ + RKV_W]
    b_lora, b_gates = p["b_in"][:, ATTN_PROJ + RKV_W:ATTN_PROJ + RWKV_PROJ], p["b_in"][:, ATTN_PROJ + RWKV_PROJ:]
    prep_params = [p["w0"], p["w2"], p["a0"], p["a2"], p["g2"], p["k_k"], p["k_a"], hsum]
    post_params = [p["ln_w"], p["ln_b"], p["r_k"], hmean]

    (u,) = _rowwise(lambda hv, g: (_rms(hv, g),), [h0], [p["norm_mix_g"]], [(D_MODEL, bf16)], name="norm_mix")
    w_pieces = [w_qkv_t, w_rkv_t, w_lora_t, w_gates_t]
    qkv, p_rkv, p_lora, gates = _mm_fanout(u, w_pieces, [b_qkv, b_rkv, b_lora, b_gates], name="proj_in",
                                           zero_rows_below=PAD)

    q, k, v = _rowwise(_attn_prep, [qkv, cos_t, sin_t], [swap], [(Q_W, bf16), (KV_W, bf16), (KV_W, bf16)],
                       name="attn_prep")
    y_attn = _attention(q, k, v, p["sinks"], name="attention")

    mix_rkv, mix_lora = p["mix"][:, :RKV_W], p["mix"][:, RKV_W:]
    r_, lw_, k_, v_, a_, b_, g_ = _mixer_inputs([p_rkv, p_lora], [mix_rkv, mix_lora], prep_params,
                                                name="mixer_inputs")
    y_scan, states, inverses = _scan(r_, lw_, k_, v_, a_, b_, name="wkv_scan")
    (y_rwkv,) = _rowwise(_rwkv_post, [y_scan, r_, k_, v_, g_], post_params, [(RWKV_DIM, bf16)], name="rwkv_post")

    if late_weights is not None:
        p = {**p, **late_weights(y_rwkv)}
    br_a, br_r, merged = _branch_merge(y_attn, y_rwkv, p["w_br_attn_t"], p["w_br_rwkv_t"], gates, name="branch_merge")
    h1 = _mm(merged, p["w_o"], "nn", name="out_proj", add=h0)
    (f,) = _rowwise(lambda hv, g: (_rms(hv, g),), [h1], [p["norm_ffn_g"]], [(D_MODEL, bf16)], name="norm_ffn")
    gate, up, act = _ffn_in(f, p["w_gate_t"], p["w_up_t"], name="ffn_in")
    h2 = _mm(act, p["w_down"], "nn", name="ffn_down", add=h1)

    loss8, dh2, d_final_g = _loss_head(h2, target, p["norm_final_g"], name="loss_head")
    dgate, dup = _ffn_in_bwd(dh2, p["w_down"], gate, up, name="ffn_in_bwd")
    d_w_down = _mm_tn(act, dh2, name="dw_down")
    d_w_gate_t = _mm_tn(dgate, f, name="dw_gate")
    d_w_up_t = _mm_tn(dup, f, name="dw_up")
    zero = emit("ffn", dict(w_down=d_w_down, w_gate_t=d_w_gate_t, w_up_t=d_w_up_t))
    df = _mm_sum([dgate, dup], [p["w_gate_t"], p["w_up_t"]], name="d_f")
    dh1, d_ffn_g = _rowwise_bwd(lambda hv, g: (_rms(hv, g), hv), [h1], [p["norm_ffn_g"] + zero], [[df], [dh2]],
                                name="norm_ffn_bwd", diff_rows=[True], diff_params=[True])
    dgates, dbr_a, dbr_r = _branch_merge_bwd(dh1, p["w_o"], gates, br_a, br_r, name="branch_merge_bwd")
    d_w_o = _mm_tn(merged, dh1, name="dw_o")
    d_w_br_attn_t = _mm_tn(dbr_a, y_attn, name="dw_br_attn")
    d_w_br_rwkv_t = _mm_tn(dbr_r, y_rwkv, name="dw_br_rwkv")
    zero = emit("branch", dict(w_o=d_w_o, w_br_attn_t=d_w_br_attn_t, w_br_rwkv_t=d_w_br_rwkv_t))
    dy_attn = _mm(dbr_a, p["w_br_attn_t"], "nn", name="d_y_attn")
    dy_rwkv = _mm(dbr_r, p["w_br_rwkv_t"], "nn", name="d_y_rwkv")

    post_params = [p["ln_w"] + zero, p["ln_b"], p["r_k"], hmean]
    res = _rowwise_bwd(_rwkv_post, [y_scan, r_, k_, v_, g_], post_params, [[dy_rwkv]], name="rwkv_post_bwd",
                       diff_rows=[True] * 5, diff_params=[True, True, True, False])
    dy_scan, dr_p, dk_p, dv_p, dg_p, d_ln_w, d_ln_b, d_r_k = res
    dr_s, dlw_s, dk_s, dv_s, da_s, db_s = _scan_bwd(r_, lw_, k_, v_, a_, b_, states, inverses, dy_scan,
                                                    name="wkv_scan_bwd")
    res = _mixer_inputs_bwd([p_rkv, p_lora], [mix_rkv, mix_lora], prep_params,
                            [[dr_s, dr_p], [dlw_s], [dk_s, dk_p], [dv_s, dv_p], [da_s], [db_s], [dg_p]],
                            name="mixer_inputs_bwd")
    dp_rkv, dp_lora, d_mix_rkv, d_mix_lora, d_w0, d_w2, d_a0, d_a2, d_g2, d_k_k, d_k_a = res

    dq, dk, dv, dkm, dvm, d_sinks = _attention_bwd(q, k, v, p["sinks"], dy_attn, name="attention_bwd")
    rest = jnp.zeros((lp - BLOCK, KV_W), f32)
    dkm, dvm = jnp.concatenate([dkm, rest], axis=0), jnp.concatenate([dvm, rest], axis=0)
    (dqkv,) = _rowwise_bwd(_attn_prep, [qkv, cos_t, sin_t], [swap], [[dq], [dk, dkm], [dv, dvm]], name="attn_prep_bwd",
                           diff_rows=[True, False, False], diff_params=[False], out_dtypes=[bf16])

    d_w_qkv_t, db_qkv = _mm_tn(dqkv, u, name="dw_qkv", colsum=True)
    d_w_rkv_t, db_rkv = _mm_tn(dp_rkv, u, name="dw_rkv", colsum=True)
    d_w_lora_t, db_lora = _mm_tn(dp_lora, u, name="dw_lora", colsum=True)
    d_w_gates_t, db_gates = _mm_tn(dgates, u, name="dw_gates", colsum=True)
    d_w_in_t = jnp.concatenate([d_w_qkv_t, d_w_rkv_t, d_w_lora_t, d_w_gates_t], axis=0)
    zero = emit("input", dict(w_in_t=d_w_in_t, g2=d_g2, w2=d_w2, a2=d_a2))
    du = _mm_sum([dqkv, dp_rkv, dp_lora, dgates], w_pieces, name="d_u")
    dh0, d_mix_g = _rowwise_bwd(lambda hv, g: (_rms(hv, g), hv), [h0], [p["norm_mix_g"] + zero], [[du], [dh1]],
                                name="norm_mix_bwd", diff_rows=[True], diff_params=[True])

    grads = dict(
        w_in_t=d_w_in_t,
        b_in=jnp.concatenate([db_qkv, db_rkv, db_lora, db_gates], axis=1),
        mix=jnp.concatenate([d_mix_rkv, d_mix_lora], axis=1),
        norm_mix_g=d_mix_g, sinks=d_sinks, w0=d_w0, w2=d_w2, a0=d_a0, a2=d_a2, g2=d_g2, k_k=d_k_k, k_a=d_k_a,
        r_k=d_r_k, ln_w=d_ln_w, ln_b=d_ln_b, w_br_attn_t=d_w_br_attn_t, w_br_rwkv_t=d_w_br_rwkv_t, w_o=d_w_o,
        norm_ffn_g=d_ffn_g, w_gate_t=d_w_gate_t, w_up_t=d_w_up_t, w_down=d_w_down, norm_final_g=d_final_g,
        meta=dh0[PAD:FRONT],
    )
    return loss8[0, 0], dh0[FRONT:], grads


def _position():
    return lax.axis_index("x"), lax.axis_index("y"), lax.axis_index("c")


def _other_chips(x, y):
    return [(1 - x, y), (x, 1 - y), (1 - x, 1 - y)]


_HBM = pl.BlockSpec(memory_space=pltpu.HBM)
_SEM = pl.BlockSpec(memory_space=pltpu.SEMAPHORE)
_EFFECT = pltpu.SideEffectType.DATAFLOW_SIDE_EFFECTING


def _chip_copies(src_refs, land_refs, send_sems, recv_sems, gather):
    x, y, c = _position()
    copies = []
    for a, (src, land) in enumerate(zip(src_refs, land_refs)):
        for j, (px, py) in enumerate(_other_chips(x, y)):
            copies.append(pltpu.make_async_remote_copy(
                src_ref=src if gather else src.at[2 * px + py],
                dst_ref=land.at[2 * x + y] if gather else land.at[j],
                send_sem=send_sems.at[3 * a + j], recv_sem=recv_sems.at[3 * a + j],
                device_id=(px, py, c), device_id_type=MESH))
    return copies


def _exchange_start(srcs, *, gather, name):
    n = len(srcs)
    lands = [lax.empty((N_CHIPS,) + s.shape if gather else (3,) + s.shape[1:], s.dtype) for s in srcs]

    def body(*refs):
        for cp in _chip_copies(refs[:n], refs[n:2 * n], refs[2 * n], refs[2 * n + 1], gather):
            cp.start()
        refs[-1][...] = jnp.zeros_like(refs[-1])

    res = pl.pallas_call(
        body, name=name,
        out_shape=(pltpu.SemaphoreType.DMA((3 * n,)), pltpu.SemaphoreType.DMA((3 * n,)),
                   *[pltpu.HBM(a.shape, a.dtype) for a in srcs + lands], jax.ShapeDtypeStruct((8, 128), f32)),
        in_specs=[_HBM] * (2 * n),
        out_specs=(_SEM, _SEM, *[_HBM] * (2 * n), pl.BlockSpec(memory_space=pltpu.VMEM)),
        input_output_aliases={i: 2 + i for i in range(2 * n)},
        compiler_params=pltpu.CompilerParams(has_side_effects=_EFFECT),
    )(*[pltpu.with_memory_space_constraint(a, pltpu.HBM) for a in srcs + lands])
    return res[0], res[1], list(res[2:2 + n]), list(res[2 + n:2 + 2 * n]), res[-1]


def _exchange_wait(handle, after, *, gather, name):
    send_sems, recv_sems, srcs, lands, _ = handle
    n = len(srcs)

    def body(*refs):
        for cp in _chip_copies(refs[:n], refs[n:2 * n], refs[2 * n], refs[2 * n + 1], gather):
            cp.wait_send()
            cp.wait_recv()

    res = pl.pallas_call(
        body, name=name,
        out_shape=tuple(pltpu.HBM(a.shape, a.dtype) for a in srcs + lands),
        in_specs=[_HBM] * (2 * n) + [_SEM, _SEM, pl.BlockSpec(memory_space=pl.ANY)],
        out_specs=tuple([_HBM] * (2 * n)),
        input_output_aliases={i: i for i in range(2 * n)},
        compiler_params=pltpu.CompilerParams(has_side_effects=_EFFECT),
    )(*srcs, *lands, send_sems, recv_sems, after)
    return list(res[:n]), list(res[n:])


def _sum_own_and_received(g, recv, *, name):
    _, r, w = g.shape
    tm = _tile(r)
    if g.dtype == bf16 and tm % 16:
        tm = r
    x, y, _ = _position()
    me = jnp.reshape(2 * x + y, (1,)).astype(jnp.int32)

    def body(me_ref, g_ref, r_ref, o_ref):
        o_ref[...] = (g_ref[0].astype(f32) + r_ref[0].astype(f32)) + (r_ref[1].astype(f32) + r_ref[2].astype(f32))

    return pl.pallas_call(
        body, name=name,
        grid_spec=pltpu.PrefetchScalarGridSpec(
            num_scalar_prefetch=1, grid=(r // tm,),
            in_specs=[pl.BlockSpec((1, tm, w), lambda i, me_ref: (me_ref[0], i, 0)),
                      pl.BlockSpec((3, tm, w), lambda i, me_ref: (0, i, 0))],
            out_specs=pl.BlockSpec((tm, w), lambda i, me_ref: (i, 0))),
        out_shape=jax.ShapeDtypeStruct((r, w), f32),
        compiler_params=_params(("parallel",)),
    )(me, g, recv)


def _swap_cores(arrs, *, name):
    n = len(arrs)

    def body(*refs):
        x, y, c = _position()
        copies = [pltpu.make_async_remote_copy(
            src_ref=refs[i], dst_ref=refs[n + i], send_sem=refs[2 * n].at[i], recv_sem=refs[2 * n + 1].at[i],
            device_id=(x, y, 1 - c), device_id_type=MESH) for i in range(n)]
        for cp in copies:
            cp.start()
        for cp in copies:
            cp.wait_recv()
        for cp in copies:
            cp.wait_send()

    return pl.pallas_call(
        body, name=name,
        in_specs=[pl.BlockSpec(memory_space=pl.ANY)] * n,
        out_specs=[pl.BlockSpec(memory_space=pl.ANY)] * n,
        out_shape=[jax.ShapeDtypeStruct(a.shape, a.dtype) for a in arrs],
        scratch_shapes=[pltpu.SemaphoreType.DMA((n,)), pltpu.SemaphoreType.DMA((n,))],
    )(*arrs)


def _all_reduce_small(a, *, name):
    rows, w = a.shape

    def body(a_ref, o_ref, buf, send_sems, recv_sems):
        x, y, c = _position()
        me = 4 * x + 2 * y + c
        buf[0] = a_ref[...]
        sends = []
        for rel in range(1, N_DEV):
            peer = ((1 - x) if rel & 4 else x, (1 - y) if rel & 2 else y, (1 - c) if rel & 1 else c)
            cp = pltpu.make_async_remote_copy(
                src_ref=a_ref, dst_ref=buf.at[rel], send_sem=send_sems.at[rel - 1], recv_sem=recv_sems.at[rel - 1],
                device_id=peer, device_id_type=MESH)
            cp.start()
            sends.append(cp)
        for cp in sends:
            cp.wait_recv()
        for cp in sends:
            cp.wait_send()
        acc = buf[jnp.bitwise_xor(me, 0)]
        for d in range(1, N_DEV):
            acc = acc + buf[jnp.bitwise_xor(me, d)]
        o_ref[...] = acc

    return pl.pallas_call(
        body, name=name,
        in_specs=[pl.BlockSpec(memory_space=pltpu.VMEM)],
        out_specs=pl.BlockSpec(memory_space=pltpu.VMEM),
        out_shape=jax.ShapeDtypeStruct((rows, w), f32),
        scratch_shapes=[pltpu.VMEM((N_DEV, rows, w), f32), pltpu.SemaphoreType.DMA((N_DEV - 1,)),
                        pltpu.SemaphoreType.DMA((N_DEV - 1,))],
    )(a)


def _adamw(w, g_parts, m, v, *, name, transposed=False):
    rows, cols = w.shape
    if transposed:
        tm = 256 if rows % 256 == 0 else rows
        g_spec = pl.BlockSpec((cols, tm), lambda i: (0, i))
    else:
        tm = _tile(rows, 256)
        g_spec = pl.BlockSpec((tm, cols), lambda i: (i, 0))
    n = len(g_parts)

    def body(*refs):
        w_ref, m_ref, v_ref = refs[0], refs[1 + n], refs[2 + n]
        g_ref, d_ref, nm_ref, nv_ref = refs[3 + n:]
        gv = refs[1][...]
        for part in refs[2:1 + n]:
            gv = gv + part[...]
        if transposed:
            gv = gv.T
        g_ref[...] = gv
        nm = ADAM_B1 * m_ref[...] + (1.0 - ADAM_B1) * gv
        nv = ADAM_B2 * v_ref[...] + (1.0 - ADAM_B2) * (gv * gv)
        m_hat = nm / (1.0 - ADAM_B1 ** ADAM_STEP)
        v_hat = nv / (1.0 - ADAM_B2 ** ADAM_STEP)
        d_ref[...] = -ADAM_LR * (m_hat / (jnp.sqrt(v_hat) + ADAM_EPS) + ADAM_WD * w_ref[...])
        nm_ref[...] = nm
        nv_ref[...] = nv

    spec = pl.BlockSpec((tm, cols), lambda i: (i, 0))
    shape = jax.ShapeDtypeStruct((rows, cols), f32)
    return pl.pallas_call(
        body, name=name, grid=(rows // tm,), in_specs=[spec] + [g_spec] * n + [spec] * 2,
        out_specs=[spec] * 4, out_shape=[shape] * 4,
        compiler_params=_params(("parallel",)),
    )(w, *g_parts, m, v)


def _pad_rows(a, rows):
    return jnp.concatenate([a, jnp.zeros((rows - a.shape[0], a.shape[1]), a.dtype)], axis=0) if rows > a.shape[0] else a


_SMALL = (("norm_mix_g", D_MODEL), ("b_in", D_IN), ("sinks", Q_HEADS), ("mix", RWKV_PROJ), ("w0", RWKV_DIM),
          ("a0", RWKV_DIM), ("k_k", RWKV_DIM), ("k_a", RWKV_DIM), ("r_k", RWKV_DIM), ("ln_w", RWKV_DIM),
          ("ln_b", RWKV_DIM), ("norm_ffn_g", D_MODEL), ("norm_final_g", D_MODEL))


def _pack_small(d):
    flat = jnp.concatenate([d[n].reshape(-1).astype(f32) for n, _ in _SMALL])
    return flat


def _unpack_small(flat):
    out, off = {}, 0
    for n, size in _SMALL:
        out[n] = flat[off:off + size]
        off += size
    return out


_SMALL_TOTAL = sum(s for _, s in _SMALL)


def kernel(x, meta_tokens, norm_mix_g, w_in, b_in, attn_sinks, rwkv_mix, rwkv_w0, rwkv_w2, rwkv_a0, rwkv_a2, rwkv_g2, rwkv_k_k, rwkv_k_a, rwkv_r_k, rwkv_ln_w, rwkv_ln_b, w_br_attn, w_br_rwkv, w_o, norm_ffn_g, w_ffn_gate, w_ffn_up, w_ffn_down, norm_final_g, loss_target, m_meta_tokens, m_norm_mix_g, m_w_in, m_b_in, m_attn_sinks, m_rwkv_mix, m_rwkv_w0, m_rwkv_w2, m_rwkv_a0, m_rwkv_a2, m_rwkv_g2, m_rwkv_k_k, m_rwkv_k_a, m_rwkv_r_k, m_rwkv_ln_w, m_rwkv_ln_b, m_w_br_attn, m_w_br_rwkv, m_w_o, m_norm_ffn_g, m_w_ffn_gate, m_w_ffn_up, m_w_ffn_down, m_norm_final_g, v_meta_tokens, v_norm_mix_g, v_w_in, v_b_in, v_attn_sinks, v_rwkv_mix, v_rwkv_w0, v_rwkv_w2, v_rwkv_a0, v_rwkv_a2, v_rwkv_g2, v_rwkv_k_k, v_rwkv_k_a, v_rwkv_r_k, v_rwkv_ln_w, v_rwkv_ln_b, v_w_br_attn, v_w_br_rwkv, v_w_o, v_norm_ffn_g, v_w_ffn_gate, v_w_ffn_up, v_w_ffn_down, v_norm_final_g):
    names = ("meta_tokens", "norm_mix_g", "w_in", "b_in", "attn_sinks", "rwkv_mix", "rwkv_w0", "rwkv_w2", "rwkv_a0",
             "rwkv_a2", "rwkv_g2", "rwkv_k_k", "rwkv_k_a", "rwkv_r_k", "rwkv_ln_w", "rwkv_ln_b", "w_br_attn",
             "w_br_rwkv", "w_o", "norm_ffn_g", "w_ffn_gate", "w_ffn_up", "w_ffn_down", "norm_final_g")
    w_all = dict(zip(names, (meta_tokens, norm_mix_g, w_in, b_in, attn_sinks, rwkv_mix, rwkv_w0, rwkv_w2, rwkv_a0,
                             rwkv_a2, rwkv_g2, rwkv_k_k, rwkv_k_a, rwkv_r_k, rwkv_ln_w, rwkv_ln_b, w_br_attn,
                             w_br_rwkv, w_o, norm_ffn_g, w_ffn_gate, w_ffn_up, w_ffn_down, norm_final_g)))
    m_all = dict(zip(names, (m_meta_tokens, m_norm_mix_g, m_w_in, m_b_in, m_attn_sinks, m_rwkv_mix, m_rwkv_w0,
                             m_rwkv_w2, m_rwkv_a0, m_rwkv_a2, m_rwkv_g2, m_rwkv_k_k, m_rwkv_k_a, m_rwkv_r_k,
                             m_rwkv_ln_w, m_rwkv_ln_b, m_w_br_attn, m_w_br_rwkv, m_w_o, m_norm_ffn_g, m_w_ffn_gate,
                             m_w_ffn_up, m_w_ffn_down, m_norm_final_g)))
    v_all = dict(zip(names, (v_meta_tokens, v_norm_mix_g, v_w_in, v_b_in, v_attn_sinks, v_rwkv_mix, v_rwkv_w0,
                             v_rwkv_w2, v_rwkv_a0, v_rwkv_a2, v_rwkv_g2, v_rwkv_k_k, v_rwkv_k_a, v_rwkv_r_k,
                             v_rwkv_ln_w, v_rwkv_ln_b, v_w_br_attn, v_w_br_rwkv, v_w_o, v_norm_ffn_g, v_w_ffn_gate,
                             v_w_ffn_up, v_w_ffn_down, v_norm_final_g)))
    cx, cy, _ = _position()
    chip = 2 * cx + cy

    t_of = dict(w_in_t="w_in", w_gate_t="w_ffn_gate", w_up_t="w_ffn_up", w_br_attn_t="w_br_attn",
                w_br_rwkv_t="w_br_rwkv", g2_t="rwkv_g2", w2_t="rwkv_w2", a2_t="rwkv_a2")
    plain_of = dict(w_down="w_ffn_down", w_o="w_o")
    meta_cols = meta_tokens.shape[1]

    def shard(k):
        return (w_all[t_of[k]][0].T if k in t_of else w_all[plain_of[k]][0]).astype(bf16)

    def whole(zone, own):
        return lax.dynamic_update_slice_in_dim(zone, own[None], chip, axis=0).reshape(-1, own.shape[-1])

    early = ("w_in_t", "g2_t", "w2_t", "a2_t")
    late = ("w_gate_t", "w_up_t", "w_down", "w_o", "w_br_attn_t", "w_br_rwkv_t")
    early_h = _exchange_start([shard(k) for k in early] + [meta_tokens], gather=True, name="gather_early_start")
    late_h = _exchange_start([shard(k) for k in late], gather=True, name="gather_late_start")
    own, zones = _exchange_wait(early_h, late_h[4], gather=True, name="gather_early_wait")
    got = {k: whole(z, o) for k, z, o in zip(early, zones, own)}
    meta_full = whole(zones[-1], own[-1]).reshape(N_CHIPS, N_META, meta_cols).transpose(1, 0, 2).reshape(N_META, -1)
    p = dict(
        w_in_t=got["w_in_t"], g2=got["g2_t"].T.astype(f32), w2=got["w2_t"].T.astype(f32),
        a2=got["a2_t"].T.astype(f32),
        b_in=b_in, sinks=attn_sinks, mix=rwkv_mix, w0=rwkv_w0, a0=rwkv_a0, k_k=rwkv_k_k, k_a=rwkv_k_a,
        r_k=rwkv_r_k.reshape(1, RWKV_DIM), ln_w=rwkv_ln_w, ln_b=rwkv_ln_b, norm_mix_g=norm_mix_g,
        norm_ffn_g=norm_ffn_g, norm_final_g=norm_final_g.reshape(1, D_MODEL),
    )

    def late_weights(after):
        own_l, zones_l = _exchange_wait(late_h, after, gather=True, name="gather_late_wait")
        return {k: whole(z, o) for k, z, o in zip(late, zones_l, own_l)}

    started = {}

    def emit(group, grads_):
        keys = list(grads_)
        slabs = []
        for k in keys:
            a = grads_[k].T if k in ("g2", "w2", "a2") else grads_[k]
            slabs.append(a.reshape(N_CHIPS, a.shape[0] // N_CHIPS, a.shape[1]))
        started[group] = (keys, _exchange_start(slabs, gather=False, name="scatter_" + group + "_start"))
        return started[group][1][4][0, 0]

    loss, dx, g = _local_step(x[0], loss_target[0], meta_full, p, late_weights, emit)

    def partial_sums(group, after):
        keys, handle = started[group]
        slabs, lands = _exchange_wait(handle, after, gather=False, name="scatter_" + group + "_wait")
        return {k: _sum_own_and_received(s, l, name="sum_chips_" + k) for k, s, l in zip(keys, slabs, lands)}

    parts = partial_sums("ffn", dx)
    parts.update(partial_sums("branch", parts["w_down"]))
    parts.update(partial_sums("input", parts["w_o"]))
    keys = list(parts)
    others = dict(zip(keys, _swap_cores([parts[k] for k in keys], name="swap_cores")))
    for short in ("g2", "w2", "a2"):
        parts[short + "_t"], others[short + "_t"] = parts.pop(short), others.pop(short)

    small = jnp.concatenate([_pack_small(g), loss.reshape(1)])
    small_rows = -(-small.shape[0] // PACK_W)
    small = jnp.concatenate([small, jnp.zeros((small_rows * PACK_W - small.shape[0],), f32)]).reshape(small_rows, PACK_W)
    small_rows8 = -(-(small_rows + N_META) // 8) * 8
    reduced = _all_reduce_small(_pad_rows(jnp.concatenate([g["meta"], small], axis=0), small_rows8),
                                name="reduce_small")
    g_meta = lax.dynamic_slice_in_dim(reduced[:N_META], chip * meta_cols, meta_cols, axis=1)
    flat = reduced[N_META:N_META + small_rows].reshape(-1)
    g_small = _unpack_small(flat)
    loss_total = flat[_SMALL_TOTAL]

    small_of = dict(norm_mix_g="norm_mix_g", b_in="b_in", attn_sinks="sinks", rwkv_mix="mix", rwkv_w0="w0",
                    rwkv_a0="a0", rwkv_k_k="k_k", rwkv_k_a="k_a", rwkv_r_k="r_k", rwkv_ln_w="ln_w",
                    rwkv_ln_b="ln_b", norm_ffn_g="norm_ffn_g", norm_final_g="norm_final_g")
    grads = {"meta_tokens": g_meta}
    for n, k in small_of.items():
        grads[n] = g_small[k].reshape(w_all[n].shape)

    delta, new_m, new_v = {}, {}, {}
    in_grad_layout = ("w_in_t", "w_gate_t", "w_up_t")
    for k, n in {**t_of, **plain_of}.items():
        shape2 = w_all[n].shape[1:]
        w_, m_, v_ = (a.reshape(shape2) for a in (w_all[n], m_all[n], v_all[n]))
        if k in in_grad_layout:
            res = [t.T for t in _adamw(w_.T, [parts[k], others[k]], m_.T, v_.T, name="adamw_" + n)]
        else:
            res = _adamw(w_, [parts[k], others[k]], m_, v_, name="adamw_" + n, transposed=k in t_of)
        grads[n], delta[n], new_m[n], new_v[n] = (t.reshape(w_all[n].shape) for t in res)
    rest = [n for n in names if n not in delta]

    def pack_rest(src):
        flat_ = jnp.concatenate([src[n].reshape(-1) for n in rest])
        rows_ = -(-flat_.shape[0] // (8 * PACK_W)) * 8
        return jnp.concatenate([flat_, jnp.ones((rows_ * PACK_W - flat_.shape[0],), f32)]).reshape(rows_, PACK_W)

    _, d_, m_, v_ = _adamw(pack_rest(w_all), [pack_rest(grads)], pack_rest(m_all), pack_rest(v_all),
                           name="adamw_small")
    off = 0
    for n in rest:
        size = w_all[n].size
        for dst, src in ((delta, d_), (new_m, m_), (new_v, v_)):
            dst[n] = src.reshape(-1)[off:off + size].reshape(w_all[n].shape)
        off += size

    return (loss_total, dx.reshape(x.shape), *[grads[n] for n in names], *[delta[n] for n in names],
            *[new_m[n] for n in names], *[new_v[n] for n in names])
```

```python
import math

import jax
import jax.numpy as jnp
from jax import lax
from jax.experimental import pallas as pl
from jax.experimental.pallas import tpu as pltpu

f32 = jnp.float32
bf16 = jnp.bfloat16

D_MODEL = 1024
N_META = 16
HEAD_DIM = 64
Q_HEADS = 8
KV_HEADS = 2
GROUP = Q_HEADS // KV_HEADS
WINDOW = 128
BLOCK = 128
ROPE_THETA = 500000.0
ROPE_DIM = HEAD_DIM // 4
RWKV_HEADS = 8
RWKV_HEAD = 64
RWKV_DIM = RWKV_HEADS * RWKV_HEAD
DECAY_LORA = 64
AAA_LORA = 64
GATE_LORA = 160
LORA_W = DECAY_LORA + AAA_LORA + GATE_LORA
RWKV_LN_EPS = 64e-5
D_FF = 2816
Q_W = Q_HEADS * HEAD_DIM
KV_W = KV_HEADS * HEAD_DIM
ATTN_PROJ = Q_W + 2 * KV_W
RKV_W = 3 * RWKV_DIM
RWKV_PROJ = RKV_W + LORA_W
D_IN = ATTN_PROJ + RWKV_PROJ + 2 * D_MODEL
RMS_EPS = 1e-6
NEG_INF = -1e30
PAD = BLOCK - N_META
FRONT = PAD + N_META

ADAM_LR = 0.001
ADAM_B1 = 0.9
ADAM_B2 = 0.999
ADAM_EPS = 1e-08
ADAM_WD = 0.01
ADAM_STEP = 10

N_CHIPS = 4
N_DEV = 8
CHUNK = 64
VMEM_LIMIT = 56 * 1024 * 1024
PACK_W = 1024
MESH = pl.DeviceIdType.MESH


def _tile(m, pref=384):
    for step in (16, 8):
        for t in range(min(m, pref) // step * step, 0, -step):
            if m % t == 0:
                return t
    return m


def _params(sem=None):
    return pltpu.CompilerParams(dimension_semantics=sem, vmem_limit_bytes=VMEM_LIMIT)


def _full(shape):
    nd = len(shape)
    return pl.BlockSpec(shape, lambda *_: (0,) * nd)


def _dot(a, b, dims="nn"):
    dn = {"nn": (((1,), (0,)), ((), ())), "nt": (((1,), (1,)), ((), ())), "tn": (((0,), (0,)), ((), ()))}[dims]
    return lax.dot_general(a.astype(bf16), b.astype(bf16), dn, preferred_element_type=f32)


def _two_pass(x, m):
    x_hi = x.astype(bf16)
    x_lo = (x - x_hi.astype(f32)).astype(bf16)
    return _dot(x_hi, m) + _dot(x_lo, m)


@jax.custom_vjp
def _dot_const(x, m):
    return _two_pass(x, m)


def _dot_const_fwd(x, m):
    return _two_pass(x, m), m


def _dot_const_bwd(m, ct):
    return _two_pass(ct, m.T), jnp.zeros_like(m)


_dot_const.defvjp(_dot_const_fwd, _dot_const_bwd)


def _two_pass_left(m, x, dims):
    x_hi = x.astype(bf16)
    x_lo = (x - x_hi.astype(f32)).astype(bf16)
    return _dot(m, x_hi, dims) + _dot(m, x_lo, dims)


@jax.custom_vjp
def _const_dot(m, x):
    return _two_pass_left(m, x, "nn")


def _const_dot_fwd(m, x):
    return _two_pass_left(m, x, "nn"), m


def _const_dot_bwd(m, ct):
    return jnp.zeros_like(m), _two_pass_left(m, ct, "tn")


_const_dot.defvjp(_const_dot_fwd, _const_dot_bwd)


def _mm(a, b, mode, *, name, out_dtype=f32, bias=None, add=None, zero_rows_below=0):
    m, _ = a.shape
    n = b.shape[1] if mode == "nn" else b.shape[0]
    tm = _tile(m)
    has_bias, has_add = bias is not None, add is not None

    def body(*refs):
        a_ref, b_ref = refs[0], refs[1]
        o_ref = refs[-1]
        acc = _dot(a_ref[...], b_ref[...], mode)
        k = 2
        if has_bias:
            acc = acc + refs[k][...]
            k += 1
        if zero_rows_below:
            rows = pl.program_id(0) * tm + lax.broadcasted_iota(jnp.int32, acc.shape, 0)
            acc = jnp.where(rows >= zero_rows_below, acc, 0.0)
        if has_add:
            acc = acc + refs[k][...].astype(f32)
        o_ref[...] = acc.astype(out_dtype)

    ins = [a, b]
    in_specs = [pl.BlockSpec((tm, a.shape[1]), lambda i: (i, 0)), _full(b.shape)]
    if has_bias:
        ins.append(bias)
        in_specs.append(_full(bias.shape))
    if has_add:
        ins.append(add)
        in_specs.append(pl.BlockSpec((tm, n), lambda i: (i, 0)))
    return pl.pallas_call(
        body, name=name, grid=(m // tm,), in_specs=in_specs,
        out_specs=pl.BlockSpec((tm, n), lambda i: (i, 0)),
        out_shape=jax.ShapeDtypeStruct((m, n), out_dtype),
        compiler_params=_params(("parallel",)),
    )(*ins)


def _mm_sum(a_list, b_list, *, name):
    m = a_list[0].shape[0]
    n = b_list[0].shape[1]
    k = len(a_list)
    tm = _tile(m)

    def body(*refs):
        acc = _dot(refs[0][...], refs[k][...])
        for i in range(1, k):
            acc = acc + _dot(refs[i][...], refs[k + i][...])
        refs[-1][...] = acc

    return pl.pallas_call(
        body, name=name, grid=(m // tm,),
        in_specs=[pl.BlockSpec((tm, a.shape[1]), lambda i: (i, 0)) for a in a_list] + [_full(b.shape) for b in b_list],
        out_specs=pl.BlockSpec((tm, n), lambda i: (i, 0)),
        out_shape=jax.ShapeDtypeStruct((m, n), f32),
        compiler_params=_params(("parallel",)),
    )(*a_list, *b_list)


def _mm_fanout(a, b_list, bias_list, *, name, zero_rows_below=0):
    m, kdim = a.shape
    k = len(b_list)
    tm = _tile(m)

    def body(*refs):
        av = refs[0][...]
        for j in range(k):
            acc = _dot(av, refs[1 + j][...], "nt") + refs[1 + k + j][...]
            if zero_rows_below:
                rows = pl.program_id(0) * tm + lax.broadcasted_iota(jnp.int32, acc.shape, 0)
                acc = jnp.where(rows >= zero_rows_below, acc, 0.0)
            refs[1 + 2 * k + j][...] = acc

    return pl.pallas_call(
        body, name=name, grid=(m // tm,),
        in_specs=[pl.BlockSpec((tm, kdim), lambda i: (i, 0))] + [_full(b.shape) for b in b_list]
        + [_full(c.shape) for c in bias_list],
        out_specs=[pl.BlockSpec((tm, b.shape[0]), lambda i: (i, 0)) for b in b_list],
        out_shape=[jax.ShapeDtypeStruct((m, b.shape[0]), f32) for b in b_list],
        compiler_params=_params(("parallel",)),
    )(a, *b_list, *bias_list)


def _mm_tn(a, b, *, name, colsum=False, out_dtype=bf16):
    r, m = a.shape
    n = b.shape[1]
    tr = _tile(r, 1408)
    tmo = m
    for cand in (1408, 1024, 768, 512):
        if m > 1024 and m % cand == 0:
            tmo = cand
            break
    steps = r // tr

    def body(a_ref, b_ref, o_ref, *rest):
        acc = rest[-1]
        i = pl.program_id(1)

        @pl.when(i == 0)
        def _():
            acc[...] = jnp.zeros_like(acc)
            if colsum:
                rest[0][...] = jnp.zeros_like(rest[0])

        acc[...] += _dot(a_ref[...], b_ref[...], "tn")
        if colsum:
            rest[0][...] += jnp.sum(a_ref[...].astype(f32), axis=0, keepdims=True)

        @pl.when(i == steps - 1)
        def _():
            o_ref[...] = acc[...].astype(out_dtype)

    out_shape = [jax.ShapeDtypeStruct((m, n), out_dtype)]
    out_specs = [pl.BlockSpec((tmo, n), lambda j, i: (j, 0))]
    if colsum:
        out_shape.append(jax.ShapeDtypeStruct((1, m), f32))
        out_specs.append(pl.BlockSpec((1, tmo), lambda j, i: (0, j)))
    res = pl.pallas_call(
        body, name=name, grid=(m // tmo, steps),
        in_specs=[pl.BlockSpec((tr, tmo), lambda j, i: (i, j)), pl.BlockSpec((tr, n), lambda j, i: (i, 0))],
        out_specs=out_specs, out_shape=out_shape,
        scratch_shapes=[pltpu.VMEM((tmo, n), f32)],
        compiler_params=_params(("parallel", "arbitrary")),
    )(a, b)
    return res if colsum else res[0]


def _rowwise(fn, rows, params, outs, *, name, tm=None):
    m = rows[0].shape[0]
    tm = tm or _tile(m)
    nr, npar = len(rows), len(params)

    def body(*refs):
        vals = [r[...] for r in refs[:nr + npar]]
        res = fn(*vals)
        for o_ref, v in zip(refs[nr + npar:], res):
            o_ref[...] = v.astype(o_ref.dtype)

    return pl.pallas_call(
        body, name=name, grid=(m // tm,),
        in_specs=[pl.BlockSpec((tm, r.shape[1]), lambda i: (i, 0)) for r in rows] + [_full(p.shape) for p in params],
        out_specs=[pl.BlockSpec((tm, w), lambda i: (i, 0)) for w, _ in outs],
        out_shape=[jax.ShapeDtypeStruct((m, w), dt) for w, dt in outs],
        compiler_params=_params(("parallel",)),
    )(*rows, *params)


def _rowwise_bwd(fn, rows, params, cts, *, name, diff_rows, diff_params, tm=None, zero_rows_below=0, out_dtypes=None):
    m = rows[0].shape[0]
    tm = tm or _tile(m)
    nr, npar = len(rows), len(params)
    d_idx = [i for i in range(nr) if diff_rows[i]]
    p_idx = [i for i in range(npar) if diff_params[i]]
    out_dtypes = out_dtypes or [f32] * len(d_idx)
    flat_cts = [c for group in cts for c in group]
    n_ct = len(flat_cts)

    def body(*refs):
        vals = [r[...] for r in refs[:nr + npar]]
        ct_refs = refs[nr + npar:nr + npar + n_ct]
        out_refs = refs[nr + npar + n_ct:]
        ct_vals, k = [], 0
        for group in cts:
            acc = ct_refs[k][...].astype(f32)
            for extra in range(1, len(group)):
                acc = acc + ct_refs[k + extra][...].astype(f32)
            k += len(group)
            if zero_rows_below:
                rr = pl.program_id(0) * tm + lax.broadcasted_iota(jnp.int32, acc.shape, 0)
                acc = jnp.where(rr >= zero_rows_below, acc, 0.0)
            ct_vals.append(acc)

        def g(*dargs):
            full = list(vals)
            for pos, i in enumerate(d_idx):
                full[i] = dargs[pos]
            for pos, i in enumerate(p_idx):
                full[nr + i] = dargs[len(d_idx) + pos]
            return tuple(fn(*full))

        _, vjp = jax.vjp(g, *[vals[i].astype(f32) for i in d_idx], *[vals[nr + i] for i in p_idx])
        grads = vjp(tuple(ct_vals))
        for pos in range(len(d_idx)):
            out_refs[pos][...] = grads[pos].astype(out_refs[pos].dtype)
        first = pl.program_id(0) == 0
        for pos in range(len(p_idx)):
            o_ref = out_refs[len(d_idx) + pos]

            @pl.when(first)
            def _(o_ref=o_ref):
                o_ref[...] = jnp.zeros_like(o_ref)

            o_ref[...] += grads[len(d_idx) + pos]

    return pl.pallas_call(
        body, name=name, grid=(m // tm,),
        in_specs=[pl.BlockSpec((tm, r.shape[1]), lambda i: (i, 0)) for r in rows] + [_full(p.shape) for p in params]
        + [pl.BlockSpec((tm, c.shape[1]), lambda i: (i, 0)) for c in flat_cts],
        out_specs=[pl.BlockSpec((tm, rows[i].shape[1]), lambda i_: (i_, 0)) for i in d_idx]
        + [_full(params[i].shape) for i in p_idx],
        out_shape=[jax.ShapeDtypeStruct(rows[i].shape, dt) for i, dt in zip(d_idx, out_dtypes)]
        + [jax.ShapeDtypeStruct(params[i].shape, f32) for i in p_idx],
        compiler_params=_params(("arbitrary",)),
    )(*rows, *params, *flat_cts)


def _rms(x, g):
    return x * lax.rsqrt(jnp.mean(x * x, axis=-1, keepdims=True) + RMS_EPS) * g


def _head_sum_matrix(width, head):
    idx = jnp.arange(width) // head
    return (idx[:, None] == idx[None, :]).astype(f32)


def _rope_tables(lp):
    half = ROPE_DIM // 2
    pos = (jnp.arange(lp) - PAD).astype(f32)
    inv_freq = jnp.power(jnp.float32(ROPE_THETA), -jnp.arange(half, dtype=f32) * (2.0 / ROPE_DIM))
    ang = pos[:, None] * inv_freq[None, :]
    cos, sin = jnp.cos(ang), jnp.sin(ang)
    ones = jnp.ones((lp, HEAD_DIM - ROPE_DIM), f32)
    zeros = jnp.zeros((lp, HEAD_DIM - ROPE_DIM), f32)
    cos_t = jnp.concatenate([cos, cos, ones], axis=1)
    sin_t = jnp.concatenate([-sin, sin, zeros], axis=1)
    i = jnp.arange(HEAD_DIM)
    src = jnp.where(i < half, i + half, jnp.where(i < ROPE_DIM, i - half, i))
    swap = ((i[:, None] == src[None, :]) & (i[None, :] < ROPE_DIM)).astype(f32)
    return cos_t, sin_t, swap


def _attn_prep(qkv, cos_t, sin_t, swap):
    outs = []
    for h in range(Q_HEADS + KV_HEADS):
        t = qkv[:, h * HEAD_DIM:(h + 1) * HEAD_DIM]
        outs.append(t * cos_t + _dot_const(t, swap) * sin_t)
    q = jnp.concatenate(outs[:Q_HEADS], axis=1)
    k = jnp.concatenate(outs[Q_HEADS:], axis=1)
    return q, k, qkv[:, Q_W + KV_W:]


def _softplus(z):
    return jnp.maximum(z, 0.0) + jnp.log1p(jnp.exp(-jnp.abs(z)))


def _rwkv_prep(rkv, lora, w0, w2, a0, a2, g2, k_k, k_a, hsum):
    r = rkv[:, :RWKV_DIM]
    k = rkv[:, RWKV_DIM:2 * RWKV_DIM]
    v = rkv[:, 2 * RWKV_DIM:]
    dw = lora[:, :DECAY_LORA]
    da = lora[:, DECAY_LORA:DECAY_LORA + AAA_LORA]
    dg = lora[:, DECAY_LORA + AAA_LORA:]
    w = -_softplus(-(w0 + _dot(jnp.tanh(dw), w2))) - 0.5
    a = jax.nn.sigmoid(a0 + _dot(da, a2))
    g = _dot(jax.nn.sigmoid(dg), g2)
    kk = k * k_k
    kk = kk * lax.rsqrt(jnp.maximum(_dot_const(kk * kk, hsum), 1e-24))
    k = k * (1.0 + (a - 1.0) * k_a)
    log_decay = -jnp.exp(w)
    return r, log_decay, k, v, -kk, kk * a, g


def _rwkv_post(y, r, k, v, g, ln_w, ln_b, r_k, hmean):
    hsum = hmean * RWKV_HEAD
    mean = _dot_const(y, hmean)
    yc = y - mean
    var = _dot_const(yc * yc, hmean)
    yn = yc * lax.rsqrt(var + RWKV_LN_EPS) * ln_w + ln_b
    bonus = _dot_const(r * k * r_k, hsum) * v
    return ((yn + bonus) * g,)


def _merge(gates, br_a, br_r):
    sg = jax.nn.sigmoid(gates)
    return (sg[:, :D_MODEL] * br_a + sg[:, D_MODEL:] * br_r,)


def _swiglu(gate, up):
    return (jax.nn.silu(gate) * up,)


def _ffn_in(f, w_gate_t, w_up_t, *, name):
    m, d = f.shape
    n = w_gate_t.shape[0]
    tm = _tile(m)

    def body(f_ref, wg_ref, wu_ref, g_ref, u_ref, a_ref):
        g = _dot(f_ref[...], wg_ref[...], "nt")
        u = _dot(f_ref[...], wu_ref[...], "nt")
        g_ref[...] = g.astype(g_ref.dtype)
        u_ref[...] = u.astype(u_ref.dtype)
        a_ref[...] = _swiglu(g, u)[0].astype(a_ref.dtype)

    spec = pl.BlockSpec((tm, n), lambda i: (i, 0))
    return pl.pallas_call(
        body, name=name, grid=(m // tm,),
        in_specs=[pl.BlockSpec((tm, d), lambda i: (i, 0)), _full(w_gate_t.shape), _full(w_up_t.shape)],
        out_specs=[spec] * 3, out_shape=[jax.ShapeDtypeStruct((m, n), bf16)] * 3,
        compiler_params=_params(("parallel",)),
    )(f, w_gate_t, w_up_t)


def _branch_merge(y_attn, y_rwkv, w_attn_t, w_rwkv_t, gates, *, name):
    m = y_attn.shape[0]
    tm = _tile(m)

    def body(ya_ref, yr_ref, wa_ref, wr_ref, g_ref, a_ref, r_ref, o_ref):
        br_a = _dot(ya_ref[...], wa_ref[...], "nt")
        br_r = _dot(yr_ref[...], wr_ref[...], "nt")
        a_ref[...] = br_a.astype(a_ref.dtype)
        r_ref[...] = br_r.astype(r_ref.dtype)
        o_ref[...] = _merge(g_ref[...], br_a, br_r)[0].astype(o_ref.dtype)

    rows = lambda a: pl.BlockSpec((tm, a.shape[1]), lambda i: (i, 0))
    spec = pl.BlockSpec((tm, D_MODEL), lambda i: (i, 0))
    return pl.pallas_call(
        body, name=name, grid=(m // tm,),
        in_specs=[rows(y_attn), rows(y_rwkv), _full(w_attn_t.shape), _full(w_rwkv_t.shape), rows(gates)],
        out_specs=[spec] * 3, out_shape=[jax.ShapeDtypeStruct((m, D_MODEL), bf16)] * 3,
        compiler_params=_params(("parallel",)),
    )(y_attn, y_rwkv, w_attn_t, w_rwkv_t, gates)


def _branch_merge_bwd(dh, w_o, gates, br_a, br_r, *, name):
    m = dh.shape[0]
    tm = _tile(m)

    def body(dh_ref, w_ref, g_ref, a_ref, r_ref, dg_ref, da_ref, dr_ref):
        dmerged = _dot(dh_ref[...], w_ref[...], "nt")
        _, vjp = jax.vjp(lambda g, a, r: _merge(g, a, r)[0], g_ref[...], a_ref[...].astype(f32),
                         r_ref[...].astype(f32))
        dg, da, dr = vjp(dmerged)
        dg_ref[...] = dg.astype(dg_ref.dtype)
        da_ref[...] = da.astype(da_ref.dtype)
        dr_ref[...] = dr.astype(dr_ref.dtype)

    rows = lambda a: pl.BlockSpec((tm, a.shape[1]), lambda i: (i, 0))
    return pl.pallas_call(
        body, name=name, grid=(m // tm,),
        in_specs=[rows(dh), _full(w_o.shape), rows(gates), rows(br_a), rows(br_r)],
        out_specs=[rows(gates), rows(br_a), rows(br_r)],
        out_shape=[jax.ShapeDtypeStruct(gates.shape, bf16), jax.ShapeDtypeStruct(br_a.shape, bf16),
                   jax.ShapeDtypeStruct(br_r.shape, bf16)],
        compiler_params=_params(("parallel",)),
    )(dh, w_o, gates, br_a, br_r)


def _ffn_in_bwd(dh, w_down, gate, up, *, name):
    m, d = dh.shape
    n = w_down.shape[0]
    tm = _tile(m)

    def body(dh_ref, w_ref, g_ref, u_ref, dg_ref, du_ref):
        dact = _dot(dh_ref[...], w_ref[...], "nt")
        _, vjp = jax.vjp(lambda a, b: _swiglu(a, b)[0], g_ref[...].astype(f32), u_ref[...].astype(f32))
        dg, du = vjp(dact)
        dg_ref[...] = dg.astype(dg_ref.dtype)
        du_ref[...] = du.astype(du_ref.dtype)

    spec = pl.BlockSpec((tm, n), lambda i: (i, 0))
    return pl.pallas_call(
        body, name=name, grid=(m // tm,),
        in_specs=[pl.BlockSpec((tm, d), lambda i: (i, 0)), _full(w_down.shape), spec, spec],
        out_specs=[spec] * 2, out_shape=[jax.ShapeDtypeStruct((m, n), bf16)] * 2,
        compiler_params=_params(("parallel",)),
    )(dh, w_down, gate, up)


def _previous_rows(x, before_ref, first_tile):
    rows = lax.broadcasted_iota(jnp.int32, x.shape, 0)
    last = jnp.where(first_tile, 0.0, before_ref[7:8, :])
    return jnp.where(rows == 0, last, pltpu.roll(x, 1, axis=0))


def _mixer_inputs(ps, mixes, params, *, name):
    m = ps[0].shape[0]
    tm = _tile(m)
    sub = tm // 8
    n_par = len(params)

    def body(*refs):
        first = pl.program_id(0) == 0
        pf = []
        for k in range(2):
            x = refs[k][...]
            pf.append(x + (_previous_rows(x, refs[2 + k], first) - x) * refs[4 + k][...])
        res = _rwkv_prep(*pf, *[ref[...] for ref in refs[6:6 + n_par]])
        for o_ref, val in zip(refs[6 + n_par:], res):
            o_ref[...] = val

    tile = lambda a: pl.BlockSpec((tm, a.shape[1]), lambda i: (i, 0))
    before = lambda a: pl.BlockSpec((8, a.shape[1]), lambda i: (jnp.maximum(i * sub - 1, 0), 0))
    out = pl.BlockSpec((tm, RWKV_DIM), lambda i: (i, 0))
    return pl.pallas_call(
        body, name=name, grid=(m // tm,),
        in_specs=[tile(a) for a in ps] + [before(a) for a in ps] + [_full(a.shape) for a in mixes + params],
        out_specs=[out] * 7, out_shape=[jax.ShapeDtypeStruct((m, RWKV_DIM), f32)] * 7,
        compiler_params=_params(("parallel",)),
    )(*ps, *ps, *mixes, *params)


def _mixer_inputs_bwd(ps, mixes, params, cts, *, name):
    m = ps[0].shape[0]
    tm = _tile(m)
    sub = tm // 8
    nt = m // tm
    n_par = len(params)
    flat_cts = [c for group in cts for c in group]
    n_ct = len(flat_cts)

    def body(*refs):
        i = pl.program_id(0)
        tile_index = nt - 1 - i
        ct_refs = refs[6 + n_par:6 + n_par + n_ct]
        dp_refs = refs[6 + n_par + n_ct:8 + n_par + n_ct]
        dmix_refs = refs[8 + n_par + n_ct:10 + n_par + n_ct]
        dpar_refs = refs[10 + n_par + n_ct:9 + 2 * n_par + n_ct]
        carries = refs[9 + 2 * n_par + n_ct:]
        rows1 = tile_index * tm + lax.broadcasted_iota(jnp.int32, (tm, 1), 0)
        live = rows1 >= PAD

        @pl.when(i == 0)
        def _():
            for ref in (*dmix_refs, *dpar_refs, *carries):
                ref[...] = jnp.zeros_like(ref)

        xs, prevs, pf = [], [], []
        for k in range(2):
            x = refs[k][...]
            xp = _previous_rows(x, refs[2 + k], tile_index == 0)
            xs.append(x)
            prevs.append(xp)
            pf.append(x + (xp - x) * refs[4 + k][...])
        ct_vals, pos = [], 0
        for group in cts:
            acc = ct_refs[pos][...].astype(f32)
            for extra in range(1, len(group)):
                acc = acc + ct_refs[pos + extra][...].astype(f32)
            pos += len(group)
            ct_vals.append(jnp.where(live, acc, 0.0))
        par_vals = [ref[...] for ref in refs[6:6 + n_par]]
        _, vjp = jax.vjp(lambda *args: _rwkv_prep(*args, par_vals[-1]), *pf, *par_vals[:-1])
        g = vjp(tuple(ct_vals))
        for k in range(2):
            dpf = g[k]
            mixv = refs[4 + k][...]
            dm = dpf * mixv
            rows = lax.broadcasted_iota(jnp.int32, dm.shape, 0)
            dm_next = jnp.where(rows == tm - 1, carries[k][...], pltpu.roll(dm, tm - 1, axis=0))
            dp_refs[k][...] = jnp.where(live, dpf - dm + dm_next, 0.0).astype(dp_refs[k].dtype)
            carries[k][...] = dm[0:1, :]
            dmix_refs[k][...] += jnp.sum(dpf * (prevs[k] - xs[k]), axis=0, keepdims=True)
        for ref, val in zip(dpar_refs, g[2:]):
            ref[...] += val

    tile = lambda a: pl.BlockSpec((tm, a.shape[1]), lambda i: (nt - 1 - i, 0))
    before = lambda a: pl.BlockSpec((8, a.shape[1]), lambda i: (jnp.maximum((nt - 1 - i) * sub - 1, 0), 0))
    return pl.pallas_call(
        body, name=name, grid=(nt,),
        in_specs=[tile(a) for a in ps] + [before(a) for a in ps] + [_full(a.shape) for a in mixes + params]
        + [tile(c) for c in flat_cts],
        out_specs=[tile(a) for a in ps] + [_full(a.shape) for a in mixes + params[:-1]],
        out_shape=[jax.ShapeDtypeStruct(a.shape, bf16) for a in ps]
        + [jax.ShapeDtypeStruct(a.shape, f32) for a in mixes + params[:-1]],
        scratch_shapes=[pltpu.VMEM((1, a.shape[1]), f32) for a in ps],
        compiler_params=_params(("arbitrary",)),
    )(*ps, *ps, *mixes, *params, *flat_cts)


def _attn_masks(blk):
    qi = lax.broadcasted_iota(jnp.int32, (BLOCK, BLOCK), 0)
    ki = lax.broadcasted_iota(jnp.int32, (BLOCK, BLOCK), 1)
    qpos = blk * BLOCK + qi - PAD
    kpos_c = blk * BLOCK + ki - PAD
    kpos_p = kpos_c - BLOCK
    kpos_m = ki - PAD

    def band(kpos):
        return (kpos >= N_META) & (kpos <= qpos) & (qpos - kpos < WINDOW)

    return band(kpos_p), band(kpos_c), (kpos_m >= 0) & (kpos_m <= qpos)


def _attn_probs(qs, k3s, sink, oks):
    s = [[jnp.where(ok, _dot(qh, kx, "nt"), NEG_INF) for kx, ok in zip(k3, oks)] for qh, k3 in zip(qs, k3s)]
    mx = [jnp.maximum(jnp.maximum(jnp.max(t[0], -1, keepdims=True), jnp.max(t[1], -1, keepdims=True)),
                      jnp.maximum(jnp.max(t[2], -1, keepdims=True), sk)) for t, sk in zip(s, sink)]
    e = [[jnp.exp(tx - m) for tx in t] for t, m in zip(s, mx)]
    e_sink = [jnp.exp(sk - m) for sk, m in zip(sink, mx)]
    inv = [1.0 / (jnp.sum(t[0], -1, keepdims=True) + jnp.sum(t[1], -1, keepdims=True)
                  + jnp.sum(t[2], -1, keepdims=True) + es) for t, es in zip(e, e_sink)]
    return [[tx * i for tx in t] for t, i in zip(e, inv)], [es * i for es, i in zip(e_sink, inv)]


def _head_cols(i):
    return slice(i * HEAD_DIM, (i + 1) * HEAD_DIM)


def _attn_operands(refs):
    q_ref, kp_ref, kc_ref, km_ref, vp_ref, vc_ref, vm_ref, s_ref = refs
    qs = [q_ref[:, _head_cols(i)] * (HEAD_DIM ** -0.5) for i in range(Q_HEADS)]
    k3 = [[ref[:, _head_cols(h)] for ref in (kp_ref, kc_ref, km_ref)] for h in range(KV_HEADS)]
    v3 = [[ref[:, _head_cols(h)] for ref in (vp_ref, vc_ref, vm_ref)] for h in range(KV_HEADS)]
    return (qs, [k3[i // GROUP] for i in range(Q_HEADS)], [v3[i // GROUP] for i in range(Q_HEADS)],
            [s_ref[:, i:i + 1] for i in range(Q_HEADS)])


def _attention(q, k, v, sinks, *, name):
    lp = q.shape[0]
    nb = lp // BLOCK
    prev = lambda i: (jnp.maximum(i - 1, 0), 0)
    cur = lambda i: (i, 0)
    meta = lambda i: (0, 0)
    kv = lambda index: pl.BlockSpec((BLOCK, KV_W), index)

    def body(*refs):
        o_ref = refs[-1]
        qs, k3s, v3s, sink = _attn_operands(refs[:-1])
        p, _ = _attn_probs(qs, k3s, sink, _attn_masks(pl.program_id(0)))
        out = [_dot(ph[0], v3[0]) + _dot(ph[1], v3[1]) + _dot(ph[2], v3[2]) for ph, v3 in zip(p, v3s)]
        for i in range(Q_HEADS):
            o_ref[:, _head_cols(i)] = out[i].astype(o_ref.dtype)

    return pl.pallas_call(
        body, name=name, grid=(nb,),
        in_specs=[pl.BlockSpec((BLOCK, Q_W), cur), kv(prev), kv(cur), kv(meta), kv(prev), kv(cur), kv(meta),
                  _full((1, Q_HEADS))],
        out_specs=pl.BlockSpec((BLOCK, Q_W), cur),
        out_shape=jax.ShapeDtypeStruct((lp, Q_W), bf16),
        compiler_params=_params(("parallel",)),
    )(q, k, k, k, v, v, v, sinks)


def _attention_bwd(q, k, v, sinks, do, *, name):
    lp = q.shape[0]
    nb = lp // BLOCK
    cur = lambda n: (jnp.minimum(n, nb - 1), 0)
    prev = lambda n: (jnp.maximum(jnp.minimum(n, nb - 1) - 1, 0), 0)
    behind = lambda n: (jnp.maximum(n - 1, 0), 0)
    meta = lambda n: (0, 0)
    kv = lambda index: pl.BlockSpec((BLOCK, KV_W), index)
    scale = HEAD_DIM ** -0.5

    def body(*refs):
        ins, do_ref = refs[:8], refs[8]
        dq_ref, dk_ref, dv_ref, dkm_ref, dvm_ref, ds_ref, carry_k, carry_v = refs[9:]
        n = pl.program_id(0)

        @pl.when(n == 0)
        def _():
            for ref in (dkm_ref, dvm_ref, ds_ref, carry_k, carry_v):
                ref[...] = jnp.zeros_like(ref)

        @pl.when(n < nb)
        def _():
            qs, k3s, v3s, sink = _attn_operands(ins)
            do = [do_ref[:, _head_cols(i)] for i in range(Q_HEADS)]
            p, p_sink = _attn_probs(qs, k3s, sink, _attn_masks(n))
            out = [_dot(ph[0], v3[0]) + _dot(ph[1], v3[1]) + _dot(ph[2], v3[2]) for ph, v3 in zip(p, v3s)]
            delta = [jnp.sum(d * o, -1, keepdims=True) for d, o in zip(do, out)]
            dp = [[_dot(d, vx, "nt") for vx in v3] for d, v3 in zip(do, v3s)]
            ds = [[px * (dx - dl) for px, dx in zip(ph, dh)] for ph, dh, dl in zip(p, dp, delta)]
            dq = [_dot(dsh[0], k3[0]) + _dot(dsh[1], k3[1]) + _dot(dsh[2], k3[2]) for dsh, k3 in zip(ds, k3s)]
            for i in range(Q_HEADS):
                dq_ref[:, _head_cols(i)] = dq[i] * scale
                ds_ref[:, i:i + 1] -= jnp.sum(p_sink[i] * delta[i], axis=0, keepdims=True)
            for h in range(KV_HEADS):
                group = slice(h * GROUP, (h + 1) * GROUP)
                q_all = jnp.concatenate(qs[group], axis=0)
                do_all = jnp.concatenate(do[group], axis=0)
                dk3 = [_dot(jnp.concatenate([dsh[x] for dsh in ds[group]], axis=0), q_all, "tn") for x in range(3)]
                dv3 = [_dot(jnp.concatenate([ph[x] for ph in p[group]], axis=0), do_all, "tn") for x in range(3)]
                hs = _head_cols(h)
                for out_ref, carry, meta_ref, d3 in ((dk_ref, carry_k, dkm_ref, dk3),
                                                     (dv_ref, carry_v, dvm_ref, dv3)):
                    out_ref[:, hs] = carry[:, hs] + d3[0]
                    carry[:, hs] = d3[1]
                    meta_ref[:, hs] += d3[2]

        @pl.when(n == nb)
        def _():
            dk_ref[...] = carry_k[...]
            dv_ref[...] = carry_v[...]

    kv_shape = jax.ShapeDtypeStruct((lp, KV_W), f32)
    one_shape = jax.ShapeDtypeStruct((BLOCK, KV_W), f32)
    return pl.pallas_call(
        body, name=name, grid=(nb + 1,),
        in_specs=[pl.BlockSpec((BLOCK, Q_W), cur), kv(prev), kv(cur), kv(meta), kv(prev), kv(cur), kv(meta),
                  _full((1, Q_HEADS)), pl.BlockSpec((BLOCK, Q_W), cur)],
        out_specs=[pl.BlockSpec((BLOCK, Q_W), cur), kv(behind), kv(behind), kv(meta), kv(meta),
                   _full((1, Q_HEADS))],
        out_shape=[jax.ShapeDtypeStruct((lp, Q_W), f32), kv_shape, kv_shape, one_shape, one_shape,
                   jax.ShapeDtypeStruct((1, Q_HEADS), f32)],
        scratch_shapes=[pltpu.VMEM((BLOCK, KV_W), f32), pltpu.VMEM((BLOCK, KV_W), f32)],
        compiler_params=_params(("arbitrary",)),
    )(q, k, k, k, v, v, v, sinks, do)


@jax.custom_vjp
def _known_inverse(l, x):
    return x


def _known_inverse_fwd(l, x):
    return x, x


def _known_inverse_bwd(x, ct):
    return _dot(_dot(x, ct, "tn"), x, "nt"), jnp.zeros_like(x)


_known_inverse.defvjp(_known_inverse_fwd, _known_inverse_bwd)


def _scan_chunk(s0, r, lw, k, v, a, b, inv=None):
    t = r[0].shape[0]
    ii = lax.broadcasted_iota(jnp.int32, (t, t), 0)
    jj = lax.broadcasted_iota(jnp.int32, (t, t), 1)
    incl = jj <= ii
    strict = jj < ii
    tri = incl.astype(f32)
    eye = jnp.where(ii == jj, 1.0, 0.0)
    cl = [_const_dot(tri, x) for x in lw]
    e_pos = [jnp.exp(c) for c in cl]
    e_neg = [jnp.exp(-c) for c in cl]
    e_prev = [jnp.exp(c - x) for c, x in zip(cl, lw)]
    rt = [x * e for x, e in zip(r, e_pos)]
    at = [x * e for x, e in zip(a, e_prev)]
    bt = [x * e for x, e in zip(b, e_neg)]
    kt = [x * e for x, e in zip(k, e_neg)]
    l_ab = [jnp.where(strict, _dot(x, y, "nt"), 0.0) for x, y in zip(at, bt)]
    l_ak = [jnp.where(strict, _dot(x, y, "nt"), 0.0) for x, y in zip(at, kt)]
    r_b = [jnp.where(incl, _dot(x, y, "nt"), 0.0) for x, y in zip(rt, bt)]
    r_k = [jnp.where(incl, _dot(x, y, "nt"), 0.0) for x, y in zip(rt, kt)]
    if inv is None:
        inv = [eye + x for x in l_ab]
        pw = l_ab
        for _ in range(int(math.log2(t)) - 1):
            pw = [_dot(x, x) for x in pw]
            inv = [x + _dot(x, y) for x, y in zip(inv, pw)]
    else:
        inv = [_known_inverse(x, y) for x, y in zip(l_ab, inv)]
    rhs = [_dot(x, s, "nt") + _dot(m, y) for x, s, m, y in zip(at, s0, l_ak, v)]
    u = [_dot(x, y) for x, y in zip(inv, rhs)]
    y_s = [_dot(x, s, "nt") for x, s in zip(rt, s0)]
    y = [ys + _dot(m, uu) + _dot(n, vv) for ys, m, uu, n, vv in zip(y_s, r_b, u, r_k, v)]
    grow = [s + _dot(uu, x, "tn") + _dot(vv, z, "tn") for s, uu, x, vv, z in zip(s0, u, bt, v, kt)]
    s1 = [g * e[t - 1:t, :] for g, e in zip(grow, e_pos)]
    return y, s1, inv


def _head_rows(h):
    return slice(h * RWKV_HEAD, (h + 1) * RWKV_HEAD)


def _per_head(ref):
    return [ref[:, _head_rows(h)] for h in range(RWKV_HEADS)]


def _scan(r, lw, k, v, a, b, *, name):
    lp = r.shape[0]
    nc = lp // CHUNK
    row = pl.BlockSpec((CHUNK, RWKV_DIM), lambda c: (c, 0))

    def body(r_ref, lw_ref, k_ref, v_ref, a_ref, b_ref, y_ref, s_ref, inv_ref, state):
        @pl.when(pl.program_id(0) == 0)
        def _():
            state[...] = jnp.zeros_like(state)

        s_ref[...] = state[...]
        s0 = [state[_head_rows(h), :] for h in range(RWKV_HEADS)]
        y, s1, inv = _scan_chunk(s0, *[_per_head(ref) for ref in (r_ref, lw_ref, k_ref, v_ref, a_ref, b_ref)])
        for h in range(RWKV_HEADS):
            y_ref[:, _head_rows(h)] = y[h]
            state[_head_rows(h), :] = s1[h]
            inv_ref[h * CHUNK:(h + 1) * CHUNK, :] = inv[h].astype(inv_ref.dtype)

    return pl.pallas_call(
        body, name=name, grid=(nc,), in_specs=[row] * 6,
        out_specs=[row, pl.BlockSpec((RWKV_DIM, RWKV_HEAD), lambda c: (c, 0)),
                   pl.BlockSpec((RWKV_HEADS * CHUNK, CHUNK), lambda c: (c, 0))],
        out_shape=[jax.ShapeDtypeStruct((lp, RWKV_DIM), f32), jax.ShapeDtypeStruct((nc * RWKV_DIM, RWKV_HEAD), f32),
                   jax.ShapeDtypeStruct((nc * RWKV_HEADS * CHUNK, CHUNK), bf16)],
        scratch_shapes=[pltpu.VMEM((RWKV_DIM, RWKV_HEAD), f32)],
        compiler_params=_params(("arbitrary",)),
    )(r, lw, k, v, a, b)


def _scan_bwd(r, lw, k, v, a, b, states, inverses, dy, *, name):
    lp = r.shape[0]
    nc = lp // CHUNK
    back = lambda c: (nc - 1 - c, 0)
    row = pl.BlockSpec((CHUNK, RWKV_DIM), back)

    def body(r_ref, lw_ref, k_ref, v_ref, a_ref, b_ref, s_ref, inv_ref, dy_ref,
             dr_ref, dlw_ref, dk_ref, dv_ref, da_ref, db_ref, dstate):
        @pl.when(pl.program_id(0) == 0)
        def _():
            dstate[...] = jnp.zeros_like(dstate)

        outs = (dr_ref, dlw_ref, dk_ref, dv_ref, da_ref, db_ref)
        s0 = [s_ref[_head_rows(h), :] for h in range(RWKV_HEADS)]
        inv = [inv_ref[h * CHUNK:(h + 1) * CHUNK, :].astype(f32) for h in range(RWKV_HEADS)]
        _, vjp = jax.vjp(lambda *args: _scan_chunk(*args, inv=inv)[:2], s0,
                         *[_per_head(ref) for ref in (r_ref, lw_ref, k_ref, v_ref, a_ref, b_ref)])
        g = vjp((_per_head(dy_ref), [dstate[_head_rows(h), :] for h in range(RWKV_HEADS)]))
        for h in range(RWKV_HEADS):
            dstate[_head_rows(h), :] = g[0][h]
            for o_ref, gv in zip(outs, g[1:]):
                o_ref[:, _head_rows(h)] = gv[h]

    shape = jax.ShapeDtypeStruct((lp, RWKV_DIM), f32)
    return pl.pallas_call(
        body, name=name, grid=(nc,),
        in_specs=[row] * 6 + [pl.BlockSpec((RWKV_DIM, RWKV_HEAD), back),
                              pl.BlockSpec((RWKV_HEADS * CHUNK, CHUNK), back), row],
        out_specs=[row] * 6, out_shape=[shape] * 6,
        scratch_shapes=[pltpu.VMEM((RWKV_DIM, RWKV_HEAD), f32)],
        compiler_params=_params(("arbitrary",)),
    )(r, lw, k, v, a, b, states, inverses, dy)


def _loss_head(h2, target, g_final, *, name):
    lp = h2.shape[0]
    tm = BLOCK
    front_tiles = FRONT // tm

    def body(h_ref, t_ref, g_ref, loss_ref, dh_ref, dg_ref):
        i = pl.program_id(0)
        real = i >= front_tiles

        def tile_loss(hv, gv):
            err = _rms(hv, gv) - t_ref[...]
            return jnp.where(real, 0.5 * jnp.sum(jnp.mean(err * err, axis=-1, keepdims=True)), 0.0)

        loss, (dh, dg) = jax.value_and_grad(tile_loss, argnums=(0, 1))(h_ref[...], g_ref[...])

        @pl.when(i == 0)
        def _():
            loss_ref[...] = jnp.zeros_like(loss_ref)
            dg_ref[...] = jnp.zeros_like(dg_ref)

        loss_ref[...] += jnp.full(loss_ref.shape, loss, f32)
        dg_ref[...] += dg
        dh_ref[...] = dh

    return pl.pallas_call(
        body, name=name, grid=(lp // tm,),
        in_specs=[pl.BlockSpec((tm, D_MODEL), lambda i: (i, 0)),
                  pl.BlockSpec((tm, D_MODEL), lambda i: (jnp.maximum(i - front_tiles, 0), 0)),
                  _full(g_final.shape)],
        out_specs=[_full((8, 128)), pl.BlockSpec((tm, D_MODEL), lambda i: (i, 0)), _full(g_final.shape)],
        out_shape=[jax.ShapeDtypeStruct((8, 128), f32), jax.ShapeDtypeStruct((lp, D_MODEL), f32),
                   jax.ShapeDtypeStruct(g_final.shape, f32)],
        compiler_params=_params(("arbitrary",)),
    )(h2, target, g_final)


def _local_step(x, target, meta, p, early_weights=None, late_weights=None, emit=None):
    emit = emit or (lambda group, grads: 0.0)
    seq = x.shape[0]
    lp = seq + FRONT
    h0 = jnp.concatenate([jnp.zeros((PAD, D_MODEL), f32), meta, x], axis=0)
    cos_t, sin_t, swap = _rope_tables(lp)
    hsum = _head_sum_matrix(RWKV_DIM, RWKV_HEAD)
    hmean = hsum / RWKV_HEAD
    b_qkv, b_rkv = p["b_in"][:, :ATTN_PROJ], p["b_in"][:, ATTN_PROJ:ATTN_PROJ + RKV_W]
    b_lora, b_gates = p["b_in"][:, ATTN_PROJ + RKV_W:ATTN_PROJ + RWKV_PROJ], p["b_in"][:, ATTN_PROJ + RWKV_PROJ:]
    post_params = [p["ln_w"], p["ln_b"], p["r_k"], hmean]

    (u,) = _rowwise(lambda hv, g: (_rms(hv, g),), [h0], [p["norm_mix_g"]], [(D_MODEL, bf16)], name="norm_mix")
    if early_weights is not None:
        p = {**p, **early_weights(u)}
    w_qkv_t, w_rkv_t = p["w_in_t"][:ATTN_PROJ], p["w_in_t"][ATTN_PROJ:ATTN_PROJ + RKV_W]
    w_lora_t, w_gates_t = p["w_in_t"][ATTN_PROJ + RKV_W:ATTN_PROJ + RWKV_PROJ], p["w_in_t"][ATTN_PROJ + RWKV_PROJ:]
    prep_params = [p["w0"], p["w2"], p["a0"], p["a2"], p["g2"], p["k_k"], p["k_a"], hsum]
    w_pieces = [w_qkv_t, w_rkv_t, w_lora_t, w_gates_t]
    qkv, p_rkv, p_lora, gates = _mm_fanout(u, w_pieces, [b_qkv, b_rkv, b_lora, b_gates], name="proj_in",
                                           zero_rows_below=PAD)

    q, k, v = _rowwise(_attn_prep, [qkv, cos_t, sin_t], [swap], [(Q_W, bf16), (KV_W, bf16), (KV_W, bf16)],
                       name="attn_prep")
    y_attn = _attention(q, k, v, p["sinks"], name="attention")

    mix_rkv, mix_lora = p["mix"][:, :RKV_W], p["mix"][:, RKV_W:]
    r_, lw_, k_, v_, a_, b_, g_ = _mixer_inputs([p_rkv, p_lora], [mix_rkv, mix_lora], prep_params,
                                                name="mixer_inputs")
    y_scan, states, inverses = _scan(r_, lw_, k_, v_, a_, b_, name="wkv_scan")
    (y_rwkv,) = _rowwise(_rwkv_post, [y_scan, r_, k_, v_, g_], post_params, [(RWKV_DIM, bf16)], name="rwkv_post")

    if late_weights is not None:
        p = {**p, **late_weights(y_rwkv)}
    br_a, br_r, merged = _branch_merge(y_attn, y_rwkv, p["w_br_attn_t"], p["w_br_rwkv_t"], gates, name="branch_merge")
    h1 = _mm(merged, p["w_o"], "nn", name="out_proj", add=h0)
    (f,) = _rowwise(lambda hv, g: (_rms(hv, g),), [h1], [p["norm_ffn_g"]], [(D_MODEL, bf16)], name="norm_ffn")
    gate, up, act = _ffn_in(f, p["w_gate_t"], p["w_up_t"], name="ffn_in")
    h2 = _mm(act, p["w_down"], "nn", name="ffn_down", add=h1)

    loss8, dh2, d_final_g = _loss_head(h2, target, p["norm_final_g"], name="loss_head")
    dgate, dup = _ffn_in_bwd(dh2, p["w_down"], gate, up, name="ffn_in_bwd")
    d_w_down = _mm_tn(act, dh2, name="dw_down")
    d_w_gate_t = _mm_tn(dgate, f, name="dw_gate")
    d_w_up_t = _mm_tn(dup, f, name="dw_up")
    zero = emit("ffn", dict(w_down=d_w_down, w_gate_t=d_w_gate_t, w_up_t=d_w_up_t))
    df = _mm_sum([dgate, dup], [p["w_gate_t"], p["w_up_t"]], name="d_f")
    dh1, d_ffn_g = _rowwise_bwd(lambda hv, g: (_rms(hv, g), hv), [h1], [p["norm_ffn_g"] + zero], [[df], [dh2]],
                                name="norm_ffn_bwd", diff_rows=[True], diff_params=[True])
    dgates, dbr_a, dbr_r = _branch_merge_bwd(dh1, p["w_o"], gates, br_a, br_r, name="branch_merge_bwd")
    d_w_o = _mm_tn(merged, dh1, name="dw_o")
    d_w_br_attn_t = _mm_tn(dbr_a, y_attn, name="dw_br_attn")
    d_w_br_rwkv_t = _mm_tn(dbr_r, y_rwkv, name="dw_br_rwkv")
    zero = emit("branch", dict(w_o=d_w_o, w_br_attn_t=d_w_br_attn_t, w_br_rwkv_t=d_w_br_rwkv_t))
    dy_attn = _mm(dbr_a, p["w_br_attn_t"], "nn", name="d_y_attn")
    dy_rwkv = _mm(dbr_r, p["w_br_rwkv_t"], "nn", name="d_y_rwkv")

    post_params = [p["ln_w"] + zero, p["ln_b"], p["r_k"], hmean]
    res = _rowwise_bwd(_rwkv_post, [y_scan, r_, k_, v_, g_], post_params, [[dy_rwkv]], name="rwkv_post_bwd",
                       diff_rows=[True] * 5, diff_params=[True, True, True, False])
    dy_scan, dr_p, dk_p, dv_p, dg_p, d_ln_w, d_ln_b, d_r_k = res
    dr_s, dlw_s, dk_s, dv_s, da_s, db_s = _scan_bwd(r_, lw_, k_, v_, a_, b_, states, inverses, dy_scan,
                                                    name="wkv_scan_bwd")
    res = _mixer_inputs_bwd([p_rkv, p_lora], [mix_rkv, mix_lora], prep_params,
                            [[dr_s, dr_p], [dlw_s], [dk_s, dk_p], [dv_s, dv_p], [da_s], [db_s], [dg_p]],
                            name="mixer_inputs_bwd")
    dp_rkv, dp_lora, d_mix_rkv, d_mix_lora, d_w0, d_w2, d_a0, d_a2, d_g2, d_k_k, d_k_a = res

    dq, dk, dv, dkm, dvm, d_sinks = _attention_bwd(q, k, v, p["sinks"], dy_attn, name="attention_bwd")
    rest = jnp.zeros((lp - BLOCK, KV_W), f32)
    dkm, dvm = jnp.concatenate([dkm, rest], axis=0), jnp.concatenate([dvm, rest], axis=0)
    (dqkv,) = _rowwise_bwd(_attn_prep, [qkv, cos_t, sin_t], [swap], [[dq], [dk, dkm], [dv, dvm]], name="attn_prep_bwd",
                           diff_rows=[True, False, False], diff_params=[False], out_dtypes=[bf16])

    d_w_qkv_t, db_qkv = _mm_tn(dqkv, u, name="dw_qkv", colsum=True)
    d_w_rkv_t, db_rkv = _mm_tn(dp_rkv, u, name="dw_rkv", colsum=True)
    d_w_lora_t, db_lora = _mm_tn(dp_lora, u, name="dw_lora", colsum=True)
    d_w_gates_t, db_gates = _mm_tn(dgates, u, name="dw_gates", colsum=True)
    d_w_in_t = jnp.concatenate([d_w_qkv_t, d_w_rkv_t, d_w_lora_t, d_w_gates_t], axis=0)
    zero = emit("input", dict(w_in_t=d_w_in_t, g2=d_g2, w2=d_w2, a2=d_a2))
    du = _mm_sum([dqkv, dp_rkv, dp_lora, dgates], w_pieces, name="d_u")
    dh0, d_mix_g = _rowwise_bwd(lambda hv, g: (_rms(hv, g), hv), [h0], [p["norm_mix_g"] + zero], [[du], [dh1]],
                                name="norm_mix_bwd", diff_rows=[True], diff_params=[True])

    grads = dict(
        w_in_t=d_w_in_t,
        b_in=jnp.concatenate([db_qkv, db_rkv, db_lora, db_gates], axis=1),
        mix=jnp.concatenate([d_mix_rkv, d_mix_lora], axis=1),
        norm_mix_g=d_mix_g, sinks=d_sinks, w0=d_w0, w2=d_w2, a0=d_a0, a2=d_a2, g2=d_g2, k_k=d_k_k, k_a=d_k_a,
        r_k=d_r_k, ln_w=d_ln_w, ln_b=d_ln_b, w_br_attn_t=d_w_br_attn_t, w_br_rwkv_t=d_w_br_rwkv_t, w_o=d_w_o,
        norm_ffn_g=d_ffn_g, w_gate_t=d_w_gate_t, w_up_t=d_w_up_t, w_down=d_w_down, norm_final_g=d_final_g,
        meta=dh0[PAD:FRONT],
    )
    return loss8[0, 0], dh0[FRONT:], grads


def _position():
    return lax.axis_index("x"), lax.axis_index("y"), lax.axis_index("c")


def _other_chips(x, y):
    return [(1 - x, y), (x, 1 - y), (1 - x, 1 - y)]


_HBM = pl.BlockSpec(memory_space=pltpu.HBM)
_SEM = pl.BlockSpec(memory_space=pltpu.SEMAPHORE)
_EFFECT = pltpu.SideEffectType.DATAFLOW_SIDE_EFFECTING


def _landing_zone(src, kind):
    shape = {"whole": (N_CHIPS,) + src.shape, "half": (2, N_CHIPS) + src.shape[1:], "slab": (3,) + src.shape[1:]}[kind]
    return lax.empty(shape, src.dtype)


def _chip_copies(src_refs, land_refs, send_sems, recv_sems, kind):
    x, y, c = _position()
    copies = []
    for a, (src, land) in enumerate(zip(src_refs, land_refs)):
        for j, (px, py) in enumerate(_other_chips(x, y)):
            if kind == "whole":
                src_ref, dst_ref = src, land.at[2 * x + y]
            elif kind == "half":
                src_ref, dst_ref = src.at[c], land.at[c, 2 * x + y]
            else:
                src_ref, dst_ref = src.at[2 * px + py], land.at[j]
            copies.append(pltpu.make_async_remote_copy(
                src_ref=src_ref, dst_ref=dst_ref, send_sem=send_sems.at[3 * a + j], recv_sem=recv_sems.at[3 * a + j],
                device_id=(px, py, c), device_id_type=MESH))
    return copies


def _exchange_start(srcs, *, kind, name):
    n = len(srcs)
    lands = [_landing_zone(s, kind) for s in srcs]

    def body(*refs):
        for cp in _chip_copies(refs[:n], refs[n:2 * n], refs[2 * n], refs[2 * n + 1], kind):
            cp.start()
        refs[-1][...] = jnp.zeros_like(refs[-1])

    res = pl.pallas_call(
        body, name=name,
        out_shape=(pltpu.SemaphoreType.DMA((3 * n,)), pltpu.SemaphoreType.DMA((3 * n,)),
                   *[pltpu.HBM(a.shape, a.dtype) for a in srcs + lands], jax.ShapeDtypeStruct((8, 128), f32)),
        in_specs=[_HBM] * (2 * n),
        out_specs=(_SEM, _SEM, *[_HBM] * (2 * n), pl.BlockSpec(memory_space=pltpu.VMEM)),
        input_output_aliases={i: 2 + i for i in range(2 * n)},
        compiler_params=pltpu.CompilerParams(has_side_effects=_EFFECT),
    )(*[pltpu.with_memory_space_constraint(a, pltpu.HBM) for a in srcs + lands])
    return res[0], res[1], list(res[2:2 + n]), list(res[2 + n:2 + 2 * n]), res[-1]


def _exchange_wait(handle, after, *, kind, name):
    send_sems, recv_sems, srcs, lands, _ = handle
    n = len(srcs)

    def body(*refs):
        for cp in _chip_copies(refs[:n], refs[n:2 * n], refs[2 * n], refs[2 * n + 1], kind):
            cp.wait_send()
            cp.wait_recv()

    res = pl.pallas_call(
        body, name=name,
        out_shape=tuple(pltpu.HBM(a.shape, a.dtype) for a in srcs + lands),
        in_specs=[_HBM] * (2 * n) + [_SEM, _SEM, pl.BlockSpec(memory_space=pl.ANY)],
        out_specs=tuple([_HBM] * (2 * n)),
        input_output_aliases={i: i for i in range(2 * n)},
        compiler_params=pltpu.CompilerParams(has_side_effects=_EFFECT),
    )(*srcs, *lands, send_sems, recv_sems, after)
    return list(res[:n]), list(res[n:])


def _sum_own_and_received(g, recv, *, name):
    _, r, w = g.shape
    tm = _tile(r)
    if g.dtype == bf16 and tm % 16:
        tm = r
    x, y, _ = _position()
    me = jnp.reshape(2 * x + y, (1,)).astype(jnp.int32)

    def body(me_ref, g_ref, r_ref, o_ref):
        o_ref[...] = (g_ref[0].astype(f32) + r_ref[0].astype(f32)) + (r_ref[1].astype(f32) + r_ref[2].astype(f32))

    return pl.pallas_call(
        body, name=name,
        grid_spec=pltpu.PrefetchScalarGridSpec(
            num_scalar_prefetch=1, grid=(r // tm,),
            in_specs=[pl.BlockSpec((1, tm, w), lambda i, me_ref: (me_ref[0], i, 0)),
                      pl.BlockSpec((3, tm, w), lambda i, me_ref: (0, i, 0))],
            out_specs=pl.BlockSpec((tm, w), lambda i, me_ref: (i, 0))),
        out_shape=jax.ShapeDtypeStruct((r, w), f32),
        compiler_params=_params(("parallel",)),
    )(me, g, recv)


def _swap_cores(arrs, *, name):
    n = len(arrs)

    def body(*refs):
        x, y, c = _position()
        copies = [pltpu.make_async_remote_copy(
            src_ref=refs[i], dst_ref=refs[n + i], send_sem=refs[2 * n].at[i], recv_sem=refs[2 * n + 1].at[i],
            device_id=(x, y, 1 - c), device_id_type=MESH) for i in range(n)]
        for cp in copies:
            cp.start()
        for cp in copies:
            cp.wait_recv()
        for cp in copies:
            cp.wait_send()

    return pl.pallas_call(
        body, name=name,
        in_specs=[pl.BlockSpec(memory_space=pl.ANY)] * n,
        out_specs=[pl.BlockSpec(memory_space=pl.ANY)] * n,
        out_shape=[jax.ShapeDtypeStruct(a.shape, a.dtype) for a in arrs],
        scratch_shapes=[pltpu.SemaphoreType.DMA((n,)), pltpu.SemaphoreType.DMA((n,))],
    )(*arrs)


def _swap_halves(zone, *, name):
    def body(z_ref, o_ref, send_sems, recv_sems):
        x, y, c = _position()
        mine = [pltpu.make_async_remote_copy(
            src_ref=o_ref.at[c, 2 * px + py], dst_ref=o_ref.at[c, 2 * px + py], send_sem=send_sems.at[j],
            recv_sem=recv_sems.at[j], device_id=(x, y, 1 - c), device_id_type=MESH)
            for j, (px, py) in enumerate(_other_chips(x, y))]
        for cp in mine:
            cp.start()
        for j, (px, py) in enumerate(_other_chips(x, y)):
            pltpu.make_async_remote_copy(
                src_ref=o_ref.at[c, 2 * px + py], dst_ref=o_ref.at[1 - c, 2 * px + py], send_sem=send_sems.at[j],
                recv_sem=recv_sems.at[j], device_id=(x, y, 1 - c), device_id_type=MESH).wait_recv()
        for cp in mine:
            cp.wait_send()

    return pl.pallas_call(
        body, name=name,
        in_specs=[pl.BlockSpec(memory_space=pl.ANY)], out_specs=pl.BlockSpec(memory_space=pl.ANY),
        out_shape=jax.ShapeDtypeStruct(zone.shape, zone.dtype), input_output_aliases={0: 0},
        scratch_shapes=[pltpu.SemaphoreType.DMA((3,)), pltpu.SemaphoreType.DMA((3,))],
    )(zone)


def _all_reduce_small(a, after, *, name):
    rows, w = a.shape

    def body(a_ref, after_ref, o_ref, buf, send_sems, recv_sems):
        x, y, c = _position()
        me = 4 * x + 2 * y + c
        buf[0] = a_ref[...]
        sends = []
        for rel in range(1, N_DEV):
            peer = ((1 - x) if rel & 4 else x, (1 - y) if rel & 2 else y, (1 - c) if rel & 1 else c)
            cp = pltpu.make_async_remote_copy(
                src_ref=a_ref, dst_ref=buf.at[rel], send_sem=send_sems.at[rel - 1], recv_sem=recv_sems.at[rel - 1],
                device_id=peer, device_id_type=MESH)
            cp.start()
            sends.append(cp)
        for cp in sends:
            cp.wait_recv()
        for cp in sends:
            cp.wait_send()
        acc = buf[jnp.bitwise_xor(me, 0)]
        for d in range(1, N_DEV):
            acc = acc + buf[jnp.bitwise_xor(me, d)]
        o_ref[...] = acc

    return pl.pallas_call(
        body, name=name,
        in_specs=[pl.BlockSpec(memory_space=pltpu.VMEM), pl.BlockSpec(memory_space=pl.ANY)],
        out_specs=pl.BlockSpec(memory_space=pltpu.VMEM),
        out_shape=jax.ShapeDtypeStruct((rows, w), f32),
        scratch_shapes=[pltpu.VMEM((N_DEV, rows, w), f32), pltpu.SemaphoreType.DMA((N_DEV - 1,)),
                        pltpu.SemaphoreType.DMA((N_DEV - 1,))],
    )(a, after)


def _adamw(w, g_parts, m, v, *, name, transposed=False):
    rows, cols = w.shape
    if transposed:
        tm = 256 if rows % 256 == 0 else rows
        g_spec = pl.BlockSpec((cols, tm), lambda i: (0, i))
    else:
        tm = _tile(rows, 256)
        g_spec = pl.BlockSpec((tm, cols), lambda i: (i, 0))
    n = len(g_parts)

    def body(*refs):
        w_ref, m_ref, v_ref = refs[0], refs[1 + n], refs[2 + n]
        g_ref, d_ref, nm_ref, nv_ref = refs[3 + n:]
        gv = refs[1][...]
        for part in refs[2:1 + n]:
            gv = gv + part[...]
        if transposed:
            gv = gv.T
        g_ref[...] = gv
        nm = ADAM_B1 * m_ref[...] + (1.0 - ADAM_B1) * gv
        nv = ADAM_B2 * v_ref[...] + (1.0 - ADAM_B2) * (gv * gv)
        m_hat = nm / (1.0 - ADAM_B1 ** ADAM_STEP)
        v_hat = nv / (1.0 - ADAM_B2 ** ADAM_STEP)
        d_ref[...] = -ADAM_LR * (m_hat / (jnp.sqrt(v_hat) + ADAM_EPS) + ADAM_WD * w_ref[...])
        nm_ref[...] = nm
        nv_ref[...] = nv

    spec = pl.BlockSpec((tm, cols), lambda i: (i, 0))
    shape = jax.ShapeDtypeStruct((rows, cols), f32)
    return pl.pallas_call(
        body, name=name, grid=(rows // tm,), in_specs=[spec] + [g_spec] * n + [spec] * 2,
        out_specs=[spec] * 4, out_shape=[shape] * 4,
        compiler_params=_params(("parallel",)),
    )(w, *g_parts, m, v)


def _pad_rows(a, rows):
    return jnp.concatenate([a, jnp.zeros((rows - a.shape[0], a.shape[1]), a.dtype)], axis=0) if rows > a.shape[0] else a


_SMALL = (("norm_mix_g", D_MODEL), ("b_in", D_IN), ("sinks", Q_HEADS), ("mix", RWKV_PROJ), ("w0", RWKV_DIM),
          ("a0", RWKV_DIM), ("k_k", RWKV_DIM), ("k_a", RWKV_DIM), ("r_k", RWKV_DIM), ("ln_w", RWKV_DIM),
          ("ln_b", RWKV_DIM), ("norm_ffn_g", D_MODEL), ("norm_final_g", D_MODEL))


def _pack_small(d):
    flat = jnp.concatenate([d[n].reshape(-1).astype(f32) for n, _ in _SMALL])
    return flat


def _unpack_small(flat):
    out, off = {}, 0
    for n, size in _SMALL:
        out[n] = flat[off:off + size]
        off += size
    return out


_SMALL_TOTAL = sum(s for _, s in _SMALL)


def kernel(x, meta_tokens, norm_mix_g, w_in, b_in, attn_sinks, rwkv_mix, rwkv_w0, rwkv_w2, rwkv_a0, rwkv_a2, rwkv_g2, rwkv_k_k, rwkv_k_a, rwkv_r_k, rwkv_ln_w, rwkv_ln_b, w_br_attn, w_br_rwkv, w_o, norm_ffn_g, w_ffn_gate, w_ffn_up, w_ffn_down, norm_final_g, loss_target, m_meta_tokens, m_norm_mix_g, m_w_in, m_b_in, m_attn_sinks, m_rwkv_mix, m_rwkv_w0, m_rwkv_w2, m_rwkv_a0, m_rwkv_a2, m_rwkv_g2, m_rwkv_k_k, m_rwkv_k_a, m_rwkv_r_k, m_rwkv_ln_w, m_rwkv_ln_b, m_w_br_attn, m_w_br_rwkv, m_w_o, m_norm_ffn_g, m_w_ffn_gate, m_w_ffn_up, m_w_ffn_down, m_norm_final_g, v_meta_tokens, v_norm_mix_g, v_w_in, v_b_in, v_attn_sinks, v_rwkv_mix, v_rwkv_w0, v_rwkv_w2, v_rwkv_a0, v_rwkv_a2, v_rwkv_g2, v_rwkv_k_k, v_rwkv_k_a, v_rwkv_r_k, v_rwkv_ln_w, v_rwkv_ln_b, v_w_br_attn, v_w_br_rwkv, v_w_o, v_norm_ffn_g, v_w_ffn_gate, v_w_ffn_up, v_w_ffn_down, v_norm_final_g):
    names = ("meta_tokens", "norm_mix_g", "w_in", "b_in", "attn_sinks", "rwkv_mix", "rwkv_w0", "rwkv_w2", "rwkv_a0",
             "rwkv_a2", "rwkv_g2", "rwkv_k_k", "rwkv_k_a", "rwkv_r_k", "rwkv_ln_w", "rwkv_ln_b", "w_br_attn",
             "w_br_rwkv", "w_o", "norm_ffn_g", "w_ffn_gate", "w_ffn_up", "w_ffn_down", "norm_final_g")
    w_all = dict(zip(names, (meta_tokens, norm_mix_g, w_in, b_in, attn_sinks, rwkv_mix, rwkv_w0, rwkv_w2, rwkv_a0,
                             rwkv_a2, rwkv_g2, rwkv_k_k, rwkv_k_a, rwkv_r_k, rwkv_ln_w, rwkv_ln_b, w_br_attn,
                             w_br_rwkv, w_o, norm_ffn_g, w_ffn_gate, w_ffn_up, w_ffn_down, norm_final_g)))
    m_all = dict(zip(names, (m_meta_tokens, m_norm_mix_g, m_w_in, m_b_in, m_attn_sinks, m_rwkv_mix, m_rwkv_w0,
                             m_rwkv_w2, m_rwkv_a0, m_rwkv_a2, m_rwkv_g2, m_rwkv_k_k, m_rwkv_k_a, m_rwkv_r_k,
                             m_rwkv_ln_w, m_rwkv_ln_b, m_w_br_attn, m_w_br_rwkv, m_w_o, m_norm_ffn_g, m_w_ffn_gate,
                             m_w_ffn_up, m_w_ffn_down, m_norm_final_g)))
    v_all = dict(zip(names, (v_meta_tokens, v_norm_mix_g, v_w_in, v_b_in, v_attn_sinks, v_rwkv_mix, v_rwkv_w0,
                             v_rwkv_w2, v_rwkv_a0, v_rwkv_a2, v_rwkv_g2, v_rwkv_k_k, v_rwkv_k_a, v_rwkv_r_k,
                             v_rwkv_ln_w, v_rwkv_ln_b, v_w_br_attn, v_w_br_rwkv, v_w_o, v_norm_ffn_g, v_w_ffn_gate,
                             v_w_ffn_up, v_w_ffn_down, v_norm_final_g)))
    cx, cy, _ = _position()
    chip = 2 * cx + cy

    t_of = dict(w_in_t="w_in", w_gate_t="w_ffn_gate", w_up_t="w_ffn_up", w_br_attn_t="w_br_attn",
                w_br_rwkv_t="w_br_rwkv", g2_t="rwkv_g2", w2_t="rwkv_w2", a2_t="rwkv_a2")
    plain_of = dict(w_down="w_ffn_down", w_o="w_o")
    meta_cols = meta_tokens.shape[1]

    def shard(k):
        return (w_all[t_of[k]][0].T if k in t_of else w_all[plain_of[k]][0]).astype(bf16)

    def whole(zone, own):
        return lax.dynamic_update_slice_in_dim(zone, own[None], chip, axis=0).reshape(-1, own.shape[-1])

    tiny = ("g2_t", "w2_t", "a2_t")
    late = ("w_gate_t", "w_up_t", "w_down", "w_o", "w_br_attn_t", "w_br_rwkv_t")
    w_in_own = shard("w_in_t")
    w_in_rows, w_in_cols = w_in_own.shape
    w_in_halves = w_in_own.reshape(w_in_rows, 2, w_in_cols // 2).transpose(1, 0, 2)
    tiny_h = _exchange_start([shard(k) for k in tiny] + [meta_tokens], kind="whole", name="gather_tiny_start")
    w_in_h = _exchange_start([w_in_halves], kind="half", name="gather_w_in_start")
    late_h = _exchange_start([shard(k) for k in late], kind="whole", name="gather_late_start")
    own, zones = _exchange_wait(tiny_h, late_h[4], kind="whole", name="gather_tiny_wait")
    got = {k: whole(z, o) for k, z, o in zip(tiny, zones, own)}
    meta_full = whole(zones[-1], own[-1]).reshape(N_CHIPS, N_META, meta_cols).transpose(1, 0, 2).reshape(N_META, -1)
    p = dict(
        g2=got["g2_t"].T.astype(f32), w2=got["w2_t"].T.astype(f32), a2=got["a2_t"].T.astype(f32),
        b_in=b_in, sinks=attn_sinks, mix=rwkv_mix, w0=rwkv_w0, a0=rwkv_a0, k_k=rwkv_k_k, k_a=rwkv_k_a,
        r_k=rwkv_r_k.reshape(1, RWKV_DIM), ln_w=rwkv_ln_w, ln_b=rwkv_ln_b, norm_mix_g=norm_mix_g,
        norm_ffn_g=norm_ffn_g, norm_final_g=norm_final_g.reshape(1, D_MODEL),
    )

    def early_weights(after):
        own_h, zones_h = _exchange_wait(w_in_h, after, kind="half", name="gather_w_in_wait")
        zone = _swap_halves(zones_h[0], name="swap_w_in_halves")
        zone = lax.dynamic_update_slice(zone, own_h[0][:, None], (0, chip, 0, 0))
        return dict(w_in_t=zone.transpose(1, 2, 0, 3).reshape(N_CHIPS * w_in_rows, w_in_cols))

    def late_weights(after):
        own_l, zones_l = _exchange_wait(late_h, after, kind="whole", name="gather_late_wait")
        return {k: whole(z, o) for k, z, o in zip(late, zones_l, own_l)}

    started = {}

    def emit(group, grads_):
        keys = list(grads_)
        slabs = []
        for k in keys:
            a = grads_[k].T if k in ("g2", "w2", "a2") else grads_[k]
            slabs.append(a.reshape(N_CHIPS, a.shape[0] // N_CHIPS, a.shape[1]))
        started[group] = (keys, _exchange_start(slabs, kind="slab", name="scatter_" + group + "_start"))
        return started[group][1][4][0, 0]

    loss, dx, g = _local_step(x[0], loss_target[0], meta_full, p, early_weights, late_weights, emit)

    grads, delta, new_m, new_v = {}, {}, {}, {}
    in_grad_layout = ("w_in_t", "w_gate_t", "w_up_t")
    weight_of = {**t_of, **plain_of}

    def finish(groups, after, tag):
        parts = {}
        for group in groups:
            keys, handle = started[group]
            slabs, lands = _exchange_wait(handle, after, kind="slab", name="scatter_" + group + "_wait")
            parts.update({k: _sum_own_and_received(s, l, name="sum_chips_" + k) for k, s, l in zip(keys, slabs, lands)})
        keys = list(parts)
        others = dict(zip(keys, _swap_cores([parts[k] for k in keys], name="swap_cores_" + tag)))
        for k in keys:
            both = [parts[k], others[k]]
            k = k + "_t" if k in ("g2", "w2", "a2") else k
            n = weight_of[k]
            shape2 = w_all[n].shape[1:]
            w_, m_, v_ = (a.reshape(shape2) for a in (w_all[n], m_all[n], v_all[n]))
            if k in in_grad_layout:
                res = [t.T for t in _adamw(w_.T, both, m_.T, v_.T, name="adamw_" + n)]
            else:
                res = _adamw(w_, both, m_, v_, name="adamw_" + n, transposed=k in t_of)
            grads[n], delta[n], new_m[n], new_v[n] = (t.reshape(w_all[n].shape) for t in res)
        return delta[n]

    done = finish(("ffn", "branch"), dx, "a")

    small = jnp.concatenate([_pack_small(g), loss.reshape(1)])
    small_rows = -(-small.shape[0] // PACK_W)
    small = jnp.concatenate([small, jnp.zeros((small_rows * PACK_W - small.shape[0],), f32)]).reshape(small_rows, PACK_W)
    small_rows8 = -(-(small_rows + N_META) // 8) * 8
    reduced = _all_reduce_small(_pad_rows(jnp.concatenate([g["meta"], small], axis=0), small_rows8), done,
                                name="reduce_small")
    finish(("input",), reduced, "b")
    g_meta = lax.dynamic_slice_in_dim(reduced[:N_META], chip * meta_cols, meta_cols, axis=1)
    flat = reduced[N_META:N_META + small_rows].reshape(-1)
    g_small = _unpack_small(flat)
    loss_total = flat[_SMALL_TOTAL]

    small_of = dict(norm_mix_g="norm_mix_g", b_in="b_in", attn_sinks="sinks", rwkv_mix="mix", rwkv_w0="w0",
                    rwkv_a0="a0", rwkv_k_k="k_k", rwkv_k_a="k_a", rwkv_r_k="r_k", rwkv_ln_w="ln_w",
                    rwkv_ln_b="ln_b", norm_ffn_g="norm_ffn_g", norm_final_g="norm_final_g")
    grads["meta_tokens"] = g_meta
    for n, k in small_of.items():
        grads[n] = g_small[k].reshape(w_all[n].shape)

    rest = [n for n in names if n not in delta]

    def pack_rest(src):
        flat_ = jnp.concatenate([src[n].reshape(-1) for n in rest])
        rows_ = -(-flat_.shape[0] // (8 * PACK_W)) * 8
        return jnp.concatenate([flat_, jnp.ones((rows_ * PACK_W - flat_.shape[0],), f32)]).reshape(rows_, PACK_W)

    _, d_, m_, v_ = _adamw(pack_rest(w_all), [pack_rest(grads)], pack_rest(m_all), pack_rest(v_all),
                           name="adamw_small")
    off = 0
    for n in rest:
        size = w_all[n].size
        for dst, src in ((delta, d_), (new_m, m_), (new_v, v_)):
            dst[n] = src.reshape(-1)[off:off + size].reshape(w_all[n].shape)
        off += size

    return (loss_total, dx.reshape(x.shape), *[grads[n] for n in names], *[delta[n] for n in names],
            *[new_m[n] for n in names], *[new_v[n] for n in names])
```

```python
import math

import jax
import jax.numpy as jnp
from jax import lax
from jax.experimental import pallas as pl
from jax.experimental.pallas import tpu as pltpu

f32 = jnp.float32
bf16 = jnp.bfloat16

D_MODEL = 1024
N_META = 16
HEAD_DIM = 64
Q_HEADS = 8
KV_HEADS = 2
GROUP = Q_HEADS // KV_HEADS
WINDOW = 128
BLOCK = 128
ROPE_THETA = 500000.0
ROPE_DIM = HEAD_DIM // 4
RWKV_HEADS = 8
RWKV_HEAD = 64
RWKV_DIM = RWKV_HEADS * RWKV_HEAD
DECAY_LORA = 64
AAA_LORA = 64
GATE_LORA = 160
LORA_W = DECAY_LORA + AAA_LORA + GATE_LORA
RWKV_LN_EPS = 64e-5
D_FF = 2816
Q_W = Q_HEADS * HEAD_DIM
KV_W = KV_HEADS * HEAD_DIM
ATTN_PROJ = Q_W + 2 * KV_W
RKV_W = 3 * RWKV_DIM
RWKV_PROJ = RKV_W + LORA_W
D_IN = ATTN_PROJ + RWKV_PROJ + 2 * D_MODEL
RMS_EPS = 1e-6
NEG_INF = -1e30
PAD = BLOCK - N_META
FRONT = PAD + N_META

ADAM_LR = 0.001
ADAM_B1 = 0.9
ADAM_B2 = 0.999
ADAM_EPS = 1e-08
ADAM_WD = 0.01
ADAM_STEP = 10

N_CHIPS = 4
N_DEV = 8
CHUNK = 64
VMEM_LIMIT = 56 * 1024 * 1024
PACK_W = 1024
MESH = pl.DeviceIdType.MESH


def _tile(m, pref=384):
    for step in (16, 8):
        for t in range(min(m, pref) // step * step, 0, -step):
            if m % t == 0:
                return t
    return m


def _params(sem=None):
    return pltpu.CompilerParams(dimension_semantics=sem, vmem_limit_bytes=VMEM_LIMIT)


def _full(shape):
    nd = len(shape)
    return pl.BlockSpec(shape, lambda *_: (0,) * nd)


def _dot(a, b, dims="nn"):
    dn = {"nn": (((1,), (0,)), ((), ())), "nt": (((1,), (1,)), ((), ())), "tn": (((0,), (0,)), ((), ()))}[dims]
    return lax.dot_general(a.astype(bf16), b.astype(bf16), dn, preferred_element_type=f32)


def _two_pass(x, m):
    x_hi = x.astype(bf16)
    x_lo = (x - x_hi.astype(f32)).astype(bf16)
    return _dot(x_hi, m) + _dot(x_lo, m)


@jax.custom_vjp
def _dot_const(x, m):
    return _two_pass(x, m)


def _dot_const_fwd(x, m):
    return _two_pass(x, m), m


def _dot_const_bwd(m, ct):
    return _two_pass(ct, m.T), jnp.zeros_like(m)


_dot_const.defvjp(_dot_const_fwd, _dot_const_bwd)


def _two_pass_left(m, x, dims):
    x_hi = x.astype(bf16)
    x_lo = (x - x_hi.astype(f32)).astype(bf16)
    return _dot(m, x_hi, dims) + _dot(m, x_lo, dims)


@jax.custom_vjp
def _const_dot(m, x):
    return _two_pass_left(m, x, "nn")


def _const_dot_fwd(m, x):
    return _two_pass_left(m, x, "nn"), m


def _const_dot_bwd(m, ct):
    return jnp.zeros_like(m), _two_pass_left(m, ct, "tn")


_const_dot.defvjp(_const_dot_fwd, _const_dot_bwd)


def _mm(a, b, mode, *, name, out_dtype=f32, bias=None, add=None, zero_rows_below=0):
    m, _ = a.shape
    n = b.shape[1] if mode == "nn" else b.shape[0]
    tm = _tile(m)
    has_bias, has_add = bias is not None, add is not None

    def body(*refs):
        a_ref, b_ref = refs[0], refs[1]
        o_ref = refs[-1]
        acc = _dot(a_ref[...], b_ref[...], mode)
        k = 2
        if has_bias:
            acc = acc + refs[k][...]
            k += 1
        if zero_rows_below:
            rows = pl.program_id(0) * tm + lax.broadcasted_iota(jnp.int32, acc.shape, 0)
            acc = jnp.where(rows >= zero_rows_below, acc, 0.0)
        if has_add:
            acc = acc + refs[k][...].astype(f32)
        o_ref[...] = acc.astype(out_dtype)

    ins = [a, b]
    in_specs = [pl.BlockSpec((tm, a.shape[1]), lambda i: (i, 0)), _full(b.shape)]
    if has_bias:
        ins.append(bias)
        in_specs.append(_full(bias.shape))
    if has_add:
        ins.append(add)
        in_specs.append(pl.BlockSpec((tm, n), lambda i: (i, 0)))
    return pl.pallas_call(
        body, name=name, grid=(m // tm,), in_specs=in_specs,
        out_specs=pl.BlockSpec((tm, n), lambda i: (i, 0)),
        out_shape=jax.ShapeDtypeStruct((m, n), out_dtype),
        compiler_params=_params(("parallel",)),
    )(*ins)


def _mm_sum(a_list, b_list, *, name):
    m = a_list[0].shape[0]
    n = b_list[0].shape[1]
    k = len(a_list)
    tm = _tile(m)

    def body(*refs):
        acc = _dot(refs[0][...], refs[k][...])
        for i in range(1, k):
            acc = acc + _dot(refs[i][...], refs[k + i][...])
        refs[-1][...] = acc

    return pl.pallas_call(
        body, name=name, grid=(m // tm,),
        in_specs=[pl.BlockSpec((tm, a.shape[1]), lambda i: (i, 0)) for a in a_list] + [_full(b.shape) for b in b_list],
        out_specs=pl.BlockSpec((tm, n), lambda i: (i, 0)),
        out_shape=jax.ShapeDtypeStruct((m, n), f32),
        compiler_params=_params(("parallel",)),
    )(*a_list, *b_list)


def _mm_fanout(a, b_list, bias_list, *, name, zero_rows_below=0):
    m, kdim = a.shape
    k = len(b_list)
    tm = _tile(m)

    def body(*refs):
        av = refs[0][...]
        for j in range(k):
            acc = _dot(av, refs[1 + j][...], "nt") + refs[1 + k + j][...]
            if zero_rows_below:
                rows = pl.program_id(0) * tm + lax.broadcasted_iota(jnp.int32, acc.shape, 0)
                acc = jnp.where(rows >= zero_rows_below, acc, 0.0)
            refs[1 + 2 * k + j][...] = acc

    return pl.pallas_call(
        body, name=name, grid=(m // tm,),
        in_specs=[pl.BlockSpec((tm, kdim), lambda i: (i, 0))] + [_full(b.shape) for b in b_list]
        + [_full(c.shape) for c in bias_list],
        out_specs=[pl.BlockSpec((tm, b.shape[0]), lambda i: (i, 0)) for b in b_list],
        out_shape=[jax.ShapeDtypeStruct((m, b.shape[0]), f32) for b in b_list],
        compiler_params=_params(("parallel",)),
    )(a, *b_list, *bias_list)


def _mm_tn(a, b, *, name, colsum=False, out_dtype=bf16):
    r, m = a.shape
    n = b.shape[1]
    tr = _tile(r, 1408)
    tmo = m
    for cand in (1408, 1024, 768, 512):
        if m > 1024 and m % cand == 0:
            tmo = cand
            break
    steps = r // tr

    def body(a_ref, b_ref, o_ref, *rest):
        acc = rest[-1]
        i = pl.program_id(1)

        @pl.when(i == 0)
        def _():
            acc[...] = jnp.zeros_like(acc)
            if colsum:
                rest[0][...] = jnp.zeros_like(rest[0])

        acc[...] += _dot(a_ref[...], b_ref[...], "tn")
        if colsum:
            rest[0][...] += jnp.sum(a_ref[...].astype(f32), axis=0, keepdims=True)

        @pl.when(i == steps - 1)
        def _():
            o_ref[...] = acc[...].astype(out_dtype)

    out_shape = [jax.ShapeDtypeStruct((m, n), out_dtype)]
    out_specs = [pl.BlockSpec((tmo, n), lambda j, i: (j, 0))]
    if colsum:
        out_shape.append(jax.ShapeDtypeStruct((1, m), f32))
        out_specs.append(pl.BlockSpec((1, tmo), lambda j, i: (0, j)))
    res = pl.pallas_call(
        body, name=name, grid=(m // tmo, steps),
        in_specs=[pl.BlockSpec((tr, tmo), lambda j, i: (i, j)), pl.BlockSpec((tr, n), lambda j, i: (i, 0))],
        out_specs=out_specs, out_shape=out_shape,
        scratch_shapes=[pltpu.VMEM((tmo, n), f32)],
        compiler_params=_params(("parallel", "arbitrary")),
    )(a, b)
    return res if colsum else res[0]


def _rowwise(fn, rows, params, outs, *, name, tm=None):
    m = rows[0].shape[0]
    tm = tm or _tile(m)
    nr, npar = len(rows), len(params)

    def body(*refs):
        vals = [r[...] for r in refs[:nr + npar]]
        res = fn(*vals)
        for o_ref, v in zip(refs[nr + npar:], res):
            o_ref[...] = v.astype(o_ref.dtype)

    return pl.pallas_call(
        body, name=name, grid=(m // tm,),
        in_specs=[pl.BlockSpec((tm, r.shape[1]), lambda i: (i, 0)) for r in rows] + [_full(p.shape) for p in params],
        out_specs=[pl.BlockSpec((tm, w), lambda i: (i, 0)) for w, _ in outs],
        out_shape=[jax.ShapeDtypeStruct((m, w), dt) for w, dt in outs],
        compiler_params=_params(("parallel",)),
    )(*rows, *params)


def _rowwise_bwd(fn, rows, params, cts, *, name, diff_rows, diff_params, tm=None, zero_rows_below=0, out_dtypes=None):
    m = rows[0].shape[0]
    tm = tm or _tile(m)
    nr, npar = len(rows), len(params)
    d_idx = [i for i in range(nr) if diff_rows[i]]
    p_idx = [i for i in range(npar) if diff_params[i]]
    out_dtypes = out_dtypes or [f32] * len(d_idx)
    flat_cts = [c for group in cts for c in group]
    n_ct = len(flat_cts)

    def body(*refs):
        vals = [r[...] for r in refs[:nr + npar]]
        ct_refs = refs[nr + npar:nr + npar + n_ct]
        out_refs = refs[nr + npar + n_ct:]
        ct_vals, k = [], 0
        for group in cts:
            acc = ct_refs[k][...].astype(f32)
            for extra in range(1, len(group)):
                acc = acc + ct_refs[k + extra][...].astype(f32)
            k += len(group)
            if zero_rows_below:
                rr = pl.program_id(0) * tm + lax.broadcasted_iota(jnp.int32, acc.shape, 0)
                acc = jnp.where(rr >= zero_rows_below, acc, 0.0)
            ct_vals.append(acc)

        def g(*dargs):
            full = list(vals)
            for pos, i in enumerate(d_idx):
                full[i] = dargs[pos]
            for pos, i in enumerate(p_idx):
                full[nr + i] = dargs[len(d_idx) + pos]
            return tuple(fn(*full))

        _, vjp = jax.vjp(g, *[vals[i].astype(f32) for i in d_idx], *[vals[nr + i] for i in p_idx])
        grads = vjp(tuple(ct_vals))
        for pos in range(len(d_idx)):
            out_refs[pos][...] = grads[pos].astype(out_refs[pos].dtype)
        first = pl.program_id(0) == 0
        for pos in range(len(p_idx)):
            o_ref = out_refs[len(d_idx) + pos]

            @pl.when(first)
            def _(o_ref=o_ref):
                o_ref[...] = jnp.zeros_like(o_ref)

            o_ref[...] += grads[len(d_idx) + pos]

    return pl.pallas_call(
        body, name=name, grid=(m // tm,),
        in_specs=[pl.BlockSpec((tm, r.shape[1]), lambda i: (i, 0)) for r in rows] + [_full(p.shape) for p in params]
        + [pl.BlockSpec((tm, c.shape[1]), lambda i: (i, 0)) for c in flat_cts],
        out_specs=[pl.BlockSpec((tm, rows[i].shape[1]), lambda i_: (i_, 0)) for i in d_idx]
        + [_full(params[i].shape) for i in p_idx],
        out_shape=[jax.ShapeDtypeStruct(rows[i].shape, dt) for i, dt in zip(d_idx, out_dtypes)]
        + [jax.ShapeDtypeStruct(params[i].shape, f32) for i in p_idx],
        compiler_params=_params(("arbitrary",)),
    )(*rows, *params, *flat_cts)


def _rms(x, g):
    return x * lax.rsqrt(jnp.mean(x * x, axis=-1, keepdims=True) + RMS_EPS) * g


def _head_sum_matrix(width, head):
    idx = jnp.arange(width) // head
    return (idx[:, None] == idx[None, :]).astype(f32)


def _rope_tables(lp):
    half = ROPE_DIM // 2
    pos = (jnp.arange(lp) - PAD).astype(f32)
    inv_freq = jnp.power(jnp.float32(ROPE_THETA), -jnp.arange(half, dtype=f32) * (2.0 / ROPE_DIM))
    ang = pos[:, None] * inv_freq[None, :]
    cos, sin = jnp.cos(ang), jnp.sin(ang)
    ones = jnp.ones((lp, HEAD_DIM - ROPE_DIM), f32)
    zeros = jnp.zeros((lp, HEAD_DIM - ROPE_DIM), f32)
    cos_t = jnp.concatenate([cos, cos, ones], axis=1)
    sin_t = jnp.concatenate([-sin, sin, zeros], axis=1)
    i = jnp.arange(HEAD_DIM)
    src = jnp.where(i < half, i + half, jnp.where(i < ROPE_DIM, i - half, i))
    swap = ((i[:, None] == src[None, :]) & (i[None, :] < ROPE_DIM)).astype(f32)
    return cos_t, sin_t, swap


def _attn_prep(qkv, cos_t, sin_t, swap):
    outs = []
    for h in range(Q_HEADS + KV_HEADS):
        t = qkv[:, h * HEAD_DIM:(h + 1) * HEAD_DIM]
        outs.append(t * cos_t + _dot_const(t, swap) * sin_t)
    q = jnp.concatenate(outs[:Q_HEADS], axis=1)
    k = jnp.concatenate(outs[Q_HEADS:], axis=1)
    return q, k, qkv[:, Q_W + KV_W:]


def _softplus(z):
    return jnp.maximum(z, 0.0) + jnp.log1p(jnp.exp(-jnp.abs(z)))


def _rwkv_prep(rkv, lora, w0, w2, a0, a2, g2, k_k, k_a, hsum):
    r = rkv[:, :RWKV_DIM]
    k = rkv[:, RWKV_DIM:2 * RWKV_DIM]
    v = rkv[:, 2 * RWKV_DIM:]
    dw = lora[:, :DECAY_LORA]
    da = lora[:, DECAY_LORA:DECAY_LORA + AAA_LORA]
    dg = lora[:, DECAY_LORA + AAA_LORA:]
    w = -_softplus(-(w0 + _dot(jnp.tanh(dw), w2))) - 0.5
    a = jax.nn.sigmoid(a0 + _dot(da, a2))
    g = _dot(jax.nn.sigmoid(dg), g2)
    kk = k * k_k
    kk = kk * lax.rsqrt(jnp.maximum(_dot_const(kk * kk, hsum), 1e-24))
    k = k * (1.0 + (a - 1.0) * k_a)
    log_decay = -jnp.exp(w)
    return r, log_decay, k, v, -kk, kk * a, g


def _rwkv_post(y, r, k, v, g, ln_w, ln_b, r_k, hmean):
    hsum = hmean * RWKV_HEAD
    mean = _dot_const(y, hmean)
    yc = y - mean
    var = _dot_const(yc * yc, hmean)
    yn = yc * lax.rsqrt(var + RWKV_LN_EPS) * ln_w + ln_b
    bonus = _dot_const(r * k * r_k, hsum) * v
    return ((yn + bonus) * g,)


def _merge(gates, br_a, br_r):
    sg = jax.nn.sigmoid(gates)
    return (sg[:, :D_MODEL] * br_a + sg[:, D_MODEL:] * br_r,)


def _swiglu(gate, up):
    return (jax.nn.silu(gate) * up,)


def _ffn_in(f, w_gate_t, w_up_t, *, name):
    m, d = f.shape
    n = w_gate_t.shape[0]
    tm = _tile(m)

    def body(f_ref, wg_ref, wu_ref, g_ref, u_ref, a_ref):
        g = _dot(f_ref[...], wg_ref[...], "nt")
        u = _dot(f_ref[...], wu_ref[...], "nt")
        g_ref[...] = g.astype(g_ref.dtype)
        u_ref[...] = u.astype(u_ref.dtype)
        a_ref[...] = _swiglu(g, u)[0].astype(a_ref.dtype)

    spec = pl.BlockSpec((tm, n), lambda i: (i, 0))
    return pl.pallas_call(
        body, name=name, grid=(m // tm,),
        in_specs=[pl.BlockSpec((tm, d), lambda i: (i, 0)), _full(w_gate_t.shape), _full(w_up_t.shape)],
        out_specs=[spec] * 3, out_shape=[jax.ShapeDtypeStruct((m, n), bf16)] * 3,
        compiler_params=_params(("parallel",)),
    )(f, w_gate_t, w_up_t)


def _branch_merge(y_attn, y_rwkv, w_attn_t, w_rwkv_t, gates, *, name):
    m = y_attn.shape[0]
    tm = _tile(m)

    def body(ya_ref, yr_ref, wa_ref, wr_ref, g_ref, a_ref, r_ref, o_ref):
        br_a = _dot(ya_ref[...], wa_ref[...], "nt")
        br_r = _dot(yr_ref[...], wr_ref[...], "nt")
        a_ref[...] = br_a.astype(a_ref.dtype)
        r_ref[...] = br_r.astype(r_ref.dtype)
        o_ref[...] = _merge(g_ref[...], br_a, br_r)[0].astype(o_ref.dtype)

    rows = lambda a: pl.BlockSpec((tm, a.shape[1]), lambda i: (i, 0))
    spec = pl.BlockSpec((tm, D_MODEL), lambda i: (i, 0))
    return pl.pallas_call(
        body, name=name, grid=(m // tm,),
        in_specs=[rows(y_attn), rows(y_rwkv), _full(w_attn_t.shape), _full(w_rwkv_t.shape), rows(gates)],
        out_specs=[spec] * 3, out_shape=[jax.ShapeDtypeStruct((m, D_MODEL), bf16)] * 3,
        compiler_params=_params(("parallel",)),
    )(y_attn, y_rwkv, w_attn_t, w_rwkv_t, gates)


def _branch_merge_bwd(dh, w_o, gates, br_a, br_r, *, name):
    m = dh.shape[0]
    tm = _tile(m)

    def body(dh_ref, w_ref, g_ref, a_ref, r_ref, dg_ref, da_ref, dr_ref):
        dmerged = _dot(dh_ref[...], w_ref[...], "nt")
        _, vjp = jax.vjp(lambda g, a, r: _merge(g, a, r)[0], g_ref[...], a_ref[...].astype(f32),
                         r_ref[...].astype(f32))
        dg, da, dr = vjp(dmerged)
        dg_ref[...] = dg.astype(dg_ref.dtype)
        da_ref[...] = da.astype(da_ref.dtype)
        dr_ref[...] = dr.astype(dr_ref.dtype)

    rows = lambda a: pl.BlockSpec((tm, a.shape[1]), lambda i: (i, 0))
    return pl.pallas_call(
        body, name=name, grid=(m // tm,),
        in_specs=[rows(dh), _full(w_o.shape), rows(gates), rows(br_a), rows(br_r)],
        out_specs=[rows(gates), rows(br_a), rows(br_r)],
        out_shape=[jax.ShapeDtypeStruct(gates.shape, bf16), jax.ShapeDtypeStruct(br_a.shape, bf16),
                   jax.ShapeDtypeStruct(br_r.shape, bf16)],
        compiler_params=_params(("parallel",)),
    )(dh, w_o, gates, br_a, br_r)


def _ffn_in_bwd(dh, w_down, gate, up, *, name):
    m, d = dh.shape
    n = w_down.shape[0]
    tm = _tile(m)

    def body(dh_ref, w_ref, g_ref, u_ref, dg_ref, du_ref):
        dact = _dot(dh_ref[...], w_ref[...], "nt")
        _, vjp = jax.vjp(lambda a, b: _swiglu(a, b)[0], g_ref[...].astype(f32), u_ref[...].astype(f32))
        dg, du = vjp(dact)
        dg_ref[...] = dg.astype(dg_ref.dtype)
        du_ref[...] = du.astype(du_ref.dtype)

    spec = pl.BlockSpec((tm, n), lambda i: (i, 0))
    return pl.pallas_call(
        body, name=name, grid=(m // tm,),
        in_specs=[pl.BlockSpec((tm, d), lambda i: (i, 0)), _full(w_down.shape), spec, spec],
        out_specs=[spec] * 2, out_shape=[jax.ShapeDtypeStruct((m, n), bf16)] * 2,
        compiler_params=_params(("parallel",)),
    )(dh, w_down, gate, up)


def _previous_rows(x, before_ref, first_tile):
    rows = lax.broadcasted_iota(jnp.int32, x.shape, 0)
    last = jnp.where(first_tile, 0.0, before_ref[7:8, :])
    return jnp.where(rows == 0, last, pltpu.roll(x, 1, axis=0))


def _mixer_inputs(ps, mixes, params, *, name):
    m = ps[0].shape[0]
    tm = _tile(m)
    sub = tm // 8
    n_par = len(params)

    def body(*refs):
        first = pl.program_id(0) == 0
        pf = []
        for k in range(2):
            x = refs[k][...]
            pf.append(x + (_previous_rows(x, refs[2 + k], first) - x) * refs[4 + k][...])
        res = _rwkv_prep(*pf, *[ref[...] for ref in refs[6:6 + n_par]])
        for o_ref, val in zip(refs[6 + n_par:], res):
            o_ref[...] = val

    tile = lambda a: pl.BlockSpec((tm, a.shape[1]), lambda i: (i, 0))
    before = lambda a: pl.BlockSpec((8, a.shape[1]), lambda i: (jnp.maximum(i * sub - 1, 0), 0))
    out = pl.BlockSpec((tm, RWKV_DIM), lambda i: (i, 0))
    return pl.pallas_call(
        body, name=name, grid=(m // tm,),
        in_specs=[tile(a) for a in ps] + [before(a) for a in ps] + [_full(a.shape) for a in mixes + params],
        out_specs=[out] * 7, out_shape=[jax.ShapeDtypeStruct((m, RWKV_DIM), f32)] * 7,
        compiler_params=_params(("parallel",)),
    )(*ps, *ps, *mixes, *params)


def _mixer_inputs_bwd(ps, mixes, params, cts, *, name):
    m = ps[0].shape[0]
    tm = _tile(m)
    sub = tm // 8
    nt = m // tm
    n_par = len(params)
    flat_cts = [c for group in cts for c in group]
    n_ct = len(flat_cts)

    def body(*refs):
        i = pl.program_id(0)
        tile_index = nt - 1 - i
        ct_refs = refs[6 + n_par:6 + n_par + n_ct]
        dp_refs = refs[6 + n_par + n_ct:8 + n_par + n_ct]
        dmix_refs = refs[8 + n_par + n_ct:10 + n_par + n_ct]
        dpar_refs = refs[10 + n_par + n_ct:9 + 2 * n_par + n_ct]
        carries = refs[9 + 2 * n_par + n_ct:]
        rows1 = tile_index * tm + lax.broadcasted_iota(jnp.int32, (tm, 1), 0)
        live = rows1 >= PAD

        @pl.when(i == 0)
        def _():
            for ref in (*dmix_refs, *dpar_refs, *carries):
                ref[...] = jnp.zeros_like(ref)

        xs, prevs, pf = [], [], []
        for k in range(2):
            x = refs[k][...]
            xp = _previous_rows(x, refs[2 + k], tile_index == 0)
            xs.append(x)
            prevs.append(xp)
            pf.append(x + (xp - x) * refs[4 + k][...])
        ct_vals, pos = [], 0
        for group in cts:
            acc = ct_refs[pos][...].astype(f32)
            for extra in range(1, len(group)):
                acc = acc + ct_refs[pos + extra][...].astype(f32)
            pos += len(group)
            ct_vals.append(jnp.where(live, acc, 0.0))
        par_vals = [ref[...] for ref in refs[6:6 + n_par]]
        _, vjp = jax.vjp(lambda *args: _rwkv_prep(*args, par_vals[-1]), *pf, *par_vals[:-1])
        g = vjp(tuple(ct_vals))
        for k in range(2):
            dpf = g[k]
            mixv = refs[4 + k][...]
            dm = dpf * mixv
            rows = lax.broadcasted_iota(jnp.int32, dm.shape, 0)
            dm_next = jnp.where(rows == tm - 1, carries[k][...], pltpu.roll(dm, tm - 1, axis=0))
            dp_refs[k][...] = jnp.where(live, dpf - dm + dm_next, 0.0).astype(dp_refs[k].dtype)
            carries[k][...] = dm[0:1, :]
            dmix_refs[k][...] += jnp.sum(dpf * (prevs[k] - xs[k]), axis=0, keepdims=True)
        for ref, val in zip(dpar_refs, g[2:]):
            ref[...] += val

    tile = lambda a: pl.BlockSpec((tm, a.shape[1]), lambda i: (nt - 1 - i, 0))
    before = lambda a: pl.BlockSpec((8, a.shape[1]), lambda i: (jnp.maximum((nt - 1 - i) * sub - 1, 0), 0))
    return pl.pallas_call(
        body, name=name, grid=(nt,),
        in_specs=[tile(a) for a in ps] + [before(a) for a in ps] + [_full(a.shape) for a in mixes + params]
        + [tile(c) for c in flat_cts],
        out_specs=[tile(a) for a in ps] + [_full(a.shape) for a in mixes + params[:-1]],
        out_shape=[jax.ShapeDtypeStruct(a.shape, bf16) for a in ps]
        + [jax.ShapeDtypeStruct(a.shape, f32) for a in mixes + params[:-1]],
        scratch_shapes=[pltpu.VMEM((1, a.shape[1]), f32) for a in ps],
        compiler_params=_params(("arbitrary",)),
    )(*ps, *ps, *mixes, *params, *flat_cts)


def _attn_masks(blk):
    qi = lax.broadcasted_iota(jnp.int32, (BLOCK, BLOCK), 0)
    ki = lax.broadcasted_iota(jnp.int32, (BLOCK, BLOCK), 1)
    qpos = blk * BLOCK + qi - PAD
    kpos_c = blk * BLOCK + ki - PAD
    kpos_p = kpos_c - BLOCK
    kpos_m = ki - PAD

    def band(kpos):
        return (kpos >= N_META) & (kpos <= qpos) & (qpos - kpos < WINDOW)

    return band(kpos_p), band(kpos_c), (kpos_m >= 0) & (kpos_m <= qpos)


def _attn_probs(qs, k3s, sink, oks):
    s = [[jnp.where(ok, _dot(qh, kx, "nt"), NEG_INF) for kx, ok in zip(k3, oks)] for qh, k3 in zip(qs, k3s)]
    mx = [jnp.maximum(jnp.maximum(jnp.max(t[0], -1, keepdims=True), jnp.max(t[1], -1, keepdims=True)),
                      jnp.maximum(jnp.max(t[2], -1, keepdims=True), sk)) for t, sk in zip(s, sink)]
    e = [[jnp.exp(tx - m) for tx in t] for t, m in zip(s, mx)]
    e_sink = [jnp.exp(sk - m) for sk, m in zip(sink, mx)]
    inv = [1.0 / (jnp.sum(t[0], -1, keepdims=True) + jnp.sum(t[1], -1, keepdims=True)
                  + jnp.sum(t[2], -1, keepdims=True) + es) for t, es in zip(e, e_sink)]
    return [[tx * i for tx in t] for t, i in zip(e, inv)], [es * i for es, i in zip(e_sink, inv)]


def _head_cols(i):
    return slice(i * HEAD_DIM, (i + 1) * HEAD_DIM)


def _attn_operands(refs):
    q_ref, kp_ref, kc_ref, km_ref, vp_ref, vc_ref, vm_ref, s_ref = refs
    qs = [q_ref[:, _head_cols(i)] * (HEAD_DIM ** -0.5) for i in range(Q_HEADS)]
    k3 = [[ref[:, _head_cols(h)] for ref in (kp_ref, kc_ref, km_ref)] for h in range(KV_HEADS)]
    v3 = [[ref[:, _head_cols(h)] for ref in (vp_ref, vc_ref, vm_ref)] for h in range(KV_HEADS)]
    return (qs, [k3[i // GROUP] for i in range(Q_HEADS)], [v3[i // GROUP] for i in range(Q_HEADS)],
            [s_ref[:, i:i + 1] for i in range(Q_HEADS)])


def _attention(q, k, v, sinks, *, name):
    lp = q.shape[0]
    nb = lp // BLOCK
    prev = lambda i: (jnp.maximum(i - 1, 0), 0)
    cur = lambda i: (i, 0)
    meta = lambda i: (0, 0)
    kv = lambda index: pl.BlockSpec((BLOCK, KV_W), index)

    def body(*refs):
        o_ref = refs[-1]
        qs, k3s, v3s, sink = _attn_operands(refs[:-1])
        p, _ = _attn_probs(qs, k3s, sink, _attn_masks(pl.program_id(0)))
        out = [_dot(ph[0], v3[0]) + _dot(ph[1], v3[1]) + _dot(ph[2], v3[2]) for ph, v3 in zip(p, v3s)]
        for i in range(Q_HEADS):
            o_ref[:, _head_cols(i)] = out[i].astype(o_ref.dtype)

    return pl.pallas_call(
        body, name=name, grid=(nb,),
        in_specs=[pl.BlockSpec((BLOCK, Q_W), cur), kv(prev), kv(cur), kv(meta), kv(prev), kv(cur), kv(meta),
                  _full((1, Q_HEADS))],
        out_specs=pl.BlockSpec((BLOCK, Q_W), cur),
        out_shape=jax.ShapeDtypeStruct((lp, Q_W), bf16),
        compiler_params=_params(("parallel",)),
    )(q, k, k, k, v, v, v, sinks)


def _attention_bwd(q, k, v, sinks, do, *, name):
    lp = q.shape[0]
    nb = lp // BLOCK
    cur = lambda n: (jnp.minimum(n, nb - 1), 0)
    prev = lambda n: (jnp.maximum(jnp.minimum(n, nb - 1) - 1, 0), 0)
    behind = lambda n: (jnp.maximum(n - 1, 0), 0)
    meta = lambda n: (0, 0)
    kv = lambda index: pl.BlockSpec((BLOCK, KV_W), index)
    scale = HEAD_DIM ** -0.5

    def body(*refs):
        ins, do_ref = refs[:8], refs[8]
        dq_ref, dk_ref, dv_ref, dkm_ref, dvm_ref, ds_ref, carry_k, carry_v = refs[9:]
        n = pl.program_id(0)

        @pl.when(n == 0)
        def _():
            for ref in (dkm_ref, dvm_ref, ds_ref, carry_k, carry_v):
                ref[...] = jnp.zeros_like(ref)

        @pl.when(n < nb)
        def _():
            qs, k3s, v3s, sink = _attn_operands(ins)
            do = [do_ref[:, _head_cols(i)] for i in range(Q_HEADS)]
            p, p_sink = _attn_probs(qs, k3s, sink, _attn_masks(n))
            out = [_dot(ph[0], v3[0]) + _dot(ph[1], v3[1]) + _dot(ph[2], v3[2]) for ph, v3 in zip(p, v3s)]
            delta = [jnp.sum(d * o, -1, keepdims=True) for d, o in zip(do, out)]
            dp = [[_dot(d, vx, "nt") for vx in v3] for d, v3 in zip(do, v3s)]
            ds = [[px * (dx - dl) for px, dx in zip(ph, dh)] for ph, dh, dl in zip(p, dp, delta)]
            dq = [_dot(dsh[0], k3[0]) + _dot(dsh[1], k3[1]) + _dot(dsh[2], k3[2]) for dsh, k3 in zip(ds, k3s)]
            for i in range(Q_HEADS):
                dq_ref[:, _head_cols(i)] = dq[i] * scale
                ds_ref[:, i:i + 1] -= jnp.sum(p_sink[i] * delta[i], axis=0, keepdims=True)
            for h in range(KV_HEADS):
                group = slice(h * GROUP, (h + 1) * GROUP)
                q_all = jnp.concatenate(qs[group], axis=0)
                do_all = jnp.concatenate(do[group], axis=0)
                dk3 = [_dot(jnp.concatenate([dsh[x] for dsh in ds[group]], axis=0), q_all, "tn") for x in range(3)]
                dv3 = [_dot(jnp.concatenate([ph[x] for ph in p[group]], axis=0), do_all, "tn") for x in range(3)]
                hs = _head_cols(h)
                for out_ref, carry, meta_ref, d3 in ((dk_ref, carry_k, dkm_ref, dk3),
                                                     (dv_ref, carry_v, dvm_ref, dv3)):
                    out_ref[:, hs] = carry[:, hs] + d3[0]
                    carry[:, hs] = d3[1]
                    meta_ref[:, hs] += d3[2]

        @pl.when(n == nb)
        def _():
            dk_ref[...] = carry_k[...]
            dv_ref[...] = carry_v[...]

    kv_shape = jax.ShapeDtypeStruct((lp, KV_W), f32)
    one_shape = jax.ShapeDtypeStruct((BLOCK, KV_W), f32)
    return pl.pallas_call(
        body, name=name, grid=(nb + 1,),
        in_specs=[pl.BlockSpec((BLOCK, Q_W), cur), kv(prev), kv(cur), kv(meta), kv(prev), kv(cur), kv(meta),
                  _full((1, Q_HEADS)), pl.BlockSpec((BLOCK, Q_W), cur)],
        out_specs=[pl.BlockSpec((BLOCK, Q_W), cur), kv(behind), kv(behind), kv(meta), kv(meta),
                   _full((1, Q_HEADS))],
        out_shape=[jax.ShapeDtypeStruct((lp, Q_W), f32), kv_shape, kv_shape, one_shape, one_shape,
                   jax.ShapeDtypeStruct((1, Q_HEADS), f32)],
        scratch_shapes=[pltpu.VMEM((BLOCK, KV_W), f32), pltpu.VMEM((BLOCK, KV_W), f32)],
        compiler_params=_params(("arbitrary",)),
    )(q, k, k, k, v, v, v, sinks, do)


@jax.custom_vjp
def _known_inverse(l, x):
    return x


def _known_inverse_fwd(l, x):
    return x, x


def _known_inverse_bwd(x, ct):
    return _dot(_dot(x, ct, "tn"), x, "nt"), jnp.zeros_like(x)


_known_inverse.defvjp(_known_inverse_fwd, _known_inverse_bwd)


def _scan_chunk(s0, r, lw, k, v, a, b, inv=None):
    t = r[0].shape[0]
    ii = lax.broadcasted_iota(jnp.int32, (t, t), 0)
    jj = lax.broadcasted_iota(jnp.int32, (t, t), 1)
    incl = jj <= ii
    strict = jj < ii
    tri = incl.astype(f32)
    eye = jnp.where(ii == jj, 1.0, 0.0)
    cl = [_const_dot(tri, x) for x in lw]
    e_pos = [jnp.exp(c) for c in cl]
    e_neg = [jnp.exp(-c) for c in cl]
    e_prev = [jnp.exp(c - x) for c, x in zip(cl, lw)]
    rt = [x * e for x, e in zip(r, e_pos)]
    at = [x * e for x, e in zip(a, e_prev)]
    bt = [x * e for x, e in zip(b, e_neg)]
    kt = [x * e for x, e in zip(k, e_neg)]
    l_ab = [jnp.where(strict, _dot(x, y, "nt"), 0.0) for x, y in zip(at, bt)]
    l_ak = [jnp.where(strict, _dot(x, y, "nt"), 0.0) for x, y in zip(at, kt)]
    r_b = [jnp.where(incl, _dot(x, y, "nt"), 0.0) for x, y in zip(rt, bt)]
    r_k = [jnp.where(incl, _dot(x, y, "nt"), 0.0) for x, y in zip(rt, kt)]
    if inv is None:
        inv = [eye + x for x in l_ab]
        pw = l_ab
        for _ in range(int(math.log2(t)) - 1):
            pw = [_dot(x, x) for x in pw]
            inv = [x + _dot(x, y) for x, y in zip(inv, pw)]
    else:
        inv = [_known_inverse(x, y) for x, y in zip(l_ab, inv)]
    rhs = [_dot(x, s, "nt") + _dot(m, y) for x, s, m, y in zip(at, s0, l_ak, v)]
    u = [_dot(x, y) for x, y in zip(inv, rhs)]
    y_s = [_dot(x, s, "nt") for x, s in zip(rt, s0)]
    y = [ys + _dot(m, uu) + _dot(n, vv) for ys, m, uu, n, vv in zip(y_s, r_b, u, r_k, v)]
    grow = [s + _dot(uu, x, "tn") + _dot(vv, z, "tn") for s, uu, x, vv, z in zip(s0, u, bt, v, kt)]
    s1 = [g * e[t - 1:t, :] for g, e in zip(grow, e_pos)]
    return y, s1, inv


def _head_rows(h):
    return slice(h * RWKV_HEAD, (h + 1) * RWKV_HEAD)


def _per_head(ref):
    return [ref[:, _head_rows(h)] for h in range(RWKV_HEADS)]


def _scan(r, lw, k, v, a, b, *, name):
    lp = r.shape[0]
    nc = lp // CHUNK
    row = pl.BlockSpec((CHUNK, RWKV_DIM), lambda c: (c, 0))

    def body(r_ref, lw_ref, k_ref, v_ref, a_ref, b_ref, y_ref, s_ref, inv_ref, state):
        @pl.when(pl.program_id(0) == 0)
        def _():
            state[...] = jnp.zeros_like(state)

        s_ref[...] = state[...]
        s0 = [state[_head_rows(h), :] for h in range(RWKV_HEADS)]
        y, s1, inv = _scan_chunk(s0, *[_per_head(ref) for ref in (r_ref, lw_ref, k_ref, v_ref, a_ref, b_ref)])
        for h in range(RWKV_HEADS):
            y_ref[:, _head_rows(h)] = y[h]
            state[_head_rows(h), :] = s1[h]
            inv_ref[h * CHUNK:(h + 1) * CHUNK, :] = inv[h].astype(inv_ref.dtype)

    return pl.pallas_call(
        body, name=name, grid=(nc,), in_specs=[row] * 6,
        out_specs=[row, pl.BlockSpec((RWKV_DIM, RWKV_HEAD), lambda c: (c, 0)),
                   pl.BlockSpec((RWKV_HEADS * CHUNK, CHUNK), lambda c: (c, 0))],
        out_shape=[jax.ShapeDtypeStruct((lp, RWKV_DIM), f32), jax.ShapeDtypeStruct((nc * RWKV_DIM, RWKV_HEAD), f32),
                   jax.ShapeDtypeStruct((nc * RWKV_HEADS * CHUNK, CHUNK), bf16)],
        scratch_shapes=[pltpu.VMEM((RWKV_DIM, RWKV_HEAD), f32)],
        compiler_params=_params(("arbitrary",)),
    )(r, lw, k, v, a, b)


def _scan_bwd(r, lw, k, v, a, b, states, inverses, dy, *, name):
    lp = r.shape[0]
    nc = lp // CHUNK
    back = lambda c: (nc - 1 - c, 0)
    row = pl.BlockSpec((CHUNK, RWKV_DIM), back)

    def body(r_ref, lw_ref, k_ref, v_ref, a_ref, b_ref, s_ref, inv_ref, dy_ref,
             dr_ref, dlw_ref, dk_ref, dv_ref, da_ref, db_ref, dstate):
        @pl.when(pl.program_id(0) == 0)
        def _():
            dstate[...] = jnp.zeros_like(dstate)

        outs = (dr_ref, dlw_ref, dk_ref, dv_ref, da_ref, db_ref)
        s0 = [s_ref[_head_rows(h), :] for h in range(RWKV_HEADS)]
        inv = [inv_ref[h * CHUNK:(h + 1) * CHUNK, :].astype(f32) for h in range(RWKV_HEADS)]
        _, vjp = jax.vjp(lambda *args: _scan_chunk(*args, inv=inv)[:2], s0,
                         *[_per_head(ref) for ref in (r_ref, lw_ref, k_ref, v_ref, a_ref, b_ref)])
        g = vjp((_per_head(dy_ref), [dstate[_head_rows(h), :] for h in range(RWKV_HEADS)]))
        for h in range(RWKV_HEADS):
            dstate[_head_rows(h), :] = g[0][h]
            for o_ref, gv in zip(outs, g[1:]):
                o_ref[:, _head_rows(h)] = gv[h]

    shape = jax.ShapeDtypeStruct((lp, RWKV_DIM), f32)
    return pl.pallas_call(
        body, name=name, grid=(nc,),
        in_specs=[row] * 6 + [pl.BlockSpec((RWKV_DIM, RWKV_HEAD), back),
                              pl.BlockSpec((RWKV_HEADS * CHUNK, CHUNK), back), row],
        out_specs=[row] * 6, out_shape=[shape] * 6,
        scratch_shapes=[pltpu.VMEM((RWKV_DIM, RWKV_HEAD), f32)],
        compiler_params=_params(("arbitrary",)),
    )(r, lw, k, v, a, b, states, inverses, dy)


def _loss_head(h2, target, g_final, *, name):
    lp = h2.shape[0]
    tm = BLOCK
    front_tiles = FRONT // tm

    def body(h_ref, t_ref, g_ref, loss_ref, dh_ref, dg_ref):
        i = pl.program_id(0)
        real = i >= front_tiles

        def tile_loss(hv, gv):
            err = _rms(hv, gv) - t_ref[...]
            return jnp.where(real, 0.5 * jnp.sum(jnp.mean(err * err, axis=-1, keepdims=True)), 0.0)

        loss, (dh, dg) = jax.value_and_grad(tile_loss, argnums=(0, 1))(h_ref[...], g_ref[...])

        @pl.when(i == 0)
        def _():
            loss_ref[...] = jnp.zeros_like(loss_ref)
            dg_ref[...] = jnp.zeros_like(dg_ref)

        loss_ref[...] += jnp.full(loss_ref.shape, loss, f32)
        dg_ref[...] += dg
        dh_ref[...] = dh

    return pl.pallas_call(
        body, name=name, grid=(lp // tm,),
        in_specs=[pl.BlockSpec((tm, D_MODEL), lambda i: (i, 0)),
                  pl.BlockSpec((tm, D_MODEL), lambda i: (jnp.maximum(i - front_tiles, 0), 0)),
                  _full(g_final.shape)],
        out_specs=[_full((8, 128)), pl.BlockSpec((tm, D_MODEL), lambda i: (i, 0)), _full(g_final.shape)],
        out_shape=[jax.ShapeDtypeStruct((8, 128), f32), jax.ShapeDtypeStruct((lp, D_MODEL), f32),
                   jax.ShapeDtypeStruct(g_final.shape, f32)],
        compiler_params=_params(("arbitrary",)),
    )(h2, target, g_final)


def _local_step(x, target, meta, p, early_weights=None, late_weights=None, emit=None):
    emit = emit or (lambda group, grads: 0.0)
    seq = x.shape[0]
    lp = seq + FRONT
    h0 = jnp.concatenate([jnp.zeros((PAD, D_MODEL), f32), meta, x], axis=0)
    cos_t, sin_t, swap = _rope_tables(lp)
    hsum = _head_sum_matrix(RWKV_DIM, RWKV_HEAD)
    hmean = hsum / RWKV_HEAD
    b_qkv, b_rkv = p["b_in"][:, :ATTN_PROJ], p["b_in"][:, ATTN_PROJ:ATTN_PROJ + RKV_W]
    b_lora, b_gates = p["b_in"][:, ATTN_PROJ + RKV_W:ATTN_PROJ + RWKV_PROJ], p["b_in"][:, ATTN_PROJ + RWKV_PROJ:]
    post_params = [p["ln_w"], p["ln_b"], p["r_k"], hmean]

    (u,) = _rowwise(lambda hv, g: (_rms(hv, g),), [h0], [p["norm_mix_g"]], [(D_MODEL, bf16)], name="norm_mix")
    if early_weights is not None:
        p = {**p, **early_weights(u)}
    w_qkv_t, w_rkv_t = p["w_in_t"][:ATTN_PROJ], p["w_in_t"][ATTN_PROJ:ATTN_PROJ + RKV_W]
    w_lora_t, w_gates_t = p["w_in_t"][ATTN_PROJ + RKV_W:ATTN_PROJ + RWKV_PROJ], p["w_in_t"][ATTN_PROJ + RWKV_PROJ:]
    prep_params = [p["w0"], p["w2"], p["a0"], p["a2"], p["g2"], p["k_k"], p["k_a"], hsum]
    w_pieces = [w_qkv_t, w_rkv_t, w_lora_t, w_gates_t]
    qkv, p_rkv, p_lora, gates = _mm_fanout(u, w_pieces, [b_qkv, b_rkv, b_lora, b_gates], name="proj_in",
                                           zero_rows_below=PAD)

    q, k, v = _rowwise(_attn_prep, [qkv, cos_t, sin_t], [swap], [(Q_W, bf16), (KV_W, bf16), (KV_W, bf16)],
                       name="attn_prep")
    y_attn = _attention(q, k, v, p["sinks"], name="attention")

    mix_rkv, mix_lora = p["mix"][:, :RKV_W], p["mix"][:, RKV_W:]
    r_, lw_, k_, v_, a_, b_, g_ = _mixer_inputs([p_rkv, p_lora], [mix_rkv, mix_lora], prep_params,
                                                name="mixer_inputs")
    y_scan, states, inverses = _scan(r_, lw_, k_, v_, a_, b_, name="wkv_scan")
    (y_rwkv,) = _rowwise(_rwkv_post, [y_scan, r_, k_, v_, g_], post_params, [(RWKV_DIM, bf16)], name="rwkv_post")

    if late_weights is not None:
        p = {**p, **late_weights(y_rwkv)}
    br_a, br_r, merged = _branch_merge(y_attn, y_rwkv, p["w_br_attn_t"], p["w_br_rwkv_t"], gates, name="branch_merge")
    h1 = _mm(merged, p["w_o"], "nn", name="out_proj", add=h0)
    (f,) = _rowwise(lambda hv, g: (_rms(hv, g),), [h1], [p["norm_ffn_g"]], [(D_MODEL, bf16)], name="norm_ffn")
    gate, up, act = _ffn_in(f, p["w_gate_t"], p["w_up_t"], name="ffn_in")
    h2 = _mm(act, p["w_down"], "nn", name="ffn_down", add=h1)

    loss8, dh2, d_final_g = _loss_head(h2, target, p["norm_final_g"], name="loss_head")
    dgate, dup = _ffn_in_bwd(dh2, p["w_down"], gate, up, name="ffn_in_bwd")
    d_w_down = _mm_tn(act, dh2, name="dw_down")
    d_w_gate_t = _mm_tn(dgate, f, name="dw_gate")
    d_w_up_t = _mm_tn(dup, f, name="dw_up")
    zero = emit("ffn", dict(w_down=d_w_down, w_gate_t=d_w_gate_t, w_up_t=d_w_up_t))
    df = _mm_sum([dgate, dup], [p["w_gate_t"], p["w_up_t"]], name="d_f")
    dh1, d_ffn_g = _rowwise_bwd(lambda hv, g: (_rms(hv, g), hv), [h1], [p["norm_ffn_g"] + zero], [[df], [dh2]],
                                name="norm_ffn_bwd", diff_rows=[True], diff_params=[True])
    dgates, dbr_a, dbr_r = _branch_merge_bwd(dh1, p["w_o"], gates, br_a, br_r, name="branch_merge_bwd")
    d_w_o = _mm_tn(merged, dh1, name="dw_o")
    d_w_br_attn_t = _mm_tn(dbr_a, y_attn, name="dw_br_attn")
    d_w_br_rwkv_t = _mm_tn(dbr_r, y_rwkv, name="dw_br_rwkv")
    zero = emit("branch", dict(w_o=d_w_o, w_br_attn_t=d_w_br_attn_t, w_br_rwkv_t=d_w_br_rwkv_t))
    dy_attn = _mm(dbr_a, p["w_br_attn_t"], "nn", name="d_y_attn")
    dy_rwkv = _mm(dbr_r, p["w_br_rwkv_t"], "nn", name="d_y_rwkv")

    post_params = [p["ln_w"] + zero, p["ln_b"], p["r_k"], hmean]
    res = _rowwise_bwd(_rwkv_post, [y_scan, r_, k_, v_, g_], post_params, [[dy_rwkv]], name="rwkv_post_bwd",
                       diff_rows=[True] * 5, diff_params=[True, True, True, False])
    dy_scan, dr_p, dk_p, dv_p, dg_p, d_ln_w, d_ln_b, d_r_k = res
    dr_s, dlw_s, dk_s, dv_s, da_s, db_s = _scan_bwd(r_, lw_, k_, v_, a_, b_, states, inverses, dy_scan,
                                                    name="wkv_scan_bwd")
    res = _mixer_inputs_bwd([p_rkv, p_lora], [mix_rkv, mix_lora], prep_params,
                            [[dr_s, dr_p], [dlw_s], [dk_s, dk_p], [dv_s, dv_p], [da_s], [db_s], [dg_p]],
                            name="mixer_inputs_bwd")
    dp_rkv, dp_lora, d_mix_rkv, d_mix_lora, d_w0, d_w2, d_a0, d_a2, d_g2, d_k_k, d_k_a = res

    dq, dk, dv, dkm, dvm, d_sinks = _attention_bwd(q, k, v, p["sinks"], dy_attn, name="attention_bwd")
    rest = jnp.zeros((lp - BLOCK, KV_W), f32)
    dkm, dvm = jnp.concatenate([dkm, rest], axis=0), jnp.concatenate([dvm, rest], axis=0)
    (dqkv,) = _rowwise_bwd(_attn_prep, [qkv, cos_t, sin_t], [swap], [[dq], [dk, dkm], [dv, dvm]], name="attn_prep_bwd",
                           diff_rows=[True, False, False], diff_params=[False], out_dtypes=[bf16])

    d_w_qkv_t, db_qkv = _mm_tn(dqkv, u, name="dw_qkv", colsum=True)
    d_w_rkv_t, db_rkv = _mm_tn(dp_rkv, u, name="dw_rkv", colsum=True)
    d_w_lora_t, db_lora = _mm_tn(dp_lora, u, name="dw_lora", colsum=True)
    d_w_gates_t, db_gates = _mm_tn(dgates, u, name="dw_gates", colsum=True)
    d_w_in_t = jnp.concatenate([d_w_qkv_t, d_w_rkv_t, d_w_lora_t, d_w_gates_t], axis=0)
    zero = emit("input", dict(w_in_t=d_w_in_t, g2=d_g2, w2=d_w2, a2=d_a2))
    du = _mm_sum([dqkv, dp_rkv, dp_lora, dgates], w_pieces, name="d_u")
    dh0, d_mix_g = _rowwise_bwd(lambda hv, g: (_rms(hv, g), hv), [h0], [p["norm_mix_g"] + zero], [[du], [dh1]],
                                name="norm_mix_bwd", diff_rows=[True], diff_params=[True])

    grads = dict(
        w_in_t=d_w_in_t,
        b_in=jnp.concatenate([db_qkv, db_rkv, db_lora, db_gates], axis=1),
        mix=jnp.concatenate([d_mix_rkv, d_mix_lora], axis=1),
        norm_mix_g=d_mix_g, sinks=d_sinks, w0=d_w0, w2=d_w2, a0=d_a0, a2=d_a2, g2=d_g2, k_k=d_k_k, k_a=d_k_a,
        r_k=d_r_k, ln_w=d_ln_w, ln_b=d_ln_b, w_br_attn_t=d_w_br_attn_t, w_br_rwkv_t=d_w_br_rwkv_t, w_o=d_w_o,
        norm_ffn_g=d_ffn_g, w_gate_t=d_w_gate_t, w_up_t=d_w_up_t, w_down=d_w_down, norm_final_g=d_final_g,
        meta=dh0[PAD:FRONT],
    )
    return loss8[0, 0], dh0[FRONT:], grads


def _position():
    return lax.axis_index("x"), lax.axis_index("y"), lax.axis_index("c")


def _other_chips(x, y):
    return [(1 - x, y), (x, 1 - y), (1 - x, 1 - y)]


_HBM = pl.BlockSpec(memory_space=pltpu.HBM)
_SEM = pl.BlockSpec(memory_space=pltpu.SEMAPHORE)
_EFFECT = pltpu.SideEffectType.DATAFLOW_SIDE_EFFECTING


def _landing_zone(src, kind):
    shape = {"whole": (N_CHIPS,) + src.shape, "half": (2, N_CHIPS, src.shape[0], src.shape[1] // 2),
             "slab": (3,) + src.shape[1:]}[kind]
    return lax.empty(shape, src.dtype)


def _chip_copies(src_refs, land_refs, send_sems, recv_sems, kind):
    x, y, c = _position()
    copies = []
    for a, (src, land) in enumerate(zip(src_refs, land_refs)):
        for j, (px, py) in enumerate(_other_chips(x, y)):
            if kind == "whole":
                src_ref, dst_ref = src, land.at[2 * x + y]
            elif kind == "half":
                half = src.shape[1] // 2
                src_ref, dst_ref = src.at[:, pl.ds(pl.multiple_of(c * half, half), half)], land.at[c, 2 * x + y]
            else:
                src_ref, dst_ref = src.at[2 * px + py], land.at[j]
            copies.append(pltpu.make_async_remote_copy(
                src_ref=src_ref, dst_ref=dst_ref, send_sem=send_sems.at[3 * a + j], recv_sem=recv_sems.at[3 * a + j],
                device_id=(px, py, c), device_id_type=MESH))
    return copies


def _exchange_start(srcs, *, kind, name):
    n = len(srcs)
    lands = [_landing_zone(s, kind) for s in srcs]

    def body(*refs):
        for cp in _chip_copies(refs[:n], refs[n:2 * n], refs[2 * n], refs[2 * n + 1], kind):
            cp.start()
        refs[-1][...] = jnp.zeros_like(refs[-1])

    res = pl.pallas_call(
        body, name=name,
        out_shape=(pltpu.SemaphoreType.DMA((3 * n,)), pltpu.SemaphoreType.DMA((3 * n,)),
                   *[pltpu.HBM(a.shape, a.dtype) for a in srcs + lands], jax.ShapeDtypeStruct((8, 128), f32)),
        in_specs=[_HBM] * (2 * n),
        out_specs=(_SEM, _SEM, *[_HBM] * (2 * n), pl.BlockSpec(memory_space=pltpu.VMEM)),
        input_output_aliases={i: 2 + i for i in range(2 * n)},
        compiler_params=pltpu.CompilerParams(has_side_effects=_EFFECT),
    )(*[pltpu.with_memory_space_constraint(a, pltpu.HBM) for a in srcs + lands])
    return res[0], res[1], list(res[2:2 + n]), list(res[2 + n:2 + 2 * n]), res[-1]


def _exchange_wait(handle, after, *, kind, name):
    send_sems, recv_sems, srcs, lands, _ = handle
    n = len(srcs)

    def body(*refs):
        for cp in _chip_copies(refs[:n], refs[n:2 * n], refs[2 * n], refs[2 * n + 1], kind):
            cp.wait_send()
            cp.wait_recv()

    res = pl.pallas_call(
        body, name=name,
        out_shape=tuple(pltpu.HBM(a.shape, a.dtype) for a in srcs + lands),
        in_specs=[_HBM] * (2 * n) + [_SEM, _SEM, pl.BlockSpec(memory_space=pl.ANY)],
        out_specs=tuple([_HBM] * (2 * n)),
        input_output_aliases={i: i for i in range(2 * n)},
        compiler_params=pltpu.CompilerParams(has_side_effects=_EFFECT),
    )(*srcs, *lands, send_sems, recv_sems, after)
    return list(res[:n]), list(res[n:])


def _sum_own_and_received(g, recv, *, name):
    _, r, w = g.shape
    tm = _tile(r)
    if g.dtype == bf16 and tm % 16:
        tm = r
    x, y, _ = _position()
    me = jnp.reshape(2 * x + y, (1,)).astype(jnp.int32)

    def body(me_ref, g_ref, r_ref, o_ref):
        o_ref[...] = (g_ref[0].astype(f32) + r_ref[0].astype(f32)) + (r_ref[1].astype(f32) + r_ref[2].astype(f32))

    return pl.pallas_call(
        body, name=name,
        grid_spec=pltpu.PrefetchScalarGridSpec(
            num_scalar_prefetch=1, grid=(r // tm,),
            in_specs=[pl.BlockSpec((1, tm, w), lambda i, me_ref: (me_ref[0], i, 0)),
                      pl.BlockSpec((3, tm, w), lambda i, me_ref: (0, i, 0))],
            out_specs=pl.BlockSpec((tm, w), lambda i, me_ref: (i, 0))),
        out_shape=jax.ShapeDtypeStruct((r, w), f32),
        compiler_params=_params(("parallel",)),
    )(me, g, recv)


def _swap_cores(arrs, *, name):
    n = len(arrs)

    def body(*refs):
        x, y, c = _position()
        copies = [pltpu.make_async_remote_copy(
            src_ref=refs[i], dst_ref=refs[n + i], send_sem=refs[2 * n].at[i], recv_sem=refs[2 * n + 1].at[i],
            device_id=(x, y, 1 - c), device_id_type=MESH) for i in range(n)]
        for cp in copies:
            cp.start()
        for cp in copies:
            cp.wait_recv()
        for cp in copies:
            cp.wait_send()

    return pl.pallas_call(
        body, name=name,
        in_specs=[pl.BlockSpec(memory_space=pl.ANY)] * n,
        out_specs=[pl.BlockSpec(memory_space=pl.ANY)] * n,
        out_shape=[jax.ShapeDtypeStruct(a.shape, a.dtype) for a in arrs],
        scratch_shapes=[pltpu.SemaphoreType.DMA((n,)), pltpu.SemaphoreType.DMA((n,))],
    )(*arrs)


def _swap_halves(zone, *, name):
    def body(z_ref, o_ref, send_sems, recv_sems):
        x, y, c = _position()
        mine = [pltpu.make_async_remote_copy(
            src_ref=o_ref.at[c, 2 * px + py], dst_ref=o_ref.at[c, 2 * px + py], send_sem=send_sems.at[j],
            recv_sem=recv_sems.at[j], device_id=(x, y, 1 - c), device_id_type=MESH)
            for j, (px, py) in enumerate(_other_chips(x, y))]
        for cp in mine:
            cp.start()
        for j, (px, py) in enumerate(_other_chips(x, y)):
            pltpu.make_async_remote_copy(
                src_ref=o_ref.at[c, 2 * px + py], dst_ref=o_ref.at[1 - c, 2 * px + py], send_sem=send_sems.at[j],
                recv_sem=recv_sems.at[j], device_id=(x, y, 1 - c), device_id_type=MESH).wait_recv()
        for cp in mine:
            cp.wait_send()

    return pl.pallas_call(
        body, name=name,
        in_specs=[pl.BlockSpec(memory_space=pl.ANY)], out_specs=pl.BlockSpec(memory_space=pl.ANY),
        out_shape=jax.ShapeDtypeStruct(zone.shape, zone.dtype), input_output_aliases={0: 0},
        scratch_shapes=[pltpu.SemaphoreType.DMA((3,)), pltpu.SemaphoreType.DMA((3,))],
    )(zone)


def _all_reduce_small(a, after, *, name):
    rows, w = a.shape

    def body(a_ref, after_ref, o_ref, buf, send_sems, recv_sems):
        x, y, c = _position()
        me = 4 * x + 2 * y + c
        buf[0] = a_ref[...]
        sends = []
        for rel in range(1, N_DEV):
            peer = ((1 - x) if rel & 4 else x, (1 - y) if rel & 2 else y, (1 - c) if rel & 1 else c)
            cp = pltpu.make_async_remote_copy(
                src_ref=a_ref, dst_ref=buf.at[rel], send_sem=send_sems.at[rel - 1], recv_sem=recv_sems.at[rel - 1],
                device_id=peer, device_id_type=MESH)
            cp.start()
            sends.append(cp)
        for cp in sends:
            cp.wait_recv()
        for cp in sends:
            cp.wait_send()
        acc = buf[jnp.bitwise_xor(me, 0)]
        for d in range(1, N_DEV):
            acc = acc + buf[jnp.bitwise_xor(me, d)]
        o_ref[...] = acc

    return pl.pallas_call(
        body, name=name,
        in_specs=[pl.BlockSpec(memory_space=pltpu.VMEM), pl.BlockSpec(memory_space=pl.ANY)],
        out_specs=pl.BlockSpec(memory_space=pltpu.VMEM),
        out_shape=jax.ShapeDtypeStruct((rows, w), f32),
        scratch_shapes=[pltpu.VMEM((N_DEV, rows, w), f32), pltpu.SemaphoreType.DMA((N_DEV - 1,)),
                        pltpu.SemaphoreType.DMA((N_DEV - 1,))],
    )(a, after)


def _adamw(w, g_parts, m, v, *, name, transposed=False):
    rows, cols = w.shape
    if transposed:
        tm = 256 if rows % 256 == 0 else rows
        g_spec = pl.BlockSpec((cols, tm), lambda i: (0, i))
    else:
        tm = _tile(rows, 256)
        g_spec = pl.BlockSpec((tm, cols), lambda i: (i, 0))
    n = len(g_parts)

    def body(*refs):
        w_ref, m_ref, v_ref = refs[0], refs[1 + n], refs[2 + n]
        g_ref, d_ref, nm_ref, nv_ref = refs[3 + n:]
        gv = refs[1][...]
        for part in refs[2:1 + n]:
            gv = gv + part[...]
        if transposed:
            gv = gv.T
        g_ref[...] = gv
        nm = ADAM_B1 * m_ref[...] + (1.0 - ADAM_B1) * gv
        nv = ADAM_B2 * v_ref[...] + (1.0 - ADAM_B2) * (gv * gv)
        m_hat = nm / (1.0 - ADAM_B1 ** ADAM_STEP)
        v_hat = nv / (1.0 - ADAM_B2 ** ADAM_STEP)
        d_ref[...] = -ADAM_LR * (m_hat / (jnp.sqrt(v_hat) + ADAM_EPS) + ADAM_WD * w_ref[...])
        nm_ref[...] = nm
        nv_ref[...] = nv

    spec = pl.BlockSpec((tm, cols), lambda i: (i, 0))
    shape = jax.ShapeDtypeStruct((rows, cols), f32)
    return pl.pallas_call(
        body, name=name, grid=(rows // tm,), in_specs=[spec] + [g_spec] * n + [spec] * 2,
        out_specs=[spec] * 4, out_shape=[shape] * 4,
        compiler_params=_params(("parallel",)),
    )(w, *g_parts, m, v)


def _pad_rows(a, rows):
    return jnp.concatenate([a, jnp.zeros((rows - a.shape[0], a.shape[1]), a.dtype)], axis=0) if rows > a.shape[0] else a


_SMALL = (("norm_mix_g", D_MODEL), ("b_in", D_IN), ("sinks", Q_HEADS), ("mix", RWKV_PROJ), ("w0", RWKV_DIM),
          ("a0", RWKV_DIM), ("k_k", RWKV_DIM), ("k_a", RWKV_DIM), ("r_k", RWKV_DIM), ("ln_w", RWKV_DIM),
          ("ln_b", RWKV_DIM), ("norm_ffn_g", D_MODEL), ("norm_final_g", D_MODEL))


def _pack_small(d):
    flat = jnp.concatenate([d[n].reshape(-1).astype(f32) for n, _ in _SMALL])
    return flat


def _unpack_small(flat):
    out, off = {}, 0
    for n, size in _SMALL:
        out[n] = flat[off:off + size]
        off += size
    return out


_SMALL_TOTAL = sum(s for _, s in _SMALL)


def kernel(x, meta_tokens, norm_mix_g, w_in, b_in, attn_sinks, rwkv_mix, rwkv_w0, rwkv_w2, rwkv_a0, rwkv_a2, rwkv_g2, rwkv_k_k, rwkv_k_a, rwkv_r_k, rwkv_ln_w, rwkv_ln_b, w_br_attn, w_br_rwkv, w_o, norm_ffn_g, w_ffn_gate, w_ffn_up, w_ffn_down, norm_final_g, loss_target, m_meta_tokens, m_norm_mix_g, m_w_in, m_b_in, m_attn_sinks, m_rwkv_mix, m_rwkv_w0, m_rwkv_w2, m_rwkv_a0, m_rwkv_a2, m_rwkv_g2, m_rwkv_k_k, m_rwkv_k_a, m_rwkv_r_k, m_rwkv_ln_w, m_rwkv_ln_b, m_w_br_attn, m_w_br_rwkv, m_w_o, m_norm_ffn_g, m_w_ffn_gate, m_w_ffn_up, m_w_ffn_down, m_norm_final_g, v_meta_tokens, v_norm_mix_g, v_w_in, v_b_in, v_attn_sinks, v_rwkv_mix, v_rwkv_w0, v_rwkv_w2, v_rwkv_a0, v_rwkv_a2, v_rwkv_g2, v_rwkv_k_k, v_rwkv_k_a, v_rwkv_r_k, v_rwkv_ln_w, v_rwkv_ln_b, v_w_br_attn, v_w_br_rwkv, v_w_o, v_norm_ffn_g, v_w_ffn_gate, v_w_ffn_up, v_w_ffn_down, v_norm_final_g):
    names = ("meta_tokens", "norm_mix_g", "w_in", "b_in", "attn_sinks", "rwkv_mix", "rwkv_w0", "rwkv_w2", "rwkv_a0",
             "rwkv_a2", "rwkv_g2", "rwkv_k_k", "rwkv_k_a", "rwkv_r_k", "rwkv_ln_w", "rwkv_ln_b", "w_br_attn",
             "w_br_rwkv", "w_o", "norm_ffn_g", "w_ffn_gate", "w_ffn_up", "w_ffn_down", "norm_final_g")
    w_all = dict(zip(names, (meta_tokens, norm_mix_g, w_in, b_in, attn_sinks, rwkv_mix, rwkv_w0, rwkv_w2, rwkv_a0,
                             rwkv_a2, rwkv_g2, rwkv_k_k, rwkv_k_a, rwkv_r_k, rwkv_ln_w, rwkv_ln_b, w_br_attn,
                             w_br_rwkv, w_o, norm_ffn_g, w_ffn_gate, w_ffn_up, w_ffn_down, norm_final_g)))
    m_all = dict(zip(names, (m_meta_tokens, m_norm_mix_g, m_w_in, m_b_in, m_attn_sinks, m_rwkv_mix, m_rwkv_w0,
                             m_rwkv_w2, m_rwkv_a0, m_rwkv_a2, m_rwkv_g2, m_rwkv_k_k, m_rwkv_k_a, m_rwkv_r_k,
                             m_rwkv_ln_w, m_rwkv_ln_b, m_w_br_attn, m_w_br_rwkv, m_w_o, m_norm_ffn_g, m_w_ffn_gate,
                             m_w_ffn_up, m_w_ffn_down, m_norm_final_g)))
    v_all = dict(zip(names, (v_meta_tokens, v_norm_mix_g, v_w_in, v_b_in, v_attn_sinks, v_rwkv_mix, v_rwkv_w0,
                             v_rwkv_w2, v_rwkv_a0, v_rwkv_a2, v_rwkv_g2, v_rwkv_k_k, v_rwkv_k_a, v_rwkv_r_k,
                             v_rwkv_ln_w, v_rwkv_ln_b, v_w_br_attn, v_w_br_rwkv, v_w_o, v_norm_ffn_g, v_w_ffn_gate,
                             v_w_ffn_up, v_w_ffn_down, v_norm_final_g)))
    cx, cy, _ = _position()
    chip = 2 * cx + cy

    t_of = dict(w_in_t="w_in", w_gate_t="w_ffn_gate", w_up_t="w_ffn_up", w_br_attn_t="w_br_attn",
                w_br_rwkv_t="w_br_rwkv", g2_t="rwkv_g2", w2_t="rwkv_w2", a2_t="rwkv_a2")
    plain_of = dict(w_down="w_ffn_down", w_o="w_o")
    meta_cols = meta_tokens.shape[1]

    def shard(k):
        return (w_all[t_of[k]][0].T if k in t_of else w_all[plain_of[k]][0]).astype(bf16)

    def whole(zone, own):
        return lax.dynamic_update_slice_in_dim(zone, own[None], chip, axis=0).reshape(-1, own.shape[-1])

    tiny = ("g2_t", "w2_t", "a2_t")
    late = ("w_gate_t", "w_up_t", "w_down", "w_o", "w_br_attn_t", "w_br_rwkv_t")
    w_in_own = shard("w_in_t")
    w_in_rows, w_in_cols = w_in_own.shape
    tiny_h = _exchange_start([shard(k) for k in tiny] + [meta_tokens], kind="whole", name="gather_tiny_start")
    w_in_h = _exchange_start([w_in_own + tiny_h[4][0, 0].astype(bf16)], kind="half", name="gather_w_in_start")
    behind = w_in_h[4][0, 0].astype(bf16)
    late_h = _exchange_start([shard(k) + behind for k in late], kind="whole", name="gather_late_start")
    own, zones = _exchange_wait(tiny_h, late_h[4], kind="whole", name="gather_tiny_wait")
    got = {k: whole(z, o) for k, z, o in zip(tiny, zones, own)}
    meta_full = whole(zones[-1], own[-1]).reshape(N_CHIPS, N_META, meta_cols).transpose(1, 0, 2).reshape(N_META, -1)
    p = dict(
        g2=got["g2_t"].T.astype(f32), w2=got["w2_t"].T.astype(f32), a2=got["a2_t"].T.astype(f32),
        b_in=b_in, sinks=attn_sinks, mix=rwkv_mix, w0=rwkv_w0, a0=rwkv_a0, k_k=rwkv_k_k, k_a=rwkv_k_a,
        r_k=rwkv_r_k.reshape(1, RWKV_DIM), ln_w=rwkv_ln_w, ln_b=rwkv_ln_b, norm_mix_g=norm_mix_g,
        norm_ffn_g=norm_ffn_g, norm_final_g=norm_final_g.reshape(1, D_MODEL),
    )

    def early_weights(after):
        own_h, zones_h = _exchange_wait(w_in_h, after, kind="half", name="gather_w_in_wait")
        zone = _swap_halves(zones_h[0], name="swap_w_in_halves")
        own_halves = own_h[0].reshape(w_in_rows, 2, w_in_cols // 2).transpose(1, 0, 2)[:, None]
        zone = lax.dynamic_update_slice(zone, own_halves, (0, chip, 0, 0))
        return dict(w_in_t=zone.transpose(1, 2, 0, 3).reshape(N_CHIPS * w_in_rows, w_in_cols))

    def late_weights(after):
        own_l, zones_l = _exchange_wait(late_h, after, kind="whole", name="gather_late_wait")
        return {k: whole(z, o) for k, z, o in zip(late, zones_l, own_l)}

    started = {}

    def emit(group, grads_):
        keys = list(grads_)
        slabs = []
        for k in keys:
            a = grads_[k].T if k in ("g2", "w2", "a2") else grads_[k]
            slabs.append(a.reshape(N_CHIPS, a.shape[0] // N_CHIPS, a.shape[1]))
        started[group] = (keys, _exchange_start(slabs, kind="slab", name="scatter_" + group + "_start"))
        return started[group][1][4][0, 0]

    loss, dx, g = _local_step(x[0], loss_target[0], meta_full, p, early_weights, late_weights, emit)

    grads, delta, new_m, new_v = {}, {}, {}, {}
    in_grad_layout = ("w_in_t", "w_gate_t", "w_up_t")
    weight_of = {**t_of, **plain_of}

    def finish(groups, after, tag):
        parts = {}
        for group in groups:
            keys, handle = started[group]
            slabs, lands = _exchange_wait(handle, after, kind="slab", name="scatter_" + group + "_wait")
            parts.update({k: _sum_own_and_received(s, l, name="sum_chips_" + k) for k, s, l in zip(keys, slabs, lands)})
        keys = list(parts)
        others = dict(zip(keys, _swap_cores([parts[k] for k in keys], name="swap_cores_" + tag)))
        for k in keys:
            both = [parts[k], others[k]]
            k = k + "_t" if k in ("g2", "w2", "a2") else k
            n = weight_of[k]
            shape2 = w_all[n].shape[1:]
            w_, m_, v_ = (a.reshape(shape2) for a in (w_all[n], m_all[n], v_all[n]))
            if k in in_grad_layout:
                res = [t.T for t in _adamw(w_.T, both, m_.T, v_.T, name="adamw_" + n)]
            else:
                res = _adamw(w_, both, m_, v_, name="adamw_" + n, transposed=k in t_of)
            grads[n], delta[n], new_m[n], new_v[n] = (t.reshape(w_all[n].shape) for t in res)
        return delta[n]

    done = finish(("ffn", "branch"), dx, "a")

    small = jnp.concatenate([_pack_small(g), loss.reshape(1)])
    small_rows = -(-small.shape[0] // PACK_W)
    small = jnp.concatenate([small, jnp.zeros((small_rows * PACK_W - small.shape[0],), f32)]).reshape(small_rows, PACK_W)
    small_rows8 = -(-(small_rows + N_META) // 8) * 8
    reduced = _all_reduce_small(_pad_rows(jnp.concatenate([g["meta"], small], axis=0), small_rows8), done,
                                name="reduce_small")
    finish(("input",), reduced, "b")
    g_meta = lax.dynamic_slice_in_dim(reduced[:N_META], chip * meta_cols, meta_cols, axis=1)
    flat = reduced[N_META:N_META + small_rows].reshape(-1)
    g_small = _unpack_small(flat)
    loss_total = flat[_SMALL_TOTAL]

    small_of = dict(norm_mix_g="norm_mix_g", b_in="b_in", attn_sinks="sinks", rwkv_mix="mix", rwkv_w0="w0",
                    rwkv_a0="a0", rwkv_k_k="k_k", rwkv_k_a="k_a", rwkv_r_k="r_k", rwkv_ln_w="ln_w",
                    rwkv_ln_b="ln_b", norm_ffn_g="norm_ffn_g", norm_final_g="norm_final_g")
    grads["meta_tokens"] = g_meta
    for n, k in small_of.items():
        grads[n] = g_small[k].reshape(w_all[n].shape)

    rest = [n for n in names if n not in delta]

    def pack_rest(src):
        flat_ = jnp.concatenate([src[n].reshape(-1) for n in rest])
        rows_ = -(-flat_.shape[0] // (8 * PACK_W)) * 8
        return jnp.concatenate([flat_, jnp.ones((rows_ * PACK_W - flat_.shape[0],), f32)]).reshape(rows_, PACK_W)

    _, d_, m_, v_ = _adamw(pack_rest(w_all), [pack_rest(grads)], pack_rest(m_all), pack_rest(v_all),
                           name="adamw_small")
    off = 0
    for n in rest:
        size = w_all[n].size
        for dst, src in ((delta, d_), (new_m, m_), (new_v, v_)):
            dst[n] = src.reshape(-1)[off:off + size].reshape(w_all[n].shape)
        off += size

    return (loss_total, dx.reshape(x.shape), *[grads[n] for n in names], *[delta[n] for n in names],
            *[new_m[n] for n in names], *[new_v[n] for n in names])
```

```python
import math

import jax
import jax.numpy as jnp
from jax import lax
from jax.experimental import pallas as pl
from jax.experimental.pallas import tpu as pltpu

f32 = jnp.float32
bf16 = jnp.bfloat16

D_MODEL = 1024
N_META = 16
HEAD_DIM = 64
Q_HEADS = 8
KV_HEADS = 2
GROUP = Q_HEADS // KV_HEADS
WINDOW = 128
BLOCK = 128
ROPE_THETA = 500000.0
ROPE_DIM = HEAD_DIM // 4
RWKV_HEADS = 8
RWKV_HEAD = 64
RWKV_DIM = RWKV_HEADS * RWKV_HEAD
DECAY_LORA = 64
AAA_LORA = 64
GATE_LORA = 160
LORA_W = DECAY_LORA + AAA_LORA + GATE_LORA
RWKV_LN_EPS = 64e-5
D_FF = 2816
Q_W = Q_HEADS * HEAD_DIM
KV_W = KV_HEADS * HEAD_DIM
ATTN_PROJ = Q_W + 2 * KV_W
RKV_W = 3 * RWKV_DIM
RWKV_PROJ = RKV_W + LORA_W
D_IN = ATTN_PROJ + RWKV_PROJ + 2 * D_MODEL
RMS_EPS = 1e-6
NEG_INF = -1e30
PAD = BLOCK - N_META
FRONT = PAD + N_META

ADAM_LR = 0.001
ADAM_B1 = 0.9
ADAM_B2 = 0.999
ADAM_EPS = 1e-08
ADAM_WD = 0.01
ADAM_STEP = 10

N_CHIPS = 4
N_DEV = 8
CHUNK = 128
VMEM_LIMIT = 56 * 1024 * 1024
PACK_W = 1024
MESH = pl.DeviceIdType.MESH


def _tile(m, pref=384):
    for step in (16, 8):
        for t in range(min(m, pref) // step * step, 0, -step):
            if m % t == 0:
                return t
    return m


def _params(sem=None):
    return pltpu.CompilerParams(dimension_semantics=sem, vmem_limit_bytes=VMEM_LIMIT)


def _full(shape):
    nd = len(shape)
    return pl.BlockSpec(shape, lambda *_: (0,) * nd)


def _dot(a, b, dims="nn"):
    dn = {"nn": (((1,), (0,)), ((), ())), "nt": (((1,), (1,)), ((), ())), "tn": (((0,), (0,)), ((), ()))}[dims]
    return lax.dot_general(a.astype(bf16), b.astype(bf16), dn, preferred_element_type=f32)


def _two_pass(x, m):
    x_hi = x.astype(bf16)
    x_lo = (x - x_hi.astype(f32)).astype(bf16)
    return _dot(x_hi, m) + _dot(x_lo, m)


@jax.custom_vjp
def _dot_const(x, m):
    return _two_pass(x, m)


def _dot_const_fwd(x, m):
    return _two_pass(x, m), m


def _dot_const_bwd(m, ct):
    return _two_pass(ct, m.T), jnp.zeros_like(m)


_dot_const.defvjp(_dot_const_fwd, _dot_const_bwd)


def _two_pass_left(m, x, dims):
    x_hi = x.astype(bf16)
    x_lo = (x - x_hi.astype(f32)).astype(bf16)
    return _dot(m, x_hi, dims) + _dot(m, x_lo, dims)


@jax.custom_vjp
def _const_dot(m, x):
    return _two_pass_left(m, x, "nn")


def _const_dot_fwd(m, x):
    return _two_pass_left(m, x, "nn"), m


def _const_dot_bwd(m, ct):
    return jnp.zeros_like(m), _two_pass_left(m, ct, "tn")


_const_dot.defvjp(_const_dot_fwd, _const_dot_bwd)


def _mm(a, b, mode, *, name, out_dtype=f32, bias=None, add=None, zero_rows_below=0):
    m, _ = a.shape
    n = b.shape[1] if mode == "nn" else b.shape[0]
    tm = _tile(m)
    has_bias, has_add = bias is not None, add is not None

    def body(*refs):
        a_ref, b_ref = refs[0], refs[1]
        o_ref = refs[-1]
        acc = _dot(a_ref[...], b_ref[...], mode)
        k = 2
        if has_bias:
            acc = acc + refs[k][...]
            k += 1
        if zero_rows_below:
            rows = pl.program_id(0) * tm + lax.broadcasted_iota(jnp.int32, acc.shape, 0)
            acc = jnp.where(rows >= zero_rows_below, acc, 0.0)
        if has_add:
            acc = acc + refs[k][...].astype(f32)
        o_ref[...] = acc.astype(out_dtype)

    ins = [a, b]
    in_specs = [pl.BlockSpec((tm, a.shape[1]), lambda i: (i, 0)), _full(b.shape)]
    if has_bias:
        ins.append(bias)
        in_specs.append(_full(bias.shape))
    if has_add:
        ins.append(add)
        in_specs.append(pl.BlockSpec((tm, n), lambda i: (i, 0)))
    return pl.pallas_call(
        body, name=name, grid=(m // tm,), in_specs=in_specs,
        out_specs=pl.BlockSpec((tm, n), lambda i: (i, 0)),
        out_shape=jax.ShapeDtypeStruct((m, n), out_dtype),
        compiler_params=_params(("parallel",)),
    )(*ins)


def _mm_sum(a_list, b_list, *, name, out_dtype=bf16):
    m = a_list[0].shape[0]
    n = b_list[0].shape[1]
    k = len(a_list)
    tm = _tile(m)

    def body(*refs):
        acc = _dot(refs[0][...], refs[k][...])
        for i in range(1, k):
            acc = acc + _dot(refs[i][...], refs[k + i][...])
        refs[-1][...] = acc.astype(out_dtype)

    return pl.pallas_call(
        body, name=name, grid=(m // tm,),
        in_specs=[pl.BlockSpec((tm, a.shape[1]), lambda i: (i, 0)) for a in a_list] + [_full(b.shape) for b in b_list],
        out_specs=pl.BlockSpec((tm, n), lambda i: (i, 0)),
        out_shape=jax.ShapeDtypeStruct((m, n), out_dtype),
        compiler_params=_params(("parallel",)),
    )(*a_list, *b_list)


def _mm_fanout(a, b_list, bias_list, *, name, zero_rows_below=0):
    m, kdim = a.shape
    k = len(b_list)
    tm = _tile(m)

    def body(*refs):
        av = refs[0][...]
        for j in range(k):
            acc = _dot(av, refs[1 + j][...], "nt") + refs[1 + k + j][...]
            if zero_rows_below:
                rows = pl.program_id(0) * tm + lax.broadcasted_iota(jnp.int32, acc.shape, 0)
                acc = jnp.where(rows >= zero_rows_below, acc, 0.0)
            refs[1 + 2 * k + j][...] = acc

    return pl.pallas_call(
        body, name=name, grid=(m // tm,),
        in_specs=[pl.BlockSpec((tm, kdim), lambda i: (i, 0))] + [_full(b.shape) for b in b_list]
        + [_full(c.shape) for c in bias_list],
        out_specs=[pl.BlockSpec((tm, b.shape[0]), lambda i: (i, 0)) for b in b_list],
        out_shape=[jax.ShapeDtypeStruct((m, b.shape[0]), f32) for b in b_list],
        compiler_params=_params(("parallel",)),
    )(a, *b_list, *bias_list)


def _mm_tn(a, b, *, name, colsum=False, out_dtype=bf16):
    r, m = a.shape
    n = b.shape[1]
    tr = _tile(r, 1408)
    tmo = m
    for cand in (1408, 1024, 768, 512):
        if m > 1024 and m % cand == 0:
            tmo = cand
            break
    steps = r // tr

    def body(a_ref, b_ref, o_ref, *rest):
        acc = rest[-1]
        i = pl.program_id(1)

        @pl.when(i == 0)
        def _():
            acc[...] = jnp.zeros_like(acc)
            if colsum:
                rest[0][...] = jnp.zeros_like(rest[0])

        acc[...] += _dot(a_ref[...], b_ref[...], "tn")
        if colsum:
            rest[0][...] += jnp.sum(a_ref[...].astype(f32), axis=0, keepdims=True)

        @pl.when(i == steps - 1)
        def _():
            o_ref[...] = acc[...].astype(out_dtype)

    out_shape = [jax.ShapeDtypeStruct((m, n), out_dtype)]
    out_specs = [pl.BlockSpec((tmo, n), lambda j, i: (j, 0))]
    if colsum:
        out_shape.append(jax.ShapeDtypeStruct((1, m), f32))
        out_specs.append(pl.BlockSpec((1, tmo), lambda j, i: (0, j)))
    res = pl.pallas_call(
        body, name=name, grid=(m // tmo, steps),
        in_specs=[pl.BlockSpec((tr, tmo), lambda j, i: (i, j)), pl.BlockSpec((tr, n), lambda j, i: (i, 0))],
        out_specs=out_specs, out_shape=out_shape,
        scratch_shapes=[pltpu.VMEM((tmo, n), f32)],
        compiler_params=_params(("parallel", "arbitrary")),
    )(a, b)
    return res if colsum else res[0]


def _rowwise(fn, rows, params, outs, *, name, tm=None):
    m = rows[0].shape[0]
    tm = tm or _tile(m)
    nr, npar = len(rows), len(params)

    def body(*refs):
        vals = [r[...] for r in refs[:nr + npar]]
        res = fn(*vals)
        for o_ref, v in zip(refs[nr + npar:], res):
            o_ref[...] = v.astype(o_ref.dtype)

    return pl.pallas_call(
        body, name=name, grid=(m // tm,),
        in_specs=[pl.BlockSpec((tm, r.shape[1]), lambda i: (i, 0)) for r in rows] + [_full(p.shape) for p in params],
        out_specs=[pl.BlockSpec((tm, w), lambda i: (i, 0)) for w, _ in outs],
        out_shape=[jax.ShapeDtypeStruct((m, w), dt) for w, dt in outs],
        compiler_params=_params(("parallel",)),
    )(*rows, *params)


def _rowwise_bwd(fn, rows, params, cts, *, name, diff_rows, diff_params, tm=None, zero_rows_below=0, out_dtypes=None):
    m = rows[0].shape[0]
    tm = tm or _tile(m)
    nr, npar = len(rows), len(params)
    d_idx = [i for i in range(nr) if diff_rows[i]]
    p_idx = [i for i in range(npar) if diff_params[i]]
    out_dtypes = out_dtypes or [f32] * len(d_idx)
    flat_cts = [c for group in cts for c in group]
    n_ct = len(flat_cts)

    def body(*refs):
        vals = [r[...] for r in refs[:nr + npar]]
        ct_refs = refs[nr + npar:nr + npar + n_ct]
        out_refs = refs[nr + npar + n_ct:]
        ct_vals, k = [], 0
        for group in cts:
            acc = ct_refs[k][...].astype(f32)
            for extra in range(1, len(group)):
                acc = acc + ct_refs[k + extra][...].astype(f32)
            k += len(group)
            if zero_rows_below:
                rr = pl.program_id(0) * tm + lax.broadcasted_iota(jnp.int32, acc.shape, 0)
                acc = jnp.where(rr >= zero_rows_below, acc, 0.0)
            ct_vals.append(acc)

        def g(*dargs):
            full = list(vals)
            for pos, i in enumerate(d_idx):
                full[i] = dargs[pos]
            for pos, i in enumerate(p_idx):
                full[nr + i] = dargs[len(d_idx) + pos]
            return tuple(fn(*full))

        _, vjp = jax.vjp(g, *[vals[i].astype(f32) for i in d_idx], *[vals[nr + i] for i in p_idx])
        grads = vjp(tuple(ct_vals))
        for pos in range(len(d_idx)):
            out_refs[pos][...] = grads[pos].astype(out_refs[pos].dtype)
        first = pl.program_id(0) == 0
        for pos in range(len(p_idx)):
            o_ref = out_refs[len(d_idx) + pos]

            @pl.when(first)
            def _(o_ref=o_ref):
                o_ref[...] = jnp.zeros_like(o_ref)

            o_ref[...] += grads[len(d_idx) + pos]

    return pl.pallas_call(
        body, name=name, grid=(m // tm,),
        in_specs=[pl.BlockSpec((tm, r.shape[1]), lambda i: (i, 0)) for r in rows] + [_full(p.shape) for p in params]
        + [pl.BlockSpec((tm, c.shape[1]), lambda i: (i, 0)) for c in flat_cts],
        out_specs=[pl.BlockSpec((tm, rows[i].shape[1]), lambda i_: (i_, 0)) for i in d_idx]
        + [_full(params[i].shape) for i in p_idx],
        out_shape=[jax.ShapeDtypeStruct(rows[i].shape, dt) for i, dt in zip(d_idx, out_dtypes)]
        + [jax.ShapeDtypeStruct(params[i].shape, f32) for i in p_idx],
        compiler_params=_params(("arbitrary",)),
    )(*rows, *params, *flat_cts)


def _rms(x, g):
    return x * lax.rsqrt(jnp.mean(x * x, axis=-1, keepdims=True) + RMS_EPS) * g


def _head_sum_matrix(width, head):
    idx = jnp.arange(width) // head
    return (idx[:, None] == idx[None, :]).astype(f32)


def _rope_tables(lp):
    half = ROPE_DIM // 2
    pos = (jnp.arange(lp) - PAD).astype(f32)
    inv_freq = jnp.power(jnp.float32(ROPE_THETA), -jnp.arange(half, dtype=f32) * (2.0 / ROPE_DIM))
    ang = pos[:, None] * inv_freq[None, :]
    cos, sin = jnp.cos(ang), jnp.sin(ang)
    ones = jnp.ones((lp, HEAD_DIM - ROPE_DIM), f32)
    zeros = jnp.zeros((lp, HEAD_DIM - ROPE_DIM), f32)
    cos_t = jnp.concatenate([cos, cos, ones], axis=1)
    sin_t = jnp.concatenate([-sin, sin, zeros], axis=1)
    i = jnp.arange(HEAD_DIM)
    src = jnp.where(i < half, i + half, jnp.where(i < ROPE_DIM, i - half, i))
    swap = ((i[:, None] == src[None, :]) & (i[None, :] < ROPE_DIM)).astype(f32)
    return cos_t, sin_t, swap


def _attn_prep(qkv, cos_t, sin_t, swap):
    outs = []
    for h in range(Q_HEADS + KV_HEADS):
        t = qkv[:, h * HEAD_DIM:(h + 1) * HEAD_DIM]
        outs.append(t * cos_t + _dot_const(t, swap) * sin_t)
    q = jnp.concatenate(outs[:Q_HEADS], axis=1)
    k = jnp.concatenate(outs[Q_HEADS:], axis=1)
    return q, k, qkv[:, Q_W + KV_W:]


def _softplus(z):
    return jnp.maximum(z, 0.0) + jnp.log1p(jnp.exp(-jnp.abs(z)))


def _rwkv_prep(rkv, lora, w0, w2, a0, a2, g2, k_k, k_a, hsum):
    r = rkv[:, :RWKV_DIM]
    k = rkv[:, RWKV_DIM:2 * RWKV_DIM]
    v = rkv[:, 2 * RWKV_DIM:]
    dw = lora[:, :DECAY_LORA]
    da = lora[:, DECAY_LORA:DECAY_LORA + AAA_LORA]
    dg = lora[:, DECAY_LORA + AAA_LORA:]
    w = -_softplus(-(w0 + _dot(jnp.tanh(dw), w2))) - 0.5
    a = jax.nn.sigmoid(a0 + _dot(da, a2))
    g = _dot(jax.nn.sigmoid(dg), g2)
    kk = k * k_k
    kk = kk * lax.rsqrt(jnp.maximum(_dot_const(kk * kk, hsum), 1e-24))
    k = k * (1.0 + (a - 1.0) * k_a)
    log_decay = -jnp.exp(w)
    return r, log_decay, k, v, -kk, kk * a, g


def _rwkv_post(y, r, k, v, g, ln_w, ln_b, r_k, hmean):
    hsum = hmean * RWKV_HEAD
    mean = _dot_const(y, hmean)
    yc = y - mean
    var = _dot_const(yc * yc, hmean)
    yn = yc * lax.rsqrt(var + RWKV_LN_EPS) * ln_w + ln_b
    bonus = _dot_const(r * k * r_k, hsum) * v
    return ((yn + bonus) * g,)


def _merge(gates, br_a, br_r):
    sg = jax.nn.sigmoid(gates)
    return (sg[:, :D_MODEL] * br_a + sg[:, D_MODEL:] * br_r,)


def _swiglu(gate, up):
    return (jax.nn.silu(gate) * up,)


def _ffn_in(f, w_gate_t, w_up_t, *, name):
    m, d = f.shape
    n = w_gate_t.shape[0]
    tm = _tile(m)

    def body(f_ref, wg_ref, wu_ref, g_ref, u_ref, a_ref):
        g = _dot(f_ref[...], wg_ref[...], "nt")
        u = _dot(f_ref[...], wu_ref[...], "nt")
        g_ref[...] = g.astype(g_ref.dtype)
        u_ref[...] = u.astype(u_ref.dtype)
        a_ref[...] = _swiglu(g, u)[0].astype(a_ref.dtype)

    spec = pl.BlockSpec((tm, n), lambda i: (i, 0))
    return pl.pallas_call(
        body, name=name, grid=(m // tm,),
        in_specs=[pl.BlockSpec((tm, d), lambda i: (i, 0)), _full(w_gate_t.shape), _full(w_up_t.shape)],
        out_specs=[spec] * 3, out_shape=[jax.ShapeDtypeStruct((m, n), bf16)] * 3,
        compiler_params=_params(("parallel",)),
    )(f, w_gate_t, w_up_t)


def _branch_merge(y_attn, y_rwkv, w_attn_t, w_rwkv_t, gates, *, name):
    m = y_attn.shape[0]
    tm = _tile(m)

    def body(ya_ref, yr_ref, wa_ref, wr_ref, g_ref, a_ref, r_ref, o_ref):
        br_a = _dot(ya_ref[...], wa_ref[...], "nt")
        br_r = _dot(yr_ref[...], wr_ref[...], "nt")
        a_ref[...] = br_a.astype(a_ref.dtype)
        r_ref[...] = br_r.astype(r_ref.dtype)
        o_ref[...] = _merge(g_ref[...], br_a, br_r)[0].astype(o_ref.dtype)

    rows = lambda a: pl.BlockSpec((tm, a.shape[1]), lambda i: (i, 0))
    spec = pl.BlockSpec((tm, D_MODEL), lambda i: (i, 0))
    return pl.pallas_call(
        body, name=name, grid=(m // tm,),
        in_specs=[rows(y_attn), rows(y_rwkv), _full(w_attn_t.shape), _full(w_rwkv_t.shape), rows(gates)],
        out_specs=[spec] * 3, out_shape=[jax.ShapeDtypeStruct((m, D_MODEL), bf16)] * 3,
        compiler_params=_params(("parallel",)),
    )(y_attn, y_rwkv, w_attn_t, w_rwkv_t, gates)


def _branch_merge_bwd(dh, w_o, gates, br_a, br_r, *, name):
    m = dh.shape[0]
    tm = _tile(m)

    def body(dh_ref, w_ref, g_ref, a_ref, r_ref, dg_ref, da_ref, dr_ref):
        dmerged = _dot(dh_ref[...], w_ref[...], "nt")
        _, vjp = jax.vjp(lambda g, a, r: _merge(g, a, r)[0], g_ref[...], a_ref[...].astype(f32),
                         r_ref[...].astype(f32))
        dg, da, dr = vjp(dmerged)
        dg_ref[...] = dg.astype(dg_ref.dtype)
        da_ref[...] = da.astype(da_ref.dtype)
        dr_ref[...] = dr.astype(dr_ref.dtype)

    rows = lambda a: pl.BlockSpec((tm, a.shape[1]), lambda i: (i, 0))
    return pl.pallas_call(
        body, name=name, grid=(m // tm,),
        in_specs=[rows(dh), _full(w_o.shape), rows(gates), rows(br_a), rows(br_r)],
        out_specs=[rows(gates), rows(br_a), rows(br_r)],
        out_shape=[jax.ShapeDtypeStruct(gates.shape, bf16), jax.ShapeDtypeStruct(br_a.shape, bf16),
                   jax.ShapeDtypeStruct(br_r.shape, bf16)],
        compiler_params=_params(("parallel",)),
    )(dh, w_o, gates, br_a, br_r)


def _ffn_in_bwd(dh, w_down, gate, up, *, name):
    m, d = dh.shape
    n = w_down.shape[0]
    tm = _tile(m)

    def body(dh_ref, w_ref, g_ref, u_ref, dg_ref, du_ref):
        dact = _dot(dh_ref[...], w_ref[...], "nt")
        _, vjp = jax.vjp(lambda a, b: _swiglu(a, b)[0], g_ref[...].astype(f32), u_ref[...].astype(f32))
        dg, du = vjp(dact)
        dg_ref[...] = dg.astype(dg_ref.dtype)
        du_ref[...] = du.astype(du_ref.dtype)

    spec = pl.BlockSpec((tm, n), lambda i: (i, 0))
    return pl.pallas_call(
        body, name=name, grid=(m // tm,),
        in_specs=[pl.BlockSpec((tm, d), lambda i: (i, 0)), _full(w_down.shape), spec, spec],
        out_specs=[spec] * 2, out_shape=[jax.ShapeDtypeStruct((m, n), bf16)] * 2,
        compiler_params=_params(("parallel",)),
    )(dh, w_down, gate, up)


def _previous_rows(x, before_ref, first_tile):
    rows = lax.broadcasted_iota(jnp.int32, x.shape, 0)
    last = jnp.where(first_tile, 0.0, before_ref[7:8, :])
    return jnp.where(rows == 0, last, pltpu.roll(x, 1, axis=0))


def _mixer_inputs(ps, mixes, params, *, name):
    m = ps[0].shape[0]
    tm = _tile(m)
    sub = tm // 8
    n_par = len(params)

    def body(*refs):
        first = pl.program_id(0) == 0
        pf = []
        for k in range(2):
            x = refs[k][...]
            pf.append(x + (_previous_rows(x, refs[2 + k], first) - x) * refs[4 + k][...])
        res = _rwkv_prep(*pf, *[ref[...] for ref in refs[6:6 + n_par]])
        for o_ref, val in zip(refs[6 + n_par:], res):
            o_ref[...] = val

    tile = lambda a: pl.BlockSpec((tm, a.shape[1]), lambda i: (i, 0))
    before = lambda a: pl.BlockSpec((8, a.shape[1]), lambda i: (jnp.maximum(i * sub - 1, 0), 0))
    out = pl.BlockSpec((tm, RWKV_DIM), lambda i: (i, 0))
    return pl.pallas_call(
        body, name=name, grid=(m // tm,),
        in_specs=[tile(a) for a in ps] + [before(a) for a in ps] + [_full(a.shape) for a in mixes + params],
        out_specs=[out] * 7, out_shape=[jax.ShapeDtypeStruct((m, RWKV_DIM), f32)] * 7,
        compiler_params=_params(("parallel",)),
    )(*ps, *ps, *mixes, *params)


def _mixer_inputs_bwd(ps, mixes, params, cts, *, name):
    m = ps[0].shape[0]
    tm = _tile(m)
    sub = tm // 8
    nt = m // tm
    n_par = len(params)
    flat_cts = [c for group in cts for c in group]
    n_ct = len(flat_cts)

    def body(*refs):
        i = pl.program_id(0)
        tile_index = nt - 1 - i
        ct_refs = refs[6 + n_par:6 + n_par + n_ct]
        dp_refs = refs[6 + n_par + n_ct:8 + n_par + n_ct]
        dmix_refs = refs[8 + n_par + n_ct:10 + n_par + n_ct]
        dpar_refs = refs[10 + n_par + n_ct:9 + 2 * n_par + n_ct]
        carries = refs[9 + 2 * n_par + n_ct:]
        rows1 = tile_index * tm + lax.broadcasted_iota(jnp.int32, (tm, 1), 0)
        live = rows1 >= PAD

        @pl.when(i == 0)
        def _():
            for ref in (*dmix_refs, *dpar_refs, *carries):
                ref[...] = jnp.zeros_like(ref)

        xs, prevs, pf = [], [], []
        for k in range(2):
            x = refs[k][...]
            xp = _previous_rows(x, refs[2 + k], tile_index == 0)
            xs.append(x)
            prevs.append(xp)
            pf.append(x + (xp - x) * refs[4 + k][...])
        ct_vals, pos = [], 0
        for group in cts:
            acc = ct_refs[pos][...].astype(f32)
            for extra in range(1, len(group)):
                acc = acc + ct_refs[pos + extra][...].astype(f32)
            pos += len(group)
            ct_vals.append(jnp.where(live, acc, 0.0))
        par_vals = [ref[...] for ref in refs[6:6 + n_par]]
        _, vjp = jax.vjp(lambda *args: _rwkv_prep(*args, par_vals[-1]), *pf, *par_vals[:-1])
        g = vjp(tuple(ct_vals))
        for k in range(2):
            dpf = g[k]
            mixv = refs[4 + k][...]
            dm = dpf * mixv
            rows = lax.broadcasted_iota(jnp.int32, dm.shape, 0)
            dm_next = jnp.where(rows == tm - 1, carries[k][...], pltpu.roll(dm, tm - 1, axis=0))
            dp_refs[k][...] = jnp.where(live, dpf - dm + dm_next, 0.0).astype(dp_refs[k].dtype)
            carries[k][...] = dm[0:1, :]
            dmix_refs[k][...] += jnp.sum(dpf * (prevs[k] - xs[k]), axis=0, keepdims=True)
        for ref, val in zip(dpar_refs, g[2:]):
            ref[...] += val

    tile = lambda a: pl.BlockSpec((tm, a.shape[1]), lambda i: (nt - 1 - i, 0))
    before = lambda a: pl.BlockSpec((8, a.shape[1]), lambda i: (jnp.maximum((nt - 1 - i) * sub - 1, 0), 0))
    return pl.pallas_call(
        body, name=name, grid=(nt,),
        in_specs=[tile(a) for a in ps] + [before(a) for a in ps] + [_full(a.shape) for a in mixes + params]
        + [tile(c) for c in flat_cts],
        out_specs=[tile(a) for a in ps] + [_full(a.shape) for a in mixes + params[:-1]],
        out_shape=[jax.ShapeDtypeStruct(a.shape, bf16) for a in ps]
        + [jax.ShapeDtypeStruct(a.shape, f32) for a in mixes + params[:-1]],
        scratch_shapes=[pltpu.VMEM((1, a.shape[1]), f32) for a in ps],
        compiler_params=_params(("arbitrary",)),
    )(*ps, *ps, *mixes, *params, *flat_cts)


def _attn_masks(blk):
    qi = lax.broadcasted_iota(jnp.int32, (BLOCK, BLOCK), 0)
    ki = lax.broadcasted_iota(jnp.int32, (BLOCK, BLOCK), 1)
    qpos = blk * BLOCK + qi - PAD
    kpos_c = blk * BLOCK + ki - PAD
    kpos_p = kpos_c - BLOCK
    kpos_m = ki - PAD

    def band(kpos):
        return (kpos >= N_META) & (kpos <= qpos) & (qpos - kpos < WINDOW)

    return band(kpos_p), band(kpos_c), (kpos_m >= 0) & (kpos_m <= qpos)


def _attn_probs(qs, k3s, sink, oks):
    s = [[jnp.where(ok, _dot(qh, kx, "nt"), NEG_INF) for kx, ok in zip(k3, oks)] for qh, k3 in zip(qs, k3s)]
    mx = [jnp.maximum(jnp.maximum(jnp.max(t[0], -1, keepdims=True), jnp.max(t[1], -1, keepdims=True)),
                      jnp.maximum(jnp.max(t[2], -1, keepdims=True), sk)) for t, sk in zip(s, sink)]
    e = [[jnp.exp(tx - m) for tx in t] for t, m in zip(s, mx)]
    e_sink = [jnp.exp(sk - m) for sk, m in zip(sink, mx)]
    inv = [1.0 / (jnp.sum(t[0], -1, keepdims=True) + jnp.sum(t[1], -1, keepdims=True)
                  + jnp.sum(t[2], -1, keepdims=True) + es) for t, es in zip(e, e_sink)]
    return [[tx * i for tx in t] for t, i in zip(e, inv)], [es * i for es, i in zip(e_sink, inv)]


def _head_cols(i):
    return slice(i * HEAD_DIM, (i + 1) * HEAD_DIM)


def _attn_operands(refs):
    q_ref, kp_ref, kc_ref, km_ref, vp_ref, vc_ref, vm_ref, s_ref = refs
    qs = [q_ref[:, _head_cols(i)] * (HEAD_DIM ** -0.5) for i in range(Q_HEADS)]
    k3 = [[ref[:, _head_cols(h)] for ref in (kp_ref, kc_ref, km_ref)] for h in range(KV_HEADS)]
    v3 = [[ref[:, _head_cols(h)] for ref in (vp_ref, vc_ref, vm_ref)] for h in range(KV_HEADS)]
    return (qs, [k3[i // GROUP] for i in range(Q_HEADS)], [v3[i // GROUP] for i in range(Q_HEADS)],
            [s_ref[:, i:i + 1] for i in range(Q_HEADS)])


def _attention(q, k, v, sinks, *, name):
    lp = q.shape[0]
    nb = lp // BLOCK
    prev = lambda i: (jnp.maximum(i - 1, 0), 0)
    cur = lambda i: (i, 0)
    meta = lambda i: (0, 0)
    kv = lambda index: pl.BlockSpec((BLOCK, KV_W), index)

    def body(*refs):
        o_ref = refs[-1]
        qs, k3s, v3s, sink = _attn_operands(refs[:-1])
        p, _ = _attn_probs(qs, k3s, sink, _attn_masks(pl.program_id(0)))
        out = [_dot(ph[0], v3[0]) + _dot(ph[1], v3[1]) + _dot(ph[2], v3[2]) for ph, v3 in zip(p, v3s)]
        for i in range(Q_HEADS):
            o_ref[:, _head_cols(i)] = out[i].astype(o_ref.dtype)

    return pl.pallas_call(
        body, name=name, grid=(nb,),
        in_specs=[pl.BlockSpec((BLOCK, Q_W), cur), kv(prev), kv(cur), kv(meta), kv(prev), kv(cur), kv(meta),
                  _full((1, Q_HEADS))],
        out_specs=pl.BlockSpec((BLOCK, Q_W), cur),
        out_shape=jax.ShapeDtypeStruct((lp, Q_W), bf16),
        compiler_params=_params(("parallel",)),
    )(q, k, k, k, v, v, v, sinks)


def _attention_bwd(q, k, v, sinks, do, *, name):
    lp = q.shape[0]
    nb = lp // BLOCK
    cur = lambda n: (jnp.minimum(n, nb - 1), 0)
    prev = lambda n: (jnp.maximum(jnp.minimum(n, nb - 1) - 1, 0), 0)
    behind = lambda n: (jnp.maximum(n - 1, 0), 0)
    meta = lambda n: (0, 0)
    kv = lambda index: pl.BlockSpec((BLOCK, KV_W), index)
    scale = HEAD_DIM ** -0.5

    def body(*refs):
        ins, do_ref = refs[:8], refs[8]
        dq_ref, dk_ref, dv_ref, dkm_ref, dvm_ref, ds_ref, carry_k, carry_v = refs[9:]
        n = pl.program_id(0)

        @pl.when(n == 0)
        def _():
            for ref in (dkm_ref, dvm_ref, ds_ref, carry_k, carry_v):
                ref[...] = jnp.zeros_like(ref)

        @pl.when(n < nb)
        def _():
            qs, k3s, v3s, sink = _attn_operands(ins)
            do = [do_ref[:, _head_cols(i)] for i in range(Q_HEADS)]
            p, p_sink = _attn_probs(qs, k3s, sink, _attn_masks(n))
            out = [_dot(ph[0], v3[0]) + _dot(ph[1], v3[1]) + _dot(ph[2], v3[2]) for ph, v3 in zip(p, v3s)]
            delta = [jnp.sum(d * o, -1, keepdims=True) for d, o in zip(do, out)]
            dp = [[_dot(d, vx, "nt") for vx in v3] for d, v3 in zip(do, v3s)]
            ds = [[px * (dx - dl) for px, dx in zip(ph, dh)] for ph, dh, dl in zip(p, dp, delta)]
            dq = [_dot(dsh[0], k3[0]) + _dot(dsh[1], k3[1]) + _dot(dsh[2], k3[2]) for dsh, k3 in zip(ds, k3s)]
            for i in range(Q_HEADS):
                dq_ref[:, _head_cols(i)] = dq[i] * scale
                ds_ref[:, i:i + 1] -= jnp.sum(p_sink[i] * delta[i], axis=0, keepdims=True)
            for h in range(KV_HEADS):
                group = slice(h * GROUP, (h + 1) * GROUP)
                q_all = jnp.concatenate(qs[group], axis=0)
                do_all = jnp.concatenate(do[group], axis=0)
                dk3 = [_dot(jnp.concatenate([dsh[x] for dsh in ds[group]], axis=0), q_all, "tn") for x in range(3)]
                dv3 = [_dot(jnp.concatenate([ph[x] for ph in p[group]], axis=0), do_all, "tn") for x in range(3)]
                hs = _head_cols(h)
                for out_ref, carry, meta_ref, d3 in ((dk_ref, carry_k, dkm_ref, dk3),
                                                     (dv_ref, carry_v, dvm_ref, dv3)):
                    out_ref[:, hs] = carry[:, hs] + d3[0]
                    carry[:, hs] = d3[1]
                    meta_ref[:, hs] += d3[2]

        @pl.when(n == nb)
        def _():
            dk_ref[...] = carry_k[...]
            dv_ref[...] = carry_v[...]

    kv_shape = jax.ShapeDtypeStruct((lp, KV_W), f32)
    one_shape = jax.ShapeDtypeStruct((BLOCK, KV_W), f32)
    return pl.pallas_call(
        body, name=name, grid=(nb + 1,),
        in_specs=[pl.BlockSpec((BLOCK, Q_W), cur), kv(prev), kv(cur), kv(meta), kv(prev), kv(cur), kv(meta),
                  _full((1, Q_HEADS)), pl.BlockSpec((BLOCK, Q_W), cur)],
        out_specs=[pl.BlockSpec((BLOCK, Q_W), cur), kv(behind), kv(behind), kv(meta), kv(meta),
                   _full((1, Q_HEADS))],
        out_shape=[jax.ShapeDtypeStruct((lp, Q_W), f32), kv_shape, kv_shape, one_shape, one_shape,
                   jax.ShapeDtypeStruct((1, Q_HEADS), f32)],
        scratch_shapes=[pltpu.VMEM((BLOCK, KV_W), f32), pltpu.VMEM((BLOCK, KV_W), f32)],
        compiler_params=_params(("arbitrary",)),
    )(q, k, k, k, v, v, v, sinks, do)


@jax.custom_vjp
def _known_inverse(l, x):
    return x


def _known_inverse_fwd(l, x):
    return x, x


def _known_inverse_bwd(x, ct):
    return _dot(_dot(x, ct, "tn"), x, "nt"), jnp.zeros_like(x)


_known_inverse.defvjp(_known_inverse_fwd, _known_inverse_bwd)


@jax.custom_vjp
def _decayed(x, c):
    return (x * jnp.exp(c)).astype(bf16).astype(f32)


def _decayed_fwd(x, c):
    e = jnp.exp(c)
    out = (x * e).astype(bf16).astype(f32)
    return out, (e, out)


def _decayed_bwd(res, ct):
    e, out = res
    return ct * e, ct * out


_decayed.defvjp(_decayed_fwd, _decayed_bwd)


@jax.custom_vjp
def _pair(x, y):
    return _dot(x, y, "nt")


def _pair_fwd(x, y):
    return _dot(x, y, "nt"), (x, y)


def _pair_bwd(res, ct):
    x, y = res
    hi = ct.astype(bf16)
    lo = (ct - hi.astype(f32)).astype(bf16)
    return _dot(hi, y) + _dot(lo, y), _dot(hi, x, "tn") + _dot(lo, x, "tn")


_pair.defvjp(_pair_fwd, _pair_bwd)


def _scan_chunk(s0, r, lw, k, v, a, b, inv=None):
    t = r[0].shape[0]
    ii = lax.broadcasted_iota(jnp.int32, (t, t), 0)
    jj = lax.broadcasted_iota(jnp.int32, (t, t), 1)
    incl = jj <= ii
    strict = jj < ii
    tri = incl.astype(f32)
    eye = jnp.where(ii == jj, 1.0, 0.0)
    cl = [_const_dot(tri, x) for x in lw]
    mid = [c[t // 2 - 1:t // 2, :] for c in cl]
    s0 = [s * jnp.exp(m) for s, m in zip(s0, mid)]
    cl = [c - m for c, m in zip(cl, mid)]
    rt = [_decayed(x, c) for x, c in zip(r, cl)]
    at = [_decayed(x, c - l) for x, c, l in zip(a, cl, lw)]
    bt = [_decayed(x, -c) for x, c in zip(b, cl)]
    kt = [_decayed(x, -c) for x, c in zip(k, cl)]
    l_ab = [jnp.where(strict, _pair(x, y), 0.0) for x, y in zip(at, bt)]
    l_ak = [jnp.where(strict, _pair(x, y), 0.0) for x, y in zip(at, kt)]
    r_b = [jnp.where(incl, _pair(x, y), 0.0) for x, y in zip(rt, bt)]
    r_k = [jnp.where(incl, _pair(x, y), 0.0) for x, y in zip(rt, kt)]
    if inv is None:
        inv = [eye + x for x in l_ab]
        pw = l_ab
        for _ in range(int(math.log2(t)) - 1):
            pw = [_dot(x, x) for x in pw]
            inv = [x + _dot(x, y) for x, y in zip(inv, pw)]
    else:
        inv = [_known_inverse(x, y) for x, y in zip(l_ab, inv)]
    rhs = [_dot(x, s, "nt") + _dot(m, y) for x, s, m, y in zip(at, s0, l_ak, v)]
    u = [_dot(x, y) for x, y in zip(inv, rhs)]
    y_s = [_dot(x, s, "nt") for x, s in zip(rt, s0)]
    y = [ys + _dot(m, uu) + _dot(n, vv) for ys, m, uu, n, vv in zip(y_s, r_b, u, r_k, v)]
    grow = [s + _dot(uu, x, "tn") + _dot(vv, z, "tn") for s, uu, x, vv, z in zip(s0, u, bt, v, kt)]
    s1 = [g * jnp.exp(c[t - 1:t, :]) for g, c in zip(grow, cl)]
    return y, s1, inv


def _head_rows(h):
    return slice(h * RWKV_HEAD, (h + 1) * RWKV_HEAD)


def _per_head(ref):
    return [ref[:, _head_rows(h)] for h in range(RWKV_HEADS)]


def _scan(r, lw, k, v, a, b, *, name):
    lp = r.shape[0]
    nc = lp // CHUNK
    row = pl.BlockSpec((CHUNK, RWKV_DIM), lambda c: (c, 0))

    def body(r_ref, lw_ref, k_ref, v_ref, a_ref, b_ref, y_ref, s_ref, inv_ref, state):
        @pl.when(pl.program_id(0) == 0)
        def _():
            state[...] = jnp.zeros_like(state)

        s_ref[...] = state[...]
        s0 = [state[_head_rows(h), :] for h in range(RWKV_HEADS)]
        y, s1, inv = _scan_chunk(s0, *[_per_head(ref) for ref in (r_ref, lw_ref, k_ref, v_ref, a_ref, b_ref)])
        for h in range(RWKV_HEADS):
            y_ref[:, _head_rows(h)] = y[h]
            state[_head_rows(h), :] = s1[h]
            inv_ref[h * CHUNK:(h + 1) * CHUNK, :] = inv[h].astype(inv_ref.dtype)

    return pl.pallas_call(
        body, name=name, grid=(nc,), in_specs=[row] * 6,
        out_specs=[row, pl.BlockSpec((RWKV_DIM, RWKV_HEAD), lambda c: (c, 0)),
                   pl.BlockSpec((RWKV_HEADS * CHUNK, CHUNK), lambda c: (c, 0))],
        out_shape=[jax.ShapeDtypeStruct((lp, RWKV_DIM), f32), jax.ShapeDtypeStruct((nc * RWKV_DIM, RWKV_HEAD), f32),
                   jax.ShapeDtypeStruct((nc * RWKV_HEADS * CHUNK, CHUNK), bf16)],
        scratch_shapes=[pltpu.VMEM((RWKV_DIM, RWKV_HEAD), f32)],
        compiler_params=_params(("arbitrary",)),
    )(r, lw, k, v, a, b)


def _scan_bwd(r, lw, k, v, a, b, states, inverses, dy, *, name):
    lp = r.shape[0]
    nc = lp // CHUNK
    back = lambda c: (nc - 1 - c, 0)
    row = pl.BlockSpec((CHUNK, RWKV_DIM), back)

    def body(r_ref, lw_ref, k_ref, v_ref, a_ref, b_ref, s_ref, inv_ref, dy_ref,
             dr_ref, dlw_ref, dk_ref, dv_ref, da_ref, db_ref, dstate):
        @pl.when(pl.program_id(0) == 0)
        def _():
            dstate[...] = jnp.zeros_like(dstate)

        outs = (dr_ref, dlw_ref, dk_ref, dv_ref, da_ref, db_ref)
        s0 = [s_ref[_head_rows(h), :] for h in range(RWKV_HEADS)]
        inv = [inv_ref[h * CHUNK:(h + 1) * CHUNK, :].astype(f32) for h in range(RWKV_HEADS)]
        _, vjp = jax.vjp(lambda *args: _scan_chunk(*args, inv=inv)[:2], s0,
                         *[_per_head(ref) for ref in (r_ref, lw_ref, k_ref, v_ref, a_ref, b_ref)])
        g = vjp((_per_head(dy_ref), [dstate[_head_rows(h), :] for h in range(RWKV_HEADS)]))
        for h in range(RWKV_HEADS):
            dstate[_head_rows(h), :] = g[0][h]
            for o_ref, gv in zip(outs, g[1:]):
                o_ref[:, _head_rows(h)] = gv[h]

    shape = jax.ShapeDtypeStruct((lp, RWKV_DIM), f32)
    return pl.pallas_call(
        body, name=name, grid=(nc,),
        in_specs=[row] * 6 + [pl.BlockSpec((RWKV_DIM, RWKV_HEAD), back),
                              pl.BlockSpec((RWKV_HEADS * CHUNK, CHUNK), back), row],
        out_specs=[row] * 6, out_shape=[shape] * 6,
        scratch_shapes=[pltpu.VMEM((RWKV_DIM, RWKV_HEAD), f32)],
        compiler_params=_params(("arbitrary",)),
    )(r, lw, k, v, a, b, states, inverses, dy)


def _loss_head(h2, target, g_final, *, name):
    lp = h2.shape[0]
    tm = BLOCK
    front_tiles = FRONT // tm

    def body(h_ref, t_ref, g_ref, loss_ref, dh_ref, dg_ref):
        i = pl.program_id(0)
        real = i >= front_tiles

        def tile_loss(hv, gv):
            err = _rms(hv, gv) - t_ref[...]
            return jnp.where(real, 0.5 * jnp.sum(jnp.mean(err * err, axis=-1, keepdims=True)), 0.0)

        loss, (dh, dg) = jax.value_and_grad(tile_loss, argnums=(0, 1))(h_ref[...], g_ref[...])

        @pl.when(i == 0)
        def _():
            loss_ref[...] = jnp.zeros_like(loss_ref)
            dg_ref[...] = jnp.zeros_like(dg_ref)

        loss_ref[...] += jnp.full(loss_ref.shape, loss, f32)
        dg_ref[...] += dg
        dh_ref[...] = dh

    return pl.pallas_call(
        body, name=name, grid=(lp // tm,),
        in_specs=[pl.BlockSpec((tm, D_MODEL), lambda i: (i, 0)),
                  pl.BlockSpec((tm, D_MODEL), lambda i: (jnp.maximum(i - front_tiles, 0), 0)),
                  _full(g_final.shape)],
        out_specs=[_full((8, 128)), pl.BlockSpec((tm, D_MODEL), lambda i: (i, 0)), _full(g_final.shape)],
        out_shape=[jax.ShapeDtypeStruct((8, 128), f32), jax.ShapeDtypeStruct((lp, D_MODEL), f32),
                   jax.ShapeDtypeStruct(g_final.shape, f32)],
        compiler_params=_params(("arbitrary",)),
    )(h2, target, g_final)


def _local_step(x, target, meta, p, early_weights=None, late_weights=None, emit=None):
    emit = emit or (lambda group, grads: 0.0)
    seq = x.shape[0]
    lp = seq + FRONT
    h0 = jnp.concatenate([jnp.zeros((PAD, D_MODEL), f32), meta, x], axis=0)
    cos_t, sin_t, swap = _rope_tables(lp)
    hsum = _head_sum_matrix(RWKV_DIM, RWKV_HEAD)
    hmean = hsum / RWKV_HEAD
    b_qkv, b_rkv = p["b_in"][:, :ATTN_PROJ], p["b_in"][:, ATTN_PROJ:ATTN_PROJ + RKV_W]
    b_lora, b_gates = p["b_in"][:, ATTN_PROJ + RKV_W:ATTN_PROJ + RWKV_PROJ], p["b_in"][:, ATTN_PROJ + RWKV_PROJ:]
    post_params = [p["ln_w"], p["ln_b"], p["r_k"], hmean]

    (u,) = _rowwise(lambda hv, g: (_rms(hv, g),), [h0], [p["norm_mix_g"]], [(D_MODEL, bf16)], name="norm_mix")
    if early_weights is not None:
        p = {**p, **early_weights(u)}
    w_qkv_t, w_rkv_t = p["w_in_t"][:ATTN_PROJ], p["w_in_t"][ATTN_PROJ:ATTN_PROJ + RKV_W]
    w_lora_t, w_gates_t = p["w_in_t"][ATTN_PROJ + RKV_W:ATTN_PROJ + RWKV_PROJ], p["w_in_t"][ATTN_PROJ + RWKV_PROJ:]
    prep_params = [p["w0"], p["w2"], p["a0"], p["a2"], p["g2"], p["k_k"], p["k_a"], hsum]
    w_pieces = [w_qkv_t, w_rkv_t, w_lora_t, w_gates_t]
    qkv, p_rkv, p_lora, gates = _mm_fanout(u, w_pieces, [b_qkv, b_rkv, b_lora, b_gates], name="proj_in",
                                           zero_rows_below=PAD)

    q, k, v = _rowwise(_attn_prep, [qkv, cos_t, sin_t], [swap], [(Q_W, bf16), (KV_W, bf16), (KV_W, bf16)],
                       name="attn_prep")
    y_attn = _attention(q, k, v, p["sinks"], name="attention")

    mix_rkv, mix_lora = p["mix"][:, :RKV_W], p["mix"][:, RKV_W:]
    r_, lw_, k_, v_, a_, b_, g_ = _mixer_inputs([p_rkv, p_lora], [mix_rkv, mix_lora], prep_params,
                                                name="mixer_inputs")
    y_scan, states, inverses = _scan(r_, lw_, k_, v_, a_, b_, name="wkv_scan")
    (y_rwkv,) = _rowwise(_rwkv_post, [y_scan, r_, k_, v_, g_], post_params, [(RWKV_DIM, bf16)], name="rwkv_post")

    if late_weights is not None:
        p = {**p, **late_weights(y_rwkv)}
    br_a, br_r, merged = _branch_merge(y_attn, y_rwkv, p["w_br_attn_t"], p["w_br_rwkv_t"], gates, name="branch_merge")
    h1 = _mm(merged, p["w_o"], "nn", name="out_proj", add=h0)
    (f,) = _rowwise(lambda hv, g: (_rms(hv, g),), [h1], [p["norm_ffn_g"]], [(D_MODEL, bf16)], name="norm_ffn")
    gate, up, act = _ffn_in(f, p["w_gate_t"], p["w_up_t"], name="ffn_in")
    h2 = _mm(act, p["w_down"], "nn", name="ffn_down", add=h1)

    loss8, dh2, d_final_g = _loss_head(h2, target, p["norm_final_g"], name="loss_head")
    dgate, dup = _ffn_in_bwd(dh2, p["w_down"], gate, up, name="ffn_in_bwd")
    d_w_down = _mm_tn(act, dh2, name="dw_down")
    d_w_gate_t = _mm_tn(dgate, f, name="dw_gate")
    d_w_up_t = _mm_tn(dup, f, name="dw_up")
    zero = emit("ffn", dict(w_down=d_w_down, w_gate_t=d_w_gate_t, w_up_t=d_w_up_t))
    df = _mm_sum([dgate, dup], [p["w_gate_t"], p["w_up_t"]], name="d_f")
    dh1, d_ffn_g = _rowwise_bwd(lambda hv, g: (_rms(hv, g), hv), [h1], [p["norm_ffn_g"] + zero], [[df], [dh2]],
                                name="norm_ffn_bwd", diff_rows=[True], diff_params=[True])
    dgates, dbr_a, dbr_r = _branch_merge_bwd(dh1, p["w_o"], gates, br_a, br_r, name="branch_merge_bwd")
    d_w_o = _mm_tn(merged, dh1, name="dw_o")
    d_w_br_attn_t = _mm_tn(dbr_a, y_attn, name="dw_br_attn")
    d_w_br_rwkv_t = _mm_tn(dbr_r, y_rwkv, name="dw_br_rwkv")
    zero = emit("branch", dict(w_o=d_w_o, w_br_attn_t=d_w_br_attn_t, w_br_rwkv_t=d_w_br_rwkv_t))
    dy_attn = _mm(dbr_a, p["w_br_attn_t"], "nn", name="d_y_attn")
    dy_rwkv = _mm(dbr_r, p["w_br_rwkv_t"], "nn", name="d_y_rwkv")

    post_params = [p["ln_w"] + zero, p["ln_b"], p["r_k"], hmean]
    res = _rowwise_bwd(_rwkv_post, [y_scan, r_, k_, v_, g_], post_params, [[dy_rwkv]], name="rwkv_post_bwd",
                       diff_rows=[True] * 5, diff_params=[True, True, True, False])
    dy_scan, dr_p, dk_p, dv_p, dg_p, d_ln_w, d_ln_b, d_r_k = res
    dr_s, dlw_s, dk_s, dv_s, da_s, db_s = _scan_bwd(r_, lw_, k_, v_, a_, b_, states, inverses, dy_scan,
                                                    name="wkv_scan_bwd")
    res = _mixer_inputs_bwd([p_rkv, p_lora], [mix_rkv, mix_lora], prep_params,
                            [[dr_s, dr_p], [dlw_s], [dk_s, dk_p], [dv_s, dv_p], [da_s], [db_s], [dg_p]],
                            name="mixer_inputs_bwd")
    dp_rkv, dp_lora, d_mix_rkv, d_mix_lora, d_w0, d_w2, d_a0, d_a2, d_g2, d_k_k, d_k_a = res

    dq, dk, dv, dkm, dvm, d_sinks = _attention_bwd(q, k, v, p["sinks"], dy_attn, name="attention_bwd")
    rest = jnp.zeros((lp - BLOCK, KV_W), f32)
    dkm, dvm = jnp.concatenate([dkm, rest], axis=0), jnp.concatenate([dvm, rest], axis=0)
    (dqkv,) = _rowwise_bwd(_attn_prep, [qkv, cos_t, sin_t], [swap], [[dq], [dk, dkm], [dv, dvm]], name="attn_prep_bwd",
                           diff_rows=[True, False, False], diff_params=[False], out_dtypes=[bf16])

    d_w_qkv_t, db_qkv = _mm_tn(dqkv, u, name="dw_qkv", colsum=True)
    d_w_rkv_t, db_rkv = _mm_tn(dp_rkv, u, name="dw_rkv", colsum=True)
    d_w_lora_t, db_lora = _mm_tn(dp_lora, u, name="dw_lora", colsum=True)
    d_w_gates_t, db_gates = _mm_tn(dgates, u, name="dw_gates", colsum=True)
    d_w_in_t = jnp.concatenate([d_w_qkv_t, d_w_rkv_t, d_w_lora_t, d_w_gates_t], axis=0)
    zero = emit("input", dict(w_in_t=d_w_in_t, g2=d_g2, w2=d_w2, a2=d_a2))
    du = _mm_sum([dqkv, dp_rkv, dp_lora, dgates], w_pieces, name="d_u")
    dh0, d_mix_g = _rowwise_bwd(lambda hv, g: (_rms(hv, g), hv), [h0], [p["norm_mix_g"] + zero], [[du], [dh1]],
                                name="norm_mix_bwd", diff_rows=[True], diff_params=[True])

    grads = dict(
        w_in_t=d_w_in_t,
        b_in=jnp.concatenate([db_qkv, db_rkv, db_lora, db_gates], axis=1),
        mix=jnp.concatenate([d_mix_rkv, d_mix_lora], axis=1),
        norm_mix_g=d_mix_g, sinks=d_sinks, w0=d_w0, w2=d_w2, a0=d_a0, a2=d_a2, g2=d_g2, k_k=d_k_k, k_a=d_k_a,
        r_k=d_r_k, ln_w=d_ln_w, ln_b=d_ln_b, w_br_attn_t=d_w_br_attn_t, w_br_rwkv_t=d_w_br_rwkv_t, w_o=d_w_o,
        norm_ffn_g=d_ffn_g, w_gate_t=d_w_gate_t, w_up_t=d_w_up_t, w_down=d_w_down, norm_final_g=d_final_g,
        meta=dh0[PAD:FRONT],
    )
    return loss8[0, 0], dh0[FRONT:], grads


def _position():
    return lax.axis_index("x"), lax.axis_index("y"), lax.axis_index("c")


def _other_chips(x, y):
    return [(1 - x, y), (x, 1 - y), (1 - x, 1 - y)]


_HBM = pl.BlockSpec(memory_space=pltpu.HBM)
_SEM = pl.BlockSpec(memory_space=pltpu.SEMAPHORE)
_EFFECT = pltpu.SideEffectType.DATAFLOW_SIDE_EFFECTING


def _landing_zone(src, kind):
    shape = {"whole": (N_CHIPS,) + src.shape, "half": (2, N_CHIPS, src.shape[0], src.shape[1] // 2),
             "slab": (3,) + src.shape[1:]}[kind]
    return lax.empty(shape, src.dtype)


def _chip_copies(src_refs, land_refs, send_sems, recv_sems, kind):
    x, y, c = _position()
    copies = []
    for a, (src, land) in enumerate(zip(src_refs, land_refs)):
        for j, (px, py) in enumerate(_other_chips(x, y)):
            if kind == "whole":
                src_ref, dst_ref = src, land.at[2 * x + y]
            elif kind == "half":
                half = src.shape[1] // 2
                src_ref, dst_ref = src.at[:, pl.ds(pl.multiple_of(c * half, half), half)], land.at[c, 2 * x + y]
            else:
                src_ref, dst_ref = src.at[2 * px + py], land.at[j]
            copies.append(pltpu.make_async_remote_copy(
                src_ref=src_ref, dst_ref=dst_ref, send_sem=send_sems.at[3 * a + j], recv_sem=recv_sems.at[3 * a + j],
                device_id=(px, py, c), device_id_type=MESH))
    return copies


def _exchange_start(srcs, *, kind, name):
    n = len(srcs)
    lands = [_landing_zone(s, kind) for s in srcs]

    def body(*refs):
        for cp in _chip_copies(refs[:n], refs[n:2 * n], refs[2 * n], refs[2 * n + 1], kind):
            cp.start()
        refs[-1][...] = jnp.zeros_like(refs[-1])

    res = pl.pallas_call(
        body, name=name,
        out_shape=(pltpu.SemaphoreType.DMA((3 * n,)), pltpu.SemaphoreType.DMA((3 * n,)),
                   *[pltpu.HBM(a.shape, a.dtype) for a in srcs + lands], jax.ShapeDtypeStruct((8, 128), f32)),
        in_specs=[_HBM] * (2 * n),
        out_specs=(_SEM, _SEM, *[_HBM] * (2 * n), pl.BlockSpec(memory_space=pltpu.VMEM)),
        input_output_aliases={i: 2 + i for i in range(2 * n)},
        compiler_params=pltpu.CompilerParams(has_side_effects=_EFFECT),
    )(*[pltpu.with_memory_space_constraint(a, pltpu.HBM) for a in srcs + lands])
    return res[0], res[1], list(res[2:2 + n]), list(res[2 + n:2 + 2 * n]), res[-1]


def _exchange_wait(handle, after, *, kind, name):
    send_sems, recv_sems, srcs, lands, _ = handle
    n = len(srcs)

    def body(*refs):
        for cp in _chip_copies(refs[:n], refs[n:2 * n], refs[2 * n], refs[2 * n + 1], kind):
            cp.wait_send()
            cp.wait_recv()

    res = pl.pallas_call(
        body, name=name,
        out_shape=tuple(pltpu.HBM(a.shape, a.dtype) for a in srcs + lands),
        in_specs=[_HBM] * (2 * n) + [_SEM, _SEM, pl.BlockSpec(memory_space=pl.ANY)],
        out_specs=tuple([_HBM] * (2 * n)),
        input_output_aliases={i: i for i in range(2 * n)},
        compiler_params=pltpu.CompilerParams(has_side_effects=_EFFECT),
    )(*srcs, *lands, send_sems, recv_sems, after)
    return list(res[:n]), list(res[n:])


def _sum_own_and_received(g, recv, *, name):
    _, r, w = g.shape
    tm = _tile(r)
    if g.dtype == bf16 and tm % 16:
        tm = r
    x, y, _ = _position()
    me = jnp.reshape(2 * x + y, (1,)).astype(jnp.int32)

    def body(me_ref, g_ref, r_ref, o_ref):
        o_ref[...] = (g_ref[0].astype(f32) + r_ref[0].astype(f32)) + (r_ref[1].astype(f32) + r_ref[2].astype(f32))

    return pl.pallas_call(
        body, name=name,
        grid_spec=pltpu.PrefetchScalarGridSpec(
            num_scalar_prefetch=1, grid=(r // tm,),
            in_specs=[pl.BlockSpec((1, tm, w), lambda i, me_ref: (me_ref[0], i, 0)),
                      pl.BlockSpec((3, tm, w), lambda i, me_ref: (0, i, 0))],
            out_specs=pl.BlockSpec((tm, w), lambda i, me_ref: (i, 0))),
        out_shape=jax.ShapeDtypeStruct((r, w), f32),
        compiler_params=_params(("parallel",)),
    )(me, g, recv)


def _swap_cores(arrs, *, name):
    n = len(arrs)

    def body(*refs):
        x, y, c = _position()
        copies = [pltpu.make_async_remote_copy(
            src_ref=refs[i], dst_ref=refs[n + i], send_sem=refs[2 * n].at[i], recv_sem=refs[2 * n + 1].at[i],
            device_id=(x, y, 1 - c), device_id_type=MESH) for i in range(n)]
        for cp in copies:
            cp.start()
        for cp in copies:
            cp.wait_recv()
        for cp in copies:
            cp.wait_send()

    return pl.pallas_call(
        body, name=name,
        in_specs=[pl.BlockSpec(memory_space=pl.ANY)] * n,
        out_specs=[pl.BlockSpec(memory_space=pl.ANY)] * n,
        out_shape=[jax.ShapeDtypeStruct(a.shape, a.dtype) for a in arrs],
        scratch_shapes=[pltpu.SemaphoreType.DMA((n,)), pltpu.SemaphoreType.DMA((n,))],
    )(*arrs)


def _swap_halves(zone, *, name):
    def body(z_ref, o_ref, send_sems, recv_sems):
        x, y, c = _position()
        mine = [pltpu.make_async_remote_copy(
            src_ref=o_ref.at[c, 2 * px + py], dst_ref=o_ref.at[c, 2 * px + py], send_sem=send_sems.at[j],
            recv_sem=recv_sems.at[j], device_id=(x, y, 1 - c), device_id_type=MESH)
            for j, (px, py) in enumerate(_other_chips(x, y))]
        for cp in mine:
            cp.start()
        for j, (px, py) in enumerate(_other_chips(x, y)):
            pltpu.make_async_remote_copy(
                src_ref=o_ref.at[c, 2 * px + py], dst_ref=o_ref.at[1 - c, 2 * px + py], send_sem=send_sems.at[j],
                recv_sem=recv_sems.at[j], device_id=(x, y, 1 - c), device_id_type=MESH).wait_recv()
        for cp in mine:
            cp.wait_send()

    return pl.pallas_call(
        body, name=name,
        in_specs=[pl.BlockSpec(memory_space=pl.ANY)], out_specs=pl.BlockSpec(memory_space=pl.ANY),
        out_shape=jax.ShapeDtypeStruct(zone.shape, zone.dtype), input_output_aliases={0: 0},
        scratch_shapes=[pltpu.SemaphoreType.DMA((3,)), pltpu.SemaphoreType.DMA((3,))],
    )(zone)


def _all_reduce_small(a, after, *, name):
    rows, w = a.shape

    def body(a_ref, after_ref, o_ref, buf, send_sems, recv_sems):
        x, y, c = _position()
        me = 4 * x + 2 * y + c
        buf[0] = a_ref[...]
        sends = []
        for rel in range(1, N_DEV):
            peer = ((1 - x) if rel & 4 else x, (1 - y) if rel & 2 else y, (1 - c) if rel & 1 else c)
            cp = pltpu.make_async_remote_copy(
                src_ref=a_ref, dst_ref=buf.at[rel], send_sem=send_sems.at[rel - 1], recv_sem=recv_sems.at[rel - 1],
                device_id=peer, device_id_type=MESH)
            cp.start()
            sends.append(cp)
        for cp in sends:
            cp.wait_recv()
        for cp in sends:
            cp.wait_send()
        acc = buf[jnp.bitwise_xor(me, 0)]
        for d in range(1, N_DEV):
            acc = acc + buf[jnp.bitwise_xor(me, d)]
        o_ref[...] = acc

    return pl.pallas_call(
        body, name=name,
        in_specs=[pl.BlockSpec(memory_space=pltpu.VMEM), pl.BlockSpec(memory_space=pl.ANY)],
        out_specs=pl.BlockSpec(memory_space=pltpu.VMEM),
        out_shape=jax.ShapeDtypeStruct((rows, w), f32),
        scratch_shapes=[pltpu.VMEM((N_DEV, rows, w), f32), pltpu.SemaphoreType.DMA((N_DEV - 1,)),
                        pltpu.SemaphoreType.DMA((N_DEV - 1,))],
    )(a, after)


def _adamw(w, g_parts, m, v, *, name, transposed=False):
    rows, cols = w.shape
    if transposed:
        tm = 256 if rows % 256 == 0 else rows
        g_spec = pl.BlockSpec((cols, tm), lambda i: (0, i))
    else:
        tm = _tile(rows, 256)
        g_spec = pl.BlockSpec((tm, cols), lambda i: (i, 0))
    n = len(g_parts)

    def body(*refs):
        w_ref, m_ref, v_ref = refs[0], refs[1 + n], refs[2 + n]
        g_ref, d_ref, nm_ref, nv_ref = refs[3 + n:]
        gv = refs[1][...]
        for part in refs[2:1 + n]:
            gv = gv + part[...]
        if transposed:
            gv = gv.T
        g_ref[...] = gv
        nm = ADAM_B1 * m_ref[...] + (1.0 - ADAM_B1) * gv
        nv = ADAM_B2 * v_ref[...] + (1.0 - ADAM_B2) * (gv * gv)
        m_hat = nm / (1.0 - ADAM_B1 ** ADAM_STEP)
        v_hat = nv / (1.0 - ADAM_B2 ** ADAM_STEP)
        d_ref[...] = -ADAM_LR * (m_hat / (jnp.sqrt(v_hat) + ADAM_EPS) + ADAM_WD * w_ref[...])
        nm_ref[...] = nm
        nv_ref[...] = nv

    spec = pl.BlockSpec((tm, cols), lambda i: (i, 0))
    shape = jax.ShapeDtypeStruct((rows, cols), f32)
    return pl.pallas_call(
        body, name=name, grid=(rows // tm,), in_specs=[spec] + [g_spec] * n + [spec] * 2,
        out_specs=[spec] * 4, out_shape=[shape] * 4,
        compiler_params=_params(("parallel",)),
    )(w, *g_parts, m, v)


def _pad_rows(a, rows):
    return jnp.concatenate([a, jnp.zeros((rows - a.shape[0], a.shape[1]), a.dtype)], axis=0) if rows > a.shape[0] else a


_SMALL = (("norm_mix_g", D_MODEL), ("b_in", D_IN), ("sinks", Q_HEADS), ("mix", RWKV_PROJ), ("w0", RWKV_DIM),
          ("a0", RWKV_DIM), ("k_k", RWKV_DIM), ("k_a", RWKV_DIM), ("r_k", RWKV_DIM), ("ln_w", RWKV_DIM),
          ("ln_b", RWKV_DIM), ("norm_ffn_g", D_MODEL), ("norm_final_g", D_MODEL))


def _pack_small(d):
    flat = jnp.concatenate([d[n].reshape(-1).astype(f32) for n, _ in _SMALL])
    return flat


def _unpack_small(flat):
    out, off = {}, 0
    for n, size in _SMALL:
        out[n] = flat[off:off + size]
        off += size
    return out


_SMALL_TOTAL = sum(s for _, s in _SMALL)


def kernel(x, meta_tokens, norm_mix_g, w_in, b_in, attn_sinks, rwkv_mix, rwkv_w0, rwkv_w2, rwkv_a0, rwkv_a2, rwkv_g2, rwkv_k_k, rwkv_k_a, rwkv_r_k, rwkv_ln_w, rwkv_ln_b, w_br_attn, w_br_rwkv, w_o, norm_ffn_g, w_ffn_gate, w_ffn_up, w_ffn_down, norm_final_g, loss_target, m_meta_tokens, m_norm_mix_g, m_w_in, m_b_in, m_attn_sinks, m_rwkv_mix, m_rwkv_w0, m_rwkv_w2, m_rwkv_a0, m_rwkv_a2, m_rwkv_g2, m_rwkv_k_k, m_rwkv_k_a, m_rwkv_r_k, m_rwkv_ln_w, m_rwkv_ln_b, m_w_br_attn, m_w_br_rwkv, m_w_o, m_norm_ffn_g, m_w_ffn_gate, m_w_ffn_up, m_w_ffn_down, m_norm_final_g, v_meta_tokens, v_norm_mix_g, v_w_in, v_b_in, v_attn_sinks, v_rwkv_mix, v_rwkv_w0, v_rwkv_w2, v_rwkv_a0, v_rwkv_a2, v_rwkv_g2, v_rwkv_k_k, v_rwkv_k_a, v_rwkv_r_k, v_rwkv_ln_w, v_rwkv_ln_b, v_w_br_attn, v_w_br_rwkv, v_w_o, v_norm_ffn_g, v_w_ffn_gate, v_w_ffn_up, v_w_ffn_down, v_norm_final_g):
    names = ("meta_tokens", "norm_mix_g", "w_in", "b_in", "attn_sinks", "rwkv_mix", "rwkv_w0", "rwkv_w2", "rwkv_a0",
             "rwkv_a2", "rwkv_g2", "rwkv_k_k", "rwkv_k_a", "rwkv_r_k", "rwkv_ln_w", "rwkv_ln_b", "w_br_attn",
             "w_br_rwkv", "w_o", "norm_ffn_g", "w_ffn_gate", "w_ffn_up", "w_ffn_down", "norm_final_g")
    w_all = dict(zip(names, (meta_tokens, norm_mix_g, w_in, b_in, attn_sinks, rwkv_mix, rwkv_w0, rwkv_w2, rwkv_a0,
                             rwkv_a2, rwkv_g2, rwkv_k_k, rwkv_k_a, rwkv_r_k, rwkv_ln_w, rwkv_ln_b, w_br_attn,
                             w_br_rwkv, w_o, norm_ffn_g, w_ffn_gate, w_ffn_up, w_ffn_down, norm_final_g)))
    m_all = dict(zip(names, (m_meta_tokens, m_norm_mix_g, m_w_in, m_b_in, m_attn_sinks, m_rwkv_mix, m_rwkv_w0,
                             m_rwkv_w2, m_rwkv_a0, m_rwkv_a2, m_rwkv_g2, m_rwkv_k_k, m_rwkv_k_a, m_rwkv_r_k,
                             m_rwkv_ln_w, m_rwkv_ln_b, m_w_br_attn, m_w_br_rwkv, m_w_o, m_norm_ffn_g, m_w_ffn_gate,
                             m_w_ffn_up, m_w_ffn_down, m_norm_final_g)))
    v_all = dict(zip(names, (v_meta_tokens, v_norm_mix_g, v_w_in, v_b_in, v_attn_sinks, v_rwkv_mix, v_rwkv_w0,
                             v_rwkv_w2, v_rwkv_a0, v_rwkv_a2, v_rwkv_g2, v_rwkv_k_k, v_rwkv_k_a, v_rwkv_r_k,
                             v_rwkv_ln_w, v_rwkv_ln_b, v_w_br_attn, v_w_br_rwkv, v_w_o, v_norm_ffn_g, v_w_ffn_gate,
                             v_w_ffn_up, v_w_ffn_down, v_norm_final_g)))
    cx, cy, _ = _position()
    chip = 2 * cx + cy

    t_of = dict(w_in_t="w_in", w_gate_t="w_ffn_gate", w_up_t="w_ffn_up", w_br_attn_t="w_br_attn",
                w_br_rwkv_t="w_br_rwkv", g2_t="rwkv_g2", w2_t="rwkv_w2", a2_t="rwkv_a2")
    plain_of = dict(w_down="w_ffn_down", w_o="w_o")
    meta_cols = meta_tokens.shape[1]

    def shard(k):
        return (w_all[t_of[k]][0].T if k in t_of else w_all[plain_of[k]][0]).astype(bf16)

    def whole(zone, own):
        return lax.dynamic_update_slice_in_dim(zone, own[None], chip, axis=0).reshape(-1, own.shape[-1])

    tiny = ("g2_t", "w2_t", "a2_t")
    late = ("w_gate_t", "w_up_t", "w_down", "w_o", "w_br_attn_t", "w_br_rwkv_t")
    w_in_own = shard("w_in_t")
    w_in_rows, w_in_cols = w_in_own.shape
    tiny_h = _exchange_start([shard(k) for k in tiny] + [meta_tokens], kind="whole", name="gather_tiny_start")
    w_in_h = _exchange_start([w_in_own + tiny_h[4][0, 0].astype(bf16)], kind="half", name="gather_w_in_start")
    behind = w_in_h[4][0, 0].astype(bf16)
    late_h = _exchange_start([shard(k) + behind for k in late], kind="whole", name="gather_late_start")
    own, zones = _exchange_wait(tiny_h, late_h[4], kind="whole", name="gather_tiny_wait")
    got = {k: whole(z, o) for k, z, o in zip(tiny, zones, own)}
    meta_full = whole(zones[-1], own[-1]).reshape(N_CHIPS, N_META, meta_cols).transpose(1, 0, 2).reshape(N_META, -1)
    p = dict(
        g2=got["g2_t"].T.astype(f32), w2=got["w2_t"].T.astype(f32), a2=got["a2_t"].T.astype(f32),
        b_in=b_in, sinks=attn_sinks, mix=rwkv_mix, w0=rwkv_w0, a0=rwkv_a0, k_k=rwkv_k_k, k_a=rwkv_k_a,
        r_k=rwkv_r_k.reshape(1, RWKV_DIM), ln_w=rwkv_ln_w, ln_b=rwkv_ln_b, norm_mix_g=norm_mix_g,
        norm_ffn_g=norm_ffn_g, norm_final_g=norm_final_g.reshape(1, D_MODEL),
    )

    def early_weights(after):
        own_h, zones_h = _exchange_wait(w_in_h, after, kind="half", name="gather_w_in_wait")
        zone = _swap_halves(zones_h[0], name="swap_w_in_halves")
        own_halves = own_h[0].reshape(w_in_rows, 2, w_in_cols // 2).transpose(1, 0, 2)[:, None]
        zone = lax.dynamic_update_slice(zone, own_halves, (0, chip, 0, 0))
        return dict(w_in_t=zone.transpose(1, 2, 0, 3).reshape(N_CHIPS * w_in_rows, w_in_cols))

    def late_weights(after):
        own_l, zones_l = _exchange_wait(late_h, after, kind="whole", name="gather_late_wait")
        return {k: whole(z, o) for k, z, o in zip(late, zones_l, own_l)}

    started = {}

    def emit(group, grads_):
        keys = list(grads_)
        slabs = []
        for k in keys:
            a = grads_[k].T if k in ("g2", "w2", "a2") else grads_[k]
            slabs.append(a.reshape(N_CHIPS, a.shape[0] // N_CHIPS, a.shape[1]))
        started[group] = (keys, _exchange_start(slabs, kind="slab", name="scatter_" + group + "_start"))
        return started[group][1][4][0, 0]

    loss, dx, g = _local_step(x[0], loss_target[0], meta_full, p, early_weights, late_weights, emit)

    grads, delta, new_m, new_v = {}, {}, {}, {}
    in_grad_layout = ("w_in_t", "w_gate_t", "w_up_t")
    weight_of = {**t_of, **plain_of}

    def finish(groups, after, tag):
        parts = {}
        for group in groups:
            keys, handle = started[group]
            slabs, lands = _exchange_wait(handle, after, kind="slab", name="scatter_" + group + "_wait")
            parts.update({k: _sum_own_and_received(s, l, name="sum_chips_" + k) for k, s, l in zip(keys, slabs, lands)})
        keys = list(parts)
        others = dict(zip(keys, _swap_cores([parts[k] for k in keys], name="swap_cores_" + tag)))
        for k in keys:
            both = [parts[k], others[k]]
            k = k + "_t" if k in ("g2", "w2", "a2") else k
            n = weight_of[k]
            shape2 = w_all[n].shape[1:]
            w_, m_, v_ = (a.reshape(shape2) for a in (w_all[n], m_all[n], v_all[n]))
            if k in in_grad_layout:
                res = [t.T for t in _adamw(w_.T, both, m_.T, v_.T, name="adamw_" + n)]
            else:
                res = _adamw(w_, both, m_, v_, name="adamw_" + n, transposed=k in t_of)
            grads[n], delta[n], new_m[n], new_v[n] = (t.reshape(w_all[n].shape) for t in res)
        return delta[n]

    done = finish(("ffn", "branch"), dx, "a")

    small = jnp.concatenate([_pack_small(g), loss.reshape(1)])
    small_rows = -(-small.shape[0] // PACK_W)
    small = jnp.concatenate([small, jnp.zeros((small_rows * PACK_W - small.shape[0],), f32)]).reshape(small_rows, PACK_W)
    small_rows8 = -(-(small_rows + N_META) // 8) * 8
    reduced = _all_reduce_small(_pad_rows(jnp.concatenate([g["meta"], small], axis=0), small_rows8), done,
                                name="reduce_small")
    finish(("input",), reduced, "b")
    g_meta = lax.dynamic_slice_in_dim(reduced[:N_META], chip * meta_cols, meta_cols, axis=1)
    flat = reduced[N_META:N_META + small_rows].reshape(-1)
    g_small = _unpack_small(flat)
    loss_total = flat[_SMALL_TOTAL]

    small_of = dict(norm_mix_g="norm_mix_g", b_in="b_in", attn_sinks="sinks", rwkv_mix="mix", rwkv_w0="w0",
                    rwkv_a0="a0", rwkv_k_k="k_k", rwkv_k_a="k_a", rwkv_r_k="r_k", rwkv_ln_w="ln_w",
                    rwkv_ln_b="ln_b", norm_ffn_g="norm_ffn_g", norm_final_g="norm_final_g")
    grads["meta_tokens"] = g_meta
    for n, k in small_of.items():
        grads[n] = g_small[k].reshape(w_all[n].shape)

    rest = [n for n in names if n not in delta]

    def pack_rest(src):
        flat_ = jnp.concatenate([src[n].reshape(-1) for n in rest])
        rows_ = -(-flat_.shape[0] // (8 * PACK_W)) * 8
        return jnp.concatenate([flat_, jnp.ones((rows_ * PACK_W - flat_.shape[0],), f32)]).reshape(rows_, PACK_W)

    _, d_, m_, v_ = _adamw(pack_rest(w_all), [pack_rest(grads)], pack_rest(m_all), pack_rest(v_all),
                           name="adamw_small")
    off = 0
    for n in rest:
        size = w_all[n].size
        for dst, src in ((delta, d_), (new_m, m_), (new_v, v_)):
            dst[n] = src.reshape(-1)[off:off + size].reshape(w_all[n].shape)
        off += size

    return (loss_total, dx.reshape(x.shape), *[grads[n] for n in names], *[delta[n] for n in names],
            *[new_m[n] for n in names], *[new_v[n] for n in names])
```

```python
import math

import jax
import jax.numpy as jnp
from jax import lax
from jax.experimental import pallas as pl
from jax.experimental.pallas import tpu as pltpu

f32 = jnp.float32
bf16 = jnp.bfloat16

D_MODEL = 1024
N_META = 16
HEAD_DIM = 64
Q_HEADS = 8
KV_HEADS = 2
GROUP = Q_HEADS // KV_HEADS
WINDOW = 128
BLOCK = 128
ROPE_THETA = 500000.0
ROPE_DIM = HEAD_DIM // 4
RWKV_HEADS = 8
RWKV_HEAD = 64
RWKV_DIM = RWKV_HEADS * RWKV_HEAD
DECAY_LORA = 64
AAA_LORA = 64
GATE_LORA = 160
LORA_W = DECAY_LORA + AAA_LORA + GATE_LORA
RWKV_LN_EPS = 64e-5
D_FF = 2816
Q_W = Q_HEADS * HEAD_DIM
KV_W = KV_HEADS * HEAD_DIM
ATTN_PROJ = Q_W + 2 * KV_W
RKV_W = 3 * RWKV_DIM
RWKV_PROJ = RKV_W + LORA_W
D_IN = ATTN_PROJ + RWKV_PROJ + 2 * D_MODEL
RMS_EPS = 1e-6
NEG_INF = -1e30
PAD = BLOCK - N_META
FRONT = PAD + N_META

ADAM_LR = 0.001
ADAM_B1 = 0.9
ADAM_B2 = 0.999
ADAM_EPS = 1e-08
ADAM_WD = 0.01
ADAM_STEP = 10

N_CHIPS = 4
N_DEV = 8
CHUNK = 128
VMEM_LIMIT = 56 * 1024 * 1024
PACK_W = 1024
MESH = pl.DeviceIdType.MESH


def _tile(m, pref=384):
    for step in (16, 8):
        for t in range(min(m, pref) // step * step, 0, -step):
            if m % t == 0:
                return t
    return m


def _params(sem=None):
    return pltpu.CompilerParams(dimension_semantics=sem, vmem_limit_bytes=VMEM_LIMIT)


def _full(shape):
    nd = len(shape)
    return pl.BlockSpec(shape, lambda *_: (0,) * nd)


def _dot(a, b, dims="nn"):
    dn = {"nn": (((1,), (0,)), ((), ())), "nt": (((1,), (1,)), ((), ())), "tn": (((0,), (0,)), ((), ()))}[dims]
    return lax.dot_general(a.astype(bf16), b.astype(bf16), dn, preferred_element_type=f32)


def _two_pass(x, m):
    x_hi = x.astype(bf16)
    x_lo = (x - x_hi.astype(f32)).astype(bf16)
    return _dot(x_hi, m) + _dot(x_lo, m)


@jax.custom_vjp
def _dot_const(x, m):
    return _two_pass(x, m)


def _dot_const_fwd(x, m):
    return _two_pass(x, m), m


def _dot_const_bwd(m, ct):
    return _two_pass(ct, m.T), jnp.zeros_like(m)


_dot_const.defvjp(_dot_const_fwd, _dot_const_bwd)


def _two_pass_left(m, x, dims):
    x_hi = x.astype(bf16)
    x_lo = (x - x_hi.astype(f32)).astype(bf16)
    return _dot(m, x_hi, dims) + _dot(m, x_lo, dims)


@jax.custom_vjp
def _const_dot(m, x):
    return _two_pass_left(m, x, "nn")


def _const_dot_fwd(m, x):
    return _two_pass_left(m, x, "nn"), m


def _const_dot_bwd(m, ct):
    return jnp.zeros_like(m), _two_pass_left(m, ct, "tn")


_const_dot.defvjp(_const_dot_fwd, _const_dot_bwd)


def _mm(a, b, mode, *, name, out_dtype=f32, bias=None, add=None, zero_rows_below=0):
    m, _ = a.shape
    n = b.shape[1] if mode == "nn" else b.shape[0]
    tm = _tile(m)
    has_bias, has_add = bias is not None, add is not None

    def body(*refs):
        a_ref, b_ref = refs[0], refs[1]
        o_ref = refs[-1]
        acc = _dot(a_ref[...], b_ref[...], mode)
        k = 2
        if has_bias:
            acc = acc + refs[k][...]
            k += 1
        if zero_rows_below:
            rows = pl.program_id(0) * tm + lax.broadcasted_iota(jnp.int32, acc.shape, 0)
            acc = jnp.where(rows >= zero_rows_below, acc, 0.0)
        if has_add:
            acc = acc + refs[k][...].astype(f32)
        o_ref[...] = acc.astype(out_dtype)

    ins = [a, b]
    in_specs = [pl.BlockSpec((tm, a.shape[1]), lambda i: (i, 0)), _full(b.shape)]
    if has_bias:
        ins.append(bias)
        in_specs.append(_full(bias.shape))
    if has_add:
        ins.append(add)
        in_specs.append(pl.BlockSpec((tm, n), lambda i: (i, 0)))
    return pl.pallas_call(
        body, name=name, grid=(m // tm,), in_specs=in_specs,
        out_specs=pl.BlockSpec((tm, n), lambda i: (i, 0)),
        out_shape=jax.ShapeDtypeStruct((m, n), out_dtype),
        compiler_params=_params(("parallel",)),
    )(*ins)


def _mm_sum(a_list, b_list, *, name, out_dtype=bf16):
    m = a_list[0].shape[0]
    n = b_list[0].shape[1]
    k = len(a_list)
    tm = _tile(m)

    def body(*refs):
        acc = _dot(refs[0][...], refs[k][...])
        for i in range(1, k):
            acc = acc + _dot(refs[i][...], refs[k + i][...])
        refs[-1][...] = acc.astype(out_dtype)

    return pl.pallas_call(
        body, name=name, grid=(m // tm,),
        in_specs=[pl.BlockSpec((tm, a.shape[1]), lambda i: (i, 0)) for a in a_list] + [_full(b.shape) for b in b_list],
        out_specs=pl.BlockSpec((tm, n), lambda i: (i, 0)),
        out_shape=jax.ShapeDtypeStruct((m, n), out_dtype),
        compiler_params=_params(("parallel",)),
    )(*a_list, *b_list)


def _mm_fanout(a, b_list, bias_list, *, name, zero_rows_below=0):
    m, kdim = a.shape
    k = len(b_list)
    tm = _tile(m)

    def body(*refs):
        av = refs[0][...]
        for j in range(k):
            acc = _dot(av, refs[1 + j][...], "nt") + refs[1 + k + j][...]
            if zero_rows_below:
                rows = pl.program_id(0) * tm + lax.broadcasted_iota(jnp.int32, acc.shape, 0)
                acc = jnp.where(rows >= zero_rows_below, acc, 0.0)
            refs[1 + 2 * k + j][...] = acc.astype(bf16)

    return pl.pallas_call(
        body, name=name, grid=(m // tm,),
        in_specs=[pl.BlockSpec((tm, kdim), lambda i: (i, 0))] + [_full(b.shape) for b in b_list]
        + [_full(c.shape) for c in bias_list],
        out_specs=[pl.BlockSpec((tm, b.shape[0]), lambda i: (i, 0)) for b in b_list],
        out_shape=[jax.ShapeDtypeStruct((m, b.shape[0]), bf16) for b in b_list],
        compiler_params=_params(("parallel",)),
    )(a, *b_list, *bias_list)


def _mm_tn(a, b, *, name, colsum=False, out_dtype=bf16):
    r, m = a.shape
    n = b.shape[1]
    tr = _tile(r, 1408)
    tmo = m
    for cand in (1408, 1024, 768, 512):
        if m > 1024 and m % cand == 0:
            tmo = cand
            break
    steps = r // tr

    def body(a_ref, b_ref, o_ref, *rest):
        acc = rest[-1]
        i = pl.program_id(1)

        @pl.when(i == 0)
        def _():
            acc[...] = jnp.zeros_like(acc)
            if colsum:
                rest[0][...] = jnp.zeros_like(rest[0])

        acc[...] += _dot(a_ref[...], b_ref[...], "tn")
        if colsum:
            rest[0][...] += jnp.sum(a_ref[...].astype(f32), axis=0, keepdims=True)

        @pl.when(i == steps - 1)
        def _():
            o_ref[...] = acc[...].astype(out_dtype)

    out_shape = [jax.ShapeDtypeStruct((m, n), out_dtype)]
    out_specs = [pl.BlockSpec((tmo, n), lambda j, i: (j, 0))]
    if colsum:
        out_shape.append(jax.ShapeDtypeStruct((1, m), f32))
        out_specs.append(pl.BlockSpec((1, tmo), lambda j, i: (0, j)))
    res = pl.pallas_call(
        body, name=name, grid=(m // tmo, steps),
        in_specs=[pl.BlockSpec((tr, tmo), lambda j, i: (i, j)), pl.BlockSpec((tr, n), lambda j, i: (i, 0))],
        out_specs=out_specs, out_shape=out_shape,
        scratch_shapes=[pltpu.VMEM((tmo, n), f32)],
        compiler_params=_params(("parallel", "arbitrary")),
    )(a, b)
    return res if colsum else res[0]


def _rowwise(fn, rows, params, outs, *, name, tm=None):
    m = rows[0].shape[0]
    tm = tm or _tile(m)
    nr, npar = len(rows), len(params)

    def body(*refs):
        vals = [r[...] for r in refs[:nr + npar]]
        res = fn(*vals)
        for o_ref, v in zip(refs[nr + npar:], res):
            o_ref[...] = v.astype(o_ref.dtype)

    return pl.pallas_call(
        body, name=name, grid=(m // tm,),
        in_specs=[pl.BlockSpec((tm, r.shape[1]), lambda i: (i, 0)) for r in rows] + [_full(p.shape) for p in params],
        out_specs=[pl.BlockSpec((tm, w), lambda i: (i, 0)) for w, _ in outs],
        out_shape=[jax.ShapeDtypeStruct((m, w), dt) for w, dt in outs],
        compiler_params=_params(("parallel",)),
    )(*rows, *params)


def _rowwise_bwd(fn, rows, params, cts, *, name, diff_rows, diff_params, tm=None, zero_rows_below=0, out_dtypes=None):
    m = rows[0].shape[0]
    tm = tm or _tile(m)
    nr, npar = len(rows), len(params)
    d_idx = [i for i in range(nr) if diff_rows[i]]
    p_idx = [i for i in range(npar) if diff_params[i]]
    out_dtypes = out_dtypes or [f32] * len(d_idx)
    flat_cts = [c for group in cts for c in group]
    n_ct = len(flat_cts)

    def body(*refs):
        vals = [r[...] for r in refs[:nr + npar]]
        ct_refs = refs[nr + npar:nr + npar + n_ct]
        out_refs = refs[nr + npar + n_ct:]
        ct_vals, k = [], 0
        for group in cts:
            acc = ct_refs[k][...].astype(f32)
            for extra in range(1, len(group)):
                acc = acc + ct_refs[k + extra][...].astype(f32)
            k += len(group)
            if zero_rows_below:
                rr = pl.program_id(0) * tm + lax.broadcasted_iota(jnp.int32, acc.shape, 0)
                acc = jnp.where(rr >= zero_rows_below, acc, 0.0)
            ct_vals.append(acc)

        def g(*dargs):
            full = list(vals)
            for pos, i in enumerate(d_idx):
                full[i] = dargs[pos]
            for pos, i in enumerate(p_idx):
                full[nr + i] = dargs[len(d_idx) + pos]
            return tuple(fn(*full))

        _, vjp = jax.vjp(g, *[vals[i].astype(f32) for i in d_idx], *[vals[nr + i] for i in p_idx])
        grads = vjp(tuple(ct_vals))
        for pos in range(len(d_idx)):
            out_refs[pos][...] = grads[pos].astype(out_refs[pos].dtype)
        first = pl.program_id(0) == 0
        for pos in range(len(p_idx)):
            o_ref = out_refs[len(d_idx) + pos]

            @pl.when(first)
            def _(o_ref=o_ref):
                o_ref[...] = jnp.zeros_like(o_ref)

            o_ref[...] += grads[len(d_idx) + pos]

    return pl.pallas_call(
        body, name=name, grid=(m // tm,),
        in_specs=[pl.BlockSpec((tm, r.shape[1]), lambda i: (i, 0)) for r in rows] + [_full(p.shape) for p in params]
        + [pl.BlockSpec((tm, c.shape[1]), lambda i: (i, 0)) for c in flat_cts],
        out_specs=[pl.BlockSpec((tm, rows[i].shape[1]), lambda i_: (i_, 0)) for i in d_idx]
        + [_full(params[i].shape) for i in p_idx],
        out_shape=[jax.ShapeDtypeStruct(rows[i].shape, dt) for i, dt in zip(d_idx, out_dtypes)]
        + [jax.ShapeDtypeStruct(params[i].shape, f32) for i in p_idx],
        compiler_params=_params(("arbitrary",)),
    )(*rows, *params, *flat_cts)


def _rms(x, g):
    return x * lax.rsqrt(jnp.mean(x * x, axis=-1, keepdims=True) + RMS_EPS) * g


def _head_sum_matrix(width, head):
    idx = jnp.arange(width) // head
    return (idx[:, None] == idx[None, :]).astype(f32)


def _rope_tables(lp):
    half = ROPE_DIM // 2
    pos = (jnp.arange(lp) - PAD).astype(f32)
    inv_freq = jnp.power(jnp.float32(ROPE_THETA), -jnp.arange(half, dtype=f32) * (2.0 / ROPE_DIM))
    ang = pos[:, None] * inv_freq[None, :]
    cos, sin = jnp.cos(ang), jnp.sin(ang)
    ones = jnp.ones((lp, HEAD_DIM - ROPE_DIM), f32)
    zeros = jnp.zeros((lp, HEAD_DIM - ROPE_DIM), f32)
    cos_t = jnp.concatenate([cos, cos, ones], axis=1)
    sin_t = jnp.concatenate([-sin, sin, zeros], axis=1)
    i = jnp.arange(HEAD_DIM)
    src = jnp.where(i < half, i + half, jnp.where(i < ROPE_DIM, i - half, i))
    swap = ((i[:, None] == src[None, :]) & (i[None, :] < ROPE_DIM)).astype(f32)
    return cos_t, sin_t, swap


def _attn_prep(qkv, cos_t, sin_t, swap):
    outs = []
    for h in range(Q_HEADS + KV_HEADS):
        t = qkv[:, h * HEAD_DIM:(h + 1) * HEAD_DIM]
        outs.append(t * cos_t + _dot_const(t, swap) * sin_t)
    q = jnp.concatenate(outs[:Q_HEADS], axis=1)
    k = jnp.concatenate(outs[Q_HEADS:], axis=1)
    return q, k, qkv[:, Q_W + KV_W:]


def _softplus(z):
    return jnp.maximum(z, 0.0) + jnp.log1p(jnp.exp(-jnp.abs(z)))


def _rwkv_prep(rkv, lora, w0, w2, a0, a2, g2, k_k, k_a, hsum):
    r = rkv[:, :RWKV_DIM]
    k = rkv[:, RWKV_DIM:2 * RWKV_DIM]
    v = rkv[:, 2 * RWKV_DIM:]
    dw = lora[:, :DECAY_LORA]
    da = lora[:, DECAY_LORA:DECAY_LORA + AAA_LORA]
    dg = lora[:, DECAY_LORA + AAA_LORA:]
    w = -_softplus(-(w0 + _dot(jnp.tanh(dw), w2))) - 0.5
    a = jax.nn.sigmoid(a0 + _dot(da, a2))
    g = _dot(jax.nn.sigmoid(dg), g2)
    kk = k * k_k
    kk = kk * lax.rsqrt(jnp.maximum(_dot_const(kk * kk, hsum), 1e-24))
    k = k * (1.0 + (a - 1.0) * k_a)
    log_decay = -jnp.exp(w)
    return r, log_decay, k, v, -kk, kk * a, g


def _rwkv_post(y, r, k, v, g, ln_w, ln_b, r_k, hmean):
    hsum = hmean * RWKV_HEAD
    mean = _dot_const(y, hmean)
    yc = y - mean
    var = _dot_const(yc * yc, hmean)
    yn = yc * lax.rsqrt(var + RWKV_LN_EPS) * ln_w + ln_b
    bonus = _dot_const(r * k * r_k, hsum) * v
    return ((yn + bonus) * g,)


def _merge(gates, br_a, br_r):
    sg = jax.nn.sigmoid(gates)
    return (sg[:, :D_MODEL] * br_a + sg[:, D_MODEL:] * br_r,)


def _swiglu(gate, up):
    return (jax.nn.silu(gate) * up,)


def _ffn_in(f, w_gate_t, w_up_t, *, name):
    m, d = f.shape
    n = w_gate_t.shape[0]
    tm = _tile(m)

    def body(f_ref, wg_ref, wu_ref, g_ref, u_ref, a_ref):
        g = _dot(f_ref[...], wg_ref[...], "nt")
        u = _dot(f_ref[...], wu_ref[...], "nt")
        g_ref[...] = g.astype(g_ref.dtype)
        u_ref[...] = u.astype(u_ref.dtype)
        a_ref[...] = _swiglu(g, u)[0].astype(a_ref.dtype)

    spec = pl.BlockSpec((tm, n), lambda i: (i, 0))
    return pl.pallas_call(
        body, name=name, grid=(m // tm,),
        in_specs=[pl.BlockSpec((tm, d), lambda i: (i, 0)), _full(w_gate_t.shape), _full(w_up_t.shape)],
        out_specs=[spec] * 3, out_shape=[jax.ShapeDtypeStruct((m, n), bf16)] * 3,
        compiler_params=_params(("parallel",)),
    )(f, w_gate_t, w_up_t)


def _branch_merge(y_attn, y_rwkv, w_attn_t, w_rwkv_t, gates, *, name):
    m = y_attn.shape[0]
    tm = _tile(m)

    def body(ya_ref, yr_ref, wa_ref, wr_ref, g_ref, a_ref, r_ref, o_ref):
        br_a = _dot(ya_ref[...], wa_ref[...], "nt")
        br_r = _dot(yr_ref[...], wr_ref[...], "nt")
        a_ref[...] = br_a.astype(a_ref.dtype)
        r_ref[...] = br_r.astype(r_ref.dtype)
        o_ref[...] = _merge(g_ref[...].astype(f32), br_a, br_r)[0].astype(o_ref.dtype)

    rows = lambda a: pl.BlockSpec((tm, a.shape[1]), lambda i: (i, 0))
    spec = pl.BlockSpec((tm, D_MODEL), lambda i: (i, 0))
    return pl.pallas_call(
        body, name=name, grid=(m // tm,),
        in_specs=[rows(y_attn), rows(y_rwkv), _full(w_attn_t.shape), _full(w_rwkv_t.shape), rows(gates)],
        out_specs=[spec] * 3, out_shape=[jax.ShapeDtypeStruct((m, D_MODEL), bf16)] * 3,
        compiler_params=_params(("parallel",)),
    )(y_attn, y_rwkv, w_attn_t, w_rwkv_t, gates)


def _branch_merge_bwd(dh, w_o, gates, br_a, br_r, *, name):
    m = dh.shape[0]
    tm = _tile(m)

    def body(dh_ref, w_ref, g_ref, a_ref, r_ref, dg_ref, da_ref, dr_ref):
        dmerged = _dot(dh_ref[...], w_ref[...], "nt")
        _, vjp = jax.vjp(lambda g, a, r: _merge(g, a, r)[0], g_ref[...].astype(f32), a_ref[...].astype(f32),
                         r_ref[...].astype(f32))
        dg, da, dr = vjp(dmerged)
        dg_ref[...] = dg.astype(dg_ref.dtype)
        da_ref[...] = da.astype(da_ref.dtype)
        dr_ref[...] = dr.astype(dr_ref.dtype)

    rows = lambda a: pl.BlockSpec((tm, a.shape[1]), lambda i: (i, 0))
    return pl.pallas_call(
        body, name=name, grid=(m // tm,),
        in_specs=[rows(dh), _full(w_o.shape), rows(gates), rows(br_a), rows(br_r)],
        out_specs=[rows(gates), rows(br_a), rows(br_r)],
        out_shape=[jax.ShapeDtypeStruct(gates.shape, bf16), jax.ShapeDtypeStruct(br_a.shape, bf16),
                   jax.ShapeDtypeStruct(br_r.shape, bf16)],
        compiler_params=_params(("parallel",)),
    )(dh, w_o, gates, br_a, br_r)


def _ffn_in_bwd(dh, w_down, gate, up, *, name):
    m, d = dh.shape
    n = w_down.shape[0]
    tm = _tile(m)

    def body(dh_ref, w_ref, g_ref, u_ref, dg_ref, du_ref):
        dact = _dot(dh_ref[...], w_ref[...], "nt")
        _, vjp = jax.vjp(lambda a, b: _swiglu(a, b)[0], g_ref[...].astype(f32), u_ref[...].astype(f32))
        dg, du = vjp(dact)
        dg_ref[...] = dg.astype(dg_ref.dtype)
        du_ref[...] = du.astype(du_ref.dtype)

    spec = pl.BlockSpec((tm, n), lambda i: (i, 0))
    return pl.pallas_call(
        body, name=name, grid=(m // tm,),
        in_specs=[pl.BlockSpec((tm, d), lambda i: (i, 0)), _full(w_down.shape), spec, spec],
        out_specs=[spec] * 2, out_shape=[jax.ShapeDtypeStruct((m, n), bf16)] * 2,
        compiler_params=_params(("parallel",)),
    )(dh, w_down, gate, up)


HALO = 16


def _previous_rows(x, before_ref, first_tile):
    rows = lax.broadcasted_iota(jnp.int32, x.shape, 0)
    last = jnp.where(first_tile, 0.0, before_ref[HALO - 1:HALO, :].astype(f32))
    return jnp.where(rows == 0, last, pltpu.roll(x, 1, axis=0))


def _mixer_inputs(ps, mixes, params, *, name):
    m = ps[0].shape[0]
    tm = _tile(m)
    sub = tm // HALO
    n_par = len(params)

    def body(*refs):
        first = pl.program_id(0) == 0
        pf = []
        for k in range(2):
            x = refs[k][...].astype(f32)
            pf.append(x + (_previous_rows(x, refs[2 + k], first) - x) * refs[4 + k][...])
        res = _rwkv_prep(*pf, *[ref[...] for ref in refs[6:6 + n_par]])
        for o_ref, val in zip(refs[6 + n_par:], res):
            o_ref[...] = val

    tile = lambda a: pl.BlockSpec((tm, a.shape[1]), lambda i: (i, 0))
    before = lambda a: pl.BlockSpec((HALO, a.shape[1]), lambda i: (jnp.maximum(i * sub - 1, 0), 0))
    out = pl.BlockSpec((tm, RWKV_DIM), lambda i: (i, 0))
    return pl.pallas_call(
        body, name=name, grid=(m // tm,),
        in_specs=[tile(a) for a in ps] + [before(a) for a in ps] + [_full(a.shape) for a in mixes + params],
        out_specs=[out] * 7, out_shape=[jax.ShapeDtypeStruct((m, RWKV_DIM), f32)] * 7,
        compiler_params=_params(("parallel",)),
    )(*ps, *ps, *mixes, *params)


def _mixer_inputs_bwd(ps, mixes, params, cts, *, name):
    m = ps[0].shape[0]
    tm = _tile(m)
    sub = tm // HALO
    nt = m // tm
    n_par = len(params)
    flat_cts = [c for group in cts for c in group]
    n_ct = len(flat_cts)

    def body(*refs):
        i = pl.program_id(0)
        tile_index = nt - 1 - i
        ct_refs = refs[6 + n_par:6 + n_par + n_ct]
        dp_refs = refs[6 + n_par + n_ct:8 + n_par + n_ct]
        dmix_refs = refs[8 + n_par + n_ct:10 + n_par + n_ct]
        dpar_refs = refs[10 + n_par + n_ct:9 + 2 * n_par + n_ct]
        carries = refs[9 + 2 * n_par + n_ct:]
        rows1 = tile_index * tm + lax.broadcasted_iota(jnp.int32, (tm, 1), 0)
        live = rows1 >= PAD

        @pl.when(i == 0)
        def _():
            for ref in (*dmix_refs, *dpar_refs, *carries):
                ref[...] = jnp.zeros_like(ref)

        xs, prevs, pf = [], [], []
        for k in range(2):
            x = refs[k][...].astype(f32)
            xp = _previous_rows(x, refs[2 + k], tile_index == 0)
            xs.append(x)
            prevs.append(xp)
            pf.append(x + (xp - x) * refs[4 + k][...])
        ct_vals, pos = [], 0
        for group in cts:
            acc = ct_refs[pos][...].astype(f32)
            for extra in range(1, len(group)):
                acc = acc + ct_refs[pos + extra][...].astype(f32)
            pos += len(group)
            ct_vals.append(jnp.where(live, acc, 0.0))
        par_vals = [ref[...] for ref in refs[6:6 + n_par]]
        _, vjp = jax.vjp(lambda *args: _rwkv_prep(*args, par_vals[-1]), *pf, *par_vals[:-1])
        g = vjp(tuple(ct_vals))
        for k in range(2):
            dpf = g[k]
            mixv = refs[4 + k][...]
            dm = dpf * mixv
            rows = lax.broadcasted_iota(jnp.int32, dm.shape, 0)
            dm_next = jnp.where(rows == tm - 1, carries[k][...], pltpu.roll(dm, tm - 1, axis=0))
            dp_refs[k][...] = jnp.where(live, dpf - dm + dm_next, 0.0).astype(dp_refs[k].dtype)
            carries[k][...] = dm[0:1, :]
            dmix_refs[k][...] += jnp.sum(dpf * (prevs[k] - xs[k]), axis=0, keepdims=True)
        for ref, val in zip(dpar_refs, g[2:]):
            ref[...] += val

    tile = lambda a: pl.BlockSpec((tm, a.shape[1]), lambda i: (nt - 1 - i, 0))
    before = lambda a: pl.BlockSpec((HALO, a.shape[1]), lambda i: (jnp.maximum((nt - 1 - i) * sub - 1, 0), 0))
    return pl.pallas_call(
        body, name=name, grid=(nt,),
        in_specs=[tile(a) for a in ps] + [before(a) for a in ps] + [_full(a.shape) for a in mixes + params]
        + [tile(c) for c in flat_cts],
        out_specs=[tile(a) for a in ps] + [_full(a.shape) for a in mixes + params[:-1]],
        out_shape=[jax.ShapeDtypeStruct(a.shape, bf16) for a in ps]
        + [jax.ShapeDtypeStruct(a.shape, f32) for a in mixes + params[:-1]],
        scratch_shapes=[pltpu.VMEM((1, a.shape[1]), f32) for a in ps],
        compiler_params=_params(("arbitrary",)),
    )(*ps, *ps, *mixes, *params, *flat_cts)


def _attn_masks(blk):
    qi = lax.broadcasted_iota(jnp.int32, (BLOCK, BLOCK), 0)
    ki = lax.broadcasted_iota(jnp.int32, (BLOCK, BLOCK), 1)
    qpos = blk * BLOCK + qi - PAD
    kpos_c = blk * BLOCK + ki - PAD
    kpos_p = kpos_c - BLOCK
    kpos_m = ki - PAD

    def band(kpos):
        return (kpos >= N_META) & (kpos <= qpos) & (qpos - kpos < WINDOW)

    return band(kpos_p), band(kpos_c), (kpos_m >= 0) & (kpos_m <= qpos)


def _attn_probs(qs, k3s, sink, oks):
    s = [[jnp.where(ok, _dot(qh, kx, "nt"), NEG_INF) for kx, ok in zip(k3, oks)] for qh, k3 in zip(qs, k3s)]
    mx = [jnp.maximum(jnp.maximum(jnp.max(t[0], -1, keepdims=True), jnp.max(t[1], -1, keepdims=True)),
                      jnp.maximum(jnp.max(t[2], -1, keepdims=True), sk)) for t, sk in zip(s, sink)]
    e = [[jnp.exp(tx - m) for tx in t] for t, m in zip(s, mx)]
    e_sink = [jnp.exp(sk - m) for sk, m in zip(sink, mx)]
    inv = [1.0 / (jnp.sum(t[0], -1, keepdims=True) + jnp.sum(t[1], -1, keepdims=True)
                  + jnp.sum(t[2], -1, keepdims=True) + es) for t, es in zip(e, e_sink)]
    return [[tx * i for tx in t] for t, i in zip(e, inv)], [es * i for es, i in zip(e_sink, inv)]


def _head_cols(i):
    return slice(i * HEAD_DIM, (i + 1) * HEAD_DIM)


def _attn_operands(refs):
    q_ref, kp_ref, kc_ref, km_ref, vp_ref, vc_ref, vm_ref, s_ref = refs
    qs = [q_ref[:, _head_cols(i)] * (HEAD_DIM ** -0.5) for i in range(Q_HEADS)]
    k3 = [[ref[:, _head_cols(h)] for ref in (kp_ref, kc_ref, km_ref)] for h in range(KV_HEADS)]
    v3 = [[ref[:, _head_cols(h)] for ref in (vp_ref, vc_ref, vm_ref)] for h in range(KV_HEADS)]
    return (qs, [k3[i // GROUP] for i in range(Q_HEADS)], [v3[i // GROUP] for i in range(Q_HEADS)],
            [s_ref[:, i:i + 1] for i in range(Q_HEADS)])


def _attention(q, k, v, sinks, *, name):
    lp = q.shape[0]
    nb = lp // BLOCK
    prev = lambda i: (jnp.maximum(i - 1, 0), 0)
    cur = lambda i: (i, 0)
    meta = lambda i: (0, 0)
    kv = lambda index: pl.BlockSpec((BLOCK, KV_W), index)

    def body(*refs):
        o_ref = refs[-1]
        qs, k3s, v3s, sink = _attn_operands(refs[:-1])
        p, _ = _attn_probs(qs, k3s, sink, _attn_masks(pl.program_id(0)))
        out = [_dot(ph[0], v3[0]) + _dot(ph[1], v3[1]) + _dot(ph[2], v3[2]) for ph, v3 in zip(p, v3s)]
        for i in range(Q_HEADS):
            o_ref[:, _head_cols(i)] = out[i].astype(o_ref.dtype)

    return pl.pallas_call(
        body, name=name, grid=(nb,),
        in_specs=[pl.BlockSpec((BLOCK, Q_W), cur), kv(prev), kv(cur), kv(meta), kv(prev), kv(cur), kv(meta),
                  _full((1, Q_HEADS))],
        out_specs=pl.BlockSpec((BLOCK, Q_W), cur),
        out_shape=jax.ShapeDtypeStruct((lp, Q_W), bf16),
        compiler_params=_params(("parallel",)),
    )(q, k, k, k, v, v, v, sinks)


def _attention_bwd(q, k, v, sinks, do, *, name):
    lp = q.shape[0]
    nb = lp // BLOCK
    cur = lambda n: (jnp.minimum(n, nb - 1), 0)
    prev = lambda n: (jnp.maximum(jnp.minimum(n, nb - 1) - 1, 0), 0)
    behind = lambda n: (jnp.maximum(n - 1, 0), 0)
    meta = lambda n: (0, 0)
    kv = lambda index: pl.BlockSpec((BLOCK, KV_W), index)
    scale = HEAD_DIM ** -0.5

    def body(*refs):
        ins, do_ref = refs[:8], refs[8]
        dq_ref, dk_ref, dv_ref, dkm_ref, dvm_ref, ds_ref, carry_k, carry_v = refs[9:]
        n = pl.program_id(0)

        @pl.when(n == 0)
        def _():
            for ref in (dkm_ref, dvm_ref, ds_ref, carry_k, carry_v):
                ref[...] = jnp.zeros_like(ref)

        @pl.when(n < nb)
        def _():
            qs, k3s, v3s, sink = _attn_operands(ins)
            do = [do_ref[:, _head_cols(i)] for i in range(Q_HEADS)]
            p, p_sink = _attn_probs(qs, k3s, sink, _attn_masks(n))
            out = [_dot(ph[0], v3[0]) + _dot(ph[1], v3[1]) + _dot(ph[2], v3[2]) for ph, v3 in zip(p, v3s)]
            delta = [jnp.sum(d * o, -1, keepdims=True) for d, o in zip(do, out)]
            dp = [[_dot(d, vx, "nt") for vx in v3] for d, v3 in zip(do, v3s)]
            ds = [[px * (dx - dl) for px, dx in zip(ph, dh)] for ph, dh, dl in zip(p, dp, delta)]
            dq = [_dot(dsh[0], k3[0]) + _dot(dsh[1], k3[1]) + _dot(dsh[2], k3[2]) for dsh, k3 in zip(ds, k3s)]
            for i in range(Q_HEADS):
                dq_ref[:, _head_cols(i)] = dq[i] * scale
                ds_ref[:, i:i + 1] -= jnp.sum(p_sink[i] * delta[i], axis=0, keepdims=True)
            for h in range(KV_HEADS):
                group = slice(h * GROUP, (h + 1) * GROUP)
                q_all = jnp.concatenate(qs[group], axis=0)
                do_all = jnp.concatenate(do[group], axis=0)
                dk3 = [_dot(jnp.concatenate([dsh[x] for dsh in ds[group]], axis=0), q_all, "tn") for x in range(3)]
                dv3 = [_dot(jnp.concatenate([ph[x] for ph in p[group]], axis=0), do_all, "tn") for x in range(3)]
                hs = _head_cols(h)
                for out_ref, carry, meta_ref, d3 in ((dk_ref, carry_k, dkm_ref, dk3),
                                                     (dv_ref, carry_v, dvm_ref, dv3)):
                    out_ref[:, hs] = carry[:, hs] + d3[0]
                    carry[:, hs] = d3[1]
                    meta_ref[:, hs] += d3[2]

        @pl.when(n == nb)
        def _():
            dk_ref[...] = carry_k[...]
            dv_ref[...] = carry_v[...]

    kv_shape = jax.ShapeDtypeStruct((lp, KV_W), f32)
    one_shape = jax.ShapeDtypeStruct((BLOCK, KV_W), f32)
    return pl.pallas_call(
        body, name=name, grid=(nb + 1,),
        in_specs=[pl.BlockSpec((BLOCK, Q_W), cur), kv(prev), kv(cur), kv(meta), kv(prev), kv(cur), kv(meta),
                  _full((1, Q_HEADS)), pl.BlockSpec((BLOCK, Q_W), cur)],
        out_specs=[pl.BlockSpec((BLOCK, Q_W), cur), kv(behind), kv(behind), kv(meta), kv(meta),
                   _full((1, Q_HEADS))],
        out_shape=[jax.ShapeDtypeStruct((lp, Q_W), f32), kv_shape, kv_shape, one_shape, one_shape,
                   jax.ShapeDtypeStruct((1, Q_HEADS), f32)],
        scratch_shapes=[pltpu.VMEM((BLOCK, KV_W), f32), pltpu.VMEM((BLOCK, KV_W), f32)],
        compiler_params=_params(("arbitrary",)),
    )(q, k, k, k, v, v, v, sinks, do)


@jax.custom_vjp
def _known_inverse(l, x):
    return x


def _known_inverse_fwd(l, x):
    return x, x


def _known_inverse_bwd(x, ct):
    return _dot(_dot(x, ct, "tn"), x, "nt"), jnp.zeros_like(x)


_known_inverse.defvjp(_known_inverse_fwd, _known_inverse_bwd)


@jax.custom_vjp
def _decayed(x, c):
    return (x * jnp.exp(c)).astype(bf16).astype(f32)


def _decayed_fwd(x, c):
    e = jnp.exp(c)
    out = (x * e).astype(bf16).astype(f32)
    return out, (e, out)


def _decayed_bwd(res, ct):
    e, out = res
    return ct * e, ct * out


_decayed.defvjp(_decayed_fwd, _decayed_bwd)


@jax.custom_vjp
def _pair(x, y):
    return _dot(x, y, "nt")


def _pair_fwd(x, y):
    return _dot(x, y, "nt"), (x, y)


def _pair_bwd(res, ct):
    x, y = res
    hi = ct.astype(bf16)
    lo = (ct - hi.astype(f32)).astype(bf16)
    return _dot(hi, y) + _dot(lo, y), _dot(hi, x, "tn") + _dot(lo, x, "tn")


_pair.defvjp(_pair_fwd, _pair_bwd)


def _scan_chunk(s0, r, lw, k, v, a, b, inv=None):
    t = r[0].shape[0]
    ii = lax.broadcasted_iota(jnp.int32, (t, t), 0)
    jj = lax.broadcasted_iota(jnp.int32, (t, t), 1)
    incl = jj <= ii
    strict = jj < ii
    tri = incl.astype(f32)
    eye = jnp.where(ii == jj, 1.0, 0.0)
    cl = [_const_dot(tri, x) for x in lw]
    mid = [c[t // 2 - 1:t // 2, :] for c in cl]
    s0 = [s * jnp.exp(m) for s, m in zip(s0, mid)]
    cl = [c - m for c, m in zip(cl, mid)]
    rt = [_decayed(x, c) for x, c in zip(r, cl)]
    at = [_decayed(x, c - l) for x, c, l in zip(a, cl, lw)]
    bt = [_decayed(x, -c) for x, c in zip(b, cl)]
    kt = [_decayed(x, -c) for x, c in zip(k, cl)]
    l_ab = [jnp.where(strict, _pair(x, y), 0.0) for x, y in zip(at, bt)]
    l_ak = [jnp.where(strict, _pair(x, y), 0.0) for x, y in zip(at, kt)]
    r_b = [jnp.where(incl, _pair(x, y), 0.0) for x, y in zip(rt, bt)]
    r_k = [jnp.where(incl, _pair(x, y), 0.0) for x, y in zip(rt, kt)]
    if inv is None:
        inv = [eye + x for x in l_ab]
        pw = l_ab
        for _ in range(int(math.log2(t)) - 1):
            pw = [_dot(x, x) for x in pw]
            inv = [x + _dot(x, y) for x, y in zip(inv, pw)]
    else:
        inv = [_known_inverse(x, y) for x, y in zip(l_ab, inv)]
    rhs = [_dot(x, s, "nt") + _dot(m, y) for x, s, m, y in zip(at, s0, l_ak, v)]
    u = [_dot(x, y) for x, y in zip(inv, rhs)]
    y_s = [_dot(x, s, "nt") for x, s in zip(rt, s0)]
    y = [ys + _dot(m, uu) + _dot(n, vv) for ys, m, uu, n, vv in zip(y_s, r_b, u, r_k, v)]
    grow = [s + _dot(uu, x, "tn") + _dot(vv, z, "tn") for s, uu, x, vv, z in zip(s0, u, bt, v, kt)]
    s1 = [g * jnp.exp(c[t - 1:t, :]) for g, c in zip(grow, cl)]
    return y, s1, inv


def _head_rows(h):
    return slice(h * RWKV_HEAD, (h + 1) * RWKV_HEAD)


def _per_head(ref):
    return [ref[:, _head_rows(h)] for h in range(RWKV_HEADS)]


def _scan(r, lw, k, v, a, b, *, name):
    lp = r.shape[0]
    nc = lp // CHUNK
    row = pl.BlockSpec((CHUNK, RWKV_DIM), lambda c: (c, 0))

    def body(r_ref, lw_ref, k_ref, v_ref, a_ref, b_ref, y_ref, s_ref, inv_ref, state):
        @pl.when(pl.program_id(0) == 0)
        def _():
            state[...] = jnp.zeros_like(state)

        s_ref[...] = state[...]
        s0 = [state[_head_rows(h), :] for h in range(RWKV_HEADS)]
        y, s1, inv = _scan_chunk(s0, *[_per_head(ref) for ref in (r_ref, lw_ref, k_ref, v_ref, a_ref, b_ref)])
        for h in range(RWKV_HEADS):
            y_ref[:, _head_rows(h)] = y[h]
            state[_head_rows(h), :] = s1[h]
            inv_ref[h * CHUNK:(h + 1) * CHUNK, :] = inv[h].astype(inv_ref.dtype)

    return pl.pallas_call(
        body, name=name, grid=(nc,), in_specs=[row] * 6,
        out_specs=[row, pl.BlockSpec((RWKV_DIM, RWKV_HEAD), lambda c: (c, 0)),
                   pl.BlockSpec((RWKV_HEADS * CHUNK, CHUNK), lambda c: (c, 0))],
        out_shape=[jax.ShapeDtypeStruct((lp, RWKV_DIM), f32), jax.ShapeDtypeStruct((nc * RWKV_DIM, RWKV_HEAD), f32),
                   jax.ShapeDtypeStruct((nc * RWKV_HEADS * CHUNK, CHUNK), bf16)],
        scratch_shapes=[pltpu.VMEM((RWKV_DIM, RWKV_HEAD), f32)],
        compiler_params=_params(("arbitrary",)),
    )(r, lw, k, v, a, b)


def _scan_bwd(r, lw, k, v, a, b, states, inverses, dy, *, name):
    lp = r.shape[0]
    nc = lp // CHUNK
    back = lambda c: (nc - 1 - c, 0)
    row = pl.BlockSpec((CHUNK, RWKV_DIM), back)

    def body(r_ref, lw_ref, k_ref, v_ref, a_ref, b_ref, s_ref, inv_ref, dy_ref,
             dr_ref, dlw_ref, dk_ref, dv_ref, da_ref, db_ref, dstate):
        @pl.when(pl.program_id(0) == 0)
        def _():
            dstate[...] = jnp.zeros_like(dstate)

        outs = (dr_ref, dlw_ref, dk_ref, dv_ref, da_ref, db_ref)
        s0 = [s_ref[_head_rows(h), :] for h in range(RWKV_HEADS)]
        inv = [inv_ref[h * CHUNK:(h + 1) * CHUNK, :].astype(f32) for h in range(RWKV_HEADS)]
        _, vjp = jax.vjp(lambda *args: _scan_chunk(*args, inv=inv)[:2], s0,
                         *[_per_head(ref) for ref in (r_ref, lw_ref, k_ref, v_ref, a_ref, b_ref)])
        g = vjp((_per_head(dy_ref), [dstate[_head_rows(h), :] for h in range(RWKV_HEADS)]))
        for h in range(RWKV_HEADS):
            dstate[_head_rows(h), :] = g[0][h]
            for o_ref, gv in zip(outs, g[1:]):
                o_ref[:, _head_rows(h)] = gv[h]

    shape = jax.ShapeDtypeStruct((lp, RWKV_DIM), f32)
    return pl.pallas_call(
        body, name=name, grid=(nc,),
        in_specs=[row] * 6 + [pl.BlockSpec((RWKV_DIM, RWKV_HEAD), back),
                              pl.BlockSpec((RWKV_HEADS * CHUNK, CHUNK), back), row],
        out_specs=[row] * 6, out_shape=[shape] * 6,
        scratch_shapes=[pltpu.VMEM((RWKV_DIM, RWKV_HEAD), f32)],
        compiler_params=_params(("arbitrary",)),
    )(r, lw, k, v, a, b, states, inverses, dy)


def _loss_head(h2, target, g_final, *, name):
    lp = h2.shape[0]
    tm = BLOCK
    front_tiles = FRONT // tm

    def body(h_ref, t_ref, g_ref, loss_ref, dh_ref, dg_ref):
        i = pl.program_id(0)
        real = i >= front_tiles

        def tile_loss(hv, gv):
            err = _rms(hv, gv) - t_ref[...]
            return jnp.where(real, 0.5 * jnp.sum(jnp.mean(err * err, axis=-1, keepdims=True)), 0.0)

        loss, (dh, dg) = jax.value_and_grad(tile_loss, argnums=(0, 1))(h_ref[...], g_ref[...])

        @pl.when(i == 0)
        def _():
            loss_ref[...] = jnp.zeros_like(loss_ref)
            dg_ref[...] = jnp.zeros_like(dg_ref)

        loss_ref[...] += jnp.full(loss_ref.shape, loss, f32)
        dg_ref[...] += dg
        dh_ref[...] = dh

    return pl.pallas_call(
        body, name=name, grid=(lp // tm,),
        in_specs=[pl.BlockSpec((tm, D_MODEL), lambda i: (i, 0)),
                  pl.BlockSpec((tm, D_MODEL), lambda i: (jnp.maximum(i - front_tiles, 0), 0)),
                  _full(g_final.shape)],
        out_specs=[_full((8, 128)), pl.BlockSpec((tm, D_MODEL), lambda i: (i, 0)), _full(g_final.shape)],
        out_shape=[jax.ShapeDtypeStruct((8, 128), f32), jax.ShapeDtypeStruct((lp, D_MODEL), f32),
                   jax.ShapeDtypeStruct(g_final.shape, f32)],
        compiler_params=_params(("arbitrary",)),
    )(h2, target, g_final)


def _local_step(x, target, meta, p, early_weights=None, late_weights=None, emit=None):
    emit = emit or (lambda group, grads: 0.0)
    seq = x.shape[0]
    lp = seq + FRONT
    h0 = jnp.concatenate([jnp.zeros((PAD, D_MODEL), f32), meta, x], axis=0)
    cos_t, sin_t, swap = _rope_tables(lp)
    hsum = _head_sum_matrix(RWKV_DIM, RWKV_HEAD)
    hmean = hsum / RWKV_HEAD
    b_qkv, b_rkv = p["b_in"][:, :ATTN_PROJ], p["b_in"][:, ATTN_PROJ:ATTN_PROJ + RKV_W]
    b_lora, b_gates = p["b_in"][:, ATTN_PROJ + RKV_W:ATTN_PROJ + RWKV_PROJ], p["b_in"][:, ATTN_PROJ + RWKV_PROJ:]
    post_params = [p["ln_w"], p["ln_b"], p["r_k"], hmean]

    (u,) = _rowwise(lambda hv, g: (_rms(hv, g),), [h0], [p["norm_mix_g"]], [(D_MODEL, bf16)], name="norm_mix")
    if early_weights is not None:
        p = {**p, **early_weights(u)}
    w_qkv_t, w_rkv_t = p["w_in_t"][:ATTN_PROJ], p["w_in_t"][ATTN_PROJ:ATTN_PROJ + RKV_W]
    w_lora_t, w_gates_t = p["w_in_t"][ATTN_PROJ + RKV_W:ATTN_PROJ + RWKV_PROJ], p["w_in_t"][ATTN_PROJ + RWKV_PROJ:]
    prep_params = [p["w0"], p["w2"], p["a0"], p["a2"], p["g2"], p["k_k"], p["k_a"], hsum]
    w_pieces = [w_qkv_t, w_rkv_t, w_lora_t, w_gates_t]
    qkv, p_rkv, p_lora, gates = _mm_fanout(u, w_pieces, [b_qkv, b_rkv, b_lora, b_gates], name="proj_in",
                                           zero_rows_below=PAD)

    q, k, v = _rowwise(_attn_prep, [qkv, cos_t, sin_t], [swap], [(Q_W, bf16), (KV_W, bf16), (KV_W, bf16)],
                       name="attn_prep")
    y_attn = _attention(q, k, v, p["sinks"], name="attention")

    mix_rkv, mix_lora = p["mix"][:, :RKV_W], p["mix"][:, RKV_W:]
    r_, lw_, k_, v_, a_, b_, g_ = _mixer_inputs([p_rkv, p_lora], [mix_rkv, mix_lora], prep_params,
                                                name="mixer_inputs")
    y_scan, states, inverses = _scan(r_, lw_, k_, v_, a_, b_, name="wkv_scan")
    (y_rwkv,) = _rowwise(_rwkv_post, [y_scan, r_, k_, v_, g_], post_params, [(RWKV_DIM, bf16)], name="rwkv_post")

    if late_weights is not None:
        p = {**p, **late_weights(y_rwkv)}
    br_a, br_r, merged = _branch_merge(y_attn, y_rwkv, p["w_br_attn_t"], p["w_br_rwkv_t"], gates, name="branch_merge")
    h1 = _mm(merged, p["w_o"], "nn", name="out_proj", add=h0)
    (f,) = _rowwise(lambda hv, g: (_rms(hv, g),), [h1], [p["norm_ffn_g"]], [(D_MODEL, bf16)], name="norm_ffn")
    gate, up, act = _ffn_in(f, p["w_gate_t"], p["w_up_t"], name="ffn_in")
    h2 = _mm(act, p["w_down"], "nn", name="ffn_down", add=h1)

    loss8, dh2, d_final_g = _loss_head(h2, target, p["norm_final_g"], name="loss_head")
    dgate, dup = _ffn_in_bwd(dh2, p["w_down"], gate, up, name="ffn_in_bwd")
    d_w_down = _mm_tn(act, dh2, name="dw_down")
    d_w_gate_t = _mm_tn(dgate, f, name="dw_gate")
    d_w_up_t = _mm_tn(dup, f, name="dw_up")
    zero = emit("ffn", dict(w_down=d_w_down, w_gate_t=d_w_gate_t, w_up_t=d_w_up_t))
    df = _mm_sum([dgate, dup], [p["w_gate_t"], p["w_up_t"]], name="d_f")
    dh1, d_ffn_g = _rowwise_bwd(lambda hv, g: (_rms(hv, g), hv), [h1], [p["norm_ffn_g"] + zero], [[df], [dh2]],
                                name="norm_ffn_bwd", diff_rows=[True], diff_params=[True])
    dgates, dbr_a, dbr_r = _branch_merge_bwd(dh1, p["w_o"], gates, br_a, br_r, name="branch_merge_bwd")
    d_w_o = _mm_tn(merged, dh1, name="dw_o")
    d_w_br_attn_t = _mm_tn(dbr_a, y_attn, name="dw_br_attn")
    d_w_br_rwkv_t = _mm_tn(dbr_r, y_rwkv, name="dw_br_rwkv")
    zero = emit("branch", dict(w_o=d_w_o, w_br_attn_t=d_w_br_attn_t, w_br_rwkv_t=d_w_br_rwkv_t))
    dy_attn = _mm(dbr_a, p["w_br_attn_t"], "nn", name="d_y_attn")
    dy_rwkv = _mm(dbr_r, p["w_br_rwkv_t"], "nn", name="d_y_rwkv")

    post_params = [p["ln_w"] + zero, p["ln_b"], p["r_k"], hmean]
    res = _rowwise_bwd(_rwkv_post, [y_scan, r_, k_, v_, g_], post_params, [[dy_rwkv]], name="rwkv_post_bwd",
                       diff_rows=[True] * 5, diff_params=[True, True, True, False])
    dy_scan, dr_p, dk_p, dv_p, dg_p, d_ln_w, d_ln_b, d_r_k = res
    dr_s, dlw_s, dk_s, dv_s, da_s, db_s = _scan_bwd(r_, lw_, k_, v_, a_, b_, states, inverses, dy_scan,
                                                    name="wkv_scan_bwd")
    res = _mixer_inputs_bwd([p_rkv, p_lora], [mix_rkv, mix_lora], prep_params,
                            [[dr_s, dr_p], [dlw_s], [dk_s, dk_p], [dv_s, dv_p], [da_s], [db_s], [dg_p]],
                            name="mixer_inputs_bwd")
    dp_rkv, dp_lora, d_mix_rkv, d_mix_lora, d_w0, d_w2, d_a0, d_a2, d_g2, d_k_k, d_k_a = res

    dq, dk, dv, dkm, dvm, d_sinks = _attention_bwd(q, k, v, p["sinks"], dy_attn, name="attention_bwd")
    rest = jnp.zeros((lp - BLOCK, KV_W), f32)
    dkm, dvm = jnp.concatenate([dkm, rest], axis=0), jnp.concatenate([dvm, rest], axis=0)
    (dqkv,) = _rowwise_bwd(_attn_prep, [qkv, cos_t, sin_t], [swap], [[dq], [dk, dkm], [dv, dvm]], name="attn_prep_bwd",
                           diff_rows=[True, False, False], diff_params=[False], out_dtypes=[bf16])

    d_w_qkv_t, db_qkv = _mm_tn(dqkv, u, name="dw_qkv", colsum=True)
    d_w_rkv_t, db_rkv = _mm_tn(dp_rkv, u, name="dw_rkv", colsum=True)
    d_w_lora_t, db_lora = _mm_tn(dp_lora, u, name="dw_lora", colsum=True)
    d_w_gates_t, db_gates = _mm_tn(dgates, u, name="dw_gates", colsum=True)
    d_w_in_t = jnp.concatenate([d_w_qkv_t, d_w_rkv_t, d_w_lora_t, d_w_gates_t], axis=0)
    zero = emit("input", dict(w_in_t=d_w_in_t, g2=d_g2, w2=d_w2, a2=d_a2))
    du = _mm_sum([dqkv, dp_rkv, dp_lora, dgates], w_pieces, name="d_u")
    dh0, d_mix_g = _rowwise_bwd(lambda hv, g: (_rms(hv, g), hv), [h0], [p["norm_mix_g"] + zero], [[du], [dh1]],
                                name="norm_mix_bwd", diff_rows=[True], diff_params=[True])

    grads = dict(
        w_in_t=d_w_in_t,
        b_in=jnp.concatenate([db_qkv, db_rkv, db_lora, db_gates], axis=1),
        mix=jnp.concatenate([d_mix_rkv, d_mix_lora], axis=1),
        norm_mix_g=d_mix_g, sinks=d_sinks, w0=d_w0, w2=d_w2, a0=d_a0, a2=d_a2, g2=d_g2, k_k=d_k_k, k_a=d_k_a,
        r_k=d_r_k, ln_w=d_ln_w, ln_b=d_ln_b, w_br_attn_t=d_w_br_attn_t, w_br_rwkv_t=d_w_br_rwkv_t, w_o=d_w_o,
        norm_ffn_g=d_ffn_g, w_gate_t=d_w_gate_t, w_up_t=d_w_up_t, w_down=d_w_down, norm_final_g=d_final_g,
        meta=dh0[PAD:FRONT],
    )
    return loss8[0, 0], dh0[FRONT:], grads


def _position():
    return lax.axis_index("x"), lax.axis_index("y"), lax.axis_index("c")


def _other_chips(x, y):
    return [(1 - x, y), (x, 1 - y), (1 - x, 1 - y)]


_HBM = pl.BlockSpec(memory_space=pltpu.HBM)
_SEM = pl.BlockSpec(memory_space=pltpu.SEMAPHORE)
_EFFECT = pltpu.SideEffectType.DATAFLOW_SIDE_EFFECTING


def _landing_zone(src, kind):
    shape = {"whole": (N_CHIPS,) + src.shape, "half": (2, N_CHIPS, src.shape[0], src.shape[1] // 2),
             "slab": (3,) + src.shape[1:], "sibling": src.shape}[kind]
    return lax.empty(shape, src.dtype)


def _copies_per_source(kind):
    return 1 if kind == "sibling" else 3


def _chip_copies(src_refs, land_refs, send_sems, recv_sems, kind):
    x, y, c = _position()
    if kind == "sibling":
        return [pltpu.make_async_remote_copy(
            src_ref=src, dst_ref=land, send_sem=send_sems.at[a], recv_sem=recv_sems.at[a],
            device_id=(x, y, 1 - c), device_id_type=MESH) for a, (src, land) in enumerate(zip(src_refs, land_refs))]
    copies = []
    for a, (src, land) in enumerate(zip(src_refs, land_refs)):
        for j, (px, py) in enumerate(_other_chips(x, y)):
            if kind == "whole":
                src_ref, dst_ref = src, land.at[2 * x + y]
            elif kind == "half":
                half = src.shape[1] // 2
                src_ref, dst_ref = src.at[:, pl.ds(pl.multiple_of(c * half, half), half)], land.at[c, 2 * x + y]
            else:
                src_ref, dst_ref = src.at[2 * px + py], land.at[j]
            copies.append(pltpu.make_async_remote_copy(
                src_ref=src_ref, dst_ref=dst_ref, send_sem=send_sems.at[3 * a + j], recv_sem=recv_sems.at[3 * a + j],
                device_id=(px, py, c), device_id_type=MESH))
    return copies


def _exchange_start(srcs, *, kind, name):
    n = len(srcs)
    lands = [_landing_zone(s, kind) for s in srcs]

    def body(*refs):
        for cp in _chip_copies(refs[:n], refs[n:2 * n], refs[2 * n], refs[2 * n + 1], kind):
            cp.start()
        refs[-1][...] = jnp.zeros_like(refs[-1])

    res = pl.pallas_call(
        body, name=name,
        out_shape=(pltpu.SemaphoreType.DMA((_copies_per_source(kind) * n,)),
                   pltpu.SemaphoreType.DMA((_copies_per_source(kind) * n,)),
                   *[pltpu.HBM(a.shape, a.dtype) for a in srcs + lands], jax.ShapeDtypeStruct((8, 128), f32)),
        in_specs=[_HBM] * (2 * n),
        out_specs=(_SEM, _SEM, *[_HBM] * (2 * n), pl.BlockSpec(memory_space=pltpu.VMEM)),
        input_output_aliases={i: 2 + i for i in range(2 * n)},
        compiler_params=pltpu.CompilerParams(has_side_effects=_EFFECT),
    )(*[pltpu.with_memory_space_constraint(a, pltpu.HBM) for a in srcs + lands])
    return res[0], res[1], list(res[2:2 + n]), list(res[2 + n:2 + 2 * n]), res[-1]


def _exchange_wait(handle, after, *, kind, name):
    send_sems, recv_sems, srcs, lands, _ = handle
    n = len(srcs)

    def body(*refs):
        for cp in _chip_copies(refs[:n], refs[n:2 * n], refs[2 * n], refs[2 * n + 1], kind):
            cp.wait_send()
            cp.wait_recv()

    res = pl.pallas_call(
        body, name=name,
        out_shape=tuple(pltpu.HBM(a.shape, a.dtype) for a in srcs + lands),
        in_specs=[_HBM] * (2 * n) + [_SEM, _SEM, pl.BlockSpec(memory_space=pl.ANY)],
        out_specs=tuple([_HBM] * (2 * n)),
        input_output_aliases={i: i for i in range(2 * n)},
        compiler_params=pltpu.CompilerParams(has_side_effects=_EFFECT),
    )(*srcs, *lands, send_sems, recv_sems, after)
    return list(res[:n]), list(res[n:])


def _sum_own_and_received(g, recv, *, name):
    _, r, w = g.shape
    tm = _tile(r)
    if g.dtype == bf16 and tm % 16:
        tm = r
    x, y, _ = _position()
    me = jnp.reshape(2 * x + y, (1,)).astype(jnp.int32)

    def body(me_ref, g_ref, r_ref, o_ref):
        o_ref[...] = (g_ref[0].astype(f32) + r_ref[0].astype(f32)) + (r_ref[1].astype(f32) + r_ref[2].astype(f32))

    return pl.pallas_call(
        body, name=name,
        grid_spec=pltpu.PrefetchScalarGridSpec(
            num_scalar_prefetch=1, grid=(r // tm,),
            in_specs=[pl.BlockSpec((1, tm, w), lambda i, me_ref: (me_ref[0], i, 0)),
                      pl.BlockSpec((3, tm, w), lambda i, me_ref: (0, i, 0))],
            out_specs=pl.BlockSpec((tm, w), lambda i, me_ref: (i, 0))),
        out_shape=jax.ShapeDtypeStruct((r, w), f32),
        compiler_params=_params(("parallel",)),
    )(me, g, recv)


def _swap_cores(arrs, *, name):
    n = len(arrs)

    def body(*refs):
        x, y, c = _position()
        copies = [pltpu.make_async_remote_copy(
            src_ref=refs[i], dst_ref=refs[n + i], send_sem=refs[2 * n].at[i], recv_sem=refs[2 * n + 1].at[i],
            device_id=(x, y, 1 - c), device_id_type=MESH) for i in range(n)]
        for cp in copies:
            cp.start()
        for cp in copies:
            cp.wait_recv()
        for cp in copies:
            cp.wait_send()

    return pl.pallas_call(
        body, name=name,
        in_specs=[pl.BlockSpec(memory_space=pl.ANY)] * n,
        out_specs=[pl.BlockSpec(memory_space=pl.ANY)] * n,
        out_shape=[jax.ShapeDtypeStruct(a.shape, a.dtype) for a in arrs],
        scratch_shapes=[pltpu.SemaphoreType.DMA((n,)), pltpu.SemaphoreType.DMA((n,))],
    )(*arrs)


def _swap_halves(zone, *, name):
    def body(z_ref, o_ref, send_sems, recv_sems):
        x, y, c = _position()
        mine = [pltpu.make_async_remote_copy(
            src_ref=o_ref.at[c, 2 * px + py], dst_ref=o_ref.at[c, 2 * px + py], send_sem=send_sems.at[j],
            recv_sem=recv_sems.at[j], device_id=(x, y, 1 - c), device_id_type=MESH)
            for j, (px, py) in enumerate(_other_chips(x, y))]
        for cp in mine:
            cp.start()
        for j, (px, py) in enumerate(_other_chips(x, y)):
            pltpu.make_async_remote_copy(
                src_ref=o_ref.at[c, 2 * px + py], dst_ref=o_ref.at[1 - c, 2 * px + py], send_sem=send_sems.at[j],
                recv_sem=recv_sems.at[j], device_id=(x, y, 1 - c), device_id_type=MESH).wait_recv()
        for cp in mine:
            cp.wait_send()

    return pl.pallas_call(
        body, name=name,
        in_specs=[pl.BlockSpec(memory_space=pl.ANY)], out_specs=pl.BlockSpec(memory_space=pl.ANY),
        out_shape=jax.ShapeDtypeStruct(zone.shape, zone.dtype), input_output_aliases={0: 0},
        scratch_shapes=[pltpu.SemaphoreType.DMA((3,)), pltpu.SemaphoreType.DMA((3,))],
    )(zone)


def _all_reduce_small(a, after, *, name):
    rows, w = a.shape

    def body(a_ref, after_ref, o_ref, buf, send_sems, recv_sems):
        x, y, c = _position()
        me = 4 * x + 2 * y + c
        buf[0] = a_ref[...]
        sends = []
        for rel in range(1, N_DEV):
            peer = ((1 - x) if rel & 4 else x, (1 - y) if rel & 2 else y, (1 - c) if rel & 1 else c)
            cp = pltpu.make_async_remote_copy(
                src_ref=a_ref, dst_ref=buf.at[rel], send_sem=send_sems.at[rel - 1], recv_sem=recv_sems.at[rel - 1],
                device_id=peer, device_id_type=MESH)
            cp.start()
            sends.append(cp)
        for cp in sends:
            cp.wait_recv()
        for cp in sends:
            cp.wait_send()
        acc = buf[jnp.bitwise_xor(me, 0)]
        for d in range(1, N_DEV):
            acc = acc + buf[jnp.bitwise_xor(me, d)]
        o_ref[...] = acc

    return pl.pallas_call(
        body, name=name,
        in_specs=[pl.BlockSpec(memory_space=pltpu.VMEM), pl.BlockSpec(memory_space=pl.ANY)],
        out_specs=pl.BlockSpec(memory_space=pltpu.VMEM),
        out_shape=jax.ShapeDtypeStruct((rows, w), f32),
        scratch_shapes=[pltpu.VMEM((N_DEV, rows, w), f32), pltpu.SemaphoreType.DMA((N_DEV - 1,)),
                        pltpu.SemaphoreType.DMA((N_DEV - 1,))],
    )(a, after)


def _adamw(w, g_parts, m, v, *, name, transposed=False):
    rows, cols = w.shape
    if transposed:
        tm = 256 if rows % 256 == 0 else rows
        g_spec = pl.BlockSpec((cols, tm), lambda i: (0, i))
    else:
        tm = _tile(rows, 256)
        g_spec = pl.BlockSpec((tm, cols), lambda i: (i, 0))
    n = len(g_parts)

    def body(*refs):
        w_ref, m_ref, v_ref = refs[0], refs[1 + n], refs[2 + n]
        g_ref, d_ref, nm_ref, nv_ref = refs[3 + n:]
        gv = refs[1][...]
        for part in refs[2:1 + n]:
            gv = gv + part[...]
        if transposed:
            gv = gv.T
        g_ref[...] = gv
        nm = ADAM_B1 * m_ref[...] + (1.0 - ADAM_B1) * gv
        nv = ADAM_B2 * v_ref[...] + (1.0 - ADAM_B2) * (gv * gv)
        m_hat = nm / (1.0 - ADAM_B1 ** ADAM_STEP)
        v_hat = nv / (1.0 - ADAM_B2 ** ADAM_STEP)
        d_ref[...] = -ADAM_LR * (m_hat / (jnp.sqrt(v_hat) + ADAM_EPS) + ADAM_WD * w_ref[...])
        nm_ref[...] = nm
        nv_ref[...] = nv

    spec = pl.BlockSpec((tm, cols), lambda i: (i, 0))
    shape = jax.ShapeDtypeStruct((rows, cols), f32)
    return pl.pallas_call(
        body, name=name, grid=(rows // tm,), in_specs=[spec] + [g_spec] * n + [spec] * 2,
        out_specs=[spec] * 4, out_shape=[shape] * 4,
        compiler_params=_params(("parallel",)),
    )(w, *g_parts, m, v)


def _pad_rows(a, rows):
    return jnp.concatenate([a, jnp.zeros((rows - a.shape[0], a.shape[1]), a.dtype)], axis=0) if rows > a.shape[0] else a


_SMALL = (("norm_mix_g", D_MODEL), ("b_in", D_IN), ("sinks", Q_HEADS), ("mix", RWKV_PROJ), ("w0", RWKV_DIM),
          ("a0", RWKV_DIM), ("k_k", RWKV_DIM), ("k_a", RWKV_DIM), ("r_k", RWKV_DIM), ("ln_w", RWKV_DIM),
          ("ln_b", RWKV_DIM), ("norm_ffn_g", D_MODEL), ("norm_final_g", D_MODEL))


def _pack_small(d):
    flat = jnp.concatenate([d[n].reshape(-1).astype(f32) for n, _ in _SMALL])
    return flat


def _unpack_small(flat):
    out, off = {}, 0
    for n, size in _SMALL:
        out[n] = flat[off:off + size]
        off += size
    return out


_SMALL_TOTAL = sum(s for _, s in _SMALL)


def kernel(x, meta_tokens, norm_mix_g, w_in, b_in, attn_sinks, rwkv_mix, rwkv_w0, rwkv_w2, rwkv_a0, rwkv_a2, rwkv_g2, rwkv_k_k, rwkv_k_a, rwkv_r_k, rwkv_ln_w, rwkv_ln_b, w_br_attn, w_br_rwkv, w_o, norm_ffn_g, w_ffn_gate, w_ffn_up, w_ffn_down, norm_final_g, loss_target, m_meta_tokens, m_norm_mix_g, m_w_in, m_b_in, m_attn_sinks, m_rwkv_mix, m_rwkv_w0, m_rwkv_w2, m_rwkv_a0, m_rwkv_a2, m_rwkv_g2, m_rwkv_k_k, m_rwkv_k_a, m_rwkv_r_k, m_rwkv_ln_w, m_rwkv_ln_b, m_w_br_attn, m_w_br_rwkv, m_w_o, m_norm_ffn_g, m_w_ffn_gate, m_w_ffn_up, m_w_ffn_down, m_norm_final_g, v_meta_tokens, v_norm_mix_g, v_w_in, v_b_in, v_attn_sinks, v_rwkv_mix, v_rwkv_w0, v_rwkv_w2, v_rwkv_a0, v_rwkv_a2, v_rwkv_g2, v_rwkv_k_k, v_rwkv_k_a, v_rwkv_r_k, v_rwkv_ln_w, v_rwkv_ln_b, v_w_br_attn, v_w_br_rwkv, v_w_o, v_norm_ffn_g, v_w_ffn_gate, v_w_ffn_up, v_w_ffn_down, v_norm_final_g):
    names = ("meta_tokens", "norm_mix_g", "w_in", "b_in", "attn_sinks", "rwkv_mix", "rwkv_w0", "rwkv_w2", "rwkv_a0",
             "rwkv_a2", "rwkv_g2", "rwkv_k_k", "rwkv_k_a", "rwkv_r_k", "rwkv_ln_w", "rwkv_ln_b", "w_br_attn",
             "w_br_rwkv", "w_o", "norm_ffn_g", "w_ffn_gate", "w_ffn_up", "w_ffn_down", "norm_final_g")
    w_all = dict(zip(names, (meta_tokens, norm_mix_g, w_in, b_in, attn_sinks, rwkv_mix, rwkv_w0, rwkv_w2, rwkv_a0,
                             rwkv_a2, rwkv_g2, rwkv_k_k, rwkv_k_a, rwkv_r_k, rwkv_ln_w, rwkv_ln_b, w_br_attn,
                             w_br_rwkv, w_o, norm_ffn_g, w_ffn_gate, w_ffn_up, w_ffn_down, norm_final_g)))
    m_all = dict(zip(names, (m_meta_tokens, m_norm_mix_g, m_w_in, m_b_in, m_attn_sinks, m_rwkv_mix, m_rwkv_w0,
                             m_rwkv_w2, m_rwkv_a0, m_rwkv_a2, m_rwkv_g2, m_rwkv_k_k, m_rwkv_k_a, m_rwkv_r_k,
                             m_rwkv_ln_w, m_rwkv_ln_b, m_w_br_attn, m_w_br_rwkv, m_w_o, m_norm_ffn_g, m_w_ffn_gate,
                             m_w_ffn_up, m_w_ffn_down, m_norm_final_g)))
    v_all = dict(zip(names, (v_meta_tokens, v_norm_mix_g, v_w_in, v_b_in, v_attn_sinks, v_rwkv_mix, v_rwkv_w0,
                             v_rwkv_w2, v_rwkv_a0, v_rwkv_a2, v_rwkv_g2, v_rwkv_k_k, v_rwkv_k_a, v_rwkv_r_k,
                             v_rwkv_ln_w, v_rwkv_ln_b, v_w_br_attn, v_w_br_rwkv, v_w_o, v_norm_ffn_g, v_w_ffn_gate,
                             v_w_ffn_up, v_w_ffn_down, v_norm_final_g)))
    cx, cy, _ = _position()
    chip = 2 * cx + cy

    t_of = dict(w_in_t="w_in", w_gate_t="w_ffn_gate", w_up_t="w_ffn_up", w_br_attn_t="w_br_attn",
                w_br_rwkv_t="w_br_rwkv", g2_t="rwkv_g2", w2_t="rwkv_w2", a2_t="rwkv_a2")
    plain_of = dict(w_down="w_ffn_down", w_o="w_o")
    meta_cols = meta_tokens.shape[1]

    def shard(k):
        return (w_all[t_of[k]][0].T if k in t_of else w_all[plain_of[k]][0]).astype(bf16)

    def whole(zone, own):
        return lax.dynamic_update_slice_in_dim(zone, own[None], chip, axis=0).reshape(-1, own.shape[-1])

    tiny = ("g2_t", "w2_t", "a2_t")
    late = ("w_gate_t", "w_up_t", "w_down", "w_o", "w_br_attn_t", "w_br_rwkv_t")
    w_in_own = shard("w_in_t")
    w_in_rows, w_in_cols = w_in_own.shape
    tiny_h = _exchange_start([shard(k) for k in tiny] + [meta_tokens], kind="whole", name="gather_tiny_start")
    w_in_h = _exchange_start([w_in_own + tiny_h[4][0, 0].astype(bf16)], kind="half", name="gather_w_in_start")
    behind = w_in_h[4][0, 0].astype(bf16)
    late_h = _exchange_start([shard(k) + behind for k in late], kind="whole", name="gather_late_start")
    own, zones = _exchange_wait(tiny_h, late_h[4], kind="whole", name="gather_tiny_wait")
    got = {k: whole(z, o) for k, z, o in zip(tiny, zones, own)}
    meta_full = whole(zones[-1], own[-1]).reshape(N_CHIPS, N_META, meta_cols).transpose(1, 0, 2).reshape(N_META, -1)
    p = dict(
        g2=got["g2_t"].T.astype(f32), w2=got["w2_t"].T.astype(f32), a2=got["a2_t"].T.astype(f32),
        b_in=b_in, sinks=attn_sinks, mix=rwkv_mix, w0=rwkv_w0, a0=rwkv_a0, k_k=rwkv_k_k, k_a=rwkv_k_a,
        r_k=rwkv_r_k.reshape(1, RWKV_DIM), ln_w=rwkv_ln_w, ln_b=rwkv_ln_b, norm_mix_g=norm_mix_g,
        norm_ffn_g=norm_ffn_g, norm_final_g=norm_final_g.reshape(1, D_MODEL),
    )

    def early_weights(after):
        own_h, zones_h = _exchange_wait(w_in_h, after, kind="half", name="gather_w_in_wait")
        zone = _swap_halves(zones_h[0], name="swap_w_in_halves")
        own_halves = own_h[0].reshape(w_in_rows, 2, w_in_cols // 2).transpose(1, 0, 2)[:, None]
        zone = lax.dynamic_update_slice(zone, own_halves, (0, chip, 0, 0))
        return dict(w_in_t=zone.transpose(1, 2, 0, 3).reshape(N_CHIPS * w_in_rows, w_in_cols))

    def late_weights(after):
        own_l, zones_l = _exchange_wait(late_h, after, kind="whole", name="gather_late_wait")
        return {k: whole(z, o) for k, z, o in zip(late, zones_l, own_l)}

    started = {}

    def emit(group, grads_):
        keys = list(grads_)
        slabs = []
        for k in keys:
            a = grads_[k].T if k in ("g2", "w2", "a2") else grads_[k]
            slabs.append(a.reshape(N_CHIPS, a.shape[0] // N_CHIPS, a.shape[1]))
        started[group] = (keys, _exchange_start(slabs, kind="slab", name="scatter_" + group + "_start"))
        return started[group][1][4][0, 0]

    loss, dx, g = _local_step(x[0], loss_target[0], meta_full, p, early_weights, late_weights, emit)

    grads, delta, new_m, new_v = {}, {}, {}, {}
    in_grad_layout = ("w_in_t", "w_gate_t", "w_up_t")
    weight_of = {**t_of, **plain_of}

    def partial_sums(groups, after):
        parts = {}
        for group in groups:
            keys, handle = started[group]
            slabs, lands = _exchange_wait(handle, after, kind="slab", name="scatter_" + group + "_wait")
            parts.update({k: _sum_own_and_received(s, l, name="sum_chips_" + k) for k, s, l in zip(keys, slabs, lands)})
        return parts

    def update(keys, mine, theirs):
        for k, part, other in zip(keys, mine, theirs):
            both = [part, other]
            k = k + "_t" if k in ("g2", "w2", "a2") else k
            n = weight_of[k]
            shape2 = w_all[n].shape[1:]
            w_, m_, v_ = (a.reshape(shape2) for a in (w_all[n], m_all[n], v_all[n]))
            if k in in_grad_layout:
                res = [t.T for t in _adamw(w_.T, both, m_.T, v_.T, name="adamw_" + n)]
            else:
                res = _adamw(w_, both, m_, v_, name="adamw_" + n, transposed=k in t_of)
            grads[n], delta[n], new_m[n], new_v[n] = (t.reshape(w_all[n].shape) for t in res)
        return delta[n]

    parts_a = partial_sums(("ffn", "branch"), dx)
    swap_a = _exchange_start(list(parts_a.values()), kind="sibling", name="swap_cores_a_start")
    small = jnp.concatenate([_pack_small(g), loss.reshape(1)])
    small_rows = -(-small.shape[0] // PACK_W)
    small = jnp.concatenate([small, jnp.zeros((small_rows * PACK_W - small.shape[0],), f32)]).reshape(small_rows, PACK_W)
    small_rows8 = -(-(small_rows + N_META) // 8) * 8
    reduced = _all_reduce_small(_pad_rows(jnp.concatenate([g["meta"], small], axis=0), small_rows8), swap_a[4],
                                name="reduce_small")
    done = update(list(parts_a), *_exchange_wait(swap_a, reduced, kind="sibling", name="swap_cores_a_wait"))
    parts_b = partial_sums(("input",), done)
    update(list(parts_b), list(parts_b.values()), _swap_cores(list(parts_b.values()), name="swap_cores_b"))
    g_meta = lax.dynamic_slice_in_dim(reduced[:N_META], chip * meta_cols, meta_cols, axis=1)
    flat = reduced[N_META:N_META + small_rows].reshape(-1)
    g_small = _unpack_small(flat)
    loss_total = flat[_SMALL_TOTAL]

    small_of = dict(norm_mix_g="norm_mix_g", b_in="b_in", attn_sinks="sinks", rwkv_mix="mix", rwkv_w0="w0",
                    rwkv_a0="a0", rwkv_k_k="k_k", rwkv_k_a="k_a", rwkv_r_k="r_k", rwkv_ln_w="ln_w",
                    rwkv_ln_b="ln_b", norm_ffn_g="norm_ffn_g", norm_final_g="norm_final_g")
    grads["meta_tokens"] = g_meta
    for n, k in small_of.items():
        grads[n] = g_small[k].reshape(w_all[n].shape)

    rest = [n for n in names if n not in delta]

    def pack_rest(src):
        flat_ = jnp.concatenate([src[n].reshape(-1) for n in rest])
        rows_ = -(-flat_.shape[0] // (8 * PACK_W)) * 8
        return jnp.concatenate([flat_, jnp.ones((rows_ * PACK_W - flat_.shape[0],), f32)]).reshape(rows_, PACK_W)

    _, d_, m_, v_ = _adamw(pack_rest(w_all), [pack_rest(grads)], pack_rest(m_all), pack_rest(v_all),
                           name="adamw_small")
    off = 0
    for n in rest:
        size = w_all[n].size
        for dst, src in ((delta, d_), (new_m, m_), (new_v, v_)):
            dst[n] = src.reshape(-1)[off:off + size].reshape(w_all[n].shape)
        off += size

    return (loss_total, dx.reshape(x.shape), *[grads[n] for n in names], *[delta[n] for n in names],
            *[new_m[n] for n in names], *[new_v[n] for n in names])
```

```python
import math

import jax
import jax.numpy as jnp
from jax import lax
from jax.experimental import pallas as pl
from jax.experimental.pallas import tpu as pltpu

f32 = jnp.float32
bf16 = jnp.bfloat16

D_MODEL = 1024
N_META = 16
HEAD_DIM = 64
Q_HEADS = 8
KV_HEADS = 2
GROUP = Q_HEADS // KV_HEADS
WINDOW = 128
BLOCK = 128
ROPE_THETA = 500000.0
ROPE_DIM = HEAD_DIM // 4
RWKV_HEADS = 8
RWKV_HEAD = 64
RWKV_DIM = RWKV_HEADS * RWKV_HEAD
DECAY_LORA = 64
AAA_LORA = 64
GATE_LORA = 160
LORA_W = DECAY_LORA + AAA_LORA + GATE_LORA
RWKV_LN_EPS = 64e-5
D_FF = 2816
Q_W = Q_HEADS * HEAD_DIM
KV_W = KV_HEADS * HEAD_DIM
ATTN_PROJ = Q_W + 2 * KV_W
RKV_W = 3 * RWKV_DIM
RWKV_PROJ = RKV_W + LORA_W
D_IN = ATTN_PROJ + RWKV_PROJ + 2 * D_MODEL
RMS_EPS = 1e-6
NEG_INF = -1e30
PAD = BLOCK - N_META
FRONT = PAD + N_META

ADAM_LR = 0.001
ADAM_B1 = 0.9
ADAM_B2 = 0.999
ADAM_EPS = 1e-08
ADAM_WD = 0.01
ADAM_STEP = 10

N_CHIPS = 4
N_DEV = 8
CHUNK = 128
VMEM_LIMIT = 56 * 1024 * 1024
PACK_W = 1024
MESH = pl.DeviceIdType.MESH


def _tile(m, pref=384):
    for step in (16, 8):
        for t in range(min(m, pref) // step * step, 0, -step):
            if m % t == 0:
                return t
    return m


def _params(sem=None):
    return pltpu.CompilerParams(dimension_semantics=sem, vmem_limit_bytes=VMEM_LIMIT)


def _full(shape):
    nd = len(shape)
    return pl.BlockSpec(shape, lambda *_: (0,) * nd)


def _dot(a, b, dims="nn"):
    dn = {"nn": (((1,), (0,)), ((), ())), "nt": (((1,), (1,)), ((), ())), "tn": (((0,), (0,)), ((), ()))}[dims]
    return lax.dot_general(a.astype(bf16), b.astype(bf16), dn, preferred_element_type=f32)


def _two_pass(x, m):
    x_hi = x.astype(bf16)
    x_lo = (x - x_hi.astype(f32)).astype(bf16)
    return _dot(x_hi, m) + _dot(x_lo, m)


@jax.custom_vjp
def _dot_const(x, m):
    return _two_pass(x, m)


def _dot_const_fwd(x, m):
    return _two_pass(x, m), m


def _dot_const_bwd(m, ct):
    return _two_pass(ct, m.T), jnp.zeros_like(m)


_dot_const.defvjp(_dot_const_fwd, _dot_const_bwd)


def _two_pass_left(m, x, dims):
    x_hi = x.astype(bf16)
    x_lo = (x - x_hi.astype(f32)).astype(bf16)
    return _dot(m, x_hi, dims) + _dot(m, x_lo, dims)


@jax.custom_vjp
def _const_dot(m, x):
    return _two_pass_left(m, x, "nn")


def _const_dot_fwd(m, x):
    return _two_pass_left(m, x, "nn"), m


def _const_dot_bwd(m, ct):
    return jnp.zeros_like(m), _two_pass_left(m, ct, "tn")


_const_dot.defvjp(_const_dot_fwd, _const_dot_bwd)


def _mm(a, b, mode, *, name, out_dtype=f32, bias=None, add=None, zero_rows_below=0):
    m, _ = a.shape
    n = b.shape[1] if mode == "nn" else b.shape[0]
    tm = _tile(m)
    has_bias, has_add = bias is not None, add is not None

    def body(*refs):
        a_ref, b_ref = refs[0], refs[1]
        o_ref = refs[-1]
        acc = _dot(a_ref[...], b_ref[...], mode)
        k = 2
        if has_bias:
            acc = acc + refs[k][...]
            k += 1
        if zero_rows_below:
            rows = pl.program_id(0) * tm + lax.broadcasted_iota(jnp.int32, acc.shape, 0)
            acc = jnp.where(rows >= zero_rows_below, acc, 0.0)
        if has_add:
            acc = acc + refs[k][...].astype(f32)
        o_ref[...] = acc.astype(out_dtype)

    ins = [a, b]
    in_specs = [pl.BlockSpec((tm, a.shape[1]), lambda i: (i, 0)), _full(b.shape)]
    if has_bias:
        ins.append(bias)
        in_specs.append(_full(bias.shape))
    if has_add:
        ins.append(add)
        in_specs.append(pl.BlockSpec((tm, n), lambda i: (i, 0)))
    return pl.pallas_call(
        body, name=name, grid=(m // tm,), in_specs=in_specs,
        out_specs=pl.BlockSpec((tm, n), lambda i: (i, 0)),
        out_shape=jax.ShapeDtypeStruct((m, n), out_dtype),
        compiler_params=_params(("parallel",)),
    )(*ins)


def _mm_sum(a_list, b_list, *, name, out_dtype=bf16):
    m = a_list[0].shape[0]
    n = b_list[0].shape[1]
    k = len(a_list)
    tm = _tile(m)

    def body(*refs):
        acc = _dot(refs[0][...], refs[k][...])
        for i in range(1, k):
            acc = acc + _dot(refs[i][...], refs[k + i][...])
        refs[-1][...] = acc.astype(out_dtype)

    return pl.pallas_call(
        body, name=name, grid=(m // tm,),
        in_specs=[pl.BlockSpec((tm, a.shape[1]), lambda i: (i, 0)) for a in a_list] + [_full(b.shape) for b in b_list],
        out_specs=pl.BlockSpec((tm, n), lambda i: (i, 0)),
        out_shape=jax.ShapeDtypeStruct((m, n), out_dtype),
        compiler_params=_params(("parallel",)),
    )(*a_list, *b_list)


def _pieces(widths):
    out, off = [], 0
    for w in widths:
        out.append((off, w))
        off += w
    return out


def _proj_in(a, w_lr, bias, widths, *, name, zero_rows_below=0):
    m, kdim = a.shape
    half = kdim // 2
    tm = _tile(m)

    def body(a_ref, w_ref, b_ref, *outs):
        a_l, a_r = a_ref[:, :half], a_ref[:, half:]
        for (off, width), o_ref in zip(_pieces(widths), outs):
            acc = _dot(a_l, w_ref[0, off:off + width, :], "nt") + _dot(a_r, w_ref[1, off:off + width, :], "nt")
            acc = acc + b_ref[:, off:off + width]
            if zero_rows_below:
                rows = pl.program_id(0) * tm + lax.broadcasted_iota(jnp.int32, acc.shape, 0)
                acc = jnp.where(rows >= zero_rows_below, acc, 0.0)
            o_ref[...] = acc.astype(o_ref.dtype)

    return pl.pallas_call(
        body, name=name, grid=(m // tm,),
        in_specs=[pl.BlockSpec((tm, kdim), lambda i: (i, 0)), _full(w_lr.shape), _full(bias.shape)],
        out_specs=[pl.BlockSpec((tm, w), lambda i: (i, 0)) for w in widths],
        out_shape=[jax.ShapeDtypeStruct((m, w), bf16) for w in widths],
        compiler_params=_params(("parallel",)),
    )(a, w_lr, bias)


def _proj_in_bwd(d_list, w_lr, *, name):
    m = d_list[0].shape[0]
    half = w_lr.shape[2]
    widths = [d.shape[1] for d in d_list]
    tm = _tile(m)

    def body(*refs):
        w_ref, o_ref = refs[-2], refs[-1]
        for side in range(2):
            acc = None
            for (off, width), d_ref in zip(_pieces(widths), refs):
                term = _dot(d_ref[...], w_ref[side, off:off + width, :])
                acc = term if acc is None else acc + term
            o_ref[:, side * half:(side + 1) * half] = acc.astype(o_ref.dtype)

    return pl.pallas_call(
        body, name=name, grid=(m // tm,),
        in_specs=[pl.BlockSpec((tm, w), lambda i: (i, 0)) for w in widths] + [_full(w_lr.shape)],
        out_specs=pl.BlockSpec((tm, 2 * half), lambda i: (i, 0)),
        out_shape=jax.ShapeDtypeStruct((m, 2 * half), bf16),
        compiler_params=_params(("parallel",)),
    )(*d_list, w_lr)


def _mm_tn(a, b, *, name, colsum=False, out_dtype=bf16):
    r, m = a.shape
    n = b.shape[1]
    tr = _tile(r, 1408)
    tmo = m
    for cand in (1408, 1024, 768, 512):
        if m > 1024 and m % cand == 0:
            tmo = cand
            break
    steps = r // tr

    def body(a_ref, b_ref, o_ref, *rest):
        acc = rest[-1]
        i = pl.program_id(1)

        @pl.when(i == 0)
        def _():
            acc[...] = jnp.zeros_like(acc)
            if colsum:
                rest[0][...] = jnp.zeros_like(rest[0])

        acc[...] += _dot(a_ref[...], b_ref[...], "tn")
        if colsum:
            rest[0][...] += jnp.sum(a_ref[...].astype(f32), axis=0, keepdims=True)

        @pl.when(i == steps - 1)
        def _():
            o_ref[...] = acc[...].astype(out_dtype)

    out_shape = [jax.ShapeDtypeStruct((m, n), out_dtype)]
    out_specs = [pl.BlockSpec((tmo, n), lambda j, i: (j, 0))]
    if colsum:
        out_shape.append(jax.ShapeDtypeStruct((1, m), f32))
        out_specs.append(pl.BlockSpec((1, tmo), lambda j, i: (0, j)))
    res = pl.pallas_call(
        body, name=name, grid=(m // tmo, steps),
        in_specs=[pl.BlockSpec((tr, tmo), lambda j, i: (i, j)), pl.BlockSpec((tr, n), lambda j, i: (i, 0))],
        out_specs=out_specs, out_shape=out_shape,
        scratch_shapes=[pltpu.VMEM((tmo, n), f32)],
        compiler_params=_params(("parallel", "arbitrary")),
    )(a, b)
    return res if colsum else res[0]


def _rowwise(fn, rows, params, outs, *, name, tm=None):
    m = rows[0].shape[0]
    tm = tm or _tile(m)
    nr, npar = len(rows), len(params)

    def body(*refs):
        vals = [r[...] for r in refs[:nr + npar]]
        res = fn(*vals)
        for o_ref, v in zip(refs[nr + npar:], res):
            o_ref[...] = v.astype(o_ref.dtype)

    return pl.pallas_call(
        body, name=name, grid=(m // tm,),
        in_specs=[pl.BlockSpec((tm, r.shape[1]), lambda i: (i, 0)) for r in rows] + [_full(p.shape) for p in params],
        out_specs=[pl.BlockSpec((tm, w), lambda i: (i, 0)) for w, _ in outs],
        out_shape=[jax.ShapeDtypeStruct((m, w), dt) for w, dt in outs],
        compiler_params=_params(("parallel",)),
    )(*rows, *params)


def _rowwise_bwd(fn, rows, params, cts, *, name, diff_rows, diff_params, tm=None, zero_rows_below=0, out_dtypes=None):
    m = rows[0].shape[0]
    tm = tm or _tile(m)
    nr, npar = len(rows), len(params)
    d_idx = [i for i in range(nr) if diff_rows[i]]
    p_idx = [i for i in range(npar) if diff_params[i]]
    out_dtypes = out_dtypes or [f32] * len(d_idx)
    flat_cts = [c for group in cts for c in group]
    n_ct = len(flat_cts)

    def body(*refs):
        vals = [r[...] for r in refs[:nr + npar]]
        ct_refs = refs[nr + npar:nr + npar + n_ct]
        out_refs = refs[nr + npar + n_ct:]
        ct_vals, k = [], 0
        for group in cts:
            acc = ct_refs[k][...].astype(f32)
            for extra in range(1, len(group)):
                acc = acc + ct_refs[k + extra][...].astype(f32)
            k += len(group)
            if zero_rows_below:
                rr = pl.program_id(0) * tm + lax.broadcasted_iota(jnp.int32, acc.shape, 0)
                acc = jnp.where(rr >= zero_rows_below, acc, 0.0)
            ct_vals.append(acc)

        def g(*dargs):
            full = list(vals)
            for pos, i in enumerate(d_idx):
                full[i] = dargs[pos]
            for pos, i in enumerate(p_idx):
                full[nr + i] = dargs[len(d_idx) + pos]
            return tuple(fn(*full))

        _, vjp = jax.vjp(g, *[vals[i].astype(f32) for i in d_idx], *[vals[nr + i] for i in p_idx])
        grads = vjp(tuple(ct_vals))
        for pos in range(len(d_idx)):
            out_refs[pos][...] = grads[pos].astype(out_refs[pos].dtype)
        first = pl.program_id(0) == 0
        for pos in range(len(p_idx)):
            o_ref = out_refs[len(d_idx) + pos]

            @pl.when(first)
            def _(o_ref=o_ref):
                o_ref[...] = jnp.zeros_like(o_ref)

            o_ref[...] += grads[len(d_idx) + pos]

    return pl.pallas_call(
        body, name=name, grid=(m // tm,),
        in_specs=[pl.BlockSpec((tm, r.shape[1]), lambda i: (i, 0)) for r in rows] + [_full(p.shape) for p in params]
        + [pl.BlockSpec((tm, c.shape[1]), lambda i: (i, 0)) for c in flat_cts],
        out_specs=[pl.BlockSpec((tm, rows[i].shape[1]), lambda i_: (i_, 0)) for i in d_idx]
        + [_full(params[i].shape) for i in p_idx],
        out_shape=[jax.ShapeDtypeStruct(rows[i].shape, dt) for i, dt in zip(d_idx, out_dtypes)]
        + [jax.ShapeDtypeStruct(params[i].shape, f32) for i in p_idx],
        compiler_params=_params(("arbitrary",)),
    )(*rows, *params, *flat_cts)


def _rms(x, g):
    return x * lax.rsqrt(jnp.mean(x * x, axis=-1, keepdims=True) + RMS_EPS) * g


def _head_sum_matrix(width, head):
    idx = jnp.arange(width) // head
    return (idx[:, None] == idx[None, :]).astype(f32)


def _rope_tables(lp):
    half = ROPE_DIM // 2
    pos = (jnp.arange(lp) - PAD).astype(f32)
    inv_freq = jnp.power(jnp.float32(ROPE_THETA), -jnp.arange(half, dtype=f32) * (2.0 / ROPE_DIM))
    ang = pos[:, None] * inv_freq[None, :]
    cos, sin = jnp.cos(ang), jnp.sin(ang)
    ones = jnp.ones((lp, HEAD_DIM - ROPE_DIM), f32)
    zeros = jnp.zeros((lp, HEAD_DIM - ROPE_DIM), f32)
    cos_t = jnp.concatenate([cos, cos, ones], axis=1)
    sin_t = jnp.concatenate([-sin, sin, zeros], axis=1)
    i = jnp.arange(HEAD_DIM)
    src = jnp.where(i < half, i + half, jnp.where(i < ROPE_DIM, i - half, i))
    swap = ((i[:, None] == src[None, :]) & (i[None, :] < ROPE_DIM)).astype(f32)
    return cos_t, sin_t, swap


def _attn_prep(qkv, cos_t, sin_t, swap):
    outs = []
    for h in range(Q_HEADS + KV_HEADS):
        t = qkv[:, h * HEAD_DIM:(h + 1) * HEAD_DIM]
        outs.append(t * cos_t + _dot_const(t, swap) * sin_t)
    q = jnp.concatenate(outs[:Q_HEADS], axis=1)
    k = jnp.concatenate(outs[Q_HEADS:], axis=1)
    return q, k, qkv[:, Q_W + KV_W:]


def _softplus(z):
    return jnp.maximum(z, 0.0) + jnp.log1p(jnp.exp(-jnp.abs(z)))


def _rwkv_prep(rkv, lora, w0, w2, a0, a2, g2, k_k, k_a, hsum):
    r = rkv[:, :RWKV_DIM]
    k = rkv[:, RWKV_DIM:2 * RWKV_DIM]
    v = rkv[:, 2 * RWKV_DIM:]
    dw = lora[:, :DECAY_LORA]
    da = lora[:, DECAY_LORA:DECAY_LORA + AAA_LORA]
    dg = lora[:, DECAY_LORA + AAA_LORA:]
    w = -_softplus(-(w0 + _dot(jnp.tanh(dw), w2))) - 0.5
    a = jax.nn.sigmoid(a0 + _dot(da, a2))
    g = _dot(jax.nn.sigmoid(dg), g2)
    kk = k * k_k
    kk = kk * lax.rsqrt(jnp.maximum(_dot_const(kk * kk, hsum), 1e-24))
    k = k * (1.0 + (a - 1.0) * k_a)
    log_decay = -jnp.exp(w)
    return r, log_decay, k, v, -kk, kk * a, g


def _rwkv_post(y, r, k, v, g, ln_w, ln_b, r_k, hmean):
    hsum = hmean * RWKV_HEAD
    mean = _dot_const(y, hmean)
    yc = y - mean
    var = _dot_const(yc * yc, hmean)
    yn = yc * lax.rsqrt(var + RWKV_LN_EPS) * ln_w + ln_b
    bonus = _dot_const(r * k * r_k, hsum) * v
    return ((yn + bonus) * g,)


def _merge(gates, br_a, br_r):
    sg = jax.nn.sigmoid(gates)
    return (sg[:, :D_MODEL] * br_a + sg[:, D_MODEL:] * br_r,)


def _swiglu(gate, up):
    return (jax.nn.silu(gate) * up,)


def _ffn_in(f, w_gate_t, w_up_t, *, name):
    m, d = f.shape
    n = w_gate_t.shape[0]
    tm = _tile(m)

    def body(f_ref, wg_ref, wu_ref, g_ref, u_ref, a_ref):
        g = _dot(f_ref[...], wg_ref[...], "nt")
        u = _dot(f_ref[...], wu_ref[...], "nt")
        g_ref[...] = g.astype(g_ref.dtype)
        u_ref[...] = u.astype(u_ref.dtype)
        a_ref[...] = _swiglu(g, u)[0].astype(a_ref.dtype)

    spec = pl.BlockSpec((tm, n), lambda i: (i, 0))
    return pl.pallas_call(
        body, name=name, grid=(m // tm,),
        in_specs=[pl.BlockSpec((tm, d), lambda i: (i, 0)), _full(w_gate_t.shape), _full(w_up_t.shape)],
        out_specs=[spec] * 3, out_shape=[jax.ShapeDtypeStruct((m, n), bf16)] * 3,
        compiler_params=_params(("parallel",)),
    )(f, w_gate_t, w_up_t)


def _branch_merge(y_attn, y_rwkv, w_attn_t, w_rwkv_t, gates, *, name):
    m = y_attn.shape[0]
    tm = _tile(m)

    def body(ya_ref, yr_ref, wa_ref, wr_ref, g_ref, a_ref, r_ref, o_ref):
        br_a = _dot(ya_ref[...], wa_ref[...], "nt")
        br_r = _dot(yr_ref[...], wr_ref[...], "nt")
        a_ref[...] = br_a.astype(a_ref.dtype)
        r_ref[...] = br_r.astype(r_ref.dtype)
        o_ref[...] = _merge(g_ref[...].astype(f32), br_a, br_r)[0].astype(o_ref.dtype)

    rows = lambda a: pl.BlockSpec((tm, a.shape[1]), lambda i: (i, 0))
    spec = pl.BlockSpec((tm, D_MODEL), lambda i: (i, 0))
    return pl.pallas_call(
        body, name=name, grid=(m // tm,),
        in_specs=[rows(y_attn), rows(y_rwkv), _full(w_attn_t.shape), _full(w_rwkv_t.shape), rows(gates)],
        out_specs=[spec] * 3, out_shape=[jax.ShapeDtypeStruct((m, D_MODEL), bf16)] * 3,
        compiler_params=_params(("parallel",)),
    )(y_attn, y_rwkv, w_attn_t, w_rwkv_t, gates)


def _branch_merge_bwd(dh, w_o, gates, br_a, br_r, *, name):
    m = dh.shape[0]
    tm = _tile(m)

    def body(dh_ref, w_ref, g_ref, a_ref, r_ref, dg_ref, da_ref, dr_ref):
        dmerged = _dot(dh_ref[...], w_ref[...], "nt")
        _, vjp = jax.vjp(lambda g, a, r: _merge(g, a, r)[0], g_ref[...].astype(f32), a_ref[...].astype(f32),
                         r_ref[...].astype(f32))
        dg, da, dr = vjp(dmerged)
        dg_ref[...] = dg.astype(dg_ref.dtype)
        da_ref[...] = da.astype(da_ref.dtype)
        dr_ref[...] = dr.astype(dr_ref.dtype)

    rows = lambda a: pl.BlockSpec((tm, a.shape[1]), lambda i: (i, 0))
    return pl.pallas_call(
        body, name=name, grid=(m // tm,),
        in_specs=[rows(dh), _full(w_o.shape), rows(gates), rows(br_a), rows(br_r)],
        out_specs=[rows(gates), rows(br_a), rows(br_r)],
        out_shape=[jax.ShapeDtypeStruct(gates.shape, bf16), jax.ShapeDtypeStruct(br_a.shape, bf16),
                   jax.ShapeDtypeStruct(br_r.shape, bf16)],
        compiler_params=_params(("parallel",)),
    )(dh, w_o, gates, br_a, br_r)


def _ffn_in_bwd(dh, w_down, gate, up, *, name):
    m, d = dh.shape
    n = w_down.shape[0]
    tm = _tile(m)

    def body(dh_ref, w_ref, g_ref, u_ref, dg_ref, du_ref):
        dact = _dot(dh_ref[...], w_ref[...], "nt")
        _, vjp = jax.vjp(lambda a, b: _swiglu(a, b)[0], g_ref[...].astype(f32), u_ref[...].astype(f32))
        dg, du = vjp(dact)
        dg_ref[...] = dg.astype(dg_ref.dtype)
        du_ref[...] = du.astype(du_ref.dtype)

    spec = pl.BlockSpec((tm, n), lambda i: (i, 0))
    return pl.pallas_call(
        body, name=name, grid=(m // tm,),
        in_specs=[pl.BlockSpec((tm, d), lambda i: (i, 0)), _full(w_down.shape), spec, spec],
        out_specs=[spec] * 2, out_shape=[jax.ShapeDtypeStruct((m, n), bf16)] * 2,
        compiler_params=_params(("parallel",)),
    )(dh, w_down, gate, up)


HALO = 16


def _previous_rows(x, before_ref, first_tile):
    rows = lax.broadcasted_iota(jnp.int32, x.shape, 0)
    last = jnp.where(first_tile, 0.0, before_ref[HALO - 1:HALO, :].astype(f32))
    return jnp.where(rows == 0, last, pltpu.roll(x, 1, axis=0))


def _mixer_inputs(ps, mixes, params, *, name):
    m = ps[0].shape[0]
    tm = _tile(m)
    sub = tm // HALO
    n_par = len(params)

    def body(*refs):
        first = pl.program_id(0) == 0
        pf = []
        for k in range(2):
            x = refs[k][...].astype(f32)
            pf.append(x + (_previous_rows(x, refs[2 + k], first) - x) * refs[4 + k][...])
        res = _rwkv_prep(*pf, *[ref[...] for ref in refs[6:6 + n_par]])
        for o_ref, val in zip(refs[6 + n_par:], res):
            o_ref[...] = val

    tile = lambda a: pl.BlockSpec((tm, a.shape[1]), lambda i: (i, 0))
    before = lambda a: pl.BlockSpec((HALO, a.shape[1]), lambda i: (jnp.maximum(i * sub - 1, 0), 0))
    out = pl.BlockSpec((tm, RWKV_DIM), lambda i: (i, 0))
    return pl.pallas_call(
        body, name=name, grid=(m // tm,),
        in_specs=[tile(a) for a in ps] + [before(a) for a in ps] + [_full(a.shape) for a in mixes + params],
        out_specs=[out] * 7, out_shape=[jax.ShapeDtypeStruct((m, RWKV_DIM), f32)] * 7,
        compiler_params=_params(("parallel",)),
    )(*ps, *ps, *mixes, *params)


def _mixer_inputs_bwd(ps, mixes, params, cts, *, name):
    m = ps[0].shape[0]
    tm = _tile(m)
    sub = tm // HALO
    nt = m // tm
    n_par = len(params)
    flat_cts = [c for group in cts for c in group]
    n_ct = len(flat_cts)

    def body(*refs):
        i = pl.program_id(0)
        tile_index = nt - 1 - i
        ct_refs = refs[6 + n_par:6 + n_par + n_ct]
        dp_refs = refs[6 + n_par + n_ct:8 + n_par + n_ct]
        dmix_refs = refs[8 + n_par + n_ct:10 + n_par + n_ct]
        dpar_refs = refs[10 + n_par + n_ct:9 + 2 * n_par + n_ct]
        carries = refs[9 + 2 * n_par + n_ct:]
        rows1 = tile_index * tm + lax.broadcasted_iota(jnp.int32, (tm, 1), 0)
        live = rows1 >= PAD

        @pl.when(i == 0)
        def _():
            for ref in (*dmix_refs, *dpar_refs, *carries):
                ref[...] = jnp.zeros_like(ref)

        xs, prevs, pf = [], [], []
        for k in range(2):
            x = refs[k][...].astype(f32)
            xp = _previous_rows(x, refs[2 + k], tile_index == 0)
            xs.append(x)
            prevs.append(xp)
            pf.append(x + (xp - x) * refs[4 + k][...])
        ct_vals, pos = [], 0
        for group in cts:
            acc = ct_refs[pos][...].astype(f32)
            for extra in range(1, len(group)):
                acc = acc + ct_refs[pos + extra][...].astype(f32)
            pos += len(group)
            ct_vals.append(jnp.where(live, acc, 0.0))
        par_vals = [ref[...] for ref in refs[6:6 + n_par]]
        _, vjp = jax.vjp(lambda *args: _rwkv_prep(*args, par_vals[-1]), *pf, *par_vals[:-1])
        g = vjp(tuple(ct_vals))
        for k in range(2):
            dpf = g[k]
            mixv = refs[4 + k][...]
            dm = dpf * mixv
            rows = lax.broadcasted_iota(jnp.int32, dm.shape, 0)
            dm_next = jnp.where(rows == tm - 1, carries[k][...], pltpu.roll(dm, tm - 1, axis=0))
            dp_refs[k][...] = jnp.where(live, dpf - dm + dm_next, 0.0).astype(dp_refs[k].dtype)
            carries[k][...] = dm[0:1, :]
            dmix_refs[k][...] += jnp.sum(dpf * (prevs[k] - xs[k]), axis=0, keepdims=True)
        for ref, val in zip(dpar_refs, g[2:]):
            ref[...] += val

    tile = lambda a: pl.BlockSpec((tm, a.shape[1]), lambda i: (nt - 1 - i, 0))
    before = lambda a: pl.BlockSpec((HALO, a.shape[1]), lambda i: (jnp.maximum((nt - 1 - i) * sub - 1, 0), 0))
    return pl.pallas_call(
        body, name=name, grid=(nt,),
        in_specs=[tile(a) for a in ps] + [before(a) for a in ps] + [_full(a.shape) for a in mixes + params]
        + [tile(c) for c in flat_cts],
        out_specs=[tile(a) for a in ps] + [_full(a.shape) for a in mixes + params[:-1]],
        out_shape=[jax.ShapeDtypeStruct(a.shape, bf16) for a in ps]
        + [jax.ShapeDtypeStruct(a.shape, f32) for a in mixes + params[:-1]],
        scratch_shapes=[pltpu.VMEM((1, a.shape[1]), f32) for a in ps],
        compiler_params=_params(("arbitrary",)),
    )(*ps, *ps, *mixes, *params, *flat_cts)


def _attn_masks(blk):
    qi = lax.broadcasted_iota(jnp.int32, (BLOCK, BLOCK), 0)
    ki = lax.broadcasted_iota(jnp.int32, (BLOCK, BLOCK), 1)
    qpos = blk * BLOCK + qi - PAD
    kpos_c = blk * BLOCK + ki - PAD
    kpos_p = kpos_c - BLOCK
    kpos_m = ki - PAD

    def band(kpos):
        return (kpos >= N_META) & (kpos <= qpos) & (qpos - kpos < WINDOW)

    return band(kpos_p), band(kpos_c), (kpos_m >= 0) & (kpos_m <= qpos)


def _attn_probs(qs, k3s, sink, oks):
    s = [[jnp.where(ok, _dot(qh, kx, "nt"), NEG_INF) for kx, ok in zip(k3, oks)] for qh, k3 in zip(qs, k3s)]
    mx = [jnp.maximum(jnp.maximum(jnp.max(t[0], -1, keepdims=True), jnp.max(t[1], -1, keepdims=True)),
                      jnp.maximum(jnp.max(t[2], -1, keepdims=True), sk)) for t, sk in zip(s, sink)]
    e = [[jnp.exp(tx - m) for tx in t] for t, m in zip(s, mx)]
    e_sink = [jnp.exp(sk - m) for sk, m in zip(sink, mx)]
    inv = [1.0 / (jnp.sum(t[0], -1, keepdims=True) + jnp.sum(t[1], -1, keepdims=True)
                  + jnp.sum(t[2], -1, keepdims=True) + es) for t, es in zip(e, e_sink)]
    return [[tx * i for tx in t] for t, i in zip(e, inv)], [es * i for es, i in zip(e_sink, inv)]


def _head_cols(i):
    return slice(i * HEAD_DIM, (i + 1) * HEAD_DIM)


def _attn_operands(refs):
    q_ref, kp_ref, kc_ref, km_ref, vp_ref, vc_ref, vm_ref, s_ref = refs
    qs = [q_ref[:, _head_cols(i)] * (HEAD_DIM ** -0.5) for i in range(Q_HEADS)]
    k3 = [[ref[:, _head_cols(h)] for ref in (kp_ref, kc_ref, km_ref)] for h in range(KV_HEADS)]
    v3 = [[ref[:, _head_cols(h)] for ref in (vp_ref, vc_ref, vm_ref)] for h in range(KV_HEADS)]
    return (qs, [k3[i // GROUP] for i in range(Q_HEADS)], [v3[i // GROUP] for i in range(Q_HEADS)],
            [s_ref[:, i:i + 1] for i in range(Q_HEADS)])


def _attention(q, k, v, sinks, *, name):
    lp = q.shape[0]
    nb = lp // BLOCK
    prev = lambda i: (jnp.maximum(i - 1, 0), 0)
    cur = lambda i: (i, 0)
    meta = lambda i: (0, 0)
    kv = lambda index: pl.BlockSpec((BLOCK, KV_W), index)

    def body(*refs):
        o_ref = refs[-1]
        qs, k3s, v3s, sink = _attn_operands(refs[:-1])
        p, _ = _attn_probs(qs, k3s, sink, _attn_masks(pl.program_id(0)))
        out = [_dot(ph[0], v3[0]) + _dot(ph[1], v3[1]) + _dot(ph[2], v3[2]) for ph, v3 in zip(p, v3s)]
        for i in range(Q_HEADS):
            o_ref[:, _head_cols(i)] = out[i].astype(o_ref.dtype)

    return pl.pallas_call(
        body, name=name, grid=(nb,),
        in_specs=[pl.BlockSpec((BLOCK, Q_W), cur), kv(prev), kv(cur), kv(meta), kv(prev), kv(cur), kv(meta),
                  _full((1, Q_HEADS))],
        out_specs=pl.BlockSpec((BLOCK, Q_W), cur),
        out_shape=jax.ShapeDtypeStruct((lp, Q_W), bf16),
        compiler_params=_params(("parallel",)),
    )(q, k, k, k, v, v, v, sinks)


def _attention_bwd(q, k, v, sinks, do, *, name):
    lp = q.shape[0]
    nb = lp // BLOCK
    cur = lambda n: (jnp.minimum(n, nb - 1), 0)
    prev = lambda n: (jnp.maximum(jnp.minimum(n, nb - 1) - 1, 0), 0)
    behind = lambda n: (jnp.maximum(n - 1, 0), 0)
    meta = lambda n: (0, 0)
    kv = lambda index: pl.BlockSpec((BLOCK, KV_W), index)
    scale = HEAD_DIM ** -0.5

    def body(*refs):
        ins, do_ref = refs[:8], refs[8]
        dq_ref, dk_ref, dv_ref, dkm_ref, dvm_ref, ds_ref, carry_k, carry_v = refs[9:]
        n = pl.program_id(0)

        @pl.when(n == 0)
        def _():
            for ref in (dkm_ref, dvm_ref, ds_ref, carry_k, carry_v):
                ref[...] = jnp.zeros_like(ref)

        @pl.when(n < nb)
        def _():
            qs, k3s, v3s, sink = _attn_operands(ins)
            do = [do_ref[:, _head_cols(i)] for i in range(Q_HEADS)]
            p, p_sink = _attn_probs(qs, k3s, sink, _attn_masks(n))
            out = [_dot(ph[0], v3[0]) + _dot(ph[1], v3[1]) + _dot(ph[2], v3[2]) for ph, v3 in zip(p, v3s)]
            delta = [jnp.sum(d * o, -1, keepdims=True) for d, o in zip(do, out)]
            dp = [[_dot(d, vx, "nt") for vx in v3] for d, v3 in zip(do, v3s)]
            ds = [[px * (dx - dl) for px, dx in zip(ph, dh)] for ph, dh, dl in zip(p, dp, delta)]
            dq = [_dot(dsh[0], k3[0]) + _dot(dsh[1], k3[1]) + _dot(dsh[2], k3[2]) for dsh, k3 in zip(ds, k3s)]
            for i in range(Q_HEADS):
                dq_ref[:, _head_cols(i)] = dq[i] * scale
                ds_ref[:, i:i + 1] -= jnp.sum(p_sink[i] * delta[i], axis=0, keepdims=True)
            for h in range(KV_HEADS):
                group = slice(h * GROUP, (h + 1) * GROUP)
                q_all = jnp.concatenate(qs[group], axis=0)
                do_all = jnp.concatenate(do[group], axis=0)
                dk3 = [_dot(jnp.concatenate([dsh[x] for dsh in ds[group]], axis=0), q_all, "tn") for x in range(3)]
                dv3 = [_dot(jnp.concatenate([ph[x] for ph in p[group]], axis=0), do_all, "tn") for x in range(3)]
                hs = _head_cols(h)
                for out_ref, carry, meta_ref, d3 in ((dk_ref, carry_k, dkm_ref, dk3),
                                                     (dv_ref, carry_v, dvm_ref, dv3)):
                    out_ref[:, hs] = carry[:, hs] + d3[0]
                    carry[:, hs] = d3[1]
                    meta_ref[:, hs] += d3[2]

        @pl.when(n == nb)
        def _():
            dk_ref[...] = carry_k[...]
            dv_ref[...] = carry_v[...]

    kv_shape = jax.ShapeDtypeStruct((lp, KV_W), f32)
    one_shape = jax.ShapeDtypeStruct((BLOCK, KV_W), f32)
    return pl.pallas_call(
        body, name=name, grid=(nb + 1,),
        in_specs=[pl.BlockSpec((BLOCK, Q_W), cur), kv(prev), kv(cur), kv(meta), kv(prev), kv(cur), kv(meta),
                  _full((1, Q_HEADS)), pl.BlockSpec((BLOCK, Q_W), cur)],
        out_specs=[pl.BlockSpec((BLOCK, Q_W), cur), kv(behind), kv(behind), kv(meta), kv(meta),
                   _full((1, Q_HEADS))],
        out_shape=[jax.ShapeDtypeStruct((lp, Q_W), f32), kv_shape, kv_shape, one_shape, one_shape,
                   jax.ShapeDtypeStruct((1, Q_HEADS), f32)],
        scratch_shapes=[pltpu.VMEM((BLOCK, KV_W), f32), pltpu.VMEM((BLOCK, KV_W), f32)],
        compiler_params=_params(("arbitrary",)),
    )(q, k, k, k, v, v, v, sinks, do)


@jax.custom_vjp
def _known_inverse(l, x):
    return x


def _known_inverse_fwd(l, x):
    return x, x


def _known_inverse_bwd(x, ct):
    return _dot(_dot(x, ct, "tn"), x, "nt"), jnp.zeros_like(x)


_known_inverse.defvjp(_known_inverse_fwd, _known_inverse_bwd)


@jax.custom_vjp
def _decayed(x, c):
    return (x * jnp.exp(c)).astype(bf16).astype(f32)


def _decayed_fwd(x, c):
    e = jnp.exp(c)
    out = (x * e).astype(bf16).astype(f32)
    return out, (e, out)


def _decayed_bwd(res, ct):
    e, out = res
    return ct * e, ct * out


_decayed.defvjp(_decayed_fwd, _decayed_bwd)


@jax.custom_vjp
def _pair(x, y):
    return _dot(x, y, "nt")


def _pair_fwd(x, y):
    return _dot(x, y, "nt"), (x, y)


def _pair_bwd(res, ct):
    x, y = res
    hi = ct.astype(bf16)
    lo = (ct - hi.astype(f32)).astype(bf16)
    return _dot(hi, y) + _dot(lo, y), _dot(hi, x, "tn") + _dot(lo, x, "tn")


_pair.defvjp(_pair_fwd, _pair_bwd)


def _scan_chunk(s0, r, lw, k, v, a, b, inv=None):
    t = r[0].shape[0]
    ii = lax.broadcasted_iota(jnp.int32, (t, t), 0)
    jj = lax.broadcasted_iota(jnp.int32, (t, t), 1)
    incl = jj <= ii
    strict = jj < ii
    tri = incl.astype(f32)
    eye = jnp.where(ii == jj, 1.0, 0.0)
    cl = [_const_dot(tri, x) for x in lw]
    mid = [c[t // 2 - 1:t // 2, :] for c in cl]
    s0 = [s * jnp.exp(m) for s, m in zip(s0, mid)]
    cl = [c - m for c, m in zip(cl, mid)]
    rt = [_decayed(x, c) for x, c in zip(r, cl)]
    at = [_decayed(x, c - l) for x, c, l in zip(a, cl, lw)]
    bt = [_decayed(x, -c) for x, c in zip(b, cl)]
    kt = [_decayed(x, -c) for x, c in zip(k, cl)]
    l_ab = [jnp.where(strict, _pair(x, y), 0.0) for x, y in zip(at, bt)]
    l_ak = [jnp.where(strict, _pair(x, y), 0.0) for x, y in zip(at, kt)]
    r_b = [jnp.where(incl, _pair(x, y), 0.0) for x, y in zip(rt, bt)]
    r_k = [jnp.where(incl, _pair(x, y), 0.0) for x, y in zip(rt, kt)]
    if inv is None:
        inv = [eye + x for x in l_ab]
        pw = l_ab
        for _ in range(int(math.log2(t)) - 1):
            pw = [_dot(x, x) for x in pw]
            inv = [x + _dot(x, y) for x, y in zip(inv, pw)]
    else:
        inv = [_known_inverse(x, y) for x, y in zip(l_ab, inv)]
    rhs = [_dot(x, s, "nt") + _dot(m, y) for x, s, m, y in zip(at, s0, l_ak, v)]
    u = [_dot(x, y) for x, y in zip(inv, rhs)]
    y_s = [_dot(x, s, "nt") for x, s in zip(rt, s0)]
    y = [ys + _dot(m, uu) + _dot(n, vv) for ys, m, uu, n, vv in zip(y_s, r_b, u, r_k, v)]
    grow = [s + _dot(uu, x, "tn") + _dot(vv, z, "tn") for s, uu, x, vv, z in zip(s0, u, bt, v, kt)]
    s1 = [g * jnp.exp(c[t - 1:t, :]) for g, c in zip(grow, cl)]
    return y, s1, inv


def _head_rows(h):
    return slice(h * RWKV_HEAD, (h + 1) * RWKV_HEAD)


def _per_head(ref):
    return [ref[:, _head_rows(h)] for h in range(RWKV_HEADS)]


def _scan(r, lw, k, v, a, b, *, name):
    lp = r.shape[0]
    nc = lp // CHUNK
    row = pl.BlockSpec((CHUNK, RWKV_DIM), lambda c: (c, 0))

    def body(r_ref, lw_ref, k_ref, v_ref, a_ref, b_ref, y_ref, s_ref, inv_ref, state):
        @pl.when(pl.program_id(0) == 0)
        def _():
            state[...] = jnp.zeros_like(state)

        s_ref[...] = state[...]
        s0 = [state[_head_rows(h), :] for h in range(RWKV_HEADS)]
        y, s1, inv = _scan_chunk(s0, *[_per_head(ref) for ref in (r_ref, lw_ref, k_ref, v_ref, a_ref, b_ref)])
        for h in range(RWKV_HEADS):
            y_ref[:, _head_rows(h)] = y[h]
            state[_head_rows(h), :] = s1[h]
            inv_ref[h * CHUNK:(h + 1) * CHUNK, :] = inv[h].astype(inv_ref.dtype)

    return pl.pallas_call(
        body, name=name, grid=(nc,), in_specs=[row] * 6,
        out_specs=[row, pl.BlockSpec((RWKV_DIM, RWKV_HEAD), lambda c: (c, 0)),
                   pl.BlockSpec((RWKV_HEADS * CHUNK, CHUNK), lambda c: (c, 0))],
        out_shape=[jax.ShapeDtypeStruct((lp, RWKV_DIM), f32), jax.ShapeDtypeStruct((nc * RWKV_DIM, RWKV_HEAD), f32),
                   jax.ShapeDtypeStruct((nc * RWKV_HEADS * CHUNK, CHUNK), bf16)],
        scratch_shapes=[pltpu.VMEM((RWKV_DIM, RWKV_HEAD), f32)],
        compiler_params=_params(("arbitrary",)),
    )(r, lw, k, v, a, b)


def _scan_bwd(r, lw, k, v, a, b, states, inverses, dy, *, name):
    lp = r.shape[0]
    nc = lp // CHUNK
    back = lambda c: (nc - 1 - c, 0)
    row = pl.BlockSpec((CHUNK, RWKV_DIM), back)

    def body(r_ref, lw_ref, k_ref, v_ref, a_ref, b_ref, s_ref, inv_ref, dy_ref,
             dr_ref, dlw_ref, dk_ref, dv_ref, da_ref, db_ref, dstate):
        @pl.when(pl.program_id(0) == 0)
        def _():
            dstate[...] = jnp.zeros_like(dstate)

        outs = (dr_ref, dlw_ref, dk_ref, dv_ref, da_ref, db_ref)
        s0 = [s_ref[_head_rows(h), :] for h in range(RWKV_HEADS)]
        inv = [inv_ref[h * CHUNK:(h + 1) * CHUNK, :].astype(f32) for h in range(RWKV_HEADS)]
        _, vjp = jax.vjp(lambda *args: _scan_chunk(*args, inv=inv)[:2], s0,
                         *[_per_head(ref) for ref in (r_ref, lw_ref, k_ref, v_ref, a_ref, b_ref)])
        g = vjp((_per_head(dy_ref), [dstate[_head_rows(h), :] for h in range(RWKV_HEADS)]))
        for h in range(RWKV_HEADS):
            dstate[_head_rows(h), :] = g[0][h]
            for o_ref, gv in zip(outs, g[1:]):
                o_ref[:, _head_rows(h)] = gv[h]

    shape = jax.ShapeDtypeStruct((lp, RWKV_DIM), f32)
    return pl.pallas_call(
        body, name=name, grid=(nc,),
        in_specs=[row] * 6 + [pl.BlockSpec((RWKV_DIM, RWKV_HEAD), back),
                              pl.BlockSpec((RWKV_HEADS * CHUNK, CHUNK), back), row],
        out_specs=[row] * 6, out_shape=[shape] * 6,
        scratch_shapes=[pltpu.VMEM((RWKV_DIM, RWKV_HEAD), f32)],
        compiler_params=_params(("arbitrary",)),
    )(r, lw, k, v, a, b, states, inverses, dy)


def _loss_head(h2, target, g_final, *, name):
    lp = h2.shape[0]
    tm = BLOCK
    front_tiles = FRONT // tm

    def body(h_ref, t_ref, g_ref, loss_ref, dh_ref, dg_ref):
        i = pl.program_id(0)
        real = i >= front_tiles

        def tile_loss(hv, gv):
            err = _rms(hv, gv) - t_ref[...]
            return jnp.where(real, 0.5 * jnp.sum(jnp.mean(err * err, axis=-1, keepdims=True)), 0.0)

        loss, (dh, dg) = jax.value_and_grad(tile_loss, argnums=(0, 1))(h_ref[...], g_ref[...])

        @pl.when(i == 0)
        def _():
            loss_ref[...] = jnp.zeros_like(loss_ref)
            dg_ref[...] = jnp.zeros_like(dg_ref)

        loss_ref[...] += jnp.full(loss_ref.shape, loss, f32)
        dg_ref[...] += dg
        dh_ref[...] = dh

    return pl.pallas_call(
        body, name=name, grid=(lp // tm,),
        in_specs=[pl.BlockSpec((tm, D_MODEL), lambda i: (i, 0)),
                  pl.BlockSpec((tm, D_MODEL), lambda i: (jnp.maximum(i - front_tiles, 0), 0)),
                  _full(g_final.shape)],
        out_specs=[_full((8, 128)), pl.BlockSpec((tm, D_MODEL), lambda i: (i, 0)), _full(g_final.shape)],
        out_shape=[jax.ShapeDtypeStruct((8, 128), f32), jax.ShapeDtypeStruct((lp, D_MODEL), f32),
                   jax.ShapeDtypeStruct(g_final.shape, f32)],
        compiler_params=_params(("arbitrary",)),
    )(h2, target, g_final)


def _input_norm_bwd(h0, g, du, dh1, *, name):
    lp = h0.shape[0]
    tm = FRONT

    def body(h_ref, g_ref, du_ref, dh1_ref, dx_ref, front_ref, dg_ref):
        i = pl.program_id(0)
        _, vjp = jax.vjp(lambda hv, gv: (_rms(hv, gv), hv), h_ref[...], g_ref[...])
        dh, dg = vjp((du_ref[...].astype(f32), dh1_ref[...]))

        @pl.when(i == 0)
        def _():
            dg_ref[...] = jnp.zeros_like(dg_ref)
            front_ref[...] = dh

        dg_ref[...] += dg
        dx_ref[...] = dh

    tile = pl.BlockSpec((tm, D_MODEL), lambda i: (i, 0))
    return pl.pallas_call(
        body, name=name, grid=(lp // tm,),
        in_specs=[tile, _full(g.shape), tile, tile],
        out_specs=[pl.BlockSpec((tm, D_MODEL), lambda i: (jnp.maximum(i - 1, 0), 0)), _full((tm, D_MODEL)),
                   _full(g.shape)],
        out_shape=[jax.ShapeDtypeStruct((lp - tm, D_MODEL), f32), jax.ShapeDtypeStruct((tm, D_MODEL), f32),
                   jax.ShapeDtypeStruct(g.shape, f32)],
        compiler_params=_params(("arbitrary",)),
    )(h0, g, du, dh1)


def _local_step(x, target, meta, p, early_weights=None, late_weights=None, emit=None):
    emit = emit or (lambda group, grads: 0.0)
    seq = x.shape[0]
    lp = seq + FRONT
    h0 = jnp.concatenate([jnp.zeros((PAD, D_MODEL), f32), meta, x], axis=0)
    cos_t, sin_t, swap = _rope_tables(lp)
    hsum = _head_sum_matrix(RWKV_DIM, RWKV_HEAD)
    hmean = hsum / RWKV_HEAD
    post_params = [p["ln_w"], p["ln_b"], p["r_k"], hmean]

    (u,) = _rowwise(lambda hv, g: (_rms(hv, g),), [h0], [p["norm_mix_g"]], [(D_MODEL, bf16)], name="norm_mix")
    if early_weights is not None:
        p = {**p, **early_weights(u)}
    prep_params = [p["w0"], p["w2"], p["a0"], p["a2"], p["g2"], p["k_k"], p["k_a"], hsum]
    qkv, p_rkv, p_lora, gates = _proj_in(u, p["w_in_lr"], p["b_in"], [ATTN_PROJ, RKV_W, LORA_W, 2 * D_MODEL],
                                         name="proj_in", zero_rows_below=PAD)

    q, k, v = _rowwise(_attn_prep, [qkv, cos_t, sin_t], [swap], [(Q_W, bf16), (KV_W, bf16), (KV_W, bf16)],
                       name="attn_prep")
    y_attn = _attention(q, k, v, p["sinks"], name="attention")

    mix_rkv, mix_lora = p["mix"][:, :RKV_W], p["mix"][:, RKV_W:]
    r_, lw_, k_, v_, a_, b_, g_ = _mixer_inputs([p_rkv, p_lora], [mix_rkv, mix_lora], prep_params,
                                                name="mixer_inputs")
    y_scan, states, inverses = _scan(r_, lw_, k_, v_, a_, b_, name="wkv_scan")
    (y_rwkv,) = _rowwise(_rwkv_post, [y_scan, r_, k_, v_, g_], post_params, [(RWKV_DIM, bf16)], name="rwkv_post")

    if late_weights is not None:
        p = {**p, **late_weights(y_rwkv)}
    br_a, br_r, merged = _branch_merge(y_attn, y_rwkv, p["w_br_attn_t"], p["w_br_rwkv_t"], gates, name="branch_merge")
    h1 = _mm(merged, p["w_o"], "nn", name="out_proj", add=h0)
    (f,) = _rowwise(lambda hv, g: (_rms(hv, g),), [h1], [p["norm_ffn_g"]], [(D_MODEL, bf16)], name="norm_ffn")
    gate, up, act = _ffn_in(f, p["w_gate_t"], p["w_up_t"], name="ffn_in")
    h2 = _mm(act, p["w_down"], "nn", name="ffn_down", add=h1)

    loss8, dh2, d_final_g = _loss_head(h2, target, p["norm_final_g"], name="loss_head")
    dgate, dup = _ffn_in_bwd(dh2, p["w_down"], gate, up, name="ffn_in_bwd")
    d_w_down = _mm_tn(act, dh2, name="dw_down")
    d_w_gate_t = _mm_tn(dgate, f, name="dw_gate")
    d_w_up_t = _mm_tn(dup, f, name="dw_up")
    zero = emit("ffn", dict(w_down=d_w_down, w_gate_t=d_w_gate_t, w_up_t=d_w_up_t))
    df = _mm_sum([dgate, dup], [p["w_gate_t"], p["w_up_t"]], name="d_f")
    dh1, d_ffn_g = _rowwise_bwd(lambda hv, g: (_rms(hv, g), hv), [h1], [p["norm_ffn_g"] + zero], [[df], [dh2]],
                                name="norm_ffn_bwd", diff_rows=[True], diff_params=[True])
    dgates, dbr_a, dbr_r = _branch_merge_bwd(dh1, p["w_o"], gates, br_a, br_r, name="branch_merge_bwd")
    d_w_o = _mm_tn(merged, dh1, name="dw_o")
    d_w_br_attn_t = _mm_tn(dbr_a, y_attn, name="dw_br_attn")
    d_w_br_rwkv_t = _mm_tn(dbr_r, y_rwkv, name="dw_br_rwkv")
    zero = emit("branch", dict(w_o=d_w_o, w_br_attn_t=d_w_br_attn_t, w_br_rwkv_t=d_w_br_rwkv_t))
    dy_attn = _mm(dbr_a, p["w_br_attn_t"], "nn", name="d_y_attn")
    dy_rwkv = _mm(dbr_r, p["w_br_rwkv_t"], "nn", name="d_y_rwkv")

    post_params = [p["ln_w"] + zero, p["ln_b"], p["r_k"], hmean]
    res = _rowwise_bwd(_rwkv_post, [y_scan, r_, k_, v_, g_], post_params, [[dy_rwkv]], name="rwkv_post_bwd",
                       diff_rows=[True] * 5, diff_params=[True, True, True, False])
    dy_scan, dr_p, dk_p, dv_p, dg_p, d_ln_w, d_ln_b, d_r_k = res
    dr_s, dlw_s, dk_s, dv_s, da_s, db_s = _scan_bwd(r_, lw_, k_, v_, a_, b_, states, inverses, dy_scan,
                                                    name="wkv_scan_bwd")
    res = _mixer_inputs_bwd([p_rkv, p_lora], [mix_rkv, mix_lora], prep_params,
                            [[dr_s, dr_p], [dlw_s], [dk_s, dk_p], [dv_s, dv_p], [da_s], [db_s], [dg_p]],
                            name="mixer_inputs_bwd")
    dp_rkv, dp_lora, d_mix_rkv, d_mix_lora, d_w0, d_w2, d_a0, d_a2, d_g2, d_k_k, d_k_a = res

    dq, dk, dv, dkm, dvm, d_sinks = _attention_bwd(q, k, v, p["sinks"], dy_attn, name="attention_bwd")
    rest = jnp.zeros((lp - BLOCK, KV_W), f32)
    dkm, dvm = jnp.concatenate([dkm, rest], axis=0), jnp.concatenate([dvm, rest], axis=0)
    (dqkv,) = _rowwise_bwd(_attn_prep, [qkv, cos_t, sin_t], [swap], [[dq], [dk, dkm], [dv, dvm]], name="attn_prep_bwd",
                           diff_rows=[True, False, False], diff_params=[False], out_dtypes=[bf16])

    d_w_qkv_t, db_qkv = _mm_tn(dqkv, u, name="dw_qkv", colsum=True)
    d_w_rkv_t, db_rkv = _mm_tn(dp_rkv, u, name="dw_rkv", colsum=True)
    d_w_lora_t, db_lora = _mm_tn(dp_lora, u, name="dw_lora", colsum=True)
    d_w_gates_t, db_gates = _mm_tn(dgates, u, name="dw_gates", colsum=True)
    d_w_in_t = jnp.concatenate([d_w_qkv_t, d_w_rkv_t, d_w_lora_t, d_w_gates_t], axis=0)
    zero = emit("input", dict(w_in_t=d_w_in_t, g2=d_g2, w2=d_w2, a2=d_a2))
    du = _proj_in_bwd([dqkv, dp_rkv, dp_lora, dgates], p["w_in_lr"], name="d_u")
    dx, d_front, d_mix_g = _input_norm_bwd(h0, p["norm_mix_g"] + zero, du, dh1, name="norm_mix_bwd")

    grads = dict(
        w_in_t=d_w_in_t,
        b_in=jnp.concatenate([db_qkv, db_rkv, db_lora, db_gates], axis=1),
        mix=jnp.concatenate([d_mix_rkv, d_mix_lora], axis=1),
        norm_mix_g=d_mix_g, sinks=d_sinks, w0=d_w0, w2=d_w2, a0=d_a0, a2=d_a2, g2=d_g2, k_k=d_k_k, k_a=d_k_a,
        r_k=d_r_k, ln_w=d_ln_w, ln_b=d_ln_b, w_br_attn_t=d_w_br_attn_t, w_br_rwkv_t=d_w_br_rwkv_t, w_o=d_w_o,
        norm_ffn_g=d_ffn_g, w_gate_t=d_w_gate_t, w_up_t=d_w_up_t, w_down=d_w_down, norm_final_g=d_final_g,
        meta=d_front[PAD:],
    )
    return loss8[0, 0], dx, grads


def _position():
    return lax.axis_index("x"), lax.axis_index("y"), lax.axis_index("c")


def _other_chips(x, y):
    return [(1 - x, y), (x, 1 - y), (1 - x, 1 - y)]


_HBM = pl.BlockSpec(memory_space=pltpu.HBM)
_SEM = pl.BlockSpec(memory_space=pltpu.SEMAPHORE)
_EFFECT = pltpu.SideEffectType.DATAFLOW_SIDE_EFFECTING


def _landing_zone(src, kind):
    shape = {"whole": (N_CHIPS,) + src.shape, "half": (2, N_CHIPS, src.shape[0], src.shape[1] // 2),
             "slab": (3,) + src.shape[1:], "sibling": src.shape}[kind]
    return lax.empty(shape, src.dtype)


def _copies_per_source(kind):
    return 1 if kind == "sibling" else 3


def _chip_copies(src_refs, land_refs, send_sems, recv_sems, kind):
    x, y, c = _position()
    if kind == "sibling":
        return [pltpu.make_async_remote_copy(
            src_ref=src, dst_ref=land, send_sem=send_sems.at[a], recv_sem=recv_sems.at[a],
            device_id=(x, y, 1 - c), device_id_type=MESH) for a, (src, land) in enumerate(zip(src_refs, land_refs))]
    copies = []
    for a, (src, land) in enumerate(zip(src_refs, land_refs)):
        for j, (px, py) in enumerate(_other_chips(x, y)):
            if kind == "whole":
                src_ref, dst_ref = src, land.at[2 * x + y]
            elif kind == "half":
                half = src.shape[1] // 2
                src_ref, dst_ref = src.at[:, pl.ds(pl.multiple_of(c * half, half), half)], land.at[c, 2 * x + y]
            else:
                src_ref, dst_ref = src.at[2 * px + py], land.at[j]
            copies.append(pltpu.make_async_remote_copy(
                src_ref=src_ref, dst_ref=dst_ref, send_sem=send_sems.at[3 * a + j], recv_sem=recv_sems.at[3 * a + j],
                device_id=(px, py, c), device_id_type=MESH))
    return copies


def _exchange_start(srcs, *, kind, name):
    n = len(srcs)
    lands = [_landing_zone(s, kind) for s in srcs]

    def body(*refs):
        for cp in _chip_copies(refs[:n], refs[n:2 * n], refs[2 * n], refs[2 * n + 1], kind):
            cp.start()
        refs[-1][...] = jnp.zeros_like(refs[-1])

    res = pl.pallas_call(
        body, name=name,
        out_shape=(pltpu.SemaphoreType.DMA((_copies_per_source(kind) * n,)),
                   pltpu.SemaphoreType.DMA((_copies_per_source(kind) * n,)),
                   *[pltpu.HBM(a.shape, a.dtype) for a in srcs + lands], jax.ShapeDtypeStruct((8, 128), f32)),
        in_specs=[_HBM] * (2 * n),
        out_specs=(_SEM, _SEM, *[_HBM] * (2 * n), pl.BlockSpec(memory_space=pltpu.VMEM)),
        input_output_aliases={i: 2 + i for i in range(2 * n)},
        compiler_params=pltpu.CompilerParams(has_side_effects=_EFFECT),
    )(*[pltpu.with_memory_space_constraint(a, pltpu.HBM) for a in srcs + lands])
    return res[0], res[1], list(res[2:2 + n]), list(res[2 + n:2 + 2 * n]), res[-1]


def _exchange_wait(handle, after, *, kind, name):
    send_sems, recv_sems, srcs, lands, _ = handle
    n = len(srcs)

    def body(*refs):
        for cp in _chip_copies(refs[:n], refs[n:2 * n], refs[2 * n], refs[2 * n + 1], kind):
            cp.wait_send()
            cp.wait_recv()

    res = pl.pallas_call(
        body, name=name,
        out_shape=tuple(pltpu.HBM(a.shape, a.dtype) for a in srcs + lands),
        in_specs=[_HBM] * (2 * n) + [_SEM, _SEM, pl.BlockSpec(memory_space=pl.ANY)],
        out_specs=tuple([_HBM] * (2 * n)),
        input_output_aliases={i: i for i in range(2 * n)},
        compiler_params=pltpu.CompilerParams(has_side_effects=_EFFECT),
    )(*srcs, *lands, send_sems, recv_sems, after)
    return list(res[:n]), list(res[n:])


def _sum_own_and_received(g, recv, *, name):
    _, r, w = g.shape
    tm = _tile(r)
    if g.dtype == bf16 and tm % 16:
        tm = r
    x, y, _ = _position()
    me = jnp.reshape(2 * x + y, (1,)).astype(jnp.int32)

    def body(me_ref, g_ref, r_ref, o_ref):
        o_ref[...] = (g_ref[0].astype(f32) + r_ref[0].astype(f32)) + (r_ref[1].astype(f32) + r_ref[2].astype(f32))

    return pl.pallas_call(
        body, name=name,
        grid_spec=pltpu.PrefetchScalarGridSpec(
            num_scalar_prefetch=1, grid=(r // tm,),
            in_specs=[pl.BlockSpec((1, tm, w), lambda i, me_ref: (me_ref[0], i, 0)),
                      pl.BlockSpec((3, tm, w), lambda i, me_ref: (0, i, 0))],
            out_specs=pl.BlockSpec((tm, w), lambda i, me_ref: (i, 0))),
        out_shape=jax.ShapeDtypeStruct((r, w), f32),
        compiler_params=_params(("parallel",)),
    )(me, g, recv)


def _swap_cores(arrs, *, name):
    n = len(arrs)

    def body(*refs):
        x, y, c = _position()
        copies = [pltpu.make_async_remote_copy(
            src_ref=refs[i], dst_ref=refs[n + i], send_sem=refs[2 * n].at[i], recv_sem=refs[2 * n + 1].at[i],
            device_id=(x, y, 1 - c), device_id_type=MESH) for i in range(n)]
        for cp in copies:
            cp.start()
        for cp in copies:
            cp.wait_recv()
        for cp in copies:
            cp.wait_send()

    return pl.pallas_call(
        body, name=name,
        in_specs=[pl.BlockSpec(memory_space=pl.ANY)] * n,
        out_specs=[pl.BlockSpec(memory_space=pl.ANY)] * n,
        out_shape=[jax.ShapeDtypeStruct(a.shape, a.dtype) for a in arrs],
        scratch_shapes=[pltpu.SemaphoreType.DMA((n,)), pltpu.SemaphoreType.DMA((n,))],
    )(*arrs)


def _swap_halves(zone, *, name):
    def body(z_ref, o_ref, send_sems, recv_sems):
        x, y, c = _position()
        mine = [pltpu.make_async_remote_copy(
            src_ref=o_ref.at[c, 2 * px + py], dst_ref=o_ref.at[c, 2 * px + py], send_sem=send_sems.at[j],
            recv_sem=recv_sems.at[j], device_id=(x, y, 1 - c), device_id_type=MESH)
            for j, (px, py) in enumerate(_other_chips(x, y))]
        for cp in mine:
            cp.start()
        for j, (px, py) in enumerate(_other_chips(x, y)):
            pltpu.make_async_remote_copy(
                src_ref=o_ref.at[c, 2 * px + py], dst_ref=o_ref.at[1 - c, 2 * px + py], send_sem=send_sems.at[j],
                recv_sem=recv_sems.at[j], device_id=(x, y, 1 - c), device_id_type=MESH).wait_recv()
        for cp in mine:
            cp.wait_send()

    return pl.pallas_call(
        body, name=name,
        in_specs=[pl.BlockSpec(memory_space=pl.ANY)], out_specs=pl.BlockSpec(memory_space=pl.ANY),
        out_shape=jax.ShapeDtypeStruct(zone.shape, zone.dtype), input_output_aliases={0: 0},
        scratch_shapes=[pltpu.SemaphoreType.DMA((3,)), pltpu.SemaphoreType.DMA((3,))],
    )(zone)


def _all_reduce_small(a, after, *, name):
    rows, w = a.shape

    def body(a_ref, after_ref, o_ref, buf, send_sems, recv_sems):
        x, y, c = _position()
        me = 4 * x + 2 * y + c
        buf[0] = a_ref[...]
        sends = []
        for rel in range(1, N_DEV):
            peer = ((1 - x) if rel & 4 else x, (1 - y) if rel & 2 else y, (1 - c) if rel & 1 else c)
            cp = pltpu.make_async_remote_copy(
                src_ref=a_ref, dst_ref=buf.at[rel], send_sem=send_sems.at[rel - 1], recv_sem=recv_sems.at[rel - 1],
                device_id=peer, device_id_type=MESH)
            cp.start()
            sends.append(cp)
        for cp in sends:
            cp.wait_recv()
        for cp in sends:
            cp.wait_send()
        acc = buf[jnp.bitwise_xor(me, 0)]
        for d in range(1, N_DEV):
            acc = acc + buf[jnp.bitwise_xor(me, d)]
        o_ref[...] = acc

    return pl.pallas_call(
        body, name=name,
        in_specs=[pl.BlockSpec(memory_space=pltpu.VMEM), pl.BlockSpec(memory_space=pl.ANY)],
        out_specs=pl.BlockSpec(memory_space=pltpu.VMEM),
        out_shape=jax.ShapeDtypeStruct((rows, w), f32),
        scratch_shapes=[pltpu.VMEM((N_DEV, rows, w), f32), pltpu.SemaphoreType.DMA((N_DEV - 1,)),
                        pltpu.SemaphoreType.DMA((N_DEV - 1,))],
    )(a, after)


def _adamw(w, g_parts, m, v, *, name, transposed=False):
    rows, cols = w.shape
    if transposed:
        tm = 256 if rows % 256 == 0 else rows
        g_spec = pl.BlockSpec((cols, tm), lambda i: (0, i))
    else:
        tm = _tile(rows, 256)
        g_spec = pl.BlockSpec((tm, cols), lambda i: (i, 0))
    n = len(g_parts)

    def body(*refs):
        w_ref, m_ref, v_ref = refs[0], refs[1 + n], refs[2 + n]
        g_ref, d_ref, nm_ref, nv_ref = refs[3 + n:]
        gv = refs[1][...]
        for part in refs[2:1 + n]:
            gv = gv + part[...]
        if transposed:
            gv = gv.T
        g_ref[...] = gv
        nm = ADAM_B1 * m_ref[...] + (1.0 - ADAM_B1) * gv
        nv = ADAM_B2 * v_ref[...] + (1.0 - ADAM_B2) * (gv * gv)
        m_hat = nm / (1.0 - ADAM_B1 ** ADAM_STEP)
        v_hat = nv / (1.0 - ADAM_B2 ** ADAM_STEP)
        d_ref[...] = -ADAM_LR * (m_hat / (jnp.sqrt(v_hat) + ADAM_EPS) + ADAM_WD * w_ref[...])
        nm_ref[...] = nm
        nv_ref[...] = nv

    spec = pl.BlockSpec((tm, cols), lambda i: (i, 0))
    shape = jax.ShapeDtypeStruct((rows, cols), f32)
    return pl.pallas_call(
        body, name=name, grid=(rows // tm,), in_specs=[spec] + [g_spec] * n + [spec] * 2,
        out_specs=[spec] * 4, out_shape=[shape] * 4,
        compiler_params=_params(("parallel",)),
    )(w, *g_parts, m, v)


def _pad_rows(a, rows):
    return jnp.concatenate([a, jnp.zeros((rows - a.shape[0], a.shape[1]), a.dtype)], axis=0) if rows > a.shape[0] else a


_SMALL = (("norm_mix_g", D_MODEL), ("b_in", D_IN), ("sinks", Q_HEADS), ("mix", RWKV_PROJ), ("w0", RWKV_DIM),
          ("a0", RWKV_DIM), ("k_k", RWKV_DIM), ("k_a", RWKV_DIM), ("r_k", RWKV_DIM), ("ln_w", RWKV_DIM),
          ("ln_b", RWKV_DIM), ("norm_ffn_g", D_MODEL), ("norm_final_g", D_MODEL))


def _pack_small(d):
    flat = jnp.concatenate([d[n].reshape(-1).astype(f32) for n, _ in _SMALL])
    return flat


def _unpack_small(flat):
    out, off = {}, 0
    for n, size in _SMALL:
        out[n] = flat[off:off + size]
        off += size
    return out


_SMALL_TOTAL = sum(s for _, s in _SMALL)


def kernel(x, meta_tokens, norm_mix_g, w_in, b_in, attn_sinks, rwkv_mix, rwkv_w0, rwkv_w2, rwkv_a0, rwkv_a2, rwkv_g2, rwkv_k_k, rwkv_k_a, rwkv_r_k, rwkv_ln_w, rwkv_ln_b, w_br_attn, w_br_rwkv, w_o, norm_ffn_g, w_ffn_gate, w_ffn_up, w_ffn_down, norm_final_g, loss_target, m_meta_tokens, m_norm_mix_g, m_w_in, m_b_in, m_attn_sinks, m_rwkv_mix, m_rwkv_w0, m_rwkv_w2, m_rwkv_a0, m_rwkv_a2, m_rwkv_g2, m_rwkv_k_k, m_rwkv_k_a, m_rwkv_r_k, m_rwkv_ln_w, m_rwkv_ln_b, m_w_br_attn, m_w_br_rwkv, m_w_o, m_norm_ffn_g, m_w_ffn_gate, m_w_ffn_up, m_w_ffn_down, m_norm_final_g, v_meta_tokens, v_norm_mix_g, v_w_in, v_b_in, v_attn_sinks, v_rwkv_mix, v_rwkv_w0, v_rwkv_w2, v_rwkv_a0, v_rwkv_a2, v_rwkv_g2, v_rwkv_k_k, v_rwkv_k_a, v_rwkv_r_k, v_rwkv_ln_w, v_rwkv_ln_b, v_w_br_attn, v_w_br_rwkv, v_w_o, v_norm_ffn_g, v_w_ffn_gate, v_w_ffn_up, v_w_ffn_down, v_norm_final_g):
    names = ("meta_tokens", "norm_mix_g", "w_in", "b_in", "attn_sinks", "rwkv_mix", "rwkv_w0", "rwkv_w2", "rwkv_a0",
             "rwkv_a2", "rwkv_g2", "rwkv_k_k", "rwkv_k_a", "rwkv_r_k", "rwkv_ln_w", "rwkv_ln_b", "w_br_attn",
             "w_br_rwkv", "w_o", "norm_ffn_g", "w_ffn_gate", "w_ffn_up", "w_ffn_down", "norm_final_g")
    w_all = dict(zip(names, (meta_tokens, norm_mix_g, w_in, b_in, attn_sinks, rwkv_mix, rwkv_w0, rwkv_w2, rwkv_a0,
                             rwkv_a2, rwkv_g2, rwkv_k_k, rwkv_k_a, rwkv_r_k, rwkv_ln_w, rwkv_ln_b, w_br_attn,
                             w_br_rwkv, w_o, norm_ffn_g, w_ffn_gate, w_ffn_up, w_ffn_down, norm_final_g)))
    m_all = dict(zip(names, (m_meta_tokens, m_norm_mix_g, m_w_in, m_b_in, m_attn_sinks, m_rwkv_mix, m_rwkv_w0,
                             m_rwkv_w2, m_rwkv_a0, m_rwkv_a2, m_rwkv_g2, m_rwkv_k_k, m_rwkv_k_a, m_rwkv_r_k,
                             m_rwkv_ln_w, m_rwkv_ln_b, m_w_br_attn, m_w_br_rwkv, m_w_o, m_norm_ffn_g, m_w_ffn_gate,
                             m_w_ffn_up, m_w_ffn_down, m_norm_final_g)))
    v_all = dict(zip(names, (v_meta_tokens, v_norm_mix_g, v_w_in, v_b_in, v_attn_sinks, v_rwkv_mix, v_rwkv_w0,
                             v_rwkv_w2, v_rwkv_a0, v_rwkv_a2, v_rwkv_g2, v_rwkv_k_k, v_rwkv_k_a, v_rwkv_r_k,
                             v_rwkv_ln_w, v_rwkv_ln_b, v_w_br_attn, v_w_br_rwkv, v_w_o, v_norm_ffn_g, v_w_ffn_gate,
                             v_w_ffn_up, v_w_ffn_down, v_norm_final_g)))
    cx, cy, _ = _position()
    chip = 2 * cx + cy

    t_of = dict(w_in_t="w_in", w_gate_t="w_ffn_gate", w_up_t="w_ffn_up", w_br_attn_t="w_br_attn",
                w_br_rwkv_t="w_br_rwkv", g2_t="rwkv_g2", w2_t="rwkv_w2", a2_t="rwkv_a2")
    plain_of = dict(w_down="w_ffn_down", w_o="w_o")
    meta_cols = meta_tokens.shape[1]

    def shard(k):
        return (w_all[t_of[k]][0].T if k in t_of else w_all[plain_of[k]][0]).astype(bf16)

    def whole(zone, own):
        return lax.dynamic_update_slice_in_dim(zone, own[None], chip, axis=0).reshape(-1, own.shape[-1])

    tiny = ("g2_t", "w2_t", "a2_t")
    late = ("w_gate_t", "w_up_t", "w_down", "w_o", "w_br_attn_t", "w_br_rwkv_t")
    w_in_own = shard("w_in_t")
    w_in_rows, w_in_cols = w_in_own.shape
    tiny_h = _exchange_start([shard(k) for k in tiny] + [meta_tokens], kind="whole", name="gather_tiny_start")
    w_in_h = _exchange_start([w_in_own + tiny_h[4][0, 0].astype(bf16)], kind="half", name="gather_w_in_start")
    behind = w_in_h[4][0, 0].astype(bf16)
    late_h = _exchange_start([shard(k) + behind for k in late], kind="whole", name="gather_late_start")
    own, zones = _exchange_wait(tiny_h, late_h[4], kind="whole", name="gather_tiny_wait")
    got = {k: whole(z, o) for k, z, o in zip(tiny, zones, own)}
    meta_full = whole(zones[-1], own[-1]).reshape(N_CHIPS, N_META, meta_cols).transpose(1, 0, 2).reshape(N_META, -1)
    p = dict(
        g2=got["g2_t"].T.astype(f32), w2=got["w2_t"].T.astype(f32), a2=got["a2_t"].T.astype(f32),
        b_in=b_in, sinks=attn_sinks, mix=rwkv_mix, w0=rwkv_w0, a0=rwkv_a0, k_k=rwkv_k_k, k_a=rwkv_k_a,
        r_k=rwkv_r_k.reshape(1, RWKV_DIM), ln_w=rwkv_ln_w, ln_b=rwkv_ln_b, norm_mix_g=norm_mix_g,
        norm_ffn_g=norm_ffn_g, norm_final_g=norm_final_g.reshape(1, D_MODEL),
    )

    def early_weights(after):
        own_h, zones_h = _exchange_wait(w_in_h, after, kind="half", name="gather_w_in_wait")
        zone = _swap_halves(zones_h[0], name="swap_w_in_halves")
        own_halves = own_h[0].reshape(w_in_rows, 2, w_in_cols // 2).transpose(1, 0, 2)[:, None]
        zone = lax.dynamic_update_slice(zone, own_halves, (0, chip, 0, 0))
        return dict(w_in_lr=zone.reshape(2, N_CHIPS * w_in_rows, w_in_cols // 2))

    def late_weights(after):
        own_l, zones_l = _exchange_wait(late_h, after, kind="whole", name="gather_late_wait")
        return {k: whole(z, o) for k, z, o in zip(late, zones_l, own_l)}

    started = {}

    def emit(group, grads_):
        keys = list(grads_)
        slabs = []
        for k in keys:
            a = grads_[k].T if k in ("g2", "w2", "a2") else grads_[k]
            slabs.append(a.reshape(N_CHIPS, a.shape[0] // N_CHIPS, a.shape[1]))
        started[group] = (keys, _exchange_start(slabs, kind="slab", name="scatter_" + group + "_start"))
        return started[group][1][4][0, 0]

    loss, dx, g = _local_step(x[0], loss_target[0], meta_full, p, early_weights, late_weights, emit)

    grads, delta, new_m, new_v = {}, {}, {}, {}
    in_grad_layout = ("w_in_t", "w_gate_t", "w_up_t")
    weight_of = {**t_of, **plain_of}

    def partial_sums(groups, after):
        parts = {}
        for group in groups:
            keys, handle = started[group]
            slabs, lands = _exchange_wait(handle, after, kind="slab", name="scatter_" + group + "_wait")
            parts.update({k: _sum_own_and_received(s, l, name="sum_chips_" + k) for k, s, l in zip(keys, slabs, lands)})
        return parts

    def update(keys, mine, theirs):
        for k, part, other in zip(keys, mine, theirs):
            both = [part, other]
            k = k + "_t" if k in ("g2", "w2", "a2") else k
            n = weight_of[k]
            shape2 = w_all[n].shape[1:]
            w_, m_, v_ = (a.reshape(shape2) for a in (w_all[n], m_all[n], v_all[n]))
            if k in in_grad_layout:
                res = [t.T for t in _adamw(w_.T, both, m_.T, v_.T, name="adamw_" + n)]
            else:
                res = _adamw(w_, both, m_, v_, name="adamw_" + n, transposed=k in t_of)
            grads[n], delta[n], new_m[n], new_v[n] = (t.reshape(w_all[n].shape) for t in res)
        return delta[n]

    parts_a = partial_sums(("ffn", "branch"), dx)
    swap_a = _exchange_start(list(parts_a.values()), kind="sibling", name="swap_cores_a_start")
    small = jnp.concatenate([_pack_small(g), loss.reshape(1)])
    small_rows = -(-small.shape[0] // PACK_W)
    small = jnp.concatenate([small, jnp.zeros((small_rows * PACK_W - small.shape[0],), f32)]).reshape(small_rows, PACK_W)
    small_rows8 = -(-(small_rows + N_META) // 8) * 8
    reduced = _all_reduce_small(_pad_rows(jnp.concatenate([g["meta"], small], axis=0), small_rows8), swap_a[4],
                                name="reduce_small")
    done = update(list(parts_a), *_exchange_wait(swap_a, reduced, kind="sibling", name="swap_cores_a_wait"))
    parts_b = partial_sums(("input",), done)
    update(list(parts_b), list(parts_b.values()), _swap_cores(list(parts_b.values()), name="swap_cores_b"))
    g_meta = lax.dynamic_slice_in_dim(reduced[:N_META], chip * meta_cols, meta_cols, axis=1)
    flat = reduced[N_META:N_META + small_rows].reshape(-1)
    g_small = _unpack_small(flat)
    loss_total = flat[_SMALL_TOTAL]

    small_of = dict(norm_mix_g="norm_mix_g", b_in="b_in", attn_sinks="sinks", rwkv_mix="mix", rwkv_w0="w0",
                    rwkv_a0="a0", rwkv_k_k="k_k", rwkv_k_a="k_a", rwkv_r_k="r_k", rwkv_ln_w="ln_w",
                    rwkv_ln_b="ln_b", norm_ffn_g="norm_ffn_g", norm_final_g="norm_final_g")
    grads["meta_tokens"] = g_meta
    for n, k in small_of.items():
        grads[n] = g_small[k].reshape(w_all[n].shape)

    rest = [n for n in names if n not in delta]

    def pack_rest(src):
        flat_ = jnp.concatenate([src[n].reshape(-1) for n in rest])
        rows_ = -(-flat_.shape[0] // (8 * PACK_W)) * 8
        return jnp.concatenate([flat_, jnp.ones((rows_ * PACK_W - flat_.shape[0],), f32)]).reshape(rows_, PACK_W)

    _, d_, m_, v_ = _adamw(pack_rest(w_all), [pack_rest(grads)], pack_rest(m_all), pack_rest(v_all),
                           name="adamw_small")
    off = 0
    for n in rest:
        size = w_all[n].size
        for dst, src in ((delta, d_), (new_m, m_), (new_v, v_)):
            dst[n] = src.reshape(-1)[off:off + size].reshape(w_all[n].shape)
        off += size

    return (loss_total, dx.reshape(x.shape), *[grads[n] for n in names], *[delta[n] for n in names],
            *[new_m[n] for n in names], *[new_v[n] for n in names])
```

```python
import math

import jax
import jax.numpy as jnp
from jax import lax
from jax.experimental import pallas as pl
from jax.experimental.pallas import tpu as pltpu

f32 = jnp.float32
bf16 = jnp.bfloat16

D_MODEL = 1024
N_META = 16
HEAD_DIM = 64
Q_HEADS = 8
KV_HEADS = 2
GROUP = Q_HEADS // KV_HEADS
WINDOW = 128
BLOCK = 128
ROPE_THETA = 500000.0
ROPE_DIM = HEAD_DIM // 4
RWKV_HEADS = 8
RWKV_HEAD = 64
RWKV_DIM = RWKV_HEADS * RWKV_HEAD
DECAY_LORA = 64
AAA_LORA = 64
GATE_LORA = 160
LORA_W = DECAY_LORA + AAA_LORA + GATE_LORA
RWKV_LN_EPS = 64e-5
D_FF = 2816
Q_W = Q_HEADS * HEAD_DIM
KV_W = KV_HEADS * HEAD_DIM
ATTN_PROJ = Q_W + 2 * KV_W
RKV_W = 3 * RWKV_DIM
RWKV_PROJ = RKV_W + LORA_W
D_IN = ATTN_PROJ + RWKV_PROJ + 2 * D_MODEL
RMS_EPS = 1e-6
NEG_INF = -1e30
PAD = BLOCK - N_META
FRONT = PAD + N_META

ADAM_LR = 0.001
ADAM_B1 = 0.9
ADAM_B2 = 0.999
ADAM_EPS = 1e-08
ADAM_WD = 0.01
ADAM_STEP = 10

N_CHIPS = 4
N_DEV = 8
CHUNK = 128
VMEM_LIMIT = 56 * 1024 * 1024
PACK_W = 1024
MESH = pl.DeviceIdType.MESH


def _tile(m, pref=384):
    for step in (16, 8):
        for t in range(min(m, pref) // step * step, 0, -step):
            if m % t == 0:
                return t
    return m


def _params(sem=None):
    return pltpu.CompilerParams(dimension_semantics=sem, vmem_limit_bytes=VMEM_LIMIT)


def _full(shape):
    nd = len(shape)
    return pl.BlockSpec(shape, lambda *_: (0,) * nd)


def _dot(a, b, dims="nn"):
    dn = {"nn": (((1,), (0,)), ((), ())), "nt": (((1,), (1,)), ((), ())), "tn": (((0,), (0,)), ((), ()))}[dims]
    return lax.dot_general(a.astype(bf16), b.astype(bf16), dn, preferred_element_type=f32)


def _two_pass(x, m, dims="nn"):
    x_hi = x.astype(bf16)
    x_lo = (x - x_hi.astype(f32)).astype(bf16)
    return _dot(x_hi, m, dims) + _dot(x_lo, m, dims)


@jax.custom_vjp
def _dot_const(x, m):
    return _two_pass(x, m)


def _dot_const_fwd(x, m):
    return _two_pass(x, m), m


def _dot_const_bwd(m, ct):
    return _two_pass(ct, m, "nt"), jnp.zeros_like(m)


_dot_const.defvjp(_dot_const_fwd, _dot_const_bwd)


def _two_pass_left(m, x, dims):
    x_hi = x.astype(bf16)
    x_lo = (x - x_hi.astype(f32)).astype(bf16)
    return _dot(m, x_hi, dims) + _dot(m, x_lo, dims)


@jax.custom_vjp
def _const_dot(m, x):
    return _two_pass_left(m, x, "nn")


def _const_dot_fwd(m, x):
    return _two_pass_left(m, x, "nn"), m


def _const_dot_bwd(m, ct):
    return jnp.zeros_like(m), _two_pass_left(m, ct, "tn")


_const_dot.defvjp(_const_dot_fwd, _const_dot_bwd)


def _mm(a, b, mode, *, name, out_dtype=f32, bias=None, add=None, zero_rows_below=0):
    m, _ = a.shape
    n = b.shape[1] if mode == "nn" else b.shape[0]
    tm = _tile(m)
    has_bias, has_add = bias is not None, add is not None

    def body(*refs):
        a_ref, b_ref = refs[0], refs[1]
        o_ref = refs[-1]
        acc = _dot(a_ref[...], b_ref[...], mode)
        k = 2
        if has_bias:
            acc = acc + refs[k][...]
            k += 1
        if zero_rows_below:
            rows = pl.program_id(0) * tm + lax.broadcasted_iota(jnp.int32, acc.shape, 0)
            acc = jnp.where(rows >= zero_rows_below, acc, 0.0)
        if has_add:
            acc = acc + refs[k][...].astype(f32)
        o_ref[...] = acc.astype(out_dtype)

    ins = [a, b]
    in_specs = [pl.BlockSpec((tm, a.shape[1]), lambda i: (i, 0)), _full(b.shape)]
    if has_bias:
        ins.append(bias)
        in_specs.append(_full(bias.shape))
    if has_add:
        ins.append(add)
        in_specs.append(pl.BlockSpec((tm, n), lambda i: (i, 0)))
    return pl.pallas_call(
        body, name=name, grid=(m // tm,), in_specs=in_specs,
        out_specs=pl.BlockSpec((tm, n), lambda i: (i, 0)),
        out_shape=jax.ShapeDtypeStruct((m, n), out_dtype),
        compiler_params=_params(("parallel",)),
    )(*ins)


def _mm_sum(a_list, b_list, *, name, out_dtype=bf16):
    m = a_list[0].shape[0]
    n = b_list[0].shape[1]
    k = len(a_list)
    tm = _tile(m)

    def body(*refs):
        acc = _dot(refs[0][...], refs[k][...])
        for i in range(1, k):
            acc = acc + _dot(refs[i][...], refs[k + i][...])
        refs[-1][...] = acc.astype(out_dtype)

    return pl.pallas_call(
        body, name=name, grid=(m // tm,),
        in_specs=[pl.BlockSpec((tm, a.shape[1]), lambda i: (i, 0)) for a in a_list] + [_full(b.shape) for b in b_list],
        out_specs=pl.BlockSpec((tm, n), lambda i: (i, 0)),
        out_shape=jax.ShapeDtypeStruct((m, n), out_dtype),
        compiler_params=_params(("parallel",)),
    )(*a_list, *b_list)


def _pieces(widths):
    out, off = [], 0
    for w in widths:
        out.append((off, w))
        off += w
    return out


def _proj_in(a, w_lr, bias, widths, *, name, zero_rows_below=0):
    m, kdim = a.shape
    half = kdim // 2
    tm = _tile(m)

    def body(a_ref, w_ref, b_ref, *outs):
        a_l, a_r = a_ref[:, :half], a_ref[:, half:]
        for (off, width), o_ref in zip(_pieces(widths), outs):
            acc = _dot(a_l, w_ref[0, off:off + width, :], "nt") + _dot(a_r, w_ref[1, off:off + width, :], "nt")
            acc = acc + b_ref[:, off:off + width]
            if zero_rows_below:
                rows = pl.program_id(0) * tm + lax.broadcasted_iota(jnp.int32, acc.shape, 0)
                acc = jnp.where(rows >= zero_rows_below, acc, 0.0)
            o_ref[...] = acc.astype(o_ref.dtype)

    return pl.pallas_call(
        body, name=name, grid=(m // tm,),
        in_specs=[pl.BlockSpec((tm, kdim), lambda i: (i, 0)), _full(w_lr.shape), _full(bias.shape)],
        out_specs=[pl.BlockSpec((tm, w), lambda i: (i, 0)) for w in widths],
        out_shape=[jax.ShapeDtypeStruct((m, w), bf16) for w in widths],
        compiler_params=_params(("parallel",)),
    )(a, w_lr, bias)


def _proj_in_bwd(d_list, w_lr, *, name):
    m = d_list[0].shape[0]
    half = w_lr.shape[2]
    widths = [d.shape[1] for d in d_list]
    tm = _tile(m)

    def body(*refs):
        w_ref, o_ref = refs[-2], refs[-1]
        for side in range(2):
            acc = None
            for (off, width), d_ref in zip(_pieces(widths), refs):
                term = _dot(d_ref[...], w_ref[side, off:off + width, :])
                acc = term if acc is None else acc + term
            o_ref[:, side * half:(side + 1) * half] = acc.astype(o_ref.dtype)

    return pl.pallas_call(
        body, name=name, grid=(m // tm,),
        in_specs=[pl.BlockSpec((tm, w), lambda i: (i, 0)) for w in widths] + [_full(w_lr.shape)],
        out_specs=pl.BlockSpec((tm, 2 * half), lambda i: (i, 0)),
        out_shape=jax.ShapeDtypeStruct((m, 2 * half), bf16),
        compiler_params=_params(("parallel",)),
    )(*d_list, w_lr)


def _mm_tn(a, b, *, name, colsum=False, out_dtype=bf16):
    r, m = a.shape
    n = b.shape[1]
    tr = _tile(r, 1408)
    tmo = m
    for cand in (1408, 1024, 768, 512):
        if m > 1024 and m % cand == 0:
            tmo = cand
            break
    steps = r // tr

    def body(a_ref, b_ref, o_ref, *rest):
        acc = rest[-1]
        i = pl.program_id(1)

        @pl.when(i == 0)
        def _():
            acc[...] = jnp.zeros_like(acc)
            if colsum:
                rest[0][...] = jnp.zeros_like(rest[0])

        acc[...] += _dot(a_ref[...], b_ref[...], "tn")
        if colsum:
            rest[0][...] += jnp.sum(a_ref[...].astype(f32), axis=0, keepdims=True)

        @pl.when(i == steps - 1)
        def _():
            o_ref[...] = acc[...].astype(out_dtype)

    out_shape = [jax.ShapeDtypeStruct((m, n), out_dtype)]
    out_specs = [pl.BlockSpec((tmo, n), lambda j, i: (j, 0))]
    if colsum:
        out_shape.append(jax.ShapeDtypeStruct((1, m), f32))
        out_specs.append(pl.BlockSpec((1, tmo), lambda j, i: (0, j)))
    res = pl.pallas_call(
        body, name=name, grid=(m // tmo, steps),
        in_specs=[pl.BlockSpec((tr, tmo), lambda j, i: (i, j)), pl.BlockSpec((tr, n), lambda j, i: (i, 0))],
        out_specs=out_specs, out_shape=out_shape,
        scratch_shapes=[pltpu.VMEM((tmo, n), f32)],
        compiler_params=_params(("parallel", "arbitrary")),
    )(a, b)
    return res if colsum else res[0]


def _rowwise(fn, rows, params, outs, *, name, tm=None):
    m = rows[0].shape[0]
    tm = tm or _tile(m)
    nr, npar = len(rows), len(params)

    def body(*refs):
        vals = [r[...] for r in refs[:nr + npar]]
        res = fn(*vals)
        for o_ref, v in zip(refs[nr + npar:], res):
            o_ref[...] = v.astype(o_ref.dtype)

    return pl.pallas_call(
        body, name=name, grid=(m // tm,),
        in_specs=[pl.BlockSpec((tm, r.shape[1]), lambda i: (i, 0)) for r in rows] + [_full(p.shape) for p in params],
        out_specs=[pl.BlockSpec((tm, w), lambda i: (i, 0)) for w, _ in outs],
        out_shape=[jax.ShapeDtypeStruct((m, w), dt) for w, dt in outs],
        compiler_params=_params(("parallel",)),
    )(*rows, *params)


def _rowwise_bwd(fn, rows, params, cts, *, name, diff_rows, diff_params, tm=None, zero_rows_below=0, out_dtypes=None):
    m = rows[0].shape[0]
    tm = tm or _tile(m)
    nr, npar = len(rows), len(params)
    d_idx = [i for i in range(nr) if diff_rows[i]]
    p_idx = [i for i in range(npar) if diff_params[i]]
    out_dtypes = out_dtypes or [f32] * len(d_idx)
    flat_cts = [c for group in cts for c in group]
    n_ct = len(flat_cts)

    def body(*refs):
        vals = [r[...] for r in refs[:nr + npar]]
        ct_refs = refs[nr + npar:nr + npar + n_ct]
        out_refs = refs[nr + npar + n_ct:]
        ct_vals, k = [], 0
        for group in cts:
            acc = ct_refs[k][...].astype(f32)
            for extra in range(1, len(group)):
                acc = acc + ct_refs[k + extra][...].astype(f32)
            k += len(group)
            if zero_rows_below:
                rr = pl.program_id(0) * tm + lax.broadcasted_iota(jnp.int32, acc.shape, 0)
                acc = jnp.where(rr >= zero_rows_below, acc, 0.0)
            ct_vals.append(acc)

        def g(*dargs):
            full = list(vals)
            for pos, i in enumerate(d_idx):
                full[i] = dargs[pos]
            for pos, i in enumerate(p_idx):
                full[nr + i] = dargs[len(d_idx) + pos]
            return tuple(fn(*full))

        _, vjp = jax.vjp(g, *[vals[i].astype(f32) for i in d_idx], *[vals[nr + i] for i in p_idx])
        grads = vjp(tuple(ct_vals))
        for pos in range(len(d_idx)):
            out_refs[pos][...] = grads[pos].astype(out_refs[pos].dtype)
        first = pl.program_id(0) == 0
        for pos in range(len(p_idx)):
            o_ref = out_refs[len(d_idx) + pos]

            @pl.when(first)
            def _(o_ref=o_ref):
                o_ref[...] = jnp.zeros_like(o_ref)

            o_ref[...] += grads[len(d_idx) + pos]

    return pl.pallas_call(
        body, name=name, grid=(m // tm,),
        in_specs=[pl.BlockSpec((tm, r.shape[1]), lambda i: (i, 0)) for r in rows] + [_full(p.shape) for p in params]
        + [pl.BlockSpec((tm, c.shape[1]), lambda i: (i, 0)) for c in flat_cts],
        out_specs=[pl.BlockSpec((tm, rows[i].shape[1]), lambda i_: (i_, 0)) for i in d_idx]
        + [_full(params[i].shape) for i in p_idx],
        out_shape=[jax.ShapeDtypeStruct(rows[i].shape, dt) for i, dt in zip(d_idx, out_dtypes)]
        + [jax.ShapeDtypeStruct(params[i].shape, f32) for i in p_idx],
        compiler_params=_params(("arbitrary",)),
    )(*rows, *params, *flat_cts)


def _rms(x, g):
    return x * lax.rsqrt(jnp.mean(x * x, axis=-1, keepdims=True) + RMS_EPS) * g


def _head_sum_matrix(width, head):
    idx = jnp.arange(width) // head
    return (idx[:, None] == idx[None, :]).astype(f32)


def _rope_tables(lp):
    half = ROPE_DIM // 2
    pos = (jnp.arange(lp) - PAD).astype(f32)
    inv_freq = jnp.power(jnp.float32(ROPE_THETA), -jnp.arange(half, dtype=f32) * (2.0 / ROPE_DIM))
    ang = pos[:, None] * inv_freq[None, :]
    cos, sin = jnp.cos(ang), jnp.sin(ang)
    ones = jnp.ones((lp, HEAD_DIM - ROPE_DIM), f32)
    zeros = jnp.zeros((lp, HEAD_DIM - ROPE_DIM), f32)
    cos_t = jnp.concatenate([cos, cos, ones], axis=1)
    sin_t = jnp.concatenate([-sin, sin, zeros], axis=1)
    i = jnp.arange(HEAD_DIM)
    src = jnp.where(i < half, i + half, jnp.where(i < ROPE_DIM, i - half, i))
    swap = ((i[:, None] == src[None, :]) & (i[None, :] < ROPE_DIM)).astype(f32)
    return cos_t, sin_t, swap


def _attn_prep(qkv, cos_t, sin_t, swap):
    outs = []
    for h in range(Q_HEADS + KV_HEADS):
        t = qkv[:, h * HEAD_DIM:(h + 1) * HEAD_DIM]
        outs.append(t * cos_t + _dot_const(t, swap) * sin_t)
    q = jnp.concatenate(outs[:Q_HEADS], axis=1)
    k = jnp.concatenate(outs[Q_HEADS:], axis=1)
    return q, k, qkv[:, Q_W + KV_W:]


def _softplus(z):
    return jnp.maximum(z, 0.0) + jnp.log1p(jnp.exp(-jnp.abs(z)))


def _rwkv_prep(rkv, lora, w0, w2, a0, a2, g2, k_k, k_a, hsum):
    r = rkv[:, :RWKV_DIM]
    k = rkv[:, RWKV_DIM:2 * RWKV_DIM]
    v = rkv[:, 2 * RWKV_DIM:]
    dw = lora[:, :DECAY_LORA]
    da = lora[:, DECAY_LORA:DECAY_LORA + AAA_LORA]
    dg = lora[:, DECAY_LORA + AAA_LORA:]
    w = -_softplus(-(w0 + _dot(jnp.tanh(dw), w2))) - 0.5
    a = jax.nn.sigmoid(a0 + _dot(da, a2))
    g = _dot(jax.nn.sigmoid(dg), g2)
    kk = k * k_k
    kk = kk * lax.rsqrt(jnp.maximum(_dot_const(kk * kk, hsum), 1e-24))
    k = k * (1.0 + (a - 1.0) * k_a)
    log_decay = -jnp.exp(w)
    return r, log_decay, k, v, -kk, kk * a, g


def _rwkv_post(y, r, k, v, g, ln_w, ln_b, r_k, hmean):
    hsum = hmean * RWKV_HEAD
    mean = _dot_const(y, hmean)
    yc = y - mean
    var = _dot_const(yc * yc, hmean)
    yn = yc * lax.rsqrt(var + RWKV_LN_EPS) * ln_w + ln_b
    bonus = _dot_const(r * k * r_k, hsum) * v
    return ((yn + bonus) * g,)


def _merge(gates, br_a, br_r):
    sg = jax.nn.sigmoid(gates)
    return (sg[:, :D_MODEL] * br_a + sg[:, D_MODEL:] * br_r,)


def _swiglu(gate, up):
    return (jax.nn.silu(gate) * up,)


def _ffn_in(f, w_gate_t, w_up_t, *, name):
    m, d = f.shape
    n = w_gate_t.shape[0]
    tm = _tile(m)

    def body(f_ref, wg_ref, wu_ref, g_ref, u_ref, a_ref):
        g = _dot(f_ref[...], wg_ref[...], "nt")
        u = _dot(f_ref[...], wu_ref[...], "nt")
        g_ref[...] = g.astype(g_ref.dtype)
        u_ref[...] = u.astype(u_ref.dtype)
        a_ref[...] = _swiglu(g, u)[0].astype(a_ref.dtype)

    spec = pl.BlockSpec((tm, n), lambda i: (i, 0))
    return pl.pallas_call(
        body, name=name, grid=(m // tm,),
        in_specs=[pl.BlockSpec((tm, d), lambda i: (i, 0)), _full(w_gate_t.shape), _full(w_up_t.shape)],
        out_specs=[spec] * 3, out_shape=[jax.ShapeDtypeStruct((m, n), bf16)] * 3,
        compiler_params=_params(("parallel",)),
    )(f, w_gate_t, w_up_t)


def _branch_merge(y_attn, y_rwkv, w_attn_t, w_rwkv_t, gates, *, name):
    m = y_attn.shape[0]
    tm = _tile(m)

    def body(ya_ref, yr_ref, wa_ref, wr_ref, g_ref, a_ref, r_ref, o_ref):
        br_a = _dot(ya_ref[...], wa_ref[...], "nt")
        br_r = _dot(yr_ref[...], wr_ref[...], "nt")
        a_ref[...] = br_a.astype(a_ref.dtype)
        r_ref[...] = br_r.astype(r_ref.dtype)
        o_ref[...] = _merge(g_ref[...].astype(f32), br_a, br_r)[0].astype(o_ref.dtype)

    rows = lambda a: pl.BlockSpec((tm, a.shape[1]), lambda i: (i, 0))
    spec = pl.BlockSpec((tm, D_MODEL), lambda i: (i, 0))
    return pl.pallas_call(
        body, name=name, grid=(m // tm,),
        in_specs=[rows(y_attn), rows(y_rwkv), _full(w_attn_t.shape), _full(w_rwkv_t.shape), rows(gates)],
        out_specs=[spec] * 3, out_shape=[jax.ShapeDtypeStruct((m, D_MODEL), bf16)] * 3,
        compiler_params=_params(("parallel",)),
    )(y_attn, y_rwkv, w_attn_t, w_rwkv_t, gates)


def _branch_merge_bwd(dh, w_o, gates, br_a, br_r, *, name):
    m = dh.shape[0]
    tm = _tile(m)

    def body(dh_ref, w_ref, g_ref, a_ref, r_ref, dg_ref, da_ref, dr_ref):
        dmerged = _dot(dh_ref[...], w_ref[...], "nt")
        _, vjp = jax.vjp(lambda g, a, r: _merge(g, a, r)[0], g_ref[...].astype(f32), a_ref[...].astype(f32),
                         r_ref[...].astype(f32))
        dg, da, dr = vjp(dmerged)
        dg_ref[...] = dg.astype(dg_ref.dtype)
        da_ref[...] = da.astype(da_ref.dtype)
        dr_ref[...] = dr.astype(dr_ref.dtype)

    rows = lambda a: pl.BlockSpec((tm, a.shape[1]), lambda i: (i, 0))
    return pl.pallas_call(
        body, name=name, grid=(m // tm,),
        in_specs=[rows(dh), _full(w_o.shape), rows(gates), rows(br_a), rows(br_r)],
        out_specs=[rows(gates), rows(br_a), rows(br_r)],
        out_shape=[jax.ShapeDtypeStruct(gates.shape, bf16), jax.ShapeDtypeStruct(br_a.shape, bf16),
                   jax.ShapeDtypeStruct(br_r.shape, bf16)],
        compiler_params=_params(("parallel",)),
    )(dh, w_o, gates, br_a, br_r)


def _ffn_in_bwd(dh, w_down, gate, up, *, name):
    m, d = dh.shape
    n = w_down.shape[0]
    tm = _tile(m)

    def body(dh_ref, w_ref, g_ref, u_ref, dg_ref, du_ref):
        dact = _dot(dh_ref[...], w_ref[...], "nt")
        _, vjp = jax.vjp(lambda a, b: _swiglu(a, b)[0], g_ref[...].astype(f32), u_ref[...].astype(f32))
        dg, du = vjp(dact)
        dg_ref[...] = dg.astype(dg_ref.dtype)
        du_ref[...] = du.astype(du_ref.dtype)

    spec = pl.BlockSpec((tm, n), lambda i: (i, 0))
    return pl.pallas_call(
        body, name=name, grid=(m // tm,),
        in_specs=[pl.BlockSpec((tm, d), lambda i: (i, 0)), _full(w_down.shape), spec, spec],
        out_specs=[spec] * 2, out_shape=[jax.ShapeDtypeStruct((m, n), bf16)] * 2,
        compiler_params=_params(("parallel",)),
    )(dh, w_down, gate, up)


HALO = 16


def _previous_rows(x, before_ref, first_tile):
    rows = lax.broadcasted_iota(jnp.int32, x.shape, 0)
    last = jnp.where(first_tile, 0.0, before_ref[HALO - 1:HALO, :].astype(f32))
    return jnp.where(rows == 0, last, pltpu.roll(x, 1, axis=0))


def _mixer_inputs(ps, mixes, params, *, name):
    m = ps[0].shape[0]
    tm = _tile(m)
    sub = tm // HALO
    n_par = len(params)

    def body(*refs):
        first = pl.program_id(0) == 0
        pf = []
        for k in range(2):
            x = refs[k][...].astype(f32)
            pf.append(x + (_previous_rows(x, refs[2 + k], first) - x) * refs[4 + k][...])
        res = _rwkv_prep(*pf, *[ref[...] for ref in refs[6:6 + n_par]])
        for o_ref, val in zip(refs[6 + n_par:], res):
            o_ref[...] = val

    tile = lambda a: pl.BlockSpec((tm, a.shape[1]), lambda i: (i, 0))
    before = lambda a: pl.BlockSpec((HALO, a.shape[1]), lambda i: (jnp.maximum(i * sub - 1, 0), 0))
    out = pl.BlockSpec((tm, RWKV_DIM), lambda i: (i, 0))
    return pl.pallas_call(
        body, name=name, grid=(m // tm,),
        in_specs=[tile(a) for a in ps] + [before(a) for a in ps] + [_full(a.shape) for a in mixes + params],
        out_specs=[out] * 7, out_shape=[jax.ShapeDtypeStruct((m, RWKV_DIM), f32)] * 7,
        compiler_params=_params(("parallel",)),
    )(*ps, *ps, *mixes, *params)


def _mixer_inputs_bwd(ps, mixes, params, cts, *, name):
    m = ps[0].shape[0]
    tm = _tile(m)
    sub = tm // HALO
    nt = m // tm
    n_par = len(params)
    flat_cts = [c for group in cts for c in group]
    n_ct = len(flat_cts)

    def body(*refs):
        i = pl.program_id(0)
        tile_index = nt - 1 - i
        ct_refs = refs[6 + n_par:6 + n_par + n_ct]
        dp_refs = refs[6 + n_par + n_ct:8 + n_par + n_ct]
        dmix_refs = refs[8 + n_par + n_ct:10 + n_par + n_ct]
        dpar_refs = refs[10 + n_par + n_ct:9 + 2 * n_par + n_ct]
        carries = refs[9 + 2 * n_par + n_ct:]
        rows1 = tile_index * tm + lax.broadcasted_iota(jnp.int32, (tm, 1), 0)
        live = rows1 >= PAD

        @pl.when(i == 0)
        def _():
            for ref in (*dmix_refs, *dpar_refs, *carries):
                ref[...] = jnp.zeros_like(ref)

        xs, prevs, pf = [], [], []
        for k in range(2):
            x = refs[k][...].astype(f32)
            xp = _previous_rows(x, refs[2 + k], tile_index == 0)
            xs.append(x)
            prevs.append(xp)
            pf.append(x + (xp - x) * refs[4 + k][...])
        ct_vals, pos = [], 0
        for group in cts:
            acc = ct_refs[pos][...].astype(f32)
            for extra in range(1, len(group)):
                acc = acc + ct_refs[pos + extra][...].astype(f32)
            pos += len(group)
            ct_vals.append(jnp.where(live, acc, 0.0))
        par_vals = [ref[...] for ref in refs[6:6 + n_par]]
        _, vjp = jax.vjp(lambda *args: _rwkv_prep(*args, par_vals[-1]), *pf, *par_vals[:-1])
        g = vjp(tuple(ct_vals))
        for k in range(2):
            dpf = g[k]
            mixv = refs[4 + k][...]
            dm = dpf * mixv
            rows = lax.broadcasted_iota(jnp.int32, dm.shape, 0)
            dm_next = jnp.where(rows == tm - 1, carries[k][...], pltpu.roll(dm, tm - 1, axis=0))
            dp_refs[k][...] = jnp.where(live, dpf - dm + dm_next, 0.0).astype(dp_refs[k].dtype)
            carries[k][...] = dm[0:1, :]
            dmix_refs[k][...] += jnp.sum(dpf * (prevs[k] - xs[k]), axis=0, keepdims=True)
        for ref, val in zip(dpar_refs, g[2:]):
            ref[...] += val

    tile = lambda a: pl.BlockSpec((tm, a.shape[1]), lambda i: (nt - 1 - i, 0))
    before = lambda a: pl.BlockSpec((HALO, a.shape[1]), lambda i: (jnp.maximum((nt - 1 - i) * sub - 1, 0), 0))
    return pl.pallas_call(
        body, name=name, grid=(nt,),
        in_specs=[tile(a) for a in ps] + [before(a) for a in ps] + [_full(a.shape) for a in mixes + params]
        + [tile(c) for c in flat_cts],
        out_specs=[tile(a) for a in ps] + [_full(a.shape) for a in mixes + params[:-1]],
        out_shape=[jax.ShapeDtypeStruct(a.shape, bf16) for a in ps]
        + [jax.ShapeDtypeStruct(a.shape, f32) for a in mixes + params[:-1]],
        scratch_shapes=[pltpu.VMEM((1, a.shape[1]), f32) for a in ps],
        compiler_params=_params(("arbitrary",)),
    )(*ps, *ps, *mixes, *params, *flat_cts)


def _attn_masks(blk):
    qi = lax.broadcasted_iota(jnp.int32, (BLOCK, BLOCK), 0)
    ki = lax.broadcasted_iota(jnp.int32, (BLOCK, BLOCK), 1)
    qpos = blk * BLOCK + qi - PAD
    kpos_c = blk * BLOCK + ki - PAD
    kpos_p = kpos_c - BLOCK
    kpos_m = ki - PAD

    def band(kpos):
        return (kpos >= N_META) & (kpos <= qpos) & (qpos - kpos < WINDOW)

    return band(kpos_p), band(kpos_c), (kpos_m >= 0) & (kpos_m <= qpos)


def _attn_probs(qs, k3s, sink, oks):
    s = [[jnp.where(ok, _dot(qh, kx, "nt"), NEG_INF) for kx, ok in zip(k3, oks)] for qh, k3 in zip(qs, k3s)]
    mx = [jnp.maximum(jnp.maximum(jnp.max(t[0], -1, keepdims=True), jnp.max(t[1], -1, keepdims=True)),
                      jnp.maximum(jnp.max(t[2], -1, keepdims=True), sk)) for t, sk in zip(s, sink)]
    e = [[jnp.exp(tx - m) for tx in t] for t, m in zip(s, mx)]
    e_sink = [jnp.exp(sk - m) for sk, m in zip(sink, mx)]
    inv = [1.0 / (jnp.sum(t[0], -1, keepdims=True) + jnp.sum(t[1], -1, keepdims=True)
                  + jnp.sum(t[2], -1, keepdims=True) + es) for t, es in zip(e, e_sink)]
    return [[tx * i for tx in t] for t, i in zip(e, inv)], [es * i for es, i in zip(e_sink, inv)]


def _head_cols(i):
    return slice(i * HEAD_DIM, (i + 1) * HEAD_DIM)


def _attn_operands(refs):
    q_ref, kp_ref, kc_ref, km_ref, vp_ref, vc_ref, vm_ref, s_ref = refs
    qs = [q_ref[:, _head_cols(i)] * (HEAD_DIM ** -0.5) for i in range(Q_HEADS)]
    k3 = [[ref[:, _head_cols(h)] for ref in (kp_ref, kc_ref, km_ref)] for h in range(KV_HEADS)]
    v3 = [[ref[:, _head_cols(h)] for ref in (vp_ref, vc_ref, vm_ref)] for h in range(KV_HEADS)]
    return (qs, [k3[i // GROUP] for i in range(Q_HEADS)], [v3[i // GROUP] for i in range(Q_HEADS)],
            [s_ref[:, i:i + 1] for i in range(Q_HEADS)])


def _attention(q, k, v, sinks, *, name):
    lp = q.shape[0]
    nb = lp // BLOCK
    prev = lambda i: (jnp.maximum(i - 1, 0), 0)
    cur = lambda i: (i, 0)
    meta = lambda i: (0, 0)
    kv = lambda index: pl.BlockSpec((BLOCK, KV_W), index)

    def body(*refs):
        o_ref = refs[-1]
        qs, k3s, v3s, sink = _attn_operands(refs[:-1])
        p, _ = _attn_probs(qs, k3s, sink, _attn_masks(pl.program_id(0)))
        out = [_dot(ph[0], v3[0]) + _dot(ph[1], v3[1]) + _dot(ph[2], v3[2]) for ph, v3 in zip(p, v3s)]
        for i in range(Q_HEADS):
            o_ref[:, _head_cols(i)] = out[i].astype(o_ref.dtype)

    return pl.pallas_call(
        body, name=name, grid=(nb,),
        in_specs=[pl.BlockSpec((BLOCK, Q_W), cur), kv(prev), kv(cur), kv(meta), kv(prev), kv(cur), kv(meta),
                  _full((1, Q_HEADS))],
        out_specs=pl.BlockSpec((BLOCK, Q_W), cur),
        out_shape=jax.ShapeDtypeStruct((lp, Q_W), bf16),
        compiler_params=_params(("parallel",)),
    )(q, k, k, k, v, v, v, sinks)


def _attention_bwd(q, k, v, sinks, out, do, *, name):
    lp = q.shape[0]
    nb = lp // BLOCK
    cur = lambda n: (jnp.minimum(n, nb - 1), 0)
    prev = lambda n: (jnp.maximum(jnp.minimum(n, nb - 1) - 1, 0), 0)
    behind = lambda n: (jnp.maximum(n - 1, 0), 0)
    meta = lambda n: (0, 0)
    kv = lambda index: pl.BlockSpec((BLOCK, KV_W), index)
    scale = HEAD_DIM ** -0.5

    def body(*refs):
        ins, fwd_ref, do_ref = refs[:8], refs[8], refs[9]
        dq_ref, dk_ref, dv_ref, dkm_ref, dvm_ref, ds_ref, carry_k, carry_v = refs[10:]
        n = pl.program_id(0)

        @pl.when(n == 0)
        def _():
            for ref in (dkm_ref, dvm_ref, ds_ref, carry_k, carry_v):
                ref[...] = jnp.zeros_like(ref)

        @pl.when(n < nb)
        def _():
            qs, k3s, v3s, sink = _attn_operands(ins)
            do = [do_ref[:, _head_cols(i)] for i in range(Q_HEADS)]
            p, p_sink = _attn_probs(qs, k3s, sink, _attn_masks(n))
            delta = [jnp.sum(d * fwd_ref[:, _head_cols(i)].astype(f32), -1, keepdims=True) for i, d in enumerate(do)]
            dp = [[_dot(d, vx, "nt") for vx in v3] for d, v3 in zip(do, v3s)]
            ds = [[px * (dx - dl) for px, dx in zip(ph, dh)] for ph, dh, dl in zip(p, dp, delta)]
            dq = [_dot(dsh[0], k3[0]) + _dot(dsh[1], k3[1]) + _dot(dsh[2], k3[2]) for dsh, k3 in zip(ds, k3s)]
            for i in range(Q_HEADS):
                dq_ref[:, _head_cols(i)] = dq[i] * scale
                ds_ref[:, i:i + 1] -= jnp.sum(p_sink[i] * delta[i], axis=0, keepdims=True)
            for h in range(KV_HEADS):
                group = slice(h * GROUP, (h + 1) * GROUP)
                q_all = jnp.concatenate(qs[group], axis=0)
                do_all = jnp.concatenate(do[group], axis=0)
                dk3 = [_dot(jnp.concatenate([dsh[x] for dsh in ds[group]], axis=0), q_all, "tn") for x in range(3)]
                dv3 = [_dot(jnp.concatenate([ph[x] for ph in p[group]], axis=0), do_all, "tn") for x in range(3)]
                hs = _head_cols(h)
                for out_ref, carry, meta_ref, d3 in ((dk_ref, carry_k, dkm_ref, dk3),
                                                     (dv_ref, carry_v, dvm_ref, dv3)):
                    out_ref[:, hs] = carry[:, hs] + d3[0]
                    carry[:, hs] = d3[1]
                    meta_ref[:, hs] += d3[2]

        @pl.when(n == nb)
        def _():
            dk_ref[...] = carry_k[...]
            dv_ref[...] = carry_v[...]

    kv_shape = jax.ShapeDtypeStruct((lp, KV_W), f32)
    one_shape = jax.ShapeDtypeStruct((BLOCK, KV_W), f32)
    return pl.pallas_call(
        body, name=name, grid=(nb + 1,),
        in_specs=[pl.BlockSpec((BLOCK, Q_W), cur), kv(prev), kv(cur), kv(meta), kv(prev), kv(cur), kv(meta),
                  _full((1, Q_HEADS)), pl.BlockSpec((BLOCK, Q_W), cur), pl.BlockSpec((BLOCK, Q_W), cur)],
        out_specs=[pl.BlockSpec((BLOCK, Q_W), cur), kv(behind), kv(behind), kv(meta), kv(meta),
                   _full((1, Q_HEADS))],
        out_shape=[jax.ShapeDtypeStruct((lp, Q_W), f32), kv_shape, kv_shape, one_shape, one_shape,
                   jax.ShapeDtypeStruct((1, Q_HEADS), f32)],
        scratch_shapes=[pltpu.VMEM((BLOCK, KV_W), f32), pltpu.VMEM((BLOCK, KV_W), f32)],
        compiler_params=_params(("arbitrary",)),
    )(q, k, k, k, v, v, v, sinks, out, do)


@jax.custom_vjp
def _known_inverse(l, x):
    return x


def _known_inverse_fwd(l, x):
    return x, x


def _known_inverse_bwd(x, ct):
    return _dot(_dot(x, ct, "tn"), x, "nt"), jnp.zeros_like(x)


_known_inverse.defvjp(_known_inverse_fwd, _known_inverse_bwd)


@jax.custom_vjp
def _decayed(x, c):
    return (x * jnp.exp(c)).astype(bf16).astype(f32)


def _decayed_fwd(x, c):
    e = jnp.exp(c)
    out = (x * e).astype(bf16).astype(f32)
    return out, (e, out)


def _decayed_bwd(res, ct):
    e, out = res
    return ct * e, ct * out


_decayed.defvjp(_decayed_fwd, _decayed_bwd)


@jax.custom_vjp
def _pair(x, y):
    return _dot(x, y, "nt")


def _pair_fwd(x, y):
    return _dot(x, y, "nt"), (x, y)


def _pair_bwd(res, ct):
    x, y = res
    hi = ct.astype(bf16)
    lo = (ct - hi.astype(f32)).astype(bf16)
    return _dot(hi, y) + _dot(lo, y), _dot(hi, x, "tn") + _dot(lo, x, "tn")


_pair.defvjp(_pair_fwd, _pair_bwd)


def _scan_chunk(s0, r, lw, k, v, a, b, inv=None):
    t = r[0].shape[0]
    ii = lax.broadcasted_iota(jnp.int32, (t, t), 0)
    jj = lax.broadcasted_iota(jnp.int32, (t, t), 1)
    incl = jj <= ii
    strict = jj < ii
    tri = incl.astype(f32)
    eye = jnp.where(ii == jj, 1.0, 0.0)
    cl = [_const_dot(tri, x) for x in lw]
    mid = [c[t // 2 - 1:t // 2, :] for c in cl]
    s0 = [s * jnp.exp(m) for s, m in zip(s0, mid)]
    cl = [c - m for c, m in zip(cl, mid)]
    rt = [_decayed(x, c) for x, c in zip(r, cl)]
    at = [_decayed(x, c - l) for x, c, l in zip(a, cl, lw)]
    bt = [_decayed(x, -c) for x, c in zip(b, cl)]
    kt = [_decayed(x, -c) for x, c in zip(k, cl)]
    l_ab = [jnp.where(strict, _pair(x, y), 0.0) for x, y in zip(at, bt)]
    l_ak = [jnp.where(strict, _pair(x, y), 0.0) for x, y in zip(at, kt)]
    r_b = [jnp.where(incl, _pair(x, y), 0.0) for x, y in zip(rt, bt)]
    r_k = [jnp.where(incl, _pair(x, y), 0.0) for x, y in zip(rt, kt)]
    if inv is None:
        inv = [eye + x for x in l_ab]
        pw = l_ab
        for _ in range(int(math.log2(t)) - 1):
            pw = [_dot(x, x) for x in pw]
            inv = [x + _dot(x, y) for x, y in zip(inv, pw)]
    else:
        inv = [_known_inverse(x, y) for x, y in zip(l_ab, inv)]
    rhs = [_dot(x, s, "nt") + _dot(m, y) for x, s, m, y in zip(at, s0, l_ak, v)]
    u = [_dot(x, y) for x, y in zip(inv, rhs)]
    y_s = [_dot(x, s, "nt") for x, s in zip(rt, s0)]
    y = [ys + _dot(m, uu) + _dot(n, vv) for ys, m, uu, n, vv in zip(y_s, r_b, u, r_k, v)]
    grow = [s + _dot(uu, x, "tn") + _dot(vv, z, "tn") for s, uu, x, vv, z in zip(s0, u, bt, v, kt)]
    s1 = [g * jnp.exp(c[t - 1:t, :]) for g, c in zip(grow, cl)]
    return y, s1, inv


def _head_rows(h):
    return slice(h * RWKV_HEAD, (h + 1) * RWKV_HEAD)


def _per_head(ref):
    return [ref[:, _head_rows(h)] for h in range(RWKV_HEADS)]


def _scan(r, lw, k, v, a, b, *, name):
    lp = r.shape[0]
    nc = lp // CHUNK
    row = pl.BlockSpec((CHUNK, RWKV_DIM), lambda c: (c, 0))

    def body(r_ref, lw_ref, k_ref, v_ref, a_ref, b_ref, y_ref, s_ref, inv_ref, state):
        @pl.when(pl.program_id(0) == 0)
        def _():
            state[...] = jnp.zeros_like(state)

        s_ref[...] = state[...]
        s0 = [state[_head_rows(h), :] for h in range(RWKV_HEADS)]
        y, s1, inv = _scan_chunk(s0, *[_per_head(ref) for ref in (r_ref, lw_ref, k_ref, v_ref, a_ref, b_ref)])
        for h in range(RWKV_HEADS):
            y_ref[:, _head_rows(h)] = y[h]
            state[_head_rows(h), :] = s1[h]
            inv_ref[h * CHUNK:(h + 1) * CHUNK, :] = inv[h].astype(inv_ref.dtype)

    return pl.pallas_call(
        body, name=name, grid=(nc,), in_specs=[row] * 6,
        out_specs=[row, pl.BlockSpec((RWKV_DIM, RWKV_HEAD), lambda c: (c, 0)),
                   pl.BlockSpec((RWKV_HEADS * CHUNK, CHUNK), lambda c: (c, 0))],
        out_shape=[jax.ShapeDtypeStruct((lp, RWKV_DIM), f32), jax.ShapeDtypeStruct((nc * RWKV_DIM, RWKV_HEAD), f32),
                   jax.ShapeDtypeStruct((nc * RWKV_HEADS * CHUNK, CHUNK), bf16)],
        scratch_shapes=[pltpu.VMEM((RWKV_DIM, RWKV_HEAD), f32)],
        compiler_params=_params(("arbitrary",)),
    )(r, lw, k, v, a, b)


def _scan_bwd(r, lw, k, v, a, b, states, inverses, dy, *, name):
    lp = r.shape[0]
    nc = lp // CHUNK
    back = lambda c: (nc - 1 - c, 0)
    row = pl.BlockSpec((CHUNK, RWKV_DIM), back)

    def body(r_ref, lw_ref, k_ref, v_ref, a_ref, b_ref, s_ref, inv_ref, dy_ref,
             dr_ref, dlw_ref, dk_ref, dv_ref, da_ref, db_ref, dstate):
        @pl.when(pl.program_id(0) == 0)
        def _():
            dstate[...] = jnp.zeros_like(dstate)

        outs = (dr_ref, dlw_ref, dk_ref, dv_ref, da_ref, db_ref)
        s0 = [s_ref[_head_rows(h), :] for h in range(RWKV_HEADS)]
        inv = [inv_ref[h * CHUNK:(h + 1) * CHUNK, :].astype(f32) for h in range(RWKV_HEADS)]
        _, vjp = jax.vjp(lambda *args: _scan_chunk(*args, inv=inv)[:2], s0,
                         *[_per_head(ref) for ref in (r_ref, lw_ref, k_ref, v_ref, a_ref, b_ref)])
        g = vjp((_per_head(dy_ref), [dstate[_head_rows(h), :] for h in range(RWKV_HEADS)]))
        for h in range(RWKV_HEADS):
            dstate[_head_rows(h), :] = g[0][h]
            for o_ref, gv in zip(outs, g[1:]):
                o_ref[:, _head_rows(h)] = gv[h]

    shape = jax.ShapeDtypeStruct((lp, RWKV_DIM), f32)
    return pl.pallas_call(
        body, name=name, grid=(nc,),
        in_specs=[row] * 6 + [pl.BlockSpec((RWKV_DIM, RWKV_HEAD), back),
                              pl.BlockSpec((RWKV_HEADS * CHUNK, CHUNK), back), row],
        out_specs=[row] * 6, out_shape=[shape] * 6,
        scratch_shapes=[pltpu.VMEM((RWKV_DIM, RWKV_HEAD), f32)],
        compiler_params=_params(("arbitrary",)),
    )(r, lw, k, v, a, b, states, inverses, dy)


def _loss_head(h2, target, g_final, *, name):
    lp = h2.shape[0]
    tm = BLOCK
    front_tiles = FRONT // tm

    def body(h_ref, t_ref, g_ref, loss_ref, dh_ref, dg_ref):
        i = pl.program_id(0)
        real = i >= front_tiles

        def tile_loss(hv, gv):
            err = _rms(hv, gv) - t_ref[...]
            return jnp.where(real, 0.5 * jnp.sum(jnp.mean(err * err, axis=-1, keepdims=True)), 0.0)

        loss, (dh, dg) = jax.value_and_grad(tile_loss, argnums=(0, 1))(h_ref[...], g_ref[...])

        @pl.when(i == 0)
        def _():
            loss_ref[...] = jnp.zeros_like(loss_ref)
            dg_ref[...] = jnp.zeros_like(dg_ref)

        loss_ref[...] += jnp.full(loss_ref.shape, loss, f32)
        dg_ref[...] += dg
        dh_ref[...] = dh

    return pl.pallas_call(
        body, name=name, grid=(lp // tm,),
        in_specs=[pl.BlockSpec((tm, D_MODEL), lambda i: (i, 0)),
                  pl.BlockSpec((tm, D_MODEL), lambda i: (jnp.maximum(i - front_tiles, 0), 0)),
                  _full(g_final.shape)],
        out_specs=[_full((8, 128)), pl.BlockSpec((tm, D_MODEL), lambda i: (i, 0)), _full(g_final.shape)],
        out_shape=[jax.ShapeDtypeStruct((8, 128), f32), jax.ShapeDtypeStruct((lp, D_MODEL), f32),
                   jax.ShapeDtypeStruct(g_final.shape, f32)],
        compiler_params=_params(("arbitrary",)),
    )(h2, target, g_final)


def _input_norm_bwd(h0, g, du, dh1, *, name):
    lp = h0.shape[0]
    tm = FRONT

    def body(h_ref, g_ref, du_ref, dh1_ref, dx_ref, front_ref, dg_ref):
        i = pl.program_id(0)
        _, vjp = jax.vjp(lambda hv, gv: (_rms(hv, gv), hv), h_ref[...], g_ref[...])
        dh, dg = vjp((du_ref[...].astype(f32), dh1_ref[...]))

        @pl.when(i == 0)
        def _():
            dg_ref[...] = jnp.zeros_like(dg_ref)
            front_ref[...] = dh

        dg_ref[...] += dg
        dx_ref[...] = dh

    tile = pl.BlockSpec((tm, D_MODEL), lambda i: (i, 0))
    return pl.pallas_call(
        body, name=name, grid=(lp // tm,),
        in_specs=[tile, _full(g.shape), tile, tile],
        out_specs=[pl.BlockSpec((tm, D_MODEL), lambda i: (jnp.maximum(i - 1, 0), 0)), _full((tm, D_MODEL)),
                   _full(g.shape)],
        out_shape=[jax.ShapeDtypeStruct((lp - tm, D_MODEL), f32), jax.ShapeDtypeStruct((tm, D_MODEL), f32),
                   jax.ShapeDtypeStruct(g.shape, f32)],
        compiler_params=_params(("arbitrary",)),
    )(h0, g, du, dh1)


def _local_step(x, target, meta, p, early_weights=None, late_weights=None, emit=None):
    emit = emit or (lambda group, grads: 0.0)
    seq = x.shape[0]
    lp = seq + FRONT
    h0 = jnp.concatenate([jnp.zeros((PAD, D_MODEL), f32), meta, x], axis=0)
    cos_t, sin_t, swap = _rope_tables(lp)
    hsum = _head_sum_matrix(RWKV_DIM, RWKV_HEAD)
    hmean = hsum / RWKV_HEAD
    post_params = [p["ln_w"], p["ln_b"], p["r_k"], hmean]

    (u,) = _rowwise(lambda hv, g: (_rms(hv, g),), [h0], [p["norm_mix_g"]], [(D_MODEL, bf16)], name="norm_mix")
    if early_weights is not None:
        p = {**p, **early_weights(u)}
    prep_params = [p["w0"], p["w2"], p["a0"], p["a2"], p["g2"], p["k_k"], p["k_a"], hsum]
    qkv, p_rkv, p_lora, gates = _proj_in(u, p["w_in_lr"], p["b_in"], [ATTN_PROJ, RKV_W, LORA_W, 2 * D_MODEL],
                                         name="proj_in", zero_rows_below=PAD)

    q, k, v = _rowwise(_attn_prep, [qkv, cos_t, sin_t], [swap], [(Q_W, bf16), (KV_W, bf16), (KV_W, bf16)],
                       name="attn_prep")
    y_attn = _attention(q, k, v, p["sinks"], name="attention")

    mix_rkv, mix_lora = p["mix"][:, :RKV_W], p["mix"][:, RKV_W:]
    r_, lw_, k_, v_, a_, b_, g_ = _mixer_inputs([p_rkv, p_lora], [mix_rkv, mix_lora], prep_params,
                                                name="mixer_inputs")
    y_scan, states, inverses = _scan(r_, lw_, k_, v_, a_, b_, name="wkv_scan")
    (y_rwkv,) = _rowwise(_rwkv_post, [y_scan, r_, k_, v_, g_], post_params, [(RWKV_DIM, bf16)], name="rwkv_post")

    if late_weights is not None:
        p = {**p, **late_weights(y_rwkv)}
    br_a, br_r, merged = _branch_merge(y_attn, y_rwkv, p["w_br_attn_t"], p["w_br_rwkv_t"], gates, name="branch_merge")
    h1 = _mm(merged, p["w_o"], "nn", name="out_proj", add=h0)
    (f,) = _rowwise(lambda hv, g: (_rms(hv, g),), [h1], [p["norm_ffn_g"]], [(D_MODEL, bf16)], name="norm_ffn")
    gate, up, act = _ffn_in(f, p["w_gate_t"], p["w_up_t"], name="ffn_in")
    h2 = _mm(act, p["w_down"], "nn", name="ffn_down", add=h1)

    loss8, dh2, d_final_g = _loss_head(h2, target, p["norm_final_g"], name="loss_head")
    dgate, dup = _ffn_in_bwd(dh2, p["w_down"], gate, up, name="ffn_in_bwd")
    d_w_down = _mm_tn(act, dh2, name="dw_down")
    d_w_gate_t = _mm_tn(dgate, f, name="dw_gate")
    d_w_up_t = _mm_tn(dup, f, name="dw_up")
    zero = emit("ffn", dict(w_down=d_w_down, w_gate_t=d_w_gate_t, w_up_t=d_w_up_t))
    df = _mm_sum([dgate, dup], [p["w_gate_t"], p["w_up_t"]], name="d_f")
    dh1, d_ffn_g = _rowwise_bwd(lambda hv, g: (_rms(hv, g), hv), [h1], [p["norm_ffn_g"] + zero], [[df], [dh2]],
                                name="norm_ffn_bwd", diff_rows=[True], diff_params=[True])
    dgates, dbr_a, dbr_r = _branch_merge_bwd(dh1, p["w_o"], gates, br_a, br_r, name="branch_merge_bwd")
    d_w_o = _mm_tn(merged, dh1, name="dw_o")
    d_w_br_attn_t = _mm_tn(dbr_a, y_attn, name="dw_br_attn")
    d_w_br_rwkv_t = _mm_tn(dbr_r, y_rwkv, name="dw_br_rwkv")
    zero = emit("branch", dict(w_o=d_w_o, w_br_attn_t=d_w_br_attn_t, w_br_rwkv_t=d_w_br_rwkv_t))
    dy_attn = _mm(dbr_a, p["w_br_attn_t"], "nn", name="d_y_attn")
    dy_rwkv = _mm(dbr_r, p["w_br_rwkv_t"], "nn", name="d_y_rwkv")

    post_params = [p["ln_w"] + zero, p["ln_b"], p["r_k"], hmean]
    res = _rowwise_bwd(_rwkv_post, [y_scan, r_, k_, v_, g_], post_params, [[dy_rwkv]], name="rwkv_post_bwd",
                       diff_rows=[True] * 5, diff_params=[True, True, True, False])
    dy_scan, dr_p, dk_p, dv_p, dg_p, d_ln_w, d_ln_b, d_r_k = res
    dr_s, dlw_s, dk_s, dv_s, da_s, db_s = _scan_bwd(r_, lw_, k_, v_, a_, b_, states, inverses, dy_scan,
                                                    name="wkv_scan_bwd")
    res = _mixer_inputs_bwd([p_rkv, p_lora], [mix_rkv, mix_lora], prep_params,
                            [[dr_s, dr_p], [dlw_s], [dk_s, dk_p], [dv_s, dv_p], [da_s], [db_s], [dg_p]],
                            name="mixer_inputs_bwd")
    dp_rkv, dp_lora, d_mix_rkv, d_mix_lora, d_w0, d_w2, d_a0, d_a2, d_g2, d_k_k, d_k_a = res

    dq, dk, dv, dkm, dvm, d_sinks = _attention_bwd(q, k, v, p["sinks"], y_attn, dy_attn, name="attention_bwd")
    rest = jnp.zeros((lp - BLOCK, KV_W), f32)
    dkm, dvm = jnp.concatenate([dkm, rest], axis=0), jnp.concatenate([dvm, rest], axis=0)
    (dqkv,) = _rowwise_bwd(_attn_prep, [qkv, cos_t, sin_t], [swap], [[dq], [dk, dkm], [dv, dvm]], name="attn_prep_bwd",
                           diff_rows=[True, False, False], diff_params=[False], out_dtypes=[bf16])

    d_w_qkv_t, db_qkv = _mm_tn(dqkv, u, name="dw_qkv", colsum=True)
    d_w_rkv_t, db_rkv = _mm_tn(dp_rkv, u, name="dw_rkv", colsum=True)
    d_w_lora_t, db_lora = _mm_tn(dp_lora, u, name="dw_lora", colsum=True)
    d_w_gates_t, db_gates = _mm_tn(dgates, u, name="dw_gates", colsum=True)
    d_w_in_t = jnp.concatenate([d_w_qkv_t, d_w_rkv_t, d_w_lora_t, d_w_gates_t], axis=0)
    zero = emit("input", dict(w_in_t=d_w_in_t, g2=d_g2, w2=d_w2, a2=d_a2))
    du = _proj_in_bwd([dqkv, dp_rkv, dp_lora, dgates], p["w_in_lr"], name="d_u")
    dx, d_front, d_mix_g = _input_norm_bwd(h0, p["norm_mix_g"] + zero, du, dh1, name="norm_mix_bwd")

    grads = dict(
        w_in_t=d_w_in_t,
        b_in=jnp.concatenate([db_qkv, db_rkv, db_lora, db_gates], axis=1),
        mix=jnp.concatenate([d_mix_rkv, d_mix_lora], axis=1),
        norm_mix_g=d_mix_g, sinks=d_sinks, w0=d_w0, w2=d_w2, a0=d_a0, a2=d_a2, g2=d_g2, k_k=d_k_k, k_a=d_k_a,
        r_k=d_r_k, ln_w=d_ln_w, ln_b=d_ln_b, w_br_attn_t=d_w_br_attn_t, w_br_rwkv_t=d_w_br_rwkv_t, w_o=d_w_o,
        norm_ffn_g=d_ffn_g, w_gate_t=d_w_gate_t, w_up_t=d_w_up_t, w_down=d_w_down, norm_final_g=d_final_g,
        meta=d_front[PAD:],
    )
    return loss8[0, 0], dx, grads


def _position():
    return lax.axis_index("x"), lax.axis_index("y"), lax.axis_index("c")


def _other_chips(x, y):
    return [(1 - x, y), (x, 1 - y), (1 - x, 1 - y)]


_HBM = pl.BlockSpec(memory_space=pltpu.HBM)
_SEM = pl.BlockSpec(memory_space=pltpu.SEMAPHORE)
_EFFECT = pltpu.SideEffectType.DATAFLOW_SIDE_EFFECTING


def _landing_zone(src, kind):
    shape = {"whole": (N_CHIPS,) + src.shape, "half": (2, N_CHIPS, src.shape[0], src.shape[1] // 2),
             "slab": (3,) + src.shape[1:], "sibling": src.shape}[kind]
    return lax.empty(shape, src.dtype)


def _copies_per_source(kind):
    return 1 if kind == "sibling" else 3


def _chip_copies(src_refs, land_refs, send_sems, recv_sems, kind):
    x, y, c = _position()
    if kind == "sibling":
        return [pltpu.make_async_remote_copy(
            src_ref=src, dst_ref=land, send_sem=send_sems.at[a], recv_sem=recv_sems.at[a],
            device_id=(x, y, 1 - c), device_id_type=MESH) for a, (src, land) in enumerate(zip(src_refs, land_refs))]
    copies = []
    for a, (src, land) in enumerate(zip(src_refs, land_refs)):
        for j, (px, py) in enumerate(_other_chips(x, y)):
            if kind == "whole":
                src_ref, dst_ref = src, land.at[2 * x + y]
            elif kind == "half":
                half = src.shape[1] // 2
                src_ref, dst_ref = src.at[:, pl.ds(pl.multiple_of(c * half, half), half)], land.at[c, 2 * x + y]
            else:
                src_ref, dst_ref = src.at[2 * px + py], land.at[j]
            copies.append(pltpu.make_async_remote_copy(
                src_ref=src_ref, dst_ref=dst_ref, send_sem=send_sems.at[3 * a + j], recv_sem=recv_sems.at[3 * a + j],
                device_id=(px, py, c), device_id_type=MESH))
    return copies


def _exchange_start(srcs, *, kind, name):
    n = len(srcs)
    lands = [_landing_zone(s, kind) for s in srcs]

    def body(*refs):
        for cp in _chip_copies(refs[:n], refs[n:2 * n], refs[2 * n], refs[2 * n + 1], kind):
            cp.start()
        refs[-1][...] = jnp.zeros_like(refs[-1])

    res = pl.pallas_call(
        body, name=name,
        out_shape=(pltpu.SemaphoreType.DMA((_copies_per_source(kind) * n,)),
                   pltpu.SemaphoreType.DMA((_copies_per_source(kind) * n,)),
                   *[pltpu.HBM(a.shape, a.dtype) for a in srcs + lands], jax.ShapeDtypeStruct((8, 128), f32)),
        in_specs=[_HBM] * (2 * n),
        out_specs=(_SEM, _SEM, *[_HBM] * (2 * n), pl.BlockSpec(memory_space=pltpu.VMEM)),
        input_output_aliases={i: 2 + i for i in range(2 * n)},
        compiler_params=pltpu.CompilerParams(has_side_effects=_EFFECT),
    )(*[pltpu.with_memory_space_constraint(a, pltpu.HBM) for a in srcs + lands])
    return res[0], res[1], list(res[2:2 + n]), list(res[2 + n:2 + 2 * n]), res[-1]


def _exchange_wait(handle, after, *, kind, name):
    send_sems, recv_sems, srcs, lands, _ = handle
    n = len(srcs)

    def body(*refs):
        for cp in _chip_copies(refs[:n], refs[n:2 * n], refs[2 * n], refs[2 * n + 1], kind):
            cp.wait_send()
            cp.wait_recv()

    res = pl.pallas_call(
        body, name=name,
        out_shape=tuple(pltpu.HBM(a.shape, a.dtype) for a in srcs + lands),
        in_specs=[_HBM] * (2 * n) + [_SEM, _SEM, pl.BlockSpec(memory_space=pl.ANY)],
        out_specs=tuple([_HBM] * (2 * n)),
        input_output_aliases={i: i for i in range(2 * n)},
        compiler_params=pltpu.CompilerParams(has_side_effects=_EFFECT),
    )(*srcs, *lands, send_sems, recv_sems, after)
    return list(res[:n]), list(res[n:])


def _sum_own_and_received(g, recv, *, name):
    _, r, w = g.shape
    tm = _tile(r)
    if g.dtype == bf16 and tm % 16:
        tm = r
    x, y, _ = _position()
    me = jnp.reshape(2 * x + y, (1,)).astype(jnp.int32)

    def body(me_ref, g_ref, r_ref, o_ref):
        o_ref[...] = (g_ref[0].astype(f32) + r_ref[0].astype(f32)) + (r_ref[1].astype(f32) + r_ref[2].astype(f32))

    return pl.pallas_call(
        body, name=name,
        grid_spec=pltpu.PrefetchScalarGridSpec(
            num_scalar_prefetch=1, grid=(r // tm,),
            in_specs=[pl.BlockSpec((1, tm, w), lambda i, me_ref: (me_ref[0], i, 0)),
                      pl.BlockSpec((3, tm, w), lambda i, me_ref: (0, i, 0))],
            out_specs=pl.BlockSpec((tm, w), lambda i, me_ref: (i, 0))),
        out_shape=jax.ShapeDtypeStruct((r, w), f32),
        compiler_params=_params(("parallel",)),
    )(me, g, recv)


def _swap_cores(arrs, *, name):
    n = len(arrs)

    def body(*refs):
        x, y, c = _position()
        copies = [pltpu.make_async_remote_copy(
            src_ref=refs[i], dst_ref=refs[n + i], send_sem=refs[2 * n].at[i], recv_sem=refs[2 * n + 1].at[i],
            device_id=(x, y, 1 - c), device_id_type=MESH) for i in range(n)]
        for cp in copies:
            cp.start()
        for cp in copies:
            cp.wait_recv()
        for cp in copies:
            cp.wait_send()

    return pl.pallas_call(
        body, name=name,
        in_specs=[pl.BlockSpec(memory_space=pl.ANY)] * n,
        out_specs=[pl.BlockSpec(memory_space=pl.ANY)] * n,
        out_shape=[jax.ShapeDtypeStruct(a.shape, a.dtype) for a in arrs],
        scratch_shapes=[pltpu.SemaphoreType.DMA((n,)), pltpu.SemaphoreType.DMA((n,))],
    )(*arrs)


def _swap_halves(zone, *, name):
    def body(z_ref, o_ref, send_sems, recv_sems):
        x, y, c = _position()
        mine = [pltpu.make_async_remote_copy(
            src_ref=o_ref.at[c, 2 * px + py], dst_ref=o_ref.at[c, 2 * px + py], send_sem=send_sems.at[j],
            recv_sem=recv_sems.at[j], device_id=(x, y, 1 - c), device_id_type=MESH)
            for j, (px, py) in enumerate(_other_chips(x, y))]
        for cp in mine:
            cp.start()
        for j, (px, py) in enumerate(_other_chips(x, y)):
            pltpu.make_async_remote_copy(
                src_ref=o_ref.at[c, 2 * px + py], dst_ref=o_ref.at[1 - c, 2 * px + py], send_sem=send_sems.at[j],
                recv_sem=recv_sems.at[j], device_id=(x, y, 1 - c), device_id_type=MESH).wait_recv()
        for cp in mine:
            cp.wait_send()

    return pl.pallas_call(
        body, name=name,
        in_specs=[pl.BlockSpec(memory_space=pl.ANY)], out_specs=pl.BlockSpec(memory_space=pl.ANY),
        out_shape=jax.ShapeDtypeStruct(zone.shape, zone.dtype), input_output_aliases={0: 0},
        scratch_shapes=[pltpu.SemaphoreType.DMA((3,)), pltpu.SemaphoreType.DMA((3,))],
    )(zone)


def _all_reduce_small(a, after, *, name):
    rows, w = a.shape

    def body(a_ref, after_ref, o_ref, buf, send_sems, recv_sems):
        x, y, c = _position()
        me = 4 * x + 2 * y + c
        buf[0] = a_ref[...]
        sends = []
        for rel in range(1, N_DEV):
            peer = ((1 - x) if rel & 4 else x, (1 - y) if rel & 2 else y, (1 - c) if rel & 1 else c)
            cp = pltpu.make_async_remote_copy(
                src_ref=a_ref, dst_ref=buf.at[rel], send_sem=send_sems.at[rel - 1], recv_sem=recv_sems.at[rel - 1],
                device_id=peer, device_id_type=MESH)
            cp.start()
            sends.append(cp)
        for cp in sends:
            cp.wait_recv()
        for cp in sends:
            cp.wait_send()
        acc = buf[jnp.bitwise_xor(me, 0)]
        for d in range(1, N_DEV):
            acc = acc + buf[jnp.bitwise_xor(me, d)]
        o_ref[...] = acc

    return pl.pallas_call(
        body, name=name,
        in_specs=[pl.BlockSpec(memory_space=pltpu.VMEM), pl.BlockSpec(memory_space=pl.ANY)],
        out_specs=pl.BlockSpec(memory_space=pltpu.VMEM),
        out_shape=jax.ShapeDtypeStruct((rows, w), f32),
        scratch_shapes=[pltpu.VMEM((N_DEV, rows, w), f32), pltpu.SemaphoreType.DMA((N_DEV - 1,)),
                        pltpu.SemaphoreType.DMA((N_DEV - 1,))],
    )(a, after)


def _adamw(w, g_parts, m, v, *, name, transposed=False):
    rows, cols = w.shape
    if transposed:
        tm = 256 if rows % 256 == 0 else rows
        g_spec = pl.BlockSpec((cols, tm), lambda i: (0, i))
    else:
        tm = _tile(rows, 256)
        g_spec = pl.BlockSpec((tm, cols), lambda i: (i, 0))
    n = len(g_parts)

    def body(*refs):
        w_ref, m_ref, v_ref = refs[0], refs[1 + n], refs[2 + n]
        g_ref, d_ref, nm_ref, nv_ref = refs[3 + n:]
        gv = refs[1][...]
        for part in refs[2:1 + n]:
            gv = gv + part[...]
        if transposed:
            gv = gv.T
        g_ref[...] = gv
        nm = ADAM_B1 * m_ref[...] + (1.0 - ADAM_B1) * gv
        nv = ADAM_B2 * v_ref[...] + (1.0 - ADAM_B2) * (gv * gv)
        m_hat = nm / (1.0 - ADAM_B1 ** ADAM_STEP)
        v_hat = nv / (1.0 - ADAM_B2 ** ADAM_STEP)
        d_ref[...] = -ADAM_LR * (m_hat / (jnp.sqrt(v_hat) + ADAM_EPS) + ADAM_WD * w_ref[...])
        nm_ref[...] = nm
        nv_ref[...] = nv

    spec = pl.BlockSpec((tm, cols), lambda i: (i, 0))
    shape = jax.ShapeDtypeStruct((rows, cols), f32)
    return pl.pallas_call(
        body, name=name, grid=(rows // tm,), in_specs=[spec] + [g_spec] * n + [spec] * 2,
        out_specs=[spec] * 4, out_shape=[shape] * 4,
        compiler_params=_params(("parallel",)),
    )(w, *g_parts, m, v)


def _pad_rows(a, rows):
    return jnp.concatenate([a, jnp.zeros((rows - a.shape[0], a.shape[1]), a.dtype)], axis=0) if rows > a.shape[0] else a


_SMALL = (("norm_mix_g", D_MODEL), ("b_in", D_IN), ("sinks", Q_HEADS), ("mix", RWKV_PROJ), ("w0", RWKV_DIM),
          ("a0", RWKV_DIM), ("k_k", RWKV_DIM), ("k_a", RWKV_DIM), ("r_k", RWKV_DIM), ("ln_w", RWKV_DIM),
          ("ln_b", RWKV_DIM), ("norm_ffn_g", D_MODEL), ("norm_final_g", D_MODEL))


def _pack_small(d):
    flat = jnp.concatenate([d[n].reshape(-1).astype(f32) for n, _ in _SMALL])
    return flat


def _unpack_small(flat):
    out, off = {}, 0
    for n, size in _SMALL:
        out[n] = flat[off:off + size]
        off += size
    return out


_SMALL_TOTAL = sum(s for _, s in _SMALL)


def kernel(x, meta_tokens, norm_mix_g, w_in, b_in, attn_sinks, rwkv_mix, rwkv_w0, rwkv_w2, rwkv_a0, rwkv_a2, rwkv_g2, rwkv_k_k, rwkv_k_a, rwkv_r_k, rwkv_ln_w, rwkv_ln_b, w_br_attn, w_br_rwkv, w_o, norm_ffn_g, w_ffn_gate, w_ffn_up, w_ffn_down, norm_final_g, loss_target, m_meta_tokens, m_norm_mix_g, m_w_in, m_b_in, m_attn_sinks, m_rwkv_mix, m_rwkv_w0, m_rwkv_w2, m_rwkv_a0, m_rwkv_a2, m_rwkv_g2, m_rwkv_k_k, m_rwkv_k_a, m_rwkv_r_k, m_rwkv_ln_w, m_rwkv_ln_b, m_w_br_attn, m_w_br_rwkv, m_w_o, m_norm_ffn_g, m_w_ffn_gate, m_w_ffn_up, m_w_ffn_down, m_norm_final_g, v_meta_tokens, v_norm_mix_g, v_w_in, v_b_in, v_attn_sinks, v_rwkv_mix, v_rwkv_w0, v_rwkv_w2, v_rwkv_a0, v_rwkv_a2, v_rwkv_g2, v_rwkv_k_k, v_rwkv_k_a, v_rwkv_r_k, v_rwkv_ln_w, v_rwkv_ln_b, v_w_br_attn, v_w_br_rwkv, v_w_o, v_norm_ffn_g, v_w_ffn_gate, v_w_ffn_up, v_w_ffn_down, v_norm_final_g):
    names = ("meta_tokens", "norm_mix_g", "w_in", "b_in", "attn_sinks", "rwkv_mix", "rwkv_w0", "rwkv_w2", "rwkv_a0",
             "rwkv_a2", "rwkv_g2", "rwkv_k_k", "rwkv_k_a", "rwkv_r_k", "rwkv_ln_w", "rwkv_ln_b", "w_br_attn",
             "w_br_rwkv", "w_o", "norm_ffn_g", "w_ffn_gate", "w_ffn_up", "w_ffn_down", "norm_final_g")
    w_all = dict(zip(names, (meta_tokens, norm_mix_g, w_in, b_in, attn_sinks, rwkv_mix, rwkv_w0, rwkv_w2, rwkv_a0,
                             rwkv_a2, rwkv_g2, rwkv_k_k, rwkv_k_a, rwkv_r_k, rwkv_ln_w, rwkv_ln_b, w_br_attn,
                             w_br_rwkv, w_o, norm_ffn_g, w_ffn_gate, w_ffn_up, w_ffn_down, norm_final_g)))
    m_all = dict(zip(names, (m_meta_tokens, m_norm_mix_g, m_w_in, m_b_in, m_attn_sinks, m_rwkv_mix, m_rwkv_w0,
                             m_rwkv_w2, m_rwkv_a0, m_rwkv_a2, m_rwkv_g2, m_rwkv_k_k, m_rwkv_k_a, m_rwkv_r_k,
                             m_rwkv_ln_w, m_rwkv_ln_b, m_w_br_attn, m_w_br_rwkv, m_w_o, m_norm_ffn_g, m_w_ffn_gate,
                             m_w_ffn_up, m_w_ffn_down, m_norm_final_g)))
    v_all = dict(zip(names, (v_meta_tokens, v_norm_mix_g, v_w_in, v_b_in, v_attn_sinks, v_rwkv_mix, v_rwkv_w0,
                             v_rwkv_w2, v_rwkv_a0, v_rwkv_a2, v_rwkv_g2, v_rwkv_k_k, v_rwkv_k_a, v_rwkv_r_k,
                             v_rwkv_ln_w, v_rwkv_ln_b, v_w_br_attn, v_w_br_rwkv, v_w_o, v_norm_ffn_g, v_w_ffn_gate,
                             v_w_ffn_up, v_w_ffn_down, v_norm_final_g)))
    cx, cy, _ = _position()
    chip = 2 * cx + cy

    t_of = dict(w_in_t="w_in", w_gate_t="w_ffn_gate", w_up_t="w_ffn_up", w_br_attn_t="w_br_attn",
                w_br_rwkv_t="w_br_rwkv", g2_t="rwkv_g2", w2_t="rwkv_w2", a2_t="rwkv_a2")
    plain_of = dict(w_down="w_ffn_down", w_o="w_o")
    meta_cols = meta_tokens.shape[1]

    def shard(k):
        return (w_all[t_of[k]][0].T if k in t_of else w_all[plain_of[k]][0]).astype(bf16)

    def whole(zone, own):
        return lax.dynamic_update_slice_in_dim(zone, own[None], chip, axis=0).reshape(-1, own.shape[-1])

    tiny = ("g2_t", "w2_t", "a2_t")
    late = ("w_gate_t", "w_up_t", "w_down", "w_o", "w_br_attn_t", "w_br_rwkv_t")
    w_in_own = shard("w_in_t")
    w_in_rows, w_in_cols = w_in_own.shape
    tiny_h = _exchange_start([shard(k) for k in tiny] + [meta_tokens], kind="whole", name="gather_tiny_start")
    w_in_h = _exchange_start([w_in_own + tiny_h[4][0, 0].astype(bf16)], kind="half", name="gather_w_in_start")
    behind = w_in_h[4][0, 0].astype(bf16)
    late_h = _exchange_start([shard(k) + behind for k in late], kind="whole", name="gather_late_start")
    own, zones = _exchange_wait(tiny_h, late_h[4], kind="whole", name="gather_tiny_wait")
    got = {k: whole(z, o) for k, z, o in zip(tiny, zones, own)}
    meta_full = whole(zones[-1], own[-1]).reshape(N_CHIPS, N_META, meta_cols).transpose(1, 0, 2).reshape(N_META, -1)
    p = dict(
        g2=got["g2_t"].T.astype(f32), w2=got["w2_t"].T.astype(f32), a2=got["a2_t"].T.astype(f32),
        b_in=b_in, sinks=attn_sinks, mix=rwkv_mix, w0=rwkv_w0, a0=rwkv_a0, k_k=rwkv_k_k, k_a=rwkv_k_a,
        r_k=rwkv_r_k.reshape(1, RWKV_DIM), ln_w=rwkv_ln_w, ln_b=rwkv_ln_b, norm_mix_g=norm_mix_g,
        norm_ffn_g=norm_ffn_g, norm_final_g=norm_final_g.reshape(1, D_MODEL),
    )

    def early_weights(after):
        own_h, zones_h = _exchange_wait(w_in_h, after, kind="half", name="gather_w_in_wait")
        zone = _swap_halves(zones_h[0], name="swap_w_in_halves")
        own_halves = own_h[0].reshape(w_in_rows, 2, w_in_cols // 2).transpose(1, 0, 2)[:, None]
        zone = lax.dynamic_update_slice(zone, own_halves, (0, chip, 0, 0))
        return dict(w_in_lr=zone.reshape(2, N_CHIPS * w_in_rows, w_in_cols // 2))

    def late_weights(after):
        own_l, zones_l = _exchange_wait(late_h, after, kind="whole", name="gather_late_wait")
        return {k: whole(z, o) for k, z, o in zip(late, zones_l, own_l)}

    started = {}

    def emit(group, grads_):
        keys = list(grads_)
        slabs = []
        for k in keys:
            a = grads_[k].T if k in ("g2", "w2", "a2") else grads_[k]
            slabs.append(a.reshape(N_CHIPS, a.shape[0] // N_CHIPS, a.shape[1]))
        started[group] = (keys, _exchange_start(slabs, kind="slab", name="scatter_" + group + "_start"))
        return started[group][1][4][0, 0]

    loss, dx, g = _local_step(x[0], loss_target[0], meta_full, p, early_weights, late_weights, emit)

    grads, delta, new_m, new_v = {}, {}, {}, {}
    in_grad_layout = ("w_in_t", "w_gate_t", "w_up_t")
    weight_of = {**t_of, **plain_of}

    def partial_sums(groups, after):
        parts = {}
        for group in groups:
            keys, handle = started[group]
            slabs, lands = _exchange_wait(handle, after, kind="slab", name="scatter_" + group + "_wait")
            parts.update({k: _sum_own_and_received(s, l, name="sum_chips_" + k) for k, s, l in zip(keys, slabs, lands)})
        return parts

    def update(keys, mine, theirs):
        for k, part, other in zip(keys, mine, theirs):
            both = [part, other]
            k = k + "_t" if k in ("g2", "w2", "a2") else k
            n = weight_of[k]
            shape2 = w_all[n].shape[1:]
            w_, m_, v_ = (a.reshape(shape2) for a in (w_all[n], m_all[n], v_all[n]))
            if k in in_grad_layout:
                res = [t.T for t in _adamw(w_.T, both, m_.T, v_.T, name="adamw_" + n)]
            else:
                res = _adamw(w_, both, m_, v_, name="adamw_" + n, transposed=k in t_of)
            grads[n], delta[n], new_m[n], new_v[n] = (t.reshape(w_all[n].shape) for t in res)
        return delta[n]

    parts_a = partial_sums(("ffn", "branch"), dx)
    swap_a = _exchange_start(list(parts_a.values()), kind="sibling", name="swap_cores_a_start")
    small = jnp.concatenate([_pack_small(g), loss.reshape(1)])
    small_rows = -(-small.shape[0] // PACK_W)
    small = jnp.concatenate([small, jnp.zeros((small_rows * PACK_W - small.shape[0],), f32)]).reshape(small_rows, PACK_W)
    small_rows8 = -(-(small_rows + N_META) // 8) * 8
    reduced = _all_reduce_small(_pad_rows(jnp.concatenate([g["meta"], small], axis=0), small_rows8), swap_a[4],
                                name="reduce_small")
    done = update(list(parts_a), *_exchange_wait(swap_a, reduced, kind="sibling", name="swap_cores_a_wait"))
    parts_b = partial_sums(("input",), done)
    update(list(parts_b), list(parts_b.values()), _swap_cores(list(parts_b.values()), name="swap_cores_b"))
    g_meta = lax.dynamic_slice_in_dim(reduced[:N_META], chip * meta_cols, meta_cols, axis=1)
    flat = reduced[N_META:N_META + small_rows].reshape(-1)
    g_small = _unpack_small(flat)
    loss_total = flat[_SMALL_TOTAL]

    small_of = dict(norm_mix_g="norm_mix_g", b_in="b_in", attn_sinks="sinks", rwkv_mix="mix", rwkv_w0="w0",
                    rwkv_a0="a0", rwkv_k_k="k_k", rwkv_k_a="k_a", rwkv_r_k="r_k", rwkv_ln_w="ln_w",
                    rwkv_ln_b="ln_b", norm_ffn_g="norm_ffn_g", norm_final_g="norm_final_g")
    grads["meta_tokens"] = g_meta
    for n, k in small_of.items():
        grads[n] = g_small[k].reshape(w_all[n].shape)

    rest = [n for n in names if n not in delta]

    def pack_rest(src):
        flat_ = jnp.concatenate([src[n].reshape(-1) for n in rest])
        rows_ = -(-flat_.shape[0] // (8 * PACK_W)) * 8
        return jnp.concatenate([flat_, jnp.ones((rows_ * PACK_W - flat_.shape[0],), f32)]).reshape(rows_, PACK_W)

    _, d_, m_, v_ = _adamw(pack_rest(w_all), [pack_rest(grads)], pack_rest(m_all), pack_rest(v_all),
                           name="adamw_small")
    off = 0
    for n in rest:
        size = w_all[n].size
        for dst, src in ((delta, d_), (new_m, m_), (new_v, v_)):
            dst[n] = src.reshape(-1)[off:off + size].reshape(w_all[n].shape)
        off += size

    return (loss_total, dx.reshape(x.shape), *[grads[n] for n in names], *[delta[n] for n in names],
            *[new_m[n] for n in names], *[new_v[n] for n in names])
```

```python
import math

import jax
import jax.numpy as jnp
from jax import lax
from jax.experimental import pallas as pl
from jax.experimental.pallas import tpu as pltpu

f32 = jnp.float32
bf16 = jnp.bfloat16

D_MODEL = 1024
N_META = 16
HEAD_DIM = 64
Q_HEADS = 8
KV_HEADS = 2
GROUP = Q_HEADS // KV_HEADS
WINDOW = 128
BLOCK = 128
ROPE_THETA = 500000.0
ROPE_DIM = HEAD_DIM // 4
RWKV_HEADS = 8
RWKV_HEAD = 64
RWKV_DIM = RWKV_HEADS * RWKV_HEAD
DECAY_LORA = 64
AAA_LORA = 64
GATE_LORA = 160
LORA_W = DECAY_LORA + AAA_LORA + GATE_LORA
RWKV_LN_EPS = 64e-5
D_FF = 2816
Q_W = Q_HEADS * HEAD_DIM
KV_W = KV_HEADS * HEAD_DIM
ATTN_PROJ = Q_W + 2 * KV_W
RKV_W = 3 * RWKV_DIM
RWKV_PROJ = RKV_W + LORA_W
D_IN = ATTN_PROJ + RWKV_PROJ + 2 * D_MODEL
RMS_EPS = 1e-6
NEG_INF = -1e30
PAD = BLOCK - N_META
FRONT = PAD + N_META

ADAM_LR = 0.001
ADAM_B1 = 0.9
ADAM_B2 = 0.999
ADAM_EPS = 1e-08
ADAM_WD = 0.01
ADAM_STEP = 10

N_CHIPS = 4
N_DEV = 8
CHUNK = 128
VMEM_LIMIT = 56 * 1024 * 1024
PACK_W = 1024
MESH = pl.DeviceIdType.MESH


def _tile(m, pref=384):
    for step in (16, 8):
        for t in range(min(m, pref) // step * step, 0, -step):
            if m % t == 0:
                return t
    return m


def _params(sem=None):
    return pltpu.CompilerParams(dimension_semantics=sem, vmem_limit_bytes=VMEM_LIMIT)


def _full(shape):
    nd = len(shape)
    return pl.BlockSpec(shape, lambda *_: (0,) * nd)


def _dot(a, b, dims="nn"):
    dn = {"nn": (((1,), (0,)), ((), ())), "nt": (((1,), (1,)), ((), ())), "tn": (((0,), (0,)), ((), ()))}[dims]
    return lax.dot_general(a.astype(bf16), b.astype(bf16), dn, preferred_element_type=f32)


def _two_pass(x, m, dims="nn"):
    x_hi = x.astype(bf16)
    x_lo = (x - x_hi.astype(f32)).astype(bf16)
    return _dot(x_hi, m, dims) + _dot(x_lo, m, dims)


@jax.custom_vjp
def _dot_const(x, m):
    return _two_pass(x, m)


def _dot_const_fwd(x, m):
    return _two_pass(x, m), m


def _dot_const_bwd(m, ct):
    return _two_pass(ct, m, "nt"), jnp.zeros_like(m)


_dot_const.defvjp(_dot_const_fwd, _dot_const_bwd)


def _two_pass_left(m, x, dims):
    x_hi = x.astype(bf16)
    x_lo = (x - x_hi.astype(f32)).astype(bf16)
    return _dot(m, x_hi, dims) + _dot(m, x_lo, dims)


@jax.custom_vjp
def _const_dot(m, x):
    return _two_pass_left(m, x, "nn")


def _const_dot_fwd(m, x):
    return _two_pass_left(m, x, "nn"), m


def _const_dot_bwd(m, ct):
    return jnp.zeros_like(m), _two_pass_left(m, ct, "tn")


_const_dot.defvjp(_const_dot_fwd, _const_dot_bwd)


def _mm(a, b, mode, *, name, out_dtype=f32, bias=None, add=None, zero_rows_below=0):
    m, _ = a.shape
    n = b.shape[1] if mode == "nn" else b.shape[0]
    tm = _tile(m)
    has_bias, has_add = bias is not None, add is not None

    def body(*refs):
        a_ref, b_ref = refs[0], refs[1]
        o_ref = refs[-1]
        acc = _dot(a_ref[...], b_ref[...], mode)
        k = 2
        if has_bias:
            acc = acc + refs[k][...]
            k += 1
        if zero_rows_below:
            rows = pl.program_id(0) * tm + lax.broadcasted_iota(jnp.int32, acc.shape, 0)
            acc = jnp.where(rows >= zero_rows_below, acc, 0.0)
        if has_add:
            acc = acc + refs[k][...].astype(f32)
        o_ref[...] = acc.astype(out_dtype)

    ins = [a, b]
    in_specs = [pl.BlockSpec((tm, a.shape[1]), lambda i: (i, 0)), _full(b.shape)]
    if has_bias:
        ins.append(bias)
        in_specs.append(_full(bias.shape))
    if has_add:
        ins.append(add)
        in_specs.append(pl.BlockSpec((tm, n), lambda i: (i, 0)))
    return pl.pallas_call(
        body, name=name, grid=(m // tm,), in_specs=in_specs,
        out_specs=pl.BlockSpec((tm, n), lambda i: (i, 0)),
        out_shape=jax.ShapeDtypeStruct((m, n), out_dtype),
        compiler_params=_params(("parallel",)),
    )(*ins)


def _pieces(widths):
    out, off = [], 0
    for w in widths:
        out.append((off, w))
        off += w
    return out


def _proj_in(a, w_lr, bias, widths, *, name, zero_rows_below=0):
    m, kdim = a.shape
    half = kdim // 2
    tm = _tile(m)

    def body(a_ref, w_ref, b_ref, *outs):
        a_l, a_r = a_ref[:, :half], a_ref[:, half:]
        for (off, width), o_ref in zip(_pieces(widths), outs):
            acc = _dot(a_l, w_ref[0, off:off + width, :], "nt") + _dot(a_r, w_ref[1, off:off + width, :], "nt")
            acc = acc + b_ref[:, off:off + width]
            if zero_rows_below:
                rows = pl.program_id(0) * tm + lax.broadcasted_iota(jnp.int32, acc.shape, 0)
                acc = jnp.where(rows >= zero_rows_below, acc, 0.0)
            o_ref[...] = acc.astype(o_ref.dtype)

    return pl.pallas_call(
        body, name=name, grid=(m // tm,),
        in_specs=[pl.BlockSpec((tm, kdim), lambda i: (i, 0)), _full(w_lr.shape), _full(bias.shape)],
        out_specs=[pl.BlockSpec((tm, w), lambda i: (i, 0)) for w in widths],
        out_shape=[jax.ShapeDtypeStruct((m, w), bf16) for w in widths],
        compiler_params=_params(("parallel",)),
    )(a, w_lr, bias)


def _proj_in_bwd(d_list, w_lr, *, name):
    m = d_list[0].shape[0]
    half = w_lr.shape[2]
    widths = [d.shape[1] for d in d_list]
    tm = _tile(m)

    def body(*refs):
        w_ref, o_ref = refs[-2], refs[-1]
        for side in range(2):
            acc = None
            for (off, width), d_ref in zip(_pieces(widths), refs):
                term = _dot(d_ref[...], w_ref[side, off:off + width, :])
                acc = term if acc is None else acc + term
            o_ref[:, side * half:(side + 1) * half] = acc.astype(o_ref.dtype)

    return pl.pallas_call(
        body, name=name, grid=(m // tm,),
        in_specs=[pl.BlockSpec((tm, w), lambda i: (i, 0)) for w in widths] + [_full(w_lr.shape)],
        out_specs=pl.BlockSpec((tm, 2 * half), lambda i: (i, 0)),
        out_shape=jax.ShapeDtypeStruct((m, 2 * half), bf16),
        compiler_params=_params(("parallel",)),
    )(*d_list, w_lr)


def _residual_norm(a, w, res, g, *, name):
    m, d = res.shape
    tm = _tile(m)

    def body(a_ref, w_ref, r_ref, g_ref, h_ref, n_ref):
        h = _dot(a_ref[...], w_ref[...]) + r_ref[...]
        h_ref[...] = h
        n_ref[...] = _rms(h, g_ref[...]).astype(n_ref.dtype)

    tile = pl.BlockSpec((tm, d), lambda i: (i, 0))
    return pl.pallas_call(
        body, name=name, grid=(m // tm,),
        in_specs=[pl.BlockSpec((tm, a.shape[1]), lambda i: (i, 0)), _full(w.shape), tile, _full(g.shape)],
        out_specs=[tile, tile],
        out_shape=[jax.ShapeDtypeStruct((m, d), f32), jax.ShapeDtypeStruct((m, d), bf16)],
        compiler_params=_params(("parallel",)),
    )(a, w, res, g)


def _residual_norm_bwd(d_list, w_list, h, g, dh_out, *, name):
    m, d = h.shape
    k = len(d_list)
    tm = _tile(m)

    def body(*refs):
        h_ref, g_ref, dho_ref, dh_ref, dg_ref = refs[2 * k:]
        dn = _dot(refs[0][...], refs[k][...])
        for i in range(1, k):
            dn = dn + _dot(refs[i][...], refs[k + i][...])
        _, vjp = jax.vjp(lambda hv, gv: (_rms(hv, gv), hv), h_ref[...], g_ref[...])
        dh, dg = vjp((dn, dho_ref[...]))
        dh_ref[...] = dh

        @pl.when(pl.program_id(0) == 0)
        def _():
            dg_ref[...] = jnp.zeros_like(dg_ref)

        dg_ref[...] += dg

    tile = pl.BlockSpec((tm, d), lambda i: (i, 0))
    return pl.pallas_call(
        body, name=name, grid=(m // tm,),
        in_specs=[pl.BlockSpec((tm, a.shape[1]), lambda i: (i, 0)) for a in d_list] + [_full(w.shape) for w in w_list]
        + [tile, _full(g.shape), tile],
        out_specs=[tile, _full(g.shape)],
        out_shape=[jax.ShapeDtypeStruct((m, d), f32), jax.ShapeDtypeStruct(g.shape, f32)],
        compiler_params=_params(("arbitrary",)),
    )(*d_list, *w_list, h, g, dh_out)


def _mm_tn(a, b, *, name, colsum=False, out_dtype=bf16):
    r, m = a.shape
    n = b.shape[1]
    tr = _tile(r, 1408)
    tmo = m
    for cand in (1408, 1024, 768, 512):
        if m > 1024 and m % cand == 0:
            tmo = cand
            break
    steps = r // tr

    def body(a_ref, b_ref, o_ref, *rest):
        acc = rest[-1]
        i = pl.program_id(1)

        @pl.when(i == 0)
        def _():
            acc[...] = jnp.zeros_like(acc)
            if colsum:
                rest[0][...] = jnp.zeros_like(rest[0])

        acc[...] += _dot(a_ref[...], b_ref[...], "tn")
        if colsum:
            rest[0][...] += jnp.sum(a_ref[...].astype(f32), axis=0, keepdims=True)

        @pl.when(i == steps - 1)
        def _():
            o_ref[...] = acc[...].astype(out_dtype)

    out_shape = [jax.ShapeDtypeStruct((m, n), out_dtype)]
    out_specs = [pl.BlockSpec((tmo, n), lambda j, i: (j, 0))]
    if colsum:
        out_shape.append(jax.ShapeDtypeStruct((1, m), f32))
        out_specs.append(pl.BlockSpec((1, tmo), lambda j, i: (0, j)))
    res = pl.pallas_call(
        body, name=name, grid=(m // tmo, steps),
        in_specs=[pl.BlockSpec((tr, tmo), lambda j, i: (i, j)), pl.BlockSpec((tr, n), lambda j, i: (i, 0))],
        out_specs=out_specs, out_shape=out_shape,
        scratch_shapes=[pltpu.VMEM((tmo, n), f32)],
        compiler_params=_params(("parallel", "arbitrary")),
    )(a, b)
    return res if colsum else res[0]


def _rowwise(fn, rows, params, outs, *, name, tm=None):
    m = rows[0].shape[0]
    tm = tm or _tile(m)
    nr, npar = len(rows), len(params)

    def body(*refs):
        vals = [r[...] for r in refs[:nr + npar]]
        res = fn(*vals)
        for o_ref, v in zip(refs[nr + npar:], res):
            o_ref[...] = v.astype(o_ref.dtype)

    return pl.pallas_call(
        body, name=name, grid=(m // tm,),
        in_specs=[pl.BlockSpec((tm, r.shape[1]), lambda i: (i, 0)) for r in rows] + [_full(p.shape) for p in params],
        out_specs=[pl.BlockSpec((tm, w), lambda i: (i, 0)) for w, _ in outs],
        out_shape=[jax.ShapeDtypeStruct((m, w), dt) for w, dt in outs],
        compiler_params=_params(("parallel",)),
    )(*rows, *params)


def _rowwise_bwd(fn, rows, params, cts, *, name, diff_rows, diff_params, tm=None, zero_rows_below=0, out_dtypes=None):
    m = rows[0].shape[0]
    tm = tm or _tile(m)
    nr, npar = len(rows), len(params)
    d_idx = [i for i in range(nr) if diff_rows[i]]
    p_idx = [i for i in range(npar) if diff_params[i]]
    out_dtypes = out_dtypes or [f32] * len(d_idx)
    flat_cts = [c for group in cts for c in group]
    n_ct = len(flat_cts)

    def body(*refs):
        vals = [r[...] for r in refs[:nr + npar]]
        ct_refs = refs[nr + npar:nr + npar + n_ct]
        out_refs = refs[nr + npar + n_ct:]
        ct_vals, k = [], 0
        for group in cts:
            acc = ct_refs[k][...].astype(f32)
            for extra in range(1, len(group)):
                acc = acc + ct_refs[k + extra][...].astype(f32)
            k += len(group)
            if zero_rows_below:
                rr = pl.program_id(0) * tm + lax.broadcasted_iota(jnp.int32, acc.shape, 0)
                acc = jnp.where(rr >= zero_rows_below, acc, 0.0)
            ct_vals.append(acc)

        def g(*dargs):
            full = list(vals)
            for pos, i in enumerate(d_idx):
                full[i] = dargs[pos]
            for pos, i in enumerate(p_idx):
                full[nr + i] = dargs[len(d_idx) + pos]
            return tuple(fn(*full))

        _, vjp = jax.vjp(g, *[vals[i].astype(f32) for i in d_idx], *[vals[nr + i] for i in p_idx])
        grads = vjp(tuple(ct_vals))
        for pos in range(len(d_idx)):
            out_refs[pos][...] = grads[pos].astype(out_refs[pos].dtype)
        first = pl.program_id(0) == 0
        for pos in range(len(p_idx)):
            o_ref = out_refs[len(d_idx) + pos]

            @pl.when(first)
            def _(o_ref=o_ref):
                o_ref[...] = jnp.zeros_like(o_ref)

            o_ref[...] += grads[len(d_idx) + pos]

    return pl.pallas_call(
        body, name=name, grid=(m // tm,),
        in_specs=[pl.BlockSpec((tm, r.shape[1]), lambda i: (i, 0)) for r in rows] + [_full(p.shape) for p in params]
        + [pl.BlockSpec((tm, c.shape[1]), lambda i: (i, 0)) for c in flat_cts],
        out_specs=[pl.BlockSpec((tm, rows[i].shape[1]), lambda i_: (i_, 0)) for i in d_idx]
        + [_full(params[i].shape) for i in p_idx],
        out_shape=[jax.ShapeDtypeStruct(rows[i].shape, dt) for i, dt in zip(d_idx, out_dtypes)]
        + [jax.ShapeDtypeStruct(params[i].shape, f32) for i in p_idx],
        compiler_params=_params(("arbitrary",)),
    )(*rows, *params, *flat_cts)


def _rms(x, g):
    return x * lax.rsqrt(jnp.mean(x * x, axis=-1, keepdims=True) + RMS_EPS) * g


def _head_sum_matrix(width, head):
    idx = jnp.arange(width) // head
    return (idx[:, None] == idx[None, :]).astype(f32)


def _rope_tables(lp):
    half = ROPE_DIM // 2
    pos = (jnp.arange(lp) - PAD).astype(f32)
    inv_freq = jnp.power(jnp.float32(ROPE_THETA), -jnp.arange(half, dtype=f32) * (2.0 / ROPE_DIM))
    ang = pos[:, None] * inv_freq[None, :]
    cos, sin = jnp.cos(ang), jnp.sin(ang)
    ones = jnp.ones((lp, HEAD_DIM - ROPE_DIM), f32)
    zeros = jnp.zeros((lp, HEAD_DIM - ROPE_DIM), f32)
    cos_t = jnp.concatenate([cos, cos, ones], axis=1)
    sin_t = jnp.concatenate([-sin, sin, zeros], axis=1)
    i = jnp.arange(HEAD_DIM)
    src = jnp.where(i < half, i + half, jnp.where(i < ROPE_DIM, i - half, i))
    swap = ((i[:, None] == src[None, :]) & (i[None, :] < ROPE_DIM)).astype(f32)
    return cos_t, sin_t, swap


def _attn_prep(qkv, cos_t, sin_t, swap):
    outs = []
    for h in range(Q_HEADS + KV_HEADS):
        t = qkv[:, h * HEAD_DIM:(h + 1) * HEAD_DIM]
        outs.append(t * cos_t + _dot_const(t, swap) * sin_t)
    q = jnp.concatenate(outs[:Q_HEADS], axis=1)
    k = jnp.concatenate(outs[Q_HEADS:], axis=1)
    return q, k, qkv[:, Q_W + KV_W:]


def _softplus(z):
    return jnp.maximum(z, 0.0) + jnp.log1p(jnp.exp(-jnp.abs(z)))


def _rwkv_prep(rkv, lora, w0, w2, a0, a2, g2, k_k, k_a, hsum):
    r = rkv[:, :RWKV_DIM]
    k = rkv[:, RWKV_DIM:2 * RWKV_DIM]
    v = rkv[:, 2 * RWKV_DIM:]
    dw = lora[:, :DECAY_LORA]
    da = lora[:, DECAY_LORA:DECAY_LORA + AAA_LORA]
    dg = lora[:, DECAY_LORA + AAA_LORA:]
    w = -_softplus(-(w0 + _dot(jnp.tanh(dw), w2))) - 0.5
    a = jax.nn.sigmoid(a0 + _dot(da, a2))
    g = _dot(jax.nn.sigmoid(dg), g2)
    kk = k * k_k
    kk = kk * lax.rsqrt(jnp.maximum(_dot_const(kk * kk, hsum), 1e-24))
    k = k * (1.0 + (a - 1.0) * k_a)
    log_decay = -jnp.exp(w)
    return r, log_decay, k, v, -kk, kk * a, g


def _rwkv_post(y, r, k, v, g, ln_w, ln_b, r_k, hmean):
    hsum = hmean * RWKV_HEAD
    mean = _dot_const(y, hmean)
    yc = y - mean
    var = _dot_const(yc * yc, hmean)
    yn = yc * lax.rsqrt(var + RWKV_LN_EPS) * ln_w + ln_b
    bonus = _dot_const(r * k * r_k, hsum) * v
    return ((yn + bonus) * g,)


def _merge(gates, br_a, br_r):
    sg = jax.nn.sigmoid(gates)
    return (sg[:, :D_MODEL] * br_a + sg[:, D_MODEL:] * br_r,)


def _swiglu(gate, up):
    return (jax.nn.silu(gate) * up,)


def _ffn_in(f, w_gate_t, w_up_t, *, name):
    m, d = f.shape
    n = w_gate_t.shape[0]
    tm = _tile(m)

    def body(f_ref, wg_ref, wu_ref, g_ref, u_ref, a_ref):
        g = _dot(f_ref[...], wg_ref[...], "nt")
        u = _dot(f_ref[...], wu_ref[...], "nt")
        g_ref[...] = g.astype(g_ref.dtype)
        u_ref[...] = u.astype(u_ref.dtype)
        a_ref[...] = _swiglu(g, u)[0].astype(a_ref.dtype)

    spec = pl.BlockSpec((tm, n), lambda i: (i, 0))
    return pl.pallas_call(
        body, name=name, grid=(m // tm,),
        in_specs=[pl.BlockSpec((tm, d), lambda i: (i, 0)), _full(w_gate_t.shape), _full(w_up_t.shape)],
        out_specs=[spec] * 3, out_shape=[jax.ShapeDtypeStruct((m, n), bf16)] * 3,
        compiler_params=_params(("parallel",)),
    )(f, w_gate_t, w_up_t)


def _branch_merge(y_attn, y_rwkv, w_attn_t, w_rwkv_t, gates, *, name):
    m = y_attn.shape[0]
    tm = _tile(m)

    def body(ya_ref, yr_ref, wa_ref, wr_ref, g_ref, a_ref, r_ref, o_ref):
        br_a = _dot(ya_ref[...], wa_ref[...], "nt")
        br_r = _dot(yr_ref[...], wr_ref[...], "nt")
        a_ref[...] = br_a.astype(a_ref.dtype)
        r_ref[...] = br_r.astype(r_ref.dtype)
        o_ref[...] = _merge(g_ref[...].astype(f32), br_a, br_r)[0].astype(o_ref.dtype)

    rows = lambda a: pl.BlockSpec((tm, a.shape[1]), lambda i: (i, 0))
    spec = pl.BlockSpec((tm, D_MODEL), lambda i: (i, 0))
    return pl.pallas_call(
        body, name=name, grid=(m // tm,),
        in_specs=[rows(y_attn), rows(y_rwkv), _full(w_attn_t.shape), _full(w_rwkv_t.shape), rows(gates)],
        out_specs=[spec] * 3, out_shape=[jax.ShapeDtypeStruct((m, D_MODEL), bf16)] * 3,
        compiler_params=_params(("parallel",)),
    )(y_attn, y_rwkv, w_attn_t, w_rwkv_t, gates)


def _branch_merge_bwd(dh, w_o, gates, br_a, br_r, *, name):
    m = dh.shape[0]
    tm = _tile(m)

    def body(dh_ref, w_ref, g_ref, a_ref, r_ref, dg_ref, da_ref, dr_ref):
        dmerged = _dot(dh_ref[...], w_ref[...], "nt")
        _, vjp = jax.vjp(lambda g, a, r: _merge(g, a, r)[0], g_ref[...].astype(f32), a_ref[...].astype(f32),
                         r_ref[...].astype(f32))
        dg, da, dr = vjp(dmerged)
        dg_ref[...] = dg.astype(dg_ref.dtype)
        da_ref[...] = da.astype(da_ref.dtype)
        dr_ref[...] = dr.astype(dr_ref.dtype)

    rows = lambda a: pl.BlockSpec((tm, a.shape[1]), lambda i: (i, 0))
    return pl.pallas_call(
        body, name=name, grid=(m // tm,),
        in_specs=[rows(dh), _full(w_o.shape), rows(gates), rows(br_a), rows(br_r)],
        out_specs=[rows(gates), rows(br_a), rows(br_r)],
        out_shape=[jax.ShapeDtypeStruct(gates.shape, bf16), jax.ShapeDtypeStruct(br_a.shape, bf16),
                   jax.ShapeDtypeStruct(br_r.shape, bf16)],
        compiler_params=_params(("parallel",)),
    )(dh, w_o, gates, br_a, br_r)


def _ffn_in_bwd(dh, w_down, gate, up, *, name):
    m, d = dh.shape
    n = w_down.shape[0]
    tm = _tile(m)

    def body(dh_ref, w_ref, g_ref, u_ref, dg_ref, du_ref):
        dact = _dot(dh_ref[...], w_ref[...], "nt")
        _, vjp = jax.vjp(lambda a, b: _swiglu(a, b)[0], g_ref[...].astype(f32), u_ref[...].astype(f32))
        dg, du = vjp(dact)
        dg_ref[...] = dg.astype(dg_ref.dtype)
        du_ref[...] = du.astype(du_ref.dtype)

    spec = pl.BlockSpec((tm, n), lambda i: (i, 0))
    return pl.pallas_call(
        body, name=name, grid=(m // tm,),
        in_specs=[pl.BlockSpec((tm, d), lambda i: (i, 0)), _full(w_down.shape), spec, spec],
        out_specs=[spec] * 2, out_shape=[jax.ShapeDtypeStruct((m, n), bf16)] * 2,
        compiler_params=_params(("parallel",)),
    )(dh, w_down, gate, up)


HALO = 16


def _previous_rows(x, before_ref, first_tile):
    rows = lax.broadcasted_iota(jnp.int32, x.shape, 0)
    last = jnp.where(first_tile, 0.0, before_ref[HALO - 1:HALO, :].astype(f32))
    return jnp.where(rows == 0, last, pltpu.roll(x, 1, axis=0))


def _mixer_inputs(ps, mixes, params, *, name):
    m = ps[0].shape[0]
    tm = _tile(m)
    sub = tm // HALO
    n_par = len(params)

    def body(*refs):
        first = pl.program_id(0) == 0
        pf = []
        for k in range(2):
            x = refs[k][...].astype(f32)
            pf.append(x + (_previous_rows(x, refs[2 + k], first) - x) * refs[4 + k][...])
        res = _rwkv_prep(*pf, *[ref[...] for ref in refs[6:6 + n_par]])
        for o_ref, val in zip(refs[6 + n_par:], res):
            o_ref[...] = val

    tile = lambda a: pl.BlockSpec((tm, a.shape[1]), lambda i: (i, 0))
    before = lambda a: pl.BlockSpec((HALO, a.shape[1]), lambda i: (jnp.maximum(i * sub - 1, 0), 0))
    out = pl.BlockSpec((tm, RWKV_DIM), lambda i: (i, 0))
    return pl.pallas_call(
        body, name=name, grid=(m // tm,),
        in_specs=[tile(a) for a in ps] + [before(a) for a in ps] + [_full(a.shape) for a in mixes + params],
        out_specs=[out] * 7, out_shape=[jax.ShapeDtypeStruct((m, RWKV_DIM), f32)] * 7,
        compiler_params=_params(("parallel",)),
    )(*ps, *ps, *mixes, *params)


def _mixer_inputs_bwd(ps, mixes, params, cts, *, name):
    m = ps[0].shape[0]
    tm = _tile(m)
    sub = tm // HALO
    nt = m // tm
    n_par = len(params)
    flat_cts = [c for group in cts for c in group]
    n_ct = len(flat_cts)

    def body(*refs):
        i = pl.program_id(0)
        tile_index = nt - 1 - i
        ct_refs = refs[6 + n_par:6 + n_par + n_ct]
        dp_refs = refs[6 + n_par + n_ct:8 + n_par + n_ct]
        dmix_refs = refs[8 + n_par + n_ct:10 + n_par + n_ct]
        dpar_refs = refs[10 + n_par + n_ct:9 + 2 * n_par + n_ct]
        carries = refs[9 + 2 * n_par + n_ct:]
        rows1 = tile_index * tm + lax.broadcasted_iota(jnp.int32, (tm, 1), 0)
        live = rows1 >= PAD

        @pl.when(i == 0)
        def _():
            for ref in (*dmix_refs, *dpar_refs, *carries):
                ref[...] = jnp.zeros_like(ref)

        xs, prevs, pf = [], [], []
        for k in range(2):
            x = refs[k][...].astype(f32)
            xp = _previous_rows(x, refs[2 + k], tile_index == 0)
            xs.append(x)
            prevs.append(xp)
            pf.append(x + (xp - x) * refs[4 + k][...])
        ct_vals, pos = [], 0
        for group in cts:
            acc = ct_refs[pos][...].astype(f32)
            for extra in range(1, len(group)):
                acc = acc + ct_refs[pos + extra][...].astype(f32)
            pos += len(group)
            ct_vals.append(jnp.where(live, acc, 0.0))
        par_vals = [ref[...] for ref in refs[6:6 + n_par]]
        _, vjp = jax.vjp(lambda *args: _rwkv_prep(*args, par_vals[-1]), *pf, *par_vals[:-1])
        g = vjp(tuple(ct_vals))
        for k in range(2):
            dpf = g[k]
            mixv = refs[4 + k][...]
            dm = dpf * mixv
            rows = lax.broadcasted_iota(jnp.int32, dm.shape, 0)
            dm_next = jnp.where(rows == tm - 1, carries[k][...], pltpu.roll(dm, tm - 1, axis=0))
            dp_refs[k][...] = jnp.where(live, dpf - dm + dm_next, 0.0).astype(dp_refs[k].dtype)
            carries[k][...] = dm[0:1, :]
            dmix_refs[k][...] += jnp.sum(dpf * (prevs[k] - xs[k]), axis=0, keepdims=True)
        for ref, val in zip(dpar_refs, g[2:]):
            ref[...] += val

    tile = lambda a: pl.BlockSpec((tm, a.shape[1]), lambda i: (nt - 1 - i, 0))
    before = lambda a: pl.BlockSpec((HALO, a.shape[1]), lambda i: (jnp.maximum((nt - 1 - i) * sub - 1, 0), 0))
    return pl.pallas_call(
        body, name=name, grid=(nt,),
        in_specs=[tile(a) for a in ps] + [before(a) for a in ps] + [_full(a.shape) for a in mixes + params]
        + [tile(c) for c in flat_cts],
        out_specs=[tile(a) for a in ps] + [_full(a.shape) for a in mixes + params[:-1]],
        out_shape=[jax.ShapeDtypeStruct(a.shape, bf16) for a in ps]
        + [jax.ShapeDtypeStruct(a.shape, f32) for a in mixes + params[:-1]],
        scratch_shapes=[pltpu.VMEM((1, a.shape[1]), f32) for a in ps],
        compiler_params=_params(("arbitrary",)),
    )(*ps, *ps, *mixes, *params, *flat_cts)


def _attn_masks(blk):
    qi = lax.broadcasted_iota(jnp.int32, (BLOCK, BLOCK), 0)
    ki = lax.broadcasted_iota(jnp.int32, (BLOCK, BLOCK), 1)
    qpos = blk * BLOCK + qi - PAD
    kpos_c = blk * BLOCK + ki - PAD
    kpos_p = kpos_c - BLOCK
    kpos_m = ki - PAD

    def band(kpos):
        return (kpos >= N_META) & (kpos <= qpos) & (qpos - kpos < WINDOW)

    return band(kpos_p), band(kpos_c), (kpos_m >= 0) & (kpos_m <= qpos)


def _attn_probs(qs, k3s, sink, oks):
    s = [[jnp.where(ok, _dot(qh, kx, "nt"), NEG_INF) for kx, ok in zip(k3, oks)] for qh, k3 in zip(qs, k3s)]
    mx = [jnp.maximum(jnp.maximum(jnp.max(t[0], -1, keepdims=True), jnp.max(t[1], -1, keepdims=True)),
                      jnp.maximum(jnp.max(t[2], -1, keepdims=True), sk)) for t, sk in zip(s, sink)]
    e = [[jnp.exp(tx - m) for tx in t] for t, m in zip(s, mx)]
    e_sink = [jnp.exp(sk - m) for sk, m in zip(sink, mx)]
    inv = [1.0 / (jnp.sum(t[0], -1, keepdims=True) + jnp.sum(t[1], -1, keepdims=True)
                  + jnp.sum(t[2], -1, keepdims=True) + es) for t, es in zip(e, e_sink)]
    return [[tx * i for tx in t] for t, i in zip(e, inv)], [es * i for es, i in zip(e_sink, inv)]


def _head_cols(i):
    return slice(i * HEAD_DIM, (i + 1) * HEAD_DIM)


def _attn_operands(refs):
    q_ref, kp_ref, kc_ref, km_ref, vp_ref, vc_ref, vm_ref, s_ref = refs
    qs = [q_ref[:, _head_cols(i)] * (HEAD_DIM ** -0.5) for i in range(Q_HEADS)]
    k3 = [[ref[:, _head_cols(h)] for ref in (kp_ref, kc_ref, km_ref)] for h in range(KV_HEADS)]
    v3 = [[ref[:, _head_cols(h)] for ref in (vp_ref, vc_ref, vm_ref)] for h in range(KV_HEADS)]
    return (qs, [k3[i // GROUP] for i in range(Q_HEADS)], [v3[i // GROUP] for i in range(Q_HEADS)],
            [s_ref[:, i:i + 1] for i in range(Q_HEADS)])


def _attention(q, k, v, sinks, *, name):
    lp = q.shape[0]
    nb = lp // BLOCK
    prev = lambda i: (jnp.maximum(i - 1, 0), 0)
    cur = lambda i: (i, 0)
    meta = lambda i: (0, 0)
    kv = lambda index: pl.BlockSpec((BLOCK, KV_W), index)

    def body(*refs):
        o_ref = refs[-1]
        qs, k3s, v3s, sink = _attn_operands(refs[:-1])
        p, _ = _attn_probs(qs, k3s, sink, _attn_masks(pl.program_id(0)))
        out = [_dot(ph[0], v3[0]) + _dot(ph[1], v3[1]) + _dot(ph[2], v3[2]) for ph, v3 in zip(p, v3s)]
        for i in range(Q_HEADS):
            o_ref[:, _head_cols(i)] = out[i].astype(o_ref.dtype)

    return pl.pallas_call(
        body, name=name, grid=(nb,),
        in_specs=[pl.BlockSpec((BLOCK, Q_W), cur), kv(prev), kv(cur), kv(meta), kv(prev), kv(cur), kv(meta),
                  _full((1, Q_HEADS))],
        out_specs=pl.BlockSpec((BLOCK, Q_W), cur),
        out_shape=jax.ShapeDtypeStruct((lp, Q_W), bf16),
        compiler_params=_params(("parallel",)),
    )(q, k, k, k, v, v, v, sinks)


def _attention_bwd(q, k, v, sinks, out, do, *, name):
    lp = q.shape[0]
    nb = lp // BLOCK
    cur = lambda n: (jnp.minimum(n, nb - 1), 0)
    prev = lambda n: (jnp.maximum(jnp.minimum(n, nb - 1) - 1, 0), 0)
    behind = lambda n: (jnp.maximum(n - 1, 0), 0)
    meta = lambda n: (0, 0)
    kv = lambda index: pl.BlockSpec((BLOCK, KV_W), index)
    scale = HEAD_DIM ** -0.5

    def body(*refs):
        ins, fwd_ref, do_ref = refs[:8], refs[8], refs[9]
        dq_ref, dk_ref, dv_ref, dkm_ref, dvm_ref, ds_ref, carry_k, carry_v = refs[10:]
        n = pl.program_id(0)

        @pl.when(n == 0)
        def _():
            for ref in (dkm_ref, dvm_ref, ds_ref, carry_k, carry_v):
                ref[...] = jnp.zeros_like(ref)

        @pl.when(n < nb)
        def _():
            qs, k3s, v3s, sink = _attn_operands(ins)
            do = [do_ref[:, _head_cols(i)] for i in range(Q_HEADS)]
            p, p_sink = _attn_probs(qs, k3s, sink, _attn_masks(n))
            delta = [jnp.sum(d * fwd_ref[:, _head_cols(i)].astype(f32), -1, keepdims=True) for i, d in enumerate(do)]
            dp = [[_dot(d, vx, "nt") for vx in v3] for d, v3 in zip(do, v3s)]
            ds = [[px * (dx - dl) for px, dx in zip(ph, dh)] for ph, dh, dl in zip(p, dp, delta)]
            dq = [_dot(dsh[0], k3[0]) + _dot(dsh[1], k3[1]) + _dot(dsh[2], k3[2]) for dsh, k3 in zip(ds, k3s)]
            for i in range(Q_HEADS):
                dq_ref[:, _head_cols(i)] = dq[i] * scale
                ds_ref[:, i:i + 1] -= jnp.sum(p_sink[i] * delta[i], axis=0, keepdims=True)
            for h in range(KV_HEADS):
                group = slice(h * GROUP, (h + 1) * GROUP)
                q_all = jnp.concatenate(qs[group], axis=0)
                do_all = jnp.concatenate(do[group], axis=0)
                dk3 = [_dot(jnp.concatenate([dsh[x] for dsh in ds[group]], axis=0), q_all, "tn") for x in range(3)]
                dv3 = [_dot(jnp.concatenate([ph[x] for ph in p[group]], axis=0), do_all, "tn") for x in range(3)]
                hs = _head_cols(h)
                for out_ref, carry, meta_ref, d3 in ((dk_ref, carry_k, dkm_ref, dk3),
                                                     (dv_ref, carry_v, dvm_ref, dv3)):
                    out_ref[:, hs] = carry[:, hs] + d3[0]
                    carry[:, hs] = d3[1]
                    meta_ref[:, hs] += d3[2]

        @pl.when(n == nb)
        def _():
            dk_ref[...] = carry_k[...]
            dv_ref[...] = carry_v[...]

    kv_shape = jax.ShapeDtypeStruct((lp, KV_W), f32)
    one_shape = jax.ShapeDtypeStruct((BLOCK, KV_W), f32)
    return pl.pallas_call(
        body, name=name, grid=(nb + 1,),
        in_specs=[pl.BlockSpec((BLOCK, Q_W), cur), kv(prev), kv(cur), kv(meta), kv(prev), kv(cur), kv(meta),
                  _full((1, Q_HEADS)), pl.BlockSpec((BLOCK, Q_W), cur), pl.BlockSpec((BLOCK, Q_W), cur)],
        out_specs=[pl.BlockSpec((BLOCK, Q_W), cur), kv(behind), kv(behind), kv(meta), kv(meta),
                   _full((1, Q_HEADS))],
        out_shape=[jax.ShapeDtypeStruct((lp, Q_W), f32), kv_shape, kv_shape, one_shape, one_shape,
                   jax.ShapeDtypeStruct((1, Q_HEADS), f32)],
        scratch_shapes=[pltpu.VMEM((BLOCK, KV_W), f32), pltpu.VMEM((BLOCK, KV_W), f32)],
        compiler_params=_params(("arbitrary",)),
    )(q, k, k, k, v, v, v, sinks, out, do)


@jax.custom_vjp
def _known_inverse(l, x):
    return x


def _known_inverse_fwd(l, x):
    return x, x


def _known_inverse_bwd(x, ct):
    return _dot(_dot(x, ct, "tn"), x, "nt"), jnp.zeros_like(x)


_known_inverse.defvjp(_known_inverse_fwd, _known_inverse_bwd)


@jax.custom_vjp
def _decayed(x, c):
    return (x * jnp.exp(c)).astype(bf16).astype(f32)


def _decayed_fwd(x, c):
    e = jnp.exp(c)
    out = (x * e).astype(bf16).astype(f32)
    return out, (e, out)


def _decayed_bwd(res, ct):
    e, out = res
    return ct * e, ct * out


_decayed.defvjp(_decayed_fwd, _decayed_bwd)


@jax.custom_vjp
def _pair(x, y):
    return _dot(x, y, "nt")


def _pair_fwd(x, y):
    return _dot(x, y, "nt"), (x, y)


def _pair_bwd(res, ct):
    x, y = res
    hi = ct.astype(bf16)
    lo = (ct - hi.astype(f32)).astype(bf16)
    return _dot(hi, y) + _dot(lo, y), _dot(hi, x, "tn") + _dot(lo, x, "tn")


_pair.defvjp(_pair_fwd, _pair_bwd)


def _scan_chunk(s0, r, lw, k, v, a, b, inv=None):
    t = r[0].shape[0]
    ii = lax.broadcasted_iota(jnp.int32, (t, t), 0)
    jj = lax.broadcasted_iota(jnp.int32, (t, t), 1)
    incl = jj <= ii
    strict = jj < ii
    tri = incl.astype(f32)
    eye = jnp.where(ii == jj, 1.0, 0.0)
    cl = [_const_dot(tri, x) for x in lw]
    mid = [c[t // 2 - 1:t // 2, :] for c in cl]
    s0 = [s * jnp.exp(m) for s, m in zip(s0, mid)]
    cl = [c - m for c, m in zip(cl, mid)]
    rt = [_decayed(x, c) for x, c in zip(r, cl)]
    at = [_decayed(x, c - l) for x, c, l in zip(a, cl, lw)]
    bt = [_decayed(x, -c) for x, c in zip(b, cl)]
    kt = [_decayed(x, -c) for x, c in zip(k, cl)]
    l_ab = [jnp.where(strict, _pair(x, y), 0.0) for x, y in zip(at, bt)]
    l_ak = [jnp.where(strict, _pair(x, y), 0.0) for x, y in zip(at, kt)]
    r_b = [jnp.where(incl, _pair(x, y), 0.0) for x, y in zip(rt, bt)]
    r_k = [jnp.where(incl, _pair(x, y), 0.0) for x, y in zip(rt, kt)]
    if inv is None:
        inv = [eye + x for x in l_ab]
        pw = l_ab
        for _ in range(int(math.log2(t)) - 1):
            pw = [_dot(x, x) for x in pw]
            inv = [x + _dot(x, y) for x, y in zip(inv, pw)]
    else:
        inv = [_known_inverse(x, y) for x, y in zip(l_ab, inv)]
    rhs = [_dot(x, s, "nt") + _dot(m, y) for x, s, m, y in zip(at, s0, l_ak, v)]
    u = [_dot(x, y) for x, y in zip(inv, rhs)]
    y_s = [_dot(x, s, "nt") for x, s in zip(rt, s0)]
    y = [ys + _dot(m, uu) + _dot(n, vv) for ys, m, uu, n, vv in zip(y_s, r_b, u, r_k, v)]
    grow = [s + _dot(uu, x, "tn") + _dot(vv, z, "tn") for s, uu, x, vv, z in zip(s0, u, bt, v, kt)]
    s1 = [g * jnp.exp(c[t - 1:t, :]) for g, c in zip(grow, cl)]
    return y, s1, inv


def _head_rows(h):
    return slice(h * RWKV_HEAD, (h + 1) * RWKV_HEAD)


def _per_head(ref):
    return [ref[:, _head_rows(h)] for h in range(RWKV_HEADS)]


def _scan(r, lw, k, v, a, b, *, name):
    lp = r.shape[0]
    nc = lp // CHUNK
    row = pl.BlockSpec((CHUNK, RWKV_DIM), lambda c: (c, 0))

    def body(r_ref, lw_ref, k_ref, v_ref, a_ref, b_ref, y_ref, s_ref, inv_ref, state):
        @pl.when(pl.program_id(0) == 0)
        def _():
            state[...] = jnp.zeros_like(state)

        s_ref[...] = state[...]
        s0 = [state[_head_rows(h), :] for h in range(RWKV_HEADS)]
        y, s1, inv = _scan_chunk(s0, *[_per_head(ref) for ref in (r_ref, lw_ref, k_ref, v_ref, a_ref, b_ref)])
        for h in range(RWKV_HEADS):
            y_ref[:, _head_rows(h)] = y[h]
            state[_head_rows(h), :] = s1[h]
            inv_ref[h * CHUNK:(h + 1) * CHUNK, :] = inv[h].astype(inv_ref.dtype)

    return pl.pallas_call(
        body, name=name, grid=(nc,), in_specs=[row] * 6,
        out_specs=[row, pl.BlockSpec((RWKV_DIM, RWKV_HEAD), lambda c: (c, 0)),
                   pl.BlockSpec((RWKV_HEADS * CHUNK, CHUNK), lambda c: (c, 0))],
        out_shape=[jax.ShapeDtypeStruct((lp, RWKV_DIM), f32), jax.ShapeDtypeStruct((nc * RWKV_DIM, RWKV_HEAD), f32),
                   jax.ShapeDtypeStruct((nc * RWKV_HEADS * CHUNK, CHUNK), bf16)],
        scratch_shapes=[pltpu.VMEM((RWKV_DIM, RWKV_HEAD), f32)],
        compiler_params=_params(("arbitrary",)),
    )(r, lw, k, v, a, b)


def _scan_bwd(r, lw, k, v, a, b, states, inverses, dy, *, name):
    lp = r.shape[0]
    nc = lp // CHUNK
    back = lambda c: (nc - 1 - c, 0)
    row = pl.BlockSpec((CHUNK, RWKV_DIM), back)

    def body(r_ref, lw_ref, k_ref, v_ref, a_ref, b_ref, s_ref, inv_ref, dy_ref,
             dr_ref, dlw_ref, dk_ref, dv_ref, da_ref, db_ref, dstate):
        @pl.when(pl.program_id(0) == 0)
        def _():
            dstate[...] = jnp.zeros_like(dstate)

        outs = (dr_ref, dlw_ref, dk_ref, dv_ref, da_ref, db_ref)
        s0 = [s_ref[_head_rows(h), :] for h in range(RWKV_HEADS)]
        inv = [inv_ref[h * CHUNK:(h + 1) * CHUNK, :].astype(f32) for h in range(RWKV_HEADS)]
        _, vjp = jax.vjp(lambda *args: _scan_chunk(*args, inv=inv)[:2], s0,
                         *[_per_head(ref) for ref in (r_ref, lw_ref, k_ref, v_ref, a_ref, b_ref)])
        g = vjp((_per_head(dy_ref), [dstate[_head_rows(h), :] for h in range(RWKV_HEADS)]))
        for h in range(RWKV_HEADS):
            dstate[_head_rows(h), :] = g[0][h]
            for o_ref, gv in zip(outs, g[1:]):
                o_ref[:, _head_rows(h)] = gv[h]

    shape = jax.ShapeDtypeStruct((lp, RWKV_DIM), f32)
    return pl.pallas_call(
        body, name=name, grid=(nc,),
        in_specs=[row] * 6 + [pl.BlockSpec((RWKV_DIM, RWKV_HEAD), back),
                              pl.BlockSpec((RWKV_HEADS * CHUNK, CHUNK), back), row],
        out_specs=[row] * 6, out_shape=[shape] * 6,
        scratch_shapes=[pltpu.VMEM((RWKV_DIM, RWKV_HEAD), f32)],
        compiler_params=_params(("arbitrary",)),
    )(r, lw, k, v, a, b, states, inverses, dy)


def _loss_head(h2, target, g_final, *, name):
    lp = h2.shape[0]
    tm = BLOCK
    front_tiles = FRONT // tm

    def body(h_ref, t_ref, g_ref, loss_ref, dh_ref, dg_ref):
        i = pl.program_id(0)
        real = i >= front_tiles

        def tile_loss(hv, gv):
            err = _rms(hv, gv) - t_ref[...]
            return jnp.where(real, 0.5 * jnp.sum(jnp.mean(err * err, axis=-1, keepdims=True)), 0.0)

        loss, (dh, dg) = jax.value_and_grad(tile_loss, argnums=(0, 1))(h_ref[...], g_ref[...])

        @pl.when(i == 0)
        def _():
            loss_ref[...] = jnp.zeros_like(loss_ref)
            dg_ref[...] = jnp.zeros_like(dg_ref)

        loss_ref[...] += jnp.full(loss_ref.shape, loss, f32)
        dg_ref[...] += dg
        dh_ref[...] = dh

    return pl.pallas_call(
        body, name=name, grid=(lp // tm,),
        in_specs=[pl.BlockSpec((tm, D_MODEL), lambda i: (i, 0)),
                  pl.BlockSpec((tm, D_MODEL), lambda i: (jnp.maximum(i - front_tiles, 0), 0)),
                  _full(g_final.shape)],
        out_specs=[_full((8, 128)), pl.BlockSpec((tm, D_MODEL), lambda i: (i, 0)), _full(g_final.shape)],
        out_shape=[jax.ShapeDtypeStruct((8, 128), f32), jax.ShapeDtypeStruct((lp, D_MODEL), f32),
                   jax.ShapeDtypeStruct(g_final.shape, f32)],
        compiler_params=_params(("arbitrary",)),
    )(h2, target, g_final)


def _input_norm_bwd(h0, g, du, dh1, *, name):
    lp = h0.shape[0]
    tm = FRONT

    def body(h_ref, g_ref, du_ref, dh1_ref, dx_ref, front_ref, dg_ref):
        i = pl.program_id(0)
        _, vjp = jax.vjp(lambda hv, gv: (_rms(hv, gv), hv), h_ref[...], g_ref[...])
        dh, dg = vjp((du_ref[...].astype(f32), dh1_ref[...]))

        @pl.when(i == 0)
        def _():
            dg_ref[...] = jnp.zeros_like(dg_ref)
            front_ref[...] = dh

        dg_ref[...] += dg
        dx_ref[...] = dh

    tile = pl.BlockSpec((tm, D_MODEL), lambda i: (i, 0))
    return pl.pallas_call(
        body, name=name, grid=(lp // tm,),
        in_specs=[tile, _full(g.shape), tile, tile],
        out_specs=[pl.BlockSpec((tm, D_MODEL), lambda i: (jnp.maximum(i - 1, 0), 0)), _full((tm, D_MODEL)),
                   _full(g.shape)],
        out_shape=[jax.ShapeDtypeStruct((lp - tm, D_MODEL), f32), jax.ShapeDtypeStruct((tm, D_MODEL), f32),
                   jax.ShapeDtypeStruct(g.shape, f32)],
        compiler_params=_params(("arbitrary",)),
    )(h0, g, du, dh1)


def _local_step(x, target, meta, p, early_weights=None, late_weights=None, emit=None):
    emit = emit or (lambda group, grads: 0.0)
    seq = x.shape[0]
    lp = seq + FRONT
    h0 = jnp.concatenate([jnp.zeros((PAD, D_MODEL), f32), meta, x], axis=0)
    cos_t, sin_t, swap = _rope_tables(lp)
    hsum = _head_sum_matrix(RWKV_DIM, RWKV_HEAD)
    hmean = hsum / RWKV_HEAD
    post_params = [p["ln_w"], p["ln_b"], p["r_k"], hmean]

    (u,) = _rowwise(lambda hv, g: (_rms(hv, g),), [h0], [p["norm_mix_g"]], [(D_MODEL, bf16)], name="norm_mix")
    if early_weights is not None:
        p = {**p, **early_weights(u)}
    prep_params = [p["w0"], p["w2"], p["a0"], p["a2"], p["g2"], p["k_k"], p["k_a"], hsum]
    qkv, p_rkv, p_lora, gates = _proj_in(u, p["w_in_lr"], p["b_in"], [ATTN_PROJ, RKV_W, LORA_W, 2 * D_MODEL],
                                         name="proj_in", zero_rows_below=PAD)

    q, k, v = _rowwise(_attn_prep, [qkv, cos_t, sin_t], [swap], [(Q_W, bf16), (KV_W, bf16), (KV_W, bf16)],
                       name="attn_prep")
    y_attn = _attention(q, k, v, p["sinks"], name="attention")

    mix_rkv, mix_lora = p["mix"][:, :RKV_W], p["mix"][:, RKV_W:]
    r_, lw_, k_, v_, a_, b_, g_ = _mixer_inputs([p_rkv, p_lora], [mix_rkv, mix_lora], prep_params,
                                                name="mixer_inputs")
    y_scan, states, inverses = _scan(r_, lw_, k_, v_, a_, b_, name="wkv_scan")
    (y_rwkv,) = _rowwise(_rwkv_post, [y_scan, r_, k_, v_, g_], post_params, [(RWKV_DIM, bf16)], name="rwkv_post")

    if late_weights is not None:
        p = {**p, **late_weights(y_rwkv)}
    br_a, br_r, merged = _branch_merge(y_attn, y_rwkv, p["w_br_attn_t"], p["w_br_rwkv_t"], gates, name="branch_merge")
    h1, f = _residual_norm(merged, p["w_o"], h0, p["norm_ffn_g"], name="out_proj")
    gate, up, act = _ffn_in(f, p["w_gate_t"], p["w_up_t"], name="ffn_in")
    h2 = _mm(act, p["w_down"], "nn", name="ffn_down", add=h1)

    loss8, dh2, d_final_g = _loss_head(h2, target, p["norm_final_g"], name="loss_head")
    dgate, dup = _ffn_in_bwd(dh2, p["w_down"], gate, up, name="ffn_in_bwd")
    d_w_down = _mm_tn(act, dh2, name="dw_down")
    d_w_gate_t = _mm_tn(dgate, f, name="dw_gate")
    d_w_up_t = _mm_tn(dup, f, name="dw_up")
    zero = emit("ffn", dict(w_down=d_w_down, w_gate_t=d_w_gate_t, w_up_t=d_w_up_t))
    dh1, d_ffn_g = _residual_norm_bwd([dgate, dup], [p["w_gate_t"], p["w_up_t"]], h1, p["norm_ffn_g"] + zero, dh2,
                                      name="norm_ffn_bwd")
    dgates, dbr_a, dbr_r = _branch_merge_bwd(dh1, p["w_o"], gates, br_a, br_r, name="branch_merge_bwd")
    d_w_o = _mm_tn(merged, dh1, name="dw_o")
    d_w_br_attn_t = _mm_tn(dbr_a, y_attn, name="dw_br_attn")
    d_w_br_rwkv_t = _mm_tn(dbr_r, y_rwkv, name="dw_br_rwkv")
    zero = emit("branch", dict(w_o=d_w_o, w_br_attn_t=d_w_br_attn_t, w_br_rwkv_t=d_w_br_rwkv_t))
    dy_attn = _mm(dbr_a, p["w_br_attn_t"], "nn", name="d_y_attn")
    dy_rwkv = _mm(dbr_r, p["w_br_rwkv_t"], "nn", name="d_y_rwkv")

    post_params = [p["ln_w"] + zero, p["ln_b"], p["r_k"], hmean]
    res = _rowwise_bwd(_rwkv_post, [y_scan, r_, k_, v_, g_], post_params, [[dy_rwkv]], name="rwkv_post_bwd",
                       diff_rows=[True] * 5, diff_params=[True, True, True, False])
    dy_scan, dr_p, dk_p, dv_p, dg_p, d_ln_w, d_ln_b, d_r_k = res
    dr_s, dlw_s, dk_s, dv_s, da_s, db_s = _scan_bwd(r_, lw_, k_, v_, a_, b_, states, inverses, dy_scan,
                                                    name="wkv_scan_bwd")
    res = _mixer_inputs_bwd([p_rkv, p_lora], [mix_rkv, mix_lora], prep_params,
                            [[dr_s, dr_p], [dlw_s], [dk_s, dk_p], [dv_s, dv_p], [da_s], [db_s], [dg_p]],
                            name="mixer_inputs_bwd")
    dp_rkv, dp_lora, d_mix_rkv, d_mix_lora, d_w0, d_w2, d_a0, d_a2, d_g2, d_k_k, d_k_a = res

    dq, dk, dv, dkm, dvm, d_sinks = _attention_bwd(q, k, v, p["sinks"], y_attn, dy_attn, name="attention_bwd")
    rest = jnp.zeros((lp - BLOCK, KV_W), f32)
    dkm, dvm = jnp.concatenate([dkm, rest], axis=0), jnp.concatenate([dvm, rest], axis=0)
    (dqkv,) = _rowwise_bwd(_attn_prep, [qkv, cos_t, sin_t], [swap], [[dq], [dk, dkm], [dv, dvm]], name="attn_prep_bwd",
                           diff_rows=[True, False, False], diff_params=[False], out_dtypes=[bf16])

    d_w_qkv_t, db_qkv = _mm_tn(dqkv, u, name="dw_qkv", colsum=True)
    d_w_rkv_t, db_rkv = _mm_tn(dp_rkv, u, name="dw_rkv", colsum=True)
    d_w_lora_t, db_lora = _mm_tn(dp_lora, u, name="dw_lora", colsum=True)
    d_w_gates_t, db_gates = _mm_tn(dgates, u, name="dw_gates", colsum=True)
    d_w_in_t = jnp.concatenate([d_w_qkv_t, d_w_rkv_t, d_w_lora_t, d_w_gates_t], axis=0)
    zero = emit("input", dict(w_in_t=d_w_in_t, g2=d_g2, w2=d_w2, a2=d_a2))
    du = _proj_in_bwd([dqkv, dp_rkv, dp_lora, dgates], p["w_in_lr"], name="d_u")
    dx, d_front, d_mix_g = _input_norm_bwd(h0, p["norm_mix_g"] + zero, du, dh1, name="norm_mix_bwd")

    grads = dict(
        w_in_t=d_w_in_t,
        b_in=jnp.concatenate([db_qkv, db_rkv, db_lora, db_gates], axis=1),
        mix=jnp.concatenate([d_mix_rkv, d_mix_lora], axis=1),
        norm_mix_g=d_mix_g, sinks=d_sinks, w0=d_w0, w2=d_w2, a0=d_a0, a2=d_a2, g2=d_g2, k_k=d_k_k, k_a=d_k_a,
        r_k=d_r_k, ln_w=d_ln_w, ln_b=d_ln_b, w_br_attn_t=d_w_br_attn_t, w_br_rwkv_t=d_w_br_rwkv_t, w_o=d_w_o,
        norm_ffn_g=d_ffn_g, w_gate_t=d_w_gate_t, w_up_t=d_w_up_t, w_down=d_w_down, norm_final_g=d_final_g,
        meta=d_front[PAD:],
    )
    return loss8[0, 0], dx, grads


def _position():
    return lax.axis_index("x"), lax.axis_index("y"), lax.axis_index("c")


def _other_chips(x, y):
    return [(1 - x, y), (x, 1 - y), (1 - x, 1 - y)]


_HBM = pl.BlockSpec(memory_space=pltpu.HBM)
_SEM = pl.BlockSpec(memory_space=pltpu.SEMAPHORE)
_EFFECT = pltpu.SideEffectType.DATAFLOW_SIDE_EFFECTING


def _landing_zone(src, kind):
    shape = {"whole": (N_CHIPS,) + src.shape, "half": (2, N_CHIPS, src.shape[0], src.shape[1] // 2),
             "slab": (3,) + src.shape[1:], "sibling": src.shape}[kind]
    return lax.empty(shape, src.dtype)


def _copies_per_source(kind):
    return 1 if kind == "sibling" else 3


def _chip_copies(src_refs, land_refs, send_sems, recv_sems, kind):
    x, y, c = _position()
    if kind == "sibling":
        return [pltpu.make_async_remote_copy(
            src_ref=src, dst_ref=land, send_sem=send_sems.at[a], recv_sem=recv_sems.at[a],
            device_id=(x, y, 1 - c), device_id_type=MESH) for a, (src, land) in enumerate(zip(src_refs, land_refs))]
    copies = []
    for a, (src, land) in enumerate(zip(src_refs, land_refs)):
        for j, (px, py) in enumerate(_other_chips(x, y)):
            if kind == "whole":
                src_ref, dst_ref = src, land.at[2 * x + y]
            elif kind == "half":
                half = src.shape[1] // 2
                src_ref, dst_ref = src.at[:, pl.ds(pl.multiple_of(c * half, half), half)], land.at[c, 2 * x + y]
            else:
                src_ref, dst_ref = src.at[2 * px + py], land.at[j]
            copies.append(pltpu.make_async_remote_copy(
                src_ref=src_ref, dst_ref=dst_ref, send_sem=send_sems.at[3 * a + j], recv_sem=recv_sems.at[3 * a + j],
                device_id=(px, py, c), device_id_type=MESH))
    return copies


def _exchange_start(srcs, *, kind, name):
    n = len(srcs)
    lands = [_landing_zone(s, kind) for s in srcs]

    def body(*refs):
        for cp in _chip_copies(refs[:n], refs[n:2 * n], refs[2 * n], refs[2 * n + 1], kind):
            cp.start()
        refs[-1][...] = jnp.zeros_like(refs[-1])

    res = pl.pallas_call(
        body, name=name,
        out_shape=(pltpu.SemaphoreType.DMA((_copies_per_source(kind) * n,)),
                   pltpu.SemaphoreType.DMA((_copies_per_source(kind) * n,)),
                   *[pltpu.HBM(a.shape, a.dtype) for a in srcs + lands], jax.ShapeDtypeStruct((8, 128), f32)),
        in_specs=[_HBM] * (2 * n),
        out_specs=(_SEM, _SEM, *[_HBM] * (2 * n), pl.BlockSpec(memory_space=pltpu.VMEM)),
        input_output_aliases={i: 2 + i for i in range(2 * n)},
        compiler_params=pltpu.CompilerParams(has_side_effects=_EFFECT),
    )(*[pltpu.with_memory_space_constraint(a, pltpu.HBM) for a in srcs + lands])
    return res[0], res[1], list(res[2:2 + n]), list(res[2 + n:2 + 2 * n]), res[-1]


def _exchange_wait(handle, after, *, kind, name):
    send_sems, recv_sems, srcs, lands, _ = handle
    n = len(srcs)

    def body(*refs):
        for cp in _chip_copies(refs[:n], refs[n:2 * n], refs[2 * n], refs[2 * n + 1], kind):
            cp.wait_send()
            cp.wait_recv()

    res = pl.pallas_call(
        body, name=name,
        out_shape=tuple(pltpu.HBM(a.shape, a.dtype) for a in srcs + lands),
        in_specs=[_HBM] * (2 * n) + [_SEM, _SEM, pl.BlockSpec(memory_space=pl.ANY)],
        out_specs=tuple([_HBM] * (2 * n)),
        input_output_aliases={i: i for i in range(2 * n)},
        compiler_params=pltpu.CompilerParams(has_side_effects=_EFFECT),
    )(*srcs, *lands, send_sems, recv_sems, after)
    return list(res[:n]), list(res[n:])


def _sum_own_and_received(g, recv, *, name):
    _, r, w = g.shape
    tm = _tile(r)
    if g.dtype == bf16 and tm % 16:
        tm = r
    x, y, _ = _position()
    me = jnp.reshape(2 * x + y, (1,)).astype(jnp.int32)

    def body(me_ref, g_ref, r_ref, o_ref):
        o_ref[...] = (g_ref[0].astype(f32) + r_ref[0].astype(f32)) + (r_ref[1].astype(f32) + r_ref[2].astype(f32))

    return pl.pallas_call(
        body, name=name,
        grid_spec=pltpu.PrefetchScalarGridSpec(
            num_scalar_prefetch=1, grid=(r // tm,),
            in_specs=[pl.BlockSpec((1, tm, w), lambda i, me_ref: (me_ref[0], i, 0)),
                      pl.BlockSpec((3, tm, w), lambda i, me_ref: (0, i, 0))],
            out_specs=pl.BlockSpec((tm, w), lambda i, me_ref: (i, 0))),
        out_shape=jax.ShapeDtypeStruct((r, w), f32),
        compiler_params=_params(("parallel",)),
    )(me, g, recv)


def _swap_cores(arrs, *, name):
    n = len(arrs)

    def body(*refs):
        x, y, c = _position()
        copies = [pltpu.make_async_remote_copy(
            src_ref=refs[i], dst_ref=refs[n + i], send_sem=refs[2 * n].at[i], recv_sem=refs[2 * n + 1].at[i],
            device_id=(x, y, 1 - c), device_id_type=MESH) for i in range(n)]
        for cp in copies:
            cp.start()
        for cp in copies:
            cp.wait_recv()
        for cp in copies:
            cp.wait_send()

    return pl.pallas_call(
        body, name=name,
        in_specs=[pl.BlockSpec(memory_space=pl.ANY)] * n,
        out_specs=[pl.BlockSpec(memory_space=pl.ANY)] * n,
        out_shape=[jax.ShapeDtypeStruct(a.shape, a.dtype) for a in arrs],
        scratch_shapes=[pltpu.SemaphoreType.DMA((n,)), pltpu.SemaphoreType.DMA((n,))],
    )(*arrs)


def _swap_halves(zone, *, name):
    def body(z_ref, o_ref, send_sems, recv_sems):
        x, y, c = _position()
        mine = [pltpu.make_async_remote_copy(
            src_ref=o_ref.at[c, 2 * px + py], dst_ref=o_ref.at[c, 2 * px + py], send_sem=send_sems.at[j],
            recv_sem=recv_sems.at[j], device_id=(x, y, 1 - c), device_id_type=MESH)
            for j, (px, py) in enumerate(_other_chips(x, y))]
        for cp in mine:
            cp.start()
        for j, (px, py) in enumerate(_other_chips(x, y)):
            pltpu.make_async_remote_copy(
                src_ref=o_ref.at[c, 2 * px + py], dst_ref=o_ref.at[1 - c, 2 * px + py], send_sem=send_sems.at[j],
                recv_sem=recv_sems.at[j], device_id=(x, y, 1 - c), device_id_type=MESH).wait_recv()
        for cp in mine:
            cp.wait_send()

    return pl.pallas_call(
        body, name=name,
        in_specs=[pl.BlockSpec(memory_space=pl.ANY)], out_specs=pl.BlockSpec(memory_space=pl.ANY),
        out_shape=jax.ShapeDtypeStruct(zone.shape, zone.dtype), input_output_aliases={0: 0},
        scratch_shapes=[pltpu.SemaphoreType.DMA((3,)), pltpu.SemaphoreType.DMA((3,))],
    )(zone)


def _all_reduce_small(a, after, *, name):
    rows, w = a.shape

    def body(a_ref, after_ref, o_ref, buf, send_sems, recv_sems):
        x, y, c = _position()
        me = 4 * x + 2 * y + c
        buf[0] = a_ref[...]
        sends = []
        for rel in range(1, N_DEV):
            peer = ((1 - x) if rel & 4 else x, (1 - y) if rel & 2 else y, (1 - c) if rel & 1 else c)
            cp = pltpu.make_async_remote_copy(
                src_ref=a_ref, dst_ref=buf.at[rel], send_sem=send_sems.at[rel - 1], recv_sem=recv_sems.at[rel - 1],
                device_id=peer, device_id_type=MESH)
            cp.start()
            sends.append(cp)
        for cp in sends:
            cp.wait_recv()
        for cp in sends:
            cp.wait_send()
        acc = buf[jnp.bitwise_xor(me, 0)]
        for d in range(1, N_DEV):
            acc = acc + buf[jnp.bitwise_xor(me, d)]
        o_ref[...] = acc

    return pl.pallas_call(
        body, name=name,
        in_specs=[pl.BlockSpec(memory_space=pltpu.VMEM), pl.BlockSpec(memory_space=pl.ANY)],
        out_specs=pl.BlockSpec(memory_space=pltpu.VMEM),
        out_shape=jax.ShapeDtypeStruct((rows, w), f32),
        scratch_shapes=[pltpu.VMEM((N_DEV, rows, w), f32), pltpu.SemaphoreType.DMA((N_DEV - 1,)),
                        pltpu.SemaphoreType.DMA((N_DEV - 1,))],
    )(a, after)


def _adamw(w, g_parts, m, v, *, name, transposed=False):
    rows, cols = w.shape
    if transposed:
        tm = 256 if rows % 256 == 0 else rows
        g_spec = pl.BlockSpec((cols, tm), lambda i: (0, i))
    else:
        tm = _tile(rows, 256)
        g_spec = pl.BlockSpec((tm, cols), lambda i: (i, 0))
    n = len(g_parts)

    def body(*refs):
        w_ref, m_ref, v_ref = refs[0], refs[1 + n], refs[2 + n]
        g_ref, d_ref, nm_ref, nv_ref = refs[3 + n:]
        gv = refs[1][...]
        for part in refs[2:1 + n]:
            gv = gv + part[...]
        if transposed:
            gv = gv.T
        g_ref[...] = gv
        nm = ADAM_B1 * m_ref[...] + (1.0 - ADAM_B1) * gv
        nv = ADAM_B2 * v_ref[...] + (1.0 - ADAM_B2) * (gv * gv)
        m_hat = nm / (1.0 - ADAM_B1 ** ADAM_STEP)
        v_hat = nv / (1.0 - ADAM_B2 ** ADAM_STEP)
        d_ref[...] = -ADAM_LR * (m_hat / (jnp.sqrt(v_hat) + ADAM_EPS) + ADAM_WD * w_ref[...])
        nm_ref[...] = nm
        nv_ref[...] = nv

    spec = pl.BlockSpec((tm, cols), lambda i: (i, 0))
    shape = jax.ShapeDtypeStruct((rows, cols), f32)
    return pl.pallas_call(
        body, name=name, grid=(rows // tm,), in_specs=[spec] + [g_spec] * n + [spec] * 2,
        out_specs=[spec] * 4, out_shape=[shape] * 4,
        compiler_params=_params(("parallel",)),
    )(w, *g_parts, m, v)


def _pad_rows(a, rows):
    return jnp.concatenate([a, jnp.zeros((rows - a.shape[0], a.shape[1]), a.dtype)], axis=0) if rows > a.shape[0] else a


_SMALL = (("norm_mix_g", D_MODEL), ("b_in", D_IN), ("sinks", Q_HEADS), ("mix", RWKV_PROJ), ("w0", RWKV_DIM),
          ("a0", RWKV_DIM), ("k_k", RWKV_DIM), ("k_a", RWKV_DIM), ("r_k", RWKV_DIM), ("ln_w", RWKV_DIM),
          ("ln_b", RWKV_DIM), ("norm_ffn_g", D_MODEL), ("norm_final_g", D_MODEL))


def _pack_small(d):
    flat = jnp.concatenate([d[n].reshape(-1).astype(f32) for n, _ in _SMALL])
    return flat


def _unpack_small(flat):
    out, off = {}, 0
    for n, size in _SMALL:
        out[n] = flat[off:off + size]
        off += size
    return out


_SMALL_TOTAL = sum(s for _, s in _SMALL)


def kernel(x, meta_tokens, norm_mix_g, w_in, b_in, attn_sinks, rwkv_mix, rwkv_w0, rwkv_w2, rwkv_a0, rwkv_a2, rwkv_g2, rwkv_k_k, rwkv_k_a, rwkv_r_k, rwkv_ln_w, rwkv_ln_b, w_br_attn, w_br_rwkv, w_o, norm_ffn_g, w_ffn_gate, w_ffn_up, w_ffn_down, norm_final_g, loss_target, m_meta_tokens, m_norm_mix_g, m_w_in, m_b_in, m_attn_sinks, m_rwkv_mix, m_rwkv_w0, m_rwkv_w2, m_rwkv_a0, m_rwkv_a2, m_rwkv_g2, m_rwkv_k_k, m_rwkv_k_a, m_rwkv_r_k, m_rwkv_ln_w, m_rwkv_ln_b, m_w_br_attn, m_w_br_rwkv, m_w_o, m_norm_ffn_g, m_w_ffn_gate, m_w_ffn_up, m_w_ffn_down, m_norm_final_g, v_meta_tokens, v_norm_mix_g, v_w_in, v_b_in, v_attn_sinks, v_rwkv_mix, v_rwkv_w0, v_rwkv_w2, v_rwkv_a0, v_rwkv_a2, v_rwkv_g2, v_rwkv_k_k, v_rwkv_k_a, v_rwkv_r_k, v_rwkv_ln_w, v_rwkv_ln_b, v_w_br_attn, v_w_br_rwkv, v_w_o, v_norm_ffn_g, v_w_ffn_gate, v_w_ffn_up, v_w_ffn_down, v_norm_final_g):
    names = ("meta_tokens", "norm_mix_g", "w_in", "b_in", "attn_sinks", "rwkv_mix", "rwkv_w0", "rwkv_w2", "rwkv_a0",
             "rwkv_a2", "rwkv_g2", "rwkv_k_k", "rwkv_k_a", "rwkv_r_k", "rwkv_ln_w", "rwkv_ln_b", "w_br_attn",
             "w_br_rwkv", "w_o", "norm_ffn_g", "w_ffn_gate", "w_ffn_up", "w_ffn_down", "norm_final_g")
    w_all = dict(zip(names, (meta_tokens, norm_mix_g, w_in, b_in, attn_sinks, rwkv_mix, rwkv_w0, rwkv_w2, rwkv_a0,
                             rwkv_a2, rwkv_g2, rwkv_k_k, rwkv_k_a, rwkv_r_k, rwkv_ln_w, rwkv_ln_b, w_br_attn,
                             w_br_rwkv, w_o, norm_ffn_g, w_ffn_gate, w_ffn_up, w_ffn_down, norm_final_g)))
    m_all = dict(zip(names, (m_meta_tokens, m_norm_mix_g, m_w_in, m_b_in, m_attn_sinks, m_rwkv_mix, m_rwkv_w0,
                             m_rwkv_w2, m_rwkv_a0, m_rwkv_a2, m_rwkv_g2, m_rwkv_k_k, m_rwkv_k_a, m_rwkv_r_k,
                             m_rwkv_ln_w, m_rwkv_ln_b, m_w_br_attn, m_w_br_rwkv, m_w_o, m_norm_ffn_g, m_w_ffn_gate,
                             m_w_ffn_up, m_w_ffn_down, m_norm_final_g)))
    v_all = dict(zip(names, (v_meta_tokens, v_norm_mix_g, v_w_in, v_b_in, v_attn_sinks, v_rwkv_mix, v_rwkv_w0,
                             v_rwkv_w2, v_rwkv_a0, v_rwkv_a2, v_rwkv_g2, v_rwkv_k_k, v_rwkv_k_a, v_rwkv_r_k,
                             v_rwkv_ln_w, v_rwkv_ln_b, v_w_br_attn, v_w_br_rwkv, v_w_o, v_norm_ffn_g, v_w_ffn_gate,
                             v_w_ffn_up, v_w_ffn_down, v_norm_final_g)))
    cx, cy, _ = _position()
    chip = 2 * cx + cy

    t_of = dict(w_in_t="w_in", w_gate_t="w_ffn_gate", w_up_t="w_ffn_up", w_br_attn_t="w_br_attn",
                w_br_rwkv_t="w_br_rwkv", g2_t="rwkv_g2", w2_t="rwkv_w2", a2_t="rwkv_a2")
    plain_of = dict(w_down="w_ffn_down", w_o="w_o")
    meta_cols = meta_tokens.shape[1]

    def shard(k):
        return (w_all[t_of[k]][0].T if k in t_of else w_all[plain_of[k]][0]).astype(bf16)

    def whole(zone, own):
        return lax.dynamic_update_slice_in_dim(zone, own[None], chip, axis=0).reshape(-1, own.shape[-1])

    tiny = ("g2_t", "w2_t", "a2_t")
    late = ("w_gate_t", "w_up_t", "w_down", "w_o", "w_br_attn_t", "w_br_rwkv_t")
    w_in_own = shard("w_in_t")
    w_in_rows, w_in_cols = w_in_own.shape
    tiny_h = _exchange_start([shard(k) for k in tiny] + [meta_tokens], kind="whole", name="gather_tiny_start")
    w_in_h = _exchange_start([w_in_own + tiny_h[4][0, 0].astype(bf16)], kind="half", name="gather_w_in_start")
    behind = w_in_h[4][0, 0].astype(bf16)
    late_h = _exchange_start([shard(k) + behind for k in late], kind="whole", name="gather_late_start")
    own, zones = _exchange_wait(tiny_h, late_h[4], kind="whole", name="gather_tiny_wait")
    got = {k: whole(z, o) for k, z, o in zip(tiny, zones, own)}
    meta_full = whole(zones[-1], own[-1]).reshape(N_CHIPS, N_META, meta_cols).transpose(1, 0, 2).reshape(N_META, -1)
    p = dict(
        g2=got["g2_t"].T.astype(f32), w2=got["w2_t"].T.astype(f32), a2=got["a2_t"].T.astype(f32),
        b_in=b_in, sinks=attn_sinks, mix=rwkv_mix, w0=rwkv_w0, a0=rwkv_a0, k_k=rwkv_k_k, k_a=rwkv_k_a,
        r_k=rwkv_r_k.reshape(1, RWKV_DIM), ln_w=rwkv_ln_w, ln_b=rwkv_ln_b, norm_mix_g=norm_mix_g,
        norm_ffn_g=norm_ffn_g, norm_final_g=norm_final_g.reshape(1, D_MODEL),
    )

    def early_weights(after):
        own_h, zones_h = _exchange_wait(w_in_h, after, kind="half", name="gather_w_in_wait")
        zone = _swap_halves(zones_h[0], name="swap_w_in_halves")
        own_halves = own_h[0].reshape(w_in_rows, 2, w_in_cols // 2).transpose(1, 0, 2)[:, None]
        zone = lax.dynamic_update_slice(zone, own_halves, (0, chip, 0, 0))
        return dict(w_in_lr=zone.reshape(2, N_CHIPS * w_in_rows, w_in_cols // 2))

    def late_weights(after):
        own_l, zones_l = _exchange_wait(late_h, after, kind="whole", name="gather_late_wait")
        return {k: whole(z, o) for k, z, o in zip(late, zones_l, own_l)}

    started = {}

    def emit(group, grads_):
        keys = list(grads_)
        slabs = []
        for k in keys:
            a = grads_[k].T if k in ("g2", "w2", "a2") else grads_[k]
            slabs.append(a.reshape(N_CHIPS, a.shape[0] // N_CHIPS, a.shape[1]))
        started[group] = (keys, _exchange_start(slabs, kind="slab", name="scatter_" + group + "_start"))
        return started[group][1][4][0, 0]

    loss, dx, g = _local_step(x[0], loss_target[0], meta_full, p, early_weights, late_weights, emit)

    grads, delta, new_m, new_v = {}, {}, {}, {}
    in_grad_layout = ("w_in_t", "w_gate_t", "w_up_t")
    weight_of = {**t_of, **plain_of}

    def partial_sums(groups, after):
        parts = {}
        for group in groups:
            keys, handle = started[group]
            slabs, lands = _exchange_wait(handle, after, kind="slab", name="scatter_" + group + "_wait")
            parts.update({k: _sum_own_and_received(s, l, name="sum_chips_" + k) for k, s, l in zip(keys, slabs, lands)})
        return parts

    def update(keys, mine, theirs):
        for k, part, other in zip(keys, mine, theirs):
            both = [part, other]
            k = k + "_t" if k in ("g2", "w2", "a2") else k
            n = weight_of[k]
            shape2 = w_all[n].shape[1:]
            w_, m_, v_ = (a.reshape(shape2) for a in (w_all[n], m_all[n], v_all[n]))
            if k in in_grad_layout:
                res = [t.T for t in _adamw(w_.T, both, m_.T, v_.T, name="adamw_" + n)]
            else:
                res = _adamw(w_, both, m_, v_, name="adamw_" + n, transposed=k in t_of)
            grads[n], delta[n], new_m[n], new_v[n] = (t.reshape(w_all[n].shape) for t in res)
        return delta[n]

    parts_a = partial_sums(("ffn", "branch"), dx)
    swap_a = _exchange_start(list(parts_a.values()), kind="sibling", name="swap_cores_a_start")
    small = jnp.concatenate([_pack_small(g), loss.reshape(1)])
    small_rows = -(-small.shape[0] // PACK_W)
    small = jnp.concatenate([small, jnp.zeros((small_rows * PACK_W - small.shape[0],), f32)]).reshape(small_rows, PACK_W)
    small_rows8 = -(-(small_rows + N_META) // 8) * 8
    reduced = _all_reduce_small(_pad_rows(jnp.concatenate([g["meta"], small], axis=0), small_rows8), swap_a[4],
                                name="reduce_small")
    done = update(list(parts_a), *_exchange_wait(swap_a, reduced, kind="sibling", name="swap_cores_a_wait"))
    parts_b = partial_sums(("input",), done)
    update(list(parts_b), list(parts_b.values()), _swap_cores(list(parts_b.values()), name="swap_cores_b"))
    g_meta = lax.dynamic_slice_in_dim(reduced[:N_META], chip * meta_cols, meta_cols, axis=1)
    flat = reduced[N_META:N_META + small_rows].reshape(-1)
    g_small = _unpack_small(flat)
    loss_total = flat[_SMALL_TOTAL]

    small_of = dict(norm_mix_g="norm_mix_g", b_in="b_in", attn_sinks="sinks", rwkv_mix="mix", rwkv_w0="w0",
                    rwkv_a0="a0", rwkv_k_k="k_k", rwkv_k_a="k_a", rwkv_r_k="r_k", rwkv_ln_w="ln_w",
                    rwkv_ln_b="ln_b", norm_ffn_g="norm_ffn_g", norm_final_g="norm_final_g")
    grads["meta_tokens"] = g_meta
    for n, k in small_of.items():
        grads[n] = g_small[k].reshape(w_all[n].shape)

    rest = [n for n in names if n not in delta]

    def pack_rest(src):
        flat_ = jnp.concatenate([src[n].reshape(-1) for n in rest])
        rows_ = -(-flat_.shape[0] // (8 * PACK_W)) * 8
        return jnp.concatenate([flat_, jnp.ones((rows_ * PACK_W - flat_.shape[0],), f32)]).reshape(rows_, PACK_W)

    _, d_, m_, v_ = _adamw(pack_rest(w_all), [pack_rest(grads)], pack_rest(m_all), pack_rest(v_all),
                           name="adamw_small")
    off = 0
    for n in rest:
        size = w_all[n].size
        for dst, src in ((delta, d_), (new_m, m_), (new_v, v_)):
            dst[n] = src.reshape(-1)[off:off + size].reshape(w_all[n].shape)
        off += size

    return (loss_total, dx.reshape(x.shape), *[grads[n] for n in names], *[delta[n] for n in names],
            *[new_m[n] for n in names], *[new_v[n] for n in names])
```

```python
import math

import jax
import jax.numpy as jnp
from jax import lax
from jax.experimental import pallas as pl
from jax.experimental.pallas import tpu as pltpu

f32 = jnp.float32
bf16 = jnp.bfloat16

D_MODEL = 1024
N_META = 16
HEAD_DIM = 64
Q_HEADS = 8
KV_HEADS = 2
GROUP = Q_HEADS // KV_HEADS
WINDOW = 128
BLOCK = 128
ROPE_THETA = 500000.0
ROPE_DIM = HEAD_DIM // 4
RWKV_HEADS = 8
RWKV_HEAD = 64
RWKV_DIM = RWKV_HEADS * RWKV_HEAD
DECAY_LORA = 64
AAA_LORA = 64
GATE_LORA = 160
LORA_W = DECAY_LORA + AAA_LORA + GATE_LORA
RWKV_LN_EPS = 64e-5
D_FF = 2816
Q_W = Q_HEADS * HEAD_DIM
KV_W = KV_HEADS * HEAD_DIM
ATTN_PROJ = Q_W + 2 * KV_W
RKV_W = 3 * RWKV_DIM
RWKV_PROJ = RKV_W + LORA_W
D_IN = ATTN_PROJ + RWKV_PROJ + 2 * D_MODEL
RMS_EPS = 1e-6
NEG_INF = -1e30
PAD = BLOCK - N_META
FRONT = PAD + N_META

ADAM_LR = 0.001
ADAM_B1 = 0.9
ADAM_B2 = 0.999
ADAM_EPS = 1e-08
ADAM_WD = 0.01
ADAM_STEP = 10

N_CHIPS = 4
N_DEV = 8
CHUNK = 128
VMEM_LIMIT = 56 * 1024 * 1024
PACK_W = 1024
MESH = pl.DeviceIdType.MESH


def _tile(m, pref=384):
    for step in (16, 8):
        for t in range(min(m, pref) // step * step, 0, -step):
            if m % t == 0:
                return t
    return m


def _params(sem=None):
    return pltpu.CompilerParams(dimension_semantics=sem, vmem_limit_bytes=VMEM_LIMIT)


def _full(shape):
    nd = len(shape)
    return pl.BlockSpec(shape, lambda *_: (0,) * nd)


def _dot(a, b, dims="nn"):
    dn = {"nn": (((1,), (0,)), ((), ())), "nt": (((1,), (1,)), ((), ())), "tn": (((0,), (0,)), ((), ()))}[dims]
    return lax.dot_general(a.astype(bf16), b.astype(bf16), dn, preferred_element_type=f32)


def _two_pass(x, m, dims="nn"):
    x_hi = x.astype(bf16)
    x_lo = (x - x_hi.astype(f32)).astype(bf16)
    return _dot(x_hi, m, dims) + _dot(x_lo, m, dims)


@jax.custom_vjp
def _dot_const(x, m):
    return _two_pass(x, m)


def _dot_const_fwd(x, m):
    return _two_pass(x, m), m


def _dot_const_bwd(m, ct):
    return _two_pass(ct, m, "nt"), jnp.zeros_like(m)


_dot_const.defvjp(_dot_const_fwd, _dot_const_bwd)


def _two_pass_left(m, x, dims):
    x_hi = x.astype(bf16)
    x_lo = (x - x_hi.astype(f32)).astype(bf16)
    return _dot(m, x_hi, dims) + _dot(m, x_lo, dims)


@jax.custom_vjp
def _const_dot(m, x):
    return _two_pass_left(m, x, "nn")


def _const_dot_fwd(m, x):
    return _two_pass_left(m, x, "nn"), m


def _const_dot_bwd(m, ct):
    return jnp.zeros_like(m), _two_pass_left(m, ct, "tn")


_const_dot.defvjp(_const_dot_fwd, _const_dot_bwd)


def _mm(a, b, mode, *, name, out_dtype=f32, bias=None, add=None, zero_rows_below=0):
    m, _ = a.shape
    n = b.shape[1] if mode == "nn" else b.shape[0]
    tm = _tile(m)
    has_bias, has_add = bias is not None, add is not None

    def body(*refs):
        a_ref, b_ref = refs[0], refs[1]
        o_ref = refs[-1]
        acc = _dot(a_ref[...], b_ref[...], mode)
        k = 2
        if has_bias:
            acc = acc + refs[k][...]
            k += 1
        if zero_rows_below:
            rows = pl.program_id(0) * tm + lax.broadcasted_iota(jnp.int32, acc.shape, 0)
            acc = jnp.where(rows >= zero_rows_below, acc, 0.0)
        if has_add:
            acc = acc + refs[k][...].astype(f32)
        o_ref[...] = acc.astype(out_dtype)

    ins = [a, b]
    in_specs = [pl.BlockSpec((tm, a.shape[1]), lambda i: (i, 0)), _full(b.shape)]
    if has_bias:
        ins.append(bias)
        in_specs.append(_full(bias.shape))
    if has_add:
        ins.append(add)
        in_specs.append(pl.BlockSpec((tm, n), lambda i: (i, 0)))
    return pl.pallas_call(
        body, name=name, grid=(m // tm,), in_specs=in_specs,
        out_specs=pl.BlockSpec((tm, n), lambda i: (i, 0)),
        out_shape=jax.ShapeDtypeStruct((m, n), out_dtype),
        compiler_params=_params(("parallel",)),
    )(*ins)


def _pieces(widths):
    out, off = [], 0
    for w in widths:
        out.append((off, w))
        off += w
    return out


def _proj_in(a, w_lr, bias, widths, *, name, zero_rows_below=0):
    m, kdim = a.shape
    half = kdim // 2
    tm = _tile(m)

    def body(a_ref, w_ref, b_ref, *outs):
        a_l, a_r = a_ref[:, :half], a_ref[:, half:]
        for (off, width), o_ref in zip(_pieces(widths), outs):
            acc = _dot(a_l, w_ref[0, off:off + width, :], "nt") + _dot(a_r, w_ref[1, off:off + width, :], "nt")
            acc = acc + b_ref[:, off:off + width]
            if zero_rows_below:
                rows = pl.program_id(0) * tm + lax.broadcasted_iota(jnp.int32, acc.shape, 0)
                acc = jnp.where(rows >= zero_rows_below, acc, 0.0)
            o_ref[...] = acc.astype(o_ref.dtype)

    return pl.pallas_call(
        body, name=name, grid=(m // tm,),
        in_specs=[pl.BlockSpec((tm, kdim), lambda i: (i, 0)), _full(w_lr.shape), _full(bias.shape)],
        out_specs=[pl.BlockSpec((tm, w), lambda i: (i, 0)) for w in widths],
        out_shape=[jax.ShapeDtypeStruct((m, w), bf16) for w in widths],
        compiler_params=_params(("parallel",)),
    )(a, w_lr, bias)


def _proj_in_bwd(d_list, w_lr, *, name):
    m = d_list[0].shape[0]
    half = w_lr.shape[2]
    widths = [d.shape[1] for d in d_list]
    tm = _tile(m)

    def body(*refs):
        w_ref, o_ref = refs[-2], refs[-1]
        for side in range(2):
            acc = None
            for (off, width), d_ref in zip(_pieces(widths), refs):
                term = _dot(d_ref[...], w_ref[side, off:off + width, :])
                acc = term if acc is None else acc + term
            o_ref[:, side * half:(side + 1) * half] = acc.astype(o_ref.dtype)

    return pl.pallas_call(
        body, name=name, grid=(m // tm,),
        in_specs=[pl.BlockSpec((tm, w), lambda i: (i, 0)) for w in widths] + [_full(w_lr.shape)],
        out_specs=pl.BlockSpec((tm, 2 * half), lambda i: (i, 0)),
        out_shape=jax.ShapeDtypeStruct((m, 2 * half), bf16),
        compiler_params=_params(("parallel",)),
    )(*d_list, w_lr)


def _residual_norm(a, w, res, g, *, name):
    m, d = res.shape
    tm = _tile(m)

    def body(a_ref, w_ref, r_ref, g_ref, h_ref, n_ref):
        h = _dot(a_ref[...], w_ref[...]) + r_ref[...]
        h_ref[...] = h
        n_ref[...] = _rms(h, g_ref[...]).astype(n_ref.dtype)

    tile = pl.BlockSpec((tm, d), lambda i: (i, 0))
    return pl.pallas_call(
        body, name=name, grid=(m // tm,),
        in_specs=[pl.BlockSpec((tm, a.shape[1]), lambda i: (i, 0)), _full(w.shape), tile, _full(g.shape)],
        out_specs=[tile, tile],
        out_shape=[jax.ShapeDtypeStruct((m, d), f32), jax.ShapeDtypeStruct((m, d), bf16)],
        compiler_params=_params(("parallel",)),
    )(a, w, res, g)


def _residual_norm_bwd(d_list, w_list, h, g, dh_out, *, name):
    m, d = h.shape
    k = len(d_list)
    tm = _tile(m)

    def body(*refs):
        h_ref, g_ref, dho_ref, dh_ref, dg_ref = refs[2 * k:]
        dn = _dot(refs[0][...], refs[k][...])
        for i in range(1, k):
            dn = dn + _dot(refs[i][...], refs[k + i][...])
        _, vjp = jax.vjp(lambda hv, gv: (_rms(hv, gv), hv), h_ref[...], g_ref[...])
        dh, dg = vjp((dn, dho_ref[...]))
        dh_ref[...] = dh

        @pl.when(pl.program_id(0) == 0)
        def _():
            dg_ref[...] = jnp.zeros_like(dg_ref)

        dg_ref[...] += dg

    tile = pl.BlockSpec((tm, d), lambda i: (i, 0))
    return pl.pallas_call(
        body, name=name, grid=(m // tm,),
        in_specs=[pl.BlockSpec((tm, a.shape[1]), lambda i: (i, 0)) for a in d_list] + [_full(w.shape) for w in w_list]
        + [tile, _full(g.shape), tile],
        out_specs=[tile, _full(g.shape)],
        out_shape=[jax.ShapeDtypeStruct((m, d), f32), jax.ShapeDtypeStruct(g.shape, f32)],
        compiler_params=_params(("arbitrary",)),
    )(*d_list, *w_list, h, g, dh_out)


def _mm_tn(a, b, *, name, colsum=False, out_dtype=bf16):
    r, m = a.shape
    n = b.shape[1]
    tr = _tile(r, 1408)
    tmo = m
    for cand in (1408, 1024, 768, 512):
        if m > 1024 and m % cand == 0:
            tmo = cand
            break
    steps = r // tr

    def body(a_ref, b_ref, o_ref, *rest):
        acc = rest[-1]
        i = pl.program_id(1)

        @pl.when(i == 0)
        def _():
            acc[...] = jnp.zeros_like(acc)
            if colsum:
                rest[0][...] = jnp.zeros_like(rest[0])

        acc[...] += _dot(a_ref[...], b_ref[...], "tn")
        if colsum:
            rest[0][...] += jnp.sum(a_ref[...].astype(f32), axis=0, keepdims=True)

        @pl.when(i == steps - 1)
        def _():
            o_ref[...] = acc[...].astype(out_dtype)

    out_shape = [jax.ShapeDtypeStruct((m, n), out_dtype)]
    out_specs = [pl.BlockSpec((tmo, n), lambda j, i: (j, 0))]
    if colsum:
        out_shape.append(jax.ShapeDtypeStruct((1, m), f32))
        out_specs.append(pl.BlockSpec((1, tmo), lambda j, i: (0, j)))
    res = pl.pallas_call(
        body, name=name, grid=(m // tmo, steps),
        in_specs=[pl.BlockSpec((tr, tmo), lambda j, i: (i, j)), pl.BlockSpec((tr, n), lambda j, i: (i, 0))],
        out_specs=out_specs, out_shape=out_shape,
        scratch_shapes=[pltpu.VMEM((tmo, n), f32)],
        compiler_params=_params(("parallel", "arbitrary")),
    )(a, b)
    return res if colsum else res[0]


def _rowwise(fn, rows, params, outs, *, name, tm=None):
    m = rows[0].shape[0]
    tm = tm or _tile(m)
    nr, npar = len(rows), len(params)

    def body(*refs):
        vals = [r[...] for r in refs[:nr + npar]]
        res = fn(*vals)
        for o_ref, v in zip(refs[nr + npar:], res):
            o_ref[...] = v.astype(o_ref.dtype)

    return pl.pallas_call(
        body, name=name, grid=(m // tm,),
        in_specs=[pl.BlockSpec((tm, r.shape[1]), lambda i: (i, 0)) for r in rows] + [_full(p.shape) for p in params],
        out_specs=[pl.BlockSpec((tm, w), lambda i: (i, 0)) for w, _ in outs],
        out_shape=[jax.ShapeDtypeStruct((m, w), dt) for w, dt in outs],
        compiler_params=_params(("parallel",)),
    )(*rows, *params)


def _rowwise_bwd(fn, rows, params, cts, *, name, diff_rows, diff_params, tm=None, zero_rows_below=0, out_dtypes=None):
    m = rows[0].shape[0]
    tm = tm or _tile(m)
    nr, npar = len(rows), len(params)
    d_idx = [i for i in range(nr) if diff_rows[i]]
    p_idx = [i for i in range(npar) if diff_params[i]]
    out_dtypes = out_dtypes or [f32] * len(d_idx)
    flat_cts = [c for group in cts for c in group]
    n_ct = len(flat_cts)

    def body(*refs):
        vals = [r[...] for r in refs[:nr + npar]]
        ct_refs = refs[nr + npar:nr + npar + n_ct]
        out_refs = refs[nr + npar + n_ct:]
        ct_vals, k = [], 0
        for group in cts:
            acc = ct_refs[k][...].astype(f32)
            for extra in range(1, len(group)):
                acc = acc + ct_refs[k + extra][...].astype(f32)
            k += len(group)
            if zero_rows_below:
                rr = pl.program_id(0) * tm + lax.broadcasted_iota(jnp.int32, acc.shape, 0)
                acc = jnp.where(rr >= zero_rows_below, acc, 0.0)
            ct_vals.append(acc)

        def g(*dargs):
            full = list(vals)
            for pos, i in enumerate(d_idx):
                full[i] = dargs[pos]
            for pos, i in enumerate(p_idx):
                full[nr + i] = dargs[len(d_idx) + pos]
            return tuple(fn(*full))

        _, vjp = jax.vjp(g, *[vals[i].astype(f32) for i in d_idx], *[vals[nr + i] for i in p_idx])
        grads = vjp(tuple(ct_vals))
        for pos in range(len(d_idx)):
            out_refs[pos][...] = grads[pos].astype(out_refs[pos].dtype)
        first = pl.program_id(0) == 0
        for pos in range(len(p_idx)):
            o_ref = out_refs[len(d_idx) + pos]

            @pl.when(first)
            def _(o_ref=o_ref):
                o_ref[...] = jnp.zeros_like(o_ref)

            o_ref[...] += grads[len(d_idx) + pos]

    return pl.pallas_call(
        body, name=name, grid=(m // tm,),
        in_specs=[pl.BlockSpec((tm, r.shape[1]), lambda i: (i, 0)) for r in rows] + [_full(p.shape) for p in params]
        + [pl.BlockSpec((tm, c.shape[1]), lambda i: (i, 0)) for c in flat_cts],
        out_specs=[pl.BlockSpec((tm, rows[i].shape[1]), lambda i_: (i_, 0)) for i in d_idx]
        + [_full(params[i].shape) for i in p_idx],
        out_shape=[jax.ShapeDtypeStruct(rows[i].shape, dt) for i, dt in zip(d_idx, out_dtypes)]
        + [jax.ShapeDtypeStruct(params[i].shape, f32) for i in p_idx],
        compiler_params=_params(("arbitrary",)),
    )(*rows, *params, *flat_cts)


def _rms(x, g):
    return x * lax.rsqrt(jnp.mean(x * x, axis=-1, keepdims=True) + RMS_EPS) * g


def _head_sum_matrix(width, head):
    idx = jnp.arange(width) // head
    return (idx[:, None] == idx[None, :]).astype(f32)


def _rope_tables(lp):
    half = ROPE_DIM // 2
    pos = (jnp.arange(lp) - PAD).astype(f32)
    inv_freq = jnp.power(jnp.float32(ROPE_THETA), -jnp.arange(half, dtype=f32) * (2.0 / ROPE_DIM))
    ang = pos[:, None] * inv_freq[None, :]
    cos, sin = jnp.cos(ang), jnp.sin(ang)
    ones = jnp.ones((lp, HEAD_DIM - ROPE_DIM), f32)
    zeros = jnp.zeros((lp, HEAD_DIM - ROPE_DIM), f32)
    cos_t = jnp.concatenate([cos, cos, ones], axis=1)
    sin_t = jnp.concatenate([-sin, sin, zeros], axis=1)
    i = jnp.arange(HEAD_DIM)
    src = jnp.where(i < half, i + half, jnp.where(i < ROPE_DIM, i - half, i))
    swap = ((i[:, None] == src[None, :]) & (i[None, :] < ROPE_DIM)).astype(f32)
    return cos_t, sin_t, swap


def _attn_prep(qkv, cos_t, sin_t, swap):
    outs = []
    for h in range(Q_HEADS + KV_HEADS):
        t = qkv[:, h * HEAD_DIM:(h + 1) * HEAD_DIM]
        outs.append(t * cos_t + _dot_const(t, swap) * sin_t)
    q = jnp.concatenate(outs[:Q_HEADS], axis=1)
    k = jnp.concatenate(outs[Q_HEADS:], axis=1)
    return q, k, qkv[:, Q_W + KV_W:]


def _softplus(z):
    return jnp.maximum(z, 0.0) + jnp.log1p(jnp.exp(-jnp.abs(z)))


def _rwkv_prep(rkv, lora, w0, w2, a0, a2, g2, k_k, k_a, hsum):
    r = rkv[:, :RWKV_DIM]
    k = rkv[:, RWKV_DIM:2 * RWKV_DIM]
    v = rkv[:, 2 * RWKV_DIM:]
    dw = lora[:, :DECAY_LORA]
    da = lora[:, DECAY_LORA:DECAY_LORA + AAA_LORA]
    dg = lora[:, DECAY_LORA + AAA_LORA:]
    w = -_softplus(-(w0 + _dot(jnp.tanh(dw), w2))) - 0.5
    a = jax.nn.sigmoid(a0 + _dot(da, a2))
    g = _dot(jax.nn.sigmoid(dg), g2)
    kk = k * k_k
    kk = kk * lax.rsqrt(jnp.maximum(_dot_const(kk * kk, hsum), 1e-24))
    k = k * (1.0 + (a - 1.0) * k_a)
    log_decay = -jnp.exp(w)
    return r, log_decay, k, v, -kk, kk * a, g


def _rwkv_post(y, r, k, v, g, ln_w, ln_b, r_k, hmean):
    hsum = hmean * RWKV_HEAD
    mean = _dot_const(y, hmean)
    yc = y - mean
    var = _dot_const(yc * yc, hmean)
    yn = yc * lax.rsqrt(var + RWKV_LN_EPS) * ln_w + ln_b
    bonus = _dot_const(r * k * r_k, hsum) * v
    return ((yn + bonus) * g,)


def _merge(gates, br_a, br_r):
    sg = jax.nn.sigmoid(gates)
    return (sg[:, :D_MODEL] * br_a + sg[:, D_MODEL:] * br_r,)


def _swiglu(gate, up):
    return (jax.nn.silu(gate) * up,)


def _ffn_in(f, w_gate_t, w_up_t, *, name):
    m, d = f.shape
    n = w_gate_t.shape[0]
    tm = _tile(m)

    def body(f_ref, wg_ref, wu_ref, g_ref, u_ref, a_ref):
        g = _dot(f_ref[...], wg_ref[...], "nt")
        u = _dot(f_ref[...], wu_ref[...], "nt")
        g_ref[...] = g.astype(g_ref.dtype)
        u_ref[...] = u.astype(u_ref.dtype)
        a_ref[...] = _swiglu(g, u)[0].astype(a_ref.dtype)

    spec = pl.BlockSpec((tm, n), lambda i: (i, 0))
    return pl.pallas_call(
        body, name=name, grid=(m // tm,),
        in_specs=[pl.BlockSpec((tm, d), lambda i: (i, 0)), _full(w_gate_t.shape), _full(w_up_t.shape)],
        out_specs=[spec] * 3, out_shape=[jax.ShapeDtypeStruct((m, n), bf16)] * 3,
        compiler_params=_params(("parallel",)),
    )(f, w_gate_t, w_up_t)


def _branch_merge(y_attn, y_rwkv, w_attn_t, w_rwkv_t, gates, *, name):
    m = y_attn.shape[0]
    tm = _tile(m)

    def body(ya_ref, yr_ref, wa_ref, wr_ref, g_ref, a_ref, r_ref, o_ref):
        br_a = _dot(ya_ref[...], wa_ref[...], "nt")
        br_r = _dot(yr_ref[...], wr_ref[...], "nt")
        a_ref[...] = br_a.astype(a_ref.dtype)
        r_ref[...] = br_r.astype(r_ref.dtype)
        o_ref[...] = _merge(g_ref[...].astype(f32), br_a, br_r)[0].astype(o_ref.dtype)

    rows = lambda a: pl.BlockSpec((tm, a.shape[1]), lambda i: (i, 0))
    spec = pl.BlockSpec((tm, D_MODEL), lambda i: (i, 0))
    return pl.pallas_call(
        body, name=name, grid=(m // tm,),
        in_specs=[rows(y_attn), rows(y_rwkv), _full(w_attn_t.shape), _full(w_rwkv_t.shape), rows(gates)],
        out_specs=[spec] * 3, out_shape=[jax.ShapeDtypeStruct((m, D_MODEL), bf16)] * 3,
        compiler_params=_params(("parallel",)),
    )(y_attn, y_rwkv, w_attn_t, w_rwkv_t, gates)


def _branch_merge_bwd(dh, w_o, gates, br_a, br_r, *, name):
    m = dh.shape[0]
    tm = _tile(m)

    def body(dh_ref, w_ref, g_ref, a_ref, r_ref, dg_ref, da_ref, dr_ref):
        dmerged = _dot(dh_ref[...], w_ref[...], "nt")
        _, vjp = jax.vjp(lambda g, a, r: _merge(g, a, r)[0], g_ref[...].astype(f32), a_ref[...].astype(f32),
                         r_ref[...].astype(f32))
        dg, da, dr = vjp(dmerged)
        dg_ref[...] = dg.astype(dg_ref.dtype)
        da_ref[...] = da.astype(da_ref.dtype)
        dr_ref[...] = dr.astype(dr_ref.dtype)

    rows = lambda a: pl.BlockSpec((tm, a.shape[1]), lambda i: (i, 0))
    return pl.pallas_call(
        body, name=name, grid=(m // tm,),
        in_specs=[rows(dh), _full(w_o.shape), rows(gates), rows(br_a), rows(br_r)],
        out_specs=[rows(gates), rows(br_a), rows(br_r)],
        out_shape=[jax.ShapeDtypeStruct(gates.shape, bf16), jax.ShapeDtypeStruct(br_a.shape, bf16),
                   jax.ShapeDtypeStruct(br_r.shape, bf16)],
        compiler_params=_params(("parallel",)),
    )(dh, w_o, gates, br_a, br_r)


def _ffn_in_bwd(dh, w_down, gate, up, *, name):
    m, d = dh.shape
    n = w_down.shape[0]
    tm = _tile(m)

    def body(dh_ref, w_ref, g_ref, u_ref, dg_ref, du_ref):
        dact = _dot(dh_ref[...], w_ref[...], "nt")
        _, vjp = jax.vjp(lambda a, b: _swiglu(a, b)[0], g_ref[...].astype(f32), u_ref[...].astype(f32))
        dg, du = vjp(dact)
        dg_ref[...] = dg.astype(dg_ref.dtype)
        du_ref[...] = du.astype(du_ref.dtype)

    spec = pl.BlockSpec((tm, n), lambda i: (i, 0))
    return pl.pallas_call(
        body, name=name, grid=(m // tm,),
        in_specs=[pl.BlockSpec((tm, d), lambda i: (i, 0)), _full(w_down.shape), spec, spec],
        out_specs=[spec] * 2, out_shape=[jax.ShapeDtypeStruct((m, n), bf16)] * 2,
        compiler_params=_params(("parallel",)),
    )(dh, w_down, gate, up)


HALO = 16


def _previous_rows(x, before_ref, first_tile):
    rows = lax.broadcasted_iota(jnp.int32, x.shape, 0)
    last = jnp.where(first_tile, 0.0, before_ref[HALO - 1:HALO, :].astype(f32))
    return jnp.where(rows == 0, last, pltpu.roll(x, 1, axis=0))


def _mixer_inputs(ps, mixes, params, *, name):
    m = ps[0].shape[0]
    tm = _tile(m)
    sub = tm // HALO
    n_par = len(params)

    def body(*refs):
        first = pl.program_id(0) == 0
        pf = []
        for k in range(2):
            x = refs[k][...].astype(f32)
            pf.append(x + (_previous_rows(x, refs[2 + k], first) - x) * refs[4 + k][...])
        res = _rwkv_prep(*pf, *[ref[...] for ref in refs[6:6 + n_par]])
        for o_ref, val in zip(refs[6 + n_par:], res):
            o_ref[...] = val

    tile = lambda a: pl.BlockSpec((tm, a.shape[1]), lambda i: (i, 0))
    before = lambda a: pl.BlockSpec((HALO, a.shape[1]), lambda i: (jnp.maximum(i * sub - 1, 0), 0))
    out = pl.BlockSpec((tm, RWKV_DIM), lambda i: (i, 0))
    return pl.pallas_call(
        body, name=name, grid=(m // tm,),
        in_specs=[tile(a) for a in ps] + [before(a) for a in ps] + [_full(a.shape) for a in mixes + params],
        out_specs=[out] * 7, out_shape=[jax.ShapeDtypeStruct((m, RWKV_DIM), f32)] * 7,
        compiler_params=_params(("parallel",)),
    )(*ps, *ps, *mixes, *params)


def _mixer_inputs_bwd(ps, mixes, params, cts, *, name):
    m = ps[0].shape[0]
    tm = _tile(m)
    sub = tm // HALO
    nt = m // tm
    n_par = len(params)
    flat_cts = [c for group in cts for c in group]
    n_ct = len(flat_cts)

    def body(*refs):
        i = pl.program_id(0)
        tile_index = nt - 1 - i
        ct_refs = refs[6 + n_par:6 + n_par + n_ct]
        dp_refs = refs[6 + n_par + n_ct:8 + n_par + n_ct]
        dmix_refs = refs[8 + n_par + n_ct:10 + n_par + n_ct]
        dpar_refs = refs[10 + n_par + n_ct:9 + 2 * n_par + n_ct]
        carries = refs[9 + 2 * n_par + n_ct:]
        rows1 = tile_index * tm + lax.broadcasted_iota(jnp.int32, (tm, 1), 0)
        live = rows1 >= PAD

        @pl.when(i == 0)
        def _():
            for ref in (*dmix_refs, *dpar_refs, *carries):
                ref[...] = jnp.zeros_like(ref)

        xs, prevs, pf = [], [], []
        for k in range(2):
            x = refs[k][...].astype(f32)
            xp = _previous_rows(x, refs[2 + k], tile_index == 0)
            xs.append(x)
            prevs.append(xp)
            pf.append(x + (xp - x) * refs[4 + k][...])
        ct_vals, pos = [], 0
        for group in cts:
            acc = ct_refs[pos][...].astype(f32)
            for extra in range(1, len(group)):
                acc = acc + ct_refs[pos + extra][...].astype(f32)
            pos += len(group)
            ct_vals.append(jnp.where(live, acc, 0.0))
        par_vals = [ref[...] for ref in refs[6:6 + n_par]]
        _, vjp = jax.vjp(lambda *args: _rwkv_prep(*args, par_vals[-1]), *pf, *par_vals[:-1])
        g = vjp(tuple(ct_vals))
        for k in range(2):
            dpf = g[k]
            mixv = refs[4 + k][...]
            dm = dpf * mixv
            rows = lax.broadcasted_iota(jnp.int32, dm.shape, 0)
            dm_next = jnp.where(rows == tm - 1, carries[k][...], pltpu.roll(dm, tm - 1, axis=0))
            dp_refs[k][...] = jnp.where(live, dpf - dm + dm_next, 0.0).astype(dp_refs[k].dtype)
            carries[k][...] = dm[0:1, :]
            dmix_refs[k][...] += jnp.sum(dpf * (prevs[k] - xs[k]), axis=0, keepdims=True)
        for ref, val in zip(dpar_refs, g[2:]):
            ref[...] += val

    tile = lambda a: pl.BlockSpec((tm, a.shape[1]), lambda i: (nt - 1 - i, 0))
    before = lambda a: pl.BlockSpec((HALO, a.shape[1]), lambda i: (jnp.maximum((nt - 1 - i) * sub - 1, 0), 0))
    return pl.pallas_call(
        body, name=name, grid=(nt,),
        in_specs=[tile(a) for a in ps] + [before(a) for a in ps] + [_full(a.shape) for a in mixes + params]
        + [tile(c) for c in flat_cts],
        out_specs=[tile(a) for a in ps] + [_full(a.shape) for a in mixes + params[:-1]],
        out_shape=[jax.ShapeDtypeStruct(a.shape, bf16) for a in ps]
        + [jax.ShapeDtypeStruct(a.shape, f32) for a in mixes + params[:-1]],
        scratch_shapes=[pltpu.VMEM((1, a.shape[1]), f32) for a in ps],
        compiler_params=_params(("arbitrary",)),
    )(*ps, *ps, *mixes, *params, *flat_cts)


def _attn_masks(blk):
    qi = lax.broadcasted_iota(jnp.int32, (BLOCK, BLOCK), 0)
    ki = lax.broadcasted_iota(jnp.int32, (BLOCK, BLOCK), 1)
    qpos = blk * BLOCK + qi - PAD
    kpos_c = blk * BLOCK + ki - PAD
    kpos_p = kpos_c - BLOCK
    kpos_m = ki - PAD

    def band(kpos):
        return (kpos >= N_META) & (kpos <= qpos) & (qpos - kpos < WINDOW)

    return band(kpos_p), band(kpos_c), (kpos_m >= 0) & (kpos_m <= qpos)


def _attn_probs(qs, k3s, sink, oks):
    s = [[jnp.where(ok, _dot(qh, kx, "nt"), NEG_INF) for kx, ok in zip(k3, oks)] for qh, k3 in zip(qs, k3s)]
    mx = [jnp.maximum(jnp.maximum(jnp.max(t[0], -1, keepdims=True), jnp.max(t[1], -1, keepdims=True)),
                      jnp.maximum(jnp.max(t[2], -1, keepdims=True), sk)) for t, sk in zip(s, sink)]
    e = [[jnp.exp(tx - m) for tx in t] for t, m in zip(s, mx)]
    e_sink = [jnp.exp(sk - m) for sk, m in zip(sink, mx)]
    inv = [1.0 / (jnp.sum(t[0], -1, keepdims=True) + jnp.sum(t[1], -1, keepdims=True)
                  + jnp.sum(t[2], -1, keepdims=True) + es) for t, es in zip(e, e_sink)]
    return [[tx * i for tx in t] for t, i in zip(e, inv)], [es * i for es, i in zip(e_sink, inv)]


def _head_cols(i):
    return slice(i * HEAD_DIM, (i + 1) * HEAD_DIM)


def _attn_operands(refs):
    q_ref, kp_ref, kc_ref, km_ref, vp_ref, vc_ref, vm_ref, s_ref = refs
    qs = [q_ref[:, _head_cols(i)] * (HEAD_DIM ** -0.5) for i in range(Q_HEADS)]
    k3 = [[ref[:, _head_cols(h)] for ref in (kp_ref, kc_ref, km_ref)] for h in range(KV_HEADS)]
    v3 = [[ref[:, _head_cols(h)] for ref in (vp_ref, vc_ref, vm_ref)] for h in range(KV_HEADS)]
    return (qs, [k3[i // GROUP] for i in range(Q_HEADS)], [v3[i // GROUP] for i in range(Q_HEADS)],
            [s_ref[:, i:i + 1] for i in range(Q_HEADS)])


def _attention(q, k, v, sinks, *, name):
    lp = q.shape[0]
    nb = lp // BLOCK
    prev = lambda i: (jnp.maximum(i - 1, 0), 0)
    cur = lambda i: (i, 0)
    meta = lambda i: (0, 0)
    kv = lambda index: pl.BlockSpec((BLOCK, KV_W), index)

    def body(*refs):
        o_ref = refs[-1]
        qs, k3s, v3s, sink = _attn_operands(refs[:-1])
        p, _ = _attn_probs(qs, k3s, sink, _attn_masks(pl.program_id(0)))
        out = [_dot(ph[0], v3[0]) + _dot(ph[1], v3[1]) + _dot(ph[2], v3[2]) for ph, v3 in zip(p, v3s)]
        for i in range(Q_HEADS):
            o_ref[:, _head_cols(i)] = out[i].astype(o_ref.dtype)

    return pl.pallas_call(
        body, name=name, grid=(nb,),
        in_specs=[pl.BlockSpec((BLOCK, Q_W), cur), kv(prev), kv(cur), kv(meta), kv(prev), kv(cur), kv(meta),
                  _full((1, Q_HEADS))],
        out_specs=pl.BlockSpec((BLOCK, Q_W), cur),
        out_shape=jax.ShapeDtypeStruct((lp, Q_W), bf16),
        compiler_params=_params(("parallel",)),
    )(q, k, k, k, v, v, v, sinks)


def _attention_bwd(q, k, v, sinks, out, do, *, name):
    lp = q.shape[0]
    nb = lp // BLOCK
    cur = lambda n: (jnp.minimum(n, nb - 1), 0)
    prev = lambda n: (jnp.maximum(jnp.minimum(n, nb - 1) - 1, 0), 0)
    behind = lambda n: (jnp.maximum(n - 1, 0), 0)
    meta = lambda n: (0, 0)
    kv = lambda index: pl.BlockSpec((BLOCK, KV_W), index)
    scale = HEAD_DIM ** -0.5

    def body(*refs):
        ins, fwd_ref, do_ref = refs[:8], refs[8], refs[9]
        dq_ref, dk_ref, dv_ref, dkm_ref, dvm_ref, ds_ref, carry_k, carry_v = refs[10:]
        n = pl.program_id(0)

        @pl.when(n == 0)
        def _():
            for ref in (dkm_ref, dvm_ref, ds_ref, carry_k, carry_v):
                ref[...] = jnp.zeros_like(ref)

        @pl.when(n < nb)
        def _():
            qs, k3s, v3s, sink = _attn_operands(ins)
            do = [do_ref[:, _head_cols(i)] for i in range(Q_HEADS)]
            p, p_sink = _attn_probs(qs, k3s, sink, _attn_masks(n))
            delta = [jnp.sum(d * fwd_ref[:, _head_cols(i)].astype(f32), -1, keepdims=True) for i, d in enumerate(do)]
            dp = [[_dot(d, vx, "nt") for vx in v3] for d, v3 in zip(do, v3s)]
            ds = [[px * (dx - dl) for px, dx in zip(ph, dh)] for ph, dh, dl in zip(p, dp, delta)]
            dq = [_dot(dsh[0], k3[0]) + _dot(dsh[1], k3[1]) + _dot(dsh[2], k3[2]) for dsh, k3 in zip(ds, k3s)]
            for i in range(Q_HEADS):
                dq_ref[:, _head_cols(i)] = dq[i] * scale
                ds_ref[:, i:i + 1] -= jnp.sum(p_sink[i] * delta[i], axis=0, keepdims=True)
            for h in range(KV_HEADS):
                group = slice(h * GROUP, (h + 1) * GROUP)
                q_all = jnp.concatenate(qs[group], axis=0)
                do_all = jnp.concatenate(do[group], axis=0)
                dk3 = [_dot(jnp.concatenate([dsh[x] for dsh in ds[group]], axis=0), q_all, "tn") for x in range(3)]
                dv3 = [_dot(jnp.concatenate([ph[x] for ph in p[group]], axis=0), do_all, "tn") for x in range(3)]
                hs = _head_cols(h)
                for out_ref, carry, meta_ref, d3 in ((dk_ref, carry_k, dkm_ref, dk3),
                                                     (dv_ref, carry_v, dvm_ref, dv3)):
                    out_ref[:, hs] = carry[:, hs] + d3[0]
                    carry[:, hs] = d3[1]
                    meta_ref[:, hs] += d3[2]

        @pl.when(n == nb)
        def _():
            dk_ref[...] = carry_k[...]
            dv_ref[...] = carry_v[...]

    kv_shape = jax.ShapeDtypeStruct((lp, KV_W), f32)
    one_shape = jax.ShapeDtypeStruct((BLOCK, KV_W), f32)
    return pl.pallas_call(
        body, name=name, grid=(nb + 1,),
        in_specs=[pl.BlockSpec((BLOCK, Q_W), cur), kv(prev), kv(cur), kv(meta), kv(prev), kv(cur), kv(meta),
                  _full((1, Q_HEADS)), pl.BlockSpec((BLOCK, Q_W), cur), pl.BlockSpec((BLOCK, Q_W), cur)],
        out_specs=[pl.BlockSpec((BLOCK, Q_W), cur), kv(behind), kv(behind), kv(meta), kv(meta),
                   _full((1, Q_HEADS))],
        out_shape=[jax.ShapeDtypeStruct((lp, Q_W), f32), kv_shape, kv_shape, one_shape, one_shape,
                   jax.ShapeDtypeStruct((1, Q_HEADS), f32)],
        scratch_shapes=[pltpu.VMEM((BLOCK, KV_W), f32), pltpu.VMEM((BLOCK, KV_W), f32)],
        compiler_params=_params(("arbitrary",)),
    )(q, k, k, k, v, v, v, sinks, out, do)


@jax.custom_vjp
def _known_inverse(l, x):
    return x


def _known_inverse_fwd(l, x):
    return x, x


def _known_inverse_bwd(x, ct):
    return _dot(_dot(x, ct, "tn"), x, "nt"), jnp.zeros_like(x)


_known_inverse.defvjp(_known_inverse_fwd, _known_inverse_bwd)


@jax.custom_vjp
def _decayed(x, c):
    return (x * jnp.exp(c)).astype(bf16).astype(f32)


def _decayed_fwd(x, c):
    e = jnp.exp(c)
    out = (x * e).astype(bf16).astype(f32)
    return out, (e, out)


def _decayed_bwd(res, ct):
    e, out = res
    return ct * e, ct * out


_decayed.defvjp(_decayed_fwd, _decayed_bwd)


@jax.custom_vjp
def _pair(x, y):
    return _dot(x, y, "nt")


def _pair_fwd(x, y):
    return _dot(x, y, "nt"), (x, y)


def _pair_bwd(res, ct):
    x, y = res
    hi = ct.astype(bf16)
    lo = (ct - hi.astype(f32)).astype(bf16)
    return _dot(hi, y) + _dot(lo, y), _dot(hi, x, "tn") + _dot(lo, x, "tn")


_pair.defvjp(_pair_fwd, _pair_bwd)


def _scan_chunk(s0, r, lw, k, v, a, b, inv=None):
    t = r[0].shape[0]
    ii = lax.broadcasted_iota(jnp.int32, (t, t), 0)
    jj = lax.broadcasted_iota(jnp.int32, (t, t), 1)
    incl = jj <= ii
    strict = jj < ii
    tri = incl.astype(f32)
    eye = jnp.where(ii == jj, 1.0, 0.0)
    cl = [_const_dot(tri, x) for x in lw]
    mid = [c[t // 2 - 1:t // 2, :] for c in cl]
    s0 = [s * jnp.exp(m) for s, m in zip(s0, mid)]
    cl = [c - m for c, m in zip(cl, mid)]
    rt = [_decayed(x, c) for x, c in zip(r, cl)]
    at = [_decayed(x, c - l) for x, c, l in zip(a, cl, lw)]
    bt = [_decayed(x, -c) for x, c in zip(b, cl)]
    kt = [_decayed(x, -c) for x, c in zip(k, cl)]
    l_ab = [jnp.where(strict, _pair(x, y), 0.0) for x, y in zip(at, bt)]
    l_ak = [jnp.where(strict, _pair(x, y), 0.0) for x, y in zip(at, kt)]
    r_b = [jnp.where(incl, _pair(x, y), 0.0) for x, y in zip(rt, bt)]
    r_k = [jnp.where(incl, _pair(x, y), 0.0) for x, y in zip(rt, kt)]
    if inv is None:
        inv = [eye + x for x in l_ab]
        pw = l_ab
        for _ in range(int(math.log2(t)) - 1):
            pw = [_dot(x, x) for x in pw]
            inv = [x + _dot(x, y) for x, y in zip(inv, pw)]
    else:
        inv = [_known_inverse(x, y) for x, y in zip(l_ab, inv)]
    rhs = [_dot(x, s, "nt") + _dot(m, y) for x, s, m, y in zip(at, s0, l_ak, v)]
    u = [_dot(x, y) for x, y in zip(inv, rhs)]
    y_s = [_dot(x, s, "nt") for x, s in zip(rt, s0)]
    y = [ys + _dot(m, uu) + _dot(n, vv) for ys, m, uu, n, vv in zip(y_s, r_b, u, r_k, v)]
    grow = [s + _dot(uu, x, "tn") + _dot(vv, z, "tn") for s, uu, x, vv, z in zip(s0, u, bt, v, kt)]
    s1 = [g * jnp.exp(c[t - 1:t, :]) for g, c in zip(grow, cl)]
    return y, s1, inv


def _head_rows(h):
    return slice(h * RWKV_HEAD, (h + 1) * RWKV_HEAD)


def _per_head(ref):
    return [ref[:, _head_rows(h)] for h in range(RWKV_HEADS)]


def _scan(r, lw, k, v, a, b, *, name):
    lp = r.shape[0]
    nc = lp // CHUNK
    row = pl.BlockSpec((CHUNK, RWKV_DIM), lambda c: (c, 0))

    def body(r_ref, lw_ref, k_ref, v_ref, a_ref, b_ref, y_ref, s_ref, inv_ref, state):
        @pl.when(pl.program_id(0) == 0)
        def _():
            state[...] = jnp.zeros_like(state)

        s_ref[...] = state[...]
        s0 = [state[_head_rows(h), :] for h in range(RWKV_HEADS)]
        y, s1, inv = _scan_chunk(s0, *[_per_head(ref) for ref in (r_ref, lw_ref, k_ref, v_ref, a_ref, b_ref)])
        for h in range(RWKV_HEADS):
            y_ref[:, _head_rows(h)] = y[h]
            state[_head_rows(h), :] = s1[h]
            inv_ref[h * CHUNK:(h + 1) * CHUNK, :] = inv[h].astype(inv_ref.dtype)

    return pl.pallas_call(
        body, name=name, grid=(nc,), in_specs=[row] * 6,
        out_specs=[row, pl.BlockSpec((RWKV_DIM, RWKV_HEAD), lambda c: (c, 0)),
                   pl.BlockSpec((RWKV_HEADS * CHUNK, CHUNK), lambda c: (c, 0))],
        out_shape=[jax.ShapeDtypeStruct((lp, RWKV_DIM), f32), jax.ShapeDtypeStruct((nc * RWKV_DIM, RWKV_HEAD), f32),
                   jax.ShapeDtypeStruct((nc * RWKV_HEADS * CHUNK, CHUNK), bf16)],
        scratch_shapes=[pltpu.VMEM((RWKV_DIM, RWKV_HEAD), f32)],
        compiler_params=_params(("arbitrary",)),
    )(r, lw, k, v, a, b)


def _scan_bwd(r, lw, k, v, a, b, states, inverses, dy, *, name):
    lp = r.shape[0]
    nc = lp // CHUNK
    back = lambda c: (nc - 1 - c, 0)
    row = pl.BlockSpec((CHUNK, RWKV_DIM), back)

    def body(r_ref, lw_ref, k_ref, v_ref, a_ref, b_ref, s_ref, inv_ref, dy_ref,
             dr_ref, dlw_ref, dk_ref, dv_ref, da_ref, db_ref, dstate):
        @pl.when(pl.program_id(0) == 0)
        def _():
            dstate[...] = jnp.zeros_like(dstate)

        outs = (dr_ref, dlw_ref, dk_ref, dv_ref, da_ref, db_ref)
        s0 = [s_ref[_head_rows(h), :] for h in range(RWKV_HEADS)]
        inv = [inv_ref[h * CHUNK:(h + 1) * CHUNK, :].astype(f32) for h in range(RWKV_HEADS)]
        _, vjp = jax.vjp(lambda *args: _scan_chunk(*args, inv=inv)[:2], s0,
                         *[_per_head(ref) for ref in (r_ref, lw_ref, k_ref, v_ref, a_ref, b_ref)])
        g = vjp((_per_head(dy_ref), [dstate[_head_rows(h), :] for h in range(RWKV_HEADS)]))
        for h in range(RWKV_HEADS):
            dstate[_head_rows(h), :] = g[0][h]
            for o_ref, gv in zip(outs, g[1:]):
                o_ref[:, _head_rows(h)] = gv[h]

    shape = jax.ShapeDtypeStruct((lp, RWKV_DIM), f32)
    return pl.pallas_call(
        body, name=name, grid=(nc,),
        in_specs=[row] * 6 + [pl.BlockSpec((RWKV_DIM, RWKV_HEAD), back),
                              pl.BlockSpec((RWKV_HEADS * CHUNK, CHUNK), back), row],
        out_specs=[row] * 6, out_shape=[shape] * 6,
        scratch_shapes=[pltpu.VMEM((RWKV_DIM, RWKV_HEAD), f32)],
        compiler_params=_params(("arbitrary",)),
    )(r, lw, k, v, a, b, states, inverses, dy)


def _loss_head(h2, target, g_final, *, name):
    lp = h2.shape[0]
    tm = BLOCK
    front_tiles = FRONT // tm

    def body(h_ref, t_ref, g_ref, loss_ref, dh_ref, dg_ref):
        i = pl.program_id(0)
        real = i >= front_tiles

        def tile_loss(hv, gv):
            err = _rms(hv, gv) - t_ref[...]
            return jnp.where(real, 0.5 * jnp.sum(jnp.mean(err * err, axis=-1, keepdims=True)), 0.0)

        loss, (dh, dg) = jax.value_and_grad(tile_loss, argnums=(0, 1))(h_ref[...], g_ref[...])

        @pl.when(i == 0)
        def _():
            loss_ref[...] = jnp.zeros_like(loss_ref)
            dg_ref[...] = jnp.zeros_like(dg_ref)

        loss_ref[...] += jnp.full(loss_ref.shape, loss, f32)
        dg_ref[...] += dg
        dh_ref[...] = dh

    return pl.pallas_call(
        body, name=name, grid=(lp // tm,),
        in_specs=[pl.BlockSpec((tm, D_MODEL), lambda i: (i, 0)),
                  pl.BlockSpec((tm, D_MODEL), lambda i: (jnp.maximum(i - front_tiles, 0), 0)),
                  _full(g_final.shape)],
        out_specs=[_full((8, 128)), pl.BlockSpec((tm, D_MODEL), lambda i: (i, 0)), _full(g_final.shape)],
        out_shape=[jax.ShapeDtypeStruct((8, 128), f32), jax.ShapeDtypeStruct((lp, D_MODEL), f32),
                   jax.ShapeDtypeStruct(g_final.shape, f32)],
        compiler_params=_params(("arbitrary",)),
    )(h2, target, g_final)


def _input_norm_bwd(h0, g, du, dh1, *, name):
    lp = h0.shape[0]
    tm = FRONT

    def body(h_ref, g_ref, du_ref, dh1_ref, dx_ref, front_ref, dg_ref):
        i = pl.program_id(0)
        _, vjp = jax.vjp(lambda hv, gv: (_rms(hv, gv), hv), h_ref[...], g_ref[...])
        dh, dg = vjp((du_ref[...].astype(f32), dh1_ref[...]))

        @pl.when(i == 0)
        def _():
            dg_ref[...] = jnp.zeros_like(dg_ref)
            front_ref[...] = dh

        dg_ref[...] += dg
        dx_ref[...] = dh

    tile = pl.BlockSpec((tm, D_MODEL), lambda i: (i, 0))
    return pl.pallas_call(
        body, name=name, grid=(lp // tm,),
        in_specs=[tile, _full(g.shape), tile, tile],
        out_specs=[pl.BlockSpec((tm, D_MODEL), lambda i: (jnp.maximum(i - 1, 0), 0)), _full((tm, D_MODEL)),
                   _full(g.shape)],
        out_shape=[jax.ShapeDtypeStruct((lp - tm, D_MODEL), f32), jax.ShapeDtypeStruct((tm, D_MODEL), f32),
                   jax.ShapeDtypeStruct(g.shape, f32)],
        compiler_params=_params(("arbitrary",)),
    )(h0, g, du, dh1)


def _local_step(x, target, meta, p, early_weights=None, late_weights=None, emit=None):
    emit = emit or (lambda group, grads: 0.0)
    seq = x.shape[0]
    lp = seq + FRONT
    h0 = jnp.concatenate([jnp.zeros((PAD, D_MODEL), f32), meta, x], axis=0)
    cos_t, sin_t, swap = _rope_tables(lp)
    hsum = _head_sum_matrix(RWKV_DIM, RWKV_HEAD)
    hmean = hsum / RWKV_HEAD
    post_params = [p["ln_w"], p["ln_b"], p["r_k"], hmean]

    (u,) = _rowwise(lambda hv, g: (_rms(hv, g),), [h0], [p["norm_mix_g"]], [(D_MODEL, bf16)], name="norm_mix")
    if early_weights is not None:
        p = {**p, **early_weights(u)}
    prep_params = [p["w0"], p["w2"], p["a0"], p["a2"], p["g2"], p["k_k"], p["k_a"], hsum]
    qkv, p_rkv, p_lora, gates = _proj_in(u, p["w_in_lr"], p["b_in"], [ATTN_PROJ, RKV_W, LORA_W, 2 * D_MODEL],
                                         name="proj_in", zero_rows_below=PAD)

    q, k, v = _rowwise(_attn_prep, [qkv, cos_t, sin_t], [swap], [(Q_W, bf16), (KV_W, bf16), (KV_W, bf16)],
                       name="attn_prep")
    y_attn = _attention(q, k, v, p["sinks"], name="attention")

    mix_rkv, mix_lora = p["mix"][:, :RKV_W], p["mix"][:, RKV_W:]
    r_, lw_, k_, v_, a_, b_, g_ = _mixer_inputs([p_rkv, p_lora], [mix_rkv, mix_lora], prep_params,
                                                name="mixer_inputs")
    y_scan, states, inverses = _scan(r_, lw_, k_, v_, a_, b_, name="wkv_scan")
    (y_rwkv,) = _rowwise(_rwkv_post, [y_scan, r_, k_, v_, g_], post_params, [(RWKV_DIM, bf16)], name="rwkv_post")

    if late_weights is not None:
        p = {**p, **late_weights(y_rwkv)}
    br_a, br_r, merged = _branch_merge(y_attn, y_rwkv, p["w_br_attn_t"], p["w_br_rwkv_t"], gates, name="branch_merge")
    h1, f = _residual_norm(merged, p["w_o"], h0, p["norm_ffn_g"], name="out_proj")
    gate, up, act = _ffn_in(f, p["w_gate_t"], p["w_up_t"], name="ffn_in")
    h2 = _mm(act, p["w_down"], "nn", name="ffn_down", add=h1)

    loss8, dh2, d_final_g = _loss_head(h2, target, p["norm_final_g"], name="loss_head")
    dgate, dup = _ffn_in_bwd(dh2, p["w_down"], gate, up, name="ffn_in_bwd")
    d_w_down = _mm_tn(act, dh2, name="dw_down")
    d_w_gate_t = _mm_tn(dgate, f, name="dw_gate")
    d_w_up_t = _mm_tn(dup, f, name="dw_up")
    zero = emit("ffn", dict(w_down=d_w_down, w_gate_t=d_w_gate_t, w_up_t=d_w_up_t))
    dh1, d_ffn_g = _residual_norm_bwd([dgate, dup], [p["w_gate_t"], p["w_up_t"]], h1, p["norm_ffn_g"] + zero, dh2,
                                      name="norm_ffn_bwd")
    dgates, dbr_a, dbr_r = _branch_merge_bwd(dh1, p["w_o"], gates, br_a, br_r, name="branch_merge_bwd")
    d_w_o = _mm_tn(merged, dh1, name="dw_o")
    d_w_br_attn_t = _mm_tn(dbr_a, y_attn, name="dw_br_attn")
    d_w_br_rwkv_t = _mm_tn(dbr_r, y_rwkv, name="dw_br_rwkv")
    zero = emit("branch", dict(w_o=d_w_o, w_br_attn_t=d_w_br_attn_t, w_br_rwkv_t=d_w_br_rwkv_t))
    dy_attn = _mm(dbr_a, p["w_br_attn_t"], "nn", name="d_y_attn")
    dy_rwkv = _mm(dbr_r, p["w_br_rwkv_t"], "nn", name="d_y_rwkv")

    post_params = [p["ln_w"] + zero, p["ln_b"], p["r_k"], hmean]
    res = _rowwise_bwd(_rwkv_post, [y_scan, r_, k_, v_, g_], post_params, [[dy_rwkv]], name="rwkv_post_bwd",
                       diff_rows=[True] * 5, diff_params=[True, True, True, False])
    dy_scan, dr_p, dk_p, dv_p, dg_p, d_ln_w, d_ln_b, d_r_k = res
    dr_s, dlw_s, dk_s, dv_s, da_s, db_s = _scan_bwd(r_, lw_, k_, v_, a_, b_, states, inverses, dy_scan,
                                                    name="wkv_scan_bwd")
    res = _mixer_inputs_bwd([p_rkv, p_lora], [mix_rkv, mix_lora], prep_params,
                            [[dr_s, dr_p], [dlw_s], [dk_s, dk_p], [dv_s, dv_p], [da_s], [db_s], [dg_p]],
                            name="mixer_inputs_bwd")
    dp_rkv, dp_lora, d_mix_rkv, d_mix_lora, d_w0, d_w2, d_a0, d_a2, d_g2, d_k_k, d_k_a = res

    dq, dk, dv, dkm, dvm, d_sinks = _attention_bwd(q, k, v, p["sinks"], y_attn, dy_attn, name="attention_bwd")
    rest = jnp.zeros((lp - BLOCK, KV_W), f32)
    dkm, dvm = jnp.concatenate([dkm, rest], axis=0), jnp.concatenate([dvm, rest], axis=0)
    (dqkv,) = _rowwise_bwd(_attn_prep, [qkv, cos_t, sin_t], [swap], [[dq], [dk, dkm], [dv, dvm]], name="attn_prep_bwd",
                           diff_rows=[True, False, False], diff_params=[False], out_dtypes=[bf16])

    d_w_qkv_t, db_qkv = _mm_tn(dqkv, u, name="dw_qkv", colsum=True)
    d_w_rkv_t, db_rkv = _mm_tn(dp_rkv, u, name="dw_rkv", colsum=True)
    d_w_lora_t, db_lora = _mm_tn(dp_lora, u, name="dw_lora", colsum=True)
    d_w_gates_t, db_gates = _mm_tn(dgates, u, name="dw_gates", colsum=True)
    d_w_in_t = jnp.concatenate([d_w_qkv_t, d_w_rkv_t, d_w_lora_t, d_w_gates_t], axis=0)
    zero = emit("input", dict(w_in_t=d_w_in_t, g2=d_g2, w2=d_w2, a2=d_a2))
    du = _proj_in_bwd([dqkv, dp_rkv, dp_lora, dgates], p["w_in_lr"], name="d_u")
    dx, d_front, d_mix_g = _input_norm_bwd(h0, p["norm_mix_g"] + zero, du, dh1, name="norm_mix_bwd")

    grads = dict(
        w_in_t=d_w_in_t,
        b_in=jnp.concatenate([db_qkv, db_rkv, db_lora, db_gates], axis=1),
        mix=jnp.concatenate([d_mix_rkv, d_mix_lora], axis=1),
        norm_mix_g=d_mix_g, sinks=d_sinks, w0=d_w0, w2=d_w2, a0=d_a0, a2=d_a2, g2=d_g2, k_k=d_k_k, k_a=d_k_a,
        r_k=d_r_k, ln_w=d_ln_w, ln_b=d_ln_b, w_br_attn_t=d_w_br_attn_t, w_br_rwkv_t=d_w_br_rwkv_t, w_o=d_w_o,
        norm_ffn_g=d_ffn_g, w_gate_t=d_w_gate_t, w_up_t=d_w_up_t, w_down=d_w_down, norm_final_g=d_final_g,
        meta=d_front[PAD:],
    )
    return loss8[0, 0], dx, grads


def _position():
    return lax.axis_index("x"), lax.axis_index("y"), lax.axis_index("c")


def _other_chips(x, y):
    return [(1 - x, y), (x, 1 - y), (1 - x, 1 - y)]


_HBM = pl.BlockSpec(memory_space=pltpu.HBM)
_SEM = pl.BlockSpec(memory_space=pltpu.SEMAPHORE)
_EFFECT = pltpu.SideEffectType.DATAFLOW_SIDE_EFFECTING


def _landing_zone(src, kind):
    shape = {"whole": (N_CHIPS,) + src.shape, "half": (2, N_CHIPS, src.shape[0], src.shape[1] // 2),
             "slab": (3,) + src.shape[1:], "sibling": src.shape}[kind]
    return lax.empty(shape, src.dtype)


def _copies_per_source(kind):
    return 1 if kind == "sibling" else 3


def _chip_copies(src_refs, land_refs, send_sems, recv_sems, kind):
    x, y, c = _position()
    if kind == "sibling":
        return [pltpu.make_async_remote_copy(
            src_ref=src, dst_ref=land, send_sem=send_sems.at[a], recv_sem=recv_sems.at[a],
            device_id=(x, y, 1 - c), device_id_type=MESH) for a, (src, land) in enumerate(zip(src_refs, land_refs))]
    copies = []
    for a, (src, land) in enumerate(zip(src_refs, land_refs)):
        for j, (px, py) in enumerate(_other_chips(x, y)):
            if kind == "whole":
                src_ref, dst_ref = src, land.at[2 * x + y]
            elif kind == "half":
                half = src.shape[1] // 2
                src_ref, dst_ref = src.at[:, pl.ds(pl.multiple_of(c * half, half), half)], land.at[c, 2 * x + y]
            else:
                src_ref, dst_ref = src.at[2 * px + py], land.at[j]
            copies.append(pltpu.make_async_remote_copy(
                src_ref=src_ref, dst_ref=dst_ref, send_sem=send_sems.at[3 * a + j], recv_sem=recv_sems.at[3 * a + j],
                device_id=(px, py, c), device_id_type=MESH))
    return copies


def _exchange_start(srcs, *, kind, name):
    n = len(srcs)
    lands = [_landing_zone(s, kind) for s in srcs]

    def body(*refs):
        for cp in _chip_copies(refs[:n], refs[n:2 * n], refs[2 * n], refs[2 * n + 1], kind):
            cp.start()
        refs[-1][...] = jnp.zeros_like(refs[-1])

    res = pl.pallas_call(
        body, name=name,
        out_shape=(pltpu.SemaphoreType.DMA((_copies_per_source(kind) * n,)),
                   pltpu.SemaphoreType.DMA((_copies_per_source(kind) * n,)),
                   *[pltpu.HBM(a.shape, a.dtype) for a in srcs + lands], jax.ShapeDtypeStruct((8, 128), f32)),
        in_specs=[_HBM] * (2 * n),
        out_specs=(_SEM, _SEM, *[_HBM] * (2 * n), pl.BlockSpec(memory_space=pltpu.VMEM)),
        input_output_aliases={i: 2 + i for i in range(2 * n)},
        compiler_params=pltpu.CompilerParams(has_side_effects=_EFFECT),
    )(*[pltpu.with_memory_space_constraint(a, pltpu.HBM) for a in srcs + lands])
    return res[0], res[1], list(res[2:2 + n]), list(res[2 + n:2 + 2 * n]), res[-1]


def _exchange_wait(handle, after, *, kind, name):
    send_sems, recv_sems, srcs, lands, _ = handle
    n = len(srcs)

    def body(*refs):
        for cp in _chip_copies(refs[:n], refs[n:2 * n], refs[2 * n], refs[2 * n + 1], kind):
            cp.wait_send()
            cp.wait_recv()

    res = pl.pallas_call(
        body, name=name,
        out_shape=tuple(pltpu.HBM(a.shape, a.dtype) for a in srcs + lands),
        in_specs=[_HBM] * (2 * n) + [_SEM, _SEM, pl.BlockSpec(memory_space=pl.ANY)],
        out_specs=tuple([_HBM] * (2 * n)),
        input_output_aliases={i: i for i in range(2 * n)},
        compiler_params=pltpu.CompilerParams(has_side_effects=_EFFECT),
    )(*srcs, *lands, send_sems, recv_sems, after)
    return list(res[:n]), list(res[n:])


def _sum_own_and_received(g, recv, *, name):
    _, r, w = g.shape
    tm = _tile(r)
    if g.dtype == bf16 and tm % 16:
        tm = r
    x, y, _ = _position()
    me = jnp.reshape(2 * x + y, (1,)).astype(jnp.int32)

    def body(me_ref, g_ref, r_ref, o_ref):
        o_ref[...] = (g_ref[0].astype(f32) + r_ref[0].astype(f32)) + (r_ref[1].astype(f32) + r_ref[2].astype(f32))

    return pl.pallas_call(
        body, name=name,
        grid_spec=pltpu.PrefetchScalarGridSpec(
            num_scalar_prefetch=1, grid=(r // tm,),
            in_specs=[pl.BlockSpec((1, tm, w), lambda i, me_ref: (me_ref[0], i, 0)),
                      pl.BlockSpec((3, tm, w), lambda i, me_ref: (0, i, 0))],
            out_specs=pl.BlockSpec((tm, w), lambda i, me_ref: (i, 0))),
        out_shape=jax.ShapeDtypeStruct((r, w), f32),
        compiler_params=_params(("parallel",)),
    )(me, g, recv)


def _swap_cores(arrs, *, name):
    n = len(arrs)

    def body(*refs):
        x, y, c = _position()
        copies = [pltpu.make_async_remote_copy(
            src_ref=refs[i], dst_ref=refs[n + i], send_sem=refs[2 * n].at[i], recv_sem=refs[2 * n + 1].at[i],
            device_id=(x, y, 1 - c), device_id_type=MESH) for i in range(n)]
        for cp in copies:
            cp.start()
        for cp in copies:
            cp.wait_recv()
        for cp in copies:
            cp.wait_send()

    return pl.pallas_call(
        body, name=name,
        in_specs=[pl.BlockSpec(memory_space=pl.ANY)] * n,
        out_specs=[pl.BlockSpec(memory_space=pl.ANY)] * n,
        out_shape=[jax.ShapeDtypeStruct(a.shape, a.dtype) for a in arrs],
        scratch_shapes=[pltpu.SemaphoreType.DMA((n,)), pltpu.SemaphoreType.DMA((n,))],
    )(*arrs)


def _swap_halves(zone, *, name):
    def body(z_ref, o_ref, send_sems, recv_sems):
        x, y, c = _position()
        mine = [pltpu.make_async_remote_copy(
            src_ref=o_ref.at[c, 2 * px + py], dst_ref=o_ref.at[c, 2 * px + py], send_sem=send_sems.at[j],
            recv_sem=recv_sems.at[j], device_id=(x, y, 1 - c), device_id_type=MESH)
            for j, (px, py) in enumerate(_other_chips(x, y))]
        for cp in mine:
            cp.start()
        for j, (px, py) in enumerate(_other_chips(x, y)):
            pltpu.make_async_remote_copy(
                src_ref=o_ref.at[c, 2 * px + py], dst_ref=o_ref.at[1 - c, 2 * px + py], send_sem=send_sems.at[j],
                recv_sem=recv_sems.at[j], device_id=(x, y, 1 - c), device_id_type=MESH).wait_recv()
        for cp in mine:
            cp.wait_send()

    return pl.pallas_call(
        body, name=name,
        in_specs=[pl.BlockSpec(memory_space=pl.ANY)], out_specs=pl.BlockSpec(memory_space=pl.ANY),
        out_shape=jax.ShapeDtypeStruct(zone.shape, zone.dtype), input_output_aliases={0: 0},
        scratch_shapes=[pltpu.SemaphoreType.DMA((3,)), pltpu.SemaphoreType.DMA((3,))],
    )(zone)


def _all_reduce_small(a, after, *, name):
    rows, w = a.shape

    def body(a_ref, after_ref, o_ref, buf, send_sems, recv_sems):
        x, y, c = _position()
        me = 4 * x + 2 * y + c
        buf[0] = a_ref[...]
        sends = []
        for rel in range(1, N_DEV):
            peer = ((1 - x) if rel & 4 else x, (1 - y) if rel & 2 else y, (1 - c) if rel & 1 else c)
            cp = pltpu.make_async_remote_copy(
                src_ref=a_ref, dst_ref=buf.at[rel], send_sem=send_sems.at[rel - 1], recv_sem=recv_sems.at[rel - 1],
                device_id=peer, device_id_type=MESH)
            cp.start()
            sends.append(cp)
        for cp in sends:
            cp.wait_recv()
        for cp in sends:
            cp.wait_send()
        acc = buf[jnp.bitwise_xor(me, 0)]
        for d in range(1, N_DEV):
            acc = acc + buf[jnp.bitwise_xor(me, d)]
        o_ref[...] = acc

    return pl.pallas_call(
        body, name=name,
        in_specs=[pl.BlockSpec(memory_space=pltpu.VMEM), pl.BlockSpec(memory_space=pl.ANY)],
        out_specs=pl.BlockSpec(memory_space=pltpu.VMEM),
        out_shape=jax.ShapeDtypeStruct((rows, w), f32),
        scratch_shapes=[pltpu.VMEM((N_DEV, rows, w), f32), pltpu.SemaphoreType.DMA((N_DEV - 1,)),
                        pltpu.SemaphoreType.DMA((N_DEV - 1,))],
    )(a, after)


def _adamw(w, g_parts, m, v, *, name, transposed=False):
    rows, cols = w.shape
    if transposed:
        tm = 256 if rows % 256 == 0 else rows
        g_spec = pl.BlockSpec((cols, tm), lambda i: (0, i))
    else:
        tm = _tile(rows, 256)
        g_spec = pl.BlockSpec((tm, cols), lambda i: (i, 0))
    n = len(g_parts)

    def body(*refs):
        w_ref, m_ref, v_ref = refs[0], refs[1 + n], refs[2 + n]
        g_ref, d_ref, nm_ref, nv_ref = refs[3 + n:]
        gv = refs[1][...]
        for part in refs[2:1 + n]:
            gv = gv + part[...]
        if transposed:
            gv = gv.T
        g_ref[...] = gv
        nm = ADAM_B1 * m_ref[...] + (1.0 - ADAM_B1) * gv
        nv = ADAM_B2 * v_ref[...] + (1.0 - ADAM_B2) * (gv * gv)
        m_hat = nm / (1.0 - ADAM_B1 ** ADAM_STEP)
        v_hat = nv / (1.0 - ADAM_B2 ** ADAM_STEP)
        d_ref[...] = -ADAM_LR * (m_hat / (jnp.sqrt(v_hat) + ADAM_EPS) + ADAM_WD * w_ref[...])
        nm_ref[...] = nm
        nv_ref[...] = nv

    spec = pl.BlockSpec((tm, cols), lambda i: (i, 0))
    shape = jax.ShapeDtypeStruct((rows, cols), f32)
    return pl.pallas_call(
        body, name=name, grid=(rows // tm,), in_specs=[spec] + [g_spec] * n + [spec] * 2,
        out_specs=[spec] * 4, out_shape=[shape] * 4,
        compiler_params=_params(("parallel",)),
    )(w, *g_parts, m, v)


def _pad_rows(a, rows):
    return jnp.concatenate([a, jnp.zeros((rows - a.shape[0], a.shape[1]), a.dtype)], axis=0) if rows > a.shape[0] else a


_SMALL = (("norm_mix_g", D_MODEL), ("b_in", D_IN), ("sinks", Q_HEADS), ("mix", RWKV_PROJ), ("w0", RWKV_DIM),
          ("a0", RWKV_DIM), ("k_k", RWKV_DIM), ("k_a", RWKV_DIM), ("r_k", RWKV_DIM), ("ln_w", RWKV_DIM),
          ("ln_b", RWKV_DIM), ("norm_ffn_g", D_MODEL), ("norm_final_g", D_MODEL))


def _pack_small(d):
    flat = jnp.concatenate([d[n].reshape(-1).astype(f32) for n, _ in _SMALL])
    return flat


def _unpack_small(flat):
    out, off = {}, 0
    for n, size in _SMALL:
        out[n] = flat[off:off + size]
        off += size
    return out


_SMALL_TOTAL = sum(s for _, s in _SMALL)


def kernel(x, meta_tokens, norm_mix_g, w_in, b_in, attn_sinks, rwkv_mix, rwkv_w0, rwkv_w2, rwkv_a0, rwkv_a2, rwkv_g2, rwkv_k_k, rwkv_k_a, rwkv_r_k, rwkv_ln_w, rwkv_ln_b, w_br_attn, w_br_rwkv, w_o, norm_ffn_g, w_ffn_gate, w_ffn_up, w_ffn_down, norm_final_g, loss_target, m_meta_tokens, m_norm_mix_g, m_w_in, m_b_in, m_attn_sinks, m_rwkv_mix, m_rwkv_w0, m_rwkv_w2, m_rwkv_a0, m_rwkv_a2, m_rwkv_g2, m_rwkv_k_k, m_rwkv_k_a, m_rwkv_r_k, m_rwkv_ln_w, m_rwkv_ln_b, m_w_br_attn, m_w_br_rwkv, m_w_o, m_norm_ffn_g, m_w_ffn_gate, m_w_ffn_up, m_w_ffn_down, m_norm_final_g, v_meta_tokens, v_norm_mix_g, v_w_in, v_b_in, v_attn_sinks, v_rwkv_mix, v_rwkv_w0, v_rwkv_w2, v_rwkv_a0, v_rwkv_a2, v_rwkv_g2, v_rwkv_k_k, v_rwkv_k_a, v_rwkv_r_k, v_rwkv_ln_w, v_rwkv_ln_b, v_w_br_attn, v_w_br_rwkv, v_w_o, v_norm_ffn_g, v_w_ffn_gate, v_w_ffn_up, v_w_ffn_down, v_norm_final_g):
    names = ("meta_tokens", "norm_mix_g", "w_in", "b_in", "attn_sinks", "rwkv_mix", "rwkv_w0", "rwkv_w2", "rwkv_a0",
             "rwkv_a2", "rwkv_g2", "rwkv_k_k", "rwkv_k_a", "rwkv_r_k", "rwkv_ln_w", "rwkv_ln_b", "w_br_attn",
             "w_br_rwkv", "w_o", "norm_ffn_g", "w_ffn_gate", "w_ffn_up", "w_ffn_down", "norm_final_g")
    w_all = dict(zip(names, (meta_tokens, norm_mix_g, w_in, b_in, attn_sinks, rwkv_mix, rwkv_w0, rwkv_w2, rwkv_a0,
                             rwkv_a2, rwkv_g2, rwkv_k_k, rwkv_k_a, rwkv_r_k, rwkv_ln_w, rwkv_ln_b, w_br_attn,
                             w_br_rwkv, w_o, norm_ffn_g, w_ffn_gate, w_ffn_up, w_ffn_down, norm_final_g)))
    m_all = dict(zip(names, (m_meta_tokens, m_norm_mix_g, m_w_in, m_b_in, m_attn_sinks, m_rwkv_mix, m_rwkv_w0,
                             m_rwkv_w2, m_rwkv_a0, m_rwkv_a2, m_rwkv_g2, m_rwkv_k_k, m_rwkv_k_a, m_rwkv_r_k,
                             m_rwkv_ln_w, m_rwkv_ln_b, m_w_br_attn, m_w_br_rwkv, m_w_o, m_norm_ffn_g, m_w_ffn_gate,
                             m_w_ffn_up, m_w_ffn_down, m_norm_final_g)))
    v_all = dict(zip(names, (v_meta_tokens, v_norm_mix_g, v_w_in, v_b_in, v_attn_sinks, v_rwkv_mix, v_rwkv_w0,
                             v_rwkv_w2, v_rwkv_a0, v_rwkv_a2, v_rwkv_g2, v_rwkv_k_k, v_rwkv_k_a, v_rwkv_r_k,
                             v_rwkv_ln_w, v_rwkv_ln_b, v_w_br_attn, v_w_br_rwkv, v_w_o, v_norm_ffn_g, v_w_ffn_gate,
                             v_w_ffn_up, v_w_ffn_down, v_norm_final_g)))
    cx, cy, _ = _position()
    chip = 2 * cx + cy

    t_of = dict(w_in_t="w_in", w_gate_t="w_ffn_gate", w_up_t="w_ffn_up", w_br_attn_t="w_br_attn",
                w_br_rwkv_t="w_br_rwkv", g2_t="rwkv_g2", w2_t="rwkv_w2", a2_t="rwkv_a2")
    plain_of = dict(w_down="w_ffn_down", w_o="w_o")
    meta_cols = meta_tokens.shape[1]

    def shard(k):
        return (w_all[t_of[k]][0].T if k in t_of else w_all[plain_of[k]][0]).astype(bf16)

    def whole(zone, own):
        return lax.dynamic_update_slice_in_dim(zone, own[None], chip, axis=0).reshape(-1, own.shape[-1])

    tiny = ("g2_t", "w2_t", "a2_t")
    late = ("w_gate_t", "w_up_t", "w_down", "w_o", "w_br_attn_t", "w_br_rwkv_t")
    w_in_own = shard("w_in_t")
    w_in_rows, w_in_cols = w_in_own.shape
    tiny_h = _exchange_start([shard(k) for k in tiny] + [meta_tokens], kind="whole", name="gather_tiny_start")
    w_in_h = _exchange_start([w_in_own + tiny_h[4][0, 0].astype(bf16)], kind="half", name="gather_w_in_start")
    behind = w_in_h[4][0, 0].astype(bf16)
    late_h = _exchange_start([shard(k) + behind for k in late], kind="whole", name="gather_late_start")
    own, zones = _exchange_wait(tiny_h, late_h[4], kind="whole", name="gather_tiny_wait")
    got = {k: whole(z, o) for k, z, o in zip(tiny, zones, own)}
    meta_full = whole(zones[-1], own[-1]).reshape(N_CHIPS, N_META, meta_cols).transpose(1, 0, 2).reshape(N_META, -1)
    p = dict(
        g2=got["g2_t"].T.astype(f32), w2=got["w2_t"].T.astype(f32), a2=got["a2_t"].T.astype(f32),
        b_in=b_in, sinks=attn_sinks, mix=rwkv_mix, w0=rwkv_w0, a0=rwkv_a0, k_k=rwkv_k_k, k_a=rwkv_k_a,
        r_k=rwkv_r_k.reshape(1, RWKV_DIM), ln_w=rwkv_ln_w, ln_b=rwkv_ln_b, norm_mix_g=norm_mix_g,
        norm_ffn_g=norm_ffn_g, norm_final_g=norm_final_g.reshape(1, D_MODEL),
    )

    def early_weights(after):
        own_h, zones_h = _exchange_wait(w_in_h, after, kind="half", name="gather_w_in_wait")
        zone = _swap_halves(zones_h[0], name="swap_w_in_halves")
        own_halves = own_h[0].reshape(w_in_rows, 2, w_in_cols // 2).transpose(1, 0, 2)[:, None]
        zone = lax.dynamic_update_slice(zone, own_halves, (0, chip, 0, 0))
        return dict(w_in_lr=zone.reshape(2, N_CHIPS * w_in_rows, w_in_cols // 2))

    def late_weights(after):
        own_l, zones_l = _exchange_wait(late_h, after, kind="whole", name="gather_late_wait")
        return {k: whole(z, o) for k, z, o in zip(late, zones_l, own_l)}

    started = {}

    def partial_sums(groups, after):
        parts = {}
        for group in groups:
            keys, handle = started[group]
            slabs, lands = _exchange_wait(handle, after, kind="slab", name="scatter_" + group + "_wait")
            parts.update({k: _sum_own_and_received(s, l, name="sum_chips_" + k) for k, s, l in zip(keys, slabs, lands)})
        return parts

    def emit(group, grads_):
        keys = list(grads_)
        slabs = []
        for k in keys:
            a = grads_[k].T if k in ("g2", "w2", "a2") else grads_[k]
            slabs.append(a.reshape(N_CHIPS, a.shape[0] // N_CHIPS, a.shape[1]))
        started[group] = (keys, _exchange_start(slabs, kind="slab", name="scatter_" + group + "_start"))
        zero = started[group][1][4]
        if group == "input":
            started["parts_a"] = partial_sums(("ffn", "branch"), zero)
            started["swap_a"] = _exchange_start(list(started["parts_a"].values()), kind="sibling",
                                                name="swap_cores_a_start")
            zero = started["swap_a"][4]
        return zero[0, 0]

    loss, dx, g = _local_step(x[0], loss_target[0], meta_full, p, early_weights, late_weights, emit)

    grads, delta, new_m, new_v = {}, {}, {}, {}
    in_grad_layout = ("w_in_t", "w_gate_t", "w_up_t")
    weight_of = {**t_of, **plain_of}

    def update(keys, mine, theirs):
        for k, part, other in zip(keys, mine, theirs):
            both = [part, other]
            k = k + "_t" if k in ("g2", "w2", "a2") else k
            n = weight_of[k]
            shape2 = w_all[n].shape[1:]
            w_, m_, v_ = (a.reshape(shape2) for a in (w_all[n], m_all[n], v_all[n]))
            if k in in_grad_layout:
                res = [t.T for t in _adamw(w_.T, both, m_.T, v_.T, name="adamw_" + n)]
            else:
                res = _adamw(w_, both, m_, v_, name="adamw_" + n, transposed=k in t_of)
            grads[n], delta[n], new_m[n], new_v[n] = (t.reshape(w_all[n].shape) for t in res)
        return delta[n]

    done = update(list(started["parts_a"]),
                  *_exchange_wait(started["swap_a"], dx, kind="sibling", name="swap_cores_a_wait"))
    small = jnp.concatenate([_pack_small(g), loss.reshape(1)])
    small_rows = -(-small.shape[0] // PACK_W)
    small = jnp.concatenate([small, jnp.zeros((small_rows * PACK_W - small.shape[0],), f32)]).reshape(small_rows, PACK_W)
    small_rows8 = -(-(small_rows + N_META) // 8) * 8
    reduced = _all_reduce_small(_pad_rows(jnp.concatenate([g["meta"], small], axis=0), small_rows8), done,
                                name="reduce_small")
    parts_b = partial_sums(("input",), reduced)
    update(list(parts_b), list(parts_b.values()), _swap_cores(list(parts_b.values()), name="swap_cores_b"))
    g_meta = lax.dynamic_slice_in_dim(reduced[:N_META], chip * meta_cols, meta_cols, axis=1)
    flat = reduced[N_META:N_META + small_rows].reshape(-1)
    g_small = _unpack_small(flat)
    loss_total = flat[_SMALL_TOTAL]

    small_of = dict(norm_mix_g="norm_mix_g", b_in="b_in", attn_sinks="sinks", rwkv_mix="mix", rwkv_w0="w0",
                    rwkv_a0="a0", rwkv_k_k="k_k", rwkv_k_a="k_a", rwkv_r_k="r_k", rwkv_ln_w="ln_w",
                    rwkv_ln_b="ln_b", norm_ffn_g="norm_ffn_g", norm_final_g="norm_final_g")
    grads["meta_tokens"] = g_meta
    for n, k in small_of.items():
        grads[n] = g_small[k].reshape(w_all[n].shape)

    rest = [n for n in names if n not in delta]

    def pack_rest(src):
        flat_ = jnp.concatenate([src[n].reshape(-1) for n in rest])
        rows_ = -(-flat_.shape[0] // (8 * PACK_W)) * 8
        return jnp.concatenate([flat_, jnp.ones((rows_ * PACK_W - flat_.shape[0],), f32)]).reshape(rows_, PACK_W)

    _, d_, m_, v_ = _adamw(pack_rest(w_all), [pack_rest(grads)], pack_rest(m_all), pack_rest(v_all),
                           name="adamw_small")
    off = 0
    for n in rest:
        size = w_all[n].size
        for dst, src in ((delta, d_), (new_m, m_), (new_v, v_)):
            dst[n] = src.reshape(-1)[off:off + size].reshape(w_all[n].shape)
        off += size

    return (loss_total, dx.reshape(x.shape), *[grads[n] for n in names], *[delta[n] for n in names],
            *[new_m[n] for n in names], *[new_v[n] for n in names])
```

```python
import math

import jax
import jax.numpy as jnp
from jax import lax
from jax.experimental import pallas as pl
from jax.experimental.pallas import tpu as pltpu

f32 = jnp.float32
bf16 = jnp.bfloat16

D_MODEL = 1024
N_META = 16
HEAD_DIM = 64
Q_HEADS = 8
KV_HEADS = 2
GROUP = Q_HEADS // KV_HEADS
WINDOW = 128
BLOCK = 128
ROPE_THETA = 500000.0
ROPE_DIM = HEAD_DIM // 4
RWKV_HEADS = 8
RWKV_HEAD = 64
RWKV_DIM = RWKV_HEADS * RWKV_HEAD
DECAY_LORA = 64
AAA_LORA = 64
GATE_LORA = 160
LORA_W = DECAY_LORA + AAA_LORA + GATE_LORA
RWKV_LN_EPS = 64e-5
D_FF = 2816
Q_W = Q_HEADS * HEAD_DIM
KV_W = KV_HEADS * HEAD_DIM
ATTN_PROJ = Q_W + 2 * KV_W
RKV_W = 3 * RWKV_DIM
RWKV_PROJ = RKV_W + LORA_W
D_IN = ATTN_PROJ + RWKV_PROJ + 2 * D_MODEL
RMS_EPS = 1e-6
NEG_INF = -1e30
PAD = BLOCK - N_META
FRONT = PAD + N_META

ADAM_LR = 0.001
ADAM_B1 = 0.9
ADAM_B2 = 0.999
ADAM_EPS = 1e-08
ADAM_WD = 0.01
ADAM_STEP = 10

N_CHIPS = 4
N_DEV = 8
CHUNK = 128
VMEM_LIMIT = 56 * 1024 * 1024
MM_ROWS = 704
PACK_W = 1024
MESH = pl.DeviceIdType.MESH


def _tile(m, pref=384):
    for step in (16, 8):
        for t in range(min(m, pref) // step * step, 0, -step):
            if m % t == 0:
                return t
    return m


def _params(sem=None):
    return pltpu.CompilerParams(dimension_semantics=sem, vmem_limit_bytes=VMEM_LIMIT)


def _full(shape):
    nd = len(shape)
    return pl.BlockSpec(shape, lambda *_: (0,) * nd)


def _dot(a, b, dims="nn"):
    dn = {"nn": (((1,), (0,)), ((), ())), "nt": (((1,), (1,)), ((), ())), "tn": (((0,), (0,)), ((), ()))}[dims]
    return lax.dot_general(a.astype(bf16), b.astype(bf16), dn, preferred_element_type=f32)


def _two_pass(x, m, dims="nn"):
    x_hi = x.astype(bf16)
    x_lo = (x - x_hi.astype(f32)).astype(bf16)
    return _dot(x_hi, m, dims) + _dot(x_lo, m, dims)


@jax.custom_vjp
def _dot_const(x, m):
    return _two_pass(x, m)


def _dot_const_fwd(x, m):
    return _two_pass(x, m), m


def _dot_const_bwd(m, ct):
    return _two_pass(ct, m, "nt"), jnp.zeros_like(m)


_dot_const.defvjp(_dot_const_fwd, _dot_const_bwd)


def _two_pass_left(m, x, dims):
    x_hi = x.astype(bf16)
    x_lo = (x - x_hi.astype(f32)).astype(bf16)
    return _dot(m, x_hi, dims) + _dot(m, x_lo, dims)


@jax.custom_vjp
def _const_dot(m, x):
    return _two_pass_left(m, x, "nn")


def _const_dot_fwd(m, x):
    return _two_pass_left(m, x, "nn"), m


def _const_dot_bwd(m, ct):
    return jnp.zeros_like(m), _two_pass_left(m, ct, "tn")


_const_dot.defvjp(_const_dot_fwd, _const_dot_bwd)


def _mm(a, b, mode, *, name, out_dtype=f32, bias=None, add=None, zero_rows_below=0):
    m, _ = a.shape
    n = b.shape[1] if mode == "nn" else b.shape[0]
    tm = _tile(m, MM_ROWS)
    has_bias, has_add = bias is not None, add is not None

    def body(*refs):
        a_ref, b_ref = refs[0], refs[1]
        o_ref = refs[-1]
        acc = _dot(a_ref[...], b_ref[...], mode)
        k = 2
        if has_bias:
            acc = acc + refs[k][...]
            k += 1
        if zero_rows_below:
            rows = pl.program_id(0) * tm + lax.broadcasted_iota(jnp.int32, acc.shape, 0)
            acc = jnp.where(rows >= zero_rows_below, acc, 0.0)
        if has_add:
            acc = acc + refs[k][...].astype(f32)
        o_ref[...] = acc.astype(out_dtype)

    ins = [a, b]
    in_specs = [pl.BlockSpec((tm, a.shape[1]), lambda i: (i, 0)), _full(b.shape)]
    if has_bias:
        ins.append(bias)
        in_specs.append(_full(bias.shape))
    if has_add:
        ins.append(add)
        in_specs.append(pl.BlockSpec((tm, n), lambda i: (i, 0)))
    return pl.pallas_call(
        body, name=name, grid=(m // tm,), in_specs=in_specs,
        out_specs=pl.BlockSpec((tm, n), lambda i: (i, 0)),
        out_shape=jax.ShapeDtypeStruct((m, n), out_dtype),
        compiler_params=_params(("parallel",)),
    )(*ins)


def _pieces(widths):
    out, off = [], 0
    for w in widths:
        out.append((off, w))
        off += w
    return out


def _proj_in(a, w_lr, bias, widths, *, name, zero_rows_below=0):
    m, kdim = a.shape
    half = kdim // 2
    tm = _tile(m, MM_ROWS)

    def body(a_ref, w_ref, b_ref, *outs):
        a_l, a_r = a_ref[:, :half], a_ref[:, half:]
        for (off, width), o_ref in zip(_pieces(widths), outs):
            acc = _dot(a_l, w_ref[0, off:off + width, :], "nt") + _dot(a_r, w_ref[1, off:off + width, :], "nt")
            acc = acc + b_ref[:, off:off + width]
            if zero_rows_below:
                rows = pl.program_id(0) * tm + lax.broadcasted_iota(jnp.int32, acc.shape, 0)
                acc = jnp.where(rows >= zero_rows_below, acc, 0.0)
            o_ref[...] = acc.astype(o_ref.dtype)

    return pl.pallas_call(
        body, name=name, grid=(m // tm,),
        in_specs=[pl.BlockSpec((tm, kdim), lambda i: (i, 0)), _full(w_lr.shape), _full(bias.shape)],
        out_specs=[pl.BlockSpec((tm, w), lambda i: (i, 0)) for w in widths],
        out_shape=[jax.ShapeDtypeStruct((m, w), bf16) for w in widths],
        compiler_params=_params(("parallel",)),
    )(a, w_lr, bias)


def _proj_in_bwd(d_list, w_lr, *, name):
    m = d_list[0].shape[0]
    half = w_lr.shape[2]
    widths = [d.shape[1] for d in d_list]
    tm = _tile(m, MM_ROWS)

    def body(*refs):
        w_ref, o_ref = refs[-2], refs[-1]
        for side in range(2):
            acc = None
            for (off, width), d_ref in zip(_pieces(widths), refs):
                term = _dot(d_ref[...], w_ref[side, off:off + width, :])
                acc = term if acc is None else acc + term
            o_ref[:, side * half:(side + 1) * half] = acc.astype(o_ref.dtype)

    return pl.pallas_call(
        body, name=name, grid=(m // tm,),
        in_specs=[pl.BlockSpec((tm, w), lambda i: (i, 0)) for w in widths] + [_full(w_lr.shape)],
        out_specs=pl.BlockSpec((tm, 2 * half), lambda i: (i, 0)),
        out_shape=jax.ShapeDtypeStruct((m, 2 * half), bf16),
        compiler_params=_params(("parallel",)),
    )(*d_list, w_lr)


def _residual_norm(a, w, res, g, *, name):
    m, d = res.shape
    tm = _tile(m, MM_ROWS)

    def body(a_ref, w_ref, r_ref, g_ref, h_ref, n_ref):
        h = _dot(a_ref[...], w_ref[...]) + r_ref[...]
        h_ref[...] = h
        n_ref[...] = _rms(h, g_ref[...]).astype(n_ref.dtype)

    tile = pl.BlockSpec((tm, d), lambda i: (i, 0))
    return pl.pallas_call(
        body, name=name, grid=(m // tm,),
        in_specs=[pl.BlockSpec((tm, a.shape[1]), lambda i: (i, 0)), _full(w.shape), tile, _full(g.shape)],
        out_specs=[tile, tile],
        out_shape=[jax.ShapeDtypeStruct((m, d), f32), jax.ShapeDtypeStruct((m, d), bf16)],
        compiler_params=_params(("parallel",)),
    )(a, w, res, g)


def _residual_norm_bwd(d_list, w_list, h, g, dh_out, *, name):
    m, d = h.shape
    k = len(d_list)
    tm = _tile(m)

    def body(*refs):
        h_ref, g_ref, dho_ref, dh_ref, dg_ref = refs[2 * k:]
        dn = _dot(refs[0][...], refs[k][...])
        for i in range(1, k):
            dn = dn + _dot(refs[i][...], refs[k + i][...])
        _, vjp = jax.vjp(lambda hv, gv: (_rms(hv, gv), hv), h_ref[...], g_ref[...])
        dh, dg = vjp((dn, dho_ref[...]))
        dh_ref[...] = dh

        @pl.when(pl.program_id(0) == 0)
        def _():
            dg_ref[...] = jnp.zeros_like(dg_ref)

        dg_ref[...] += dg

    tile = pl.BlockSpec((tm, d), lambda i: (i, 0))
    return pl.pallas_call(
        body, name=name, grid=(m // tm,),
        in_specs=[pl.BlockSpec((tm, a.shape[1]), lambda i: (i, 0)) for a in d_list] + [_full(w.shape) for w in w_list]
        + [tile, _full(g.shape), tile],
        out_specs=[tile, _full(g.shape)],
        out_shape=[jax.ShapeDtypeStruct((m, d), f32), jax.ShapeDtypeStruct(g.shape, f32)],
        compiler_params=_params(("arbitrary",)),
    )(*d_list, *w_list, h, g, dh_out)


def _mm_tn(a, b, *, name, colsum=False, out_dtype=bf16):
    r, m = a.shape
    n = b.shape[1]
    tr = _tile(r, 1408)
    tmo = m
    for cand in (1408, 1024, 768, 512):
        if m > 1024 and m % cand == 0:
            tmo = cand
            break
    steps = r // tr

    def body(a_ref, b_ref, o_ref, *rest):
        acc = rest[-1]
        i = pl.program_id(1)

        @pl.when(i == 0)
        def _():
            acc[...] = jnp.zeros_like(acc)
            if colsum:
                rest[0][...] = jnp.zeros_like(rest[0])

        acc[...] += _dot(a_ref[...], b_ref[...], "tn")
        if colsum:
            rest[0][...] += jnp.sum(a_ref[...].astype(f32), axis=0, keepdims=True)

        @pl.when(i == steps - 1)
        def _():
            o_ref[...] = acc[...].astype(out_dtype)

    out_shape = [jax.ShapeDtypeStruct((m, n), out_dtype)]
    out_specs = [pl.BlockSpec((tmo, n), lambda j, i: (j, 0))]
    if colsum:
        out_shape.append(jax.ShapeDtypeStruct((1, m), f32))
        out_specs.append(pl.BlockSpec((1, tmo), lambda j, i: (0, j)))
    res = pl.pallas_call(
        body, name=name, grid=(m // tmo, steps),
        in_specs=[pl.BlockSpec((tr, tmo), lambda j, i: (i, j)), pl.BlockSpec((tr, n), lambda j, i: (i, 0))],
        out_specs=out_specs, out_shape=out_shape,
        scratch_shapes=[pltpu.VMEM((tmo, n), f32)],
        compiler_params=_params(("parallel", "arbitrary")),
    )(a, b)
    return res if colsum else res[0]


def _rowwise(fn, rows, params, outs, *, name, tm=None):
    m = rows[0].shape[0]
    tm = tm or _tile(m)
    nr, npar = len(rows), len(params)

    def body(*refs):
        vals = [r[...] for r in refs[:nr + npar]]
        res = fn(*vals)
        for o_ref, v in zip(refs[nr + npar:], res):
            o_ref[...] = v.astype(o_ref.dtype)

    return pl.pallas_call(
        body, name=name, grid=(m // tm,),
        in_specs=[pl.BlockSpec((tm, r.shape[1]), lambda i: (i, 0)) for r in rows] + [_full(p.shape) for p in params],
        out_specs=[pl.BlockSpec((tm, w), lambda i: (i, 0)) for w, _ in outs],
        out_shape=[jax.ShapeDtypeStruct((m, w), dt) for w, dt in outs],
        compiler_params=_params(("parallel",)),
    )(*rows, *params)


def _rowwise_bwd(fn, rows, params, cts, *, name, diff_rows, diff_params, tm=None, zero_rows_below=0, out_dtypes=None):
    m = rows[0].shape[0]
    tm = tm or _tile(m)
    nr, npar = len(rows), len(params)
    d_idx = [i for i in range(nr) if diff_rows[i]]
    p_idx = [i for i in range(npar) if diff_params[i]]
    out_dtypes = out_dtypes or [f32] * len(d_idx)
    flat_cts = [c for group in cts for c in group]
    n_ct = len(flat_cts)

    def body(*refs):
        vals = [r[...] for r in refs[:nr + npar]]
        ct_refs = refs[nr + npar:nr + npar + n_ct]
        out_refs = refs[nr + npar + n_ct:]
        ct_vals, k = [], 0
        for group in cts:
            acc = ct_refs[k][...].astype(f32)
            for extra in range(1, len(group)):
                acc = acc + ct_refs[k + extra][...].astype(f32)
            k += len(group)
            if zero_rows_below:
                rr = pl.program_id(0) * tm + lax.broadcasted_iota(jnp.int32, acc.shape, 0)
                acc = jnp.where(rr >= zero_rows_below, acc, 0.0)
            ct_vals.append(acc)

        def g(*dargs):
            full = list(vals)
            for pos, i in enumerate(d_idx):
                full[i] = dargs[pos]
            for pos, i in enumerate(p_idx):
                full[nr + i] = dargs[len(d_idx) + pos]
            return tuple(fn(*full))

        _, vjp = jax.vjp(g, *[vals[i].astype(f32) for i in d_idx], *[vals[nr + i] for i in p_idx])
        grads = vjp(tuple(ct_vals))
        for pos in range(len(d_idx)):
            out_refs[pos][...] = grads[pos].astype(out_refs[pos].dtype)
        first = pl.program_id(0) == 0
        for pos in range(len(p_idx)):
            o_ref = out_refs[len(d_idx) + pos]

            @pl.when(first)
            def _(o_ref=o_ref):
                o_ref[...] = jnp.zeros_like(o_ref)

            o_ref[...] += grads[len(d_idx) + pos]

    return pl.pallas_call(
        body, name=name, grid=(m // tm,),
        in_specs=[pl.BlockSpec((tm, r.shape[1]), lambda i: (i, 0)) for r in rows] + [_full(p.shape) for p in params]
        + [pl.BlockSpec((tm, c.shape[1]), lambda i: (i, 0)) for c in flat_cts],
        out_specs=[pl.BlockSpec((tm, rows[i].shape[1]), lambda i_: (i_, 0)) for i in d_idx]
        + [_full(params[i].shape) for i in p_idx],
        out_shape=[jax.ShapeDtypeStruct(rows[i].shape, dt) for i, dt in zip(d_idx, out_dtypes)]
        + [jax.ShapeDtypeStruct(params[i].shape, f32) for i in p_idx],
        compiler_params=_params(("arbitrary",)),
    )(*rows, *params, *flat_cts)


def _rms(x, g):
    return x * lax.rsqrt(jnp.mean(x * x, axis=-1, keepdims=True) + RMS_EPS) * g


def _head_sum_matrix(width, head):
    idx = jnp.arange(width) // head
    return (idx[:, None] == idx[None, :]).astype(f32)


def _rope_tables(lp):
    half = ROPE_DIM // 2
    pos = (jnp.arange(lp) - PAD).astype(f32)
    inv_freq = jnp.power(jnp.float32(ROPE_THETA), -jnp.arange(half, dtype=f32) * (2.0 / ROPE_DIM))
    ang = pos[:, None] * inv_freq[None, :]
    cos, sin = jnp.cos(ang), jnp.sin(ang)
    ones = jnp.ones((lp, HEAD_DIM - ROPE_DIM), f32)
    zeros = jnp.zeros((lp, HEAD_DIM - ROPE_DIM), f32)
    cos_t = jnp.concatenate([cos, cos, ones], axis=1)
    sin_t = jnp.concatenate([-sin, sin, zeros], axis=1)
    i = jnp.arange(HEAD_DIM)
    src = jnp.where(i < half, i + half, jnp.where(i < ROPE_DIM, i - half, i))
    swap = ((i[:, None] == src[None, :]) & (i[None, :] < ROPE_DIM)).astype(f32)
    return cos_t, sin_t, swap


def _attn_prep(qkv, cos_t, sin_t, swap):
    outs = []
    for h in range(Q_HEADS + KV_HEADS):
        t = qkv[:, h * HEAD_DIM:(h + 1) * HEAD_DIM]
        outs.append(t * cos_t + _dot_const(t, swap) * sin_t)
    q = jnp.concatenate(outs[:Q_HEADS], axis=1)
    k = jnp.concatenate(outs[Q_HEADS:], axis=1)
    return q, k, qkv[:, Q_W + KV_W:]


def _softplus(z):
    return jnp.maximum(z, 0.0) + jnp.log1p(jnp.exp(-jnp.abs(z)))


def _rwkv_prep(rkv, lora, w0, w2, a0, a2, g2, k_k, k_a, hsum):
    r = rkv[:, :RWKV_DIM]
    k = rkv[:, RWKV_DIM:2 * RWKV_DIM]
    v = rkv[:, 2 * RWKV_DIM:]
    dw = lora[:, :DECAY_LORA]
    da = lora[:, DECAY_LORA:DECAY_LORA + AAA_LORA]
    dg = lora[:, DECAY_LORA + AAA_LORA:]
    w = -_softplus(-(w0 + _dot(jnp.tanh(dw), w2))) - 0.5
    a = jax.nn.sigmoid(a0 + _dot(da, a2))
    g = _dot(jax.nn.sigmoid(dg), g2)
    kk = k * k_k
    kk = kk * lax.rsqrt(jnp.maximum(_dot_const(kk * kk, hsum), 1e-24))
    k = k * (1.0 + (a - 1.0) * k_a)
    log_decay = -jnp.exp(w)
    return r, log_decay, k, v, -kk, kk * a, g


def _rwkv_post(y, r, k, v, g, ln_w, ln_b, r_k, hmean):
    hsum = hmean * RWKV_HEAD
    mean = _dot_const(y, hmean)
    yc = y - mean
    var = _dot_const(yc * yc, hmean)
    yn = yc * lax.rsqrt(var + RWKV_LN_EPS) * ln_w + ln_b
    bonus = _dot_const(r * k * r_k, hsum) * v
    return ((yn + bonus) * g,)


def _merge(gates, br_a, br_r):
    sg = jax.nn.sigmoid(gates)
    return (sg[:, :D_MODEL] * br_a + sg[:, D_MODEL:] * br_r,)


def _swiglu(gate, up):
    return (jax.nn.silu(gate) * up,)


def _ffn_in(f, w_gate_t, w_up_t, *, name):
    m, d = f.shape
    n = w_gate_t.shape[0]
    tm = _tile(m)

    def body(f_ref, wg_ref, wu_ref, g_ref, u_ref, a_ref):
        g = _dot(f_ref[...], wg_ref[...], "nt")
        u = _dot(f_ref[...], wu_ref[...], "nt")
        g_ref[...] = g.astype(g_ref.dtype)
        u_ref[...] = u.astype(u_ref.dtype)
        a_ref[...] = _swiglu(g, u)[0].astype(a_ref.dtype)

    spec = pl.BlockSpec((tm, n), lambda i: (i, 0))
    return pl.pallas_call(
        body, name=name, grid=(m // tm,),
        in_specs=[pl.BlockSpec((tm, d), lambda i: (i, 0)), _full(w_gate_t.shape), _full(w_up_t.shape)],
        out_specs=[spec] * 3, out_shape=[jax.ShapeDtypeStruct((m, n), bf16)] * 3,
        compiler_params=_params(("parallel",)),
    )(f, w_gate_t, w_up_t)


def _branch_merge(y_attn, y_rwkv, w_attn_t, w_rwkv_t, gates, *, name):
    m = y_attn.shape[0]
    tm = _tile(m, MM_ROWS)

    def body(ya_ref, yr_ref, wa_ref, wr_ref, g_ref, a_ref, r_ref, o_ref):
        br_a = _dot(ya_ref[...], wa_ref[...], "nt")
        br_r = _dot(yr_ref[...], wr_ref[...], "nt")
        a_ref[...] = br_a.astype(a_ref.dtype)
        r_ref[...] = br_r.astype(r_ref.dtype)
        o_ref[...] = _merge(g_ref[...].astype(f32), br_a, br_r)[0].astype(o_ref.dtype)

    rows = lambda a: pl.BlockSpec((tm, a.shape[1]), lambda i: (i, 0))
    spec = pl.BlockSpec((tm, D_MODEL), lambda i: (i, 0))
    return pl.pallas_call(
        body, name=name, grid=(m // tm,),
        in_specs=[rows(y_attn), rows(y_rwkv), _full(w_attn_t.shape), _full(w_rwkv_t.shape), rows(gates)],
        out_specs=[spec] * 3, out_shape=[jax.ShapeDtypeStruct((m, D_MODEL), bf16)] * 3,
        compiler_params=_params(("parallel",)),
    )(y_attn, y_rwkv, w_attn_t, w_rwkv_t, gates)


def _branch_merge_bwd(dh, w_o, gates, br_a, br_r, *, name):
    m = dh.shape[0]
    tm = _tile(m)

    def body(dh_ref, w_ref, g_ref, a_ref, r_ref, dg_ref, da_ref, dr_ref):
        dmerged = _dot(dh_ref[...], w_ref[...], "nt")
        _, vjp = jax.vjp(lambda g, a, r: _merge(g, a, r)[0], g_ref[...].astype(f32), a_ref[...].astype(f32),
                         r_ref[...].astype(f32))
        dg, da, dr = vjp(dmerged)
        dg_ref[...] = dg.astype(dg_ref.dtype)
        da_ref[...] = da.astype(da_ref.dtype)
        dr_ref[...] = dr.astype(dr_ref.dtype)

    rows = lambda a: pl.BlockSpec((tm, a.shape[1]), lambda i: (i, 0))
    return pl.pallas_call(
        body, name=name, grid=(m // tm,),
        in_specs=[rows(dh), _full(w_o.shape), rows(gates), rows(br_a), rows(br_r)],
        out_specs=[rows(gates), rows(br_a), rows(br_r)],
        out_shape=[jax.ShapeDtypeStruct(gates.shape, bf16), jax.ShapeDtypeStruct(br_a.shape, bf16),
                   jax.ShapeDtypeStruct(br_r.shape, bf16)],
        compiler_params=_params(("parallel",)),
    )(dh, w_o, gates, br_a, br_r)


def _ffn_in_bwd(dh, w_down, gate, up, *, name):
    m, d = dh.shape
    n = w_down.shape[0]
    tm = _tile(m)

    def body(dh_ref, w_ref, g_ref, u_ref, dg_ref, du_ref):
        dact = _dot(dh_ref[...], w_ref[...], "nt")
        _, vjp = jax.vjp(lambda a, b: _swiglu(a, b)[0], g_ref[...].astype(f32), u_ref[...].astype(f32))
        dg, du = vjp(dact)
        dg_ref[...] = dg.astype(dg_ref.dtype)
        du_ref[...] = du.astype(du_ref.dtype)

    spec = pl.BlockSpec((tm, n), lambda i: (i, 0))
    return pl.pallas_call(
        body, name=name, grid=(m // tm,),
        in_specs=[pl.BlockSpec((tm, d), lambda i: (i, 0)), _full(w_down.shape), spec, spec],
        out_specs=[spec] * 2, out_shape=[jax.ShapeDtypeStruct((m, n), bf16)] * 2,
        compiler_params=_params(("parallel",)),
    )(dh, w_down, gate, up)


HALO = 16


def _previous_rows(x, before_ref, first_tile):
    rows = lax.broadcasted_iota(jnp.int32, x.shape, 0)
    last = jnp.where(first_tile, 0.0, before_ref[HALO - 1:HALO, :].astype(f32))
    return jnp.where(rows == 0, last, pltpu.roll(x, 1, axis=0))


def _mixer_inputs(ps, mixes, params, *, name):
    m = ps[0].shape[0]
    tm = _tile(m)
    sub = tm // HALO
    n_par = len(params)

    def body(*refs):
        first = pl.program_id(0) == 0
        pf = []
        for k in range(2):
            x = refs[k][...].astype(f32)
            pf.append(x + (_previous_rows(x, refs[2 + k], first) - x) * refs[4 + k][...])
        res = _rwkv_prep(*pf, *[ref[...] for ref in refs[6:6 + n_par]])
        for o_ref, val in zip(refs[6 + n_par:], res):
            o_ref[...] = val

    tile = lambda a: pl.BlockSpec((tm, a.shape[1]), lambda i: (i, 0))
    before = lambda a: pl.BlockSpec((HALO, a.shape[1]), lambda i: (jnp.maximum(i * sub - 1, 0), 0))
    out = pl.BlockSpec((tm, RWKV_DIM), lambda i: (i, 0))
    return pl.pallas_call(
        body, name=name, grid=(m // tm,),
        in_specs=[tile(a) for a in ps] + [before(a) for a in ps] + [_full(a.shape) for a in mixes + params],
        out_specs=[out] * 7, out_shape=[jax.ShapeDtypeStruct((m, RWKV_DIM), f32)] * 7,
        compiler_params=_params(("parallel",)),
    )(*ps, *ps, *mixes, *params)


def _mixer_inputs_bwd(ps, mixes, params, cts, *, name):
    m = ps[0].shape[0]
    tm = _tile(m)
    sub = tm // HALO
    nt = m // tm
    n_par = len(params)
    flat_cts = [c for group in cts for c in group]
    n_ct = len(flat_cts)

    def body(*refs):
        i = pl.program_id(0)
        tile_index = nt - 1 - i
        ct_refs = refs[6 + n_par:6 + n_par + n_ct]
        dp_refs = refs[6 + n_par + n_ct:8 + n_par + n_ct]
        dmix_refs = refs[8 + n_par + n_ct:10 + n_par + n_ct]
        dpar_refs = refs[10 + n_par + n_ct:9 + 2 * n_par + n_ct]
        carries = refs[9 + 2 * n_par + n_ct:]
        rows1 = tile_index * tm + lax.broadcasted_iota(jnp.int32, (tm, 1), 0)
        live = rows1 >= PAD

        @pl.when(i == 0)
        def _():
            for ref in (*dmix_refs, *dpar_refs, *carries):
                ref[...] = jnp.zeros_like(ref)

        xs, prevs, pf = [], [], []
        for k in range(2):
            x = refs[k][...].astype(f32)
            xp = _previous_rows(x, refs[2 + k], tile_index == 0)
            xs.append(x)
            prevs.append(xp)
            pf.append(x + (xp - x) * refs[4 + k][...])
        ct_vals, pos = [], 0
        for group in cts:
            acc = ct_refs[pos][...].astype(f32)
            for extra in range(1, len(group)):
                acc = acc + ct_refs[pos + extra][...].astype(f32)
            pos += len(group)
            ct_vals.append(jnp.where(live, acc, 0.0))
        par_vals = [ref[...] for ref in refs[6:6 + n_par]]
        _, vjp = jax.vjp(lambda *args: _rwkv_prep(*args, par_vals[-1]), *pf, *par_vals[:-1])
        g = vjp(tuple(ct_vals))
        for k in range(2):
            dpf = g[k]
            mixv = refs[4 + k][...]
            dm = dpf * mixv
            rows = lax.broadcasted_iota(jnp.int32, dm.shape, 0)
            dm_next = jnp.where(rows == tm - 1, carries[k][...], pltpu.roll(dm, tm - 1, axis=0))
            dp_refs[k][...] = jnp.where(live, dpf - dm + dm_next, 0.0).astype(dp_refs[k].dtype)
            carries[k][...] = dm[0:1, :]
            dmix_refs[k][...] += jnp.sum(dpf * (prevs[k] - xs[k]), axis=0, keepdims=True)
        for ref, val in zip(dpar_refs, g[2:]):
            ref[...] += val

    tile = lambda a: pl.BlockSpec((tm, a.shape[1]), lambda i: (nt - 1 - i, 0))
    before = lambda a: pl.BlockSpec((HALO, a.shape[1]), lambda i: (jnp.maximum((nt - 1 - i) * sub - 1, 0), 0))
    return pl.pallas_call(
        body, name=name, grid=(nt,),
        in_specs=[tile(a) for a in ps] + [before(a) for a in ps] + [_full(a.shape) for a in mixes + params]
        + [tile(c) for c in flat_cts],
        out_specs=[tile(a) for a in ps] + [_full(a.shape) for a in mixes + params[:-1]],
        out_shape=[jax.ShapeDtypeStruct(a.shape, bf16) for a in ps]
        + [jax.ShapeDtypeStruct(a.shape, f32) for a in mixes + params[:-1]],
        scratch_shapes=[pltpu.VMEM((1, a.shape[1]), f32) for a in ps],
        compiler_params=_params(("arbitrary",)),
    )(*ps, *ps, *mixes, *params, *flat_cts)


def _attn_masks(blk):
    qi = lax.broadcasted_iota(jnp.int32, (BLOCK, BLOCK), 0)
    ki = lax.broadcasted_iota(jnp.int32, (BLOCK, BLOCK), 1)
    qpos = blk * BLOCK + qi - PAD
    kpos_c = blk * BLOCK + ki - PAD
    kpos_p = kpos_c - BLOCK
    kpos_m = ki - PAD

    def band(kpos):
        return (kpos >= N_META) & (kpos <= qpos) & (qpos - kpos < WINDOW)

    return band(kpos_p), band(kpos_c), (kpos_m >= 0) & (kpos_m <= qpos)


def _attn_probs(qs, k3s, sink, oks):
    s = [[jnp.where(ok, _dot(qh, kx, "nt"), NEG_INF) for kx, ok in zip(k3, oks)] for qh, k3 in zip(qs, k3s)]
    mx = [jnp.maximum(jnp.maximum(jnp.max(t[0], -1, keepdims=True), jnp.max(t[1], -1, keepdims=True)),
                      jnp.maximum(jnp.max(t[2], -1, keepdims=True), sk)) for t, sk in zip(s, sink)]
    e = [[jnp.exp(tx - m) for tx in t] for t, m in zip(s, mx)]
    e_sink = [jnp.exp(sk - m) for sk, m in zip(sink, mx)]
    inv = [1.0 / (jnp.sum(t[0], -1, keepdims=True) + jnp.sum(t[1], -1, keepdims=True)
                  + jnp.sum(t[2], -1, keepdims=True) + es) for t, es in zip(e, e_sink)]
    return [[tx * i for tx in t] for t, i in zip(e, inv)], [es * i for es, i in zip(e_sink, inv)]


def _head_cols(i):
    return slice(i * HEAD_DIM, (i + 1) * HEAD_DIM)


def _attn_operands(refs):
    q_ref, kp_ref, kc_ref, km_ref, vp_ref, vc_ref, vm_ref, s_ref = refs
    qs = [q_ref[:, _head_cols(i)] * (HEAD_DIM ** -0.5) for i in range(Q_HEADS)]
    k3 = [[ref[:, _head_cols(h)] for ref in (kp_ref, kc_ref, km_ref)] for h in range(KV_HEADS)]
    v3 = [[ref[:, _head_cols(h)] for ref in (vp_ref, vc_ref, vm_ref)] for h in range(KV_HEADS)]
    return (qs, [k3[i // GROUP] for i in range(Q_HEADS)], [v3[i // GROUP] for i in range(Q_HEADS)],
            [s_ref[:, i:i + 1] for i in range(Q_HEADS)])


def _attention(q, k, v, sinks, *, name):
    lp = q.shape[0]
    nb = lp // BLOCK
    prev = lambda i: (jnp.maximum(i - 1, 0), 0)
    cur = lambda i: (i, 0)
    meta = lambda i: (0, 0)
    kv = lambda index: pl.BlockSpec((BLOCK, KV_W), index)

    def body(*refs):
        o_ref = refs[-1]
        qs, k3s, v3s, sink = _attn_operands(refs[:-1])
        p, _ = _attn_probs(qs, k3s, sink, _attn_masks(pl.program_id(0)))
        out = [_dot(ph[0], v3[0]) + _dot(ph[1], v3[1]) + _dot(ph[2], v3[2]) for ph, v3 in zip(p, v3s)]
        for i in range(Q_HEADS):
            o_ref[:, _head_cols(i)] = out[i].astype(o_ref.dtype)

    return pl.pallas_call(
        body, name=name, grid=(nb,),
        in_specs=[pl.BlockSpec((BLOCK, Q_W), cur), kv(prev), kv(cur), kv(meta), kv(prev), kv(cur), kv(meta),
                  _full((1, Q_HEADS))],
        out_specs=pl.BlockSpec((BLOCK, Q_W), cur),
        out_shape=jax.ShapeDtypeStruct((lp, Q_W), bf16),
        compiler_params=_params(("parallel",)),
    )(q, k, k, k, v, v, v, sinks)


def _attention_bwd(q, k, v, sinks, out, do, *, name):
    lp = q.shape[0]
    nb = lp // BLOCK
    cur = lambda n: (jnp.minimum(n, nb - 1), 0)
    prev = lambda n: (jnp.maximum(jnp.minimum(n, nb - 1) - 1, 0), 0)
    behind = lambda n: (jnp.maximum(n - 1, 0), 0)
    meta = lambda n: (0, 0)
    kv = lambda index: pl.BlockSpec((BLOCK, KV_W), index)
    scale = HEAD_DIM ** -0.5

    def body(*refs):
        ins, fwd_ref, do_ref = refs[:8], refs[8], refs[9]
        dq_ref, dk_ref, dv_ref, dkm_ref, dvm_ref, ds_ref, carry_k, carry_v = refs[10:]
        n = pl.program_id(0)

        @pl.when(n == 0)
        def _():
            for ref in (dkm_ref, dvm_ref, ds_ref, carry_k, carry_v):
                ref[...] = jnp.zeros_like(ref)

        @pl.when(n < nb)
        def _():
            qs, k3s, v3s, sink = _attn_operands(ins)
            do = [do_ref[:, _head_cols(i)] for i in range(Q_HEADS)]
            p, p_sink = _attn_probs(qs, k3s, sink, _attn_masks(n))
            delta = [jnp.sum(d * fwd_ref[:, _head_cols(i)].astype(f32), -1, keepdims=True) for i, d in enumerate(do)]
            dp = [[_dot(d, vx, "nt") for vx in v3] for d, v3 in zip(do, v3s)]
            ds = [[px * (dx - dl) for px, dx in zip(ph, dh)] for ph, dh, dl in zip(p, dp, delta)]
            dq = [_dot(dsh[0], k3[0]) + _dot(dsh[1], k3[1]) + _dot(dsh[2], k3[2]) for dsh, k3 in zip(ds, k3s)]
            for i in range(Q_HEADS):
                dq_ref[:, _head_cols(i)] = dq[i] * scale
                ds_ref[:, i:i + 1] -= jnp.sum(p_sink[i] * delta[i], axis=0, keepdims=True)
            for h in range(KV_HEADS):
                group = slice(h * GROUP, (h + 1) * GROUP)
                q_all = jnp.concatenate(qs[group], axis=0)
                do_all = jnp.concatenate(do[group], axis=0)
                dk3 = [_dot(jnp.concatenate([dsh[x] for dsh in ds[group]], axis=0), q_all, "tn") for x in range(3)]
                dv3 = [_dot(jnp.concatenate([ph[x] for ph in p[group]], axis=0), do_all, "tn") for x in range(3)]
                hs = _head_cols(h)
                for out_ref, carry, meta_ref, d3 in ((dk_ref, carry_k, dkm_ref, dk3),
                                                     (dv_ref, carry_v, dvm_ref, dv3)):
                    out_ref[:, hs] = carry[:, hs] + d3[0]
                    carry[:, hs] = d3[1]
                    meta_ref[:, hs] += d3[2]

        @pl.when(n == nb)
        def _():
            dk_ref[...] = carry_k[...]
            dv_ref[...] = carry_v[...]

    kv_shape = jax.ShapeDtypeStruct((lp, KV_W), f32)
    one_shape = jax.ShapeDtypeStruct((BLOCK, KV_W), f32)
    return pl.pallas_call(
        body, name=name, grid=(nb + 1,),
        in_specs=[pl.BlockSpec((BLOCK, Q_W), cur), kv(prev), kv(cur), kv(meta), kv(prev), kv(cur), kv(meta),
                  _full((1, Q_HEADS)), pl.BlockSpec((BLOCK, Q_W), cur), pl.BlockSpec((BLOCK, Q_W), cur)],
        out_specs=[pl.BlockSpec((BLOCK, Q_W), cur), kv(behind), kv(behind), kv(meta), kv(meta),
                   _full((1, Q_HEADS))],
        out_shape=[jax.ShapeDtypeStruct((lp, Q_W), f32), kv_shape, kv_shape, one_shape, one_shape,
                   jax.ShapeDtypeStruct((1, Q_HEADS), f32)],
        scratch_shapes=[pltpu.VMEM((BLOCK, KV_W), f32), pltpu.VMEM((BLOCK, KV_W), f32)],
        compiler_params=_params(("arbitrary",)),
    )(q, k, k, k, v, v, v, sinks, out, do)


@jax.custom_vjp
def _known_inverse(l, x):
    return x


def _known_inverse_fwd(l, x):
    return x, x


def _known_inverse_bwd(x, ct):
    return _dot(_dot(x, ct, "tn"), x, "nt"), jnp.zeros_like(x)


_known_inverse.defvjp(_known_inverse_fwd, _known_inverse_bwd)


@jax.custom_vjp
def _decayed(x, c):
    return (x * jnp.exp(c)).astype(bf16).astype(f32)


def _decayed_fwd(x, c):
    e = jnp.exp(c)
    out = (x * e).astype(bf16).astype(f32)
    return out, (e, out)


def _decayed_bwd(res, ct):
    e, out = res
    return ct * e, ct * out


_decayed.defvjp(_decayed_fwd, _decayed_bwd)


@jax.custom_vjp
def _pair(x, y):
    return _dot(x, y, "nt")


def _pair_fwd(x, y):
    return _dot(x, y, "nt"), (x, y)


def _pair_bwd(res, ct):
    x, y = res
    hi = ct.astype(bf16)
    lo = (ct - hi.astype(f32)).astype(bf16)
    return _dot(hi, y) + _dot(lo, y), _dot(hi, x, "tn") + _dot(lo, x, "tn")


_pair.defvjp(_pair_fwd, _pair_bwd)


def _scan_chunk(s0, r, lw, k, v, a, b, inv=None):
    t = r[0].shape[0]
    ii = lax.broadcasted_iota(jnp.int32, (t, t), 0)
    jj = lax.broadcasted_iota(jnp.int32, (t, t), 1)
    incl = jj <= ii
    strict = jj < ii
    tri = incl.astype(f32)
    eye = jnp.where(ii == jj, 1.0, 0.0)
    cl = [_const_dot(tri, x) for x in lw]
    mid = [c[t // 2 - 1:t // 2, :] for c in cl]
    s0 = [s * jnp.exp(m) for s, m in zip(s0, mid)]
    cl = [c - m for c, m in zip(cl, mid)]
    rt = [_decayed(x, c) for x, c in zip(r, cl)]
    at = [_decayed(x, c - l) for x, c, l in zip(a, cl, lw)]
    bt = [_decayed(x, -c) for x, c in zip(b, cl)]
    kt = [_decayed(x, -c) for x, c in zip(k, cl)]
    l_ab = [jnp.where(strict, _pair(x, y), 0.0) for x, y in zip(at, bt)]
    l_ak = [jnp.where(strict, _pair(x, y), 0.0) for x, y in zip(at, kt)]
    r_b = [jnp.where(incl, _pair(x, y), 0.0) for x, y in zip(rt, bt)]
    r_k = [jnp.where(incl, _pair(x, y), 0.0) for x, y in zip(rt, kt)]
    if inv is None:
        inv = [eye + x for x in l_ab]
        pw = l_ab
        for _ in range(int(math.log2(t)) - 1):
            pw = [_dot(x, x) for x in pw]
            inv = [x + _dot(x, y) for x, y in zip(inv, pw)]
    else:
        inv = [_known_inverse(x, y) for x, y in zip(l_ab, inv)]
    rhs = [_dot(x, s, "nt") + _dot(m, y) for x, s, m, y in zip(at, s0, l_ak, v)]
    u = [_dot(x, y) for x, y in zip(inv, rhs)]
    y_s = [_dot(x, s, "nt") for x, s in zip(rt, s0)]
    y = [ys + _dot(m, uu) + _dot(n, vv) for ys, m, uu, n, vv in zip(y_s, r_b, u, r_k, v)]
    grow = [s + _dot(uu, x, "tn") + _dot(vv, z, "tn") for s, uu, x, vv, z in zip(s0, u, bt, v, kt)]
    s1 = [g * jnp.exp(c[t - 1:t, :]) for g, c in zip(grow, cl)]
    return y, s1, inv


def _head_rows(h):
    return slice(h * RWKV_HEAD, (h + 1) * RWKV_HEAD)


def _per_head(ref):
    return [ref[:, _head_rows(h)] for h in range(RWKV_HEADS)]


def _scan(r, lw, k, v, a, b, *, name):
    lp = r.shape[0]
    nc = lp // CHUNK
    row = pl.BlockSpec((CHUNK, RWKV_DIM), lambda c: (c, 0))

    def body(r_ref, lw_ref, k_ref, v_ref, a_ref, b_ref, y_ref, s_ref, inv_ref, state):
        @pl.when(pl.program_id(0) == 0)
        def _():
            state[...] = jnp.zeros_like(state)

        s_ref[...] = state[...]
        s0 = [state[_head_rows(h), :] for h in range(RWKV_HEADS)]
        y, s1, inv = _scan_chunk(s0, *[_per_head(ref) for ref in (r_ref, lw_ref, k_ref, v_ref, a_ref, b_ref)])
        for h in range(RWKV_HEADS):
            y_ref[:, _head_rows(h)] = y[h]
            state[_head_rows(h), :] = s1[h]
            inv_ref[h * CHUNK:(h + 1) * CHUNK, :] = inv[h].astype(inv_ref.dtype)

    return pl.pallas_call(
        body, name=name, grid=(nc,), in_specs=[row] * 6,
        out_specs=[row, pl.BlockSpec((RWKV_DIM, RWKV_HEAD), lambda c: (c, 0)),
                   pl.BlockSpec((RWKV_HEADS * CHUNK, CHUNK), lambda c: (c, 0))],
        out_shape=[jax.ShapeDtypeStruct((lp, RWKV_DIM), f32), jax.ShapeDtypeStruct((nc * RWKV_DIM, RWKV_HEAD), f32),
                   jax.ShapeDtypeStruct((nc * RWKV_HEADS * CHUNK, CHUNK), bf16)],
        scratch_shapes=[pltpu.VMEM((RWKV_DIM, RWKV_HEAD), f32)],
        compiler_params=_params(("arbitrary",)),
    )(r, lw, k, v, a, b)


def _scan_bwd(r, lw, k, v, a, b, states, inverses, dy, *, name):
    lp = r.shape[0]
    nc = lp // CHUNK
    back = lambda c: (nc - 1 - c, 0)
    row = pl.BlockSpec((CHUNK, RWKV_DIM), back)

    def body(r_ref, lw_ref, k_ref, v_ref, a_ref, b_ref, s_ref, inv_ref, dy_ref,
             dr_ref, dlw_ref, dk_ref, dv_ref, da_ref, db_ref, dstate):
        @pl.when(pl.program_id(0) == 0)
        def _():
            dstate[...] = jnp.zeros_like(dstate)

        outs = (dr_ref, dlw_ref, dk_ref, dv_ref, da_ref, db_ref)
        s0 = [s_ref[_head_rows(h), :] for h in range(RWKV_HEADS)]
        inv = [inv_ref[h * CHUNK:(h + 1) * CHUNK, :].astype(f32) for h in range(RWKV_HEADS)]
        _, vjp = jax.vjp(lambda *args: _scan_chunk(*args, inv=inv)[:2], s0,
                         *[_per_head(ref) for ref in (r_ref, lw_ref, k_ref, v_ref, a_ref, b_ref)])
        g = vjp((_per_head(dy_ref), [dstate[_head_rows(h), :] for h in range(RWKV_HEADS)]))
        for h in range(RWKV_HEADS):
            dstate[_head_rows(h), :] = g[0][h]
            for o_ref, gv in zip(outs, g[1:]):
                o_ref[:, _head_rows(h)] = gv[h]

    shape = jax.ShapeDtypeStruct((lp, RWKV_DIM), f32)
    return pl.pallas_call(
        body, name=name, grid=(nc,),
        in_specs=[row] * 6 + [pl.BlockSpec((RWKV_DIM, RWKV_HEAD), back),
                              pl.BlockSpec((RWKV_HEADS * CHUNK, CHUNK), back), row],
        out_specs=[row] * 6, out_shape=[shape] * 6,
        scratch_shapes=[pltpu.VMEM((RWKV_DIM, RWKV_HEAD), f32)],
        compiler_params=_params(("arbitrary",)),
    )(r, lw, k, v, a, b, states, inverses, dy)


def _loss_head(h2, target, g_final, *, name):
    lp = h2.shape[0]
    tm = BLOCK
    front_tiles = FRONT // tm

    def body(h_ref, t_ref, g_ref, loss_ref, dh_ref, dg_ref):
        i = pl.program_id(0)
        real = i >= front_tiles

        def tile_loss(hv, gv):
            err = _rms(hv, gv) - t_ref[...]
            return jnp.where(real, 0.5 * jnp.sum(jnp.mean(err * err, axis=-1, keepdims=True)), 0.0)

        loss, (dh, dg) = jax.value_and_grad(tile_loss, argnums=(0, 1))(h_ref[...], g_ref[...])

        @pl.when(i == 0)
        def _():
            loss_ref[...] = jnp.zeros_like(loss_ref)
            dg_ref[...] = jnp.zeros_like(dg_ref)

        loss_ref[...] += jnp.full(loss_ref.shape, loss, f32)
        dg_ref[...] += dg
        dh_ref[...] = dh

    return pl.pallas_call(
        body, name=name, grid=(lp // tm,),
        in_specs=[pl.BlockSpec((tm, D_MODEL), lambda i: (i, 0)),
                  pl.BlockSpec((tm, D_MODEL), lambda i: (jnp.maximum(i - front_tiles, 0), 0)),
                  _full(g_final.shape)],
        out_specs=[_full((8, 128)), pl.BlockSpec((tm, D_MODEL), lambda i: (i, 0)), _full(g_final.shape)],
        out_shape=[jax.ShapeDtypeStruct((8, 128), f32), jax.ShapeDtypeStruct((lp, D_MODEL), f32),
                   jax.ShapeDtypeStruct(g_final.shape, f32)],
        compiler_params=_params(("arbitrary",)),
    )(h2, target, g_final)


def _input_norm_bwd(h0, g, du, dh1, *, name):
    lp = h0.shape[0]
    tm = FRONT

    def body(h_ref, g_ref, du_ref, dh1_ref, dx_ref, front_ref, dg_ref):
        i = pl.program_id(0)
        _, vjp = jax.vjp(lambda hv, gv: (_rms(hv, gv), hv), h_ref[...], g_ref[...])
        dh, dg = vjp((du_ref[...].astype(f32), dh1_ref[...]))

        @pl.when(i == 0)
        def _():
            dg_ref[...] = jnp.zeros_like(dg_ref)
            front_ref[...] = dh

        dg_ref[...] += dg
        dx_ref[...] = dh

    tile = pl.BlockSpec((tm, D_MODEL), lambda i: (i, 0))
    return pl.pallas_call(
        body, name=name, grid=(lp // tm,),
        in_specs=[tile, _full(g.shape), tile, tile],
        out_specs=[pl.BlockSpec((tm, D_MODEL), lambda i: (jnp.maximum(i - 1, 0), 0)), _full((tm, D_MODEL)),
                   _full(g.shape)],
        out_shape=[jax.ShapeDtypeStruct((lp - tm, D_MODEL), f32), jax.ShapeDtypeStruct((tm, D_MODEL), f32),
                   jax.ShapeDtypeStruct(g.shape, f32)],
        compiler_params=_params(("arbitrary",)),
    )(h0, g, du, dh1)


def _local_step(x, target, meta, p, early_weights=None, late_weights=None, emit=None):
    emit = emit or (lambda group, grads: 0.0)
    seq = x.shape[0]
    lp = seq + FRONT
    h0 = jnp.concatenate([jnp.zeros((PAD, D_MODEL), f32), meta, x], axis=0)
    cos_t, sin_t, swap = _rope_tables(lp)
    hsum = _head_sum_matrix(RWKV_DIM, RWKV_HEAD)
    hmean = hsum / RWKV_HEAD
    post_params = [p["ln_w"], p["ln_b"], p["r_k"], hmean]

    (u,) = _rowwise(lambda hv, g: (_rms(hv, g),), [h0], [p["norm_mix_g"]], [(D_MODEL, bf16)], name="norm_mix")
    if early_weights is not None:
        p = {**p, **early_weights(u)}
    prep_params = [p["w0"], p["w2"], p["a0"], p["a2"], p["g2"], p["k_k"], p["k_a"], hsum]
    qkv, p_rkv, p_lora, gates = _proj_in(u, p["w_in_lr"], p["b_in"], [ATTN_PROJ, RKV_W, LORA_W, 2 * D_MODEL],
                                         name="proj_in", zero_rows_below=PAD)

    q, k, v = _rowwise(_attn_prep, [qkv, cos_t, sin_t], [swap], [(Q_W, bf16), (KV_W, bf16), (KV_W, bf16)],
                       name="attn_prep")
    y_attn = _attention(q, k, v, p["sinks"], name="attention")

    mix_rkv, mix_lora = p["mix"][:, :RKV_W], p["mix"][:, RKV_W:]
    r_, lw_, k_, v_, a_, b_, g_ = _mixer_inputs([p_rkv, p_lora], [mix_rkv, mix_lora], prep_params,
                                                name="mixer_inputs")
    y_scan, states, inverses = _scan(r_, lw_, k_, v_, a_, b_, name="wkv_scan")
    (y_rwkv,) = _rowwise(_rwkv_post, [y_scan, r_, k_, v_, g_], post_params, [(RWKV_DIM, bf16)], name="rwkv_post")

    if late_weights is not None:
        p = {**p, **late_weights(y_rwkv)}
    br_a, br_r, merged = _branch_merge(y_attn, y_rwkv, p["w_br_attn_t"], p["w_br_rwkv_t"], gates, name="branch_merge")
    h1, f = _residual_norm(merged, p["w_o"], h0, p["norm_ffn_g"], name="out_proj")
    gate, up, act = _ffn_in(f, p["w_gate_t"], p["w_up_t"], name="ffn_in")
    h2 = _mm(act, p["w_down"], "nn", name="ffn_down", add=h1)

    loss8, dh2, d_final_g = _loss_head(h2, target, p["norm_final_g"], name="loss_head")
    dgate, dup = _ffn_in_bwd(dh2, p["w_down"], gate, up, name="ffn_in_bwd")
    d_w_down = _mm_tn(act, dh2, name="dw_down")
    d_w_gate_t = _mm_tn(dgate, f, name="dw_gate")
    d_w_up_t = _mm_tn(dup, f, name="dw_up")
    zero = emit("ffn", dict(w_down=d_w_down, w_gate_t=d_w_gate_t, w_up_t=d_w_up_t))
    dh1, d_ffn_g = _residual_norm_bwd([dgate, dup], [p["w_gate_t"], p["w_up_t"]], h1, p["norm_ffn_g"] + zero, dh2,
                                      name="norm_ffn_bwd")
    dgates, dbr_a, dbr_r = _branch_merge_bwd(dh1, p["w_o"], gates, br_a, br_r, name="branch_merge_bwd")
    d_w_o = _mm_tn(merged, dh1, name="dw_o")
    d_w_br_attn_t = _mm_tn(dbr_a, y_attn, name="dw_br_attn")
    d_w_br_rwkv_t = _mm_tn(dbr_r, y_rwkv, name="dw_br_rwkv")
    zero = emit("branch", dict(w_o=d_w_o, w_br_attn_t=d_w_br_attn_t, w_br_rwkv_t=d_w_br_rwkv_t))
    dy_attn = _mm(dbr_a, p["w_br_attn_t"], "nn", name="d_y_attn")
    dy_rwkv = _mm(dbr_r, p["w_br_rwkv_t"], "nn", name="d_y_rwkv")

    post_params = [p["ln_w"] + zero, p["ln_b"], p["r_k"], hmean]
    res = _rowwise_bwd(_rwkv_post, [y_scan, r_, k_, v_, g_], post_params, [[dy_rwkv]], name="rwkv_post_bwd",
                       diff_rows=[True] * 5, diff_params=[True, True, True, False])
    dy_scan, dr_p, dk_p, dv_p, dg_p, d_ln_w, d_ln_b, d_r_k = res
    dr_s, dlw_s, dk_s, dv_s, da_s, db_s = _scan_bwd(r_, lw_, k_, v_, a_, b_, states, inverses, dy_scan,
                                                    name="wkv_scan_bwd")
    res = _mixer_inputs_bwd([p_rkv, p_lora], [mix_rkv, mix_lora], prep_params,
                            [[dr_s, dr_p], [dlw_s], [dk_s, dk_p], [dv_s, dv_p], [da_s], [db_s], [dg_p]],
                            name="mixer_inputs_bwd")
    dp_rkv, dp_lora, d_mix_rkv, d_mix_lora, d_w0, d_w2, d_a0, d_a2, d_g2, d_k_k, d_k_a = res

    dq, dk, dv, dkm, dvm, d_sinks = _attention_bwd(q, k, v, p["sinks"], y_attn, dy_attn, name="attention_bwd")
    rest = jnp.zeros((lp - BLOCK, KV_W), f32)
    dkm, dvm = jnp.concatenate([dkm, rest], axis=0), jnp.concatenate([dvm, rest], axis=0)
    (dqkv,) = _rowwise_bwd(_attn_prep, [qkv, cos_t, sin_t], [swap], [[dq], [dk, dkm], [dv, dvm]], name="attn_prep_bwd",
                           diff_rows=[True, False, False], diff_params=[False], out_dtypes=[bf16])

    d_w_qkv_t, db_qkv = _mm_tn(dqkv, u, name="dw_qkv", colsum=True)
    d_w_rkv_t, db_rkv = _mm_tn(dp_rkv, u, name="dw_rkv", colsum=True)
    d_w_lora_t, db_lora = _mm_tn(dp_lora, u, name="dw_lora", colsum=True)
    d_w_gates_t, db_gates = _mm_tn(dgates, u, name="dw_gates", colsum=True)
    d_w_in_t = jnp.concatenate([d_w_qkv_t, d_w_rkv_t, d_w_lora_t, d_w_gates_t], axis=0)
    zero = emit("input", dict(w_in_t=d_w_in_t, g2=d_g2, w2=d_w2, a2=d_a2))
    du = _proj_in_bwd([dqkv, dp_rkv, dp_lora, dgates], p["w_in_lr"], name="d_u")
    dx, d_front, d_mix_g = _input_norm_bwd(h0, p["norm_mix_g"] + zero, du, dh1, name="norm_mix_bwd")

    grads = dict(
        w_in_t=d_w_in_t,
        b_in=jnp.concatenate([db_qkv, db_rkv, db_lora, db_gates], axis=1),
        mix=jnp.concatenate([d_mix_rkv, d_mix_lora], axis=1),
        norm_mix_g=d_mix_g, sinks=d_sinks, w0=d_w0, w2=d_w2, a0=d_a0, a2=d_a2, g2=d_g2, k_k=d_k_k, k_a=d_k_a,
        r_k=d_r_k, ln_w=d_ln_w, ln_b=d_ln_b, w_br_attn_t=d_w_br_attn_t, w_br_rwkv_t=d_w_br_rwkv_t, w_o=d_w_o,
        norm_ffn_g=d_ffn_g, w_gate_t=d_w_gate_t, w_up_t=d_w_up_t, w_down=d_w_down, norm_final_g=d_final_g,
        meta=d_front[PAD:],
    )
    return loss8[0, 0], dx, grads


def _position():
    return lax.axis_index("x"), lax.axis_index("y"), lax.axis_index("c")


def _other_chips(x, y):
    return [(1 - x, y), (x, 1 - y), (1 - x, 1 - y)]


_HBM = pl.BlockSpec(memory_space=pltpu.HBM)
_SEM = pl.BlockSpec(memory_space=pltpu.SEMAPHORE)
_EFFECT = pltpu.SideEffectType.DATAFLOW_SIDE_EFFECTING


def _landing_zone(src, kind):
    shape = {"whole": (N_CHIPS,) + src.shape, "half": (2, N_CHIPS, src.shape[0], src.shape[1] // 2),
             "slab": (3,) + src.shape[1:], "sibling": src.shape}[kind]
    return lax.empty(shape, src.dtype)


def _copies_per_source(kind):
    return 1 if kind == "sibling" else 3


def _chip_copies(src_refs, land_refs, send_sems, recv_sems, kind):
    x, y, c = _position()
    if kind == "sibling":
        return [pltpu.make_async_remote_copy(
            src_ref=src, dst_ref=land, send_sem=send_sems.at[a], recv_sem=recv_sems.at[a],
            device_id=(x, y, 1 - c), device_id_type=MESH) for a, (src, land) in enumerate(zip(src_refs, land_refs))]
    copies = []
    for a, (src, land) in enumerate(zip(src_refs, land_refs)):
        for j, (px, py) in enumerate(_other_chips(x, y)):
            if kind == "whole":
                src_ref, dst_ref = src, land.at[2 * x + y]
            elif kind == "half":
                half = src.shape[1] // 2
                src_ref, dst_ref = src.at[:, pl.ds(pl.multiple_of(c * half, half), half)], land.at[c, 2 * x + y]
            else:
                src_ref, dst_ref = src.at[2 * px + py], land.at[j]
            copies.append(pltpu.make_async_remote_copy(
                src_ref=src_ref, dst_ref=dst_ref, send_sem=send_sems.at[3 * a + j], recv_sem=recv_sems.at[3 * a + j],
                device_id=(px, py, c), device_id_type=MESH))
    return copies


def _exchange_start(srcs, *, kind, name):
    n = len(srcs)
    lands = [_landing_zone(s, kind) for s in srcs]

    def body(*refs):
        for cp in _chip_copies(refs[:n], refs[n:2 * n], refs[2 * n], refs[2 * n + 1], kind):
            cp.start()
        refs[-1][...] = jnp.zeros_like(refs[-1])

    res = pl.pallas_call(
        body, name=name,
        out_shape=(pltpu.SemaphoreType.DMA((_copies_per_source(kind) * n,)),
                   pltpu.SemaphoreType.DMA((_copies_per_source(kind) * n,)),
                   *[pltpu.HBM(a.shape, a.dtype) for a in srcs + lands], jax.ShapeDtypeStruct((8, 128), f32)),
        in_specs=[_HBM] * (2 * n),
        out_specs=(_SEM, _SEM, *[_HBM] * (2 * n), pl.BlockSpec(memory_space=pltpu.VMEM)),
        input_output_aliases={i: 2 + i for i in range(2 * n)},
        compiler_params=pltpu.CompilerParams(has_side_effects=_EFFECT),
    )(*[pltpu.with_memory_space_constraint(a, pltpu.HBM) for a in srcs + lands])
    return res[0], res[1], list(res[2:2 + n]), list(res[2 + n:2 + 2 * n]), res[-1]


def _exchange_wait(handle, after, *, kind, name):
    send_sems, recv_sems, srcs, lands, _ = handle
    n = len(srcs)

    def body(*refs):
        for cp in _chip_copies(refs[:n], refs[n:2 * n], refs[2 * n], refs[2 * n + 1], kind):
            cp.wait_send()
            cp.wait_recv()

    res = pl.pallas_call(
        body, name=name,
        out_shape=tuple(pltpu.HBM(a.shape, a.dtype) for a in srcs + lands),
        in_specs=[_HBM] * (2 * n) + [_SEM, _SEM, pl.BlockSpec(memory_space=pl.ANY)],
        out_specs=tuple([_HBM] * (2 * n)),
        input_output_aliases={i: i for i in range(2 * n)},
        compiler_params=pltpu.CompilerParams(has_side_effects=_EFFECT),
    )(*srcs, *lands, send_sems, recv_sems, after)
    return list(res[:n]), list(res[n:])


def _sum_own_and_received(g, recv, *, name):
    _, r, w = g.shape
    tm = _tile(r)
    if g.dtype == bf16 and tm % 16:
        tm = r
    x, y, _ = _position()
    me = jnp.reshape(2 * x + y, (1,)).astype(jnp.int32)

    def body(me_ref, g_ref, r_ref, o_ref):
        o_ref[...] = (g_ref[0].astype(f32) + r_ref[0].astype(f32)) + (r_ref[1].astype(f32) + r_ref[2].astype(f32))

    return pl.pallas_call(
        body, name=name,
        grid_spec=pltpu.PrefetchScalarGridSpec(
            num_scalar_prefetch=1, grid=(r // tm,),
            in_specs=[pl.BlockSpec((1, tm, w), lambda i, me_ref: (me_ref[0], i, 0)),
                      pl.BlockSpec((3, tm, w), lambda i, me_ref: (0, i, 0))],
            out_specs=pl.BlockSpec((tm, w), lambda i, me_ref: (i, 0))),
        out_shape=jax.ShapeDtypeStruct((r, w), f32),
        compiler_params=_params(("parallel",)),
    )(me, g, recv)


def _swap_cores(arrs, *, name):
    n = len(arrs)

    def body(*refs):
        x, y, c = _position()
        copies = [pltpu.make_async_remote_copy(
            src_ref=refs[i], dst_ref=refs[n + i], send_sem=refs[2 * n].at[i], recv_sem=refs[2 * n + 1].at[i],
            device_id=(x, y, 1 - c), device_id_type=MESH) for i in range(n)]
        for cp in copies:
            cp.start()
        for cp in copies:
            cp.wait_recv()
        for cp in copies:
            cp.wait_send()

    return pl.pallas_call(
        body, name=name,
        in_specs=[pl.BlockSpec(memory_space=pl.ANY)] * n,
        out_specs=[pl.BlockSpec(memory_space=pl.ANY)] * n,
        out_shape=[jax.ShapeDtypeStruct(a.shape, a.dtype) for a in arrs],
        scratch_shapes=[pltpu.SemaphoreType.DMA((n,)), pltpu.SemaphoreType.DMA((n,))],
    )(*arrs)


def _swap_halves(zone, *, name):
    def body(z_ref, o_ref, send_sems, recv_sems):
        x, y, c = _position()
        mine = [pltpu.make_async_remote_copy(
            src_ref=o_ref.at[c, 2 * px + py], dst_ref=o_ref.at[c, 2 * px + py], send_sem=send_sems.at[j],
            recv_sem=recv_sems.at[j], device_id=(x, y, 1 - c), device_id_type=MESH)
            for j, (px, py) in enumerate(_other_chips(x, y))]
        for cp in mine:
            cp.start()
        for j, (px, py) in enumerate(_other_chips(x, y)):
            pltpu.make_async_remote_copy(
                src_ref=o_ref.at[c, 2 * px + py], dst_ref=o_ref.at[1 - c, 2 * px + py], send_sem=send_sems.at[j],
                recv_sem=recv_sems.at[j], device_id=(x, y, 1 - c), device_id_type=MESH).wait_recv()
        for cp in mine:
            cp.wait_send()

    return pl.pallas_call(
        body, name=name,
        in_specs=[pl.BlockSpec(memory_space=pl.ANY)], out_specs=pl.BlockSpec(memory_space=pl.ANY),
        out_shape=jax.ShapeDtypeStruct(zone.shape, zone.dtype), input_output_aliases={0: 0},
        scratch_shapes=[pltpu.SemaphoreType.DMA((3,)), pltpu.SemaphoreType.DMA((3,))],
    )(zone)


def _all_reduce_small(a, after, *, name):
    rows, w = a.shape

    def body(a_ref, after_ref, o_ref, buf, send_sems, recv_sems):
        x, y, c = _position()
        me = 4 * x + 2 * y + c
        buf[0] = a_ref[...]
        sends = []
        for rel in range(1, N_DEV):
            peer = ((1 - x) if rel & 4 else x, (1 - y) if rel & 2 else y, (1 - c) if rel & 1 else c)
            cp = pltpu.make_async_remote_copy(
                src_ref=a_ref, dst_ref=buf.at[rel], send_sem=send_sems.at[rel - 1], recv_sem=recv_sems.at[rel - 1],
                device_id=peer, device_id_type=MESH)
            cp.start()
            sends.append(cp)
        for cp in sends:
            cp.wait_recv()
        for cp in sends:
            cp.wait_send()
        acc = buf[jnp.bitwise_xor(me, 0)]
        for d in range(1, N_DEV):
            acc = acc + buf[jnp.bitwise_xor(me, d)]
        o_ref[...] = acc

    return pl.pallas_call(
        body, name=name,
        in_specs=[pl.BlockSpec(memory_space=pltpu.VMEM), pl.BlockSpec(memory_space=pl.ANY)],
        out_specs=pl.BlockSpec(memory_space=pltpu.VMEM),
        out_shape=jax.ShapeDtypeStruct((rows, w), f32),
        scratch_shapes=[pltpu.VMEM((N_DEV, rows, w), f32), pltpu.SemaphoreType.DMA((N_DEV - 1,)),
                        pltpu.SemaphoreType.DMA((N_DEV - 1,))],
    )(a, after)


def _adamw(w, g_parts, m, v, *, name, transposed=False):
    rows, cols = w.shape
    if transposed:
        tm = 256 if rows % 256 == 0 else rows
        g_spec = pl.BlockSpec((cols, tm), lambda i: (0, i))
    else:
        tm = _tile(rows, 256)
        g_spec = pl.BlockSpec((tm, cols), lambda i: (i, 0))
    n = len(g_parts)

    def body(*refs):
        w_ref, m_ref, v_ref = refs[0], refs[1 + n], refs[2 + n]
        g_ref, d_ref, nm_ref, nv_ref = refs[3 + n:]
        gv = refs[1][...]
        for part in refs[2:1 + n]:
            gv = gv + part[...]
        if transposed:
            gv = gv.T
        g_ref[...] = gv
        nm = ADAM_B1 * m_ref[...] + (1.0 - ADAM_B1) * gv
        nv = ADAM_B2 * v_ref[...] + (1.0 - ADAM_B2) * (gv * gv)
        m_hat = nm / (1.0 - ADAM_B1 ** ADAM_STEP)
        v_hat = nv / (1.0 - ADAM_B2 ** ADAM_STEP)
        d_ref[...] = -ADAM_LR * (m_hat / (jnp.sqrt(v_hat) + ADAM_EPS) + ADAM_WD * w_ref[...])
        nm_ref[...] = nm
        nv_ref[...] = nv

    spec = pl.BlockSpec((tm, cols), lambda i: (i, 0))
    shape = jax.ShapeDtypeStruct((rows, cols), f32)
    return pl.pallas_call(
        body, name=name, grid=(rows // tm,), in_specs=[spec] + [g_spec] * n + [spec] * 2,
        out_specs=[spec] * 4, out_shape=[shape] * 4,
        compiler_params=_params(("parallel",)),
    )(w, *g_parts, m, v)


def _pad_rows(a, rows):
    return jnp.concatenate([a, jnp.zeros((rows - a.shape[0], a.shape[1]), a.dtype)], axis=0) if rows > a.shape[0] else a


_SMALL = (("norm_mix_g", D_MODEL), ("b_in", D_IN), ("sinks", Q_HEADS), ("mix", RWKV_PROJ), ("w0", RWKV_DIM),
          ("a0", RWKV_DIM), ("k_k", RWKV_DIM), ("k_a", RWKV_DIM), ("r_k", RWKV_DIM), ("ln_w", RWKV_DIM),
          ("ln_b", RWKV_DIM), ("norm_ffn_g", D_MODEL), ("norm_final_g", D_MODEL))


def _pack_small(d):
    flat = jnp.concatenate([d[n].reshape(-1).astype(f32) for n, _ in _SMALL])
    return flat


def _unpack_small(flat):
    out, off = {}, 0
    for n, size in _SMALL:
        out[n] = flat[off:off + size]
        off += size
    return out


_SMALL_TOTAL = sum(s for _, s in _SMALL)


def kernel(x, meta_tokens, norm_mix_g, w_in, b_in, attn_sinks, rwkv_mix, rwkv_w0, rwkv_w2, rwkv_a0, rwkv_a2, rwkv_g2, rwkv_k_k, rwkv_k_a, rwkv_r_k, rwkv_ln_w, rwkv_ln_b, w_br_attn, w_br_rwkv, w_o, norm_ffn_g, w_ffn_gate, w_ffn_up, w_ffn_down, norm_final_g, loss_target, m_meta_tokens, m_norm_mix_g, m_w_in, m_b_in, m_attn_sinks, m_rwkv_mix, m_rwkv_w0, m_rwkv_w2, m_rwkv_a0, m_rwkv_a2, m_rwkv_g2, m_rwkv_k_k, m_rwkv_k_a, m_rwkv_r_k, m_rwkv_ln_w, m_rwkv_ln_b, m_w_br_attn, m_w_br_rwkv, m_w_o, m_norm_ffn_g, m_w_ffn_gate, m_w_ffn_up, m_w_ffn_down, m_norm_final_g, v_meta_tokens, v_norm_mix_g, v_w_in, v_b_in, v_attn_sinks, v_rwkv_mix, v_rwkv_w0, v_rwkv_w2, v_rwkv_a0, v_rwkv_a2, v_rwkv_g2, v_rwkv_k_k, v_rwkv_k_a, v_rwkv_r_k, v_rwkv_ln_w, v_rwkv_ln_b, v_w_br_attn, v_w_br_rwkv, v_w_o, v_norm_ffn_g, v_w_ffn_gate, v_w_ffn_up, v_w_ffn_down, v_norm_final_g):
    names = ("meta_tokens", "norm_mix_g", "w_in", "b_in", "attn_sinks", "rwkv_mix", "rwkv_w0", "rwkv_w2", "rwkv_a0",
             "rwkv_a2", "rwkv_g2", "rwkv_k_k", "rwkv_k_a", "rwkv_r_k", "rwkv_ln_w", "rwkv_ln_b", "w_br_attn",
             "w_br_rwkv", "w_o", "norm_ffn_g", "w_ffn_gate", "w_ffn_up", "w_ffn_down", "norm_final_g")
    w_all = dict(zip(names, (meta_tokens, norm_mix_g, w_in, b_in, attn_sinks, rwkv_mix, rwkv_w0, rwkv_w2, rwkv_a0,
                             rwkv_a2, rwkv_g2, rwkv_k_k, rwkv_k_a, rwkv_r_k, rwkv_ln_w, rwkv_ln_b, w_br_attn,
                             w_br_rwkv, w_o, norm_ffn_g, w_ffn_gate, w_ffn_up, w_ffn_down, norm_final_g)))
    m_all = dict(zip(names, (m_meta_tokens, m_norm_mix_g, m_w_in, m_b_in, m_attn_sinks, m_rwkv_mix, m_rwkv_w0,
                             m_rwkv_w2, m_rwkv_a0, m_rwkv_a2, m_rwkv_g2, m_rwkv_k_k, m_rwkv_k_a, m_rwkv_r_k,
                             m_rwkv_ln_w, m_rwkv_ln_b, m_w_br_attn, m_w_br_rwkv, m_w_o, m_norm_ffn_g, m_w_ffn_gate,
                             m_w_ffn_up, m_w_ffn_down, m_norm_final_g)))
    v_all = dict(zip(names, (v_meta_tokens, v_norm_mix_g, v_w_in, v_b_in, v_attn_sinks, v_rwkv_mix, v_rwkv_w0,
                             v_rwkv_w2, v_rwkv_a0, v_rwkv_a2, v_rwkv_g2, v_rwkv_k_k, v_rwkv_k_a, v_rwkv_r_k,
                             v_rwkv_ln_w, v_rwkv_ln_b, v_w_br_attn, v_w_br_rwkv, v_w_o, v_norm_ffn_g, v_w_ffn_gate,
                             v_w_ffn_up, v_w_ffn_down, v_norm_final_g)))
    cx, cy, _ = _position()
    chip = 2 * cx + cy

    t_of = dict(w_in_t="w_in", w_gate_t="w_ffn_gate", w_up_t="w_ffn_up", w_br_attn_t="w_br_attn",
                w_br_rwkv_t="w_br_rwkv", g2_t="rwkv_g2", w2_t="rwkv_w2", a2_t="rwkv_a2")
    plain_of = dict(w_down="w_ffn_down", w_o="w_o")
    meta_cols = meta_tokens.shape[1]

    def shard(k):
        return (w_all[t_of[k]][0].T if k in t_of else w_all[plain_of[k]][0]).astype(bf16)

    def whole(zone, own):
        return lax.dynamic_update_slice_in_dim(zone, own[None], chip, axis=0).reshape(-1, own.shape[-1])

    tiny = ("g2_t", "w2_t", "a2_t")
    late = ("w_gate_t", "w_up_t", "w_down", "w_o", "w_br_attn_t", "w_br_rwkv_t")
    w_in_own = shard("w_in_t")
    w_in_rows, w_in_cols = w_in_own.shape
    tiny_h = _exchange_start([shard(k) for k in tiny] + [meta_tokens], kind="whole", name="gather_tiny_start")
    w_in_h = _exchange_start([w_in_own + tiny_h[4][0, 0].astype(bf16)], kind="half", name="gather_w_in_start")
    behind = w_in_h[4][0, 0].astype(bf16)
    late_h = _exchange_start([shard(k) + behind for k in late], kind="whole", name="gather_late_start")
    own, zones = _exchange_wait(tiny_h, late_h[4], kind="whole", name="gather_tiny_wait")
    got = {k: whole(z, o) for k, z, o in zip(tiny, zones, own)}
    meta_full = whole(zones[-1], own[-1]).reshape(N_CHIPS, N_META, meta_cols).transpose(1, 0, 2).reshape(N_META, -1)
    p = dict(
        g2=got["g2_t"].T.astype(f32), w2=got["w2_t"].T.astype(f32), a2=got["a2_t"].T.astype(f32),
        b_in=b_in, sinks=attn_sinks, mix=rwkv_mix, w0=rwkv_w0, a0=rwkv_a0, k_k=rwkv_k_k, k_a=rwkv_k_a,
        r_k=rwkv_r_k.reshape(1, RWKV_DIM), ln_w=rwkv_ln_w, ln_b=rwkv_ln_b, norm_mix_g=norm_mix_g,
        norm_ffn_g=norm_ffn_g, norm_final_g=norm_final_g.reshape(1, D_MODEL),
    )

    def early_weights(after):
        own_h, zones_h = _exchange_wait(w_in_h, after, kind="half", name="gather_w_in_wait")
        zone = _swap_halves(zones_h[0], name="swap_w_in_halves")
        own_halves = own_h[0].reshape(w_in_rows, 2, w_in_cols // 2).transpose(1, 0, 2)[:, None]
        zone = lax.dynamic_update_slice(zone, own_halves, (0, chip, 0, 0))
        return dict(w_in_lr=zone.reshape(2, N_CHIPS * w_in_rows, w_in_cols // 2))

    def late_weights(after):
        own_l, zones_l = _exchange_wait(late_h, after, kind="whole", name="gather_late_wait")
        return {k: whole(z, o) for k, z, o in zip(late, zones_l, own_l)}

    started = {}

    def partial_sums(groups, after):
        parts = {}
        for group in groups:
            keys, handle = started[group]
            slabs, lands = _exchange_wait(handle, after, kind="slab", name="scatter_" + group + "_wait")
            parts.update({k: _sum_own_and_received(s, l, name="sum_chips_" + k) for k, s, l in zip(keys, slabs, lands)})
        return parts

    def emit(group, grads_):
        keys = list(grads_)
        slabs = []
        for k in keys:
            a = grads_[k].T if k in ("g2", "w2", "a2") else grads_[k]
            slabs.append(a.reshape(N_CHIPS, a.shape[0] // N_CHIPS, a.shape[1]))
        started[group] = (keys, _exchange_start(slabs, kind="slab", name="scatter_" + group + "_start"))
        zero = started[group][1][4]
        if group == "input":
            started["parts_a"] = partial_sums(("ffn", "branch"), zero)
            started["swap_a"] = _exchange_start(list(started["parts_a"].values()), kind="sibling",
                                                name="swap_cores_a_start")
            zero = started["swap_a"][4]
        return zero[0, 0]

    loss, dx, g = _local_step(x[0], loss_target[0], meta_full, p, early_weights, late_weights, emit)

    grads, delta, new_m, new_v = {}, {}, {}, {}
    in_grad_layout = ("w_in_t", "w_gate_t", "w_up_t")
    weight_of = {**t_of, **plain_of}

    def update(keys, mine, theirs):
        for k, part, other in zip(keys, mine, theirs):
            both = [part, other]
            k = k + "_t" if k in ("g2", "w2", "a2") else k
            n = weight_of[k]
            shape2 = w_all[n].shape[1:]
            w_, m_, v_ = (a.reshape(shape2) for a in (w_all[n], m_all[n], v_all[n]))
            if k in in_grad_layout:
                res = [t.T for t in _adamw(w_.T, both, m_.T, v_.T, name="adamw_" + n)]
            else:
                res = _adamw(w_, both, m_, v_, name="adamw_" + n, transposed=k in t_of)
            grads[n], delta[n], new_m[n], new_v[n] = (t.reshape(w_all[n].shape) for t in res)
        return delta[n]

    done = update(list(started["parts_a"]),
                  *_exchange_wait(started["swap_a"], dx, kind="sibling", name="swap_cores_a_wait"))
    small = jnp.concatenate([_pack_small(g), loss.reshape(1)])
    small_rows = -(-small.shape[0] // PACK_W)
    small = jnp.concatenate([small, jnp.zeros((small_rows * PACK_W - small.shape[0],), f32)]).reshape(small_rows, PACK_W)
    small_rows8 = -(-(small_rows + N_META) // 8) * 8
    reduced = _all_reduce_small(_pad_rows(jnp.concatenate([g["meta"], small], axis=0), small_rows8), done,
                                name="reduce_small")
    parts_b = partial_sums(("input",), reduced)
    update(list(parts_b), list(parts_b.values()), _swap_cores(list(parts_b.values()), name="swap_cores_b"))
    g_meta = lax.dynamic_slice_in_dim(reduced[:N_META], chip * meta_cols, meta_cols, axis=1)
    flat = reduced[N_META:N_META + small_rows].reshape(-1)
    g_small = _unpack_small(flat)
    loss_total = flat[_SMALL_TOTAL]

    small_of = dict(norm_mix_g="norm_mix_g", b_in="b_in", attn_sinks="sinks", rwkv_mix="mix", rwkv_w0="w0",
                    rwkv_a0="a0", rwkv_k_k="k_k", rwkv_k_a="k_a", rwkv_r_k="r_k", rwkv_ln_w="ln_w",
                    rwkv_ln_b="ln_b", norm_ffn_g="norm_ffn_g", norm_final_g="norm_final_g")
    grads["meta_tokens"] = g_meta
    for n, k in small_of.items():
        grads[n] = g_small[k].reshape(w_all[n].shape)

    rest = [n for n in names if n not in delta]

    def pack_rest(src):
        flat_ = jnp.concatenate([src[n].reshape(-1) for n in rest])
        rows_ = -(-flat_.shape[0] // (8 * PACK_W)) * 8
        return jnp.concatenate([flat_, jnp.ones((rows_ * PACK_W - flat_.shape[0],), f32)]).reshape(rows_, PACK_W)

    _, d_, m_, v_ = _adamw(pack_rest(w_all), [pack_rest(grads)], pack_rest(m_all), pack_rest(v_all),
                           name="adamw_small")
    off = 0
    for n in rest:
        size = w_all[n].size
        for dst, src in ((delta, d_), (new_m, m_), (new_v, v_)):
            dst[n] = src.reshape(-1)[off:off + size].reshape(w_all[n].shape)
        off += size

    return (loss_total, dx.reshape(x.shape), *[grads[n] for n in names], *[delta[n] for n in names],
            *[new_m[n] for n in names], *[new_v[n] for n in names])
```

```python
import math

import jax
import jax.numpy as jnp
from jax import lax
from jax.experimental import pallas as pl
from jax.experimental.pallas import tpu as pltpu

f32 = jnp.float32
bf16 = jnp.bfloat16

D_MODEL = 1024
N_META = 16
HEAD_DIM = 64
Q_HEADS = 8
KV_HEADS = 2
GROUP = Q_HEADS // KV_HEADS
WINDOW = 128
BLOCK = 128
ROPE_THETA = 500000.0
ROPE_DIM = HEAD_DIM // 4
RWKV_HEADS = 8
RWKV_HEAD = 64
RWKV_DIM = RWKV_HEADS * RWKV_HEAD
DECAY_LORA = 64
AAA_LORA = 64
GATE_LORA = 160
LORA_W = DECAY_LORA + AAA_LORA + GATE_LORA
RWKV_LN_EPS = 64e-5
D_FF = 2816
Q_W = Q_HEADS * HEAD_DIM
KV_W = KV_HEADS * HEAD_DIM
ATTN_PROJ = Q_W + 2 * KV_W
RKV_W = 3 * RWKV_DIM
RWKV_PROJ = RKV_W + LORA_W
D_IN = ATTN_PROJ + RWKV_PROJ + 2 * D_MODEL
RMS_EPS = 1e-6
NEG_INF = -1e30
PAD = BLOCK - N_META
FRONT = PAD + N_META

ADAM_LR = 0.001
ADAM_B1 = 0.9
ADAM_B2 = 0.999
ADAM_EPS = 1e-08
ADAM_WD = 0.01
ADAM_STEP = 10

N_CHIPS = 4
N_DEV = 8
CHUNK = 128
VMEM_LIMIT = 56 * 1024 * 1024
MM_ROWS = 704
PACK_W = 1024
MESH = pl.DeviceIdType.MESH


def _tile(m, pref=384):
    for step in (16, 8):
        for t in range(min(m, pref) // step * step, 0, -step):
            if m % t == 0:
                return t
    return m


def _params(sem=None):
    return pltpu.CompilerParams(dimension_semantics=sem, vmem_limit_bytes=VMEM_LIMIT)


def _full(shape):
    nd = len(shape)
    return pl.BlockSpec(shape, lambda *_: (0,) * nd)


def _dot(a, b, dims="nn"):
    dn = {"nn": (((1,), (0,)), ((), ())), "nt": (((1,), (1,)), ((), ())), "tn": (((0,), (0,)), ((), ()))}[dims]
    return lax.dot_general(a.astype(bf16), b.astype(bf16), dn, preferred_element_type=f32)


def _two_pass(x, m, dims="nn"):
    x_hi = x.astype(bf16)
    x_lo = (x - x_hi.astype(f32)).astype(bf16)
    return _dot(x_hi, m, dims) + _dot(x_lo, m, dims)


@jax.custom_vjp
def _dot_const(x, m):
    return _two_pass(x, m)


def _dot_const_fwd(x, m):
    return _two_pass(x, m), m


def _dot_const_bwd(m, ct):
    return _two_pass(ct, m, "nt"), jnp.zeros_like(m)


_dot_const.defvjp(_dot_const_fwd, _dot_const_bwd)


def _two_pass_left(m, x, dims):
    x_hi = x.astype(bf16)
    x_lo = (x - x_hi.astype(f32)).astype(bf16)
    return _dot(m, x_hi, dims) + _dot(m, x_lo, dims)


@jax.custom_vjp
def _const_dot(m, x):
    return _two_pass_left(m, x, "nn")


def _const_dot_fwd(m, x):
    return _two_pass_left(m, x, "nn"), m


def _const_dot_bwd(m, ct):
    return jnp.zeros_like(m), _two_pass_left(m, ct, "tn")


_const_dot.defvjp(_const_dot_fwd, _const_dot_bwd)


def _mm(a, b, mode, *, name, out_dtype=f32, bias=None, add=None, zero_rows_below=0):
    m, _ = a.shape
    n = b.shape[1] if mode == "nn" else b.shape[0]
    tm = _tile(m, MM_ROWS)
    has_bias, has_add = bias is not None, add is not None

    def body(*refs):
        a_ref, b_ref = refs[0], refs[1]
        o_ref = refs[-1]
        acc = _dot(a_ref[...], b_ref[...], mode)
        k = 2
        if has_bias:
            acc = acc + refs[k][...]
            k += 1
        if zero_rows_below:
            rows = pl.program_id(0) * tm + lax.broadcasted_iota(jnp.int32, acc.shape, 0)
            acc = jnp.where(rows >= zero_rows_below, acc, 0.0)
        if has_add:
            acc = acc + refs[k][...].astype(f32)
        o_ref[...] = acc.astype(out_dtype)

    ins = [a, b]
    in_specs = [pl.BlockSpec((tm, a.shape[1]), lambda i: (i, 0)), _full(b.shape)]
    if has_bias:
        ins.append(bias)
        in_specs.append(_full(bias.shape))
    if has_add:
        ins.append(add)
        in_specs.append(pl.BlockSpec((tm, n), lambda i: (i, 0)))
    return pl.pallas_call(
        body, name=name, grid=(m // tm,), in_specs=in_specs,
        out_specs=pl.BlockSpec((tm, n), lambda i: (i, 0)),
        out_shape=jax.ShapeDtypeStruct((m, n), out_dtype),
        compiler_params=_params(("parallel",)),
    )(*ins)


def _pieces(widths):
    out, off = [], 0
    for w in widths:
        out.append((off, w))
        off += w
    return out


def _proj_in(a, w_lr, bias, widths, *, name, zero_rows_below=0):
    m, kdim = a.shape
    half = kdim // 2
    tm = _tile(m, MM_ROWS)

    def body(a_ref, w_ref, b_ref, *outs):
        a_l, a_r = a_ref[:, :half], a_ref[:, half:]
        for (off, width), o_ref in zip(_pieces(widths), outs):
            acc = _dot(a_l, w_ref[0, off:off + width, :], "nt") + _dot(a_r, w_ref[1, off:off + width, :], "nt")
            acc = acc + b_ref[:, off:off + width]
            if zero_rows_below:
                rows = pl.program_id(0) * tm + lax.broadcasted_iota(jnp.int32, acc.shape, 0)
                acc = jnp.where(rows >= zero_rows_below, acc, 0.0)
            o_ref[...] = acc.astype(o_ref.dtype)

    return pl.pallas_call(
        body, name=name, grid=(m // tm,),
        in_specs=[pl.BlockSpec((tm, kdim), lambda i: (i, 0)), _full(w_lr.shape), _full(bias.shape)],
        out_specs=[pl.BlockSpec((tm, w), lambda i: (i, 0)) for w in widths],
        out_shape=[jax.ShapeDtypeStruct((m, w), bf16) for w in widths],
        compiler_params=_params(("parallel",)),
    )(a, w_lr, bias)


def _proj_in_bwd(d_list, w_lr, *, name):
    m = d_list[0].shape[0]
    half = w_lr.shape[2]
    widths = [d.shape[1] for d in d_list]
    tm = _tile(m, MM_ROWS)

    def body(*refs):
        w_ref, o_ref = refs[-2], refs[-1]
        for side in range(2):
            acc = None
            for (off, width), d_ref in zip(_pieces(widths), refs):
                term = _dot(d_ref[...], w_ref[side, off:off + width, :])
                acc = term if acc is None else acc + term
            o_ref[:, side * half:(side + 1) * half] = acc.astype(o_ref.dtype)

    return pl.pallas_call(
        body, name=name, grid=(m // tm,),
        in_specs=[pl.BlockSpec((tm, w), lambda i: (i, 0)) for w in widths] + [_full(w_lr.shape)],
        out_specs=pl.BlockSpec((tm, 2 * half), lambda i: (i, 0)),
        out_shape=jax.ShapeDtypeStruct((m, 2 * half), bf16),
        compiler_params=_params(("parallel",)),
    )(*d_list, w_lr)


def _residual_norm(a, w, res, g, *, name):
    m, d = res.shape
    tm = _tile(m, MM_ROWS)

    def body(a_ref, w_ref, r_ref, g_ref, h_ref, n_ref):
        h = _dot(a_ref[...], w_ref[...]) + r_ref[...]
        h_ref[...] = h
        n_ref[...] = _rms(h, g_ref[...]).astype(n_ref.dtype)

    tile = pl.BlockSpec((tm, d), lambda i: (i, 0))
    return pl.pallas_call(
        body, name=name, grid=(m // tm,),
        in_specs=[pl.BlockSpec((tm, a.shape[1]), lambda i: (i, 0)), _full(w.shape), tile, _full(g.shape)],
        out_specs=[tile, tile],
        out_shape=[jax.ShapeDtypeStruct((m, d), f32), jax.ShapeDtypeStruct((m, d), bf16)],
        compiler_params=_params(("parallel",)),
    )(a, w, res, g)


def _residual_norm_bwd(d_list, w_list, h, g, dh_out, *, name):
    m, d = h.shape
    k = len(d_list)
    tm = _tile(m)

    def body(*refs):
        h_ref, g_ref, dho_ref, dh_ref, dg_ref = refs[2 * k:]
        dn = _dot(refs[0][...], refs[k][...])
        for i in range(1, k):
            dn = dn + _dot(refs[i][...], refs[k + i][...])
        _, vjp = jax.vjp(lambda hv, gv: (_rms(hv, gv), hv), h_ref[...], g_ref[...])
        dh, dg = vjp((dn, dho_ref[...]))
        dh_ref[...] = dh

        @pl.when(pl.program_id(0) == 0)
        def _():
            dg_ref[...] = jnp.zeros_like(dg_ref)

        dg_ref[...] += dg

    tile = pl.BlockSpec((tm, d), lambda i: (i, 0))
    return pl.pallas_call(
        body, name=name, grid=(m // tm,),
        in_specs=[pl.BlockSpec((tm, a.shape[1]), lambda i: (i, 0)) for a in d_list] + [_full(w.shape) for w in w_list]
        + [tile, _full(g.shape), tile],
        out_specs=[tile, _full(g.shape)],
        out_shape=[jax.ShapeDtypeStruct((m, d), f32), jax.ShapeDtypeStruct(g.shape, f32)],
        compiler_params=_params(("arbitrary",)),
    )(*d_list, *w_list, h, g, dh_out)


def _mm_tn(a, b, *, name, colsum=False, out_dtype=bf16):
    r, m = a.shape
    n = b.shape[1]
    tr = _tile(r, 1408)
    tmo = m
    for cand in (1408, 1024, 768, 512):
        if m > 1024 and m % cand == 0:
            tmo = cand
            break
    steps = r // tr

    def body(a_ref, b_ref, o_ref, *rest):
        acc = rest[-1]
        i = pl.program_id(1)

        @pl.when(i == 0)
        def _():
            acc[...] = jnp.zeros_like(acc)
            if colsum:
                rest[0][...] = jnp.zeros_like(rest[0])

        acc[...] += _dot(a_ref[...], b_ref[...], "tn")
        if colsum:
            rest[0][...] += jnp.sum(a_ref[...].astype(f32), axis=0, keepdims=True)

        @pl.when(i == steps - 1)
        def _():
            o_ref[...] = acc[...].astype(out_dtype)

    out_shape = [jax.ShapeDtypeStruct((m, n), out_dtype)]
    out_specs = [pl.BlockSpec((tmo, n), lambda j, i: (j, 0))]
    if colsum:
        out_shape.append(jax.ShapeDtypeStruct((1, m), f32))
        out_specs.append(pl.BlockSpec((1, tmo), lambda j, i: (0, j)))
    res = pl.pallas_call(
        body, name=name, grid=(m // tmo, steps),
        in_specs=[pl.BlockSpec((tr, tmo), lambda j, i: (i, j)), pl.BlockSpec((tr, n), lambda j, i: (i, 0))],
        out_specs=out_specs, out_shape=out_shape,
        scratch_shapes=[pltpu.VMEM((tmo, n), f32)],
        compiler_params=_params(("parallel", "arbitrary")),
    )(a, b)
    return res if colsum else res[0]


def _rowwise(fn, rows, params, outs, *, name, tm=None):
    m = rows[0].shape[0]
    tm = tm or _tile(m, MM_ROWS)
    nr, npar = len(rows), len(params)

    def body(*refs):
        vals = [r[...] for r in refs[:nr + npar]]
        res = fn(*vals)
        for o_ref, v in zip(refs[nr + npar:], res):
            o_ref[...] = v.astype(o_ref.dtype)

    return pl.pallas_call(
        body, name=name, grid=(m // tm,),
        in_specs=[pl.BlockSpec((tm, r.shape[1]), lambda i: (i, 0)) for r in rows] + [_full(p.shape) for p in params],
        out_specs=[pl.BlockSpec((tm, w), lambda i: (i, 0)) for w, _ in outs],
        out_shape=[jax.ShapeDtypeStruct((m, w), dt) for w, dt in outs],
        compiler_params=_params(("parallel",)),
    )(*rows, *params)


def _rowwise_bwd(fn, rows, params, cts, *, name, diff_rows, diff_params, tm=None, zero_rows_below=0, out_dtypes=None):
    m = rows[0].shape[0]
    tm = tm or _tile(m)
    nr, npar = len(rows), len(params)
    d_idx = [i for i in range(nr) if diff_rows[i]]
    p_idx = [i for i in range(npar) if diff_params[i]]
    out_dtypes = out_dtypes or [f32] * len(d_idx)
    flat_cts = [c for group in cts for c in group]
    n_ct = len(flat_cts)

    def body(*refs):
        vals = [r[...] for r in refs[:nr + npar]]
        ct_refs = refs[nr + npar:nr + npar + n_ct]
        out_refs = refs[nr + npar + n_ct:]
        ct_vals, k = [], 0
        for group in cts:
            acc = ct_refs[k][...].astype(f32)
            for extra in range(1, len(group)):
                acc = acc + ct_refs[k + extra][...].astype(f32)
            k += len(group)
            if zero_rows_below:
                rr = pl.program_id(0) * tm + lax.broadcasted_iota(jnp.int32, acc.shape, 0)
                acc = jnp.where(rr >= zero_rows_below, acc, 0.0)
            ct_vals.append(acc)

        def g(*dargs):
            full = list(vals)
            for pos, i in enumerate(d_idx):
                full[i] = dargs[pos]
            for pos, i in enumerate(p_idx):
                full[nr + i] = dargs[len(d_idx) + pos]
            return tuple(fn(*full))

        _, vjp = jax.vjp(g, *[vals[i].astype(f32) for i in d_idx], *[vals[nr + i] for i in p_idx])
        grads = vjp(tuple(ct_vals))
        for pos in range(len(d_idx)):
            out_refs[pos][...] = grads[pos].astype(out_refs[pos].dtype)
        first = pl.program_id(0) == 0
        for pos in range(len(p_idx)):
            o_ref = out_refs[len(d_idx) + pos]

            @pl.when(first)
            def _(o_ref=o_ref):
                o_ref[...] = jnp.zeros_like(o_ref)

            o_ref[...] += grads[len(d_idx) + pos]

    return pl.pallas_call(
        body, name=name, grid=(m // tm,),
        in_specs=[pl.BlockSpec((tm, r.shape[1]), lambda i: (i, 0)) for r in rows] + [_full(p.shape) for p in params]
        + [pl.BlockSpec((tm, c.shape[1]), lambda i: (i, 0)) for c in flat_cts],
        out_specs=[pl.BlockSpec((tm, rows[i].shape[1]), lambda i_: (i_, 0)) for i in d_idx]
        + [_full(params[i].shape) for i in p_idx],
        out_shape=[jax.ShapeDtypeStruct(rows[i].shape, dt) for i, dt in zip(d_idx, out_dtypes)]
        + [jax.ShapeDtypeStruct(params[i].shape, f32) for i in p_idx],
        compiler_params=_params(("arbitrary",)),
    )(*rows, *params, *flat_cts)


def _rms(x, g):
    return x * lax.rsqrt(jnp.mean(x * x, axis=-1, keepdims=True) + RMS_EPS) * g


def _head_sum_matrix(width, head):
    idx = jnp.arange(width) // head
    return (idx[:, None] == idx[None, :]).astype(f32)


def _rope_tables(lp):
    half = ROPE_DIM // 2
    pos = (jnp.arange(lp) - PAD).astype(f32)
    inv_freq = jnp.power(jnp.float32(ROPE_THETA), -jnp.arange(half, dtype=f32) * (2.0 / ROPE_DIM))
    ang = pos[:, None] * inv_freq[None, :]
    cos, sin = jnp.cos(ang), jnp.sin(ang)
    ones = jnp.ones((lp, HEAD_DIM - ROPE_DIM), f32)
    zeros = jnp.zeros((lp, HEAD_DIM - ROPE_DIM), f32)
    cos_t = jnp.concatenate([cos, cos, ones], axis=1)
    sin_t = jnp.concatenate([-sin, sin, zeros], axis=1)
    i = jnp.arange(HEAD_DIM)
    src = jnp.where(i < half, i + half, jnp.where(i < ROPE_DIM, i - half, i))
    swap = ((i[:, None] == src[None, :]) & (i[None, :] < ROPE_DIM)).astype(f32)
    return cos_t, sin_t, swap


def _attn_prep(qkv, cos_t, sin_t, swap):
    outs = []
    for h in range(Q_HEADS + KV_HEADS):
        t = qkv[:, h * HEAD_DIM:(h + 1) * HEAD_DIM]
        outs.append(t * cos_t + _dot_const(t, swap) * sin_t)
    q = jnp.concatenate(outs[:Q_HEADS], axis=1)
    k = jnp.concatenate(outs[Q_HEADS:], axis=1)
    return q, k, qkv[:, Q_W + KV_W:]


def _softplus(z):
    return jnp.maximum(z, 0.0) + jnp.log1p(jnp.exp(-jnp.abs(z)))


def _rwkv_prep(rkv, lora, w0, w2, a0, a2, g2, k_k, k_a, hsum):
    r = rkv[:, :RWKV_DIM]
    k = rkv[:, RWKV_DIM:2 * RWKV_DIM]
    v = rkv[:, 2 * RWKV_DIM:]
    dw = lora[:, :DECAY_LORA]
    da = lora[:, DECAY_LORA:DECAY_LORA + AAA_LORA]
    dg = lora[:, DECAY_LORA + AAA_LORA:]
    w = -_softplus(-(w0 + _dot(jnp.tanh(dw), w2))) - 0.5
    a = jax.nn.sigmoid(a0 + _dot(da, a2))
    g = _dot(jax.nn.sigmoid(dg), g2)
    kk = k * k_k
    kk = kk * lax.rsqrt(jnp.maximum(_dot_const(kk * kk, hsum), 1e-24))
    k = k * (1.0 + (a - 1.0) * k_a)
    log_decay = -jnp.exp(w)
    return r, log_decay, k, v, -kk, kk * a, g


def _rwkv_post(y, r, k, v, g, ln_w, ln_b, r_k, hmean):
    hsum = hmean * RWKV_HEAD
    mean = _dot_const(y, hmean)
    yc = y - mean
    var = _dot_const(yc * yc, hmean)
    yn = yc * lax.rsqrt(var + RWKV_LN_EPS) * ln_w + ln_b
    bonus = _dot_const(r * k * r_k, hsum) * v
    return ((yn + bonus) * g,)


def _merge(gates, br_a, br_r):
    sg = jax.nn.sigmoid(gates)
    return (sg[:, :D_MODEL] * br_a + sg[:, D_MODEL:] * br_r,)


def _swiglu(gate, up):
    return (jax.nn.silu(gate) * up,)


def _ffn_in(f, w_gate_t, w_up_t, *, name):
    m, d = f.shape
    n = w_gate_t.shape[0]
    tm = _tile(m)

    def body(f_ref, wg_ref, wu_ref, g_ref, u_ref, a_ref):
        g = _dot(f_ref[...], wg_ref[...], "nt")
        u = _dot(f_ref[...], wu_ref[...], "nt")
        g_ref[...] = g.astype(g_ref.dtype)
        u_ref[...] = u.astype(u_ref.dtype)
        a_ref[...] = _swiglu(g, u)[0].astype(a_ref.dtype)

    spec = pl.BlockSpec((tm, n), lambda i: (i, 0))
    return pl.pallas_call(
        body, name=name, grid=(m // tm,),
        in_specs=[pl.BlockSpec((tm, d), lambda i: (i, 0)), _full(w_gate_t.shape), _full(w_up_t.shape)],
        out_specs=[spec] * 3, out_shape=[jax.ShapeDtypeStruct((m, n), bf16)] * 3,
        compiler_params=_params(("parallel",)),
    )(f, w_gate_t, w_up_t)


def _branch_merge(y_attn, y_rwkv, w_attn_t, w_rwkv_t, gates, *, name):
    m = y_attn.shape[0]
    tm = _tile(m, MM_ROWS)

    def body(ya_ref, yr_ref, wa_ref, wr_ref, g_ref, a_ref, r_ref, o_ref):
        br_a = _dot(ya_ref[...], wa_ref[...], "nt")
        br_r = _dot(yr_ref[...], wr_ref[...], "nt")
        a_ref[...] = br_a.astype(a_ref.dtype)
        r_ref[...] = br_r.astype(r_ref.dtype)
        o_ref[...] = _merge(g_ref[...].astype(f32), br_a, br_r)[0].astype(o_ref.dtype)

    rows = lambda a: pl.BlockSpec((tm, a.shape[1]), lambda i: (i, 0))
    spec = pl.BlockSpec((tm, D_MODEL), lambda i: (i, 0))
    return pl.pallas_call(
        body, name=name, grid=(m // tm,),
        in_specs=[rows(y_attn), rows(y_rwkv), _full(w_attn_t.shape), _full(w_rwkv_t.shape), rows(gates)],
        out_specs=[spec] * 3, out_shape=[jax.ShapeDtypeStruct((m, D_MODEL), bf16)] * 3,
        compiler_params=_params(("parallel",)),
    )(y_attn, y_rwkv, w_attn_t, w_rwkv_t, gates)


def _branch_merge_bwd(dh, w_o, gates, br_a, br_r, w_attn_t, w_rwkv_t, *, name):
    m = dh.shape[0]
    tm = _tile(m, MM_ROWS)

    def body(dh_ref, w_ref, g_ref, a_ref, r_ref, wa_ref, wr_ref, dg_ref, da_ref, dr_ref, dya_ref, dyr_ref):
        dmerged = _dot(dh_ref[...], w_ref[...], "nt")
        _, vjp = jax.vjp(lambda g, a, r: _merge(g, a, r)[0], g_ref[...].astype(f32), a_ref[...].astype(f32),
                         r_ref[...].astype(f32))
        dg, da, dr = vjp(dmerged)
        dg_ref[...] = dg.astype(dg_ref.dtype)
        da_ref[...] = da.astype(da_ref.dtype)
        dr_ref[...] = dr.astype(dr_ref.dtype)
        dya_ref[...] = _dot(da, wa_ref[...])
        dyr_ref[...] = _dot(dr, wr_ref[...])

    rows = lambda a: pl.BlockSpec((tm, a.shape[1]), lambda i: (i, 0))
    mixer = pl.BlockSpec((tm, w_attn_t.shape[1]), lambda i: (i, 0))
    return pl.pallas_call(
        body, name=name, grid=(m // tm,),
        in_specs=[rows(dh), _full(w_o.shape), rows(gates), rows(br_a), rows(br_r), _full(w_attn_t.shape),
                  _full(w_rwkv_t.shape)],
        out_specs=[rows(gates), rows(br_a), rows(br_r), mixer, mixer],
        out_shape=[jax.ShapeDtypeStruct(gates.shape, bf16), jax.ShapeDtypeStruct(br_a.shape, bf16),
                   jax.ShapeDtypeStruct(br_r.shape, bf16), jax.ShapeDtypeStruct((m, w_attn_t.shape[1]), f32),
                   jax.ShapeDtypeStruct((m, w_rwkv_t.shape[1]), f32)],
        compiler_params=_params(("parallel",)),
    )(dh, w_o, gates, br_a, br_r, w_attn_t, w_rwkv_t)


def _ffn_in_bwd(dh, w_down, gate, up, *, name):
    m, d = dh.shape
    n = w_down.shape[0]
    tm = _tile(m)

    def body(dh_ref, w_ref, g_ref, u_ref, dg_ref, du_ref):
        dact = _dot(dh_ref[...], w_ref[...], "nt")
        _, vjp = jax.vjp(lambda a, b: _swiglu(a, b)[0], g_ref[...].astype(f32), u_ref[...].astype(f32))
        dg, du = vjp(dact)
        dg_ref[...] = dg.astype(dg_ref.dtype)
        du_ref[...] = du.astype(du_ref.dtype)

    spec = pl.BlockSpec((tm, n), lambda i: (i, 0))
    return pl.pallas_call(
        body, name=name, grid=(m // tm,),
        in_specs=[pl.BlockSpec((tm, d), lambda i: (i, 0)), _full(w_down.shape), spec, spec],
        out_specs=[spec] * 2, out_shape=[jax.ShapeDtypeStruct((m, n), bf16)] * 2,
        compiler_params=_params(("parallel",)),
    )(dh, w_down, gate, up)


HALO = 16


def _previous_rows(x, before_ref, first_tile):
    rows = lax.broadcasted_iota(jnp.int32, x.shape, 0)
    last = jnp.where(first_tile, 0.0, before_ref[HALO - 1:HALO, :].astype(f32))
    return jnp.where(rows == 0, last, pltpu.roll(x, 1, axis=0))


def _mixer_inputs(ps, mixes, params, *, name):
    m = ps[0].shape[0]
    tm = _tile(m)
    sub = tm // HALO
    n_par = len(params)

    def body(*refs):
        first = pl.program_id(0) == 0
        pf = []
        for k in range(2):
            x = refs[k][...].astype(f32)
            pf.append(x + (_previous_rows(x, refs[2 + k], first) - x) * refs[4 + k][...])
        res = _rwkv_prep(*pf, *[ref[...] for ref in refs[6:6 + n_par]])
        for o_ref, val in zip(refs[6 + n_par:], res):
            o_ref[...] = val

    tile = lambda a: pl.BlockSpec((tm, a.shape[1]), lambda i: (i, 0))
    before = lambda a: pl.BlockSpec((HALO, a.shape[1]), lambda i: (jnp.maximum(i * sub - 1, 0), 0))
    out = pl.BlockSpec((tm, RWKV_DIM), lambda i: (i, 0))
    return pl.pallas_call(
        body, name=name, grid=(m // tm,),
        in_specs=[tile(a) for a in ps] + [before(a) for a in ps] + [_full(a.shape) for a in mixes + params],
        out_specs=[out] * 7, out_shape=[jax.ShapeDtypeStruct((m, RWKV_DIM), f32)] * 7,
        compiler_params=_params(("parallel",)),
    )(*ps, *ps, *mixes, *params)


def _mixer_inputs_bwd(ps, mixes, params, cts, *, name):
    m = ps[0].shape[0]
    tm = _tile(m)
    sub = tm // HALO
    nt = m // tm
    n_par = len(params)
    flat_cts = [c for group in cts for c in group]
    n_ct = len(flat_cts)

    def body(*refs):
        i = pl.program_id(0)
        tile_index = nt - 1 - i
        ct_refs = refs[6 + n_par:6 + n_par + n_ct]
        dp_refs = refs[6 + n_par + n_ct:8 + n_par + n_ct]
        dmix_refs = refs[8 + n_par + n_ct:10 + n_par + n_ct]
        dpar_refs = refs[10 + n_par + n_ct:9 + 2 * n_par + n_ct]
        carries = refs[9 + 2 * n_par + n_ct:]
        rows1 = tile_index * tm + lax.broadcasted_iota(jnp.int32, (tm, 1), 0)
        live = rows1 >= PAD

        @pl.when(i == 0)
        def _():
            for ref in (*dmix_refs, *dpar_refs, *carries):
                ref[...] = jnp.zeros_like(ref)

        xs, prevs, pf = [], [], []
        for k in range(2):
            x = refs[k][...].astype(f32)
            xp = _previous_rows(x, refs[2 + k], tile_index == 0)
            xs.append(x)
            prevs.append(xp)
            pf.append(x + (xp - x) * refs[4 + k][...])
        ct_vals, pos = [], 0
        for group in cts:
            acc = ct_refs[pos][...].astype(f32)
            for extra in range(1, len(group)):
                acc = acc + ct_refs[pos + extra][...].astype(f32)
            pos += len(group)
            ct_vals.append(jnp.where(live, acc, 0.0))
        par_vals = [ref[...] for ref in refs[6:6 + n_par]]
        _, vjp = jax.vjp(lambda *args: _rwkv_prep(*args, par_vals[-1]), *pf, *par_vals[:-1])
        g = vjp(tuple(ct_vals))
        for k in range(2):
            dpf = g[k]
            mixv = refs[4 + k][...]
            dm = dpf * mixv
            rows = lax.broadcasted_iota(jnp.int32, dm.shape, 0)
            dm_next = jnp.where(rows == tm - 1, carries[k][...], pltpu.roll(dm, tm - 1, axis=0))
            dp_refs[k][...] = jnp.where(live, dpf - dm + dm_next, 0.0).astype(dp_refs[k].dtype)
            carries[k][...] = dm[0:1, :]
            dmix_refs[k][...] += jnp.sum(dpf * (prevs[k] - xs[k]), axis=0, keepdims=True)
        for ref, val in zip(dpar_refs, g[2:]):
            ref[...] += val

    tile = lambda a: pl.BlockSpec((tm, a.shape[1]), lambda i: (nt - 1 - i, 0))
    before = lambda a: pl.BlockSpec((HALO, a.shape[1]), lambda i: (jnp.maximum((nt - 1 - i) * sub - 1, 0), 0))
    return pl.pallas_call(
        body, name=name, grid=(nt,),
        in_specs=[tile(a) for a in ps] + [before(a) for a in ps] + [_full(a.shape) for a in mixes + params]
        + [tile(c) for c in flat_cts],
        out_specs=[tile(a) for a in ps] + [_full(a.shape) for a in mixes + params[:-1]],
        out_shape=[jax.ShapeDtypeStruct(a.shape, bf16) for a in ps]
        + [jax.ShapeDtypeStruct(a.shape, f32) for a in mixes + params[:-1]],
        scratch_shapes=[pltpu.VMEM((1, a.shape[1]), f32) for a in ps],
        compiler_params=_params(("arbitrary",)),
    )(*ps, *ps, *mixes, *params, *flat_cts)


def _attn_masks(blk):
    qi = lax.broadcasted_iota(jnp.int32, (BLOCK, BLOCK), 0)
    ki = lax.broadcasted_iota(jnp.int32, (BLOCK, BLOCK), 1)
    qpos = blk * BLOCK + qi - PAD
    kpos_c = blk * BLOCK + ki - PAD
    kpos_p = kpos_c - BLOCK
    kpos_m = ki - PAD

    def band(kpos):
        return (kpos >= N_META) & (kpos <= qpos) & (qpos - kpos < WINDOW)

    return band(kpos_p), band(kpos_c), (kpos_m >= 0) & (kpos_m <= qpos)


def _attn_probs(qs, k3s, sink, oks):
    s = [[jnp.where(ok, _dot(qh, kx, "nt"), NEG_INF) for kx, ok in zip(k3, oks)] for qh, k3 in zip(qs, k3s)]
    mx = [jnp.maximum(jnp.maximum(jnp.max(t[0], -1, keepdims=True), jnp.max(t[1], -1, keepdims=True)),
                      jnp.maximum(jnp.max(t[2], -1, keepdims=True), sk)) for t, sk in zip(s, sink)]
    e = [[jnp.exp(tx - m) for tx in t] for t, m in zip(s, mx)]
    e_sink = [jnp.exp(sk - m) for sk, m in zip(sink, mx)]
    inv = [1.0 / (jnp.sum(t[0], -1, keepdims=True) + jnp.sum(t[1], -1, keepdims=True)
                  + jnp.sum(t[2], -1, keepdims=True) + es) for t, es in zip(e, e_sink)]
    return [[tx * i for tx in t] for t, i in zip(e, inv)], [es * i for es, i in zip(e_sink, inv)]


def _head_cols(i):
    return slice(i * HEAD_DIM, (i + 1) * HEAD_DIM)


def _attn_operands(refs):
    q_ref, kp_ref, kc_ref, km_ref, vp_ref, vc_ref, vm_ref, s_ref = refs
    qs = [q_ref[:, _head_cols(i)] * (HEAD_DIM ** -0.5) for i in range(Q_HEADS)]
    k3 = [[ref[:, _head_cols(h)] for ref in (kp_ref, kc_ref, km_ref)] for h in range(KV_HEADS)]
    v3 = [[ref[:, _head_cols(h)] for ref in (vp_ref, vc_ref, vm_ref)] for h in range(KV_HEADS)]
    return (qs, [k3[i // GROUP] for i in range(Q_HEADS)], [v3[i // GROUP] for i in range(Q_HEADS)],
            [s_ref[:, i:i + 1] for i in range(Q_HEADS)])


def _attention(q, k, v, sinks, *, name):
    lp = q.shape[0]
    nb = lp // BLOCK
    prev = lambda i: (jnp.maximum(i - 1, 0), 0)
    cur = lambda i: (i, 0)
    meta = lambda i: (0, 0)
    kv = lambda index: pl.BlockSpec((BLOCK, KV_W), index)

    def body(*refs):
        o_ref = refs[-1]
        qs, k3s, v3s, sink = _attn_operands(refs[:-1])
        p, _ = _attn_probs(qs, k3s, sink, _attn_masks(pl.program_id(0)))
        out = [_dot(ph[0], v3[0]) + _dot(ph[1], v3[1]) + _dot(ph[2], v3[2]) for ph, v3 in zip(p, v3s)]
        for i in range(Q_HEADS):
            o_ref[:, _head_cols(i)] = out[i].astype(o_ref.dtype)

    return pl.pallas_call(
        body, name=name, grid=(nb,),
        in_specs=[pl.BlockSpec((BLOCK, Q_W), cur), kv(prev), kv(cur), kv(meta), kv(prev), kv(cur), kv(meta),
                  _full((1, Q_HEADS))],
        out_specs=pl.BlockSpec((BLOCK, Q_W), cur),
        out_shape=jax.ShapeDtypeStruct((lp, Q_W), bf16),
        compiler_params=_params(("parallel",)),
    )(q, k, k, k, v, v, v, sinks)


def _attention_bwd(q, k, v, sinks, out, do, *, name):
    lp = q.shape[0]
    nb = lp // BLOCK
    cur = lambda n: (jnp.minimum(n, nb - 1), 0)
    prev = lambda n: (jnp.maximum(jnp.minimum(n, nb - 1) - 1, 0), 0)
    behind = lambda n: (jnp.maximum(n - 1, 0), 0)
    meta = lambda n: (0, 0)
    kv = lambda index: pl.BlockSpec((BLOCK, KV_W), index)
    scale = HEAD_DIM ** -0.5

    def body(*refs):
        ins, fwd_ref, do_ref = refs[:8], refs[8], refs[9]
        dq_ref, dk_ref, dv_ref, dkm_ref, dvm_ref, ds_ref, carry_k, carry_v = refs[10:]
        n = pl.program_id(0)

        @pl.when(n == 0)
        def _():
            for ref in (dkm_ref, dvm_ref, ds_ref, carry_k, carry_v):
                ref[...] = jnp.zeros_like(ref)

        @pl.when(n < nb)
        def _():
            qs, k3s, v3s, sink = _attn_operands(ins)
            do = [do_ref[:, _head_cols(i)] for i in range(Q_HEADS)]
            p, p_sink = _attn_probs(qs, k3s, sink, _attn_masks(n))
            delta = [jnp.sum(d * fwd_ref[:, _head_cols(i)].astype(f32), -1, keepdims=True) for i, d in enumerate(do)]
            dp = [[_dot(d, vx, "nt") for vx in v3] for d, v3 in zip(do, v3s)]
            ds = [[px * (dx - dl) for px, dx in zip(ph, dh)] for ph, dh, dl in zip(p, dp, delta)]
            dq = [_dot(dsh[0], k3[0]) + _dot(dsh[1], k3[1]) + _dot(dsh[2], k3[2]) for dsh, k3 in zip(ds, k3s)]
            for i in range(Q_HEADS):
                dq_ref[:, _head_cols(i)] = dq[i] * scale
                ds_ref[:, i:i + 1] -= jnp.sum(p_sink[i] * delta[i], axis=0, keepdims=True)
            for h in range(KV_HEADS):
                group = slice(h * GROUP, (h + 1) * GROUP)
                q_all = jnp.concatenate(qs[group], axis=0)
                do_all = jnp.concatenate(do[group], axis=0)
                dk3 = [_dot(jnp.concatenate([dsh[x] for dsh in ds[group]], axis=0), q_all, "tn") for x in range(3)]
                dv3 = [_dot(jnp.concatenate([ph[x] for ph in p[group]], axis=0), do_all, "tn") for x in range(3)]
                hs = _head_cols(h)
                for out_ref, carry, meta_ref, d3 in ((dk_ref, carry_k, dkm_ref, dk3),
                                                     (dv_ref, carry_v, dvm_ref, dv3)):
                    out_ref[:, hs] = carry[:, hs] + d3[0]
                    carry[:, hs] = d3[1]
                    meta_ref[:, hs] += d3[2]

        @pl.when(n == nb)
        def _():
            dk_ref[...] = carry_k[...]
            dv_ref[...] = carry_v[...]

    kv_shape = jax.ShapeDtypeStruct((lp, KV_W), f32)
    one_shape = jax.ShapeDtypeStruct((BLOCK, KV_W), f32)
    return pl.pallas_call(
        body, name=name, grid=(nb + 1,),
        in_specs=[pl.BlockSpec((BLOCK, Q_W), cur), kv(prev), kv(cur), kv(meta), kv(prev), kv(cur), kv(meta),
                  _full((1, Q_HEADS)), pl.BlockSpec((BLOCK, Q_W), cur), pl.BlockSpec((BLOCK, Q_W), cur)],
        out_specs=[pl.BlockSpec((BLOCK, Q_W), cur), kv(behind), kv(behind), kv(meta), kv(meta),
                   _full((1, Q_HEADS))],
        out_shape=[jax.ShapeDtypeStruct((lp, Q_W), f32), kv_shape, kv_shape, one_shape, one_shape,
                   jax.ShapeDtypeStruct((1, Q_HEADS), f32)],
        scratch_shapes=[pltpu.VMEM((BLOCK, KV_W), f32), pltpu.VMEM((BLOCK, KV_W), f32)],
        compiler_params=_params(("arbitrary",)),
    )(q, k, k, k, v, v, v, sinks, out, do)


@jax.custom_vjp
def _known_inverse(l, x):
    return x


def _known_inverse_fwd(l, x):
    return x, x


def _known_inverse_bwd(x, ct):
    return _dot(_dot(x, ct, "tn"), x, "nt"), jnp.zeros_like(x)


_known_inverse.defvjp(_known_inverse_fwd, _known_inverse_bwd)


@jax.custom_vjp
def _decayed(x, c):
    return (x * jnp.exp(c)).astype(bf16).astype(f32)


def _decayed_fwd(x, c):
    e = jnp.exp(c)
    out = (x * e).astype(bf16).astype(f32)
    return out, (e, out)


def _decayed_bwd(res, ct):
    e, out = res
    return ct * e, ct * out


_decayed.defvjp(_decayed_fwd, _decayed_bwd)


@jax.custom_vjp
def _pair(x, y):
    return _dot(x, y, "nt")


def _pair_fwd(x, y):
    return _dot(x, y, "nt"), (x, y)


def _pair_bwd(res, ct):
    x, y = res
    hi = ct.astype(bf16)
    lo = (ct - hi.astype(f32)).astype(bf16)
    return _dot(hi, y) + _dot(lo, y), _dot(hi, x, "tn") + _dot(lo, x, "tn")


_pair.defvjp(_pair_fwd, _pair_bwd)


def _scan_chunk(s0, r, lw, k, v, a, b, inv=None):
    t = r[0].shape[0]
    ii = lax.broadcasted_iota(jnp.int32, (t, t), 0)
    jj = lax.broadcasted_iota(jnp.int32, (t, t), 1)
    incl = jj <= ii
    strict = jj < ii
    tri = incl.astype(f32)
    eye = jnp.where(ii == jj, 1.0, 0.0)
    cl = [_const_dot(tri, x) for x in lw]
    mid = [c[t // 2 - 1:t // 2, :] for c in cl]
    s0 = [s * jnp.exp(m) for s, m in zip(s0, mid)]
    cl = [c - m for c, m in zip(cl, mid)]
    rt = [_decayed(x, c) for x, c in zip(r, cl)]
    at = [_decayed(x, c - l) for x, c, l in zip(a, cl, lw)]
    bt = [_decayed(x, -c) for x, c in zip(b, cl)]
    kt = [_decayed(x, -c) for x, c in zip(k, cl)]
    l_ab = [jnp.where(strict, _pair(x, y), 0.0) for x, y in zip(at, bt)]
    l_ak = [jnp.where(strict, _pair(x, y), 0.0) for x, y in zip(at, kt)]
    r_b = [jnp.where(incl, _pair(x, y), 0.0) for x, y in zip(rt, bt)]
    r_k = [jnp.where(incl, _pair(x, y), 0.0) for x, y in zip(rt, kt)]
    if inv is None:
        inv = [eye + x for x in l_ab]
        pw = l_ab
        for _ in range(int(math.log2(t)) - 1):
            pw = [_dot(x, x) for x in pw]
            inv = [x + _dot(x, y) for x, y in zip(inv, pw)]
    else:
        inv = [_known_inverse(x, y) for x, y in zip(l_ab, inv)]
    rhs = [_dot(x, s, "nt") + _dot(m, y) for x, s, m, y in zip(at, s0, l_ak, v)]
    u = [_dot(x, y) for x, y in zip(inv, rhs)]
    y_s = [_dot(x, s, "nt") for x, s in zip(rt, s0)]
    y = [ys + _dot(m, uu) + _dot(n, vv) for ys, m, uu, n, vv in zip(y_s, r_b, u, r_k, v)]
    grow = [s + _dot(uu, x, "tn") + _dot(vv, z, "tn") for s, uu, x, vv, z in zip(s0, u, bt, v, kt)]
    s1 = [g * jnp.exp(c[t - 1:t, :]) for g, c in zip(grow, cl)]
    return y, s1, inv


def _head_rows(h):
    return slice(h * RWKV_HEAD, (h + 1) * RWKV_HEAD)


def _per_head(ref):
    return [ref[:, _head_rows(h)] for h in range(RWKV_HEADS)]


def _scan(r, lw, k, v, a, b, *, name):
    lp = r.shape[0]
    nc = lp // CHUNK
    row = pl.BlockSpec((CHUNK, RWKV_DIM), lambda c: (c, 0))

    def body(r_ref, lw_ref, k_ref, v_ref, a_ref, b_ref, y_ref, s_ref, inv_ref, state):
        @pl.when(pl.program_id(0) == 0)
        def _():
            state[...] = jnp.zeros_like(state)

        s_ref[...] = state[...]
        s0 = [state[_head_rows(h), :] for h in range(RWKV_HEADS)]
        y, s1, inv = _scan_chunk(s0, *[_per_head(ref) for ref in (r_ref, lw_ref, k_ref, v_ref, a_ref, b_ref)])
        for h in range(RWKV_HEADS):
            y_ref[:, _head_rows(h)] = y[h]
            state[_head_rows(h), :] = s1[h]
            inv_ref[h * CHUNK:(h + 1) * CHUNK, :] = inv[h].astype(inv_ref.dtype)

    return pl.pallas_call(
        body, name=name, grid=(nc,), in_specs=[row] * 6,
        out_specs=[row, pl.BlockSpec((RWKV_DIM, RWKV_HEAD), lambda c: (c, 0)),
                   pl.BlockSpec((RWKV_HEADS * CHUNK, CHUNK), lambda c: (c, 0))],
        out_shape=[jax.ShapeDtypeStruct((lp, RWKV_DIM), f32), jax.ShapeDtypeStruct((nc * RWKV_DIM, RWKV_HEAD), f32),
                   jax.ShapeDtypeStruct((nc * RWKV_HEADS * CHUNK, CHUNK), bf16)],
        scratch_shapes=[pltpu.VMEM((RWKV_DIM, RWKV_HEAD), f32)],
        compiler_params=_params(("arbitrary",)),
    )(r, lw, k, v, a, b)


def _scan_bwd(r, lw, k, v, a, b, states, inverses, dy, *, name):
    lp = r.shape[0]
    nc = lp // CHUNK
    back = lambda c: (nc - 1 - c, 0)
    row = pl.BlockSpec((CHUNK, RWKV_DIM), back)

    def body(r_ref, lw_ref, k_ref, v_ref, a_ref, b_ref, s_ref, inv_ref, dy_ref,
             dr_ref, dlw_ref, dk_ref, dv_ref, da_ref, db_ref, dstate):
        @pl.when(pl.program_id(0) == 0)
        def _():
            dstate[...] = jnp.zeros_like(dstate)

        outs = (dr_ref, dlw_ref, dk_ref, dv_ref, da_ref, db_ref)
        s0 = [s_ref[_head_rows(h), :] for h in range(RWKV_HEADS)]
        inv = [inv_ref[h * CHUNK:(h + 1) * CHUNK, :].astype(f32) for h in range(RWKV_HEADS)]
        _, vjp = jax.vjp(lambda *args: _scan_chunk(*args, inv=inv)[:2], s0,
                         *[_per_head(ref) for ref in (r_ref, lw_ref, k_ref, v_ref, a_ref, b_ref)])
        g = vjp((_per_head(dy_ref), [dstate[_head_rows(h), :] for h in range(RWKV_HEADS)]))
        for h in range(RWKV_HEADS):
            dstate[_head_rows(h), :] = g[0][h]
            for o_ref, gv in zip(outs, g[1:]):
                o_ref[:, _head_rows(h)] = gv[h]

    shape = jax.ShapeDtypeStruct((lp, RWKV_DIM), f32)
    return pl.pallas_call(
        body, name=name, grid=(nc,),
        in_specs=[row] * 6 + [pl.BlockSpec((RWKV_DIM, RWKV_HEAD), back),
                              pl.BlockSpec((RWKV_HEADS * CHUNK, CHUNK), back), row],
        out_specs=[row] * 6, out_shape=[shape] * 6,
        scratch_shapes=[pltpu.VMEM((RWKV_DIM, RWKV_HEAD), f32)],
        compiler_params=_params(("arbitrary",)),
    )(r, lw, k, v, a, b, states, inverses, dy)


def _loss_head(h2, target, g_final, *, name):
    lp = h2.shape[0]
    tm = BLOCK
    front_tiles = FRONT // tm

    def body(h_ref, t_ref, g_ref, loss_ref, dh_ref, dg_ref):
        i = pl.program_id(0)
        real = i >= front_tiles

        def tile_loss(hv, gv):
            err = _rms(hv, gv) - t_ref[...]
            return jnp.where(real, 0.5 * jnp.sum(jnp.mean(err * err, axis=-1, keepdims=True)), 0.0)

        loss, (dh, dg) = jax.value_and_grad(tile_loss, argnums=(0, 1))(h_ref[...], g_ref[...])

        @pl.when(i == 0)
        def _():
            loss_ref[...] = jnp.zeros_like(loss_ref)
            dg_ref[...] = jnp.zeros_like(dg_ref)

        loss_ref[...] += jnp.full(loss_ref.shape, loss, f32)
        dg_ref[...] += dg
        dh_ref[...] = dh

    return pl.pallas_call(
        body, name=name, grid=(lp // tm,),
        in_specs=[pl.BlockSpec((tm, D_MODEL), lambda i: (i, 0)),
                  pl.BlockSpec((tm, D_MODEL), lambda i: (jnp.maximum(i - front_tiles, 0), 0)),
                  _full(g_final.shape)],
        out_specs=[_full((8, 128)), pl.BlockSpec((tm, D_MODEL), lambda i: (i, 0)), _full(g_final.shape)],
        out_shape=[jax.ShapeDtypeStruct((8, 128), f32), jax.ShapeDtypeStruct((lp, D_MODEL), f32),
                   jax.ShapeDtypeStruct(g_final.shape, f32)],
        compiler_params=_params(("arbitrary",)),
    )(h2, target, g_final)


def _input_norm_bwd(h0, g, du, dh1, *, name):
    lp = h0.shape[0]
    tm = FRONT

    def body(h_ref, g_ref, du_ref, dh1_ref, dx_ref, front_ref, dg_ref):
        i = pl.program_id(0)
        _, vjp = jax.vjp(lambda hv, gv: (_rms(hv, gv), hv), h_ref[...], g_ref[...])
        dh, dg = vjp((du_ref[...].astype(f32), dh1_ref[...]))

        @pl.when(i == 0)
        def _():
            dg_ref[...] = jnp.zeros_like(dg_ref)
            front_ref[...] = dh

        dg_ref[...] += dg
        dx_ref[...] = dh

    tile = pl.BlockSpec((tm, D_MODEL), lambda i: (i, 0))
    return pl.pallas_call(
        body, name=name, grid=(lp // tm,),
        in_specs=[tile, _full(g.shape), tile, tile],
        out_specs=[pl.BlockSpec((tm, D_MODEL), lambda i: (jnp.maximum(i - 1, 0), 0)), _full((tm, D_MODEL)),
                   _full(g.shape)],
        out_shape=[jax.ShapeDtypeStruct((lp - tm, D_MODEL), f32), jax.ShapeDtypeStruct((tm, D_MODEL), f32),
                   jax.ShapeDtypeStruct(g.shape, f32)],
        compiler_params=_params(("arbitrary",)),
    )(h0, g, du, dh1)


def _local_step(x, target, meta, p, early_weights=None, late_weights=None, emit=None):
    emit = emit or (lambda group, grads: 0.0)
    seq = x.shape[0]
    lp = seq + FRONT
    h0 = jnp.concatenate([jnp.zeros((PAD, D_MODEL), f32), meta, x], axis=0)
    cos_t, sin_t, swap = _rope_tables(lp)
    hsum = _head_sum_matrix(RWKV_DIM, RWKV_HEAD)
    hmean = hsum / RWKV_HEAD
    post_params = [p["ln_w"], p["ln_b"], p["r_k"], hmean]

    (u,) = _rowwise(lambda hv, g: (_rms(hv, g),), [h0], [p["norm_mix_g"]], [(D_MODEL, bf16)], name="norm_mix")
    if early_weights is not None:
        p = {**p, **early_weights(u)}
    prep_params = [p["w0"], p["w2"], p["a0"], p["a2"], p["g2"], p["k_k"], p["k_a"], hsum]
    qkv, p_rkv, p_lora, gates = _proj_in(u, p["w_in_lr"], p["b_in"], [ATTN_PROJ, RKV_W, LORA_W, 2 * D_MODEL],
                                         name="proj_in", zero_rows_below=PAD)

    q, k, v = _rowwise(_attn_prep, [qkv, cos_t, sin_t], [swap], [(Q_W, bf16), (KV_W, bf16), (KV_W, bf16)],
                       name="attn_prep")
    y_attn = _attention(q, k, v, p["sinks"], name="attention")

    mix_rkv, mix_lora = p["mix"][:, :RKV_W], p["mix"][:, RKV_W:]
    r_, lw_, k_, v_, a_, b_, g_ = _mixer_inputs([p_rkv, p_lora], [mix_rkv, mix_lora], prep_params,
                                                name="mixer_inputs")
    y_scan, states, inverses = _scan(r_, lw_, k_, v_, a_, b_, name="wkv_scan")
    (y_rwkv,) = _rowwise(_rwkv_post, [y_scan, r_, k_, v_, g_], post_params, [(RWKV_DIM, bf16)], name="rwkv_post")

    if late_weights is not None:
        p = {**p, **late_weights(y_rwkv)}
    br_a, br_r, merged = _branch_merge(y_attn, y_rwkv, p["w_br_attn_t"], p["w_br_rwkv_t"], gates, name="branch_merge")
    h1, f = _residual_norm(merged, p["w_o"], h0, p["norm_ffn_g"], name="out_proj")
    gate, up, act = _ffn_in(f, p["w_gate_t"], p["w_up_t"], name="ffn_in")
    h2 = _mm(act, p["w_down"], "nn", name="ffn_down", add=h1)

    loss8, dh2, d_final_g = _loss_head(h2, target, p["norm_final_g"], name="loss_head")
    dgate, dup = _ffn_in_bwd(dh2, p["w_down"], gate, up, name="ffn_in_bwd")
    d_w_down = _mm_tn(act, dh2, name="dw_down")
    d_w_gate_t = _mm_tn(dgate, f, name="dw_gate")
    d_w_up_t = _mm_tn(dup, f, name="dw_up")
    zero = emit("ffn", dict(w_down=d_w_down, w_gate_t=d_w_gate_t, w_up_t=d_w_up_t))
    dh1, d_ffn_g = _residual_norm_bwd([dgate, dup], [p["w_gate_t"], p["w_up_t"]], h1, p["norm_ffn_g"] + zero, dh2,
                                      name="norm_ffn_bwd")
    dgates, dbr_a, dbr_r, dy_attn, dy_rwkv = _branch_merge_bwd(
        dh1, p["w_o"], gates, br_a, br_r, p["w_br_attn_t"], p["w_br_rwkv_t"], name="branch_merge_bwd")
    d_w_o = _mm_tn(merged, dh1, name="dw_o")
    d_w_br_attn_t = _mm_tn(dbr_a, y_attn, name="dw_br_attn")
    d_w_br_rwkv_t = _mm_tn(dbr_r, y_rwkv, name="dw_br_rwkv")
    zero = emit("branch", dict(w_o=d_w_o, w_br_attn_t=d_w_br_attn_t, w_br_rwkv_t=d_w_br_rwkv_t))

    post_params = [p["ln_w"] + zero, p["ln_b"], p["r_k"], hmean]
    res = _rowwise_bwd(_rwkv_post, [y_scan, r_, k_, v_, g_], post_params, [[dy_rwkv]], name="rwkv_post_bwd",
                       diff_rows=[True] * 5, diff_params=[True, True, True, False])
    dy_scan, dr_p, dk_p, dv_p, dg_p, d_ln_w, d_ln_b, d_r_k = res
    dr_s, dlw_s, dk_s, dv_s, da_s, db_s = _scan_bwd(r_, lw_, k_, v_, a_, b_, states, inverses, dy_scan,
                                                    name="wkv_scan_bwd")
    res = _mixer_inputs_bwd([p_rkv, p_lora], [mix_rkv, mix_lora], prep_params,
                            [[dr_s, dr_p], [dlw_s], [dk_s, dk_p], [dv_s, dv_p], [da_s], [db_s], [dg_p]],
                            name="mixer_inputs_bwd")
    dp_rkv, dp_lora, d_mix_rkv, d_mix_lora, d_w0, d_w2, d_a0, d_a2, d_g2, d_k_k, d_k_a = res

    dq, dk, dv, dkm, dvm, d_sinks = _attention_bwd(q, k, v, p["sinks"], y_attn, dy_attn, name="attention_bwd")
    rest = jnp.zeros((lp - BLOCK, KV_W), f32)
    dkm, dvm = jnp.concatenate([dkm, rest], axis=0), jnp.concatenate([dvm, rest], axis=0)
    (dqkv,) = _rowwise_bwd(_attn_prep, [qkv, cos_t, sin_t], [swap], [[dq], [dk, dkm], [dv, dvm]], name="attn_prep_bwd",
                           diff_rows=[True, False, False], diff_params=[False], out_dtypes=[bf16])

    d_w_qkv_t, db_qkv = _mm_tn(dqkv, u, name="dw_qkv", colsum=True)
    d_w_rkv_t, db_rkv = _mm_tn(dp_rkv, u, name="dw_rkv", colsum=True)
    d_w_lora_t, db_lora = _mm_tn(dp_lora, u, name="dw_lora", colsum=True)
    d_w_gates_t, db_gates = _mm_tn(dgates, u, name="dw_gates", colsum=True)
    d_w_in_t = jnp.concatenate([d_w_qkv_t, d_w_rkv_t, d_w_lora_t, d_w_gates_t], axis=0)
    zero = emit("input", dict(w_in_t=d_w_in_t, g2=d_g2, w2=d_w2, a2=d_a2))
    du = _proj_in_bwd([dqkv, dp_rkv, dp_lora, dgates], p["w_in_lr"], name="d_u")
    dx, d_front, d_mix_g = _input_norm_bwd(h0, p["norm_mix_g"] + zero, du, dh1, name="norm_mix_bwd")

    grads = dict(
        w_in_t=d_w_in_t,
        b_in=jnp.concatenate([db_qkv, db_rkv, db_lora, db_gates], axis=1),
        mix=jnp.concatenate([d_mix_rkv, d_mix_lora], axis=1),
        norm_mix_g=d_mix_g, sinks=d_sinks, w0=d_w0, w2=d_w2, a0=d_a0, a2=d_a2, g2=d_g2, k_k=d_k_k, k_a=d_k_a,
        r_k=d_r_k, ln_w=d_ln_w, ln_b=d_ln_b, w_br_attn_t=d_w_br_attn_t, w_br_rwkv_t=d_w_br_rwkv_t, w_o=d_w_o,
        norm_ffn_g=d_ffn_g, w_gate_t=d_w_gate_t, w_up_t=d_w_up_t, w_down=d_w_down, norm_final_g=d_final_g,
        meta=d_front[PAD:],
    )
    return loss8[0, 0], dx, grads


def _position():
    return lax.axis_index("x"), lax.axis_index("y"), lax.axis_index("c")


def _other_chips(x, y):
    return [(1 - x, y), (x, 1 - y), (1 - x, 1 - y)]


_HBM = pl.BlockSpec(memory_space=pltpu.HBM)
_SEM = pl.BlockSpec(memory_space=pltpu.SEMAPHORE)
_EFFECT = pltpu.SideEffectType.DATAFLOW_SIDE_EFFECTING


def _landing_zone(src, kind):
    shape = {"whole": (N_CHIPS,) + src.shape, "half": (2, N_CHIPS, src.shape[0], src.shape[1] // 2),
             "slab": (3,) + src.shape[1:], "sibling": src.shape}[kind]
    return lax.empty(shape, src.dtype)


def _copies_per_source(kind):
    return 1 if kind == "sibling" else 3


def _chip_copies(src_refs, land_refs, send_sems, recv_sems, kind):
    x, y, c = _position()
    if kind == "sibling":
        return [pltpu.make_async_remote_copy(
            src_ref=src, dst_ref=land, send_sem=send_sems.at[a], recv_sem=recv_sems.at[a],
            device_id=(x, y, 1 - c), device_id_type=MESH) for a, (src, land) in enumerate(zip(src_refs, land_refs))]
    copies = []
    for a, (src, land) in enumerate(zip(src_refs, land_refs)):
        for j, (px, py) in enumerate(_other_chips(x, y)):
            if kind == "whole":
                src_ref, dst_ref = src, land.at[2 * x + y]
            elif kind == "half":
                half = src.shape[1] // 2
                src_ref, dst_ref = src.at[:, pl.ds(pl.multiple_of(c * half, half), half)], land.at[c, 2 * x + y]
            else:
                src_ref, dst_ref = src.at[2 * px + py], land.at[j]
            copies.append(pltpu.make_async_remote_copy(
                src_ref=src_ref, dst_ref=dst_ref, send_sem=send_sems.at[3 * a + j], recv_sem=recv_sems.at[3 * a + j],
                device_id=(px, py, c), device_id_type=MESH))
    return copies


def _exchange_start(srcs, *, kind, name):
    n = len(srcs)
    lands = [_landing_zone(s, kind) for s in srcs]

    def body(*refs):
        for cp in _chip_copies(refs[:n], refs[n:2 * n], refs[2 * n], refs[2 * n + 1], kind):
            cp.start()
        refs[-1][...] = jnp.zeros_like(refs[-1])

    res = pl.pallas_call(
        body, name=name,
        out_shape=(pltpu.SemaphoreType.DMA((_copies_per_source(kind) * n,)),
                   pltpu.SemaphoreType.DMA((_copies_per_source(kind) * n,)),
                   *[pltpu.HBM(a.shape, a.dtype) for a in srcs + lands], jax.ShapeDtypeStruct((8, 128), f32)),
        in_specs=[_HBM] * (2 * n),
        out_specs=(_SEM, _SEM, *[_HBM] * (2 * n), pl.BlockSpec(memory_space=pltpu.VMEM)),
        input_output_aliases={i: 2 + i for i in range(2 * n)},
        compiler_params=pltpu.CompilerParams(has_side_effects=_EFFECT),
    )(*[pltpu.with_memory_space_constraint(a, pltpu.HBM) for a in srcs + lands])
    return res[0], res[1], list(res[2:2 + n]), list(res[2 + n:2 + 2 * n]), res[-1]


def _exchange_wait(handle, after, *, kind, name):
    send_sems, recv_sems, srcs, lands, _ = handle
    n = len(srcs)

    def body(*refs):
        for cp in _chip_copies(refs[:n], refs[n:2 * n], refs[2 * n], refs[2 * n + 1], kind):
            cp.wait_send()
            cp.wait_recv()

    res = pl.pallas_call(
        body, name=name,
        out_shape=tuple(pltpu.HBM(a.shape, a.dtype) for a in srcs + lands),
        in_specs=[_HBM] * (2 * n) + [_SEM, _SEM, pl.BlockSpec(memory_space=pl.ANY)],
        out_specs=tuple([_HBM] * (2 * n)),
        input_output_aliases={i: i for i in range(2 * n)},
        compiler_params=pltpu.CompilerParams(has_side_effects=_EFFECT),
    )(*srcs, *lands, send_sems, recv_sems, after)
    return list(res[:n]), list(res[n:])


def _sum_own_and_received(g, recv, *, name):
    _, r, w = g.shape
    tm = _tile(r)
    if g.dtype == bf16 and tm % 16:
        tm = r
    x, y, _ = _position()
    me = jnp.reshape(2 * x + y, (1,)).astype(jnp.int32)

    def body(me_ref, g_ref, r_ref, o_ref):
        o_ref[...] = (g_ref[0].astype(f32) + r_ref[0].astype(f32)) + (r_ref[1].astype(f32) + r_ref[2].astype(f32))

    return pl.pallas_call(
        body, name=name,
        grid_spec=pltpu.PrefetchScalarGridSpec(
            num_scalar_prefetch=1, grid=(r // tm,),
            in_specs=[pl.BlockSpec((1, tm, w), lambda i, me_ref: (me_ref[0], i, 0)),
                      pl.BlockSpec((3, tm, w), lambda i, me_ref: (0, i, 0))],
            out_specs=pl.BlockSpec((tm, w), lambda i, me_ref: (i, 0))),
        out_shape=jax.ShapeDtypeStruct((r, w), f32),
        compiler_params=_params(("parallel",)),
    )(me, g, recv)


def _swap_cores(arrs, *, name):
    n = len(arrs)

    def body(*refs):
        x, y, c = _position()
        copies = [pltpu.make_async_remote_copy(
            src_ref=refs[i], dst_ref=refs[n + i], send_sem=refs[2 * n].at[i], recv_sem=refs[2 * n + 1].at[i],
            device_id=(x, y, 1 - c), device_id_type=MESH) for i in range(n)]
        for cp in copies:
            cp.start()
        for cp in copies:
            cp.wait_recv()
        for cp in copies:
            cp.wait_send()

    return pl.pallas_call(
        body, name=name,
        in_specs=[pl.BlockSpec(memory_space=pl.ANY)] * n,
        out_specs=[pl.BlockSpec(memory_space=pl.ANY)] * n,
        out_shape=[jax.ShapeDtypeStruct(a.shape, a.dtype) for a in arrs],
        scratch_shapes=[pltpu.SemaphoreType.DMA((n,)), pltpu.SemaphoreType.DMA((n,))],
    )(*arrs)


def _swap_halves(zone, *, name):
    def body(z_ref, o_ref, send_sems, recv_sems):
        x, y, c = _position()
        mine = [pltpu.make_async_remote_copy(
            src_ref=o_ref.at[c, 2 * px + py], dst_ref=o_ref.at[c, 2 * px + py], send_sem=send_sems.at[j],
            recv_sem=recv_sems.at[j], device_id=(x, y, 1 - c), device_id_type=MESH)
            for j, (px, py) in enumerate(_other_chips(x, y))]
        for cp in mine:
            cp.start()
        for j, (px, py) in enumerate(_other_chips(x, y)):
            pltpu.make_async_remote_copy(
                src_ref=o_ref.at[c, 2 * px + py], dst_ref=o_ref.at[1 - c, 2 * px + py], send_sem=send_sems.at[j],
                recv_sem=recv_sems.at[j], device_id=(x, y, 1 - c), device_id_type=MESH).wait_recv()
        for cp in mine:
            cp.wait_send()

    return pl.pallas_call(
        body, name=name,
        in_specs=[pl.BlockSpec(memory_space=pl.ANY)], out_specs=pl.BlockSpec(memory_space=pl.ANY),
        out_shape=jax.ShapeDtypeStruct(zone.shape, zone.dtype), input_output_aliases={0: 0},
        scratch_shapes=[pltpu.SemaphoreType.DMA((3,)), pltpu.SemaphoreType.DMA((3,))],
    )(zone)


def _all_reduce_small(a, after, *, name):
    rows, w = a.shape

    def body(a_ref, after_ref, o_ref, buf, send_sems, recv_sems):
        x, y, c = _position()
        me = 4 * x + 2 * y + c
        buf[0] = a_ref[...]
        sends = []
        for rel in range(1, N_DEV):
            peer = ((1 - x) if rel & 4 else x, (1 - y) if rel & 2 else y, (1 - c) if rel & 1 else c)
            cp = pltpu.make_async_remote_copy(
                src_ref=a_ref, dst_ref=buf.at[rel], send_sem=send_sems.at[rel - 1], recv_sem=recv_sems.at[rel - 1],
                device_id=peer, device_id_type=MESH)
            cp.start()
            sends.append(cp)
        for cp in sends:
            cp.wait_recv()
        for cp in sends:
            cp.wait_send()
        acc = buf[jnp.bitwise_xor(me, 0)]
        for d in range(1, N_DEV):
            acc = acc + buf[jnp.bitwise_xor(me, d)]
        o_ref[...] = acc

    return pl.pallas_call(
        body, name=name,
        in_specs=[pl.BlockSpec(memory_space=pltpu.VMEM), pl.BlockSpec(memory_space=pl.ANY)],
        out_specs=pl.BlockSpec(memory_space=pltpu.VMEM),
        out_shape=jax.ShapeDtypeStruct((rows, w), f32),
        scratch_shapes=[pltpu.VMEM((N_DEV, rows, w), f32), pltpu.SemaphoreType.DMA((N_DEV - 1,)),
                        pltpu.SemaphoreType.DMA((N_DEV - 1,))],
    )(a, after)


def _adamw(w, g_parts, m, v, *, name, transposed=False):
    rows, cols = w.shape
    if transposed:
        tm = 256 if rows % 256 == 0 else rows
        g_spec = pl.BlockSpec((cols, tm), lambda i: (0, i))
    else:
        tm = _tile(rows, 256)
        g_spec = pl.BlockSpec((tm, cols), lambda i: (i, 0))
    n = len(g_parts)

    def body(*refs):
        w_ref, m_ref, v_ref = refs[0], refs[1 + n], refs[2 + n]
        g_ref, d_ref, nm_ref, nv_ref = refs[3 + n:]
        gv = refs[1][...]
        for part in refs[2:1 + n]:
            gv = gv + part[...]
        if transposed:
            gv = gv.T
        g_ref[...] = gv
        nm = ADAM_B1 * m_ref[...] + (1.0 - ADAM_B1) * gv
        nv = ADAM_B2 * v_ref[...] + (1.0 - ADAM_B2) * (gv * gv)
        m_hat = nm / (1.0 - ADAM_B1 ** ADAM_STEP)
        v_hat = nv / (1.0 - ADAM_B2 ** ADAM_STEP)
        d_ref[...] = -ADAM_LR * (m_hat / (jnp.sqrt(v_hat) + ADAM_EPS) + ADAM_WD * w_ref[...])
        nm_ref[...] = nm
        nv_ref[...] = nv

    spec = pl.BlockSpec((tm, cols), lambda i: (i, 0))
    shape = jax.ShapeDtypeStruct((rows, cols), f32)
    return pl.pallas_call(
        body, name=name, grid=(rows // tm,), in_specs=[spec] + [g_spec] * n + [spec] * 2,
        out_specs=[spec] * 4, out_shape=[shape] * 4,
        compiler_params=_params(("parallel",)),
    )(w, *g_parts, m, v)


def _pad_rows(a, rows):
    return jnp.concatenate([a, jnp.zeros((rows - a.shape[0], a.shape[1]), a.dtype)], axis=0) if rows > a.shape[0] else a


_SMALL = (("norm_mix_g", D_MODEL), ("b_in", D_IN), ("sinks", Q_HEADS), ("mix", RWKV_PROJ), ("w0", RWKV_DIM),
          ("a0", RWKV_DIM), ("k_k", RWKV_DIM), ("k_a", RWKV_DIM), ("r_k", RWKV_DIM), ("ln_w", RWKV_DIM),
          ("ln_b", RWKV_DIM), ("norm_ffn_g", D_MODEL), ("norm_final_g", D_MODEL))


def _pack_small(d):
    flat = jnp.concatenate([d[n].reshape(-1).astype(f32) for n, _ in _SMALL])
    return flat


def _unpack_small(flat):
    out, off = {}, 0
    for n, size in _SMALL:
        out[n] = flat[off:off + size]
        off += size
    return out


_SMALL_TOTAL = sum(s for _, s in _SMALL)


def kernel(x, meta_tokens, norm_mix_g, w_in, b_in, attn_sinks, rwkv_mix, rwkv_w0, rwkv_w2, rwkv_a0, rwkv_a2, rwkv_g2, rwkv_k_k, rwkv_k_a, rwkv_r_k, rwkv_ln_w, rwkv_ln_b, w_br_attn, w_br_rwkv, w_o, norm_ffn_g, w_ffn_gate, w_ffn_up, w_ffn_down, norm_final_g, loss_target, m_meta_tokens, m_norm_mix_g, m_w_in, m_b_in, m_attn_sinks, m_rwkv_mix, m_rwkv_w0, m_rwkv_w2, m_rwkv_a0, m_rwkv_a2, m_rwkv_g2, m_rwkv_k_k, m_rwkv_k_a, m_rwkv_r_k, m_rwkv_ln_w, m_rwkv_ln_b, m_w_br_attn, m_w_br_rwkv, m_w_o, m_norm_ffn_g, m_w_ffn_gate, m_w_ffn_up, m_w_ffn_down, m_norm_final_g, v_meta_tokens, v_norm_mix_g, v_w_in, v_b_in, v_attn_sinks, v_rwkv_mix, v_rwkv_w0, v_rwkv_w2, v_rwkv_a0, v_rwkv_a2, v_rwkv_g2, v_rwkv_k_k, v_rwkv_k_a, v_rwkv_r_k, v_rwkv_ln_w, v_rwkv_ln_b, v_w_br_attn, v_w_br_rwkv, v_w_o, v_norm_ffn_g, v_w_ffn_gate, v_w_ffn_up, v_w_ffn_down, v_norm_final_g):
    names = ("meta_tokens", "norm_mix_g", "w_in", "b_in", "attn_sinks", "rwkv_mix", "rwkv_w0", "rwkv_w2", "rwkv_a0",
             "rwkv_a2", "rwkv_g2", "rwkv_k_k", "rwkv_k_a", "rwkv_r_k", "rwkv_ln_w", "rwkv_ln_b", "w_br_attn",
             "w_br_rwkv", "w_o", "norm_ffn_g", "w_ffn_gate", "w_ffn_up", "w_ffn_down", "norm_final_g")
    w_all = dict(zip(names, (meta_tokens, norm_mix_g, w_in, b_in, attn_sinks, rwkv_mix, rwkv_w0, rwkv_w2, rwkv_a0,
                             rwkv_a2, rwkv_g2, rwkv_k_k, rwkv_k_a, rwkv_r_k, rwkv_ln_w, rwkv_ln_b, w_br_attn,
                             w_br_rwkv, w_o, norm_ffn_g, w_ffn_gate, w_ffn_up, w_ffn_down, norm_final_g)))
    m_all = dict(zip(names, (m_meta_tokens, m_norm_mix_g, m_w_in, m_b_in, m_attn_sinks, m_rwkv_mix, m_rwkv_w0,
                             m_rwkv_w2, m_rwkv_a0, m_rwkv_a2, m_rwkv_g2, m_rwkv_k_k, m_rwkv_k_a, m_rwkv_r_k,
                             m_rwkv_ln_w, m_rwkv_ln_b, m_w_br_attn, m_w_br_rwkv, m_w_o, m_norm_ffn_g, m_w_ffn_gate,
                             m_w_ffn_up, m_w_ffn_down, m_norm_final_g)))
    v_all = dict(zip(names, (v_meta_tokens, v_norm_mix_g, v_w_in, v_b_in, v_attn_sinks, v_rwkv_mix, v_rwkv_w0,
                             v_rwkv_w2, v_rwkv_a0, v_rwkv_a2, v_rwkv_g2, v_rwkv_k_k, v_rwkv_k_a, v_rwkv_r_k,
                             v_rwkv_ln_w, v_rwkv_ln_b, v_w_br_attn, v_w_br_rwkv, v_w_o, v_norm_ffn_g, v_w_ffn_gate,
                             v_w_ffn_up, v_w_ffn_down, v_norm_final_g)))
    cx, cy, _ = _position()
    chip = 2 * cx + cy

    t_of = dict(w_in_t="w_in", w_gate_t="w_ffn_gate", w_up_t="w_ffn_up", w_br_attn_t="w_br_attn",
                w_br_rwkv_t="w_br_rwkv", g2_t="rwkv_g2", w2_t="rwkv_w2", a2_t="rwkv_a2")
    plain_of = dict(w_down="w_ffn_down", w_o="w_o")
    meta_cols = meta_tokens.shape[1]

    def shard(k):
        return (w_all[t_of[k]][0].T if k in t_of else w_all[plain_of[k]][0]).astype(bf16)

    def whole(zone, own):
        return lax.dynamic_update_slice_in_dim(zone, own[None], chip, axis=0).reshape(-1, own.shape[-1])

    tiny = ("g2_t", "w2_t", "a2_t")
    late = ("w_gate_t", "w_up_t", "w_down", "w_o", "w_br_attn_t", "w_br_rwkv_t")
    w_in_own = shard("w_in_t")
    w_in_rows, w_in_cols = w_in_own.shape
    tiny_h = _exchange_start([shard(k) for k in tiny] + [meta_tokens], kind="whole", name="gather_tiny_start")
    w_in_h = _exchange_start([w_in_own + tiny_h[4][0, 0].astype(bf16)], kind="half", name="gather_w_in_start")
    behind = w_in_h[4][0, 0].astype(bf16)
    late_h = _exchange_start([shard(k) + behind for k in late], kind="whole", name="gather_late_start")
    own, zones = _exchange_wait(tiny_h, late_h[4], kind="whole", name="gather_tiny_wait")
    got = {k: whole(z, o) for k, z, o in zip(tiny, zones, own)}
    meta_full = whole(zones[-1], own[-1]).reshape(N_CHIPS, N_META, meta_cols).transpose(1, 0, 2).reshape(N_META, -1)
    p = dict(
        g2=got["g2_t"].T.astype(f32), w2=got["w2_t"].T.astype(f32), a2=got["a2_t"].T.astype(f32),
        b_in=b_in, sinks=attn_sinks, mix=rwkv_mix, w0=rwkv_w0, a0=rwkv_a0, k_k=rwkv_k_k, k_a=rwkv_k_a,
        r_k=rwkv_r_k.reshape(1, RWKV_DIM), ln_w=rwkv_ln_w, ln_b=rwkv_ln_b, norm_mix_g=norm_mix_g,
        norm_ffn_g=norm_ffn_g, norm_final_g=norm_final_g.reshape(1, D_MODEL),
    )

    def early_weights(after):
        own_h, zones_h = _exchange_wait(w_in_h, after, kind="half", name="gather_w_in_wait")
        zone = _swap_halves(zones_h[0], name="swap_w_in_halves")
        own_halves = own_h[0].reshape(w_in_rows, 2, w_in_cols // 2).transpose(1, 0, 2)[:, None]
        zone = lax.dynamic_update_slice(zone, own_halves, (0, chip, 0, 0))
        return dict(w_in_lr=zone.reshape(2, N_CHIPS * w_in_rows, w_in_cols // 2))

    def late_weights(after):
        own_l, zones_l = _exchange_wait(late_h, after, kind="whole", name="gather_late_wait")
        return {k: whole(z, o) for k, z, o in zip(late, zones_l, own_l)}

    started = {}

    def partial_sums(groups, after):
        parts = {}
        for group in groups:
            keys, handle = started[group]
            slabs, lands = _exchange_wait(handle, after, kind="slab", name="scatter_" + group + "_wait")
            parts.update({k: _sum_own_and_received(s, l, name="sum_chips_" + k) for k, s, l in zip(keys, slabs, lands)})
        return parts

    def emit(group, grads_):
        keys = list(grads_)
        slabs = []
        for k in keys:
            a = grads_[k].T if k in ("g2", "w2", "a2") else grads_[k]
            slabs.append(a.reshape(N_CHIPS, a.shape[0] // N_CHIPS, a.shape[1]))
        started[group] = (keys, _exchange_start(slabs, kind="slab", name="scatter_" + group + "_start"))
        zero = started[group][1][4]
        if group == "input":
            started["parts_a"] = partial_sums(("ffn", "branch"), zero)
            started["swap_a"] = _exchange_start(list(started["parts_a"].values()), kind="sibling",
                                                name="swap_cores_a_start")
            zero = started["swap_a"][4]
        return zero[0, 0]

    loss, dx, g = _local_step(x[0], loss_target[0], meta_full, p, early_weights, late_weights, emit)

    grads, delta, new_m, new_v = {}, {}, {}, {}
    in_grad_layout = ("w_in_t", "w_gate_t", "w_up_t")
    weight_of = {**t_of, **plain_of}

    def update(keys, mine, theirs):
        for k, part, other in zip(keys, mine, theirs):
            both = [part, other]
            k = k + "_t" if k in ("g2", "w2", "a2") else k
            n = weight_of[k]
            shape2 = w_all[n].shape[1:]
            w_, m_, v_ = (a.reshape(shape2) for a in (w_all[n], m_all[n], v_all[n]))
            if k in in_grad_layout:
                res = [t.T for t in _adamw(w_.T, both, m_.T, v_.T, name="adamw_" + n)]
            else:
                res = _adamw(w_, both, m_, v_, name="adamw_" + n, transposed=k in t_of)
            grads[n], delta[n], new_m[n], new_v[n] = (t.reshape(w_all[n].shape) for t in res)
        return delta[n]

    done = update(list(started["parts_a"]),
                  *_exchange_wait(started["swap_a"], dx, kind="sibling", name="swap_cores_a_wait"))
    small = jnp.concatenate([_pack_small(g), loss.reshape(1)])
    small_rows = -(-small.shape[0] // PACK_W)
    small = jnp.concatenate([small, jnp.zeros((small_rows * PACK_W - small.shape[0],), f32)]).reshape(small_rows, PACK_W)
    small_rows8 = -(-(small_rows + N_META) // 8) * 8
    reduced = _all_reduce_small(_pad_rows(jnp.concatenate([g["meta"], small], axis=0), small_rows8), done,
                                name="reduce_small")
    parts_b = partial_sums(("input",), reduced)
    update(list(parts_b), list(parts_b.values()), _swap_cores(list(parts_b.values()), name="swap_cores_b"))
    g_meta = lax.dynamic_slice_in_dim(reduced[:N_META], chip * meta_cols, meta_cols, axis=1)
    flat = reduced[N_META:N_META + small_rows].reshape(-1)
    g_small = _unpack_small(flat)
    loss_total = flat[_SMALL_TOTAL]

    small_of = dict(norm_mix_g="norm_mix_g", b_in="b_in", attn_sinks="sinks", rwkv_mix="mix", rwkv_w0="w0",
                    rwkv_a0="a0", rwkv_k_k="k_k", rwkv_k_a="k_a", rwkv_r_k="r_k", rwkv_ln_w="ln_w",
                    rwkv_ln_b="ln_b", norm_ffn_g="norm_ffn_g", norm_final_g="norm_final_g")
    grads["meta_tokens"] = g_meta
    for n, k in small_of.items():
        grads[n] = g_small[k].reshape(w_all[n].shape)

    rest = [n for n in names if n not in delta]

    def pack_rest(src):
        flat_ = jnp.concatenate([src[n].reshape(-1) for n in rest])
        rows_ = -(-flat_.shape[0] // (8 * PACK_W)) * 8
        return jnp.concatenate([flat_, jnp.ones((rows_ * PACK_W - flat_.shape[0],), f32)]).reshape(rows_, PACK_W)

    _, d_, m_, v_ = _adamw(pack_rest(w_all), [pack_rest(grads)], pack_rest(m_all), pack_rest(v_all),
                           name="adamw_small")
    off = 0
    for n in rest:
        size = w_all[n].size
        for dst, src in ((delta, d_), (new_m, m_), (new_v, v_)):
            dst[n] = src.reshape(-1)[off:off + size].reshape(w_all[n].shape)
        off += size

    return (loss_total, dx.reshape(x.shape), *[grads[n] for n in names], *[delta[n] for n in names],
            *[new_m[n] for n in names], *[new_v[n] for n in names])
```

```python
import math

import jax
import jax.numpy as jnp
from jax import lax
from jax.experimental import pallas as pl
from jax.experimental.pallas import tpu as pltpu

f32 = jnp.float32
bf16 = jnp.bfloat16

D_MODEL = 1024
N_META = 16
HEAD_DIM = 64
Q_HEADS = 8
KV_HEADS = 2
GROUP = Q_HEADS // KV_HEADS
WINDOW = 128
BLOCK = 128
ROPE_THETA = 500000.0
ROPE_DIM = HEAD_DIM // 4
RWKV_HEADS = 8
RWKV_HEAD = 64
RWKV_DIM = RWKV_HEADS * RWKV_HEAD
DECAY_LORA = 64
AAA_LORA = 64
GATE_LORA = 160
LORA_W = DECAY_LORA + AAA_LORA + GATE_LORA
RWKV_LN_EPS = 64e-5
D_FF = 2816
Q_W = Q_HEADS * HEAD_DIM
KV_W = KV_HEADS * HEAD_DIM
ATTN_PROJ = Q_W + 2 * KV_W
RKV_W = 3 * RWKV_DIM
RWKV_PROJ = RKV_W + LORA_W
D_IN = ATTN_PROJ + RWKV_PROJ + 2 * D_MODEL
RMS_EPS = 1e-6
NEG_INF = -1e30
PAD = BLOCK - N_META
FRONT = PAD + N_META

ADAM_LR = 0.001
ADAM_B1 = 0.9
ADAM_B2 = 0.999
ADAM_EPS = 1e-08
ADAM_WD = 0.01
ADAM_STEP = 10

N_CHIPS = 4
N_DEV = 8
CHUNK = 128
VMEM_LIMIT = 56 * 1024 * 1024
MM_ROWS = 704
PACK_W = 1024
MESH = pl.DeviceIdType.MESH


def _tile(m, pref=384):
    for step in (16, 8):
        for t in range(min(m, pref) // step * step, 0, -step):
            if m % t == 0:
                return t
    return m


def _params(sem=None):
    return pltpu.CompilerParams(dimension_semantics=sem, vmem_limit_bytes=VMEM_LIMIT)


def _full(shape):
    nd = len(shape)
    return pl.BlockSpec(shape, lambda *_: (0,) * nd)


def _dot(a, b, dims="nn"):
    dn = {"nn": (((1,), (0,)), ((), ())), "nt": (((1,), (1,)), ((), ())), "tn": (((0,), (0,)), ((), ()))}[dims]
    return lax.dot_general(a.astype(bf16), b.astype(bf16), dn, preferred_element_type=f32)


def _two_pass(x, m, dims="nn"):
    x_hi = x.astype(bf16)
    x_lo = (x - x_hi.astype(f32)).astype(bf16)
    return _dot(x_hi, m, dims) + _dot(x_lo, m, dims)


@jax.custom_vjp
def _dot_const(x, m):
    return _two_pass(x, m)


def _dot_const_fwd(x, m):
    return _two_pass(x, m), m


def _dot_const_bwd(m, ct):
    return _two_pass(ct, m, "nt"), jnp.zeros_like(m)


_dot_const.defvjp(_dot_const_fwd, _dot_const_bwd)


def _two_pass_left(m, x, dims):
    x_hi = x.astype(bf16)
    x_lo = (x - x_hi.astype(f32)).astype(bf16)
    return _dot(m, x_hi, dims) + _dot(m, x_lo, dims)


@jax.custom_vjp
def _const_dot(m, x):
    return _two_pass_left(m, x, "nn")


def _const_dot_fwd(m, x):
    return _two_pass_left(m, x, "nn"), m


def _const_dot_bwd(m, ct):
    return jnp.zeros_like(m), _two_pass_left(m, ct, "tn")


_const_dot.defvjp(_const_dot_fwd, _const_dot_bwd)


def _mm(a, b, mode, *, name, out_dtype=f32, bias=None, add=None, zero_rows_below=0):
    m, _ = a.shape
    n = b.shape[1] if mode == "nn" else b.shape[0]
    tm = _tile(m, MM_ROWS)
    has_bias, has_add = bias is not None, add is not None

    def body(*refs):
        a_ref, b_ref = refs[0], refs[1]
        o_ref = refs[-1]
        acc = _dot(a_ref[...], b_ref[...], mode)
        k = 2
        if has_bias:
            acc = acc + refs[k][...]
            k += 1
        if zero_rows_below:
            rows = pl.program_id(0) * tm + lax.broadcasted_iota(jnp.int32, acc.shape, 0)
            acc = jnp.where(rows >= zero_rows_below, acc, 0.0)
        if has_add:
            acc = acc + refs[k][...].astype(f32)
        o_ref[...] = acc.astype(out_dtype)

    ins = [a, b]
    in_specs = [pl.BlockSpec((tm, a.shape[1]), lambda i: (i, 0)), _full(b.shape)]
    if has_bias:
        ins.append(bias)
        in_specs.append(_full(bias.shape))
    if has_add:
        ins.append(add)
        in_specs.append(pl.BlockSpec((tm, n), lambda i: (i, 0)))
    return pl.pallas_call(
        body, name=name, grid=(m // tm,), in_specs=in_specs,
        out_specs=pl.BlockSpec((tm, n), lambda i: (i, 0)),
        out_shape=jax.ShapeDtypeStruct((m, n), out_dtype),
        compiler_params=_params(("parallel",)),
    )(*ins)


def _pieces(widths):
    out, off = [], 0
    for w in widths:
        out.append((off, w))
        off += w
    return out


def _proj_in(a, w_lr, bias, widths, *, name, zero_rows_below=0):
    m, kdim = a.shape
    half = kdim // 2
    tm = _tile(m, MM_ROWS)

    def body(a_ref, w_ref, b_ref, *outs):
        a_l, a_r = a_ref[:, :half], a_ref[:, half:]
        for (off, width), o_ref in zip(_pieces(widths), outs):
            acc = _dot(a_l, w_ref[0, off:off + width, :], "nt") + _dot(a_r, w_ref[1, off:off + width, :], "nt")
            acc = acc + b_ref[:, off:off + width]
            if zero_rows_below:
                rows = pl.program_id(0) * tm + lax.broadcasted_iota(jnp.int32, acc.shape, 0)
                acc = jnp.where(rows >= zero_rows_below, acc, 0.0)
            o_ref[...] = acc.astype(o_ref.dtype)

    return pl.pallas_call(
        body, name=name, grid=(m // tm,),
        in_specs=[pl.BlockSpec((tm, kdim), lambda i: (i, 0)), _full(w_lr.shape), _full(bias.shape)],
        out_specs=[pl.BlockSpec((tm, w), lambda i: (i, 0)) for w in widths],
        out_shape=[jax.ShapeDtypeStruct((m, w), bf16) for w in widths],
        compiler_params=_params(("parallel",)),
    )(a, w_lr, bias)


def _proj_in_bwd(d_list, w_lr, *, name):
    m = d_list[0].shape[0]
    half = w_lr.shape[2]
    widths = [d.shape[1] for d in d_list]
    tm = _tile(m, MM_ROWS)

    def body(*refs):
        w_ref, o_ref = refs[-2], refs[-1]
        for side in range(2):
            acc = None
            for (off, width), d_ref in zip(_pieces(widths), refs):
                term = _dot(d_ref[...], w_ref[side, off:off + width, :])
                acc = term if acc is None else acc + term
            o_ref[:, side * half:(side + 1) * half] = acc.astype(o_ref.dtype)

    return pl.pallas_call(
        body, name=name, grid=(m // tm,),
        in_specs=[pl.BlockSpec((tm, w), lambda i: (i, 0)) for w in widths] + [_full(w_lr.shape)],
        out_specs=pl.BlockSpec((tm, 2 * half), lambda i: (i, 0)),
        out_shape=jax.ShapeDtypeStruct((m, 2 * half), bf16),
        compiler_params=_params(("parallel",)),
    )(*d_list, w_lr)


def _residual_norm(a, w, res, g, *, name):
    m, d = res.shape
    tm = _tile(m, MM_ROWS)

    def body(a_ref, w_ref, r_ref, g_ref, h_ref, n_ref):
        h = _dot(a_ref[...], w_ref[...]) + r_ref[...]
        h_ref[...] = h
        n_ref[...] = _rms(h, g_ref[...]).astype(n_ref.dtype)

    tile = pl.BlockSpec((tm, d), lambda i: (i, 0))
    return pl.pallas_call(
        body, name=name, grid=(m // tm,),
        in_specs=[pl.BlockSpec((tm, a.shape[1]), lambda i: (i, 0)), _full(w.shape), tile, _full(g.shape)],
        out_specs=[tile, tile],
        out_shape=[jax.ShapeDtypeStruct((m, d), f32), jax.ShapeDtypeStruct((m, d), bf16)],
        compiler_params=_params(("parallel",)),
    )(a, w, res, g)


def _residual_norm_bwd(d_list, w_list, h, g, dh_out, *, name):
    m, d = h.shape
    k = len(d_list)
    tm = _tile(m)

    def body(*refs):
        h_ref, g_ref, dho_ref, dh_ref, dg_ref = refs[2 * k:]
        dn = _dot(refs[0][...], refs[k][...])
        for i in range(1, k):
            dn = dn + _dot(refs[i][...], refs[k + i][...])
        _, vjp = jax.vjp(lambda hv, gv: (_rms(hv, gv), hv), h_ref[...], g_ref[...])
        dh, dg = vjp((dn, dho_ref[...]))
        dh_ref[...] = dh

        @pl.when(pl.program_id(0) == 0)
        def _():
            dg_ref[...] = jnp.zeros_like(dg_ref)

        dg_ref[...] += dg

    tile = pl.BlockSpec((tm, d), lambda i: (i, 0))
    return pl.pallas_call(
        body, name=name, grid=(m // tm,),
        in_specs=[pl.BlockSpec((tm, a.shape[1]), lambda i: (i, 0)) for a in d_list] + [_full(w.shape) for w in w_list]
        + [tile, _full(g.shape), tile],
        out_specs=[tile, _full(g.shape)],
        out_shape=[jax.ShapeDtypeStruct((m, d), f32), jax.ShapeDtypeStruct(g.shape, f32)],
        compiler_params=_params(("arbitrary",)),
    )(*d_list, *w_list, h, g, dh_out)


def _mm_tn(a, b, *, name, colsum=False, out_dtype=bf16):
    r, m = a.shape
    n = b.shape[1]
    tr = _tile(r, 1408)
    tmo = m
    for cand in (1408, 1024, 768, 512):
        if m > 1024 and m % cand == 0:
            tmo = cand
            break
    steps = r // tr

    def body(a_ref, b_ref, o_ref, *rest):
        acc = rest[-1]
        i = pl.program_id(1)

        @pl.when(i == 0)
        def _():
            acc[...] = jnp.zeros_like(acc)
            if colsum:
                rest[0][...] = jnp.zeros_like(rest[0])

        acc[...] += _dot(a_ref[...], b_ref[...], "tn")
        if colsum:
            rest[0][...] += jnp.sum(a_ref[...].astype(f32), axis=0, keepdims=True)

        @pl.when(i == steps - 1)
        def _():
            o_ref[...] = acc[...].astype(out_dtype)

    out_shape = [jax.ShapeDtypeStruct((m, n), out_dtype)]
    out_specs = [pl.BlockSpec((tmo, n), lambda j, i: (j, 0))]
    if colsum:
        out_shape.append(jax.ShapeDtypeStruct((1, m), f32))
        out_specs.append(pl.BlockSpec((1, tmo), lambda j, i: (0, j)))
    res = pl.pallas_call(
        body, name=name, grid=(m // tmo, steps),
        in_specs=[pl.BlockSpec((tr, tmo), lambda j, i: (i, j)), pl.BlockSpec((tr, n), lambda j, i: (i, 0))],
        out_specs=out_specs, out_shape=out_shape,
        scratch_shapes=[pltpu.VMEM((tmo, n), f32)],
        compiler_params=_params(("parallel", "arbitrary")),
    )(a, b)
    return res if colsum else res[0]


def _rowwise(fn, rows, params, outs, *, name, tm=None):
    m = rows[0].shape[0]
    tm = tm or _tile(m, MM_ROWS)
    nr, npar = len(rows), len(params)

    def body(*refs):
        vals = [r[...] for r in refs[:nr + npar]]
        res = fn(*vals)
        for o_ref, v in zip(refs[nr + npar:], res):
            o_ref[...] = v.astype(o_ref.dtype)

    return pl.pallas_call(
        body, name=name, grid=(m // tm,),
        in_specs=[pl.BlockSpec((tm, r.shape[1]), lambda i: (i, 0)) for r in rows] + [_full(p.shape) for p in params],
        out_specs=[pl.BlockSpec((tm, w), lambda i: (i, 0)) for w, _ in outs],
        out_shape=[jax.ShapeDtypeStruct((m, w), dt) for w, dt in outs],
        compiler_params=_params(("parallel",)),
    )(*rows, *params)


def _rowwise_bwd(fn, rows, params, cts, *, name, diff_rows, diff_params, tm=None, zero_rows_below=0, out_dtypes=None):
    m = rows[0].shape[0]
    tm = tm or _tile(m)
    nr, npar = len(rows), len(params)
    d_idx = [i for i in range(nr) if diff_rows[i]]
    p_idx = [i for i in range(npar) if diff_params[i]]
    out_dtypes = out_dtypes or [f32] * len(d_idx)
    flat_cts = [c for group in cts for c in group]
    n_ct = len(flat_cts)

    def body(*refs):
        vals = [r[...] for r in refs[:nr + npar]]
        ct_refs = refs[nr + npar:nr + npar + n_ct]
        out_refs = refs[nr + npar + n_ct:]
        ct_vals, k = [], 0
        for group in cts:
            acc = ct_refs[k][...].astype(f32)
            for extra in range(1, len(group)):
                acc = acc + ct_refs[k + extra][...].astype(f32)
            k += len(group)
            if zero_rows_below:
                rr = pl.program_id(0) * tm + lax.broadcasted_iota(jnp.int32, acc.shape, 0)
                acc = jnp.where(rr >= zero_rows_below, acc, 0.0)
            ct_vals.append(acc)

        def g(*dargs):
            full = list(vals)
            for pos, i in enumerate(d_idx):
                full[i] = dargs[pos]
            for pos, i in enumerate(p_idx):
                full[nr + i] = dargs[len(d_idx) + pos]
            return tuple(fn(*full))

        _, vjp = jax.vjp(g, *[vals[i].astype(f32) for i in d_idx], *[vals[nr + i] for i in p_idx])
        grads = vjp(tuple(ct_vals))
        for pos in range(len(d_idx)):
            out_refs[pos][...] = grads[pos].astype(out_refs[pos].dtype)
        first = pl.program_id(0) == 0
        for pos in range(len(p_idx)):
            o_ref = out_refs[len(d_idx) + pos]

            @pl.when(first)
            def _(o_ref=o_ref):
                o_ref[...] = jnp.zeros_like(o_ref)

            o_ref[...] += grads[len(d_idx) + pos]

    return pl.pallas_call(
        body, name=name, grid=(m // tm,),
        in_specs=[pl.BlockSpec((tm, r.shape[1]), lambda i: (i, 0)) for r in rows] + [_full(p.shape) for p in params]
        + [pl.BlockSpec((tm, c.shape[1]), lambda i: (i, 0)) for c in flat_cts],
        out_specs=[pl.BlockSpec((tm, rows[i].shape[1]), lambda i_: (i_, 0)) for i in d_idx]
        + [_full(params[i].shape) for i in p_idx],
        out_shape=[jax.ShapeDtypeStruct(rows[i].shape, dt) for i, dt in zip(d_idx, out_dtypes)]
        + [jax.ShapeDtypeStruct(params[i].shape, f32) for i in p_idx],
        compiler_params=_params(("arbitrary",)),
    )(*rows, *params, *flat_cts)


def _rms(x, g):
    return x * lax.rsqrt(jnp.mean(x * x, axis=-1, keepdims=True) + RMS_EPS) * g


def _head_sum_matrix(width, head):
    idx = jnp.arange(width) // head
    return (idx[:, None] == idx[None, :]).astype(f32)


def _rope_tables(lp):
    half = ROPE_DIM // 2
    pos = (jnp.arange(lp) - PAD).astype(f32)
    inv_freq = jnp.power(jnp.float32(ROPE_THETA), -jnp.arange(half, dtype=f32) * (2.0 / ROPE_DIM))
    ang = pos[:, None] * inv_freq[None, :]
    cos, sin = jnp.cos(ang), jnp.sin(ang)
    ones = jnp.ones((lp, HEAD_DIM - ROPE_DIM), f32)
    zeros = jnp.zeros((lp, HEAD_DIM - ROPE_DIM), f32)
    cos_t = jnp.concatenate([cos, cos, ones], axis=1)
    sin_t = jnp.concatenate([-sin, sin, zeros], axis=1)
    i = jnp.arange(HEAD_DIM)
    src = jnp.where(i < half, i + half, jnp.where(i < ROPE_DIM, i - half, i))
    swap = ((i[:, None] == src[None, :]) & (i[None, :] < ROPE_DIM)).astype(f32)
    return cos_t, sin_t, swap


def _attn_prep(qkv, cos_t, sin_t, swap):
    outs = []
    for h in range(Q_HEADS + KV_HEADS):
        t = qkv[:, h * HEAD_DIM:(h + 1) * HEAD_DIM]
        outs.append(t * cos_t + _dot_const(t, swap) * sin_t)
    q = jnp.concatenate(outs[:Q_HEADS], axis=1)
    k = jnp.concatenate(outs[Q_HEADS:], axis=1)
    return q, k, qkv[:, Q_W + KV_W:]


def _softplus(z):
    return jnp.maximum(z, 0.0) + jnp.log1p(jnp.exp(-jnp.abs(z)))


def _rwkv_prep(rkv, lora, w0, w2, a0, a2, g2, k_k, k_a, hsum):
    r = rkv[:, :RWKV_DIM]
    k = rkv[:, RWKV_DIM:2 * RWKV_DIM]
    v = rkv[:, 2 * RWKV_DIM:]
    dw = lora[:, :DECAY_LORA]
    da = lora[:, DECAY_LORA:DECAY_LORA + AAA_LORA]
    dg = lora[:, DECAY_LORA + AAA_LORA:]
    w = -_softplus(-(w0 + _dot(jnp.tanh(dw), w2))) - 0.5
    a = jax.nn.sigmoid(a0 + _dot(da, a2))
    g = _dot(jax.nn.sigmoid(dg), g2)
    kk = k * k_k
    kk = kk * lax.rsqrt(jnp.maximum(_dot_const(kk * kk, hsum), 1e-24))
    k = k * (1.0 + (a - 1.0) * k_a)
    log_decay = -jnp.exp(w)
    return r, log_decay, k, v, -kk, kk * a, g


def _rwkv_post(y, r, k, v, g, ln_w, ln_b, r_k, hmean):
    hsum = hmean * RWKV_HEAD
    mean = _dot_const(y, hmean)
    yc = y - mean
    var = _dot_const(yc * yc, hmean)
    yn = yc * lax.rsqrt(var + RWKV_LN_EPS) * ln_w + ln_b
    bonus = _dot_const(r * k * r_k, hsum) * v
    return ((yn + bonus) * g,)


def _merge(gates, br_a, br_r):
    sg = jax.nn.sigmoid(gates)
    return (sg[:, :D_MODEL] * br_a + sg[:, D_MODEL:] * br_r,)


def _swiglu(gate, up):
    return (jax.nn.silu(gate) * up,)


def _ffn_in(f, w_gate_t, w_up_t, *, name):
    m, d = f.shape
    n = w_gate_t.shape[0]
    tm = _tile(m)

    def body(f_ref, wg_ref, wu_ref, g_ref, u_ref, a_ref):
        g = _dot(f_ref[...], wg_ref[...], "nt")
        u = _dot(f_ref[...], wu_ref[...], "nt")
        g_ref[...] = g.astype(g_ref.dtype)
        u_ref[...] = u.astype(u_ref.dtype)
        a_ref[...] = _swiglu(g, u)[0].astype(a_ref.dtype)

    spec = pl.BlockSpec((tm, n), lambda i: (i, 0))
    return pl.pallas_call(
        body, name=name, grid=(m // tm,),
        in_specs=[pl.BlockSpec((tm, d), lambda i: (i, 0)), _full(w_gate_t.shape), _full(w_up_t.shape)],
        out_specs=[spec] * 3, out_shape=[jax.ShapeDtypeStruct((m, n), bf16)] * 3,
        compiler_params=_params(("parallel",)),
    )(f, w_gate_t, w_up_t)


def _branch_merge(y_attn, y_rwkv, w_attn_t, w_rwkv_t, gates, *, name):
    m = y_attn.shape[0]
    tm = _tile(m, MM_ROWS)

    def body(ya_ref, yr_ref, wa_ref, wr_ref, g_ref, a_ref, r_ref, o_ref):
        br_a = _dot(ya_ref[...], wa_ref[...], "nt")
        br_r = _dot(yr_ref[...], wr_ref[...], "nt")
        a_ref[...] = br_a.astype(a_ref.dtype)
        r_ref[...] = br_r.astype(r_ref.dtype)
        o_ref[...] = _merge(g_ref[...].astype(f32), br_a, br_r)[0].astype(o_ref.dtype)

    rows = lambda a: pl.BlockSpec((tm, a.shape[1]), lambda i: (i, 0))
    spec = pl.BlockSpec((tm, D_MODEL), lambda i: (i, 0))
    return pl.pallas_call(
        body, name=name, grid=(m // tm,),
        in_specs=[rows(y_attn), rows(y_rwkv), _full(w_attn_t.shape), _full(w_rwkv_t.shape), rows(gates)],
        out_specs=[spec] * 3, out_shape=[jax.ShapeDtypeStruct((m, D_MODEL), bf16)] * 3,
        compiler_params=_params(("parallel",)),
    )(y_attn, y_rwkv, w_attn_t, w_rwkv_t, gates)


def _branch_merge_bwd(dh, w_o, gates, br_a, br_r, w_attn_t, w_rwkv_t, *, name):
    m = dh.shape[0]
    tm = _tile(m, MM_ROWS)

    def body(dh_ref, w_ref, g_ref, a_ref, r_ref, wa_ref, wr_ref, dg_ref, da_ref, dr_ref, dya_ref, dyr_ref):
        dmerged = _dot(dh_ref[...], w_ref[...], "nt")
        _, vjp = jax.vjp(lambda g, a, r: _merge(g, a, r)[0], g_ref[...].astype(f32), a_ref[...].astype(f32),
                         r_ref[...].astype(f32))
        dg, da, dr = vjp(dmerged)
        dg_ref[...] = dg.astype(dg_ref.dtype)
        da_ref[...] = da.astype(da_ref.dtype)
        dr_ref[...] = dr.astype(dr_ref.dtype)
        dya_ref[...] = _dot(da, wa_ref[...])
        dyr_ref[...] = _dot(dr, wr_ref[...])

    rows = lambda a: pl.BlockSpec((tm, a.shape[1]), lambda i: (i, 0))
    mixer = pl.BlockSpec((tm, w_attn_t.shape[1]), lambda i: (i, 0))
    return pl.pallas_call(
        body, name=name, grid=(m // tm,),
        in_specs=[rows(dh), _full(w_o.shape), rows(gates), rows(br_a), rows(br_r), _full(w_attn_t.shape),
                  _full(w_rwkv_t.shape)],
        out_specs=[rows(gates), rows(br_a), rows(br_r), mixer, mixer],
        out_shape=[jax.ShapeDtypeStruct(gates.shape, bf16), jax.ShapeDtypeStruct(br_a.shape, bf16),
                   jax.ShapeDtypeStruct(br_r.shape, bf16), jax.ShapeDtypeStruct((m, w_attn_t.shape[1]), f32),
                   jax.ShapeDtypeStruct((m, w_rwkv_t.shape[1]), f32)],
        compiler_params=_params(("parallel",)),
    )(dh, w_o, gates, br_a, br_r, w_attn_t, w_rwkv_t)


def _ffn_in_bwd(dh, w_down, gate, up, *, name):
    m, d = dh.shape
    n = w_down.shape[0]
    tm = _tile(m)

    def body(dh_ref, w_ref, g_ref, u_ref, dg_ref, du_ref):
        dact = _dot(dh_ref[...], w_ref[...], "nt")
        _, vjp = jax.vjp(lambda a, b: _swiglu(a, b)[0], g_ref[...].astype(f32), u_ref[...].astype(f32))
        dg, du = vjp(dact)
        dg_ref[...] = dg.astype(dg_ref.dtype)
        du_ref[...] = du.astype(du_ref.dtype)

    spec = pl.BlockSpec((tm, n), lambda i: (i, 0))
    return pl.pallas_call(
        body, name=name, grid=(m // tm,),
        in_specs=[pl.BlockSpec((tm, d), lambda i: (i, 0)), _full(w_down.shape), spec, spec],
        out_specs=[spec] * 2, out_shape=[jax.ShapeDtypeStruct((m, n), bf16)] * 2,
        compiler_params=_params(("parallel",)),
    )(dh, w_down, gate, up)


HALO = 16


def _previous_rows(x, before_ref, first_tile):
    rows = lax.broadcasted_iota(jnp.int32, x.shape, 0)
    last = jnp.where(first_tile, 0.0, before_ref[HALO - 1:HALO, :].astype(f32))
    return jnp.where(rows == 0, last, pltpu.roll(x, 1, axis=0))


def _mixer_inputs(ps, mixes, params, *, name):
    m = ps[0].shape[0]
    tm = _tile(m)
    sub = tm // HALO
    n_par = len(params)

    def body(*refs):
        first = pl.program_id(0) == 0
        pf = []
        for k in range(2):
            x = refs[k][...].astype(f32)
            pf.append(x + (_previous_rows(x, refs[2 + k], first) - x) * refs[4 + k][...])
        res = _rwkv_prep(*pf, *[ref[...] for ref in refs[6:6 + n_par]])
        for o_ref, val in zip(refs[6 + n_par:], res):
            o_ref[...] = val

    tile = lambda a: pl.BlockSpec((tm, a.shape[1]), lambda i: (i, 0))
    before = lambda a: pl.BlockSpec((HALO, a.shape[1]), lambda i: (jnp.maximum(i * sub - 1, 0), 0))
    out = pl.BlockSpec((tm, RWKV_DIM), lambda i: (i, 0))
    return pl.pallas_call(
        body, name=name, grid=(m // tm,),
        in_specs=[tile(a) for a in ps] + [before(a) for a in ps] + [_full(a.shape) for a in mixes + params],
        out_specs=[out] * 7, out_shape=[jax.ShapeDtypeStruct((m, RWKV_DIM), f32)] * 7,
        compiler_params=_params(("parallel",)),
    )(*ps, *ps, *mixes, *params)


def _mixer_inputs_bwd(ps, mixes, params, cts, *, name):
    m = ps[0].shape[0]
    tm = _tile(m)
    sub = tm // HALO
    nt = m // tm
    n_par = len(params)
    flat_cts = [c for group in cts for c in group]
    n_ct = len(flat_cts)

    def body(*refs):
        i = pl.program_id(0)
        tile_index = nt - 1 - i
        ct_refs = refs[6 + n_par:6 + n_par + n_ct]
        dp_refs = refs[6 + n_par + n_ct:8 + n_par + n_ct]
        dmix_refs = refs[8 + n_par + n_ct:10 + n_par + n_ct]
        dpar_refs = refs[10 + n_par + n_ct:9 + 2 * n_par + n_ct]
        carries = refs[9 + 2 * n_par + n_ct:]
        rows1 = tile_index * tm + lax.broadcasted_iota(jnp.int32, (tm, 1), 0)
        live = rows1 >= PAD

        @pl.when(i == 0)
        def _():
            for ref in (*dmix_refs, *dpar_refs, *carries):
                ref[...] = jnp.zeros_like(ref)

        xs, prevs, pf = [], [], []
        for k in range(2):
            x = refs[k][...].astype(f32)
            xp = _previous_rows(x, refs[2 + k], tile_index == 0)
            xs.append(x)
            prevs.append(xp)
            pf.append(x + (xp - x) * refs[4 + k][...])
        ct_vals, pos = [], 0
        for group in cts:
            acc = ct_refs[pos][...].astype(f32)
            for extra in range(1, len(group)):
                acc = acc + ct_refs[pos + extra][...].astype(f32)
            pos += len(group)
            ct_vals.append(jnp.where(live, acc, 0.0))
        par_vals = [ref[...] for ref in refs[6:6 + n_par]]
        _, vjp = jax.vjp(lambda *args: _rwkv_prep(*args, par_vals[-1]), *pf, *par_vals[:-1])
        g = vjp(tuple(ct_vals))
        for k in range(2):
            dpf = g[k]
            mixv = refs[4 + k][...]
            dm = dpf * mixv
            rows = lax.broadcasted_iota(jnp.int32, dm.shape, 0)
            dm_next = jnp.where(rows == tm - 1, carries[k][...], pltpu.roll(dm, tm - 1, axis=0))
            dp_refs[k][...] = jnp.where(live, dpf - dm + dm_next, 0.0).astype(dp_refs[k].dtype)
            carries[k][...] = dm[0:1, :]
            dmix_refs[k][...] += jnp.sum(dpf * (prevs[k] - xs[k]), axis=0, keepdims=True)
        for ref, val in zip(dpar_refs, g[2:]):
            ref[...] += val

    tile = lambda a: pl.BlockSpec((tm, a.shape[1]), lambda i: (nt - 1 - i, 0))
    before = lambda a: pl.BlockSpec((HALO, a.shape[1]), lambda i: (jnp.maximum((nt - 1 - i) * sub - 1, 0), 0))
    return pl.pallas_call(
        body, name=name, grid=(nt,),
        in_specs=[tile(a) for a in ps] + [before(a) for a in ps] + [_full(a.shape) for a in mixes + params]
        + [tile(c) for c in flat_cts],
        out_specs=[tile(a) for a in ps] + [_full(a.shape) for a in mixes + params[:-1]],
        out_shape=[jax.ShapeDtypeStruct(a.shape, bf16) for a in ps]
        + [jax.ShapeDtypeStruct(a.shape, f32) for a in mixes + params[:-1]],
        scratch_shapes=[pltpu.VMEM((1, a.shape[1]), f32) for a in ps],
        compiler_params=_params(("arbitrary",)),
    )(*ps, *ps, *mixes, *params, *flat_cts)


def _attn_masks(blk):
    qi = lax.broadcasted_iota(jnp.int32, (BLOCK, BLOCK), 0)
    ki = lax.broadcasted_iota(jnp.int32, (BLOCK, BLOCK), 1)
    qpos = blk * BLOCK + qi - PAD
    kpos_c = blk * BLOCK + ki - PAD
    kpos_p = kpos_c - BLOCK
    kpos_m = ki - PAD

    def band(kpos):
        return (kpos >= N_META) & (kpos <= qpos) & (qpos - kpos < WINDOW)

    return band(kpos_p), band(kpos_c), (kpos_m >= 0) & (kpos_m <= qpos)


def _attn_probs(qs, k3s, sink, oks):
    s = [[jnp.where(ok, _dot(qh, kx, "nt"), NEG_INF) for kx, ok in zip(k3, oks)] for qh, k3 in zip(qs, k3s)]
    mx = [jnp.maximum(jnp.maximum(jnp.max(t[0], -1, keepdims=True), jnp.max(t[1], -1, keepdims=True)),
                      jnp.maximum(jnp.max(t[2], -1, keepdims=True), sk)) for t, sk in zip(s, sink)]
    e = [[jnp.exp(tx - m) for tx in t] for t, m in zip(s, mx)]
    e_sink = [jnp.exp(sk - m) for sk, m in zip(sink, mx)]
    inv = [1.0 / (jnp.sum(t[0], -1, keepdims=True) + jnp.sum(t[1], -1, keepdims=True)
                  + jnp.sum(t[2], -1, keepdims=True) + es) for t, es in zip(e, e_sink)]
    return [[tx * i for tx in t] for t, i in zip(e, inv)], [es * i for es, i in zip(e_sink, inv)]


def _head_cols(i):
    return slice(i * HEAD_DIM, (i + 1) * HEAD_DIM)


def _attn_operands(refs):
    q_ref, kp_ref, kc_ref, km_ref, vp_ref, vc_ref, vm_ref, s_ref = refs
    qs = [q_ref[:, _head_cols(i)] * (HEAD_DIM ** -0.5) for i in range(Q_HEADS)]
    k3 = [[ref[:, _head_cols(h)] for ref in (kp_ref, kc_ref, km_ref)] for h in range(KV_HEADS)]
    v3 = [[ref[:, _head_cols(h)] for ref in (vp_ref, vc_ref, vm_ref)] for h in range(KV_HEADS)]
    return (qs, [k3[i // GROUP] for i in range(Q_HEADS)], [v3[i // GROUP] for i in range(Q_HEADS)],
            [s_ref[:, i:i + 1] for i in range(Q_HEADS)])


def _attention(q, k, v, sinks, *, name):
    lp = q.shape[0]
    nb = lp // BLOCK
    prev = lambda i: (jnp.maximum(i - 1, 0), 0)
    cur = lambda i: (i, 0)
    meta = lambda i: (0, 0)
    kv = lambda index: pl.BlockSpec((BLOCK, KV_W), index)

    def body(*refs):
        o_ref = refs[-1]
        qs, k3s, v3s, sink = _attn_operands(refs[:-1])
        p, _ = _attn_probs(qs, k3s, sink, _attn_masks(pl.program_id(0)))
        out = [_dot(ph[0], v3[0]) + _dot(ph[1], v3[1]) + _dot(ph[2], v3[2]) for ph, v3 in zip(p, v3s)]
        for i in range(Q_HEADS):
            o_ref[:, _head_cols(i)] = out[i].astype(o_ref.dtype)

    return pl.pallas_call(
        body, name=name, grid=(nb,),
        in_specs=[pl.BlockSpec((BLOCK, Q_W), cur), kv(prev), kv(cur), kv(meta), kv(prev), kv(cur), kv(meta),
                  _full((1, Q_HEADS))],
        out_specs=pl.BlockSpec((BLOCK, Q_W), cur),
        out_shape=jax.ShapeDtypeStruct((lp, Q_W), bf16),
        compiler_params=_params(("parallel",)),
    )(q, k, k, k, v, v, v, sinks)


def _attention_bwd(q, k, v, sinks, out, do, *, name):
    lp = q.shape[0]
    nb = lp // BLOCK
    cur = lambda n: (jnp.minimum(n, nb - 1), 0)
    prev = lambda n: (jnp.maximum(jnp.minimum(n, nb - 1) - 1, 0), 0)
    behind = lambda n: (jnp.maximum(n - 1, 0), 0)
    meta = lambda n: (0, 0)
    kv = lambda index: pl.BlockSpec((BLOCK, KV_W), index)
    scale = HEAD_DIM ** -0.5

    def body(*refs):
        ins, fwd_ref, do_ref = refs[:8], refs[8], refs[9]
        dq_ref, dk_ref, dv_ref, dkm_ref, dvm_ref, ds_ref, carry_k, carry_v = refs[10:]
        n = pl.program_id(0)

        @pl.when(n == 0)
        def _():
            for ref in (dkm_ref, dvm_ref, ds_ref, carry_k, carry_v):
                ref[...] = jnp.zeros_like(ref)

        @pl.when(n < nb)
        def _():
            qs, k3s, v3s, sink = _attn_operands(ins)
            do = [do_ref[:, _head_cols(i)] for i in range(Q_HEADS)]
            p, p_sink = _attn_probs(qs, k3s, sink, _attn_masks(n))
            delta = [jnp.sum(d * fwd_ref[:, _head_cols(i)].astype(f32), -1, keepdims=True) for i, d in enumerate(do)]
            dp = [[_dot(d, vx, "nt") for vx in v3] for d, v3 in zip(do, v3s)]
            ds = [[px * (dx - dl) for px, dx in zip(ph, dh)] for ph, dh, dl in zip(p, dp, delta)]
            dq = [_dot(dsh[0], k3[0]) + _dot(dsh[1], k3[1]) + _dot(dsh[2], k3[2]) for dsh, k3 in zip(ds, k3s)]
            for i in range(Q_HEADS):
                dq_ref[:, _head_cols(i)] = dq[i] * scale
                ds_ref[:, i:i + 1] -= jnp.sum(p_sink[i] * delta[i], axis=0, keepdims=True)
            for h in range(KV_HEADS):
                group = slice(h * GROUP, (h + 1) * GROUP)
                q_all = jnp.concatenate(qs[group], axis=0)
                do_all = jnp.concatenate(do[group], axis=0)
                dk3 = [_dot(jnp.concatenate([dsh[x] for dsh in ds[group]], axis=0), q_all, "tn") for x in range(3)]
                dv3 = [_dot(jnp.concatenate([ph[x] for ph in p[group]], axis=0), do_all, "tn") for x in range(3)]
                hs = _head_cols(h)
                for out_ref, carry, meta_ref, d3 in ((dk_ref, carry_k, dkm_ref, dk3),
                                                     (dv_ref, carry_v, dvm_ref, dv3)):
                    out_ref[:, hs] = carry[:, hs] + d3[0]
                    carry[:, hs] = d3[1]
                    meta_ref[:, hs] += d3[2]

        @pl.when(n == nb)
        def _():
            dk_ref[...] = carry_k[...]
            dv_ref[...] = carry_v[...]

    kv_shape = jax.ShapeDtypeStruct((lp, KV_W), f32)
    one_shape = jax.ShapeDtypeStruct((BLOCK, KV_W), f32)
    return pl.pallas_call(
        body, name=name, grid=(nb + 1,),
        in_specs=[pl.BlockSpec((BLOCK, Q_W), cur), kv(prev), kv(cur), kv(meta), kv(prev), kv(cur), kv(meta),
                  _full((1, Q_HEADS)), pl.BlockSpec((BLOCK, Q_W), cur), pl.BlockSpec((BLOCK, Q_W), cur)],
        out_specs=[pl.BlockSpec((BLOCK, Q_W), cur), kv(behind), kv(behind), kv(meta), kv(meta),
                   _full((1, Q_HEADS))],
        out_shape=[jax.ShapeDtypeStruct((lp, Q_W), f32), kv_shape, kv_shape, one_shape, one_shape,
                   jax.ShapeDtypeStruct((1, Q_HEADS), f32)],
        scratch_shapes=[pltpu.VMEM((BLOCK, KV_W), f32), pltpu.VMEM((BLOCK, KV_W), f32)],
        compiler_params=_params(("arbitrary",)),
    )(q, k, k, k, v, v, v, sinks, out, do)


@jax.custom_vjp
def _known_inverse(l, x):
    return x


def _known_inverse_fwd(l, x):
    return x, x


def _known_inverse_bwd(x, ct):
    return _dot(_dot(x, ct, "tn"), x, "nt"), jnp.zeros_like(x)


_known_inverse.defvjp(_known_inverse_fwd, _known_inverse_bwd)


@jax.custom_vjp
def _decayed(x, c):
    return (x * jnp.exp(c)).astype(bf16).astype(f32)


def _decayed_fwd(x, c):
    e = jnp.exp(c)
    out = (x * e).astype(bf16).astype(f32)
    return out, (e, out)


def _decayed_bwd(res, ct):
    e, out = res
    return ct * e, ct * out


_decayed.defvjp(_decayed_fwd, _decayed_bwd)


@jax.custom_vjp
def _pair(x, y):
    return _dot(x, y, "nt")


def _pair_fwd(x, y):
    return _dot(x, y, "nt"), (x, y)


def _pair_bwd(res, ct):
    x, y = res
    hi = ct.astype(bf16)
    lo = (ct - hi.astype(f32)).astype(bf16)
    return _dot(hi, y) + _dot(lo, y), _dot(hi, x, "tn") + _dot(lo, x, "tn")


_pair.defvjp(_pair_fwd, _pair_bwd)


def _scan_chunk(s0, r, lw, k, v, a, b, inv=None):
    t = r[0].shape[0]
    ii = lax.broadcasted_iota(jnp.int32, (t, t), 0)
    jj = lax.broadcasted_iota(jnp.int32, (t, t), 1)
    incl = jj <= ii
    strict = jj < ii
    tri = incl.astype(f32)
    eye = jnp.where(ii == jj, 1.0, 0.0)
    cl = [_const_dot(tri, x) for x in lw]
    mid = [c[t // 2 - 1:t // 2, :] for c in cl]
    s0 = [s * jnp.exp(m) for s, m in zip(s0, mid)]
    cl = [c - m for c, m in zip(cl, mid)]
    rt = [_decayed(x, c) for x, c in zip(r, cl)]
    at = [_decayed(x, c - l) for x, c, l in zip(a, cl, lw)]
    bt = [_decayed(x, -c) for x, c in zip(b, cl)]
    kt = [_decayed(x, -c) for x, c in zip(k, cl)]
    l_ab = [jnp.where(strict, _pair(x, y), 0.0) for x, y in zip(at, bt)]
    l_ak = [jnp.where(strict, _pair(x, y), 0.0) for x, y in zip(at, kt)]
    r_b = [jnp.where(incl, _pair(x, y), 0.0) for x, y in zip(rt, bt)]
    r_k = [jnp.where(incl, _pair(x, y), 0.0) for x, y in zip(rt, kt)]
    if inv is None:
        inv = [eye + x for x in l_ab]
        pw = l_ab
        for _ in range(int(math.log2(t)) - 1):
            pw = [_dot(x, x) for x in pw]
            inv = [x + _dot(x, y) for x, y in zip(inv, pw)]
    else:
        inv = [_known_inverse(x, y) for x, y in zip(l_ab, inv)]
    rhs = [_dot(x, s, "nt") + _dot(m, y) for x, s, m, y in zip(at, s0, l_ak, v)]
    u = [_dot(x, y) for x, y in zip(inv, rhs)]
    y_s = [_dot(x, s, "nt") for x, s in zip(rt, s0)]
    y = [ys + _dot(m, uu) + _dot(n, vv) for ys, m, uu, n, vv in zip(y_s, r_b, u, r_k, v)]
    grow = [s + _dot(uu, x, "tn") + _dot(vv, z, "tn") for s, uu, x, vv, z in zip(s0, u, bt, v, kt)]
    s1 = [g * jnp.exp(c[t - 1:t, :]) for g, c in zip(grow, cl)]
    return y, s1, inv


def _head_rows(h):
    return slice(h * RWKV_HEAD, (h + 1) * RWKV_HEAD)


def _per_head(ref):
    return [ref[:, _head_rows(h)] for h in range(RWKV_HEADS)]


def _scan(r, lw, k, v, a, b, *, name):
    lp = r.shape[0]
    nc = lp // CHUNK
    row = pl.BlockSpec((CHUNK, RWKV_DIM), lambda c: (c, 0))

    def body(r_ref, lw_ref, k_ref, v_ref, a_ref, b_ref, y_ref, s_ref, inv_ref, state):
        @pl.when(pl.program_id(0) == 0)
        def _():
            state[...] = jnp.zeros_like(state)

        s_ref[...] = state[...]
        s0 = [state[_head_rows(h), :] for h in range(RWKV_HEADS)]
        y, s1, inv = _scan_chunk(s0, *[_per_head(ref) for ref in (r_ref, lw_ref, k_ref, v_ref, a_ref, b_ref)])
        for h in range(RWKV_HEADS):
            y_ref[:, _head_rows(h)] = y[h]
            state[_head_rows(h), :] = s1[h]
            inv_ref[h * CHUNK:(h + 1) * CHUNK, :] = inv[h].astype(inv_ref.dtype)

    return pl.pallas_call(
        body, name=name, grid=(nc,), in_specs=[row] * 6,
        out_specs=[row, pl.BlockSpec((RWKV_DIM, RWKV_HEAD), lambda c: (c, 0)),
                   pl.BlockSpec((RWKV_HEADS * CHUNK, CHUNK), lambda c: (c, 0))],
        out_shape=[jax.ShapeDtypeStruct((lp, RWKV_DIM), f32), jax.ShapeDtypeStruct((nc * RWKV_DIM, RWKV_HEAD), f32),
                   jax.ShapeDtypeStruct((nc * RWKV_HEADS * CHUNK, CHUNK), bf16)],
        scratch_shapes=[pltpu.VMEM((RWKV_DIM, RWKV_HEAD), f32)],
        compiler_params=_params(("arbitrary",)),
    )(r, lw, k, v, a, b)


def _scan_bwd(r, lw, k, v, a, b, states, inverses, dy, *, name):
    lp = r.shape[0]
    nc = lp // CHUNK
    back = lambda c: (nc - 1 - c, 0)
    row = pl.BlockSpec((CHUNK, RWKV_DIM), back)

    def body(r_ref, lw_ref, k_ref, v_ref, a_ref, b_ref, s_ref, inv_ref, dy_ref,
             dr_ref, dlw_ref, dk_ref, dv_ref, da_ref, db_ref, dstate):
        @pl.when(pl.program_id(0) == 0)
        def _():
            dstate[...] = jnp.zeros_like(dstate)

        outs = (dr_ref, dlw_ref, dk_ref, dv_ref, da_ref, db_ref)
        s0 = [s_ref[_head_rows(h), :] for h in range(RWKV_HEADS)]
        inv = [inv_ref[h * CHUNK:(h + 1) * CHUNK, :].astype(f32) for h in range(RWKV_HEADS)]
        _, vjp = jax.vjp(lambda *args: _scan_chunk(*args, inv=inv)[:2], s0,
                         *[_per_head(ref) for ref in (r_ref, lw_ref, k_ref, v_ref, a_ref, b_ref)])
        g = vjp((_per_head(dy_ref), [dstate[_head_rows(h), :] for h in range(RWKV_HEADS)]))
        for h in range(RWKV_HEADS):
            dstate[_head_rows(h), :] = g[0][h]
            for o_ref, gv in zip(outs, g[1:]):
                o_ref[:, _head_rows(h)] = gv[h]

    shape = jax.ShapeDtypeStruct((lp, RWKV_DIM), f32)
    return pl.pallas_call(
        body, name=name, grid=(nc,),
        in_specs=[row] * 6 + [pl.BlockSpec((RWKV_DIM, RWKV_HEAD), back),
                              pl.BlockSpec((RWKV_HEADS * CHUNK, CHUNK), back), row],
        out_specs=[row] * 6, out_shape=[shape] * 6,
        scratch_shapes=[pltpu.VMEM((RWKV_DIM, RWKV_HEAD), f32)],
        compiler_params=_params(("arbitrary",)),
    )(r, lw, k, v, a, b, states, inverses, dy)


def _loss_head(h2, target, g_final, *, name):
    lp = h2.shape[0]
    per_tile = 3
    tm = per_tile * BLOCK
    last_block = (lp - FRONT) // BLOCK - 1

    def body(h_ref, t0_ref, t1_ref, t2_ref, g_ref, loss_ref, dh_ref, dg_ref):
        i = pl.program_id(0)
        target_rows = jnp.concatenate([t0_ref[...], t1_ref[...], t2_ref[...]], axis=0)
        real = i * tm + lax.broadcasted_iota(jnp.int32, (tm, 1), 0) >= FRONT

        def tile_loss(hv, gv):
            err = _rms(hv, gv) - target_rows
            return 0.5 * jnp.sum(jnp.where(real, jnp.mean(err * err, axis=-1, keepdims=True), 0.0))

        loss, (dh, dg) = jax.value_and_grad(tile_loss, argnums=(0, 1))(h_ref[...], g_ref[...])

        @pl.when(i == 0)
        def _():
            loss_ref[...] = jnp.zeros_like(loss_ref)
            dg_ref[...] = jnp.zeros_like(dg_ref)

        loss_ref[...] += jnp.full(loss_ref.shape, loss, f32)
        dg_ref[...] += dg
        dh_ref[...] = dh

    def target_block(j):
        return pl.BlockSpec((BLOCK, D_MODEL),
                            lambda i: (jnp.clip(per_tile * i + j - FRONT // BLOCK, 0, last_block), 0))

    return pl.pallas_call(
        body, name=name, grid=(lp // tm,),
        in_specs=[pl.BlockSpec((tm, D_MODEL), lambda i: (i, 0)), target_block(0), target_block(1), target_block(2),
                  _full(g_final.shape)],
        out_specs=[_full((8, 128)), pl.BlockSpec((tm, D_MODEL), lambda i: (i, 0)), _full(g_final.shape)],
        out_shape=[jax.ShapeDtypeStruct((8, 128), f32), jax.ShapeDtypeStruct((lp, D_MODEL), f32),
                   jax.ShapeDtypeStruct(g_final.shape, f32)],
        compiler_params=_params(("arbitrary",)),
    )(h2, target, target, target, g_final)


def _embed_norm(x, meta, g, *, name):
    seq = x.shape[0]
    lp = seq + FRONT
    per_tile = 3
    tm = per_tile * BLOCK
    last_block = seq // BLOCK - 1

    def body(x0_ref, x1_ref, x2_ref, meta_ref, g_ref, h_ref, u_ref):
        front = jnp.concatenate([jnp.zeros((PAD, D_MODEL), f32), meta_ref[...]], axis=0)
        first = jnp.where(pl.program_id(0) == 0, front, x0_ref[...])
        h = jnp.concatenate([first, x1_ref[...], x2_ref[...]], axis=0)
        h_ref[...] = h
        u_ref[...] = _rms(h, g_ref[...]).astype(u_ref.dtype)

    def x_block(j):
        return pl.BlockSpec((BLOCK, D_MODEL),
                            lambda i: (jnp.clip(per_tile * i + j - FRONT // BLOCK, 0, last_block), 0))

    tile = pl.BlockSpec((tm, D_MODEL), lambda i: (i, 0))
    return pl.pallas_call(
        body, name=name, grid=(lp // tm,),
        in_specs=[x_block(0), x_block(1), x_block(2), _full(meta.shape), _full(g.shape)],
        out_specs=[tile, tile],
        out_shape=[jax.ShapeDtypeStruct((lp, D_MODEL), f32), jax.ShapeDtypeStruct((lp, D_MODEL), bf16)],
        compiler_params=_params(("parallel",)),
    )(x, x, x, meta, g)


def _input_norm_bwd(h0, g, du, dh1, *, name):
    lp = h0.shape[0]
    tm = FRONT

    def body(h_ref, g_ref, du_ref, dh1_ref, dx_ref, front_ref, dg_ref):
        i = pl.program_id(0)
        _, vjp = jax.vjp(lambda hv, gv: (_rms(hv, gv), hv), h_ref[...], g_ref[...])
        dh, dg = vjp((du_ref[...].astype(f32), dh1_ref[...]))

        @pl.when(i == 0)
        def _():
            dg_ref[...] = jnp.zeros_like(dg_ref)
            front_ref[...] = dh

        dg_ref[...] += dg
        dx_ref[...] = dh

    tile = pl.BlockSpec((tm, D_MODEL), lambda i: (i, 0))
    return pl.pallas_call(
        body, name=name, grid=(lp // tm,),
        in_specs=[tile, _full(g.shape), tile, tile],
        out_specs=[pl.BlockSpec((tm, D_MODEL), lambda i: (jnp.maximum(i - 1, 0), 0)), _full((tm, D_MODEL)),
                   _full(g.shape)],
        out_shape=[jax.ShapeDtypeStruct((lp - tm, D_MODEL), f32), jax.ShapeDtypeStruct((tm, D_MODEL), f32),
                   jax.ShapeDtypeStruct(g.shape, f32)],
        compiler_params=_params(("arbitrary",)),
    )(h0, g, du, dh1)


def _local_step(x, target, meta, p, early_weights=None, late_weights=None, emit=None):
    emit = emit or (lambda group, grads: 0.0)
    seq = x.shape[0]
    lp = seq + FRONT
    cos_t, sin_t, swap = _rope_tables(lp)
    hsum = _head_sum_matrix(RWKV_DIM, RWKV_HEAD)
    hmean = hsum / RWKV_HEAD
    post_params = [p["ln_w"], p["ln_b"], p["r_k"], hmean]

    h0, u = _embed_norm(x, meta, p["norm_mix_g"], name="norm_mix")
    if early_weights is not None:
        p = {**p, **early_weights(u)}
    prep_params = [p["w0"], p["w2"], p["a0"], p["a2"], p["g2"], p["k_k"], p["k_a"], hsum]
    qkv, p_rkv, p_lora, gates = _proj_in(u, p["w_in_lr"], p["b_in"], [ATTN_PROJ, RKV_W, LORA_W, 2 * D_MODEL],
                                         name="proj_in", zero_rows_below=PAD)

    q, k, v = _rowwise(_attn_prep, [qkv, cos_t, sin_t], [swap], [(Q_W, bf16), (KV_W, bf16), (KV_W, bf16)],
                       name="attn_prep")
    y_attn = _attention(q, k, v, p["sinks"], name="attention")

    mix_rkv, mix_lora = p["mix"][:, :RKV_W], p["mix"][:, RKV_W:]
    r_, lw_, k_, v_, a_, b_, g_ = _mixer_inputs([p_rkv, p_lora], [mix_rkv, mix_lora], prep_params,
                                                name="mixer_inputs")
    y_scan, states, inverses = _scan(r_, lw_, k_, v_, a_, b_, name="wkv_scan")
    (y_rwkv,) = _rowwise(_rwkv_post, [y_scan, r_, k_, v_, g_], post_params, [(RWKV_DIM, bf16)], name="rwkv_post")

    if late_weights is not None:
        p = {**p, **late_weights(y_rwkv)}
    br_a, br_r, merged = _branch_merge(y_attn, y_rwkv, p["w_br_attn_t"], p["w_br_rwkv_t"], gates, name="branch_merge")
    h1, f = _residual_norm(merged, p["w_o"], h0, p["norm_ffn_g"], name="out_proj")
    gate, up, act = _ffn_in(f, p["w_gate_t"], p["w_up_t"], name="ffn_in")
    h2 = _mm(act, p["w_down"], "nn", name="ffn_down", add=h1)

    loss8, dh2, d_final_g = _loss_head(h2, target, p["norm_final_g"], name="loss_head")
    dgate, dup = _ffn_in_bwd(dh2, p["w_down"], gate, up, name="ffn_in_bwd")
    d_w_down = _mm_tn(act, dh2, name="dw_down")
    d_w_gate_t = _mm_tn(dgate, f, name="dw_gate")
    d_w_up_t = _mm_tn(dup, f, name="dw_up")
    zero = emit("ffn", dict(w_down=d_w_down, w_gate_t=d_w_gate_t, w_up_t=d_w_up_t))
    dh1, d_ffn_g = _residual_norm_bwd([dgate, dup], [p["w_gate_t"], p["w_up_t"]], h1, p["norm_ffn_g"] + zero, dh2,
                                      name="norm_ffn_bwd")
    dgates, dbr_a, dbr_r, dy_attn, dy_rwkv = _branch_merge_bwd(
        dh1, p["w_o"], gates, br_a, br_r, p["w_br_attn_t"], p["w_br_rwkv_t"], name="branch_merge_bwd")
    d_w_o = _mm_tn(merged, dh1, name="dw_o")
    d_w_br_attn_t = _mm_tn(dbr_a, y_attn, name="dw_br_attn")
    d_w_br_rwkv_t = _mm_tn(dbr_r, y_rwkv, name="dw_br_rwkv")
    zero = emit("branch", dict(w_o=d_w_o, w_br_attn_t=d_w_br_attn_t, w_br_rwkv_t=d_w_br_rwkv_t))

    post_params = [p["ln_w"] + zero, p["ln_b"], p["r_k"], hmean]
    res = _rowwise_bwd(_rwkv_post, [y_scan, r_, k_, v_, g_], post_params, [[dy_rwkv]], name="rwkv_post_bwd",
                       diff_rows=[True] * 5, diff_params=[True, True, True, False])
    dy_scan, dr_p, dk_p, dv_p, dg_p, d_ln_w, d_ln_b, d_r_k = res
    dr_s, dlw_s, dk_s, dv_s, da_s, db_s = _scan_bwd(r_, lw_, k_, v_, a_, b_, states, inverses, dy_scan,
                                                    name="wkv_scan_bwd")
    res = _mixer_inputs_bwd([p_rkv, p_lora], [mix_rkv, mix_lora], prep_params,
                            [[dr_s, dr_p], [dlw_s], [dk_s, dk_p], [dv_s, dv_p], [da_s], [db_s], [dg_p]],
                            name="mixer_inputs_bwd")
    dp_rkv, dp_lora, d_mix_rkv, d_mix_lora, d_w0, d_w2, d_a0, d_a2, d_g2, d_k_k, d_k_a = res

    dq, dk, dv, dkm, dvm, d_sinks = _attention_bwd(q, k, v, p["sinks"], y_attn, dy_attn, name="attention_bwd")
    rest = jnp.zeros((lp - BLOCK, KV_W), f32)
    dkm, dvm = jnp.concatenate([dkm, rest], axis=0), jnp.concatenate([dvm, rest], axis=0)
    (dqkv,) = _rowwise_bwd(_attn_prep, [qkv, cos_t, sin_t], [swap], [[dq], [dk, dkm], [dv, dvm]], name="attn_prep_bwd",
                           diff_rows=[True, False, False], diff_params=[False], out_dtypes=[bf16])

    d_w_qkv_t, db_qkv = _mm_tn(dqkv, u, name="dw_qkv", colsum=True)
    d_w_rkv_t, db_rkv = _mm_tn(dp_rkv, u, name="dw_rkv", colsum=True)
    d_w_lora_t, db_lora = _mm_tn(dp_lora, u, name="dw_lora", colsum=True)
    d_w_gates_t, db_gates = _mm_tn(dgates, u, name="dw_gates", colsum=True)
    d_w_in_t = jnp.concatenate([d_w_qkv_t, d_w_rkv_t, d_w_lora_t, d_w_gates_t], axis=0)
    zero = emit("input", dict(w_in_t=d_w_in_t, g2=d_g2, w2=d_w2, a2=d_a2))
    du = _proj_in_bwd([dqkv, dp_rkv, dp_lora, dgates], p["w_in_lr"], name="d_u")
    dx, d_front, d_mix_g = _input_norm_bwd(h0, p["norm_mix_g"] + zero, du, dh1, name="norm_mix_bwd")

    grads = dict(
        w_in_t=d_w_in_t,
        b_in=jnp.concatenate([db_qkv, db_rkv, db_lora, db_gates], axis=1),
        mix=jnp.concatenate([d_mix_rkv, d_mix_lora], axis=1),
        norm_mix_g=d_mix_g, sinks=d_sinks, w0=d_w0, w2=d_w2, a0=d_a0, a2=d_a2, g2=d_g2, k_k=d_k_k, k_a=d_k_a,
        r_k=d_r_k, ln_w=d_ln_w, ln_b=d_ln_b, w_br_attn_t=d_w_br_attn_t, w_br_rwkv_t=d_w_br_rwkv_t, w_o=d_w_o,
        norm_ffn_g=d_ffn_g, w_gate_t=d_w_gate_t, w_up_t=d_w_up_t, w_down=d_w_down, norm_final_g=d_final_g,
        meta=d_front[PAD:],
    )
    return loss8[0, 0], dx, grads


def _position():
    return lax.axis_index("x"), lax.axis_index("y"), lax.axis_index("c")


def _other_chips(x, y):
    return [(1 - x, y), (x, 1 - y), (1 - x, 1 - y)]


_HBM = pl.BlockSpec(memory_space=pltpu.HBM)
_SEM = pl.BlockSpec(memory_space=pltpu.SEMAPHORE)
_EFFECT = pltpu.SideEffectType.DATAFLOW_SIDE_EFFECTING


def _landing_zone(src, kind):
    shape = {"whole": (N_CHIPS,) + src.shape, "half": (2, N_CHIPS, src.shape[0], src.shape[1] // 2),
             "slab": (3,) + src.shape[1:], "sibling": src.shape}[kind]
    return lax.empty(shape, src.dtype)


def _copies_per_source(kind):
    return 1 if kind == "sibling" else 3


def _chip_copies(src_refs, land_refs, send_sems, recv_sems, kind):
    x, y, c = _position()
    if kind == "sibling":
        return [pltpu.make_async_remote_copy(
            src_ref=src, dst_ref=land, send_sem=send_sems.at[a], recv_sem=recv_sems.at[a],
            device_id=(x, y, 1 - c), device_id_type=MESH) for a, (src, land) in enumerate(zip(src_refs, land_refs))]
    copies = []
    for a, (src, land) in enumerate(zip(src_refs, land_refs)):
        for j, (px, py) in enumerate(_other_chips(x, y)):
            if kind == "whole":
                src_ref, dst_ref = src, land.at[2 * x + y]
            elif kind == "half":
                half = src.shape[1] // 2
                src_ref, dst_ref = src.at[:, pl.ds(pl.multiple_of(c * half, half), half)], land.at[c, 2 * x + y]
            else:
                src_ref, dst_ref = src.at[2 * px + py], land.at[j]
            copies.append(pltpu.make_async_remote_copy(
                src_ref=src_ref, dst_ref=dst_ref, send_sem=send_sems.at[3 * a + j], recv_sem=recv_sems.at[3 * a + j],
                device_id=(px, py, c), device_id_type=MESH))
    return copies


def _exchange_start(srcs, *, kind, name):
    n = len(srcs)
    lands = [_landing_zone(s, kind) for s in srcs]

    def body(*refs):
        for cp in _chip_copies(refs[:n], refs[n:2 * n], refs[2 * n], refs[2 * n + 1], kind):
            cp.start()
        refs[-1][...] = jnp.zeros_like(refs[-1])

    res = pl.pallas_call(
        body, name=name,
        out_shape=(pltpu.SemaphoreType.DMA((_copies_per_source(kind) * n,)),
                   pltpu.SemaphoreType.DMA((_copies_per_source(kind) * n,)),
                   *[pltpu.HBM(a.shape, a.dtype) for a in srcs + lands], jax.ShapeDtypeStruct((8, 128), f32)),
        in_specs=[_HBM] * (2 * n),
        out_specs=(_SEM, _SEM, *[_HBM] * (2 * n), pl.BlockSpec(memory_space=pltpu.VMEM)),
        input_output_aliases={i: 2 + i for i in range(2 * n)},
        compiler_params=pltpu.CompilerParams(has_side_effects=_EFFECT),
    )(*[pltpu.with_memory_space_constraint(a, pltpu.HBM) for a in srcs + lands])
    return res[0], res[1], list(res[2:2 + n]), list(res[2 + n:2 + 2 * n]), res[-1]


def _exchange_wait(handle, after, *, kind, name):
    send_sems, recv_sems, srcs, lands, _ = handle
    n = len(srcs)

    def body(*refs):
        for cp in _chip_copies(refs[:n], refs[n:2 * n], refs[2 * n], refs[2 * n + 1], kind):
            cp.wait_send()
            cp.wait_recv()

    res = pl.pallas_call(
        body, name=name,
        out_shape=tuple(pltpu.HBM(a.shape, a.dtype) for a in srcs + lands),
        in_specs=[_HBM] * (2 * n) + [_SEM, _SEM, pl.BlockSpec(memory_space=pl.ANY)],
        out_specs=tuple([_HBM] * (2 * n)),
        input_output_aliases={i: i for i in range(2 * n)},
        compiler_params=pltpu.CompilerParams(has_side_effects=_EFFECT),
    )(*srcs, *lands, send_sems, recv_sems, after)
    return list(res[:n]), list(res[n:])


def _sum_own_and_received(g, recv, *, name):
    _, r, w = g.shape
    tm = _tile(r)
    if g.dtype == bf16 and tm % 16:
        tm = r
    x, y, _ = _position()
    me = jnp.reshape(2 * x + y, (1,)).astype(jnp.int32)

    def body(me_ref, g_ref, r_ref, o_ref):
        o_ref[...] = (g_ref[0].astype(f32) + r_ref[0].astype(f32)) + (r_ref[1].astype(f32) + r_ref[2].astype(f32))

    return pl.pallas_call(
        body, name=name,
        grid_spec=pltpu.PrefetchScalarGridSpec(
            num_scalar_prefetch=1, grid=(r // tm,),
            in_specs=[pl.BlockSpec((1, tm, w), lambda i, me_ref: (me_ref[0], i, 0)),
                      pl.BlockSpec((3, tm, w), lambda i, me_ref: (0, i, 0))],
            out_specs=pl.BlockSpec((tm, w), lambda i, me_ref: (i, 0))),
        out_shape=jax.ShapeDtypeStruct((r, w), f32),
        compiler_params=_params(("parallel",)),
    )(me, g, recv)


def _swap_cores(arrs, *, name):
    n = len(arrs)

    def body(*refs):
        x, y, c = _position()
        copies = [pltpu.make_async_remote_copy(
            src_ref=refs[i], dst_ref=refs[n + i], send_sem=refs[2 * n].at[i], recv_sem=refs[2 * n + 1].at[i],
            device_id=(x, y, 1 - c), device_id_type=MESH) for i in range(n)]
        for cp in copies:
            cp.start()
        for cp in copies:
            cp.wait_recv()
        for cp in copies:
            cp.wait_send()

    return pl.pallas_call(
        body, name=name,
        in_specs=[pl.BlockSpec(memory_space=pl.ANY)] * n,
        out_specs=[pl.BlockSpec(memory_space=pl.ANY)] * n,
        out_shape=[jax.ShapeDtypeStruct(a.shape, a.dtype) for a in arrs],
        scratch_shapes=[pltpu.SemaphoreType.DMA((n,)), pltpu.SemaphoreType.DMA((n,))],
    )(*arrs)


def _swap_halves(zone, *, name):
    def body(z_ref, o_ref, send_sems, recv_sems):
        x, y, c = _position()
        mine = [pltpu.make_async_remote_copy(
            src_ref=o_ref.at[c, 2 * px + py], dst_ref=o_ref.at[c, 2 * px + py], send_sem=send_sems.at[j],
            recv_sem=recv_sems.at[j], device_id=(x, y, 1 - c), device_id_type=MESH)
            for j, (px, py) in enumerate(_other_chips(x, y))]
        for cp in mine:
            cp.start()
        for j, (px, py) in enumerate(_other_chips(x, y)):
            pltpu.make_async_remote_copy(
                src_ref=o_ref.at[c, 2 * px + py], dst_ref=o_ref.at[1 - c, 2 * px + py], send_sem=send_sems.at[j],
                recv_sem=recv_sems.at[j], device_id=(x, y, 1 - c), device_id_type=MESH).wait_recv()
        for cp in mine:
            cp.wait_send()

    return pl.pallas_call(
        body, name=name,
        in_specs=[pl.BlockSpec(memory_space=pl.ANY)], out_specs=pl.BlockSpec(memory_space=pl.ANY),
        out_shape=jax.ShapeDtypeStruct(zone.shape, zone.dtype), input_output_aliases={0: 0},
        scratch_shapes=[pltpu.SemaphoreType.DMA((3,)), pltpu.SemaphoreType.DMA((3,))],
    )(zone)


def _all_reduce_small(a, after, *, name):
    rows, w = a.shape

    def body(a_ref, after_ref, o_ref, buf, send_sems, recv_sems):
        x, y, c = _position()
        me = 4 * x + 2 * y + c
        buf[0] = a_ref[...]
        sends = []
        for rel in range(1, N_DEV):
            peer = ((1 - x) if rel & 4 else x, (1 - y) if rel & 2 else y, (1 - c) if rel & 1 else c)
            cp = pltpu.make_async_remote_copy(
                src_ref=a_ref, dst_ref=buf.at[rel], send_sem=send_sems.at[rel - 1], recv_sem=recv_sems.at[rel - 1],
                device_id=peer, device_id_type=MESH)
            cp.start()
            sends.append(cp)
        for cp in sends:
            cp.wait_recv()
        for cp in sends:
            cp.wait_send()
        acc = buf[jnp.bitwise_xor(me, 0)]
        for d in range(1, N_DEV):
            acc = acc + buf[jnp.bitwise_xor(me, d)]
        o_ref[...] = acc

    return pl.pallas_call(
        body, name=name,
        in_specs=[pl.BlockSpec(memory_space=pltpu.VMEM), pl.BlockSpec(memory_space=pl.ANY)],
        out_specs=pl.BlockSpec(memory_space=pltpu.VMEM),
        out_shape=jax.ShapeDtypeStruct((rows, w), f32),
        scratch_shapes=[pltpu.VMEM((N_DEV, rows, w), f32), pltpu.SemaphoreType.DMA((N_DEV - 1,)),
                        pltpu.SemaphoreType.DMA((N_DEV - 1,))],
    )(a, after)


def _adamw(w, g_parts, m, v, *, name, transposed=False):
    rows, cols = w.shape
    if transposed:
        tm = 256 if rows % 256 == 0 else rows
        g_spec = pl.BlockSpec((cols, tm), lambda i: (0, i))
    else:
        tm = _tile(rows, 256)
        g_spec = pl.BlockSpec((tm, cols), lambda i: (i, 0))
    n = len(g_parts)

    def body(*refs):
        w_ref, m_ref, v_ref = refs[0], refs[1 + n], refs[2 + n]
        g_ref, d_ref, nm_ref, nv_ref = refs[3 + n:]
        gv = refs[1][...]
        for part in refs[2:1 + n]:
            gv = gv + part[...]
        if transposed:
            gv = gv.T
        g_ref[...] = gv
        nm = ADAM_B1 * m_ref[...] + (1.0 - ADAM_B1) * gv
        nv = ADAM_B2 * v_ref[...] + (1.0 - ADAM_B2) * (gv * gv)
        m_hat = nm / (1.0 - ADAM_B1 ** ADAM_STEP)
        v_hat = nv / (1.0 - ADAM_B2 ** ADAM_STEP)
        d_ref[...] = -ADAM_LR * (m_hat / (jnp.sqrt(v_hat) + ADAM_EPS) + ADAM_WD * w_ref[...])
        nm_ref[...] = nm
        nv_ref[...] = nv

    spec = pl.BlockSpec((tm, cols), lambda i: (i, 0))
    shape = jax.ShapeDtypeStruct((rows, cols), f32)
    return pl.pallas_call(
        body, name=name, grid=(rows // tm,), in_specs=[spec] + [g_spec] * n + [spec] * 2,
        out_specs=[spec] * 4, out_shape=[shape] * 4,
        compiler_params=_params(("parallel",)),
    )(w, *g_parts, m, v)


def _pad_rows(a, rows):
    return jnp.concatenate([a, jnp.zeros((rows - a.shape[0], a.shape[1]), a.dtype)], axis=0) if rows > a.shape[0] else a


_SMALL = (("norm_mix_g", D_MODEL), ("b_in", D_IN), ("sinks", Q_HEADS), ("mix", RWKV_PROJ), ("w0", RWKV_DIM),
          ("a0", RWKV_DIM), ("k_k", RWKV_DIM), ("k_a", RWKV_DIM), ("r_k", RWKV_DIM), ("ln_w", RWKV_DIM),
          ("ln_b", RWKV_DIM), ("norm_ffn_g", D_MODEL), ("norm_final_g", D_MODEL))


def _pack_small(d):
    flat = jnp.concatenate([d[n].reshape(-1).astype(f32) for n, _ in _SMALL])
    return flat


def _unpack_small(flat):
    out, off = {}, 0
    for n, size in _SMALL:
        out[n] = flat[off:off + size]
        off += size
    return out


_SMALL_TOTAL = sum(s for _, s in _SMALL)


def kernel(x, meta_tokens, norm_mix_g, w_in, b_in, attn_sinks, rwkv_mix, rwkv_w0, rwkv_w2, rwkv_a0, rwkv_a2, rwkv_g2, rwkv_k_k, rwkv_k_a, rwkv_r_k, rwkv_ln_w, rwkv_ln_b, w_br_attn, w_br_rwkv, w_o, norm_ffn_g, w_ffn_gate, w_ffn_up, w_ffn_down, norm_final_g, loss_target, m_meta_tokens, m_norm_mix_g, m_w_in, m_b_in, m_attn_sinks, m_rwkv_mix, m_rwkv_w0, m_rwkv_w2, m_rwkv_a0, m_rwkv_a2, m_rwkv_g2, m_rwkv_k_k, m_rwkv_k_a, m_rwkv_r_k, m_rwkv_ln_w, m_rwkv_ln_b, m_w_br_attn, m_w_br_rwkv, m_w_o, m_norm_ffn_g, m_w_ffn_gate, m_w_ffn_up, m_w_ffn_down, m_norm_final_g, v_meta_tokens, v_norm_mix_g, v_w_in, v_b_in, v_attn_sinks, v_rwkv_mix, v_rwkv_w0, v_rwkv_w2, v_rwkv_a0, v_rwkv_a2, v_rwkv_g2, v_rwkv_k_k, v_rwkv_k_a, v_rwkv_r_k, v_rwkv_ln_w, v_rwkv_ln_b, v_w_br_attn, v_w_br_rwkv, v_w_o, v_norm_ffn_g, v_w_ffn_gate, v_w_ffn_up, v_w_ffn_down, v_norm_final_g):
    names = ("meta_tokens", "norm_mix_g", "w_in", "b_in", "attn_sinks", "rwkv_mix", "rwkv_w0", "rwkv_w2", "rwkv_a0",
             "rwkv_a2", "rwkv_g2", "rwkv_k_k", "rwkv_k_a", "rwkv_r_k", "rwkv_ln_w", "rwkv_ln_b", "w_br_attn",
             "w_br_rwkv", "w_o", "norm_ffn_g", "w_ffn_gate", "w_ffn_up", "w_ffn_down", "norm_final_g")
    w_all = dict(zip(names, (meta_tokens, norm_mix_g, w_in, b_in, attn_sinks, rwkv_mix, rwkv_w0, rwkv_w2, rwkv_a0,
                             rwkv_a2, rwkv_g2, rwkv_k_k, rwkv_k_a, rwkv_r_k, rwkv_ln_w, rwkv_ln_b, w_br_attn,
                             w_br_rwkv, w_o, norm_ffn_g, w_ffn_gate, w_ffn_up, w_ffn_down, norm_final_g)))
    m_all = dict(zip(names, (m_meta_tokens, m_norm_mix_g, m_w_in, m_b_in, m_attn_sinks, m_rwkv_mix, m_rwkv_w0,
                             m_rwkv_w2, m_rwkv_a0, m_rwkv_a2, m_rwkv_g2, m_rwkv_k_k, m_rwkv_k_a, m_rwkv_r_k,
                             m_rwkv_ln_w, m_rwkv_ln_b, m_w_br_attn, m_w_br_rwkv, m_w_o, m_norm_ffn_g, m_w_ffn_gate,
                             m_w_ffn_up, m_w_ffn_down, m_norm_final_g)))
    v_all = dict(zip(names, (v_meta_tokens, v_norm_mix_g, v_w_in, v_b_in, v_attn_sinks, v_rwkv_mix, v_rwkv_w0,
                             v_rwkv_w2, v_rwkv_a0, v_rwkv_a2, v_rwkv_g2, v_rwkv_k_k, v_rwkv_k_a, v_rwkv_r_k,
                             v_rwkv_ln_w, v_rwkv_ln_b, v_w_br_attn, v_w_br_rwkv, v_w_o, v_norm_ffn_g, v_w_ffn_gate,
                             v_w_ffn_up, v_w_ffn_down, v_norm_final_g)))
    cx, cy, _ = _position()
    chip = 2 * cx + cy

    t_of = dict(w_in_t="w_in", w_gate_t="w_ffn_gate", w_up_t="w_ffn_up", w_br_attn_t="w_br_attn",
                w_br_rwkv_t="w_br_rwkv", g2_t="rwkv_g2", w2_t="rwkv_w2", a2_t="rwkv_a2")
    plain_of = dict(w_down="w_ffn_down", w_o="w_o")
    meta_cols = meta_tokens.shape[1]

    def shard(k):
        return (w_all[t_of[k]][0].T if k in t_of else w_all[plain_of[k]][0]).astype(bf16)

    def whole(zone, own):
        return lax.dynamic_update_slice_in_dim(zone, own[None], chip, axis=0).reshape(-1, own.shape[-1])

    tiny = ("g2_t", "w2_t", "a2_t")
    late = ("w_gate_t", "w_up_t", "w_down", "w_o", "w_br_attn_t", "w_br_rwkv_t")
    w_in_own = shard("w_in_t")
    w_in_rows, w_in_cols = w_in_own.shape
    tiny_h = _exchange_start([shard(k) for k in tiny] + [meta_tokens], kind="whole", name="gather_tiny_start")
    w_in_h = _exchange_start([w_in_own + tiny_h[4][0, 0].astype(bf16)], kind="half", name="gather_w_in_start")
    behind = w_in_h[4][0, 0].astype(bf16)
    late_h = _exchange_start([shard(k) + behind for k in late], kind="whole", name="gather_late_start")
    own, zones = _exchange_wait(tiny_h, late_h[4], kind="whole", name="gather_tiny_wait")
    got = {k: whole(z, o) for k, z, o in zip(tiny, zones, own)}
    meta_full = whole(zones[-1], own[-1]).reshape(N_CHIPS, N_META, meta_cols).transpose(1, 0, 2).reshape(N_META, -1)
    p = dict(
        g2=got["g2_t"].T.astype(f32), w2=got["w2_t"].T.astype(f32), a2=got["a2_t"].T.astype(f32),
        b_in=b_in, sinks=attn_sinks, mix=rwkv_mix, w0=rwkv_w0, a0=rwkv_a0, k_k=rwkv_k_k, k_a=rwkv_k_a,
        r_k=rwkv_r_k.reshape(1, RWKV_DIM), ln_w=rwkv_ln_w, ln_b=rwkv_ln_b, norm_mix_g=norm_mix_g,
        norm_ffn_g=norm_ffn_g, norm_final_g=norm_final_g.reshape(1, D_MODEL),
    )

    def early_weights(after):
        own_h, zones_h = _exchange_wait(w_in_h, after, kind="half", name="gather_w_in_wait")
        zone = _swap_halves(zones_h[0], name="swap_w_in_halves")
        own_halves = own_h[0].reshape(w_in_rows, 2, w_in_cols // 2).transpose(1, 0, 2)[:, None]
        zone = lax.dynamic_update_slice(zone, own_halves, (0, chip, 0, 0))
        return dict(w_in_lr=zone.reshape(2, N_CHIPS * w_in_rows, w_in_cols // 2))

    def late_weights(after):
        own_l, zones_l = _exchange_wait(late_h, after, kind="whole", name="gather_late_wait")
        return {k: whole(z, o) for k, z, o in zip(late, zones_l, own_l)}

    started = {}

    def partial_sums(groups, after):
        parts = {}
        for group in groups:
            keys, handle = started[group]
            slabs, lands = _exchange_wait(handle, after, kind="slab", name="scatter_" + group + "_wait")
            parts.update({k: _sum_own_and_received(s, l, name="sum_chips_" + k) for k, s, l in zip(keys, slabs, lands)})
        return parts

    def emit(group, grads_):
        keys = list(grads_)
        slabs = []
        for k in keys:
            a = grads_[k].T if k in ("g2", "w2", "a2") else grads_[k]
            slabs.append(a.reshape(N_CHIPS, a.shape[0] // N_CHIPS, a.shape[1]))
        started[group] = (keys, _exchange_start(slabs, kind="slab", name="scatter_" + group + "_start"))
        zero = started[group][1][4]
        if group == "input":
            started["parts_a"] = partial_sums(("ffn", "branch"), zero)
            started["swap_a"] = _exchange_start(list(started["parts_a"].values()), kind="sibling",
                                                name="swap_cores_a_start")
            zero = started["swap_a"][4]
        return zero[0, 0]

    loss, dx, g = _local_step(x[0], loss_target[0], meta_full, p, early_weights, late_weights, emit)

    grads, delta, new_m, new_v = {}, {}, {}, {}
    in_grad_layout = ("w_in_t", "w_gate_t", "w_up_t")
    weight_of = {**t_of, **plain_of}

    def update(keys, mine, theirs):
        for k, part, other in zip(keys, mine, theirs):
            both = [part, other]
            k = k + "_t" if k in ("g2", "w2", "a2") else k
            n = weight_of[k]
            shape2 = w_all[n].shape[1:]
            w_, m_, v_ = (a.reshape(shape2) for a in (w_all[n], m_all[n], v_all[n]))
            if k in in_grad_layout:
                res = [t.T for t in _adamw(w_.T, both, m_.T, v_.T, name="adamw_" + n)]
            else:
                res = _adamw(w_, both, m_, v_, name="adamw_" + n, transposed=k in t_of)
            grads[n], delta[n], new_m[n], new_v[n] = (t.reshape(w_all[n].shape) for t in res)
        return delta[n]

    done = update(list(started["parts_a"]),
                  *_exchange_wait(started["swap_a"], dx, kind="sibling", name="swap_cores_a_wait"))
    small = jnp.concatenate([_pack_small(g), loss.reshape(1)])
    small_rows = -(-small.shape[0] // PACK_W)
    small = jnp.concatenate([small, jnp.zeros((small_rows * PACK_W - small.shape[0],), f32)]).reshape(small_rows, PACK_W)
    small_rows8 = -(-(small_rows + N_META) // 8) * 8
    reduced = _all_reduce_small(_pad_rows(jnp.concatenate([g["meta"], small], axis=0), small_rows8), done,
                                name="reduce_small")
    parts_b = partial_sums(("input",), reduced)
    update(list(parts_b), list(parts_b.values()), _swap_cores(list(parts_b.values()), name="swap_cores_b"))
    g_meta = lax.dynamic_slice_in_dim(reduced[:N_META], chip * meta_cols, meta_cols, axis=1)
    flat = reduced[N_META:N_META + small_rows].reshape(-1)
    g_small = _unpack_small(flat)
    loss_total = flat[_SMALL_TOTAL]

    small_of = dict(norm_mix_g="norm_mix_g", b_in="b_in", attn_sinks="sinks", rwkv_mix="mix", rwkv_w0="w0",
                    rwkv_a0="a0", rwkv_k_k="k_k", rwkv_k_a="k_a", rwkv_r_k="r_k", rwkv_ln_w="ln_w",
                    rwkv_ln_b="ln_b", norm_ffn_g="norm_ffn_g", norm_final_g="norm_final_g")
    grads["meta_tokens"] = g_meta
    for n, k in small_of.items():
        grads[n] = g_small[k].reshape(w_all[n].shape)

    rest = [n for n in names if n not in delta]

    def pack_rest(src):
        flat_ = jnp.concatenate([src[n].reshape(-1) for n in rest])
        rows_ = -(-flat_.shape[0] // (8 * PACK_W)) * 8
        return jnp.concatenate([flat_, jnp.ones((rows_ * PACK_W - flat_.shape[0],), f32)]).reshape(rows_, PACK_W)

    _, d_, m_, v_ = _adamw(pack_rest(w_all), [pack_rest(grads)], pack_rest(m_all), pack_rest(v_all),
                           name="adamw_small")
    off = 0
    for n in rest:
        size = w_all[n].size
        for dst, src in ((delta, d_), (new_m, m_), (new_v, v_)):
            dst[n] = src.reshape(-1)[off:off + size].reshape(w_all[n].shape)
        off += size

    return (loss_total, dx.reshape(x.shape), *[grads[n] for n in names], *[delta[n] for n in names],
            *[new_m[n] for n in names], *[new_v[n] for n in names])
```

```python
import math

import jax
import jax.numpy as jnp
from jax import lax
from jax.experimental import pallas as pl
from jax.experimental.pallas import tpu as pltpu

f32 = jnp.float32
bf16 = jnp.bfloat16

D_MODEL = 1024
N_META = 16
HEAD_DIM = 64
Q_HEADS = 8
KV_HEADS = 2
GROUP = Q_HEADS // KV_HEADS
WINDOW = 128
BLOCK = 128
ROPE_THETA = 500000.0
ROPE_DIM = HEAD_DIM // 4
RWKV_HEADS = 8
RWKV_HEAD = 64
RWKV_DIM = RWKV_HEADS * RWKV_HEAD
DECAY_LORA = 64
AAA_LORA = 64
GATE_LORA = 160
LORA_W = DECAY_LORA + AAA_LORA + GATE_LORA
RWKV_LN_EPS = 64e-5
D_FF = 2816
Q_W = Q_HEADS * HEAD_DIM
KV_W = KV_HEADS * HEAD_DIM
ATTN_PROJ = Q_W + 2 * KV_W
RKV_W = 3 * RWKV_DIM
RWKV_PROJ = RKV_W + LORA_W
D_IN = ATTN_PROJ + RWKV_PROJ + 2 * D_MODEL
RMS_EPS = 1e-6
NEG_INF = -1e30
PAD = BLOCK - N_META
FRONT = PAD + N_META

ADAM_LR = 0.001
ADAM_B1 = 0.9
ADAM_B2 = 0.999
ADAM_EPS = 1e-08
ADAM_WD = 0.01
ADAM_STEP = 10

N_CHIPS = 4
N_DEV = 8
CHUNK = 128
VMEM_LIMIT = 56 * 1024 * 1024
MM_ROWS = 704
PACK_W = 1024
MESH = pl.DeviceIdType.MESH


def _tile(m, pref=384):
    for step in (16, 8):
        for t in range(min(m, pref) // step * step, 0, -step):
            if m % t == 0:
                return t
    return m


def _params(sem=None):
    return pltpu.CompilerParams(dimension_semantics=sem, vmem_limit_bytes=VMEM_LIMIT)


def _full(shape):
    nd = len(shape)
    return pl.BlockSpec(shape, lambda *_: (0,) * nd)


def _dot(a, b, dims="nn"):
    dn = {"nn": (((1,), (0,)), ((), ())), "nt": (((1,), (1,)), ((), ())), "tn": (((0,), (0,)), ((), ()))}[dims]
    return lax.dot_general(a.astype(bf16), b.astype(bf16), dn, preferred_element_type=f32)


def _two_pass(x, m, dims="nn"):
    x_hi = x.astype(bf16)
    x_lo = (x - x_hi.astype(f32)).astype(bf16)
    return _dot(x_hi, m, dims) + _dot(x_lo, m, dims)


@jax.custom_vjp
def _dot_const(x, m):
    return _two_pass(x, m)


def _dot_const_fwd(x, m):
    return _two_pass(x, m), m


def _dot_const_bwd(m, ct):
    return _two_pass(ct, m, "nt"), jnp.zeros_like(m)


_dot_const.defvjp(_dot_const_fwd, _dot_const_bwd)


def _two_pass_left(m, x, dims):
    x_hi = x.astype(bf16)
    x_lo = (x - x_hi.astype(f32)).astype(bf16)
    return _dot(m, x_hi, dims) + _dot(m, x_lo, dims)


@jax.custom_vjp
def _const_dot(m, x):
    return _two_pass_left(m, x, "nn")


def _const_dot_fwd(m, x):
    return _two_pass_left(m, x, "nn"), m


def _const_dot_bwd(m, ct):
    return jnp.zeros_like(m), _two_pass_left(m, ct, "tn")


_const_dot.defvjp(_const_dot_fwd, _const_dot_bwd)


def _mm(a, b, mode, *, name, out_dtype=f32, bias=None, add=None, zero_rows_below=0):
    m, _ = a.shape
    n = b.shape[1] if mode == "nn" else b.shape[0]
    tm = _tile(m, MM_ROWS)
    has_bias, has_add = bias is not None, add is not None

    def body(*refs):
        a_ref, b_ref = refs[0], refs[1]
        o_ref = refs[-1]
        acc = _dot(a_ref[...], b_ref[...], mode)
        k = 2
        if has_bias:
            acc = acc + refs[k][...]
            k += 1
        if zero_rows_below:
            rows = pl.program_id(0) * tm + lax.broadcasted_iota(jnp.int32, acc.shape, 0)
            acc = jnp.where(rows >= zero_rows_below, acc, 0.0)
        if has_add:
            acc = acc + refs[k][...].astype(f32)
        o_ref[...] = acc.astype(out_dtype)

    ins = [a, b]
    in_specs = [pl.BlockSpec((tm, a.shape[1]), lambda i: (i, 0)), _full(b.shape)]
    if has_bias:
        ins.append(bias)
        in_specs.append(_full(bias.shape))
    if has_add:
        ins.append(add)
        in_specs.append(pl.BlockSpec((tm, n), lambda i: (i, 0)))
    return pl.pallas_call(
        body, name=name, grid=(m // tm,), in_specs=in_specs,
        out_specs=pl.BlockSpec((tm, n), lambda i: (i, 0)),
        out_shape=jax.ShapeDtypeStruct((m, n), out_dtype),
        compiler_params=_params(("parallel",)),
    )(*ins)


def _pieces(widths):
    out, off = [], 0
    for w in widths:
        out.append((off, w))
        off += w
    return out


def _proj_in(a, w_lr, bias, widths, rope, *, name, zero_rows_below=0):
    m, kdim = a.shape
    half = kdim // 2
    tm = _tile(m, MM_ROWS)
    cos_t, sin_t, swap = rope
    out_widths = [Q_W, KV_W, KV_W] + list(widths[1:])

    def body(a_ref, w_ref, b_ref, cos_ref, sin_ref, swap_ref, *outs):
        a_l, a_r = a_ref[:, :half], a_ref[:, half:]
        for j, (off, width) in enumerate(_pieces(widths)):
            acc = _dot(a_l, w_ref[0, off:off + width, :], "nt") + _dot(a_r, w_ref[1, off:off + width, :], "nt")
            acc = acc + b_ref[:, off:off + width]
            if zero_rows_below:
                rows = pl.program_id(0) * tm + lax.broadcasted_iota(jnp.int32, acc.shape, 0)
                acc = jnp.where(rows >= zero_rows_below, acc, 0.0)
            if j == 0:
                qkv = acc.astype(bf16).astype(f32)
                for o_ref, val in zip(outs[:3], _attn_prep(qkv, cos_ref[...], sin_ref[...], swap_ref[...])):
                    o_ref[...] = val.astype(o_ref.dtype)
            else:
                outs[2 + j][...] = acc.astype(outs[2 + j].dtype)

    table = pl.BlockSpec((tm, HEAD_DIM), lambda i: (i, 0))
    return pl.pallas_call(
        body, name=name, grid=(m // tm,),
        in_specs=[pl.BlockSpec((tm, kdim), lambda i: (i, 0)), _full(w_lr.shape), _full(bias.shape), table, table,
                  _full(swap.shape)],
        out_specs=[pl.BlockSpec((tm, w), lambda i: (i, 0)) for w in out_widths],
        out_shape=[jax.ShapeDtypeStruct((m, w), bf16) for w in out_widths],
        compiler_params=_params(("parallel",)),
    )(a, w_lr, bias, cos_t, sin_t, swap)


def _proj_in_bwd(d_list, w_lr, *, name):
    m = d_list[0].shape[0]
    half = w_lr.shape[2]
    widths = [d.shape[1] for d in d_list]
    tm = _tile(m, MM_ROWS)

    def body(*refs):
        w_ref, o_ref = refs[-2], refs[-1]
        for side in range(2):
            acc = None
            for (off, width), d_ref in zip(_pieces(widths), refs):
                term = _dot(d_ref[...], w_ref[side, off:off + width, :])
                acc = term if acc is None else acc + term
            o_ref[:, side * half:(side + 1) * half] = acc.astype(o_ref.dtype)

    return pl.pallas_call(
        body, name=name, grid=(m // tm,),
        in_specs=[pl.BlockSpec((tm, w), lambda i: (i, 0)) for w in widths] + [_full(w_lr.shape)],
        out_specs=pl.BlockSpec((tm, 2 * half), lambda i: (i, 0)),
        out_shape=jax.ShapeDtypeStruct((m, 2 * half), bf16),
        compiler_params=_params(("parallel",)),
    )(*d_list, w_lr)


def _residual_norm(a, w, res, g, *, name):
    m, d = res.shape
    tm = _tile(m, MM_ROWS)

    def body(a_ref, w_ref, r_ref, g_ref, h_ref, n_ref):
        h = _dot(a_ref[...], w_ref[...]) + r_ref[...]
        h_ref[...] = h
        n_ref[...] = _rms(h, g_ref[...]).astype(n_ref.dtype)

    tile = pl.BlockSpec((tm, d), lambda i: (i, 0))
    return pl.pallas_call(
        body, name=name, grid=(m // tm,),
        in_specs=[pl.BlockSpec((tm, a.shape[1]), lambda i: (i, 0)), _full(w.shape), tile, _full(g.shape)],
        out_specs=[tile, tile],
        out_shape=[jax.ShapeDtypeStruct((m, d), f32), jax.ShapeDtypeStruct((m, d), bf16)],
        compiler_params=_params(("parallel",)),
    )(a, w, res, g)


def _residual_norm_bwd(d_list, w_list, h, g, dh_out, *, name):
    m, d = h.shape
    k = len(d_list)
    tm = _tile(m)

    def body(*refs):
        h_ref, g_ref, dho_ref, dh_ref, dg_ref = refs[2 * k:]
        dn = _dot(refs[0][...], refs[k][...])
        for i in range(1, k):
            dn = dn + _dot(refs[i][...], refs[k + i][...])
        _, vjp = jax.vjp(lambda hv, gv: (_rms(hv, gv), hv), h_ref[...], g_ref[...])
        dh, dg = vjp((dn, dho_ref[...]))
        dh_ref[...] = dh

        @pl.when(pl.program_id(0) == 0)
        def _():
            dg_ref[...] = jnp.zeros_like(dg_ref)

        dg_ref[...] += dg

    tile = pl.BlockSpec((tm, d), lambda i: (i, 0))
    return pl.pallas_call(
        body, name=name, grid=(m // tm,),
        in_specs=[pl.BlockSpec((tm, a.shape[1]), lambda i: (i, 0)) for a in d_list] + [_full(w.shape) for w in w_list]
        + [tile, _full(g.shape), tile],
        out_specs=[tile, _full(g.shape)],
        out_shape=[jax.ShapeDtypeStruct((m, d), f32), jax.ShapeDtypeStruct(g.shape, f32)],
        compiler_params=_params(("arbitrary",)),
    )(*d_list, *w_list, h, g, dh_out)


def _mm_tn(a, b, *, name, colsum=False, out_dtype=bf16):
    r, m = a.shape
    n = b.shape[1]
    tr = _tile(r, 1408)
    tmo = m
    for cand in (1408, 1024, 768, 512):
        if m > 1024 and m % cand == 0:
            tmo = cand
            break
    steps = r // tr

    def body(a_ref, b_ref, o_ref, *rest):
        acc = rest[-1]
        i = pl.program_id(1)

        @pl.when(i == 0)
        def _():
            acc[...] = jnp.zeros_like(acc)
            if colsum:
                rest[0][...] = jnp.zeros_like(rest[0])

        acc[...] += _dot(a_ref[...], b_ref[...], "tn")
        if colsum:
            rest[0][...] += jnp.sum(a_ref[...].astype(f32), axis=0, keepdims=True)

        @pl.when(i == steps - 1)
        def _():
            o_ref[...] = acc[...].astype(out_dtype)

    out_shape = [jax.ShapeDtypeStruct((m, n), out_dtype)]
    out_specs = [pl.BlockSpec((tmo, n), lambda j, i: (j, 0))]
    if colsum:
        out_shape.append(jax.ShapeDtypeStruct((1, m), f32))
        out_specs.append(pl.BlockSpec((1, tmo), lambda j, i: (0, j)))
    res = pl.pallas_call(
        body, name=name, grid=(m // tmo, steps),
        in_specs=[pl.BlockSpec((tr, tmo), lambda j, i: (i, j)), pl.BlockSpec((tr, n), lambda j, i: (i, 0))],
        out_specs=out_specs, out_shape=out_shape,
        scratch_shapes=[pltpu.VMEM((tmo, n), f32)],
        compiler_params=_params(("parallel", "arbitrary")),
    )(a, b)
    return res if colsum else res[0]


def _rowwise(fn, rows, params, outs, *, name, tm=None):
    m = rows[0].shape[0]
    tm = tm or _tile(m, MM_ROWS)
    nr, npar = len(rows), len(params)

    def body(*refs):
        vals = [r[...] for r in refs[:nr + npar]]
        res = fn(*vals)
        for o_ref, v in zip(refs[nr + npar:], res):
            o_ref[...] = v.astype(o_ref.dtype)

    return pl.pallas_call(
        body, name=name, grid=(m // tm,),
        in_specs=[pl.BlockSpec((tm, r.shape[1]), lambda i: (i, 0)) for r in rows] + [_full(p.shape) for p in params],
        out_specs=[pl.BlockSpec((tm, w), lambda i: (i, 0)) for w, _ in outs],
        out_shape=[jax.ShapeDtypeStruct((m, w), dt) for w, dt in outs],
        compiler_params=_params(("parallel",)),
    )(*rows, *params)


def _rowwise_bwd(fn, rows, params, cts, *, name, diff_rows, diff_params, tm=None, zero_rows_below=0, out_dtypes=None):
    m = rows[0].shape[0]
    tm = tm or _tile(m)
    nr, npar = len(rows), len(params)
    d_idx = [i for i in range(nr) if diff_rows[i]]
    p_idx = [i for i in range(npar) if diff_params[i]]
    out_dtypes = out_dtypes or [f32] * len(d_idx)
    flat_cts = [c for group in cts for c in group]
    n_ct = len(flat_cts)

    def body(*refs):
        vals = [r[...] for r in refs[:nr + npar]]
        ct_refs = refs[nr + npar:nr + npar + n_ct]
        out_refs = refs[nr + npar + n_ct:]
        ct_vals, k = [], 0
        for group in cts:
            acc = ct_refs[k][...].astype(f32)
            for extra in range(1, len(group)):
                acc = acc + ct_refs[k + extra][...].astype(f32)
            k += len(group)
            if zero_rows_below:
                rr = pl.program_id(0) * tm + lax.broadcasted_iota(jnp.int32, acc.shape, 0)
                acc = jnp.where(rr >= zero_rows_below, acc, 0.0)
            ct_vals.append(acc)

        def g(*dargs):
            full = list(vals)
            for pos, i in enumerate(d_idx):
                full[i] = dargs[pos]
            for pos, i in enumerate(p_idx):
                full[nr + i] = dargs[len(d_idx) + pos]
            return tuple(fn(*full))

        _, vjp = jax.vjp(g, *[vals[i].astype(f32) for i in d_idx], *[vals[nr + i] for i in p_idx])
        grads = vjp(tuple(ct_vals))
        for pos in range(len(d_idx)):
            out_refs[pos][...] = grads[pos].astype(out_refs[pos].dtype)
        first = pl.program_id(0) == 0
        for pos in range(len(p_idx)):
            o_ref = out_refs[len(d_idx) + pos]

            @pl.when(first)
            def _(o_ref=o_ref):
                o_ref[...] = jnp.zeros_like(o_ref)

            o_ref[...] += grads[len(d_idx) + pos]

    return pl.pallas_call(
        body, name=name, grid=(m // tm,),
        in_specs=[pl.BlockSpec((tm, r.shape[1]), lambda i: (i, 0)) for r in rows] + [_full(p.shape) for p in params]
        + [pl.BlockSpec((tm, c.shape[1]), lambda i: (i, 0)) for c in flat_cts],
        out_specs=[pl.BlockSpec((tm, rows[i].shape[1]), lambda i_: (i_, 0)) for i in d_idx]
        + [_full(params[i].shape) for i in p_idx],
        out_shape=[jax.ShapeDtypeStruct(rows[i].shape, dt) for i, dt in zip(d_idx, out_dtypes)]
        + [jax.ShapeDtypeStruct(params[i].shape, f32) for i in p_idx],
        compiler_params=_params(("arbitrary",)),
    )(*rows, *params, *flat_cts)


def _rms(x, g):
    return x * lax.rsqrt(jnp.mean(x * x, axis=-1, keepdims=True) + RMS_EPS) * g


def _head_sum_matrix(width, head):
    idx = jnp.arange(width) // head
    return (idx[:, None] == idx[None, :]).astype(f32)


def _rope_tables(lp):
    half = ROPE_DIM // 2
    pos = (jnp.arange(lp) - PAD).astype(f32)
    inv_freq = jnp.power(jnp.float32(ROPE_THETA), -jnp.arange(half, dtype=f32) * (2.0 / ROPE_DIM))
    ang = pos[:, None] * inv_freq[None, :]
    cos, sin = jnp.cos(ang), jnp.sin(ang)
    ones = jnp.ones((lp, HEAD_DIM - ROPE_DIM), f32)
    zeros = jnp.zeros((lp, HEAD_DIM - ROPE_DIM), f32)
    cos_t = jnp.concatenate([cos, cos, ones], axis=1)
    sin_t = jnp.concatenate([-sin, sin, zeros], axis=1)
    i = jnp.arange(HEAD_DIM)
    src = jnp.where(i < half, i + half, jnp.where(i < ROPE_DIM, i - half, i))
    swap = ((i[:, None] == src[None, :]) & (i[None, :] < ROPE_DIM)).astype(f32)
    return cos_t, sin_t, swap


def _attn_prep(qkv, cos_t, sin_t, swap):
    outs = []
    for h in range(Q_HEADS + KV_HEADS):
        t = qkv[:, h * HEAD_DIM:(h + 1) * HEAD_DIM]
        outs.append(t * cos_t + _dot_const(t, swap) * sin_t)
    q = jnp.concatenate(outs[:Q_HEADS], axis=1)
    k = jnp.concatenate(outs[Q_HEADS:], axis=1)
    return q, k, qkv[:, Q_W + KV_W:]


def _attn_prep_transposed(dq, dk, dk_meta, dv, dv_meta, cos_t, sin_t, swap):
    parts = []
    for d, heads in ((dq.astype(f32), Q_HEADS), (dk.astype(f32) + dk_meta.astype(f32), KV_HEADS)):
        for h in range(heads):
            t = d[:, h * HEAD_DIM:(h + 1) * HEAD_DIM]
            parts.append(t * cos_t + _two_pass(t * sin_t, swap, "nt"))
    return (jnp.concatenate(parts + [dv.astype(f32) + dv_meta.astype(f32)], axis=1),)


def _softplus(z):
    return jnp.maximum(z, 0.0) + jnp.log1p(jnp.exp(-jnp.abs(z)))


def _rwkv_prep(rkv, lora, w0, w2, a0, a2, g2, k_k, k_a, hsum):
    r = rkv[:, :RWKV_DIM]
    k = rkv[:, RWKV_DIM:2 * RWKV_DIM]
    v = rkv[:, 2 * RWKV_DIM:]
    dw = lora[:, :DECAY_LORA]
    da = lora[:, DECAY_LORA:DECAY_LORA + AAA_LORA]
    dg = lora[:, DECAY_LORA + AAA_LORA:]
    w = -_softplus(-(w0 + _dot(jnp.tanh(dw), w2))) - 0.5
    a = jax.nn.sigmoid(a0 + _dot(da, a2))
    g = _dot(jax.nn.sigmoid(dg), g2)
    kk = k * k_k
    kk = kk * lax.rsqrt(jnp.maximum(_dot_const(kk * kk, hsum), 1e-24))
    k = k * (1.0 + (a - 1.0) * k_a)
    log_decay = -jnp.exp(w)
    return r, log_decay, k, v, -kk, kk * a, g


def _rwkv_post(y, r, k, v, g, ln_w, ln_b, r_k, hmean):
    hsum = hmean * RWKV_HEAD
    mean = _dot_const(y, hmean)
    yc = y - mean
    var = _dot_const(yc * yc, hmean)
    yn = yc * lax.rsqrt(var + RWKV_LN_EPS) * ln_w + ln_b
    bonus = _dot_const(r * k * r_k, hsum) * v
    return ((yn + bonus) * g,)


def _merge(gates, br_a, br_r):
    sg = jax.nn.sigmoid(gates)
    return (sg[:, :D_MODEL] * br_a + sg[:, D_MODEL:] * br_r,)


def _swiglu(gate, up):
    return (jax.nn.silu(gate) * up,)


def _ffn_in(f, w_gate_t, w_up_t, *, name):
    m, d = f.shape
    n = w_gate_t.shape[0]
    tm = _tile(m)

    def body(f_ref, wg_ref, wu_ref, g_ref, u_ref, a_ref):
        g = _dot(f_ref[...], wg_ref[...], "nt")
        u = _dot(f_ref[...], wu_ref[...], "nt")
        g_ref[...] = g.astype(g_ref.dtype)
        u_ref[...] = u.astype(u_ref.dtype)
        a_ref[...] = _swiglu(g, u)[0].astype(a_ref.dtype)

    spec = pl.BlockSpec((tm, n), lambda i: (i, 0))
    return pl.pallas_call(
        body, name=name, grid=(m // tm,),
        in_specs=[pl.BlockSpec((tm, d), lambda i: (i, 0)), _full(w_gate_t.shape), _full(w_up_t.shape)],
        out_specs=[spec] * 3, out_shape=[jax.ShapeDtypeStruct((m, n), bf16)] * 3,
        compiler_params=_params(("parallel",)),
    )(f, w_gate_t, w_up_t)


def _branch_merge(y_attn, y_rwkv, w_attn_t, w_rwkv_t, gates, *, name):
    m = y_attn.shape[0]
    tm = _tile(m, MM_ROWS)

    def body(ya_ref, yr_ref, wa_ref, wr_ref, g_ref, a_ref, r_ref, o_ref):
        br_a = _dot(ya_ref[...], wa_ref[...], "nt")
        br_r = _dot(yr_ref[...], wr_ref[...], "nt")
        a_ref[...] = br_a.astype(a_ref.dtype)
        r_ref[...] = br_r.astype(r_ref.dtype)
        o_ref[...] = _merge(g_ref[...].astype(f32), br_a, br_r)[0].astype(o_ref.dtype)

    rows = lambda a: pl.BlockSpec((tm, a.shape[1]), lambda i: (i, 0))
    spec = pl.BlockSpec((tm, D_MODEL), lambda i: (i, 0))
    return pl.pallas_call(
        body, name=name, grid=(m // tm,),
        in_specs=[rows(y_attn), rows(y_rwkv), _full(w_attn_t.shape), _full(w_rwkv_t.shape), rows(gates)],
        out_specs=[spec] * 3, out_shape=[jax.ShapeDtypeStruct((m, D_MODEL), bf16)] * 3,
        compiler_params=_params(("parallel",)),
    )(y_attn, y_rwkv, w_attn_t, w_rwkv_t, gates)


def _branch_merge_bwd(dh, w_o, gates, br_a, br_r, w_attn_t, w_rwkv_t, *, name):
    m = dh.shape[0]
    tm = _tile(m, MM_ROWS)

    def body(dh_ref, w_ref, g_ref, a_ref, r_ref, wa_ref, wr_ref, dg_ref, da_ref, dr_ref, dya_ref, dyr_ref):
        dmerged = _dot(dh_ref[...], w_ref[...], "nt")
        _, vjp = jax.vjp(lambda g, a, r: _merge(g, a, r)[0], g_ref[...].astype(f32), a_ref[...].astype(f32),
                         r_ref[...].astype(f32))
        dg, da, dr = vjp(dmerged)
        dg_ref[...] = dg.astype(dg_ref.dtype)
        da_ref[...] = da.astype(da_ref.dtype)
        dr_ref[...] = dr.astype(dr_ref.dtype)
        dya_ref[...] = _dot(da, wa_ref[...])
        dyr_ref[...] = _dot(dr, wr_ref[...])

    rows = lambda a: pl.BlockSpec((tm, a.shape[1]), lambda i: (i, 0))
    mixer = pl.BlockSpec((tm, w_attn_t.shape[1]), lambda i: (i, 0))
    return pl.pallas_call(
        body, name=name, grid=(m // tm,),
        in_specs=[rows(dh), _full(w_o.shape), rows(gates), rows(br_a), rows(br_r), _full(w_attn_t.shape),
                  _full(w_rwkv_t.shape)],
        out_specs=[rows(gates), rows(br_a), rows(br_r), mixer, mixer],
        out_shape=[jax.ShapeDtypeStruct(gates.shape, bf16), jax.ShapeDtypeStruct(br_a.shape, bf16),
                   jax.ShapeDtypeStruct(br_r.shape, bf16), jax.ShapeDtypeStruct((m, w_attn_t.shape[1]), f32),
                   jax.ShapeDtypeStruct((m, w_rwkv_t.shape[1]), f32)],
        compiler_params=_params(("parallel",)),
    )(dh, w_o, gates, br_a, br_r, w_attn_t, w_rwkv_t)


def _ffn_in_bwd(dh, w_down, gate, up, *, name):
    m, d = dh.shape
    n = w_down.shape[0]
    tm = _tile(m)

    def body(dh_ref, w_ref, g_ref, u_ref, dg_ref, du_ref):
        dact = _dot(dh_ref[...], w_ref[...], "nt")
        _, vjp = jax.vjp(lambda a, b: _swiglu(a, b)[0], g_ref[...].astype(f32), u_ref[...].astype(f32))
        dg, du = vjp(dact)
        dg_ref[...] = dg.astype(dg_ref.dtype)
        du_ref[...] = du.astype(du_ref.dtype)

    spec = pl.BlockSpec((tm, n), lambda i: (i, 0))
    return pl.pallas_call(
        body, name=name, grid=(m // tm,),
        in_specs=[pl.BlockSpec((tm, d), lambda i: (i, 0)), _full(w_down.shape), spec, spec],
        out_specs=[spec] * 2, out_shape=[jax.ShapeDtypeStruct((m, n), bf16)] * 2,
        compiler_params=_params(("parallel",)),
    )(dh, w_down, gate, up)


HALO = 16


def _previous_rows(x, before_ref, first_tile):
    rows = lax.broadcasted_iota(jnp.int32, x.shape, 0)
    last = jnp.where(first_tile, 0.0, before_ref[HALO - 1:HALO, :].astype(f32))
    return jnp.where(rows == 0, last, pltpu.roll(x, 1, axis=0))


def _mixer_inputs(ps, mixes, params, *, name):
    m = ps[0].shape[0]
    tm = _tile(m)
    sub = tm // HALO
    n_par = len(params)

    def body(*refs):
        first = pl.program_id(0) == 0
        pf = []
        for k in range(2):
            x = refs[k][...].astype(f32)
            pf.append(x + (_previous_rows(x, refs[2 + k], first) - x) * refs[4 + k][...])
        res = _rwkv_prep(*pf, *[ref[...] for ref in refs[6:6 + n_par]])
        for o_ref, val in zip(refs[6 + n_par:], res):
            o_ref[...] = val

    tile = lambda a: pl.BlockSpec((tm, a.shape[1]), lambda i: (i, 0))
    before = lambda a: pl.BlockSpec((HALO, a.shape[1]), lambda i: (jnp.maximum(i * sub - 1, 0), 0))
    out = pl.BlockSpec((tm, RWKV_DIM), lambda i: (i, 0))
    return pl.pallas_call(
        body, name=name, grid=(m // tm,),
        in_specs=[tile(a) for a in ps] + [before(a) for a in ps] + [_full(a.shape) for a in mixes + params],
        out_specs=[out] * 7, out_shape=[jax.ShapeDtypeStruct((m, RWKV_DIM), f32)] * 7,
        compiler_params=_params(("parallel",)),
    )(*ps, *ps, *mixes, *params)


def _mixer_inputs_bwd(ps, mixes, params, cts, *, name):
    m = ps[0].shape[0]
    tm = _tile(m)
    sub = tm // HALO
    nt = m // tm
    n_par = len(params)
    flat_cts = [c for group in cts for c in group]
    n_ct = len(flat_cts)

    def body(*refs):
        i = pl.program_id(0)
        tile_index = nt - 1 - i
        ct_refs = refs[6 + n_par:6 + n_par + n_ct]
        dp_refs = refs[6 + n_par + n_ct:8 + n_par + n_ct]
        dmix_refs = refs[8 + n_par + n_ct:10 + n_par + n_ct]
        dpar_refs = refs[10 + n_par + n_ct:9 + 2 * n_par + n_ct]
        carries = refs[9 + 2 * n_par + n_ct:]
        rows1 = tile_index * tm + lax.broadcasted_iota(jnp.int32, (tm, 1), 0)
        live = rows1 >= PAD

        @pl.when(i == 0)
        def _():
            for ref in (*dmix_refs, *dpar_refs, *carries):
                ref[...] = jnp.zeros_like(ref)

        xs, prevs, pf = [], [], []
        for k in range(2):
            x = refs[k][...].astype(f32)
            xp = _previous_rows(x, refs[2 + k], tile_index == 0)
            xs.append(x)
            prevs.append(xp)
            pf.append(x + (xp - x) * refs[4 + k][...])
        ct_vals, pos = [], 0
        for group in cts:
            acc = ct_refs[pos][...].astype(f32)
            for extra in range(1, len(group)):
                acc = acc + ct_refs[pos + extra][...].astype(f32)
            pos += len(group)
            ct_vals.append(jnp.where(live, acc, 0.0))
        par_vals = [ref[...] for ref in refs[6:6 + n_par]]
        _, vjp = jax.vjp(lambda *args: _rwkv_prep(*args, par_vals[-1]), *pf, *par_vals[:-1])
        g = vjp(tuple(ct_vals))
        for k in range(2):
            dpf = g[k]
            mixv = refs[4 + k][...]
            dm = dpf * mixv
            rows = lax.broadcasted_iota(jnp.int32, dm.shape, 0)
            dm_next = jnp.where(rows == tm - 1, carries[k][...], pltpu.roll(dm, tm - 1, axis=0))
            dp_refs[k][...] = jnp.where(live, dpf - dm + dm_next, 0.0).astype(dp_refs[k].dtype)
            carries[k][...] = dm[0:1, :]
            dmix_refs[k][...] += jnp.sum(dpf * (prevs[k] - xs[k]), axis=0, keepdims=True)
        for ref, val in zip(dpar_refs, g[2:]):
            ref[...] += val

    tile = lambda a: pl.BlockSpec((tm, a.shape[1]), lambda i: (nt - 1 - i, 0))
    before = lambda a: pl.BlockSpec((HALO, a.shape[1]), lambda i: (jnp.maximum((nt - 1 - i) * sub - 1, 0), 0))
    return pl.pallas_call(
        body, name=name, grid=(nt,),
        in_specs=[tile(a) for a in ps] + [before(a) for a in ps] + [_full(a.shape) for a in mixes + params]
        + [tile(c) for c in flat_cts],
        out_specs=[tile(a) for a in ps] + [_full(a.shape) for a in mixes + params[:-1]],
        out_shape=[jax.ShapeDtypeStruct(a.shape, bf16) for a in ps]
        + [jax.ShapeDtypeStruct(a.shape, f32) for a in mixes + params[:-1]],
        scratch_shapes=[pltpu.VMEM((1, a.shape[1]), f32) for a in ps],
        compiler_params=_params(("arbitrary",)),
    )(*ps, *ps, *mixes, *params, *flat_cts)


def _attn_masks(blk):
    qi = lax.broadcasted_iota(jnp.int32, (BLOCK, BLOCK), 0)
    ki = lax.broadcasted_iota(jnp.int32, (BLOCK, BLOCK), 1)
    qpos = blk * BLOCK + qi - PAD
    kpos_c = blk * BLOCK + ki - PAD
    kpos_p = kpos_c - BLOCK
    kpos_m = ki - PAD

    def band(kpos):
        return (kpos >= N_META) & (kpos <= qpos) & (qpos - kpos < WINDOW)

    return band(kpos_p), band(kpos_c), (kpos_m >= 0) & (kpos_m <= qpos)


def _attn_probs(qs, k3s, sink, oks):
    s = [[jnp.where(ok, _dot(qh, kx, "nt"), NEG_INF) for kx, ok in zip(k3, oks)] for qh, k3 in zip(qs, k3s)]
    mx = [jnp.maximum(jnp.maximum(jnp.max(t[0], -1, keepdims=True), jnp.max(t[1], -1, keepdims=True)),
                      jnp.maximum(jnp.max(t[2], -1, keepdims=True), sk)) for t, sk in zip(s, sink)]
    e = [[jnp.exp(tx - m) for tx in t] for t, m in zip(s, mx)]
    e_sink = [jnp.exp(sk - m) for sk, m in zip(sink, mx)]
    inv = [1.0 / (jnp.sum(t[0], -1, keepdims=True) + jnp.sum(t[1], -1, keepdims=True)
                  + jnp.sum(t[2], -1, keepdims=True) + es) for t, es in zip(e, e_sink)]
    return [[tx * i for tx in t] for t, i in zip(e, inv)], [es * i for es, i in zip(e_sink, inv)]


def _head_cols(i):
    return slice(i * HEAD_DIM, (i + 1) * HEAD_DIM)


def _attn_operands(refs):
    q_ref, kp_ref, kc_ref, km_ref, vp_ref, vc_ref, vm_ref, s_ref = refs
    qs = [q_ref[:, _head_cols(i)] * (HEAD_DIM ** -0.5) for i in range(Q_HEADS)]
    k3 = [[ref[:, _head_cols(h)] for ref in (kp_ref, kc_ref, km_ref)] for h in range(KV_HEADS)]
    v3 = [[ref[:, _head_cols(h)] for ref in (vp_ref, vc_ref, vm_ref)] for h in range(KV_HEADS)]
    return (qs, [k3[i // GROUP] for i in range(Q_HEADS)], [v3[i // GROUP] for i in range(Q_HEADS)],
            [s_ref[:, i:i + 1] for i in range(Q_HEADS)])


def _attention(q, k, v, sinks, *, name):
    lp = q.shape[0]
    nb = lp // BLOCK
    prev = lambda i: (jnp.maximum(i - 1, 0), 0)
    cur = lambda i: (i, 0)
    meta = lambda i: (0, 0)
    kv = lambda index: pl.BlockSpec((BLOCK, KV_W), index)

    def body(*refs):
        o_ref = refs[-1]
        qs, k3s, v3s, sink = _attn_operands(refs[:-1])
        p, _ = _attn_probs(qs, k3s, sink, _attn_masks(pl.program_id(0)))
        out = [_dot(ph[0], v3[0]) + _dot(ph[1], v3[1]) + _dot(ph[2], v3[2]) for ph, v3 in zip(p, v3s)]
        for i in range(Q_HEADS):
            o_ref[:, _head_cols(i)] = out[i].astype(o_ref.dtype)

    return pl.pallas_call(
        body, name=name, grid=(nb,),
        in_specs=[pl.BlockSpec((BLOCK, Q_W), cur), kv(prev), kv(cur), kv(meta), kv(prev), kv(cur), kv(meta),
                  _full((1, Q_HEADS))],
        out_specs=pl.BlockSpec((BLOCK, Q_W), cur),
        out_shape=jax.ShapeDtypeStruct((lp, Q_W), bf16),
        compiler_params=_params(("parallel",)),
    )(q, k, k, k, v, v, v, sinks)


def _attention_bwd(q, k, v, sinks, out, do, *, name):
    lp = q.shape[0]
    nb = lp // BLOCK
    cur = lambda n: (jnp.minimum(n, nb - 1), 0)
    prev = lambda n: (jnp.maximum(jnp.minimum(n, nb - 1) - 1, 0), 0)
    behind = lambda n: (jnp.maximum(n - 1, 0), 0)
    meta = lambda n: (0, 0)
    kv = lambda index: pl.BlockSpec((BLOCK, KV_W), index)
    scale = HEAD_DIM ** -0.5

    def body(*refs):
        ins, fwd_ref, do_ref = refs[:8], refs[8], refs[9]
        dq_ref, dk_ref, dv_ref, dkm_ref, dvm_ref, ds_ref, carry_k, carry_v = refs[10:]
        n = pl.program_id(0)

        @pl.when(n == 0)
        def _():
            for ref in (dkm_ref, dvm_ref, ds_ref, carry_k, carry_v):
                ref[...] = jnp.zeros_like(ref)

        @pl.when(n < nb)
        def _():
            qs, k3s, v3s, sink = _attn_operands(ins)
            do = [do_ref[:, _head_cols(i)] for i in range(Q_HEADS)]
            p, p_sink = _attn_probs(qs, k3s, sink, _attn_masks(n))
            delta = [jnp.sum(d * fwd_ref[:, _head_cols(i)].astype(f32), -1, keepdims=True) for i, d in enumerate(do)]
            dp = [[_dot(d, vx, "nt") for vx in v3] for d, v3 in zip(do, v3s)]
            ds = [[px * (dx - dl) for px, dx in zip(ph, dh)] for ph, dh, dl in zip(p, dp, delta)]
            dq = [_dot(dsh[0], k3[0]) + _dot(dsh[1], k3[1]) + _dot(dsh[2], k3[2]) for dsh, k3 in zip(ds, k3s)]
            for i in range(Q_HEADS):
                dq_ref[:, _head_cols(i)] = dq[i] * scale
                ds_ref[:, i:i + 1] -= jnp.sum(p_sink[i] * delta[i], axis=0, keepdims=True)
            for h in range(KV_HEADS):
                group = slice(h * GROUP, (h + 1) * GROUP)
                q_all = jnp.concatenate(qs[group], axis=0)
                do_all = jnp.concatenate(do[group], axis=0)
                dk3 = [_dot(jnp.concatenate([dsh[x] for dsh in ds[group]], axis=0), q_all, "tn") for x in range(3)]
                dv3 = [_dot(jnp.concatenate([ph[x] for ph in p[group]], axis=0), do_all, "tn") for x in range(3)]
                hs = _head_cols(h)
                for out_ref, carry, meta_ref, d3 in ((dk_ref, carry_k, dkm_ref, dk3),
                                                     (dv_ref, carry_v, dvm_ref, dv3)):
                    out_ref[:, hs] = carry[:, hs] + d3[0]
                    carry[:, hs] = d3[1]
                    meta_ref[:, hs] += d3[2]

        @pl.when(n == nb)
        def _():
            dk_ref[...] = carry_k[...]
            dv_ref[...] = carry_v[...]

    kv_shape = jax.ShapeDtypeStruct((lp, KV_W), f32)
    one_shape = jax.ShapeDtypeStruct((BLOCK, KV_W), f32)
    return pl.pallas_call(
        body, name=name, grid=(nb + 1,),
        in_specs=[pl.BlockSpec((BLOCK, Q_W), cur), kv(prev), kv(cur), kv(meta), kv(prev), kv(cur), kv(meta),
                  _full((1, Q_HEADS)), pl.BlockSpec((BLOCK, Q_W), cur), pl.BlockSpec((BLOCK, Q_W), cur)],
        out_specs=[pl.BlockSpec((BLOCK, Q_W), cur), kv(behind), kv(behind), kv(meta), kv(meta),
                   _full((1, Q_HEADS))],
        out_shape=[jax.ShapeDtypeStruct((lp, Q_W), f32), kv_shape, kv_shape, one_shape, one_shape,
                   jax.ShapeDtypeStruct((1, Q_HEADS), f32)],
        scratch_shapes=[pltpu.VMEM((BLOCK, KV_W), f32), pltpu.VMEM((BLOCK, KV_W), f32)],
        compiler_params=_params(("arbitrary",)),
    )(q, k, k, k, v, v, v, sinks, out, do)


@jax.custom_vjp
def _known_inverse(l, x):
    return x


def _known_inverse_fwd(l, x):
    return x, x


def _known_inverse_bwd(x, ct):
    return _dot(_dot(x, ct, "tn"), x, "nt"), jnp.zeros_like(x)


_known_inverse.defvjp(_known_inverse_fwd, _known_inverse_bwd)


@jax.custom_vjp
def _decayed(x, c):
    return (x * jnp.exp(c)).astype(bf16).astype(f32)


def _decayed_fwd(x, c):
    e = jnp.exp(c)
    out = (x * e).astype(bf16).astype(f32)
    return out, (e, out)


def _decayed_bwd(res, ct):
    e, out = res
    return ct * e, ct * out


_decayed.defvjp(_decayed_fwd, _decayed_bwd)


@jax.custom_vjp
def _pair(x, y):
    return _dot(x, y, "nt")


def _pair_fwd(x, y):
    return _dot(x, y, "nt"), (x, y)


def _pair_bwd(res, ct):
    x, y = res
    hi = ct.astype(bf16)
    lo = (ct - hi.astype(f32)).astype(bf16)
    return _dot(hi, y) + _dot(lo, y), _dot(hi, x, "tn") + _dot(lo, x, "tn")


_pair.defvjp(_pair_fwd, _pair_bwd)


def _scan_chunk(s0, r, lw, k, v, a, b, inv=None):
    t = r[0].shape[0]
    ii = lax.broadcasted_iota(jnp.int32, (t, t), 0)
    jj = lax.broadcasted_iota(jnp.int32, (t, t), 1)
    incl = jj <= ii
    strict = jj < ii
    tri = incl.astype(f32)
    eye = jnp.where(ii == jj, 1.0, 0.0)
    cl = [_const_dot(tri, x) for x in lw]
    mid = [c[t // 2 - 1:t // 2, :] for c in cl]
    s0 = [s * jnp.exp(m) for s, m in zip(s0, mid)]
    cl = [c - m for c, m in zip(cl, mid)]
    rt = [_decayed(x, c) for x, c in zip(r, cl)]
    at = [_decayed(x, c - l) for x, c, l in zip(a, cl, lw)]
    bt = [_decayed(x, -c) for x, c in zip(b, cl)]
    kt = [_decayed(x, -c) for x, c in zip(k, cl)]
    l_ab = [jnp.where(strict, _pair(x, y), 0.0) for x, y in zip(at, bt)]
    l_ak = [jnp.where(strict, _pair(x, y), 0.0) for x, y in zip(at, kt)]
    r_b = [jnp.where(incl, _pair(x, y), 0.0) for x, y in zip(rt, bt)]
    r_k = [jnp.where(incl, _pair(x, y), 0.0) for x, y in zip(rt, kt)]
    if inv is None:
        inv = [eye + x for x in l_ab]
        pw = l_ab
        for _ in range(int(math.log2(t)) - 1):
            pw = [_dot(x, x) for x in pw]
            inv = [x + _dot(x, y) for x, y in zip(inv, pw)]
    else:
        inv = [_known_inverse(x, y) for x, y in zip(l_ab, inv)]
    rhs = [_dot(x, s, "nt") + _dot(m, y) for x, s, m, y in zip(at, s0, l_ak, v)]
    u = [_dot(x, y) for x, y in zip(inv, rhs)]
    y_s = [_dot(x, s, "nt") for x, s in zip(rt, s0)]
    y = [ys + _dot(m, uu) + _dot(n, vv) for ys, m, uu, n, vv in zip(y_s, r_b, u, r_k, v)]
    grow = [s + _dot(uu, x, "tn") + _dot(vv, z, "tn") for s, uu, x, vv, z in zip(s0, u, bt, v, kt)]
    s1 = [g * jnp.exp(c[t - 1:t, :]) for g, c in zip(grow, cl)]
    return y, s1, inv


def _head_rows(h):
    return slice(h * RWKV_HEAD, (h + 1) * RWKV_HEAD)


def _per_head(ref):
    return [ref[:, _head_rows(h)] for h in range(RWKV_HEADS)]


def _scan(r, lw, k, v, a, b, *, name):
    lp = r.shape[0]
    nc = lp // CHUNK
    row = pl.BlockSpec((CHUNK, RWKV_DIM), lambda c: (c, 0))

    def body(r_ref, lw_ref, k_ref, v_ref, a_ref, b_ref, y_ref, s_ref, inv_ref, state):
        @pl.when(pl.program_id(0) == 0)
        def _():
            state[...] = jnp.zeros_like(state)

        s_ref[...] = state[...]
        s0 = [state[_head_rows(h), :] for h in range(RWKV_HEADS)]
        y, s1, inv = _scan_chunk(s0, *[_per_head(ref) for ref in (r_ref, lw_ref, k_ref, v_ref, a_ref, b_ref)])
        for h in range(RWKV_HEADS):
            y_ref[:, _head_rows(h)] = y[h]
            state[_head_rows(h), :] = s1[h]
            inv_ref[h * CHUNK:(h + 1) * CHUNK, :] = inv[h].astype(inv_ref.dtype)

    return pl.pallas_call(
        body, name=name, grid=(nc,), in_specs=[row] * 6,
        out_specs=[row, pl.BlockSpec((RWKV_DIM, RWKV_HEAD), lambda c: (c, 0)),
                   pl.BlockSpec((RWKV_HEADS * CHUNK, CHUNK), lambda c: (c, 0))],
        out_shape=[jax.ShapeDtypeStruct((lp, RWKV_DIM), f32), jax.ShapeDtypeStruct((nc * RWKV_DIM, RWKV_HEAD), f32),
                   jax.ShapeDtypeStruct((nc * RWKV_HEADS * CHUNK, CHUNK), bf16)],
        scratch_shapes=[pltpu.VMEM((RWKV_DIM, RWKV_HEAD), f32)],
        compiler_params=_params(("arbitrary",)),
    )(r, lw, k, v, a, b)


def _scan_bwd(r, lw, k, v, a, b, states, inverses, dy, *, name):
    lp = r.shape[0]
    nc = lp // CHUNK
    back = lambda c: (nc - 1 - c, 0)
    row = pl.BlockSpec((CHUNK, RWKV_DIM), back)

    def body(r_ref, lw_ref, k_ref, v_ref, a_ref, b_ref, s_ref, inv_ref, dy_ref,
             dr_ref, dlw_ref, dk_ref, dv_ref, da_ref, db_ref, dstate):
        @pl.when(pl.program_id(0) == 0)
        def _():
            dstate[...] = jnp.zeros_like(dstate)

        outs = (dr_ref, dlw_ref, dk_ref, dv_ref, da_ref, db_ref)
        s0 = [s_ref[_head_rows(h), :] for h in range(RWKV_HEADS)]
        inv = [inv_ref[h * CHUNK:(h + 1) * CHUNK, :].astype(f32) for h in range(RWKV_HEADS)]
        _, vjp = jax.vjp(lambda *args: _scan_chunk(*args, inv=inv)[:2], s0,
                         *[_per_head(ref) for ref in (r_ref, lw_ref, k_ref, v_ref, a_ref, b_ref)])
        g = vjp((_per_head(dy_ref), [dstate[_head_rows(h), :] for h in range(RWKV_HEADS)]))
        for h in range(RWKV_HEADS):
            dstate[_head_rows(h), :] = g[0][h]
            for o_ref, gv in zip(outs, g[1:]):
                o_ref[:, _head_rows(h)] = gv[h]

    shape = jax.ShapeDtypeStruct((lp, RWKV_DIM), f32)
    return pl.pallas_call(
        body, name=name, grid=(nc,),
        in_specs=[row] * 6 + [pl.BlockSpec((RWKV_DIM, RWKV_HEAD), back),
                              pl.BlockSpec((RWKV_HEADS * CHUNK, CHUNK), back), row],
        out_specs=[row] * 6, out_shape=[shape] * 6,
        scratch_shapes=[pltpu.VMEM((RWKV_DIM, RWKV_HEAD), f32)],
        compiler_params=_params(("arbitrary",)),
    )(r, lw, k, v, a, b, states, inverses, dy)


def _loss_head(h2, target, g_final, *, name):
    lp = h2.shape[0]
    per_tile = 3
    tm = per_tile * BLOCK
    last_block = (lp - FRONT) // BLOCK - 1

    def body(h_ref, t0_ref, t1_ref, t2_ref, g_ref, loss_ref, dh_ref, dg_ref):
        i = pl.program_id(0)
        target_rows = jnp.concatenate([t0_ref[...], t1_ref[...], t2_ref[...]], axis=0)
        real = i * tm + lax.broadcasted_iota(jnp.int32, (tm, 1), 0) >= FRONT

        def tile_loss(hv, gv):
            err = _rms(hv, gv) - target_rows
            return 0.5 * jnp.sum(jnp.where(real, jnp.mean(err * err, axis=-1, keepdims=True), 0.0))

        loss, (dh, dg) = jax.value_and_grad(tile_loss, argnums=(0, 1))(h_ref[...], g_ref[...])

        @pl.when(i == 0)
        def _():
            loss_ref[...] = jnp.zeros_like(loss_ref)
            dg_ref[...] = jnp.zeros_like(dg_ref)

        loss_ref[...] += jnp.full(loss_ref.shape, loss, f32)
        dg_ref[...] += dg
        dh_ref[...] = dh

    def target_block(j):
        return pl.BlockSpec((BLOCK, D_MODEL),
                            lambda i: (jnp.clip(per_tile * i + j - FRONT // BLOCK, 0, last_block), 0))

    return pl.pallas_call(
        body, name=name, grid=(lp // tm,),
        in_specs=[pl.BlockSpec((tm, D_MODEL), lambda i: (i, 0)), target_block(0), target_block(1), target_block(2),
                  _full(g_final.shape)],
        out_specs=[_full((8, 128)), pl.BlockSpec((tm, D_MODEL), lambda i: (i, 0)), _full(g_final.shape)],
        out_shape=[jax.ShapeDtypeStruct((8, 128), f32), jax.ShapeDtypeStruct((lp, D_MODEL), f32),
                   jax.ShapeDtypeStruct(g_final.shape, f32)],
        compiler_params=_params(("arbitrary",)),
    )(h2, target, target, target, g_final)


def _embed_norm(x, meta, g, *, name):
    seq = x.shape[0]
    lp = seq + FRONT
    per_tile = 3
    tm = per_tile * BLOCK
    last_block = seq // BLOCK - 1

    def body(x0_ref, x1_ref, x2_ref, meta_ref, g_ref, h_ref, u_ref):
        front = jnp.concatenate([jnp.zeros((PAD, D_MODEL), f32), meta_ref[...]], axis=0)
        first = jnp.where(pl.program_id(0) == 0, front, x0_ref[...])
        h = jnp.concatenate([first, x1_ref[...], x2_ref[...]], axis=0)
        h_ref[...] = h
        u_ref[...] = _rms(h, g_ref[...]).astype(u_ref.dtype)

    def x_block(j):
        return pl.BlockSpec((BLOCK, D_MODEL),
                            lambda i: (jnp.clip(per_tile * i + j - FRONT // BLOCK, 0, last_block), 0))

    tile = pl.BlockSpec((tm, D_MODEL), lambda i: (i, 0))
    return pl.pallas_call(
        body, name=name, grid=(lp // tm,),
        in_specs=[x_block(0), x_block(1), x_block(2), _full(meta.shape), _full(g.shape)],
        out_specs=[tile, tile],
        out_shape=[jax.ShapeDtypeStruct((lp, D_MODEL), f32), jax.ShapeDtypeStruct((lp, D_MODEL), bf16)],
        compiler_params=_params(("parallel",)),
    )(x, x, x, meta, g)


def _input_norm_bwd(h0, g, du, dh1, *, name):
    lp = h0.shape[0]
    tm = FRONT

    def body(h_ref, g_ref, du_ref, dh1_ref, dx_ref, front_ref, dg_ref):
        i = pl.program_id(0)
        _, vjp = jax.vjp(lambda hv, gv: (_rms(hv, gv), hv), h_ref[...], g_ref[...])
        dh, dg = vjp((du_ref[...].astype(f32), dh1_ref[...]))

        @pl.when(i == 0)
        def _():
            dg_ref[...] = jnp.zeros_like(dg_ref)
            front_ref[...] = dh

        dg_ref[...] += dg
        dx_ref[...] = dh

    tile = pl.BlockSpec((tm, D_MODEL), lambda i: (i, 0))
    return pl.pallas_call(
        body, name=name, grid=(lp // tm,),
        in_specs=[tile, _full(g.shape), tile, tile],
        out_specs=[pl.BlockSpec((tm, D_MODEL), lambda i: (jnp.maximum(i - 1, 0), 0)), _full((tm, D_MODEL)),
                   _full(g.shape)],
        out_shape=[jax.ShapeDtypeStruct((lp - tm, D_MODEL), f32), jax.ShapeDtypeStruct((tm, D_MODEL), f32),
                   jax.ShapeDtypeStruct(g.shape, f32)],
        compiler_params=_params(("arbitrary",)),
    )(h0, g, du, dh1)


def _local_step(x, target, meta, p, early_weights=None, late_weights=None, emit=None):
    emit = emit or (lambda group, grads: 0.0)
    seq = x.shape[0]
    lp = seq + FRONT
    cos_t, sin_t, swap = _rope_tables(lp)
    hsum = _head_sum_matrix(RWKV_DIM, RWKV_HEAD)
    hmean = hsum / RWKV_HEAD
    post_params = [p["ln_w"], p["ln_b"], p["r_k"], hmean]

    h0, u = _embed_norm(x, meta, p["norm_mix_g"], name="norm_mix")
    if early_weights is not None:
        p = {**p, **early_weights(u)}
    prep_params = [p["w0"], p["w2"], p["a0"], p["a2"], p["g2"], p["k_k"], p["k_a"], hsum]
    q, k, v, p_rkv, p_lora, gates = _proj_in(u, p["w_in_lr"], p["b_in"], [ATTN_PROJ, RKV_W, LORA_W, 2 * D_MODEL],
                                             (cos_t, sin_t, swap), name="proj_in", zero_rows_below=PAD)
    y_attn = _attention(q, k, v, p["sinks"], name="attention")

    mix_rkv, mix_lora = p["mix"][:, :RKV_W], p["mix"][:, RKV_W:]
    r_, lw_, k_, v_, a_, b_, g_ = _mixer_inputs([p_rkv, p_lora], [mix_rkv, mix_lora], prep_params,
                                                name="mixer_inputs")
    y_scan, states, inverses = _scan(r_, lw_, k_, v_, a_, b_, name="wkv_scan")
    (y_rwkv,) = _rowwise(_rwkv_post, [y_scan, r_, k_, v_, g_], post_params, [(RWKV_DIM, bf16)], name="rwkv_post")

    if late_weights is not None:
        p = {**p, **late_weights(y_rwkv)}
    br_a, br_r, merged = _branch_merge(y_attn, y_rwkv, p["w_br_attn_t"], p["w_br_rwkv_t"], gates, name="branch_merge")
    h1, f = _residual_norm(merged, p["w_o"], h0, p["norm_ffn_g"], name="out_proj")
    gate, up, act = _ffn_in(f, p["w_gate_t"], p["w_up_t"], name="ffn_in")
    h2 = _mm(act, p["w_down"], "nn", name="ffn_down", add=h1)

    loss8, dh2, d_final_g = _loss_head(h2, target, p["norm_final_g"], name="loss_head")
    dgate, dup = _ffn_in_bwd(dh2, p["w_down"], gate, up, name="ffn_in_bwd")
    d_w_down = _mm_tn(act, dh2, name="dw_down")
    d_w_gate_t = _mm_tn(dgate, f, name="dw_gate")
    d_w_up_t = _mm_tn(dup, f, name="dw_up")
    zero = emit("ffn", dict(w_down=d_w_down, w_gate_t=d_w_gate_t, w_up_t=d_w_up_t))
    dh1, d_ffn_g = _residual_norm_bwd([dgate, dup], [p["w_gate_t"], p["w_up_t"]], h1, p["norm_ffn_g"] + zero, dh2,
                                      name="norm_ffn_bwd")
    dgates, dbr_a, dbr_r, dy_attn, dy_rwkv = _branch_merge_bwd(
        dh1, p["w_o"], gates, br_a, br_r, p["w_br_attn_t"], p["w_br_rwkv_t"], name="branch_merge_bwd")
    d_w_o = _mm_tn(merged, dh1, name="dw_o")
    d_w_br_attn_t = _mm_tn(dbr_a, y_attn, name="dw_br_attn")
    d_w_br_rwkv_t = _mm_tn(dbr_r, y_rwkv, name="dw_br_rwkv")
    zero = emit("branch", dict(w_o=d_w_o, w_br_attn_t=d_w_br_attn_t, w_br_rwkv_t=d_w_br_rwkv_t))

    post_params = [p["ln_w"] + zero, p["ln_b"], p["r_k"], hmean]
    res = _rowwise_bwd(_rwkv_post, [y_scan, r_, k_, v_, g_], post_params, [[dy_rwkv]], name="rwkv_post_bwd",
                       diff_rows=[True] * 5, diff_params=[True, True, True, False])
    dy_scan, dr_p, dk_p, dv_p, dg_p, d_ln_w, d_ln_b, d_r_k = res
    dr_s, dlw_s, dk_s, dv_s, da_s, db_s = _scan_bwd(r_, lw_, k_, v_, a_, b_, states, inverses, dy_scan,
                                                    name="wkv_scan_bwd")
    res = _mixer_inputs_bwd([p_rkv, p_lora], [mix_rkv, mix_lora], prep_params,
                            [[dr_s, dr_p], [dlw_s], [dk_s, dk_p], [dv_s, dv_p], [da_s], [db_s], [dg_p]],
                            name="mixer_inputs_bwd")
    dp_rkv, dp_lora, d_mix_rkv, d_mix_lora, d_w0, d_w2, d_a0, d_a2, d_g2, d_k_k, d_k_a = res

    dq, dk, dv, dkm, dvm, d_sinks = _attention_bwd(q, k, v, p["sinks"], y_attn, dy_attn, name="attention_bwd")
    rest = jnp.zeros((lp - BLOCK, KV_W), f32)
    dkm, dvm = jnp.concatenate([dkm, rest], axis=0), jnp.concatenate([dvm, rest], axis=0)
    (dqkv,) = _rowwise(_attn_prep_transposed, [dq, dk, dkm, dv, dvm, cos_t, sin_t], [swap], [(ATTN_PROJ, bf16)],
                       name="attn_prep_bwd")

    d_w_qkv_t, db_qkv = _mm_tn(dqkv, u, name="dw_qkv", colsum=True)
    d_w_rkv_t, db_rkv = _mm_tn(dp_rkv, u, name="dw_rkv", colsum=True)
    d_w_lora_t, db_lora = _mm_tn(dp_lora, u, name="dw_lora", colsum=True)
    d_w_gates_t, db_gates = _mm_tn(dgates, u, name="dw_gates", colsum=True)
    d_w_in_t = jnp.concatenate([d_w_qkv_t, d_w_rkv_t, d_w_lora_t, d_w_gates_t], axis=0)
    zero = emit("input", dict(w_in_t=d_w_in_t, g2=d_g2, w2=d_w2, a2=d_a2))
    du = _proj_in_bwd([dqkv, dp_rkv, dp_lora, dgates], p["w_in_lr"], name="d_u")
    dx, d_front, d_mix_g = _input_norm_bwd(h0, p["norm_mix_g"] + zero, du, dh1, name="norm_mix_bwd")

    grads = dict(
        w_in_t=d_w_in_t,
        b_in=jnp.concatenate([db_qkv, db_rkv, db_lora, db_gates], axis=1),
        mix=jnp.concatenate([d_mix_rkv, d_mix_lora], axis=1),
        norm_mix_g=d_mix_g, sinks=d_sinks, w0=d_w0, w2=d_w2, a0=d_a0, a2=d_a2, g2=d_g2, k_k=d_k_k, k_a=d_k_a,
        r_k=d_r_k, ln_w=d_ln_w, ln_b=d_ln_b, w_br_attn_t=d_w_br_attn_t, w_br_rwkv_t=d_w_br_rwkv_t, w_o=d_w_o,
        norm_ffn_g=d_ffn_g, w_gate_t=d_w_gate_t, w_up_t=d_w_up_t, w_down=d_w_down, norm_final_g=d_final_g,
        meta=d_front[PAD:],
    )
    return loss8[0, 0], dx, grads


def _position():
    return lax.axis_index("x"), lax.axis_index("y"), lax.axis_index("c")


def _other_chips(x, y):
    return [(1 - x, y), (x, 1 - y), (1 - x, 1 - y)]


_HBM = pl.BlockSpec(memory_space=pltpu.HBM)
_SEM = pl.BlockSpec(memory_space=pltpu.SEMAPHORE)
_EFFECT = pltpu.SideEffectType.DATAFLOW_SIDE_EFFECTING


def _landing_zone(src, kind):
    shape = {"whole": (N_CHIPS,) + src.shape, "half": (2, N_CHIPS, src.shape[0], src.shape[1] // 2),
             "slab": (3,) + src.shape[1:], "sibling": src.shape}[kind]
    return lax.empty(shape, src.dtype)


def _copies_per_source(kind):
    return 1 if kind == "sibling" else 3


def _chip_copies(src_refs, land_refs, send_sems, recv_sems, kind):
    x, y, c = _position()
    if kind == "sibling":
        return [pltpu.make_async_remote_copy(
            src_ref=src, dst_ref=land, send_sem=send_sems.at[a], recv_sem=recv_sems.at[a],
            device_id=(x, y, 1 - c), device_id_type=MESH) for a, (src, land) in enumerate(zip(src_refs, land_refs))]
    copies = []
    for a, (src, land) in enumerate(zip(src_refs, land_refs)):
        for j, (px, py) in enumerate(_other_chips(x, y)):
            if kind == "whole":
                src_ref, dst_ref = src, land.at[2 * x + y]
            elif kind == "half":
                half = src.shape[1] // 2
                src_ref, dst_ref = src.at[:, pl.ds(pl.multiple_of(c * half, half), half)], land.at[c, 2 * x + y]
            else:
                src_ref, dst_ref = src.at[2 * px + py], land.at[j]
            copies.append(pltpu.make_async_remote_copy(
                src_ref=src_ref, dst_ref=dst_ref, send_sem=send_sems.at[3 * a + j], recv_sem=recv_sems.at[3 * a + j],
                device_id=(px, py, c), device_id_type=MESH))
    return copies


def _exchange_start(srcs, *, kind, name):
    n = len(srcs)
    lands = [_landing_zone(s, kind) for s in srcs]

    def body(*refs):
        for cp in _chip_copies(refs[:n], refs[n:2 * n], refs[2 * n], refs[2 * n + 1], kind):
            cp.start()
        refs[-1][...] = jnp.zeros_like(refs[-1])

    res = pl.pallas_call(
        body, name=name,
        out_shape=(pltpu.SemaphoreType.DMA((_copies_per_source(kind) * n,)),
                   pltpu.SemaphoreType.DMA((_copies_per_source(kind) * n,)),
                   *[pltpu.HBM(a.shape, a.dtype) for a in srcs + lands], jax.ShapeDtypeStruct((8, 128), f32)),
        in_specs=[_HBM] * (2 * n),
        out_specs=(_SEM, _SEM, *[_HBM] * (2 * n), pl.BlockSpec(memory_space=pltpu.VMEM)),
        input_output_aliases={i: 2 + i for i in range(2 * n)},
        compiler_params=pltpu.CompilerParams(has_side_effects=_EFFECT),
    )(*[pltpu.with_memory_space_constraint(a, pltpu.HBM) for a in srcs + lands])
    return res[0], res[1], list(res[2:2 + n]), list(res[2 + n:2 + 2 * n]), res[-1]


def _exchange_wait(handle, after, *, kind, name):
    send_sems, recv_sems, srcs, lands, _ = handle
    n = len(srcs)

    def body(*refs):
        for cp in _chip_copies(refs[:n], refs[n:2 * n], refs[2 * n], refs[2 * n + 1], kind):
            cp.wait_send()
            cp.wait_recv()

    res = pl.pallas_call(
        body, name=name,
        out_shape=tuple(pltpu.HBM(a.shape, a.dtype) for a in srcs + lands),
        in_specs=[_HBM] * (2 * n) + [_SEM, _SEM, pl.BlockSpec(memory_space=pl.ANY)],
        out_specs=tuple([_HBM] * (2 * n)),
        input_output_aliases={i: i for i in range(2 * n)},
        compiler_params=pltpu.CompilerParams(has_side_effects=_EFFECT),
    )(*srcs, *lands, send_sems, recv_sems, after)
    return list(res[:n]), list(res[n:])


def _sum_own_and_received(g, recv, *, name):
    _, r, w = g.shape
    tm = _tile(r)
    if g.dtype == bf16 and tm % 16:
        tm = r
    x, y, _ = _position()
    me = jnp.reshape(2 * x + y, (1,)).astype(jnp.int32)

    def body(me_ref, g_ref, r_ref, o_ref):
        o_ref[...] = (g_ref[0].astype(f32) + r_ref[0].astype(f32)) + (r_ref[1].astype(f32) + r_ref[2].astype(f32))

    return pl.pallas_call(
        body, name=name,
        grid_spec=pltpu.PrefetchScalarGridSpec(
            num_scalar_prefetch=1, grid=(r // tm,),
            in_specs=[pl.BlockSpec((1, tm, w), lambda i, me_ref: (me_ref[0], i, 0)),
                      pl.BlockSpec((3, tm, w), lambda i, me_ref: (0, i, 0))],
            out_specs=pl.BlockSpec((tm, w), lambda i, me_ref: (i, 0))),
        out_shape=jax.ShapeDtypeStruct((r, w), f32),
        compiler_params=_params(("parallel",)),
    )(me, g, recv)


def _swap_cores(arrs, *, name):
    n = len(arrs)

    def body(*refs):
        x, y, c = _position()
        copies = [pltpu.make_async_remote_copy(
            src_ref=refs[i], dst_ref=refs[n + i], send_sem=refs[2 * n].at[i], recv_sem=refs[2 * n + 1].at[i],
            device_id=(x, y, 1 - c), device_id_type=MESH) for i in range(n)]
        for cp in copies:
            cp.start()
        for cp in copies:
            cp.wait_recv()
        for cp in copies:
            cp.wait_send()

    return pl.pallas_call(
        body, name=name,
        in_specs=[pl.BlockSpec(memory_space=pl.ANY)] * n,
        out_specs=[pl.BlockSpec(memory_space=pl.ANY)] * n,
        out_shape=[jax.ShapeDtypeStruct(a.shape, a.dtype) for a in arrs],
        scratch_shapes=[pltpu.SemaphoreType.DMA((n,)), pltpu.SemaphoreType.DMA((n,))],
    )(*arrs)


def _swap_halves(zone, *, name):
    def body(z_ref, o_ref, send_sems, recv_sems):
        x, y, c = _position()
        mine = [pltpu.make_async_remote_copy(
            src_ref=o_ref.at[c, 2 * px + py], dst_ref=o_ref.at[c, 2 * px + py], send_sem=send_sems.at[j],
            recv_sem=recv_sems.at[j], device_id=(x, y, 1 - c), device_id_type=MESH)
            for j, (px, py) in enumerate(_other_chips(x, y))]
        for cp in mine:
            cp.start()
        for j, (px, py) in enumerate(_other_chips(x, y)):
            pltpu.make_async_remote_copy(
                src_ref=o_ref.at[c, 2 * px + py], dst_ref=o_ref.at[1 - c, 2 * px + py], send_sem=send_sems.at[j],
                recv_sem=recv_sems.at[j], device_id=(x, y, 1 - c), device_id_type=MESH).wait_recv()
        for cp in mine:
            cp.wait_send()

    return pl.pallas_call(
        body, name=name,
        in_specs=[pl.BlockSpec(memory_space=pl.ANY)], out_specs=pl.BlockSpec(memory_space=pl.ANY),
        out_shape=jax.ShapeDtypeStruct(zone.shape, zone.dtype), input_output_aliases={0: 0},
        scratch_shapes=[pltpu.SemaphoreType.DMA((3,)), pltpu.SemaphoreType.DMA((3,))],
    )(zone)


def _all_reduce_small(a, after, *, name):
    rows, w = a.shape

    def body(a_ref, after_ref, o_ref, buf, send_sems, recv_sems):
        x, y, c = _position()
        me = 4 * x + 2 * y + c
        buf[0] = a_ref[...]
        sends = []
        for rel in range(1, N_DEV):
            peer = ((1 - x) if rel & 4 else x, (1 - y) if rel & 2 else y, (1 - c) if rel & 1 else c)
            cp = pltpu.make_async_remote_copy(
                src_ref=a_ref, dst_ref=buf.at[rel], send_sem=send_sems.at[rel - 1], recv_sem=recv_sems.at[rel - 1],
                device_id=peer, device_id_type=MESH)
            cp.start()
            sends.append(cp)
        for cp in sends:
            cp.wait_recv()
        for cp in sends:
            cp.wait_send()
        acc = buf[jnp.bitwise_xor(me, 0)]
        for d in range(1, N_DEV):
            acc = acc + buf[jnp.bitwise_xor(me, d)]
        o_ref[...] = acc

    return pl.pallas_call(
        body, name=name,
        in_specs=[pl.BlockSpec(memory_space=pltpu.VMEM), pl.BlockSpec(memory_space=pl.ANY)],
        out_specs=pl.BlockSpec(memory_space=pltpu.VMEM),
        out_shape=jax.ShapeDtypeStruct((rows, w), f32),
        scratch_shapes=[pltpu.VMEM((N_DEV, rows, w), f32), pltpu.SemaphoreType.DMA((N_DEV - 1,)),
                        pltpu.SemaphoreType.DMA((N_DEV - 1,))],
    )(a, after)


def _adamw(w, g_parts, m, v, *, name, transposed=False):
    rows, cols = w.shape
    if transposed:
        tm = 256 if rows % 256 == 0 else rows
        g_spec = pl.BlockSpec((cols, tm), lambda i: (0, i))
    else:
        tm = _tile(rows, 256)
        g_spec = pl.BlockSpec((tm, cols), lambda i: (i, 0))
    n = len(g_parts)

    def body(*refs):
        w_ref, m_ref, v_ref = refs[0], refs[1 + n], refs[2 + n]
        g_ref, d_ref, nm_ref, nv_ref = refs[3 + n:]
        gv = refs[1][...]
        for part in refs[2:1 + n]:
            gv = gv + part[...]
        if transposed:
            gv = gv.T
        g_ref[...] = gv
        nm = ADAM_B1 * m_ref[...] + (1.0 - ADAM_B1) * gv
        nv = ADAM_B2 * v_ref[...] + (1.0 - ADAM_B2) * (gv * gv)
        m_hat = nm / (1.0 - ADAM_B1 ** ADAM_STEP)
        v_hat = nv / (1.0 - ADAM_B2 ** ADAM_STEP)
        d_ref[...] = -ADAM_LR * (m_hat / (jnp.sqrt(v_hat) + ADAM_EPS) + ADAM_WD * w_ref[...])
        nm_ref[...] = nm
        nv_ref[...] = nv

    spec = pl.BlockSpec((tm, cols), lambda i: (i, 0))
    shape = jax.ShapeDtypeStruct((rows, cols), f32)
    return pl.pallas_call(
        body, name=name, grid=(rows // tm,), in_specs=[spec] + [g_spec] * n + [spec] * 2,
        out_specs=[spec] * 4, out_shape=[shape] * 4,
        compiler_params=_params(("parallel",)),
    )(w, *g_parts, m, v)


def _pad_rows(a, rows):
    return jnp.concatenate([a, jnp.zeros((rows - a.shape[0], a.shape[1]), a.dtype)], axis=0) if rows > a.shape[0] else a


_SMALL = (("norm_mix_g", D_MODEL), ("b_in", D_IN), ("sinks", Q_HEADS), ("mix", RWKV_PROJ), ("w0", RWKV_DIM),
          ("a0", RWKV_DIM), ("k_k", RWKV_DIM), ("k_a", RWKV_DIM), ("r_k", RWKV_DIM), ("ln_w", RWKV_DIM),
          ("ln_b", RWKV_DIM), ("norm_ffn_g", D_MODEL), ("norm_final_g", D_MODEL))


def _pack_small(d):
    flat = jnp.concatenate([d[n].reshape(-1).astype(f32) for n, _ in _SMALL])
    return flat


def _unpack_small(flat):
    out, off = {}, 0
    for n, size in _SMALL:
        out[n] = flat[off:off + size]
        off += size
    return out


_SMALL_TOTAL = sum(s for _, s in _SMALL)


def kernel(x, meta_tokens, norm_mix_g, w_in, b_in, attn_sinks, rwkv_mix, rwkv_w0, rwkv_w2, rwkv_a0, rwkv_a2, rwkv_g2, rwkv_k_k, rwkv_k_a, rwkv_r_k, rwkv_ln_w, rwkv_ln_b, w_br_attn, w_br_rwkv, w_o, norm_ffn_g, w_ffn_gate, w_ffn_up, w_ffn_down, norm_final_g, loss_target, m_meta_tokens, m_norm_mix_g, m_w_in, m_b_in, m_attn_sinks, m_rwkv_mix, m_rwkv_w0, m_rwkv_w2, m_rwkv_a0, m_rwkv_a2, m_rwkv_g2, m_rwkv_k_k, m_rwkv_k_a, m_rwkv_r_k, m_rwkv_ln_w, m_rwkv_ln_b, m_w_br_attn, m_w_br_rwkv, m_w_o, m_norm_ffn_g, m_w_ffn_gate, m_w_ffn_up, m_w_ffn_down, m_norm_final_g, v_meta_tokens, v_norm_mix_g, v_w_in, v_b_in, v_attn_sinks, v_rwkv_mix, v_rwkv_w0, v_rwkv_w2, v_rwkv_a0, v_rwkv_a2, v_rwkv_g2, v_rwkv_k_k, v_rwkv_k_a, v_rwkv_r_k, v_rwkv_ln_w, v_rwkv_ln_b, v_w_br_attn, v_w_br_rwkv, v_w_o, v_norm_ffn_g, v_w_ffn_gate, v_w_ffn_up, v_w_ffn_down, v_norm_final_g):
    names = ("meta_tokens", "norm_mix_g", "w_in", "b_in", "attn_sinks", "rwkv_mix", "rwkv_w0", "rwkv_w2", "rwkv_a0",
             "rwkv_a2", "rwkv_g2", "rwkv_k_k", "rwkv_k_a", "rwkv_r_k", "rwkv_ln_w", "rwkv_ln_b", "w_br_attn",
             "w_br_rwkv", "w_o", "norm_ffn_g", "w_ffn_gate", "w_ffn_up", "w_ffn_down", "norm_final_g")
    w_all = dict(zip(names, (meta_tokens, norm_mix_g, w_in, b_in, attn_sinks, rwkv_mix, rwkv_w0, rwkv_w2, rwkv_a0,
                             rwkv_a2, rwkv_g2, rwkv_k_k, rwkv_k_a, rwkv_r_k, rwkv_ln_w, rwkv_ln_b, w_br_attn,
                             w_br_rwkv, w_o, norm_ffn_g, w_ffn_gate, w_ffn_up, w_ffn_down, norm_final_g)))
    m_all = dict(zip(names, (m_meta_tokens, m_norm_mix_g, m_w_in, m_b_in, m_attn_sinks, m_rwkv_mix, m_rwkv_w0,
                             m_rwkv_w2, m_rwkv_a0, m_rwkv_a2, m_rwkv_g2, m_rwkv_k_k, m_rwkv_k_a, m_rwkv_r_k,
                             m_rwkv_ln_w, m_rwkv_ln_b, m_w_br_attn, m_w_br_rwkv, m_w_o, m_norm_ffn_g, m_w_ffn_gate,
                             m_w_ffn_up, m_w_ffn_down, m_norm_final_g)))
    v_all = dict(zip(names, (v_meta_tokens, v_norm_mix_g, v_w_in, v_b_in, v_attn_sinks, v_rwkv_mix, v_rwkv_w0,
                             v_rwkv_w2, v_rwkv_a0, v_rwkv_a2, v_rwkv_g2, v_rwkv_k_k, v_rwkv_k_a, v_rwkv_r_k,
                             v_rwkv_ln_w, v_rwkv_ln_b, v_w_br_attn, v_w_br_rwkv, v_w_o, v_norm_ffn_g, v_w_ffn_gate,
                             v_w_ffn_up, v_w_ffn_down, v_norm_final_g)))
    cx, cy, _ = _position()
    chip = 2 * cx + cy

    t_of = dict(w_in_t="w_in", w_gate_t="w_ffn_gate", w_up_t="w_ffn_up", w_br_attn_t="w_br_attn",
                w_br_rwkv_t="w_br_rwkv", g2_t="rwkv_g2", w2_t="rwkv_w2", a2_t="rwkv_a2")
    plain_of = dict(w_down="w_ffn_down", w_o="w_o")
    meta_cols = meta_tokens.shape[1]

    def shard(k):
        return (w_all[t_of[k]][0].T if k in t_of else w_all[plain_of[k]][0]).astype(bf16)

    def whole(zone, own):
        return lax.dynamic_update_slice_in_dim(zone, own[None], chip, axis=0).reshape(-1, own.shape[-1])

    tiny = ("g2_t", "w2_t", "a2_t")
    late = ("w_gate_t", "w_up_t", "w_down", "w_o", "w_br_attn_t", "w_br_rwkv_t")
    w_in_own = shard("w_in_t")
    w_in_rows, w_in_cols = w_in_own.shape
    tiny_h = _exchange_start([shard(k) for k in tiny] + [meta_tokens], kind="whole", name="gather_tiny_start")
    w_in_h = _exchange_start([w_in_own + tiny_h[4][0, 0].astype(bf16)], kind="half", name="gather_w_in_start")
    behind = w_in_h[4][0, 0].astype(bf16)
    late_h = _exchange_start([shard(k) + behind for k in late], kind="whole", name="gather_late_start")
    own, zones = _exchange_wait(tiny_h, late_h[4], kind="whole", name="gather_tiny_wait")
    got = {k: whole(z, o) for k, z, o in zip(tiny, zones, own)}
    meta_full = whole(zones[-1], own[-1]).reshape(N_CHIPS, N_META, meta_cols).transpose(1, 0, 2).reshape(N_META, -1)
    p = dict(
        g2=got["g2_t"].T.astype(f32), w2=got["w2_t"].T.astype(f32), a2=got["a2_t"].T.astype(f32),
        b_in=b_in, sinks=attn_sinks, mix=rwkv_mix, w0=rwkv_w0, a0=rwkv_a0, k_k=rwkv_k_k, k_a=rwkv_k_a,
        r_k=rwkv_r_k.reshape(1, RWKV_DIM), ln_w=rwkv_ln_w, ln_b=rwkv_ln_b, norm_mix_g=norm_mix_g,
        norm_ffn_g=norm_ffn_g, norm_final_g=norm_final_g.reshape(1, D_MODEL),
    )

    def early_weights(after):
        own_h, zones_h = _exchange_wait(w_in_h, after, kind="half", name="gather_w_in_wait")
        zone = _swap_halves(zones_h[0], name="swap_w_in_halves")
        own_halves = own_h[0].reshape(w_in_rows, 2, w_in_cols // 2).transpose(1, 0, 2)[:, None]
        zone = lax.dynamic_update_slice(zone, own_halves, (0, chip, 0, 0))
        return dict(w_in_lr=zone.reshape(2, N_CHIPS * w_in_rows, w_in_cols // 2))

    def late_weights(after):
        own_l, zones_l = _exchange_wait(late_h, after, kind="whole", name="gather_late_wait")
        return {k: whole(z, o) for k, z, o in zip(late, zones_l, own_l)}

    started = {}

    def partial_sums(groups, after):
        parts = {}
        for group in groups:
            keys, handle = started[group]
            slabs, lands = _exchange_wait(handle, after, kind="slab", name="scatter_" + group + "_wait")
            parts.update({k: _sum_own_and_received(s, l, name="sum_chips_" + k) for k, s, l in zip(keys, slabs, lands)})
        return parts

    def emit(group, grads_):
        keys = list(grads_)
        slabs = []
        for k in keys:
            a = grads_[k].T if k in ("g2", "w2", "a2") else grads_[k]
            slabs.append(a.reshape(N_CHIPS, a.shape[0] // N_CHIPS, a.shape[1]))
        started[group] = (keys, _exchange_start(slabs, kind="slab", name="scatter_" + group + "_start"))
        zero = started[group][1][4]
        if group == "input":
            started["parts_a"] = partial_sums(("ffn", "branch"), zero)
            started["swap_a"] = _exchange_start(list(started["parts_a"].values()), kind="sibling",
                                                name="swap_cores_a_start")
            zero = started["swap_a"][4]
        return zero[0, 0]

    loss, dx, g = _local_step(x[0], loss_target[0], meta_full, p, early_weights, late_weights, emit)

    grads, delta, new_m, new_v = {}, {}, {}, {}
    in_grad_layout = ("w_in_t", "w_gate_t", "w_up_t")
    weight_of = {**t_of, **plain_of}

    def update(keys, mine, theirs):
        for k, part, other in zip(keys, mine, theirs):
            both = [part, other]
            k = k + "_t" if k in ("g2", "w2", "a2") else k
            n = weight_of[k]
            shape2 = w_all[n].shape[1:]
            w_, m_, v_ = (a.reshape(shape2) for a in (w_all[n], m_all[n], v_all[n]))
            if k in in_grad_layout:
                res = [t.T for t in _adamw(w_.T, both, m_.T, v_.T, name="adamw_" + n)]
            else:
                res = _adamw(w_, both, m_, v_, name="adamw_" + n, transposed=k in t_of)
            grads[n], delta[n], new_m[n], new_v[n] = (t.reshape(w_all[n].shape) for t in res)
        return delta[n]

    done = update(list(started["parts_a"]),
                  *_exchange_wait(started["swap_a"], dx, kind="sibling", name="swap_cores_a_wait"))
    small = jnp.concatenate([_pack_small(g), loss.reshape(1)])
    small_rows = -(-small.shape[0] // PACK_W)
    small = jnp.concatenate([small, jnp.zeros((small_rows * PACK_W - small.shape[0],), f32)]).reshape(small_rows, PACK_W)
    small_rows8 = -(-(small_rows + N_META) // 8) * 8
    reduced = _all_reduce_small(_pad_rows(jnp.concatenate([g["meta"], small], axis=0), small_rows8), done,
                                name="reduce_small")
    parts_b = partial_sums(("input",), reduced)
    update(list(parts_b), list(parts_b.values()), _swap_cores(list(parts_b.values()), name="swap_cores_b"))
    g_meta = lax.dynamic_slice_in_dim(reduced[:N_META], chip * meta_cols, meta_cols, axis=1)
    flat = reduced[N_META:N_META + small_rows].reshape(-1)
    g_small = _unpack_small(flat)
    loss_total = flat[_SMALL_TOTAL]

    small_of = dict(norm_mix_g="norm_mix_g", b_in="b_in", attn_sinks="sinks", rwkv_mix="mix", rwkv_w0="w0",
                    rwkv_a0="a0", rwkv_k_k="k_k", rwkv_k_a="k_a", rwkv_r_k="r_k", rwkv_ln_w="ln_w",
                    rwkv_ln_b="ln_b", norm_ffn_g="norm_ffn_g", norm_final_g="norm_final_g")
    grads["meta_tokens"] = g_meta
    for n, k in small_of.items():
        grads[n] = g_small[k].reshape(w_all[n].shape)

    rest = [n for n in names if n not in delta]

    def pack_rest(src):
        flat_ = jnp.concatenate([src[n].reshape(-1) for n in rest])
        rows_ = -(-flat_.shape[0] // (8 * PACK_W)) * 8
        return jnp.concatenate([flat_, jnp.ones((rows_ * PACK_W - flat_.shape[0],), f32)]).reshape(rows_, PACK_W)

    _, d_, m_, v_ = _adamw(pack_rest(w_all), [pack_rest(grads)], pack_rest(m_all), pack_rest(v_all),
                           name="adamw_small")
    off = 0
    for n in rest:
        size = w_all[n].size
        for dst, src in ((delta, d_), (new_m, m_), (new_v, v_)):
            dst[n] = src.reshape(-1)[off:off + size].reshape(w_all[n].shape)
        off += size

    return (loss_total, dx.reshape(x.shape), *[grads[n] for n in names], *[delta[n] for n in names],
            *[new_m[n] for n in names], *[new_v[n] for n in names])
```

```python
import math

import jax
import jax.numpy as jnp
import numpy as np
from jax import lax
from jax.experimental import pallas as pl
from jax.experimental.pallas import tpu as pltpu

f32 = jnp.float32
bf16 = jnp.bfloat16

D_MODEL = 1024
N_META = 16
HEAD_DIM = 64
Q_HEADS = 8
KV_HEADS = 2
GROUP = Q_HEADS // KV_HEADS
WINDOW = 128
BLOCK = 128
ROPE_THETA = 500000.0
ROPE_DIM = HEAD_DIM // 4
RWKV_HEADS = 8
RWKV_HEAD = 64
RWKV_DIM = RWKV_HEADS * RWKV_HEAD
DECAY_LORA = 64
AAA_LORA = 64
GATE_LORA = 160
LORA_W = DECAY_LORA + AAA_LORA + GATE_LORA
RWKV_LN_EPS = 64e-5
D_FF = 2816
Q_W = Q_HEADS * HEAD_DIM
KV_W = KV_HEADS * HEAD_DIM
ATTN_PROJ = Q_W + 2 * KV_W
RKV_W = 3 * RWKV_DIM
RWKV_PROJ = RKV_W + LORA_W
D_IN = ATTN_PROJ + RWKV_PROJ + 2 * D_MODEL
RMS_EPS = 1e-6
NEG_INF = -1e30
PAD = BLOCK - N_META
FRONT = PAD + N_META

ADAM_LR = 0.001
ADAM_B1 = 0.9
ADAM_B2 = 0.999
ADAM_EPS = 1e-08
ADAM_WD = 0.01
ADAM_STEP = 10

N_CHIPS = 4
N_DEV = 8
CHUNK = 128
VMEM_LIMIT = 56 * 1024 * 1024
MM_ROWS = 704
PACK_W = 1024
MESH = pl.DeviceIdType.MESH


def _tile(m, pref=384):
    for step in (16, 8):
        for t in range(min(m, pref) // step * step, 0, -step):
            if m % t == 0:
                return t
    return m


def _params(sem=None):
    return pltpu.CompilerParams(dimension_semantics=sem, vmem_limit_bytes=VMEM_LIMIT)


def _full(shape):
    nd = len(shape)
    return pl.BlockSpec(shape, lambda *_: (0,) * nd)


def _dot(a, b, dims="nn"):
    dn = {"nn": (((1,), (0,)), ((), ())), "nt": (((1,), (1,)), ((), ())), "tn": (((0,), (0,)), ((), ()))}[dims]
    return lax.dot_general(a.astype(bf16), b.astype(bf16), dn, preferred_element_type=f32)


def _two_pass(x, m, dims="nn"):
    x_hi = x.astype(bf16)
    x_lo = (x - x_hi.astype(f32)).astype(bf16)
    return _dot(x_hi, m, dims) + _dot(x_lo, m, dims)


@jax.custom_vjp
def _dot_const(x, m):
    return _two_pass(x, m)


def _dot_const_fwd(x, m):
    return _two_pass(x, m), m


def _dot_const_bwd(m, ct):
    return _two_pass(ct, m, "nt"), jnp.zeros_like(m)


_dot_const.defvjp(_dot_const_fwd, _dot_const_bwd)


def _two_pass_left(m, x, dims):
    x_hi = x.astype(bf16)
    x_lo = (x - x_hi.astype(f32)).astype(bf16)
    return _dot(m, x_hi, dims) + _dot(m, x_lo, dims)


@jax.custom_vjp
def _const_dot(m, x):
    return _two_pass_left(m, x, "nn")


def _const_dot_fwd(m, x):
    return _two_pass_left(m, x, "nn"), m


def _const_dot_bwd(m, ct):
    return jnp.zeros_like(m), _two_pass_left(m, ct, "tn")


_const_dot.defvjp(_const_dot_fwd, _const_dot_bwd)


def _mm(a, b, mode, *, name, out_dtype=f32, bias=None, add=None, zero_rows_below=0):
    m, _ = a.shape
    n = b.shape[1] if mode == "nn" else b.shape[0]
    tm = _tile(m, MM_ROWS)
    has_bias, has_add = bias is not None, add is not None

    def body(*refs):
        a_ref, b_ref = refs[0], refs[1]
        o_ref = refs[-1]
        acc = _dot(a_ref[...], b_ref[...], mode)
        k = 2
        if has_bias:
            acc = acc + refs[k][...]
            k += 1
        if zero_rows_below:
            rows = pl.program_id(0) * tm + lax.broadcasted_iota(jnp.int32, acc.shape, 0)
            acc = jnp.where(rows >= zero_rows_below, acc, 0.0)
        if has_add:
            acc = acc + refs[k][...].astype(f32)
        o_ref[...] = acc.astype(out_dtype)

    ins = [a, b]
    in_specs = [pl.BlockSpec((tm, a.shape[1]), lambda i: (i, 0)), _full(b.shape)]
    if has_bias:
        ins.append(bias)
        in_specs.append(_full(bias.shape))
    if has_add:
        ins.append(add)
        in_specs.append(pl.BlockSpec((tm, n), lambda i: (i, 0)))
    return pl.pallas_call(
        body, name=name, grid=(m // tm,), in_specs=in_specs,
        out_specs=pl.BlockSpec((tm, n), lambda i: (i, 0)),
        out_shape=jax.ShapeDtypeStruct((m, n), out_dtype),
        compiler_params=_params(("parallel",)),
    )(*ins)


def _pieces(widths):
    out, off = [], 0
    for w in widths:
        out.append((off, w))
        off += w
    return out


def _proj_in(a, w_lr, bias, widths, rope, *, name, zero_rows_below=0):
    m, kdim = a.shape
    half = kdim // 2
    tm = _tile(m, MM_ROWS)
    cos_t, sin_t, swap = rope
    out_widths = [Q_W, KV_W, KV_W] + list(widths[1:])

    def body(a_ref, w_ref, b_ref, cos_ref, sin_ref, swap_ref, *outs):
        a_l, a_r = a_ref[:, :half], a_ref[:, half:]
        for j, (off, width) in enumerate(_pieces(widths)):
            acc = _dot(a_l, w_ref[0, off:off + width, :], "nt") + _dot(a_r, w_ref[1, off:off + width, :], "nt")
            acc = acc + b_ref[:, off:off + width]
            if zero_rows_below:
                rows = pl.program_id(0) * tm + lax.broadcasted_iota(jnp.int32, acc.shape, 0)
                acc = jnp.where(rows >= zero_rows_below, acc, 0.0)
            if j == 0:
                qkv = acc.astype(bf16).astype(f32)
                for o_ref, val in zip(outs[:3], _attn_prep(qkv, cos_ref[...], sin_ref[...], swap_ref[...])):
                    o_ref[...] = val.astype(o_ref.dtype)
            else:
                outs[2 + j][...] = acc.astype(outs[2 + j].dtype)

    table = pl.BlockSpec((tm, HEAD_DIM), lambda i: (i, 0))
    return pl.pallas_call(
        body, name=name, grid=(m // tm,),
        in_specs=[pl.BlockSpec((tm, kdim), lambda i: (i, 0)), _full(w_lr.shape), _full(bias.shape), table, table,
                  _full(swap.shape)],
        out_specs=[pl.BlockSpec((tm, w), lambda i: (i, 0)) for w in out_widths],
        out_shape=[jax.ShapeDtypeStruct((m, w), bf16) for w in out_widths],
        compiler_params=_params(("parallel",)),
    )(a, w_lr, bias, cos_t, sin_t, swap)


def _proj_in_bwd(d_list, w_lr, *, name):
    m = d_list[0].shape[0]
    half = w_lr.shape[2]
    widths = [d.shape[1] for d in d_list]
    tm = _tile(m, MM_ROWS)

    def body(*refs):
        w_ref, o_ref = refs[-2], refs[-1]
        for side in range(2):
            acc = None
            for (off, width), d_ref in zip(_pieces(widths), refs):
                term = _dot(d_ref[...], w_ref[side, off:off + width, :])
                acc = term if acc is None else acc + term
            o_ref[:, side * half:(side + 1) * half] = acc.astype(o_ref.dtype)

    return pl.pallas_call(
        body, name=name, grid=(m // tm,),
        in_specs=[pl.BlockSpec((tm, w), lambda i: (i, 0)) for w in widths] + [_full(w_lr.shape)],
        out_specs=pl.BlockSpec((tm, 2 * half), lambda i: (i, 0)),
        out_shape=jax.ShapeDtypeStruct((m, 2 * half), bf16),
        compiler_params=_params(("parallel",)),
    )(*d_list, w_lr)


def _residual_norm(a, w, res, g, *, name):
    m, d = res.shape
    tm = _tile(m, MM_ROWS)

    def body(a_ref, w_ref, r_ref, g_ref, h_ref, n_ref):
        h = _dot(a_ref[...], w_ref[...]) + r_ref[...]
        h_ref[...] = h
        n_ref[...] = _rms(h, g_ref[...]).astype(n_ref.dtype)

    tile = pl.BlockSpec((tm, d), lambda i: (i, 0))
    return pl.pallas_call(
        body, name=name, grid=(m // tm,),
        in_specs=[pl.BlockSpec((tm, a.shape[1]), lambda i: (i, 0)), _full(w.shape), tile, _full(g.shape)],
        out_specs=[tile, tile],
        out_shape=[jax.ShapeDtypeStruct((m, d), f32), jax.ShapeDtypeStruct((m, d), bf16)],
        compiler_params=_params(("parallel",)),
    )(a, w, res, g)


def _residual_norm_bwd(d_list, w_list, h, g, dh_out, *, name):
    m, d = h.shape
    k = len(d_list)
    tm = _tile(m)

    def body(*refs):
        h_ref, g_ref, dho_ref, dh_ref, dg_ref = refs[2 * k:]
        dn = _dot(refs[0][...], refs[k][...])
        for i in range(1, k):
            dn = dn + _dot(refs[i][...], refs[k + i][...])
        _, vjp = jax.vjp(lambda hv, gv: (_rms(hv, gv), hv), h_ref[...], g_ref[...])
        dh, dg = vjp((dn, dho_ref[...]))
        dh_ref[...] = dh

        @pl.when(pl.program_id(0) == 0)
        def _():
            dg_ref[...] = jnp.zeros_like(dg_ref)

        dg_ref[...] += dg

    tile = pl.BlockSpec((tm, d), lambda i: (i, 0))
    return pl.pallas_call(
        body, name=name, grid=(m // tm,),
        in_specs=[pl.BlockSpec((tm, a.shape[1]), lambda i: (i, 0)) for a in d_list] + [_full(w.shape) for w in w_list]
        + [tile, _full(g.shape), tile],
        out_specs=[tile, _full(g.shape)],
        out_shape=[jax.ShapeDtypeStruct((m, d), f32), jax.ShapeDtypeStruct(g.shape, f32)],
        compiler_params=_params(("arbitrary",)),
    )(*d_list, *w_list, h, g, dh_out)


def _mm_tn(a, b, *, name, colsum=False, out_dtype=bf16):
    r, m = a.shape
    n = b.shape[1]
    tr = _tile(r, 1408)
    tmo = m
    for cand in (1408, 1024, 768, 512):
        if m > 1024 and m % cand == 0:
            tmo = cand
            break
    steps = r // tr

    def body(a_ref, b_ref, o_ref, *rest):
        acc = rest[-1]
        i = pl.program_id(1)

        @pl.when(i == 0)
        def _():
            acc[...] = jnp.zeros_like(acc)
            if colsum:
                rest[0][...] = jnp.zeros_like(rest[0])

        acc[...] += _dot(a_ref[...], b_ref[...], "tn")
        if colsum:
            rest[0][...] += jnp.sum(a_ref[...].astype(f32), axis=0, keepdims=True)

        @pl.when(i == steps - 1)
        def _():
            o_ref[...] = acc[...].astype(out_dtype)

    out_shape = [jax.ShapeDtypeStruct((m, n), out_dtype)]
    out_specs = [pl.BlockSpec((tmo, n), lambda j, i: (j, 0))]
    if colsum:
        out_shape.append(jax.ShapeDtypeStruct((1, m), f32))
        out_specs.append(pl.BlockSpec((1, tmo), lambda j, i: (0, j)))
    res = pl.pallas_call(
        body, name=name, grid=(m // tmo, steps),
        in_specs=[pl.BlockSpec((tr, tmo), lambda j, i: (i, j)), pl.BlockSpec((tr, n), lambda j, i: (i, 0))],
        out_specs=out_specs, out_shape=out_shape,
        scratch_shapes=[pltpu.VMEM((tmo, n), f32)],
        compiler_params=_params(("parallel", "arbitrary")),
    )(a, b)
    return res if colsum else res[0]


def _rowwise(fn, rows, params, outs, *, name, tm=None):
    m = rows[0].shape[0]
    tm = tm or _tile(m, MM_ROWS)
    nr, npar = len(rows), len(params)

    def body(*refs):
        vals = [r[...] for r in refs[:nr + npar]]
        res = fn(*vals)
        for o_ref, v in zip(refs[nr + npar:], res):
            o_ref[...] = v.astype(o_ref.dtype)

    return pl.pallas_call(
        body, name=name, grid=(m // tm,),
        in_specs=[pl.BlockSpec((tm, r.shape[1]), lambda i: (i, 0)) for r in rows] + [_full(p.shape) for p in params],
        out_specs=[pl.BlockSpec((tm, w), lambda i: (i, 0)) for w, _ in outs],
        out_shape=[jax.ShapeDtypeStruct((m, w), dt) for w, dt in outs],
        compiler_params=_params(("parallel",)),
    )(*rows, *params)


def _rowwise_bwd(fn, rows, params, cts, *, name, diff_rows, diff_params, tm=None, zero_rows_below=0, out_dtypes=None):
    m = rows[0].shape[0]
    tm = tm or _tile(m)
    nr, npar = len(rows), len(params)
    d_idx = [i for i in range(nr) if diff_rows[i]]
    p_idx = [i for i in range(npar) if diff_params[i]]
    out_dtypes = out_dtypes or [f32] * len(d_idx)
    flat_cts = [c for group in cts for c in group]
    n_ct = len(flat_cts)

    def body(*refs):
        vals = [r[...] for r in refs[:nr + npar]]
        ct_refs = refs[nr + npar:nr + npar + n_ct]
        out_refs = refs[nr + npar + n_ct:]
        ct_vals, k = [], 0
        for group in cts:
            acc = ct_refs[k][...].astype(f32)
            for extra in range(1, len(group)):
                acc = acc + ct_refs[k + extra][...].astype(f32)
            k += len(group)
            if zero_rows_below:
                rr = pl.program_id(0) * tm + lax.broadcasted_iota(jnp.int32, acc.shape, 0)
                acc = jnp.where(rr >= zero_rows_below, acc, 0.0)
            ct_vals.append(acc)

        def g(*dargs):
            full = list(vals)
            for pos, i in enumerate(d_idx):
                full[i] = dargs[pos]
            for pos, i in enumerate(p_idx):
                full[nr + i] = dargs[len(d_idx) + pos]
            return tuple(fn(*full))

        _, vjp = jax.vjp(g, *[vals[i].astype(f32) for i in d_idx], *[vals[nr + i] for i in p_idx])
        grads = vjp(tuple(ct_vals))
        for pos in range(len(d_idx)):
            out_refs[pos][...] = grads[pos].astype(out_refs[pos].dtype)
        first = pl.program_id(0) == 0
        for pos in range(len(p_idx)):
            o_ref = out_refs[len(d_idx) + pos]

            @pl.when(first)
            def _(o_ref=o_ref):
                o_ref[...] = jnp.zeros_like(o_ref)

            o_ref[...] += grads[len(d_idx) + pos]

    return pl.pallas_call(
        body, name=name, grid=(m // tm,),
        in_specs=[pl.BlockSpec((tm, r.shape[1]), lambda i: (i, 0)) for r in rows] + [_full(p.shape) for p in params]
        + [pl.BlockSpec((tm, c.shape[1]), lambda i: (i, 0)) for c in flat_cts],
        out_specs=[pl.BlockSpec((tm, rows[i].shape[1]), lambda i_: (i_, 0)) for i in d_idx]
        + [_full(params[i].shape) for i in p_idx],
        out_shape=[jax.ShapeDtypeStruct(rows[i].shape, dt) for i, dt in zip(d_idx, out_dtypes)]
        + [jax.ShapeDtypeStruct(params[i].shape, f32) for i in p_idx],
        compiler_params=_params(("arbitrary",)),
    )(*rows, *params, *flat_cts)


def _rms(x, g):
    return x * lax.rsqrt(jnp.mean(x * x, axis=-1, keepdims=True) + RMS_EPS) * g


def _head_sum_matrix(width, head):
    idx = jnp.arange(width) // head
    return (idx[:, None] == idx[None, :]).astype(f32)


def _rope_tables(lp):
    half = ROPE_DIM // 2
    pos = (np.arange(lp) - PAD).astype(np.float32)
    inv_freq = np.power(np.float32(ROPE_THETA), -np.arange(half, dtype=np.float32) * np.float32(2.0 / ROPE_DIM))
    ang = pos[:, None] * inv_freq[None, :].astype(np.float32)
    cos, sin = np.cos(ang), np.sin(ang)
    ones = np.ones((lp, HEAD_DIM - ROPE_DIM), np.float32)
    cos_t = np.concatenate([cos, cos, ones], axis=1)
    sin_t = np.concatenate([-sin, sin, 0.0 * ones], axis=1)
    i = np.arange(HEAD_DIM)
    src = np.where(i < half, i + half, np.where(i < ROPE_DIM, i - half, i))
    swap = ((i[:, None] == src[None, :]) & (i[None, :] < ROPE_DIM)).astype(np.float32)
    return jnp.asarray(cos_t, f32), jnp.asarray(sin_t, f32), jnp.asarray(swap, f32)


def _attn_prep(qkv, cos_t, sin_t, swap):
    outs = []
    for h in range(Q_HEADS + KV_HEADS):
        t = qkv[:, h * HEAD_DIM:(h + 1) * HEAD_DIM]
        outs.append(t * cos_t + _dot_const(t, swap) * sin_t)
    q = jnp.concatenate(outs[:Q_HEADS], axis=1)
    k = jnp.concatenate(outs[Q_HEADS:], axis=1)
    return q, k, qkv[:, Q_W + KV_W:]


def _attn_prep_transposed(dq, dk, dk_meta, dv, dv_meta, cos_t, sin_t, swap):
    parts = []
    for d, heads in ((dq.astype(f32), Q_HEADS), (dk.astype(f32) + dk_meta.astype(f32), KV_HEADS)):
        for h in range(heads):
            t = d[:, h * HEAD_DIM:(h + 1) * HEAD_DIM]
            parts.append(t * cos_t + _two_pass(t * sin_t, swap, "nt"))
    return (jnp.concatenate(parts + [dv.astype(f32) + dv_meta.astype(f32)], axis=1),)


def _softplus(z):
    return jnp.maximum(z, 0.0) + jnp.log1p(jnp.exp(-jnp.abs(z)))


def _rwkv_prep(rkv, lora, w0, w2, a0, a2, g2, k_k, k_a, hsum):
    r = rkv[:, :RWKV_DIM]
    k = rkv[:, RWKV_DIM:2 * RWKV_DIM]
    v = rkv[:, 2 * RWKV_DIM:]
    dw = lora[:, :DECAY_LORA]
    da = lora[:, DECAY_LORA:DECAY_LORA + AAA_LORA]
    dg = lora[:, DECAY_LORA + AAA_LORA:]
    w = -_softplus(-(w0 + _dot(jnp.tanh(dw), w2))) - 0.5
    a = jax.nn.sigmoid(a0 + _dot(da, a2))
    g = _dot(jax.nn.sigmoid(dg), g2)
    kk = k * k_k
    kk = kk * lax.rsqrt(jnp.maximum(_dot_const(kk * kk, hsum), 1e-24))
    k = k * (1.0 + (a - 1.0) * k_a)
    log_decay = -jnp.exp(w)
    return r, log_decay, k, v, -kk, kk * a, g


def _rwkv_post(y, r, k, v, g, ln_w, ln_b, r_k, hmean):
    hsum = hmean * RWKV_HEAD
    mean = _dot_const(y, hmean)
    yc = y - mean
    var = _dot_const(yc * yc, hmean)
    yn = yc * lax.rsqrt(var + RWKV_LN_EPS) * ln_w + ln_b
    bonus = _dot_const(r * k * r_k, hsum) * v
    return ((yn + bonus) * g,)


def _merge(gates, br_a, br_r):
    sg = jax.nn.sigmoid(gates)
    return (sg[:, :D_MODEL] * br_a + sg[:, D_MODEL:] * br_r,)


def _swiglu(gate, up):
    return (jax.nn.silu(gate) * up,)


def _ffn_in(f, w_gate_t, w_up_t, *, name):
    m, d = f.shape
    n = w_gate_t.shape[0]
    tm = _tile(m)

    def body(f_ref, wg_ref, wu_ref, g_ref, u_ref, a_ref):
        g = _dot(f_ref[...], wg_ref[...], "nt")
        u = _dot(f_ref[...], wu_ref[...], "nt")
        g_ref[...] = g.astype(g_ref.dtype)
        u_ref[...] = u.astype(u_ref.dtype)
        a_ref[...] = _swiglu(g, u)[0].astype(a_ref.dtype)

    spec = pl.BlockSpec((tm, n), lambda i: (i, 0))
    return pl.pallas_call(
        body, name=name, grid=(m // tm,),
        in_specs=[pl.BlockSpec((tm, d), lambda i: (i, 0)), _full(w_gate_t.shape), _full(w_up_t.shape)],
        out_specs=[spec] * 3, out_shape=[jax.ShapeDtypeStruct((m, n), bf16)] * 3,
        compiler_params=_params(("parallel",)),
    )(f, w_gate_t, w_up_t)


def _branch_merge(y_attn, y_rwkv, w_attn_t, w_rwkv_t, gates, *, name):
    m = y_attn.shape[0]
    tm = _tile(m, MM_ROWS)

    def body(ya_ref, yr_ref, wa_ref, wr_ref, g_ref, a_ref, r_ref, o_ref):
        br_a = _dot(ya_ref[...], wa_ref[...], "nt")
        br_r = _dot(yr_ref[...], wr_ref[...], "nt")
        a_ref[...] = br_a.astype(a_ref.dtype)
        r_ref[...] = br_r.astype(r_ref.dtype)
        o_ref[...] = _merge(g_ref[...].astype(f32), br_a, br_r)[0].astype(o_ref.dtype)

    rows = lambda a: pl.BlockSpec((tm, a.shape[1]), lambda i: (i, 0))
    spec = pl.BlockSpec((tm, D_MODEL), lambda i: (i, 0))
    return pl.pallas_call(
        body, name=name, grid=(m // tm,),
        in_specs=[rows(y_attn), rows(y_rwkv), _full(w_attn_t.shape), _full(w_rwkv_t.shape), rows(gates)],
        out_specs=[spec] * 3, out_shape=[jax.ShapeDtypeStruct((m, D_MODEL), bf16)] * 3,
        compiler_params=_params(("parallel",)),
    )(y_attn, y_rwkv, w_attn_t, w_rwkv_t, gates)


def _branch_merge_bwd(dh, w_o, gates, br_a, br_r, w_attn_t, w_rwkv_t, *, name):
    m = dh.shape[0]
    tm = _tile(m, MM_ROWS)

    def body(dh_ref, w_ref, g_ref, a_ref, r_ref, wa_ref, wr_ref, dg_ref, da_ref, dr_ref, dya_ref, dyr_ref):
        dmerged = _dot(dh_ref[...], w_ref[...], "nt")
        _, vjp = jax.vjp(lambda g, a, r: _merge(g, a, r)[0], g_ref[...].astype(f32), a_ref[...].astype(f32),
                         r_ref[...].astype(f32))
        dg, da, dr = vjp(dmerged)
        dg_ref[...] = dg.astype(dg_ref.dtype)
        da_ref[...] = da.astype(da_ref.dtype)
        dr_ref[...] = dr.astype(dr_ref.dtype)
        dya_ref[...] = _dot(da, wa_ref[...])
        dyr_ref[...] = _dot(dr, wr_ref[...])

    rows = lambda a: pl.BlockSpec((tm, a.shape[1]), lambda i: (i, 0))
    mixer = pl.BlockSpec((tm, w_attn_t.shape[1]), lambda i: (i, 0))
    return pl.pallas_call(
        body, name=name, grid=(m // tm,),
        in_specs=[rows(dh), _full(w_o.shape), rows(gates), rows(br_a), rows(br_r), _full(w_attn_t.shape),
                  _full(w_rwkv_t.shape)],
        out_specs=[rows(gates), rows(br_a), rows(br_r), mixer, mixer],
        out_shape=[jax.ShapeDtypeStruct(gates.shape, bf16), jax.ShapeDtypeStruct(br_a.shape, bf16),
                   jax.ShapeDtypeStruct(br_r.shape, bf16), jax.ShapeDtypeStruct((m, w_attn_t.shape[1]), f32),
                   jax.ShapeDtypeStruct((m, w_rwkv_t.shape[1]), f32)],
        compiler_params=_params(("parallel",)),
    )(dh, w_o, gates, br_a, br_r, w_attn_t, w_rwkv_t)


def _ffn_in_bwd(dh, w_down, gate, up, *, name):
    m, d = dh.shape
    n = w_down.shape[0]
    tm = _tile(m)

    def body(dh_ref, w_ref, g_ref, u_ref, dg_ref, du_ref):
        dact = _dot(dh_ref[...], w_ref[...], "nt")
        _, vjp = jax.vjp(lambda a, b: _swiglu(a, b)[0], g_ref[...].astype(f32), u_ref[...].astype(f32))
        dg, du = vjp(dact)
        dg_ref[...] = dg.astype(dg_ref.dtype)
        du_ref[...] = du.astype(du_ref.dtype)

    spec = pl.BlockSpec((tm, n), lambda i: (i, 0))
    return pl.pallas_call(
        body, name=name, grid=(m // tm,),
        in_specs=[pl.BlockSpec((tm, d), lambda i: (i, 0)), _full(w_down.shape), spec, spec],
        out_specs=[spec] * 2, out_shape=[jax.ShapeDtypeStruct((m, n), bf16)] * 2,
        compiler_params=_params(("parallel",)),
    )(dh, w_down, gate, up)


HALO = 16


def _previous_rows(x, before_ref, first_tile):
    rows = lax.broadcasted_iota(jnp.int32, x.shape, 0)
    last = jnp.where(first_tile, 0.0, before_ref[HALO - 1:HALO, :].astype(f32))
    return jnp.where(rows == 0, last, pltpu.roll(x, 1, axis=0))


def _mixer_inputs(ps, mixes, params, *, name):
    m = ps[0].shape[0]
    tm = _tile(m)
    sub = tm // HALO
    n_par = len(params)

    def body(*refs):
        first = pl.program_id(0) == 0
        pf = []
        for k in range(2):
            x = refs[k][...].astype(f32)
            pf.append(x + (_previous_rows(x, refs[2 + k], first) - x) * refs[4 + k][...])
        res = _rwkv_prep(*pf, *[ref[...] for ref in refs[6:6 + n_par]])
        for o_ref, val in zip(refs[6 + n_par:], res):
            o_ref[...] = val

    tile = lambda a: pl.BlockSpec((tm, a.shape[1]), lambda i: (i, 0))
    before = lambda a: pl.BlockSpec((HALO, a.shape[1]), lambda i: (jnp.maximum(i * sub - 1, 0), 0))
    out = pl.BlockSpec((tm, RWKV_DIM), lambda i: (i, 0))
    return pl.pallas_call(
        body, name=name, grid=(m // tm,),
        in_specs=[tile(a) for a in ps] + [before(a) for a in ps] + [_full(a.shape) for a in mixes + params],
        out_specs=[out] * 7, out_shape=[jax.ShapeDtypeStruct((m, RWKV_DIM), f32)] * 7,
        compiler_params=_params(("parallel",)),
    )(*ps, *ps, *mixes, *params)


def _mixer_inputs_bwd(ps, mixes, params, cts, *, name):
    m = ps[0].shape[0]
    tm = _tile(m)
    sub = tm // HALO
    nt = m // tm
    n_par = len(params)
    flat_cts = [c for group in cts for c in group]
    n_ct = len(flat_cts)

    def body(*refs):
        i = pl.program_id(0)
        tile_index = nt - 1 - i
        ct_refs = refs[6 + n_par:6 + n_par + n_ct]
        dp_refs = refs[6 + n_par + n_ct:8 + n_par + n_ct]
        dmix_refs = refs[8 + n_par + n_ct:10 + n_par + n_ct]
        dpar_refs = refs[10 + n_par + n_ct:9 + 2 * n_par + n_ct]
        carries = refs[9 + 2 * n_par + n_ct:]
        rows1 = tile_index * tm + lax.broadcasted_iota(jnp.int32, (tm, 1), 0)
        live = rows1 >= PAD

        @pl.when(i == 0)
        def _():
            for ref in (*dmix_refs, *dpar_refs, *carries):
                ref[...] = jnp.zeros_like(ref)

        xs, prevs, pf = [], [], []
        for k in range(2):
            x = refs[k][...].astype(f32)
            xp = _previous_rows(x, refs[2 + k], tile_index == 0)
            xs.append(x)
            prevs.append(xp)
            pf.append(x + (xp - x) * refs[4 + k][...])
        ct_vals, pos = [], 0
        for group in cts:
            acc = ct_refs[pos][...].astype(f32)
            for extra in range(1, len(group)):
                acc = acc + ct_refs[pos + extra][...].astype(f32)
            pos += len(group)
            ct_vals.append(jnp.where(live, acc, 0.0))
        par_vals = [ref[...] for ref in refs[6:6 + n_par]]
        _, vjp = jax.vjp(lambda *args: _rwkv_prep(*args, par_vals[-1]), *pf, *par_vals[:-1])
        g = vjp(tuple(ct_vals))
        for k in range(2):
            dpf = g[k]
            mixv = refs[4 + k][...]
            dm = dpf * mixv
            rows = lax.broadcasted_iota(jnp.int32, dm.shape, 0)
            dm_next = jnp.where(rows == tm - 1, carries[k][...], pltpu.roll(dm, tm - 1, axis=0))
            dp_refs[k][...] = jnp.where(live, dpf - dm + dm_next, 0.0).astype(dp_refs[k].dtype)
            carries[k][...] = dm[0:1, :]
            dmix_refs[k][...] += jnp.sum(dpf * (prevs[k] - xs[k]), axis=0, keepdims=True)
        for ref, val in zip(dpar_refs, g[2:]):
            ref[...] += val

    tile = lambda a: pl.BlockSpec((tm, a.shape[1]), lambda i: (nt - 1 - i, 0))
    before = lambda a: pl.BlockSpec((HALO, a.shape[1]), lambda i: (jnp.maximum((nt - 1 - i) * sub - 1, 0), 0))
    return pl.pallas_call(
        body, name=name, grid=(nt,),
        in_specs=[tile(a) for a in ps] + [before(a) for a in ps] + [_full(a.shape) for a in mixes + params]
        + [tile(c) for c in flat_cts],
        out_specs=[tile(a) for a in ps] + [_full(a.shape) for a in mixes + params[:-1]],
        out_shape=[jax.ShapeDtypeStruct(a.shape, bf16) for a in ps]
        + [jax.ShapeDtypeStruct(a.shape, f32) for a in mixes + params[:-1]],
        scratch_shapes=[pltpu.VMEM((1, a.shape[1]), f32) for a in ps],
        compiler_params=_params(("arbitrary",)),
    )(*ps, *ps, *mixes, *params, *flat_cts)


def _attn_masks(blk):
    qi = lax.broadcasted_iota(jnp.int32, (BLOCK, BLOCK), 0)
    ki = lax.broadcasted_iota(jnp.int32, (BLOCK, BLOCK), 1)
    qpos = blk * BLOCK + qi - PAD
    kpos_c = blk * BLOCK + ki - PAD
    kpos_p = kpos_c - BLOCK
    kpos_m = ki - PAD

    def band(kpos):
        return (kpos >= N_META) & (kpos <= qpos) & (qpos - kpos < WINDOW)

    return band(kpos_p), band(kpos_c), (kpos_m >= 0) & (kpos_m <= qpos)


def _attn_probs(qs, k3s, sink, oks):
    s = [[jnp.where(ok, _dot(qh, kx, "nt"), NEG_INF) for kx, ok in zip(k3, oks)] for qh, k3 in zip(qs, k3s)]
    mx = [jnp.maximum(jnp.maximum(jnp.max(t[0], -1, keepdims=True), jnp.max(t[1], -1, keepdims=True)),
                      jnp.maximum(jnp.max(t[2], -1, keepdims=True), sk)) for t, sk in zip(s, sink)]
    e = [[jnp.exp(tx - m) for tx in t] for t, m in zip(s, mx)]
    e_sink = [jnp.exp(sk - m) for sk, m in zip(sink, mx)]
    inv = [1.0 / (jnp.sum(t[0], -1, keepdims=True) + jnp.sum(t[1], -1, keepdims=True)
                  + jnp.sum(t[2], -1, keepdims=True) + es) for t, es in zip(e, e_sink)]
    return [[tx * i for tx in t] for t, i in zip(e, inv)], [es * i for es, i in zip(e_sink, inv)]


def _head_cols(i):
    return slice(i * HEAD_DIM, (i + 1) * HEAD_DIM)


def _attn_operands(refs):
    q_ref, kp_ref, kc_ref, km_ref, vp_ref, vc_ref, vm_ref, s_ref = refs
    qs = [q_ref[:, _head_cols(i)] * (HEAD_DIM ** -0.5) for i in range(Q_HEADS)]
    k3 = [[ref[:, _head_cols(h)] for ref in (kp_ref, kc_ref, km_ref)] for h in range(KV_HEADS)]
    v3 = [[ref[:, _head_cols(h)] for ref in (vp_ref, vc_ref, vm_ref)] for h in range(KV_HEADS)]
    return (qs, [k3[i // GROUP] for i in range(Q_HEADS)], [v3[i // GROUP] for i in range(Q_HEADS)],
            [s_ref[:, i:i + 1] for i in range(Q_HEADS)])


def _attention(q, k, v, sinks, *, name):
    lp = q.shape[0]
    nb = lp // BLOCK
    prev = lambda i: (jnp.maximum(i - 1, 0), 0)
    cur = lambda i: (i, 0)
    meta = lambda i: (0, 0)
    kv = lambda index: pl.BlockSpec((BLOCK, KV_W), index)

    def body(*refs):
        o_ref = refs[-1]
        qs, k3s, v3s, sink = _attn_operands(refs[:-1])
        p, _ = _attn_probs(qs, k3s, sink, _attn_masks(pl.program_id(0)))
        out = [_dot(ph[0], v3[0]) + _dot(ph[1], v3[1]) + _dot(ph[2], v3[2]) for ph, v3 in zip(p, v3s)]
        for i in range(Q_HEADS):
            o_ref[:, _head_cols(i)] = out[i].astype(o_ref.dtype)

    return pl.pallas_call(
        body, name=name, grid=(nb,),
        in_specs=[pl.BlockSpec((BLOCK, Q_W), cur), kv(prev), kv(cur), kv(meta), kv(prev), kv(cur), kv(meta),
                  _full((1, Q_HEADS))],
        out_specs=pl.BlockSpec((BLOCK, Q_W), cur),
        out_shape=jax.ShapeDtypeStruct((lp, Q_W), bf16),
        compiler_params=_params(("parallel",)),
    )(q, k, k, k, v, v, v, sinks)


def _attention_bwd(q, k, v, sinks, out, do, *, name):
    lp = q.shape[0]
    nb = lp // BLOCK
    cur = lambda n: (jnp.minimum(n, nb - 1), 0)
    prev = lambda n: (jnp.maximum(jnp.minimum(n, nb - 1) - 1, 0), 0)
    behind = lambda n: (jnp.maximum(n - 1, 0), 0)
    meta = lambda n: (0, 0)
    kv = lambda index: pl.BlockSpec((BLOCK, KV_W), index)
    scale = HEAD_DIM ** -0.5

    def body(*refs):
        ins, fwd_ref, do_ref = refs[:8], refs[8], refs[9]
        dq_ref, dk_ref, dv_ref, dkm_ref, dvm_ref, ds_ref, carry_k, carry_v = refs[10:]
        n = pl.program_id(0)

        @pl.when(n == 0)
        def _():
            for ref in (dkm_ref, dvm_ref, ds_ref, carry_k, carry_v):
                ref[...] = jnp.zeros_like(ref)

        @pl.when(n < nb)
        def _():
            qs, k3s, v3s, sink = _attn_operands(ins)
            do = [do_ref[:, _head_cols(i)] for i in range(Q_HEADS)]
            p, p_sink = _attn_probs(qs, k3s, sink, _attn_masks(n))
            delta = [jnp.sum(d * fwd_ref[:, _head_cols(i)].astype(f32), -1, keepdims=True) for i, d in enumerate(do)]
            dp = [[_dot(d, vx, "nt") for vx in v3] for d, v3 in zip(do, v3s)]
            ds = [[px * (dx - dl) for px, dx in zip(ph, dh)] for ph, dh, dl in zip(p, dp, delta)]
            dq = [_dot(dsh[0], k3[0]) + _dot(dsh[1], k3[1]) + _dot(dsh[2], k3[2]) for dsh, k3 in zip(ds, k3s)]
            for i in range(Q_HEADS):
                dq_ref[:, _head_cols(i)] = dq[i] * scale
                ds_ref[:, i:i + 1] -= jnp.sum(p_sink[i] * delta[i], axis=0, keepdims=True)
            for h in range(KV_HEADS):
                group = slice(h * GROUP, (h + 1) * GROUP)
                q_all = jnp.concatenate(qs[group], axis=0)
                do_all = jnp.concatenate(do[group], axis=0)
                dk3 = [_dot(jnp.concatenate([dsh[x] for dsh in ds[group]], axis=0), q_all, "tn") for x in range(3)]
                dv3 = [_dot(jnp.concatenate([ph[x] for ph in p[group]], axis=0), do_all, "tn") for x in range(3)]
                hs = _head_cols(h)
                for out_ref, carry, meta_ref, d3 in ((dk_ref, carry_k, dkm_ref, dk3),
                                                     (dv_ref, carry_v, dvm_ref, dv3)):
                    out_ref[:, hs] = carry[:, hs] + d3[0]
                    carry[:, hs] = d3[1]
                    meta_ref[:, hs] += d3[2]

        @pl.when(n == nb)
        def _():
            dk_ref[...] = carry_k[...]
            dv_ref[...] = carry_v[...]

    kv_shape = jax.ShapeDtypeStruct((lp, KV_W), f32)
    one_shape = jax.ShapeDtypeStruct((BLOCK, KV_W), f32)
    return pl.pallas_call(
        body, name=name, grid=(nb + 1,),
        in_specs=[pl.BlockSpec((BLOCK, Q_W), cur), kv(prev), kv(cur), kv(meta), kv(prev), kv(cur), kv(meta),
                  _full((1, Q_HEADS)), pl.BlockSpec((BLOCK, Q_W), cur), pl.BlockSpec((BLOCK, Q_W), cur)],
        out_specs=[pl.BlockSpec((BLOCK, Q_W), cur), kv(behind), kv(behind), kv(meta), kv(meta),
                   _full((1, Q_HEADS))],
        out_shape=[jax.ShapeDtypeStruct((lp, Q_W), f32), kv_shape, kv_shape, one_shape, one_shape,
                   jax.ShapeDtypeStruct((1, Q_HEADS), f32)],
        scratch_shapes=[pltpu.VMEM((BLOCK, KV_W), f32), pltpu.VMEM((BLOCK, KV_W), f32)],
        compiler_params=_params(("arbitrary",)),
    )(q, k, k, k, v, v, v, sinks, out, do)


@jax.custom_vjp
def _known_inverse(l, x):
    return x


def _known_inverse_fwd(l, x):
    return x, x


def _known_inverse_bwd(x, ct):
    return _dot(_dot(x, ct, "tn"), x, "nt"), jnp.zeros_like(x)


_known_inverse.defvjp(_known_inverse_fwd, _known_inverse_bwd)


@jax.custom_vjp
def _decayed(x, c):
    return (x * jnp.exp(c)).astype(bf16).astype(f32)


def _decayed_fwd(x, c):
    e = jnp.exp(c)
    out = (x * e).astype(bf16).astype(f32)
    return out, (e, out)


def _decayed_bwd(res, ct):
    e, out = res
    return ct * e, ct * out


_decayed.defvjp(_decayed_fwd, _decayed_bwd)


@jax.custom_vjp
def _pair(x, y):
    return _dot(x, y, "nt")


def _pair_fwd(x, y):
    return _dot(x, y, "nt"), (x, y)


def _pair_bwd(res, ct):
    x, y = res
    hi = ct.astype(bf16)
    lo = (ct - hi.astype(f32)).astype(bf16)
    return _dot(hi, y) + _dot(lo, y), _dot(hi, x, "tn") + _dot(lo, x, "tn")


_pair.defvjp(_pair_fwd, _pair_bwd)


def _scan_chunk(s0, r, lw, k, v, a, b, inv=None):
    t = r[0].shape[0]
    ii = lax.broadcasted_iota(jnp.int32, (t, t), 0)
    jj = lax.broadcasted_iota(jnp.int32, (t, t), 1)
    incl = jj <= ii
    strict = jj < ii
    tri = incl.astype(f32)
    eye = jnp.where(ii == jj, 1.0, 0.0)
    cl = [_const_dot(tri, x) for x in lw]
    mid = [c[t // 2 - 1:t // 2, :] for c in cl]
    s0 = [s * jnp.exp(m) for s, m in zip(s0, mid)]
    cl = [c - m for c, m in zip(cl, mid)]
    rt = [_decayed(x, c) for x, c in zip(r, cl)]
    at = [_decayed(x, c - l) for x, c, l in zip(a, cl, lw)]
    bt = [_decayed(x, -c) for x, c in zip(b, cl)]
    kt = [_decayed(x, -c) for x, c in zip(k, cl)]
    l_ab = [jnp.where(strict, _pair(x, y), 0.0) for x, y in zip(at, bt)]
    l_ak = [jnp.where(strict, _pair(x, y), 0.0) for x, y in zip(at, kt)]
    r_b = [jnp.where(incl, _pair(x, y), 0.0) for x, y in zip(rt, bt)]
    r_k = [jnp.where(incl, _pair(x, y), 0.0) for x, y in zip(rt, kt)]
    if inv is None:
        inv = [eye + x for x in l_ab]
        pw = l_ab
        for _ in range(int(math.log2(t)) - 1):
            pw = [_dot(x, x) for x in pw]
            inv = [x + _dot(x, y) for x, y in zip(inv, pw)]
    else:
        inv = [_known_inverse(x, y) for x, y in zip(l_ab, inv)]
    rhs = [_dot(x, s, "nt") + _dot(m, y) for x, s, m, y in zip(at, s0, l_ak, v)]
    u = [_dot(x, y) for x, y in zip(inv, rhs)]
    y_s = [_dot(x, s, "nt") for x, s in zip(rt, s0)]
    y = [ys + _dot(m, uu) + _dot(n, vv) for ys, m, uu, n, vv in zip(y_s, r_b, u, r_k, v)]
    grow = [s + _dot(uu, x, "tn") + _dot(vv, z, "tn") for s, uu, x, vv, z in zip(s0, u, bt, v, kt)]
    s1 = [g * jnp.exp(c[t - 1:t, :]) for g, c in zip(grow, cl)]
    return y, s1, inv


def _head_rows(h):
    return slice(h * RWKV_HEAD, (h + 1) * RWKV_HEAD)


def _per_head(ref):
    return [ref[:, _head_rows(h)] for h in range(RWKV_HEADS)]


def _scan(r, lw, k, v, a, b, *, name):
    lp = r.shape[0]
    nc = lp // CHUNK
    row = pl.BlockSpec((CHUNK, RWKV_DIM), lambda c: (c, 0))

    def body(r_ref, lw_ref, k_ref, v_ref, a_ref, b_ref, y_ref, s_ref, inv_ref, state):
        @pl.when(pl.program_id(0) == 0)
        def _():
            state[...] = jnp.zeros_like(state)

        s_ref[...] = state[...]
        s0 = [state[_head_rows(h), :] for h in range(RWKV_HEADS)]
        y, s1, inv = _scan_chunk(s0, *[_per_head(ref) for ref in (r_ref, lw_ref, k_ref, v_ref, a_ref, b_ref)])
        for h in range(RWKV_HEADS):
            y_ref[:, _head_rows(h)] = y[h]
            state[_head_rows(h), :] = s1[h]
            inv_ref[h * CHUNK:(h + 1) * CHUNK, :] = inv[h].astype(inv_ref.dtype)

    return pl.pallas_call(
        body, name=name, grid=(nc,), in_specs=[row] * 6,
        out_specs=[row, pl.BlockSpec((RWKV_DIM, RWKV_HEAD), lambda c: (c, 0)),
                   pl.BlockSpec((RWKV_HEADS * CHUNK, CHUNK), lambda c: (c, 0))],
        out_shape=[jax.ShapeDtypeStruct((lp, RWKV_DIM), f32), jax.ShapeDtypeStruct((nc * RWKV_DIM, RWKV_HEAD), f32),
                   jax.ShapeDtypeStruct((nc * RWKV_HEADS * CHUNK, CHUNK), bf16)],
        scratch_shapes=[pltpu.VMEM((RWKV_DIM, RWKV_HEAD), f32)],
        compiler_params=_params(("arbitrary",)),
    )(r, lw, k, v, a, b)


def _scan_bwd(r, lw, k, v, a, b, states, inverses, dy, *, name):
    lp = r.shape[0]
    nc = lp // CHUNK
    back = lambda c: (nc - 1 - c, 0)
    row = pl.BlockSpec((CHUNK, RWKV_DIM), back)

    def body(r_ref, lw_ref, k_ref, v_ref, a_ref, b_ref, s_ref, inv_ref, dy_ref,
             dr_ref, dlw_ref, dk_ref, dv_ref, da_ref, db_ref, dstate):
        @pl.when(pl.program_id(0) == 0)
        def _():
            dstate[...] = jnp.zeros_like(dstate)

        outs = (dr_ref, dlw_ref, dk_ref, dv_ref, da_ref, db_ref)
        s0 = [s_ref[_head_rows(h), :] for h in range(RWKV_HEADS)]
        inv = [inv_ref[h * CHUNK:(h + 1) * CHUNK, :].astype(f32) for h in range(RWKV_HEADS)]
        _, vjp = jax.vjp(lambda *args: _scan_chunk(*args, inv=inv)[:2], s0,
                         *[_per_head(ref) for ref in (r_ref, lw_ref, k_ref, v_ref, a_ref, b_ref)])
        g = vjp((_per_head(dy_ref), [dstate[_head_rows(h), :] for h in range(RWKV_HEADS)]))
        for h in range(RWKV_HEADS):
            dstate[_head_rows(h), :] = g[0][h]
            for o_ref, gv in zip(outs, g[1:]):
                o_ref[:, _head_rows(h)] = gv[h]

    shape = jax.ShapeDtypeStruct((lp, RWKV_DIM), f32)
    return pl.pallas_call(
        body, name=name, grid=(nc,),
        in_specs=[row] * 6 + [pl.BlockSpec((RWKV_DIM, RWKV_HEAD), back),
                              pl.BlockSpec((RWKV_HEADS * CHUNK, CHUNK), back), row],
        out_specs=[row] * 6, out_shape=[shape] * 6,
        scratch_shapes=[pltpu.VMEM((RWKV_DIM, RWKV_HEAD), f32)],
        compiler_params=_params(("arbitrary",)),
    )(r, lw, k, v, a, b, states, inverses, dy)


def _loss_head(h2, target, g_final, *, name):
    lp = h2.shape[0]
    per_tile = 3
    tm = per_tile * BLOCK
    last_block = (lp - FRONT) // BLOCK - 1

    def body(h_ref, t0_ref, t1_ref, t2_ref, g_ref, loss_ref, dh_ref, dg_ref):
        i = pl.program_id(0)
        target_rows = jnp.concatenate([t0_ref[...], t1_ref[...], t2_ref[...]], axis=0)
        real = i * tm + lax.broadcasted_iota(jnp.int32, (tm, 1), 0) >= FRONT

        def tile_loss(hv, gv):
            err = _rms(hv, gv) - target_rows
            return 0.5 * jnp.sum(jnp.where(real, jnp.mean(err * err, axis=-1, keepdims=True), 0.0))

        loss, (dh, dg) = jax.value_and_grad(tile_loss, argnums=(0, 1))(h_ref[...], g_ref[...])

        @pl.when(i == 0)
        def _():
            loss_ref[...] = jnp.zeros_like(loss_ref)
            dg_ref[...] = jnp.zeros_like(dg_ref)

        loss_ref[...] += jnp.full(loss_ref.shape, loss, f32)
        dg_ref[...] += dg
        dh_ref[...] = dh

    def target_block(j):
        return pl.BlockSpec((BLOCK, D_MODEL),
                            lambda i: (jnp.clip(per_tile * i + j - FRONT // BLOCK, 0, last_block), 0))

    return pl.pallas_call(
        body, name=name, grid=(lp // tm,),
        in_specs=[pl.BlockSpec((tm, D_MODEL), lambda i: (i, 0)), target_block(0), target_block(1), target_block(2),
                  _full(g_final.shape)],
        out_specs=[_full((8, 128)), pl.BlockSpec((tm, D_MODEL), lambda i: (i, 0)), _full(g_final.shape)],
        out_shape=[jax.ShapeDtypeStruct((8, 128), f32), jax.ShapeDtypeStruct((lp, D_MODEL), f32),
                   jax.ShapeDtypeStruct(g_final.shape, f32)],
        compiler_params=_params(("arbitrary",)),
    )(h2, target, target, target, g_final)


def _embed_norm(x, meta, g, *, name):
    seq = x.shape[0]
    lp = seq + FRONT
    per_tile = 3
    tm = per_tile * BLOCK
    last_block = seq // BLOCK - 1

    def body(x0_ref, x1_ref, x2_ref, meta_ref, g_ref, h_ref, u_ref):
        front = jnp.concatenate([jnp.zeros((PAD, D_MODEL), f32), meta_ref[...]], axis=0)
        first = jnp.where(pl.program_id(0) == 0, front, x0_ref[...])
        h = jnp.concatenate([first, x1_ref[...], x2_ref[...]], axis=0)
        h_ref[...] = h
        u_ref[...] = _rms(h, g_ref[...]).astype(u_ref.dtype)

    def x_block(j):
        return pl.BlockSpec((BLOCK, D_MODEL),
                            lambda i: (jnp.clip(per_tile * i + j - FRONT // BLOCK, 0, last_block), 0))

    tile = pl.BlockSpec((tm, D_MODEL), lambda i: (i, 0))
    return pl.pallas_call(
        body, name=name, grid=(lp // tm,),
        in_specs=[x_block(0), x_block(1), x_block(2), _full(meta.shape), _full(g.shape)],
        out_specs=[tile, tile],
        out_shape=[jax.ShapeDtypeStruct((lp, D_MODEL), f32), jax.ShapeDtypeStruct((lp, D_MODEL), bf16)],
        compiler_params=_params(("parallel",)),
    )(x, x, x, meta, g)


def _input_norm_bwd(h0, g, du, dh1, *, name):
    lp = h0.shape[0]
    blocks = (lp - FRONT) // FRONT
    per_tile = max(n for n in (4, 3, 2, 1) if blocks % n == 0)
    ins = (h0, du, dh1)

    def body(*refs):
        tiles = [refs[k * per_tile:(k + 1) * per_tile] for k in range(len(ins))]
        front_refs = refs[len(ins) * per_tile:len(ins) * (per_tile + 1)]
        g_ref, dx_ref, front_ref, dg_ref = refs[len(ins) * (per_tile + 1):]

        def cotangents(h_ref, du_ref, dh1_ref):
            _, vjp = jax.vjp(lambda hv, gv: (_rms(hv, gv), hv), h_ref[...], g_ref[...])
            return vjp((du_ref[...].astype(f32), dh1_ref[...]))

        @pl.when(pl.program_id(0) == 0)
        def _():
            front_ref[...], dg_ref[...] = cotangents(*front_refs)

        dg = jnp.zeros(dg_ref.shape, f32)
        for j in range(per_tile):
            dh, dg_j = cotangents(*(t[j] for t in tiles))
            dx_ref[j * FRONT:(j + 1) * FRONT, :] = dh
            dg = dg + dg_j
        dg_ref[...] += dg

    def block(j):
        return pl.BlockSpec((FRONT, D_MODEL), lambda i: (per_tile * i + j + 1, 0))

    first = pl.BlockSpec((FRONT, D_MODEL), lambda i: (0, 0))
    return pl.pallas_call(
        body, name=name, grid=(blocks // per_tile,),
        in_specs=[block(j) for _ in ins for j in range(per_tile)] + [first] * len(ins) + [_full(g.shape)],
        out_specs=[pl.BlockSpec((per_tile * FRONT, D_MODEL), lambda i: (i, 0)), _full((FRONT, D_MODEL)),
                   _full(g.shape)],
        out_shape=[jax.ShapeDtypeStruct((lp - FRONT, D_MODEL), f32), jax.ShapeDtypeStruct((FRONT, D_MODEL), f32),
                   jax.ShapeDtypeStruct(g.shape, f32)],
        compiler_params=_params(("arbitrary",)),
    )(*(a for a in ins for _ in range(per_tile)), *ins, g)


def _local_step(x, target, meta, p, early_weights=None, late_weights=None, emit=None):
    emit = emit or (lambda group, grads: 0.0)
    seq = x.shape[0]
    lp = seq + FRONT
    cos_t, sin_t, swap = _rope_tables(lp)
    hsum = _head_sum_matrix(RWKV_DIM, RWKV_HEAD)
    hmean = hsum / RWKV_HEAD
    post_params = [p["ln_w"], p["ln_b"], p["r_k"], hmean]

    h0, u = _embed_norm(x, meta, p["norm_mix_g"], name="norm_mix")
    if early_weights is not None:
        p = {**p, **early_weights(u)}
    prep_params = [p["w0"], p["w2"], p["a0"], p["a2"], p["g2"], p["k_k"], p["k_a"], hsum]
    q, k, v, p_rkv, p_lora, gates = _proj_in(u, p["w_in_lr"], p["b_in"], [ATTN_PROJ, RKV_W, LORA_W, 2 * D_MODEL],
                                             (cos_t, sin_t, swap), name="proj_in", zero_rows_below=PAD)
    y_attn = _attention(q, k, v, p["sinks"], name="attention")

    mix_rkv, mix_lora = p["mix"][:, :RKV_W], p["mix"][:, RKV_W:]
    r_, lw_, k_, v_, a_, b_, g_ = _mixer_inputs([p_rkv, p_lora], [mix_rkv, mix_lora], prep_params,
                                                name="mixer_inputs")
    y_scan, states, inverses = _scan(r_, lw_, k_, v_, a_, b_, name="wkv_scan")
    (y_rwkv,) = _rowwise(_rwkv_post, [y_scan, r_, k_, v_, g_], post_params, [(RWKV_DIM, bf16)], name="rwkv_post")

    if late_weights is not None:
        p = {**p, **late_weights(y_rwkv)}
    br_a, br_r, merged = _branch_merge(y_attn, y_rwkv, p["w_br_attn_t"], p["w_br_rwkv_t"], gates, name="branch_merge")
    h1, f = _residual_norm(merged, p["w_o"], h0, p["norm_ffn_g"], name="out_proj")
    gate, up, act = _ffn_in(f, p["w_gate_t"], p["w_up_t"], name="ffn_in")
    h2 = _mm(act, p["w_down"], "nn", name="ffn_down", add=h1)

    loss8, dh2, d_final_g = _loss_head(h2, target, p["norm_final_g"], name="loss_head")
    dgate, dup = _ffn_in_bwd(dh2, p["w_down"], gate, up, name="ffn_in_bwd")
    d_w_down = _mm_tn(act, dh2, name="dw_down")
    d_w_gate_t = _mm_tn(dgate, f, name="dw_gate")
    d_w_up_t = _mm_tn(dup, f, name="dw_up")
    zero = emit("ffn", dict(w_down=d_w_down, w_gate_t=d_w_gate_t, w_up_t=d_w_up_t))
    dh1, d_ffn_g = _residual_norm_bwd([dgate, dup], [p["w_gate_t"], p["w_up_t"]], h1, p["norm_ffn_g"] + zero, dh2,
                                      name="norm_ffn_bwd")
    dgates, dbr_a, dbr_r, dy_attn, dy_rwkv = _branch_merge_bwd(
        dh1, p["w_o"], gates, br_a, br_r, p["w_br_attn_t"], p["w_br_rwkv_t"], name="branch_merge_bwd")
    d_w_o = _mm_tn(merged, dh1, name="dw_o")
    d_w_br_attn_t = _mm_tn(dbr_a, y_attn, name="dw_br_attn")
    d_w_br_rwkv_t = _mm_tn(dbr_r, y_rwkv, name="dw_br_rwkv")
    zero = emit("branch", dict(w_o=d_w_o, w_br_attn_t=d_w_br_attn_t, w_br_rwkv_t=d_w_br_rwkv_t))

    post_params = [p["ln_w"] + zero, p["ln_b"], p["r_k"], hmean]
    res = _rowwise_bwd(_rwkv_post, [y_scan, r_, k_, v_, g_], post_params, [[dy_rwkv]], name="rwkv_post_bwd",
                       diff_rows=[True] * 5, diff_params=[True, True, True, False])
    dy_scan, dr_p, dk_p, dv_p, dg_p, d_ln_w, d_ln_b, d_r_k = res
    dr_s, dlw_s, dk_s, dv_s, da_s, db_s = _scan_bwd(r_, lw_, k_, v_, a_, b_, states, inverses, dy_scan,
                                                    name="wkv_scan_bwd")
    res = _mixer_inputs_bwd([p_rkv, p_lora], [mix_rkv, mix_lora], prep_params,
                            [[dr_s, dr_p], [dlw_s], [dk_s, dk_p], [dv_s, dv_p], [da_s], [db_s], [dg_p]],
                            name="mixer_inputs_bwd")
    dp_rkv, dp_lora, d_mix_rkv, d_mix_lora, d_w0, d_w2, d_a0, d_a2, d_g2, d_k_k, d_k_a = res

    dq, dk, dv, dkm, dvm, d_sinks = _attention_bwd(q, k, v, p["sinks"], y_attn, dy_attn, name="attention_bwd")
    rest = jnp.zeros((lp - BLOCK, KV_W), f32)
    dkm, dvm = jnp.concatenate([dkm, rest], axis=0), jnp.concatenate([dvm, rest], axis=0)
    (dqkv,) = _rowwise(_attn_prep_transposed, [dq, dk, dkm, dv, dvm, cos_t, sin_t], [swap], [(ATTN_PROJ, bf16)],
                       name="attn_prep_bwd")

    d_w_qkv_t, db_qkv = _mm_tn(dqkv, u, name="dw_qkv", colsum=True)
    d_w_rkv_t, db_rkv = _mm_tn(dp_rkv, u, name="dw_rkv", colsum=True)
    d_w_lora_t, db_lora = _mm_tn(dp_lora, u, name="dw_lora", colsum=True)
    d_w_gates_t, db_gates = _mm_tn(dgates, u, name="dw_gates", colsum=True)
    d_w_in_t = jnp.concatenate([d_w_qkv_t, d_w_rkv_t, d_w_lora_t, d_w_gates_t], axis=0)
    zero = emit("input", dict(w_in_t=d_w_in_t, g2=d_g2, w2=d_w2, a2=d_a2))
    du = _proj_in_bwd([dqkv, dp_rkv, dp_lora, dgates], p["w_in_lr"], name="d_u")
    dx, d_front, d_mix_g = _input_norm_bwd(h0, p["norm_mix_g"] + zero, du, dh1, name="norm_mix_bwd")

    grads = dict(
        w_in_t=d_w_in_t,
        b_in=jnp.concatenate([db_qkv, db_rkv, db_lora, db_gates], axis=1),
        mix=jnp.concatenate([d_mix_rkv, d_mix_lora], axis=1),
        norm_mix_g=d_mix_g, sinks=d_sinks, w0=d_w0, w2=d_w2, a0=d_a0, a2=d_a2, g2=d_g2, k_k=d_k_k, k_a=d_k_a,
        r_k=d_r_k, ln_w=d_ln_w, ln_b=d_ln_b, w_br_attn_t=d_w_br_attn_t, w_br_rwkv_t=d_w_br_rwkv_t, w_o=d_w_o,
        norm_ffn_g=d_ffn_g, w_gate_t=d_w_gate_t, w_up_t=d_w_up_t, w_down=d_w_down, norm_final_g=d_final_g,
        meta=d_front[PAD:],
    )
    return loss8[0, 0], dx, grads


def _position():
    return lax.axis_index("x"), lax.axis_index("y"), lax.axis_index("c")


def _other_chips(x, y):
    return [(1 - x, y), (x, 1 - y), (1 - x, 1 - y)]


_HBM = pl.BlockSpec(memory_space=pltpu.HBM)
_SEM = pl.BlockSpec(memory_space=pltpu.SEMAPHORE)
_EFFECT = pltpu.SideEffectType.DATAFLOW_SIDE_EFFECTING


def _landing_zone(src, kind):
    shape = {"whole": (N_CHIPS,) + src.shape, "half": (2, N_CHIPS, src.shape[0], src.shape[1] // 2),
             "slab": (3,) + src.shape[1:], "sibling": src.shape}[kind]
    return lax.empty(shape, src.dtype)


def _copies_per_source(kind):
    return 1 if kind == "sibling" else 3


def _chip_copies(src_refs, land_refs, send_sems, recv_sems, kind):
    x, y, c = _position()
    if kind == "sibling":
        return [pltpu.make_async_remote_copy(
            src_ref=src, dst_ref=land, send_sem=send_sems.at[a], recv_sem=recv_sems.at[a],
            device_id=(x, y, 1 - c), device_id_type=MESH) for a, (src, land) in enumerate(zip(src_refs, land_refs))]
    copies = []
    for a, (src, land) in enumerate(zip(src_refs, land_refs)):
        for j, (px, py) in enumerate(_other_chips(x, y)):
            if kind == "whole":
                src_ref, dst_ref = src, land.at[2 * x + y]
            elif kind == "half":
                half = src.shape[1] // 2
                src_ref, dst_ref = src.at[:, pl.ds(pl.multiple_of(c * half, half), half)], land.at[c, 2 * x + y]
            else:
                src_ref, dst_ref = src.at[2 * px + py], land.at[j]
            copies.append(pltpu.make_async_remote_copy(
                src_ref=src_ref, dst_ref=dst_ref, send_sem=send_sems.at[3 * a + j], recv_sem=recv_sems.at[3 * a + j],
                device_id=(px, py, c), device_id_type=MESH))
    return copies


def _exchange_start(srcs, *, kind, name):
    n = len(srcs)
    lands = [_landing_zone(s, kind) for s in srcs]

    def body(*refs):
        for cp in _chip_copies(refs[:n], refs[n:2 * n], refs[2 * n], refs[2 * n + 1], kind):
            cp.start()
        refs[-1][...] = jnp.zeros_like(refs[-1])

    res = pl.pallas_call(
        body, name=name,
        out_shape=(pltpu.SemaphoreType.DMA((_copies_per_source(kind) * n,)),
                   pltpu.SemaphoreType.DMA((_copies_per_source(kind) * n,)),
                   *[pltpu.HBM(a.shape, a.dtype) for a in srcs + lands], jax.ShapeDtypeStruct((8, 128), f32)),
        in_specs=[_HBM] * (2 * n),
        out_specs=(_SEM, _SEM, *[_HBM] * (2 * n), pl.BlockSpec(memory_space=pltpu.VMEM)),
        input_output_aliases={i: 2 + i for i in range(2 * n)},
        compiler_params=pltpu.CompilerParams(has_side_effects=_EFFECT),
    )(*[pltpu.with_memory_space_constraint(a, pltpu.HBM) for a in srcs + lands])
    return res[0], res[1], list(res[2:2 + n]), list(res[2 + n:2 + 2 * n]), res[-1]


def _exchange_wait(handle, after, *, kind, name):
    send_sems, recv_sems, srcs, lands, _ = handle
    n = len(srcs)

    def body(*refs):
        for cp in _chip_copies(refs[:n], refs[n:2 * n], refs[2 * n], refs[2 * n + 1], kind):
            cp.wait_send()
            cp.wait_recv()

    res = pl.pallas_call(
        body, name=name,
        out_shape=tuple(pltpu.HBM(a.shape, a.dtype) for a in srcs + lands),
        in_specs=[_HBM] * (2 * n) + [_SEM, _SEM, pl.BlockSpec(memory_space=pl.ANY)],
        out_specs=tuple([_HBM] * (2 * n)),
        input_output_aliases={i: i for i in range(2 * n)},
        compiler_params=pltpu.CompilerParams(has_side_effects=_EFFECT),
    )(*srcs, *lands, send_sems, recv_sems, after)
    return list(res[:n]), list(res[n:])


def _sum_own_and_received(g, recv, *, name):
    _, r, w = g.shape
    tm = _tile(r)
    if g.dtype == bf16 and tm % 16:
        tm = r
    x, y, _ = _position()
    me = jnp.reshape(2 * x + y, (1,)).astype(jnp.int32)

    def body(me_ref, g_ref, r_ref, o_ref):
        o_ref[...] = (g_ref[0].astype(f32) + r_ref[0].astype(f32)) + (r_ref[1].astype(f32) + r_ref[2].astype(f32))

    return pl.pallas_call(
        body, name=name,
        grid_spec=pltpu.PrefetchScalarGridSpec(
            num_scalar_prefetch=1, grid=(r // tm,),
            in_specs=[pl.BlockSpec((1, tm, w), lambda i, me_ref: (me_ref[0], i, 0)),
                      pl.BlockSpec((3, tm, w), lambda i, me_ref: (0, i, 0))],
            out_specs=pl.BlockSpec((tm, w), lambda i, me_ref: (i, 0))),
        out_shape=jax.ShapeDtypeStruct((r, w), f32),
        compiler_params=_params(("parallel",)),
    )(me, g, recv)


def _swap_cores(arrs, *, name):
    n = len(arrs)

    def body(*refs):
        x, y, c = _position()
        copies = [pltpu.make_async_remote_copy(
            src_ref=refs[i], dst_ref=refs[n + i], send_sem=refs[2 * n].at[i], recv_sem=refs[2 * n + 1].at[i],
            device_id=(x, y, 1 - c), device_id_type=MESH) for i in range(n)]
        for cp in copies:
            cp.start()
        for cp in copies:
            cp.wait_recv()
        for cp in copies:
            cp.wait_send()

    return pl.pallas_call(
        body, name=name,
        in_specs=[pl.BlockSpec(memory_space=pl.ANY)] * n,
        out_specs=[pl.BlockSpec(memory_space=pl.ANY)] * n,
        out_shape=[jax.ShapeDtypeStruct(a.shape, a.dtype) for a in arrs],
        scratch_shapes=[pltpu.SemaphoreType.DMA((n,)), pltpu.SemaphoreType.DMA((n,))],
    )(*arrs)


def _swap_halves(zone, *, name):
    def body(z_ref, o_ref, send_sems, recv_sems):
        x, y, c = _position()
        mine = [pltpu.make_async_remote_copy(
            src_ref=o_ref.at[c, 2 * px + py], dst_ref=o_ref.at[c, 2 * px + py], send_sem=send_sems.at[j],
            recv_sem=recv_sems.at[j], device_id=(x, y, 1 - c), device_id_type=MESH)
            for j, (px, py) in enumerate(_other_chips(x, y))]
        for cp in mine:
            cp.start()
        for j, (px, py) in enumerate(_other_chips(x, y)):
            pltpu.make_async_remote_copy(
                src_ref=o_ref.at[c, 2 * px + py], dst_ref=o_ref.at[1 - c, 2 * px + py], send_sem=send_sems.at[j],
                recv_sem=recv_sems.at[j], device_id=(x, y, 1 - c), device_id_type=MESH).wait_recv()
        for cp in mine:
            cp.wait_send()

    return pl.pallas_call(
        body, name=name,
        in_specs=[pl.BlockSpec(memory_space=pl.ANY)], out_specs=pl.BlockSpec(memory_space=pl.ANY),
        out_shape=jax.ShapeDtypeStruct(zone.shape, zone.dtype), input_output_aliases={0: 0},
        scratch_shapes=[pltpu.SemaphoreType.DMA((3,)), pltpu.SemaphoreType.DMA((3,))],
    )(zone)


def _all_reduce_small(a, after, *, name):
    rows, w = a.shape

    def body(a_ref, after_ref, o_ref, buf, send_sems, recv_sems):
        x, y, c = _position()
        me = 4 * x + 2 * y + c
        buf[0] = a_ref[...]
        sends = []
        for rel in range(1, N_DEV):
            peer = ((1 - x) if rel & 4 else x, (1 - y) if rel & 2 else y, (1 - c) if rel & 1 else c)
            cp = pltpu.make_async_remote_copy(
                src_ref=a_ref, dst_ref=buf.at[rel], send_sem=send_sems.at[rel - 1], recv_sem=recv_sems.at[rel - 1],
                device_id=peer, device_id_type=MESH)
            cp.start()
            sends.append(cp)
        for cp in sends:
            cp.wait_recv()
        for cp in sends:
            cp.wait_send()
        acc = buf[jnp.bitwise_xor(me, 0)]
        for d in range(1, N_DEV):
            acc = acc + buf[jnp.bitwise_xor(me, d)]
        o_ref[...] = acc

    return pl.pallas_call(
        body, name=name,
        in_specs=[pl.BlockSpec(memory_space=pltpu.VMEM), pl.BlockSpec(memory_space=pl.ANY)],
        out_specs=pl.BlockSpec(memory_space=pltpu.VMEM),
        out_shape=jax.ShapeDtypeStruct((rows, w), f32),
        scratch_shapes=[pltpu.VMEM((N_DEV, rows, w), f32), pltpu.SemaphoreType.DMA((N_DEV - 1,)),
                        pltpu.SemaphoreType.DMA((N_DEV - 1,))],
    )(a, after)


def _adamw(w, g_parts, m, v, *, name, transposed=False):
    rows, cols = w.shape
    if transposed:
        tm = 256 if rows % 256 == 0 else rows
        g_spec = pl.BlockSpec((cols, tm), lambda i: (0, i))
    else:
        tm = _tile(rows, 256)
        g_spec = pl.BlockSpec((tm, cols), lambda i: (i, 0))
    n = len(g_parts)

    def body(*refs):
        w_ref, m_ref, v_ref = refs[0], refs[1 + n], refs[2 + n]
        g_ref, d_ref, nm_ref, nv_ref = refs[3 + n:]
        gv = refs[1][...]
        for part in refs[2:1 + n]:
            gv = gv + part[...]
        if transposed:
            gv = gv.T
        g_ref[...] = gv
        nm = ADAM_B1 * m_ref[...] + (1.0 - ADAM_B1) * gv
        nv = ADAM_B2 * v_ref[...] + (1.0 - ADAM_B2) * (gv * gv)
        m_hat = nm / (1.0 - ADAM_B1 ** ADAM_STEP)
        v_hat = nv / (1.0 - ADAM_B2 ** ADAM_STEP)
        d_ref[...] = -ADAM_LR * (m_hat / (jnp.sqrt(v_hat) + ADAM_EPS) + ADAM_WD * w_ref[...])
        nm_ref[...] = nm
        nv_ref[...] = nv

    spec = pl.BlockSpec((tm, cols), lambda i: (i, 0))
    shape = jax.ShapeDtypeStruct((rows, cols), f32)
    return pl.pallas_call(
        body, name=name, grid=(rows // tm,), in_specs=[spec] + [g_spec] * n + [spec] * 2,
        out_specs=[spec] * 4, out_shape=[shape] * 4,
        compiler_params=_params(("parallel",)),
    )(w, *g_parts, m, v)


def _pad_rows(a, rows):
    return jnp.concatenate([a, jnp.zeros((rows - a.shape[0], a.shape[1]), a.dtype)], axis=0) if rows > a.shape[0] else a


_SMALL = (("norm_mix_g", D_MODEL), ("b_in", D_IN), ("sinks", Q_HEADS), ("mix", RWKV_PROJ), ("w0", RWKV_DIM),
          ("a0", RWKV_DIM), ("k_k", RWKV_DIM), ("k_a", RWKV_DIM), ("r_k", RWKV_DIM), ("ln_w", RWKV_DIM),
          ("ln_b", RWKV_DIM), ("norm_ffn_g", D_MODEL), ("norm_final_g", D_MODEL))


def _pack_small(d):
    flat = jnp.concatenate([d[n].reshape(-1).astype(f32) for n, _ in _SMALL])
    return flat


def _unpack_small(flat):
    out, off = {}, 0
    for n, size in _SMALL:
        out[n] = flat[off:off + size]
        off += size
    return out


_SMALL_TOTAL = sum(s for _, s in _SMALL)


def kernel(x, meta_tokens, norm_mix_g, w_in, b_in, attn_sinks, rwkv_mix, rwkv_w0, rwkv_w2, rwkv_a0, rwkv_a2, rwkv_g2, rwkv_k_k, rwkv_k_a, rwkv_r_k, rwkv_ln_w, rwkv_ln_b, w_br_attn, w_br_rwkv, w_o, norm_ffn_g, w_ffn_gate, w_ffn_up, w_ffn_down, norm_final_g, loss_target, m_meta_tokens, m_norm_mix_g, m_w_in, m_b_in, m_attn_sinks, m_rwkv_mix, m_rwkv_w0, m_rwkv_w2, m_rwkv_a0, m_rwkv_a2, m_rwkv_g2, m_rwkv_k_k, m_rwkv_k_a, m_rwkv_r_k, m_rwkv_ln_w, m_rwkv_ln_b, m_w_br_attn, m_w_br_rwkv, m_w_o, m_norm_ffn_g, m_w_ffn_gate, m_w_ffn_up, m_w_ffn_down, m_norm_final_g, v_meta_tokens, v_norm_mix_g, v_w_in, v_b_in, v_attn_sinks, v_rwkv_mix, v_rwkv_w0, v_rwkv_w2, v_rwkv_a0, v_rwkv_a2, v_rwkv_g2, v_rwkv_k_k, v_rwkv_k_a, v_rwkv_r_k, v_rwkv_ln_w, v_rwkv_ln_b, v_w_br_attn, v_w_br_rwkv, v_w_o, v_norm_ffn_g, v_w_ffn_gate, v_w_ffn_up, v_w_ffn_down, v_norm_final_g):
    names = ("meta_tokens", "norm_mix_g", "w_in", "b_in", "attn_sinks", "rwkv_mix", "rwkv_w0", "rwkv_w2", "rwkv_a0",
             "rwkv_a2", "rwkv_g2", "rwkv_k_k", "rwkv_k_a", "rwkv_r_k", "rwkv_ln_w", "rwkv_ln_b", "w_br_attn",
             "w_br_rwkv", "w_o", "norm_ffn_g", "w_ffn_gate", "w_ffn_up", "w_ffn_down", "norm_final_g")
    w_all = dict(zip(names, (meta_tokens, norm_mix_g, w_in, b_in, attn_sinks, rwkv_mix, rwkv_w0, rwkv_w2, rwkv_a0,
                             rwkv_a2, rwkv_g2, rwkv_k_k, rwkv_k_a, rwkv_r_k, rwkv_ln_w, rwkv_ln_b, w_br_attn,
                             w_br_rwkv, w_o, norm_ffn_g, w_ffn_gate, w_ffn_up, w_ffn_down, norm_final_g)))
    m_all = dict(zip(names, (m_meta_tokens, m_norm_mix_g, m_w_in, m_b_in, m_attn_sinks, m_rwkv_mix, m_rwkv_w0,
                             m_rwkv_w2, m_rwkv_a0, m_rwkv_a2, m_rwkv_g2, m_rwkv_k_k, m_rwkv_k_a, m_rwkv_r_k,
                             m_rwkv_ln_w, m_rwkv_ln_b, m_w_br_attn, m_w_br_rwkv, m_w_o, m_norm_ffn_g, m_w_ffn_gate,
                             m_w_ffn_up, m_w_ffn_down, m_norm_final_g)))
    v_all = dict(zip(names, (v_meta_tokens, v_norm_mix_g, v_w_in, v_b_in, v_attn_sinks, v_rwkv_mix, v_rwkv_w0,
                             v_rwkv_w2, v_rwkv_a0, v_rwkv_a2, v_rwkv_g2, v_rwkv_k_k, v_rwkv_k_a, v_rwkv_r_k,
                             v_rwkv_ln_w, v_rwkv_ln_b, v_w_br_attn, v_w_br_rwkv, v_w_o, v_norm_ffn_g, v_w_ffn_gate,
                             v_w_ffn_up, v_w_ffn_down, v_norm_final_g)))
    cx, cy, _ = _position()
    chip = 2 * cx + cy

    t_of = dict(w_in_t="w_in", w_gate_t="w_ffn_gate", w_up_t="w_ffn_up", w_br_attn_t="w_br_attn",
                w_br_rwkv_t="w_br_rwkv", g2_t="rwkv_g2", w2_t="rwkv_w2", a2_t="rwkv_a2")
    plain_of = dict(w_down="w_ffn_down", w_o="w_o")
    meta_cols = meta_tokens.shape[1]

    def shard(k):
        return (w_all[t_of[k]][0].T if k in t_of else w_all[plain_of[k]][0]).astype(bf16)

    def whole(zone, own):
        return lax.dynamic_update_slice_in_dim(zone, own[None], chip, axis=0).reshape(-1, own.shape[-1])

    tiny = ("g2_t", "w2_t", "a2_t")
    late = ("w_gate_t", "w_up_t", "w_down", "w_o", "w_br_attn_t", "w_br_rwkv_t")
    w_in_own = shard("w_in_t")
    w_in_rows, w_in_cols = w_in_own.shape
    tiny_h = _exchange_start([shard(k) for k in tiny] + [meta_tokens], kind="whole", name="gather_tiny_start")
    w_in_h = _exchange_start([w_in_own + tiny_h[4][0, 0].astype(bf16)], kind="half", name="gather_w_in_start")
    behind = w_in_h[4][0, 0].astype(bf16)
    late_h = _exchange_start([shard(k) + behind for k in late], kind="whole", name="gather_late_start")
    own, zones = _exchange_wait(tiny_h, late_h[4], kind="whole", name="gather_tiny_wait")
    got = {k: whole(z, o) for k, z, o in zip(tiny, zones, own)}
    meta_full = whole(zones[-1], own[-1]).reshape(N_CHIPS, N_META, meta_cols).transpose(1, 0, 2).reshape(N_META, -1)
    p = dict(
        g2=got["g2_t"].T.astype(f32), w2=got["w2_t"].T.astype(f32), a2=got["a2_t"].T.astype(f32),
        b_in=b_in, sinks=attn_sinks, mix=rwkv_mix, w0=rwkv_w0, a0=rwkv_a0, k_k=rwkv_k_k, k_a=rwkv_k_a,
        r_k=rwkv_r_k.reshape(1, RWKV_DIM), ln_w=rwkv_ln_w, ln_b=rwkv_ln_b, norm_mix_g=norm_mix_g,
        norm_ffn_g=norm_ffn_g, norm_final_g=norm_final_g.reshape(1, D_MODEL),
    )

    def early_weights(after):
        own_h, zones_h = _exchange_wait(w_in_h, after, kind="half", name="gather_w_in_wait")
        zone = _swap_halves(zones_h[0], name="swap_w_in_halves")
        own_halves = own_h[0].reshape(w_in_rows, 2, w_in_cols // 2).transpose(1, 0, 2)[:, None]
        zone = lax.dynamic_update_slice(zone, own_halves, (0, chip, 0, 0))
        return dict(w_in_lr=zone.reshape(2, N_CHIPS * w_in_rows, w_in_cols // 2))

    def late_weights(after):
        own_l, zones_l = _exchange_wait(late_h, after, kind="whole", name="gather_late_wait")
        return {k: whole(z, o) for k, z, o in zip(late, zones_l, own_l)}

    started = {}

    def partial_sums(groups, after):
        parts = {}
        for group in groups:
            keys, handle = started[group]
            slabs, lands = _exchange_wait(handle, after, kind="slab", name="scatter_" + group + "_wait")
            parts.update({k: _sum_own_and_received(s, l, name="sum_chips_" + k) for k, s, l in zip(keys, slabs, lands)})
        return parts

    def emit(group, grads_):
        keys = list(grads_)
        slabs = []
        for k in keys:
            a = grads_[k].T if k in ("g2", "w2", "a2") else grads_[k]
            slabs.append(a.reshape(N_CHIPS, a.shape[0] // N_CHIPS, a.shape[1]))
        started[group] = (keys, _exchange_start(slabs, kind="slab", name="scatter_" + group + "_start"))
        zero = started[group][1][4]
        if group == "input":
            started["parts_a"] = partial_sums(("ffn", "branch"), zero)
            started["swap_a"] = _exchange_start(list(started["parts_a"].values()), kind="sibling",
                                                name="swap_cores_a_start")
            zero = started["swap_a"][4]
        return zero[0, 0]

    loss, dx, g = _local_step(x[0], loss_target[0], meta_full, p, early_weights, late_weights, emit)

    grads, delta, new_m, new_v = {}, {}, {}, {}
    in_grad_layout = ("w_in_t", "w_gate_t", "w_up_t")
    weight_of = {**t_of, **plain_of}

    def update(keys, mine, theirs):
        for k, part, other in zip(keys, mine, theirs):
            both = [part, other]
            k = k + "_t" if k in ("g2", "w2", "a2") else k
            n = weight_of[k]
            shape2 = w_all[n].shape[1:]
            w_, m_, v_ = (a.reshape(shape2) for a in (w_all[n], m_all[n], v_all[n]))
            if k in in_grad_layout:
                res = [t.T for t in _adamw(w_.T, both, m_.T, v_.T, name="adamw_" + n)]
            else:
                res = _adamw(w_, both, m_, v_, name="adamw_" + n, transposed=k in t_of)
            grads[n], delta[n], new_m[n], new_v[n] = (t.reshape(w_all[n].shape) for t in res)
        return delta[n]

    done = update(list(started["parts_a"]),
                  *_exchange_wait(started["swap_a"], dx, kind="sibling", name="swap_cores_a_wait"))
    small = jnp.concatenate([_pack_small(g), loss.reshape(1)])
    small_rows = -(-small.shape[0] // PACK_W)
    small = jnp.concatenate([small, jnp.zeros((small_rows * PACK_W - small.shape[0],), f32)]).reshape(small_rows, PACK_W)
    small_rows8 = -(-(small_rows + N_META) // 8) * 8
    reduced = _all_reduce_small(_pad_rows(jnp.concatenate([g["meta"], small], axis=0), small_rows8), done,
                                name="reduce_small")
    parts_b = partial_sums(("input",), reduced)
    update(list(parts_b), list(parts_b.values()), _swap_cores(list(parts_b.values()), name="swap_cores_b"))
    g_meta = lax.dynamic_slice_in_dim(reduced[:N_META], chip * meta_cols, meta_cols, axis=1)
    flat = reduced[N_META:N_META + small_rows].reshape(-1)
    g_small = _unpack_small(flat)
    loss_total = flat[_SMALL_TOTAL]

    small_of = dict(norm_mix_g="norm_mix_g", b_in="b_in", attn_sinks="sinks", rwkv_mix="mix", rwkv_w0="w0",
                    rwkv_a0="a0", rwkv_k_k="k_k", rwkv_k_a="k_a", rwkv_r_k="r_k", rwkv_ln_w="ln_w",
                    rwkv_ln_b="ln_b", norm_ffn_g="norm_ffn_g", norm_final_g="norm_final_g")
    grads["meta_tokens"] = g_meta
    for n, k in small_of.items():
        grads[n] = g_small[k].reshape(w_all[n].shape)

    rest = [n for n in names if n not in delta]

    def pack_rest(src):
        flat_ = jnp.concatenate([src[n].reshape(-1) for n in rest])
        rows_ = -(-flat_.shape[0] // (8 * PACK_W)) * 8
        return jnp.concatenate([flat_, jnp.ones((rows_ * PACK_W - flat_.shape[0],), f32)]).reshape(rows_, PACK_W)

    _, d_, m_, v_ = _adamw(pack_rest(w_all), [pack_rest(grads)], pack_rest(m_all), pack_rest(v_all),
                           name="adamw_small")
    off = 0
    for n in rest:
        size = w_all[n].size
        for dst, src in ((delta, d_), (new_m, m_), (new_v, v_)):
            dst[n] = src.reshape(-1)[off:off + size].reshape(w_all[n].shape)
        off += size

    return (loss_total, dx.reshape(x.shape), *[grads[n] for n in names], *[delta[n] for n in names],
            *[new_m[n] for n in names], *[new_v[n] for n in names])
```

```python
import math

import jax
import jax.numpy as jnp
import numpy as np
from jax import lax
from jax.experimental import pallas as pl
from jax.experimental.pallas import tpu as pltpu

f32 = jnp.float32
bf16 = jnp.bfloat16

D_MODEL = 1024
N_META = 16
HEAD_DIM = 64
Q_HEADS = 8
KV_HEADS = 2
GROUP = Q_HEADS // KV_HEADS
WINDOW = 128
BLOCK = 128
ROPE_THETA = 500000.0
ROPE_DIM = HEAD_DIM // 4
RWKV_HEADS = 8
RWKV_HEAD = 64
RWKV_DIM = RWKV_HEADS * RWKV_HEAD
DECAY_LORA = 64
AAA_LORA = 64
GATE_LORA = 160
LORA_W = DECAY_LORA + AAA_LORA + GATE_LORA
RWKV_LN_EPS = 64e-5
D_FF = 2816
Q_W = Q_HEADS * HEAD_DIM
KV_W = KV_HEADS * HEAD_DIM
ATTN_PROJ = Q_W + 2 * KV_W
RKV_W = 3 * RWKV_DIM
RWKV_PROJ = RKV_W + LORA_W
D_IN = ATTN_PROJ + RWKV_PROJ + 2 * D_MODEL
RMS_EPS = 1e-6
NEG_INF = -1e30
PAD = BLOCK - N_META
FRONT = PAD + N_META

ADAM_LR = 0.001
ADAM_B1 = 0.9
ADAM_B2 = 0.999
ADAM_EPS = 1e-08
ADAM_WD = 0.01
ADAM_STEP = 10

N_CHIPS = 4
N_DEV = 8
CHUNK = 128
VMEM_LIMIT = 56 * 1024 * 1024
MM_ROWS = 704
PACK_W = 1024
MESH = pl.DeviceIdType.MESH


def _tile(m, pref=384):
    for step in (16, 8):
        for t in range(min(m, pref) // step * step, 0, -step):
            if m % t == 0:
                return t
    return m


def _params(sem=None):
    return pltpu.CompilerParams(dimension_semantics=sem, vmem_limit_bytes=VMEM_LIMIT)


def _full(shape):
    nd = len(shape)
    return pl.BlockSpec(shape, lambda *_: (0,) * nd)


def _dot(a, b, dims="nn"):
    dn = {"nn": (((1,), (0,)), ((), ())), "nt": (((1,), (1,)), ((), ())), "tn": (((0,), (0,)), ((), ()))}[dims]
    return lax.dot_general(a.astype(bf16), b.astype(bf16), dn, preferred_element_type=f32)


def _two_pass(x, m, dims="nn"):
    x_hi = x.astype(bf16)
    x_lo = (x - x_hi.astype(f32)).astype(bf16)
    return _dot(x_hi, m, dims) + _dot(x_lo, m, dims)


@jax.custom_vjp
def _dot_const(x, m):
    return _two_pass(x, m)


def _dot_const_fwd(x, m):
    return _two_pass(x, m), m


def _dot_const_bwd(m, ct):
    return _two_pass(ct, m, "nt"), jnp.zeros_like(m)


_dot_const.defvjp(_dot_const_fwd, _dot_const_bwd)


def _two_pass_left(m, x, dims):
    x_hi = x.astype(bf16)
    x_lo = (x - x_hi.astype(f32)).astype(bf16)
    return _dot(m, x_hi, dims) + _dot(m, x_lo, dims)


@jax.custom_vjp
def _const_dot(m, x):
    return _two_pass_left(m, x, "nn")


def _const_dot_fwd(m, x):
    return _two_pass_left(m, x, "nn"), m


def _const_dot_bwd(m, ct):
    return jnp.zeros_like(m), _two_pass_left(m, ct, "tn")


_const_dot.defvjp(_const_dot_fwd, _const_dot_bwd)


def _mm(a, b, mode, *, name, out_dtype=f32, bias=None, add=None, zero_rows_below=0):
    m, _ = a.shape
    n = b.shape[1] if mode == "nn" else b.shape[0]
    tm = _tile(m, MM_ROWS)
    has_bias, has_add = bias is not None, add is not None

    def body(*refs):
        a_ref, b_ref = refs[0], refs[1]
        o_ref = refs[-1]
        acc = _dot(a_ref[...], b_ref[...], mode)
        k = 2
        if has_bias:
            acc = acc + refs[k][...]
            k += 1
        if zero_rows_below:
            rows = pl.program_id(0) * tm + lax.broadcasted_iota(jnp.int32, acc.shape, 0)
            acc = jnp.where(rows >= zero_rows_below, acc, 0.0)
        if has_add:
            acc = acc + refs[k][...].astype(f32)
        o_ref[...] = acc.astype(out_dtype)

    ins = [a, b]
    in_specs = [pl.BlockSpec((tm, a.shape[1]), lambda i: (i, 0)), _full(b.shape)]
    if has_bias:
        ins.append(bias)
        in_specs.append(_full(bias.shape))
    if has_add:
        ins.append(add)
        in_specs.append(pl.BlockSpec((tm, n), lambda i: (i, 0)))
    return pl.pallas_call(
        body, name=name, grid=(m // tm,), in_specs=in_specs,
        out_specs=pl.BlockSpec((tm, n), lambda i: (i, 0)),
        out_shape=jax.ShapeDtypeStruct((m, n), out_dtype),
        compiler_params=_params(("parallel",)),
    )(*ins)


def _pieces(widths):
    out, off = [], 0
    for w in widths:
        out.append((off, w))
        off += w
    return out


def _proj_in(a, w_lr, bias, widths, rope, *, name, zero_rows_below=0):
    m, kdim = a.shape
    half = kdim // 2
    tm = _tile(m, MM_ROWS)
    cos_t, sin_t, swap = rope
    out_widths = [Q_W, KV_W, KV_W] + list(widths[1:])

    def body(a_ref, w_ref, b_ref, cos_ref, sin_ref, swap_ref, *outs):
        a_l, a_r = a_ref[:, :half], a_ref[:, half:]
        for j, (off, width) in enumerate(_pieces(widths)):
            acc = _dot(a_l, w_ref[0, off:off + width, :], "nt") + _dot(a_r, w_ref[1, off:off + width, :], "nt")
            acc = acc + b_ref[:, off:off + width]
            if zero_rows_below:
                rows = pl.program_id(0) * tm + lax.broadcasted_iota(jnp.int32, acc.shape, 0)
                acc = jnp.where(rows >= zero_rows_below, acc, 0.0)
            if j == 0:
                qkv = acc.astype(bf16).astype(f32)
                for o_ref, val in zip(outs[:3], _attn_prep(qkv, cos_ref[...], sin_ref[...], swap_ref[...])):
                    o_ref[...] = val.astype(o_ref.dtype)
            else:
                outs[2 + j][...] = acc.astype(outs[2 + j].dtype)

    table = pl.BlockSpec((tm, HEAD_DIM), lambda i: (i, 0))
    return pl.pallas_call(
        body, name=name, grid=(m // tm,),
        in_specs=[pl.BlockSpec((tm, kdim), lambda i: (i, 0)), _full(w_lr.shape), _full(bias.shape), table, table,
                  _full(swap.shape)],
        out_specs=[pl.BlockSpec((tm, w), lambda i: (i, 0)) for w in out_widths],
        out_shape=[jax.ShapeDtypeStruct((m, w), bf16) for w in out_widths],
        compiler_params=_params(("parallel",)),
    )(a, w_lr, bias, cos_t, sin_t, swap)


def _proj_in_bwd(d_list, w_lr, *, name):
    m = d_list[0].shape[0]
    half = w_lr.shape[2]
    widths = [d.shape[1] for d in d_list]
    tm = _tile(m, MM_ROWS)

    def body(*refs):
        w_ref, o_ref = refs[-2], refs[-1]
        for side in range(2):
            acc = None
            for (off, width), d_ref in zip(_pieces(widths), refs):
                term = _dot(d_ref[...], w_ref[side, off:off + width, :])
                acc = term if acc is None else acc + term
            o_ref[:, side * half:(side + 1) * half] = acc.astype(o_ref.dtype)

    return pl.pallas_call(
        body, name=name, grid=(m // tm,),
        in_specs=[pl.BlockSpec((tm, w), lambda i: (i, 0)) for w in widths] + [_full(w_lr.shape)],
        out_specs=pl.BlockSpec((tm, 2 * half), lambda i: (i, 0)),
        out_shape=jax.ShapeDtypeStruct((m, 2 * half), bf16),
        compiler_params=_params(("parallel",)),
    )(*d_list, w_lr)


def _residual_norm(a, w, res, g, *, name):
    m, d = res.shape
    tm = _tile(m, MM_ROWS)

    def body(a_ref, w_ref, r_ref, g_ref, h_ref, n_ref):
        h = _dot(a_ref[...], w_ref[...]) + r_ref[...]
        h_ref[...] = h
        n_ref[...] = _rms(h, g_ref[...]).astype(n_ref.dtype)

    tile = pl.BlockSpec((tm, d), lambda i: (i, 0))
    return pl.pallas_call(
        body, name=name, grid=(m // tm,),
        in_specs=[pl.BlockSpec((tm, a.shape[1]), lambda i: (i, 0)), _full(w.shape), tile, _full(g.shape)],
        out_specs=[tile, tile],
        out_shape=[jax.ShapeDtypeStruct((m, d), f32), jax.ShapeDtypeStruct((m, d), bf16)],
        compiler_params=_params(("parallel",)),
    )(a, w, res, g)


def _residual_norm_bwd(d_list, w_list, h, g, dh_out, *, name):
    m, d = h.shape
    k = len(d_list)
    tm = _tile(m)

    def body(*refs):
        h_ref, g_ref, dho_ref, dh_ref, dg_ref = refs[2 * k:]
        dn = _dot(refs[0][...], refs[k][...])
        for i in range(1, k):
            dn = dn + _dot(refs[i][...], refs[k + i][...])
        _, vjp = jax.vjp(lambda hv, gv: (_rms(hv, gv), hv), h_ref[...], g_ref[...])
        dh, dg = vjp((dn, dho_ref[...]))
        dh_ref[...] = dh

        @pl.when(pl.program_id(0) == 0)
        def _():
            dg_ref[...] = jnp.zeros_like(dg_ref)

        dg_ref[...] += dg

    tile = pl.BlockSpec((tm, d), lambda i: (i, 0))
    return pl.pallas_call(
        body, name=name, grid=(m // tm,),
        in_specs=[pl.BlockSpec((tm, a.shape[1]), lambda i: (i, 0)) for a in d_list] + [_full(w.shape) for w in w_list]
        + [tile, _full(g.shape), tile],
        out_specs=[tile, _full(g.shape)],
        out_shape=[jax.ShapeDtypeStruct((m, d), f32), jax.ShapeDtypeStruct(g.shape, f32)],
        compiler_params=_params(("arbitrary",)),
    )(*d_list, *w_list, h, g, dh_out)


def _mm_tn(a, b, *, name, colsum=False, out_dtype=bf16, into=None):
    r, m = a.shape
    n = b.shape[1]
    tr = _tile(r, 1408)
    tmo = m
    for cand in (1408, 1024, 768, 512):
        if m > 1024 and m % cand == 0:
            tmo = cand
            break
    steps = r // tr

    rows, offset, target = into or (m, 0, None)

    def body(a_ref, b_ref, *rest):
        o_ref, rest = (rest[0], rest[1:]) if target is None else (rest[1], rest[2:])
        acc = rest[-1]
        i = pl.program_id(1)

        @pl.when(i == 0)
        def _():
            acc[...] = jnp.zeros_like(acc)
            if colsum:
                rest[0][...] = jnp.zeros_like(rest[0])

        acc[...] += _dot(a_ref[...], b_ref[...], "tn")
        if colsum:
            rest[0][...] += jnp.sum(a_ref[...].astype(f32), axis=0, keepdims=True)

        @pl.when(i == steps - 1)
        def _():
            o_ref[...] = acc[...].astype(out_dtype)

    out_shape = [jax.ShapeDtypeStruct((rows, n), out_dtype)]
    if offset % tmo == 0:
        out_specs = [pl.BlockSpec((tmo, n), lambda j, i: (offset // tmo + j, 0))]
    else:
        out_specs = [pl.BlockSpec((pl.Element(tmo), pl.Element(n)), lambda j, i: (pl.multiple_of(offset + j * tmo, math.gcd(offset, tmo)), 0))]
    if colsum:
        out_shape.append(jax.ShapeDtypeStruct((1, m), f32))
        out_specs.append(pl.BlockSpec((1, tmo), lambda j, i: (0, j)))
    in_specs = [pl.BlockSpec((tr, tmo), lambda j, i: (i, j)), pl.BlockSpec((tr, n), lambda j, i: (i, 0))]
    res = pl.pallas_call(
        body, name=name, grid=(m // tmo, steps),
        in_specs=in_specs + ([] if target is None else [pl.BlockSpec(memory_space=pl.ANY)]),
        out_specs=out_specs, out_shape=out_shape,
        scratch_shapes=[pltpu.VMEM((tmo, n), f32)],
        input_output_aliases={} if target is None else {2: 0},
        compiler_params=_params(("parallel", "arbitrary")),
    )(a, b, *([] if target is None else [target]))
    return res if colsum else res[0]


def _rowwise(fn, rows, params, outs, *, name, tm=None):
    m = rows[0].shape[0]
    tm = tm or _tile(m, MM_ROWS)
    nr, npar = len(rows), len(params)

    def body(*refs):
        vals = [r[...] for r in refs[:nr + npar]]
        res = fn(*vals)
        for o_ref, v in zip(refs[nr + npar:], res):
            o_ref[...] = v.astype(o_ref.dtype)

    return pl.pallas_call(
        body, name=name, grid=(m // tm,),
        in_specs=[pl.BlockSpec((tm, r.shape[1]), lambda i: (i, 0)) for r in rows] + [_full(p.shape) for p in params],
        out_specs=[pl.BlockSpec((tm, w), lambda i: (i, 0)) for w, _ in outs],
        out_shape=[jax.ShapeDtypeStruct((m, w), dt) for w, dt in outs],
        compiler_params=_params(("parallel",)),
    )(*rows, *params)


def _rowwise_bwd(fn, rows, params, cts, *, name, diff_rows, diff_params, tm=None, zero_rows_below=0, out_dtypes=None):
    m = rows[0].shape[0]
    tm = tm or _tile(m)
    nr, npar = len(rows), len(params)
    d_idx = [i for i in range(nr) if diff_rows[i]]
    p_idx = [i for i in range(npar) if diff_params[i]]
    out_dtypes = out_dtypes or [f32] * len(d_idx)
    flat_cts = [c for group in cts for c in group]
    n_ct = len(flat_cts)

    def body(*refs):
        vals = [r[...] for r in refs[:nr + npar]]
        ct_refs = refs[nr + npar:nr + npar + n_ct]
        out_refs = refs[nr + npar + n_ct:]
        ct_vals, k = [], 0
        for group in cts:
            acc = ct_refs[k][...].astype(f32)
            for extra in range(1, len(group)):
                acc = acc + ct_refs[k + extra][...].astype(f32)
            k += len(group)
            if zero_rows_below:
                rr = pl.program_id(0) * tm + lax.broadcasted_iota(jnp.int32, acc.shape, 0)
                acc = jnp.where(rr >= zero_rows_below, acc, 0.0)
            ct_vals.append(acc)

        def g(*dargs):
            full = list(vals)
            for pos, i in enumerate(d_idx):
                full[i] = dargs[pos]
            for pos, i in enumerate(p_idx):
                full[nr + i] = dargs[len(d_idx) + pos]
            return tuple(fn(*full))

        _, vjp = jax.vjp(g, *[vals[i].astype(f32) for i in d_idx], *[vals[nr + i] for i in p_idx])
        grads = vjp(tuple(ct_vals))
        for pos in range(len(d_idx)):
            out_refs[pos][...] = grads[pos].astype(out_refs[pos].dtype)
        first = pl.program_id(0) == 0
        for pos in range(len(p_idx)):
            o_ref = out_refs[len(d_idx) + pos]

            @pl.when(first)
            def _(o_ref=o_ref):
                o_ref[...] = jnp.zeros_like(o_ref)

            o_ref[...] += grads[len(d_idx) + pos]

    return pl.pallas_call(
        body, name=name, grid=(m // tm,),
        in_specs=[pl.BlockSpec((tm, r.shape[1]), lambda i: (i, 0)) for r in rows] + [_full(p.shape) for p in params]
        + [pl.BlockSpec((tm, c.shape[1]), lambda i: (i, 0)) for c in flat_cts],
        out_specs=[pl.BlockSpec((tm, rows[i].shape[1]), lambda i_: (i_, 0)) for i in d_idx]
        + [_full(params[i].shape) for i in p_idx],
        out_shape=[jax.ShapeDtypeStruct(rows[i].shape, dt) for i, dt in zip(d_idx, out_dtypes)]
        + [jax.ShapeDtypeStruct(params[i].shape, f32) for i in p_idx],
        compiler_params=_params(("arbitrary",)),
    )(*rows, *params, *flat_cts)


def _rms(x, g):
    return x * lax.rsqrt(jnp.mean(x * x, axis=-1, keepdims=True) + RMS_EPS) * g


def _head_sum_matrix(width, head):
    idx = jnp.arange(width) // head
    return (idx[:, None] == idx[None, :]).astype(f32)


def _rope_tables(lp):
    half = ROPE_DIM // 2
    pos = (np.arange(lp) - PAD).astype(np.float32)
    inv_freq = np.power(np.float32(ROPE_THETA), -np.arange(half, dtype=np.float32) * np.float32(2.0 / ROPE_DIM))
    ang = pos[:, None] * inv_freq[None, :].astype(np.float32)
    cos, sin = np.cos(ang), np.sin(ang)
    ones = np.ones((lp, HEAD_DIM - ROPE_DIM), np.float32)
    cos_t = np.concatenate([cos, cos, ones], axis=1)
    sin_t = np.concatenate([-sin, sin, 0.0 * ones], axis=1)
    i = np.arange(HEAD_DIM)
    src = np.where(i < half, i + half, np.where(i < ROPE_DIM, i - half, i))
    swap = ((i[:, None] == src[None, :]) & (i[None, :] < ROPE_DIM)).astype(np.float32)
    return jnp.asarray(cos_t, f32), jnp.asarray(sin_t, f32), jnp.asarray(swap, f32)


def _attn_prep(qkv, cos_t, sin_t, swap):
    outs = []
    for h in range(Q_HEADS + KV_HEADS):
        t = qkv[:, h * HEAD_DIM:(h + 1) * HEAD_DIM]
        outs.append(t * cos_t + _dot_const(t, swap) * sin_t)
    q = jnp.concatenate(outs[:Q_HEADS], axis=1)
    k = jnp.concatenate(outs[Q_HEADS:], axis=1)
    return q, k, qkv[:, Q_W + KV_W:]


def _attn_prep_transposed(dq, dk, dk_meta, dv, dv_meta, cos_t, sin_t, swap):
    parts = []
    for d, heads in ((dq.astype(f32), Q_HEADS), (dk.astype(f32) + dk_meta.astype(f32), KV_HEADS)):
        for h in range(heads):
            t = d[:, h * HEAD_DIM:(h + 1) * HEAD_DIM]
            parts.append(t * cos_t + _two_pass(t * sin_t, swap, "nt"))
    return (jnp.concatenate(parts + [dv.astype(f32) + dv_meta.astype(f32)], axis=1),)


def _softplus(z):
    return jnp.maximum(z, 0.0) + jnp.log1p(jnp.exp(-jnp.abs(z)))


def _rwkv_prep(rkv, lora, w0, w2, a0, a2, g2, k_k, k_a, hsum):
    r = rkv[:, :RWKV_DIM]
    k = rkv[:, RWKV_DIM:2 * RWKV_DIM]
    v = rkv[:, 2 * RWKV_DIM:]
    dw = lora[:, :DECAY_LORA]
    da = lora[:, DECAY_LORA:DECAY_LORA + AAA_LORA]
    dg = lora[:, DECAY_LORA + AAA_LORA:]
    w = -_softplus(-(w0 + _dot(jnp.tanh(dw), w2))) - 0.5
    a = jax.nn.sigmoid(a0 + _dot(da, a2))
    g = _dot(jax.nn.sigmoid(dg), g2)
    kk = k * k_k
    kk = kk * lax.rsqrt(jnp.maximum(_dot_const(kk * kk, hsum), 1e-24))
    k = k * (1.0 + (a - 1.0) * k_a)
    log_decay = -jnp.exp(w)
    return r, log_decay, k, v, -kk, kk * a, g


def _rwkv_post(y, r, k, v, g, ln_w, ln_b, r_k, hmean):
    hsum = hmean * RWKV_HEAD
    mean = _dot_const(y, hmean)
    yc = y - mean
    var = _dot_const(yc * yc, hmean)
    yn = yc * lax.rsqrt(var + RWKV_LN_EPS) * ln_w + ln_b
    bonus = _dot_const(r * k * r_k, hsum) * v
    return ((yn + bonus) * g,)


def _merge(gates, br_a, br_r):
    sg = jax.nn.sigmoid(gates)
    return (sg[:, :D_MODEL] * br_a + sg[:, D_MODEL:] * br_r,)


def _swiglu(gate, up):
    return (jax.nn.silu(gate) * up,)


def _ffn_in(f, w_gate_t, w_up_t, *, name):
    m, d = f.shape
    n = w_gate_t.shape[0]
    tm = _tile(m)

    def body(f_ref, wg_ref, wu_ref, g_ref, u_ref, a_ref):
        g = _dot(f_ref[...], wg_ref[...], "nt")
        u = _dot(f_ref[...], wu_ref[...], "nt")
        g_ref[...] = g.astype(g_ref.dtype)
        u_ref[...] = u.astype(u_ref.dtype)
        a_ref[...] = _swiglu(g, u)[0].astype(a_ref.dtype)

    spec = pl.BlockSpec((tm, n), lambda i: (i, 0))
    return pl.pallas_call(
        body, name=name, grid=(m // tm,),
        in_specs=[pl.BlockSpec((tm, d), lambda i: (i, 0)), _full(w_gate_t.shape), _full(w_up_t.shape)],
        out_specs=[spec] * 3, out_shape=[jax.ShapeDtypeStruct((m, n), bf16)] * 3,
        compiler_params=_params(("parallel",)),
    )(f, w_gate_t, w_up_t)


def _branch_merge(y_attn, y_rwkv, w_attn_t, w_rwkv_t, gates, *, name):
    m = y_attn.shape[0]
    tm = _tile(m, MM_ROWS)

    def body(ya_ref, yr_ref, wa_ref, wr_ref, g_ref, a_ref, r_ref, o_ref):
        br_a = _dot(ya_ref[...], wa_ref[...], "nt")
        br_r = _dot(yr_ref[...], wr_ref[...], "nt")
        a_ref[...] = br_a.astype(a_ref.dtype)
        r_ref[...] = br_r.astype(r_ref.dtype)
        o_ref[...] = _merge(g_ref[...].astype(f32), br_a, br_r)[0].astype(o_ref.dtype)

    rows = lambda a: pl.BlockSpec((tm, a.shape[1]), lambda i: (i, 0))
    spec = pl.BlockSpec((tm, D_MODEL), lambda i: (i, 0))
    return pl.pallas_call(
        body, name=name, grid=(m // tm,),
        in_specs=[rows(y_attn), rows(y_rwkv), _full(w_attn_t.shape), _full(w_rwkv_t.shape), rows(gates)],
        out_specs=[spec] * 3, out_shape=[jax.ShapeDtypeStruct((m, D_MODEL), bf16)] * 3,
        compiler_params=_params(("parallel",)),
    )(y_attn, y_rwkv, w_attn_t, w_rwkv_t, gates)


def _branch_merge_bwd(dh, w_o, gates, br_a, br_r, w_attn_t, w_rwkv_t, *, name):
    m = dh.shape[0]
    tm = _tile(m, MM_ROWS)

    def body(dh_ref, w_ref, g_ref, a_ref, r_ref, wa_ref, wr_ref, dg_ref, da_ref, dr_ref, dya_ref, dyr_ref):
        dmerged = _dot(dh_ref[...], w_ref[...], "nt")
        _, vjp = jax.vjp(lambda g, a, r: _merge(g, a, r)[0], g_ref[...].astype(f32), a_ref[...].astype(f32),
                         r_ref[...].astype(f32))
        dg, da, dr = vjp(dmerged)
        dg_ref[...] = dg.astype(dg_ref.dtype)
        da_ref[...] = da.astype(da_ref.dtype)
        dr_ref[...] = dr.astype(dr_ref.dtype)
        dya_ref[...] = _dot(da, wa_ref[...])
        dyr_ref[...] = _dot(dr, wr_ref[...])

    rows = lambda a: pl.BlockSpec((tm, a.shape[1]), lambda i: (i, 0))
    mixer = pl.BlockSpec((tm, w_attn_t.shape[1]), lambda i: (i, 0))
    return pl.pallas_call(
        body, name=name, grid=(m // tm,),
        in_specs=[rows(dh), _full(w_o.shape), rows(gates), rows(br_a), rows(br_r), _full(w_attn_t.shape),
                  _full(w_rwkv_t.shape)],
        out_specs=[rows(gates), rows(br_a), rows(br_r), mixer, mixer],
        out_shape=[jax.ShapeDtypeStruct(gates.shape, bf16), jax.ShapeDtypeStruct(br_a.shape, bf16),
                   jax.ShapeDtypeStruct(br_r.shape, bf16), jax.ShapeDtypeStruct((m, w_attn_t.shape[1]), f32),
                   jax.ShapeDtypeStruct((m, w_rwkv_t.shape[1]), f32)],
        compiler_params=_params(("parallel",)),
    )(dh, w_o, gates, br_a, br_r, w_attn_t, w_rwkv_t)


def _ffn_in_bwd(dh, w_down, gate, up, *, name):
    m, d = dh.shape
    n = w_down.shape[0]
    tm = _tile(m)

    def body(dh_ref, w_ref, g_ref, u_ref, dg_ref, du_ref):
        dact = _dot(dh_ref[...], w_ref[...], "nt")
        _, vjp = jax.vjp(lambda a, b: _swiglu(a, b)[0], g_ref[...].astype(f32), u_ref[...].astype(f32))
        dg, du = vjp(dact)
        dg_ref[...] = dg.astype(dg_ref.dtype)
        du_ref[...] = du.astype(du_ref.dtype)

    spec = pl.BlockSpec((tm, n), lambda i: (i, 0))
    return pl.pallas_call(
        body, name=name, grid=(m // tm,),
        in_specs=[pl.BlockSpec((tm, d), lambda i: (i, 0)), _full(w_down.shape), spec, spec],
        out_specs=[spec] * 2, out_shape=[jax.ShapeDtypeStruct((m, n), bf16)] * 2,
        compiler_params=_params(("parallel",)),
    )(dh, w_down, gate, up)


HALO = 16


def _previous_rows(x, before_ref, first_tile):
    rows = lax.broadcasted_iota(jnp.int32, x.shape, 0)
    last = jnp.where(first_tile, 0.0, before_ref[HALO - 1:HALO, :].astype(f32))
    return jnp.where(rows == 0, last, pltpu.roll(x, 1, axis=0))


def _mixer_inputs(ps, mixes, params, *, name):
    m = ps[0].shape[0]
    tm = _tile(m)
    sub = tm // HALO
    n_par = len(params)

    def body(*refs):
        first = pl.program_id(0) == 0
        pf = []
        for k in range(2):
            x = refs[k][...].astype(f32)
            pf.append(x + (_previous_rows(x, refs[2 + k], first) - x) * refs[4 + k][...])
        res = _rwkv_prep(*pf, *[ref[...] for ref in refs[6:6 + n_par]])
        for o_ref, val in zip(refs[6 + n_par:], res):
            o_ref[...] = val

    tile = lambda a: pl.BlockSpec((tm, a.shape[1]), lambda i: (i, 0))
    before = lambda a: pl.BlockSpec((HALO, a.shape[1]), lambda i: (jnp.maximum(i * sub - 1, 0), 0))
    out = pl.BlockSpec((tm, RWKV_DIM), lambda i: (i, 0))
    return pl.pallas_call(
        body, name=name, grid=(m // tm,),
        in_specs=[tile(a) for a in ps] + [before(a) for a in ps] + [_full(a.shape) for a in mixes + params],
        out_specs=[out] * 7, out_shape=[jax.ShapeDtypeStruct((m, RWKV_DIM), f32)] * 7,
        compiler_params=_params(("parallel",)),
    )(*ps, *ps, *mixes, *params)


def _mixer_inputs_bwd(ps, mixes, params, cts, *, name):
    m = ps[0].shape[0]
    tm = _tile(m)
    sub = tm // HALO
    nt = m // tm
    n_par = len(params)
    flat_cts = [c for group in cts for c in group]
    n_ct = len(flat_cts)

    def body(*refs):
        i = pl.program_id(0)
        tile_index = nt - 1 - i
        ct_refs = refs[6 + n_par:6 + n_par + n_ct]
        dp_refs = refs[6 + n_par + n_ct:8 + n_par + n_ct]
        dmix_refs = refs[8 + n_par + n_ct:10 + n_par + n_ct]
        dpar_refs = refs[10 + n_par + n_ct:9 + 2 * n_par + n_ct]
        carries = refs[9 + 2 * n_par + n_ct:]
        rows1 = tile_index * tm + lax.broadcasted_iota(jnp.int32, (tm, 1), 0)
        live = rows1 >= PAD

        @pl.when(i == 0)
        def _():
            for ref in (*dmix_refs, *dpar_refs, *carries):
                ref[...] = jnp.zeros_like(ref)

        xs, prevs, pf = [], [], []
        for k in range(2):
            x = refs[k][...].astype(f32)
            xp = _previous_rows(x, refs[2 + k], tile_index == 0)
            xs.append(x)
            prevs.append(xp)
            pf.append(x + (xp - x) * refs[4 + k][...])
        ct_vals, pos = [], 0
        for group in cts:
            acc = ct_refs[pos][...].astype(f32)
            for extra in range(1, len(group)):
                acc = acc + ct_refs[pos + extra][...].astype(f32)
            pos += len(group)
            ct_vals.append(jnp.where(live, acc, 0.0))
        par_vals = [ref[...] for ref in refs[6:6 + n_par]]
        _, vjp = jax.vjp(lambda *args: _rwkv_prep(*args, par_vals[-1]), *pf, *par_vals[:-1])
        g = vjp(tuple(ct_vals))
        for k in range(2):
            dpf = g[k]
            mixv = refs[4 + k][...]
            dm = dpf * mixv
            rows = lax.broadcasted_iota(jnp.int32, dm.shape, 0)
            dm_next = jnp.where(rows == tm - 1, carries[k][...], pltpu.roll(dm, tm - 1, axis=0))
            dp_refs[k][...] = jnp.where(live, dpf - dm + dm_next, 0.0).astype(dp_refs[k].dtype)
            carries[k][...] = dm[0:1, :]
            dmix_refs[k][...] += jnp.sum(dpf * (prevs[k] - xs[k]), axis=0, keepdims=True)
        for ref, val in zip(dpar_refs, g[2:]):
            ref[...] += val

    tile = lambda a: pl.BlockSpec((tm, a.shape[1]), lambda i: (nt - 1 - i, 0))
    before = lambda a: pl.BlockSpec((HALO, a.shape[1]), lambda i: (jnp.maximum((nt - 1 - i) * sub - 1, 0), 0))
    return pl.pallas_call(
        body, name=name, grid=(nt,),
        in_specs=[tile(a) for a in ps] + [before(a) for a in ps] + [_full(a.shape) for a in mixes + params]
        + [tile(c) for c in flat_cts],
        out_specs=[tile(a) for a in ps] + [_full(a.shape) for a in mixes + params[:-1]],
        out_shape=[jax.ShapeDtypeStruct(a.shape, bf16) for a in ps]
        + [jax.ShapeDtypeStruct(a.shape, f32) for a in mixes + params[:-1]],
        scratch_shapes=[pltpu.VMEM((1, a.shape[1]), f32) for a in ps],
        compiler_params=_params(("arbitrary",)),
    )(*ps, *ps, *mixes, *params, *flat_cts)


def _attn_masks(blk):
    qi = lax.broadcasted_iota(jnp.int32, (BLOCK, BLOCK), 0)
    ki = lax.broadcasted_iota(jnp.int32, (BLOCK, BLOCK), 1)
    qpos = blk * BLOCK + qi - PAD
    kpos_c = blk * BLOCK + ki - PAD
    kpos_p = kpos_c - BLOCK
    kpos_m = ki - PAD

    def band(kpos):
        return (kpos >= N_META) & (kpos <= qpos) & (qpos - kpos < WINDOW)

    return band(kpos_p), band(kpos_c), (kpos_m >= 0) & (kpos_m <= qpos)


def _attn_probs(qs, k3s, sink, oks):
    s = [[jnp.where(ok, _dot(qh, kx, "nt"), NEG_INF) for kx, ok in zip(k3, oks)] for qh, k3 in zip(qs, k3s)]
    mx = [jnp.maximum(jnp.maximum(jnp.max(t[0], -1, keepdims=True), jnp.max(t[1], -1, keepdims=True)),
                      jnp.maximum(jnp.max(t[2], -1, keepdims=True), sk)) for t, sk in zip(s, sink)]
    e = [[jnp.exp(tx - m) for tx in t] for t, m in zip(s, mx)]
    e_sink = [jnp.exp(sk - m) for sk, m in zip(sink, mx)]
    inv = [1.0 / (jnp.sum(t[0], -1, keepdims=True) + jnp.sum(t[1], -1, keepdims=True)
                  + jnp.sum(t[2], -1, keepdims=True) + es) for t, es in zip(e, e_sink)]
    return [[tx * i for tx in t] for t, i in zip(e, inv)], [es * i for es, i in zip(e_sink, inv)]


def _head_cols(i):
    return slice(i * HEAD_DIM, (i + 1) * HEAD_DIM)


def _attn_operands(refs):
    q_ref, kp_ref, kc_ref, km_ref, vp_ref, vc_ref, vm_ref, s_ref = refs
    qs = [q_ref[:, _head_cols(i)] * (HEAD_DIM ** -0.5) for i in range(Q_HEADS)]
    k3 = [[ref[:, _head_cols(h)] for ref in (kp_ref, kc_ref, km_ref)] for h in range(KV_HEADS)]
    v3 = [[ref[:, _head_cols(h)] for ref in (vp_ref, vc_ref, vm_ref)] for h in range(KV_HEADS)]
    return (qs, [k3[i // GROUP] for i in range(Q_HEADS)], [v3[i // GROUP] for i in range(Q_HEADS)],
            [s_ref[:, i:i + 1] for i in range(Q_HEADS)])


def _attention(q, k, v, sinks, *, name):
    lp = q.shape[0]
    nb = lp // BLOCK
    prev = lambda i: (jnp.maximum(i - 1, 0), 0)
    cur = lambda i: (i, 0)
    meta = lambda i: (0, 0)
    kv = lambda index: pl.BlockSpec((BLOCK, KV_W), index)

    def body(*refs):
        o_ref = refs[-1]
        qs, k3s, v3s, sink = _attn_operands(refs[:-1])
        p, _ = _attn_probs(qs, k3s, sink, _attn_masks(pl.program_id(0)))
        out = [_dot(ph[0], v3[0]) + _dot(ph[1], v3[1]) + _dot(ph[2], v3[2]) for ph, v3 in zip(p, v3s)]
        for i in range(Q_HEADS):
            o_ref[:, _head_cols(i)] = out[i].astype(o_ref.dtype)

    return pl.pallas_call(
        body, name=name, grid=(nb,),
        in_specs=[pl.BlockSpec((BLOCK, Q_W), cur), kv(prev), kv(cur), kv(meta), kv(prev), kv(cur), kv(meta),
                  _full((1, Q_HEADS))],
        out_specs=pl.BlockSpec((BLOCK, Q_W), cur),
        out_shape=jax.ShapeDtypeStruct((lp, Q_W), bf16),
        compiler_params=_params(("parallel",)),
    )(q, k, k, k, v, v, v, sinks)


def _attention_bwd(q, k, v, sinks, out, do, *, name):
    lp = q.shape[0]
    nb = lp // BLOCK
    cur = lambda n: (jnp.minimum(n, nb - 1), 0)
    prev = lambda n: (jnp.maximum(jnp.minimum(n, nb - 1) - 1, 0), 0)
    behind = lambda n: (jnp.maximum(n - 1, 0), 0)
    meta = lambda n: (0, 0)
    kv = lambda index: pl.BlockSpec((BLOCK, KV_W), index)
    scale = HEAD_DIM ** -0.5

    def body(*refs):
        ins, fwd_ref, do_ref = refs[:8], refs[8], refs[9]
        dq_ref, dk_ref, dv_ref, dkm_ref, dvm_ref, ds_ref, carry_k, carry_v = refs[10:]
        n = pl.program_id(0)

        @pl.when(n == 0)
        def _():
            for ref in (dkm_ref, dvm_ref, ds_ref, carry_k, carry_v):
                ref[...] = jnp.zeros_like(ref)

        @pl.when(n < nb)
        def _():
            qs, k3s, v3s, sink = _attn_operands(ins)
            do = [do_ref[:, _head_cols(i)] for i in range(Q_HEADS)]
            p, p_sink = _attn_probs(qs, k3s, sink, _attn_masks(n))
            delta = [jnp.sum(d * fwd_ref[:, _head_cols(i)].astype(f32), -1, keepdims=True) for i, d in enumerate(do)]
            dp = [[_dot(d, vx, "nt") for vx in v3] for d, v3 in zip(do, v3s)]
            ds = [[px * (dx - dl) for px, dx in zip(ph, dh)] for ph, dh, dl in zip(p, dp, delta)]
            dq = [_dot(dsh[0], k3[0]) + _dot(dsh[1], k3[1]) + _dot(dsh[2], k3[2]) for dsh, k3 in zip(ds, k3s)]
            for i in range(Q_HEADS):
                dq_ref[:, _head_cols(i)] = dq[i] * scale
                ds_ref[:, i:i + 1] -= jnp.sum(p_sink[i] * delta[i], axis=0, keepdims=True)
            for h in range(KV_HEADS):
                group = slice(h * GROUP, (h + 1) * GROUP)
                q_all = jnp.concatenate(qs[group], axis=0)
                do_all = jnp.concatenate(do[group], axis=0)
                dk3 = [_dot(jnp.concatenate([dsh[x] for dsh in ds[group]], axis=0), q_all, "tn") for x in range(3)]
                dv3 = [_dot(jnp.concatenate([ph[x] for ph in p[group]], axis=0), do_all, "tn") for x in range(3)]
                hs = _head_cols(h)
                for out_ref, carry, meta_ref, d3 in ((dk_ref, carry_k, dkm_ref, dk3),
                                                     (dv_ref, carry_v, dvm_ref, dv3)):
                    out_ref[:, hs] = carry[:, hs] + d3[0]
                    carry[:, hs] = d3[1]
                    meta_ref[:, hs] += d3[2]

        @pl.when(n == nb)
        def _():
            dk_ref[...] = carry_k[...]
            dv_ref[...] = carry_v[...]

    kv_shape = jax.ShapeDtypeStruct((lp, KV_W), f32)
    one_shape = jax.ShapeDtypeStruct((BLOCK, KV_W), f32)
    return pl.pallas_call(
        body, name=name, grid=(nb + 1,),
        in_specs=[pl.BlockSpec((BLOCK, Q_W), cur), kv(prev), kv(cur), kv(meta), kv(prev), kv(cur), kv(meta),
                  _full((1, Q_HEADS)), pl.BlockSpec((BLOCK, Q_W), cur), pl.BlockSpec((BLOCK, Q_W), cur)],
        out_specs=[pl.BlockSpec((BLOCK, Q_W), cur), kv(behind), kv(behind), kv(meta), kv(meta),
                   _full((1, Q_HEADS))],
        out_shape=[jax.ShapeDtypeStruct((lp, Q_W), f32), kv_shape, kv_shape, one_shape, one_shape,
                   jax.ShapeDtypeStruct((1, Q_HEADS), f32)],
        scratch_shapes=[pltpu.VMEM((BLOCK, KV_W), f32), pltpu.VMEM((BLOCK, KV_W), f32)],
        compiler_params=_params(("arbitrary",)),
    )(q, k, k, k, v, v, v, sinks, out, do)


@jax.custom_vjp
def _known_inverse(l, x):
    return x


def _known_inverse_fwd(l, x):
    return x, x


def _known_inverse_bwd(x, ct):
    return _dot(_dot(x, ct, "tn"), x, "nt"), jnp.zeros_like(x)


_known_inverse.defvjp(_known_inverse_fwd, _known_inverse_bwd)


@jax.custom_vjp
def _decayed(x, c):
    return (x * jnp.exp(c)).astype(bf16).astype(f32)


def _decayed_fwd(x, c):
    e = jnp.exp(c)
    out = (x * e).astype(bf16).astype(f32)
    return out, (e, out)


def _decayed_bwd(res, ct):
    e, out = res
    return ct * e, ct * out


_decayed.defvjp(_decayed_fwd, _decayed_bwd)


@jax.custom_vjp
def _pair(x, y):
    return _dot(x, y, "nt")


def _pair_fwd(x, y):
    return _dot(x, y, "nt"), (x, y)


def _pair_bwd(res, ct):
    x, y = res
    hi = ct.astype(bf16)
    lo = (ct - hi.astype(f32)).astype(bf16)
    return _dot(hi, y) + _dot(lo, y), _dot(hi, x, "tn") + _dot(lo, x, "tn")


_pair.defvjp(_pair_fwd, _pair_bwd)


def _scan_chunk(s0, r, lw, k, v, a, b, inv=None):
    t = r[0].shape[0]
    ii = lax.broadcasted_iota(jnp.int32, (t, t), 0)
    jj = lax.broadcasted_iota(jnp.int32, (t, t), 1)
    incl = jj <= ii
    strict = jj < ii
    tri = incl.astype(f32)
    eye = jnp.where(ii == jj, 1.0, 0.0)
    cl = [_const_dot(tri, x) for x in lw]
    mid = [c[t // 2 - 1:t // 2, :] for c in cl]
    s0 = [s * jnp.exp(m) for s, m in zip(s0, mid)]
    cl = [c - m for c, m in zip(cl, mid)]
    rt = [_decayed(x, c) for x, c in zip(r, cl)]
    at = [_decayed(x, c - l) for x, c, l in zip(a, cl, lw)]
    bt = [_decayed(x, -c) for x, c in zip(b, cl)]
    kt = [_decayed(x, -c) for x, c in zip(k, cl)]
    l_ab = [jnp.where(strict, _pair(x, y), 0.0) for x, y in zip(at, bt)]
    l_ak = [jnp.where(strict, _pair(x, y), 0.0) for x, y in zip(at, kt)]
    r_b = [jnp.where(incl, _pair(x, y), 0.0) for x, y in zip(rt, bt)]
    r_k = [jnp.where(incl, _pair(x, y), 0.0) for x, y in zip(rt, kt)]
    if inv is None:
        inv = [eye + x for x in l_ab]
        pw = l_ab
        for _ in range(int(math.log2(t)) - 1):
            pw = [_dot(x, x) for x in pw]
            inv = [x + _dot(x, y) for x, y in zip(inv, pw)]
    else:
        inv = [_known_inverse(x, y) for x, y in zip(l_ab, inv)]
    rhs = [_dot(x, s, "nt") + _dot(m, y) for x, s, m, y in zip(at, s0, l_ak, v)]
    u = [_dot(x, y) for x, y in zip(inv, rhs)]
    y_s = [_dot(x, s, "nt") for x, s in zip(rt, s0)]
    y = [ys + _dot(m, uu) + _dot(n, vv) for ys, m, uu, n, vv in zip(y_s, r_b, u, r_k, v)]
    grow = [s + _dot(uu, x, "tn") + _dot(vv, z, "tn") for s, uu, x, vv, z in zip(s0, u, bt, v, kt)]
    s1 = [g * jnp.exp(c[t - 1:t, :]) for g, c in zip(grow, cl)]
    return y, s1, inv


def _head_rows(h):
    return slice(h * RWKV_HEAD, (h + 1) * RWKV_HEAD)


def _per_head(ref):
    return [ref[:, _head_rows(h)] for h in range(RWKV_HEADS)]


def _scan(r, lw, k, v, a, b, *, name):
    lp = r.shape[0]
    nc = lp // CHUNK
    row = pl.BlockSpec((CHUNK, RWKV_DIM), lambda c: (c, 0))

    def body(r_ref, lw_ref, k_ref, v_ref, a_ref, b_ref, y_ref, s_ref, inv_ref, state):
        @pl.when(pl.program_id(0) == 0)
        def _():
            state[...] = jnp.zeros_like(state)

        s_ref[...] = state[...]
        s0 = [state[_head_rows(h), :] for h in range(RWKV_HEADS)]
        y, s1, inv = _scan_chunk(s0, *[_per_head(ref) for ref in (r_ref, lw_ref, k_ref, v_ref, a_ref, b_ref)])
        for h in range(RWKV_HEADS):
            y_ref[:, _head_rows(h)] = y[h]
            state[_head_rows(h), :] = s1[h]
            inv_ref[h * CHUNK:(h + 1) * CHUNK, :] = inv[h].astype(inv_ref.dtype)

    return pl.pallas_call(
        body, name=name, grid=(nc,), in_specs=[row] * 6,
        out_specs=[row, pl.BlockSpec((RWKV_DIM, RWKV_HEAD), lambda c: (c, 0)),
                   pl.BlockSpec((RWKV_HEADS * CHUNK, CHUNK), lambda c: (c, 0))],
        out_shape=[jax.ShapeDtypeStruct((lp, RWKV_DIM), f32), jax.ShapeDtypeStruct((nc * RWKV_DIM, RWKV_HEAD), f32),
                   jax.ShapeDtypeStruct((nc * RWKV_HEADS * CHUNK, CHUNK), bf16)],
        scratch_shapes=[pltpu.VMEM((RWKV_DIM, RWKV_HEAD), f32)],
        compiler_params=_params(("arbitrary",)),
    )(r, lw, k, v, a, b)


def _scan_bwd(r, lw, k, v, a, b, states, inverses, dy, *, name):
    lp = r.shape[0]
    nc = lp // CHUNK
    back = lambda c: (nc - 1 - c, 0)
    row = pl.BlockSpec((CHUNK, RWKV_DIM), back)

    def body(r_ref, lw_ref, k_ref, v_ref, a_ref, b_ref, s_ref, inv_ref, dy_ref,
             dr_ref, dlw_ref, dk_ref, dv_ref, da_ref, db_ref, dstate):
        @pl.when(pl.program_id(0) == 0)
        def _():
            dstate[...] = jnp.zeros_like(dstate)

        outs = (dr_ref, dlw_ref, dk_ref, dv_ref, da_ref, db_ref)
        s0 = [s_ref[_head_rows(h), :] for h in range(RWKV_HEADS)]
        inv = [inv_ref[h * CHUNK:(h + 1) * CHUNK, :].astype(f32) for h in range(RWKV_HEADS)]
        _, vjp = jax.vjp(lambda *args: _scan_chunk(*args, inv=inv)[:2], s0,
                         *[_per_head(ref) for ref in (r_ref, lw_ref, k_ref, v_ref, a_ref, b_ref)])
        g = vjp((_per_head(dy_ref), [dstate[_head_rows(h), :] for h in range(RWKV_HEADS)]))
        for h in range(RWKV_HEADS):
            dstate[_head_rows(h), :] = g[0][h]
            for o_ref, gv in zip(outs, g[1:]):
                o_ref[:, _head_rows(h)] = gv[h]

    shape = jax.ShapeDtypeStruct((lp, RWKV_DIM), f32)
    return pl.pallas_call(
        body, name=name, grid=(nc,),
        in_specs=[row] * 6 + [pl.BlockSpec((RWKV_DIM, RWKV_HEAD), back),
                              pl.BlockSpec((RWKV_HEADS * CHUNK, CHUNK), back), row],
        out_specs=[row] * 6, out_shape=[shape] * 6,
        scratch_shapes=[pltpu.VMEM((RWKV_DIM, RWKV_HEAD), f32)],
        compiler_params=_params(("arbitrary",)),
    )(r, lw, k, v, a, b, states, inverses, dy)


def _loss_head(h2, target, g_final, *, name):
    lp = h2.shape[0]
    per_tile = 3
    tm = per_tile * BLOCK
    last_block = (lp - FRONT) // BLOCK - 1

    def body(h_ref, t0_ref, t1_ref, t2_ref, g_ref, loss_ref, dh_ref, dg_ref):
        i = pl.program_id(0)
        target_rows = jnp.concatenate([t0_ref[...], t1_ref[...], t2_ref[...]], axis=0)
        real = i * tm + lax.broadcasted_iota(jnp.int32, (tm, 1), 0) >= FRONT

        def tile_loss(hv, gv):
            err = _rms(hv, gv) - target_rows
            return 0.5 * jnp.sum(jnp.where(real, jnp.mean(err * err, axis=-1, keepdims=True), 0.0))

        loss, (dh, dg) = jax.value_and_grad(tile_loss, argnums=(0, 1))(h_ref[...], g_ref[...])

        @pl.when(i == 0)
        def _():
            loss_ref[...] = jnp.zeros_like(loss_ref)
            dg_ref[...] = jnp.zeros_like(dg_ref)

        loss_ref[...] += jnp.full(loss_ref.shape, loss, f32)
        dg_ref[...] += dg
        dh_ref[...] = dh

    def target_block(j):
        return pl.BlockSpec((BLOCK, D_MODEL),
                            lambda i: (jnp.clip(per_tile * i + j - FRONT // BLOCK, 0, last_block), 0))

    return pl.pallas_call(
        body, name=name, grid=(lp // tm,),
        in_specs=[pl.BlockSpec((tm, D_MODEL), lambda i: (i, 0)), target_block(0), target_block(1), target_block(2),
                  _full(g_final.shape)],
        out_specs=[_full((8, 128)), pl.BlockSpec((tm, D_MODEL), lambda i: (i, 0)), _full(g_final.shape)],
        out_shape=[jax.ShapeDtypeStruct((8, 128), f32), jax.ShapeDtypeStruct((lp, D_MODEL), f32),
                   jax.ShapeDtypeStruct(g_final.shape, f32)],
        compiler_params=_params(("arbitrary",)),
    )(h2, target, target, target, g_final)


def _embed_norm(x, meta, g, *, name):
    seq = x.shape[0]
    lp = seq + FRONT
    per_tile = 3
    tm = per_tile * BLOCK
    last_block = seq // BLOCK - 1

    def body(x0_ref, x1_ref, x2_ref, meta_ref, g_ref, h_ref, u_ref):
        front = jnp.concatenate([jnp.zeros((PAD, D_MODEL), f32), meta_ref[...]], axis=0)
        first = jnp.where(pl.program_id(0) == 0, front, x0_ref[...])
        h = jnp.concatenate([first, x1_ref[...], x2_ref[...]], axis=0)
        h_ref[...] = h
        u_ref[...] = _rms(h, g_ref[...]).astype(u_ref.dtype)

    def x_block(j):
        return pl.BlockSpec((BLOCK, D_MODEL),
                            lambda i: (jnp.clip(per_tile * i + j - FRONT // BLOCK, 0, last_block), 0))

    tile = pl.BlockSpec((tm, D_MODEL), lambda i: (i, 0))
    return pl.pallas_call(
        body, name=name, grid=(lp // tm,),
        in_specs=[x_block(0), x_block(1), x_block(2), _full(meta.shape), _full(g.shape)],
        out_specs=[tile, tile],
        out_shape=[jax.ShapeDtypeStruct((lp, D_MODEL), f32), jax.ShapeDtypeStruct((lp, D_MODEL), bf16)],
        compiler_params=_params(("parallel",)),
    )(x, x, x, meta, g)


def _input_norm_bwd(h0, g, du, dh1, *, name):
    lp = h0.shape[0]
    blocks = (lp - FRONT) // FRONT
    per_tile = max(n for n in (4, 3, 2, 1) if blocks % n == 0)
    ins = (h0, du, dh1)

    def body(*refs):
        tiles = [refs[k * per_tile:(k + 1) * per_tile] for k in range(len(ins))]
        front_refs = refs[len(ins) * per_tile:len(ins) * (per_tile + 1)]
        g_ref, dx_ref, front_ref, dg_ref = refs[len(ins) * (per_tile + 1):]

        def cotangents(h_ref, du_ref, dh1_ref):
            _, vjp = jax.vjp(lambda hv, gv: (_rms(hv, gv), hv), h_ref[...], g_ref[...])
            return vjp((du_ref[...].astype(f32), dh1_ref[...]))

        @pl.when(pl.program_id(0) == 0)
        def _():
            front_ref[...], dg_ref[...] = cotangents(*front_refs)

        dg = jnp.zeros(dg_ref.shape, f32)
        for j in range(per_tile):
            dh, dg_j = cotangents(*(t[j] for t in tiles))
            dx_ref[j * FRONT:(j + 1) * FRONT, :] = dh
            dg = dg + dg_j
        dg_ref[...] += dg

    def block(j):
        return pl.BlockSpec((FRONT, D_MODEL), lambda i: (per_tile * i + j + 1, 0))

    first = pl.BlockSpec((FRONT, D_MODEL), lambda i: (0, 0))
    return pl.pallas_call(
        body, name=name, grid=(blocks // per_tile,),
        in_specs=[block(j) for _ in ins for j in range(per_tile)] + [first] * len(ins) + [_full(g.shape)],
        out_specs=[pl.BlockSpec((per_tile * FRONT, D_MODEL), lambda i: (i, 0)), _full((FRONT, D_MODEL)),
                   _full(g.shape)],
        out_shape=[jax.ShapeDtypeStruct((lp - FRONT, D_MODEL), f32), jax.ShapeDtypeStruct((FRONT, D_MODEL), f32),
                   jax.ShapeDtypeStruct(g.shape, f32)],
        compiler_params=_params(("arbitrary",)),
    )(*(a for a in ins for _ in range(per_tile)), *ins, g)


def _local_step(x, target, meta, p, early_weights=None, late_weights=None, emit=None):
    emit = emit or (lambda group, grads: 0.0)
    seq = x.shape[0]
    lp = seq + FRONT
    cos_t, sin_t, swap = _rope_tables(lp)
    hsum = _head_sum_matrix(RWKV_DIM, RWKV_HEAD)
    hmean = hsum / RWKV_HEAD
    post_params = [p["ln_w"], p["ln_b"], p["r_k"], hmean]

    h0, u = _embed_norm(x, meta, p["norm_mix_g"], name="norm_mix")
    if early_weights is not None:
        p = {**p, **early_weights(u)}
    prep_params = [p["w0"], p["w2"], p["a0"], p["a2"], p["g2"], p["k_k"], p["k_a"], hsum]
    in_widths = [ATTN_PROJ, RKV_W, LORA_W, 2 * D_MODEL]
    q, k, v, p_rkv, p_lora, gates = _proj_in(u, p["w_in_lr"], p["b_in"], in_widths,
                                             (cos_t, sin_t, swap), name="proj_in", zero_rows_below=PAD)
    y_attn = _attention(q, k, v, p["sinks"], name="attention")

    mix_rkv, mix_lora = p["mix"][:, :RKV_W], p["mix"][:, RKV_W:]
    r_, lw_, k_, v_, a_, b_, g_ = _mixer_inputs([p_rkv, p_lora], [mix_rkv, mix_lora], prep_params,
                                                name="mixer_inputs")
    y_scan, states, inverses = _scan(r_, lw_, k_, v_, a_, b_, name="wkv_scan")
    (y_rwkv,) = _rowwise(_rwkv_post, [y_scan, r_, k_, v_, g_], post_params, [(RWKV_DIM, bf16)], name="rwkv_post")

    if late_weights is not None:
        p = {**p, **late_weights(y_rwkv)}
    br_a, br_r, merged = _branch_merge(y_attn, y_rwkv, p["w_br_attn_t"], p["w_br_rwkv_t"], gates, name="branch_merge")
    h1, f = _residual_norm(merged, p["w_o"], h0, p["norm_ffn_g"], name="out_proj")
    gate, up, act = _ffn_in(f, p["w_gate_t"], p["w_up_t"], name="ffn_in")
    h2 = _mm(act, p["w_down"], "nn", name="ffn_down", add=h1)

    loss8, dh2, d_final_g = _loss_head(h2, target, p["norm_final_g"], name="loss_head")
    dgate, dup = _ffn_in_bwd(dh2, p["w_down"], gate, up, name="ffn_in_bwd")
    d_w_down = _mm_tn(act, dh2, name="dw_down")
    d_w_gate_t = _mm_tn(dgate, f, name="dw_gate")
    d_w_up_t = _mm_tn(dup, f, name="dw_up")
    zero = emit("ffn", dict(w_down=d_w_down, w_gate_t=d_w_gate_t, w_up_t=d_w_up_t))
    dh1, d_ffn_g = _residual_norm_bwd([dgate, dup], [p["w_gate_t"], p["w_up_t"]], h1, p["norm_ffn_g"] + zero, dh2,
                                      name="norm_ffn_bwd")
    dgates, dbr_a, dbr_r, dy_attn, dy_rwkv = _branch_merge_bwd(
        dh1, p["w_o"], gates, br_a, br_r, p["w_br_attn_t"], p["w_br_rwkv_t"], name="branch_merge_bwd")
    d_w_o = _mm_tn(merged, dh1, name="dw_o")
    d_w_br_attn_t = _mm_tn(dbr_a, y_attn, name="dw_br_attn")
    d_w_br_rwkv_t = _mm_tn(dbr_r, y_rwkv, name="dw_br_rwkv")
    zero = emit("branch", dict(w_o=d_w_o, w_br_attn_t=d_w_br_attn_t, w_br_rwkv_t=d_w_br_rwkv_t))

    post_params = [p["ln_w"] + zero, p["ln_b"], p["r_k"], hmean]
    res = _rowwise_bwd(_rwkv_post, [y_scan, r_, k_, v_, g_], post_params, [[dy_rwkv]], name="rwkv_post_bwd",
                       diff_rows=[True] * 5, diff_params=[True, True, True, False])
    dy_scan, dr_p, dk_p, dv_p, dg_p, d_ln_w, d_ln_b, d_r_k = res
    dr_s, dlw_s, dk_s, dv_s, da_s, db_s = _scan_bwd(r_, lw_, k_, v_, a_, b_, states, inverses, dy_scan,
                                                    name="wkv_scan_bwd")
    res = _mixer_inputs_bwd([p_rkv, p_lora], [mix_rkv, mix_lora], prep_params,
                            [[dr_s, dr_p], [dlw_s], [dk_s, dk_p], [dv_s, dv_p], [da_s], [db_s], [dg_p]],
                            name="mixer_inputs_bwd")
    dp_rkv, dp_lora, d_mix_rkv, d_mix_lora, d_w0, d_w2, d_a0, d_a2, d_g2, d_k_k, d_k_a = res

    dq, dk, dv, dkm, dvm, d_sinks = _attention_bwd(q, k, v, p["sinks"], y_attn, dy_attn, name="attention_bwd")
    rest = jnp.zeros((lp - BLOCK, KV_W), f32)
    dkm, dvm = jnp.concatenate([dkm, rest], axis=0), jnp.concatenate([dvm, rest], axis=0)
    (dqkv,) = _rowwise(_attn_prep_transposed, [dq, dk, dkm, dv, dvm, cos_t, sin_t], [swap], [(ATTN_PROJ, bf16)],
                       name="attn_prep_bwd")

    in_rows, (at_qkv, at_rkv, at_lora, at_gates) = p["w_in_lr"].shape[1], [off for off, _ in _pieces(in_widths)]
    d_w_in_t, db_gates = _mm_tn(dgates, u, name="dw_gates", colsum=True, into=(in_rows, at_gates, None))
    d_w_in_t, db_rkv = _mm_tn(dp_rkv, u, name="dw_rkv", colsum=True, into=(in_rows, at_rkv, d_w_in_t))
    d_w_in_t, db_lora = _mm_tn(dp_lora, u, name="dw_lora", colsum=True, into=(in_rows, at_lora, d_w_in_t))
    d_w_in_t, db_qkv = _mm_tn(dqkv, u, name="dw_qkv", colsum=True, into=(in_rows, at_qkv, d_w_in_t))
    zero = emit("input", dict(w_in_t=d_w_in_t, g2=d_g2, w2=d_w2, a2=d_a2))
    du = _proj_in_bwd([dqkv, dp_rkv, dp_lora, dgates], p["w_in_lr"], name="d_u")
    dx, d_front, d_mix_g = _input_norm_bwd(h0, p["norm_mix_g"] + zero, du, dh1, name="norm_mix_bwd")

    grads = dict(
        w_in_t=d_w_in_t,
        b_in=jnp.concatenate([db_qkv, db_rkv, db_lora, db_gates], axis=1),
        mix=jnp.concatenate([d_mix_rkv, d_mix_lora], axis=1),
        norm_mix_g=d_mix_g, sinks=d_sinks, w0=d_w0, w2=d_w2, a0=d_a0, a2=d_a2, g2=d_g2, k_k=d_k_k, k_a=d_k_a,
        r_k=d_r_k, ln_w=d_ln_w, ln_b=d_ln_b, w_br_attn_t=d_w_br_attn_t, w_br_rwkv_t=d_w_br_rwkv_t, w_o=d_w_o,
        norm_ffn_g=d_ffn_g, w_gate_t=d_w_gate_t, w_up_t=d_w_up_t, w_down=d_w_down, norm_final_g=d_final_g,
        meta=d_front[PAD:],
    )
    return loss8[0, 0], dx, grads


def _position():
    return lax.axis_index("x"), lax.axis_index("y"), lax.axis_index("c")


def _other_chips(x, y):
    return [(1 - x, y), (x, 1 - y), (1 - x, 1 - y)]


_HBM = pl.BlockSpec(memory_space=pltpu.HBM)
_SEM = pl.BlockSpec(memory_space=pltpu.SEMAPHORE)
_EFFECT = pltpu.SideEffectType.DATAFLOW_SIDE_EFFECTING


def _landing_zone(src, kind):
    shape = {"whole": (N_CHIPS,) + src.shape, "half": (2, N_CHIPS, src.shape[0], src.shape[1] // 2),
             "slab": (3,) + src.shape[1:], "sibling": src.shape}[kind]
    return lax.empty(shape, src.dtype)


def _copies_per_source(kind):
    return 1 if kind == "sibling" else 3


def _chip_copies(src_refs, land_refs, send_sems, recv_sems, kind):
    x, y, c = _position()
    if kind == "sibling":
        return [pltpu.make_async_remote_copy(
            src_ref=src, dst_ref=land, send_sem=send_sems.at[a], recv_sem=recv_sems.at[a],
            device_id=(x, y, 1 - c), device_id_type=MESH) for a, (src, land) in enumerate(zip(src_refs, land_refs))]
    copies = []
    for a, (src, land) in enumerate(zip(src_refs, land_refs)):
        for j, (px, py) in enumerate(_other_chips(x, y)):
            if kind == "whole":
                src_ref, dst_ref = src, land.at[2 * x + y]
            elif kind == "half":
                half = src.shape[1] // 2
                src_ref, dst_ref = src.at[:, pl.ds(pl.multiple_of(c * half, half), half)], land.at[c, 2 * x + y]
            else:
                src_ref, dst_ref = src.at[2 * px + py], land.at[j]
            copies.append(pltpu.make_async_remote_copy(
                src_ref=src_ref, dst_ref=dst_ref, send_sem=send_sems.at[3 * a + j], recv_sem=recv_sems.at[3 * a + j],
                device_id=(px, py, c), device_id_type=MESH))
    return copies


def _exchange_start(srcs, *, kind, name):
    n = len(srcs)
    lands = [_landing_zone(s, kind) for s in srcs]

    def body(*refs):
        for cp in _chip_copies(refs[:n], refs[n:2 * n], refs[2 * n], refs[2 * n + 1], kind):
            cp.start()
        refs[-1][...] = jnp.zeros_like(refs[-1])

    res = pl.pallas_call(
        body, name=name,
        out_shape=(pltpu.SemaphoreType.DMA((_copies_per_source(kind) * n,)),
                   pltpu.SemaphoreType.DMA((_copies_per_source(kind) * n,)),
                   *[pltpu.HBM(a.shape, a.dtype) for a in srcs + lands], jax.ShapeDtypeStruct((8, 128), f32)),
        in_specs=[_HBM] * (2 * n),
        out_specs=(_SEM, _SEM, *[_HBM] * (2 * n), pl.BlockSpec(memory_space=pltpu.VMEM)),
        input_output_aliases={i: 2 + i for i in range(2 * n)},
        compiler_params=pltpu.CompilerParams(has_side_effects=_EFFECT),
    )(*[pltpu.with_memory_space_constraint(a, pltpu.HBM) for a in srcs + lands])
    return res[0], res[1], list(res[2:2 + n]), list(res[2 + n:2 + 2 * n]), res[-1]


def _exchange_wait(handle, after, *, kind, name):
    send_sems, recv_sems, srcs, lands, _ = handle
    n = len(srcs)

    def body(*refs):
        for cp in _chip_copies(refs[:n], refs[n:2 * n], refs[2 * n], refs[2 * n + 1], kind):
            cp.wait_send()
            cp.wait_recv()

    res = pl.pallas_call(
        body, name=name,
        out_shape=tuple(pltpu.HBM(a.shape, a.dtype) for a in srcs + lands),
        in_specs=[_HBM] * (2 * n) + [_SEM, _SEM, pl.BlockSpec(memory_space=pl.ANY)],
        out_specs=tuple([_HBM] * (2 * n)),
        input_output_aliases={i: i for i in range(2 * n)},
        compiler_params=pltpu.CompilerParams(has_side_effects=_EFFECT),
    )(*srcs, *lands, send_sems, recv_sems, after)
    return list(res[:n]), list(res[n:])


def _sum_own_and_received(g, recv, *, name):
    _, r, w = g.shape
    tm = _tile(r)
    if g.dtype == bf16 and tm % 16:
        tm = r
    x, y, _ = _position()
    me = jnp.reshape(2 * x + y, (1,)).astype(jnp.int32)

    def body(me_ref, g_ref, r_ref, o_ref):
        o_ref[...] = (g_ref[0].astype(f32) + r_ref[0].astype(f32)) + (r_ref[1].astype(f32) + r_ref[2].astype(f32))

    return pl.pallas_call(
        body, name=name,
        grid_spec=pltpu.PrefetchScalarGridSpec(
            num_scalar_prefetch=1, grid=(r // tm,),
            in_specs=[pl.BlockSpec((1, tm, w), lambda i, me_ref: (me_ref[0], i, 0)),
                      pl.BlockSpec((3, tm, w), lambda i, me_ref: (0, i, 0))],
            out_specs=pl.BlockSpec((tm, w), lambda i, me_ref: (i, 0))),
        out_shape=jax.ShapeDtypeStruct((r, w), f32),
        compiler_params=_params(("parallel",)),
    )(me, g, recv)


def _swap_cores(arrs, *, name):
    n = len(arrs)

    def body(*refs):
        x, y, c = _position()
        copies = [pltpu.make_async_remote_copy(
            src_ref=refs[i], dst_ref=refs[n + i], send_sem=refs[2 * n].at[i], recv_sem=refs[2 * n + 1].at[i],
            device_id=(x, y, 1 - c), device_id_type=MESH) for i in range(n)]
        for cp in copies:
            cp.start()
        for cp in copies:
            cp.wait_recv()
        for cp in copies:
            cp.wait_send()

    return pl.pallas_call(
        body, name=name,
        in_specs=[pl.BlockSpec(memory_space=pl.ANY)] * n,
        out_specs=[pl.BlockSpec(memory_space=pl.ANY)] * n,
        out_shape=[jax.ShapeDtypeStruct(a.shape, a.dtype) for a in arrs],
        scratch_shapes=[pltpu.SemaphoreType.DMA((n,)), pltpu.SemaphoreType.DMA((n,))],
    )(*arrs)


def _swap_halves(zone, *, name):
    def body(z_ref, o_ref, send_sems, recv_sems):
        x, y, c = _position()
        mine = [pltpu.make_async_remote_copy(
            src_ref=o_ref.at[c, 2 * px + py], dst_ref=o_ref.at[c, 2 * px + py], send_sem=send_sems.at[j],
            recv_sem=recv_sems.at[j], device_id=(x, y, 1 - c), device_id_type=MESH)
            for j, (px, py) in enumerate(_other_chips(x, y))]
        for cp in mine:
            cp.start()
        for j, (px, py) in enumerate(_other_chips(x, y)):
            pltpu.make_async_remote_copy(
                src_ref=o_ref.at[c, 2 * px + py], dst_ref=o_ref.at[1 - c, 2 * px + py], send_sem=send_sems.at[j],
                recv_sem=recv_sems.at[j], device_id=(x, y, 1 - c), device_id_type=MESH).wait_recv()
        for cp in mine:
            cp.wait_send()

    return pl.pallas_call(
        body, name=name,
        in_specs=[pl.BlockSpec(memory_space=pl.ANY)], out_specs=pl.BlockSpec(memory_space=pl.ANY),
        out_shape=jax.ShapeDtypeStruct(zone.shape, zone.dtype), input_output_aliases={0: 0},
        scratch_shapes=[pltpu.SemaphoreType.DMA((3,)), pltpu.SemaphoreType.DMA((3,))],
    )(zone)


def _all_reduce_small(a, after, *, name):
    rows, w = a.shape

    def body(a_ref, after_ref, o_ref, buf, send_sems, recv_sems):
        x, y, c = _position()
        me = 4 * x + 2 * y + c
        buf[0] = a_ref[...]
        sends = []
        for rel in range(1, N_DEV):
            peer = ((1 - x) if rel & 4 else x, (1 - y) if rel & 2 else y, (1 - c) if rel & 1 else c)
            cp = pltpu.make_async_remote_copy(
                src_ref=a_ref, dst_ref=buf.at[rel], send_sem=send_sems.at[rel - 1], recv_sem=recv_sems.at[rel - 1],
                device_id=peer, device_id_type=MESH)
            cp.start()
            sends.append(cp)
        for cp in sends:
            cp.wait_recv()
        for cp in sends:
            cp.wait_send()
        acc = buf[jnp.bitwise_xor(me, 0)]
        for d in range(1, N_DEV):
            acc = acc + buf[jnp.bitwise_xor(me, d)]
        o_ref[...] = acc

    return pl.pallas_call(
        body, name=name,
        in_specs=[pl.BlockSpec(memory_space=pltpu.VMEM), pl.BlockSpec(memory_space=pl.ANY)],
        out_specs=pl.BlockSpec(memory_space=pltpu.VMEM),
        out_shape=jax.ShapeDtypeStruct((rows, w), f32),
        scratch_shapes=[pltpu.VMEM((N_DEV, rows, w), f32), pltpu.SemaphoreType.DMA((N_DEV - 1,)),
                        pltpu.SemaphoreType.DMA((N_DEV - 1,))],
    )(a, after)


def _adamw(w, g_parts, m, v, *, name, transposed=False):
    rows, cols = w.shape
    if transposed:
        tm = 256 if rows % 256 == 0 else rows
        g_spec = pl.BlockSpec((cols, tm), lambda i: (0, i))
    else:
        tm = _tile(rows, 256)
        g_spec = pl.BlockSpec((tm, cols), lambda i: (i, 0))
    n = len(g_parts)

    def body(*refs):
        w_ref, m_ref, v_ref = refs[0], refs[1 + n], refs[2 + n]
        g_ref, d_ref, nm_ref, nv_ref = refs[3 + n:]
        gv = refs[1][...]
        for part in refs[2:1 + n]:
            gv = gv + part[...]
        if transposed:
            gv = gv.T
        g_ref[...] = gv
        nm = ADAM_B1 * m_ref[...] + (1.0 - ADAM_B1) * gv
        nv = ADAM_B2 * v_ref[...] + (1.0 - ADAM_B2) * (gv * gv)
        m_hat = nm / (1.0 - ADAM_B1 ** ADAM_STEP)
        v_hat = nv / (1.0 - ADAM_B2 ** ADAM_STEP)
        d_ref[...] = -ADAM_LR * (m_hat / (jnp.sqrt(v_hat) + ADAM_EPS) + ADAM_WD * w_ref[...])
        nm_ref[...] = nm
        nv_ref[...] = nv

    spec = pl.BlockSpec((tm, cols), lambda i: (i, 0))
    shape = jax.ShapeDtypeStruct((rows, cols), f32)
    return pl.pallas_call(
        body, name=name, grid=(rows // tm,), in_specs=[spec] + [g_spec] * n + [spec] * 2,
        out_specs=[spec] * 4, out_shape=[shape] * 4,
        compiler_params=_params(("parallel",)),
    )(w, *g_parts, m, v)


def _pad_rows(a, rows):
    return jnp.concatenate([a, jnp.zeros((rows - a.shape[0], a.shape[1]), a.dtype)], axis=0) if rows > a.shape[0] else a


_SMALL = (("norm_mix_g", D_MODEL), ("b_in", D_IN), ("sinks", Q_HEADS), ("mix", RWKV_PROJ), ("w0", RWKV_DIM),
          ("a0", RWKV_DIM), ("k_k", RWKV_DIM), ("k_a", RWKV_DIM), ("r_k", RWKV_DIM), ("ln_w", RWKV_DIM),
          ("ln_b", RWKV_DIM), ("norm_ffn_g", D_MODEL), ("norm_final_g", D_MODEL))


def _pack_small(d):
    flat = jnp.concatenate([d[n].reshape(-1).astype(f32) for n, _ in _SMALL])
    return flat


def _unpack_small(flat):
    out, off = {}, 0
    for n, size in _SMALL:
        out[n] = flat[off:off + size]
        off += size
    return out


_SMALL_TOTAL = sum(s for _, s in _SMALL)


def kernel(x, meta_tokens, norm_mix_g, w_in, b_in, attn_sinks, rwkv_mix, rwkv_w0, rwkv_w2, rwkv_a0, rwkv_a2, rwkv_g2, rwkv_k_k, rwkv_k_a, rwkv_r_k, rwkv_ln_w, rwkv_ln_b, w_br_attn, w_br_rwkv, w_o, norm_ffn_g, w_ffn_gate, w_ffn_up, w_ffn_down, norm_final_g, loss_target, m_meta_tokens, m_norm_mix_g, m_w_in, m_b_in, m_attn_sinks, m_rwkv_mix, m_rwkv_w0, m_rwkv_w2, m_rwkv_a0, m_rwkv_a2, m_rwkv_g2, m_rwkv_k_k, m_rwkv_k_a, m_rwkv_r_k, m_rwkv_ln_w, m_rwkv_ln_b, m_w_br_attn, m_w_br_rwkv, m_w_o, m_norm_ffn_g, m_w_ffn_gate, m_w_ffn_up, m_w_ffn_down, m_norm_final_g, v_meta_tokens, v_norm_mix_g, v_w_in, v_b_in, v_attn_sinks, v_rwkv_mix, v_rwkv_w0, v_rwkv_w2, v_rwkv_a0, v_rwkv_a2, v_rwkv_g2, v_rwkv_k_k, v_rwkv_k_a, v_rwkv_r_k, v_rwkv_ln_w, v_rwkv_ln_b, v_w_br_attn, v_w_br_rwkv, v_w_o, v_norm_ffn_g, v_w_ffn_gate, v_w_ffn_up, v_w_ffn_down, v_norm_final_g):
    names = ("meta_tokens", "norm_mix_g", "w_in", "b_in", "attn_sinks", "rwkv_mix", "rwkv_w0", "rwkv_w2", "rwkv_a0",
             "rwkv_a2", "rwkv_g2", "rwkv_k_k", "rwkv_k_a", "rwkv_r_k", "rwkv_ln_w", "rwkv_ln_b", "w_br_attn",
             "w_br_rwkv", "w_o", "norm_ffn_g", "w_ffn_gate", "w_ffn_up", "w_ffn_down", "norm_final_g")
    w_all = dict(zip(names, (meta_tokens, norm_mix_g, w_in, b_in, attn_sinks, rwkv_mix, rwkv_w0, rwkv_w2, rwkv_a0,
                             rwkv_a2, rwkv_g2, rwkv_k_k, rwkv_k_a, rwkv_r_k, rwkv_ln_w, rwkv_ln_b, w_br_attn,
                             w_br_rwkv, w_o, norm_ffn_g, w_ffn_gate, w_ffn_up, w_ffn_down, norm_final_g)))
    m_all = dict(zip(names, (m_meta_tokens, m_norm_mix_g, m_w_in, m_b_in, m_attn_sinks, m_rwkv_mix, m_rwkv_w0,
                             m_rwkv_w2, m_rwkv_a0, m_rwkv_a2, m_rwkv_g2, m_rwkv_k_k, m_rwkv_k_a, m_rwkv_r_k,
                             m_rwkv_ln_w, m_rwkv_ln_b, m_w_br_attn, m_w_br_rwkv, m_w_o, m_norm_ffn_g, m_w_ffn_gate,
                             m_w_ffn_up, m_w_ffn_down, m_norm_final_g)))
    v_all = dict(zip(names, (v_meta_tokens, v_norm_mix_g, v_w_in, v_b_in, v_attn_sinks, v_rwkv_mix, v_rwkv_w0,
                             v_rwkv_w2, v_rwkv_a0, v_rwkv_a2, v_rwkv_g2, v_rwkv_k_k, v_rwkv_k_a, v_rwkv_r_k,
                             v_rwkv_ln_w, v_rwkv_ln_b, v_w_br_attn, v_w_br_rwkv, v_w_o, v_norm_ffn_g, v_w_ffn_gate,
                             v_w_ffn_up, v_w_ffn_down, v_norm_final_g)))
    cx, cy, _ = _position()
    chip = 2 * cx + cy

    t_of = dict(w_in_t="w_in", w_gate_t="w_ffn_gate", w_up_t="w_ffn_up", w_br_attn_t="w_br_attn",
                w_br_rwkv_t="w_br_rwkv", g2_t="rwkv_g2", w2_t="rwkv_w2", a2_t="rwkv_a2")
    plain_of = dict(w_down="w_ffn_down", w_o="w_o")
    meta_cols = meta_tokens.shape[1]

    def shard(k):
        return (w_all[t_of[k]][0].T if k in t_of else w_all[plain_of[k]][0]).astype(bf16)

    def whole(zone, own):
        return lax.dynamic_update_slice_in_dim(zone, own[None], chip, axis=0).reshape(-1, own.shape[-1])

    tiny = ("g2_t", "w2_t", "a2_t")
    late = ("w_gate_t", "w_up_t", "w_down", "w_o", "w_br_attn_t", "w_br_rwkv_t")
    w_in_own = shard("w_in_t")
    w_in_rows, w_in_cols = w_in_own.shape
    tiny_h = _exchange_start([shard(k) for k in tiny] + [meta_tokens], kind="whole", name="gather_tiny_start")
    w_in_h = _exchange_start([w_in_own + tiny_h[4][0, 0].astype(bf16)], kind="half", name="gather_w_in_start")
    behind = w_in_h[4][0, 0].astype(bf16)
    late_h = _exchange_start([shard(k) + behind for k in late], kind="whole", name="gather_late_start")
    own, zones = _exchange_wait(tiny_h, late_h[4], kind="whole", name="gather_tiny_wait")
    got = {k: whole(z, o) for k, z, o in zip(tiny, zones, own)}
    meta_full = whole(zones[-1], own[-1]).reshape(N_CHIPS, N_META, meta_cols).transpose(1, 0, 2).reshape(N_META, -1)
    p = dict(
        g2=got["g2_t"].T.astype(f32), w2=got["w2_t"].T.astype(f32), a2=got["a2_t"].T.astype(f32),
        b_in=b_in, sinks=attn_sinks, mix=rwkv_mix, w0=rwkv_w0, a0=rwkv_a0, k_k=rwkv_k_k, k_a=rwkv_k_a,
        r_k=rwkv_r_k.reshape(1, RWKV_DIM), ln_w=rwkv_ln_w, ln_b=rwkv_ln_b, norm_mix_g=norm_mix_g,
        norm_ffn_g=norm_ffn_g, norm_final_g=norm_final_g.reshape(1, D_MODEL),
    )

    def early_weights(after):
        own_h, zones_h = _exchange_wait(w_in_h, after, kind="half", name="gather_w_in_wait")
        zone = _swap_halves(zones_h[0], name="swap_w_in_halves")
        own_halves = own_h[0].reshape(w_in_rows, 2, w_in_cols // 2).transpose(1, 0, 2)[:, None]
        zone = lax.dynamic_update_slice(zone, own_halves, (0, chip, 0, 0))
        return dict(w_in_lr=zone.reshape(2, N_CHIPS * w_in_rows, w_in_cols // 2))

    def late_weights(after):
        own_l, zones_l = _exchange_wait(late_h, after, kind="whole", name="gather_late_wait")
        return {k: whole(z, o) for k, z, o in zip(late, zones_l, own_l)}

    started = {}

    def partial_sums(groups, after):
        parts = {}
        for group in groups:
            keys, handle = started[group]
            slabs, lands = _exchange_wait(handle, after, kind="slab", name="scatter_" + group + "_wait")
            parts.update({k: _sum_own_and_received(s, l, name="sum_chips_" + k) for k, s, l in zip(keys, slabs, lands)})
        return parts

    def emit(group, grads_):
        keys = list(grads_)
        slabs = []
        for k in keys:
            a = grads_[k].T if k in ("g2", "w2", "a2") else grads_[k]
            slabs.append(a.reshape(N_CHIPS, a.shape[0] // N_CHIPS, a.shape[1]))
        started[group] = (keys, _exchange_start(slabs, kind="slab", name="scatter_" + group + "_start"))
        zero = started[group][1][4]
        if group == "input":
            started["parts_a"] = partial_sums(("ffn", "branch"), zero)
            started["swap_a"] = _exchange_start(list(started["parts_a"].values()), kind="sibling",
                                                name="swap_cores_a_start")
            zero = started["swap_a"][4]
        return zero[0, 0]

    loss, dx, g = _local_step(x[0], loss_target[0], meta_full, p, early_weights, late_weights, emit)

    grads, delta, new_m, new_v = {}, {}, {}, {}
    in_grad_layout = ("w_in_t", "w_gate_t", "w_up_t")
    weight_of = {**t_of, **plain_of}

    def update(keys, mine, theirs):
        for k, part, other in zip(keys, mine, theirs):
            both = [part, other]
            k = k + "_t" if k in ("g2", "w2", "a2") else k
            n = weight_of[k]
            shape2 = w_all[n].shape[1:]
            w_, m_, v_ = (a.reshape(shape2) for a in (w_all[n], m_all[n], v_all[n]))
            if k in in_grad_layout:
                res = [t.T for t in _adamw(w_.T, both, m_.T, v_.T, name="adamw_" + n)]
            else:
                res = _adamw(w_, both, m_, v_, name="adamw_" + n, transposed=k in t_of)
            grads[n], delta[n], new_m[n], new_v[n] = (t.reshape(w_all[n].shape) for t in res)
        return delta[n]

    done = update(list(started["parts_a"]),
                  *_exchange_wait(started["swap_a"], dx, kind="sibling", name="swap_cores_a_wait"))
    small = jnp.concatenate([_pack_small(g), loss.reshape(1)])
    small_rows = -(-small.shape[0] // PACK_W)
    small = jnp.concatenate([small, jnp.zeros((small_rows * PACK_W - small.shape[0],), f32)]).reshape(small_rows, PACK_W)
    small_rows8 = -(-(small_rows + N_META) // 8) * 8
    reduced = _all_reduce_small(_pad_rows(jnp.concatenate([g["meta"], small], axis=0), small_rows8), done,
                                name="reduce_small")
    parts_b = partial_sums(("input",), reduced)
    update(list(parts_b), list(parts_b.values()), _swap_cores(list(parts_b.values()), name="swap_cores_b"))
    g_meta = lax.dynamic_slice_in_dim(reduced[:N_META], chip * meta_cols, meta_cols, axis=1)
    flat = reduced[N_META:N_META + small_rows].reshape(-1)
    g_small = _unpack_small(flat)
    loss_total = flat[_SMALL_TOTAL]

    small_of = dict(norm_mix_g="norm_mix_g", b_in="b_in", attn_sinks="sinks", rwkv_mix="mix", rwkv_w0="w0",
                    rwkv_a0="a0", rwkv_k_k="k_k", rwkv_k_a="k_a", rwkv_r_k="r_k", rwkv_ln_w="ln_w",
                    rwkv_ln_b="ln_b", norm_ffn_g="norm_ffn_g", norm_final_g="norm_final_g")
    grads["meta_tokens"] = g_meta
    for n, k in small_of.items():
        grads[n] = g_small[k].reshape(w_all[n].shape)

    rest = [n for n in names if n not in delta]

    def pack_rest(src):
        flat_ = jnp.concatenate([src[n].reshape(-1) for n in rest])
        rows_ = -(-flat_.shape[0] // (8 * PACK_W)) * 8
        return jnp.concatenate([flat_, jnp.ones((rows_ * PACK_W - flat_.shape[0],), f32)]).reshape(rows_, PACK_W)

    _, d_, m_, v_ = _adamw(pack_rest(w_all), [pack_rest(grads)], pack_rest(m_all), pack_rest(v_all),
                           name="adamw_small")
    off = 0
    for n in rest:
        size = w_all[n].size
        for dst, src in ((delta, d_), (new_m, m_), (new_v, v_)):
            dst[n] = src.reshape(-1)[off:off + size].reshape(w_all[n].shape)
        off += size

    return (loss_total, dx.reshape(x.shape), *[grads[n] for n in names], *[delta[n] for n in names],
            *[new_m[n] for n in names], *[new_v[n] for n in names])
```

```python
import math

import jax
import jax.numpy as jnp
import numpy as np
from jax import lax
from jax.experimental import pallas as pl
from jax.experimental.pallas import tpu as pltpu

f32 = jnp.float32
bf16 = jnp.bfloat16

D_MODEL = 1024
N_META = 16
HEAD_DIM = 64
Q_HEADS = 8
KV_HEADS = 2
GROUP = Q_HEADS // KV_HEADS
WINDOW = 128
BLOCK = 128
ROPE_THETA = 500000.0
ROPE_DIM = HEAD_DIM // 4
RWKV_HEADS = 8
RWKV_HEAD = 64
RWKV_DIM = RWKV_HEADS * RWKV_HEAD
DECAY_LORA = 64
AAA_LORA = 64
GATE_LORA = 160
LORA_W = DECAY_LORA + AAA_LORA + GATE_LORA
RWKV_LN_EPS = 64e-5
D_FF = 2816
Q_W = Q_HEADS * HEAD_DIM
KV_W = KV_HEADS * HEAD_DIM
ATTN_PROJ = Q_W + 2 * KV_W
RKV_W = 3 * RWKV_DIM
RWKV_PROJ = RKV_W + LORA_W
D_IN = ATTN_PROJ + RWKV_PROJ + 2 * D_MODEL
RMS_EPS = 1e-6
NEG_INF = -1e30
PAD = BLOCK - N_META
FRONT = PAD + N_META

ADAM_LR = 0.001
ADAM_B1 = 0.9
ADAM_B2 = 0.999
ADAM_EPS = 1e-08
ADAM_WD = 0.01
ADAM_STEP = 10

N_CHIPS = 4
N_DEV = 8
CHUNK = 128
VMEM_LIMIT = 56 * 1024 * 1024
MM_ROWS = 704
PACK_W = 1024
MESH = pl.DeviceIdType.MESH


def _tile(m, pref=384):
    for step in (16, 8):
        for t in range(min(m, pref) // step * step, 0, -step):
            if m % t == 0:
                return t
    return m


def _params(sem=None):
    return pltpu.CompilerParams(dimension_semantics=sem, vmem_limit_bytes=VMEM_LIMIT)


def _full(shape):
    nd = len(shape)
    return pl.BlockSpec(shape, lambda *_: (0,) * nd)


def _dot(a, b, dims="nn"):
    dn = {"nn": (((1,), (0,)), ((), ())), "nt": (((1,), (1,)), ((), ())), "tn": (((0,), (0,)), ((), ()))}[dims]
    return lax.dot_general(a.astype(bf16), b.astype(bf16), dn, preferred_element_type=f32)


def _two_pass(x, m, dims="nn"):
    x_hi = x.astype(bf16)
    x_lo = (x - x_hi.astype(f32)).astype(bf16)
    return _dot(x_hi, m, dims) + _dot(x_lo, m, dims)


@jax.custom_vjp
def _dot_const(x, m):
    return _two_pass(x, m)


def _dot_const_fwd(x, m):
    return _two_pass(x, m), m


def _dot_const_bwd(m, ct):
    return _two_pass(ct, m, "nt"), jnp.zeros_like(m)


_dot_const.defvjp(_dot_const_fwd, _dot_const_bwd)


def _two_pass_left(m, x, dims):
    x_hi = x.astype(bf16)
    x_lo = (x - x_hi.astype(f32)).astype(bf16)
    return _dot(m, x_hi, dims) + _dot(m, x_lo, dims)


@jax.custom_vjp
def _const_dot(m, x):
    return _two_pass_left(m, x, "nn")


def _const_dot_fwd(m, x):
    return _two_pass_left(m, x, "nn"), m


def _const_dot_bwd(m, ct):
    return jnp.zeros_like(m), _two_pass_left(m, ct, "tn")


_const_dot.defvjp(_const_dot_fwd, _const_dot_bwd)


def _mm(a, b, mode, *, name, out_dtype=f32, bias=None, add=None, zero_rows_below=0):
    m, _ = a.shape
    n = b.shape[1] if mode == "nn" else b.shape[0]
    tm = _tile(m, MM_ROWS)
    has_bias, has_add = bias is not None, add is not None

    def body(*refs):
        a_ref, b_ref = refs[0], refs[1]
        o_ref = refs[-1]
        acc = _dot(a_ref[...], b_ref[...], mode)
        k = 2
        if has_bias:
            acc = acc + refs[k][...]
            k += 1
        if zero_rows_below:
            rows = pl.program_id(0) * tm + lax.broadcasted_iota(jnp.int32, acc.shape, 0)
            acc = jnp.where(rows >= zero_rows_below, acc, 0.0)
        if has_add:
            acc = acc + refs[k][...].astype(f32)
        o_ref[...] = acc.astype(out_dtype)

    ins = [a, b]
    in_specs = [pl.BlockSpec((tm, a.shape[1]), lambda i: (i, 0)), _full(b.shape)]
    if has_bias:
        ins.append(bias)
        in_specs.append(_full(bias.shape))
    if has_add:
        ins.append(add)
        in_specs.append(pl.BlockSpec((tm, n), lambda i: (i, 0)))
    return pl.pallas_call(
        body, name=name, grid=(m // tm,), in_specs=in_specs,
        out_specs=pl.BlockSpec((tm, n), lambda i: (i, 0)),
        out_shape=jax.ShapeDtypeStruct((m, n), out_dtype),
        compiler_params=_params(("parallel",)),
    )(*ins)


def _pieces(widths):
    out, off = [], 0
    for w in widths:
        out.append((off, w))
        off += w
    return out


def _proj_in(a, w_lr, bias, widths, rope, *, name, zero_rows_below=0):
    m, kdim = a.shape
    half = kdim // 2
    tm = _tile(m, MM_ROWS)
    cos_t, sin_t, swap = rope
    out_widths = [Q_W, KV_W, KV_W] + list(widths[1:])

    def body(a_ref, w_ref, b_ref, cos_ref, sin_ref, swap_ref, *outs):
        a_l, a_r = a_ref[:, :half], a_ref[:, half:]
        for j, (off, width) in enumerate(_pieces(widths)):
            acc = _dot(a_l, w_ref[0, off:off + width, :], "nt") + _dot(a_r, w_ref[1, off:off + width, :], "nt")
            acc = acc + b_ref[:, off:off + width]
            if zero_rows_below:
                rows = pl.program_id(0) * tm + lax.broadcasted_iota(jnp.int32, acc.shape, 0)
                acc = jnp.where(rows >= zero_rows_below, acc, 0.0)
            if j == 0:
                qkv = acc.astype(bf16).astype(f32)
                for o_ref, val in zip(outs[:3], _attn_prep(qkv, cos_ref[...], sin_ref[...], swap_ref[...])):
                    o_ref[...] = val.astype(o_ref.dtype)
            else:
                outs[2 + j][...] = acc.astype(outs[2 + j].dtype)

    table = pl.BlockSpec((tm, HEAD_DIM), lambda i: (i, 0))
    return pl.pallas_call(
        body, name=name, grid=(m // tm,),
        in_specs=[pl.BlockSpec((tm, kdim), lambda i: (i, 0)), _full(w_lr.shape), _full(bias.shape), table, table,
                  _full(swap.shape)],
        out_specs=[pl.BlockSpec((tm, w), lambda i: (i, 0)) for w in out_widths],
        out_shape=[jax.ShapeDtypeStruct((m, w), bf16) for w in out_widths],
        compiler_params=_params(("parallel",)),
    )(a, w_lr, bias, cos_t, sin_t, swap)


def _proj_in_bwd(d_list, w_lr, *, name):
    m = d_list[0].shape[0]
    half = w_lr.shape[2]
    widths = [d.shape[1] for d in d_list]
    tm = _tile(m, MM_ROWS)

    def body(*refs):
        w_ref, o_ref = refs[-2], refs[-1]
        for side in range(2):
            acc = None
            for (off, width), d_ref in zip(_pieces(widths), refs):
                term = _dot(d_ref[...], w_ref[side, off:off + width, :])
                acc = term if acc is None else acc + term
            o_ref[:, side * half:(side + 1) * half] = acc.astype(o_ref.dtype)

    return pl.pallas_call(
        body, name=name, grid=(m // tm,),
        in_specs=[pl.BlockSpec((tm, w), lambda i: (i, 0)) for w in widths] + [_full(w_lr.shape)],
        out_specs=pl.BlockSpec((tm, 2 * half), lambda i: (i, 0)),
        out_shape=jax.ShapeDtypeStruct((m, 2 * half), bf16),
        compiler_params=_params(("parallel",)),
    )(*d_list, w_lr)


def _residual_norm(a, w, res, g, *, name):
    m, d = res.shape
    tm = _tile(m, MM_ROWS)

    def body(a_ref, w_ref, r_ref, g_ref, h_ref, n_ref):
        h = _dot(a_ref[...], w_ref[...]) + r_ref[...]
        h_ref[...] = h
        n_ref[...] = _rms(h, g_ref[...]).astype(n_ref.dtype)

    tile = pl.BlockSpec((tm, d), lambda i: (i, 0))
    return pl.pallas_call(
        body, name=name, grid=(m // tm,),
        in_specs=[pl.BlockSpec((tm, a.shape[1]), lambda i: (i, 0)), _full(w.shape), tile, _full(g.shape)],
        out_specs=[tile, tile],
        out_shape=[jax.ShapeDtypeStruct((m, d), f32), jax.ShapeDtypeStruct((m, d), bf16)],
        compiler_params=_params(("parallel",)),
    )(a, w, res, g)


def _residual_norm_bwd(d_list, w_list, h, g, dh_out, *, name):
    m, d = h.shape
    k = len(d_list)
    tm = _tile(m)

    def body(*refs):
        h_ref, g_ref, dho_ref, dh_ref, dg_ref = refs[2 * k:]
        dn = _dot(refs[0][...], refs[k][...])
        for i in range(1, k):
            dn = dn + _dot(refs[i][...], refs[k + i][...])
        _, vjp = jax.vjp(lambda hv, gv: (_rms(hv, gv), hv), h_ref[...], g_ref[...])
        dh, dg = vjp((dn, dho_ref[...]))
        dh_ref[...] = dh

        @pl.when(pl.program_id(0) == 0)
        def _():
            dg_ref[...] = jnp.zeros_like(dg_ref)

        dg_ref[...] += dg

    tile = pl.BlockSpec((tm, d), lambda i: (i, 0))
    return pl.pallas_call(
        body, name=name, grid=(m // tm,),
        in_specs=[pl.BlockSpec((tm, a.shape[1]), lambda i: (i, 0)) for a in d_list] + [_full(w.shape) for w in w_list]
        + [tile, _full(g.shape), tile],
        out_specs=[tile, _full(g.shape)],
        out_shape=[jax.ShapeDtypeStruct((m, d), f32), jax.ShapeDtypeStruct(g.shape, f32)],
        compiler_params=_params(("arbitrary",)),
    )(*d_list, *w_list, h, g, dh_out)


def _mm_tn(a, b, *, name, colsum=False, out_dtype=bf16, into=None):
    r, m = a.shape
    n = b.shape[1]
    tr = _tile(r, 1408)
    tmo = m
    for cand in (1408, 1024, 768, 512):
        if m > 1024 and m % cand == 0:
            tmo = cand
            break
    steps = r // tr

    rows, offset, target = into or (m, 0, None)

    def body(a_ref, b_ref, *rest):
        o_ref, rest = (rest[0], rest[1:]) if target is None else (rest[1], rest[2:])
        acc = rest[-1]
        i = pl.program_id(1)

        @pl.when(i == 0)
        def _():
            acc[...] = jnp.zeros_like(acc)
            if colsum:
                rest[0][...] = jnp.zeros_like(rest[0])

        acc[...] += _dot(a_ref[...], b_ref[...], "tn")
        if colsum:
            rest[0][...] += jnp.sum(a_ref[...].astype(f32), axis=0, keepdims=True)

        @pl.when(i == steps - 1)
        def _():
            o_ref[...] = acc[...].astype(out_dtype)

    out_shape = [jax.ShapeDtypeStruct((rows, n), out_dtype)]
    if offset % tmo == 0:
        out_specs = [pl.BlockSpec((tmo, n), lambda j, i: (offset // tmo + j, 0))]
    else:
        out_specs = [pl.BlockSpec((pl.Element(tmo), pl.Element(n)), lambda j, i: (pl.multiple_of(offset + j * tmo, math.gcd(offset, tmo)), 0))]
    if colsum:
        out_shape.append(jax.ShapeDtypeStruct((1, m), f32))
        out_specs.append(pl.BlockSpec((1, tmo), lambda j, i: (0, j)))
    in_specs = [pl.BlockSpec((tr, tmo), lambda j, i: (i, j)), pl.BlockSpec((tr, n), lambda j, i: (i, 0))]
    res = pl.pallas_call(
        body, name=name, grid=(m // tmo, steps),
        in_specs=in_specs + ([] if target is None else [pl.BlockSpec(memory_space=pl.ANY)]),
        out_specs=out_specs, out_shape=out_shape,
        scratch_shapes=[pltpu.VMEM((tmo, n), f32)],
        input_output_aliases={} if target is None else {2: 0},
        compiler_params=_params(("parallel", "arbitrary")),
    )(a, b, *([] if target is None else [target]))
    return res if colsum else res[0]


def _rowwise(fn, rows, params, outs, *, name, tm=None):
    m = rows[0].shape[0]
    tm = tm or _tile(m, MM_ROWS)
    nr, npar = len(rows), len(params)

    def body(*refs):
        vals = [r[...] for r in refs[:nr + npar]]
        res = fn(*vals)
        for o_ref, v in zip(refs[nr + npar:], res):
            o_ref[...] = v.astype(o_ref.dtype)

    return pl.pallas_call(
        body, name=name, grid=(m // tm,),
        in_specs=[pl.BlockSpec((tm, r.shape[1]), lambda i: (i, 0)) for r in rows] + [_full(p.shape) for p in params],
        out_specs=[pl.BlockSpec((tm, w), lambda i: (i, 0)) for w, _ in outs],
        out_shape=[jax.ShapeDtypeStruct((m, w), dt) for w, dt in outs],
        compiler_params=_params(("parallel",)),
    )(*rows, *params)


def _rowwise_bwd(fn, rows, params, cts, *, name, diff_rows, diff_params, tm=None, zero_rows_below=0, out_dtypes=None):
    m = rows[0].shape[0]
    tm = tm or _tile(m)
    nr, npar = len(rows), len(params)
    d_idx = [i for i in range(nr) if diff_rows[i]]
    p_idx = [i for i in range(npar) if diff_params[i]]
    out_dtypes = out_dtypes or [f32] * len(d_idx)
    flat_cts = [c for group in cts for c in group]
    n_ct = len(flat_cts)

    def body(*refs):
        vals = [r[...] for r in refs[:nr + npar]]
        ct_refs = refs[nr + npar:nr + npar + n_ct]
        out_refs = refs[nr + npar + n_ct:]
        ct_vals, k = [], 0
        for group in cts:
            acc = ct_refs[k][...].astype(f32)
            for extra in range(1, len(group)):
                acc = acc + ct_refs[k + extra][...].astype(f32)
            k += len(group)
            if zero_rows_below:
                rr = pl.program_id(0) * tm + lax.broadcasted_iota(jnp.int32, acc.shape, 0)
                acc = jnp.where(rr >= zero_rows_below, acc, 0.0)
            ct_vals.append(acc)

        def g(*dargs):
            full = list(vals)
            for pos, i in enumerate(d_idx):
                full[i] = dargs[pos]
            for pos, i in enumerate(p_idx):
                full[nr + i] = dargs[len(d_idx) + pos]
            return tuple(fn(*full))

        _, vjp = jax.vjp(g, *[vals[i].astype(f32) for i in d_idx], *[vals[nr + i] for i in p_idx])
        grads = vjp(tuple(ct_vals))
        for pos in range(len(d_idx)):
            out_refs[pos][...] = grads[pos].astype(out_refs[pos].dtype)
        first = pl.program_id(0) == 0
        for pos in range(len(p_idx)):
            o_ref = out_refs[len(d_idx) + pos]

            @pl.when(first)
            def _(o_ref=o_ref):
                o_ref[...] = jnp.zeros_like(o_ref)

            o_ref[...] += grads[len(d_idx) + pos]

    return pl.pallas_call(
        body, name=name, grid=(m // tm,),
        in_specs=[pl.BlockSpec((tm, r.shape[1]), lambda i: (i, 0)) for r in rows] + [_full(p.shape) for p in params]
        + [pl.BlockSpec((tm, c.shape[1]), lambda i: (i, 0)) for c in flat_cts],
        out_specs=[pl.BlockSpec((tm, rows[i].shape[1]), lambda i_: (i_, 0)) for i in d_idx]
        + [_full(params[i].shape) for i in p_idx],
        out_shape=[jax.ShapeDtypeStruct(rows[i].shape, dt) for i, dt in zip(d_idx, out_dtypes)]
        + [jax.ShapeDtypeStruct(params[i].shape, f32) for i in p_idx],
        compiler_params=_params(("arbitrary",)),
    )(*rows, *params, *flat_cts)


def _rms(x, g):
    return x * lax.rsqrt(jnp.mean(x * x, axis=-1, keepdims=True) + RMS_EPS) * g


def _head_sum_matrix(width, head):
    idx = jnp.arange(width) // head
    return (idx[:, None] == idx[None, :]).astype(f32)


def _rope_tables(lp):
    half = ROPE_DIM // 2
    pos = (np.arange(lp) - PAD).astype(np.float32)
    inv_freq = np.power(np.float32(ROPE_THETA), -np.arange(half, dtype=np.float32) * np.float32(2.0 / ROPE_DIM))
    ang = pos[:, None] * inv_freq[None, :].astype(np.float32)
    cos, sin = np.cos(ang), np.sin(ang)
    ones = np.ones((lp, HEAD_DIM - ROPE_DIM), np.float32)
    cos_t = np.concatenate([cos, cos, ones], axis=1)
    sin_t = np.concatenate([-sin, sin, 0.0 * ones], axis=1)
    i = np.arange(HEAD_DIM)
    src = np.where(i < half, i + half, np.where(i < ROPE_DIM, i - half, i))
    swap = ((i[:, None] == src[None, :]) & (i[None, :] < ROPE_DIM)).astype(np.float32)
    return jnp.asarray(cos_t, f32), jnp.asarray(sin_t, f32), jnp.asarray(swap, f32)


def _attn_prep(qkv, cos_t, sin_t, swap):
    outs = []
    for h in range(Q_HEADS + KV_HEADS):
        t = qkv[:, h * HEAD_DIM:(h + 1) * HEAD_DIM]
        outs.append(t * cos_t + _dot_const(t, swap) * sin_t)
    q = jnp.concatenate(outs[:Q_HEADS], axis=1)
    k = jnp.concatenate(outs[Q_HEADS:], axis=1)
    return q, k, qkv[:, Q_W + KV_W:]


def _attn_prep_transposed(dq, dk, dk_meta, dv, dv_meta, cos_t, sin_t, swap):
    parts = []
    for d, heads in ((dq.astype(f32), Q_HEADS), (dk.astype(f32) + dk_meta.astype(f32), KV_HEADS)):
        for h in range(heads):
            t = d[:, h * HEAD_DIM:(h + 1) * HEAD_DIM]
            parts.append(t * cos_t + _two_pass(t * sin_t, swap, "nt"))
    return (jnp.concatenate(parts + [dv.astype(f32) + dv_meta.astype(f32)], axis=1),)


def _softplus(z):
    return jnp.maximum(z, 0.0) + jnp.log1p(jnp.exp(-jnp.abs(z)))


def _rwkv_prep(rkv, lora, w0, w2, a0, a2, g2, k_k, k_a, hsum):
    r = rkv[:, :RWKV_DIM]
    k = rkv[:, RWKV_DIM:2 * RWKV_DIM]
    v = rkv[:, 2 * RWKV_DIM:]
    dw = lora[:, :DECAY_LORA]
    da = lora[:, DECAY_LORA:DECAY_LORA + AAA_LORA]
    dg = lora[:, DECAY_LORA + AAA_LORA:]
    w = -_softplus(-(w0 + _dot(jnp.tanh(dw), w2))) - 0.5
    a = jax.nn.sigmoid(a0 + _dot(da, a2))
    g = _dot(jax.nn.sigmoid(dg), g2)
    kk = k * k_k
    kk = kk * lax.rsqrt(jnp.maximum(_dot_const(kk * kk, hsum), 1e-24))
    k = k * (1.0 + (a - 1.0) * k_a)
    log_decay = -jnp.exp(w)
    return r, log_decay, k, v, -kk, kk * a, g


def _rwkv_post(y, r, k, v, g, ln_w, ln_b, r_k, hmean):
    hsum = hmean * RWKV_HEAD
    mean = _dot_const(y, hmean)
    yc = y - mean
    var = _dot_const(yc * yc, hmean)
    yn = yc * lax.rsqrt(var + RWKV_LN_EPS) * ln_w + ln_b
    bonus = _dot_const(r * k * r_k, hsum) * v
    return ((yn + bonus) * g,)


def _merge(gates, br_a, br_r):
    sg = jax.nn.sigmoid(gates)
    return (sg[:, :D_MODEL] * br_a + sg[:, D_MODEL:] * br_r,)


def _swiglu(gate, up):
    return (jax.nn.silu(gate) * up,)


def _ffn_in(f, w_gate_t, w_up_t, *, name):
    m, d = f.shape
    n = w_gate_t.shape[0]
    tm = _tile(m)

    def body(f_ref, wg_ref, wu_ref, g_ref, u_ref, a_ref):
        g = _dot(f_ref[...], wg_ref[...], "nt")
        u = _dot(f_ref[...], wu_ref[...], "nt")
        g_ref[...] = g.astype(g_ref.dtype)
        u_ref[...] = u.astype(u_ref.dtype)
        a_ref[...] = _swiglu(g, u)[0].astype(a_ref.dtype)

    spec = pl.BlockSpec((tm, n), lambda i: (i, 0))
    return pl.pallas_call(
        body, name=name, grid=(m // tm,),
        in_specs=[pl.BlockSpec((tm, d), lambda i: (i, 0)), _full(w_gate_t.shape), _full(w_up_t.shape)],
        out_specs=[spec] * 3, out_shape=[jax.ShapeDtypeStruct((m, n), bf16)] * 3,
        compiler_params=_params(("parallel",)),
    )(f, w_gate_t, w_up_t)


def _branch_merge(y_attn, y_rwkv, w_attn_t, w_rwkv_t, gates, *, name):
    m = y_attn.shape[0]
    tm = _tile(m, MM_ROWS)

    def body(ya_ref, yr_ref, wa_ref, wr_ref, g_ref, a_ref, r_ref, o_ref):
        br_a = _dot(ya_ref[...], wa_ref[...], "nt")
        br_r = _dot(yr_ref[...], wr_ref[...], "nt")
        a_ref[...] = br_a.astype(a_ref.dtype)
        r_ref[...] = br_r.astype(r_ref.dtype)
        o_ref[...] = _merge(g_ref[...].astype(f32), br_a, br_r)[0].astype(o_ref.dtype)

    rows = lambda a: pl.BlockSpec((tm, a.shape[1]), lambda i: (i, 0))
    spec = pl.BlockSpec((tm, D_MODEL), lambda i: (i, 0))
    return pl.pallas_call(
        body, name=name, grid=(m // tm,),
        in_specs=[rows(y_attn), rows(y_rwkv), _full(w_attn_t.shape), _full(w_rwkv_t.shape), rows(gates)],
        out_specs=[spec] * 3, out_shape=[jax.ShapeDtypeStruct((m, D_MODEL), bf16)] * 3,
        compiler_params=_params(("parallel",)),
    )(y_attn, y_rwkv, w_attn_t, w_rwkv_t, gates)


def _branch_merge_bwd(dh, w_o, gates, br_a, br_r, w_attn_t, w_rwkv_t, *, name):
    m = dh.shape[0]
    tm = _tile(m, MM_ROWS)

    def body(dh_ref, w_ref, g_ref, a_ref, r_ref, wa_ref, wr_ref, dg_ref, da_ref, dr_ref, dya_ref, dyr_ref):
        dmerged = _dot(dh_ref[...], w_ref[...], "nt")
        _, vjp = jax.vjp(lambda g, a, r: _merge(g, a, r)[0], g_ref[...].astype(f32), a_ref[...].astype(f32),
                         r_ref[...].astype(f32))
        dg, da, dr = vjp(dmerged)
        dg_ref[...] = dg.astype(dg_ref.dtype)
        da_ref[...] = da.astype(da_ref.dtype)
        dr_ref[...] = dr.astype(dr_ref.dtype)
        dya_ref[...] = _dot(da, wa_ref[...])
        dyr_ref[...] = _dot(dr, wr_ref[...])

    rows = lambda a: pl.BlockSpec((tm, a.shape[1]), lambda i: (i, 0))
    mixer = pl.BlockSpec((tm, w_attn_t.shape[1]), lambda i: (i, 0))
    return pl.pallas_call(
        body, name=name, grid=(m // tm,),
        in_specs=[rows(dh), _full(w_o.shape), rows(gates), rows(br_a), rows(br_r), _full(w_attn_t.shape),
                  _full(w_rwkv_t.shape)],
        out_specs=[rows(gates), rows(br_a), rows(br_r), mixer, mixer],
        out_shape=[jax.ShapeDtypeStruct(gates.shape, bf16), jax.ShapeDtypeStruct(br_a.shape, bf16),
                   jax.ShapeDtypeStruct(br_r.shape, bf16), jax.ShapeDtypeStruct((m, w_attn_t.shape[1]), f32),
                   jax.ShapeDtypeStruct((m, w_rwkv_t.shape[1]), f32)],
        compiler_params=_params(("parallel",)),
    )(dh, w_o, gates, br_a, br_r, w_attn_t, w_rwkv_t)


def _ffn_in_bwd(dh, w_down, gate, up, *, name):
    m, d = dh.shape
    n = w_down.shape[0]
    tm = _tile(m)

    def body(dh_ref, w_ref, g_ref, u_ref, dg_ref, du_ref):
        dact = _dot(dh_ref[...], w_ref[...], "nt")
        _, vjp = jax.vjp(lambda a, b: _swiglu(a, b)[0], g_ref[...].astype(f32), u_ref[...].astype(f32))
        dg, du = vjp(dact)
        dg_ref[...] = dg.astype(dg_ref.dtype)
        du_ref[...] = du.astype(du_ref.dtype)

    spec = pl.BlockSpec((tm, n), lambda i: (i, 0))
    return pl.pallas_call(
        body, name=name, grid=(m // tm,),
        in_specs=[pl.BlockSpec((tm, d), lambda i: (i, 0)), _full(w_down.shape), spec, spec],
        out_specs=[spec] * 2, out_shape=[jax.ShapeDtypeStruct((m, n), bf16)] * 2,
        compiler_params=_params(("parallel",)),
    )(dh, w_down, gate, up)


HALO = 16


def _previous_rows(x, before_ref, first_tile):
    rows = lax.broadcasted_iota(jnp.int32, x.shape, 0)
    last = jnp.where(first_tile, 0.0, before_ref[HALO - 1:HALO, :].astype(f32))
    return jnp.where(rows == 0, last, pltpu.roll(x, 1, axis=0))


def _mixer_inputs(ps, mixes, params, *, name):
    m = ps[0].shape[0]
    tm = _tile(m)
    sub = tm // HALO
    n_par = len(params)

    def body(*refs):
        first = pl.program_id(0) == 0
        pf = []
        for k in range(2):
            x = refs[k][...].astype(f32)
            pf.append(x + (_previous_rows(x, refs[2 + k], first) - x) * refs[4 + k][...])
        res = _rwkv_prep(*pf, *[ref[...] for ref in refs[6:6 + n_par]])
        for o_ref, val in zip(refs[6 + n_par:], res):
            o_ref[...] = val

    tile = lambda a: pl.BlockSpec((tm, a.shape[1]), lambda i: (i, 0))
    before = lambda a: pl.BlockSpec((HALO, a.shape[1]), lambda i: (jnp.maximum(i * sub - 1, 0), 0))
    out = pl.BlockSpec((tm, RWKV_DIM), lambda i: (i, 0))
    return pl.pallas_call(
        body, name=name, grid=(m // tm,),
        in_specs=[tile(a) for a in ps] + [before(a) for a in ps] + [_full(a.shape) for a in mixes + params],
        out_specs=[out] * 7, out_shape=[jax.ShapeDtypeStruct((m, RWKV_DIM), f32)] * 7,
        compiler_params=_params(("parallel",)),
    )(*ps, *ps, *mixes, *params)


def _mixer_inputs_bwd(ps, mixes, params, cts, *, name):
    m = ps[0].shape[0]
    tm = _tile(m)
    sub = tm // HALO
    nt = m // tm
    n_par = len(params)
    flat_cts = [c for group in cts for c in group]
    n_ct = len(flat_cts)

    def body(*refs):
        i = pl.program_id(0)
        tile_index = nt - 1 - i
        ct_refs = refs[6 + n_par:6 + n_par + n_ct]
        dp_refs = refs[6 + n_par + n_ct:8 + n_par + n_ct]
        dmix_refs = refs[8 + n_par + n_ct:10 + n_par + n_ct]
        dpar_refs = refs[10 + n_par + n_ct:9 + 2 * n_par + n_ct]
        carries = refs[9 + 2 * n_par + n_ct:]
        rows1 = tile_index * tm + lax.broadcasted_iota(jnp.int32, (tm, 1), 0)
        live = rows1 >= PAD

        @pl.when(i == 0)
        def _():
            for ref in (*dmix_refs, *dpar_refs, *carries):
                ref[...] = jnp.zeros_like(ref)

        xs, prevs, pf = [], [], []
        for k in range(2):
            x = refs[k][...].astype(f32)
            xp = _previous_rows(x, refs[2 + k], tile_index == 0)
            xs.append(x)
            prevs.append(xp)
            pf.append(x + (xp - x) * refs[4 + k][...])
        ct_vals, pos = [], 0
        for group in cts:
            acc = ct_refs[pos][...].astype(f32)
            for extra in range(1, len(group)):
                acc = acc + ct_refs[pos + extra][...].astype(f32)
            pos += len(group)
            ct_vals.append(jnp.where(live, acc, 0.0))
        par_vals = [ref[...] for ref in refs[6:6 + n_par]]
        _, vjp = jax.vjp(lambda *args: _rwkv_prep(*args, par_vals[-1]), *pf, *par_vals[:-1])
        g = vjp(tuple(ct_vals))
        for k in range(2):
            dpf = g[k]
            mixv = refs[4 + k][...]
            dm = dpf * mixv
            rows = lax.broadcasted_iota(jnp.int32, dm.shape, 0)
            dm_next = jnp.where(rows == tm - 1, carries[k][...], pltpu.roll(dm, tm - 1, axis=0))
            dp_refs[k][...] = jnp.where(live, dpf - dm + dm_next, 0.0).astype(dp_refs[k].dtype)
            carries[k][...] = dm[0:1, :]
            dmix_refs[k][...] += jnp.sum(dpf * (prevs[k] - xs[k]), axis=0, keepdims=True)
        for ref, val in zip(dpar_refs, g[2:]):
            ref[...] += val

    tile = lambda a: pl.BlockSpec((tm, a.shape[1]), lambda i: (nt - 1 - i, 0))
    before = lambda a: pl.BlockSpec((HALO, a.shape[1]), lambda i: (jnp.maximum((nt - 1 - i) * sub - 1, 0), 0))
    return pl.pallas_call(
        body, name=name, grid=(nt,),
        in_specs=[tile(a) for a in ps] + [before(a) for a in ps] + [_full(a.shape) for a in mixes + params]
        + [tile(c) for c in flat_cts],
        out_specs=[tile(a) for a in ps] + [_full(a.shape) for a in mixes + params[:-1]],
        out_shape=[jax.ShapeDtypeStruct(a.shape, bf16) for a in ps]
        + [jax.ShapeDtypeStruct(a.shape, f32) for a in mixes + params[:-1]],
        scratch_shapes=[pltpu.VMEM((1, a.shape[1]), f32) for a in ps],
        compiler_params=_params(("arbitrary",)),
    )(*ps, *ps, *mixes, *params, *flat_cts)


def _attn_masks(blk):
    qi = lax.broadcasted_iota(jnp.int32, (BLOCK, BLOCK), 0)
    ki = lax.broadcasted_iota(jnp.int32, (BLOCK, BLOCK), 1)
    qpos = blk * BLOCK + qi - PAD
    kpos_c = blk * BLOCK + ki - PAD
    kpos_p = kpos_c - BLOCK
    kpos_m = ki - PAD

    def band(kpos):
        return (kpos >= N_META) & (kpos <= qpos) & (qpos - kpos < WINDOW)

    return band(kpos_p), band(kpos_c), (kpos_m >= 0) & (kpos_m <= qpos)


def _attn_probs(qs, k3s, sink, oks):
    s = [[jnp.where(ok, _dot(qh, kx, "nt"), NEG_INF) for kx, ok in zip(k3, oks)] for qh, k3 in zip(qs, k3s)]
    mx = [jnp.maximum(jnp.maximum(jnp.max(t[0], -1, keepdims=True), jnp.max(t[1], -1, keepdims=True)),
                      jnp.maximum(jnp.max(t[2], -1, keepdims=True), sk)) for t, sk in zip(s, sink)]
    e = [[jnp.exp(tx - m) for tx in t] for t, m in zip(s, mx)]
    e_sink = [jnp.exp(sk - m) for sk, m in zip(sink, mx)]
    inv = [1.0 / (jnp.sum(t[0], -1, keepdims=True) + jnp.sum(t[1], -1, keepdims=True)
                  + jnp.sum(t[2], -1, keepdims=True) + es) for t, es in zip(e, e_sink)]
    return [[tx * i for tx in t] for t, i in zip(e, inv)], [es * i for es, i in zip(e_sink, inv)]


def _head_cols(i):
    return slice(i * HEAD_DIM, (i + 1) * HEAD_DIM)


def _attn_operands(refs):
    q_ref, kp_ref, kc_ref, km_ref, vp_ref, vc_ref, vm_ref, s_ref = refs
    qs = [q_ref[:, _head_cols(i)] * (HEAD_DIM ** -0.5) for i in range(Q_HEADS)]
    k3 = [[ref[:, _head_cols(h)] for ref in (kp_ref, kc_ref, km_ref)] for h in range(KV_HEADS)]
    v3 = [[ref[:, _head_cols(h)] for ref in (vp_ref, vc_ref, vm_ref)] for h in range(KV_HEADS)]
    return (qs, [k3[i // GROUP] for i in range(Q_HEADS)], [v3[i // GROUP] for i in range(Q_HEADS)],
            [s_ref[:, i:i + 1] for i in range(Q_HEADS)])


def _attention(q, k, v, sinks, *, name):
    lp = q.shape[0]
    nb = lp // BLOCK
    prev = lambda i: (jnp.maximum(i - 1, 0), 0)
    cur = lambda i: (i, 0)
    meta = lambda i: (0, 0)
    kv = lambda index: pl.BlockSpec((BLOCK, KV_W), index)

    def body(*refs):
        o_ref = refs[-1]
        qs, k3s, v3s, sink = _attn_operands(refs[:-1])
        p, _ = _attn_probs(qs, k3s, sink, _attn_masks(pl.program_id(0)))
        out = [_dot(ph[0], v3[0]) + _dot(ph[1], v3[1]) + _dot(ph[2], v3[2]) for ph, v3 in zip(p, v3s)]
        for i in range(Q_HEADS):
            o_ref[:, _head_cols(i)] = out[i].astype(o_ref.dtype)

    return pl.pallas_call(
        body, name=name, grid=(nb,),
        in_specs=[pl.BlockSpec((BLOCK, Q_W), cur), kv(prev), kv(cur), kv(meta), kv(prev), kv(cur), kv(meta),
                  _full((1, Q_HEADS))],
        out_specs=pl.BlockSpec((BLOCK, Q_W), cur),
        out_shape=jax.ShapeDtypeStruct((lp, Q_W), bf16),
        compiler_params=_params(("parallel",)),
    )(q, k, k, k, v, v, v, sinks)


def _attention_bwd(q, k, v, sinks, out, do, *, name):
    lp = q.shape[0]
    nb = lp // BLOCK
    cur = lambda n: (jnp.minimum(n, nb - 1), 0)
    prev = lambda n: (jnp.maximum(jnp.minimum(n, nb - 1) - 1, 0), 0)
    behind = lambda n: (jnp.maximum(n - 1, 0), 0)
    meta = lambda n: (0, 0)
    kv = lambda index: pl.BlockSpec((BLOCK, KV_W), index)
    scale = HEAD_DIM ** -0.5

    def body(*refs):
        ins, fwd_ref, do_ref = refs[:8], refs[8], refs[9]
        dq_ref, dk_ref, dv_ref, dkm_ref, dvm_ref, ds_ref, carry_k, carry_v = refs[10:]
        n = pl.program_id(0)

        @pl.when(n == 0)
        def _():
            for ref in (dkm_ref, dvm_ref, ds_ref, carry_k, carry_v):
                ref[...] = jnp.zeros_like(ref)

        @pl.when(n < nb)
        def _():
            qs, k3s, v3s, sink = _attn_operands(ins)
            do = [do_ref[:, _head_cols(i)] for i in range(Q_HEADS)]
            p, p_sink = _attn_probs(qs, k3s, sink, _attn_masks(n))
            delta = [jnp.sum(d * fwd_ref[:, _head_cols(i)].astype(f32), -1, keepdims=True) for i, d in enumerate(do)]
            dp = [[_dot(d, vx, "nt") for vx in v3] for d, v3 in zip(do, v3s)]
            ds = [[px * (dx - dl) for px, dx in zip(ph, dh)] for ph, dh, dl in zip(p, dp, delta)]
            dq = [_dot(dsh[0], k3[0]) + _dot(dsh[1], k3[1]) + _dot(dsh[2], k3[2]) for dsh, k3 in zip(ds, k3s)]
            for i in range(Q_HEADS):
                dq_ref[:, _head_cols(i)] = dq[i] * scale
                ds_ref[:, i:i + 1] -= jnp.sum(p_sink[i] * delta[i], axis=0, keepdims=True)
            for h in range(KV_HEADS):
                group = slice(h * GROUP, (h + 1) * GROUP)
                q_all = jnp.concatenate(qs[group], axis=0)
                do_all = jnp.concatenate(do[group], axis=0)
                dk3 = [_dot(jnp.concatenate([dsh[x] for dsh in ds[group]], axis=0), q_all, "tn") for x in range(3)]
                dv3 = [_dot(jnp.concatenate([ph[x] for ph in p[group]], axis=0), do_all, "tn") for x in range(3)]
                hs = _head_cols(h)
                for out_ref, carry, meta_ref, d3 in ((dk_ref, carry_k, dkm_ref, dk3),
                                                     (dv_ref, carry_v, dvm_ref, dv3)):
                    out_ref[:, hs] = carry[:, hs] + d3[0]
                    carry[:, hs] = d3[1]
                    meta_ref[:, hs] += d3[2]

        @pl.when(n == nb)
        def _():
            dk_ref[...] = carry_k[...]
            dv_ref[...] = carry_v[...]

    kv_shape = jax.ShapeDtypeStruct((lp, KV_W), f32)
    one_shape = jax.ShapeDtypeStruct((BLOCK, KV_W), f32)
    return pl.pallas_call(
        body, name=name, grid=(nb + 1,),
        in_specs=[pl.BlockSpec((BLOCK, Q_W), cur), kv(prev), kv(cur), kv(meta), kv(prev), kv(cur), kv(meta),
                  _full((1, Q_HEADS)), pl.BlockSpec((BLOCK, Q_W), cur), pl.BlockSpec((BLOCK, Q_W), cur)],
        out_specs=[pl.BlockSpec((BLOCK, Q_W), cur), kv(behind), kv(behind), kv(meta), kv(meta),
                   _full((1, Q_HEADS))],
        out_shape=[jax.ShapeDtypeStruct((lp, Q_W), f32), kv_shape, kv_shape, one_shape, one_shape,
                   jax.ShapeDtypeStruct((1, Q_HEADS), f32)],
        scratch_shapes=[pltpu.VMEM((BLOCK, KV_W), f32), pltpu.VMEM((BLOCK, KV_W), f32)],
        compiler_params=_params(("arbitrary",)),
    )(q, k, k, k, v, v, v, sinks, out, do)


@jax.custom_vjp
def _known_inverse(l, x):
    return x


def _known_inverse_fwd(l, x):
    return x, x


def _known_inverse_bwd(x, ct):
    return _dot(_dot(x, ct, "tn"), x, "nt"), jnp.zeros_like(x)


_known_inverse.defvjp(_known_inverse_fwd, _known_inverse_bwd)


@jax.custom_vjp
def _decayed(x, c):
    return (x * jnp.exp(c)).astype(bf16).astype(f32)


def _decayed_fwd(x, c):
    e = jnp.exp(c)
    out = (x * e).astype(bf16).astype(f32)
    return out, (e, out)


def _decayed_bwd(res, ct):
    e, out = res
    return ct * e, ct * out


_decayed.defvjp(_decayed_fwd, _decayed_bwd)


@jax.custom_vjp
def _pair(x, y):
    return _dot(x, y, "nt")


def _pair_fwd(x, y):
    return _dot(x, y, "nt"), (x, y)


def _pair_bwd(res, ct):
    x, y = res
    hi = ct.astype(bf16)
    lo = (ct - hi.astype(f32)).astype(bf16)
    return _dot(hi, y) + _dot(lo, y), _dot(hi, x, "tn") + _dot(lo, x, "tn")


_pair.defvjp(_pair_fwd, _pair_bwd)


def _scan_chunk(s0, r, lw, k, v, a, b, inv=None):
    t = r[0].shape[0]
    ii = lax.broadcasted_iota(jnp.int32, (t, t), 0)
    jj = lax.broadcasted_iota(jnp.int32, (t, t), 1)
    incl = jj <= ii
    strict = jj < ii
    tri = incl.astype(f32)
    eye = jnp.where(ii == jj, 1.0, 0.0)
    cl = [_const_dot(tri, x) for x in lw]
    mid = [c[t // 2 - 1:t // 2, :] for c in cl]
    s0 = [s * jnp.exp(m) for s, m in zip(s0, mid)]
    cl = [c - m for c, m in zip(cl, mid)]
    rt = [_decayed(x, c) for x, c in zip(r, cl)]
    at = [_decayed(x, c - l) for x, c, l in zip(a, cl, lw)]
    bt = [_decayed(x, -c) for x, c in zip(b, cl)]
    kt = [_decayed(x, -c) for x, c in zip(k, cl)]
    l_ab = [jnp.where(strict, _pair(x, y), 0.0) for x, y in zip(at, bt)]
    l_ak = [jnp.where(strict, _pair(x, y), 0.0) for x, y in zip(at, kt)]
    r_b = [jnp.where(incl, _pair(x, y), 0.0) for x, y in zip(rt, bt)]
    r_k = [jnp.where(incl, _pair(x, y), 0.0) for x, y in zip(rt, kt)]
    if inv is None:
        inv = [eye + x for x in l_ab]
        pw = l_ab
        for _ in range(int(math.log2(t)) - 1):
            pw = [_dot(x, x) for x in pw]
            inv = [x + _dot(x, y) for x, y in zip(inv, pw)]
    else:
        inv = [_known_inverse(x, y) for x, y in zip(l_ab, inv)]
    rhs = [_dot(x, s, "nt") + _dot(m, y) for x, s, m, y in zip(at, s0, l_ak, v)]
    u = [_dot(x, y) for x, y in zip(inv, rhs)]
    y_s = [_dot(x, s, "nt") for x, s in zip(rt, s0)]
    y = [ys + _dot(m, uu) + _dot(n, vv) for ys, m, uu, n, vv in zip(y_s, r_b, u, r_k, v)]
    grow = [s + _dot(uu, x, "tn") + _dot(vv, z, "tn") for s, uu, x, vv, z in zip(s0, u, bt, v, kt)]
    s1 = [g * jnp.exp(c[t - 1:t, :]) for g, c in zip(grow, cl)]
    return y, s1, inv


def _head_rows(h):
    return slice(h * RWKV_HEAD, (h + 1) * RWKV_HEAD)


def _per_head(ref):
    return [ref[:, _head_rows(h)] for h in range(RWKV_HEADS)]


def _scan(r, lw, k, v, a, b, *, name):
    lp = r.shape[0]
    nc = lp // CHUNK
    row = pl.BlockSpec((CHUNK, RWKV_DIM), lambda c: (c, 0))

    def body(r_ref, lw_ref, k_ref, v_ref, a_ref, b_ref, y_ref, s_ref, inv_ref, state):
        @pl.when(pl.program_id(0) == 0)
        def _():
            state[...] = jnp.zeros_like(state)

        s_ref[...] = state[...]
        s0 = [state[_head_rows(h), :] for h in range(RWKV_HEADS)]
        y, s1, inv = _scan_chunk(s0, *[_per_head(ref) for ref in (r_ref, lw_ref, k_ref, v_ref, a_ref, b_ref)])
        for h in range(RWKV_HEADS):
            y_ref[:, _head_rows(h)] = y[h]
            state[_head_rows(h), :] = s1[h]
            inv_ref[h * CHUNK:(h + 1) * CHUNK, :] = inv[h].astype(inv_ref.dtype)

    return pl.pallas_call(
        body, name=name, grid=(nc,), in_specs=[row] * 6,
        out_specs=[row, pl.BlockSpec((RWKV_DIM, RWKV_HEAD), lambda c: (c, 0)),
                   pl.BlockSpec((RWKV_HEADS * CHUNK, CHUNK), lambda c: (c, 0))],
        out_shape=[jax.ShapeDtypeStruct((lp, RWKV_DIM), f32), jax.ShapeDtypeStruct((nc * RWKV_DIM, RWKV_HEAD), f32),
                   jax.ShapeDtypeStruct((nc * RWKV_HEADS * CHUNK, CHUNK), bf16)],
        scratch_shapes=[pltpu.VMEM((RWKV_DIM, RWKV_HEAD), f32)],
        compiler_params=_params(("arbitrary",)),
    )(r, lw, k, v, a, b)


def _scan_bwd(r, lw, k, v, a, b, states, inverses, dy, *, name):
    lp = r.shape[0]
    nc = lp // CHUNK
    back = lambda c: (nc - 1 - c, 0)
    row = pl.BlockSpec((CHUNK, RWKV_DIM), back)

    def body(r_ref, lw_ref, k_ref, v_ref, a_ref, b_ref, s_ref, inv_ref, dy_ref,
             dr_ref, dlw_ref, dk_ref, dv_ref, da_ref, db_ref, dstate):
        @pl.when(pl.program_id(0) == 0)
        def _():
            dstate[...] = jnp.zeros_like(dstate)

        outs = (dr_ref, dlw_ref, dk_ref, dv_ref, da_ref, db_ref)
        s0 = [s_ref[_head_rows(h), :] for h in range(RWKV_HEADS)]
        inv = [inv_ref[h * CHUNK:(h + 1) * CHUNK, :].astype(f32) for h in range(RWKV_HEADS)]
        _, vjp = jax.vjp(lambda *args: _scan_chunk(*args, inv=inv)[:2], s0,
                         *[_per_head(ref) for ref in (r_ref, lw_ref, k_ref, v_ref, a_ref, b_ref)])
        g = vjp((_per_head(dy_ref), [dstate[_head_rows(h), :] for h in range(RWKV_HEADS)]))
        for h in range(RWKV_HEADS):
            dstate[_head_rows(h), :] = g[0][h]
            for o_ref, gv in zip(outs, g[1:]):
                o_ref[:, _head_rows(h)] = gv[h]

    shape = jax.ShapeDtypeStruct((lp, RWKV_DIM), f32)
    return pl.pallas_call(
        body, name=name, grid=(nc,),
        in_specs=[row] * 6 + [pl.BlockSpec((RWKV_DIM, RWKV_HEAD), back),
                              pl.BlockSpec((RWKV_HEADS * CHUNK, CHUNK), back), row],
        out_specs=[row] * 6, out_shape=[shape] * 6,
        scratch_shapes=[pltpu.VMEM((RWKV_DIM, RWKV_HEAD), f32)],
        compiler_params=_params(("arbitrary",)),
    )(r, lw, k, v, a, b, states, inverses, dy)


def _loss_head(h2, target, g_final, *, name):
    lp = h2.shape[0]
    per_tile = 3
    tm = per_tile * BLOCK
    last_block = (lp - FRONT) // BLOCK - 1

    def body(h_ref, t0_ref, t1_ref, t2_ref, g_ref, loss_ref, dh_ref, dg_ref):
        i = pl.program_id(0)
        target_rows = jnp.concatenate([t0_ref[...], t1_ref[...], t2_ref[...]], axis=0)
        real = i * tm + lax.broadcasted_iota(jnp.int32, (tm, 1), 0) >= FRONT

        def tile_loss(hv, gv):
            err = _rms(hv, gv) - target_rows
            return 0.5 * jnp.sum(jnp.where(real, jnp.mean(err * err, axis=-1, keepdims=True), 0.0))

        loss, (dh, dg) = jax.value_and_grad(tile_loss, argnums=(0, 1))(h_ref[...], g_ref[...])

        @pl.when(i == 0)
        def _():
            loss_ref[...] = jnp.zeros_like(loss_ref)
            dg_ref[...] = jnp.zeros_like(dg_ref)

        loss_ref[...] += jnp.full(loss_ref.shape, loss, f32)
        dg_ref[...] += dg
        dh_ref[...] = dh

    def target_block(j):
        return pl.BlockSpec((BLOCK, D_MODEL),
                            lambda i: (jnp.clip(per_tile * i + j - FRONT // BLOCK, 0, last_block), 0))

    return pl.pallas_call(
        body, name=name, grid=(lp // tm,),
        in_specs=[pl.BlockSpec((tm, D_MODEL), lambda i: (i, 0)), target_block(0), target_block(1), target_block(2),
                  _full(g_final.shape)],
        out_specs=[_full((8, 128)), pl.BlockSpec((tm, D_MODEL), lambda i: (i, 0)), _full(g_final.shape)],
        out_shape=[jax.ShapeDtypeStruct((8, 128), f32), jax.ShapeDtypeStruct((lp, D_MODEL), f32),
                   jax.ShapeDtypeStruct(g_final.shape, f32)],
        compiler_params=_params(("arbitrary",)),
    )(h2, target, target, target, g_final)


def _embed_norm(x, meta, g, *, name):
    seq = x.shape[0]
    lp = seq + FRONT
    per_tile = 3
    tm = per_tile * BLOCK
    last_block = seq // BLOCK - 1

    def body(x0_ref, x1_ref, x2_ref, meta_ref, g_ref, h_ref, u_ref):
        front = jnp.concatenate([jnp.zeros((PAD, D_MODEL), f32), meta_ref[...]], axis=0)
        first = jnp.where(pl.program_id(0) == 0, front, x0_ref[...])
        h = jnp.concatenate([first, x1_ref[...], x2_ref[...]], axis=0)
        h_ref[...] = h
        u_ref[...] = _rms(h, g_ref[...]).astype(u_ref.dtype)

    def x_block(j):
        return pl.BlockSpec((BLOCK, D_MODEL),
                            lambda i: (jnp.clip(per_tile * i + j - FRONT // BLOCK, 0, last_block), 0))

    tile = pl.BlockSpec((tm, D_MODEL), lambda i: (i, 0))
    return pl.pallas_call(
        body, name=name, grid=(lp // tm,),
        in_specs=[x_block(0), x_block(1), x_block(2), _full(meta.shape), _full(g.shape)],
        out_specs=[tile, tile],
        out_shape=[jax.ShapeDtypeStruct((lp, D_MODEL), f32), jax.ShapeDtypeStruct((lp, D_MODEL), bf16)],
        compiler_params=_params(("parallel",)),
    )(x, x, x, meta, g)


def _input_norm_bwd(h0, g, du, dh1, *, name):
    lp = h0.shape[0]
    blocks = (lp - FRONT) // FRONT
    per_tile = max(n for n in (4, 3, 2, 1) if blocks % n == 0)
    ins = (h0, du, dh1)

    def body(*refs):
        tiles = [refs[k * per_tile:(k + 1) * per_tile] for k in range(len(ins))]
        front_refs = refs[len(ins) * per_tile:len(ins) * (per_tile + 1)]
        g_ref, dx_ref, front_ref, dg_ref = refs[len(ins) * (per_tile + 1):]

        def cotangents(h_ref, du_ref, dh1_ref):
            _, vjp = jax.vjp(lambda hv, gv: (_rms(hv, gv), hv), h_ref[...], g_ref[...])
            return vjp((du_ref[...].astype(f32), dh1_ref[...]))

        @pl.when(pl.program_id(0) == 0)
        def _():
            front_ref[...], dg_ref[...] = cotangents(*front_refs)

        dg = jnp.zeros(dg_ref.shape, f32)
        for j in range(per_tile):
            dh, dg_j = cotangents(*(t[j] for t in tiles))
            dx_ref[j * FRONT:(j + 1) * FRONT, :] = dh
            dg = dg + dg_j
        dg_ref[...] += dg

    def block(j):
        return pl.BlockSpec((FRONT, D_MODEL), lambda i: (per_tile * i + j + 1, 0))

    first = pl.BlockSpec((FRONT, D_MODEL), lambda i: (0, 0))
    return pl.pallas_call(
        body, name=name, grid=(blocks // per_tile,),
        in_specs=[block(j) for _ in ins for j in range(per_tile)] + [first] * len(ins) + [_full(g.shape)],
        out_specs=[pl.BlockSpec((per_tile * FRONT, D_MODEL), lambda i: (i, 0)), _full((FRONT, D_MODEL)),
                   _full(g.shape)],
        out_shape=[jax.ShapeDtypeStruct((lp - FRONT, D_MODEL), f32), jax.ShapeDtypeStruct((FRONT, D_MODEL), f32),
                   jax.ShapeDtypeStruct(g.shape, f32)],
        compiler_params=_params(("arbitrary",)),
    )(*(a for a in ins for _ in range(per_tile)), *ins, g)


def _local_step(x, target, meta, p, early_weights=None, late_weights=None, emit=None):
    emit = emit or (lambda group, grads: 0.0)
    seq = x.shape[0]
    lp = seq + FRONT
    cos_t, sin_t, swap = _rope_tables(lp)
    hsum = _head_sum_matrix(RWKV_DIM, RWKV_HEAD)
    hmean = hsum / RWKV_HEAD
    post_params = [p["ln_w"], p["ln_b"], p["r_k"], hmean]

    h0, u = _embed_norm(x, meta, p["norm_mix_g"], name="norm_mix")
    if early_weights is not None:
        p = {**p, **early_weights(u)}
    prep_params = [p["w0"], p["w2"], p["a0"], p["a2"], p["g2"], p["k_k"], p["k_a"], hsum]
    in_widths = [ATTN_PROJ, RKV_W, LORA_W, 2 * D_MODEL]
    q, k, v, p_rkv, p_lora, gates = _proj_in(u, p["w_in_lr"], p["b_in"], in_widths,
                                             (cos_t, sin_t, swap), name="proj_in", zero_rows_below=PAD)
    y_attn = _attention(q, k, v, p["sinks"], name="attention")

    mix_rkv, mix_lora = p["mix"][:, :RKV_W], p["mix"][:, RKV_W:]
    r_, lw_, k_, v_, a_, b_, g_ = _mixer_inputs([p_rkv, p_lora], [mix_rkv, mix_lora], prep_params,
                                                name="mixer_inputs")
    y_scan, states, inverses = _scan(r_, lw_, k_, v_, a_, b_, name="wkv_scan")
    (y_rwkv,) = _rowwise(_rwkv_post, [y_scan, r_, k_, v_, g_], post_params, [(RWKV_DIM, bf16)], name="rwkv_post")

    if late_weights is not None:
        p = {**p, **late_weights(y_rwkv)}
    br_a, br_r, merged = _branch_merge(y_attn, y_rwkv, p["w_br_attn_t"], p["w_br_rwkv_t"], gates, name="branch_merge")
    h1, f = _residual_norm(merged, p["w_o"], h0, p["norm_ffn_g"], name="out_proj")
    gate, up, act = _ffn_in(f, p["w_gate_t"], p["w_up_t"], name="ffn_in")
    h2 = _mm(act, p["w_down"], "nn", name="ffn_down", add=h1)

    loss8, dh2, d_final_g = _loss_head(h2, target, p["norm_final_g"], name="loss_head")
    dgate, dup = _ffn_in_bwd(dh2, p["w_down"], gate, up, name="ffn_in_bwd")
    d_w_down = _mm_tn(act, dh2, name="dw_down")
    d_w_gate_t = _mm_tn(dgate, f, name="dw_gate")
    d_w_up_t = _mm_tn(dup, f, name="dw_up")
    zero = emit("ffn", dict(w_down=d_w_down, w_gate_t=d_w_gate_t, w_up_t=d_w_up_t))
    dh1, d_ffn_g = _residual_norm_bwd([dgate, dup], [p["w_gate_t"], p["w_up_t"]], h1, p["norm_ffn_g"] + zero, dh2,
                                      name="norm_ffn_bwd")
    dgates, dbr_a, dbr_r, dy_attn, dy_rwkv = _branch_merge_bwd(
        dh1, p["w_o"], gates, br_a, br_r, p["w_br_attn_t"], p["w_br_rwkv_t"], name="branch_merge_bwd")
    d_w_o = _mm_tn(merged, dh1, name="dw_o")
    d_w_br_attn_t = _mm_tn(dbr_a, y_attn, name="dw_br_attn")
    d_w_br_rwkv_t = _mm_tn(dbr_r, y_rwkv, name="dw_br_rwkv")
    zero = emit("branch", dict(w_o=d_w_o, w_br_attn_t=d_w_br_attn_t, w_br_rwkv_t=d_w_br_rwkv_t))

    post_params = [p["ln_w"] + zero, p["ln_b"], p["r_k"], hmean]
    res = _rowwise_bwd(_rwkv_post, [y_scan, r_, k_, v_, g_], post_params, [[dy_rwkv]], name="rwkv_post_bwd",
                       diff_rows=[True] * 5, diff_params=[True, True, True, False])
    dy_scan, dr_p, dk_p, dv_p, dg_p, d_ln_w, d_ln_b, d_r_k = res
    dr_s, dlw_s, dk_s, dv_s, da_s, db_s = _scan_bwd(r_, lw_, k_, v_, a_, b_, states, inverses, dy_scan,
                                                    name="wkv_scan_bwd")
    res = _mixer_inputs_bwd([p_rkv, p_lora], [mix_rkv, mix_lora], prep_params,
                            [[dr_s, dr_p], [dlw_s], [dk_s, dk_p], [dv_s, dv_p], [da_s], [db_s], [dg_p]],
                            name="mixer_inputs_bwd")
    dp_rkv, dp_lora, d_mix_rkv, d_mix_lora, d_w0, d_w2, d_a0, d_a2, d_g2, d_k_k, d_k_a = res

    dq, dk, dv, dkm, dvm, d_sinks = _attention_bwd(q, k, v, p["sinks"], y_attn, dy_attn, name="attention_bwd")
    rest = jnp.zeros((lp - BLOCK, KV_W), f32)
    dkm, dvm = jnp.concatenate([dkm, rest], axis=0), jnp.concatenate([dvm, rest], axis=0)
    (dqkv,) = _rowwise(_attn_prep_transposed, [dq, dk, dkm, dv, dvm, cos_t, sin_t], [swap], [(ATTN_PROJ, bf16)],
                       name="attn_prep_bwd")

    in_rows, (at_qkv, at_rkv, at_lora, at_gates) = p["w_in_lr"].shape[1], [off for off, _ in _pieces(in_widths)]
    d_w_in_t, db_gates = _mm_tn(dgates, u, name="dw_gates", colsum=True, into=(in_rows, at_gates, None))
    d_w_in_t, db_rkv = _mm_tn(dp_rkv, u, name="dw_rkv", colsum=True, into=(in_rows, at_rkv, d_w_in_t))
    d_w_in_t, db_lora = _mm_tn(dp_lora, u, name="dw_lora", colsum=True, into=(in_rows, at_lora, d_w_in_t))
    d_w_in_t, db_qkv = _mm_tn(dqkv, u, name="dw_qkv", colsum=True, into=(in_rows, at_qkv, d_w_in_t))
    zero = emit("input", dict(w_in_t=d_w_in_t, g2=d_g2, w2=d_w2, a2=d_a2))
    du = _proj_in_bwd([dqkv, dp_rkv, dp_lora, dgates], p["w_in_lr"], name="d_u")
    dx, d_front, d_mix_g = _input_norm_bwd(h0, p["norm_mix_g"] + zero, du, dh1, name="norm_mix_bwd")

    grads = dict(
        w_in_t=d_w_in_t,
        b_in=jnp.concatenate([db_qkv, db_rkv, db_lora, db_gates], axis=1),
        mix=jnp.concatenate([d_mix_rkv, d_mix_lora], axis=1),
        norm_mix_g=d_mix_g, sinks=d_sinks, w0=d_w0, w2=d_w2, a0=d_a0, a2=d_a2, g2=d_g2, k_k=d_k_k, k_a=d_k_a,
        r_k=d_r_k, ln_w=d_ln_w, ln_b=d_ln_b, w_br_attn_t=d_w_br_attn_t, w_br_rwkv_t=d_w_br_rwkv_t, w_o=d_w_o,
        norm_ffn_g=d_ffn_g, w_gate_t=d_w_gate_t, w_up_t=d_w_up_t, w_down=d_w_down, norm_final_g=d_final_g,
        meta=d_front[PAD:],
    )
    return loss8[0, 0], dx, grads


def _position():
    return lax.axis_index("x"), lax.axis_index("y"), lax.axis_index("c")


def _other_chips(x, y):
    return [(1 - x, y), (x, 1 - y), (1 - x, 1 - y)]


_HBM = pl.BlockSpec(memory_space=pltpu.HBM)
_SEM = pl.BlockSpec(memory_space=pltpu.SEMAPHORE)
_EFFECT = pltpu.SideEffectType.DATAFLOW_SIDE_EFFECTING


def _landing_zone(src, kind):
    shape = {"whole": (N_CHIPS,) + src.shape, "half": (2, N_CHIPS, src.shape[0], src.shape[1] // 2),
             "slab": (3,) + src.shape[1:], "sibling": src.shape}[kind]
    return lax.empty(shape, src.dtype)


def _copies_per_source(kind):
    return 1 if kind == "sibling" else 3


def _chip_copies(src_refs, land_refs, send_sems, recv_sems, kind):
    x, y, c = _position()
    if kind == "sibling":
        return [pltpu.make_async_remote_copy(
            src_ref=src, dst_ref=land, send_sem=send_sems.at[a], recv_sem=recv_sems.at[a],
            device_id=(x, y, 1 - c), device_id_type=MESH) for a, (src, land) in enumerate(zip(src_refs, land_refs))]
    copies = []
    for a, (src, land) in enumerate(zip(src_refs, land_refs)):
        for j, (px, py) in enumerate(_other_chips(x, y)):
            if kind == "whole":
                src_ref, dst_ref = src, land.at[2 * x + y]
            elif kind == "half":
                half = src.shape[1] // 2
                src_ref, dst_ref = src.at[:, pl.ds(pl.multiple_of(c * half, half), half)], land.at[c, 2 * x + y]
            else:
                src_ref, dst_ref = src.at[2 * px + py], land.at[j]
            copies.append(pltpu.make_async_remote_copy(
                src_ref=src_ref, dst_ref=dst_ref, send_sem=send_sems.at[3 * a + j], recv_sem=recv_sems.at[3 * a + j],
                device_id=(px, py, c), device_id_type=MESH))
    return copies


def _exchange_start(srcs, *, kind, name):
    n = len(srcs)
    lands = [_landing_zone(s, kind) for s in srcs]

    def body(*refs):
        for cp in _chip_copies(refs[:n], refs[n:2 * n], refs[2 * n], refs[2 * n + 1], kind):
            cp.start()
        refs[-1][...] = jnp.zeros_like(refs[-1])

    res = pl.pallas_call(
        body, name=name,
        out_shape=(pltpu.SemaphoreType.DMA((_copies_per_source(kind) * n,)),
                   pltpu.SemaphoreType.DMA((_copies_per_source(kind) * n,)),
                   *[pltpu.HBM(a.shape, a.dtype) for a in srcs + lands], jax.ShapeDtypeStruct((8, 128), f32)),
        in_specs=[_HBM] * (2 * n),
        out_specs=(_SEM, _SEM, *[_HBM] * (2 * n), pl.BlockSpec(memory_space=pltpu.VMEM)),
        input_output_aliases={i: 2 + i for i in range(2 * n)},
        compiler_params=pltpu.CompilerParams(has_side_effects=_EFFECT),
    )(*[pltpu.with_memory_space_constraint(a, pltpu.HBM) for a in srcs + lands])
    return res[0], res[1], list(res[2:2 + n]), list(res[2 + n:2 + 2 * n]), res[-1]


def _exchange_wait(handle, after, *, kind, name):
    send_sems, recv_sems, srcs, lands, _ = handle
    n = len(srcs)

    def body(*refs):
        for cp in _chip_copies(refs[:n], refs[n:2 * n], refs[2 * n], refs[2 * n + 1], kind):
            cp.wait_send()
            cp.wait_recv()

    res = pl.pallas_call(
        body, name=name,
        out_shape=tuple(pltpu.HBM(a.shape, a.dtype) for a in srcs + lands),
        in_specs=[_HBM] * (2 * n) + [_SEM, _SEM, pl.BlockSpec(memory_space=pl.ANY)],
        out_specs=tuple([_HBM] * (2 * n)),
        input_output_aliases={i: i for i in range(2 * n)},
        compiler_params=pltpu.CompilerParams(has_side_effects=_EFFECT),
    )(*srcs, *lands, send_sems, recv_sems, after)
    return list(res[:n]), list(res[n:])


def _sum_own_and_received(g, recv, *, name):
    _, r, w = g.shape
    tm = _tile(r)
    if g.dtype == bf16 and tm % 16:
        tm = r
    x, y, _ = _position()
    me = jnp.reshape(2 * x + y, (1,)).astype(jnp.int32)

    def body(me_ref, g_ref, r_ref, o_ref):
        o_ref[...] = (g_ref[0].astype(f32) + r_ref[0].astype(f32)) + (r_ref[1].astype(f32) + r_ref[2].astype(f32))

    return pl.pallas_call(
        body, name=name,
        grid_spec=pltpu.PrefetchScalarGridSpec(
            num_scalar_prefetch=1, grid=(r // tm,),
            in_specs=[pl.BlockSpec((1, tm, w), lambda i, me_ref: (me_ref[0], i, 0)),
                      pl.BlockSpec((3, tm, w), lambda i, me_ref: (0, i, 0))],
            out_specs=pl.BlockSpec((tm, w), lambda i, me_ref: (i, 0))),
        out_shape=jax.ShapeDtypeStruct((r, w), f32),
        compiler_params=_params(("parallel",)),
    )(me, g, recv)


def _swap_cores(arrs, *, name):
    n = len(arrs)

    def body(*refs):
        x, y, c = _position()
        copies = [pltpu.make_async_remote_copy(
            src_ref=refs[i], dst_ref=refs[n + i], send_sem=refs[2 * n].at[i], recv_sem=refs[2 * n + 1].at[i],
            device_id=(x, y, 1 - c), device_id_type=MESH) for i in range(n)]
        for cp in copies:
            cp.start()
        for cp in copies:
            cp.wait_recv()
        for cp in copies:
            cp.wait_send()

    return pl.pallas_call(
        body, name=name,
        in_specs=[pl.BlockSpec(memory_space=pl.ANY)] * n,
        out_specs=[pl.BlockSpec(memory_space=pl.ANY)] * n,
        out_shape=[jax.ShapeDtypeStruct(a.shape, a.dtype) for a in arrs],
        scratch_shapes=[pltpu.SemaphoreType.DMA((n,)), pltpu.SemaphoreType.DMA((n,))],
    )(*arrs)


def _swap_halves(zone, *, name):
    def body(z_ref, o_ref, send_sems, recv_sems):
        x, y, c = _position()
        mine = [pltpu.make_async_remote_copy(
            src_ref=o_ref.at[c, 2 * px + py], dst_ref=o_ref.at[c, 2 * px + py], send_sem=send_sems.at[j],
            recv_sem=recv_sems.at[j], device_id=(x, y, 1 - c), device_id_type=MESH)
            for j, (px, py) in enumerate(_other_chips(x, y))]
        for cp in mine:
            cp.start()
        for j, (px, py) in enumerate(_other_chips(x, y)):
            pltpu.make_async_remote_copy(
                src_ref=o_ref.at[c, 2 * px + py], dst_ref=o_ref.at[1 - c, 2 * px + py], send_sem=send_sems.at[j],
                recv_sem=recv_sems.at[j], device_id=(x, y, 1 - c), device_id_type=MESH).wait_recv()
        for cp in mine:
            cp.wait_send()

    return pl.pallas_call(
        body, name=name,
        in_specs=[pl.BlockSpec(memory_space=pl.ANY)], out_specs=pl.BlockSpec(memory_space=pl.ANY),
        out_shape=jax.ShapeDtypeStruct(zone.shape, zone.dtype), input_output_aliases={0: 0},
        scratch_shapes=[pltpu.SemaphoreType.DMA((3,)), pltpu.SemaphoreType.DMA((3,))],
    )(zone)


def _all_reduce_small(a, after, *, name):
    rows, w = a.shape

    def body(a_ref, after_ref, o_ref, buf, send_sems, recv_sems):
        x, y, c = _position()
        me = 4 * x + 2 * y + c
        buf[0] = a_ref[...]
        sends = []
        for rel in range(1, N_DEV):
            peer = ((1 - x) if rel & 4 else x, (1 - y) if rel & 2 else y, (1 - c) if rel & 1 else c)
            cp = pltpu.make_async_remote_copy(
                src_ref=a_ref, dst_ref=buf.at[rel], send_sem=send_sems.at[rel - 1], recv_sem=recv_sems.at[rel - 1],
                device_id=peer, device_id_type=MESH)
            cp.start()
            sends.append(cp)
        for cp in sends:
            cp.wait_recv()
        for cp in sends:
            cp.wait_send()
        acc = buf[jnp.bitwise_xor(me, 0)]
        for d in range(1, N_DEV):
            acc = acc + buf[jnp.bitwise_xor(me, d)]
        o_ref[...] = acc

    return pl.pallas_call(
        body, name=name,
        in_specs=[pl.BlockSpec(memory_space=pltpu.VMEM), pl.BlockSpec(memory_space=pl.ANY)],
        out_specs=pl.BlockSpec(memory_space=pltpu.VMEM),
        out_shape=jax.ShapeDtypeStruct((rows, w), f32),
        scratch_shapes=[pltpu.VMEM((N_DEV, rows, w), f32), pltpu.SemaphoreType.DMA((N_DEV - 1,)),
                        pltpu.SemaphoreType.DMA((N_DEV - 1,))],
    )(a, after)


def _adam_math(w, g, m, v):
    nm = ADAM_B1 * m + (1.0 - ADAM_B1) * g
    nv = ADAM_B2 * v + (1.0 - ADAM_B2) * (g * g)
    m_hat = nm / (1.0 - ADAM_B1 ** ADAM_STEP)
    v_hat = nv / (1.0 - ADAM_B2 ** ADAM_STEP)
    return -ADAM_LR * (m_hat / (jnp.sqrt(v_hat) + ADAM_EPS) + ADAM_WD * w), nm, nv


def _adamw(w, g_parts, m, v, *, name, transposed=False):
    rows, cols = w.shape
    if transposed:
        tm = 256 if rows % 256 == 0 else rows
        g_spec = pl.BlockSpec((cols, tm), lambda i: (0, i))
    else:
        tm = _tile(rows, 256)
        g_spec = pl.BlockSpec((tm, cols), lambda i: (i, 0))
    n = len(g_parts)

    def body(*refs):
        w_ref, m_ref, v_ref = refs[0], refs[1 + n], refs[2 + n]
        g_ref, d_ref, nm_ref, nv_ref = refs[3 + n:]
        gv = refs[1][...]
        for part in refs[2:1 + n]:
            gv = gv + part[...]
        if transposed:
            gv = gv.T
        g_ref[...] = gv
        d_ref[...], nm_ref[...], nv_ref[...] = _adam_math(w_ref[...], gv, m_ref[...], v_ref[...])

    spec = pl.BlockSpec((tm, cols), lambda i: (i, 0))
    shape = jax.ShapeDtypeStruct((rows, cols), f32)
    return pl.pallas_call(
        body, name=name, grid=(rows // tm,), in_specs=[spec] + [g_spec] * n + [spec] * 2,
        out_specs=[spec] * 4, out_shape=[shape] * 4,
        compiler_params=_params(("parallel",)),
    )(w, *g_parts, m, v)


def _pad_rows(a, rows):
    return jnp.concatenate([a, jnp.zeros((rows - a.shape[0], a.shape[1]), a.dtype)], axis=0) if rows > a.shape[0] else a


_SMALL = (("norm_mix_g", D_MODEL), ("b_in", D_IN), ("sinks", Q_HEADS), ("mix", RWKV_PROJ), ("w0", RWKV_DIM),
          ("a0", RWKV_DIM), ("k_k", RWKV_DIM), ("k_a", RWKV_DIM), ("r_k", RWKV_DIM), ("ln_w", RWKV_DIM),
          ("ln_b", RWKV_DIM), ("norm_ffn_g", D_MODEL), ("norm_final_g", D_MODEL))


LANES = 128


def _small_layout():
    out, off = {}, 0
    for n, size in _SMALL + (("loss", 1),):
        pieces, col = [], 0
        while col < size:
            row, lane = divmod(off + col, PACK_W)
            width = min(size - col, PACK_W - lane)
            pieces.append((row, lane, width, col))
            col += width
        out[n] = pieces
        off += -(-size // LANES) * LANES
    return out, -(-off // PACK_W)


def _pack_small(d, loss):
    layout, rows = _small_layout()
    parts, used = [], 0
    for n, size in _SMALL + (("loss", 1),):
        item = loss if n == "loss" else d[n]
        fill = -size % LANES
        parts += [item.reshape(-1).astype(f32), jnp.zeros((fill,), f32)]
        used += size + fill
    parts.append(jnp.zeros((rows * PACK_W - used,), f32))
    return jnp.concatenate(parts).reshape(rows, PACK_W)


def _adamw_small(packed, first_row, ws, ms, vs, meta, *, name):
    layout, _ = _small_layout()
    names = [n for n, _ in _SMALL]
    k = len(names)

    def body(*refs):
        packed_ref = refs[0]
        w_refs, m_refs, v_refs = refs[1:1 + k], refs[1 + k:1 + 2 * k], refs[1 + 2 * k:1 + 3 * k]
        meta_refs = refs[1 + 3 * k:5 + 3 * k]
        outs = refs[5 + 3 * k:]
        for idx, n in enumerate(names):
            for row, lane, width, col in layout[n]:
                gv = packed_ref[first_row + row:first_row + row + 1, lane:lane + width]
                at = (slice(None), slice(col, col + width))
                new = _adam_math(w_refs[idx][at], gv, m_refs[idx][at], v_refs[idx][at])
                for o_ref, val in zip(outs[4 * idx:4 * idx + 4], (gv,) + new):
                    o_ref[at] = val
        for o_ref, val in zip(outs[4 * k:], _adam_math(*(r[...] for r in meta_refs))):
            o_ref[...] = val

    ins = [packed] + [d[n] for d in (ws, ms, vs) for n in names] + list(meta)
    shapes = [jax.ShapeDtypeStruct(ws[n].shape, f32) for n in names for _ in range(4)]
    shapes += [jax.ShapeDtypeStruct(meta[0].shape, f32)] * 3
    res = pl.pallas_call(
        body, name=name, grid=(1,), in_specs=[_full(a.shape) for a in ins],
        out_specs=[_full(s.shape) for s in shapes], out_shape=shapes,
        compiler_params=_params(("arbitrary",)),
    )(*ins)
    return {n: res[4 * i:4 * i + 4] for i, n in enumerate(names)}, res[4 * k:]


def kernel(x, meta_tokens, norm_mix_g, w_in, b_in, attn_sinks, rwkv_mix, rwkv_w0, rwkv_w2, rwkv_a0, rwkv_a2, rwkv_g2, rwkv_k_k, rwkv_k_a, rwkv_r_k, rwkv_ln_w, rwkv_ln_b, w_br_attn, w_br_rwkv, w_o, norm_ffn_g, w_ffn_gate, w_ffn_up, w_ffn_down, norm_final_g, loss_target, m_meta_tokens, m_norm_mix_g, m_w_in, m_b_in, m_attn_sinks, m_rwkv_mix, m_rwkv_w0, m_rwkv_w2, m_rwkv_a0, m_rwkv_a2, m_rwkv_g2, m_rwkv_k_k, m_rwkv_k_a, m_rwkv_r_k, m_rwkv_ln_w, m_rwkv_ln_b, m_w_br_attn, m_w_br_rwkv, m_w_o, m_norm_ffn_g, m_w_ffn_gate, m_w_ffn_up, m_w_ffn_down, m_norm_final_g, v_meta_tokens, v_norm_mix_g, v_w_in, v_b_in, v_attn_sinks, v_rwkv_mix, v_rwkv_w0, v_rwkv_w2, v_rwkv_a0, v_rwkv_a2, v_rwkv_g2, v_rwkv_k_k, v_rwkv_k_a, v_rwkv_r_k, v_rwkv_ln_w, v_rwkv_ln_b, v_w_br_attn, v_w_br_rwkv, v_w_o, v_norm_ffn_g, v_w_ffn_gate, v_w_ffn_up, v_w_ffn_down, v_norm_final_g):
    names = ("meta_tokens", "norm_mix_g", "w_in", "b_in", "attn_sinks", "rwkv_mix", "rwkv_w0", "rwkv_w2", "rwkv_a0",
             "rwkv_a2", "rwkv_g2", "rwkv_k_k", "rwkv_k_a", "rwkv_r_k", "rwkv_ln_w", "rwkv_ln_b", "w_br_attn",
             "w_br_rwkv", "w_o", "norm_ffn_g", "w_ffn_gate", "w_ffn_up", "w_ffn_down", "norm_final_g")
    w_all = dict(zip(names, (meta_tokens, norm_mix_g, w_in, b_in, attn_sinks, rwkv_mix, rwkv_w0, rwkv_w2, rwkv_a0,
                             rwkv_a2, rwkv_g2, rwkv_k_k, rwkv_k_a, rwkv_r_k, rwkv_ln_w, rwkv_ln_b, w_br_attn,
                             w_br_rwkv, w_o, norm_ffn_g, w_ffn_gate, w_ffn_up, w_ffn_down, norm_final_g)))
    m_all = dict(zip(names, (m_meta_tokens, m_norm_mix_g, m_w_in, m_b_in, m_attn_sinks, m_rwkv_mix, m_rwkv_w0,
                             m_rwkv_w2, m_rwkv_a0, m_rwkv_a2, m_rwkv_g2, m_rwkv_k_k, m_rwkv_k_a, m_rwkv_r_k,
                             m_rwkv_ln_w, m_rwkv_ln_b, m_w_br_attn, m_w_br_rwkv, m_w_o, m_norm_ffn_g, m_w_ffn_gate,
                             m_w_ffn_up, m_w_ffn_down, m_norm_final_g)))
    v_all = dict(zip(names, (v_meta_tokens, v_norm_mix_g, v_w_in, v_b_in, v_attn_sinks, v_rwkv_mix, v_rwkv_w0,
                             v_rwkv_w2, v_rwkv_a0, v_rwkv_a2, v_rwkv_g2, v_rwkv_k_k, v_rwkv_k_a, v_rwkv_r_k,
                             v_rwkv_ln_w, v_rwkv_ln_b, v_w_br_attn, v_w_br_rwkv, v_w_o, v_norm_ffn_g, v_w_ffn_gate,
                             v_w_ffn_up, v_w_ffn_down, v_norm_final_g)))
    cx, cy, _ = _position()
    chip = 2 * cx + cy

    t_of = dict(w_in_t="w_in", w_gate_t="w_ffn_gate", w_up_t="w_ffn_up", w_br_attn_t="w_br_attn",
                w_br_rwkv_t="w_br_rwkv", g2_t="rwkv_g2", w2_t="rwkv_w2", a2_t="rwkv_a2")
    plain_of = dict(w_down="w_ffn_down", w_o="w_o")
    meta_cols = meta_tokens.shape[1]

    def shard(k):
        return (w_all[t_of[k]][0].T if k in t_of else w_all[plain_of[k]][0]).astype(bf16)

    def whole(zone, own):
        return lax.dynamic_update_slice_in_dim(zone, own[None], chip, axis=0).reshape(-1, own.shape[-1])

    tiny = ("g2_t", "w2_t", "a2_t")
    late = ("w_gate_t", "w_up_t", "w_down", "w_o", "w_br_attn_t", "w_br_rwkv_t")
    w_in_own = shard("w_in_t")
    w_in_rows, w_in_cols = w_in_own.shape
    tiny_h = _exchange_start([shard(k) for k in tiny] + [meta_tokens], kind="whole", name="gather_tiny_start")
    w_in_h = _exchange_start([w_in_own + tiny_h[4][0, 0].astype(bf16)], kind="half", name="gather_w_in_start")
    behind = w_in_h[4][0, 0].astype(bf16)
    late_h = _exchange_start([shard(k) + behind for k in late], kind="whole", name="gather_late_start")
    own, zones = _exchange_wait(tiny_h, late_h[4], kind="whole", name="gather_tiny_wait")
    got = {k: whole(z, o) for k, z, o in zip(tiny, zones, own)}
    meta_full = whole(zones[-1], own[-1]).reshape(N_CHIPS, N_META, meta_cols).transpose(1, 0, 2).reshape(N_META, -1)
    p = dict(
        g2=got["g2_t"].T.astype(f32), w2=got["w2_t"].T.astype(f32), a2=got["a2_t"].T.astype(f32),
        b_in=b_in, sinks=attn_sinks, mix=rwkv_mix, w0=rwkv_w0, a0=rwkv_a0, k_k=rwkv_k_k, k_a=rwkv_k_a,
        r_k=rwkv_r_k.reshape(1, RWKV_DIM), ln_w=rwkv_ln_w, ln_b=rwkv_ln_b, norm_mix_g=norm_mix_g,
        norm_ffn_g=norm_ffn_g, norm_final_g=norm_final_g.reshape(1, D_MODEL),
    )

    def early_weights(after):
        own_h, zones_h = _exchange_wait(w_in_h, after, kind="half", name="gather_w_in_wait")
        zone = _swap_halves(zones_h[0], name="swap_w_in_halves")
        own_halves = own_h[0].reshape(w_in_rows, 2, w_in_cols // 2).transpose(1, 0, 2)[:, None]
        zone = lax.dynamic_update_slice(zone, own_halves, (0, chip, 0, 0))
        return dict(w_in_lr=zone.reshape(2, N_CHIPS * w_in_rows, w_in_cols // 2))

    def late_weights(after):
        own_l, zones_l = _exchange_wait(late_h, after, kind="whole", name="gather_late_wait")
        return {k: whole(z, o) for k, z, o in zip(late, zones_l, own_l)}

    started = {}

    def partial_sums(groups, after):
        parts = {}
        for group in groups:
            keys, handle = started[group]
            slabs, lands = _exchange_wait(handle, after, kind="slab", name="scatter_" + group + "_wait")
            parts.update({k: _sum_own_and_received(s, l, name="sum_chips_" + k) for k, s, l in zip(keys, slabs, lands)})
        return parts

    def emit(group, grads_):
        keys = list(grads_)
        slabs = []
        for k in keys:
            a = grads_[k].T if k in ("g2", "w2", "a2") else grads_[k]
            slabs.append(a.reshape(N_CHIPS, a.shape[0] // N_CHIPS, a.shape[1]))
        started[group] = (keys, _exchange_start(slabs, kind="slab", name="scatter_" + group + "_start"))
        zero = started[group][1][4]
        if group == "input":
            started["parts_a"] = partial_sums(("ffn", "branch"), zero)
            started["swap_a"] = _exchange_start(list(started["parts_a"].values()), kind="sibling",
                                                name="swap_cores_a_start")
            zero = started["swap_a"][4]
        return zero[0, 0]

    loss, dx, g = _local_step(x[0], loss_target[0], meta_full, p, early_weights, late_weights, emit)

    grads, delta, new_m, new_v = {}, {}, {}, {}
    in_grad_layout = ("w_in_t", "w_gate_t", "w_up_t")
    weight_of = {**t_of, **plain_of}

    def update(keys, mine, theirs):
        for k, part, other in zip(keys, mine, theirs):
            both = [part, other]
            k = k + "_t" if k in ("g2", "w2", "a2") else k
            n = weight_of[k]
            shape2 = w_all[n].shape[1:]
            w_, m_, v_ = (a.reshape(shape2) for a in (w_all[n], m_all[n], v_all[n]))
            if k in in_grad_layout:
                res = [t.T for t in _adamw(w_.T, both, m_.T, v_.T, name="adamw_" + n)]
            else:
                res = _adamw(w_, both, m_, v_, name="adamw_" + n, transposed=k in t_of)
            grads[n], delta[n], new_m[n], new_v[n] = (t.reshape(w_all[n].shape) for t in res)
        return delta[n]

    done = update(list(started["parts_a"]),
                  *_exchange_wait(started["swap_a"], dx, kind="sibling", name="swap_cores_a_wait"))
    small = _pack_small(g, loss)
    small_rows8 = -(-(small.shape[0] + N_META) // 8) * 8
    reduced = _all_reduce_small(_pad_rows(jnp.concatenate([g["meta"], small], axis=0), small_rows8), done,
                                name="reduce_small")
    parts_b = partial_sums(("input",), reduced)
    update(list(parts_b), list(parts_b.values()), _swap_cores(list(parts_b.values()), name="swap_cores_b"))
    g_meta = lax.dynamic_slice_in_dim(reduced[:N_META], chip * meta_cols, meta_cols, axis=1)
    (loss_row, loss_lane, _, _), = _small_layout()[0]["loss"]
    loss_total = reduced[N_META + loss_row, loss_lane]

    small_of = dict(norm_mix_g="norm_mix_g", b_in="b_in", attn_sinks="sinks", rwkv_mix="mix", rwkv_w0="w0",
                    rwkv_a0="a0", rwkv_k_k="k_k", rwkv_k_a="k_a", rwkv_r_k="r_k", rwkv_ln_w="ln_w",
                    rwkv_ln_b="ln_b", norm_ffn_g="norm_ffn_g", norm_final_g="norm_final_g")
    as_rows = [{k: src[n].reshape(1, -1) for n, k in small_of.items()} for src in (w_all, m_all, v_all)]
    meta_in = (meta_tokens, g_meta, m_meta_tokens, v_meta_tokens)
    small_out, meta_out = _adamw_small(reduced, N_META, *as_rows, meta_in, name="adamw_small")
    grads["meta_tokens"] = g_meta
    delta["meta_tokens"], new_m["meta_tokens"], new_v["meta_tokens"] = meta_out
    for n, k in small_of.items():
        grads[n], delta[n], new_m[n], new_v[n] = (t.reshape(w_all[n].shape) for t in small_out[k])

    return (loss_total, dx.reshape(x.shape), *[grads[n] for n in names], *[delta[n] for n in names],
            *[new_m[n] for n in names], *[new_v[n] for n in names])
```

```python
import math

import jax
import jax.numpy as jnp
import numpy as np
from jax import lax
from jax.experimental import pallas as pl
from jax.experimental.pallas import tpu as pltpu

f32 = jnp.float32
bf16 = jnp.bfloat16

D_MODEL = 1024
N_META = 16
HEAD_DIM = 64
Q_HEADS = 8
KV_HEADS = 2
GROUP = Q_HEADS // KV_HEADS
WINDOW = 128
BLOCK = 128
ROPE_THETA = 500000.0
ROPE_DIM = HEAD_DIM // 4
RWKV_HEADS = 8
RWKV_HEAD = 64
RWKV_DIM = RWKV_HEADS * RWKV_HEAD
DECAY_LORA = 64
AAA_LORA = 64
GATE_LORA = 160
LORA_W = DECAY_LORA + AAA_LORA + GATE_LORA
RWKV_LN_EPS = 64e-5
D_FF = 2816
Q_W = Q_HEADS * HEAD_DIM
KV_W = KV_HEADS * HEAD_DIM
ATTN_PROJ = Q_W + 2 * KV_W
RKV_W = 3 * RWKV_DIM
RWKV_PROJ = RKV_W + LORA_W
D_IN = ATTN_PROJ + RWKV_PROJ + 2 * D_MODEL
RMS_EPS = 1e-6
NEG_INF = -1e30
PAD = BLOCK - N_META
FRONT = PAD + N_META

ADAM_LR = 0.001
ADAM_B1 = 0.9
ADAM_B2 = 0.999
ADAM_EPS = 1e-08
ADAM_WD = 0.01
ADAM_STEP = 10

N_CHIPS = 4
N_DEV = 8
CHUNK = 128
VMEM_LIMIT = 56 * 1024 * 1024
MM_ROWS = 704
PACK_W = 1024
MESH = pl.DeviceIdType.MESH


def _tile(m, pref=384):
    for step in (16, 8):
        for t in range(min(m, pref) // step * step, 0, -step):
            if m % t == 0:
                return t
    return m


def _params(sem=None):
    return pltpu.CompilerParams(dimension_semantics=sem, vmem_limit_bytes=VMEM_LIMIT)


def _full(shape):
    nd = len(shape)
    return pl.BlockSpec(shape, lambda *_: (0,) * nd)


def _dot(a, b, dims="nn"):
    dn = {"nn": (((1,), (0,)), ((), ())), "nt": (((1,), (1,)), ((), ())), "tn": (((0,), (0,)), ((), ()))}[dims]
    return lax.dot_general(a.astype(bf16), b.astype(bf16), dn, preferred_element_type=f32)


def _two_pass(x, m, dims="nn"):
    x_hi = x.astype(bf16)
    x_lo = (x - x_hi.astype(f32)).astype(bf16)
    return _dot(x_hi, m, dims) + _dot(x_lo, m, dims)


@jax.custom_vjp
def _dot_const(x, m):
    return _two_pass(x, m)


def _dot_const_fwd(x, m):
    return _two_pass(x, m), m


def _dot_const_bwd(m, ct):
    return _two_pass(ct, m, "nt"), jnp.zeros_like(m)


_dot_const.defvjp(_dot_const_fwd, _dot_const_bwd)


def _two_pass_left(m, x, dims):
    x_hi = x.astype(bf16)
    x_lo = (x - x_hi.astype(f32)).astype(bf16)
    return _dot(m, x_hi, dims) + _dot(m, x_lo, dims)


@jax.custom_vjp
def _const_dot(m, x):
    return _two_pass_left(m, x, "nn")


def _const_dot_fwd(m, x):
    return _two_pass_left(m, x, "nn"), m


def _const_dot_bwd(m, ct):
    return jnp.zeros_like(m), _two_pass_left(m, ct, "tn")


_const_dot.defvjp(_const_dot_fwd, _const_dot_bwd)


def _mm(a, b, mode, *, name, out_dtype=f32, bias=None, add=None, zero_rows_below=0):
    m, _ = a.shape
    n = b.shape[1] if mode == "nn" else b.shape[0]
    tm = _tile(m, MM_ROWS)
    has_bias, has_add = bias is not None, add is not None

    def body(*refs):
        a_ref, b_ref = refs[0], refs[1]
        o_ref = refs[-1]
        acc = _dot(a_ref[...], b_ref[...], mode)
        k = 2
        if has_bias:
            acc = acc + refs[k][...]
            k += 1
        if zero_rows_below:
            rows = pl.program_id(0) * tm + lax.broadcasted_iota(jnp.int32, acc.shape, 0)
            acc = jnp.where(rows >= zero_rows_below, acc, 0.0)
        if has_add:
            acc = acc + refs[k][...].astype(f32)
        o_ref[...] = acc.astype(out_dtype)

    ins = [a, b]
    in_specs = [pl.BlockSpec((tm, a.shape[1]), lambda i: (i, 0)), _full(b.shape)]
    if has_bias:
        ins.append(bias)
        in_specs.append(_full(bias.shape))
    if has_add:
        ins.append(add)
        in_specs.append(pl.BlockSpec((tm, n), lambda i: (i, 0)))
    return pl.pallas_call(
        body, name=name, grid=(m // tm,), in_specs=in_specs,
        out_specs=pl.BlockSpec((tm, n), lambda i: (i, 0)),
        out_shape=jax.ShapeDtypeStruct((m, n), out_dtype),
        compiler_params=_params(("parallel",)),
    )(*ins)


def _pieces(widths):
    out, off = [], 0
    for w in widths:
        out.append((off, w))
        off += w
    return out


def _proj_in(a, w_lr, bias, widths, rope, *, name, zero_rows_below=0):
    m, kdim = a.shape
    half = kdim // 2
    tm = _tile(m, MM_ROWS)
    cos_t, sin_t, swap = rope
    out_widths = [Q_W, KV_W, KV_W] + list(widths[1:])

    def body(a_ref, w_ref, b_ref, cos_ref, sin_ref, swap_ref, *outs):
        a_l, a_r = a_ref[:, :half], a_ref[:, half:]
        for j, (off, width) in enumerate(_pieces(widths)):
            acc = _dot(a_l, w_ref[0, off:off + width, :], "nt") + _dot(a_r, w_ref[1, off:off + width, :], "nt")
            acc = acc + b_ref[:, off:off + width]
            if zero_rows_below:
                rows = pl.program_id(0) * tm + lax.broadcasted_iota(jnp.int32, acc.shape, 0)
                acc = jnp.where(rows >= zero_rows_below, acc, 0.0)
            if j == 0:
                qkv = acc.astype(bf16).astype(f32)
                for o_ref, val in zip(outs[:3], _attn_prep(qkv, cos_ref[...], sin_ref[...], swap_ref[...])):
                    o_ref[...] = val.astype(o_ref.dtype)
            else:
                outs[2 + j][...] = acc.astype(outs[2 + j].dtype)

    table = pl.BlockSpec((tm, HEAD_DIM), lambda i: (i, 0))
    return pl.pallas_call(
        body, name=name, grid=(m // tm,),
        in_specs=[pl.BlockSpec((tm, kdim), lambda i: (i, 0)), _full(w_lr.shape), _full(bias.shape), table, table,
                  _full(swap.shape)],
        out_specs=[pl.BlockSpec((tm, w), lambda i: (i, 0)) for w in out_widths],
        out_shape=[jax.ShapeDtypeStruct((m, w), bf16) for w in out_widths],
        compiler_params=_params(("parallel",)),
    )(a, w_lr, bias, cos_t, sin_t, swap)


def _proj_in_bwd(d_list, w_lr, *, name):
    m = d_list[0].shape[0]
    half = w_lr.shape[2]
    widths = [d.shape[1] for d in d_list]
    tm = _tile(m, MM_ROWS)

    def body(*refs):
        w_ref, o_ref = refs[-2], refs[-1]
        for side in range(2):
            acc = None
            for (off, width), d_ref in zip(_pieces(widths), refs):
                term = _dot(d_ref[...], w_ref[side, off:off + width, :])
                acc = term if acc is None else acc + term
            o_ref[:, side * half:(side + 1) * half] = acc.astype(o_ref.dtype)

    return pl.pallas_call(
        body, name=name, grid=(m // tm,),
        in_specs=[pl.BlockSpec((tm, w), lambda i: (i, 0)) for w in widths] + [_full(w_lr.shape)],
        out_specs=pl.BlockSpec((tm, 2 * half), lambda i: (i, 0)),
        out_shape=jax.ShapeDtypeStruct((m, 2 * half), bf16),
        compiler_params=_params(("parallel",)),
    )(*d_list, w_lr)


def _residual_norm(a, w, res, g, *, name):
    m, d = res.shape
    tm = _tile(m, MM_ROWS)

    def body(a_ref, w_ref, r_ref, g_ref, h_ref, n_ref):
        h = _dot(a_ref[...], w_ref[...]) + r_ref[...]
        h_ref[...] = h
        n_ref[...] = _rms(h, g_ref[...]).astype(n_ref.dtype)

    tile = pl.BlockSpec((tm, d), lambda i: (i, 0))
    return pl.pallas_call(
        body, name=name, grid=(m // tm,),
        in_specs=[pl.BlockSpec((tm, a.shape[1]), lambda i: (i, 0)), _full(w.shape), tile, _full(g.shape)],
        out_specs=[tile, tile],
        out_shape=[jax.ShapeDtypeStruct((m, d), f32), jax.ShapeDtypeStruct((m, d), bf16)],
        compiler_params=_params(("parallel",)),
    )(a, w, res, g)


def _residual_norm_bwd(d_list, w_list, h, g, dh_out, *, name):
    m, d = h.shape
    k = len(d_list)
    tm = _tile(m)

    def body(*refs):
        h_ref, g_ref, dho_ref, dh_ref, dg_ref = refs[2 * k:]
        dn = _dot(refs[0][...], refs[k][...])
        for i in range(1, k):
            dn = dn + _dot(refs[i][...], refs[k + i][...])
        _, vjp = jax.vjp(lambda hv, gv: (_rms(hv, gv), hv), h_ref[...], g_ref[...])
        dh, dg = vjp((dn, dho_ref[...]))
        dh_ref[...] = dh

        @pl.when(pl.program_id(0) == 0)
        def _():
            dg_ref[...] = jnp.zeros_like(dg_ref)

        dg_ref[...] += dg

    tile = pl.BlockSpec((tm, d), lambda i: (i, 0))
    return pl.pallas_call(
        body, name=name, grid=(m // tm,),
        in_specs=[pl.BlockSpec((tm, a.shape[1]), lambda i: (i, 0)) for a in d_list] + [_full(w.shape) for w in w_list]
        + [tile, _full(g.shape), tile],
        out_specs=[tile, _full(g.shape)],
        out_shape=[jax.ShapeDtypeStruct((m, d), f32), jax.ShapeDtypeStruct(g.shape, f32)],
        compiler_params=_params(("arbitrary",)),
    )(*d_list, *w_list, h, g, dh_out)


def _mm_tn(a, b, *, name, colsum=False, out_dtype=bf16, into=None):
    r, m = a.shape
    n = b.shape[1]
    tr = _tile(r, 1408)
    tmo = m
    for cand in (1408, 1024, 768, 512):
        if m > 1024 and m % cand == 0:
            tmo = cand
            break
    steps = r // tr

    rows, offset, target = into or (m, 0, None)

    def body(a_ref, b_ref, *rest):
        o_ref, rest = (rest[0], rest[1:]) if target is None else (rest[1], rest[2:])
        acc = rest[-1]
        i = pl.program_id(1)

        @pl.when(i == 0)
        def _():
            acc[...] = jnp.zeros_like(acc)
            if colsum:
                rest[0][...] = jnp.zeros_like(rest[0])

        acc[...] += _dot(a_ref[...], b_ref[...], "tn")
        if colsum:
            rest[0][...] += jnp.sum(a_ref[...].astype(f32), axis=0, keepdims=True)

        @pl.when(i == steps - 1)
        def _():
            o_ref[...] = acc[...].astype(out_dtype)

    out_shape = [jax.ShapeDtypeStruct((rows, n), out_dtype)]
    if offset % tmo == 0:
        out_specs = [pl.BlockSpec((tmo, n), lambda j, i: (offset // tmo + j, 0))]
    else:
        out_specs = [pl.BlockSpec((pl.Element(tmo), pl.Element(n)), lambda j, i: (pl.multiple_of(offset + j * tmo, math.gcd(offset, tmo)), 0))]
    if colsum:
        out_shape.append(jax.ShapeDtypeStruct((1, m), f32))
        out_specs.append(pl.BlockSpec((1, tmo), lambda j, i: (0, j)))
    in_specs = [pl.BlockSpec((tr, tmo), lambda j, i: (i, j)), pl.BlockSpec((tr, n), lambda j, i: (i, 0))]
    res = pl.pallas_call(
        body, name=name, grid=(m // tmo, steps),
        in_specs=in_specs + ([] if target is None else [pl.BlockSpec(memory_space=pl.ANY)]),
        out_specs=out_specs, out_shape=out_shape,
        scratch_shapes=[pltpu.VMEM((tmo, n), f32)],
        input_output_aliases={} if target is None else {2: 0},
        compiler_params=_params(("parallel", "arbitrary")),
    )(a, b, *([] if target is None else [target]))
    return res if colsum else res[0]


def _rowwise(fn, rows, params, outs, *, name, tm=None):
    m = rows[0].shape[0]
    tm = tm or _tile(m, MM_ROWS)
    nr, npar = len(rows), len(params)

    def body(*refs):
        vals = [r[...] for r in refs[:nr + npar]]
        res = fn(*vals)
        for o_ref, v in zip(refs[nr + npar:], res):
            o_ref[...] = v.astype(o_ref.dtype)

    return pl.pallas_call(
        body, name=name, grid=(m // tm,),
        in_specs=[pl.BlockSpec((tm, r.shape[1]), lambda i: (i, 0)) for r in rows] + [_full(p.shape) for p in params],
        out_specs=[pl.BlockSpec((tm, w), lambda i: (i, 0)) for w, _ in outs],
        out_shape=[jax.ShapeDtypeStruct((m, w), dt) for w, dt in outs],
        compiler_params=_params(("parallel",)),
    )(*rows, *params)


def _rowwise_bwd(fn, rows, params, cts, *, name, diff_rows, diff_params, tm=None, zero_rows_below=0, out_dtypes=None):
    m = rows[0].shape[0]
    tm = tm or _tile(m)
    nr, npar = len(rows), len(params)
    d_idx = [i for i in range(nr) if diff_rows[i]]
    p_idx = [i for i in range(npar) if diff_params[i]]
    out_dtypes = out_dtypes or [f32] * len(d_idx)
    flat_cts = [c for group in cts for c in group]
    n_ct = len(flat_cts)

    def body(*refs):
        vals = [r[...] for r in refs[:nr + npar]]
        ct_refs = refs[nr + npar:nr + npar + n_ct]
        out_refs = refs[nr + npar + n_ct:]
        ct_vals, k = [], 0
        for group in cts:
            acc = ct_refs[k][...].astype(f32)
            for extra in range(1, len(group)):
                acc = acc + ct_refs[k + extra][...].astype(f32)
            k += len(group)
            if zero_rows_below:
                rr = pl.program_id(0) * tm + lax.broadcasted_iota(jnp.int32, acc.shape, 0)
                acc = jnp.where(rr >= zero_rows_below, acc, 0.0)
            ct_vals.append(acc)

        def g(*dargs):
            full = list(vals)
            for pos, i in enumerate(d_idx):
                full[i] = dargs[pos]
            for pos, i in enumerate(p_idx):
                full[nr + i] = dargs[len(d_idx) + pos]
            return tuple(fn(*full))

        _, vjp = jax.vjp(g, *[vals[i].astype(f32) for i in d_idx], *[vals[nr + i] for i in p_idx])
        grads = vjp(tuple(ct_vals))
        for pos in range(len(d_idx)):
            out_refs[pos][...] = grads[pos].astype(out_refs[pos].dtype)
        first = pl.program_id(0) == 0
        for pos in range(len(p_idx)):
            o_ref = out_refs[len(d_idx) + pos]

            @pl.when(first)
            def _(o_ref=o_ref):
                o_ref[...] = jnp.zeros_like(o_ref)

            o_ref[...] += grads[len(d_idx) + pos]

    return pl.pallas_call(
        body, name=name, grid=(m // tm,),
        in_specs=[pl.BlockSpec((tm, r.shape[1]), lambda i: (i, 0)) for r in rows] + [_full(p.shape) for p in params]
        + [pl.BlockSpec((tm, c.shape[1]), lambda i: (i, 0)) for c in flat_cts],
        out_specs=[pl.BlockSpec((tm, rows[i].shape[1]), lambda i_: (i_, 0)) for i in d_idx]
        + [_full(params[i].shape) for i in p_idx],
        out_shape=[jax.ShapeDtypeStruct(rows[i].shape, dt) for i, dt in zip(d_idx, out_dtypes)]
        + [jax.ShapeDtypeStruct(params[i].shape, f32) for i in p_idx],
        compiler_params=_params(("arbitrary",)),
    )(*rows, *params, *flat_cts)


def _rms(x, g):
    return x * lax.rsqrt(jnp.mean(x * x, axis=-1, keepdims=True) + RMS_EPS) * g


def _head_sum_matrix(width, head):
    idx = jnp.arange(width) // head
    return (idx[:, None] == idx[None, :]).astype(f32)


def _rope_tables(lp):
    half = ROPE_DIM // 2
    pos = (np.arange(lp) - PAD).astype(np.float32)
    inv_freq = np.power(np.float32(ROPE_THETA), -np.arange(half, dtype=np.float32) * np.float32(2.0 / ROPE_DIM))
    ang = pos[:, None] * inv_freq[None, :].astype(np.float32)
    cos, sin = np.cos(ang), np.sin(ang)
    ones = np.ones((lp, HEAD_DIM - ROPE_DIM), np.float32)
    cos_t = np.concatenate([cos, cos, ones], axis=1)
    sin_t = np.concatenate([-sin, sin, 0.0 * ones], axis=1)
    i = np.arange(HEAD_DIM)
    src = np.where(i < half, i + half, np.where(i < ROPE_DIM, i - half, i))
    swap = ((i[:, None] == src[None, :]) & (i[None, :] < ROPE_DIM)).astype(np.float32)
    return jnp.asarray(cos_t, f32), jnp.asarray(sin_t, f32), jnp.asarray(swap, f32)


def _attn_prep(qkv, cos_t, sin_t, swap):
    outs = []
    for h in range(Q_HEADS + KV_HEADS):
        t = qkv[:, h * HEAD_DIM:(h + 1) * HEAD_DIM]
        outs.append(t * cos_t + _dot_const(t, swap) * sin_t)
    q = jnp.concatenate(outs[:Q_HEADS], axis=1)
    k = jnp.concatenate(outs[Q_HEADS:], axis=1)
    return q, k, qkv[:, Q_W + KV_W:]


def _attn_prep_transposed(dq, dk, dv, cos_t, sin_t, swap, dk_meta, dv_meta):
    first = pl.program_id(0) == 0

    def with_meta(d, d_meta):
        rest = jnp.zeros((d.shape[0] - BLOCK, KV_W), f32)
        return d.astype(f32) + jnp.where(first, jnp.concatenate([d_meta, rest], axis=0), 0.0)

    parts = []
    for d, heads in ((dq.astype(f32), Q_HEADS), (with_meta(dk, dk_meta), KV_HEADS)):
        for h in range(heads):
            t = d[:, h * HEAD_DIM:(h + 1) * HEAD_DIM]
            parts.append(t * cos_t + _two_pass(t * sin_t, swap, "nt"))
    return (jnp.concatenate(parts + [with_meta(dv, dv_meta)], axis=1),)


def _softplus(z):
    return jnp.maximum(z, 0.0) + jnp.log1p(jnp.exp(-jnp.abs(z)))


def _rwkv_prep(rkv, lora, w0, w2, a0, a2, g2, k_k, k_a, hsum):
    r = rkv[:, :RWKV_DIM]
    k = rkv[:, RWKV_DIM:2 * RWKV_DIM]
    v = rkv[:, 2 * RWKV_DIM:]
    dw = lora[:, :DECAY_LORA]
    da = lora[:, DECAY_LORA:DECAY_LORA + AAA_LORA]
    dg = lora[:, DECAY_LORA + AAA_LORA:]
    w = -_softplus(-(w0 + _dot(jnp.tanh(dw), w2))) - 0.5
    a = jax.nn.sigmoid(a0 + _dot(da, a2))
    g = _dot(jax.nn.sigmoid(dg), g2)
    kk = k * k_k
    kk = kk * lax.rsqrt(jnp.maximum(_dot_const(kk * kk, hsum), 1e-24))
    k = k * (1.0 + (a - 1.0) * k_a)
    log_decay = -jnp.exp(w)
    return r, log_decay, k, v, -kk, kk * a, g


def _rwkv_post(y, r, k, v, g, ln_w, ln_b, r_k, hmean):
    hsum = hmean * RWKV_HEAD
    mean = _dot_const(y, hmean)
    yc = y - mean
    var = _dot_const(yc * yc, hmean)
    yn = yc * lax.rsqrt(var + RWKV_LN_EPS) * ln_w + ln_b
    bonus = _dot_const(r * k * r_k, hsum) * v
    return ((yn + bonus) * g,)


def _merge(gates, br_a, br_r):
    sg = jax.nn.sigmoid(gates)
    return (sg[:, :D_MODEL] * br_a + sg[:, D_MODEL:] * br_r,)


def _swiglu(gate, up):
    return (jax.nn.silu(gate) * up,)


def _ffn_in(f, w_gate_t, w_up_t, *, name):
    m, d = f.shape
    n = w_gate_t.shape[0]
    tm = _tile(m)

    def body(f_ref, wg_ref, wu_ref, g_ref, u_ref, a_ref):
        g = _dot(f_ref[...], wg_ref[...], "nt")
        u = _dot(f_ref[...], wu_ref[...], "nt")
        g_ref[...] = g.astype(g_ref.dtype)
        u_ref[...] = u.astype(u_ref.dtype)
        a_ref[...] = _swiglu(g, u)[0].astype(a_ref.dtype)

    spec = pl.BlockSpec((tm, n), lambda i: (i, 0))
    return pl.pallas_call(
        body, name=name, grid=(m // tm,),
        in_specs=[pl.BlockSpec((tm, d), lambda i: (i, 0)), _full(w_gate_t.shape), _full(w_up_t.shape)],
        out_specs=[spec] * 3, out_shape=[jax.ShapeDtypeStruct((m, n), bf16)] * 3,
        compiler_params=_params(("parallel",)),
    )(f, w_gate_t, w_up_t)


def _branch_merge(y_attn, y_rwkv, w_attn_t, w_rwkv_t, gates, *, name):
    m = y_attn.shape[0]
    tm = _tile(m, MM_ROWS)

    def body(ya_ref, yr_ref, wa_ref, wr_ref, g_ref, a_ref, r_ref, o_ref):
        br_a = _dot(ya_ref[...], wa_ref[...], "nt")
        br_r = _dot(yr_ref[...], wr_ref[...], "nt")
        a_ref[...] = br_a.astype(a_ref.dtype)
        r_ref[...] = br_r.astype(r_ref.dtype)
        o_ref[...] = _merge(g_ref[...].astype(f32), br_a, br_r)[0].astype(o_ref.dtype)

    rows = lambda a: pl.BlockSpec((tm, a.shape[1]), lambda i: (i, 0))
    spec = pl.BlockSpec((tm, D_MODEL), lambda i: (i, 0))
    return pl.pallas_call(
        body, name=name, grid=(m // tm,),
        in_specs=[rows(y_attn), rows(y_rwkv), _full(w_attn_t.shape), _full(w_rwkv_t.shape), rows(gates)],
        out_specs=[spec] * 3, out_shape=[jax.ShapeDtypeStruct((m, D_MODEL), bf16)] * 3,
        compiler_params=_params(("parallel",)),
    )(y_attn, y_rwkv, w_attn_t, w_rwkv_t, gates)


def _branch_merge_bwd(dh, w_o, gates, br_a, br_r, w_attn_t, w_rwkv_t, *, name):
    m = dh.shape[0]
    tm = _tile(m, MM_ROWS)

    def body(dh_ref, w_ref, g_ref, a_ref, r_ref, wa_ref, wr_ref, dg_ref, da_ref, dr_ref, dya_ref, dyr_ref):
        dmerged = _dot(dh_ref[...], w_ref[...], "nt")
        _, vjp = jax.vjp(lambda g, a, r: _merge(g, a, r)[0], g_ref[...].astype(f32), a_ref[...].astype(f32),
                         r_ref[...].astype(f32))
        dg, da, dr = vjp(dmerged)
        dg_ref[...] = dg.astype(dg_ref.dtype)
        da_ref[...] = da.astype(da_ref.dtype)
        dr_ref[...] = dr.astype(dr_ref.dtype)
        dya_ref[...] = _dot(da, wa_ref[...])
        dyr_ref[...] = _dot(dr, wr_ref[...])

    rows = lambda a: pl.BlockSpec((tm, a.shape[1]), lambda i: (i, 0))
    mixer = pl.BlockSpec((tm, w_attn_t.shape[1]), lambda i: (i, 0))
    return pl.pallas_call(
        body, name=name, grid=(m // tm,),
        in_specs=[rows(dh), _full(w_o.shape), rows(gates), rows(br_a), rows(br_r), _full(w_attn_t.shape),
                  _full(w_rwkv_t.shape)],
        out_specs=[rows(gates), rows(br_a), rows(br_r), mixer, mixer],
        out_shape=[jax.ShapeDtypeStruct(gates.shape, bf16), jax.ShapeDtypeStruct(br_a.shape, bf16),
                   jax.ShapeDtypeStruct(br_r.shape, bf16), jax.ShapeDtypeStruct((m, w_attn_t.shape[1]), f32),
                   jax.ShapeDtypeStruct((m, w_rwkv_t.shape[1]), f32)],
        compiler_params=_params(("parallel",)),
    )(dh, w_o, gates, br_a, br_r, w_attn_t, w_rwkv_t)


def _ffn_in_bwd(dh, w_down, gate, up, *, name):
    m, d = dh.shape
    n = w_down.shape[0]
    tm = _tile(m)

    def body(dh_ref, w_ref, g_ref, u_ref, dg_ref, du_ref):
        dact = _dot(dh_ref[...], w_ref[...], "nt")
        _, vjp = jax.vjp(lambda a, b: _swiglu(a, b)[0], g_ref[...].astype(f32), u_ref[...].astype(f32))
        dg, du = vjp(dact)
        dg_ref[...] = dg.astype(dg_ref.dtype)
        du_ref[...] = du.astype(du_ref.dtype)

    spec = pl.BlockSpec((tm, n), lambda i: (i, 0))
    return pl.pallas_call(
        body, name=name, grid=(m // tm,),
        in_specs=[pl.BlockSpec((tm, d), lambda i: (i, 0)), _full(w_down.shape), spec, spec],
        out_specs=[spec] * 2, out_shape=[jax.ShapeDtypeStruct((m, n), bf16)] * 2,
        compiler_params=_params(("parallel",)),
    )(dh, w_down, gate, up)


HALO = 16


def _previous_rows(x, before_ref, first_tile):
    rows = lax.broadcasted_iota(jnp.int32, x.shape, 0)
    last = jnp.where(first_tile, 0.0, before_ref[HALO - 1:HALO, :].astype(f32))
    return jnp.where(rows == 0, last, pltpu.roll(x, 1, axis=0))


def _mixer_inputs(ps, mixes, params, *, name):
    m = ps[0].shape[0]
    tm = _tile(m)
    sub = tm // HALO
    n_par = len(params)

    def body(*refs):
        first = pl.program_id(0) == 0
        pf = []
        for k in range(2):
            x = refs[k][...].astype(f32)
            pf.append(x + (_previous_rows(x, refs[2 + k], first) - x) * refs[4 + k][...])
        res = _rwkv_prep(*pf, *[ref[...] for ref in refs[6:6 + n_par]])
        for o_ref, val in zip(refs[6 + n_par:], res):
            o_ref[...] = val

    tile = lambda a: pl.BlockSpec((tm, a.shape[1]), lambda i: (i, 0))
    before = lambda a: pl.BlockSpec((HALO, a.shape[1]), lambda i: (jnp.maximum(i * sub - 1, 0), 0))
    out = pl.BlockSpec((tm, RWKV_DIM), lambda i: (i, 0))
    return pl.pallas_call(
        body, name=name, grid=(m // tm,),
        in_specs=[tile(a) for a in ps] + [before(a) for a in ps] + [_full(a.shape) for a in mixes + params],
        out_specs=[out] * 7, out_shape=[jax.ShapeDtypeStruct((m, RWKV_DIM), f32)] * 7,
        compiler_params=_params(("parallel",)),
    )(*ps, *ps, *mixes, *params)


def _mixer_inputs_bwd(ps, mixes, params, cts, *, name):
    m = ps[0].shape[0]
    tm = _tile(m)
    sub = tm // HALO
    nt = m // tm
    n_par = len(params)
    flat_cts = [c for group in cts for c in group]
    n_ct = len(flat_cts)

    def body(*refs):
        i = pl.program_id(0)
        tile_index = nt - 1 - i
        ct_refs = refs[6 + n_par:6 + n_par + n_ct]
        dp_refs = refs[6 + n_par + n_ct:8 + n_par + n_ct]
        dmix_refs = refs[8 + n_par + n_ct:10 + n_par + n_ct]
        dpar_refs = refs[10 + n_par + n_ct:9 + 2 * n_par + n_ct]
        carries = refs[9 + 2 * n_par + n_ct:]
        rows1 = tile_index * tm + lax.broadcasted_iota(jnp.int32, (tm, 1), 0)
        live = rows1 >= PAD

        @pl.when(i == 0)
        def _():
            for ref in (*dmix_refs, *dpar_refs, *carries):
                ref[...] = jnp.zeros_like(ref)

        xs, prevs, pf = [], [], []
        for k in range(2):
            x = refs[k][...].astype(f32)
            xp = _previous_rows(x, refs[2 + k], tile_index == 0)
            xs.append(x)
            prevs.append(xp)
            pf.append(x + (xp - x) * refs[4 + k][...])
        ct_vals, pos = [], 0
        for group in cts:
            acc = ct_refs[pos][...].astype(f32)
            for extra in range(1, len(group)):
                acc = acc + ct_refs[pos + extra][...].astype(f32)
            pos += len(group)
            ct_vals.append(jnp.where(live, acc, 0.0))
        par_vals = [ref[...] for ref in refs[6:6 + n_par]]
        _, vjp = jax.vjp(lambda *args: _rwkv_prep(*args, par_vals[-1]), *pf, *par_vals[:-1])
        g = vjp(tuple(ct_vals))
        for k in range(2):
            dpf = g[k]
            mixv = refs[4 + k][...]
            dm = dpf * mixv
            rows = lax.broadcasted_iota(jnp.int32, dm.shape, 0)
            dm_next = jnp.where(rows == tm - 1, carries[k][...], pltpu.roll(dm, tm - 1, axis=0))
            dp_refs[k][...] = jnp.where(live, dpf - dm + dm_next, 0.0).astype(dp_refs[k].dtype)
            carries[k][...] = dm[0:1, :]
            dmix_refs[k][...] += jnp.sum(dpf * (prevs[k] - xs[k]), axis=0, keepdims=True)
        for ref, val in zip(dpar_refs, g[2:]):
            ref[...] += val

    tile = lambda a: pl.BlockSpec((tm, a.shape[1]), lambda i: (nt - 1 - i, 0))
    before = lambda a: pl.BlockSpec((HALO, a.shape[1]), lambda i: (jnp.maximum((nt - 1 - i) * sub - 1, 0), 0))
    return pl.pallas_call(
        body, name=name, grid=(nt,),
        in_specs=[tile(a) for a in ps] + [before(a) for a in ps] + [_full(a.shape) for a in mixes + params]
        + [tile(c) for c in flat_cts],
        out_specs=[tile(a) for a in ps] + [_full(a.shape) for a in mixes + params[:-1]],
        out_shape=[jax.ShapeDtypeStruct(a.shape, bf16) for a in ps]
        + [jax.ShapeDtypeStruct(a.shape, f32) for a in mixes + params[:-1]],
        scratch_shapes=[pltpu.VMEM((1, a.shape[1]), f32) for a in ps],
        compiler_params=_params(("arbitrary",)),
    )(*ps, *ps, *mixes, *params, *flat_cts)


def _attn_masks(blk):
    qi = lax.broadcasted_iota(jnp.int32, (BLOCK, BLOCK), 0)
    ki = lax.broadcasted_iota(jnp.int32, (BLOCK, BLOCK), 1)
    qpos = blk * BLOCK + qi - PAD
    kpos_c = blk * BLOCK + ki - PAD
    kpos_p = kpos_c - BLOCK
    kpos_m = ki - PAD

    def band(kpos):
        return (kpos >= N_META) & (kpos <= qpos) & (qpos - kpos < WINDOW)

    return band(kpos_p), band(kpos_c), (kpos_m >= 0) & (kpos_m <= qpos)


def _attn_probs(qs, k3s, sink, oks):
    s = [[jnp.where(ok, _dot(qh, kx, "nt"), NEG_INF) for kx, ok in zip(k3, oks)] for qh, k3 in zip(qs, k3s)]
    mx = [jnp.maximum(jnp.maximum(jnp.max(t[0], -1, keepdims=True), jnp.max(t[1], -1, keepdims=True)),
                      jnp.maximum(jnp.max(t[2], -1, keepdims=True), sk)) for t, sk in zip(s, sink)]
    e = [[jnp.exp(tx - m) for tx in t] for t, m in zip(s, mx)]
    e_sink = [jnp.exp(sk - m) for sk, m in zip(sink, mx)]
    inv = [1.0 / (jnp.sum(t[0], -1, keepdims=True) + jnp.sum(t[1], -1, keepdims=True)
                  + jnp.sum(t[2], -1, keepdims=True) + es) for t, es in zip(e, e_sink)]
    return [[tx * i for tx in t] for t, i in zip(e, inv)], [es * i for es, i in zip(e_sink, inv)]


def _head_cols(i):
    return slice(i * HEAD_DIM, (i + 1) * HEAD_DIM)


def _attn_operands(refs):
    q_ref, kp_ref, kc_ref, km_ref, vp_ref, vc_ref, vm_ref, s_ref = refs
    qs = [q_ref[:, _head_cols(i)] * (HEAD_DIM ** -0.5) for i in range(Q_HEADS)]
    k3 = [[ref[:, _head_cols(h)] for ref in (kp_ref, kc_ref, km_ref)] for h in range(KV_HEADS)]
    v3 = [[ref[:, _head_cols(h)] for ref in (vp_ref, vc_ref, vm_ref)] for h in range(KV_HEADS)]
    return (qs, [k3[i // GROUP] for i in range(Q_HEADS)], [v3[i // GROUP] for i in range(Q_HEADS)],
            [s_ref[:, i:i + 1] for i in range(Q_HEADS)])


def _attention(q, k, v, sinks, *, name):
    lp = q.shape[0]
    nb = lp // BLOCK
    prev = lambda i: (jnp.maximum(i - 1, 0), 0)
    cur = lambda i: (i, 0)
    meta = lambda i: (0, 0)
    kv = lambda index: pl.BlockSpec((BLOCK, KV_W), index)

    def body(*refs):
        o_ref = refs[-1]
        qs, k3s, v3s, sink = _attn_operands(refs[:-1])
        p, _ = _attn_probs(qs, k3s, sink, _attn_masks(pl.program_id(0)))
        out = [_dot(ph[0], v3[0]) + _dot(ph[1], v3[1]) + _dot(ph[2], v3[2]) for ph, v3 in zip(p, v3s)]
        for i in range(Q_HEADS):
            o_ref[:, _head_cols(i)] = out[i].astype(o_ref.dtype)

    return pl.pallas_call(
        body, name=name, grid=(nb,),
        in_specs=[pl.BlockSpec((BLOCK, Q_W), cur), kv(prev), kv(cur), kv(meta), kv(prev), kv(cur), kv(meta),
                  _full((1, Q_HEADS))],
        out_specs=pl.BlockSpec((BLOCK, Q_W), cur),
        out_shape=jax.ShapeDtypeStruct((lp, Q_W), bf16),
        compiler_params=_params(("parallel",)),
    )(q, k, k, k, v, v, v, sinks)


def _attention_bwd(q, k, v, sinks, out, do, *, name):
    lp = q.shape[0]
    nb = lp // BLOCK
    cur = lambda n: (jnp.minimum(n, nb - 1), 0)
    prev = lambda n: (jnp.maximum(jnp.minimum(n, nb - 1) - 1, 0), 0)
    behind = lambda n: (jnp.maximum(n - 1, 0), 0)
    meta = lambda n: (0, 0)
    kv = lambda index: pl.BlockSpec((BLOCK, KV_W), index)
    scale = HEAD_DIM ** -0.5

    def body(*refs):
        ins, fwd_ref, do_ref = refs[:8], refs[8], refs[9]
        dq_ref, dk_ref, dv_ref, dkm_ref, dvm_ref, ds_ref, carry_k, carry_v = refs[10:]
        n = pl.program_id(0)

        @pl.when(n == 0)
        def _():
            for ref in (dkm_ref, dvm_ref, ds_ref, carry_k, carry_v):
                ref[...] = jnp.zeros_like(ref)

        @pl.when(n < nb)
        def _():
            qs, k3s, v3s, sink = _attn_operands(ins)
            do = [do_ref[:, _head_cols(i)] for i in range(Q_HEADS)]
            p, p_sink = _attn_probs(qs, k3s, sink, _attn_masks(n))
            delta = [jnp.sum(d * fwd_ref[:, _head_cols(i)].astype(f32), -1, keepdims=True) for i, d in enumerate(do)]
            dp = [[_dot(d, vx, "nt") for vx in v3] for d, v3 in zip(do, v3s)]
            ds = [[px * (dx - dl) for px, dx in zip(ph, dh)] for ph, dh, dl in zip(p, dp, delta)]
            dq = [_dot(dsh[0], k3[0]) + _dot(dsh[1], k3[1]) + _dot(dsh[2], k3[2]) for dsh, k3 in zip(ds, k3s)]
            for i in range(Q_HEADS):
                dq_ref[:, _head_cols(i)] = dq[i] * scale
                ds_ref[:, i:i + 1] -= jnp.sum(p_sink[i] * delta[i], axis=0, keepdims=True)
            for h in range(KV_HEADS):
                group = slice(h * GROUP, (h + 1) * GROUP)
                q_all = jnp.concatenate(qs[group], axis=0)
                do_all = jnp.concatenate(do[group], axis=0)
                dk3 = [_dot(jnp.concatenate([dsh[x] for dsh in ds[group]], axis=0), q_all, "tn") for x in range(3)]
                dv3 = [_dot(jnp.concatenate([ph[x] for ph in p[group]], axis=0), do_all, "tn") for x in range(3)]
                hs = _head_cols(h)
                for out_ref, carry, meta_ref, d3 in ((dk_ref, carry_k, dkm_ref, dk3),
                                                     (dv_ref, carry_v, dvm_ref, dv3)):
                    out_ref[:, hs] = carry[:, hs] + d3[0]
                    carry[:, hs] = d3[1]
                    meta_ref[:, hs] += d3[2]

        @pl.when(n == nb)
        def _():
            dk_ref[...] = carry_k[...]
            dv_ref[...] = carry_v[...]

    kv_shape = jax.ShapeDtypeStruct((lp, KV_W), f32)
    one_shape = jax.ShapeDtypeStruct((BLOCK, KV_W), f32)
    return pl.pallas_call(
        body, name=name, grid=(nb + 1,),
        in_specs=[pl.BlockSpec((BLOCK, Q_W), cur), kv(prev), kv(cur), kv(meta), kv(prev), kv(cur), kv(meta),
                  _full((1, Q_HEADS)), pl.BlockSpec((BLOCK, Q_W), cur), pl.BlockSpec((BLOCK, Q_W), cur)],
        out_specs=[pl.BlockSpec((BLOCK, Q_W), cur), kv(behind), kv(behind), kv(meta), kv(meta),
                   _full((1, Q_HEADS))],
        out_shape=[jax.ShapeDtypeStruct((lp, Q_W), f32), kv_shape, kv_shape, one_shape, one_shape,
                   jax.ShapeDtypeStruct((1, Q_HEADS), f32)],
        scratch_shapes=[pltpu.VMEM((BLOCK, KV_W), f32), pltpu.VMEM((BLOCK, KV_W), f32)],
        compiler_params=_params(("arbitrary",)),
    )(q, k, k, k, v, v, v, sinks, out, do)


@jax.custom_vjp
def _known_inverse(l, x):
    return x


def _known_inverse_fwd(l, x):
    return x, x


def _known_inverse_bwd(x, ct):
    return _dot(_dot(x, ct, "tn"), x, "nt"), jnp.zeros_like(x)


_known_inverse.defvjp(_known_inverse_fwd, _known_inverse_bwd)


@jax.custom_vjp
def _decayed(x, c):
    return (x * jnp.exp(c)).astype(bf16).astype(f32)


def _decayed_fwd(x, c):
    e = jnp.exp(c)
    out = (x * e).astype(bf16).astype(f32)
    return out, (e, out)


def _decayed_bwd(res, ct):
    e, out = res
    return ct * e, ct * out


_decayed.defvjp(_decayed_fwd, _decayed_bwd)


@jax.custom_vjp
def _pair(x, y):
    return _dot(x, y, "nt")


def _pair_fwd(x, y):
    return _dot(x, y, "nt"), (x, y)


def _pair_bwd(res, ct):
    x, y = res
    hi = ct.astype(bf16)
    lo = (ct - hi.astype(f32)).astype(bf16)
    return _dot(hi, y) + _dot(lo, y), _dot(hi, x, "tn") + _dot(lo, x, "tn")


_pair.defvjp(_pair_fwd, _pair_bwd)


def _scan_chunk(s0, r, lw, k, v, a, b, inv=None):
    t = r[0].shape[0]
    ii = lax.broadcasted_iota(jnp.int32, (t, t), 0)
    jj = lax.broadcasted_iota(jnp.int32, (t, t), 1)
    incl = jj <= ii
    strict = jj < ii
    tri = incl.astype(f32)
    eye = jnp.where(ii == jj, 1.0, 0.0)
    cl = [_const_dot(tri, x) for x in lw]
    mid = [c[t // 2 - 1:t // 2, :] for c in cl]
    s0 = [s * jnp.exp(m) for s, m in zip(s0, mid)]
    cl = [c - m for c, m in zip(cl, mid)]
    rt = [_decayed(x, c) for x, c in zip(r, cl)]
    at = [_decayed(x, c - l) for x, c, l in zip(a, cl, lw)]
    bt = [_decayed(x, -c) for x, c in zip(b, cl)]
    kt = [_decayed(x, -c) for x, c in zip(k, cl)]
    l_ab = [jnp.where(strict, _pair(x, y), 0.0) for x, y in zip(at, bt)]
    l_ak = [jnp.where(strict, _pair(x, y), 0.0) for x, y in zip(at, kt)]
    r_b = [jnp.where(incl, _pair(x, y), 0.0) for x, y in zip(rt, bt)]
    r_k = [jnp.where(incl, _pair(x, y), 0.0) for x, y in zip(rt, kt)]
    if inv is None:
        inv = [eye + x for x in l_ab]
        pw = l_ab
        for _ in range(int(math.log2(t)) - 1):
            pw = [_dot(x, x) for x in pw]
            inv = [x + _dot(x, y) for x, y in zip(inv, pw)]
    else:
        inv = [_known_inverse(x, y) for x, y in zip(l_ab, inv)]
    rhs = [_dot(x, s, "nt") + _dot(m, y) for x, s, m, y in zip(at, s0, l_ak, v)]
    u = [_dot(x, y) for x, y in zip(inv, rhs)]
    y_s = [_dot(x, s, "nt") for x, s in zip(rt, s0)]
    y = [ys + _dot(m, uu) + _dot(n, vv) for ys, m, uu, n, vv in zip(y_s, r_b, u, r_k, v)]
    grow = [s + _dot(uu, x, "tn") + _dot(vv, z, "tn") for s, uu, x, vv, z in zip(s0, u, bt, v, kt)]
    s1 = [g * jnp.exp(c[t - 1:t, :]) for g, c in zip(grow, cl)]
    return y, s1, inv


def _head_rows(h):
    return slice(h * RWKV_HEAD, (h + 1) * RWKV_HEAD)


def _per_head(ref):
    return [ref[:, _head_rows(h)] for h in range(RWKV_HEADS)]


def _scan(r, lw, k, v, a, b, *, name):
    lp = r.shape[0]
    nc = lp // CHUNK
    row = pl.BlockSpec((CHUNK, RWKV_DIM), lambda c: (c, 0))

    def body(r_ref, lw_ref, k_ref, v_ref, a_ref, b_ref, y_ref, s_ref, inv_ref, state):
        @pl.when(pl.program_id(0) == 0)
        def _():
            state[...] = jnp.zeros_like(state)

        s_ref[...] = state[...]
        s0 = [state[_head_rows(h), :] for h in range(RWKV_HEADS)]
        y, s1, inv = _scan_chunk(s0, *[_per_head(ref) for ref in (r_ref, lw_ref, k_ref, v_ref, a_ref, b_ref)])
        for h in range(RWKV_HEADS):
            y_ref[:, _head_rows(h)] = y[h]
            state[_head_rows(h), :] = s1[h]
            inv_ref[h * CHUNK:(h + 1) * CHUNK, :] = inv[h].astype(inv_ref.dtype)

    return pl.pallas_call(
        body, name=name, grid=(nc,), in_specs=[row] * 6,
        out_specs=[row, pl.BlockSpec((RWKV_DIM, RWKV_HEAD), lambda c: (c, 0)),
                   pl.BlockSpec((RWKV_HEADS * CHUNK, CHUNK), lambda c: (c, 0))],
        out_shape=[jax.ShapeDtypeStruct((lp, RWKV_DIM), f32), jax.ShapeDtypeStruct((nc * RWKV_DIM, RWKV_HEAD), f32),
                   jax.ShapeDtypeStruct((nc * RWKV_HEADS * CHUNK, CHUNK), bf16)],
        scratch_shapes=[pltpu.VMEM((RWKV_DIM, RWKV_HEAD), f32)],
        compiler_params=_params(("arbitrary",)),
    )(r, lw, k, v, a, b)


def _scan_bwd(r, lw, k, v, a, b, states, inverses, dy, *, name):
    lp = r.shape[0]
    nc = lp // CHUNK
    back = lambda c: (nc - 1 - c, 0)
    row = pl.BlockSpec((CHUNK, RWKV_DIM), back)

    def body(r_ref, lw_ref, k_ref, v_ref, a_ref, b_ref, s_ref, inv_ref, dy_ref,
             dr_ref, dlw_ref, dk_ref, dv_ref, da_ref, db_ref, dstate):
        @pl.when(pl.program_id(0) == 0)
        def _():
            dstate[...] = jnp.zeros_like(dstate)

        outs = (dr_ref, dlw_ref, dk_ref, dv_ref, da_ref, db_ref)
        s0 = [s_ref[_head_rows(h), :] for h in range(RWKV_HEADS)]
        inv = [inv_ref[h * CHUNK:(h + 1) * CHUNK, :].astype(f32) for h in range(RWKV_HEADS)]
        _, vjp = jax.vjp(lambda *args: _scan_chunk(*args, inv=inv)[:2], s0,
                         *[_per_head(ref) for ref in (r_ref, lw_ref, k_ref, v_ref, a_ref, b_ref)])
        g = vjp((_per_head(dy_ref), [dstate[_head_rows(h), :] for h in range(RWKV_HEADS)]))
        for h in range(RWKV_HEADS):
            dstate[_head_rows(h), :] = g[0][h]
            for o_ref, gv in zip(outs, g[1:]):
                o_ref[:, _head_rows(h)] = gv[h]

    shape = jax.ShapeDtypeStruct((lp, RWKV_DIM), f32)
    return pl.pallas_call(
        body, name=name, grid=(nc,),
        in_specs=[row] * 6 + [pl.BlockSpec((RWKV_DIM, RWKV_HEAD), back),
                              pl.BlockSpec((RWKV_HEADS * CHUNK, CHUNK), back), row],
        out_specs=[row] * 6, out_shape=[shape] * 6,
        scratch_shapes=[pltpu.VMEM((RWKV_DIM, RWKV_HEAD), f32)],
        compiler_params=_params(("arbitrary",)),
    )(r, lw, k, v, a, b, states, inverses, dy)


def _loss_head(act, w_down, h1, target, g_final, *, name):
    lp = h1.shape[0]
    per_tile = 3
    tm = per_tile * BLOCK
    last_block = (lp - FRONT) // BLOCK - 1

    def body(a_ref, w_ref, h_ref, t0_ref, t1_ref, t2_ref, g_ref, loss_ref, dh_ref, dg_ref):
        i = pl.program_id(0)
        target_rows = jnp.concatenate([t0_ref[...], t1_ref[...], t2_ref[...]], axis=0)
        real = i * tm + lax.broadcasted_iota(jnp.int32, (tm, 1), 0) >= FRONT

        def tile_loss(hv, gv):
            err = _rms(hv, gv) - target_rows
            return 0.5 * jnp.sum(jnp.where(real, jnp.mean(err * err, axis=-1, keepdims=True), 0.0))

        h2 = _dot(a_ref[...], w_ref[...], "nn") + h_ref[...]
        loss, (dh, dg) = jax.value_and_grad(tile_loss, argnums=(0, 1))(h2, g_ref[...])

        @pl.when(i == 0)
        def _():
            loss_ref[...] = jnp.zeros_like(loss_ref)
            dg_ref[...] = jnp.zeros_like(dg_ref)

        loss_ref[...] += jnp.full(loss_ref.shape, loss, f32)
        dg_ref[...] += dg
        dh_ref[...] = dh

    def target_block(j):
        return pl.BlockSpec((BLOCK, D_MODEL),
                            lambda i: (jnp.clip(per_tile * i + j - FRONT // BLOCK, 0, last_block), 0))

    return pl.pallas_call(
        body, name=name, grid=(lp // tm,),
        in_specs=[pl.BlockSpec((tm, act.shape[1]), lambda i: (i, 0)), _full(w_down.shape),
                  pl.BlockSpec((tm, D_MODEL), lambda i: (i, 0)), target_block(0), target_block(1), target_block(2),
                  _full(g_final.shape)],
        out_specs=[_full((8, 128)), pl.BlockSpec((tm, D_MODEL), lambda i: (i, 0)), _full(g_final.shape)],
        out_shape=[jax.ShapeDtypeStruct((8, 128), f32), jax.ShapeDtypeStruct((lp, D_MODEL), f32),
                   jax.ShapeDtypeStruct(g_final.shape, f32)],
        compiler_params=_params(("arbitrary",)),
    )(act, w_down, h1, target, target, target, g_final)


def _embed_norm(x, meta, g, *, name):
    seq = x.shape[0]
    lp = seq + FRONT
    per_tile = 3
    tm = per_tile * BLOCK
    last_block = seq // BLOCK - 1

    def body(x0_ref, x1_ref, x2_ref, meta_ref, g_ref, h_ref, u_ref):
        front = jnp.concatenate([jnp.zeros((PAD, D_MODEL), f32), meta_ref[...]], axis=0)
        first = jnp.where(pl.program_id(0) == 0, front, x0_ref[...])
        h = jnp.concatenate([first, x1_ref[...], x2_ref[...]], axis=0)
        h_ref[...] = h
        u_ref[...] = _rms(h, g_ref[...]).astype(u_ref.dtype)

    def x_block(j):
        return pl.BlockSpec((BLOCK, D_MODEL),
                            lambda i: (jnp.clip(per_tile * i + j - FRONT // BLOCK, 0, last_block), 0))

    tile = pl.BlockSpec((tm, D_MODEL), lambda i: (i, 0))
    return pl.pallas_call(
        body, name=name, grid=(lp // tm,),
        in_specs=[x_block(0), x_block(1), x_block(2), _full(meta.shape), _full(g.shape)],
        out_specs=[tile, tile],
        out_shape=[jax.ShapeDtypeStruct((lp, D_MODEL), f32), jax.ShapeDtypeStruct((lp, D_MODEL), bf16)],
        compiler_params=_params(("parallel",)),
    )(x, x, x, meta, g)


def _input_norm_bwd(h0, g, du, dh1, *, name):
    lp = h0.shape[0]
    blocks = (lp - FRONT) // FRONT
    per_tile = max(n for n in (4, 3, 2, 1) if blocks % n == 0)
    ins = (h0, du, dh1)

    def body(*refs):
        tiles = [refs[k * per_tile:(k + 1) * per_tile] for k in range(len(ins))]
        front_refs = refs[len(ins) * per_tile:len(ins) * (per_tile + 1)]
        g_ref, dx_ref, front_ref, dg_ref = refs[len(ins) * (per_tile + 1):]

        def cotangents(h_ref, du_ref, dh1_ref):
            _, vjp = jax.vjp(lambda hv, gv: (_rms(hv, gv), hv), h_ref[...], g_ref[...])
            return vjp((du_ref[...].astype(f32), dh1_ref[...]))

        @pl.when(pl.program_id(0) == 0)
        def _():
            front_ref[...], dg_ref[...] = cotangents(*front_refs)

        dg = jnp.zeros(dg_ref.shape, f32)
        for j in range(per_tile):
            dh, dg_j = cotangents(*(t[j] for t in tiles))
            dx_ref[j * FRONT:(j + 1) * FRONT, :] = dh
            dg = dg + dg_j
        dg_ref[...] += dg

    def block(j):
        return pl.BlockSpec((FRONT, D_MODEL), lambda i: (per_tile * i + j + 1, 0))

    first = pl.BlockSpec((FRONT, D_MODEL), lambda i: (0, 0))
    return pl.pallas_call(
        body, name=name, grid=(blocks // per_tile,),
        in_specs=[block(j) for _ in ins for j in range(per_tile)] + [first] * len(ins) + [_full(g.shape)],
        out_specs=[pl.BlockSpec((per_tile * FRONT, D_MODEL), lambda i: (i, 0)), _full((FRONT, D_MODEL)),
                   _full(g.shape)],
        out_shape=[jax.ShapeDtypeStruct((lp - FRONT, D_MODEL), f32), jax.ShapeDtypeStruct((FRONT, D_MODEL), f32),
                   jax.ShapeDtypeStruct(g.shape, f32)],
        compiler_params=_params(("arbitrary",)),
    )(*(a for a in ins for _ in range(per_tile)), *ins, g)


def _local_step(x, target, meta, p, early_weights=None, late_weights=None, emit=None):
    emit = emit or (lambda group, grads: 0.0)
    seq = x.shape[0]
    lp = seq + FRONT
    cos_t, sin_t, swap = _rope_tables(lp)
    hsum = _head_sum_matrix(RWKV_DIM, RWKV_HEAD)
    hmean = hsum / RWKV_HEAD
    post_params = [p["ln_w"], p["ln_b"], p["r_k"], hmean]

    h0, u = _embed_norm(x, meta, p["norm_mix_g"], name="norm_mix")
    if early_weights is not None:
        p = {**p, **early_weights(u)}
    prep_params = [p["w0"], p["w2"], p["a0"], p["a2"], p["g2"], p["k_k"], p["k_a"], hsum]
    in_widths = [ATTN_PROJ, RKV_W, LORA_W, 2 * D_MODEL]
    q, k, v, p_rkv, p_lora, gates = _proj_in(u, p["w_in_lr"], p["b_in"], in_widths,
                                             (cos_t, sin_t, swap), name="proj_in", zero_rows_below=PAD)
    y_attn = _attention(q, k, v, p["sinks"], name="attention")

    mix_rkv, mix_lora = p["mix"][:, :RKV_W], p["mix"][:, RKV_W:]
    r_, lw_, k_, v_, a_, b_, g_ = _mixer_inputs([p_rkv, p_lora], [mix_rkv, mix_lora], prep_params,
                                                name="mixer_inputs")
    y_scan, states, inverses = _scan(r_, lw_, k_, v_, a_, b_, name="wkv_scan")
    (y_rwkv,) = _rowwise(_rwkv_post, [y_scan, r_, k_, v_, g_], post_params, [(RWKV_DIM, bf16)], name="rwkv_post")

    if late_weights is not None:
        p = {**p, **late_weights(y_rwkv)}
    br_a, br_r, merged = _branch_merge(y_attn, y_rwkv, p["w_br_attn_t"], p["w_br_rwkv_t"], gates, name="branch_merge")
    h1, f = _residual_norm(merged, p["w_o"], h0, p["norm_ffn_g"], name="out_proj")
    gate, up, act = _ffn_in(f, p["w_gate_t"], p["w_up_t"], name="ffn_in")

    loss8, dh2, d_final_g = _loss_head(act, p["w_down"], h1, target, p["norm_final_g"], name="loss_head")
    dgate, dup = _ffn_in_bwd(dh2, p["w_down"], gate, up, name="ffn_in_bwd")
    d_w_down = _mm_tn(act, dh2, name="dw_down")
    d_w_gate_t = _mm_tn(dgate, f, name="dw_gate")
    d_w_up_t = _mm_tn(dup, f, name="dw_up")
    zero = emit("ffn", dict(w_down=d_w_down, w_gate_t=d_w_gate_t, w_up_t=d_w_up_t))
    dh1, d_ffn_g = _residual_norm_bwd([dgate, dup], [p["w_gate_t"], p["w_up_t"]], h1, p["norm_ffn_g"] + zero, dh2,
                                      name="norm_ffn_bwd")
    dgates, dbr_a, dbr_r, dy_attn, dy_rwkv = _branch_merge_bwd(
        dh1, p["w_o"], gates, br_a, br_r, p["w_br_attn_t"], p["w_br_rwkv_t"], name="branch_merge_bwd")
    d_w_o = _mm_tn(merged, dh1, name="dw_o")
    d_w_br_attn_t = _mm_tn(dbr_a, y_attn, name="dw_br_attn")
    d_w_br_rwkv_t = _mm_tn(dbr_r, y_rwkv, name="dw_br_rwkv")
    zero = emit("branch", dict(w_o=d_w_o, w_br_attn_t=d_w_br_attn_t, w_br_rwkv_t=d_w_br_rwkv_t))

    post_params = [p["ln_w"] + zero, p["ln_b"], p["r_k"], hmean]
    res = _rowwise_bwd(_rwkv_post, [y_scan, r_, k_, v_, g_], post_params, [[dy_rwkv]], name="rwkv_post_bwd",
                       diff_rows=[True] * 5, diff_params=[True, True, True, False])
    dy_scan, dr_p, dk_p, dv_p, dg_p, d_ln_w, d_ln_b, d_r_k = res
    dr_s, dlw_s, dk_s, dv_s, da_s, db_s = _scan_bwd(r_, lw_, k_, v_, a_, b_, states, inverses, dy_scan,
                                                    name="wkv_scan_bwd")
    res = _mixer_inputs_bwd([p_rkv, p_lora], [mix_rkv, mix_lora], prep_params,
                            [[dr_s, dr_p], [dlw_s], [dk_s, dk_p], [dv_s, dv_p], [da_s], [db_s], [dg_p]],
                            name="mixer_inputs_bwd")
    dp_rkv, dp_lora, d_mix_rkv, d_mix_lora, d_w0, d_w2, d_a0, d_a2, d_g2, d_k_k, d_k_a = res

    dq, dk, dv, dkm, dvm, d_sinks = _attention_bwd(q, k, v, p["sinks"], y_attn, dy_attn, name="attention_bwd")
    (dqkv,) = _rowwise(_attn_prep_transposed, [dq, dk, dv, cos_t, sin_t], [swap, dkm, dvm], [(ATTN_PROJ, bf16)],
                       name="attn_prep_bwd")

    in_rows, (at_qkv, at_rkv, at_lora, at_gates) = p["w_in_lr"].shape[1], [off for off, _ in _pieces(in_widths)]
    d_w_in_t, db_gates = _mm_tn(dgates, u, name="dw_gates", colsum=True, into=(in_rows, at_gates, None))
    d_w_in_t, db_rkv = _mm_tn(dp_rkv, u, name="dw_rkv", colsum=True, into=(in_rows, at_rkv, d_w_in_t))
    d_w_in_t, db_lora = _mm_tn(dp_lora, u, name="dw_lora", colsum=True, into=(in_rows, at_lora, d_w_in_t))
    d_w_in_t, db_qkv = _mm_tn(dqkv, u, name="dw_qkv", colsum=True, into=(in_rows, at_qkv, d_w_in_t))
    zero = emit("input", dict(w_in_t=d_w_in_t, g2=d_g2, w2=d_w2, a2=d_a2))
    du = _proj_in_bwd([dqkv, dp_rkv, dp_lora, dgates], p["w_in_lr"], name="d_u")
    dx, d_front, d_mix_g = _input_norm_bwd(h0, p["norm_mix_g"] + zero, du, dh1, name="norm_mix_bwd")

    grads = dict(
        w_in_t=d_w_in_t,
        b_in=jnp.concatenate([db_qkv, db_rkv, db_lora, db_gates], axis=1),
        mix=jnp.concatenate([d_mix_rkv, d_mix_lora], axis=1),
        norm_mix_g=d_mix_g, sinks=d_sinks, w0=d_w0, w2=d_w2, a0=d_a0, a2=d_a2, g2=d_g2, k_k=d_k_k, k_a=d_k_a,
        r_k=d_r_k, ln_w=d_ln_w, ln_b=d_ln_b, w_br_attn_t=d_w_br_attn_t, w_br_rwkv_t=d_w_br_rwkv_t, w_o=d_w_o,
        norm_ffn_g=d_ffn_g, w_gate_t=d_w_gate_t, w_up_t=d_w_up_t, w_down=d_w_down, norm_final_g=d_final_g,
        meta=d_front[PAD:],
    )
    return loss8[0, 0], dx, grads


def _position():
    return lax.axis_index("x"), lax.axis_index("y"), lax.axis_index("c")


def _other_chips(x, y):
    return [(1 - x, y), (x, 1 - y), (1 - x, 1 - y)]


_HBM = pl.BlockSpec(memory_space=pltpu.HBM)
_SEM = pl.BlockSpec(memory_space=pltpu.SEMAPHORE)
_EFFECT = pltpu.SideEffectType.DATAFLOW_SIDE_EFFECTING


def _landing_zone(src, kind):
    shape = {"whole": (N_CHIPS,) + src.shape, "half": (2, N_CHIPS, src.shape[0], src.shape[1] // 2),
             "slab": (3,) + src.shape[1:], "sibling": src.shape}[kind]
    return lax.empty(shape, src.dtype)


def _copies_per_source(kind):
    return 1 if kind == "sibling" else 3


def _chip_copies(src_refs, land_refs, send_sems, recv_sems, kind):
    x, y, c = _position()
    if kind == "sibling":
        return [pltpu.make_async_remote_copy(
            src_ref=src, dst_ref=land, send_sem=send_sems.at[a], recv_sem=recv_sems.at[a],
            device_id=(x, y, 1 - c), device_id_type=MESH) for a, (src, land) in enumerate(zip(src_refs, land_refs))]
    copies = []
    for a, (src, land) in enumerate(zip(src_refs, land_refs)):
        for j, (px, py) in enumerate(_other_chips(x, y)):
            if kind == "whole":
                src_ref, dst_ref = src, land.at[2 * x + y]
            elif kind == "half":
                half = src.shape[1] // 2
                src_ref, dst_ref = src.at[:, pl.ds(pl.multiple_of(c * half, half), half)], land.at[c, 2 * x + y]
            else:
                src_ref, dst_ref = src.at[2 * px + py], land.at[j]
            copies.append(pltpu.make_async_remote_copy(
                src_ref=src_ref, dst_ref=dst_ref, send_sem=send_sems.at[3 * a + j], recv_sem=recv_sems.at[3 * a + j],
                device_id=(px, py, c), device_id_type=MESH))
    return copies


def _exchange_start(srcs, *, kind, name):
    n = len(srcs)
    lands = [_landing_zone(s, kind) for s in srcs]

    def body(*refs):
        for cp in _chip_copies(refs[:n], refs[n:2 * n], refs[2 * n], refs[2 * n + 1], kind):
            cp.start()
        refs[-1][...] = jnp.zeros_like(refs[-1])

    res = pl.pallas_call(
        body, name=name,
        out_shape=(pltpu.SemaphoreType.DMA((_copies_per_source(kind) * n,)),
                   pltpu.SemaphoreType.DMA((_copies_per_source(kind) * n,)),
                   *[pltpu.HBM(a.shape, a.dtype) for a in srcs + lands], jax.ShapeDtypeStruct((8, 128), f32)),
        in_specs=[_HBM] * (2 * n),
        out_specs=(_SEM, _SEM, *[_HBM] * (2 * n), pl.BlockSpec(memory_space=pltpu.VMEM)),
        input_output_aliases={i: 2 + i for i in range(2 * n)},
        compiler_params=pltpu.CompilerParams(has_side_effects=_EFFECT),
    )(*[pltpu.with_memory_space_constraint(a, pltpu.HBM) for a in srcs + lands])
    return res[0], res[1], list(res[2:2 + n]), list(res[2 + n:2 + 2 * n]), res[-1]


def _exchange_wait(handle, after, *, kind, name):
    send_sems, recv_sems, srcs, lands, _ = handle
    n = len(srcs)

    def body(*refs):
        for cp in _chip_copies(refs[:n], refs[n:2 * n], refs[2 * n], refs[2 * n + 1], kind):
            cp.wait_send()
            cp.wait_recv()

    res = pl.pallas_call(
        body, name=name,
        out_shape=tuple(pltpu.HBM(a.shape, a.dtype) for a in srcs + lands),
        in_specs=[_HBM] * (2 * n) + [_SEM, _SEM, pl.BlockSpec(memory_space=pl.ANY)],
        out_specs=tuple([_HBM] * (2 * n)),
        input_output_aliases={i: i for i in range(2 * n)},
        compiler_params=pltpu.CompilerParams(has_side_effects=_EFFECT),
    )(*srcs, *lands, send_sems, recv_sems, after)
    return list(res[:n]), list(res[n:])


def _sum_own_and_received(g, recv, *, name):
    _, r, w = g.shape
    tm = _tile(r)
    if g.dtype == bf16 and tm % 16:
        tm = r
    x, y, _ = _position()
    me = jnp.reshape(2 * x + y, (1,)).astype(jnp.int32)

    def body(me_ref, g_ref, r_ref, o_ref):
        o_ref[...] = (g_ref[0].astype(f32) + r_ref[0].astype(f32)) + (r_ref[1].astype(f32) + r_ref[2].astype(f32))

    return pl.pallas_call(
        body, name=name,
        grid_spec=pltpu.PrefetchScalarGridSpec(
            num_scalar_prefetch=1, grid=(r // tm,),
            in_specs=[pl.BlockSpec((1, tm, w), lambda i, me_ref: (me_ref[0], i, 0)),
                      pl.BlockSpec((3, tm, w), lambda i, me_ref: (0, i, 0))],
            out_specs=pl.BlockSpec((tm, w), lambda i, me_ref: (i, 0))),
        out_shape=jax.ShapeDtypeStruct((r, w), f32),
        compiler_params=_params(("parallel",)),
    )(me, g, recv)


def _swap_cores(arrs, *, name):
    n = len(arrs)

    def body(*refs):
        x, y, c = _position()
        copies = [pltpu.make_async_remote_copy(
            src_ref=refs[i], dst_ref=refs[n + i], send_sem=refs[2 * n].at[i], recv_sem=refs[2 * n + 1].at[i],
            device_id=(x, y, 1 - c), device_id_type=MESH) for i in range(n)]
        for cp in copies:
            cp.start()
        for cp in copies:
            cp.wait_recv()
        for cp in copies:
            cp.wait_send()

    return pl.pallas_call(
        body, name=name,
        in_specs=[pl.BlockSpec(memory_space=pl.ANY)] * n,
        out_specs=[pl.BlockSpec(memory_space=pl.ANY)] * n,
        out_shape=[jax.ShapeDtypeStruct(a.shape, a.dtype) for a in arrs],
        scratch_shapes=[pltpu.SemaphoreType.DMA((n,)), pltpu.SemaphoreType.DMA((n,))],
    )(*arrs)


def _swap_halves(zone, *, name):
    def body(z_ref, o_ref, send_sems, recv_sems):
        x, y, c = _position()
        mine = [pltpu.make_async_remote_copy(
            src_ref=o_ref.at[c, 2 * px + py], dst_ref=o_ref.at[c, 2 * px + py], send_sem=send_sems.at[j],
            recv_sem=recv_sems.at[j], device_id=(x, y, 1 - c), device_id_type=MESH)
            for j, (px, py) in enumerate(_other_chips(x, y))]
        for cp in mine:
            cp.start()
        for j, (px, py) in enumerate(_other_chips(x, y)):
            pltpu.make_async_remote_copy(
                src_ref=o_ref.at[c, 2 * px + py], dst_ref=o_ref.at[1 - c, 2 * px + py], send_sem=send_sems.at[j],
                recv_sem=recv_sems.at[j], device_id=(x, y, 1 - c), device_id_type=MESH).wait_recv()
        for cp in mine:
            cp.wait_send()

    return pl.pallas_call(
        body, name=name,
        in_specs=[pl.BlockSpec(memory_space=pl.ANY)], out_specs=pl.BlockSpec(memory_space=pl.ANY),
        out_shape=jax.ShapeDtypeStruct(zone.shape, zone.dtype), input_output_aliases={0: 0},
        scratch_shapes=[pltpu.SemaphoreType.DMA((3,)), pltpu.SemaphoreType.DMA((3,))],
    )(zone)


def _all_reduce_small(a, after, *, name):
    rows, w = a.shape

    def body(a_ref, after_ref, o_ref, buf, send_sems, recv_sems):
        x, y, c = _position()
        me = 4 * x + 2 * y + c
        buf[0] = a_ref[...]
        sends = []
        for rel in range(1, N_DEV):
            peer = ((1 - x) if rel & 4 else x, (1 - y) if rel & 2 else y, (1 - c) if rel & 1 else c)
            cp = pltpu.make_async_remote_copy(
                src_ref=a_ref, dst_ref=buf.at[rel], send_sem=send_sems.at[rel - 1], recv_sem=recv_sems.at[rel - 1],
                device_id=peer, device_id_type=MESH)
            cp.start()
            sends.append(cp)
        for cp in sends:
            cp.wait_recv()
        for cp in sends:
            cp.wait_send()
        acc = buf[jnp.bitwise_xor(me, 0)]
        for d in range(1, N_DEV):
            acc = acc + buf[jnp.bitwise_xor(me, d)]
        o_ref[...] = acc

    return pl.pallas_call(
        body, name=name,
        in_specs=[pl.BlockSpec(memory_space=pltpu.VMEM), pl.BlockSpec(memory_space=pl.ANY)],
        out_specs=pl.BlockSpec(memory_space=pltpu.VMEM),
        out_shape=jax.ShapeDtypeStruct((rows, w), f32),
        scratch_shapes=[pltpu.VMEM((N_DEV, rows, w), f32), pltpu.SemaphoreType.DMA((N_DEV - 1,)),
                        pltpu.SemaphoreType.DMA((N_DEV - 1,))],
    )(a, after)


def _adam_math(w, g, m, v):
    nm = ADAM_B1 * m + (1.0 - ADAM_B1) * g
    nv = ADAM_B2 * v + (1.0 - ADAM_B2) * (g * g)
    m_hat = nm / (1.0 - ADAM_B1 ** ADAM_STEP)
    v_hat = nv / (1.0 - ADAM_B2 ** ADAM_STEP)
    return -ADAM_LR * (m_hat / (jnp.sqrt(v_hat) + ADAM_EPS) + ADAM_WD * w), nm, nv


def _adamw(w, g_parts, m, v, *, name, transposed=False):
    rows, cols = w.shape
    if transposed:
        tm = 256 if rows % 256 == 0 else rows
        g_spec = pl.BlockSpec((cols, tm), lambda i: (0, i))
    else:
        tm = _tile(rows, 256)
        g_spec = pl.BlockSpec((tm, cols), lambda i: (i, 0))
    n = len(g_parts)

    def body(*refs):
        w_ref, m_ref, v_ref = refs[0], refs[1 + n], refs[2 + n]
        g_ref, d_ref, nm_ref, nv_ref = refs[3 + n:]
        gv = refs[1][...]
        for part in refs[2:1 + n]:
            gv = gv + part[...]
        if transposed:
            gv = gv.T
        g_ref[...] = gv
        d_ref[...], nm_ref[...], nv_ref[...] = _adam_math(w_ref[...], gv, m_ref[...], v_ref[...])

    spec = pl.BlockSpec((tm, cols), lambda i: (i, 0))
    shape = jax.ShapeDtypeStruct((rows, cols), f32)
    return pl.pallas_call(
        body, name=name, grid=(rows // tm,), in_specs=[spec] + [g_spec] * n + [spec] * 2,
        out_specs=[spec] * 4, out_shape=[shape] * 4,
        compiler_params=_params(("parallel",)),
    )(w, *g_parts, m, v)


def _pad_rows(a, rows):
    return jnp.concatenate([a, jnp.zeros((rows - a.shape[0], a.shape[1]), a.dtype)], axis=0) if rows > a.shape[0] else a


_SMALL = (("norm_mix_g", D_MODEL), ("b_in", D_IN), ("sinks", Q_HEADS), ("mix", RWKV_PROJ), ("w0", RWKV_DIM),
          ("a0", RWKV_DIM), ("k_k", RWKV_DIM), ("k_a", RWKV_DIM), ("r_k", RWKV_DIM), ("ln_w", RWKV_DIM),
          ("ln_b", RWKV_DIM), ("norm_ffn_g", D_MODEL), ("norm_final_g", D_MODEL))


LANES = 128


def _small_layout():
    out, off = {}, 0
    for n, size in _SMALL + (("loss", 1),):
        pieces, col = [], 0
        while col < size:
            row, lane = divmod(off + col, PACK_W)
            width = min(size - col, PACK_W - lane)
            pieces.append((row, lane, width, col))
            col += width
        out[n] = pieces
        off += -(-size // LANES) * LANES
    return out, -(-off // PACK_W)


def _pack_small(d, loss):
    layout, rows = _small_layout()
    parts, used = [], 0
    for n, size in _SMALL + (("loss", 1),):
        item = loss if n == "loss" else d[n]
        fill = -size % LANES
        parts += [item.reshape(-1).astype(f32), jnp.zeros((fill,), f32)]
        used += size + fill
    parts.append(jnp.zeros((rows * PACK_W - used,), f32))
    return jnp.concatenate(parts).reshape(rows, PACK_W)


def _adamw_small(packed, first_row, ws, ms, vs, meta, *, name):
    layout, _ = _small_layout()
    names = [n for n, _ in _SMALL]
    k = len(names)

    def body(*refs):
        packed_ref = refs[0]
        w_refs, m_refs, v_refs = refs[1:1 + k], refs[1 + k:1 + 2 * k], refs[1 + 2 * k:1 + 3 * k]
        meta_refs = refs[1 + 3 * k:5 + 3 * k]
        outs = refs[5 + 3 * k:]
        for idx, n in enumerate(names):
            for row, lane, width, col in layout[n]:
                gv = packed_ref[first_row + row:first_row + row + 1, lane:lane + width]
                at = (slice(None), slice(col, col + width))
                new = _adam_math(w_refs[idx][at], gv, m_refs[idx][at], v_refs[idx][at])
                for o_ref, val in zip(outs[4 * idx:4 * idx + 4], (gv,) + new):
                    o_ref[at] = val
        for o_ref, val in zip(outs[4 * k:], _adam_math(*(r[...] for r in meta_refs))):
            o_ref[...] = val

    ins = [packed] + [d[n] for d in (ws, ms, vs) for n in names] + list(meta)
    shapes = [jax.ShapeDtypeStruct(ws[n].shape, f32) for n in names for _ in range(4)]
    shapes += [jax.ShapeDtypeStruct(meta[0].shape, f32)] * 3
    res = pl.pallas_call(
        body, name=name, grid=(1,), in_specs=[_full(a.shape) for a in ins],
        out_specs=[_full(s.shape) for s in shapes], out_shape=shapes,
        compiler_params=_params(("arbitrary",)),
    )(*ins)
    return {n: res[4 * i:4 * i + 4] for i, n in enumerate(names)}, res[4 * k:]


def kernel(x, meta_tokens, norm_mix_g, w_in, b_in, attn_sinks, rwkv_mix, rwkv_w0, rwkv_w2, rwkv_a0, rwkv_a2, rwkv_g2, rwkv_k_k, rwkv_k_a, rwkv_r_k, rwkv_ln_w, rwkv_ln_b, w_br_attn, w_br_rwkv, w_o, norm_ffn_g, w_ffn_gate, w_ffn_up, w_ffn_down, norm_final_g, loss_target, m_meta_tokens, m_norm_mix_g, m_w_in, m_b_in, m_attn_sinks, m_rwkv_mix, m_rwkv_w0, m_rwkv_w2, m_rwkv_a0, m_rwkv_a2, m_rwkv_g2, m_rwkv_k_k, m_rwkv_k_a, m_rwkv_r_k, m_rwkv_ln_w, m_rwkv_ln_b, m_w_br_attn, m_w_br_rwkv, m_w_o, m_norm_ffn_g, m_w_ffn_gate, m_w_ffn_up, m_w_ffn_down, m_norm_final_g, v_meta_tokens, v_norm_mix_g, v_w_in, v_b_in, v_attn_sinks, v_rwkv_mix, v_rwkv_w0, v_rwkv_w2, v_rwkv_a0, v_rwkv_a2, v_rwkv_g2, v_rwkv_k_k, v_rwkv_k_a, v_rwkv_r_k, v_rwkv_ln_w, v_rwkv_ln_b, v_w_br_attn, v_w_br_rwkv, v_w_o, v_norm_ffn_g, v_w_ffn_gate, v_w_ffn_up, v_w_ffn_down, v_norm_final_g):
    names = ("meta_tokens", "norm_mix_g", "w_in", "b_in", "attn_sinks", "rwkv_mix", "rwkv_w0", "rwkv_w2", "rwkv_a0",
             "rwkv_a2", "rwkv_g2", "rwkv_k_k", "rwkv_k_a", "rwkv_r_k", "rwkv_ln_w", "rwkv_ln_b", "w_br_attn",
             "w_br_rwkv", "w_o", "norm_ffn_g", "w_ffn_gate", "w_ffn_up", "w_ffn_down", "norm_final_g")
    w_all = dict(zip(names, (meta_tokens, norm_mix_g, w_in, b_in, attn_sinks, rwkv_mix, rwkv_w0, rwkv_w2, rwkv_a0,
                             rwkv_a2, rwkv_g2, rwkv_k_k, rwkv_k_a, rwkv_r_k, rwkv_ln_w, rwkv_ln_b, w_br_attn,
                             w_br_rwkv, w_o, norm_ffn_g, w_ffn_gate, w_ffn_up, w_ffn_down, norm_final_g)))
    m_all = dict(zip(names, (m_meta_tokens, m_norm_mix_g, m_w_in, m_b_in, m_attn_sinks, m_rwkv_mix, m_rwkv_w0,
                             m_rwkv_w2, m_rwkv_a0, m_rwkv_a2, m_rwkv_g2, m_rwkv_k_k, m_rwkv_k_a, m_rwkv_r_k,
                             m_rwkv_ln_w, m_rwkv_ln_b, m_w_br_attn, m_w_br_rwkv, m_w_o, m_norm_ffn_g, m_w_ffn_gate,
                             m_w_ffn_up, m_w_ffn_down, m_norm_final_g)))
    v_all = dict(zip(names, (v_meta_tokens, v_norm_mix_g, v_w_in, v_b_in, v_attn_sinks, v_rwkv_mix, v_rwkv_w0,
                             v_rwkv_w2, v_rwkv_a0, v_rwkv_a2, v_rwkv_g2, v_rwkv_k_k, v_rwkv_k_a, v_rwkv_r_k,
                             v_rwkv_ln_w, v_rwkv_ln_b, v_w_br_attn, v_w_br_rwkv, v_w_o, v_norm_ffn_g, v_w_ffn_gate,
                             v_w_ffn_up, v_w_ffn_down, v_norm_final_g)))
    cx, cy, _ = _position()
    chip = 2 * cx + cy

    t_of = dict(w_in_t="w_in", w_gate_t="w_ffn_gate", w_up_t="w_ffn_up", w_br_attn_t="w_br_attn",
                w_br_rwkv_t="w_br_rwkv", g2_t="rwkv_g2", w2_t="rwkv_w2", a2_t="rwkv_a2")
    plain_of = dict(w_down="w_ffn_down", w_o="w_o")
    meta_cols = meta_tokens.shape[1]

    def shard(k):
        return (w_all[t_of[k]][0].T if k in t_of else w_all[plain_of[k]][0]).astype(bf16)

    def whole(zone, own):
        return lax.dynamic_update_slice_in_dim(zone, own[None], chip, axis=0).reshape(-1, own.shape[-1])

    tiny = ("g2_t", "w2_t", "a2_t")
    late = ("w_gate_t", "w_up_t", "w_down", "w_o", "w_br_attn_t", "w_br_rwkv_t")
    w_in_own = shard("w_in_t")
    w_in_rows, w_in_cols = w_in_own.shape
    tiny_h = _exchange_start([shard(k) for k in tiny] + [meta_tokens], kind="whole", name="gather_tiny_start")
    w_in_h = _exchange_start([w_in_own + tiny_h[4][0, 0].astype(bf16)], kind="half", name="gather_w_in_start")
    behind = w_in_h[4][0, 0].astype(bf16)
    late_h = _exchange_start([shard(k) + behind for k in late], kind="whole", name="gather_late_start")
    own, zones = _exchange_wait(tiny_h, late_h[4], kind="whole", name="gather_tiny_wait")
    got = {k: whole(z, o) for k, z, o in zip(tiny, zones, own)}
    meta_full = whole(zones[-1], own[-1]).reshape(N_CHIPS, N_META, meta_cols).transpose(1, 0, 2).reshape(N_META, -1)
    p = dict(
        g2=got["g2_t"].T.astype(f32), w2=got["w2_t"].T.astype(f32), a2=got["a2_t"].T.astype(f32),
        b_in=b_in, sinks=attn_sinks, mix=rwkv_mix, w0=rwkv_w0, a0=rwkv_a0, k_k=rwkv_k_k, k_a=rwkv_k_a,
        r_k=rwkv_r_k.reshape(1, RWKV_DIM), ln_w=rwkv_ln_w, ln_b=rwkv_ln_b, norm_mix_g=norm_mix_g,
        norm_ffn_g=norm_ffn_g, norm_final_g=norm_final_g.reshape(1, D_MODEL),
    )

    def early_weights(after):
        own_h, zones_h = _exchange_wait(w_in_h, after, kind="half", name="gather_w_in_wait")
        zone = _swap_halves(zones_h[0], name="swap_w_in_halves")
        own_halves = own_h[0].reshape(w_in_rows, 2, w_in_cols // 2).transpose(1, 0, 2)[:, None]
        zone = lax.dynamic_update_slice(zone, own_halves, (0, chip, 0, 0))
        return dict(w_in_lr=zone.reshape(2, N_CHIPS * w_in_rows, w_in_cols // 2))

    def late_weights(after):
        own_l, zones_l = _exchange_wait(late_h, after, kind="whole", name="gather_late_wait")
        return {k: whole(z, o) for k, z, o in zip(late, zones_l, own_l)}

    started = {}

    def partial_sums(groups, after):
        parts = {}
        for group in groups:
            keys, handle = started[group]
            slabs, lands = _exchange_wait(handle, after, kind="slab", name="scatter_" + group + "_wait")
            parts.update({k: _sum_own_and_received(s, l, name="sum_chips_" + k) for k, s, l in zip(keys, slabs, lands)})
        return parts

    def emit(group, grads_):
        keys = list(grads_)
        slabs = []
        for k in keys:
            a = grads_[k].T if k in ("g2", "w2", "a2") else grads_[k]
            slabs.append(a.reshape(N_CHIPS, a.shape[0] // N_CHIPS, a.shape[1]))
        started[group] = (keys, _exchange_start(slabs, kind="slab", name="scatter_" + group + "_start"))
        zero = started[group][1][4]
        if group == "input":
            started["parts_a"] = partial_sums(("ffn", "branch"), zero)
            started["swap_a"] = _exchange_start(list(started["parts_a"].values()), kind="sibling",
                                                name="swap_cores_a_start")
            zero = started["swap_a"][4]
        return zero[0, 0]

    loss, dx, g = _local_step(x[0], loss_target[0], meta_full, p, early_weights, late_weights, emit)

    grads, delta, new_m, new_v = {}, {}, {}, {}
    in_grad_layout = ("w_in_t", "w_gate_t", "w_up_t")
    weight_of = {**t_of, **plain_of}

    def update(keys, mine, theirs):
        for k, part, other in zip(keys, mine, theirs):
            both = [part, other]
            k = k + "_t" if k in ("g2", "w2", "a2") else k
            n = weight_of[k]
            shape2 = w_all[n].shape[1:]
            w_, m_, v_ = (a.reshape(shape2) for a in (w_all[n], m_all[n], v_all[n]))
            if k in in_grad_layout:
                res = [t.T for t in _adamw(w_.T, both, m_.T, v_.T, name="adamw_" + n)]
            else:
                res = _adamw(w_, both, m_, v_, name="adamw_" + n, transposed=k in t_of)
            grads[n], delta[n], new_m[n], new_v[n] = (t.reshape(w_all[n].shape) for t in res)
        return delta[n]

    done = update(list(started["parts_a"]),
                  *_exchange_wait(started["swap_a"], dx, kind="sibling", name="swap_cores_a_wait"))
    small = _pack_small(g, loss)
    small_rows8 = -(-(small.shape[0] + N_META) // 8) * 8
    reduced = _all_reduce_small(_pad_rows(jnp.concatenate([g["meta"], small], axis=0), small_rows8), done,
                                name="reduce_small")
    parts_b = partial_sums(("input",), reduced)
    update(list(parts_b), list(parts_b.values()), _swap_cores(list(parts_b.values()), name="swap_cores_b"))
    g_meta = lax.dynamic_slice_in_dim(reduced[:N_META], chip * meta_cols, meta_cols, axis=1)
    (loss_row, loss_lane, _, _), = _small_layout()[0]["loss"]
    loss_total = reduced[N_META + loss_row, loss_lane]

    small_of = dict(norm_mix_g="norm_mix_g", b_in="b_in", attn_sinks="sinks", rwkv_mix="mix", rwkv_w0="w0",
                    rwkv_a0="a0", rwkv_k_k="k_k", rwkv_k_a="k_a", rwkv_r_k="r_k", rwkv_ln_w="ln_w",
                    rwkv_ln_b="ln_b", norm_ffn_g="norm_ffn_g", norm_final_g="norm_final_g")
    as_rows = [{k: src[n].reshape(1, -1) for n, k in small_of.items()} for src in (w_all, m_all, v_all)]
    meta_in = (meta_tokens, g_meta, m_meta_tokens, v_meta_tokens)
    small_out, meta_out = _adamw_small(reduced, N_META, *as_rows, meta_in, name="adamw_small")
    grads["meta_tokens"] = g_meta
    delta["meta_tokens"], new_m["meta_tokens"], new_v["meta_tokens"] = meta_out
    for n, k in small_of.items():
        grads[n], delta[n], new_m[n], new_v[n] = (t.reshape(w_all[n].shape) for t in small_out[k])

    return (loss_total, dx.reshape(x.shape), *[grads[n] for n in names], *[delta[n] for n in names],
            *[new_m[n] for n in names], *[new_v[n] for n in names])
```

```python
import math

import jax
import jax.numpy as jnp
import numpy as np
from jax import lax
from jax.experimental import pallas as pl
from jax.experimental.pallas import tpu as pltpu

f32 = jnp.float32
bf16 = jnp.bfloat16

D_MODEL = 1024
N_META = 16
HEAD_DIM = 64
Q_HEADS = 8
KV_HEADS = 2
GROUP = Q_HEADS // KV_HEADS
WINDOW = 128
BLOCK = 128
ROPE_THETA = 500000.0
ROPE_DIM = HEAD_DIM // 4
RWKV_HEADS = 8
RWKV_HEAD = 64
RWKV_DIM = RWKV_HEADS * RWKV_HEAD
DECAY_LORA = 64
AAA_LORA = 64
GATE_LORA = 160
LORA_W = DECAY_LORA + AAA_LORA + GATE_LORA
RWKV_LN_EPS = 64e-5
D_FF = 2816
Q_W = Q_HEADS * HEAD_DIM
KV_W = KV_HEADS * HEAD_DIM
ATTN_PROJ = Q_W + 2 * KV_W
RKV_W = 3 * RWKV_DIM
RWKV_PROJ = RKV_W + LORA_W
D_IN = ATTN_PROJ + RWKV_PROJ + 2 * D_MODEL
RMS_EPS = 1e-6
NEG_INF = -1e30
PAD = BLOCK - N_META
FRONT = PAD + N_META

ADAM_LR = 0.001
ADAM_B1 = 0.9
ADAM_B2 = 0.999
ADAM_EPS = 1e-08
ADAM_WD = 0.01
ADAM_STEP = 10

N_CHIPS = 4
N_DEV = 8
CHUNK = 128
VMEM_LIMIT = 56 * 1024 * 1024
MM_ROWS = 704
PACK_W = 1024
MESH = pl.DeviceIdType.MESH


def _tile(m, pref=384):
    for step in (16, 8):
        for t in range(min(m, pref) // step * step, 0, -step):
            if m % t == 0:
                return t
    return m


def _params(sem=None):
    return pltpu.CompilerParams(dimension_semantics=sem, vmem_limit_bytes=VMEM_LIMIT)


def _full(shape):
    nd = len(shape)
    return pl.BlockSpec(shape, lambda *_: (0,) * nd)


def _dot(a, b, dims="nn"):
    dn = {"nn": (((1,), (0,)), ((), ())), "nt": (((1,), (1,)), ((), ())), "tn": (((0,), (0,)), ((), ()))}[dims]
    return lax.dot_general(a.astype(bf16), b.astype(bf16), dn, preferred_element_type=f32)


def _two_pass(x, m, dims="nn"):
    x_hi = x.astype(bf16)
    x_lo = (x - x_hi.astype(f32)).astype(bf16)
    return _dot(x_hi, m, dims) + _dot(x_lo, m, dims)


@jax.custom_vjp
def _dot_const(x, m):
    return _two_pass(x, m)


def _dot_const_fwd(x, m):
    return _two_pass(x, m), m


def _dot_const_bwd(m, ct):
    return _two_pass(ct, m, "nt"), jnp.zeros_like(m)


_dot_const.defvjp(_dot_const_fwd, _dot_const_bwd)


def _two_pass_left(m, x, dims):
    x_hi = x.astype(bf16)
    x_lo = (x - x_hi.astype(f32)).astype(bf16)
    return _dot(m, x_hi, dims) + _dot(m, x_lo, dims)


@jax.custom_vjp
def _const_dot(m, x):
    return _two_pass_left(m, x, "nn")


def _const_dot_fwd(m, x):
    return _two_pass_left(m, x, "nn"), m


def _const_dot_bwd(m, ct):
    return jnp.zeros_like(m), _two_pass_left(m, ct, "tn")


_const_dot.defvjp(_const_dot_fwd, _const_dot_bwd)


def _mm(a, b, mode, *, name, out_dtype=f32, bias=None, add=None, zero_rows_below=0):
    m, _ = a.shape
    n = b.shape[1] if mode == "nn" else b.shape[0]
    tm = _tile(m, MM_ROWS)
    has_bias, has_add = bias is not None, add is not None

    def body(*refs):
        a_ref, b_ref = refs[0], refs[1]
        o_ref = refs[-1]
        acc = _dot(a_ref[...], b_ref[...], mode)
        k = 2
        if has_bias:
            acc = acc + refs[k][...]
            k += 1
        if zero_rows_below:
            rows = pl.program_id(0) * tm + lax.broadcasted_iota(jnp.int32, acc.shape, 0)
            acc = jnp.where(rows >= zero_rows_below, acc, 0.0)
        if has_add:
            acc = acc + refs[k][...].astype(f32)
        o_ref[...] = acc.astype(out_dtype)

    ins = [a, b]
    in_specs = [pl.BlockSpec((tm, a.shape[1]), lambda i: (i, 0)), _full(b.shape)]
    if has_bias:
        ins.append(bias)
        in_specs.append(_full(bias.shape))
    if has_add:
        ins.append(add)
        in_specs.append(pl.BlockSpec((tm, n), lambda i: (i, 0)))
    return pl.pallas_call(
        body, name=name, grid=(m // tm,), in_specs=in_specs,
        out_specs=pl.BlockSpec((tm, n), lambda i: (i, 0)),
        out_shape=jax.ShapeDtypeStruct((m, n), out_dtype),
        compiler_params=_params(("parallel",)),
    )(*ins)


def _pieces(widths):
    out, off = [], 0
    for w in widths:
        out.append((off, w))
        off += w
    return out


def _proj_in(a, w_lr, bias, widths, rope, *, name, zero_rows_below=0):
    m, kdim = a.shape
    half = kdim // 2
    tm = _tile(m, MM_ROWS)
    cos_t, sin_t, swap = rope
    out_widths = [Q_W, KV_W, KV_W] + list(widths[1:])

    def body(a_ref, w_ref, b_ref, cos_ref, sin_ref, swap_ref, *outs):
        a_l, a_r = a_ref[:, :half], a_ref[:, half:]
        for j, (off, width) in enumerate(_pieces(widths)):
            acc = _dot(a_l, w_ref[0, off:off + width, :], "nt") + _dot(a_r, w_ref[1, off:off + width, :], "nt")
            acc = acc + b_ref[:, off:off + width]
            if zero_rows_below:
                rows = pl.program_id(0) * tm + lax.broadcasted_iota(jnp.int32, acc.shape, 0)
                acc = jnp.where(rows >= zero_rows_below, acc, 0.0)
            if j == 0:
                qkv = acc.astype(bf16).astype(f32)
                for o_ref, val in zip(outs[:3], _attn_prep(qkv, cos_ref[...], sin_ref[...], swap_ref[...])):
                    o_ref[...] = val.astype(o_ref.dtype)
            else:
                outs[2 + j][...] = acc.astype(outs[2 + j].dtype)

    table = pl.BlockSpec((tm, HEAD_DIM), lambda i: (i, 0))
    return pl.pallas_call(
        body, name=name, grid=(m // tm,),
        in_specs=[pl.BlockSpec((tm, kdim), lambda i: (i, 0)), _full(w_lr.shape), _full(bias.shape), table, table,
                  _full(swap.shape)],
        out_specs=[pl.BlockSpec((tm, w), lambda i: (i, 0)) for w in out_widths],
        out_shape=[jax.ShapeDtypeStruct((m, w), bf16) for w in out_widths],
        compiler_params=_params(("parallel",)),
    )(a, w_lr, bias, cos_t, sin_t, swap)


def _proj_in_bwd(d_list, w_lr, *, name):
    m = d_list[0].shape[0]
    half = w_lr.shape[2]
    widths = [d.shape[1] for d in d_list]
    tm = _tile(m, MM_ROWS)

    def body(*refs):
        w_ref, o_ref = refs[-2], refs[-1]
        for side in range(2):
            acc = None
            for (off, width), d_ref in zip(_pieces(widths), refs):
                term = _dot(d_ref[...], w_ref[side, off:off + width, :])
                acc = term if acc is None else acc + term
            o_ref[:, side * half:(side + 1) * half] = acc.astype(o_ref.dtype)

    return pl.pallas_call(
        body, name=name, grid=(m // tm,),
        in_specs=[pl.BlockSpec((tm, w), lambda i: (i, 0)) for w in widths] + [_full(w_lr.shape)],
        out_specs=pl.BlockSpec((tm, 2 * half), lambda i: (i, 0)),
        out_shape=jax.ShapeDtypeStruct((m, 2 * half), bf16),
        compiler_params=_params(("parallel",)),
    )(*d_list, w_lr)


def _residual_norm(a, w, res, g, *, name):
    m, d = res.shape
    tm = _tile(m, MM_ROWS)

    def body(a_ref, w_ref, r_ref, g_ref, h_ref, n_ref):
        h = _dot(a_ref[...], w_ref[...]) + r_ref[...]
        h_ref[...] = h
        n_ref[...] = _rms(h, g_ref[...]).astype(n_ref.dtype)

    tile = pl.BlockSpec((tm, d), lambda i: (i, 0))
    return pl.pallas_call(
        body, name=name, grid=(m // tm,),
        in_specs=[pl.BlockSpec((tm, a.shape[1]), lambda i: (i, 0)), _full(w.shape), tile, _full(g.shape)],
        out_specs=[tile, tile],
        out_shape=[jax.ShapeDtypeStruct((m, d), f32), jax.ShapeDtypeStruct((m, d), bf16)],
        compiler_params=_params(("parallel",)),
    )(a, w, res, g)


def _residual_norm_bwd(d_list, w_list, h, g, dh_out, *, name):
    m, d = h.shape
    k = len(d_list)
    tm = _tile(m)

    def body(*refs):
        h_ref, g_ref, dho_ref, dh_ref, dg_ref = refs[2 * k:]
        dn = _dot(refs[0][...], refs[k][...])
        for i in range(1, k):
            dn = dn + _dot(refs[i][...], refs[k + i][...])
        _, vjp = jax.vjp(lambda hv, gv: (_rms(hv, gv), hv), h_ref[...], g_ref[...])
        dh, dg = vjp((dn, dho_ref[...]))
        dh_ref[...] = dh

        @pl.when(pl.program_id(0) == 0)
        def _():
            dg_ref[...] = jnp.zeros_like(dg_ref)

        dg_ref[...] += dg

    tile = pl.BlockSpec((tm, d), lambda i: (i, 0))
    return pl.pallas_call(
        body, name=name, grid=(m // tm,),
        in_specs=[pl.BlockSpec((tm, a.shape[1]), lambda i: (i, 0)) for a in d_list] + [_full(w.shape) for w in w_list]
        + [tile, _full(g.shape), tile],
        out_specs=[tile, _full(g.shape)],
        out_shape=[jax.ShapeDtypeStruct((m, d), f32), jax.ShapeDtypeStruct(g.shape, f32)],
        compiler_params=_params(("arbitrary",)),
    )(*d_list, *w_list, h, g, dh_out)


def _mm_tn(a, b, *, name, colsum=False, out_dtype=bf16, into=None):
    r, m = a.shape
    n = b.shape[1]
    tr = _tile(r, 1408)
    tmo = m
    for cand in (1408, 1024, 768, 512):
        if m > 1024 and m % cand == 0:
            tmo = cand
            break
    steps = r // tr

    rows, offset, target = into or (m, 0, None)

    def body(a_ref, b_ref, *rest):
        o_ref, rest = (rest[0], rest[1:]) if target is None else (rest[1], rest[2:])
        acc = rest[-1]
        i = pl.program_id(1)

        @pl.when(i == 0)
        def _():
            acc[...] = jnp.zeros_like(acc)
            if colsum:
                rest[0][...] = jnp.zeros_like(rest[0])

        acc[...] += _dot(a_ref[...], b_ref[...], "tn")
        if colsum:
            rest[0][...] += jnp.sum(a_ref[...].astype(f32), axis=0, keepdims=True)

        @pl.when(i == steps - 1)
        def _():
            o_ref[...] = acc[...].astype(out_dtype)

    out_shape = [jax.ShapeDtypeStruct((rows, n), out_dtype)]
    if offset % tmo == 0:
        out_specs = [pl.BlockSpec((tmo, n), lambda j, i: (offset // tmo + j, 0))]
    else:
        out_specs = [pl.BlockSpec((pl.Element(tmo), pl.Element(n)), lambda j, i: (pl.multiple_of(offset + j * tmo, math.gcd(offset, tmo)), 0))]
    if colsum:
        out_shape.append(jax.ShapeDtypeStruct((1, m), f32))
        out_specs.append(pl.BlockSpec((1, tmo), lambda j, i: (0, j)))
    in_specs = [pl.BlockSpec((tr, tmo), lambda j, i: (i, j)), pl.BlockSpec((tr, n), lambda j, i: (i, 0))]
    res = pl.pallas_call(
        body, name=name, grid=(m // tmo, steps),
        in_specs=in_specs + ([] if target is None else [pl.BlockSpec(memory_space=pl.ANY)]),
        out_specs=out_specs, out_shape=out_shape,
        scratch_shapes=[pltpu.VMEM((tmo, n), f32)],
        input_output_aliases={} if target is None else {2: 0},
        compiler_params=_params(("parallel", "arbitrary")),
    )(a, b, *([] if target is None else [target]))
    return res if colsum else res[0]


def _rowwise(fn, rows, params, outs, *, name, tm=None):
    m = rows[0].shape[0]
    tm = tm or _tile(m, MM_ROWS)
    nr, npar = len(rows), len(params)

    def body(*refs):
        vals = [r[...] for r in refs[:nr + npar]]
        res = fn(*vals)
        for o_ref, v in zip(refs[nr + npar:], res):
            o_ref[...] = v.astype(o_ref.dtype)

    return pl.pallas_call(
        body, name=name, grid=(m // tm,),
        in_specs=[pl.BlockSpec((tm, r.shape[1]), lambda i: (i, 0)) for r in rows] + [_full(p.shape) for p in params],
        out_specs=[pl.BlockSpec((tm, w), lambda i: (i, 0)) for w, _ in outs],
        out_shape=[jax.ShapeDtypeStruct((m, w), dt) for w, dt in outs],
        compiler_params=_params(("parallel",)),
    )(*rows, *params)


def _rowwise_bwd(fn, rows, params, cts, *, name, diff_rows, diff_params, tm=None, zero_rows_below=0, out_dtypes=None):
    m = rows[0].shape[0]
    tm = tm or _tile(m)
    nr, npar = len(rows), len(params)
    d_idx = [i for i in range(nr) if diff_rows[i]]
    p_idx = [i for i in range(npar) if diff_params[i]]
    out_dtypes = out_dtypes or [f32] * len(d_idx)
    flat_cts = [c for group in cts for c in group]
    n_ct = len(flat_cts)

    def body(*refs):
        vals = [r[...] for r in refs[:nr + npar]]
        ct_refs = refs[nr + npar:nr + npar + n_ct]
        out_refs = refs[nr + npar + n_ct:]
        ct_vals, k = [], 0
        for group in cts:
            acc = ct_refs[k][...].astype(f32)
            for extra in range(1, len(group)):
                acc = acc + ct_refs[k + extra][...].astype(f32)
            k += len(group)
            if zero_rows_below:
                rr = pl.program_id(0) * tm + lax.broadcasted_iota(jnp.int32, acc.shape, 0)
                acc = jnp.where(rr >= zero_rows_below, acc, 0.0)
            ct_vals.append(acc)

        def g(*dargs):
            full = list(vals)
            for pos, i in enumerate(d_idx):
                full[i] = dargs[pos]
            for pos, i in enumerate(p_idx):
                full[nr + i] = dargs[len(d_idx) + pos]
            return tuple(fn(*full))

        _, vjp = jax.vjp(g, *[vals[i].astype(f32) for i in d_idx], *[vals[nr + i] for i in p_idx])
        grads = vjp(tuple(ct_vals))
        for pos in range(len(d_idx)):
            out_refs[pos][...] = grads[pos].astype(out_refs[pos].dtype)
        first = pl.program_id(0) == 0
        for pos in range(len(p_idx)):
            o_ref = out_refs[len(d_idx) + pos]

            @pl.when(first)
            def _(o_ref=o_ref):
                o_ref[...] = jnp.zeros_like(o_ref)

            o_ref[...] += grads[len(d_idx) + pos]

    return pl.pallas_call(
        body, name=name, grid=(m // tm,),
        in_specs=[pl.BlockSpec((tm, r.shape[1]), lambda i: (i, 0)) for r in rows] + [_full(p.shape) for p in params]
        + [pl.BlockSpec((tm, c.shape[1]), lambda i: (i, 0)) for c in flat_cts],
        out_specs=[pl.BlockSpec((tm, rows[i].shape[1]), lambda i_: (i_, 0)) for i in d_idx]
        + [_full(params[i].shape) for i in p_idx],
        out_shape=[jax.ShapeDtypeStruct(rows[i].shape, dt) for i, dt in zip(d_idx, out_dtypes)]
        + [jax.ShapeDtypeStruct(params[i].shape, f32) for i in p_idx],
        compiler_params=_params(("arbitrary",)),
    )(*rows, *params, *flat_cts)


def _rms(x, g):
    return x * lax.rsqrt(jnp.mean(x * x, axis=-1, keepdims=True) + RMS_EPS) * g


def _head_sum_matrix(width, head):
    idx = jnp.arange(width) // head
    return (idx[:, None] == idx[None, :]).astype(f32)


def _rope_tables(lp):
    half = ROPE_DIM // 2
    pos = (np.arange(lp) - PAD).astype(np.float32)
    inv_freq = np.power(np.float32(ROPE_THETA), -np.arange(half, dtype=np.float32) * np.float32(2.0 / ROPE_DIM))
    ang = pos[:, None] * inv_freq[None, :].astype(np.float32)
    cos, sin = np.cos(ang), np.sin(ang)
    ones = np.ones((lp, HEAD_DIM - ROPE_DIM), np.float32)
    cos_t = np.concatenate([cos, cos, ones], axis=1)
    sin_t = np.concatenate([-sin, sin, 0.0 * ones], axis=1)
    i = np.arange(HEAD_DIM)
    src = np.where(i < half, i + half, np.where(i < ROPE_DIM, i - half, i))
    swap = ((i[:, None] == src[None, :]) & (i[None, :] < ROPE_DIM)).astype(np.float32)
    return jnp.asarray(cos_t, f32), jnp.asarray(sin_t, f32), jnp.asarray(swap, f32)


def _attn_prep(qkv, cos_t, sin_t, swap):
    outs = []
    for h in range(Q_HEADS + KV_HEADS):
        t = qkv[:, h * HEAD_DIM:(h + 1) * HEAD_DIM]
        outs.append(t * cos_t + _dot_const(t, swap) * sin_t)
    q = jnp.concatenate(outs[:Q_HEADS], axis=1)
    k = jnp.concatenate(outs[Q_HEADS:], axis=1)
    return q, k, qkv[:, Q_W + KV_W:]


def _attn_prep_transposed(dq, dk, dv, cos_t, sin_t, swap, dk_meta, dv_meta):
    first = pl.program_id(0) == 0

    def with_meta(d, d_meta):
        rest = jnp.zeros((d.shape[0] - BLOCK, KV_W), f32)
        return d.astype(f32) + jnp.where(first, jnp.concatenate([d_meta, rest], axis=0), 0.0)

    parts = []
    for d, heads in ((dq.astype(f32), Q_HEADS), (with_meta(dk, dk_meta), KV_HEADS)):
        for h in range(heads):
            t = d[:, h * HEAD_DIM:(h + 1) * HEAD_DIM]
            parts.append(t * cos_t + _two_pass(t * sin_t, swap, "nt"))
    return (jnp.concatenate(parts + [with_meta(dv, dv_meta)], axis=1),)


def _softplus(z):
    return jnp.maximum(z, 0.0) + jnp.log1p(jnp.exp(-jnp.abs(z)))


def _rwkv_prep(rkv, lora, w0, w2, a0, a2, g2, k_k, k_a, hsum):
    r = rkv[:, :RWKV_DIM]
    k = rkv[:, RWKV_DIM:2 * RWKV_DIM]
    v = rkv[:, 2 * RWKV_DIM:]
    dw = lora[:, :DECAY_LORA]
    da = lora[:, DECAY_LORA:DECAY_LORA + AAA_LORA]
    dg = lora[:, DECAY_LORA + AAA_LORA:]
    w = -_softplus(-(w0 + _dot(jnp.tanh(dw), w2))) - 0.5
    a = jax.nn.sigmoid(a0 + _dot(da, a2))
    g = _dot(jax.nn.sigmoid(dg), g2)
    kk = k * k_k
    kk = kk * lax.rsqrt(jnp.maximum(_dot_const(kk * kk, hsum), 1e-24))
    k = k * (1.0 + (a - 1.0) * k_a)
    log_decay = -jnp.exp(w)
    return r, log_decay, k, v, -kk, kk * a, g


def _rwkv_post(y, r, k, v, g, ln_w, ln_b, r_k, hmean):
    hsum = hmean * RWKV_HEAD
    mean = _dot_const(y, hmean)
    yc = y - mean
    var = _dot_const(yc * yc, hmean)
    yn = yc * lax.rsqrt(var + RWKV_LN_EPS) * ln_w + ln_b
    bonus = _dot_const(r * k * r_k, hsum) * v
    return ((yn + bonus) * g,)


def _merge(gates, br_a, br_r):
    sg = jax.nn.sigmoid(gates)
    return (sg[:, :D_MODEL] * br_a + sg[:, D_MODEL:] * br_r,)


def _swiglu(gate, up):
    return (jax.nn.silu(gate) * up,)


def _ffn_in(f, w_gate_t, w_up_t, *, name):
    m, d = f.shape
    n = w_gate_t.shape[0]
    tm = _tile(m)

    def body(f_ref, wg_ref, wu_ref, g_ref, u_ref, a_ref):
        g = _dot(f_ref[...], wg_ref[...], "nt")
        u = _dot(f_ref[...], wu_ref[...], "nt")
        g_ref[...] = g.astype(g_ref.dtype)
        u_ref[...] = u.astype(u_ref.dtype)
        a_ref[...] = _swiglu(g, u)[0].astype(a_ref.dtype)

    spec = pl.BlockSpec((tm, n), lambda i: (i, 0))
    return pl.pallas_call(
        body, name=name, grid=(m // tm,),
        in_specs=[pl.BlockSpec((tm, d), lambda i: (i, 0)), _full(w_gate_t.shape), _full(w_up_t.shape)],
        out_specs=[spec] * 3, out_shape=[jax.ShapeDtypeStruct((m, n), bf16)] * 3,
        compiler_params=_params(("parallel",)),
    )(f, w_gate_t, w_up_t)


def _branch_merge(y_attn, y_rwkv, w_attn_t, w_rwkv_t, gates, *, name):
    m = y_attn.shape[0]
    tm = _tile(m, MM_ROWS)

    def body(ya_ref, yr_ref, wa_ref, wr_ref, g_ref, a_ref, r_ref, o_ref):
        br_a = _dot(ya_ref[...], wa_ref[...], "nt")
        br_r = _dot(yr_ref[...], wr_ref[...], "nt")
        a_ref[...] = br_a.astype(a_ref.dtype)
        r_ref[...] = br_r.astype(r_ref.dtype)
        o_ref[...] = _merge(g_ref[...].astype(f32), br_a, br_r)[0].astype(o_ref.dtype)

    rows = lambda a: pl.BlockSpec((tm, a.shape[1]), lambda i: (i, 0))
    spec = pl.BlockSpec((tm, D_MODEL), lambda i: (i, 0))
    return pl.pallas_call(
        body, name=name, grid=(m // tm,),
        in_specs=[rows(y_attn), rows(y_rwkv), _full(w_attn_t.shape), _full(w_rwkv_t.shape), rows(gates)],
        out_specs=[spec] * 3, out_shape=[jax.ShapeDtypeStruct((m, D_MODEL), bf16)] * 3,
        compiler_params=_params(("parallel",)),
    )(y_attn, y_rwkv, w_attn_t, w_rwkv_t, gates)


def _branch_merge_bwd(dh, w_o, gates, br_a, br_r, w_attn_t, w_rwkv_t, *, name):
    m = dh.shape[0]
    tm = _tile(m, MM_ROWS)

    def body(dh_ref, w_ref, g_ref, a_ref, r_ref, wa_ref, wr_ref, dg_ref, da_ref, dr_ref, dya_ref, dyr_ref):
        dmerged = _dot(dh_ref[...], w_ref[...], "nt")
        _, vjp = jax.vjp(lambda g, a, r: _merge(g, a, r)[0], g_ref[...].astype(f32), a_ref[...].astype(f32),
                         r_ref[...].astype(f32))
        dg, da, dr = vjp(dmerged)
        dg_ref[...] = dg.astype(dg_ref.dtype)
        da_ref[...] = da.astype(da_ref.dtype)
        dr_ref[...] = dr.astype(dr_ref.dtype)
        dya_ref[...] = _dot(da, wa_ref[...])
        dyr_ref[...] = _dot(dr, wr_ref[...])

    rows = lambda a: pl.BlockSpec((tm, a.shape[1]), lambda i: (i, 0))
    mixer = pl.BlockSpec((tm, w_attn_t.shape[1]), lambda i: (i, 0))
    return pl.pallas_call(
        body, name=name, grid=(m // tm,),
        in_specs=[rows(dh), _full(w_o.shape), rows(gates), rows(br_a), rows(br_r), _full(w_attn_t.shape),
                  _full(w_rwkv_t.shape)],
        out_specs=[rows(gates), rows(br_a), rows(br_r), mixer, mixer],
        out_shape=[jax.ShapeDtypeStruct(gates.shape, bf16), jax.ShapeDtypeStruct(br_a.shape, bf16),
                   jax.ShapeDtypeStruct(br_r.shape, bf16), jax.ShapeDtypeStruct((m, w_attn_t.shape[1]), f32),
                   jax.ShapeDtypeStruct((m, w_rwkv_t.shape[1]), f32)],
        compiler_params=_params(("parallel",)),
    )(dh, w_o, gates, br_a, br_r, w_attn_t, w_rwkv_t)


def _ffn_in_bwd(dh, w_down, gate, up, *, name):
    m, d = dh.shape
    n = w_down.shape[0]
    tm = _tile(m)

    def body(dh_ref, w_ref, g_ref, u_ref, dg_ref, du_ref):
        dact = _dot(dh_ref[...], w_ref[...], "nt")
        _, vjp = jax.vjp(lambda a, b: _swiglu(a, b)[0], g_ref[...].astype(f32), u_ref[...].astype(f32))
        dg, du = vjp(dact)
        dg_ref[...] = dg.astype(dg_ref.dtype)
        du_ref[...] = du.astype(du_ref.dtype)

    spec = pl.BlockSpec((tm, n), lambda i: (i, 0))
    return pl.pallas_call(
        body, name=name, grid=(m // tm,),
        in_specs=[pl.BlockSpec((tm, d), lambda i: (i, 0)), _full(w_down.shape), spec, spec],
        out_specs=[spec] * 2, out_shape=[jax.ShapeDtypeStruct((m, n), bf16)] * 2,
        compiler_params=_params(("parallel",)),
    )(dh, w_down, gate, up)


def _ffn_bwd(dh, w_down, gate, up, w_gate_t, w_up_t, h, g, *, name):
    m, d = dh.shape
    n = w_down.shape[0]
    tm = _tile(m)
    halves = [(c * (n // 2), (c + 1) * (n // 2)) for c in range(2)]

    def body(dh_ref, wd_ref, g_ref, u_ref, wg_ref, wu_ref, h_ref, gn_ref, dg_ref, du_ref, dh1_ref, dgn_ref):
        dn = jnp.zeros((tm, d), f32)
        for lo, hi in halves:
            dact = _dot(dh_ref[...], wd_ref[lo:hi, :], "nt")
            _, vjp = jax.vjp(lambda a, b: _swiglu(a, b)[0], g_ref[:, lo:hi].astype(f32), u_ref[:, lo:hi].astype(f32))
            dg, du = (t.astype(bf16) for t in vjp(dact))
            dg_ref[:, lo:hi] = dg
            du_ref[:, lo:hi] = du
            dn = dn + _dot(dg, wg_ref[lo:hi, :]) + _dot(du, wu_ref[lo:hi, :])
        _, vjp = jax.vjp(lambda hv, gv: (_rms(hv, gv), hv), h_ref[...], gn_ref[...])
        dh1, dgn = vjp((dn, dh_ref[...]))
        dh1_ref[...] = dh1

        @pl.when(pl.program_id(0) == 0)
        def _():
            dgn_ref[...] = jnp.zeros_like(dgn_ref)

        dgn_ref[...] += dgn

    wide = pl.BlockSpec((tm, n), lambda i: (i, 0))
    tile = pl.BlockSpec((tm, d), lambda i: (i, 0))
    return pl.pallas_call(
        body, name=name, grid=(m // tm,),
        in_specs=[tile, _full(w_down.shape), wide, wide, _full(w_gate_t.shape), _full(w_up_t.shape), tile,
                  _full(g.shape)],
        out_specs=[wide, wide, tile, _full(g.shape)],
        out_shape=[jax.ShapeDtypeStruct((m, n), bf16)] * 2 + [jax.ShapeDtypeStruct((m, d), f32),
                                                               jax.ShapeDtypeStruct(g.shape, f32)],
        compiler_params=_params(("arbitrary",)),
    )(dh, w_down, gate, up, w_gate_t, w_up_t, h, g)


HALO = 16


def _previous_rows(x, before_ref, first_tile):
    rows = lax.broadcasted_iota(jnp.int32, x.shape, 0)
    last = jnp.where(first_tile, 0.0, before_ref[HALO - 1:HALO, :].astype(f32))
    return jnp.where(rows == 0, last, pltpu.roll(x, 1, axis=0))


def _mixer_inputs(ps, mixes, params, *, name):
    m = ps[0].shape[0]
    tm = _tile(m)
    sub = tm // HALO
    n_par = len(params)

    def body(*refs):
        first = pl.program_id(0) == 0
        pf = []
        for k in range(2):
            x = refs[k][...].astype(f32)
            pf.append(x + (_previous_rows(x, refs[2 + k], first) - x) * refs[4 + k][...])
        res = _rwkv_prep(*pf, *[ref[...] for ref in refs[6:6 + n_par]])
        for o_ref, val in zip(refs[6 + n_par:], res):
            o_ref[...] = val

    tile = lambda a: pl.BlockSpec((tm, a.shape[1]), lambda i: (i, 0))
    before = lambda a: pl.BlockSpec((HALO, a.shape[1]), lambda i: (jnp.maximum(i * sub - 1, 0), 0))
    out = pl.BlockSpec((tm, RWKV_DIM), lambda i: (i, 0))
    return pl.pallas_call(
        body, name=name, grid=(m // tm,),
        in_specs=[tile(a) for a in ps] + [before(a) for a in ps] + [_full(a.shape) for a in mixes + params],
        out_specs=[out] * 7, out_shape=[jax.ShapeDtypeStruct((m, RWKV_DIM), f32)] * 7,
        compiler_params=_params(("parallel",)),
    )(*ps, *ps, *mixes, *params)


def _mixer_inputs_bwd(ps, mixes, params, cts, *, name):
    m = ps[0].shape[0]
    tm = _tile(m)
    sub = tm // HALO
    nt = m // tm
    n_par = len(params)
    flat_cts = [c for group in cts for c in group]
    n_ct = len(flat_cts)

    def body(*refs):
        i = pl.program_id(0)
        tile_index = nt - 1 - i
        ct_refs = refs[6 + n_par:6 + n_par + n_ct]
        dp_refs = refs[6 + n_par + n_ct:8 + n_par + n_ct]
        dmix_refs = refs[8 + n_par + n_ct:10 + n_par + n_ct]
        dpar_refs = refs[10 + n_par + n_ct:9 + 2 * n_par + n_ct]
        carries = refs[9 + 2 * n_par + n_ct:]
        rows1 = tile_index * tm + lax.broadcasted_iota(jnp.int32, (tm, 1), 0)
        live = rows1 >= PAD

        @pl.when(i == 0)
        def _():
            for ref in (*dmix_refs, *dpar_refs, *carries):
                ref[...] = jnp.zeros_like(ref)

        xs, prevs, pf = [], [], []
        for k in range(2):
            x = refs[k][...].astype(f32)
            xp = _previous_rows(x, refs[2 + k], tile_index == 0)
            xs.append(x)
            prevs.append(xp)
            pf.append(x + (xp - x) * refs[4 + k][...])
        ct_vals, pos = [], 0
        for group in cts:
            acc = ct_refs[pos][...].astype(f32)
            for extra in range(1, len(group)):
                acc = acc + ct_refs[pos + extra][...].astype(f32)
            pos += len(group)
            ct_vals.append(jnp.where(live, acc, 0.0))
        par_vals = [ref[...] for ref in refs[6:6 + n_par]]
        _, vjp = jax.vjp(lambda *args: _rwkv_prep(*args, par_vals[-1]), *pf, *par_vals[:-1])
        g = vjp(tuple(ct_vals))
        for k in range(2):
            dpf = g[k]
            mixv = refs[4 + k][...]
            dm = dpf * mixv
            rows = lax.broadcasted_iota(jnp.int32, dm.shape, 0)
            dm_next = jnp.where(rows == tm - 1, carries[k][...], pltpu.roll(dm, tm - 1, axis=0))
            dp_refs[k][...] = jnp.where(live, dpf - dm + dm_next, 0.0).astype(dp_refs[k].dtype)
            carries[k][...] = dm[0:1, :]
            dmix_refs[k][...] += jnp.sum(dpf * (prevs[k] - xs[k]), axis=0, keepdims=True)
        for ref, val in zip(dpar_refs, g[2:]):
            ref[...] += val

    tile = lambda a: pl.BlockSpec((tm, a.shape[1]), lambda i: (nt - 1 - i, 0))
    before = lambda a: pl.BlockSpec((HALO, a.shape[1]), lambda i: (jnp.maximum((nt - 1 - i) * sub - 1, 0), 0))
    return pl.pallas_call(
        body, name=name, grid=(nt,),
        in_specs=[tile(a) for a in ps] + [before(a) for a in ps] + [_full(a.shape) for a in mixes + params]
        + [tile(c) for c in flat_cts],
        out_specs=[tile(a) for a in ps] + [_full(a.shape) for a in mixes + params[:-1]],
        out_shape=[jax.ShapeDtypeStruct(a.shape, bf16) for a in ps]
        + [jax.ShapeDtypeStruct(a.shape, f32) for a in mixes + params[:-1]],
        scratch_shapes=[pltpu.VMEM((1, a.shape[1]), f32) for a in ps],
        compiler_params=_params(("arbitrary",)),
    )(*ps, *ps, *mixes, *params, *flat_cts)


def _attn_masks(blk):
    qi = lax.broadcasted_iota(jnp.int32, (BLOCK, BLOCK), 0)
    ki = lax.broadcasted_iota(jnp.int32, (BLOCK, BLOCK), 1)
    qpos = blk * BLOCK + qi - PAD
    kpos_c = blk * BLOCK + ki - PAD
    kpos_p = kpos_c - BLOCK
    kpos_m = ki - PAD

    def band(kpos):
        return (kpos >= N_META) & (kpos <= qpos) & (qpos - kpos < WINDOW)

    return band(kpos_p), band(kpos_c), (kpos_m >= 0) & (kpos_m <= qpos)


def _attn_probs(qs, k3s, sink, oks):
    s = [[jnp.where(ok, _dot(qh, kx, "nt"), NEG_INF) for kx, ok in zip(k3, oks)] for qh, k3 in zip(qs, k3s)]
    mx = [jnp.maximum(jnp.maximum(jnp.max(t[0], -1, keepdims=True), jnp.max(t[1], -1, keepdims=True)),
                      jnp.maximum(jnp.max(t[2], -1, keepdims=True), sk)) for t, sk in zip(s, sink)]
    e = [[jnp.exp(tx - m) for tx in t] for t, m in zip(s, mx)]
    e_sink = [jnp.exp(sk - m) for sk, m in zip(sink, mx)]
    inv = [1.0 / (jnp.sum(t[0], -1, keepdims=True) + jnp.sum(t[1], -1, keepdims=True)
                  + jnp.sum(t[2], -1, keepdims=True) + es) for t, es in zip(e, e_sink)]
    return [[tx * i for tx in t] for t, i in zip(e, inv)], [es * i for es, i in zip(e_sink, inv)]


def _head_cols(i):
    return slice(i * HEAD_DIM, (i + 1) * HEAD_DIM)


def _attn_operands(refs):
    q_ref, kp_ref, kc_ref, km_ref, vp_ref, vc_ref, vm_ref, s_ref = refs
    qs = [q_ref[:, _head_cols(i)] * (HEAD_DIM ** -0.5) for i in range(Q_HEADS)]
    k3 = [[ref[:, _head_cols(h)] for ref in (kp_ref, kc_ref, km_ref)] for h in range(KV_HEADS)]
    v3 = [[ref[:, _head_cols(h)] for ref in (vp_ref, vc_ref, vm_ref)] for h in range(KV_HEADS)]
    return (qs, [k3[i // GROUP] for i in range(Q_HEADS)], [v3[i // GROUP] for i in range(Q_HEADS)],
            [s_ref[:, i:i + 1] for i in range(Q_HEADS)])


def _attention(q, k, v, sinks, *, name):
    lp = q.shape[0]
    nb = lp // BLOCK
    prev = lambda i: (jnp.maximum(i - 1, 0), 0)
    cur = lambda i: (i, 0)
    meta = lambda i: (0, 0)
    kv = lambda index: pl.BlockSpec((BLOCK, KV_W), index)

    def body(*refs):
        o_ref = refs[-1]
        qs, k3s, v3s, sink = _attn_operands(refs[:-1])
        p, _ = _attn_probs(qs, k3s, sink, _attn_masks(pl.program_id(0)))
        out = [_dot(ph[0], v3[0]) + _dot(ph[1], v3[1]) + _dot(ph[2], v3[2]) for ph, v3 in zip(p, v3s)]
        for i in range(Q_HEADS):
            o_ref[:, _head_cols(i)] = out[i].astype(o_ref.dtype)

    return pl.pallas_call(
        body, name=name, grid=(nb,),
        in_specs=[pl.BlockSpec((BLOCK, Q_W), cur), kv(prev), kv(cur), kv(meta), kv(prev), kv(cur), kv(meta),
                  _full((1, Q_HEADS))],
        out_specs=pl.BlockSpec((BLOCK, Q_W), cur),
        out_shape=jax.ShapeDtypeStruct((lp, Q_W), bf16),
        compiler_params=_params(("parallel",)),
    )(q, k, k, k, v, v, v, sinks)


def _attention_bwd(q, k, v, sinks, out, do, *, name):
    lp = q.shape[0]
    nb = lp // BLOCK
    cur = lambda n: (jnp.minimum(n, nb - 1), 0)
    prev = lambda n: (jnp.maximum(jnp.minimum(n, nb - 1) - 1, 0), 0)
    behind = lambda n: (jnp.maximum(n - 1, 0), 0)
    meta = lambda n: (0, 0)
    kv = lambda index: pl.BlockSpec((BLOCK, KV_W), index)
    scale = HEAD_DIM ** -0.5

    def body(*refs):
        ins, fwd_ref, do_ref = refs[:8], refs[8], refs[9]
        dq_ref, dk_ref, dv_ref, dkm_ref, dvm_ref, ds_ref, carry_k, carry_v = refs[10:]
        n = pl.program_id(0)

        @pl.when(n == 0)
        def _():
            for ref in (dkm_ref, dvm_ref, ds_ref, carry_k, carry_v):
                ref[...] = jnp.zeros_like(ref)

        @pl.when(n < nb)
        def _():
            qs, k3s, v3s, sink = _attn_operands(ins)
            do = [do_ref[:, _head_cols(i)] for i in range(Q_HEADS)]
            p, p_sink = _attn_probs(qs, k3s, sink, _attn_masks(n))
            delta = [jnp.sum(d * fwd_ref[:, _head_cols(i)].astype(f32), -1, keepdims=True) for i, d in enumerate(do)]
            dp = [[_dot(d, vx, "nt") for vx in v3] for d, v3 in zip(do, v3s)]
            ds = [[px * (dx - dl) for px, dx in zip(ph, dh)] for ph, dh, dl in zip(p, dp, delta)]
            dq = [_dot(dsh[0], k3[0]) + _dot(dsh[1], k3[1]) + _dot(dsh[2], k3[2]) for dsh, k3 in zip(ds, k3s)]
            for i in range(Q_HEADS):
                dq_ref[:, _head_cols(i)] = dq[i] * scale
                ds_ref[:, i:i + 1] -= jnp.sum(p_sink[i] * delta[i], axis=0, keepdims=True)
            for h in range(KV_HEADS):
                group = slice(h * GROUP, (h + 1) * GROUP)
                q_all = jnp.concatenate(qs[group], axis=0)
                do_all = jnp.concatenate(do[group], axis=0)
                dk3 = [_dot(jnp.concatenate([dsh[x] for dsh in ds[group]], axis=0), q_all, "tn") for x in range(3)]
                dv3 = [_dot(jnp.concatenate([ph[x] for ph in p[group]], axis=0), do_all, "tn") for x in range(3)]
                hs = _head_cols(h)
                for out_ref, carry, meta_ref, d3 in ((dk_ref, carry_k, dkm_ref, dk3),
                                                     (dv_ref, carry_v, dvm_ref, dv3)):
                    out_ref[:, hs] = carry[:, hs] + d3[0]
                    carry[:, hs] = d3[1]
                    meta_ref[:, hs] += d3[2]

        @pl.when(n == nb)
        def _():
            dk_ref[...] = carry_k[...]
            dv_ref[...] = carry_v[...]

    kv_shape = jax.ShapeDtypeStruct((lp, KV_W), f32)
    one_shape = jax.ShapeDtypeStruct((BLOCK, KV_W), f32)
    return pl.pallas_call(
        body, name=name, grid=(nb + 1,),
        in_specs=[pl.BlockSpec((BLOCK, Q_W), cur), kv(prev), kv(cur), kv(meta), kv(prev), kv(cur), kv(meta),
                  _full((1, Q_HEADS)), pl.BlockSpec((BLOCK, Q_W), cur), pl.BlockSpec((BLOCK, Q_W), cur)],
        out_specs=[pl.BlockSpec((BLOCK, Q_W), cur), kv(behind), kv(behind), kv(meta), kv(meta),
                   _full((1, Q_HEADS))],
        out_shape=[jax.ShapeDtypeStruct((lp, Q_W), f32), kv_shape, kv_shape, one_shape, one_shape,
                   jax.ShapeDtypeStruct((1, Q_HEADS), f32)],
        scratch_shapes=[pltpu.VMEM((BLOCK, KV_W), f32), pltpu.VMEM((BLOCK, KV_W), f32)],
        compiler_params=_params(("arbitrary",)),
    )(q, k, k, k, v, v, v, sinks, out, do)


@jax.custom_vjp
def _known_inverse(l, x):
    return x


def _known_inverse_fwd(l, x):
    return x, x


def _known_inverse_bwd(x, ct):
    return _dot(_dot(x, ct, "tn"), x, "nt"), jnp.zeros_like(x)


_known_inverse.defvjp(_known_inverse_fwd, _known_inverse_bwd)


@jax.custom_vjp
def _decayed(x, c):
    return (x * jnp.exp(c)).astype(bf16).astype(f32)


def _decayed_fwd(x, c):
    e = jnp.exp(c)
    out = (x * e).astype(bf16).astype(f32)
    return out, (e, out)


def _decayed_bwd(res, ct):
    e, out = res
    return ct * e, ct * out


_decayed.defvjp(_decayed_fwd, _decayed_bwd)


@jax.custom_vjp
def _pair(x, y):
    return _dot(x, y, "nt")


def _pair_fwd(x, y):
    return _dot(x, y, "nt"), (x, y)


def _pair_bwd(res, ct):
    x, y = res
    hi = ct.astype(bf16)
    lo = (ct - hi.astype(f32)).astype(bf16)
    return _dot(hi, y) + _dot(lo, y), _dot(hi, x, "tn") + _dot(lo, x, "tn")


_pair.defvjp(_pair_fwd, _pair_bwd)


def _scan_chunk(s0, r, lw, k, v, a, b, inv=None):
    t = r[0].shape[0]
    ii = lax.broadcasted_iota(jnp.int32, (t, t), 0)
    jj = lax.broadcasted_iota(jnp.int32, (t, t), 1)
    incl = jj <= ii
    strict = jj < ii
    tri = incl.astype(f32)
    eye = jnp.where(ii == jj, 1.0, 0.0)
    cl = [_const_dot(tri, x) for x in lw]
    mid = [c[t // 2 - 1:t // 2, :] for c in cl]
    s0 = [s * jnp.exp(m) for s, m in zip(s0, mid)]
    cl = [c - m for c, m in zip(cl, mid)]
    rt = [_decayed(x, c) for x, c in zip(r, cl)]
    at = [_decayed(x, c - l) for x, c, l in zip(a, cl, lw)]
    bt = [_decayed(x, -c) for x, c in zip(b, cl)]
    kt = [_decayed(x, -c) for x, c in zip(k, cl)]
    l_ab = [jnp.where(strict, _pair(x, y), 0.0) for x, y in zip(at, bt)]
    l_ak = [jnp.where(strict, _pair(x, y), 0.0) for x, y in zip(at, kt)]
    r_b = [jnp.where(incl, _pair(x, y), 0.0) for x, y in zip(rt, bt)]
    r_k = [jnp.where(incl, _pair(x, y), 0.0) for x, y in zip(rt, kt)]
    if inv is None:
        inv = [eye + x for x in l_ab]
        pw = l_ab
        for _ in range(int(math.log2(t)) - 1):
            pw = [_dot(x, x) for x in pw]
            inv = [x + _dot(x, y) for x, y in zip(inv, pw)]
    else:
        inv = [_known_inverse(x, y) for x, y in zip(l_ab, inv)]
    rhs = [_dot(x, s, "nt") + _dot(m, y) for x, s, m, y in zip(at, s0, l_ak, v)]
    u = [_dot(x, y) for x, y in zip(inv, rhs)]
    y_s = [_dot(x, s, "nt") for x, s in zip(rt, s0)]
    y = [ys + _dot(m, uu) + _dot(n, vv) for ys, m, uu, n, vv in zip(y_s, r_b, u, r_k, v)]
    grow = [s + _dot(uu, x, "tn") + _dot(vv, z, "tn") for s, uu, x, vv, z in zip(s0, u, bt, v, kt)]
    s1 = [g * jnp.exp(c[t - 1:t, :]) for g, c in zip(grow, cl)]
    return y, s1, inv


def _head_rows(h):
    return slice(h * RWKV_HEAD, (h + 1) * RWKV_HEAD)


def _per_head(ref):
    return [ref[:, _head_rows(h)] for h in range(RWKV_HEADS)]


def _scan(r, lw, k, v, a, b, *, name):
    lp = r.shape[0]
    nc = lp // CHUNK
    row = pl.BlockSpec((CHUNK, RWKV_DIM), lambda c: (c, 0))

    def body(r_ref, lw_ref, k_ref, v_ref, a_ref, b_ref, y_ref, s_ref, inv_ref, state):
        @pl.when(pl.program_id(0) == 0)
        def _():
            state[...] = jnp.zeros_like(state)

        s_ref[...] = state[...]
        s0 = [state[_head_rows(h), :] for h in range(RWKV_HEADS)]
        y, s1, inv = _scan_chunk(s0, *[_per_head(ref) for ref in (r_ref, lw_ref, k_ref, v_ref, a_ref, b_ref)])
        for h in range(RWKV_HEADS):
            y_ref[:, _head_rows(h)] = y[h]
            state[_head_rows(h), :] = s1[h]
            inv_ref[h * CHUNK:(h + 1) * CHUNK, :] = inv[h].astype(inv_ref.dtype)

    return pl.pallas_call(
        body, name=name, grid=(nc,), in_specs=[row] * 6,
        out_specs=[row, pl.BlockSpec((RWKV_DIM, RWKV_HEAD), lambda c: (c, 0)),
                   pl.BlockSpec((RWKV_HEADS * CHUNK, CHUNK), lambda c: (c, 0))],
        out_shape=[jax.ShapeDtypeStruct((lp, RWKV_DIM), f32), jax.ShapeDtypeStruct((nc * RWKV_DIM, RWKV_HEAD), f32),
                   jax.ShapeDtypeStruct((nc * RWKV_HEADS * CHUNK, CHUNK), bf16)],
        scratch_shapes=[pltpu.VMEM((RWKV_DIM, RWKV_HEAD), f32)],
        compiler_params=_params(("arbitrary",)),
    )(r, lw, k, v, a, b)


def _scan_bwd(r, lw, k, v, a, b, states, inverses, dy, *, name):
    lp = r.shape[0]
    nc = lp // CHUNK
    back = lambda c: (nc - 1 - c, 0)
    row = pl.BlockSpec((CHUNK, RWKV_DIM), back)

    def body(r_ref, lw_ref, k_ref, v_ref, a_ref, b_ref, s_ref, inv_ref, dy_ref,
             dr_ref, dlw_ref, dk_ref, dv_ref, da_ref, db_ref, dstate):
        @pl.when(pl.program_id(0) == 0)
        def _():
            dstate[...] = jnp.zeros_like(dstate)

        outs = (dr_ref, dlw_ref, dk_ref, dv_ref, da_ref, db_ref)
        s0 = [s_ref[_head_rows(h), :] for h in range(RWKV_HEADS)]
        inv = [inv_ref[h * CHUNK:(h + 1) * CHUNK, :].astype(f32) for h in range(RWKV_HEADS)]
        _, vjp = jax.vjp(lambda *args: _scan_chunk(*args, inv=inv)[:2], s0,
                         *[_per_head(ref) for ref in (r_ref, lw_ref, k_ref, v_ref, a_ref, b_ref)])
        g = vjp((_per_head(dy_ref), [dstate[_head_rows(h), :] for h in range(RWKV_HEADS)]))
        for h in range(RWKV_HEADS):
            dstate[_head_rows(h), :] = g[0][h]
            for o_ref, gv in zip(outs, g[1:]):
                o_ref[:, _head_rows(h)] = gv[h]

    shape = jax.ShapeDtypeStruct((lp, RWKV_DIM), f32)
    return pl.pallas_call(
        body, name=name, grid=(nc,),
        in_specs=[row] * 6 + [pl.BlockSpec((RWKV_DIM, RWKV_HEAD), back),
                              pl.BlockSpec((RWKV_HEADS * CHUNK, CHUNK), back), row],
        out_specs=[row] * 6, out_shape=[shape] * 6,
        scratch_shapes=[pltpu.VMEM((RWKV_DIM, RWKV_HEAD), f32)],
        compiler_params=_params(("arbitrary",)),
    )(r, lw, k, v, a, b, states, inverses, dy)


def _loss_head(act, w_down, h1, target, g_final, *, name):
    lp = h1.shape[0]
    per_tile = 3
    tm = per_tile * BLOCK
    last_block = (lp - FRONT) // BLOCK - 1

    def body(a_ref, w_ref, h_ref, t0_ref, t1_ref, t2_ref, g_ref, loss_ref, dh_ref, dg_ref):
        i = pl.program_id(0)
        target_rows = jnp.concatenate([t0_ref[...], t1_ref[...], t2_ref[...]], axis=0)
        real = i * tm + lax.broadcasted_iota(jnp.int32, (tm, 1), 0) >= FRONT

        def tile_loss(hv, gv):
            err = _rms(hv, gv) - target_rows
            return 0.5 * jnp.sum(jnp.where(real, jnp.mean(err * err, axis=-1, keepdims=True), 0.0))

        h2 = _dot(a_ref[...], w_ref[...], "nn") + h_ref[...]
        loss, (dh, dg) = jax.value_and_grad(tile_loss, argnums=(0, 1))(h2, g_ref[...])

        @pl.when(i == 0)
        def _():
            loss_ref[...] = jnp.zeros_like(loss_ref)
            dg_ref[...] = jnp.zeros_like(dg_ref)

        loss_ref[...] += jnp.full(loss_ref.shape, loss, f32)
        dg_ref[...] += dg
        dh_ref[...] = dh

    def target_block(j):
        return pl.BlockSpec((BLOCK, D_MODEL),
                            lambda i: (jnp.clip(per_tile * i + j - FRONT // BLOCK, 0, last_block), 0))

    return pl.pallas_call(
        body, name=name, grid=(lp // tm,),
        in_specs=[pl.BlockSpec((tm, act.shape[1]), lambda i: (i, 0)), _full(w_down.shape),
                  pl.BlockSpec((tm, D_MODEL), lambda i: (i, 0)), target_block(0), target_block(1), target_block(2),
                  _full(g_final.shape)],
        out_specs=[_full((8, 128)), pl.BlockSpec((tm, D_MODEL), lambda i: (i, 0)), _full(g_final.shape)],
        out_shape=[jax.ShapeDtypeStruct((8, 128), f32), jax.ShapeDtypeStruct((lp, D_MODEL), f32),
                   jax.ShapeDtypeStruct(g_final.shape, f32)],
        compiler_params=_params(("arbitrary",)),
    )(act, w_down, h1, target, target, target, g_final)


def _embed_norm(x, meta, g, *, name):
    seq = x.shape[0]
    lp = seq + FRONT
    per_tile = 3
    tm = per_tile * BLOCK
    last_block = seq // BLOCK - 1

    def body(x0_ref, x1_ref, x2_ref, meta_ref, g_ref, h_ref, u_ref):
        front = jnp.concatenate([jnp.zeros((PAD, D_MODEL), f32), meta_ref[...]], axis=0)
        first = jnp.where(pl.program_id(0) == 0, front, x0_ref[...])
        h = jnp.concatenate([first, x1_ref[...], x2_ref[...]], axis=0)
        h_ref[...] = h
        u_ref[...] = _rms(h, g_ref[...]).astype(u_ref.dtype)

    def x_block(j):
        return pl.BlockSpec((BLOCK, D_MODEL),
                            lambda i: (jnp.clip(per_tile * i + j - FRONT // BLOCK, 0, last_block), 0))

    tile = pl.BlockSpec((tm, D_MODEL), lambda i: (i, 0))
    return pl.pallas_call(
        body, name=name, grid=(lp // tm,),
        in_specs=[x_block(0), x_block(1), x_block(2), _full(meta.shape), _full(g.shape)],
        out_specs=[tile, tile],
        out_shape=[jax.ShapeDtypeStruct((lp, D_MODEL), f32), jax.ShapeDtypeStruct((lp, D_MODEL), bf16)],
        compiler_params=_params(("parallel",)),
    )(x, x, x, meta, g)


def _input_norm_bwd(h0, g, du, dh1, *, name):
    lp = h0.shape[0]
    blocks = (lp - FRONT) // FRONT
    per_tile = max(n for n in (4, 3, 2, 1) if blocks % n == 0)
    ins = (h0, du, dh1)

    def body(*refs):
        tiles = [refs[k * per_tile:(k + 1) * per_tile] for k in range(len(ins))]
        front_refs = refs[len(ins) * per_tile:len(ins) * (per_tile + 1)]
        g_ref, dx_ref, front_ref, dg_ref = refs[len(ins) * (per_tile + 1):]

        def cotangents(h_ref, du_ref, dh1_ref):
            _, vjp = jax.vjp(lambda hv, gv: (_rms(hv, gv), hv), h_ref[...], g_ref[...])
            return vjp((du_ref[...].astype(f32), dh1_ref[...]))

        @pl.when(pl.program_id(0) == 0)
        def _():
            front_ref[...], dg_ref[...] = cotangents(*front_refs)

        dg = jnp.zeros(dg_ref.shape, f32)
        for j in range(per_tile):
            dh, dg_j = cotangents(*(t[j] for t in tiles))
            dx_ref[j * FRONT:(j + 1) * FRONT, :] = dh
            dg = dg + dg_j
        dg_ref[...] += dg

    def block(j):
        return pl.BlockSpec((FRONT, D_MODEL), lambda i: (per_tile * i + j + 1, 0))

    first = pl.BlockSpec((FRONT, D_MODEL), lambda i: (0, 0))
    return pl.pallas_call(
        body, name=name, grid=(blocks // per_tile,),
        in_specs=[block(j) for _ in ins for j in range(per_tile)] + [first] * len(ins) + [_full(g.shape)],
        out_specs=[pl.BlockSpec((per_tile * FRONT, D_MODEL), lambda i: (i, 0)), _full((FRONT, D_MODEL)),
                   _full(g.shape)],
        out_shape=[jax.ShapeDtypeStruct((lp - FRONT, D_MODEL), f32), jax.ShapeDtypeStruct((FRONT, D_MODEL), f32),
                   jax.ShapeDtypeStruct(g.shape, f32)],
        compiler_params=_params(("arbitrary",)),
    )(*(a for a in ins for _ in range(per_tile)), *ins, g)


def _local_step(x, target, meta, p, early_weights=None, late_weights=None, emit=None):
    emit = emit or (lambda group, grads: 0.0)
    seq = x.shape[0]
    lp = seq + FRONT
    cos_t, sin_t, swap = _rope_tables(lp)
    hsum = _head_sum_matrix(RWKV_DIM, RWKV_HEAD)
    hmean = hsum / RWKV_HEAD
    post_params = [p["ln_w"], p["ln_b"], p["r_k"], hmean]

    h0, u = _embed_norm(x, meta, p["norm_mix_g"], name="norm_mix")
    if early_weights is not None:
        p = {**p, **early_weights(u)}
    prep_params = [p["w0"], p["w2"], p["a0"], p["a2"], p["g2"], p["k_k"], p["k_a"], hsum]
    in_widths = [ATTN_PROJ, RKV_W, LORA_W, 2 * D_MODEL]
    q, k, v, p_rkv, p_lora, gates = _proj_in(u, p["w_in_lr"], p["b_in"], in_widths,
                                             (cos_t, sin_t, swap), name="proj_in", zero_rows_below=PAD)
    y_attn = _attention(q, k, v, p["sinks"], name="attention")

    mix_rkv, mix_lora = p["mix"][:, :RKV_W], p["mix"][:, RKV_W:]
    r_, lw_, k_, v_, a_, b_, g_ = _mixer_inputs([p_rkv, p_lora], [mix_rkv, mix_lora], prep_params,
                                                name="mixer_inputs")
    y_scan, states, inverses = _scan(r_, lw_, k_, v_, a_, b_, name="wkv_scan")
    (y_rwkv,) = _rowwise(_rwkv_post, [y_scan, r_, k_, v_, g_], post_params, [(RWKV_DIM, bf16)], name="rwkv_post")

    if late_weights is not None:
        p = {**p, **late_weights(y_rwkv)}
    br_a, br_r, merged = _branch_merge(y_attn, y_rwkv, p["w_br_attn_t"], p["w_br_rwkv_t"], gates, name="branch_merge")
    h1, f = _residual_norm(merged, p["w_o"], h0, p["norm_ffn_g"], name="out_proj")
    gate, up, act = _ffn_in(f, p["w_gate_t"], p["w_up_t"], name="ffn_in")

    loss8, dh2, d_final_g = _loss_head(act, p["w_down"], h1, target, p["norm_final_g"], name="loss_head")
    dgate, dup, dh1, d_ffn_g = _ffn_bwd(dh2, p["w_down"], gate, up, p["w_gate_t"], p["w_up_t"], h1, p["norm_ffn_g"],
                                        name="ffn_bwd")
    d_w_down = _mm_tn(act, dh2, name="dw_down")
    d_w_gate_t = _mm_tn(dgate, f, name="dw_gate")
    d_w_up_t = _mm_tn(dup, f, name="dw_up")
    zero = emit("ffn", dict(w_down=d_w_down, w_gate_t=d_w_gate_t, w_up_t=d_w_up_t))
    dgates, dbr_a, dbr_r, dy_attn, dy_rwkv = _branch_merge_bwd(
        dh1, p["w_o"], gates, br_a, br_r, p["w_br_attn_t"], p["w_br_rwkv_t"], name="branch_merge_bwd")
    d_w_o = _mm_tn(merged, dh1, name="dw_o")
    d_w_br_attn_t = _mm_tn(dbr_a, y_attn, name="dw_br_attn")
    d_w_br_rwkv_t = _mm_tn(dbr_r, y_rwkv, name="dw_br_rwkv")
    zero = zero + emit("branch", dict(w_o=d_w_o, w_br_attn_t=d_w_br_attn_t, w_br_rwkv_t=d_w_br_rwkv_t))

    post_params = [p["ln_w"] + zero, p["ln_b"], p["r_k"], hmean]
    res = _rowwise_bwd(_rwkv_post, [y_scan, r_, k_, v_, g_], post_params, [[dy_rwkv]], name="rwkv_post_bwd",
                       diff_rows=[True] * 5, diff_params=[True, True, True, False])
    dy_scan, dr_p, dk_p, dv_p, dg_p, d_ln_w, d_ln_b, d_r_k = res
    dr_s, dlw_s, dk_s, dv_s, da_s, db_s = _scan_bwd(r_, lw_, k_, v_, a_, b_, states, inverses, dy_scan,
                                                    name="wkv_scan_bwd")
    res = _mixer_inputs_bwd([p_rkv, p_lora], [mix_rkv, mix_lora], prep_params,
                            [[dr_s, dr_p], [dlw_s], [dk_s, dk_p], [dv_s, dv_p], [da_s], [db_s], [dg_p]],
                            name="mixer_inputs_bwd")
    dp_rkv, dp_lora, d_mix_rkv, d_mix_lora, d_w0, d_w2, d_a0, d_a2, d_g2, d_k_k, d_k_a = res

    dq, dk, dv, dkm, dvm, d_sinks = _attention_bwd(q, k, v, p["sinks"], y_attn, dy_attn, name="attention_bwd")
    (dqkv,) = _rowwise(_attn_prep_transposed, [dq, dk, dv, cos_t, sin_t], [swap, dkm, dvm], [(ATTN_PROJ, bf16)],
                       name="attn_prep_bwd")

    in_rows, (at_qkv, at_rkv, at_lora, at_gates) = p["w_in_lr"].shape[1], [off for off, _ in _pieces(in_widths)]
    d_w_in_t, db_gates = _mm_tn(dgates, u, name="dw_gates", colsum=True, into=(in_rows, at_gates, None))
    d_w_in_t, db_rkv = _mm_tn(dp_rkv, u, name="dw_rkv", colsum=True, into=(in_rows, at_rkv, d_w_in_t))
    d_w_in_t, db_lora = _mm_tn(dp_lora, u, name="dw_lora", colsum=True, into=(in_rows, at_lora, d_w_in_t))
    d_w_in_t, db_qkv = _mm_tn(dqkv, u, name="dw_qkv", colsum=True, into=(in_rows, at_qkv, d_w_in_t))
    zero = emit("input", dict(w_in_t=d_w_in_t, g2=d_g2, w2=d_w2, a2=d_a2))
    du = _proj_in_bwd([dqkv, dp_rkv, dp_lora, dgates], p["w_in_lr"], name="d_u")
    dx, d_front, d_mix_g = _input_norm_bwd(h0, p["norm_mix_g"] + zero, du, dh1, name="norm_mix_bwd")

    grads = dict(
        w_in_t=d_w_in_t,
        b_in=jnp.concatenate([db_qkv, db_rkv, db_lora, db_gates], axis=1),
        mix=jnp.concatenate([d_mix_rkv, d_mix_lora], axis=1),
        norm_mix_g=d_mix_g, sinks=d_sinks, w0=d_w0, w2=d_w2, a0=d_a0, a2=d_a2, g2=d_g2, k_k=d_k_k, k_a=d_k_a,
        r_k=d_r_k, ln_w=d_ln_w, ln_b=d_ln_b, w_br_attn_t=d_w_br_attn_t, w_br_rwkv_t=d_w_br_rwkv_t, w_o=d_w_o,
        norm_ffn_g=d_ffn_g, w_gate_t=d_w_gate_t, w_up_t=d_w_up_t, w_down=d_w_down, norm_final_g=d_final_g,
        meta=d_front[PAD:],
    )
    return loss8[0, 0], dx, grads


def _position():
    return lax.axis_index("x"), lax.axis_index("y"), lax.axis_index("c")


def _other_chips(x, y):
    return [(1 - x, y), (x, 1 - y), (1 - x, 1 - y)]


_HBM = pl.BlockSpec(memory_space=pltpu.HBM)
_SEM = pl.BlockSpec(memory_space=pltpu.SEMAPHORE)
_EFFECT = pltpu.SideEffectType.DATAFLOW_SIDE_EFFECTING


def _landing_zone(src, kind):
    shape = {"whole": (N_CHIPS,) + src.shape, "half": (2, N_CHIPS, src.shape[0], src.shape[1] // 2),
             "slab": (3,) + src.shape[1:], "sibling": src.shape}[kind]
    return lax.empty(shape, src.dtype)


def _copies_per_source(kind):
    return 1 if kind == "sibling" else 3


def _chip_copies(src_refs, land_refs, send_sems, recv_sems, kind):
    x, y, c = _position()
    if kind == "sibling":
        return [pltpu.make_async_remote_copy(
            src_ref=src, dst_ref=land, send_sem=send_sems.at[a], recv_sem=recv_sems.at[a],
            device_id=(x, y, 1 - c), device_id_type=MESH) for a, (src, land) in enumerate(zip(src_refs, land_refs))]
    copies = []
    for a, (src, land) in enumerate(zip(src_refs, land_refs)):
        for j, (px, py) in enumerate(_other_chips(x, y)):
            if kind == "whole":
                src_ref, dst_ref = src, land.at[2 * x + y]
            elif kind == "half":
                half = src.shape[1] // 2
                src_ref, dst_ref = src.at[:, pl.ds(pl.multiple_of(c * half, half), half)], land.at[c, 2 * x + y]
            else:
                src_ref, dst_ref = src.at[2 * px + py], land.at[j]
            copies.append(pltpu.make_async_remote_copy(
                src_ref=src_ref, dst_ref=dst_ref, send_sem=send_sems.at[3 * a + j], recv_sem=recv_sems.at[3 * a + j],
                device_id=(px, py, c), device_id_type=MESH))
    return copies


def _exchange_start(srcs, *, kind, name):
    n = len(srcs)
    lands = [_landing_zone(s, kind) for s in srcs]

    def body(*refs):
        for cp in _chip_copies(refs[:n], refs[n:2 * n], refs[2 * n], refs[2 * n + 1], kind):
            cp.start()
        refs[-1][...] = jnp.zeros_like(refs[-1])

    res = pl.pallas_call(
        body, name=name,
        out_shape=(pltpu.SemaphoreType.DMA((_copies_per_source(kind) * n,)),
                   pltpu.SemaphoreType.DMA((_copies_per_source(kind) * n,)),
                   *[pltpu.HBM(a.shape, a.dtype) for a in srcs + lands], jax.ShapeDtypeStruct((8, 128), f32)),
        in_specs=[_HBM] * (2 * n),
        out_specs=(_SEM, _SEM, *[_HBM] * (2 * n), pl.BlockSpec(memory_space=pltpu.VMEM)),
        input_output_aliases={i: 2 + i for i in range(2 * n)},
        compiler_params=pltpu.CompilerParams(has_side_effects=_EFFECT),
    )(*[pltpu.with_memory_space_constraint(a, pltpu.HBM) for a in srcs + lands])
    return res[0], res[1], list(res[2:2 + n]), list(res[2 + n:2 + 2 * n]), res[-1]


def _exchange_wait(handle, after, *, kind, name):
    send_sems, recv_sems, srcs, lands, _ = handle
    n = len(srcs)

    def body(*refs):
        for cp in _chip_copies(refs[:n], refs[n:2 * n], refs[2 * n], refs[2 * n + 1], kind):
            cp.wait_send()
            cp.wait_recv()

    res = pl.pallas_call(
        body, name=name,
        out_shape=tuple(pltpu.HBM(a.shape, a.dtype) for a in srcs + lands),
        in_specs=[_HBM] * (2 * n) + [_SEM, _SEM, pl.BlockSpec(memory_space=pl.ANY)],
        out_specs=tuple([_HBM] * (2 * n)),
        input_output_aliases={i: i for i in range(2 * n)},
        compiler_params=pltpu.CompilerParams(has_side_effects=_EFFECT),
    )(*srcs, *lands, send_sems, recv_sems, after)
    return list(res[:n]), list(res[n:])


def _sum_own_and_received(g, recv, *, name):
    _, r, w = g.shape
    tm = _tile(r)
    if g.dtype == bf16 and tm % 16:
        tm = r
    x, y, _ = _position()
    me = jnp.reshape(2 * x + y, (1,)).astype(jnp.int32)

    def body(me_ref, g_ref, r_ref, o_ref):
        o_ref[...] = (g_ref[0].astype(f32) + r_ref[0].astype(f32)) + (r_ref[1].astype(f32) + r_ref[2].astype(f32))

    return pl.pallas_call(
        body, name=name,
        grid_spec=pltpu.PrefetchScalarGridSpec(
            num_scalar_prefetch=1, grid=(r // tm,),
            in_specs=[pl.BlockSpec((1, tm, w), lambda i, me_ref: (me_ref[0], i, 0)),
                      pl.BlockSpec((3, tm, w), lambda i, me_ref: (0, i, 0))],
            out_specs=pl.BlockSpec((tm, w), lambda i, me_ref: (i, 0))),
        out_shape=jax.ShapeDtypeStruct((r, w), f32),
        compiler_params=_params(("parallel",)),
    )(me, g, recv)


def _swap_cores(arrs, *, name):
    n = len(arrs)

    def body(*refs):
        x, y, c = _position()
        copies = [pltpu.make_async_remote_copy(
            src_ref=refs[i], dst_ref=refs[n + i], send_sem=refs[2 * n].at[i], recv_sem=refs[2 * n + 1].at[i],
            device_id=(x, y, 1 - c), device_id_type=MESH) for i in range(n)]
        for cp in copies:
            cp.start()
        for cp in copies:
            cp.wait_recv()
        for cp in copies:
            cp.wait_send()

    return pl.pallas_call(
        body, name=name,
        in_specs=[pl.BlockSpec(memory_space=pl.ANY)] * n,
        out_specs=[pl.BlockSpec(memory_space=pl.ANY)] * n,
        out_shape=[jax.ShapeDtypeStruct(a.shape, a.dtype) for a in arrs],
        scratch_shapes=[pltpu.SemaphoreType.DMA((n,)), pltpu.SemaphoreType.DMA((n,))],
    )(*arrs)


def _swap_halves(zone, *, name):
    def body(z_ref, o_ref, send_sems, recv_sems):
        x, y, c = _position()
        mine = [pltpu.make_async_remote_copy(
            src_ref=o_ref.at[c, 2 * px + py], dst_ref=o_ref.at[c, 2 * px + py], send_sem=send_sems.at[j],
            recv_sem=recv_sems.at[j], device_id=(x, y, 1 - c), device_id_type=MESH)
            for j, (px, py) in enumerate(_other_chips(x, y))]
        for cp in mine:
            cp.start()
        for j, (px, py) in enumerate(_other_chips(x, y)):
            pltpu.make_async_remote_copy(
                src_ref=o_ref.at[c, 2 * px + py], dst_ref=o_ref.at[1 - c, 2 * px + py], send_sem=send_sems.at[j],
                recv_sem=recv_sems.at[j], device_id=(x, y, 1 - c), device_id_type=MESH).wait_recv()
        for cp in mine:
            cp.wait_send()

    return pl.pallas_call(
        body, name=name,
        in_specs=[pl.BlockSpec(memory_space=pl.ANY)], out_specs=pl.BlockSpec(memory_space=pl.ANY),
        out_shape=jax.ShapeDtypeStruct(zone.shape, zone.dtype), input_output_aliases={0: 0},
        scratch_shapes=[pltpu.SemaphoreType.DMA((3,)), pltpu.SemaphoreType.DMA((3,))],
    )(zone)


def _all_reduce_small(a, after, *, name):
    rows, w = a.shape

    def body(a_ref, after_ref, o_ref, buf, send_sems, recv_sems):
        x, y, c = _position()
        me = 4 * x + 2 * y + c
        buf[0] = a_ref[...]
        sends = []
        for rel in range(1, N_DEV):
            peer = ((1 - x) if rel & 4 else x, (1 - y) if rel & 2 else y, (1 - c) if rel & 1 else c)
            cp = pltpu.make_async_remote_copy(
                src_ref=a_ref, dst_ref=buf.at[rel], send_sem=send_sems.at[rel - 1], recv_sem=recv_sems.at[rel - 1],
                device_id=peer, device_id_type=MESH)
            cp.start()
            sends.append(cp)
        for cp in sends:
            cp.wait_recv()
        for cp in sends:
            cp.wait_send()
        acc = buf[jnp.bitwise_xor(me, 0)]
        for d in range(1, N_DEV):
            acc = acc + buf[jnp.bitwise_xor(me, d)]
        o_ref[...] = acc

    return pl.pallas_call(
        body, name=name,
        in_specs=[pl.BlockSpec(memory_space=pltpu.VMEM), pl.BlockSpec(memory_space=pl.ANY)],
        out_specs=pl.BlockSpec(memory_space=pltpu.VMEM),
        out_shape=jax.ShapeDtypeStruct((rows, w), f32),
        scratch_shapes=[pltpu.VMEM((N_DEV, rows, w), f32), pltpu.SemaphoreType.DMA((N_DEV - 1,)),
                        pltpu.SemaphoreType.DMA((N_DEV - 1,))],
    )(a, after)


def _adam_math(w, g, m, v):
    nm = ADAM_B1 * m + (1.0 - ADAM_B1) * g
    nv = ADAM_B2 * v + (1.0 - ADAM_B2) * (g * g)
    m_hat = nm / (1.0 - ADAM_B1 ** ADAM_STEP)
    v_hat = nv / (1.0 - ADAM_B2 ** ADAM_STEP)
    return -ADAM_LR * (m_hat / (jnp.sqrt(v_hat) + ADAM_EPS) + ADAM_WD * w), nm, nv


def _adamw(w, g_parts, m, v, *, name, transposed=False):
    rows, cols = w.shape
    if transposed:
        tm = 256 if rows % 256 == 0 else rows
        g_spec = pl.BlockSpec((cols, tm), lambda i: (0, i))
    else:
        tm = _tile(rows, 256)
        g_spec = pl.BlockSpec((tm, cols), lambda i: (i, 0))
    n = len(g_parts)

    def body(*refs):
        w_ref, m_ref, v_ref = refs[0], refs[1 + n], refs[2 + n]
        g_ref, d_ref, nm_ref, nv_ref = refs[3 + n:]
        gv = refs[1][...]
        for part in refs[2:1 + n]:
            gv = gv + part[...]
        if transposed:
            gv = gv.T
        g_ref[...] = gv
        d_ref[...], nm_ref[...], nv_ref[...] = _adam_math(w_ref[...], gv, m_ref[...], v_ref[...])

    spec = pl.BlockSpec((tm, cols), lambda i: (i, 0))
    shape = jax.ShapeDtypeStruct((rows, cols), f32)
    return pl.pallas_call(
        body, name=name, grid=(rows // tm,), in_specs=[spec] + [g_spec] * n + [spec] * 2,
        out_specs=[spec] * 4, out_shape=[shape] * 4,
        compiler_params=_params(("parallel",)),
    )(w, *g_parts, m, v)


def _pad_rows(a, rows):
    return jnp.concatenate([a, jnp.zeros((rows - a.shape[0], a.shape[1]), a.dtype)], axis=0) if rows > a.shape[0] else a


_SMALL = (("norm_mix_g", D_MODEL), ("b_in", D_IN), ("sinks", Q_HEADS), ("mix", RWKV_PROJ), ("w0", RWKV_DIM),
          ("a0", RWKV_DIM), ("k_k", RWKV_DIM), ("k_a", RWKV_DIM), ("r_k", RWKV_DIM), ("ln_w", RWKV_DIM),
          ("ln_b", RWKV_DIM), ("norm_ffn_g", D_MODEL), ("norm_final_g", D_MODEL))


LANES = 128


def _small_layout():
    out, off = {}, 0
    for n, size in _SMALL + (("loss", 1),):
        pieces, col = [], 0
        while col < size:
            row, lane = divmod(off + col, PACK_W)
            width = min(size - col, PACK_W - lane)
            pieces.append((row, lane, width, col))
            col += width
        out[n] = pieces
        off += -(-size // LANES) * LANES
    return out, -(-off // PACK_W)


def _pack_small(d, loss):
    layout, rows = _small_layout()
    parts, used = [], 0
    for n, size in _SMALL + (("loss", 1),):
        item = loss if n == "loss" else d[n]
        fill = -size % LANES
        parts += [item.reshape(-1).astype(f32), jnp.zeros((fill,), f32)]
        used += size + fill
    parts.append(jnp.zeros((rows * PACK_W - used,), f32))
    return jnp.concatenate(parts).reshape(rows, PACK_W)


def _adamw_small(packed, first_row, ws, ms, vs, meta, *, name):
    layout, _ = _small_layout()
    names = [n for n, _ in _SMALL]
    k = len(names)

    def body(*refs):
        packed_ref = refs[0]
        w_refs, m_refs, v_refs = refs[1:1 + k], refs[1 + k:1 + 2 * k], refs[1 + 2 * k:1 + 3 * k]
        meta_refs = refs[1 + 3 * k:5 + 3 * k]
        outs = refs[5 + 3 * k:]
        for idx, n in enumerate(names):
            for row, lane, width, col in layout[n]:
                gv = packed_ref[first_row + row:first_row + row + 1, lane:lane + width]
                at = (slice(None), slice(col, col + width))
                new = _adam_math(w_refs[idx][at], gv, m_refs[idx][at], v_refs[idx][at])
                for o_ref, val in zip(outs[4 * idx:4 * idx + 4], (gv,) + new):
                    o_ref[at] = val
        for o_ref, val in zip(outs[4 * k:], _adam_math(*(r[...] for r in meta_refs))):
            o_ref[...] = val

    ins = [packed] + [d[n] for d in (ws, ms, vs) for n in names] + list(meta)
    shapes = [jax.ShapeDtypeStruct(ws[n].shape, f32) for n in names for _ in range(4)]
    shapes += [jax.ShapeDtypeStruct(meta[0].shape, f32)] * 3
    res = pl.pallas_call(
        body, name=name, grid=(1,), in_specs=[_full(a.shape) for a in ins],
        out_specs=[_full(s.shape) for s in shapes], out_shape=shapes,
        compiler_params=_params(("arbitrary",)),
    )(*ins)
    return {n: res[4 * i:4 * i + 4] for i, n in enumerate(names)}, res[4 * k:]


def kernel(x, meta_tokens, norm_mix_g, w_in, b_in, attn_sinks, rwkv_mix, rwkv_w0, rwkv_w2, rwkv_a0, rwkv_a2, rwkv_g2, rwkv_k_k, rwkv_k_a, rwkv_r_k, rwkv_ln_w, rwkv_ln_b, w_br_attn, w_br_rwkv, w_o, norm_ffn_g, w_ffn_gate, w_ffn_up, w_ffn_down, norm_final_g, loss_target, m_meta_tokens, m_norm_mix_g, m_w_in, m_b_in, m_attn_sinks, m_rwkv_mix, m_rwkv_w0, m_rwkv_w2, m_rwkv_a0, m_rwkv_a2, m_rwkv_g2, m_rwkv_k_k, m_rwkv_k_a, m_rwkv_r_k, m_rwkv_ln_w, m_rwkv_ln_b, m_w_br_attn, m_w_br_rwkv, m_w_o, m_norm_ffn_g, m_w_ffn_gate, m_w_ffn_up, m_w_ffn_down, m_norm_final_g, v_meta_tokens, v_norm_mix_g, v_w_in, v_b_in, v_attn_sinks, v_rwkv_mix, v_rwkv_w0, v_rwkv_w2, v_rwkv_a0, v_rwkv_a2, v_rwkv_g2, v_rwkv_k_k, v_rwkv_k_a, v_rwkv_r_k, v_rwkv_ln_w, v_rwkv_ln_b, v_w_br_attn, v_w_br_rwkv, v_w_o, v_norm_ffn_g, v_w_ffn_gate, v_w_ffn_up, v_w_ffn_down, v_norm_final_g):
    names = ("meta_tokens", "norm_mix_g", "w_in", "b_in", "attn_sinks", "rwkv_mix", "rwkv_w0", "rwkv_w2", "rwkv_a0",
             "rwkv_a2", "rwkv_g2", "rwkv_k_k", "rwkv_k_a", "rwkv_r_k", "rwkv_ln_w", "rwkv_ln_b", "w_br_attn",
             "w_br_rwkv", "w_o", "norm_ffn_g", "w_ffn_gate", "w_ffn_up", "w_ffn_down", "norm_final_g")
    w_all = dict(zip(names, (meta_tokens, norm_mix_g, w_in, b_in, attn_sinks, rwkv_mix, rwkv_w0, rwkv_w2, rwkv_a0,
                             rwkv_a2, rwkv_g2, rwkv_k_k, rwkv_k_a, rwkv_r_k, rwkv_ln_w, rwkv_ln_b, w_br_attn,
                             w_br_rwkv, w_o, norm_ffn_g, w_ffn_gate, w_ffn_up, w_ffn_down, norm_final_g)))
    m_all = dict(zip(names, (m_meta_tokens, m_norm_mix_g, m_w_in, m_b_in, m_attn_sinks, m_rwkv_mix, m_rwkv_w0,
                             m_rwkv_w2, m_rwkv_a0, m_rwkv_a2, m_rwkv_g2, m_rwkv_k_k, m_rwkv_k_a, m_rwkv_r_k,
                             m_rwkv_ln_w, m_rwkv_ln_b, m_w_br_attn, m_w_br_rwkv, m_w_o, m_norm_ffn_g, m_w_ffn_gate,
                             m_w_ffn_up, m_w_ffn_down, m_norm_final_g)))
    v_all = dict(zip(names, (v_meta_tokens, v_norm_mix_g, v_w_in, v_b_in, v_attn_sinks, v_rwkv_mix, v_rwkv_w0,
                             v_rwkv_w2, v_rwkv_a0, v_rwkv_a2, v_rwkv_g2, v_rwkv_k_k, v_rwkv_k_a, v_rwkv_r_k,
                             v_rwkv_ln_w, v_rwkv_ln_b, v_w_br_attn, v_w_br_rwkv, v_w_o, v_norm_ffn_g, v_w_ffn_gate,
                             v_w_ffn_up, v_w_ffn_down, v_norm_final_g)))
    cx, cy, _ = _position()
    chip = 2 * cx + cy

    t_of = dict(w_in_t="w_in", w_gate_t="w_ffn_gate", w_up_t="w_ffn_up", w_br_attn_t="w_br_attn",
                w_br_rwkv_t="w_br_rwkv", g2_t="rwkv_g2", w2_t="rwkv_w2", a2_t="rwkv_a2")
    plain_of = dict(w_down="w_ffn_down", w_o="w_o")
    meta_cols = meta_tokens.shape[1]

    def shard(k):
        return (w_all[t_of[k]][0].T if k in t_of else w_all[plain_of[k]][0]).astype(bf16)

    def whole(zone, own):
        return lax.dynamic_update_slice_in_dim(zone, own[None], chip, axis=0).reshape(-1, own.shape[-1])

    tiny = ("g2_t", "w2_t", "a2_t")
    late = ("w_gate_t", "w_up_t", "w_down", "w_o", "w_br_attn_t", "w_br_rwkv_t")
    w_in_own = shard("w_in_t")
    w_in_rows, w_in_cols = w_in_own.shape
    tiny_h = _exchange_start([shard(k) for k in tiny] + [meta_tokens], kind="whole", name="gather_tiny_start")
    w_in_h = _exchange_start([w_in_own + tiny_h[4][0, 0].astype(bf16)], kind="half", name="gather_w_in_start")
    behind = w_in_h[4][0, 0].astype(bf16)
    late_h = _exchange_start([shard(k) + behind for k in late], kind="whole", name="gather_late_start")
    own, zones = _exchange_wait(tiny_h, late_h[4], kind="whole", name="gather_tiny_wait")
    got = {k: whole(z, o) for k, z, o in zip(tiny, zones, own)}
    meta_full = whole(zones[-1], own[-1]).reshape(N_CHIPS, N_META, meta_cols).transpose(1, 0, 2).reshape(N_META, -1)
    p = dict(
        g2=got["g2_t"].T.astype(f32), w2=got["w2_t"].T.astype(f32), a2=got["a2_t"].T.astype(f32),
        b_in=b_in, sinks=attn_sinks, mix=rwkv_mix, w0=rwkv_w0, a0=rwkv_a0, k_k=rwkv_k_k, k_a=rwkv_k_a,
        r_k=rwkv_r_k.reshape(1, RWKV_DIM), ln_w=rwkv_ln_w, ln_b=rwkv_ln_b, norm_mix_g=norm_mix_g,
        norm_ffn_g=norm_ffn_g, norm_final_g=norm_final_g.reshape(1, D_MODEL),
    )

    def early_weights(after):
        own_h, zones_h = _exchange_wait(w_in_h, after, kind="half", name="gather_w_in_wait")
        zone = _swap_halves(zones_h[0], name="swap_w_in_halves")
        own_halves = own_h[0].reshape(w_in_rows, 2, w_in_cols // 2).transpose(1, 0, 2)[:, None]
        zone = lax.dynamic_update_slice(zone, own_halves, (0, chip, 0, 0))
        return dict(w_in_lr=zone.reshape(2, N_CHIPS * w_in_rows, w_in_cols // 2))

    def late_weights(after):
        own_l, zones_l = _exchange_wait(late_h, after, kind="whole", name="gather_late_wait")
        return {k: whole(z, o) for k, z, o in zip(late, zones_l, own_l)}

    started = {}

    def partial_sums(groups, after):
        parts = {}
        for group in groups:
            keys, handle = started[group]
            slabs, lands = _exchange_wait(handle, after, kind="slab", name="scatter_" + group + "_wait")
            parts.update({k: _sum_own_and_received(s, l, name="sum_chips_" + k) for k, s, l in zip(keys, slabs, lands)})
        return parts

    def emit(group, grads_):
        keys = list(grads_)
        slabs = []
        for k in keys:
            a = grads_[k].T if k in ("g2", "w2", "a2") else grads_[k]
            slabs.append(a.reshape(N_CHIPS, a.shape[0] // N_CHIPS, a.shape[1]))
        started[group] = (keys, _exchange_start(slabs, kind="slab", name="scatter_" + group + "_start"))
        zero = started[group][1][4]
        if group == "input":
            started["parts_a"] = partial_sums(("ffn", "branch"), zero)
            started["swap_a"] = _exchange_start(list(started["parts_a"].values()), kind="sibling",
                                                name="swap_cores_a_start")
            zero = started["swap_a"][4]
        return zero[0, 0]

    loss, dx, g = _local_step(x[0], loss_target[0], meta_full, p, early_weights, late_weights, emit)

    grads, delta, new_m, new_v = {}, {}, {}, {}
    in_grad_layout = ("w_in_t", "w_gate_t", "w_up_t")
    weight_of = {**t_of, **plain_of}

    def update(keys, mine, theirs):
        for k, part, other in zip(keys, mine, theirs):
            both = [part, other]
            k = k + "_t" if k in ("g2", "w2", "a2") else k
            n = weight_of[k]
            shape2 = w_all[n].shape[1:]
            w_, m_, v_ = (a.reshape(shape2) for a in (w_all[n], m_all[n], v_all[n]))
            if k in in_grad_layout:
                res = [t.T for t in _adamw(w_.T, both, m_.T, v_.T, name="adamw_" + n)]
            else:
                res = _adamw(w_, both, m_, v_, name="adamw_" + n, transposed=k in t_of)
            grads[n], delta[n], new_m[n], new_v[n] = (t.reshape(w_all[n].shape) for t in res)
        return delta[n]

    done = update(list(started["parts_a"]),
                  *_exchange_wait(started["swap_a"], dx, kind="sibling", name="swap_cores_a_wait"))
    small = _pack_small(g, loss)
    small_rows8 = -(-(small.shape[0] + N_META) // 8) * 8
    reduced = _all_reduce_small(_pad_rows(jnp.concatenate([g["meta"], small], axis=0), small_rows8), done,
                                name="reduce_small")
    parts_b = partial_sums(("input",), reduced)
    update(list(parts_b), list(parts_b.values()), _swap_cores(list(parts_b.values()), name="swap_cores_b"))
    g_meta = lax.dynamic_slice_in_dim(reduced[:N_META], chip * meta_cols, meta_cols, axis=1)
    (loss_row, loss_lane, _, _), = _small_layout()[0]["loss"]
    loss_total = reduced[N_META + loss_row, loss_lane]

    small_of = dict(norm_mix_g="norm_mix_g", b_in="b_in", attn_sinks="sinks", rwkv_mix="mix", rwkv_w0="w0",
                    rwkv_a0="a0", rwkv_k_k="k_k", rwkv_k_a="k_a", rwkv_r_k="r_k", rwkv_ln_w="ln_w",
                    rwkv_ln_b="ln_b", norm_ffn_g="norm_ffn_g", norm_final_g="norm_final_g")
    as_rows = [{k: src[n].reshape(1, -1) for n, k in small_of.items()} for src in (w_all, m_all, v_all)]
    meta_in = (meta_tokens, g_meta, m_meta_tokens, v_meta_tokens)
    small_out, meta_out = _adamw_small(reduced, N_META, *as_rows, meta_in, name="adamw_small")
    grads["meta_tokens"] = g_meta
    delta["meta_tokens"], new_m["meta_tokens"], new_v["meta_tokens"] = meta_out
    for n, k in small_of.items():
        grads[n], delta[n], new_m[n], new_v[n] = (t.reshape(w_all[n].shape) for t in small_out[k])

    return (loss_total, dx.reshape(x.shape), *[grads[n] for n in names], *[delta[n] for n in names],
            *[new_m[n] for n in names], *[new_v[n] for n in names])
```

```python
import math

import jax
import jax.numpy as jnp
import numpy as np
from jax import lax
from jax.experimental import pallas as pl
from jax.experimental.pallas import tpu as pltpu

f32 = jnp.float32
bf16 = jnp.bfloat16

D_MODEL = 1024
N_META = 16
HEAD_DIM = 64
Q_HEADS = 8
KV_HEADS = 2
GROUP = Q_HEADS // KV_HEADS
WINDOW = 128
BLOCK = 128
ROPE_THETA = 500000.0
ROPE_DIM = HEAD_DIM // 4
RWKV_HEADS = 8
RWKV_HEAD = 64
RWKV_DIM = RWKV_HEADS * RWKV_HEAD
DECAY_LORA = 64
AAA_LORA = 64
GATE_LORA = 160
LORA_W = DECAY_LORA + AAA_LORA + GATE_LORA
RWKV_LN_EPS = 64e-5
D_FF = 2816
Q_W = Q_HEADS * HEAD_DIM
KV_W = KV_HEADS * HEAD_DIM
ATTN_PROJ = Q_W + 2 * KV_W
RKV_W = 3 * RWKV_DIM
RWKV_PROJ = RKV_W + LORA_W
D_IN = ATTN_PROJ + RWKV_PROJ + 2 * D_MODEL
RMS_EPS = 1e-6
NEG_INF = -1e30
PAD = BLOCK - N_META
FRONT = PAD + N_META

ADAM_LR = 0.001
ADAM_B1 = 0.9
ADAM_B2 = 0.999
ADAM_EPS = 1e-08
ADAM_WD = 0.01
ADAM_STEP = 10

N_CHIPS = 4
N_DEV = 8
CHUNK = 128
VMEM_LIMIT = 56 * 1024 * 1024
MM_ROWS = 704
PACK_W = 1024
MESH = pl.DeviceIdType.MESH


def _tile(m, pref=384):
    for step in (16, 8):
        for t in range(min(m, pref) // step * step, 0, -step):
            if m % t == 0:
                return t
    return m


def _params(sem=None):
    return pltpu.CompilerParams(dimension_semantics=sem, vmem_limit_bytes=VMEM_LIMIT)


def _full(shape):
    nd = len(shape)
    return pl.BlockSpec(shape, lambda *_: (0,) * nd)


def _dot(a, b, dims="nn"):
    dn = {"nn": (((1,), (0,)), ((), ())), "nt": (((1,), (1,)), ((), ())), "tn": (((0,), (0,)), ((), ()))}[dims]
    return lax.dot_general(a.astype(bf16), b.astype(bf16), dn, preferred_element_type=f32)


def _two_pass(x, m, dims="nn"):
    x_hi = x.astype(bf16)
    x_lo = (x - x_hi.astype(f32)).astype(bf16)
    return _dot(x_hi, m, dims) + _dot(x_lo, m, dims)


@jax.custom_vjp
def _dot_const(x, m):
    return _two_pass(x, m)


def _dot_const_fwd(x, m):
    return _two_pass(x, m), m


def _dot_const_bwd(m, ct):
    return _two_pass(ct, m, "nt"), jnp.zeros_like(m)


_dot_const.defvjp(_dot_const_fwd, _dot_const_bwd)


def _two_pass_left(m, x, dims):
    x_hi = x.astype(bf16)
    x_lo = (x - x_hi.astype(f32)).astype(bf16)
    return _dot(m, x_hi, dims) + _dot(m, x_lo, dims)


@jax.custom_vjp
def _const_dot(m, x):
    return _two_pass_left(m, x, "nn")


def _const_dot_fwd(m, x):
    return _two_pass_left(m, x, "nn"), m


def _const_dot_bwd(m, ct):
    return jnp.zeros_like(m), _two_pass_left(m, ct, "tn")


_const_dot.defvjp(_const_dot_fwd, _const_dot_bwd)


def _mm(a, b, mode, *, name, out_dtype=f32, bias=None, add=None, zero_rows_below=0):
    m, _ = a.shape
    n = b.shape[1] if mode == "nn" else b.shape[0]
    tm = _tile(m, MM_ROWS)
    has_bias, has_add = bias is not None, add is not None

    def body(*refs):
        a_ref, b_ref = refs[0], refs[1]
        o_ref = refs[-1]
        acc = _dot(a_ref[...], b_ref[...], mode)
        k = 2
        if has_bias:
            acc = acc + refs[k][...]
            k += 1
        if zero_rows_below:
            rows = pl.program_id(0) * tm + lax.broadcasted_iota(jnp.int32, acc.shape, 0)
            acc = jnp.where(rows >= zero_rows_below, acc, 0.0)
        if has_add:
            acc = acc + refs[k][...].astype(f32)
        o_ref[...] = acc.astype(out_dtype)

    ins = [a, b]
    in_specs = [pl.BlockSpec((tm, a.shape[1]), lambda i: (i, 0)), _full(b.shape)]
    if has_bias:
        ins.append(bias)
        in_specs.append(_full(bias.shape))
    if has_add:
        ins.append(add)
        in_specs.append(pl.BlockSpec((tm, n), lambda i: (i, 0)))
    return pl.pallas_call(
        body, name=name, grid=(m // tm,), in_specs=in_specs,
        out_specs=pl.BlockSpec((tm, n), lambda i: (i, 0)),
        out_shape=jax.ShapeDtypeStruct((m, n), out_dtype),
        compiler_params=_params(("parallel",)),
    )(*ins)


def _pieces(widths):
    out, off = [], 0
    for w in widths:
        out.append((off, w))
        off += w
    return out


def _proj_in(a, w_lr, bias, widths, rope, *, name, zero_rows_below=0):
    m, kdim = a.shape
    half = kdim // 2
    tm = _tile(m, MM_ROWS)
    cos_t, sin_t, swap = rope
    out_widths = [Q_W, KV_W, KV_W] + list(widths[1:])

    def body(a_ref, w_ref, b_ref, cos_ref, sin_ref, swap_ref, *outs):
        a_l, a_r = a_ref[:, :half], a_ref[:, half:]
        for j, (off, width) in enumerate(_pieces(widths)):
            acc = _dot(a_l, w_ref[0, off:off + width, :], "nt") + _dot(a_r, w_ref[1, off:off + width, :], "nt")
            acc = acc + b_ref[:, off:off + width]
            if zero_rows_below:
                rows = pl.program_id(0) * tm + lax.broadcasted_iota(jnp.int32, acc.shape, 0)
                acc = jnp.where(rows >= zero_rows_below, acc, 0.0)
            if j == 0:
                qkv = acc.astype(bf16).astype(f32)
                for o_ref, val in zip(outs[:3], _attn_prep(qkv, cos_ref[...], sin_ref[...], swap_ref[...])):
                    o_ref[...] = val.astype(o_ref.dtype)
            else:
                outs[2 + j][...] = acc.astype(outs[2 + j].dtype)

    table = pl.BlockSpec((tm, HEAD_DIM), lambda i: (i, 0))
    return pl.pallas_call(
        body, name=name, grid=(m // tm,),
        in_specs=[pl.BlockSpec((tm, kdim), lambda i: (i, 0)), _full(w_lr.shape), _full(bias.shape), table, table,
                  _full(swap.shape)],
        out_specs=[pl.BlockSpec((tm, w), lambda i: (i, 0)) for w in out_widths],
        out_shape=[jax.ShapeDtypeStruct((m, w), bf16) for w in out_widths],
        compiler_params=_params(("parallel",)),
    )(a, w_lr, bias, cos_t, sin_t, swap)


def _proj_in_bwd(d_list, w_lr, *, name):
    m = d_list[0].shape[0]
    half = w_lr.shape[2]
    widths = [d.shape[1] for d in d_list]
    tm = _tile(m, MM_ROWS)

    def body(*refs):
        w_ref, o_ref = refs[-2], refs[-1]
        for side in range(2):
            acc = None
            for (off, width), d_ref in zip(_pieces(widths), refs):
                term = _dot(d_ref[...], w_ref[side, off:off + width, :])
                acc = term if acc is None else acc + term
            o_ref[:, side * half:(side + 1) * half] = acc.astype(o_ref.dtype)

    return pl.pallas_call(
        body, name=name, grid=(m // tm,),
        in_specs=[pl.BlockSpec((tm, w), lambda i: (i, 0)) for w in widths] + [_full(w_lr.shape)],
        out_specs=pl.BlockSpec((tm, 2 * half), lambda i: (i, 0)),
        out_shape=jax.ShapeDtypeStruct((m, 2 * half), bf16),
        compiler_params=_params(("parallel",)),
    )(*d_list, w_lr)


def _residual_norm(a, w, res, g, *, name):
    m, d = res.shape
    tm = _tile(m, MM_ROWS)

    def body(a_ref, w_ref, r_ref, g_ref, h_ref, n_ref):
        h = _dot(a_ref[...], w_ref[...]) + r_ref[...]
        h_ref[...] = h
        n_ref[...] = _rms(h, g_ref[...]).astype(n_ref.dtype)

    tile = pl.BlockSpec((tm, d), lambda i: (i, 0))
    return pl.pallas_call(
        body, name=name, grid=(m // tm,),
        in_specs=[pl.BlockSpec((tm, a.shape[1]), lambda i: (i, 0)), _full(w.shape), tile, _full(g.shape)],
        out_specs=[tile, tile],
        out_shape=[jax.ShapeDtypeStruct((m, d), f32), jax.ShapeDtypeStruct((m, d), bf16)],
        compiler_params=_params(("parallel",)),
    )(a, w, res, g)


def _residual_norm_bwd(d_list, w_list, h, g, dh_out, *, name):
    m, d = h.shape
    k = len(d_list)
    tm = _tile(m)

    def body(*refs):
        h_ref, g_ref, dho_ref, dh_ref, dg_ref = refs[2 * k:]
        dn = _dot(refs[0][...], refs[k][...])
        for i in range(1, k):
            dn = dn + _dot(refs[i][...], refs[k + i][...])
        _, vjp = jax.vjp(lambda hv, gv: (_rms(hv, gv), hv), h_ref[...], g_ref[...])
        dh, dg = vjp((dn, dho_ref[...]))
        dh_ref[...] = dh

        @pl.when(pl.program_id(0) == 0)
        def _():
            dg_ref[...] = jnp.zeros_like(dg_ref)

        dg_ref[...] += dg

    tile = pl.BlockSpec((tm, d), lambda i: (i, 0))
    return pl.pallas_call(
        body, name=name, grid=(m // tm,),
        in_specs=[pl.BlockSpec((tm, a.shape[1]), lambda i: (i, 0)) for a in d_list] + [_full(w.shape) for w in w_list]
        + [tile, _full(g.shape), tile],
        out_specs=[tile, _full(g.shape)],
        out_shape=[jax.ShapeDtypeStruct((m, d), f32), jax.ShapeDtypeStruct(g.shape, f32)],
        compiler_params=_params(("arbitrary",)),
    )(*d_list, *w_list, h, g, dh_out)


def _mm_tn(a, b, *, name, colsum=False, out_dtype=bf16, into=None):
    r, m = a.shape
    n = b.shape[1]
    tr = _tile(r, 1408)
    tmo = m
    for cand in (1408, 1024, 768, 512):
        if m > 1024 and m % cand == 0:
            tmo = cand
            break
    steps = r // tr

    rows, offset, target = into or (m, 0, None)

    def body(a_ref, b_ref, *rest):
        o_ref, rest = (rest[0], rest[1:]) if target is None else (rest[1], rest[2:])
        acc = rest[-1]
        i = pl.program_id(1)

        @pl.when(i == 0)
        def _():
            acc[...] = jnp.zeros_like(acc)
            if colsum:
                rest[0][...] = jnp.zeros_like(rest[0])

        acc[...] += _dot(a_ref[...], b_ref[...], "tn")
        if colsum:
            rest[0][...] += jnp.sum(a_ref[...].astype(f32), axis=0, keepdims=True)

        @pl.when(i == steps - 1)
        def _():
            o_ref[...] = acc[...].astype(out_dtype)

    out_shape = [jax.ShapeDtypeStruct((rows, n), out_dtype)]
    if offset % tmo == 0:
        out_specs = [pl.BlockSpec((tmo, n), lambda j, i: (offset // tmo + j, 0))]
    else:
        out_specs = [pl.BlockSpec((pl.Element(tmo), pl.Element(n)), lambda j, i: (pl.multiple_of(offset + j * tmo, math.gcd(offset, tmo)), 0))]
    if colsum:
        out_shape.append(jax.ShapeDtypeStruct((1, m), f32))
        out_specs.append(pl.BlockSpec((1, tmo), lambda j, i: (0, j)))
    in_specs = [pl.BlockSpec((tr, tmo), lambda j, i: (i, j)), pl.BlockSpec((tr, n), lambda j, i: (i, 0))]
    res = pl.pallas_call(
        body, name=name, grid=(m // tmo, steps),
        in_specs=in_specs + ([] if target is None else [pl.BlockSpec(memory_space=pl.ANY)]),
        out_specs=out_specs, out_shape=out_shape,
        scratch_shapes=[pltpu.VMEM((tmo, n), f32)],
        input_output_aliases={} if target is None else {2: 0},
        compiler_params=_params(("parallel", "arbitrary")),
    )(a, b, *([] if target is None else [target]))
    return res if colsum else res[0]


def _rowwise(fn, rows, params, outs, *, name, tm=None):
    m = rows[0].shape[0]
    tm = tm or _tile(m, MM_ROWS)
    nr, npar = len(rows), len(params)

    def body(*refs):
        vals = [r[...] for r in refs[:nr + npar]]
        res = fn(*vals)
        for o_ref, v in zip(refs[nr + npar:], res):
            o_ref[...] = v.astype(o_ref.dtype)

    return pl.pallas_call(
        body, name=name, grid=(m // tm,),
        in_specs=[pl.BlockSpec((tm, r.shape[1]), lambda i: (i, 0)) for r in rows] + [_full(p.shape) for p in params],
        out_specs=[pl.BlockSpec((tm, w), lambda i: (i, 0)) for w, _ in outs],
        out_shape=[jax.ShapeDtypeStruct((m, w), dt) for w, dt in outs],
        compiler_params=_params(("parallel",)),
    )(*rows, *params)


def _rowwise_bwd(fn, rows, params, cts, *, name, diff_rows, diff_params, tm=None, zero_rows_below=0, out_dtypes=None):
    m = rows[0].shape[0]
    tm = tm or _tile(m)
    nr, npar = len(rows), len(params)
    d_idx = [i for i in range(nr) if diff_rows[i]]
    p_idx = [i for i in range(npar) if diff_params[i]]
    out_dtypes = out_dtypes or [f32] * len(d_idx)
    flat_cts = [c for group in cts for c in group]
    n_ct = len(flat_cts)

    def body(*refs):
        vals = [r[...] for r in refs[:nr + npar]]
        ct_refs = refs[nr + npar:nr + npar + n_ct]
        out_refs = refs[nr + npar + n_ct:]
        ct_vals, k = [], 0
        for group in cts:
            acc = ct_refs[k][...].astype(f32)
            for extra in range(1, len(group)):
                acc = acc + ct_refs[k + extra][...].astype(f32)
            k += len(group)
            if zero_rows_below:
                rr = pl.program_id(0) * tm + lax.broadcasted_iota(jnp.int32, acc.shape, 0)
                acc = jnp.where(rr >= zero_rows_below, acc, 0.0)
            ct_vals.append(acc)

        def g(*dargs):
            full = list(vals)
            for pos, i in enumerate(d_idx):
                full[i] = dargs[pos]
            for pos, i in enumerate(p_idx):
                full[nr + i] = dargs[len(d_idx) + pos]
            return tuple(fn(*full))

        _, vjp = jax.vjp(g, *[vals[i].astype(f32) for i in d_idx], *[vals[nr + i] for i in p_idx])
        grads = vjp(tuple(ct_vals))
        for pos in range(len(d_idx)):
            out_refs[pos][...] = grads[pos].astype(out_refs[pos].dtype)
        first = pl.program_id(0) == 0
        for pos in range(len(p_idx)):
            o_ref = out_refs[len(d_idx) + pos]

            @pl.when(first)
            def _(o_ref=o_ref):
                o_ref[...] = jnp.zeros_like(o_ref)

            o_ref[...] += grads[len(d_idx) + pos]

    return pl.pallas_call(
        body, name=name, grid=(m // tm,),
        in_specs=[pl.BlockSpec((tm, r.shape[1]), lambda i: (i, 0)) for r in rows] + [_full(p.shape) for p in params]
        + [pl.BlockSpec((tm, c.shape[1]), lambda i: (i, 0)) for c in flat_cts],
        out_specs=[pl.BlockSpec((tm, rows[i].shape[1]), lambda i_: (i_, 0)) for i in d_idx]
        + [_full(params[i].shape) for i in p_idx],
        out_shape=[jax.ShapeDtypeStruct(rows[i].shape, dt) for i, dt in zip(d_idx, out_dtypes)]
        + [jax.ShapeDtypeStruct(params[i].shape, f32) for i in p_idx],
        compiler_params=_params(("arbitrary",)),
    )(*rows, *params, *flat_cts)


def _rms(x, g):
    return x * lax.rsqrt(jnp.mean(x * x, axis=-1, keepdims=True) + RMS_EPS) * g


def _head_sum_matrix(width, head):
    idx = jnp.arange(width) // head
    return (idx[:, None] == idx[None, :]).astype(f32)


def _rope_tables(lp):
    half = ROPE_DIM // 2
    pos = (np.arange(lp) - PAD).astype(np.float32)
    inv_freq = np.power(np.float32(ROPE_THETA), -np.arange(half, dtype=np.float32) * np.float32(2.0 / ROPE_DIM))
    ang = pos[:, None] * inv_freq[None, :].astype(np.float32)
    cos, sin = np.cos(ang), np.sin(ang)
    ones = np.ones((lp, HEAD_DIM - ROPE_DIM), np.float32)
    cos_t = np.concatenate([cos, cos, ones], axis=1)
    sin_t = np.concatenate([-sin, sin, 0.0 * ones], axis=1)
    i = np.arange(HEAD_DIM)
    src = np.where(i < half, i + half, np.where(i < ROPE_DIM, i - half, i))
    swap = ((i[:, None] == src[None, :]) & (i[None, :] < ROPE_DIM)).astype(np.float32)
    return jnp.asarray(cos_t, f32), jnp.asarray(sin_t, f32), jnp.asarray(swap, f32)


def _attn_prep(qkv, cos_t, sin_t, swap):
    outs = []
    for h in range(Q_HEADS + KV_HEADS):
        t = qkv[:, h * HEAD_DIM:(h + 1) * HEAD_DIM]
        outs.append(t * cos_t + _dot_const(t, swap) * sin_t)
    q = jnp.concatenate(outs[:Q_HEADS], axis=1)
    k = jnp.concatenate(outs[Q_HEADS:], axis=1)
    return q, k, qkv[:, Q_W + KV_W:]


def _attn_prep_transposed(dq, dk, dv, cos_t, sin_t, swap, dk_meta, dv_meta):
    first = pl.program_id(0) == 0

    def with_meta(d, d_meta):
        rest = jnp.zeros((d.shape[0] - BLOCK, KV_W), f32)
        return d.astype(f32) + jnp.where(first, jnp.concatenate([d_meta, rest], axis=0), 0.0)

    parts = []
    for d, heads in ((dq.astype(f32), Q_HEADS), (with_meta(dk, dk_meta), KV_HEADS)):
        for h in range(heads):
            t = d[:, h * HEAD_DIM:(h + 1) * HEAD_DIM]
            parts.append(t * cos_t + _two_pass(t * sin_t, swap, "nt"))
    return (jnp.concatenate(parts + [with_meta(dv, dv_meta)], axis=1),)


def _softplus(z):
    return jnp.maximum(z, 0.0) + jnp.log1p(jnp.exp(-jnp.abs(z)))


def _rwkv_prep(rkv, lora, w0, w2, a0, a2, g2, k_k, k_a, hsum):
    r = rkv[:, :RWKV_DIM]
    k = rkv[:, RWKV_DIM:2 * RWKV_DIM]
    v = rkv[:, 2 * RWKV_DIM:]
    dw = lora[:, :DECAY_LORA]
    da = lora[:, DECAY_LORA:DECAY_LORA + AAA_LORA]
    dg = lora[:, DECAY_LORA + AAA_LORA:]
    w = -_softplus(-(w0 + _dot(jnp.tanh(dw), w2))) - 0.5
    a = jax.nn.sigmoid(a0 + _dot(da, a2))
    g = _dot(jax.nn.sigmoid(dg), g2)
    kk = k * k_k
    kk = kk * lax.rsqrt(jnp.maximum(_dot_const(kk * kk, hsum), 1e-24))
    k = k * (1.0 + (a - 1.0) * k_a)
    log_decay = -jnp.exp(w)
    return r, log_decay, k, v, -kk, kk * a, g


def _rwkv_post(y, r, k, v, g, ln_w, ln_b, r_k, hmean):
    hsum = hmean * RWKV_HEAD
    mean = _dot_const(y, hmean)
    yc = y - mean
    var = _dot_const(yc * yc, hmean)
    yn = yc * lax.rsqrt(var + RWKV_LN_EPS) * ln_w + ln_b
    bonus = _dot_const(r * k * r_k, hsum) * v
    return ((yn + bonus) * g,)


def _merge(gates, br_a, br_r):
    sg = jax.nn.sigmoid(gates)
    return (sg[:, :D_MODEL] * br_a + sg[:, D_MODEL:] * br_r,)


def _swiglu(gate, up):
    return (jax.nn.silu(gate) * up,)


def _ffn_in(f, w_gate_t, w_up_t, *, name):
    m, d = f.shape
    n = w_gate_t.shape[0]
    tm = _tile(m)

    def body(f_ref, wg_ref, wu_ref, g_ref, u_ref, a_ref):
        g = _dot(f_ref[...], wg_ref[...], "nt")
        u = _dot(f_ref[...], wu_ref[...], "nt")
        g_ref[...] = g.astype(g_ref.dtype)
        u_ref[...] = u.astype(u_ref.dtype)
        a_ref[...] = _swiglu(g, u)[0].astype(a_ref.dtype)

    spec = pl.BlockSpec((tm, n), lambda i: (i, 0))
    return pl.pallas_call(
        body, name=name, grid=(m // tm,),
        in_specs=[pl.BlockSpec((tm, d), lambda i: (i, 0)), _full(w_gate_t.shape), _full(w_up_t.shape)],
        out_specs=[spec] * 3, out_shape=[jax.ShapeDtypeStruct((m, n), bf16)] * 3,
        compiler_params=_params(("parallel",)),
    )(f, w_gate_t, w_up_t)


def _branch_merge(y_attn, y_rwkv, w_attn_t, w_rwkv_t, gates, *, name):
    m = y_attn.shape[0]
    tm = _tile(m, MM_ROWS)

    def body(ya_ref, yr_ref, wa_ref, wr_ref, g_ref, a_ref, r_ref, o_ref):
        br_a = _dot(ya_ref[...], wa_ref[...], "nt")
        br_r = _dot(yr_ref[...], wr_ref[...], "nt")
        a_ref[...] = br_a.astype(a_ref.dtype)
        r_ref[...] = br_r.astype(r_ref.dtype)
        o_ref[...] = _merge(g_ref[...].astype(f32), br_a, br_r)[0].astype(o_ref.dtype)

    rows = lambda a: pl.BlockSpec((tm, a.shape[1]), lambda i: (i, 0))
    spec = pl.BlockSpec((tm, D_MODEL), lambda i: (i, 0))
    return pl.pallas_call(
        body, name=name, grid=(m // tm,),
        in_specs=[rows(y_attn), rows(y_rwkv), _full(w_attn_t.shape), _full(w_rwkv_t.shape), rows(gates)],
        out_specs=[spec] * 3, out_shape=[jax.ShapeDtypeStruct((m, D_MODEL), bf16)] * 3,
        compiler_params=_params(("parallel",)),
    )(y_attn, y_rwkv, w_attn_t, w_rwkv_t, gates)


def _branch_merge_bwd(dh, w_o, gates, br_a, br_r, w_attn_t, w_rwkv_t, *, name):
    m = dh.shape[0]
    tm = _tile(m, MM_ROWS)

    def body(dh_ref, w_ref, g_ref, a_ref, r_ref, wa_ref, wr_ref, dg_ref, da_ref, dr_ref, dya_ref, dyr_ref):
        dmerged = _dot(dh_ref[...], w_ref[...], "nt")
        _, vjp = jax.vjp(lambda g, a, r: _merge(g, a, r)[0], g_ref[...].astype(f32), a_ref[...].astype(f32),
                         r_ref[...].astype(f32))
        dg, da, dr = vjp(dmerged)
        dg_ref[...] = dg.astype(dg_ref.dtype)
        da_ref[...] = da.astype(da_ref.dtype)
        dr_ref[...] = dr.astype(dr_ref.dtype)
        dya_ref[...] = _dot(da, wa_ref[...])
        dyr_ref[...] = _dot(dr, wr_ref[...])

    rows = lambda a: pl.BlockSpec((tm, a.shape[1]), lambda i: (i, 0))
    mixer = pl.BlockSpec((tm, w_attn_t.shape[1]), lambda i: (i, 0))
    return pl.pallas_call(
        body, name=name, grid=(m // tm,),
        in_specs=[rows(dh), _full(w_o.shape), rows(gates), rows(br_a), rows(br_r), _full(w_attn_t.shape),
                  _full(w_rwkv_t.shape)],
        out_specs=[rows(gates), rows(br_a), rows(br_r), mixer, mixer],
        out_shape=[jax.ShapeDtypeStruct(gates.shape, bf16), jax.ShapeDtypeStruct(br_a.shape, bf16),
                   jax.ShapeDtypeStruct(br_r.shape, bf16), jax.ShapeDtypeStruct((m, w_attn_t.shape[1]), f32),
                   jax.ShapeDtypeStruct((m, w_rwkv_t.shape[1]), f32)],
        compiler_params=_params(("parallel",)),
    )(dh, w_o, gates, br_a, br_r, w_attn_t, w_rwkv_t)


def _ffn_in_bwd(dh, w_down, gate, up, *, name):
    m, d = dh.shape
    n = w_down.shape[0]
    tm = _tile(m)

    def body(dh_ref, w_ref, g_ref, u_ref, dg_ref, du_ref):
        dact = _dot(dh_ref[...], w_ref[...], "nt")
        _, vjp = jax.vjp(lambda a, b: _swiglu(a, b)[0], g_ref[...].astype(f32), u_ref[...].astype(f32))
        dg, du = vjp(dact)
        dg_ref[...] = dg.astype(dg_ref.dtype)
        du_ref[...] = du.astype(du_ref.dtype)

    spec = pl.BlockSpec((tm, n), lambda i: (i, 0))
    return pl.pallas_call(
        body, name=name, grid=(m // tm,),
        in_specs=[pl.BlockSpec((tm, d), lambda i: (i, 0)), _full(w_down.shape), spec, spec],
        out_specs=[spec] * 2, out_shape=[jax.ShapeDtypeStruct((m, n), bf16)] * 2,
        compiler_params=_params(("parallel",)),
    )(dh, w_down, gate, up)


def _ffn_bwd(dh, w_down, gate, up, w_gate_t, w_up_t, h, g, *, name):
    m, d = dh.shape
    n = w_down.shape[0]
    tm = _tile(m)
    halves = [(c * (n // 2), (c + 1) * (n // 2)) for c in range(2)]

    def body(dh_ref, wd_ref, g_ref, u_ref, wg_ref, wu_ref, h_ref, gn_ref, dg_ref, du_ref, dh1_ref, dgn_ref):
        dn = jnp.zeros((tm, d), f32)
        for lo, hi in halves:
            dact = _dot(dh_ref[...], wd_ref[lo:hi, :], "nt")
            _, vjp = jax.vjp(lambda a, b: _swiglu(a, b)[0], g_ref[:, lo:hi].astype(f32), u_ref[:, lo:hi].astype(f32))
            dg, du = (t.astype(bf16) for t in vjp(dact))
            dg_ref[:, lo:hi] = dg
            du_ref[:, lo:hi] = du
            dn = dn + _dot(dg, wg_ref[lo:hi, :]) + _dot(du, wu_ref[lo:hi, :])
        _, vjp = jax.vjp(lambda hv, gv: (_rms(hv, gv), hv), h_ref[...], gn_ref[...])
        dh1, dgn = vjp((dn, dh_ref[...]))
        dh1_ref[...] = dh1

        @pl.when(pl.program_id(0) == 0)
        def _():
            dgn_ref[...] = jnp.zeros_like(dgn_ref)

        dgn_ref[...] += dgn

    wide = pl.BlockSpec((tm, n), lambda i: (i, 0))
    tile = pl.BlockSpec((tm, d), lambda i: (i, 0))
    return pl.pallas_call(
        body, name=name, grid=(m // tm,),
        in_specs=[tile, _full(w_down.shape), wide, wide, _full(w_gate_t.shape), _full(w_up_t.shape), tile,
                  _full(g.shape)],
        out_specs=[wide, wide, tile, _full(g.shape)],
        out_shape=[jax.ShapeDtypeStruct((m, n), bf16)] * 2 + [jax.ShapeDtypeStruct((m, d), f32),
                                                               jax.ShapeDtypeStruct(g.shape, f32)],
        compiler_params=_params(("arbitrary",)),
    )(dh, w_down, gate, up, w_gate_t, w_up_t, h, g)


HALO = 16


def _previous_rows(x, before_ref, first_tile):
    rows = lax.broadcasted_iota(jnp.int32, x.shape, 0)
    last = jnp.where(first_tile, 0.0, before_ref[HALO - 1:HALO, :].astype(f32))
    return jnp.where(rows == 0, last, pltpu.roll(x, 1, axis=0))


def _mixer_inputs(ps, mixes, params, *, name):
    m = ps[0].shape[0]
    tm = _tile(m)
    sub = tm // HALO
    n_par = len(params)

    def body(*refs):
        first = pl.program_id(0) == 0
        pf = []
        for k in range(2):
            x = refs[k][...].astype(f32)
            pf.append(x + (_previous_rows(x, refs[2 + k], first) - x) * refs[4 + k][...])
        res = _rwkv_prep(*pf, *[ref[...] for ref in refs[6:6 + n_par]])
        for o_ref, val in zip(refs[6 + n_par:], res):
            o_ref[...] = val

    tile = lambda a: pl.BlockSpec((tm, a.shape[1]), lambda i: (i, 0))
    before = lambda a: pl.BlockSpec((HALO, a.shape[1]), lambda i: (jnp.maximum(i * sub - 1, 0), 0))
    out = pl.BlockSpec((tm, RWKV_DIM), lambda i: (i, 0))
    return pl.pallas_call(
        body, name=name, grid=(m // tm,),
        in_specs=[tile(a) for a in ps] + [before(a) for a in ps] + [_full(a.shape) for a in mixes + params],
        out_specs=[out] * 7, out_shape=[jax.ShapeDtypeStruct((m, RWKV_DIM), f32)] * 7,
        compiler_params=_params(("parallel",)),
    )(*ps, *ps, *mixes, *params)


def _mixer_inputs_bwd(ps, mixes, params, cts, *, name):
    m = ps[0].shape[0]
    tm = _tile(m)
    sub = tm // HALO
    nt = m // tm
    n_par = len(params)
    flat_cts = [c for group in cts for c in group]
    n_ct = len(flat_cts)

    def body(*refs):
        i = pl.program_id(0)
        tile_index = nt - 1 - i
        ct_refs = refs[6 + n_par:6 + n_par + n_ct]
        dp_refs = refs[6 + n_par + n_ct:8 + n_par + n_ct]
        dmix_refs = refs[8 + n_par + n_ct:10 + n_par + n_ct]
        dpar_refs = refs[10 + n_par + n_ct:9 + 2 * n_par + n_ct]
        carries = refs[9 + 2 * n_par + n_ct:]
        rows1 = tile_index * tm + lax.broadcasted_iota(jnp.int32, (tm, 1), 0)
        live = rows1 >= PAD

        @pl.when(i == 0)
        def _():
            for ref in (*dmix_refs, *dpar_refs, *carries):
                ref[...] = jnp.zeros_like(ref)

        xs, prevs, pf = [], [], []
        for k in range(2):
            x = refs[k][...].astype(f32)
            xp = _previous_rows(x, refs[2 + k], tile_index == 0)
            xs.append(x)
            prevs.append(xp)
            pf.append(x + (xp - x) * refs[4 + k][...])
        ct_vals, pos = [], 0
        for group in cts:
            acc = ct_refs[pos][...].astype(f32)
            for extra in range(1, len(group)):
                acc = acc + ct_refs[pos + extra][...].astype(f32)
            pos += len(group)
            ct_vals.append(jnp.where(live, acc, 0.0))
        par_vals = [ref[...] for ref in refs[6:6 + n_par]]
        _, vjp = jax.vjp(lambda *args: _rwkv_prep(*args, par_vals[-1]), *pf, *par_vals[:-1])
        g = vjp(tuple(ct_vals))
        for k in range(2):
            dpf = g[k]
            mixv = refs[4 + k][...]
            dm = dpf * mixv
            rows = lax.broadcasted_iota(jnp.int32, dm.shape, 0)
            dm_next = jnp.where(rows == tm - 1, carries[k][...], pltpu.roll(dm, tm - 1, axis=0))
            dp_refs[k][...] = jnp.where(live, dpf - dm + dm_next, 0.0).astype(dp_refs[k].dtype)
            carries[k][...] = dm[0:1, :]
            dmix_refs[k][...] += jnp.sum(dpf * (prevs[k] - xs[k]), axis=0, keepdims=True)
        for ref, val in zip(dpar_refs, g[2:]):
            ref[...] += val

    tile = lambda a: pl.BlockSpec((tm, a.shape[1]), lambda i: (nt - 1 - i, 0))
    before = lambda a: pl.BlockSpec((HALO, a.shape[1]), lambda i: (jnp.maximum((nt - 1 - i) * sub - 1, 0), 0))
    return pl.pallas_call(
        body, name=name, grid=(nt,),
        in_specs=[tile(a) for a in ps] + [before(a) for a in ps] + [_full(a.shape) for a in mixes + params]
        + [tile(c) for c in flat_cts],
        out_specs=[tile(a) for a in ps] + [_full(a.shape) for a in mixes + params[:-1]],
        out_shape=[jax.ShapeDtypeStruct(a.shape, bf16) for a in ps]
        + [jax.ShapeDtypeStruct(a.shape, f32) for a in mixes + params[:-1]],
        scratch_shapes=[pltpu.VMEM((1, a.shape[1]), f32) for a in ps],
        compiler_params=_params(("arbitrary",)),
    )(*ps, *ps, *mixes, *params, *flat_cts)


def _attn_masks(blk):
    qi = lax.broadcasted_iota(jnp.int32, (BLOCK, BLOCK), 0)
    ki = lax.broadcasted_iota(jnp.int32, (BLOCK, BLOCK), 1)
    qpos = blk * BLOCK + qi - PAD
    kpos_c = blk * BLOCK + ki - PAD
    kpos_p = kpos_c - BLOCK
    kpos_m = ki - PAD

    def band(kpos):
        return (kpos >= N_META) & (kpos <= qpos) & (qpos - kpos < WINDOW)

    return band(kpos_p), band(kpos_c), (kpos_m >= 0) & (kpos_m <= qpos)


def _attn_probs(qs, k3s, sink, oks):
    s = [[jnp.where(ok, _dot(qh, kx, "nt"), NEG_INF) for kx, ok in zip(k3, oks)] for qh, k3 in zip(qs, k3s)]
    mx = [jnp.maximum(jnp.maximum(jnp.max(t[0], -1, keepdims=True), jnp.max(t[1], -1, keepdims=True)),
                      jnp.maximum(jnp.max(t[2], -1, keepdims=True), sk)) for t, sk in zip(s, sink)]
    e = [[jnp.exp(tx - m) for tx in t] for t, m in zip(s, mx)]
    e_sink = [jnp.exp(sk - m) for sk, m in zip(sink, mx)]
    inv = [1.0 / (jnp.sum(t[0], -1, keepdims=True) + jnp.sum(t[1], -1, keepdims=True)
                  + jnp.sum(t[2], -1, keepdims=True) + es) for t, es in zip(e, e_sink)]
    return [[tx * i for tx in t] for t, i in zip(e, inv)], [es * i for es, i in zip(e_sink, inv)]


def _head_cols(i):
    return slice(i * HEAD_DIM, (i + 1) * HEAD_DIM)


def _attn_operands(refs):
    q_ref, kp_ref, kc_ref, km_ref, vp_ref, vc_ref, vm_ref, s_ref = refs
    qs = [q_ref[:, _head_cols(i)] * (HEAD_DIM ** -0.5) for i in range(Q_HEADS)]
    k3 = [[ref[:, _head_cols(h)] for ref in (kp_ref, kc_ref, km_ref)] for h in range(KV_HEADS)]
    v3 = [[ref[:, _head_cols(h)] for ref in (vp_ref, vc_ref, vm_ref)] for h in range(KV_HEADS)]
    return (qs, [k3[i // GROUP] for i in range(Q_HEADS)], [v3[i // GROUP] for i in range(Q_HEADS)],
            [s_ref[:, i:i + 1] for i in range(Q_HEADS)])


def _attention(q, k, v, sinks, *, name):
    lp = q.shape[0]
    nb = lp // BLOCK
    prev = lambda i: (jnp.maximum(i - 1, 0), 0)
    cur = lambda i: (i, 0)
    meta = lambda i: (0, 0)
    kv = lambda index: pl.BlockSpec((BLOCK, KV_W), index)

    def body(*refs):
        o_ref = refs[-1]
        qs, k3s, v3s, sink = _attn_operands(refs[:-1])
        p, _ = _attn_probs(qs, k3s, sink, _attn_masks(pl.program_id(0)))
        out = [_dot(ph[0], v3[0]) + _dot(ph[1], v3[1]) + _dot(ph[2], v3[2]) for ph, v3 in zip(p, v3s)]
        for i in range(Q_HEADS):
            o_ref[:, _head_cols(i)] = out[i].astype(o_ref.dtype)

    return pl.pallas_call(
        body, name=name, grid=(nb,),
        in_specs=[pl.BlockSpec((BLOCK, Q_W), cur), kv(prev), kv(cur), kv(meta), kv(prev), kv(cur), kv(meta),
                  _full((1, Q_HEADS))],
        out_specs=pl.BlockSpec((BLOCK, Q_W), cur),
        out_shape=jax.ShapeDtypeStruct((lp, Q_W), bf16),
        compiler_params=_params(("parallel",)),
    )(q, k, k, k, v, v, v, sinks)


def _attention_bwd(q, k, v, sinks, out, do, *, name):
    lp = q.shape[0]
    nb = lp // BLOCK
    cur = lambda n: (jnp.minimum(n, nb - 1), 0)
    prev = lambda n: (jnp.maximum(jnp.minimum(n, nb - 1) - 1, 0), 0)
    behind = lambda n: (jnp.maximum(n - 1, 0), 0)
    meta = lambda n: (0, 0)
    kv = lambda index: pl.BlockSpec((BLOCK, KV_W), index)
    scale = HEAD_DIM ** -0.5

    def body(*refs):
        ins, fwd_ref, do_ref = refs[:8], refs[8], refs[9]
        dq_ref, dk_ref, dv_ref, dkm_ref, dvm_ref, ds_ref, carry_k, carry_v = refs[10:]
        n = pl.program_id(0)

        @pl.when(n == 0)
        def _():
            for ref in (dkm_ref, dvm_ref, ds_ref, carry_k, carry_v):
                ref[...] = jnp.zeros_like(ref)

        @pl.when(n < nb)
        def _():
            qs, k3s, v3s, sink = _attn_operands(ins)
            do = [do_ref[:, _head_cols(i)] for i in range(Q_HEADS)]
            p, p_sink = _attn_probs(qs, k3s, sink, _attn_masks(n))
            delta = [jnp.sum(d * fwd_ref[:, _head_cols(i)].astype(f32), -1, keepdims=True) for i, d in enumerate(do)]
            dp = [[_dot(d, vx, "nt") for vx in v3] for d, v3 in zip(do, v3s)]
            ds = [[px * (dx - dl) for px, dx in zip(ph, dh)] for ph, dh, dl in zip(p, dp, delta)]
            dq = [_dot(dsh[0], k3[0]) + _dot(dsh[1], k3[1]) + _dot(dsh[2], k3[2]) for dsh, k3 in zip(ds, k3s)]
            for i in range(Q_HEADS):
                dq_ref[:, _head_cols(i)] = dq[i] * scale
                ds_ref[:, i:i + 1] -= jnp.sum(p_sink[i] * delta[i], axis=0, keepdims=True)
            for h in range(KV_HEADS):
                group = slice(h * GROUP, (h + 1) * GROUP)
                q_all = jnp.concatenate(qs[group], axis=0)
                do_all = jnp.concatenate(do[group], axis=0)
                dk3 = [_dot(jnp.concatenate([dsh[x] for dsh in ds[group]], axis=0), q_all, "tn") for x in range(3)]
                dv3 = [_dot(jnp.concatenate([ph[x] for ph in p[group]], axis=0), do_all, "tn") for x in range(3)]
                hs = _head_cols(h)
                for out_ref, carry, meta_ref, d3 in ((dk_ref, carry_k, dkm_ref, dk3),
                                                     (dv_ref, carry_v, dvm_ref, dv3)):
                    out_ref[:, hs] = carry[:, hs] + d3[0]
                    carry[:, hs] = d3[1]
                    meta_ref[:, hs] += d3[2]

        @pl.when(n == nb)
        def _():
            dk_ref[...] = carry_k[...]
            dv_ref[...] = carry_v[...]

    kv_shape = jax.ShapeDtypeStruct((lp, KV_W), f32)
    one_shape = jax.ShapeDtypeStruct((BLOCK, KV_W), f32)
    return pl.pallas_call(
        body, name=name, grid=(nb + 1,),
        in_specs=[pl.BlockSpec((BLOCK, Q_W), cur), kv(prev), kv(cur), kv(meta), kv(prev), kv(cur), kv(meta),
                  _full((1, Q_HEADS)), pl.BlockSpec((BLOCK, Q_W), cur), pl.BlockSpec((BLOCK, Q_W), cur)],
        out_specs=[pl.BlockSpec((BLOCK, Q_W), cur), kv(behind), kv(behind), kv(meta), kv(meta),
                   _full((1, Q_HEADS))],
        out_shape=[jax.ShapeDtypeStruct((lp, Q_W), f32), kv_shape, kv_shape, one_shape, one_shape,
                   jax.ShapeDtypeStruct((1, Q_HEADS), f32)],
        scratch_shapes=[pltpu.VMEM((BLOCK, KV_W), f32), pltpu.VMEM((BLOCK, KV_W), f32)],
        compiler_params=_params(("arbitrary",)),
    )(q, k, k, k, v, v, v, sinks, out, do)


@jax.custom_vjp
def _known_inverse(l, x):
    return x


def _known_inverse_fwd(l, x):
    return x, x


def _known_inverse_bwd(x, ct):
    return _dot(_dot(x, ct, "tn"), x, "nt"), jnp.zeros_like(x)


_known_inverse.defvjp(_known_inverse_fwd, _known_inverse_bwd)


@jax.custom_vjp
def _decayed(x, c):
    return (x * jnp.exp(c)).astype(bf16).astype(f32)


def _decayed_fwd(x, c):
    e = jnp.exp(c)
    out = (x * e).astype(bf16).astype(f32)
    return out, (e, out)


def _decayed_bwd(res, ct):
    e, out = res
    return ct * e, ct * out


_decayed.defvjp(_decayed_fwd, _decayed_bwd)


@jax.custom_vjp
def _pair(x, y):
    return _dot(x, y, "nt")


def _pair_fwd(x, y):
    return _dot(x, y, "nt"), (x, y)


def _pair_bwd(res, ct):
    x, y = res
    hi = ct.astype(bf16)
    lo = (ct - hi.astype(f32)).astype(bf16)
    return _dot(hi, y) + _dot(lo, y), _dot(hi, x, "tn") + _dot(lo, x, "tn")


_pair.defvjp(_pair_fwd, _pair_bwd)


def _scan_chunk(s0, r, lw, k, v, a, b, inv=None):
    t = r[0].shape[0]
    ii = lax.broadcasted_iota(jnp.int32, (t, t), 0)
    jj = lax.broadcasted_iota(jnp.int32, (t, t), 1)
    incl = jj <= ii
    strict = jj < ii
    tri = incl.astype(f32)
    eye = jnp.where(ii == jj, 1.0, 0.0)
    cl = [_const_dot(tri, x) for x in lw]
    mid = [c[t // 2 - 1:t // 2, :] for c in cl]
    s0 = [s * jnp.exp(m) for s, m in zip(s0, mid)]
    cl = [c - m for c, m in zip(cl, mid)]
    rt = [_decayed(x, c) for x, c in zip(r, cl)]
    at = [_decayed(x, c - l) for x, c, l in zip(a, cl, lw)]
    bt = [_decayed(x, -c) for x, c in zip(b, cl)]
    kt = [_decayed(x, -c) for x, c in zip(k, cl)]
    l_ab = [jnp.where(strict, _pair(x, y), 0.0) for x, y in zip(at, bt)]
    l_ak = [jnp.where(strict, _pair(x, y), 0.0) for x, y in zip(at, kt)]
    r_b = [jnp.where(incl, _pair(x, y), 0.0) for x, y in zip(rt, bt)]
    r_k = [jnp.where(incl, _pair(x, y), 0.0) for x, y in zip(rt, kt)]
    if inv is None:
        inv = [eye + x for x in l_ab]
        pw = l_ab
        for _ in range(int(math.log2(t)) - 1):
            pw = [_dot(x, x) for x in pw]
            inv = [x + _dot(x, y) for x, y in zip(inv, pw)]
    else:
        inv = [_known_inverse(x, y) for x, y in zip(l_ab, inv)]
    rhs = [_dot(x, s, "nt") + _dot(m, y) for x, s, m, y in zip(at, s0, l_ak, v)]
    u = [_dot(x, y) for x, y in zip(inv, rhs)]
    y_s = [_dot(x, s, "nt") for x, s in zip(rt, s0)]
    y = [ys + _dot(m, uu) + _dot(n, vv) for ys, m, uu, n, vv in zip(y_s, r_b, u, r_k, v)]
    grow = [s + _dot(uu, x, "tn") + _dot(vv, z, "tn") for s, uu, x, vv, z in zip(s0, u, bt, v, kt)]
    s1 = [g * jnp.exp(c[t - 1:t, :]) for g, c in zip(grow, cl)]
    return y, s1, inv


def _head_rows(h):
    return slice(h * RWKV_HEAD, (h + 1) * RWKV_HEAD)


def _per_head(ref):
    return [ref[:, _head_rows(h)] for h in range(RWKV_HEADS)]


def _scan(r, lw, k, v, a, b, *, name):
    lp = r.shape[0]
    nc = lp // CHUNK
    row = pl.BlockSpec((CHUNK, RWKV_DIM), lambda c: (c, 0))

    def body(r_ref, lw_ref, k_ref, v_ref, a_ref, b_ref, y_ref, s_ref, inv_ref, state):
        @pl.when(pl.program_id(0) == 0)
        def _():
            state[...] = jnp.zeros_like(state)

        s_ref[...] = state[...]
        s0 = [state[_head_rows(h), :] for h in range(RWKV_HEADS)]
        y, s1, inv = _scan_chunk(s0, *[_per_head(ref) for ref in (r_ref, lw_ref, k_ref, v_ref, a_ref, b_ref)])
        for h in range(RWKV_HEADS):
            y_ref[:, _head_rows(h)] = y[h]
            state[_head_rows(h), :] = s1[h]
            inv_ref[h * CHUNK:(h + 1) * CHUNK, :] = inv[h].astype(inv_ref.dtype)

    return pl.pallas_call(
        body, name=name, grid=(nc,), in_specs=[row] * 6,
        out_specs=[row, pl.BlockSpec((RWKV_DIM, RWKV_HEAD), lambda c: (c, 0)),
                   pl.BlockSpec((RWKV_HEADS * CHUNK, CHUNK), lambda c: (c, 0))],
        out_shape=[jax.ShapeDtypeStruct((lp, RWKV_DIM), f32), jax.ShapeDtypeStruct((nc * RWKV_DIM, RWKV_HEAD), f32),
                   jax.ShapeDtypeStruct((nc * RWKV_HEADS * CHUNK, CHUNK), bf16)],
        scratch_shapes=[pltpu.VMEM((RWKV_DIM, RWKV_HEAD), f32)],
        compiler_params=_params(("arbitrary",)),
    )(r, lw, k, v, a, b)


def _scan_bwd(r, lw, k, v, a, b, states, inverses, dy, *, name):
    lp = r.shape[0]
    nc = lp // CHUNK
    back = lambda c: (nc - 1 - c, 0)
    row = pl.BlockSpec((CHUNK, RWKV_DIM), back)

    def body(r_ref, lw_ref, k_ref, v_ref, a_ref, b_ref, s_ref, inv_ref, dy_ref,
             dr_ref, dlw_ref, dk_ref, dv_ref, da_ref, db_ref, dstate):
        @pl.when(pl.program_id(0) == 0)
        def _():
            dstate[...] = jnp.zeros_like(dstate)

        outs = (dr_ref, dlw_ref, dk_ref, dv_ref, da_ref, db_ref)
        s0 = [s_ref[_head_rows(h), :] for h in range(RWKV_HEADS)]
        inv = [inv_ref[h * CHUNK:(h + 1) * CHUNK, :].astype(f32) for h in range(RWKV_HEADS)]
        _, vjp = jax.vjp(lambda *args: _scan_chunk(*args, inv=inv)[:2], s0,
                         *[_per_head(ref) for ref in (r_ref, lw_ref, k_ref, v_ref, a_ref, b_ref)])
        g = vjp((_per_head(dy_ref), [dstate[_head_rows(h), :] for h in range(RWKV_HEADS)]))
        for h in range(RWKV_HEADS):
            dstate[_head_rows(h), :] = g[0][h]
            for o_ref, gv in zip(outs, g[1:]):
                o_ref[:, _head_rows(h)] = gv[h]

    shape = jax.ShapeDtypeStruct((lp, RWKV_DIM), f32)
    return pl.pallas_call(
        body, name=name, grid=(nc,),
        in_specs=[row] * 6 + [pl.BlockSpec((RWKV_DIM, RWKV_HEAD), back),
                              pl.BlockSpec((RWKV_HEADS * CHUNK, CHUNK), back), row],
        out_specs=[row] * 6, out_shape=[shape] * 6,
        scratch_shapes=[pltpu.VMEM((RWKV_DIM, RWKV_HEAD), f32)],
        compiler_params=_params(("arbitrary",)),
    )(r, lw, k, v, a, b, states, inverses, dy)


def _loss_head(act, w_down, h1, target, g_final, *, name):
    lp = h1.shape[0]
    per_tile = 3
    tm = per_tile * BLOCK
    last_block = (lp - FRONT) // BLOCK - 1

    def body(a_ref, w_ref, h_ref, t0_ref, t1_ref, t2_ref, g_ref, loss_ref, dh_ref, dg_ref):
        i = pl.program_id(0)
        target_rows = jnp.concatenate([t0_ref[...], t1_ref[...], t2_ref[...]], axis=0)
        real = i * tm + lax.broadcasted_iota(jnp.int32, (tm, 1), 0) >= FRONT

        def tile_loss(hv, gv):
            err = _rms(hv, gv) - target_rows
            return 0.5 * jnp.sum(jnp.where(real, jnp.mean(err * err, axis=-1, keepdims=True), 0.0))

        h2 = _dot(a_ref[...], w_ref[...], "nn") + h_ref[...]
        loss, (dh, dg) = jax.value_and_grad(tile_loss, argnums=(0, 1))(h2, g_ref[...])

        @pl.when(i == 0)
        def _():
            loss_ref[...] = jnp.zeros_like(loss_ref)
            dg_ref[...] = jnp.zeros_like(dg_ref)

        loss_ref[...] += jnp.full(loss_ref.shape, loss, f32)
        dg_ref[...] += dg
        dh_ref[...] = dh

    def target_block(j):
        return pl.BlockSpec((BLOCK, D_MODEL),
                            lambda i: (jnp.clip(per_tile * i + j - FRONT // BLOCK, 0, last_block), 0))

    return pl.pallas_call(
        body, name=name, grid=(lp // tm,),
        in_specs=[pl.BlockSpec((tm, act.shape[1]), lambda i: (i, 0)), _full(w_down.shape),
                  pl.BlockSpec((tm, D_MODEL), lambda i: (i, 0)), target_block(0), target_block(1), target_block(2),
                  _full(g_final.shape)],
        out_specs=[_full((8, 128)), pl.BlockSpec((tm, D_MODEL), lambda i: (i, 0)), _full(g_final.shape)],
        out_shape=[jax.ShapeDtypeStruct((8, 128), f32), jax.ShapeDtypeStruct((lp, D_MODEL), f32),
                   jax.ShapeDtypeStruct(g_final.shape, f32)],
        compiler_params=_params(("arbitrary",)),
    )(act, w_down, h1, target, target, target, g_final)


def _embed_norm(x, meta, g, *, name):
    seq = x.shape[0]
    lp = seq + FRONT
    per_tile = 3
    tm = per_tile * BLOCK
    last_block = seq // BLOCK - 1

    def body(x0_ref, x1_ref, x2_ref, meta_ref, g_ref, h_ref, u_ref):
        front = jnp.concatenate([jnp.zeros((PAD, D_MODEL), f32), meta_ref[...]], axis=0)
        first = jnp.where(pl.program_id(0) == 0, front, x0_ref[...])
        h = jnp.concatenate([first, x1_ref[...], x2_ref[...]], axis=0)
        h_ref[...] = h
        u_ref[...] = _rms(h, g_ref[...]).astype(u_ref.dtype)

    def x_block(j):
        return pl.BlockSpec((BLOCK, D_MODEL),
                            lambda i: (jnp.clip(per_tile * i + j - FRONT // BLOCK, 0, last_block), 0))

    tile = pl.BlockSpec((tm, D_MODEL), lambda i: (i, 0))
    return pl.pallas_call(
        body, name=name, grid=(lp // tm,),
        in_specs=[x_block(0), x_block(1), x_block(2), _full(meta.shape), _full(g.shape)],
        out_specs=[tile, tile],
        out_shape=[jax.ShapeDtypeStruct((lp, D_MODEL), f32), jax.ShapeDtypeStruct((lp, D_MODEL), bf16)],
        compiler_params=_params(("parallel",)),
    )(x, x, x, meta, g)


def _input_norm_bwd(h0, g, du, dh1, *, name):
    lp = h0.shape[0]
    blocks = (lp - FRONT) // FRONT
    per_tile = max(n for n in (4, 3, 2, 1) if blocks % n == 0)
    ins = (h0, du, dh1)

    def body(*refs):
        tiles = [refs[k * per_tile:(k + 1) * per_tile] for k in range(len(ins))]
        front_refs = refs[len(ins) * per_tile:len(ins) * (per_tile + 1)]
        g_ref, dx_ref, front_ref, dg_ref = refs[len(ins) * (per_tile + 1):]

        def cotangents(h_ref, du_ref, dh1_ref):
            _, vjp = jax.vjp(lambda hv, gv: (_rms(hv, gv), hv), h_ref[...], g_ref[...])
            return vjp((du_ref[...].astype(f32), dh1_ref[...]))

        @pl.when(pl.program_id(0) == 0)
        def _():
            front_ref[...], dg_ref[...] = cotangents(*front_refs)

        dg = jnp.zeros(dg_ref.shape, f32)
        for j in range(per_tile):
            dh, dg_j = cotangents(*(t[j] for t in tiles))
            dx_ref[j * FRONT:(j + 1) * FRONT, :] = dh
            dg = dg + dg_j
        dg_ref[...] += dg

    def block(j):
        return pl.BlockSpec((FRONT, D_MODEL), lambda i: (per_tile * i + j + 1, 0))

    first = pl.BlockSpec((FRONT, D_MODEL), lambda i: (0, 0))
    return pl.pallas_call(
        body, name=name, grid=(blocks // per_tile,),
        in_specs=[block(j) for _ in ins for j in range(per_tile)] + [first] * len(ins) + [_full(g.shape)],
        out_specs=[pl.BlockSpec((per_tile * FRONT, D_MODEL), lambda i: (i, 0)), _full((FRONT, D_MODEL)),
                   _full(g.shape)],
        out_shape=[jax.ShapeDtypeStruct((lp - FRONT, D_MODEL), f32), jax.ShapeDtypeStruct((FRONT, D_MODEL), f32),
                   jax.ShapeDtypeStruct(g.shape, f32)],
        compiler_params=_params(("arbitrary",)),
    )(*(a for a in ins for _ in range(per_tile)), *ins, g)


def _local_step(x, target, meta, p, early_weights=None, late_weights=None, emit=None):
    emit = emit or (lambda group, grads: 0.0)
    seq = x.shape[0]
    lp = seq + FRONT
    cos_t, sin_t, swap = _rope_tables(lp)
    hsum = _head_sum_matrix(RWKV_DIM, RWKV_HEAD)
    hmean = hsum / RWKV_HEAD
    post_params = [p["ln_w"], p["ln_b"], p["r_k"], hmean]

    h0, u = _embed_norm(x, meta, p["norm_mix_g"], name="norm_mix")
    if early_weights is not None:
        p = {**p, **early_weights(u)}
    prep_params = [p["w0"], p["w2"], p["a0"], p["a2"], p["g2"], p["k_k"], p["k_a"], hsum]
    in_widths = [ATTN_PROJ, RKV_W, LORA_W, 2 * D_MODEL]
    q, k, v, p_rkv, p_lora, gates = _proj_in(u, p["w_in_lr"], p["b_in"], in_widths,
                                             (cos_t, sin_t, swap), name="proj_in", zero_rows_below=PAD)
    y_attn = _attention(q, k, v, p["sinks"], name="attention")

    mix_rkv, mix_lora = p["mix"][:, :RKV_W], p["mix"][:, RKV_W:]
    r_, lw_, k_, v_, a_, b_, g_ = _mixer_inputs([p_rkv, p_lora], [mix_rkv, mix_lora], prep_params,
                                                name="mixer_inputs")
    y_scan, states, inverses = _scan(r_, lw_, k_, v_, a_, b_, name="wkv_scan")
    (y_rwkv,) = _rowwise(_rwkv_post, [y_scan, r_, k_, v_, g_], post_params, [(RWKV_DIM, bf16)], name="rwkv_post")

    if late_weights is not None:
        p = {**p, **late_weights(y_rwkv)}
    br_a, br_r, merged = _branch_merge(y_attn, y_rwkv, p["w_br_attn_t"], p["w_br_rwkv_t"], gates, name="branch_merge")
    h1, f = _residual_norm(merged, p["w_o"], h0, p["norm_ffn_g"], name="out_proj")
    gate, up, act = _ffn_in(f, p["w_gate_t"], p["w_up_t"], name="ffn_in")

    loss8, dh2, d_final_g = _loss_head(act, p["w_down"], h1, target, p["norm_final_g"], name="loss_head")
    dgate, dup, dh1, d_ffn_g = _ffn_bwd(dh2, p["w_down"], gate, up, p["w_gate_t"], p["w_up_t"], h1, p["norm_ffn_g"],
                                        name="ffn_bwd")
    d_w_down = _mm_tn(act, dh2, name="dw_down")
    d_w_gate_t = _mm_tn(dgate, f, name="dw_gate")
    d_w_up_t = _mm_tn(dup, f, name="dw_up")
    zero = emit("ffn", dict(w_down=d_w_down, w_gate_t=d_w_gate_t, w_up_t=d_w_up_t))
    dgates, dbr_a, dbr_r, dy_attn, dy_rwkv = _branch_merge_bwd(
        dh1, p["w_o"], gates, br_a, br_r, p["w_br_attn_t"], p["w_br_rwkv_t"], name="branch_merge_bwd")
    d_w_o = _mm_tn(merged, dh1, name="dw_o")
    d_w_br_attn_t = _mm_tn(dbr_a, y_attn, name="dw_br_attn")
    d_w_br_rwkv_t = _mm_tn(dbr_r, y_rwkv, name="dw_br_rwkv")
    zero = zero + emit("branch", dict(w_o=d_w_o, w_br_attn_t=d_w_br_attn_t, w_br_rwkv_t=d_w_br_rwkv_t))

    post_params = [p["ln_w"] + zero, p["ln_b"], p["r_k"], hmean]
    res = _rowwise_bwd(_rwkv_post, [y_scan, r_, k_, v_, g_], post_params, [[dy_rwkv]], name="rwkv_post_bwd",
                       diff_rows=[True] * 5, diff_params=[True, True, True, False])
    dy_scan, dr_p, dk_p, dv_p, dg_p, d_ln_w, d_ln_b, d_r_k = res
    dr_s, dlw_s, dk_s, dv_s, da_s, db_s = _scan_bwd(r_, lw_, k_, v_, a_, b_, states, inverses, dy_scan,
                                                    name="wkv_scan_bwd")
    res = _mixer_inputs_bwd([p_rkv, p_lora], [mix_rkv, mix_lora], prep_params,
                            [[dr_s, dr_p], [dlw_s], [dk_s, dk_p], [dv_s, dv_p], [da_s], [db_s], [dg_p]],
                            name="mixer_inputs_bwd")
    dp_rkv, dp_lora, d_mix_rkv, d_mix_lora, d_w0, d_w2, d_a0, d_a2, d_g2, d_k_k, d_k_a = res

    dq, dk, dv, dkm, dvm, d_sinks = _attention_bwd(q, k, v, p["sinks"], y_attn, dy_attn, name="attention_bwd")
    (dqkv,) = _rowwise(_attn_prep_transposed, [dq, dk, dv, cos_t, sin_t], [swap, dkm, dvm], [(ATTN_PROJ, bf16)],
                       name="attn_prep_bwd")

    in_rows, (at_qkv, at_rkv, at_lora, at_gates) = p["w_in_lr"].shape[1], [off for off, _ in _pieces(in_widths)]
    d_w_in_t, db_gates = _mm_tn(dgates, u, name="dw_gates", colsum=True, into=(in_rows, at_gates, None))
    d_w_in_t, db_rkv = _mm_tn(dp_rkv, u, name="dw_rkv", colsum=True, into=(in_rows, at_rkv, d_w_in_t))
    d_w_in_t, db_lora = _mm_tn(dp_lora, u, name="dw_lora", colsum=True, into=(in_rows, at_lora, d_w_in_t))
    d_w_in_t, db_qkv = _mm_tn(dqkv, u, name="dw_qkv", colsum=True, into=(in_rows, at_qkv, d_w_in_t))
    zero = emit("input", dict(w_in_t=d_w_in_t, g2=d_g2, w2=d_w2, a2=d_a2))
    du = _proj_in_bwd([dqkv, dp_rkv, dp_lora, dgates], p["w_in_lr"], name="d_u")
    dx, d_front, d_mix_g = _input_norm_bwd(h0, p["norm_mix_g"] + zero, du, dh1, name="norm_mix_bwd")

    grads = dict(
        w_in_t=d_w_in_t,
        b_in=jnp.concatenate([db_qkv, db_rkv, db_lora, db_gates], axis=1),
        mix=jnp.concatenate([d_mix_rkv, d_mix_lora], axis=1),
        norm_mix_g=d_mix_g, sinks=d_sinks, w0=d_w0, w2=d_w2, a0=d_a0, a2=d_a2, g2=d_g2, k_k=d_k_k, k_a=d_k_a,
        r_k=d_r_k, ln_w=d_ln_w, ln_b=d_ln_b, w_br_attn_t=d_w_br_attn_t, w_br_rwkv_t=d_w_br_rwkv_t, w_o=d_w_o,
        norm_ffn_g=d_ffn_g, w_gate_t=d_w_gate_t, w_up_t=d_w_up_t, w_down=d_w_down, norm_final_g=d_final_g,
        meta=d_front[PAD:],
    )
    return loss8[0, 0], dx, grads


def _position():
    return lax.axis_index("x"), lax.axis_index("y"), lax.axis_index("c")


def _other_chips(x, y):
    return [(1 - x, y), (x, 1 - y), (1 - x, 1 - y)]


_HBM = pl.BlockSpec(memory_space=pltpu.HBM)
_SEM = pl.BlockSpec(memory_space=pltpu.SEMAPHORE)
_EFFECT = pltpu.SideEffectType.DATAFLOW_SIDE_EFFECTING


def _landing_zone(src, kind):
    shape = {"whole": (N_CHIPS,) + src.shape, "half": (2, N_CHIPS, src.shape[0], src.shape[1] // 2),
             "slab": (3,) + src.shape[1:], "sibling": src.shape, "all": (N_DEV - 1,) + src.shape}[kind]
    return lax.empty(shape, src.dtype)


def _copies_per_source(kind):
    return {"sibling": 1, "all": N_DEV - 1}.get(kind, 3)


def _chip_copies(src_refs, land_refs, send_sems, recv_sems, kind):
    x, y, c = _position()
    if kind == "all":
        copies = []
        for a, (src, land) in enumerate(zip(src_refs, land_refs)):
            for rel in range(1, N_DEV):
                peer = ((1 - x) if rel & 4 else x, (1 - y) if rel & 2 else y, (1 - c) if rel & 1 else c)
                k = (N_DEV - 1) * a + rel - 1
                copies.append(pltpu.make_async_remote_copy(
                    src_ref=src, dst_ref=land.at[rel - 1], send_sem=send_sems.at[k], recv_sem=recv_sems.at[k],
                    device_id=peer, device_id_type=MESH))
        return copies
    if kind == "sibling":
        return [pltpu.make_async_remote_copy(
            src_ref=src, dst_ref=land, send_sem=send_sems.at[a], recv_sem=recv_sems.at[a],
            device_id=(x, y, 1 - c), device_id_type=MESH) for a, (src, land) in enumerate(zip(src_refs, land_refs))]
    copies = []
    for a, (src, land) in enumerate(zip(src_refs, land_refs)):
        for j, (px, py) in enumerate(_other_chips(x, y)):
            if kind == "whole":
                src_ref, dst_ref = src, land.at[2 * x + y]
            elif kind == "half":
                half = src.shape[1] // 2
                src_ref, dst_ref = src.at[:, pl.ds(pl.multiple_of(c * half, half), half)], land.at[c, 2 * x + y]
            else:
                src_ref, dst_ref = src.at[2 * px + py], land.at[j]
            copies.append(pltpu.make_async_remote_copy(
                src_ref=src_ref, dst_ref=dst_ref, send_sem=send_sems.at[3 * a + j], recv_sem=recv_sems.at[3 * a + j],
                device_id=(px, py, c), device_id_type=MESH))
    return copies


def _exchange_start(srcs, *, kind, name):
    n = len(srcs)
    lands = [_landing_zone(s, kind) for s in srcs]

    def body(*refs):
        for cp in _chip_copies(refs[:n], refs[n:2 * n], refs[2 * n], refs[2 * n + 1], kind):
            cp.start()
        refs[-1][...] = jnp.zeros_like(refs[-1])

    res = pl.pallas_call(
        body, name=name,
        out_shape=(pltpu.SemaphoreType.DMA((_copies_per_source(kind) * n,)),
                   pltpu.SemaphoreType.DMA((_copies_per_source(kind) * n,)),
                   *[pltpu.HBM(a.shape, a.dtype) for a in srcs + lands], jax.ShapeDtypeStruct((8, 128), f32)),
        in_specs=[_HBM] * (2 * n),
        out_specs=(_SEM, _SEM, *[_HBM] * (2 * n), pl.BlockSpec(memory_space=pltpu.VMEM)),
        input_output_aliases={i: 2 + i for i in range(2 * n)},
        compiler_params=pltpu.CompilerParams(has_side_effects=_EFFECT),
    )(*[pltpu.with_memory_space_constraint(a, pltpu.HBM) for a in srcs + lands])
    return res[0], res[1], list(res[2:2 + n]), list(res[2 + n:2 + 2 * n]), res[-1]


def _exchange_wait(handle, after, *, kind, name):
    send_sems, recv_sems, srcs, lands, _ = handle
    n = len(srcs)

    def body(*refs):
        for cp in _chip_copies(refs[:n], refs[n:2 * n], refs[2 * n], refs[2 * n + 1], kind):
            cp.wait_send()
            cp.wait_recv()

    res = pl.pallas_call(
        body, name=name,
        out_shape=tuple(pltpu.HBM(a.shape, a.dtype) for a in srcs + lands),
        in_specs=[_HBM] * (2 * n) + [_SEM, _SEM, pl.BlockSpec(memory_space=pl.ANY)],
        out_specs=tuple([_HBM] * (2 * n)),
        input_output_aliases={i: i for i in range(2 * n)},
        compiler_params=pltpu.CompilerParams(has_side_effects=_EFFECT),
    )(*srcs, *lands, send_sems, recv_sems, after)
    return list(res[:n]), list(res[n:])


def _sum_own_and_received(g, recv, *, name):
    _, r, w = g.shape
    tm = _tile(r)
    if g.dtype == bf16 and tm % 16:
        tm = r
    x, y, _ = _position()
    me = jnp.reshape(2 * x + y, (1,)).astype(jnp.int32)

    def body(me_ref, g_ref, r_ref, o_ref):
        o_ref[...] = (g_ref[0].astype(f32) + r_ref[0].astype(f32)) + (r_ref[1].astype(f32) + r_ref[2].astype(f32))

    return pl.pallas_call(
        body, name=name,
        grid_spec=pltpu.PrefetchScalarGridSpec(
            num_scalar_prefetch=1, grid=(r // tm,),
            in_specs=[pl.BlockSpec((1, tm, w), lambda i, me_ref: (me_ref[0], i, 0)),
                      pl.BlockSpec((3, tm, w), lambda i, me_ref: (0, i, 0))],
            out_specs=pl.BlockSpec((tm, w), lambda i, me_ref: (i, 0))),
        out_shape=jax.ShapeDtypeStruct((r, w), f32),
        compiler_params=_params(("parallel",)),
    )(me, g, recv)


def _swap_cores(arrs, *, name):
    n = len(arrs)

    def body(*refs):
        x, y, c = _position()
        copies = [pltpu.make_async_remote_copy(
            src_ref=refs[i], dst_ref=refs[n + i], send_sem=refs[2 * n].at[i], recv_sem=refs[2 * n + 1].at[i],
            device_id=(x, y, 1 - c), device_id_type=MESH) for i in range(n)]
        for cp in copies:
            cp.start()
        for cp in copies:
            cp.wait_recv()
        for cp in copies:
            cp.wait_send()

    return pl.pallas_call(
        body, name=name,
        in_specs=[pl.BlockSpec(memory_space=pl.ANY)] * n,
        out_specs=[pl.BlockSpec(memory_space=pl.ANY)] * n,
        out_shape=[jax.ShapeDtypeStruct(a.shape, a.dtype) for a in arrs],
        scratch_shapes=[pltpu.SemaphoreType.DMA((n,)), pltpu.SemaphoreType.DMA((n,))],
    )(*arrs)


def _swap_halves(zone, *, name):
    def body(z_ref, o_ref, send_sems, recv_sems):
        x, y, c = _position()
        mine = [pltpu.make_async_remote_copy(
            src_ref=o_ref.at[c, 2 * px + py], dst_ref=o_ref.at[c, 2 * px + py], send_sem=send_sems.at[j],
            recv_sem=recv_sems.at[j], device_id=(x, y, 1 - c), device_id_type=MESH)
            for j, (px, py) in enumerate(_other_chips(x, y))]
        for cp in mine:
            cp.start()
        for j, (px, py) in enumerate(_other_chips(x, y)):
            pltpu.make_async_remote_copy(
                src_ref=o_ref.at[c, 2 * px + py], dst_ref=o_ref.at[1 - c, 2 * px + py], send_sem=send_sems.at[j],
                recv_sem=recv_sems.at[j], device_id=(x, y, 1 - c), device_id_type=MESH).wait_recv()
        for cp in mine:
            cp.wait_send()

    return pl.pallas_call(
        body, name=name,
        in_specs=[pl.BlockSpec(memory_space=pl.ANY)], out_specs=pl.BlockSpec(memory_space=pl.ANY),
        out_shape=jax.ShapeDtypeStruct(zone.shape, zone.dtype), input_output_aliases={0: 0},
        scratch_shapes=[pltpu.SemaphoreType.DMA((3,)), pltpu.SemaphoreType.DMA((3,))],
    )(zone)


def _sum_all_devices(own, received, *, name):
    def body(own_ref, got_ref, o_ref):
        x, y, c = _position()
        me = 4 * x + 2 * y + c
        acc = None
        for d in range(N_DEV):
            rel = jnp.bitwise_xor(me, d)
            block = jnp.where(rel == 0, own_ref[...], got_ref[jnp.maximum(rel, 1) - 1])
            acc = block if acc is None else acc + block
        o_ref[...] = acc

    return pl.pallas_call(
        body, name=name,
        in_specs=[pl.BlockSpec(memory_space=pltpu.VMEM)] * 2, out_specs=pl.BlockSpec(memory_space=pltpu.VMEM),
        out_shape=jax.ShapeDtypeStruct(own.shape, f32),
    )(own, received)


def _adam_math(w, g, m, v):
    nm = ADAM_B1 * m + (1.0 - ADAM_B1) * g
    nv = ADAM_B2 * v + (1.0 - ADAM_B2) * (g * g)
    m_hat = nm / (1.0 - ADAM_B1 ** ADAM_STEP)
    v_hat = nv / (1.0 - ADAM_B2 ** ADAM_STEP)
    return -ADAM_LR * (m_hat / (jnp.sqrt(v_hat) + ADAM_EPS) + ADAM_WD * w), nm, nv


def _adamw(w, g_parts, m, v, *, name, transposed=False):
    rows, cols = w.shape
    if transposed:
        tm = 256 if rows % 256 == 0 else rows
        g_spec = pl.BlockSpec((cols, tm), lambda i: (0, i))
    else:
        tm = _tile(rows, 256)
        g_spec = pl.BlockSpec((tm, cols), lambda i: (i, 0))
    n = len(g_parts)

    def body(*refs):
        w_ref, m_ref, v_ref = refs[0], refs[1 + n], refs[2 + n]
        g_ref, d_ref, nm_ref, nv_ref = refs[3 + n:]
        gv = refs[1][...]
        for part in refs[2:1 + n]:
            gv = gv + part[...]
        if transposed:
            gv = gv.T
        g_ref[...] = gv
        d_ref[...], nm_ref[...], nv_ref[...] = _adam_math(w_ref[...], gv, m_ref[...], v_ref[...])

    spec = pl.BlockSpec((tm, cols), lambda i: (i, 0))
    shape = jax.ShapeDtypeStruct((rows, cols), f32)
    return pl.pallas_call(
        body, name=name, grid=(rows // tm,), in_specs=[spec] + [g_spec] * n + [spec] * 2,
        out_specs=[spec] * 4, out_shape=[shape] * 4,
        compiler_params=_params(("parallel",)),
    )(w, *g_parts, m, v)


def _pad_rows(a, rows):
    return jnp.concatenate([a, jnp.zeros((rows - a.shape[0], a.shape[1]), a.dtype)], axis=0) if rows > a.shape[0] else a


_SMALL = (("norm_mix_g", D_MODEL), ("b_in", D_IN), ("sinks", Q_HEADS), ("mix", RWKV_PROJ), ("w0", RWKV_DIM),
          ("a0", RWKV_DIM), ("k_k", RWKV_DIM), ("k_a", RWKV_DIM), ("r_k", RWKV_DIM), ("ln_w", RWKV_DIM),
          ("ln_b", RWKV_DIM), ("norm_ffn_g", D_MODEL), ("norm_final_g", D_MODEL))


LANES = 128


def _small_layout():
    out, off = {}, 0
    for n, size in _SMALL + (("loss", 1),):
        pieces, col = [], 0
        while col < size:
            row, lane = divmod(off + col, PACK_W)
            width = min(size - col, PACK_W - lane)
            pieces.append((row, lane, width, col))
            col += width
        out[n] = pieces
        off += -(-size // LANES) * LANES
    return out, -(-off // PACK_W)


def _pack_small(d, loss):
    layout, rows = _small_layout()
    parts, used = [], 0
    for n, size in _SMALL + (("loss", 1),):
        item = loss if n == "loss" else d[n]
        fill = -size % LANES
        parts += [item.reshape(-1).astype(f32), jnp.zeros((fill,), f32)]
        used += size + fill
    parts.append(jnp.zeros((rows * PACK_W - used,), f32))
    return jnp.concatenate(parts).reshape(rows, PACK_W)


def _adamw_small(packed, first_row, ws, ms, vs, meta, *, name):
    layout, _ = _small_layout()
    names = [n for n, _ in _SMALL]
    k = len(names)

    def body(*refs):
        packed_ref = refs[0]
        w_refs, m_refs, v_refs = refs[1:1 + k], refs[1 + k:1 + 2 * k], refs[1 + 2 * k:1 + 3 * k]
        meta_refs = refs[1 + 3 * k:5 + 3 * k]
        outs = refs[5 + 3 * k:]
        for idx, n in enumerate(names):
            for row, lane, width, col in layout[n]:
                gv = packed_ref[first_row + row:first_row + row + 1, lane:lane + width]
                at = (slice(None), slice(col, col + width))
                new = _adam_math(w_refs[idx][at], gv, m_refs[idx][at], v_refs[idx][at])
                for o_ref, val in zip(outs[4 * idx:4 * idx + 4], (gv,) + new):
                    o_ref[at] = val
        for o_ref, val in zip(outs[4 * k:], _adam_math(*(r[...] for r in meta_refs))):
            o_ref[...] = val

    ins = [packed] + [d[n] for d in (ws, ms, vs) for n in names] + list(meta)
    shapes = [jax.ShapeDtypeStruct(ws[n].shape, f32) for n in names for _ in range(4)]
    shapes += [jax.ShapeDtypeStruct(meta[0].shape, f32)] * 3
    res = pl.pallas_call(
        body, name=name, grid=(1,), in_specs=[_full(a.shape) for a in ins],
        out_specs=[_full(s.shape) for s in shapes], out_shape=shapes,
        compiler_params=_params(("arbitrary",)),
    )(*ins)
    return {n: res[4 * i:4 * i + 4] for i, n in enumerate(names)}, res[4 * k:]


def kernel(x, meta_tokens, norm_mix_g, w_in, b_in, attn_sinks, rwkv_mix, rwkv_w0, rwkv_w2, rwkv_a0, rwkv_a2, rwkv_g2, rwkv_k_k, rwkv_k_a, rwkv_r_k, rwkv_ln_w, rwkv_ln_b, w_br_attn, w_br_rwkv, w_o, norm_ffn_g, w_ffn_gate, w_ffn_up, w_ffn_down, norm_final_g, loss_target, m_meta_tokens, m_norm_mix_g, m_w_in, m_b_in, m_attn_sinks, m_rwkv_mix, m_rwkv_w0, m_rwkv_w2, m_rwkv_a0, m_rwkv_a2, m_rwkv_g2, m_rwkv_k_k, m_rwkv_k_a, m_rwkv_r_k, m_rwkv_ln_w, m_rwkv_ln_b, m_w_br_attn, m_w_br_rwkv, m_w_o, m_norm_ffn_g, m_w_ffn_gate, m_w_ffn_up, m_w_ffn_down, m_norm_final_g, v_meta_tokens, v_norm_mix_g, v_w_in, v_b_in, v_attn_sinks, v_rwkv_mix, v_rwkv_w0, v_rwkv_w2, v_rwkv_a0, v_rwkv_a2, v_rwkv_g2, v_rwkv_k_k, v_rwkv_k_a, v_rwkv_r_k, v_rwkv_ln_w, v_rwkv_ln_b, v_w_br_attn, v_w_br_rwkv, v_w_o, v_norm_ffn_g, v_w_ffn_gate, v_w_ffn_up, v_w_ffn_down, v_norm_final_g):
    names = ("meta_tokens", "norm_mix_g", "w_in", "b_in", "attn_sinks", "rwkv_mix", "rwkv_w0", "rwkv_w2", "rwkv_a0",
             "rwkv_a2", "rwkv_g2", "rwkv_k_k", "rwkv_k_a", "rwkv_r_k", "rwkv_ln_w", "rwkv_ln_b", "w_br_attn",
             "w_br_rwkv", "w_o", "norm_ffn_g", "w_ffn_gate", "w_ffn_up", "w_ffn_down", "norm_final_g")
    w_all = dict(zip(names, (meta_tokens, norm_mix_g, w_in, b_in, attn_sinks, rwkv_mix, rwkv_w0, rwkv_w2, rwkv_a0,
                             rwkv_a2, rwkv_g2, rwkv_k_k, rwkv_k_a, rwkv_r_k, rwkv_ln_w, rwkv_ln_b, w_br_attn,
                             w_br_rwkv, w_o, norm_ffn_g, w_ffn_gate, w_ffn_up, w_ffn_down, norm_final_g)))
    m_all = dict(zip(names, (m_meta_tokens, m_norm_mix_g, m_w_in, m_b_in, m_attn_sinks, m_rwkv_mix, m_rwkv_w0,
                             m_rwkv_w2, m_rwkv_a0, m_rwkv_a2, m_rwkv_g2, m_rwkv_k_k, m_rwkv_k_a, m_rwkv_r_k,
                             m_rwkv_ln_w, m_rwkv_ln_b, m_w_br_attn, m_w_br_rwkv, m_w_o, m_norm_ffn_g, m_w_ffn_gate,
                             m_w_ffn_up, m_w_ffn_down, m_norm_final_g)))
    v_all = dict(zip(names, (v_meta_tokens, v_norm_mix_g, v_w_in, v_b_in, v_attn_sinks, v_rwkv_mix, v_rwkv_w0,
                             v_rwkv_w2, v_rwkv_a0, v_rwkv_a2, v_rwkv_g2, v_rwkv_k_k, v_rwkv_k_a, v_rwkv_r_k,
                             v_rwkv_ln_w, v_rwkv_ln_b, v_w_br_attn, v_w_br_rwkv, v_w_o, v_norm_ffn_g, v_w_ffn_gate,
                             v_w_ffn_up, v_w_ffn_down, v_norm_final_g)))
    cx, cy, _ = _position()
    chip = 2 * cx + cy

    t_of = dict(w_in_t="w_in", w_gate_t="w_ffn_gate", w_up_t="w_ffn_up", w_br_attn_t="w_br_attn",
                w_br_rwkv_t="w_br_rwkv", g2_t="rwkv_g2", w2_t="rwkv_w2", a2_t="rwkv_a2")
    plain_of = dict(w_down="w_ffn_down", w_o="w_o")
    meta_cols = meta_tokens.shape[1]

    def shard(k):
        return (w_all[t_of[k]][0].T if k in t_of else w_all[plain_of[k]][0]).astype(bf16)

    def whole(zone, own):
        return lax.dynamic_update_slice_in_dim(zone, own[None], chip, axis=0).reshape(-1, own.shape[-1])

    tiny = ("g2_t", "w2_t", "a2_t")
    late = ("w_gate_t", "w_up_t", "w_down", "w_o", "w_br_attn_t", "w_br_rwkv_t")
    w_in_own = shard("w_in_t")
    w_in_rows, w_in_cols = w_in_own.shape
    tiny_h = _exchange_start([shard(k) for k in tiny] + [meta_tokens], kind="whole", name="gather_tiny_start")
    w_in_h = _exchange_start([w_in_own + tiny_h[4][0, 0].astype(bf16)], kind="half", name="gather_w_in_start")
    behind = w_in_h[4][0, 0].astype(bf16)
    late_h = _exchange_start([shard(k) + behind for k in late], kind="whole", name="gather_late_start")
    own, zones = _exchange_wait(tiny_h, late_h[4], kind="whole", name="gather_tiny_wait")
    got = {k: whole(z, o) for k, z, o in zip(tiny, zones, own)}
    meta_full = whole(zones[-1], own[-1]).reshape(N_CHIPS, N_META, meta_cols).transpose(1, 0, 2).reshape(N_META, -1)
    p = dict(
        g2=got["g2_t"].T.astype(f32), w2=got["w2_t"].T.astype(f32), a2=got["a2_t"].T.astype(f32),
        b_in=b_in, sinks=attn_sinks, mix=rwkv_mix, w0=rwkv_w0, a0=rwkv_a0, k_k=rwkv_k_k, k_a=rwkv_k_a,
        r_k=rwkv_r_k.reshape(1, RWKV_DIM), ln_w=rwkv_ln_w, ln_b=rwkv_ln_b, norm_mix_g=norm_mix_g,
        norm_ffn_g=norm_ffn_g, norm_final_g=norm_final_g.reshape(1, D_MODEL),
    )

    def early_weights(after):
        own_h, zones_h = _exchange_wait(w_in_h, after, kind="half", name="gather_w_in_wait")
        zone = _swap_halves(zones_h[0], name="swap_w_in_halves")
        own_halves = own_h[0].reshape(w_in_rows, 2, w_in_cols // 2).transpose(1, 0, 2)[:, None]
        zone = lax.dynamic_update_slice(zone, own_halves, (0, chip, 0, 0))
        return dict(w_in_lr=zone.reshape(2, N_CHIPS * w_in_rows, w_in_cols // 2))

    def late_weights(after):
        own_l, zones_l = _exchange_wait(late_h, after, kind="whole", name="gather_late_wait")
        return {k: whole(z, o) for k, z, o in zip(late, zones_l, own_l)}

    started = {}

    def partial_sums(groups, after):
        parts = {}
        for group in groups:
            keys, handle = started[group]
            slabs, lands = _exchange_wait(handle, after, kind="slab", name="scatter_" + group + "_wait")
            parts.update({k: _sum_own_and_received(s, l, name="sum_chips_" + k) for k, s, l in zip(keys, slabs, lands)})
        return parts

    def emit(group, grads_):
        keys = list(grads_)
        slabs = []
        for k in keys:
            a = grads_[k].T if k in ("g2", "w2", "a2") else grads_[k]
            slabs.append(a.reshape(N_CHIPS, a.shape[0] // N_CHIPS, a.shape[1]))
        started[group] = (keys, _exchange_start(slabs, kind="slab", name="scatter_" + group + "_start"))
        zero = started[group][1][4]
        if group == "input":
            started["parts_a"] = partial_sums(("ffn", "branch"), zero)
            started["swap_a"] = _exchange_start(list(started["parts_a"].values()), kind="sibling",
                                                name="swap_cores_a_start")
            zero = started["swap_a"][4]
        return zero[0, 0]

    loss, dx, g = _local_step(x[0], loss_target[0], meta_full, p, early_weights, late_weights, emit)

    grads, delta, new_m, new_v = {}, {}, {}, {}
    in_grad_layout = ("w_in_t", "w_gate_t", "w_up_t")
    weight_of = {**t_of, **plain_of}

    def update(keys, mine, theirs):
        for k, part, other in zip(keys, mine, theirs):
            both = [part, other]
            k = k + "_t" if k in ("g2", "w2", "a2") else k
            n = weight_of[k]
            shape2 = w_all[n].shape[1:]
            w_, m_, v_ = (a.reshape(shape2) for a in (w_all[n], m_all[n], v_all[n]))
            if k in in_grad_layout:
                res = [t.T for t in _adamw(w_.T, both, m_.T, v_.T, name="adamw_" + n)]
            else:
                res = _adamw(w_, both, m_, v_, name="adamw_" + n, transposed=k in t_of)
            grads[n], delta[n], new_m[n], new_v[n] = (t.reshape(w_all[n].shape) for t in res)
        return delta[n]

    small = _pack_small(g, loss)
    small_rows8 = -(-(small.shape[0] + N_META) // 8) * 8
    small_h = _exchange_start([_pad_rows(jnp.concatenate([g["meta"], small], axis=0), small_rows8)], kind="all",
                              name="reduce_small_start")
    done = update(list(started["parts_a"]),
                  *_exchange_wait(started["swap_a"], small_h[4], kind="sibling", name="swap_cores_a_wait"))
    small_own, small_got = _exchange_wait(small_h, done, kind="all", name="reduce_small_wait")
    reduced = _sum_all_devices(small_own[0], small_got[0], name="reduce_small_sum")
    parts_b = partial_sums(("input",), reduced)
    update(list(parts_b), list(parts_b.values()), _swap_cores(list(parts_b.values()), name="swap_cores_b"))
    g_meta = lax.dynamic_slice_in_dim(reduced[:N_META], chip * meta_cols, meta_cols, axis=1)
    (loss_row, loss_lane, _, _), = _small_layout()[0]["loss"]
    loss_total = reduced[N_META + loss_row, loss_lane]

    small_of = dict(norm_mix_g="norm_mix_g", b_in="b_in", attn_sinks="sinks", rwkv_mix="mix", rwkv_w0="w0",
                    rwkv_a0="a0", rwkv_k_k="k_k", rwkv_k_a="k_a", rwkv_r_k="r_k", rwkv_ln_w="ln_w",
                    rwkv_ln_b="ln_b", norm_ffn_g="norm_ffn_g", norm_final_g="norm_final_g")
    as_rows = [{k: src[n].reshape(1, -1) for n, k in small_of.items()} for src in (w_all, m_all, v_all)]
    meta_in = (meta_tokens, g_meta, m_meta_tokens, v_meta_tokens)
    small_out, meta_out = _adamw_small(reduced, N_META, *as_rows, meta_in, name="adamw_small")
    grads["meta_tokens"] = g_meta
    delta["meta_tokens"], new_m["meta_tokens"], new_v["meta_tokens"] = meta_out
    for n, k in small_of.items():
        grads[n], delta[n], new_m[n], new_v[n] = (t.reshape(w_all[n].shape) for t in small_out[k])

    return (loss_total, dx.reshape(x.shape), *[grads[n] for n in names], *[delta[n] for n in names],
            *[new_m[n] for n in names], *[new_v[n] for n in names])
```

```python
import math

import jax
import jax.numpy as jnp
import numpy as np
from jax import lax
from jax.experimental import pallas as pl
from jax.experimental.pallas import tpu as pltpu

f32 = jnp.float32
bf16 = jnp.bfloat16

D_MODEL = 1024
N_META = 16
HEAD_DIM = 64
Q_HEADS = 8
KV_HEADS = 2
GROUP = Q_HEADS // KV_HEADS
WINDOW = 128
BLOCK = 128
ROPE_THETA = 500000.0
ROPE_DIM = HEAD_DIM // 4
RWKV_HEADS = 8
RWKV_HEAD = 64
RWKV_DIM = RWKV_HEADS * RWKV_HEAD
DECAY_LORA = 64
AAA_LORA = 64
GATE_LORA = 160
LORA_W = DECAY_LORA + AAA_LORA + GATE_LORA
RWKV_LN_EPS = 64e-5
D_FF = 2816
Q_W = Q_HEADS * HEAD_DIM
KV_W = KV_HEADS * HEAD_DIM
ATTN_PROJ = Q_W + 2 * KV_W
RKV_W = 3 * RWKV_DIM
RWKV_PROJ = RKV_W + LORA_W
D_IN = ATTN_PROJ + RWKV_PROJ + 2 * D_MODEL
RMS_EPS = 1e-6
NEG_INF = -1e30
PAD = BLOCK - N_META
FRONT = PAD + N_META

ADAM_LR = 0.001
ADAM_B1 = 0.9
ADAM_B2 = 0.999
ADAM_EPS = 1e-08
ADAM_WD = 0.01
ADAM_STEP = 10

N_CHIPS = 4
N_DEV = 8
CHUNK = 128
VMEM_LIMIT = 56 * 1024 * 1024
MM_ROWS = 704
PACK_W = 1024
MESH = pl.DeviceIdType.MESH


def _tile(m, pref=384):
    for step in (16, 8):
        for t in range(min(m, pref) // step * step, 0, -step):
            if m % t == 0:
                return t
    return m


def _params(sem=None):
    return pltpu.CompilerParams(dimension_semantics=sem, vmem_limit_bytes=VMEM_LIMIT)


def _full(shape):
    nd = len(shape)
    return pl.BlockSpec(shape, lambda *_: (0,) * nd)


def _dot(a, b, dims="nn"):
    dn = {"nn": (((1,), (0,)), ((), ())), "nt": (((1,), (1,)), ((), ())), "tn": (((0,), (0,)), ((), ()))}[dims]
    return lax.dot_general(a.astype(bf16), b.astype(bf16), dn, preferred_element_type=f32)


def _two_pass(x, m, dims="nn"):
    x_hi = x.astype(bf16)
    x_lo = (x - x_hi.astype(f32)).astype(bf16)
    return _dot(x_hi, m, dims) + _dot(x_lo, m, dims)


@jax.custom_vjp
def _dot_const(x, m):
    return _two_pass(x, m)


def _dot_const_fwd(x, m):
    return _two_pass(x, m), m


def _dot_const_bwd(m, ct):
    return _two_pass(ct, m, "nt"), jnp.zeros_like(m)


_dot_const.defvjp(_dot_const_fwd, _dot_const_bwd)


def _two_pass_left(m, x, dims):
    x_hi = x.astype(bf16)
    x_lo = (x - x_hi.astype(f32)).astype(bf16)
    return _dot(m, x_hi, dims) + _dot(m, x_lo, dims)


@jax.custom_vjp
def _const_dot(m, x):
    return _two_pass_left(m, x, "nn")


def _const_dot_fwd(m, x):
    return _two_pass_left(m, x, "nn"), m


def _const_dot_bwd(m, ct):
    return jnp.zeros_like(m), _two_pass_left(m, ct, "tn")


_const_dot.defvjp(_const_dot_fwd, _const_dot_bwd)


def _mm(a, b, mode, *, name, out_dtype=f32, bias=None, add=None, zero_rows_below=0):
    m, _ = a.shape
    n = b.shape[1] if mode == "nn" else b.shape[0]
    tm = _tile(m, MM_ROWS)
    has_bias, has_add = bias is not None, add is not None

    def body(*refs):
        a_ref, b_ref = refs[0], refs[1]
        o_ref = refs[-1]
        acc = _dot(a_ref[...], b_ref[...], mode)
        k = 2
        if has_bias:
            acc = acc + refs[k][...]
            k += 1
        if zero_rows_below:
            rows = pl.program_id(0) * tm + lax.broadcasted_iota(jnp.int32, acc.shape, 0)
            acc = jnp.where(rows >= zero_rows_below, acc, 0.0)
        if has_add:
            acc = acc + refs[k][...].astype(f32)
        o_ref[...] = acc.astype(out_dtype)

    ins = [a, b]
    in_specs = [pl.BlockSpec((tm, a.shape[1]), lambda i: (i, 0)), _full(b.shape)]
    if has_bias:
        ins.append(bias)
        in_specs.append(_full(bias.shape))
    if has_add:
        ins.append(add)
        in_specs.append(pl.BlockSpec((tm, n), lambda i: (i, 0)))
    return pl.pallas_call(
        body, name=name, grid=(m // tm,), in_specs=in_specs,
        out_specs=pl.BlockSpec((tm, n), lambda i: (i, 0)),
        out_shape=jax.ShapeDtypeStruct((m, n), out_dtype),
        compiler_params=_params(("parallel",)),
    )(*ins)


def _pieces(widths):
    out, off = [], 0
    for w in widths:
        out.append((off, w))
        off += w
    return out


def _proj_in(a, w_lr, bias, widths, rope, *, name, zero_rows_below=0):
    m, kdim = a.shape
    half = kdim // 2
    tm = _tile(m, MM_ROWS)
    cos_t, sin_t, swap = rope
    out_widths = [Q_W, KV_W, KV_W] + list(widths[1:])

    def body(a_ref, w_ref, b_ref, cos_ref, sin_ref, swap_ref, *outs):
        a_l, a_r = a_ref[:, :half], a_ref[:, half:]
        for j, (off, width) in enumerate(_pieces(widths)):
            acc = _dot(a_l, w_ref[0, off:off + width, :], "nt") + _dot(a_r, w_ref[1, off:off + width, :], "nt")
            acc = acc + b_ref[:, off:off + width]
            if zero_rows_below:
                rows = pl.program_id(0) * tm + lax.broadcasted_iota(jnp.int32, acc.shape, 0)
                acc = jnp.where(rows >= zero_rows_below, acc, 0.0)
            if j == 0:
                qkv = acc.astype(bf16).astype(f32)
                for o_ref, val in zip(outs[:3], _attn_prep(qkv, cos_ref[...], sin_ref[...], swap_ref[...])):
                    o_ref[...] = val.astype(o_ref.dtype)
            else:
                outs[2 + j][...] = acc.astype(outs[2 + j].dtype)

    table = pl.BlockSpec((tm, HEAD_DIM), lambda i: (i, 0))
    return pl.pallas_call(
        body, name=name, grid=(m // tm,),
        in_specs=[pl.BlockSpec((tm, kdim), lambda i: (i, 0)), _full(w_lr.shape), _full(bias.shape), table, table,
                  _full(swap.shape)],
        out_specs=[pl.BlockSpec((tm, w), lambda i: (i, 0)) for w in out_widths],
        out_shape=[jax.ShapeDtypeStruct((m, w), bf16) for w in out_widths],
        compiler_params=_params(("parallel",)),
    )(a, w_lr, bias, cos_t, sin_t, swap)


def _proj_in_bwd(d_list, w_lr, *, name):
    m = d_list[0].shape[0]
    half = w_lr.shape[2]
    widths = [d.shape[1] for d in d_list]
    tm = _tile(m, MM_ROWS)

    def body(*refs):
        w_ref, o_ref = refs[-2], refs[-1]
        for side in range(2):
            acc = None
            for (off, width), d_ref in zip(_pieces(widths), refs):
                term = _dot(d_ref[...], w_ref[side, off:off + width, :])
                acc = term if acc is None else acc + term
            o_ref[:, side * half:(side + 1) * half] = acc.astype(o_ref.dtype)

    return pl.pallas_call(
        body, name=name, grid=(m // tm,),
        in_specs=[pl.BlockSpec((tm, w), lambda i: (i, 0)) for w in widths] + [_full(w_lr.shape)],
        out_specs=pl.BlockSpec((tm, 2 * half), lambda i: (i, 0)),
        out_shape=jax.ShapeDtypeStruct((m, 2 * half), bf16),
        compiler_params=_params(("parallel",)),
    )(*d_list, w_lr)


def _residual_norm(a, w, res, g, *, name):
    m, d = res.shape
    tm = _tile(m, MM_ROWS)

    def body(a_ref, w_ref, r_ref, g_ref, h_ref, n_ref):
        h = _dot(a_ref[...], w_ref[...]) + r_ref[...]
        h_ref[...] = h
        n_ref[...] = _rms(h, g_ref[...]).astype(n_ref.dtype)

    tile = pl.BlockSpec((tm, d), lambda i: (i, 0))
    return pl.pallas_call(
        body, name=name, grid=(m // tm,),
        in_specs=[pl.BlockSpec((tm, a.shape[1]), lambda i: (i, 0)), _full(w.shape), tile, _full(g.shape)],
        out_specs=[tile, tile],
        out_shape=[jax.ShapeDtypeStruct((m, d), f32), jax.ShapeDtypeStruct((m, d), bf16)],
        compiler_params=_params(("parallel",)),
    )(a, w, res, g)


def _residual_norm_bwd(d_list, w_list, h, g, dh_out, *, name):
    m, d = h.shape
    k = len(d_list)
    tm = _tile(m)

    def body(*refs):
        h_ref, g_ref, dho_ref, dh_ref, dg_ref = refs[2 * k:]
        dn = _dot(refs[0][...], refs[k][...])
        for i in range(1, k):
            dn = dn + _dot(refs[i][...], refs[k + i][...])
        _, vjp = jax.vjp(lambda hv, gv: (_rms(hv, gv), hv), h_ref[...], g_ref[...])
        dh, dg = vjp((dn, dho_ref[...]))
        dh_ref[...] = dh

        @pl.when(pl.program_id(0) == 0)
        def _():
            dg_ref[...] = jnp.zeros_like(dg_ref)

        dg_ref[...] += dg

    tile = pl.BlockSpec((tm, d), lambda i: (i, 0))
    return pl.pallas_call(
        body, name=name, grid=(m // tm,),
        in_specs=[pl.BlockSpec((tm, a.shape[1]), lambda i: (i, 0)) for a in d_list] + [_full(w.shape) for w in w_list]
        + [tile, _full(g.shape), tile],
        out_specs=[tile, _full(g.shape)],
        out_shape=[jax.ShapeDtypeStruct((m, d), f32), jax.ShapeDtypeStruct(g.shape, f32)],
        compiler_params=_params(("arbitrary",)),
    )(*d_list, *w_list, h, g, dh_out)


def _mm_tn(a, b, *, name, colsum=False, out_dtype=bf16, into=None):
    r, m = a.shape
    n = b.shape[1]
    tr = _tile(r, 1408)
    tmo = m
    for cand in (1408, 1024, 768, 512):
        if m > 1024 and m % cand == 0:
            tmo = cand
            break
    steps = r // tr

    rows, offset, target = into or (m, 0, None)

    def body(a_ref, b_ref, *rest):
        o_ref, rest = (rest[0], rest[1:]) if target is None else (rest[1], rest[2:])
        acc = rest[-1]
        i = pl.program_id(1)

        @pl.when(i == 0)
        def _():
            acc[...] = jnp.zeros_like(acc)
            if colsum:
                rest[0][...] = jnp.zeros_like(rest[0])

        acc[...] += _dot(a_ref[...], b_ref[...], "tn")
        if colsum:
            rest[0][...] += jnp.sum(a_ref[...].astype(f32), axis=0, keepdims=True)

        @pl.when(i == steps - 1)
        def _():
            o_ref[...] = acc[...].astype(out_dtype)

    out_shape = [jax.ShapeDtypeStruct((rows, n), out_dtype)]
    if offset % tmo == 0:
        out_specs = [pl.BlockSpec((tmo, n), lambda j, i: (offset // tmo + j, 0))]
    else:
        out_specs = [pl.BlockSpec((pl.Element(tmo), pl.Element(n)), lambda j, i: (pl.multiple_of(offset + j * tmo, math.gcd(offset, tmo)), 0))]
    if colsum:
        out_shape.append(jax.ShapeDtypeStruct((1, m), f32))
        out_specs.append(pl.BlockSpec((1, tmo), lambda j, i: (0, j)))
    in_specs = [pl.BlockSpec((tr, tmo), lambda j, i: (i, j)), pl.BlockSpec((tr, n), lambda j, i: (i, 0))]
    res = pl.pallas_call(
        body, name=name, grid=(m // tmo, steps),
        in_specs=in_specs + ([] if target is None else [pl.BlockSpec(memory_space=pl.ANY)]),
        out_specs=out_specs, out_shape=out_shape,
        scratch_shapes=[pltpu.VMEM((tmo, n), f32)],
        input_output_aliases={} if target is None else {2: 0},
        compiler_params=_params(("parallel", "arbitrary")),
    )(a, b, *([] if target is None else [target]))
    return res if colsum else res[0]


def _rowwise(fn, rows, params, outs, *, name, tm=None):
    m = rows[0].shape[0]
    tm = tm or _tile(m, MM_ROWS)
    nr, npar = len(rows), len(params)

    def body(*refs):
        vals = [r[...] for r in refs[:nr + npar]]
        res = fn(*vals)
        for o_ref, v in zip(refs[nr + npar:], res):
            o_ref[...] = v.astype(o_ref.dtype)

    return pl.pallas_call(
        body, name=name, grid=(m // tm,),
        in_specs=[pl.BlockSpec((tm, r.shape[1]), lambda i: (i, 0)) for r in rows] + [_full(p.shape) for p in params],
        out_specs=[pl.BlockSpec((tm, w), lambda i: (i, 0)) for w, _ in outs],
        out_shape=[jax.ShapeDtypeStruct((m, w), dt) for w, dt in outs],
        compiler_params=_params(("parallel",)),
    )(*rows, *params)


def _rowwise_bwd(fn, rows, params, cts, *, name, diff_rows, diff_params, tm=None, zero_rows_below=0, out_dtypes=None):
    m = rows[0].shape[0]
    tm = tm or _tile(m)
    nr, npar = len(rows), len(params)
    d_idx = [i for i in range(nr) if diff_rows[i]]
    p_idx = [i for i in range(npar) if diff_params[i]]
    out_dtypes = out_dtypes or [f32] * len(d_idx)
    flat_cts = [c for group in cts for c in group]
    n_ct = len(flat_cts)

    def body(*refs):
        vals = [r[...] for r in refs[:nr + npar]]
        ct_refs = refs[nr + npar:nr + npar + n_ct]
        out_refs = refs[nr + npar + n_ct:]
        ct_vals, k = [], 0
        for group in cts:
            acc = ct_refs[k][...].astype(f32)
            for extra in range(1, len(group)):
                acc = acc + ct_refs[k + extra][...].astype(f32)
            k += len(group)
            if zero_rows_below:
                rr = pl.program_id(0) * tm + lax.broadcasted_iota(jnp.int32, acc.shape, 0)
                acc = jnp.where(rr >= zero_rows_below, acc, 0.0)
            ct_vals.append(acc)

        def g(*dargs):
            full = list(vals)
            for pos, i in enumerate(d_idx):
                full[i] = dargs[pos]
            for pos, i in enumerate(p_idx):
                full[nr + i] = dargs[len(d_idx) + pos]
            return tuple(fn(*full))

        _, vjp = jax.vjp(g, *[vals[i].astype(f32) for i in d_idx], *[vals[nr + i] for i in p_idx])
        grads = vjp(tuple(ct_vals))
        for pos in range(len(d_idx)):
            out_refs[pos][...] = grads[pos].astype(out_refs[pos].dtype)
        first = pl.program_id(0) == 0
        for pos in range(len(p_idx)):
            o_ref = out_refs[len(d_idx) + pos]

            @pl.when(first)
            def _(o_ref=o_ref):
                o_ref[...] = jnp.zeros_like(o_ref)

            o_ref[...] += grads[len(d_idx) + pos]

    return pl.pallas_call(
        body, name=name, grid=(m // tm,),
        in_specs=[pl.BlockSpec((tm, r.shape[1]), lambda i: (i, 0)) for r in rows] + [_full(p.shape) for p in params]
        + [pl.BlockSpec((tm, c.shape[1]), lambda i: (i, 0)) for c in flat_cts],
        out_specs=[pl.BlockSpec((tm, rows[i].shape[1]), lambda i_: (i_, 0)) for i in d_idx]
        + [_full(params[i].shape) for i in p_idx],
        out_shape=[jax.ShapeDtypeStruct(rows[i].shape, dt) for i, dt in zip(d_idx, out_dtypes)]
        + [jax.ShapeDtypeStruct(params[i].shape, f32) for i in p_idx],
        compiler_params=_params(("arbitrary",)),
    )(*rows, *params, *flat_cts)


def _rms(x, g):
    return x * lax.rsqrt(jnp.mean(x * x, axis=-1, keepdims=True) + RMS_EPS) * g


def _head_sum_matrix(width, head):
    idx = jnp.arange(width) // head
    return (idx[:, None] == idx[None, :]).astype(f32)


def _rope_tables(lp):
    half = ROPE_DIM // 2
    pos = (np.arange(lp) - PAD).astype(np.float32)
    inv_freq = np.power(np.float32(ROPE_THETA), -np.arange(half, dtype=np.float32) * np.float32(2.0 / ROPE_DIM))
    ang = pos[:, None] * inv_freq[None, :].astype(np.float32)
    cos, sin = np.cos(ang), np.sin(ang)
    ones = np.ones((lp, HEAD_DIM - ROPE_DIM), np.float32)
    cos_t = np.concatenate([cos, cos, ones], axis=1)
    sin_t = np.concatenate([-sin, sin, 0.0 * ones], axis=1)
    i = np.arange(HEAD_DIM)
    src = np.where(i < half, i + half, np.where(i < ROPE_DIM, i - half, i))
    swap = ((i[:, None] == src[None, :]) & (i[None, :] < ROPE_DIM)).astype(np.float32)
    return jnp.asarray(cos_t, f32), jnp.asarray(sin_t, f32), jnp.asarray(swap, f32)


def _attn_prep(qkv, cos_t, sin_t, swap):
    outs = []
    for h in range(Q_HEADS + KV_HEADS):
        t = qkv[:, h * HEAD_DIM:(h + 1) * HEAD_DIM]
        outs.append(t * cos_t + _dot_const(t, swap) * sin_t)
    q = jnp.concatenate(outs[:Q_HEADS], axis=1)
    k = jnp.concatenate(outs[Q_HEADS:], axis=1)
    return q, k, qkv[:, Q_W + KV_W:]


def _attn_prep_transposed(dq, dk, dv, cos_t, sin_t, swap, dk_meta, dv_meta):
    first = pl.program_id(0) == 0

    def with_meta(d, d_meta):
        rest = jnp.zeros((d.shape[0] - BLOCK, KV_W), f32)
        return d.astype(f32) + jnp.where(first, jnp.concatenate([d_meta, rest], axis=0), 0.0)

    parts = []
    for d, heads in ((dq.astype(f32), Q_HEADS), (with_meta(dk, dk_meta), KV_HEADS)):
        for h in range(heads):
            t = d[:, h * HEAD_DIM:(h + 1) * HEAD_DIM]
            parts.append(t * cos_t + _two_pass(t * sin_t, swap, "nt"))
    return (jnp.concatenate(parts + [with_meta(dv, dv_meta)], axis=1),)


def _softplus(z):
    return jnp.maximum(z, 0.0) + jnp.log1p(jnp.exp(-jnp.abs(z)))


def _rwkv_prep(rkv, lora, w0, w2, a0, a2, g2, k_k, k_a, hsum):
    r = rkv[:, :RWKV_DIM]
    k = rkv[:, RWKV_DIM:2 * RWKV_DIM]
    v = rkv[:, 2 * RWKV_DIM:]
    dw = lora[:, :DECAY_LORA]
    da = lora[:, DECAY_LORA:DECAY_LORA + AAA_LORA]
    dg = lora[:, DECAY_LORA + AAA_LORA:]
    w = -_softplus(-(w0 + _dot(jnp.tanh(dw), w2))) - 0.5
    a = jax.nn.sigmoid(a0 + _dot(da, a2))
    g = _dot(jax.nn.sigmoid(dg), g2)
    kk = k * k_k
    kk = kk * lax.rsqrt(jnp.maximum(_dot_const(kk * kk, hsum), 1e-24))
    k = k * (1.0 + (a - 1.0) * k_a)
    log_decay = -jnp.exp(w)
    return r, log_decay, k, v, -kk, kk * a, g


def _rwkv_post(y, r, k, v, g, ln_w, ln_b, r_k, hmean):
    hsum = hmean * RWKV_HEAD
    mean = _dot_const(y, hmean)
    yc = y - mean
    var = _dot_const(yc * yc, hmean)
    yn = yc * lax.rsqrt(var + RWKV_LN_EPS) * ln_w + ln_b
    bonus = _dot_const(r * k * r_k, hsum) * v
    return ((yn + bonus) * g,)


def _merge(gates, br_a, br_r):
    sg = jax.nn.sigmoid(gates)
    return (sg[:, :D_MODEL] * br_a + sg[:, D_MODEL:] * br_r,)


def _swiglu(gate, up):
    return (jax.nn.silu(gate) * up,)


def _ffn_in(f, w_gate_t, w_up_t, *, name):
    m, d = f.shape
    n = w_gate_t.shape[0]
    tm = _tile(m)

    def body(f_ref, wg_ref, wu_ref, g_ref, u_ref, a_ref):
        g = _dot(f_ref[...], wg_ref[...], "nt")
        u = _dot(f_ref[...], wu_ref[...], "nt")
        g_ref[...] = g.astype(g_ref.dtype)
        u_ref[...] = u.astype(u_ref.dtype)
        a_ref[...] = _swiglu(g, u)[0].astype(a_ref.dtype)

    spec = pl.BlockSpec((tm, n), lambda i: (i, 0))
    return pl.pallas_call(
        body, name=name, grid=(m // tm,),
        in_specs=[pl.BlockSpec((tm, d), lambda i: (i, 0)), _full(w_gate_t.shape), _full(w_up_t.shape)],
        out_specs=[spec] * 3, out_shape=[jax.ShapeDtypeStruct((m, n), bf16)] * 3,
        compiler_params=_params(("parallel",)),
    )(f, w_gate_t, w_up_t)


def _branch_merge(y_attn, y_rwkv, w_attn_t, w_rwkv_t, gates, *, name):
    m = y_attn.shape[0]
    tm = _tile(m, MM_ROWS)

    def body(ya_ref, yr_ref, wa_ref, wr_ref, g_ref, a_ref, r_ref, o_ref):
        br_a = _dot(ya_ref[...], wa_ref[...], "nt")
        br_r = _dot(yr_ref[...], wr_ref[...], "nt")
        a_ref[...] = br_a.astype(a_ref.dtype)
        r_ref[...] = br_r.astype(r_ref.dtype)
        o_ref[...] = _merge(g_ref[...].astype(f32), br_a, br_r)[0].astype(o_ref.dtype)

    rows = lambda a: pl.BlockSpec((tm, a.shape[1]), lambda i: (i, 0))
    spec = pl.BlockSpec((tm, D_MODEL), lambda i: (i, 0))
    return pl.pallas_call(
        body, name=name, grid=(m // tm,),
        in_specs=[rows(y_attn), rows(y_rwkv), _full(w_attn_t.shape), _full(w_rwkv_t.shape), rows(gates)],
        out_specs=[spec] * 3, out_shape=[jax.ShapeDtypeStruct((m, D_MODEL), bf16)] * 3,
        compiler_params=_params(("parallel",)),
    )(y_attn, y_rwkv, w_attn_t, w_rwkv_t, gates)


def _branch_merge_bwd(dh, w_o, gates, br_a, br_r, w_attn_t, w_rwkv_t, *, name):
    m = dh.shape[0]
    tm = _tile(m, MM_ROWS)

    def body(dh_ref, w_ref, g_ref, a_ref, r_ref, wa_ref, wr_ref, dg_ref, da_ref, dr_ref, dya_ref, dyr_ref):
        dmerged = _dot(dh_ref[...], w_ref[...], "nt")
        _, vjp = jax.vjp(lambda g, a, r: _merge(g, a, r)[0], g_ref[...].astype(f32), a_ref[...].astype(f32),
                         r_ref[...].astype(f32))
        dg, da, dr = vjp(dmerged)
        dg_ref[...] = dg.astype(dg_ref.dtype)
        da_ref[...] = da.astype(da_ref.dtype)
        dr_ref[...] = dr.astype(dr_ref.dtype)
        dya_ref[...] = _dot(da, wa_ref[...])
        dyr_ref[...] = _dot(dr, wr_ref[...])

    rows = lambda a: pl.BlockSpec((tm, a.shape[1]), lambda i: (i, 0))
    mixer = pl.BlockSpec((tm, w_attn_t.shape[1]), lambda i: (i, 0))
    return pl.pallas_call(
        body, name=name, grid=(m // tm,),
        in_specs=[rows(dh), _full(w_o.shape), rows(gates), rows(br_a), rows(br_r), _full(w_attn_t.shape),
                  _full(w_rwkv_t.shape)],
        out_specs=[rows(gates), rows(br_a), rows(br_r), mixer, mixer],
        out_shape=[jax.ShapeDtypeStruct(gates.shape, bf16), jax.ShapeDtypeStruct(br_a.shape, bf16),
                   jax.ShapeDtypeStruct(br_r.shape, bf16), jax.ShapeDtypeStruct((m, w_attn_t.shape[1]), f32),
                   jax.ShapeDtypeStruct((m, w_rwkv_t.shape[1]), f32)],
        compiler_params=_params(("parallel",)),
    )(dh, w_o, gates, br_a, br_r, w_attn_t, w_rwkv_t)


def _ffn_in_bwd(dh, w_down, gate, up, *, name):
    m, d = dh.shape
    n = w_down.shape[0]
    tm = _tile(m)

    def body(dh_ref, w_ref, g_ref, u_ref, dg_ref, du_ref):
        dact = _dot(dh_ref[...], w_ref[...], "nt")
        _, vjp = jax.vjp(lambda a, b: _swiglu(a, b)[0], g_ref[...].astype(f32), u_ref[...].astype(f32))
        dg, du = vjp(dact)
        dg_ref[...] = dg.astype(dg_ref.dtype)
        du_ref[...] = du.astype(du_ref.dtype)

    spec = pl.BlockSpec((tm, n), lambda i: (i, 0))
    return pl.pallas_call(
        body, name=name, grid=(m // tm,),
        in_specs=[pl.BlockSpec((tm, d), lambda i: (i, 0)), _full(w_down.shape), spec, spec],
        out_specs=[spec] * 2, out_shape=[jax.ShapeDtypeStruct((m, n), bf16)] * 2,
        compiler_params=_params(("parallel",)),
    )(dh, w_down, gate, up)


def _ffn_bwd(dh, w_down, gate, up, w_gate_t, w_up_t, h, g, *, name):
    m, d = dh.shape
    n = w_down.shape[0]
    tm = _tile(m)
    halves = [(c * (n // 2), (c + 1) * (n // 2)) for c in range(2)]

    def body(dh_ref, wd_ref, g_ref, u_ref, wg_ref, wu_ref, h_ref, gn_ref, dg_ref, du_ref, dh1_ref, dgn_ref):
        dn = jnp.zeros((tm, d), f32)
        for lo, hi in halves:
            dact = _dot(dh_ref[...], wd_ref[lo:hi, :], "nt")
            _, vjp = jax.vjp(lambda a, b: _swiglu(a, b)[0], g_ref[:, lo:hi].astype(f32), u_ref[:, lo:hi].astype(f32))
            dg, du = (t.astype(bf16) for t in vjp(dact))
            dg_ref[:, lo:hi] = dg
            du_ref[:, lo:hi] = du
            dn = dn + _dot(dg, wg_ref[lo:hi, :]) + _dot(du, wu_ref[lo:hi, :])
        _, vjp = jax.vjp(lambda hv, gv: (_rms(hv, gv), hv), h_ref[...], gn_ref[...])
        dh1, dgn = vjp((dn, dh_ref[...]))
        dh1_ref[...] = dh1

        @pl.when(pl.program_id(0) == 0)
        def _():
            dgn_ref[...] = jnp.zeros_like(dgn_ref)

        dgn_ref[...] += dgn

    wide = pl.BlockSpec((tm, n), lambda i: (i, 0))
    tile = pl.BlockSpec((tm, d), lambda i: (i, 0))
    return pl.pallas_call(
        body, name=name, grid=(m // tm,),
        in_specs=[tile, _full(w_down.shape), wide, wide, _full(w_gate_t.shape), _full(w_up_t.shape), tile,
                  _full(g.shape)],
        out_specs=[wide, wide, tile, _full(g.shape)],
        out_shape=[jax.ShapeDtypeStruct((m, n), bf16)] * 2 + [jax.ShapeDtypeStruct((m, d), f32),
                                                               jax.ShapeDtypeStruct(g.shape, f32)],
        compiler_params=_params(("arbitrary",)),
    )(dh, w_down, gate, up, w_gate_t, w_up_t, h, g)


HALO = 16


def _previous_rows(x, before_ref, first_tile):
    rows = lax.broadcasted_iota(jnp.int32, x.shape, 0)
    last = jnp.where(first_tile, 0.0, before_ref[HALO - 1:HALO, :].astype(f32))
    return jnp.where(rows == 0, last, pltpu.roll(x, 1, axis=0))


def _mixer_inputs(ps, mixes, params, *, name):
    m = ps[0].shape[0]
    tm = _tile(m)
    sub = tm // HALO
    n_par = len(params)

    def body(*refs):
        first = pl.program_id(0) == 0
        pf = []
        for k in range(2):
            x = refs[k][...].astype(f32)
            pf.append(x + (_previous_rows(x, refs[2 + k], first) - x) * refs[4 + k][...])
        res = _rwkv_prep(*pf, *[ref[...] for ref in refs[6:6 + n_par]])
        for o_ref, val in zip(refs[6 + n_par:], res):
            o_ref[...] = val

    tile = lambda a: pl.BlockSpec((tm, a.shape[1]), lambda i: (i, 0))
    before = lambda a: pl.BlockSpec((HALO, a.shape[1]), lambda i: (jnp.maximum(i * sub - 1, 0), 0))
    out = pl.BlockSpec((tm, RWKV_DIM), lambda i: (i, 0))
    return pl.pallas_call(
        body, name=name, grid=(m // tm,),
        in_specs=[tile(a) for a in ps] + [before(a) for a in ps] + [_full(a.shape) for a in mixes + params],
        out_specs=[out] * 7, out_shape=[jax.ShapeDtypeStruct((m, RWKV_DIM), f32)] * 7,
        compiler_params=_params(("parallel",)),
    )(*ps, *ps, *mixes, *params)


def _mixer_inputs_bwd(ps, mixes, params, cts, *, name):
    m = ps[0].shape[0]
    tm = _tile(m)
    sub = tm // HALO
    nt = m // tm
    n_par = len(params)
    flat_cts = [c for group in cts for c in group]
    n_ct = len(flat_cts)

    def body(*refs):
        i = pl.program_id(0)
        tile_index = nt - 1 - i
        ct_refs = refs[6 + n_par:6 + n_par + n_ct]
        dp_refs = refs[6 + n_par + n_ct:8 + n_par + n_ct]
        dmix_refs = refs[8 + n_par + n_ct:10 + n_par + n_ct]
        dpar_refs = refs[10 + n_par + n_ct:9 + 2 * n_par + n_ct]
        carries = refs[9 + 2 * n_par + n_ct:]
        rows1 = tile_index * tm + lax.broadcasted_iota(jnp.int32, (tm, 1), 0)
        live = rows1 >= PAD

        @pl.when(i == 0)
        def _():
            for ref in (*dmix_refs, *dpar_refs, *carries):
                ref[...] = jnp.zeros_like(ref)

        xs, prevs, pf = [], [], []
        for k in range(2):
            x = refs[k][...].astype(f32)
            xp = _previous_rows(x, refs[2 + k], tile_index == 0)
            xs.append(x)
            prevs.append(xp)
            pf.append(x + (xp - x) * refs[4 + k][...])
        ct_vals, pos = [], 0
        for group in cts:
            acc = ct_refs[pos][...].astype(f32)
            for extra in range(1, len(group)):
                acc = acc + ct_refs[pos + extra][...].astype(f32)
            pos += len(group)
            ct_vals.append(jnp.where(live, acc, 0.0))
        par_vals = [ref[...] for ref in refs[6:6 + n_par]]
        _, vjp = jax.vjp(lambda *args: _rwkv_prep(*args, par_vals[-1]), *pf, *par_vals[:-1])
        g = vjp(tuple(ct_vals))
        for k in range(2):
            dpf = g[k]
            mixv = refs[4 + k][...]
            dm = dpf * mixv
            rows = lax.broadcasted_iota(jnp.int32, dm.shape, 0)
            dm_next = jnp.where(rows == tm - 1, carries[k][...], pltpu.roll(dm, tm - 1, axis=0))
            dp_refs[k][...] = jnp.where(live, dpf - dm + dm_next, 0.0).astype(dp_refs[k].dtype)
            carries[k][...] = dm[0:1, :]
            dmix_refs[k][...] += jnp.sum(dpf * (prevs[k] - xs[k]), axis=0, keepdims=True)
        for ref, val in zip(dpar_refs, g[2:]):
            ref[...] += val

    tile = lambda a: pl.BlockSpec((tm, a.shape[1]), lambda i: (nt - 1 - i, 0))
    before = lambda a: pl.BlockSpec((HALO, a.shape[1]), lambda i: (jnp.maximum((nt - 1 - i) * sub - 1, 0), 0))
    return pl.pallas_call(
        body, name=name, grid=(nt,),
        in_specs=[tile(a) for a in ps] + [before(a) for a in ps] + [_full(a.shape) for a in mixes + params]
        + [tile(c) for c in flat_cts],
        out_specs=[tile(a) for a in ps] + [_full(a.shape) for a in mixes + params[:-1]],
        out_shape=[jax.ShapeDtypeStruct(a.shape, bf16) for a in ps]
        + [jax.ShapeDtypeStruct(a.shape, f32) for a in mixes + params[:-1]],
        scratch_shapes=[pltpu.VMEM((1, a.shape[1]), f32) for a in ps],
        compiler_params=_params(("arbitrary",)),
    )(*ps, *ps, *mixes, *params, *flat_cts)


def _attn_masks(blk):
    qi = lax.broadcasted_iota(jnp.int32, (BLOCK, BLOCK), 0)
    ki = lax.broadcasted_iota(jnp.int32, (BLOCK, BLOCK), 1)
    qpos = blk * BLOCK + qi - PAD
    kpos_c = blk * BLOCK + ki - PAD
    kpos_p = kpos_c - BLOCK
    kpos_m = ki - PAD

    def band(kpos):
        return (kpos >= N_META) & (kpos <= qpos) & (qpos - kpos < WINDOW)

    return band(kpos_p), band(kpos_c), (kpos_m >= 0) & (kpos_m <= qpos)


def _attn_probs(qs, k3s, sink, oks):
    s = [[jnp.where(ok, _dot(qh, kx, "nt"), NEG_INF) for kx, ok in zip(k3, oks)] for qh, k3 in zip(qs, k3s)]
    mx = [jnp.maximum(jnp.maximum(jnp.max(t[0], -1, keepdims=True), jnp.max(t[1], -1, keepdims=True)),
                      jnp.maximum(jnp.max(t[2], -1, keepdims=True), sk)) for t, sk in zip(s, sink)]
    e = [[jnp.exp(tx - m) for tx in t] for t, m in zip(s, mx)]
    e_sink = [jnp.exp(sk - m) for sk, m in zip(sink, mx)]
    inv = [1.0 / (jnp.sum(t[0], -1, keepdims=True) + jnp.sum(t[1], -1, keepdims=True)
                  + jnp.sum(t[2], -1, keepdims=True) + es) for t, es in zip(e, e_sink)]
    return [[tx * i for tx in t] for t, i in zip(e, inv)], [es * i for es, i in zip(e_sink, inv)]


def _head_cols(i):
    return slice(i * HEAD_DIM, (i + 1) * HEAD_DIM)


def _attn_operands(refs):
    q_ref, kp_ref, kc_ref, km_ref, vp_ref, vc_ref, vm_ref, s_ref = refs
    qs = [q_ref[:, _head_cols(i)] * (HEAD_DIM ** -0.5) for i in range(Q_HEADS)]
    k3 = [[ref[:, _head_cols(h)] for ref in (kp_ref, kc_ref, km_ref)] for h in range(KV_HEADS)]
    v3 = [[ref[:, _head_cols(h)] for ref in (vp_ref, vc_ref, vm_ref)] for h in range(KV_HEADS)]
    return (qs, [k3[i // GROUP] for i in range(Q_HEADS)], [v3[i // GROUP] for i in range(Q_HEADS)],
            [s_ref[:, i:i + 1] for i in range(Q_HEADS)])


def _attention(q, k, v, sinks, *, name):
    lp = q.shape[0]
    nb = lp // BLOCK
    prev = lambda i: (jnp.maximum(i - 1, 0), 0)
    cur = lambda i: (i, 0)
    meta = lambda i: (0, 0)
    kv = lambda index: pl.BlockSpec((BLOCK, KV_W), index)

    def body(*refs):
        o_ref = refs[-1]
        qs, k3s, v3s, sink = _attn_operands(refs[:-1])
        p, _ = _attn_probs(qs, k3s, sink, _attn_masks(pl.program_id(0)))
        out = [_dot(ph[0], v3[0]) + _dot(ph[1], v3[1]) + _dot(ph[2], v3[2]) for ph, v3 in zip(p, v3s)]
        for i in range(Q_HEADS):
            o_ref[:, _head_cols(i)] = out[i].astype(o_ref.dtype)

    return pl.pallas_call(
        body, name=name, grid=(nb,),
        in_specs=[pl.BlockSpec((BLOCK, Q_W), cur), kv(prev), kv(cur), kv(meta), kv(prev), kv(cur), kv(meta),
                  _full((1, Q_HEADS))],
        out_specs=pl.BlockSpec((BLOCK, Q_W), cur),
        out_shape=jax.ShapeDtypeStruct((lp, Q_W), bf16),
        compiler_params=_params(("parallel",)),
    )(q, k, k, k, v, v, v, sinks)


def _attention_bwd(q, k, v, sinks, out, do, *, name):
    lp = q.shape[0]
    nb = lp // BLOCK
    cur = lambda n: (jnp.minimum(n, nb - 1), 0)
    prev = lambda n: (jnp.maximum(jnp.minimum(n, nb - 1) - 1, 0), 0)
    behind = lambda n: (jnp.maximum(n - 1, 0), 0)
    meta = lambda n: (0, 0)
    kv = lambda index: pl.BlockSpec((BLOCK, KV_W), index)
    scale = HEAD_DIM ** -0.5

    def body(*refs):
        ins, fwd_ref, do_ref = refs[:8], refs[8], refs[9]
        dq_ref, dk_ref, dv_ref, dkm_ref, dvm_ref, ds_ref, carry_k, carry_v = refs[10:]
        n = pl.program_id(0)

        @pl.when(n == 0)
        def _():
            for ref in (dkm_ref, dvm_ref, ds_ref, carry_k, carry_v):
                ref[...] = jnp.zeros_like(ref)

        @pl.when(n < nb)
        def _():
            qs, k3s, v3s, sink = _attn_operands(ins)
            do = [do_ref[:, _head_cols(i)] for i in range(Q_HEADS)]
            p, p_sink = _attn_probs(qs, k3s, sink, _attn_masks(n))
            delta = [jnp.sum(d * fwd_ref[:, _head_cols(i)].astype(f32), -1, keepdims=True) for i, d in enumerate(do)]
            dp = [[_dot(d, vx, "nt") for vx in v3] for d, v3 in zip(do, v3s)]
            ds = [[px * (dx - dl) for px, dx in zip(ph, dh)] for ph, dh, dl in zip(p, dp, delta)]
            dq = [_dot(dsh[0], k3[0]) + _dot(dsh[1], k3[1]) + _dot(dsh[2], k3[2]) for dsh, k3 in zip(ds, k3s)]
            for i in range(Q_HEADS):
                dq_ref[:, _head_cols(i)] = dq[i] * scale
                ds_ref[:, i:i + 1] -= jnp.sum(p_sink[i] * delta[i], axis=0, keepdims=True)
            for h in range(KV_HEADS):
                group = slice(h * GROUP, (h + 1) * GROUP)
                q_all = jnp.concatenate(qs[group], axis=0)
                do_all = jnp.concatenate(do[group], axis=0)
                dk3 = [_dot(jnp.concatenate([dsh[x] for dsh in ds[group]], axis=0), q_all, "tn") for x in range(3)]
                dv3 = [_dot(jnp.concatenate([ph[x] for ph in p[group]], axis=0), do_all, "tn") for x in range(3)]
                hs = _head_cols(h)
                for out_ref, carry, meta_ref, d3 in ((dk_ref, carry_k, dkm_ref, dk3),
                                                     (dv_ref, carry_v, dvm_ref, dv3)):
                    out_ref[:, hs] = carry[:, hs] + d3[0]
                    carry[:, hs] = d3[1]
                    meta_ref[:, hs] += d3[2]

        @pl.when(n == nb)
        def _():
            dk_ref[...] = carry_k[...]
            dv_ref[...] = carry_v[...]

    kv_shape = jax.ShapeDtypeStruct((lp, KV_W), f32)
    one_shape = jax.ShapeDtypeStruct((BLOCK, KV_W), f32)
    return pl.pallas_call(
        body, name=name, grid=(nb + 1,),
        in_specs=[pl.BlockSpec((BLOCK, Q_W), cur), kv(prev), kv(cur), kv(meta), kv(prev), kv(cur), kv(meta),
                  _full((1, Q_HEADS)), pl.BlockSpec((BLOCK, Q_W), cur), pl.BlockSpec((BLOCK, Q_W), cur)],
        out_specs=[pl.BlockSpec((BLOCK, Q_W), cur), kv(behind), kv(behind), kv(meta), kv(meta),
                   _full((1, Q_HEADS))],
        out_shape=[jax.ShapeDtypeStruct((lp, Q_W), f32), kv_shape, kv_shape, one_shape, one_shape,
                   jax.ShapeDtypeStruct((1, Q_HEADS), f32)],
        scratch_shapes=[pltpu.VMEM((BLOCK, KV_W), f32), pltpu.VMEM((BLOCK, KV_W), f32)],
        compiler_params=_params(("arbitrary",)),
    )(q, k, k, k, v, v, v, sinks, out, do)


@jax.custom_vjp
def _known_inverse(l, x):
    return x


def _known_inverse_fwd(l, x):
    return x, x


def _known_inverse_bwd(x, ct):
    return _dot(_dot(x, ct, "tn"), x, "nt"), jnp.zeros_like(x)


_known_inverse.defvjp(_known_inverse_fwd, _known_inverse_bwd)


@jax.custom_vjp
def _decayed(x, c):
    return (x * jnp.exp(c)).astype(bf16).astype(f32)


def _decayed_fwd(x, c):
    e = jnp.exp(c)
    out = (x * e).astype(bf16).astype(f32)
    return out, (e, out)


def _decayed_bwd(res, ct):
    e, out = res
    return ct * e, ct * out


_decayed.defvjp(_decayed_fwd, _decayed_bwd)


@jax.custom_vjp
def _pair(x, y):
    return _dot(x, y, "nt")


def _pair_fwd(x, y):
    return _dot(x, y, "nt"), (x, y)


def _pair_bwd(res, ct):
    x, y = res
    hi = ct.astype(bf16)
    lo = (ct - hi.astype(f32)).astype(bf16)
    return _dot(hi, y) + _dot(lo, y), _dot(hi, x, "tn") + _dot(lo, x, "tn")


_pair.defvjp(_pair_fwd, _pair_bwd)


def _scan_chunk(s0, r, lw, k, v, a, b, inv=None):
    t = r[0].shape[0]
    ii = lax.broadcasted_iota(jnp.int32, (t, t), 0)
    jj = lax.broadcasted_iota(jnp.int32, (t, t), 1)
    incl = jj <= ii
    strict = jj < ii
    tri = incl.astype(f32)
    eye = jnp.where(ii == jj, 1.0, 0.0)
    cl = [_const_dot(tri, x) for x in lw]
    mid = [c[t // 2 - 1:t // 2, :] for c in cl]
    s0 = [s * jnp.exp(m) for s, m in zip(s0, mid)]
    cl = [c - m for c, m in zip(cl, mid)]
    rt = [_decayed(x, c) for x, c in zip(r, cl)]
    at = [_decayed(x, c - l) for x, c, l in zip(a, cl, lw)]
    bt = [_decayed(x, -c) for x, c in zip(b, cl)]
    kt = [_decayed(x, -c) for x, c in zip(k, cl)]
    l_ab = [jnp.where(strict, _pair(x, y), 0.0) for x, y in zip(at, bt)]
    l_ak = [jnp.where(strict, _pair(x, y), 0.0) for x, y in zip(at, kt)]
    r_b = [jnp.where(incl, _pair(x, y), 0.0) for x, y in zip(rt, bt)]
    r_k = [jnp.where(incl, _pair(x, y), 0.0) for x, y in zip(rt, kt)]
    if inv is None:
        inv = [eye + x for x in l_ab]
        pw = l_ab
        for _ in range(int(math.log2(t)) - 1):
            pw = [_dot(x, x) for x in pw]
            inv = [x + _dot(x, y) for x, y in zip(inv, pw)]
    else:
        inv = [_known_inverse(x, y) for x, y in zip(l_ab, inv)]
    rhs = [_dot(x, s, "nt") + _dot(m, y) for x, s, m, y in zip(at, s0, l_ak, v)]
    u = [_dot(x, y) for x, y in zip(inv, rhs)]
    y_s = [_dot(x, s, "nt") for x, s in zip(rt, s0)]
    y = [ys + _dot(m, uu) + _dot(n, vv) for ys, m, uu, n, vv in zip(y_s, r_b, u, r_k, v)]
    grow = [s + _dot(uu, x, "tn") + _dot(vv, z, "tn") for s, uu, x, vv, z in zip(s0, u, bt, v, kt)]
    s1 = [g * jnp.exp(c[t - 1:t, :]) for g, c in zip(grow, cl)]
    return y, s1, inv


def _head_rows(h):
    return slice(h * RWKV_HEAD, (h + 1) * RWKV_HEAD)


def _per_head(ref):
    return [ref[:, _head_rows(h)] for h in range(RWKV_HEADS)]


def _scan(r, lw, k, v, a, b, *, name):
    lp = r.shape[0]
    nc = lp // CHUNK
    row = pl.BlockSpec((CHUNK, RWKV_DIM), lambda c: (c, 0))

    def body(r_ref, lw_ref, k_ref, v_ref, a_ref, b_ref, y_ref, s_ref, inv_ref, state):
        @pl.when(pl.program_id(0) == 0)
        def _():
            state[...] = jnp.zeros_like(state)

        s_ref[...] = state[...]
        s0 = [state[_head_rows(h), :] for h in range(RWKV_HEADS)]
        y, s1, inv = _scan_chunk(s0, *[_per_head(ref) for ref in (r_ref, lw_ref, k_ref, v_ref, a_ref, b_ref)])
        for h in range(RWKV_HEADS):
            y_ref[:, _head_rows(h)] = y[h]
            state[_head_rows(h), :] = s1[h]
            inv_ref[h * CHUNK:(h + 1) * CHUNK, :] = inv[h].astype(inv_ref.dtype)

    return pl.pallas_call(
        body, name=name, grid=(nc,), in_specs=[row] * 6,
        out_specs=[row, pl.BlockSpec((RWKV_DIM, RWKV_HEAD), lambda c: (c, 0)),
                   pl.BlockSpec((RWKV_HEADS * CHUNK, CHUNK), lambda c: (c, 0))],
        out_shape=[jax.ShapeDtypeStruct((lp, RWKV_DIM), f32), jax.ShapeDtypeStruct((nc * RWKV_DIM, RWKV_HEAD), f32),
                   jax.ShapeDtypeStruct((nc * RWKV_HEADS * CHUNK, CHUNK), bf16)],
        scratch_shapes=[pltpu.VMEM((RWKV_DIM, RWKV_HEAD), f32)],
        compiler_params=_params(("arbitrary",)),
    )(r, lw, k, v, a, b)


def _scan_bwd(r, lw, k, v, a, b, states, inverses, dy, *, name):
    lp = r.shape[0]
    nc = lp // CHUNK
    back = lambda c: (nc - 1 - c, 0)
    row = pl.BlockSpec((CHUNK, RWKV_DIM), back)

    def body(r_ref, lw_ref, k_ref, v_ref, a_ref, b_ref, s_ref, inv_ref, dy_ref,
             dr_ref, dlw_ref, dk_ref, dv_ref, da_ref, db_ref, dstate):
        @pl.when(pl.program_id(0) == 0)
        def _():
            dstate[...] = jnp.zeros_like(dstate)

        outs = (dr_ref, dlw_ref, dk_ref, dv_ref, da_ref, db_ref)
        s0 = [s_ref[_head_rows(h), :] for h in range(RWKV_HEADS)]
        inv = [inv_ref[h * CHUNK:(h + 1) * CHUNK, :].astype(f32) for h in range(RWKV_HEADS)]
        _, vjp = jax.vjp(lambda *args: _scan_chunk(*args, inv=inv)[:2], s0,
                         *[_per_head(ref) for ref in (r_ref, lw_ref, k_ref, v_ref, a_ref, b_ref)])
        g = vjp((_per_head(dy_ref), [dstate[_head_rows(h), :] for h in range(RWKV_HEADS)]))
        for h in range(RWKV_HEADS):
            dstate[_head_rows(h), :] = g[0][h]
            for o_ref, gv in zip(outs, g[1:]):
                o_ref[:, _head_rows(h)] = gv[h]

    shape = jax.ShapeDtypeStruct((lp, RWKV_DIM), f32)
    return pl.pallas_call(
        body, name=name, grid=(nc,),
        in_specs=[row] * 6 + [pl.BlockSpec((RWKV_DIM, RWKV_HEAD), back),
                              pl.BlockSpec((RWKV_HEADS * CHUNK, CHUNK), back), row],
        out_specs=[row] * 6, out_shape=[shape] * 6,
        scratch_shapes=[pltpu.VMEM((RWKV_DIM, RWKV_HEAD), f32)],
        compiler_params=_params(("arbitrary",)),
    )(r, lw, k, v, a, b, states, inverses, dy)


def _loss_head(act, w_down, h1, target, g_final, *, name):
    lp = h1.shape[0]
    per_tile = 3
    tm = per_tile * BLOCK
    last_block = (lp - FRONT) // BLOCK - 1

    def body(a_ref, w_ref, h_ref, t0_ref, t1_ref, t2_ref, g_ref, loss_ref, dh_ref, dg_ref):
        i = pl.program_id(0)
        target_rows = jnp.concatenate([t0_ref[...], t1_ref[...], t2_ref[...]], axis=0)
        real = i * tm + lax.broadcasted_iota(jnp.int32, (tm, 1), 0) >= FRONT

        def tile_loss(hv, gv):
            err = _rms(hv, gv) - target_rows
            return 0.5 * jnp.sum(jnp.where(real, jnp.mean(err * err, axis=-1, keepdims=True), 0.0))

        h2 = _dot(a_ref[...], w_ref[...], "nn") + h_ref[...]
        loss, (dh, dg) = jax.value_and_grad(tile_loss, argnums=(0, 1))(h2, g_ref[...])

        @pl.when(i == 0)
        def _():
            loss_ref[...] = jnp.zeros_like(loss_ref)
            dg_ref[...] = jnp.zeros_like(dg_ref)

        loss_ref[...] += jnp.full(loss_ref.shape, loss, f32)
        dg_ref[...] += dg
        dh_ref[...] = dh

    def target_block(j):
        return pl.BlockSpec((BLOCK, D_MODEL),
                            lambda i: (jnp.clip(per_tile * i + j - FRONT // BLOCK, 0, last_block), 0))

    return pl.pallas_call(
        body, name=name, grid=(lp // tm,),
        in_specs=[pl.BlockSpec((tm, act.shape[1]), lambda i: (i, 0)), _full(w_down.shape),
                  pl.BlockSpec((tm, D_MODEL), lambda i: (i, 0)), target_block(0), target_block(1), target_block(2),
                  _full(g_final.shape)],
        out_specs=[_full((8, 128)), pl.BlockSpec((tm, D_MODEL), lambda i: (i, 0)), _full(g_final.shape)],
        out_shape=[jax.ShapeDtypeStruct((8, 128), f32), jax.ShapeDtypeStruct((lp, D_MODEL), f32),
                   jax.ShapeDtypeStruct(g_final.shape, f32)],
        compiler_params=_params(("arbitrary",)),
    )(act, w_down, h1, target, target, target, g_final)


def _embed_norm(x, meta, g, *, name):
    seq = x.shape[0]
    lp = seq + FRONT
    per_tile = 3
    tm = per_tile * BLOCK
    last_block = seq // BLOCK - 1

    def body(x0_ref, x1_ref, x2_ref, meta_ref, g_ref, h_ref, u_ref):
        front = jnp.concatenate([jnp.zeros((PAD, D_MODEL), f32), meta_ref[...]], axis=0)
        first = jnp.where(pl.program_id(0) == 0, front, x0_ref[...])
        h = jnp.concatenate([first, x1_ref[...], x2_ref[...]], axis=0)
        h_ref[...] = h
        u_ref[...] = _rms(h, g_ref[...]).astype(u_ref.dtype)

    def x_block(j):
        return pl.BlockSpec((BLOCK, D_MODEL),
                            lambda i: (jnp.clip(per_tile * i + j - FRONT // BLOCK, 0, last_block), 0))

    tile = pl.BlockSpec((tm, D_MODEL), lambda i: (i, 0))
    return pl.pallas_call(
        body, name=name, grid=(lp // tm,),
        in_specs=[x_block(0), x_block(1), x_block(2), _full(meta.shape), _full(g.shape)],
        out_specs=[tile, tile],
        out_shape=[jax.ShapeDtypeStruct((lp, D_MODEL), f32), jax.ShapeDtypeStruct((lp, D_MODEL), bf16)],
        compiler_params=_params(("parallel",)),
    )(x, x, x, meta, g)


def _input_norm_bwd(h0, g, du, dh1, *, name):
    lp = h0.shape[0]
    blocks = (lp - FRONT) // FRONT
    per_tile = max(n for n in (4, 3, 2, 1) if blocks % n == 0)
    ins = (h0, du, dh1)

    def body(*refs):
        tiles = [refs[k * per_tile:(k + 1) * per_tile] for k in range(len(ins))]
        front_refs = refs[len(ins) * per_tile:len(ins) * (per_tile + 1)]
        g_ref, dx_ref, front_ref, dg_ref = refs[len(ins) * (per_tile + 1):]

        def cotangents(h_ref, du_ref, dh1_ref):
            _, vjp = jax.vjp(lambda hv, gv: (_rms(hv, gv), hv), h_ref[...], g_ref[...])
            return vjp((du_ref[...].astype(f32), dh1_ref[...]))

        @pl.when(pl.program_id(0) == 0)
        def _():
            front_ref[...], dg_ref[...] = cotangents(*front_refs)

        dg = jnp.zeros(dg_ref.shape, f32)
        for j in range(per_tile):
            dh, dg_j = cotangents(*(t[j] for t in tiles))
            dx_ref[j * FRONT:(j + 1) * FRONT, :] = dh
            dg = dg + dg_j
        dg_ref[...] += dg

    def block(j):
        return pl.BlockSpec((FRONT, D_MODEL), lambda i: (per_tile * i + j + 1, 0))

    first = pl.BlockSpec((FRONT, D_MODEL), lambda i: (0, 0))
    return pl.pallas_call(
        body, name=name, grid=(blocks // per_tile,),
        in_specs=[block(j) for _ in ins for j in range(per_tile)] + [first] * len(ins) + [_full(g.shape)],
        out_specs=[pl.BlockSpec((per_tile * FRONT, D_MODEL), lambda i: (i, 0)), _full((FRONT, D_MODEL)),
                   _full(g.shape)],
        out_shape=[jax.ShapeDtypeStruct((lp - FRONT, D_MODEL), f32), jax.ShapeDtypeStruct((FRONT, D_MODEL), f32),
                   jax.ShapeDtypeStruct(g.shape, f32)],
        compiler_params=_params(("arbitrary",)),
    )(*(a for a in ins for _ in range(per_tile)), *ins, g)


def _local_step(x, target, meta, p, early_weights=None, late_weights=None, emit=None):
    emit = emit or (lambda group, grads: 0.0)
    seq = x.shape[0]
    lp = seq + FRONT
    cos_t, sin_t, swap = _rope_tables(lp)
    hsum = _head_sum_matrix(RWKV_DIM, RWKV_HEAD)
    hmean = hsum / RWKV_HEAD
    post_params = [p["ln_w"], p["ln_b"], p["r_k"], hmean]

    h0, u = _embed_norm(x, meta, p["norm_mix_g"], name="norm_mix")
    if early_weights is not None:
        p = {**p, **early_weights(u)}
    prep_params = [p["w0"], p["w2"], p["a0"], p["a2"], p["g2"], p["k_k"], p["k_a"], hsum]
    in_widths = [ATTN_PROJ, RKV_W, LORA_W, 2 * D_MODEL]
    q, k, v, p_rkv, p_lora, gates = _proj_in(u, p["w_in_lr"], p["b_in"], in_widths,
                                             (cos_t, sin_t, swap), name="proj_in", zero_rows_below=PAD)
    y_attn = _attention(q, k, v, p["sinks"], name="attention")

    mix_rkv, mix_lora = p["mix"][:, :RKV_W], p["mix"][:, RKV_W:]
    r_, lw_, k_, v_, a_, b_, g_ = _mixer_inputs([p_rkv, p_lora], [mix_rkv, mix_lora], prep_params,
                                                name="mixer_inputs")
    y_scan, states, inverses = _scan(r_, lw_, k_, v_, a_, b_, name="wkv_scan")
    (y_rwkv,) = _rowwise(_rwkv_post, [y_scan, r_, k_, v_, g_], post_params, [(RWKV_DIM, bf16)], name="rwkv_post")

    if late_weights is not None:
        p = {**p, **late_weights(y_rwkv)}
    br_a, br_r, merged = _branch_merge(y_attn, y_rwkv, p["w_br_attn_t"], p["w_br_rwkv_t"], gates, name="branch_merge")
    h1, f = _residual_norm(merged, p["w_o"], h0, p["norm_ffn_g"], name="out_proj")
    gate, up, act = _ffn_in(f, p["w_gate_t"], p["w_up_t"], name="ffn_in")

    loss8, dh2, d_final_g = _loss_head(act, p["w_down"], h1, target, p["norm_final_g"], name="loss_head")
    dgate, dup, dh1, d_ffn_g = _ffn_bwd(dh2, p["w_down"], gate, up, p["w_gate_t"], p["w_up_t"], h1, p["norm_ffn_g"],
                                        name="ffn_bwd")
    d_w_down = _mm_tn(act, dh2, name="dw_down")
    d_w_gate_t = _mm_tn(dgate, f, name="dw_gate")
    d_w_up_t = _mm_tn(dup, f, name="dw_up")
    zero = emit("ffn", dict(w_down=d_w_down, w_gate_t=d_w_gate_t, w_up_t=d_w_up_t))
    dgates, dbr_a, dbr_r, dy_attn, dy_rwkv = _branch_merge_bwd(
        dh1, p["w_o"], gates, br_a, br_r, p["w_br_attn_t"], p["w_br_rwkv_t"], name="branch_merge_bwd")
    d_w_o = _mm_tn(merged, dh1, name="dw_o")
    d_w_br_attn_t = _mm_tn(dbr_a, y_attn, name="dw_br_attn")
    d_w_br_rwkv_t = _mm_tn(dbr_r, y_rwkv, name="dw_br_rwkv")
    zero = zero + emit("branch", dict(w_o=d_w_o, w_br_attn_t=d_w_br_attn_t, w_br_rwkv_t=d_w_br_rwkv_t))

    post_params = [p["ln_w"] + zero, p["ln_b"], p["r_k"], hmean]
    res = _rowwise_bwd(_rwkv_post, [y_scan, r_, k_, v_, g_], post_params, [[dy_rwkv]], name="rwkv_post_bwd",
                       diff_rows=[True] * 5, diff_params=[True, True, True, False])
    dy_scan, dr_p, dk_p, dv_p, dg_p, d_ln_w, d_ln_b, d_r_k = res
    dr_s, dlw_s, dk_s, dv_s, da_s, db_s = _scan_bwd(r_, lw_, k_, v_, a_, b_, states, inverses, dy_scan,
                                                    name="wkv_scan_bwd")
    res = _mixer_inputs_bwd([p_rkv, p_lora], [mix_rkv, mix_lora], prep_params,
                            [[dr_s, dr_p], [dlw_s], [dk_s, dk_p], [dv_s, dv_p], [da_s], [db_s], [dg_p]],
                            name="mixer_inputs_bwd")
    dp_rkv, dp_lora, d_mix_rkv, d_mix_lora, d_w0, d_w2, d_a0, d_a2, d_g2, d_k_k, d_k_a = res

    dq, dk, dv, dkm, dvm, d_sinks = _attention_bwd(q, k, v, p["sinks"], y_attn, dy_attn, name="attention_bwd")
    (dqkv,) = _rowwise(_attn_prep_transposed, [dq, dk, dv, cos_t, sin_t], [swap, dkm, dvm], [(ATTN_PROJ, bf16)],
                       name="attn_prep_bwd")

    in_rows, (at_qkv, at_rkv, at_lora, at_gates) = p["w_in_lr"].shape[1], [off for off, _ in _pieces(in_widths)]
    d_w_in_t, db_gates = _mm_tn(dgates, u, name="dw_gates", colsum=True, into=(in_rows, at_gates, None))
    d_w_in_t, db_rkv = _mm_tn(dp_rkv, u, name="dw_rkv", colsum=True, into=(in_rows, at_rkv, d_w_in_t))
    d_w_in_t, db_lora = _mm_tn(dp_lora, u, name="dw_lora", colsum=True, into=(in_rows, at_lora, d_w_in_t))
    d_w_in_t, db_qkv = _mm_tn(dqkv, u, name="dw_qkv", colsum=True, into=(in_rows, at_qkv, d_w_in_t))
    zero = emit("input", dict(w_in_t=d_w_in_t, g2=d_g2, w2=d_w2, a2=d_a2))
    du = _proj_in_bwd([dqkv, dp_rkv, dp_lora, dgates], p["w_in_lr"], name="d_u")
    dx, d_front, d_mix_g = _input_norm_bwd(h0, p["norm_mix_g"] + zero, du, dh1, name="norm_mix_bwd")

    grads = dict(
        w_in_t=d_w_in_t,
        b_in=jnp.concatenate([db_qkv, db_rkv, db_lora, db_gates], axis=1),
        mix=jnp.concatenate([d_mix_rkv, d_mix_lora], axis=1),
        norm_mix_g=d_mix_g, sinks=d_sinks, w0=d_w0, w2=d_w2, a0=d_a0, a2=d_a2, g2=d_g2, k_k=d_k_k, k_a=d_k_a,
        r_k=d_r_k, ln_w=d_ln_w, ln_b=d_ln_b, w_br_attn_t=d_w_br_attn_t, w_br_rwkv_t=d_w_br_rwkv_t, w_o=d_w_o,
        norm_ffn_g=d_ffn_g, w_gate_t=d_w_gate_t, w_up_t=d_w_up_t, w_down=d_w_down, norm_final_g=d_final_g,
        meta=d_front[PAD:],
    )
    return loss8[0, 0], dx, grads


def _position():
    return lax.axis_index("x"), lax.axis_index("y"), lax.axis_index("c")


def _other_chips(x, y):
    return [(1 - x, y), (x, 1 - y), (1 - x, 1 - y)]


_HBM = pl.BlockSpec(memory_space=pltpu.HBM)
_SEM = pl.BlockSpec(memory_space=pltpu.SEMAPHORE)
_EFFECT = pltpu.SideEffectType.DATAFLOW_SIDE_EFFECTING


def _landing_zone(src, kind):
    shape = {"whole": (N_CHIPS,) + src.shape, "half": (2, N_CHIPS, src.shape[0], src.shape[1] // 2),
             "slab": (3,) + src.shape[1:], "sibling": src.shape, "all": (N_DEV - 1,) + src.shape}[kind]
    return lax.empty(shape, src.dtype)


def _copies_per_source(kind):
    return {"sibling": 1, "all": N_DEV - 1}.get(kind, 3)


def _chip_copies(src_refs, land_refs, send_sems, recv_sems, kind):
    x, y, c = _position()
    if kind == "all":
        copies = []
        for a, (src, land) in enumerate(zip(src_refs, land_refs)):
            for rel in range(1, N_DEV):
                peer = ((1 - x) if rel & 4 else x, (1 - y) if rel & 2 else y, (1 - c) if rel & 1 else c)
                k = (N_DEV - 1) * a + rel - 1
                copies.append(pltpu.make_async_remote_copy(
                    src_ref=src, dst_ref=land.at[rel - 1], send_sem=send_sems.at[k], recv_sem=recv_sems.at[k],
                    device_id=peer, device_id_type=MESH))
        return copies
    if kind == "sibling":
        return [pltpu.make_async_remote_copy(
            src_ref=src, dst_ref=land, send_sem=send_sems.at[a], recv_sem=recv_sems.at[a],
            device_id=(x, y, 1 - c), device_id_type=MESH) for a, (src, land) in enumerate(zip(src_refs, land_refs))]
    copies = []
    for a, (src, land) in enumerate(zip(src_refs, land_refs)):
        for j, (px, py) in enumerate(_other_chips(x, y)):
            if kind == "whole":
                src_ref, dst_ref = src, land.at[2 * x + y]
            elif kind == "half":
                half = src.shape[1] // 2
                src_ref, dst_ref = src.at[:, pl.ds(pl.multiple_of(c * half, half), half)], land.at[c, 2 * x + y]
            else:
                src_ref, dst_ref = src.at[2 * px + py], land.at[j]
            copies.append(pltpu.make_async_remote_copy(
                src_ref=src_ref, dst_ref=dst_ref, send_sem=send_sems.at[3 * a + j], recv_sem=recv_sems.at[3 * a + j],
                device_id=(px, py, c), device_id_type=MESH))
    return copies


def _exchange_start(srcs, *, kind, name):
    n = len(srcs)
    lands = [_landing_zone(s, kind) for s in srcs]

    def body(*refs):
        for cp in _chip_copies(refs[:n], refs[n:2 * n], refs[2 * n], refs[2 * n + 1], kind):
            cp.start()
        refs[-1][...] = jnp.zeros_like(refs[-1])

    res = pl.pallas_call(
        body, name=name,
        out_shape=(pltpu.SemaphoreType.DMA((_copies_per_source(kind) * n,)),
                   pltpu.SemaphoreType.DMA((_copies_per_source(kind) * n,)),
                   *[pltpu.HBM(a.shape, a.dtype) for a in srcs + lands], jax.ShapeDtypeStruct((8, 128), f32)),
        in_specs=[_HBM] * (2 * n),
        out_specs=(_SEM, _SEM, *[_HBM] * (2 * n), pl.BlockSpec(memory_space=pltpu.VMEM)),
        input_output_aliases={i: 2 + i for i in range(2 * n)},
        compiler_params=pltpu.CompilerParams(has_side_effects=_EFFECT),
    )(*[pltpu.with_memory_space_constraint(a, pltpu.HBM) for a in srcs + lands])
    return res[0], res[1], list(res[2:2 + n]), list(res[2 + n:2 + 2 * n]), res[-1]


def _exchange_wait(handle, after, *, kind, name):
    send_sems, recv_sems, srcs, lands, _ = handle
    n = len(srcs)

    def body(*refs):
        for cp in _chip_copies(refs[:n], refs[n:2 * n], refs[2 * n], refs[2 * n + 1], kind):
            cp.wait_send()
            cp.wait_recv()

    res = pl.pallas_call(
        body, name=name,
        out_shape=tuple(pltpu.HBM(a.shape, a.dtype) for a in srcs + lands),
        in_specs=[_HBM] * (2 * n) + [_SEM, _SEM, pl.BlockSpec(memory_space=pl.ANY)],
        out_specs=tuple([_HBM] * (2 * n)),
        input_output_aliases={i: i for i in range(2 * n)},
        compiler_params=pltpu.CompilerParams(has_side_effects=_EFFECT),
    )(*srcs, *lands, send_sems, recv_sems, after)
    return list(res[:n]), list(res[n:])


def _sum_own_and_received(g, recv, *, name):
    _, r, w = g.shape
    tm = _tile(r)
    if g.dtype == bf16 and tm % 16:
        tm = r
    x, y, _ = _position()
    me = jnp.reshape(2 * x + y, (1,)).astype(jnp.int32)

    def body(me_ref, g_ref, r_ref, o_ref):
        o_ref[...] = (g_ref[0].astype(f32) + r_ref[0].astype(f32)) + (r_ref[1].astype(f32) + r_ref[2].astype(f32))

    return pl.pallas_call(
        body, name=name,
        grid_spec=pltpu.PrefetchScalarGridSpec(
            num_scalar_prefetch=1, grid=(r // tm,),
            in_specs=[pl.BlockSpec((1, tm, w), lambda i, me_ref: (me_ref[0], i, 0)),
                      pl.BlockSpec((3, tm, w), lambda i, me_ref: (0, i, 0))],
            out_specs=pl.BlockSpec((tm, w), lambda i, me_ref: (i, 0))),
        out_shape=jax.ShapeDtypeStruct((r, w), f32),
        compiler_params=_params(("parallel",)),
    )(me, g, recv)


def _swap_halves(zone, *, name):
    def body(z_ref, o_ref, send_sems, recv_sems):
        x, y, c = _position()
        mine = [pltpu.make_async_remote_copy(
            src_ref=o_ref.at[c, 2 * px + py], dst_ref=o_ref.at[c, 2 * px + py], send_sem=send_sems.at[j],
            recv_sem=recv_sems.at[j], device_id=(x, y, 1 - c), device_id_type=MESH)
            for j, (px, py) in enumerate(_other_chips(x, y))]
        for cp in mine:
            cp.start()
        for j, (px, py) in enumerate(_other_chips(x, y)):
            pltpu.make_async_remote_copy(
                src_ref=o_ref.at[c, 2 * px + py], dst_ref=o_ref.at[1 - c, 2 * px + py], send_sem=send_sems.at[j],
                recv_sem=recv_sems.at[j], device_id=(x, y, 1 - c), device_id_type=MESH).wait_recv()
        for cp in mine:
            cp.wait_send()

    return pl.pallas_call(
        body, name=name,
        in_specs=[pl.BlockSpec(memory_space=pl.ANY)], out_specs=pl.BlockSpec(memory_space=pl.ANY),
        out_shape=jax.ShapeDtypeStruct(zone.shape, zone.dtype), input_output_aliases={0: 0},
        scratch_shapes=[pltpu.SemaphoreType.DMA((3,)), pltpu.SemaphoreType.DMA((3,))],
    )(zone)


def _sum_all_devices(own, received, *, name):
    def body(own_ref, got_ref, o_ref):
        x, y, c = _position()
        me = 4 * x + 2 * y + c
        acc = None
        for d in range(N_DEV):
            rel = jnp.bitwise_xor(me, d)
            block = jnp.where(rel == 0, own_ref[...], got_ref[jnp.maximum(rel, 1) - 1])
            acc = block if acc is None else acc + block
        o_ref[...] = acc

    return pl.pallas_call(
        body, name=name,
        in_specs=[pl.BlockSpec(memory_space=pltpu.VMEM)] * 2, out_specs=pl.BlockSpec(memory_space=pltpu.VMEM),
        out_shape=jax.ShapeDtypeStruct(own.shape, f32),
    )(own, received)


def _adam_math(w, g, m, v):
    nm = ADAM_B1 * m + (1.0 - ADAM_B1) * g
    nv = ADAM_B2 * v + (1.0 - ADAM_B2) * (g * g)
    m_hat = nm / (1.0 - ADAM_B1 ** ADAM_STEP)
    v_hat = nv / (1.0 - ADAM_B2 ** ADAM_STEP)
    return -ADAM_LR * (m_hat / (jnp.sqrt(v_hat) + ADAM_EPS) + ADAM_WD * w), nm, nv


def _adamw(w, g_parts, m, v, *, name, transposed=False, after=None):
    rows, cols = w.shape
    ordered = after is not None
    if transposed:
        tm = 256 if rows % 256 == 0 else rows
        g_spec = pl.BlockSpec((cols, tm), lambda i: (0, i))
    else:
        tm = _tile(rows, 256)
        g_spec = pl.BlockSpec((tm, cols), lambda i: (i, 0))
    n = len(g_parts)

    def body(*refs):
        refs = refs[1:] if ordered else refs
        w_ref, m_ref, v_ref = refs[0], refs[1 + n], refs[2 + n]
        g_ref, d_ref, nm_ref, nv_ref = refs[3 + n:]
        gv = refs[1][...]
        for part in refs[2:1 + n]:
            gv = gv + part[...]
        if transposed:
            gv = gv.T
        g_ref[...] = gv
        d_ref[...], nm_ref[...], nv_ref[...] = _adam_math(w_ref[...], gv, m_ref[...], v_ref[...])

    spec = pl.BlockSpec((tm, cols), lambda i: (i, 0))
    shape = jax.ShapeDtypeStruct((rows, cols), f32)
    return pl.pallas_call(
        body, name=name, grid=(rows // tm,),
        in_specs=[pl.BlockSpec(memory_space=pl.ANY)] * ordered + [spec] + [g_spec] * n + [spec] * 2,
        out_specs=[spec] * 4, out_shape=[shape] * 4,
        compiler_params=_params(("parallel",)),
    )(*([after] if ordered else []), w, *g_parts, m, v)


def _pad_rows(a, rows):
    return jnp.concatenate([a, jnp.zeros((rows - a.shape[0], a.shape[1]), a.dtype)], axis=0) if rows > a.shape[0] else a


_SMALL = (("norm_mix_g", D_MODEL), ("b_in", D_IN), ("sinks", Q_HEADS), ("mix", RWKV_PROJ), ("w0", RWKV_DIM),
          ("a0", RWKV_DIM), ("k_k", RWKV_DIM), ("k_a", RWKV_DIM), ("r_k", RWKV_DIM), ("ln_w", RWKV_DIM),
          ("ln_b", RWKV_DIM), ("norm_ffn_g", D_MODEL), ("norm_final_g", D_MODEL))


LANES = 128


def _small_layout():
    out, off = {}, 0
    for n, size in _SMALL + (("loss", 1),):
        pieces, col = [], 0
        while col < size:
            row, lane = divmod(off + col, PACK_W)
            width = min(size - col, PACK_W - lane)
            pieces.append((row, lane, width, col))
            col += width
        out[n] = pieces
        off += -(-size // LANES) * LANES
    return out, -(-off // PACK_W)


def _pack_small(d, loss):
    layout, rows = _small_layout()
    parts, used = [], 0
    for n, size in _SMALL + (("loss", 1),):
        item = loss if n == "loss" else d[n]
        fill = -size % LANES
        parts += [item.reshape(-1).astype(f32), jnp.zeros((fill,), f32)]
        used += size + fill
    parts.append(jnp.zeros((rows * PACK_W - used,), f32))
    return jnp.concatenate(parts).reshape(rows, PACK_W)


def _adamw_small(packed, first_row, ws, ms, vs, meta, *, name):
    layout, _ = _small_layout()
    names = [n for n, _ in _SMALL]
    k = len(names)

    def body(*refs):
        packed_ref = refs[0]
        w_refs, m_refs, v_refs = refs[1:1 + k], refs[1 + k:1 + 2 * k], refs[1 + 2 * k:1 + 3 * k]
        meta_refs = refs[1 + 3 * k:5 + 3 * k]
        outs = refs[5 + 3 * k:]
        for idx, n in enumerate(names):
            for row, lane, width, col in layout[n]:
                gv = packed_ref[first_row + row:first_row + row + 1, lane:lane + width]
                at = (slice(None), slice(col, col + width))
                new = _adam_math(w_refs[idx][at], gv, m_refs[idx][at], v_refs[idx][at])
                for o_ref, val in zip(outs[4 * idx:4 * idx + 4], (gv,) + new):
                    o_ref[at] = val
        for o_ref, val in zip(outs[4 * k:], _adam_math(*(r[...] for r in meta_refs))):
            o_ref[...] = val

    ins = [packed] + [d[n] for d in (ws, ms, vs) for n in names] + list(meta)
    shapes = [jax.ShapeDtypeStruct(ws[n].shape, f32) for n in names for _ in range(4)]
    shapes += [jax.ShapeDtypeStruct(meta[0].shape, f32)] * 3
    res = pl.pallas_call(
        body, name=name, grid=(1,), in_specs=[_full(a.shape) for a in ins],
        out_specs=[_full(s.shape) for s in shapes], out_shape=shapes,
        compiler_params=_params(("arbitrary",)),
    )(*ins)
    return {n: res[4 * i:4 * i + 4] for i, n in enumerate(names)}, res[4 * k:]


def kernel(x, meta_tokens, norm_mix_g, w_in, b_in, attn_sinks, rwkv_mix, rwkv_w0, rwkv_w2, rwkv_a0, rwkv_a2, rwkv_g2, rwkv_k_k, rwkv_k_a, rwkv_r_k, rwkv_ln_w, rwkv_ln_b, w_br_attn, w_br_rwkv, w_o, norm_ffn_g, w_ffn_gate, w_ffn_up, w_ffn_down, norm_final_g, loss_target, m_meta_tokens, m_norm_mix_g, m_w_in, m_b_in, m_attn_sinks, m_rwkv_mix, m_rwkv_w0, m_rwkv_w2, m_rwkv_a0, m_rwkv_a2, m_rwkv_g2, m_rwkv_k_k, m_rwkv_k_a, m_rwkv_r_k, m_rwkv_ln_w, m_rwkv_ln_b, m_w_br_attn, m_w_br_rwkv, m_w_o, m_norm_ffn_g, m_w_ffn_gate, m_w_ffn_up, m_w_ffn_down, m_norm_final_g, v_meta_tokens, v_norm_mix_g, v_w_in, v_b_in, v_attn_sinks, v_rwkv_mix, v_rwkv_w0, v_rwkv_w2, v_rwkv_a0, v_rwkv_a2, v_rwkv_g2, v_rwkv_k_k, v_rwkv_k_a, v_rwkv_r_k, v_rwkv_ln_w, v_rwkv_ln_b, v_w_br_attn, v_w_br_rwkv, v_w_o, v_norm_ffn_g, v_w_ffn_gate, v_w_ffn_up, v_w_ffn_down, v_norm_final_g):
    names = ("meta_tokens", "norm_mix_g", "w_in", "b_in", "attn_sinks", "rwkv_mix", "rwkv_w0", "rwkv_w2", "rwkv_a0",
             "rwkv_a2", "rwkv_g2", "rwkv_k_k", "rwkv_k_a", "rwkv_r_k", "rwkv_ln_w", "rwkv_ln_b", "w_br_attn",
             "w_br_rwkv", "w_o", "norm_ffn_g", "w_ffn_gate", "w_ffn_up", "w_ffn_down", "norm_final_g")
    w_all = dict(zip(names, (meta_tokens, norm_mix_g, w_in, b_in, attn_sinks, rwkv_mix, rwkv_w0, rwkv_w2, rwkv_a0,
                             rwkv_a2, rwkv_g2, rwkv_k_k, rwkv_k_a, rwkv_r_k, rwkv_ln_w, rwkv_ln_b, w_br_attn,
                             w_br_rwkv, w_o, norm_ffn_g, w_ffn_gate, w_ffn_up, w_ffn_down, norm_final_g)))
    m_all = dict(zip(names, (m_meta_tokens, m_norm_mix_g, m_w_in, m_b_in, m_attn_sinks, m_rwkv_mix, m_rwkv_w0,
                             m_rwkv_w2, m_rwkv_a0, m_rwkv_a2, m_rwkv_g2, m_rwkv_k_k, m_rwkv_k_a, m_rwkv_r_k,
                             m_rwkv_ln_w, m_rwkv_ln_b, m_w_br_attn, m_w_br_rwkv, m_w_o, m_norm_ffn_g, m_w_ffn_gate,
                             m_w_ffn_up, m_w_ffn_down, m_norm_final_g)))
    v_all = dict(zip(names, (v_meta_tokens, v_norm_mix_g, v_w_in, v_b_in, v_attn_sinks, v_rwkv_mix, v_rwkv_w0,
                             v_rwkv_w2, v_rwkv_a0, v_rwkv_a2, v_rwkv_g2, v_rwkv_k_k, v_rwkv_k_a, v_rwkv_r_k,
                             v_rwkv_ln_w, v_rwkv_ln_b, v_w_br_attn, v_w_br_rwkv, v_w_o, v_norm_ffn_g, v_w_ffn_gate,
                             v_w_ffn_up, v_w_ffn_down, v_norm_final_g)))
    cx, cy, _ = _position()
    chip = 2 * cx + cy

    t_of = dict(w_in_t="w_in", w_gate_t="w_ffn_gate", w_up_t="w_ffn_up", w_br_attn_t="w_br_attn",
                w_br_rwkv_t="w_br_rwkv", g2_t="rwkv_g2", w2_t="rwkv_w2", a2_t="rwkv_a2")
    plain_of = dict(w_down="w_ffn_down", w_o="w_o")
    meta_cols = meta_tokens.shape[1]

    def shard(k):
        return (w_all[t_of[k]][0].T if k in t_of else w_all[plain_of[k]][0]).astype(bf16)

    def whole(zone, own):
        return lax.dynamic_update_slice_in_dim(zone, own[None], chip, axis=0).reshape(-1, own.shape[-1])

    tiny = ("g2_t", "w2_t", "a2_t")
    late = ("w_gate_t", "w_up_t", "w_down", "w_o", "w_br_attn_t", "w_br_rwkv_t")
    w_in_own = shard("w_in_t")
    w_in_rows, w_in_cols = w_in_own.shape
    tiny_h = _exchange_start([shard(k) for k in tiny] + [meta_tokens], kind="whole", name="gather_tiny_start")
    w_in_h = _exchange_start([w_in_own + tiny_h[4][0, 0].astype(bf16)], kind="half", name="gather_w_in_start")
    behind = w_in_h[4][0, 0].astype(bf16)
    late_h = _exchange_start([shard(k) + behind for k in late], kind="whole", name="gather_late_start")
    own, zones = _exchange_wait(tiny_h, late_h[4], kind="whole", name="gather_tiny_wait")
    got = {k: whole(z, o) for k, z, o in zip(tiny, zones, own)}
    meta_full = whole(zones[-1], own[-1]).reshape(N_CHIPS, N_META, meta_cols).transpose(1, 0, 2).reshape(N_META, -1)
    p = dict(
        g2=got["g2_t"].T.astype(f32), w2=got["w2_t"].T.astype(f32), a2=got["a2_t"].T.astype(f32),
        b_in=b_in, sinks=attn_sinks, mix=rwkv_mix, w0=rwkv_w0, a0=rwkv_a0, k_k=rwkv_k_k, k_a=rwkv_k_a,
        r_k=rwkv_r_k.reshape(1, RWKV_DIM), ln_w=rwkv_ln_w, ln_b=rwkv_ln_b, norm_mix_g=norm_mix_g,
        norm_ffn_g=norm_ffn_g, norm_final_g=norm_final_g.reshape(1, D_MODEL),
    )

    def early_weights(after):
        own_h, zones_h = _exchange_wait(w_in_h, after, kind="half", name="gather_w_in_wait")
        zone = _swap_halves(zones_h[0], name="swap_w_in_halves")
        own_halves = own_h[0].reshape(w_in_rows, 2, w_in_cols // 2).transpose(1, 0, 2)[:, None]
        zone = lax.dynamic_update_slice(zone, own_halves, (0, chip, 0, 0))
        return dict(w_in_lr=zone.reshape(2, N_CHIPS * w_in_rows, w_in_cols // 2))

    def late_weights(after):
        own_l, zones_l = _exchange_wait(late_h, after, kind="whole", name="gather_late_wait")
        return {k: whole(z, o) for k, z, o in zip(late, zones_l, own_l)}

    started = {}

    def partial_sums(groups, after):
        parts = {}
        for group in groups:
            keys, handle = started[group]
            slabs, lands = _exchange_wait(handle, after, kind="slab", name="scatter_" + group + "_wait")
            parts.update({k: _sum_own_and_received(s, l, name="sum_chips_" + k) for k, s, l in zip(keys, slabs, lands)})
        return parts

    def emit(group, grads_):
        keys = list(grads_)
        slabs = []
        for k in keys:
            a = grads_[k].T if k in ("g2", "w2", "a2") else grads_[k]
            slabs.append(a.reshape(N_CHIPS, a.shape[0] // N_CHIPS, a.shape[1]))
        started[group] = (keys, _exchange_start(slabs, kind="slab", name="scatter_" + group + "_start"))
        zero = started[group][1][4]
        if group == "input":
            started["parts_a"] = partial_sums(("ffn", "branch"), zero)
            started["swap_a"] = _exchange_start(list(started["parts_a"].values()), kind="sibling",
                                                name="swap_cores_a_start")
            zero = started["swap_a"][4]
        return zero[0, 0]

    loss, dx, g = _local_step(x[0], loss_target[0], meta_full, p, early_weights, late_weights, emit)

    grads, delta, new_m, new_v = {}, {}, {}, {}
    in_grad_layout = ("w_in_t", "w_gate_t", "w_up_t")
    weight_of = {**t_of, **plain_of}

    def update(keys, mine, theirs, after=None):
        for k, part, other in zip(keys, mine, theirs):
            both = [part, other]
            k = k + "_t" if k in ("g2", "w2", "a2") else k
            n = weight_of[k]
            shape2 = w_all[n].shape[1:]
            w_, m_, v_ = (a.reshape(shape2) for a in (w_all[n], m_all[n], v_all[n]))
            if k in in_grad_layout:
                res = [t.T for t in _adamw(w_.T, both, m_.T, v_.T, name="adamw_" + n, after=after)]
            else:
                res = _adamw(w_, both, m_, v_, name="adamw_" + n, transposed=k in t_of, after=after)
            grads[n], delta[n], new_m[n], new_v[n] = (t.reshape(w_all[n].shape) for t in res)
        return delta[n]

    small = _pack_small(g, loss)
    small_rows8 = -(-(small.shape[0] + N_META) // 8) * 8
    small_h = _exchange_start([_pad_rows(jnp.concatenate([g["meta"], small], axis=0), small_rows8)], kind="all",
                              name="reduce_small_start")
    keys_a = list(started["parts_a"])
    mine_a, theirs_a = _exchange_wait(started["swap_a"], small_h[4], kind="sibling", name="swap_cores_a_wait")
    cut = 2
    done = update(keys_a[:cut], mine_a[:cut], theirs_a[:cut])
    parts_b = partial_sums(("input",), done)
    swap_b = _exchange_start(list(parts_b.values()), kind="sibling", name="swap_cores_b_start")
    done = update(keys_a[cut:], mine_a[cut:], theirs_a[cut:], after=swap_b[4])
    small_own, small_got = _exchange_wait(small_h, done, kind="all", name="reduce_small_wait")
    reduced = _sum_all_devices(small_own[0], small_got[0], name="reduce_small_sum")
    update(list(parts_b), *_exchange_wait(swap_b, reduced, kind="sibling", name="swap_cores_b_wait"))
    g_meta = lax.dynamic_slice_in_dim(reduced[:N_META], chip * meta_cols, meta_cols, axis=1)
    (loss_row, loss_lane, _, _), = _small_layout()[0]["loss"]
    loss_total = reduced[N_META + loss_row, loss_lane]

    small_of = dict(norm_mix_g="norm_mix_g", b_in="b_in", attn_sinks="sinks", rwkv_mix="mix", rwkv_w0="w0",
                    rwkv_a0="a0", rwkv_k_k="k_k", rwkv_k_a="k_a", rwkv_r_k="r_k", rwkv_ln_w="ln_w",
                    rwkv_ln_b="ln_b", norm_ffn_g="norm_ffn_g", norm_final_g="norm_final_g")
    as_rows = [{k: src[n].reshape(1, -1) for n, k in small_of.items()} for src in (w_all, m_all, v_all)]
    meta_in = (meta_tokens, g_meta, m_meta_tokens, v_meta_tokens)
    small_out, meta_out = _adamw_small(reduced, N_META, *as_rows, meta_in, name="adamw_small")
    grads["meta_tokens"] = g_meta
    delta["meta_tokens"], new_m["meta_tokens"], new_v["meta_tokens"] = meta_out
    for n, k in small_of.items():
        grads[n], delta[n], new_m[n], new_v[n] = (t.reshape(w_all[n].shape) for t in small_out[k])

    return (loss_total, dx.reshape(x.shape), *[grads[n] for n in names], *[delta[n] for n in names],
            *[new_m[n] for n in names], *[new_v[n] for n in names])
```

```python
import math

import jax
import jax.numpy as jnp
import numpy as np
from jax import lax
from jax.experimental import pallas as pl
from jax.experimental.pallas import tpu as pltpu

f32 = jnp.float32
bf16 = jnp.bfloat16

D_MODEL = 1024
N_META = 16
HEAD_DIM = 64
Q_HEADS = 8
KV_HEADS = 2
GROUP = Q_HEADS // KV_HEADS
WINDOW = 128
BLOCK = 128
ROPE_THETA = 500000.0
ROPE_DIM = HEAD_DIM // 4
RWKV_HEADS = 8
RWKV_HEAD = 64
RWKV_DIM = RWKV_HEADS * RWKV_HEAD
DECAY_LORA = 64
AAA_LORA = 64
GATE_LORA = 160
LORA_W = DECAY_LORA + AAA_LORA + GATE_LORA
RWKV_LN_EPS = 64e-5
D_FF = 2816
Q_W = Q_HEADS * HEAD_DIM
KV_W = KV_HEADS * HEAD_DIM
ATTN_PROJ = Q_W + 2 * KV_W
RKV_W = 3 * RWKV_DIM
RWKV_PROJ = RKV_W + LORA_W
D_IN = ATTN_PROJ + RWKV_PROJ + 2 * D_MODEL
RMS_EPS = 1e-6
NEG_INF = -1e30
PAD = BLOCK - N_META
FRONT = PAD + N_META

ADAM_LR = 0.001
ADAM_B1 = 0.9
ADAM_B2 = 0.999
ADAM_EPS = 1e-08
ADAM_WD = 0.01
ADAM_STEP = 10

N_CHIPS = 4
N_DEV = 8
CHUNK = 128
VMEM_LIMIT = 56 * 1024 * 1024
MM_ROWS = 704
PACK_W = 1024
MESH = pl.DeviceIdType.MESH


def _tile(m, pref=384):
    for step in (16, 8):
        for t in range(min(m, pref) // step * step, 0, -step):
            if m % t == 0:
                return t
    return m


def _params(sem=None):
    return pltpu.CompilerParams(dimension_semantics=sem, vmem_limit_bytes=VMEM_LIMIT)


def _full(shape):
    nd = len(shape)
    return pl.BlockSpec(shape, lambda *_: (0,) * nd)


def _dot(a, b, dims="nn"):
    dn = {"nn": (((1,), (0,)), ((), ())), "nt": (((1,), (1,)), ((), ())), "tn": (((0,), (0,)), ((), ()))}[dims]
    return lax.dot_general(a.astype(bf16), b.astype(bf16), dn, preferred_element_type=f32)


def _two_pass(x, m, dims="nn"):
    x_hi = x.astype(bf16)
    x_lo = (x - x_hi.astype(f32)).astype(bf16)
    return _dot(x_hi, m, dims) + _dot(x_lo, m, dims)


@jax.custom_vjp
def _dot_const(x, m):
    return _two_pass(x, m)


def _dot_const_fwd(x, m):
    return _two_pass(x, m), m


def _dot_const_bwd(m, ct):
    return _two_pass(ct, m, "nt"), jnp.zeros_like(m)


_dot_const.defvjp(_dot_const_fwd, _dot_const_bwd)


def _two_pass_left(m, x, dims):
    x_hi = x.astype(bf16)
    x_lo = (x - x_hi.astype(f32)).astype(bf16)
    return _dot(m, x_hi, dims) + _dot(m, x_lo, dims)


@jax.custom_vjp
def _const_dot(m, x):
    return _two_pass_left(m, x, "nn")


def _const_dot_fwd(m, x):
    return _two_pass_left(m, x, "nn"), m


def _const_dot_bwd(m, ct):
    return jnp.zeros_like(m), _two_pass_left(m, ct, "tn")


_const_dot.defvjp(_const_dot_fwd, _const_dot_bwd)


def _mm(a, b, mode, *, name, out_dtype=f32, bias=None, add=None, zero_rows_below=0):
    m, _ = a.shape
    n = b.shape[1] if mode == "nn" else b.shape[0]
    tm = _tile(m, MM_ROWS)
    has_bias, has_add = bias is not None, add is not None

    def body(*refs):
        a_ref, b_ref = refs[0], refs[1]
        o_ref = refs[-1]
        acc = _dot(a_ref[...], b_ref[...], mode)
        k = 2
        if has_bias:
            acc = acc + refs[k][...]
            k += 1
        if zero_rows_below:
            rows = pl.program_id(0) * tm + lax.broadcasted_iota(jnp.int32, acc.shape, 0)
            acc = jnp.where(rows >= zero_rows_below, acc, 0.0)
        if has_add:
            acc = acc + refs[k][...].astype(f32)
        o_ref[...] = acc.astype(out_dtype)

    ins = [a, b]
    in_specs = [pl.BlockSpec((tm, a.shape[1]), lambda i: (i, 0)), _full(b.shape)]
    if has_bias:
        ins.append(bias)
        in_specs.append(_full(bias.shape))
    if has_add:
        ins.append(add)
        in_specs.append(pl.BlockSpec((tm, n), lambda i: (i, 0)))
    return pl.pallas_call(
        body, name=name, grid=(m // tm,), in_specs=in_specs,
        out_specs=pl.BlockSpec((tm, n), lambda i: (i, 0)),
        out_shape=jax.ShapeDtypeStruct((m, n), out_dtype),
        compiler_params=_params(("parallel",)),
    )(*ins)


def _pieces(widths):
    out, off = [], 0
    for w in widths:
        out.append((off, w))
        off += w
    return out


def _proj_in(a, w_lr, bias, widths, rope, *, name, zero_rows_below=0):
    m, kdim = a.shape
    half = kdim // 2
    tm = _tile(m, MM_ROWS)
    cos_t, sin_t, swap = rope
    out_widths = [Q_W, KV_W, KV_W] + list(widths[1:])

    def body(a_ref, w_ref, b_ref, cos_ref, sin_ref, swap_ref, *outs):
        a_l, a_r = a_ref[:, :half], a_ref[:, half:]
        for j, (off, width) in enumerate(_pieces(widths)):
            acc = _dot(a_l, w_ref[0, off:off + width, :], "nt") + _dot(a_r, w_ref[1, off:off + width, :], "nt")
            acc = acc + b_ref[:, off:off + width]
            if zero_rows_below:
                rows = pl.program_id(0) * tm + lax.broadcasted_iota(jnp.int32, acc.shape, 0)
                acc = jnp.where(rows >= zero_rows_below, acc, 0.0)
            if j == 0:
                qkv = acc.astype(bf16).astype(f32)
                for o_ref, val in zip(outs[:3], _attn_prep(qkv, cos_ref[...], sin_ref[...], swap_ref[...])):
                    o_ref[...] = val.astype(o_ref.dtype)
            else:
                outs[2 + j][...] = acc.astype(outs[2 + j].dtype)

    table = pl.BlockSpec((tm, HEAD_DIM), lambda i: (i, 0))
    return pl.pallas_call(
        body, name=name, grid=(m // tm,),
        in_specs=[pl.BlockSpec((tm, kdim), lambda i: (i, 0)), _full(w_lr.shape), _full(bias.shape), table, table,
                  _full(swap.shape)],
        out_specs=[pl.BlockSpec((tm, w), lambda i: (i, 0)) for w in out_widths],
        out_shape=[jax.ShapeDtypeStruct((m, w), bf16) for w in out_widths],
        compiler_params=_params(("parallel",)),
    )(a, w_lr, bias, cos_t, sin_t, swap)


def _proj_in_bwd(d_list, w_lr, *, name):
    m = d_list[0].shape[0]
    half = w_lr.shape[2]
    widths = [d.shape[1] for d in d_list]
    tm = _tile(m, MM_ROWS)

    def body(*refs):
        w_ref, o_ref = refs[-2], refs[-1]
        for side in range(2):
            acc = None
            for (off, width), d_ref in zip(_pieces(widths), refs):
                term = _dot(d_ref[...], w_ref[side, off:off + width, :])
                acc = term if acc is None else acc + term
            o_ref[:, side * half:(side + 1) * half] = acc.astype(o_ref.dtype)

    return pl.pallas_call(
        body, name=name, grid=(m // tm,),
        in_specs=[pl.BlockSpec((tm, w), lambda i: (i, 0)) for w in widths] + [_full(w_lr.shape)],
        out_specs=pl.BlockSpec((tm, 2 * half), lambda i: (i, 0)),
        out_shape=jax.ShapeDtypeStruct((m, 2 * half), bf16),
        compiler_params=_params(("parallel",)),
    )(*d_list, w_lr)


def _residual_norm(a, w, res, g, *, name):
    m, d = res.shape
    tm = _tile(m, MM_ROWS)

    def body(a_ref, w_ref, r_ref, g_ref, h_ref, n_ref):
        h = _dot(a_ref[...], w_ref[...]) + r_ref[...]
        h_ref[...] = h
        n_ref[...] = _rms(h, g_ref[...]).astype(n_ref.dtype)

    tile = pl.BlockSpec((tm, d), lambda i: (i, 0))
    return pl.pallas_call(
        body, name=name, grid=(m // tm,),
        in_specs=[pl.BlockSpec((tm, a.shape[1]), lambda i: (i, 0)), _full(w.shape), tile, _full(g.shape)],
        out_specs=[tile, tile],
        out_shape=[jax.ShapeDtypeStruct((m, d), f32), jax.ShapeDtypeStruct((m, d), bf16)],
        compiler_params=_params(("parallel",)),
    )(a, w, res, g)


def _residual_norm_bwd(d_list, w_list, h, g, dh_out, *, name):
    m, d = h.shape
    k = len(d_list)
    tm = _tile(m)

    def body(*refs):
        h_ref, g_ref, dho_ref, dh_ref, dg_ref = refs[2 * k:]
        dn = _dot(refs[0][...], refs[k][...])
        for i in range(1, k):
            dn = dn + _dot(refs[i][...], refs[k + i][...])
        _, vjp = jax.vjp(lambda hv, gv: (_rms(hv, gv), hv), h_ref[...], g_ref[...])
        dh, dg = vjp((dn, dho_ref[...]))
        dh_ref[...] = dh

        @pl.when(pl.program_id(0) == 0)
        def _():
            dg_ref[...] = jnp.zeros_like(dg_ref)

        dg_ref[...] += dg

    tile = pl.BlockSpec((tm, d), lambda i: (i, 0))
    return pl.pallas_call(
        body, name=name, grid=(m // tm,),
        in_specs=[pl.BlockSpec((tm, a.shape[1]), lambda i: (i, 0)) for a in d_list] + [_full(w.shape) for w in w_list]
        + [tile, _full(g.shape), tile],
        out_specs=[tile, _full(g.shape)],
        out_shape=[jax.ShapeDtypeStruct((m, d), f32), jax.ShapeDtypeStruct(g.shape, f32)],
        compiler_params=_params(("arbitrary",)),
    )(*d_list, *w_list, h, g, dh_out)


def _mm_tn(a, b, *, name, colsum=False, out_dtype=bf16, into=None):
    r, m = a.shape
    n = b.shape[1]
    tr = _tile(r, 1408)
    tmo = m
    for cand in (1408, 1024, 768, 512):
        if m > 1024 and m % cand == 0:
            tmo = cand
            break
    steps = r // tr

    rows, offset, target = into or (m, 0, None)

    def body(a_ref, b_ref, *rest):
        o_ref, rest = (rest[0], rest[1:]) if target is None else (rest[1], rest[2:])
        acc = rest[-1]
        i = pl.program_id(1)

        @pl.when(i == 0)
        def _():
            acc[...] = jnp.zeros_like(acc)
            if colsum:
                rest[0][...] = jnp.zeros_like(rest[0])

        acc[...] += _dot(a_ref[...], b_ref[...], "tn")
        if colsum:
            rest[0][...] += jnp.sum(a_ref[...].astype(f32), axis=0, keepdims=True)

        @pl.when(i == steps - 1)
        def _():
            o_ref[...] = acc[...].astype(out_dtype)

    out_shape = [jax.ShapeDtypeStruct((rows, n), out_dtype)]
    if offset % tmo == 0:
        out_specs = [pl.BlockSpec((tmo, n), lambda j, i: (offset // tmo + j, 0))]
    else:
        out_specs = [pl.BlockSpec((pl.Element(tmo), pl.Element(n)), lambda j, i: (pl.multiple_of(offset + j * tmo, math.gcd(offset, tmo)), 0))]
    if colsum:
        out_shape.append(jax.ShapeDtypeStruct((1, m), f32))
        out_specs.append(pl.BlockSpec((1, tmo), lambda j, i: (0, j)))
    in_specs = [pl.BlockSpec((tr, tmo), lambda j, i: (i, j)), pl.BlockSpec((tr, n), lambda j, i: (i, 0))]
    res = pl.pallas_call(
        body, name=name, grid=(m // tmo, steps),
        in_specs=in_specs + ([] if target is None else [pl.BlockSpec(memory_space=pl.ANY)]),
        out_specs=out_specs, out_shape=out_shape,
        scratch_shapes=[pltpu.VMEM((tmo, n), f32)],
        input_output_aliases={} if target is None else {2: 0},
        compiler_params=_params(("parallel", "arbitrary")),
    )(a, b, *([] if target is None else [target]))
    return res if colsum else res[0]


def _rowwise(fn, rows, params, outs, *, name, tm=None):
    m = rows[0].shape[0]
    tm = tm or _tile(m, MM_ROWS)
    nr, npar = len(rows), len(params)

    def body(*refs):
        vals = [r[...] for r in refs[:nr + npar]]
        res = fn(*vals)
        for o_ref, v in zip(refs[nr + npar:], res):
            o_ref[...] = v.astype(o_ref.dtype)

    return pl.pallas_call(
        body, name=name, grid=(m // tm,),
        in_specs=[pl.BlockSpec((tm, r.shape[1]), lambda i: (i, 0)) for r in rows] + [_full(p.shape) for p in params],
        out_specs=[pl.BlockSpec((tm, w), lambda i: (i, 0)) for w, _ in outs],
        out_shape=[jax.ShapeDtypeStruct((m, w), dt) for w, dt in outs],
        compiler_params=_params(("parallel",)),
    )(*rows, *params)


def _rowwise_bwd(fn, rows, params, cts, *, name, diff_rows, diff_params, tm=None, zero_rows_below=0, out_dtypes=None):
    m = rows[0].shape[0]
    tm = tm or _tile(m)
    nr, npar = len(rows), len(params)
    d_idx = [i for i in range(nr) if diff_rows[i]]
    p_idx = [i for i in range(npar) if diff_params[i]]
    out_dtypes = out_dtypes or [f32] * len(d_idx)
    flat_cts = [c for group in cts for c in group]
    n_ct = len(flat_cts)

    def body(*refs):
        vals = [r[...] for r in refs[:nr + npar]]
        ct_refs = refs[nr + npar:nr + npar + n_ct]
        out_refs = refs[nr + npar + n_ct:]
        ct_vals, k = [], 0
        for group in cts:
            acc = ct_refs[k][...].astype(f32)
            for extra in range(1, len(group)):
                acc = acc + ct_refs[k + extra][...].astype(f32)
            k += len(group)
            if zero_rows_below:
                rr = pl.program_id(0) * tm + lax.broadcasted_iota(jnp.int32, acc.shape, 0)
                acc = jnp.where(rr >= zero_rows_below, acc, 0.0)
            ct_vals.append(acc)

        def g(*dargs):
            full = list(vals)
            for pos, i in enumerate(d_idx):
                full[i] = dargs[pos]
            for pos, i in enumerate(p_idx):
                full[nr + i] = dargs[len(d_idx) + pos]
            return tuple(fn(*full))

        _, vjp = jax.vjp(g, *[vals[i].astype(f32) for i in d_idx], *[vals[nr + i] for i in p_idx])
        grads = vjp(tuple(ct_vals))
        for pos in range(len(d_idx)):
            out_refs[pos][...] = grads[pos].astype(out_refs[pos].dtype)
        first = pl.program_id(0) == 0
        for pos in range(len(p_idx)):
            o_ref = out_refs[len(d_idx) + pos]

            @pl.when(first)
            def _(o_ref=o_ref):
                o_ref[...] = jnp.zeros_like(o_ref)

            o_ref[...] += grads[len(d_idx) + pos]

    return pl.pallas_call(
        body, name=name, grid=(m // tm,),
        in_specs=[pl.BlockSpec((tm, r.shape[1]), lambda i: (i, 0)) for r in rows] + [_full(p.shape) for p in params]
        + [pl.BlockSpec((tm, c.shape[1]), lambda i: (i, 0)) for c in flat_cts],
        out_specs=[pl.BlockSpec((tm, rows[i].shape[1]), lambda i_: (i_, 0)) for i in d_idx]
        + [_full(params[i].shape) for i in p_idx],
        out_shape=[jax.ShapeDtypeStruct(rows[i].shape, dt) for i, dt in zip(d_idx, out_dtypes)]
        + [jax.ShapeDtypeStruct(params[i].shape, f32) for i in p_idx],
        compiler_params=_params(("arbitrary",)),
    )(*rows, *params, *flat_cts)


def _rms(x, g):
    return x * lax.rsqrt(jnp.mean(x * x, axis=-1, keepdims=True) + RMS_EPS) * g


def _head_sum_matrix(width, head):
    idx = jnp.arange(width) // head
    return (idx[:, None] == idx[None, :]).astype(f32)


def _rope_tables(lp):
    half = ROPE_DIM // 2
    pos = (np.arange(lp) - PAD).astype(np.float32)
    inv_freq = np.power(np.float32(ROPE_THETA), -np.arange(half, dtype=np.float32) * np.float32(2.0 / ROPE_DIM))
    ang = pos[:, None] * inv_freq[None, :].astype(np.float32)
    cos, sin = np.cos(ang), np.sin(ang)
    ones = np.ones((lp, HEAD_DIM - ROPE_DIM), np.float32)
    cos_t = np.concatenate([cos, cos, ones], axis=1)
    sin_t = np.concatenate([-sin, sin, 0.0 * ones], axis=1)
    i = np.arange(HEAD_DIM)
    src = np.where(i < half, i + half, np.where(i < ROPE_DIM, i - half, i))
    swap = ((i[:, None] == src[None, :]) & (i[None, :] < ROPE_DIM)).astype(np.float32)
    return jnp.asarray(cos_t, f32), jnp.asarray(sin_t, f32), jnp.asarray(swap, f32)


def _attn_prep(qkv, cos_t, sin_t, swap):
    outs = []
    for h in range(Q_HEADS + KV_HEADS):
        t = qkv[:, h * HEAD_DIM:(h + 1) * HEAD_DIM]
        outs.append(t * cos_t + _dot_const(t, swap) * sin_t)
    q = jnp.concatenate(outs[:Q_HEADS], axis=1)
    k = jnp.concatenate(outs[Q_HEADS:], axis=1)
    return q, k, qkv[:, Q_W + KV_W:]


def _attn_prep_transposed(dq, dk, dv, cos_t, sin_t, swap, dk_meta, dv_meta):
    first = pl.program_id(0) == 0

    def with_meta(d, d_meta):
        rest = jnp.zeros((d.shape[0] - BLOCK, KV_W), f32)
        return d.astype(f32) + jnp.where(first, jnp.concatenate([d_meta, rest], axis=0), 0.0)

    parts = []
    for d, heads in ((dq.astype(f32), Q_HEADS), (with_meta(dk, dk_meta), KV_HEADS)):
        for h in range(heads):
            t = d[:, h * HEAD_DIM:(h + 1) * HEAD_DIM]
            parts.append(t * cos_t + _two_pass(t * sin_t, swap, "nt"))
    return (jnp.concatenate(parts + [with_meta(dv, dv_meta)], axis=1),)


def _softplus(z):
    return jnp.maximum(z, 0.0) + jnp.log1p(jnp.exp(-jnp.abs(z)))


def _rwkv_prep(rkv, lora, w0, w2, a0, a2, g2, k_k, k_a, hsum):
    r = rkv[:, :RWKV_DIM]
    k = rkv[:, RWKV_DIM:2 * RWKV_DIM]
    v = rkv[:, 2 * RWKV_DIM:]
    dw = lora[:, :DECAY_LORA]
    da = lora[:, DECAY_LORA:DECAY_LORA + AAA_LORA]
    dg = lora[:, DECAY_LORA + AAA_LORA:]
    w = -_softplus(-(w0 + _dot(jnp.tanh(dw), w2))) - 0.5
    a = jax.nn.sigmoid(a0 + _dot(da, a2))
    g = _dot(jax.nn.sigmoid(dg), g2)
    kk = k * k_k
    kk = kk * lax.rsqrt(jnp.maximum(_dot_const(kk * kk, hsum), 1e-24))
    k = k * (1.0 + (a - 1.0) * k_a)
    log_decay = -jnp.exp(w)
    return r, log_decay, k, v, -kk, kk * a, g


def _rwkv_post(y, r, k, v, g, ln_w, ln_b, r_k, hmean):
    hsum = hmean * RWKV_HEAD
    mean = _dot_const(y, hmean)
    yc = y - mean
    var = _dot_const(yc * yc, hmean)
    yn = yc * lax.rsqrt(var + RWKV_LN_EPS) * ln_w + ln_b
    bonus = _dot_const(r * k * r_k, hsum) * v
    return ((yn + bonus) * g,)


def _merge(gates, br_a, br_r):
    sg = jax.nn.sigmoid(gates)
    return (sg[:, :D_MODEL] * br_a + sg[:, D_MODEL:] * br_r,)


def _swiglu(gate, up):
    return (jax.nn.silu(gate) * up,)


def _ffn_in(f, w_gate_t, w_up_t, *, name):
    m, d = f.shape
    n = w_gate_t.shape[0]
    tm = _tile(m)

    def body(f_ref, wg_ref, wu_ref, g_ref, u_ref, a_ref):
        g = _dot(f_ref[...], wg_ref[...], "nt")
        u = _dot(f_ref[...], wu_ref[...], "nt")
        g_ref[...] = g.astype(g_ref.dtype)
        u_ref[...] = u.astype(u_ref.dtype)
        a_ref[...] = _swiglu(g, u)[0].astype(a_ref.dtype)

    spec = pl.BlockSpec((tm, n), lambda i: (i, 0))
    return pl.pallas_call(
        body, name=name, grid=(m // tm,),
        in_specs=[pl.BlockSpec((tm, d), lambda i: (i, 0)), _full(w_gate_t.shape), _full(w_up_t.shape)],
        out_specs=[spec] * 3, out_shape=[jax.ShapeDtypeStruct((m, n), bf16)] * 3,
        compiler_params=_params(("parallel",)),
    )(f, w_gate_t, w_up_t)


def _branch_merge(y_attn, y_rwkv, w_attn_t, w_rwkv_t, gates, *, name):
    m = y_attn.shape[0]
    tm = _tile(m, MM_ROWS)

    def body(ya_ref, yr_ref, wa_ref, wr_ref, g_ref, a_ref, r_ref, o_ref):
        br_a = _dot(ya_ref[...], wa_ref[...], "nt")
        br_r = _dot(yr_ref[...], wr_ref[...], "nt")
        a_ref[...] = br_a.astype(a_ref.dtype)
        r_ref[...] = br_r.astype(r_ref.dtype)
        o_ref[...] = _merge(g_ref[...].astype(f32), br_a, br_r)[0].astype(o_ref.dtype)

    rows = lambda a: pl.BlockSpec((tm, a.shape[1]), lambda i: (i, 0))
    spec = pl.BlockSpec((tm, D_MODEL), lambda i: (i, 0))
    return pl.pallas_call(
        body, name=name, grid=(m // tm,),
        in_specs=[rows(y_attn), rows(y_rwkv), _full(w_attn_t.shape), _full(w_rwkv_t.shape), rows(gates)],
        out_specs=[spec] * 3, out_shape=[jax.ShapeDtypeStruct((m, D_MODEL), bf16)] * 3,
        compiler_params=_params(("parallel",)),
    )(y_attn, y_rwkv, w_attn_t, w_rwkv_t, gates)


def _branch_merge_bwd(dh, w_o, gates, br_a, br_r, w_attn_t, w_rwkv_t, *, name):
    m = dh.shape[0]
    tm = _tile(m, MM_ROWS)

    def body(dh_ref, w_ref, g_ref, a_ref, r_ref, wa_ref, wr_ref, dg_ref, da_ref, dr_ref, dya_ref, dyr_ref):
        dmerged = _dot(dh_ref[...], w_ref[...], "nt")
        _, vjp = jax.vjp(lambda g, a, r: _merge(g, a, r)[0], g_ref[...].astype(f32), a_ref[...].astype(f32),
                         r_ref[...].astype(f32))
        dg, da, dr = vjp(dmerged)
        dg_ref[...] = dg.astype(dg_ref.dtype)
        da_ref[...] = da.astype(da_ref.dtype)
        dr_ref[...] = dr.astype(dr_ref.dtype)
        dya_ref[...] = _dot(da, wa_ref[...])
        dyr_ref[...] = _dot(dr, wr_ref[...])

    rows = lambda a: pl.BlockSpec((tm, a.shape[1]), lambda i: (i, 0))
    mixer = pl.BlockSpec((tm, w_attn_t.shape[1]), lambda i: (i, 0))
    return pl.pallas_call(
        body, name=name, grid=(m // tm,),
        in_specs=[rows(dh), _full(w_o.shape), rows(gates), rows(br_a), rows(br_r), _full(w_attn_t.shape),
                  _full(w_rwkv_t.shape)],
        out_specs=[rows(gates), rows(br_a), rows(br_r), mixer, mixer],
        out_shape=[jax.ShapeDtypeStruct(gates.shape, bf16), jax.ShapeDtypeStruct(br_a.shape, bf16),
                   jax.ShapeDtypeStruct(br_r.shape, bf16), jax.ShapeDtypeStruct((m, w_attn_t.shape[1]), f32),
                   jax.ShapeDtypeStruct((m, w_rwkv_t.shape[1]), f32)],
        compiler_params=_params(("parallel",)),
    )(dh, w_o, gates, br_a, br_r, w_attn_t, w_rwkv_t)


def _ffn_in_bwd(dh, w_down, gate, up, *, name):
    m, d = dh.shape
    n = w_down.shape[0]
    tm = _tile(m)

    def body(dh_ref, w_ref, g_ref, u_ref, dg_ref, du_ref):
        dact = _dot(dh_ref[...], w_ref[...], "nt")
        _, vjp = jax.vjp(lambda a, b: _swiglu(a, b)[0], g_ref[...].astype(f32), u_ref[...].astype(f32))
        dg, du = vjp(dact)
        dg_ref[...] = dg.astype(dg_ref.dtype)
        du_ref[...] = du.astype(du_ref.dtype)

    spec = pl.BlockSpec((tm, n), lambda i: (i, 0))
    return pl.pallas_call(
        body, name=name, grid=(m // tm,),
        in_specs=[pl.BlockSpec((tm, d), lambda i: (i, 0)), _full(w_down.shape), spec, spec],
        out_specs=[spec] * 2, out_shape=[jax.ShapeDtypeStruct((m, n), bf16)] * 2,
        compiler_params=_params(("parallel",)),
    )(dh, w_down, gate, up)


def _ffn_bwd(dh, w_down, gate, up, w_gate_t, w_up_t, h, g, *, name):
    m, d = dh.shape
    n = w_down.shape[0]
    tm = _tile(m)
    halves = [(c * (n // 2), (c + 1) * (n // 2)) for c in range(2)]

    def body(dh_ref, wd_ref, g_ref, u_ref, wg_ref, wu_ref, h_ref, gn_ref, dg_ref, du_ref, dh1_ref, dgn_ref):
        dn = jnp.zeros((tm, d), f32)
        for lo, hi in halves:
            dact = _dot(dh_ref[...], wd_ref[lo:hi, :], "nt")
            _, vjp = jax.vjp(lambda a, b: _swiglu(a, b)[0], g_ref[:, lo:hi].astype(f32), u_ref[:, lo:hi].astype(f32))
            dg, du = (t.astype(bf16) for t in vjp(dact))
            dg_ref[:, lo:hi] = dg
            du_ref[:, lo:hi] = du
            dn = dn + _dot(dg, wg_ref[lo:hi, :]) + _dot(du, wu_ref[lo:hi, :])
        _, vjp = jax.vjp(lambda hv, gv: (_rms(hv, gv), hv), h_ref[...], gn_ref[...])
        dh1, dgn = vjp((dn, dh_ref[...]))
        dh1_ref[...] = dh1

        @pl.when(pl.program_id(0) == 0)
        def _():
            dgn_ref[...] = jnp.zeros_like(dgn_ref)

        dgn_ref[...] += dgn

    wide = pl.BlockSpec((tm, n), lambda i: (i, 0))
    tile = pl.BlockSpec((tm, d), lambda i: (i, 0))
    return pl.pallas_call(
        body, name=name, grid=(m // tm,),
        in_specs=[tile, _full(w_down.shape), wide, wide, _full(w_gate_t.shape), _full(w_up_t.shape), tile,
                  _full(g.shape)],
        out_specs=[wide, wide, tile, _full(g.shape)],
        out_shape=[jax.ShapeDtypeStruct((m, n), bf16)] * 2 + [jax.ShapeDtypeStruct((m, d), f32),
                                                               jax.ShapeDtypeStruct(g.shape, f32)],
        compiler_params=_params(("arbitrary",)),
    )(dh, w_down, gate, up, w_gate_t, w_up_t, h, g)


HALO = 16


def _previous_rows(x, before_ref, first_tile):
    rows = lax.broadcasted_iota(jnp.int32, x.shape, 0)
    last = jnp.where(first_tile, 0.0, before_ref[HALO - 1:HALO, :].astype(f32))
    return jnp.where(rows == 0, last, pltpu.roll(x, 1, axis=0))


def _mixer_inputs(ps, mixes, params, *, name):
    m = ps[0].shape[0]
    tm = _tile(m)
    sub = tm // HALO
    n_par = len(params)

    def body(*refs):
        first = pl.program_id(0) == 0
        pf = []
        for k in range(2):
            x = refs[k][...].astype(f32)
            pf.append(x + (_previous_rows(x, refs[2 + k], first) - x) * refs[4 + k][...])
        res = _rwkv_prep(*pf, *[ref[...] for ref in refs[6:6 + n_par]])
        for o_ref, val in zip(refs[6 + n_par:], res):
            o_ref[...] = val

    tile = lambda a: pl.BlockSpec((tm, a.shape[1]), lambda i: (i, 0))
    before = lambda a: pl.BlockSpec((HALO, a.shape[1]), lambda i: (jnp.maximum(i * sub - 1, 0), 0))
    out = pl.BlockSpec((tm, RWKV_DIM), lambda i: (i, 0))
    return pl.pallas_call(
        body, name=name, grid=(m // tm,),
        in_specs=[tile(a) for a in ps] + [before(a) for a in ps] + [_full(a.shape) for a in mixes + params],
        out_specs=[out] * 7, out_shape=[jax.ShapeDtypeStruct((m, RWKV_DIM), f32)] * 7,
        compiler_params=_params(("parallel",)),
    )(*ps, *ps, *mixes, *params)


def _mixer_inputs_bwd(ps, mixes, params, cts, *, name):
    m = ps[0].shape[0]
    tm = _tile(m)
    sub = tm // HALO
    nt = m // tm
    n_par = len(params)
    flat_cts = [c for group in cts for c in group]
    n_ct = len(flat_cts)

    def body(*refs):
        i = pl.program_id(0)
        tile_index = nt - 1 - i
        ct_refs = refs[6 + n_par:6 + n_par + n_ct]
        dp_refs = refs[6 + n_par + n_ct:8 + n_par + n_ct]
        dmix_refs = refs[8 + n_par + n_ct:10 + n_par + n_ct]
        dpar_refs = refs[10 + n_par + n_ct:9 + 2 * n_par + n_ct]
        carries = refs[9 + 2 * n_par + n_ct:]
        rows1 = tile_index * tm + lax.broadcasted_iota(jnp.int32, (tm, 1), 0)
        live = rows1 >= PAD

        @pl.when(i == 0)
        def _():
            for ref in (*dmix_refs, *dpar_refs, *carries):
                ref[...] = jnp.zeros_like(ref)

        xs, prevs, pf = [], [], []
        for k in range(2):
            x = refs[k][...].astype(f32)
            xp = _previous_rows(x, refs[2 + k], tile_index == 0)
            xs.append(x)
            prevs.append(xp)
            pf.append(x + (xp - x) * refs[4 + k][...])
        ct_vals, pos = [], 0
        for group in cts:
            acc = ct_refs[pos][...].astype(f32)
            for extra in range(1, len(group)):
                acc = acc + ct_refs[pos + extra][...].astype(f32)
            pos += len(group)
            ct_vals.append(jnp.where(live, acc, 0.0))
        par_vals = [ref[...] for ref in refs[6:6 + n_par]]
        _, vjp = jax.vjp(lambda *args: _rwkv_prep(*args, par_vals[-1]), *pf, *par_vals[:-1])
        g = vjp(tuple(ct_vals))
        for k in range(2):
            dpf = g[k]
            mixv = refs[4 + k][...]
            dm = dpf * mixv
            rows = lax.broadcasted_iota(jnp.int32, dm.shape, 0)
            dm_next = jnp.where(rows == tm - 1, carries[k][...], pltpu.roll(dm, tm - 1, axis=0))
            dp_refs[k][...] = jnp.where(live, dpf - dm + dm_next, 0.0).astype(dp_refs[k].dtype)
            carries[k][...] = dm[0:1, :]
            dmix_refs[k][...] += jnp.sum(dpf * (prevs[k] - xs[k]), axis=0, keepdims=True)
        for ref, val in zip(dpar_refs, g[2:]):
            ref[...] += val

    tile = lambda a: pl.BlockSpec((tm, a.shape[1]), lambda i: (nt - 1 - i, 0))
    before = lambda a: pl.BlockSpec((HALO, a.shape[1]), lambda i: (jnp.maximum((nt - 1 - i) * sub - 1, 0), 0))
    return pl.pallas_call(
        body, name=name, grid=(nt,),
        in_specs=[tile(a) for a in ps] + [before(a) for a in ps] + [_full(a.shape) for a in mixes + params]
        + [tile(c) for c in flat_cts],
        out_specs=[tile(a) for a in ps] + [_full(a.shape) for a in mixes + params[:-1]],
        out_shape=[jax.ShapeDtypeStruct(a.shape, bf16) for a in ps]
        + [jax.ShapeDtypeStruct(a.shape, f32) for a in mixes + params[:-1]],
        scratch_shapes=[pltpu.VMEM((1, a.shape[1]), f32) for a in ps],
        compiler_params=_params(("arbitrary",)),
    )(*ps, *ps, *mixes, *params, *flat_cts)


def _attn_masks(blk):
    qi = lax.broadcasted_iota(jnp.int32, (BLOCK, BLOCK), 0)
    ki = lax.broadcasted_iota(jnp.int32, (BLOCK, BLOCK), 1)
    qpos = blk * BLOCK + qi - PAD
    kpos_c = blk * BLOCK + ki - PAD
    kpos_p = kpos_c - BLOCK
    kpos_m = ki - PAD

    def band(kpos):
        return (kpos >= N_META) & (kpos <= qpos) & (qpos - kpos < WINDOW)

    return band(kpos_p), band(kpos_c), (kpos_m >= 0) & (kpos_m <= qpos)


def _attn_probs(qs, k3s, sink, oks):
    s = [[jnp.where(ok, _dot(qh, kx, "nt"), NEG_INF) for kx, ok in zip(k3, oks)] for qh, k3 in zip(qs, k3s)]
    mx = [jnp.maximum(jnp.maximum(jnp.max(t[0], -1, keepdims=True), jnp.max(t[1], -1, keepdims=True)),
                      jnp.maximum(jnp.max(t[2], -1, keepdims=True), sk)) for t, sk in zip(s, sink)]
    e = [[jnp.exp(tx - m) for tx in t] for t, m in zip(s, mx)]
    e_sink = [jnp.exp(sk - m) for sk, m in zip(sink, mx)]
    inv = [1.0 / (jnp.sum(t[0], -1, keepdims=True) + jnp.sum(t[1], -1, keepdims=True)
                  + jnp.sum(t[2], -1, keepdims=True) + es) for t, es in zip(e, e_sink)]
    return [[tx * i for tx in t] for t, i in zip(e, inv)], [es * i for es, i in zip(e_sink, inv)]


def _head_cols(i):
    return slice(i * HEAD_DIM, (i + 1) * HEAD_DIM)


def _attn_operands(refs):
    q_ref, kp_ref, kc_ref, km_ref, vp_ref, vc_ref, vm_ref, s_ref = refs
    qs = [q_ref[:, _head_cols(i)] * (HEAD_DIM ** -0.5) for i in range(Q_HEADS)]
    k3 = [[ref[:, _head_cols(h)] for ref in (kp_ref, kc_ref, km_ref)] for h in range(KV_HEADS)]
    v3 = [[ref[:, _head_cols(h)] for ref in (vp_ref, vc_ref, vm_ref)] for h in range(KV_HEADS)]
    return (qs, [k3[i // GROUP] for i in range(Q_HEADS)], [v3[i // GROUP] for i in range(Q_HEADS)],
            [s_ref[:, i:i + 1] for i in range(Q_HEADS)])


def _attention(q, k, v, sinks, *, name):
    lp = q.shape[0]
    nb = lp // BLOCK
    prev = lambda i: (jnp.maximum(i - 1, 0), 0)
    cur = lambda i: (i, 0)
    meta = lambda i: (0, 0)
    kv = lambda index: pl.BlockSpec((BLOCK, KV_W), index)

    def body(*refs):
        o_ref = refs[-1]
        qs, k3s, v3s, sink = _attn_operands(refs[:-1])
        p, _ = _attn_probs(qs, k3s, sink, _attn_masks(pl.program_id(0)))
        out = [_dot(ph[0], v3[0]) + _dot(ph[1], v3[1]) + _dot(ph[2], v3[2]) for ph, v3 in zip(p, v3s)]
        for i in range(Q_HEADS):
            o_ref[:, _head_cols(i)] = out[i].astype(o_ref.dtype)

    return pl.pallas_call(
        body, name=name, grid=(nb,),
        in_specs=[pl.BlockSpec((BLOCK, Q_W), cur), kv(prev), kv(cur), kv(meta), kv(prev), kv(cur), kv(meta),
                  _full((1, Q_HEADS))],
        out_specs=pl.BlockSpec((BLOCK, Q_W), cur),
        out_shape=jax.ShapeDtypeStruct((lp, Q_W), bf16),
        compiler_params=_params(("parallel",)),
    )(q, k, k, k, v, v, v, sinks)


def _attention_bwd(q, k, v, sinks, out, do, *, name):
    lp = q.shape[0]
    nb = lp // BLOCK
    cur = lambda n: (jnp.minimum(n, nb - 1), 0)
    prev = lambda n: (jnp.maximum(jnp.minimum(n, nb - 1) - 1, 0), 0)
    behind = lambda n: (jnp.maximum(n - 1, 0), 0)
    meta = lambda n: (0, 0)
    kv = lambda index: pl.BlockSpec((BLOCK, KV_W), index)
    scale = HEAD_DIM ** -0.5

    def body(*refs):
        ins, fwd_ref, do_ref = refs[:8], refs[8], refs[9]
        dq_ref, dk_ref, dv_ref, dkm_ref, dvm_ref, ds_ref, carry_k, carry_v = refs[10:]
        n = pl.program_id(0)

        @pl.when(n == 0)
        def _():
            for ref in (dkm_ref, dvm_ref, ds_ref, carry_k, carry_v):
                ref[...] = jnp.zeros_like(ref)

        @pl.when(n < nb)
        def _():
            qs, k3s, v3s, sink = _attn_operands(ins)
            do = [do_ref[:, _head_cols(i)] for i in range(Q_HEADS)]
            p, p_sink = _attn_probs(qs, k3s, sink, _attn_masks(n))
            delta = [jnp.sum(d * fwd_ref[:, _head_cols(i)].astype(f32), -1, keepdims=True) for i, d in enumerate(do)]
            dp = [[_dot(d, vx, "nt") for vx in v3] for d, v3 in zip(do, v3s)]
            ds = [[px * (dx - dl) for px, dx in zip(ph, dh)] for ph, dh, dl in zip(p, dp, delta)]
            dq = [_dot(dsh[0], k3[0]) + _dot(dsh[1], k3[1]) + _dot(dsh[2], k3[2]) for dsh, k3 in zip(ds, k3s)]
            for i in range(Q_HEADS):
                dq_ref[:, _head_cols(i)] = dq[i] * scale
                ds_ref[:, i:i + 1] -= jnp.sum(p_sink[i] * delta[i], axis=0, keepdims=True)
            for h in range(KV_HEADS):
                group = slice(h * GROUP, (h + 1) * GROUP)
                q_all = jnp.concatenate(qs[group], axis=0)
                do_all = jnp.concatenate(do[group], axis=0)
                dk3 = [_dot(jnp.concatenate([dsh[x] for dsh in ds[group]], axis=0), q_all, "tn") for x in range(3)]
                dv3 = [_dot(jnp.concatenate([ph[x] for ph in p[group]], axis=0), do_all, "tn") for x in range(3)]
                hs = _head_cols(h)
                for out_ref, carry, meta_ref, d3 in ((dk_ref, carry_k, dkm_ref, dk3),
                                                     (dv_ref, carry_v, dvm_ref, dv3)):
                    out_ref[:, hs] = carry[:, hs] + d3[0]
                    carry[:, hs] = d3[1]
                    meta_ref[:, hs] += d3[2]

        @pl.when(n == nb)
        def _():
            dk_ref[...] = carry_k[...]
            dv_ref[...] = carry_v[...]

    kv_shape = jax.ShapeDtypeStruct((lp, KV_W), f32)
    one_shape = jax.ShapeDtypeStruct((BLOCK, KV_W), f32)
    return pl.pallas_call(
        body, name=name, grid=(nb + 1,),
        in_specs=[pl.BlockSpec((BLOCK, Q_W), cur), kv(prev), kv(cur), kv(meta), kv(prev), kv(cur), kv(meta),
                  _full((1, Q_HEADS)), pl.BlockSpec((BLOCK, Q_W), cur), pl.BlockSpec((BLOCK, Q_W), cur)],
        out_specs=[pl.BlockSpec((BLOCK, Q_W), cur), kv(behind), kv(behind), kv(meta), kv(meta),
                   _full((1, Q_HEADS))],
        out_shape=[jax.ShapeDtypeStruct((lp, Q_W), f32), kv_shape, kv_shape, one_shape, one_shape,
                   jax.ShapeDtypeStruct((1, Q_HEADS), f32)],
        scratch_shapes=[pltpu.VMEM((BLOCK, KV_W), f32), pltpu.VMEM((BLOCK, KV_W), f32)],
        compiler_params=_params(("arbitrary",)),
    )(q, k, k, k, v, v, v, sinks, out, do)


@jax.custom_vjp
def _known_inverse(l, x):
    return x


def _known_inverse_fwd(l, x):
    return x, x


def _known_inverse_bwd(x, ct):
    return _dot(_dot(x, ct, "tn"), x, "nt"), jnp.zeros_like(x)


_known_inverse.defvjp(_known_inverse_fwd, _known_inverse_bwd)


@jax.custom_vjp
def _decayed(x, c):
    return (x * jnp.exp(c)).astype(bf16).astype(f32)


def _decayed_fwd(x, c):
    e = jnp.exp(c)
    out = (x * e).astype(bf16).astype(f32)
    return out, (e, out)


def _decayed_bwd(res, ct):
    e, out = res
    return ct * e, ct * out


_decayed.defvjp(_decayed_fwd, _decayed_bwd)


@jax.custom_vjp
def _pair(x, y):
    return _dot(x, y, "nt")


def _pair_fwd(x, y):
    return _dot(x, y, "nt"), (x, y)


def _pair_bwd(res, ct):
    x, y = res
    hi = ct.astype(bf16)
    lo = (ct - hi.astype(f32)).astype(bf16)
    return _dot(hi, y) + _dot(lo, y), _dot(hi, x, "tn") + _dot(lo, x, "tn")


_pair.defvjp(_pair_fwd, _pair_bwd)


def _scan_chunk(s0, r, lw, k, v, a, b, inv=None):
    t = r[0].shape[0]
    ii = lax.broadcasted_iota(jnp.int32, (t, t), 0)
    jj = lax.broadcasted_iota(jnp.int32, (t, t), 1)
    incl = jj <= ii
    strict = jj < ii
    tri = incl.astype(f32)
    eye = jnp.where(ii == jj, 1.0, 0.0)
    cl = [_const_dot(tri, x) for x in lw]
    mid = [c[t // 2 - 1:t // 2, :] for c in cl]
    s0 = [s * jnp.exp(m) for s, m in zip(s0, mid)]
    cl = [c - m for c, m in zip(cl, mid)]
    rt = [_decayed(x, c) for x, c in zip(r, cl)]
    at = [_decayed(x, c - l) for x, c, l in zip(a, cl, lw)]
    bt = [_decayed(x, -c) for x, c in zip(b, cl)]
    kt = [_decayed(x, -c) for x, c in zip(k, cl)]
    l_ab = [jnp.where(strict, _pair(x, y), 0.0) for x, y in zip(at, bt)]
    l_ak = [jnp.where(strict, _pair(x, y), 0.0) for x, y in zip(at, kt)]
    r_b = [jnp.where(incl, _pair(x, y), 0.0) for x, y in zip(rt, bt)]
    r_k = [jnp.where(incl, _pair(x, y), 0.0) for x, y in zip(rt, kt)]
    if inv is None:
        inv = [eye + x for x in l_ab]
        pw = l_ab
        for _ in range(int(math.log2(t)) - 1):
            pw = [_dot(x, x) for x in pw]
            inv = [x + _dot(x, y) for x, y in zip(inv, pw)]
    else:
        inv = [_known_inverse(x, y) for x, y in zip(l_ab, inv)]
    rhs = [_dot(x, s, "nt") + _dot(m, y) for x, s, m, y in zip(at, s0, l_ak, v)]
    u = [_dot(x, y) for x, y in zip(inv, rhs)]
    y_s = [_dot(x, s, "nt") for x, s in zip(rt, s0)]
    y = [ys + _dot(m, uu) + _dot(n, vv) for ys, m, uu, n, vv in zip(y_s, r_b, u, r_k, v)]
    grow = [s + _dot(uu, x, "tn") + _dot(vv, z, "tn") for s, uu, x, vv, z in zip(s0, u, bt, v, kt)]
    s1 = [g * jnp.exp(c[t - 1:t, :]) for g, c in zip(grow, cl)]
    return y, s1, inv


def _head_rows(h):
    return slice(h * RWKV_HEAD, (h + 1) * RWKV_HEAD)


def _per_head(ref):
    return [ref[:, _head_rows(h)] for h in range(RWKV_HEADS)]


def _scan(r, lw, k, v, a, b, *, name):
    lp = r.shape[0]
    nc = lp // CHUNK
    row = pl.BlockSpec((CHUNK, RWKV_DIM), lambda c: (c, 0))

    def body(r_ref, lw_ref, k_ref, v_ref, a_ref, b_ref, y_ref, s_ref, inv_ref, state):
        @pl.when(pl.program_id(0) == 0)
        def _():
            state[...] = jnp.zeros_like(state)

        s_ref[...] = state[...]
        s0 = [state[_head_rows(h), :] for h in range(RWKV_HEADS)]
        y, s1, inv = _scan_chunk(s0, *[_per_head(ref) for ref in (r_ref, lw_ref, k_ref, v_ref, a_ref, b_ref)])
        for h in range(RWKV_HEADS):
            y_ref[:, _head_rows(h)] = y[h]
            state[_head_rows(h), :] = s1[h]
            inv_ref[h * CHUNK:(h + 1) * CHUNK, :] = inv[h].astype(inv_ref.dtype)

    return pl.pallas_call(
        body, name=name, grid=(nc,), in_specs=[row] * 6,
        out_specs=[row, pl.BlockSpec((RWKV_DIM, RWKV_HEAD), lambda c: (c, 0)),
                   pl.BlockSpec((RWKV_HEADS * CHUNK, CHUNK), lambda c: (c, 0))],
        out_shape=[jax.ShapeDtypeStruct((lp, RWKV_DIM), f32), jax.ShapeDtypeStruct((nc * RWKV_DIM, RWKV_HEAD), f32),
                   jax.ShapeDtypeStruct((nc * RWKV_HEADS * CHUNK, CHUNK), bf16)],
        scratch_shapes=[pltpu.VMEM((RWKV_DIM, RWKV_HEAD), f32)],
        compiler_params=_params(("arbitrary",)),
    )(r, lw, k, v, a, b)


def _scan_bwd(r, lw, k, v, a, b, states, inverses, dy, *, name):
    lp = r.shape[0]
    nc = lp // CHUNK
    back = lambda c: (nc - 1 - c, 0)
    row = pl.BlockSpec((CHUNK, RWKV_DIM), back)

    def body(r_ref, lw_ref, k_ref, v_ref, a_ref, b_ref, s_ref, inv_ref, dy_ref,
             dr_ref, dlw_ref, dk_ref, dv_ref, da_ref, db_ref, dstate):
        @pl.when(pl.program_id(0) == 0)
        def _():
            dstate[...] = jnp.zeros_like(dstate)

        outs = (dr_ref, dlw_ref, dk_ref, dv_ref, da_ref, db_ref)
        s0 = [s_ref[_head_rows(h), :] for h in range(RWKV_HEADS)]
        inv = [inv_ref[h * CHUNK:(h + 1) * CHUNK, :].astype(f32) for h in range(RWKV_HEADS)]
        _, vjp = jax.vjp(lambda *args: _scan_chunk(*args, inv=inv)[:2], s0,
                         *[_per_head(ref) for ref in (r_ref, lw_ref, k_ref, v_ref, a_ref, b_ref)])
        g = vjp((_per_head(dy_ref), [dstate[_head_rows(h), :] for h in range(RWKV_HEADS)]))
        for h in range(RWKV_HEADS):
            dstate[_head_rows(h), :] = g[0][h]
            for o_ref, gv in zip(outs, g[1:]):
                o_ref[:, _head_rows(h)] = gv[h]

    shape = jax.ShapeDtypeStruct((lp, RWKV_DIM), f32)
    return pl.pallas_call(
        body, name=name, grid=(nc,),
        in_specs=[row] * 6 + [pl.BlockSpec((RWKV_DIM, RWKV_HEAD), back),
                              pl.BlockSpec((RWKV_HEADS * CHUNK, CHUNK), back), row],
        out_specs=[row] * 6, out_shape=[shape] * 6,
        scratch_shapes=[pltpu.VMEM((RWKV_DIM, RWKV_HEAD), f32)],
        compiler_params=_params(("arbitrary",)),
    )(r, lw, k, v, a, b, states, inverses, dy)


def _loss_head(act, w_down, h1, target, g_final, *, name):
    lp = h1.shape[0]
    per_tile = 3
    tm = per_tile * BLOCK
    last_block = (lp - FRONT) // BLOCK - 1

    def body(a_ref, w_ref, h_ref, t0_ref, t1_ref, t2_ref, g_ref, loss_ref, dh_ref, dg_ref):
        i = pl.program_id(0)
        target_rows = jnp.concatenate([t0_ref[...], t1_ref[...], t2_ref[...]], axis=0)
        real = i * tm + lax.broadcasted_iota(jnp.int32, (tm, 1), 0) >= FRONT

        def tile_loss(hv, gv):
            err = _rms(hv, gv) - target_rows
            return 0.5 * jnp.sum(jnp.where(real, jnp.mean(err * err, axis=-1, keepdims=True), 0.0))

        h2 = _dot(a_ref[...], w_ref[...], "nn") + h_ref[...]
        loss, (dh, dg) = jax.value_and_grad(tile_loss, argnums=(0, 1))(h2, g_ref[...])

        @pl.when(i == 0)
        def _():
            loss_ref[...] = jnp.zeros_like(loss_ref)
            dg_ref[...] = jnp.zeros_like(dg_ref)

        loss_ref[...] += jnp.full(loss_ref.shape, loss, f32)
        dg_ref[...] += dg
        dh_ref[...] = dh

    def target_block(j):
        return pl.BlockSpec((BLOCK, D_MODEL),
                            lambda i: (jnp.clip(per_tile * i + j - FRONT // BLOCK, 0, last_block), 0))

    return pl.pallas_call(
        body, name=name, grid=(lp // tm,),
        in_specs=[pl.BlockSpec((tm, act.shape[1]), lambda i: (i, 0)), _full(w_down.shape),
                  pl.BlockSpec((tm, D_MODEL), lambda i: (i, 0)), target_block(0), target_block(1), target_block(2),
                  _full(g_final.shape)],
        out_specs=[_full((8, 128)), pl.BlockSpec((tm, D_MODEL), lambda i: (i, 0)), _full(g_final.shape)],
        out_shape=[jax.ShapeDtypeStruct((8, 128), f32), jax.ShapeDtypeStruct((lp, D_MODEL), f32),
                   jax.ShapeDtypeStruct(g_final.shape, f32)],
        compiler_params=_params(("arbitrary",)),
    )(act, w_down, h1, target, target, target, g_final)


def _embed_norm(x, meta, g, *, name):
    seq = x.shape[0]
    lp = seq + FRONT
    per_tile = 3
    tm = per_tile * BLOCK
    last_block = seq // BLOCK - 1

    def body(x0_ref, x1_ref, x2_ref, meta_ref, g_ref, h_ref, u_ref):
        front = jnp.concatenate([jnp.zeros((PAD, D_MODEL), f32), meta_ref[...]], axis=0)
        first = jnp.where(pl.program_id(0) == 0, front, x0_ref[...])
        h = jnp.concatenate([first, x1_ref[...], x2_ref[...]], axis=0)
        h_ref[...] = h
        u_ref[...] = _rms(h, g_ref[...]).astype(u_ref.dtype)

    def x_block(j):
        return pl.BlockSpec((BLOCK, D_MODEL),
                            lambda i: (jnp.clip(per_tile * i + j - FRONT // BLOCK, 0, last_block), 0))

    tile = pl.BlockSpec((tm, D_MODEL), lambda i: (i, 0))
    return pl.pallas_call(
        body, name=name, grid=(lp // tm,),
        in_specs=[x_block(0), x_block(1), x_block(2), _full(meta.shape), _full(g.shape)],
        out_specs=[tile, tile],
        out_shape=[jax.ShapeDtypeStruct((lp, D_MODEL), f32), jax.ShapeDtypeStruct((lp, D_MODEL), bf16)],
        compiler_params=_params(("parallel",)),
    )(x, x, x, meta, g)


def _input_norm_bwd(h0, g, du, dh1, *, name):
    lp = h0.shape[0]
    blocks = (lp - FRONT) // FRONT
    per_tile = max(n for n in (4, 3, 2, 1) if blocks % n == 0)
    ins = (h0, du, dh1)

    def body(*refs):
        tiles = [refs[k * per_tile:(k + 1) * per_tile] for k in range(len(ins))]
        front_refs = refs[len(ins) * per_tile:len(ins) * (per_tile + 1)]
        g_ref, dx_ref, front_ref, dg_ref = refs[len(ins) * (per_tile + 1):]

        def cotangents(h_ref, du_ref, dh1_ref):
            _, vjp = jax.vjp(lambda hv, gv: (_rms(hv, gv), hv), h_ref[...], g_ref[...])
            return vjp((du_ref[...].astype(f32), dh1_ref[...]))

        @pl.when(pl.program_id(0) == 0)
        def _():
            front_ref[...], dg_ref[...] = cotangents(*front_refs)

        dg = jnp.zeros(dg_ref.shape, f32)
        for j in range(per_tile):
            dh, dg_j = cotangents(*(t[j] for t in tiles))
            dx_ref[j * FRONT:(j + 1) * FRONT, :] = dh
            dg = dg + dg_j
        dg_ref[...] += dg

    def block(j):
        return pl.BlockSpec((FRONT, D_MODEL), lambda i: (per_tile * i + j + 1, 0))

    first = pl.BlockSpec((FRONT, D_MODEL), lambda i: (0, 0))
    return pl.pallas_call(
        body, name=name, grid=(blocks // per_tile,),
        in_specs=[block(j) for _ in ins for j in range(per_tile)] + [first] * len(ins) + [_full(g.shape)],
        out_specs=[pl.BlockSpec((per_tile * FRONT, D_MODEL), lambda i: (i, 0)), _full((FRONT, D_MODEL)),
                   _full(g.shape)],
        out_shape=[jax.ShapeDtypeStruct((lp - FRONT, D_MODEL), f32), jax.ShapeDtypeStruct((FRONT, D_MODEL), f32),
                   jax.ShapeDtypeStruct(g.shape, f32)],
        compiler_params=_params(("arbitrary",)),
    )(*(a for a in ins for _ in range(per_tile)), *ins, g)


def _local_step(x, target, meta, p, early_weights=None, late_weights=None, emit=None):
    emit = emit or (lambda group, grads: 0.0)
    seq = x.shape[0]
    lp = seq + FRONT
    cos_t, sin_t, swap = _rope_tables(lp)
    hsum = _head_sum_matrix(RWKV_DIM, RWKV_HEAD)
    hmean = hsum / RWKV_HEAD
    post_params = [p["ln_w"], p["ln_b"], p["r_k"], hmean]

    h0, u = _embed_norm(x, meta, p["norm_mix_g"], name="norm_mix")
    if early_weights is not None:
        p = {**p, **early_weights(u)}
    prep_params = [p["w0"], p["w2"], p["a0"], p["a2"], p["g2"], p["k_k"], p["k_a"], hsum]
    in_widths = [ATTN_PROJ, RKV_W, LORA_W, 2 * D_MODEL]
    q, k, v, p_rkv, p_lora, gates = _proj_in(u, p["w_in_lr"], p["b_in"], in_widths,
                                             (cos_t, sin_t, swap), name="proj_in", zero_rows_below=PAD)
    y_attn = _attention(q, k, v, p["sinks"], name="attention")

    mix_rkv, mix_lora = p["mix"][:, :RKV_W], p["mix"][:, RKV_W:]
    r_, lw_, k_, v_, a_, b_, g_ = _mixer_inputs([p_rkv, p_lora], [mix_rkv, mix_lora], prep_params,
                                                name="mixer_inputs")
    y_scan, states, inverses = _scan(r_, lw_, k_, v_, a_, b_, name="wkv_scan")
    (y_rwkv,) = _rowwise(_rwkv_post, [y_scan, r_, k_, v_, g_], post_params, [(RWKV_DIM, bf16)], name="rwkv_post")

    if late_weights is not None:
        p = {**p, **late_weights(y_rwkv)}
    br_a, br_r, merged = _branch_merge(y_attn, y_rwkv, p["w_br_attn_t"], p["w_br_rwkv_t"], gates, name="branch_merge")
    h1, f = _residual_norm(merged, p["w_o"], h0, p["norm_ffn_g"], name="out_proj")
    gate, up, act = _ffn_in(f, p["w_gate_t"], p["w_up_t"], name="ffn_in")

    loss8, dh2, d_final_g = _loss_head(act, p["w_down"], h1, target, p["norm_final_g"], name="loss_head")
    dgate, dup, dh1, d_ffn_g = _ffn_bwd(dh2, p["w_down"], gate, up, p["w_gate_t"], p["w_up_t"], h1, p["norm_ffn_g"],
                                        name="ffn_bwd")
    d_w_down = _mm_tn(act, dh2, name="dw_down")
    d_w_gate_t = _mm_tn(dgate, f, name="dw_gate")
    d_w_up_t = _mm_tn(dup, f, name="dw_up")
    zero = emit("ffn", dict(w_down=d_w_down, w_gate_t=d_w_gate_t, w_up_t=d_w_up_t))
    dgates, dbr_a, dbr_r, dy_attn, dy_rwkv = _branch_merge_bwd(
        dh1, p["w_o"], gates, br_a, br_r, p["w_br_attn_t"], p["w_br_rwkv_t"], name="branch_merge_bwd")
    d_w_o = _mm_tn(merged, dh1, name="dw_o")
    d_w_br_attn_t = _mm_tn(dbr_a, y_attn, name="dw_br_attn")
    d_w_br_rwkv_t = _mm_tn(dbr_r, y_rwkv, name="dw_br_rwkv")
    zero = zero + emit("branch", dict(w_o=d_w_o, w_br_attn_t=d_w_br_attn_t, w_br_rwkv_t=d_w_br_rwkv_t))

    post_params = [p["ln_w"] + zero, p["ln_b"], p["r_k"], hmean]
    res = _rowwise_bwd(_rwkv_post, [y_scan, r_, k_, v_, g_], post_params, [[dy_rwkv]], name="rwkv_post_bwd",
                       diff_rows=[True] * 5, diff_params=[True, True, True, False])
    dy_scan, dr_p, dk_p, dv_p, dg_p, d_ln_w, d_ln_b, d_r_k = res
    dr_s, dlw_s, dk_s, dv_s, da_s, db_s = _scan_bwd(r_, lw_, k_, v_, a_, b_, states, inverses, dy_scan,
                                                    name="wkv_scan_bwd")
    res = _mixer_inputs_bwd([p_rkv, p_lora], [mix_rkv, mix_lora], prep_params,
                            [[dr_s, dr_p], [dlw_s], [dk_s, dk_p], [dv_s, dv_p], [da_s], [db_s], [dg_p]],
                            name="mixer_inputs_bwd")
    dp_rkv, dp_lora, d_mix_rkv, d_mix_lora, d_w0, d_w2, d_a0, d_a2, d_g2, d_k_k, d_k_a = res

    dq, dk, dv, dkm, dvm, d_sinks = _attention_bwd(q, k, v, p["sinks"], y_attn, dy_attn, name="attention_bwd")
    (dqkv,) = _rowwise(_attn_prep_transposed, [dq, dk, dv, cos_t, sin_t], [swap, dkm, dvm], [(ATTN_PROJ, bf16)],
                       name="attn_prep_bwd")

    in_rows, (at_qkv, at_rkv, at_lora, at_gates) = p["w_in_lr"].shape[1], [off for off, _ in _pieces(in_widths)]
    d_w_in_t, db_gates = _mm_tn(dgates, u, name="dw_gates", colsum=True, into=(in_rows, at_gates, None))
    d_w_in_t, db_rkv = _mm_tn(dp_rkv, u, name="dw_rkv", colsum=True, into=(in_rows, at_rkv, d_w_in_t))
    d_w_in_t, db_lora = _mm_tn(dp_lora, u, name="dw_lora", colsum=True, into=(in_rows, at_lora, d_w_in_t))
    d_w_in_t, db_qkv = _mm_tn(dqkv, u, name="dw_qkv", colsum=True, into=(in_rows, at_qkv, d_w_in_t))
    zero = emit("input", dict(w_in_t=d_w_in_t, g2=d_g2, w2=d_w2, a2=d_a2))
    du = _proj_in_bwd([dqkv, dp_rkv, dp_lora, dgates], p["w_in_lr"], name="d_u")
    dx, d_front, d_mix_g = _input_norm_bwd(h0, p["norm_mix_g"] + zero, du, dh1, name="norm_mix_bwd")

    grads = dict(
        w_in_t=d_w_in_t,
        b_in=jnp.concatenate([db_qkv, db_rkv, db_lora, db_gates], axis=1),
        mix=jnp.concatenate([d_mix_rkv, d_mix_lora], axis=1),
        norm_mix_g=d_mix_g, sinks=d_sinks, w0=d_w0, w2=d_w2, a0=d_a0, a2=d_a2, g2=d_g2, k_k=d_k_k, k_a=d_k_a,
        r_k=d_r_k, ln_w=d_ln_w, ln_b=d_ln_b, w_br_attn_t=d_w_br_attn_t, w_br_rwkv_t=d_w_br_rwkv_t, w_o=d_w_o,
        norm_ffn_g=d_ffn_g, w_gate_t=d_w_gate_t, w_up_t=d_w_up_t, w_down=d_w_down, norm_final_g=d_final_g,
        meta=d_front[PAD:],
    )
    return loss8[0, 0], dx, grads


def _position():
    return lax.axis_index("x"), lax.axis_index("y"), lax.axis_index("c")


def _other_chips(x, y):
    return [(1 - x, y), (x, 1 - y), (1 - x, 1 - y)]


_HBM = pl.BlockSpec(memory_space=pltpu.HBM)
_SEM = pl.BlockSpec(memory_space=pltpu.SEMAPHORE)
_EFFECT = pltpu.SideEffectType.DATAFLOW_SIDE_EFFECTING


def _landing_zone(src, kind):
    shape = {"whole": (N_CHIPS,) + src.shape, "half": (2, N_CHIPS, src.shape[0], src.shape[1] // 2),
             "slab": (3,) + src.shape[1:], "sibling": src.shape, "all": (N_DEV - 1,) + src.shape}[kind]
    return lax.empty(shape, src.dtype)


def _copies_per_source(kind):
    return {"sibling": 1, "all": N_DEV - 1}.get(kind, 3)


def _chip_copies(src_refs, land_refs, send_sems, recv_sems, kind):
    x, y, c = _position()
    if kind == "all":
        copies = []
        for a, (src, land) in enumerate(zip(src_refs, land_refs)):
            for rel in range(1, N_DEV):
                peer = ((1 - x) if rel & 4 else x, (1 - y) if rel & 2 else y, (1 - c) if rel & 1 else c)
                k = (N_DEV - 1) * a + rel - 1
                copies.append(pltpu.make_async_remote_copy(
                    src_ref=src, dst_ref=land.at[rel - 1], send_sem=send_sems.at[k], recv_sem=recv_sems.at[k],
                    device_id=peer, device_id_type=MESH))
        return copies
    if kind == "sibling":
        return [pltpu.make_async_remote_copy(
            src_ref=src, dst_ref=land, send_sem=send_sems.at[a], recv_sem=recv_sems.at[a],
            device_id=(x, y, 1 - c), device_id_type=MESH) for a, (src, land) in enumerate(zip(src_refs, land_refs))]
    copies = []
    for a, (src, land) in enumerate(zip(src_refs, land_refs)):
        for j, (px, py) in enumerate(_other_chips(x, y)):
            if kind == "whole":
                src_ref, dst_ref = src, land.at[2 * x + y]
            elif kind == "half":
                half = src.shape[1] // 2
                src_ref, dst_ref = src.at[:, pl.ds(pl.multiple_of(c * half, half), half)], land.at[c, 2 * x + y]
            else:
                src_ref, dst_ref = src.at[2 * px + py], land.at[j]
            copies.append(pltpu.make_async_remote_copy(
                src_ref=src_ref, dst_ref=dst_ref, send_sem=send_sems.at[3 * a + j], recv_sem=recv_sems.at[3 * a + j],
                device_id=(px, py, c), device_id_type=MESH))
    return copies


def _exchange_start(srcs, *, kind, name):
    n = len(srcs)
    lands = [_landing_zone(s, kind) for s in srcs]

    def body(*refs):
        for cp in _chip_copies(refs[:n], refs[n:2 * n], refs[2 * n], refs[2 * n + 1], kind):
            cp.start()
        refs[-1][...] = jnp.zeros_like(refs[-1])

    res = pl.pallas_call(
        body, name=name,
        out_shape=(pltpu.SemaphoreType.DMA((_copies_per_source(kind) * n,)),
                   pltpu.SemaphoreType.DMA((_copies_per_source(kind) * n,)),
                   *[pltpu.HBM(a.shape, a.dtype) for a in srcs + lands], jax.ShapeDtypeStruct((8, 128), f32)),
        in_specs=[_HBM] * (2 * n),
        out_specs=(_SEM, _SEM, *[_HBM] * (2 * n), pl.BlockSpec(memory_space=pltpu.VMEM)),
        input_output_aliases={i: 2 + i for i in range(2 * n)},
        compiler_params=pltpu.CompilerParams(has_side_effects=_EFFECT),
    )(*[pltpu.with_memory_space_constraint(a, pltpu.HBM) for a in srcs + lands])
    return res[0], res[1], list(res[2:2 + n]), list(res[2 + n:2 + 2 * n]), res[-1]


def _exchange_wait(handle, after, *, kind, name):
    send_sems, recv_sems, srcs, lands, _ = handle
    n = len(srcs)

    def body(*refs):
        for cp in _chip_copies(refs[:n], refs[n:2 * n], refs[2 * n], refs[2 * n + 1], kind):
            cp.wait_send()
            cp.wait_recv()

    res = pl.pallas_call(
        body, name=name,
        out_shape=tuple(pltpu.HBM(a.shape, a.dtype) for a in srcs + lands),
        in_specs=[_HBM] * (2 * n) + [_SEM, _SEM, pl.BlockSpec(memory_space=pl.ANY)],
        out_specs=tuple([_HBM] * (2 * n)),
        input_output_aliases={i: i for i in range(2 * n)},
        compiler_params=pltpu.CompilerParams(has_side_effects=_EFFECT),
    )(*srcs, *lands, send_sems, recv_sems, after)
    return list(res[:n]), list(res[n:])


def _sum_own_and_received(g, recv, *, name):
    _, r, w = g.shape
    tm = _tile(r)
    if g.dtype == bf16 and tm % 16:
        tm = r
    x, y, _ = _position()
    me = jnp.reshape(2 * x + y, (1,)).astype(jnp.int32)

    def body(me_ref, g_ref, r_ref, o_ref):
        o_ref[...] = (g_ref[0].astype(f32) + r_ref[0].astype(f32)) + (r_ref[1].astype(f32) + r_ref[2].astype(f32))

    return pl.pallas_call(
        body, name=name,
        grid_spec=pltpu.PrefetchScalarGridSpec(
            num_scalar_prefetch=1, grid=(r // tm,),
            in_specs=[pl.BlockSpec((1, tm, w), lambda i, me_ref: (me_ref[0], i, 0)),
                      pl.BlockSpec((3, tm, w), lambda i, me_ref: (0, i, 0))],
            out_specs=pl.BlockSpec((tm, w), lambda i, me_ref: (i, 0))),
        out_shape=jax.ShapeDtypeStruct((r, w), f32),
        compiler_params=_params(("parallel",)),
    )(me, g, recv)


def _swap_halves(zone, *, name):
    def body(z_ref, o_ref, send_sems, recv_sems):
        x, y, c = _position()
        mine = [pltpu.make_async_remote_copy(
            src_ref=o_ref.at[c, 2 * px + py], dst_ref=o_ref.at[c, 2 * px + py], send_sem=send_sems.at[j],
            recv_sem=recv_sems.at[j], device_id=(x, y, 1 - c), device_id_type=MESH)
            for j, (px, py) in enumerate(_other_chips(x, y))]
        for cp in mine:
            cp.start()
        for j, (px, py) in enumerate(_other_chips(x, y)):
            pltpu.make_async_remote_copy(
                src_ref=o_ref.at[c, 2 * px + py], dst_ref=o_ref.at[1 - c, 2 * px + py], send_sem=send_sems.at[j],
                recv_sem=recv_sems.at[j], device_id=(x, y, 1 - c), device_id_type=MESH).wait_recv()
        for cp in mine:
            cp.wait_send()

    return pl.pallas_call(
        body, name=name,
        in_specs=[pl.BlockSpec(memory_space=pl.ANY)], out_specs=pl.BlockSpec(memory_space=pl.ANY),
        out_shape=jax.ShapeDtypeStruct(zone.shape, zone.dtype), input_output_aliases={0: 0},
        scratch_shapes=[pltpu.SemaphoreType.DMA((3,)), pltpu.SemaphoreType.DMA((3,))],
    )(zone)


def _sum_all_devices(own, received, *, name):
    def body(own_ref, got_ref, o_ref):
        x, y, c = _position()
        me = 4 * x + 2 * y + c
        acc = None
        for d in range(N_DEV):
            rel = jnp.bitwise_xor(me, d)
            block = jnp.where(rel == 0, own_ref[...], got_ref[jnp.maximum(rel, 1) - 1])
            acc = block if acc is None else acc + block
        o_ref[...] = acc

    return pl.pallas_call(
        body, name=name,
        in_specs=[pl.BlockSpec(memory_space=pltpu.VMEM)] * 2, out_specs=pl.BlockSpec(memory_space=pltpu.VMEM),
        out_shape=jax.ShapeDtypeStruct(own.shape, f32),
    )(own, received)


def _adam_math(w, g, m, v):
    nm = ADAM_B1 * m + (1.0 - ADAM_B1) * g
    nv = ADAM_B2 * v + (1.0 - ADAM_B2) * (g * g)
    m_hat = nm / (1.0 - ADAM_B1 ** ADAM_STEP)
    v_hat = nv / (1.0 - ADAM_B2 ** ADAM_STEP)
    return -ADAM_LR * (m_hat / (jnp.sqrt(v_hat) + ADAM_EPS) + ADAM_WD * w), nm, nv


def _adamw(w, g_parts, m, v, *, name, transposed=False, after=None):
    rows, cols = w.shape
    ordered = after is not None
    if transposed:
        tm = 256 if rows % 256 == 0 else rows
        g_spec = pl.BlockSpec((cols, tm), lambda i: (0, i))
    else:
        tm = _tile(rows, 256)
        g_spec = pl.BlockSpec((tm, cols), lambda i: (i, 0))
    n = len(g_parts)

    def body(*refs):
        refs = refs[1:] if ordered else refs
        w_ref, m_ref, v_ref = refs[0], refs[1 + n], refs[2 + n]
        g_ref, d_ref, nm_ref, nv_ref = refs[3 + n:]
        gv = refs[1][...]
        for part in refs[2:1 + n]:
            gv = gv + part[...]
        if transposed:
            gv = gv.T
        g_ref[...] = gv
        d_ref[...], nm_ref[...], nv_ref[...] = _adam_math(w_ref[...], gv, m_ref[...], v_ref[...])

    spec = pl.BlockSpec((tm, cols), lambda i: (i, 0))
    shape = jax.ShapeDtypeStruct((rows, cols), f32)
    return pl.pallas_call(
        body, name=name, grid=(rows // tm,),
        in_specs=[pl.BlockSpec(memory_space=pl.ANY)] * ordered + [spec] + [g_spec] * n + [spec] * 2,
        out_specs=[spec] * 4, out_shape=[shape] * 4,
        compiler_params=_params(("parallel",)),
    )(*([after] if ordered else []), w, *g_parts, m, v)


def _pad_rows(a, rows):
    return jnp.concatenate([a, jnp.zeros((rows - a.shape[0], a.shape[1]), a.dtype)], axis=0) if rows > a.shape[0] else a


_SMALL = (("norm_mix_g", D_MODEL), ("b_in", D_IN), ("sinks", Q_HEADS), ("mix", RWKV_PROJ), ("w0", RWKV_DIM),
          ("a0", RWKV_DIM), ("k_k", RWKV_DIM), ("k_a", RWKV_DIM), ("r_k", RWKV_DIM), ("ln_w", RWKV_DIM),
          ("ln_b", RWKV_DIM), ("norm_ffn_g", D_MODEL), ("norm_final_g", D_MODEL))


LANES = 128


def _small_layout():
    out, off = {}, 0
    for n, size in _SMALL + (("loss", 1),):
        pieces, col = [], 0
        while col < size:
            row, lane = divmod(off + col, PACK_W)
            width = min(size - col, PACK_W - lane)
            pieces.append((row, lane, width, col))
            col += width
        out[n] = pieces
        off += -(-size // LANES) * LANES
    return out, -(-off // PACK_W)


def _pack_small(d, loss):
    layout, rows = _small_layout()
    parts, used = [], 0
    for n, size in _SMALL + (("loss", 1),):
        item = loss if n == "loss" else d[n]
        fill = -size % LANES
        parts += [item.reshape(-1).astype(f32), jnp.zeros((fill,), f32)]
        used += size + fill
    parts.append(jnp.zeros((rows * PACK_W - used,), f32))
    return jnp.concatenate(parts).reshape(rows, PACK_W)


def _adamw_small(packed, first_row, ws, ms, vs, meta, *, name):
    layout, _ = _small_layout()
    names = [n for n, _ in _SMALL]
    k = len(names)

    def body(*refs):
        packed_ref = refs[0]
        w_refs, m_refs, v_refs = refs[1:1 + k], refs[1 + k:1 + 2 * k], refs[1 + 2 * k:1 + 3 * k]
        meta_refs = refs[1 + 3 * k:5 + 3 * k]
        outs = refs[5 + 3 * k:]
        for idx, n in enumerate(names):
            for row, lane, width, col in layout[n]:
                gv = packed_ref[first_row + row:first_row + row + 1, lane:lane + width]
                at = (slice(None), slice(col, col + width))
                new = _adam_math(w_refs[idx][at], gv, m_refs[idx][at], v_refs[idx][at])
                for o_ref, val in zip(outs[4 * idx:4 * idx + 4], (gv,) + new):
                    o_ref[at] = val
        for o_ref, val in zip(outs[4 * k:], _adam_math(*(r[...] for r in meta_refs))):
            o_ref[...] = val

    ins = [packed] + [d[n] for d in (ws, ms, vs) for n in names] + list(meta)
    shapes = [jax.ShapeDtypeStruct(ws[n].shape, f32) for n in names for _ in range(4)]
    shapes += [jax.ShapeDtypeStruct(meta[0].shape, f32)] * 3
    res = pl.pallas_call(
        body, name=name, grid=(1,), in_specs=[_full(a.shape) for a in ins],
        out_specs=[_full(s.shape) for s in shapes], out_shape=shapes,
        compiler_params=_params(("arbitrary",)),
    )(*ins)
    return {n: res[4 * i:4 * i + 4] for i, n in enumerate(names)}, res[4 * k:]


def kernel(x, meta_tokens, norm_mix_g, w_in, b_in, attn_sinks, rwkv_mix, rwkv_w0, rwkv_w2, rwkv_a0, rwkv_a2, rwkv_g2, rwkv_k_k, rwkv_k_a, rwkv_r_k, rwkv_ln_w, rwkv_ln_b, w_br_attn, w_br_rwkv, w_o, norm_ffn_g, w_ffn_gate, w_ffn_up, w_ffn_down, norm_final_g, loss_target, m_meta_tokens, m_norm_mix_g, m_w_in, m_b_in, m_attn_sinks, m_rwkv_mix, m_rwkv_w0, m_rwkv_w2, m_rwkv_a0, m_rwkv_a2, m_rwkv_g2, m_rwkv_k_k, m_rwkv_k_a, m_rwkv_r_k, m_rwkv_ln_w, m_rwkv_ln_b, m_w_br_attn, m_w_br_rwkv, m_w_o, m_norm_ffn_g, m_w_ffn_gate, m_w_ffn_up, m_w_ffn_down, m_norm_final_g, v_meta_tokens, v_norm_mix_g, v_w_in, v_b_in, v_attn_sinks, v_rwkv_mix, v_rwkv_w0, v_rwkv_w2, v_rwkv_a0, v_rwkv_a2, v_rwkv_g2, v_rwkv_k_k, v_rwkv_k_a, v_rwkv_r_k, v_rwkv_ln_w, v_rwkv_ln_b, v_w_br_attn, v_w_br_rwkv, v_w_o, v_norm_ffn_g, v_w_ffn_gate, v_w_ffn_up, v_w_ffn_down, v_norm_final_g):
    names = ("meta_tokens", "norm_mix_g", "w_in", "b_in", "attn_sinks", "rwkv_mix", "rwkv_w0", "rwkv_w2", "rwkv_a0",
             "rwkv_a2", "rwkv_g2", "rwkv_k_k", "rwkv_k_a", "rwkv_r_k", "rwkv_ln_w", "rwkv_ln_b", "w_br_attn",
             "w_br_rwkv", "w_o", "norm_ffn_g", "w_ffn_gate", "w_ffn_up", "w_ffn_down", "norm_final_g")
    w_all = dict(zip(names, (meta_tokens, norm_mix_g, w_in, b_in, attn_sinks, rwkv_mix, rwkv_w0, rwkv_w2, rwkv_a0,
                             rwkv_a2, rwkv_g2, rwkv_k_k, rwkv_k_a, rwkv_r_k, rwkv_ln_w, rwkv_ln_b, w_br_attn,
                             w_br_rwkv, w_o, norm_ffn_g, w_ffn_gate, w_ffn_up, w_ffn_down, norm_final_g)))
    m_all = dict(zip(names, (m_meta_tokens, m_norm_mix_g, m_w_in, m_b_in, m_attn_sinks, m_rwkv_mix, m_rwkv_w0,
                             m_rwkv_w2, m_rwkv_a0, m_rwkv_a2, m_rwkv_g2, m_rwkv_k_k, m_rwkv_k_a, m_rwkv_r_k,
                             m_rwkv_ln_w, m_rwkv_ln_b, m_w_br_attn, m_w_br_rwkv, m_w_o, m_norm_ffn_g, m_w_ffn_gate,
                             m_w_ffn_up, m_w_ffn_down, m_norm_final_g)))
    v_all = dict(zip(names, (v_meta_tokens, v_norm_mix_g, v_w_in, v_b_in, v_attn_sinks, v_rwkv_mix, v_rwkv_w0,
                             v_rwkv_w2, v_rwkv_a0, v_rwkv_a2, v_rwkv_g2, v_rwkv_k_k, v_rwkv_k_a, v_rwkv_r_k,
                             v_rwkv_ln_w, v_rwkv_ln_b, v_w_br_attn, v_w_br_rwkv, v_w_o, v_norm_ffn_g, v_w_ffn_gate,
                             v_w_ffn_up, v_w_ffn_down, v_norm_final_g)))
    cx, cy, _ = _position()
    chip = 2 * cx + cy

    t_of = dict(w_in_t="w_in", w_gate_t="w_ffn_gate", w_up_t="w_ffn_up", w_br_attn_t="w_br_attn",
                w_br_rwkv_t="w_br_rwkv", g2_t="rwkv_g2", w2_t="rwkv_w2", a2_t="rwkv_a2")
    plain_of = dict(w_down="w_ffn_down", w_o="w_o")
    meta_cols = meta_tokens.shape[1]

    def shard(k):
        return (w_all[t_of[k]][0].T if k in t_of else w_all[plain_of[k]][0]).astype(bf16)

    def whole(zone, own):
        return lax.dynamic_update_slice_in_dim(zone, own[None], chip, axis=0).reshape(-1, own.shape[-1])

    tiny = ("g2_t", "w2_t", "a2_t")
    late = ("w_gate_t", "w_up_t", "w_down", "w_o", "w_br_attn_t", "w_br_rwkv_t")
    w_in_own = shard("w_in_t")
    w_in_rows, w_in_cols = w_in_own.shape
    tiny_h = _exchange_start([shard(k) for k in tiny] + [meta_tokens], kind="whole", name="gather_tiny_start")
    w_in_h = _exchange_start([w_in_own + tiny_h[4][0, 0].astype(bf16)], kind="half", name="gather_w_in_start")
    behind = w_in_h[4][0, 0].astype(bf16)
    late_h = _exchange_start([shard(k) + behind for k in late], kind="whole", name="gather_late_start")
    own, zones = _exchange_wait(tiny_h, late_h[4], kind="whole", name="gather_tiny_wait")
    got = {k: whole(z, o) for k, z, o in zip(tiny, zones, own)}
    meta_full = whole(zones[-1], own[-1]).reshape(N_CHIPS, N_META, meta_cols).transpose(1, 0, 2).reshape(N_META, -1)
    p = dict(
        g2=got["g2_t"].T.astype(f32), w2=got["w2_t"].T.astype(f32), a2=got["a2_t"].T.astype(f32),
        b_in=b_in, sinks=attn_sinks, mix=rwkv_mix, w0=rwkv_w0, a0=rwkv_a0, k_k=rwkv_k_k, k_a=rwkv_k_a,
        r_k=rwkv_r_k.reshape(1, RWKV_DIM), ln_w=rwkv_ln_w, ln_b=rwkv_ln_b, norm_mix_g=norm_mix_g,
        norm_ffn_g=norm_ffn_g, norm_final_g=norm_final_g.reshape(1, D_MODEL),
    )

    def early_weights(after):
        own_h, zones_h = _exchange_wait(w_in_h, after, kind="half", name="gather_w_in_wait")
        zone = _swap_halves(zones_h[0], name="swap_w_in_halves")
        own_halves = own_h[0].reshape(w_in_rows, 2, w_in_cols // 2).transpose(1, 0, 2)[:, None]
        zone = lax.dynamic_update_slice(zone, own_halves, (0, chip, 0, 0))
        return dict(w_in_lr=zone.reshape(2, N_CHIPS * w_in_rows, w_in_cols // 2))

    def late_weights(after):
        own_l, zones_l = _exchange_wait(late_h, after, kind="whole", name="gather_late_wait")
        return {k: whole(z, o) for k, z, o in zip(late, zones_l, own_l)}

    started = {}

    def partial_sums(groups, after):
        parts = {}
        for group in groups:
            keys, handle = started[group]
            slabs, lands = _exchange_wait(handle, after, kind="slab", name="scatter_" + group + "_wait")
            parts.update({k: _sum_own_and_received(s, l, name="sum_chips_" + k) for k, s, l in zip(keys, slabs, lands)})
        return parts

    def emit(group, grads_):
        keys = list(grads_)
        slabs = []
        for k in keys:
            a = grads_[k].T if k in ("g2", "w2", "a2") else grads_[k]
            slabs.append(a.reshape(N_CHIPS, a.shape[0] // N_CHIPS, a.shape[1]))
        started[group] = (keys, _exchange_start(slabs, kind="slab", name="scatter_" + group + "_start"))
        zero = started[group][1][4]
        if group == "input":
            started["parts_a"] = partial_sums(("ffn", "branch"), zero)
            started["swap_a"] = _exchange_start(list(started["parts_a"].values()), kind="sibling",
                                                name="swap_cores_a_start")
            zero = started["swap_a"][4]
        return zero[0, 0]

    loss, dx, g = _local_step(x[0], loss_target[0], meta_full, p, early_weights, late_weights, emit)

    grads, delta, new_m, new_v = {}, {}, {}, {}
    in_grad_layout = ("w_in_t", "w_gate_t", "w_up_t")
    weight_of = {**t_of, **plain_of}

    def update(keys, mine, theirs, after=None):
        for k, part, other in zip(keys, mine, theirs):
            both = [part, other]
            k = k + "_t" if k in ("g2", "w2", "a2") else k
            n = weight_of[k]
            shape2 = w_all[n].shape[1:]
            w_, m_, v_ = (a.reshape(shape2) for a in (w_all[n], m_all[n], v_all[n]))
            if k in in_grad_layout:
                raw = _adamw(w_.T, both, m_.T, v_.T, name="adamw_" + n, after=after)
                res = [t.T for t in raw]
            else:
                res = raw = _adamw(w_, both, m_, v_, name="adamw_" + n, transposed=k in t_of, after=after)
            grads[n], delta[n], new_m[n], new_v[n] = (t.reshape(w_all[n].shape) for t in res)
        return raw[1]

    small = _pack_small(g, loss)
    small_rows8 = -(-(small.shape[0] + N_META) // 8) * 8
    small_h = _exchange_start([_pad_rows(jnp.concatenate([g["meta"], small], axis=0), small_rows8)], kind="all",
                              name="reduce_small_start")
    keys_a = list(started["parts_a"])
    mine_a, theirs_a = _exchange_wait(started["swap_a"], small_h[4], kind="sibling", name="swap_cores_a_wait")
    cut = len(started["ffn"][0])
    done = update(keys_a[:cut], mine_a[:cut], theirs_a[:cut])
    parts_b = partial_sums(("input",), done)
    swap_b = _exchange_start(list(parts_b.values()), kind="sibling", name="swap_cores_b_start")
    done = update(keys_a[cut:], mine_a[cut:], theirs_a[cut:], after=swap_b[4])
    small_own, small_got = _exchange_wait(small_h, done, kind="all", name="reduce_small_wait")
    reduced = _sum_all_devices(small_own[0], small_got[0], name="reduce_small_sum")
    update(list(parts_b), *_exchange_wait(swap_b, reduced, kind="sibling", name="swap_cores_b_wait"))
    g_meta = lax.dynamic_slice_in_dim(reduced[:N_META], chip * meta_cols, meta_cols, axis=1)
    (loss_row, loss_lane, _, _), = _small_layout()[0]["loss"]
    loss_total = reduced[N_META + loss_row, loss_lane]

    small_of = dict(norm_mix_g="norm_mix_g", b_in="b_in", attn_sinks="sinks", rwkv_mix="mix", rwkv_w0="w0",
                    rwkv_a0="a0", rwkv_k_k="k_k", rwkv_k_a="k_a", rwkv_r_k="r_k", rwkv_ln_w="ln_w",
                    rwkv_ln_b="ln_b", norm_ffn_g="norm_ffn_g", norm_final_g="norm_final_g")
    as_rows = [{k: src[n].reshape(1, -1) for n, k in small_of.items()} for src in (w_all, m_all, v_all)]
    meta_in = (meta_tokens, g_meta, m_meta_tokens, v_meta_tokens)
    small_out, meta_out = _adamw_small(reduced, N_META, *as_rows, meta_in, name="adamw_small")
    grads["meta_tokens"] = g_meta
    delta["meta_tokens"], new_m["meta_tokens"], new_v["meta_tokens"] = meta_out
    for n, k in small_of.items():
        grads[n], delta[n], new_m[n], new_v[n] = (t.reshape(w_all[n].shape) for t in small_out[k])

    return (loss_total, dx.reshape(x.shape), *[grads[n] for n in names], *[delta[n] for n in names],
            *[new_m[n] for n in names], *[new_v[n] for n in names])
```

```python
import math

import jax
import jax.numpy as jnp
import numpy as np
from jax import lax
from jax.experimental import pallas as pl
from jax.experimental.pallas import tpu as pltpu

f32 = jnp.float32
bf16 = jnp.bfloat16

D_MODEL = 1024
N_META = 16
HEAD_DIM = 64
Q_HEADS = 8
KV_HEADS = 2
GROUP = Q_HEADS // KV_HEADS
WINDOW = 128
BLOCK = 128
ROPE_THETA = 500000.0
ROPE_DIM = HEAD_DIM // 4
RWKV_HEADS = 8
RWKV_HEAD = 64
RWKV_DIM = RWKV_HEADS * RWKV_HEAD
DECAY_LORA = 64
AAA_LORA = 64
GATE_LORA = 160
LORA_W = DECAY_LORA + AAA_LORA + GATE_LORA
RWKV_LN_EPS = 64e-5
D_FF = 2816
Q_W = Q_HEADS * HEAD_DIM
KV_W = KV_HEADS * HEAD_DIM
ATTN_PROJ = Q_W + 2 * KV_W
RKV_W = 3 * RWKV_DIM
RWKV_PROJ = RKV_W + LORA_W
D_IN = ATTN_PROJ + RWKV_PROJ + 2 * D_MODEL
RMS_EPS = 1e-6
NEG_INF = -1e30
PAD = BLOCK - N_META
FRONT = PAD + N_META

ADAM_LR = 0.001
ADAM_B1 = 0.9
ADAM_B2 = 0.999
ADAM_EPS = 1e-08
ADAM_WD = 0.01
ADAM_STEP = 10

N_CHIPS = 4
N_DEV = 8
CHUNK = 128
VMEM_LIMIT = 56 * 1024 * 1024
MM_ROWS = 704
PACK_W = 1024
MESH = pl.DeviceIdType.MESH


def _tile(m, pref=384):
    for step in (16, 8):
        for t in range(min(m, pref) // step * step, 0, -step):
            if m % t == 0:
                return t
    return m


def _params(sem=None):
    return pltpu.CompilerParams(dimension_semantics=sem, vmem_limit_bytes=VMEM_LIMIT)


def _full(shape):
    nd = len(shape)
    return pl.BlockSpec(shape, lambda *_: (0,) * nd)


def _dot(a, b, dims="nn"):
    dn = {"nn": (((1,), (0,)), ((), ())), "nt": (((1,), (1,)), ((), ())), "tn": (((0,), (0,)), ((), ()))}[dims]
    return lax.dot_general(a.astype(bf16), b.astype(bf16), dn, preferred_element_type=f32)


def _two_pass(x, m, dims="nn"):
    x_hi = x.astype(bf16)
    x_lo = (x - x_hi.astype(f32)).astype(bf16)
    return _dot(x_hi, m, dims) + _dot(x_lo, m, dims)


@jax.custom_vjp
def _dot_const(x, m):
    return _two_pass(x, m)


def _dot_const_fwd(x, m):
    return _two_pass(x, m), m


def _dot_const_bwd(m, ct):
    return _two_pass(ct, m, "nt"), jnp.zeros_like(m)


_dot_const.defvjp(_dot_const_fwd, _dot_const_bwd)


def _two_pass_left(m, x, dims):
    x_hi = x.astype(bf16)
    x_lo = (x - x_hi.astype(f32)).astype(bf16)
    return _dot(m, x_hi, dims) + _dot(m, x_lo, dims)


@jax.custom_vjp
def _const_dot(m, x):
    return _two_pass_left(m, x, "nn")


def _const_dot_fwd(m, x):
    return _two_pass_left(m, x, "nn"), m


def _const_dot_bwd(m, ct):
    return jnp.zeros_like(m), _two_pass_left(m, ct, "tn")


_const_dot.defvjp(_const_dot_fwd, _const_dot_bwd)


def _mm(a, b, mode, *, name, out_dtype=f32, bias=None, add=None, zero_rows_below=0):
    m, _ = a.shape
    n = b.shape[1] if mode == "nn" else b.shape[0]
    tm = _tile(m, MM_ROWS)
    has_bias, has_add = bias is not None, add is not None

    def body(*refs):
        a_ref, b_ref = refs[0], refs[1]
        o_ref = refs[-1]
        acc = _dot(a_ref[...], b_ref[...], mode)
        k = 2
        if has_bias:
            acc = acc + refs[k][...]
            k += 1
        if zero_rows_below:
            rows = pl.program_id(0) * tm + lax.broadcasted_iota(jnp.int32, acc.shape, 0)
            acc = jnp.where(rows >= zero_rows_below, acc, 0.0)
        if has_add:
            acc = acc + refs[k][...].astype(f32)
        o_ref[...] = acc.astype(out_dtype)

    ins = [a, b]
    in_specs = [pl.BlockSpec((tm, a.shape[1]), lambda i: (i, 0)), _full(b.shape)]
    if has_bias:
        ins.append(bias)
        in_specs.append(_full(bias.shape))
    if has_add:
        ins.append(add)
        in_specs.append(pl.BlockSpec((tm, n), lambda i: (i, 0)))
    return pl.pallas_call(
        body, name=name, grid=(m // tm,), in_specs=in_specs,
        out_specs=pl.BlockSpec((tm, n), lambda i: (i, 0)),
        out_shape=jax.ShapeDtypeStruct((m, n), out_dtype),
        compiler_params=_params(("parallel",)),
    )(*ins)


def _pieces(widths):
    out, off = [], 0
    for w in widths:
        out.append((off, w))
        off += w
    return out


def _proj_in(a, w_lr, bias, widths, rope, *, name, zero_rows_below=0):
    m, kdim = a.shape
    half = kdim // 2
    tm = _tile(m, MM_ROWS)
    cos_t, sin_t, swap = rope
    out_widths = [Q_W, KV_W, KV_W] + list(widths[1:])

    def body(a_ref, w_ref, b_ref, cos_ref, sin_ref, swap_ref, *outs):
        a_l, a_r = a_ref[:, :half], a_ref[:, half:]
        for j, (off, width) in enumerate(_pieces(widths)):
            acc = _dot(a_l, w_ref[0, off:off + width, :], "nt") + _dot(a_r, w_ref[1, off:off + width, :], "nt")
            acc = acc + b_ref[:, off:off + width]
            if zero_rows_below:
                rows = pl.program_id(0) * tm + lax.broadcasted_iota(jnp.int32, acc.shape, 0)
                acc = jnp.where(rows >= zero_rows_below, acc, 0.0)
            if j == 0:
                qkv = acc.astype(bf16).astype(f32)
                for o_ref, val in zip(outs[:3], _attn_prep(qkv, cos_ref[...], sin_ref[...], swap_ref[...])):
                    o_ref[...] = val.astype(o_ref.dtype)
            else:
                outs[2 + j][...] = acc.astype(outs[2 + j].dtype)

    table = pl.BlockSpec((tm, HEAD_DIM), lambda i: (i, 0))
    return pl.pallas_call(
        body, name=name, grid=(m // tm,),
        in_specs=[pl.BlockSpec((tm, kdim), lambda i: (i, 0)), _full(w_lr.shape), _full(bias.shape), table, table,
                  _full(swap.shape)],
        out_specs=[pl.BlockSpec((tm, w), lambda i: (i, 0)) for w in out_widths],
        out_shape=[jax.ShapeDtypeStruct((m, w), bf16) for w in out_widths],
        compiler_params=_params(("parallel",)),
    )(a, w_lr, bias, cos_t, sin_t, swap)


def _proj_in_bwd(d_list, w_lr, *, name):
    m = d_list[0].shape[0]
    half = w_lr.shape[2]
    widths = [d.shape[1] for d in d_list]
    tm = _tile(m, MM_ROWS)

    def body(*refs):
        w_ref, o_ref = refs[-2], refs[-1]
        for side in range(2):
            acc = None
            for (off, width), d_ref in zip(_pieces(widths), refs):
                term = _dot(d_ref[...], w_ref[side, off:off + width, :])
                acc = term if acc is None else acc + term
            o_ref[:, side * half:(side + 1) * half] = acc.astype(o_ref.dtype)

    return pl.pallas_call(
        body, name=name, grid=(m // tm,),
        in_specs=[pl.BlockSpec((tm, w), lambda i: (i, 0)) for w in widths] + [_full(w_lr.shape)],
        out_specs=pl.BlockSpec((tm, 2 * half), lambda i: (i, 0)),
        out_shape=jax.ShapeDtypeStruct((m, 2 * half), bf16),
        compiler_params=_params(("parallel",)),
    )(*d_list, w_lr)


def _residual_norm(a, w, res, g, *, name):
    m, d = res.shape
    tm = _tile(m, MM_ROWS)

    def body(a_ref, w_ref, r_ref, g_ref, h_ref, n_ref):
        h = _dot(a_ref[...], w_ref[...]) + r_ref[...]
        h_ref[...] = h
        n_ref[...] = _rms(h, g_ref[...]).astype(n_ref.dtype)

    tile = pl.BlockSpec((tm, d), lambda i: (i, 0))
    return pl.pallas_call(
        body, name=name, grid=(m // tm,),
        in_specs=[pl.BlockSpec((tm, a.shape[1]), lambda i: (i, 0)), _full(w.shape), tile, _full(g.shape)],
        out_specs=[tile, tile],
        out_shape=[jax.ShapeDtypeStruct((m, d), f32), jax.ShapeDtypeStruct((m, d), bf16)],
        compiler_params=_params(("parallel",)),
    )(a, w, res, g)


def _residual_norm_bwd(d_list, w_list, h, g, dh_out, *, name):
    m, d = h.shape
    k = len(d_list)
    tm = _tile(m)

    def body(*refs):
        h_ref, g_ref, dho_ref, dh_ref, dg_ref = refs[2 * k:]
        dn = _dot(refs[0][...], refs[k][...])
        for i in range(1, k):
            dn = dn + _dot(refs[i][...], refs[k + i][...])
        _, vjp = jax.vjp(lambda hv, gv: (_rms(hv, gv), hv), h_ref[...], g_ref[...])
        dh, dg = vjp((dn, dho_ref[...]))
        dh_ref[...] = dh

        @pl.when(pl.program_id(0) == 0)
        def _():
            dg_ref[...] = jnp.zeros_like(dg_ref)

        dg_ref[...] += dg

    tile = pl.BlockSpec((tm, d), lambda i: (i, 0))
    return pl.pallas_call(
        body, name=name, grid=(m // tm,),
        in_specs=[pl.BlockSpec((tm, a.shape[1]), lambda i: (i, 0)) for a in d_list] + [_full(w.shape) for w in w_list]
        + [tile, _full(g.shape), tile],
        out_specs=[tile, _full(g.shape)],
        out_shape=[jax.ShapeDtypeStruct((m, d), f32), jax.ShapeDtypeStruct(g.shape, f32)],
        compiler_params=_params(("arbitrary",)),
    )(*d_list, *w_list, h, g, dh_out)


def _mm_tn(a, b, *, name, colsum=False, out_dtype=bf16, into=None):
    r, m = a.shape
    n = b.shape[1]
    tr = _tile(r, 1408)
    tmo = m
    for cand in (1408, 1024, 768, 512):
        if m > 1024 and m % cand == 0:
            tmo = cand
            break
    steps = r // tr

    rows, offset, target = into or (m, 0, None)

    def body(a_ref, b_ref, *rest):
        o_ref, rest = (rest[0], rest[1:]) if target is None else (rest[1], rest[2:])
        acc = rest[-1]
        i = pl.program_id(1)

        @pl.when(i == 0)
        def _():
            acc[...] = jnp.zeros_like(acc)
            if colsum:
                rest[0][...] = jnp.zeros_like(rest[0])

        acc[...] += _dot(a_ref[...], b_ref[...], "tn")
        if colsum:
            rest[0][...] += jnp.sum(a_ref[...].astype(f32), axis=0, keepdims=True)

        @pl.when(i == steps - 1)
        def _():
            o_ref[...] = acc[...].astype(out_dtype)

    out_shape = [jax.ShapeDtypeStruct((rows, n), out_dtype)]
    if offset % tmo == 0:
        out_specs = [pl.BlockSpec((tmo, n), lambda j, i: (offset // tmo + j, 0))]
    else:
        out_specs = [pl.BlockSpec((pl.Element(tmo), pl.Element(n)), lambda j, i: (pl.multiple_of(offset + j * tmo, math.gcd(offset, tmo)), 0))]
    if colsum:
        out_shape.append(jax.ShapeDtypeStruct((1, m), f32))
        out_specs.append(pl.BlockSpec((1, tmo), lambda j, i: (0, j)))
    in_specs = [pl.BlockSpec((tr, tmo), lambda j, i: (i, j)), pl.BlockSpec((tr, n), lambda j, i: (i, 0))]
    res = pl.pallas_call(
        body, name=name, grid=(m // tmo, steps),
        in_specs=in_specs + ([] if target is None else [pl.BlockSpec(memory_space=pl.ANY)]),
        out_specs=out_specs, out_shape=out_shape,
        scratch_shapes=[pltpu.VMEM((tmo, n), f32)],
        input_output_aliases={} if target is None else {2: 0},
        compiler_params=_params(("parallel", "arbitrary")),
    )(a, b, *([] if target is None else [target]))
    return res if colsum else res[0]


def _rowwise(fn, rows, params, outs, *, name, tm=None):
    m = rows[0].shape[0]
    tm = tm or _tile(m, MM_ROWS)
    nr, npar = len(rows), len(params)

    def body(*refs):
        vals = [r[...] for r in refs[:nr + npar]]
        res = fn(*vals)
        for o_ref, v in zip(refs[nr + npar:], res):
            o_ref[...] = v.astype(o_ref.dtype)

    return pl.pallas_call(
        body, name=name, grid=(m // tm,),
        in_specs=[pl.BlockSpec((tm, r.shape[1]), lambda i: (i, 0)) for r in rows] + [_full(p.shape) for p in params],
        out_specs=[pl.BlockSpec((tm, w), lambda i: (i, 0)) for w, _ in outs],
        out_shape=[jax.ShapeDtypeStruct((m, w), dt) for w, dt in outs],
        compiler_params=_params(("parallel",)),
    )(*rows, *params)


def _rowwise_bwd(fn, rows, params, cts, *, name, diff_rows, diff_params, tm=None, zero_rows_below=0, out_dtypes=None):
    m = rows[0].shape[0]
    tm = tm or _tile(m)
    nr, npar = len(rows), len(params)
    d_idx = [i for i in range(nr) if diff_rows[i]]
    p_idx = [i for i in range(npar) if diff_params[i]]
    out_dtypes = out_dtypes or [f32] * len(d_idx)
    flat_cts = [c for group in cts for c in group]
    n_ct = len(flat_cts)

    def body(*refs):
        vals = [r[...] for r in refs[:nr + npar]]
        ct_refs = refs[nr + npar:nr + npar + n_ct]
        out_refs = refs[nr + npar + n_ct:]
        ct_vals, k = [], 0
        for group in cts:
            acc = ct_refs[k][...].astype(f32)
            for extra in range(1, len(group)):
                acc = acc + ct_refs[k + extra][...].astype(f32)
            k += len(group)
            if zero_rows_below:
                rr = pl.program_id(0) * tm + lax.broadcasted_iota(jnp.int32, acc.shape, 0)
                acc = jnp.where(rr >= zero_rows_below, acc, 0.0)
            ct_vals.append(acc)

        def g(*dargs):
            full = list(vals)
            for pos, i in enumerate(d_idx):
                full[i] = dargs[pos]
            for pos, i in enumerate(p_idx):
                full[nr + i] = dargs[len(d_idx) + pos]
            return tuple(fn(*full))

        _, vjp = jax.vjp(g, *[vals[i].astype(f32) for i in d_idx], *[vals[nr + i] for i in p_idx])
        grads = vjp(tuple(ct_vals))
        for pos in range(len(d_idx)):
            out_refs[pos][...] = grads[pos].astype(out_refs[pos].dtype)
        first = pl.program_id(0) == 0
        for pos in range(len(p_idx)):
            o_ref = out_refs[len(d_idx) + pos]

            @pl.when(first)
            def _(o_ref=o_ref):
                o_ref[...] = jnp.zeros_like(o_ref)

            o_ref[...] += grads[len(d_idx) + pos]

    return pl.pallas_call(
        body, name=name, grid=(m // tm,),
        in_specs=[pl.BlockSpec((tm, r.shape[1]), lambda i: (i, 0)) for r in rows] + [_full(p.shape) for p in params]
        + [pl.BlockSpec((tm, c.shape[1]), lambda i: (i, 0)) for c in flat_cts],
        out_specs=[pl.BlockSpec((tm, rows[i].shape[1]), lambda i_: (i_, 0)) for i in d_idx]
        + [_full(params[i].shape) for i in p_idx],
        out_shape=[jax.ShapeDtypeStruct(rows[i].shape, dt) for i, dt in zip(d_idx, out_dtypes)]
        + [jax.ShapeDtypeStruct(params[i].shape, f32) for i in p_idx],
        compiler_params=_params(("arbitrary",)),
    )(*rows, *params, *flat_cts)


def _rms(x, g):
    return x * lax.rsqrt(jnp.mean(x * x, axis=-1, keepdims=True) + RMS_EPS) * g


def _head_sum_matrix(width, head):
    idx = jnp.arange(width) // head
    return (idx[:, None] == idx[None, :]).astype(f32)


def _rope_tables(lp):
    half = ROPE_DIM // 2
    pos = (np.arange(lp) - PAD).astype(np.float32)
    inv_freq = np.power(np.float32(ROPE_THETA), -np.arange(half, dtype=np.float32) * np.float32(2.0 / ROPE_DIM))
    ang = pos[:, None] * inv_freq[None, :].astype(np.float32)
    cos, sin = np.cos(ang), np.sin(ang)
    ones = np.ones((lp, HEAD_DIM - ROPE_DIM), np.float32)
    cos_t = np.concatenate([cos, cos, ones], axis=1)
    sin_t = np.concatenate([-sin, sin, 0.0 * ones], axis=1)
    i = np.arange(HEAD_DIM)
    src = np.where(i < half, i + half, np.where(i < ROPE_DIM, i - half, i))
    swap = ((i[:, None] == src[None, :]) & (i[None, :] < ROPE_DIM)).astype(np.float32)
    return jnp.asarray(cos_t, f32), jnp.asarray(sin_t, f32), jnp.asarray(swap, f32)


def _attn_prep(qkv, cos_t, sin_t, swap):
    outs = []
    for h in range(Q_HEADS + KV_HEADS):
        t = qkv[:, h * HEAD_DIM:(h + 1) * HEAD_DIM]
        outs.append(t * cos_t + _dot_const(t, swap) * sin_t)
    q = jnp.concatenate(outs[:Q_HEADS], axis=1)
    k = jnp.concatenate(outs[Q_HEADS:], axis=1)
    return q, k, qkv[:, Q_W + KV_W:]


def _attn_prep_transposed(dq, dk, dv, cos_t, sin_t, swap, dk_meta, dv_meta):
    first = pl.program_id(0) == 0

    def with_meta(d, d_meta):
        rest = jnp.zeros((d.shape[0] - BLOCK, KV_W), f32)
        return d.astype(f32) + jnp.where(first, jnp.concatenate([d_meta, rest], axis=0), 0.0)

    parts = []
    for d, heads in ((dq.astype(f32), Q_HEADS), (with_meta(dk, dk_meta), KV_HEADS)):
        for h in range(heads):
            t = d[:, h * HEAD_DIM:(h + 1) * HEAD_DIM]
            parts.append(t * cos_t + _two_pass(t * sin_t, swap, "nt"))
    return (jnp.concatenate(parts + [with_meta(dv, dv_meta)], axis=1),)


def _softplus(z):
    return jnp.maximum(z, 0.0) + jnp.log1p(jnp.exp(-jnp.abs(z)))


def _rwkv_prep(rkv, lora, w0, w2, a0, a2, g2, k_k, k_a, hsum):
    r = rkv[:, :RWKV_DIM]
    k = rkv[:, RWKV_DIM:2 * RWKV_DIM]
    v = rkv[:, 2 * RWKV_DIM:]
    dw = lora[:, :DECAY_LORA]
    da = lora[:, DECAY_LORA:DECAY_LORA + AAA_LORA]
    dg = lora[:, DECAY_LORA + AAA_LORA:]
    w = -_softplus(-(w0 + _dot(jnp.tanh(dw), w2))) - 0.5
    a = jax.nn.sigmoid(a0 + _dot(da, a2))
    g = _dot(jax.nn.sigmoid(dg), g2)
    kk = k * k_k
    kk = kk * lax.rsqrt(jnp.maximum(_dot_const(kk * kk, hsum), 1e-24))
    k = k * (1.0 + (a - 1.0) * k_a)
    log_decay = -jnp.exp(w)
    return r, log_decay, k, v, -kk, kk * a, g


def _rwkv_post(y, r, k, v, g, ln_w, ln_b, r_k, hmean):
    hsum = hmean * RWKV_HEAD
    mean = _dot_const(y, hmean)
    yc = y - mean
    var = _dot_const(yc * yc, hmean)
    yn = yc * lax.rsqrt(var + RWKV_LN_EPS) * ln_w + ln_b
    bonus = _dot_const(r * k * r_k, hsum) * v
    return ((yn + bonus) * g,)


def _merge(gates, br_a, br_r):
    sg = jax.nn.sigmoid(gates)
    return (sg[:, :D_MODEL] * br_a + sg[:, D_MODEL:] * br_r,)


def _swiglu(gate, up):
    return (jax.nn.silu(gate) * up,)


def _ffn_in(f, w_gate_t, w_up_t, *, name):
    m, d = f.shape
    n = w_gate_t.shape[0]
    tm = _tile(m)

    def body(f_ref, wg_ref, wu_ref, g_ref, u_ref, a_ref):
        g = _dot(f_ref[...], wg_ref[...], "nt")
        u = _dot(f_ref[...], wu_ref[...], "nt")
        g_ref[...] = g.astype(g_ref.dtype)
        u_ref[...] = u.astype(u_ref.dtype)
        a_ref[...] = _swiglu(g, u)[0].astype(a_ref.dtype)

    spec = pl.BlockSpec((tm, n), lambda i: (i, 0))
    return pl.pallas_call(
        body, name=name, grid=(m // tm,),
        in_specs=[pl.BlockSpec((tm, d), lambda i: (i, 0)), _full(w_gate_t.shape), _full(w_up_t.shape)],
        out_specs=[spec] * 3, out_shape=[jax.ShapeDtypeStruct((m, n), bf16)] * 3,
        compiler_params=_params(("parallel",)),
    )(f, w_gate_t, w_up_t)


def _branch_merge(y_attn, y_rwkv, w_attn_t, w_rwkv_t, gates, *, name):
    m = y_attn.shape[0]
    tm = _tile(m, MM_ROWS)

    def body(ya_ref, yr_ref, wa_ref, wr_ref, g_ref, a_ref, r_ref, o_ref):
        br_a = _dot(ya_ref[...], wa_ref[...], "nt")
        br_r = _dot(yr_ref[...], wr_ref[...], "nt")
        a_ref[...] = br_a.astype(a_ref.dtype)
        r_ref[...] = br_r.astype(r_ref.dtype)
        o_ref[...] = _merge(g_ref[...].astype(f32), br_a, br_r)[0].astype(o_ref.dtype)

    rows = lambda a: pl.BlockSpec((tm, a.shape[1]), lambda i: (i, 0))
    spec = pl.BlockSpec((tm, D_MODEL), lambda i: (i, 0))
    return pl.pallas_call(
        body, name=name, grid=(m // tm,),
        in_specs=[rows(y_attn), rows(y_rwkv), _full(w_attn_t.shape), _full(w_rwkv_t.shape), rows(gates)],
        out_specs=[spec] * 3, out_shape=[jax.ShapeDtypeStruct((m, D_MODEL), bf16)] * 3,
        compiler_params=_params(("parallel",)),
    )(y_attn, y_rwkv, w_attn_t, w_rwkv_t, gates)


def _branch_merge_bwd(dh, w_o, gates, br_a, br_r, w_attn_t, w_rwkv_t, *, name):
    m = dh.shape[0]
    tm = _tile(m, MM_ROWS)

    def body(dh_ref, w_ref, g_ref, a_ref, r_ref, wa_ref, wr_ref, dg_ref, da_ref, dr_ref, dya_ref, dyr_ref):
        dmerged = _dot(dh_ref[...], w_ref[...], "nt")
        _, vjp = jax.vjp(lambda g, a, r: _merge(g, a, r)[0], g_ref[...].astype(f32), a_ref[...].astype(f32),
                         r_ref[...].astype(f32))
        dg, da, dr = vjp(dmerged)
        dg_ref[...] = dg.astype(dg_ref.dtype)
        da_ref[...] = da.astype(da_ref.dtype)
        dr_ref[...] = dr.astype(dr_ref.dtype)
        dya_ref[...] = _dot(da, wa_ref[...])
        dyr_ref[...] = _dot(dr, wr_ref[...])

    rows = lambda a: pl.BlockSpec((tm, a.shape[1]), lambda i: (i, 0))
    mixer = pl.BlockSpec((tm, w_attn_t.shape[1]), lambda i: (i, 0))
    return pl.pallas_call(
        body, name=name, grid=(m // tm,),
        in_specs=[rows(dh), _full(w_o.shape), rows(gates), rows(br_a), rows(br_r), _full(w_attn_t.shape),
                  _full(w_rwkv_t.shape)],
        out_specs=[rows(gates), rows(br_a), rows(br_r), mixer, mixer],
        out_shape=[jax.ShapeDtypeStruct(gates.shape, bf16), jax.ShapeDtypeStruct(br_a.shape, bf16),
                   jax.ShapeDtypeStruct(br_r.shape, bf16), jax.ShapeDtypeStruct((m, w_attn_t.shape[1]), f32),
                   jax.ShapeDtypeStruct((m, w_rwkv_t.shape[1]), f32)],
        compiler_params=_params(("parallel",)),
    )(dh, w_o, gates, br_a, br_r, w_attn_t, w_rwkv_t)


def _ffn_in_bwd(dh, w_down, gate, up, *, name):
    m, d = dh.shape
    n = w_down.shape[0]
    tm = _tile(m)

    def body(dh_ref, w_ref, g_ref, u_ref, dg_ref, du_ref):
        dact = _dot(dh_ref[...], w_ref[...], "nt")
        _, vjp = jax.vjp(lambda a, b: _swiglu(a, b)[0], g_ref[...].astype(f32), u_ref[...].astype(f32))
        dg, du = vjp(dact)
        dg_ref[...] = dg.astype(dg_ref.dtype)
        du_ref[...] = du.astype(du_ref.dtype)

    spec = pl.BlockSpec((tm, n), lambda i: (i, 0))
    return pl.pallas_call(
        body, name=name, grid=(m // tm,),
        in_specs=[pl.BlockSpec((tm, d), lambda i: (i, 0)), _full(w_down.shape), spec, spec],
        out_specs=[spec] * 2, out_shape=[jax.ShapeDtypeStruct((m, n), bf16)] * 2,
        compiler_params=_params(("parallel",)),
    )(dh, w_down, gate, up)


def _ffn_bwd(dh, w_down, gate, up, w_gate_t, w_up_t, h, g, *, name):
    m, d = dh.shape
    n = w_down.shape[0]
    tm = _tile(m)
    halves = [(c * (n // 2), (c + 1) * (n // 2)) for c in range(2)]

    def body(dh_ref, wd_ref, g_ref, u_ref, wg_ref, wu_ref, h_ref, gn_ref, dg_ref, du_ref, dh1_ref, dgn_ref):
        dn = jnp.zeros((tm, d), f32)
        for lo, hi in halves:
            dact = _dot(dh_ref[...], wd_ref[lo:hi, :], "nt")
            _, vjp = jax.vjp(lambda a, b: _swiglu(a, b)[0], g_ref[:, lo:hi].astype(f32), u_ref[:, lo:hi].astype(f32))
            dg, du = (t.astype(bf16) for t in vjp(dact))
            dg_ref[:, lo:hi] = dg
            du_ref[:, lo:hi] = du
            dn = dn + _dot(dg, wg_ref[lo:hi, :]) + _dot(du, wu_ref[lo:hi, :])
        _, vjp = jax.vjp(lambda hv, gv: (_rms(hv, gv), hv), h_ref[...], gn_ref[...])
        dh1, dgn = vjp((dn, dh_ref[...]))
        dh1_ref[...] = dh1

        @pl.when(pl.program_id(0) == 0)
        def _():
            dgn_ref[...] = jnp.zeros_like(dgn_ref)

        dgn_ref[...] += dgn

    wide = pl.BlockSpec((tm, n), lambda i: (i, 0))
    tile = pl.BlockSpec((tm, d), lambda i: (i, 0))
    return pl.pallas_call(
        body, name=name, grid=(m // tm,),
        in_specs=[tile, _full(w_down.shape), wide, wide, _full(w_gate_t.shape), _full(w_up_t.shape), tile,
                  _full(g.shape)],
        out_specs=[wide, wide, tile, _full(g.shape)],
        out_shape=[jax.ShapeDtypeStruct((m, n), bf16)] * 2 + [jax.ShapeDtypeStruct((m, d), f32),
                                                               jax.ShapeDtypeStruct(g.shape, f32)],
        compiler_params=_params(("arbitrary",)),
    )(dh, w_down, gate, up, w_gate_t, w_up_t, h, g)


HALO = 16


def _previous_rows(x, before_ref, first_tile):
    rows = lax.broadcasted_iota(jnp.int32, x.shape, 0)
    last = jnp.where(first_tile, 0.0, before_ref[HALO - 1:HALO, :].astype(f32))
    return jnp.where(rows == 0, last, pltpu.roll(x, 1, axis=0))


def _mixer_inputs(ps, mixes, params, *, name):
    m = ps[0].shape[0]
    tm = _tile(m)
    sub = tm // HALO
    n_par = len(params)

    def body(*refs):
        first = pl.program_id(0) == 0
        pf = []
        for k in range(2):
            x = refs[k][...].astype(f32)
            pf.append(x + (_previous_rows(x, refs[2 + k], first) - x) * refs[4 + k][...])
        res = _rwkv_prep(*pf, *[ref[...] for ref in refs[6:6 + n_par]])
        for o_ref, val in zip(refs[6 + n_par:], res):
            o_ref[...] = val

    tile = lambda a: pl.BlockSpec((tm, a.shape[1]), lambda i: (i, 0))
    before = lambda a: pl.BlockSpec((HALO, a.shape[1]), lambda i: (jnp.maximum(i * sub - 1, 0), 0))
    out = pl.BlockSpec((tm, RWKV_DIM), lambda i: (i, 0))
    return pl.pallas_call(
        body, name=name, grid=(m // tm,),
        in_specs=[tile(a) for a in ps] + [before(a) for a in ps] + [_full(a.shape) for a in mixes + params],
        out_specs=[out] * 7, out_shape=[jax.ShapeDtypeStruct((m, RWKV_DIM), f32)] * 7,
        compiler_params=_params(("parallel",)),
    )(*ps, *ps, *mixes, *params)


def _mixer_inputs_bwd(ps, mixes, params, cts, *, name):
    m = ps[0].shape[0]
    tm = _tile(m)
    sub = tm // HALO
    nt = m // tm
    n_par = len(params)
    flat_cts = [c for group in cts for c in group]
    n_ct = len(flat_cts)

    def body(*refs):
        i = pl.program_id(0)
        tile_index = nt - 1 - i
        ct_refs = refs[6 + n_par:6 + n_par + n_ct]
        dp_refs = refs[6 + n_par + n_ct:8 + n_par + n_ct]
        dmix_refs = refs[8 + n_par + n_ct:10 + n_par + n_ct]
        dpar_refs = refs[10 + n_par + n_ct:9 + 2 * n_par + n_ct]
        carries = refs[9 + 2 * n_par + n_ct:]
        rows1 = tile_index * tm + lax.broadcasted_iota(jnp.int32, (tm, 1), 0)
        live = rows1 >= PAD

        @pl.when(i == 0)
        def _():
            for ref in (*dmix_refs, *dpar_refs, *carries):
                ref[...] = jnp.zeros_like(ref)

        xs, prevs, pf = [], [], []
        for k in range(2):
            x = refs[k][...].astype(f32)
            xp = _previous_rows(x, refs[2 + k], tile_index == 0)
            xs.append(x)
            prevs.append(xp)
            pf.append(x + (xp - x) * refs[4 + k][...])
        ct_vals, pos = [], 0
        for group in cts:
            acc = ct_refs[pos][...].astype(f32)
            for extra in range(1, len(group)):
                acc = acc + ct_refs[pos + extra][...].astype(f32)
            pos += len(group)
            ct_vals.append(jnp.where(live, acc, 0.0))
        par_vals = [ref[...] for ref in refs[6:6 + n_par]]
        _, vjp = jax.vjp(lambda *args: _rwkv_prep(*args, par_vals[-1]), *pf, *par_vals[:-1])
        g = vjp(tuple(ct_vals))
        for k in range(2):
            dpf = g[k]
            mixv = refs[4 + k][...]
            dm = dpf * mixv
            rows = lax.broadcasted_iota(jnp.int32, dm.shape, 0)
            dm_next = jnp.where(rows == tm - 1, carries[k][...], pltpu.roll(dm, tm - 1, axis=0))
            dp_refs[k][...] = jnp.where(live, dpf - dm + dm_next, 0.0).astype(dp_refs[k].dtype)
            carries[k][...] = dm[0:1, :]
            dmix_refs[k][...] += jnp.sum(dpf * (prevs[k] - xs[k]), axis=0, keepdims=True)
        for ref, val in zip(dpar_refs, g[2:]):
            ref[...] += val

    tile = lambda a: pl.BlockSpec((tm, a.shape[1]), lambda i: (nt - 1 - i, 0))
    before = lambda a: pl.BlockSpec((HALO, a.shape[1]), lambda i: (jnp.maximum((nt - 1 - i) * sub - 1, 0), 0))
    return pl.pallas_call(
        body, name=name, grid=(nt,),
        in_specs=[tile(a) for a in ps] + [before(a) for a in ps] + [_full(a.shape) for a in mixes + params]
        + [tile(c) for c in flat_cts],
        out_specs=[tile(a) for a in ps] + [_full(a.shape) for a in mixes + params[:-1]],
        out_shape=[jax.ShapeDtypeStruct(a.shape, bf16) for a in ps]
        + [jax.ShapeDtypeStruct(a.shape, f32) for a in mixes + params[:-1]],
        scratch_shapes=[pltpu.VMEM((1, a.shape[1]), f32) for a in ps],
        compiler_params=_params(("arbitrary",)),
    )(*ps, *ps, *mixes, *params, *flat_cts)


def _attn_masks(blk):
    qi = lax.broadcasted_iota(jnp.int32, (BLOCK, BLOCK), 0)
    ki = lax.broadcasted_iota(jnp.int32, (BLOCK, BLOCK), 1)
    qpos = blk * BLOCK + qi - PAD
    kpos_c = blk * BLOCK + ki - PAD
    kpos_p = kpos_c - BLOCK
    kpos_m = ki - PAD

    def band(kpos):
        return (kpos >= N_META) & (kpos <= qpos) & (qpos - kpos < WINDOW)

    return band(kpos_p), band(kpos_c), (kpos_m >= 0) & (kpos_m <= qpos)


def _attn_probs(qs, k3s, sink, oks):
    s = [[jnp.where(ok, _dot(qh, kx, "nt"), NEG_INF) for kx, ok in zip(k3, oks)] for qh, k3 in zip(qs, k3s)]
    mx = [jnp.maximum(jnp.maximum(jnp.max(t[0], -1, keepdims=True), jnp.max(t[1], -1, keepdims=True)),
                      jnp.maximum(jnp.max(t[2], -1, keepdims=True), sk)) for t, sk in zip(s, sink)]
    e = [[jnp.exp(tx - m) for tx in t] for t, m in zip(s, mx)]
    e_sink = [jnp.exp(sk - m) for sk, m in zip(sink, mx)]
    inv = [1.0 / (jnp.sum(t[0], -1, keepdims=True) + jnp.sum(t[1], -1, keepdims=True)
                  + jnp.sum(t[2], -1, keepdims=True) + es) for t, es in zip(e, e_sink)]
    return [[tx * i for tx in t] for t, i in zip(e, inv)], [es * i for es, i in zip(e_sink, inv)]


def _head_cols(i):
    return slice(i * HEAD_DIM, (i + 1) * HEAD_DIM)


def _attn_operands(refs):
    q_ref, kp_ref, kc_ref, km_ref, vp_ref, vc_ref, vm_ref, s_ref = refs
    qs = [q_ref[:, _head_cols(i)] * (HEAD_DIM ** -0.5) for i in range(Q_HEADS)]
    k3 = [[ref[:, _head_cols(h)] for ref in (kp_ref, kc_ref, km_ref)] for h in range(KV_HEADS)]
    v3 = [[ref[:, _head_cols(h)] for ref in (vp_ref, vc_ref, vm_ref)] for h in range(KV_HEADS)]
    return (qs, [k3[i // GROUP] for i in range(Q_HEADS)], [v3[i // GROUP] for i in range(Q_HEADS)],
            [s_ref[:, i:i + 1] for i in range(Q_HEADS)])


def _attention(q, k, v, sinks, *, name):
    lp = q.shape[0]
    nb = lp // BLOCK
    prev = lambda i: (jnp.maximum(i - 1, 0), 0)
    cur = lambda i: (i, 0)
    meta = lambda i: (0, 0)
    kv = lambda index: pl.BlockSpec((BLOCK, KV_W), index)

    def body(*refs):
        o_ref = refs[-1]
        qs, k3s, v3s, sink = _attn_operands(refs[:-1])
        p, _ = _attn_probs(qs, k3s, sink, _attn_masks(pl.program_id(0)))
        out = [_dot(ph[0], v3[0]) + _dot(ph[1], v3[1]) + _dot(ph[2], v3[2]) for ph, v3 in zip(p, v3s)]
        for i in range(Q_HEADS):
            o_ref[:, _head_cols(i)] = out[i].astype(o_ref.dtype)

    return pl.pallas_call(
        body, name=name, grid=(nb,),
        in_specs=[pl.BlockSpec((BLOCK, Q_W), cur), kv(prev), kv(cur), kv(meta), kv(prev), kv(cur), kv(meta),
                  _full((1, Q_HEADS))],
        out_specs=pl.BlockSpec((BLOCK, Q_W), cur),
        out_shape=jax.ShapeDtypeStruct((lp, Q_W), bf16),
        compiler_params=_params(("parallel",)),
    )(q, k, k, k, v, v, v, sinks)


def _attention_bwd(q, k, v, sinks, out, do, *, name):
    lp = q.shape[0]
    nb = lp // BLOCK
    cur = lambda n: (jnp.minimum(n, nb - 1), 0)
    prev = lambda n: (jnp.maximum(jnp.minimum(n, nb - 1) - 1, 0), 0)
    behind = lambda n: (jnp.maximum(n - 1, 0), 0)
    meta = lambda n: (0, 0)
    kv = lambda index: pl.BlockSpec((BLOCK, KV_W), index)
    scale = HEAD_DIM ** -0.5

    def body(*refs):
        ins, fwd_ref, do_ref = refs[:8], refs[8], refs[9]
        dq_ref, dk_ref, dv_ref, dkm_ref, dvm_ref, ds_ref, carry_k, carry_v = refs[10:]
        n = pl.program_id(0)

        @pl.when(n == 0)
        def _():
            for ref in (dkm_ref, dvm_ref, ds_ref, carry_k, carry_v):
                ref[...] = jnp.zeros_like(ref)

        @pl.when(n < nb)
        def _():
            qs, k3s, v3s, sink = _attn_operands(ins)
            do = [do_ref[:, _head_cols(i)] for i in range(Q_HEADS)]
            p, p_sink = _attn_probs(qs, k3s, sink, _attn_masks(n))
            delta = [jnp.sum(d * fwd_ref[:, _head_cols(i)].astype(f32), -1, keepdims=True) for i, d in enumerate(do)]
            dp = [[_dot(d, vx, "nt") for vx in v3] for d, v3 in zip(do, v3s)]
            ds = [[px * (dx - dl) for px, dx in zip(ph, dh)] for ph, dh, dl in zip(p, dp, delta)]
            dq = [_dot(dsh[0], k3[0]) + _dot(dsh[1], k3[1]) + _dot(dsh[2], k3[2]) for dsh, k3 in zip(ds, k3s)]
            for i in range(Q_HEADS):
                dq_ref[:, _head_cols(i)] = dq[i] * scale
                ds_ref[:, i:i + 1] -= jnp.sum(p_sink[i] * delta[i], axis=0, keepdims=True)
            for h in range(KV_HEADS):
                group = slice(h * GROUP, (h + 1) * GROUP)
                q_all = jnp.concatenate(qs[group], axis=0)
                do_all = jnp.concatenate(do[group], axis=0)
                dk3 = [_dot(jnp.concatenate([dsh[x] for dsh in ds[group]], axis=0), q_all, "tn") for x in range(3)]
                dv3 = [_dot(jnp.concatenate([ph[x] for ph in p[group]], axis=0), do_all, "tn") for x in range(3)]
                hs = _head_cols(h)
                for out_ref, carry, meta_ref, d3 in ((dk_ref, carry_k, dkm_ref, dk3),
                                                     (dv_ref, carry_v, dvm_ref, dv3)):
                    out_ref[:, hs] = carry[:, hs] + d3[0]
                    carry[:, hs] = d3[1]
                    meta_ref[:, hs] += d3[2]

        @pl.when(n == nb)
        def _():
            dk_ref[...] = carry_k[...]
            dv_ref[...] = carry_v[...]

    kv_shape = jax.ShapeDtypeStruct((lp, KV_W), f32)
    one_shape = jax.ShapeDtypeStruct((BLOCK, KV_W), f32)
    return pl.pallas_call(
        body, name=name, grid=(nb + 1,),
        in_specs=[pl.BlockSpec((BLOCK, Q_W), cur), kv(prev), kv(cur), kv(meta), kv(prev), kv(cur), kv(meta),
                  _full((1, Q_HEADS)), pl.BlockSpec((BLOCK, Q_W), cur), pl.BlockSpec((BLOCK, Q_W), cur)],
        out_specs=[pl.BlockSpec((BLOCK, Q_W), cur), kv(behind), kv(behind), kv(meta), kv(meta),
                   _full((1, Q_HEADS))],
        out_shape=[jax.ShapeDtypeStruct((lp, Q_W), f32), kv_shape, kv_shape, one_shape, one_shape,
                   jax.ShapeDtypeStruct((1, Q_HEADS), f32)],
        scratch_shapes=[pltpu.VMEM((BLOCK, KV_W), f32), pltpu.VMEM((BLOCK, KV_W), f32)],
        compiler_params=_params(("arbitrary",)),
    )(q, k, k, k, v, v, v, sinks, out, do)


@jax.custom_vjp
def _known_inverse(l, x):
    return x


def _known_inverse_fwd(l, x):
    return x, x


def _known_inverse_bwd(x, ct):
    return _dot(_dot(x, ct, "tn"), x, "nt"), jnp.zeros_like(x)


_known_inverse.defvjp(_known_inverse_fwd, _known_inverse_bwd)


@jax.custom_vjp
def _decayed(x, c):
    return (x * jnp.exp(c)).astype(bf16).astype(f32)


def _decayed_fwd(x, c):
    e = jnp.exp(c)
    out = (x * e).astype(bf16).astype(f32)
    return out, (e, out)


def _decayed_bwd(res, ct):
    e, out = res
    return ct * e, ct * out


_decayed.defvjp(_decayed_fwd, _decayed_bwd)


@jax.custom_vjp
def _pair(x, y):
    return _dot(x, y, "nt")


def _pair_fwd(x, y):
    return _dot(x, y, "nt"), (x, y)


def _pair_bwd(res, ct):
    x, y = res
    hi = ct.astype(bf16)
    lo = (ct - hi.astype(f32)).astype(bf16)
    return _dot(hi, y) + _dot(lo, y), _dot(hi, x, "tn") + _dot(lo, x, "tn")


_pair.defvjp(_pair_fwd, _pair_bwd)


def _scan_chunk(s0, r, lw, k, v, a, b, inv=None):
    t = r[0].shape[0]
    ii = lax.broadcasted_iota(jnp.int32, (t, t), 0)
    jj = lax.broadcasted_iota(jnp.int32, (t, t), 1)
    incl = jj <= ii
    strict = jj < ii
    tri = incl.astype(f32)
    eye = jnp.where(ii == jj, 1.0, 0.0)
    cl = [_const_dot(tri, x) for x in lw]
    mid = [c[t // 2 - 1:t // 2, :] for c in cl]
    s0 = [s * jnp.exp(m) for s, m in zip(s0, mid)]
    cl = [c - m for c, m in zip(cl, mid)]
    rt = [_decayed(x, c) for x, c in zip(r, cl)]
    at = [_decayed(x, c - l) for x, c, l in zip(a, cl, lw)]
    bt = [_decayed(x, -c) for x, c in zip(b, cl)]
    kt = [_decayed(x, -c) for x, c in zip(k, cl)]
    l_ab = [jnp.where(strict, _pair(x, y), 0.0) for x, y in zip(at, bt)]
    l_ak = [jnp.where(strict, _pair(x, y), 0.0) for x, y in zip(at, kt)]
    r_b = [jnp.where(incl, _pair(x, y), 0.0) for x, y in zip(rt, bt)]
    r_k = [jnp.where(incl, _pair(x, y), 0.0) for x, y in zip(rt, kt)]
    if inv is None:
        inv = [eye + x for x in l_ab]
        pw = l_ab
        for _ in range(int(math.log2(t)) - 1):
            pw = [_dot(x, x) for x in pw]
            inv = [x + _dot(x, y) for x, y in zip(inv, pw)]
    else:
        inv = [_known_inverse(x, y) for x, y in zip(l_ab, inv)]
    rhs = [_dot(x, s, "nt") + _dot(m, y) for x, s, m, y in zip(at, s0, l_ak, v)]
    u = [_dot(x, y) for x, y in zip(inv, rhs)]
    y_s = [_dot(x, s, "nt") for x, s in zip(rt, s0)]
    y = [ys + _dot(m, uu) + _dot(n, vv) for ys, m, uu, n, vv in zip(y_s, r_b, u, r_k, v)]
    grow = [s + _dot(uu, x, "tn") + _dot(vv, z, "tn") for s, uu, x, vv, z in zip(s0, u, bt, v, kt)]
    s1 = [g * jnp.exp(c[t - 1:t, :]) for g, c in zip(grow, cl)]
    return y, s1, inv


def _head_rows(h):
    return slice(h * RWKV_HEAD, (h + 1) * RWKV_HEAD)


def _per_head(ref):
    return [ref[:, _head_rows(h)] for h in range(RWKV_HEADS)]


def _scan(r, lw, k, v, a, b, *, name):
    lp = r.shape[0]
    nc = lp // CHUNK
    row = pl.BlockSpec((CHUNK, RWKV_DIM), lambda c: (c, 0))

    def body(r_ref, lw_ref, k_ref, v_ref, a_ref, b_ref, y_ref, s_ref, inv_ref, state):
        @pl.when(pl.program_id(0) == 0)
        def _():
            state[...] = jnp.zeros_like(state)

        s_ref[...] = state[...]
        s0 = [state[_head_rows(h), :] for h in range(RWKV_HEADS)]
        y, s1, inv = _scan_chunk(s0, *[_per_head(ref) for ref in (r_ref, lw_ref, k_ref, v_ref, a_ref, b_ref)])
        for h in range(RWKV_HEADS):
            y_ref[:, _head_rows(h)] = y[h]
            state[_head_rows(h), :] = s1[h]
            inv_ref[h * CHUNK:(h + 1) * CHUNK, :] = inv[h].astype(inv_ref.dtype)

    return pl.pallas_call(
        body, name=name, grid=(nc,), in_specs=[row] * 6,
        out_specs=[row, pl.BlockSpec((RWKV_DIM, RWKV_HEAD), lambda c: (c, 0)),
                   pl.BlockSpec((RWKV_HEADS * CHUNK, CHUNK), lambda c: (c, 0))],
        out_shape=[jax.ShapeDtypeStruct((lp, RWKV_DIM), f32), jax.ShapeDtypeStruct((nc * RWKV_DIM, RWKV_HEAD), f32),
                   jax.ShapeDtypeStruct((nc * RWKV_HEADS * CHUNK, CHUNK), bf16)],
        scratch_shapes=[pltpu.VMEM((RWKV_DIM, RWKV_HEAD), f32)],
        compiler_params=_params(("arbitrary",)),
    )(r, lw, k, v, a, b)


def _scan_bwd(r, lw, k, v, a, b, states, inverses, dy, *, name):
    lp = r.shape[0]
    nc = lp // CHUNK
    back = lambda c: (nc - 1 - c, 0)
    row = pl.BlockSpec((CHUNK, RWKV_DIM), back)

    def body(r_ref, lw_ref, k_ref, v_ref, a_ref, b_ref, s_ref, inv_ref, dy_ref,
             dr_ref, dlw_ref, dk_ref, dv_ref, da_ref, db_ref, dstate):
        @pl.when(pl.program_id(0) == 0)
        def _():
            dstate[...] = jnp.zeros_like(dstate)

        outs = (dr_ref, dlw_ref, dk_ref, dv_ref, da_ref, db_ref)
        s0 = [s_ref[_head_rows(h), :] for h in range(RWKV_HEADS)]
        inv = [inv_ref[h * CHUNK:(h + 1) * CHUNK, :].astype(f32) for h in range(RWKV_HEADS)]
        _, vjp = jax.vjp(lambda *args: _scan_chunk(*args, inv=inv)[:2], s0,
                         *[_per_head(ref) for ref in (r_ref, lw_ref, k_ref, v_ref, a_ref, b_ref)])
        g = vjp((_per_head(dy_ref), [dstate[_head_rows(h), :] for h in range(RWKV_HEADS)]))
        for h in range(RWKV_HEADS):
            dstate[_head_rows(h), :] = g[0][h]
            for o_ref, gv in zip(outs, g[1:]):
                o_ref[:, _head_rows(h)] = gv[h]

    shape = jax.ShapeDtypeStruct((lp, RWKV_DIM), f32)
    return pl.pallas_call(
        body, name=name, grid=(nc,),
        in_specs=[row] * 6 + [pl.BlockSpec((RWKV_DIM, RWKV_HEAD), back),
                              pl.BlockSpec((RWKV_HEADS * CHUNK, CHUNK), back), row],
        out_specs=[row] * 6, out_shape=[shape] * 6,
        scratch_shapes=[pltpu.VMEM((RWKV_DIM, RWKV_HEAD), f32)],
        compiler_params=_params(("arbitrary",)),
    )(r, lw, k, v, a, b, states, inverses, dy)


def _loss_head(act, w_down, h1, target, g_final, *, name):
    lp = h1.shape[0]
    per_tile = 3
    tm = per_tile * BLOCK
    last_block = (lp - FRONT) // BLOCK - 1

    def body(a_ref, w_ref, h_ref, t0_ref, t1_ref, t2_ref, g_ref, loss_ref, dh_ref, dg_ref):
        i = pl.program_id(0)
        target_rows = jnp.concatenate([t0_ref[...], t1_ref[...], t2_ref[...]], axis=0)
        real = i * tm + lax.broadcasted_iota(jnp.int32, (tm, 1), 0) >= FRONT

        def tile_loss(hv, gv):
            err = _rms(hv, gv) - target_rows
            return 0.5 * jnp.sum(jnp.where(real, jnp.mean(err * err, axis=-1, keepdims=True), 0.0))

        h2 = _dot(a_ref[...], w_ref[...], "nn") + h_ref[...]
        loss, (dh, dg) = jax.value_and_grad(tile_loss, argnums=(0, 1))(h2, g_ref[...])

        @pl.when(i == 0)
        def _():
            loss_ref[...] = jnp.zeros_like(loss_ref)
            dg_ref[...] = jnp.zeros_like(dg_ref)

        loss_ref[...] += jnp.full(loss_ref.shape, loss, f32)
        dg_ref[...] += dg
        dh_ref[...] = dh

    def target_block(j):
        return pl.BlockSpec((BLOCK, D_MODEL),
                            lambda i: (jnp.clip(per_tile * i + j - FRONT // BLOCK, 0, last_block), 0))

    return pl.pallas_call(
        body, name=name, grid=(lp // tm,),
        in_specs=[pl.BlockSpec((tm, act.shape[1]), lambda i: (i, 0)), _full(w_down.shape),
                  pl.BlockSpec((tm, D_MODEL), lambda i: (i, 0)), target_block(0), target_block(1), target_block(2),
                  _full(g_final.shape)],
        out_specs=[_full((8, 128)), pl.BlockSpec((tm, D_MODEL), lambda i: (i, 0)), _full(g_final.shape)],
        out_shape=[jax.ShapeDtypeStruct((8, 128), f32), jax.ShapeDtypeStruct((lp, D_MODEL), f32),
                   jax.ShapeDtypeStruct(g_final.shape, f32)],
        compiler_params=_params(("arbitrary",)),
    )(act, w_down, h1, target, target, target, g_final)


def _embed_norm(x, meta, g, *, name):
    seq = x.shape[0]
    lp = seq + FRONT
    per_tile = 3
    tm = per_tile * BLOCK
    last_block = seq // BLOCK - 1

    def body(x0_ref, x1_ref, x2_ref, meta_ref, g_ref, h_ref, u_ref):
        front = jnp.concatenate([jnp.zeros((PAD, D_MODEL), f32), meta_ref[...]], axis=0)
        first = jnp.where(pl.program_id(0) == 0, front, x0_ref[...])
        h = jnp.concatenate([first, x1_ref[...], x2_ref[...]], axis=0)
        h_ref[...] = h
        u_ref[...] = _rms(h, g_ref[...]).astype(u_ref.dtype)

    def x_block(j):
        return pl.BlockSpec((BLOCK, D_MODEL),
                            lambda i: (jnp.clip(per_tile * i + j - FRONT // BLOCK, 0, last_block), 0))

    tile = pl.BlockSpec((tm, D_MODEL), lambda i: (i, 0))
    return pl.pallas_call(
        body, name=name, grid=(lp // tm,),
        in_specs=[x_block(0), x_block(1), x_block(2), _full(meta.shape), _full(g.shape)],
        out_specs=[tile, tile],
        out_shape=[jax.ShapeDtypeStruct((lp, D_MODEL), f32), jax.ShapeDtypeStruct((lp, D_MODEL), bf16)],
        compiler_params=_params(("parallel",)),
    )(x, x, x, meta, g)


def _input_norm_bwd(h0, g, du, dh1, *, name):
    lp = h0.shape[0]
    blocks = (lp - FRONT) // FRONT
    per_tile = max(n for n in (4, 3, 2, 1) if blocks % n == 0)
    ins = (h0, du, dh1)

    def body(*refs):
        tiles = [refs[k * per_tile:(k + 1) * per_tile] for k in range(len(ins))]
        front_refs = refs[len(ins) * per_tile:len(ins) * (per_tile + 1)]
        g_ref, dx_ref, front_ref, dg_ref = refs[len(ins) * (per_tile + 1):]

        def cotangents(h_ref, du_ref, dh1_ref):
            _, vjp = jax.vjp(lambda hv, gv: (_rms(hv, gv), hv), h_ref[...], g_ref[...])
            return vjp((du_ref[...].astype(f32), dh1_ref[...]))

        @pl.when(pl.program_id(0) == 0)
        def _():
            front_ref[...], dg_ref[...] = cotangents(*front_refs)

        dg = jnp.zeros(dg_ref.shape, f32)
        for j in range(per_tile):
            dh, dg_j = cotangents(*(t[j] for t in tiles))
            dx_ref[j * FRONT:(j + 1) * FRONT, :] = dh
            dg = dg + dg_j
        dg_ref[...] += dg

    def block(j):
        return pl.BlockSpec((FRONT, D_MODEL), lambda i: (per_tile * i + j + 1, 0))

    first = pl.BlockSpec((FRONT, D_MODEL), lambda i: (0, 0))
    return pl.pallas_call(
        body, name=name, grid=(blocks // per_tile,),
        in_specs=[block(j) for _ in ins for j in range(per_tile)] + [first] * len(ins) + [_full(g.shape)],
        out_specs=[pl.BlockSpec((per_tile * FRONT, D_MODEL), lambda i: (i, 0)), _full((FRONT, D_MODEL)),
                   _full(g.shape)],
        out_shape=[jax.ShapeDtypeStruct((lp - FRONT, D_MODEL), f32), jax.ShapeDtypeStruct((FRONT, D_MODEL), f32),
                   jax.ShapeDtypeStruct(g.shape, f32)],
        compiler_params=_params(("arbitrary",)),
    )(*(a for a in ins for _ in range(per_tile)), *ins, g)


def _local_step(x, target, meta, p, early_weights=None, late_weights=None, emit=None):
    emit = emit or (lambda group, grads: 0.0)
    seq = x.shape[0]
    lp = seq + FRONT
    cos_t, sin_t, swap = _rope_tables(lp)
    hsum = _head_sum_matrix(RWKV_DIM, RWKV_HEAD)
    hmean = hsum / RWKV_HEAD
    post_params = [p["ln_w"], p["ln_b"], p["r_k"], hmean]

    h0, u = _embed_norm(x, meta, p["norm_mix_g"], name="norm_mix")
    if early_weights is not None:
        p = {**p, **early_weights(u)}
    prep_params = [p["w0"], p["w2"], p["a0"], p["a2"], p["g2"], p["k_k"], p["k_a"], hsum]
    in_widths = [ATTN_PROJ, RKV_W, LORA_W, 2 * D_MODEL]
    q, k, v, p_rkv, p_lora, gates = _proj_in(u, p["w_in_lr"], p["b_in"], in_widths,
                                             (cos_t, sin_t, swap), name="proj_in", zero_rows_below=PAD)
    y_attn = _attention(q, k, v, p["sinks"], name="attention")

    mix_rkv, mix_lora = p["mix"][:, :RKV_W], p["mix"][:, RKV_W:]
    r_, lw_, k_, v_, a_, b_, g_ = _mixer_inputs([p_rkv, p_lora], [mix_rkv, mix_lora], prep_params,
                                                name="mixer_inputs")
    y_scan, states, inverses = _scan(r_, lw_, k_, v_, a_, b_, name="wkv_scan")
    (y_rwkv,) = _rowwise(_rwkv_post, [y_scan, r_, k_, v_, g_], post_params, [(RWKV_DIM, bf16)], name="rwkv_post")

    if late_weights is not None:
        p = {**p, **late_weights(y_rwkv)}
    br_a, br_r, merged = _branch_merge(y_attn, y_rwkv, p["w_br_attn_t"], p["w_br_rwkv_t"], gates, name="branch_merge")
    h1, f = _residual_norm(merged, p["w_o"], h0, p["norm_ffn_g"], name="out_proj")
    gate, up, act = _ffn_in(f, p["w_gate_t"], p["w_up_t"], name="ffn_in")

    loss8, dh2, d_final_g = _loss_head(act, p["w_down"], h1, target, p["norm_final_g"], name="loss_head")
    dgate, dup, dh1, d_ffn_g = _ffn_bwd(dh2, p["w_down"], gate, up, p["w_gate_t"], p["w_up_t"], h1, p["norm_ffn_g"],
                                        name="ffn_bwd")
    d_w_down = _mm_tn(act, dh2, name="dw_down")
    d_w_gate_t = _mm_tn(dgate, f, name="dw_gate")
    d_w_up_t = _mm_tn(dup, f, name="dw_up")
    zero = emit("ffn", dict(w_down=d_w_down, w_gate_t=d_w_gate_t, w_up_t=d_w_up_t))
    dgates, dbr_a, dbr_r, dy_attn, dy_rwkv = _branch_merge_bwd(
        dh1, p["w_o"], gates, br_a, br_r, p["w_br_attn_t"], p["w_br_rwkv_t"], name="branch_merge_bwd")
    d_w_o = _mm_tn(merged, dh1, name="dw_o")
    d_w_br_attn_t = _mm_tn(dbr_a, y_attn, name="dw_br_attn")
    d_w_br_rwkv_t = _mm_tn(dbr_r, y_rwkv, name="dw_br_rwkv")
    zero = zero + emit("branch", dict(w_o=d_w_o, w_br_attn_t=d_w_br_attn_t, w_br_rwkv_t=d_w_br_rwkv_t))

    post_params = [p["ln_w"] + zero, p["ln_b"], p["r_k"], hmean]
    res = _rowwise_bwd(_rwkv_post, [y_scan, r_, k_, v_, g_], post_params, [[dy_rwkv]], name="rwkv_post_bwd",
                       diff_rows=[True] * 5, diff_params=[True, True, True, False])
    dy_scan, dr_p, dk_p, dv_p, dg_p, d_ln_w, d_ln_b, d_r_k = res
    dr_s, dlw_s, dk_s, dv_s, da_s, db_s = _scan_bwd(r_, lw_, k_, v_, a_, b_, states, inverses, dy_scan,
                                                    name="wkv_scan_bwd")
    res = _mixer_inputs_bwd([p_rkv, p_lora], [mix_rkv, mix_lora], prep_params,
                            [[dr_s, dr_p], [dlw_s], [dk_s, dk_p], [dv_s, dv_p], [da_s], [db_s], [dg_p]],
                            name="mixer_inputs_bwd")
    dp_rkv, dp_lora, d_mix_rkv, d_mix_lora, d_w0, d_w2, d_a0, d_a2, d_g2, d_k_k, d_k_a = res

    dq, dk, dv, dkm, dvm, d_sinks = _attention_bwd(q, k, v, p["sinks"], y_attn, dy_attn, name="attention_bwd")
    (dqkv,) = _rowwise(_attn_prep_transposed, [dq, dk, dv, cos_t, sin_t], [swap, dkm, dvm], [(ATTN_PROJ, bf16)],
                       name="attn_prep_bwd")

    in_rows, (at_qkv, at_rkv, at_lora, at_gates) = p["w_in_lr"].shape[1], [off for off, _ in _pieces(in_widths)]
    d_w_in_t, db_gates = _mm_tn(dgates, u, name="dw_gates", colsum=True, into=(in_rows, at_gates, None))
    d_w_in_t, db_rkv = _mm_tn(dp_rkv, u, name="dw_rkv", colsum=True, into=(in_rows, at_rkv, d_w_in_t))
    d_w_in_t, db_lora = _mm_tn(dp_lora, u, name="dw_lora", colsum=True, into=(in_rows, at_lora, d_w_in_t))
    d_w_in_t, db_qkv = _mm_tn(dqkv, u, name="dw_qkv", colsum=True, into=(in_rows, at_qkv, d_w_in_t))
    zero = emit("input", dict(w_in_t=d_w_in_t, g2=d_g2, w2=d_w2, a2=d_a2))
    du = _proj_in_bwd([dqkv, dp_rkv, dp_lora, dgates], p["w_in_lr"], name="d_u")
    dx, d_front, d_mix_g = _input_norm_bwd(h0, p["norm_mix_g"] + zero, du, dh1, name="norm_mix_bwd")

    grads = dict(
        w_in_t=d_w_in_t,
        b_in=jnp.concatenate([db_qkv, db_rkv, db_lora, db_gates], axis=1),
        mix=jnp.concatenate([d_mix_rkv, d_mix_lora], axis=1),
        norm_mix_g=d_mix_g, sinks=d_sinks, w0=d_w0, w2=d_w2, a0=d_a0, a2=d_a2, g2=d_g2, k_k=d_k_k, k_a=d_k_a,
        r_k=d_r_k, ln_w=d_ln_w, ln_b=d_ln_b, w_br_attn_t=d_w_br_attn_t, w_br_rwkv_t=d_w_br_rwkv_t, w_o=d_w_o,
        norm_ffn_g=d_ffn_g, w_gate_t=d_w_gate_t, w_up_t=d_w_up_t, w_down=d_w_down, norm_final_g=d_final_g,
        meta=d_front[PAD:],
    )
    return loss8[0, 0], dx, grads


def _position():
    return lax.axis_index("x"), lax.axis_index("y"), lax.axis_index("c")


def _other_chips(x, y):
    return [(1 - x, y), (x, 1 - y), (1 - x, 1 - y)]


_HBM = pl.BlockSpec(memory_space=pltpu.HBM)
_SEM = pl.BlockSpec(memory_space=pltpu.SEMAPHORE)
_EFFECT = pltpu.SideEffectType.DATAFLOW_SIDE_EFFECTING


def _landing_zone(src, kind):
    shape = {"whole": (N_CHIPS,) + src.shape, "half": (2, N_CHIPS, src.shape[0], src.shape[1] // 2),
             "slab": (3,) + src.shape[1:], "sibling": src.shape, "all": (N_DEV - 1,) + src.shape}[kind]
    return lax.empty(shape, src.dtype)


def _copies_per_source(kind):
    return {"sibling": 1, "all": N_DEV - 1}.get(kind, 3)


def _chip_copies(src_refs, land_refs, send_sems, recv_sems, kind):
    x, y, c = _position()
    if kind == "all":
        copies = []
        for a, (src, land) in enumerate(zip(src_refs, land_refs)):
            for rel in range(1, N_DEV):
                peer = ((1 - x) if rel & 4 else x, (1 - y) if rel & 2 else y, (1 - c) if rel & 1 else c)
                k = (N_DEV - 1) * a + rel - 1
                copies.append(pltpu.make_async_remote_copy(
                    src_ref=src, dst_ref=land.at[rel - 1], send_sem=send_sems.at[k], recv_sem=recv_sems.at[k],
                    device_id=peer, device_id_type=MESH))
        return copies
    if kind == "sibling":
        return [pltpu.make_async_remote_copy(
            src_ref=src, dst_ref=land, send_sem=send_sems.at[a], recv_sem=recv_sems.at[a],
            device_id=(x, y, 1 - c), device_id_type=MESH) for a, (src, land) in enumerate(zip(src_refs, land_refs))]
    copies = []
    for a, (src, land) in enumerate(zip(src_refs, land_refs)):
        for j, (px, py) in enumerate(_other_chips(x, y)):
            if kind == "whole":
                src_ref, dst_ref = src, land.at[2 * x + y]
            elif kind == "half":
                half = src.shape[1] // 2
                src_ref, dst_ref = src.at[:, pl.ds(pl.multiple_of(c * half, half), half)], land.at[c, 2 * x + y]
            else:
                src_ref, dst_ref = src.at[2 * px + py], land.at[j]
            copies.append(pltpu.make_async_remote_copy(
                src_ref=src_ref, dst_ref=dst_ref, send_sem=send_sems.at[3 * a + j], recv_sem=recv_sems.at[3 * a + j],
                device_id=(px, py, c), device_id_type=MESH))
    return copies


def _exchange_start_groups(groups, *, name):
    counts = [len(srcs) for srcs, _ in groups]
    arrays = []
    for srcs, kind in groups:
        arrays += list(srcs) + [_landing_zone(s, kind) for s in srcs]
    n_groups, n_arrays = len(groups), len(arrays)

    def body(*refs):
        sems = refs[n_arrays:n_arrays + 2 * n_groups]
        at = 0
        for i, ((_, kind), n) in enumerate(zip(groups, counts)):
            for cp in _chip_copies(refs[at:at + n], refs[at + n:at + 2 * n], sems[2 * i], sems[2 * i + 1], kind):
                cp.start()
            at += 2 * n
        refs[-1][...] = jnp.zeros_like(refs[-1])

    sem_shapes = [pltpu.SemaphoreType.DMA((_copies_per_source(kind) * n,))
                  for (_, kind), n in zip(groups, counts) for _ in range(2)]
    res = pl.pallas_call(
        body, name=name,
        out_shape=(*sem_shapes, *[pltpu.HBM(a.shape, a.dtype) for a in arrays], jax.ShapeDtypeStruct((8, 128), f32)),
        in_specs=[_HBM] * n_arrays,
        out_specs=(*[_SEM] * (2 * n_groups), *[_HBM] * n_arrays, pl.BlockSpec(memory_space=pltpu.VMEM)),
        input_output_aliases={i: 2 * n_groups + i for i in range(n_arrays)},
        compiler_params=pltpu.CompilerParams(has_side_effects=_EFFECT),
    )(*[pltpu.with_memory_space_constraint(a, pltpu.HBM) for a in arrays])
    handles, at = [], 2 * n_groups
    for i, n in enumerate(counts):
        handles.append((res[2 * i], res[2 * i + 1], list(res[at:at + n]), list(res[at + n:at + 2 * n]), res[-1]))
        at += 2 * n
    return handles


def _exchange_start(srcs, *, kind, name):
    return _exchange_start_groups([(srcs, kind)], name=name)[0]


def _exchange_wait(handle, after, *, kind, name):
    send_sems, recv_sems, srcs, lands, _ = handle
    n = len(srcs)

    def body(*refs):
        for cp in _chip_copies(refs[:n], refs[n:2 * n], refs[2 * n], refs[2 * n + 1], kind):
            cp.wait_send()
            cp.wait_recv()

    res = pl.pallas_call(
        body, name=name,
        out_shape=tuple(pltpu.HBM(a.shape, a.dtype) for a in srcs + lands),
        in_specs=[_HBM] * (2 * n) + [_SEM, _SEM, pl.BlockSpec(memory_space=pl.ANY)],
        out_specs=tuple([_HBM] * (2 * n)),
        input_output_aliases={i: i for i in range(2 * n)},
        compiler_params=pltpu.CompilerParams(has_side_effects=_EFFECT),
    )(*srcs, *lands, send_sems, recv_sems, after)
    return list(res[:n]), list(res[n:])


def _sum_own_and_received(g, recv, *, name):
    _, r, w = g.shape
    tm = _tile(r)
    if g.dtype == bf16 and tm % 16:
        tm = r
    x, y, _ = _position()
    me = jnp.reshape(2 * x + y, (1,)).astype(jnp.int32)

    def body(me_ref, g_ref, r_ref, o_ref):
        o_ref[...] = (g_ref[0].astype(f32) + r_ref[0].astype(f32)) + (r_ref[1].astype(f32) + r_ref[2].astype(f32))

    return pl.pallas_call(
        body, name=name,
        grid_spec=pltpu.PrefetchScalarGridSpec(
            num_scalar_prefetch=1, grid=(r // tm,),
            in_specs=[pl.BlockSpec((1, tm, w), lambda i, me_ref: (me_ref[0], i, 0)),
                      pl.BlockSpec((3, tm, w), lambda i, me_ref: (0, i, 0))],
            out_specs=pl.BlockSpec((tm, w), lambda i, me_ref: (i, 0))),
        out_shape=jax.ShapeDtypeStruct((r, w), f32),
        compiler_params=_params(("parallel",)),
    )(me, g, recv)


def _swap_halves(zone, *, name):
    def body(z_ref, o_ref, send_sems, recv_sems):
        x, y, c = _position()
        mine = [pltpu.make_async_remote_copy(
            src_ref=o_ref.at[c, 2 * px + py], dst_ref=o_ref.at[c, 2 * px + py], send_sem=send_sems.at[j],
            recv_sem=recv_sems.at[j], device_id=(x, y, 1 - c), device_id_type=MESH)
            for j, (px, py) in enumerate(_other_chips(x, y))]
        for cp in mine:
            cp.start()
        for j, (px, py) in enumerate(_other_chips(x, y)):
            pltpu.make_async_remote_copy(
                src_ref=o_ref.at[c, 2 * px + py], dst_ref=o_ref.at[1 - c, 2 * px + py], send_sem=send_sems.at[j],
                recv_sem=recv_sems.at[j], device_id=(x, y, 1 - c), device_id_type=MESH).wait_recv()
        for cp in mine:
            cp.wait_send()

    return pl.pallas_call(
        body, name=name,
        in_specs=[pl.BlockSpec(memory_space=pl.ANY)], out_specs=pl.BlockSpec(memory_space=pl.ANY),
        out_shape=jax.ShapeDtypeStruct(zone.shape, zone.dtype), input_output_aliases={0: 0},
        scratch_shapes=[pltpu.SemaphoreType.DMA((3,)), pltpu.SemaphoreType.DMA((3,))],
    )(zone)


def _sum_all_devices(own, received, *, name):
    def body(own_ref, got_ref, o_ref):
        x, y, c = _position()
        me = 4 * x + 2 * y + c
        acc = None
        for d in range(N_DEV):
            rel = jnp.bitwise_xor(me, d)
            block = jnp.where(rel == 0, own_ref[...], got_ref[jnp.maximum(rel, 1) - 1])
            acc = block if acc is None else acc + block
        o_ref[...] = acc

    return pl.pallas_call(
        body, name=name,
        in_specs=[pl.BlockSpec(memory_space=pltpu.VMEM)] * 2, out_specs=pl.BlockSpec(memory_space=pltpu.VMEM),
        out_shape=jax.ShapeDtypeStruct(own.shape, f32),
    )(own, received)


def _adam_math(w, g, m, v):
    nm = ADAM_B1 * m + (1.0 - ADAM_B1) * g
    nv = ADAM_B2 * v + (1.0 - ADAM_B2) * (g * g)
    m_hat = nm / (1.0 - ADAM_B1 ** ADAM_STEP)
    v_hat = nv / (1.0 - ADAM_B2 ** ADAM_STEP)
    return -ADAM_LR * (m_hat / (jnp.sqrt(v_hat) + ADAM_EPS) + ADAM_WD * w), nm, nv


def _adamw(w, g_parts, m, v, *, name, transposed=False, after=None):
    rows, cols = w.shape
    ordered = after is not None
    if transposed:
        tm = 256 if rows % 256 == 0 else rows
        g_spec = pl.BlockSpec((cols, tm), lambda i: (0, i))
    else:
        tm = _tile(rows, 256)
        g_spec = pl.BlockSpec((tm, cols), lambda i: (i, 0))
    n = len(g_parts)

    def body(*refs):
        refs = refs[1:] if ordered else refs
        w_ref, m_ref, v_ref = refs[0], refs[1 + n], refs[2 + n]
        g_ref, d_ref, nm_ref, nv_ref = refs[3 + n:]
        gv = refs[1][...]
        for part in refs[2:1 + n]:
            gv = gv + part[...]
        if transposed:
            gv = gv.T
        g_ref[...] = gv
        d_ref[...], nm_ref[...], nv_ref[...] = _adam_math(w_ref[...], gv, m_ref[...], v_ref[...])

    spec = pl.BlockSpec((tm, cols), lambda i: (i, 0))
    shape = jax.ShapeDtypeStruct((rows, cols), f32)
    return pl.pallas_call(
        body, name=name, grid=(rows // tm,),
        in_specs=[pl.BlockSpec(memory_space=pl.ANY)] * ordered + [spec] + [g_spec] * n + [spec] * 2,
        out_specs=[spec] * 4, out_shape=[shape] * 4,
        compiler_params=_params(("parallel",)),
    )(*([after] if ordered else []), w, *g_parts, m, v)


def _pad_rows(a, rows):
    return jnp.concatenate([a, jnp.zeros((rows - a.shape[0], a.shape[1]), a.dtype)], axis=0) if rows > a.shape[0] else a


_SMALL = (("norm_mix_g", D_MODEL), ("b_in", D_IN), ("sinks", Q_HEADS), ("mix", RWKV_PROJ), ("w0", RWKV_DIM),
          ("a0", RWKV_DIM), ("k_k", RWKV_DIM), ("k_a", RWKV_DIM), ("r_k", RWKV_DIM), ("ln_w", RWKV_DIM),
          ("ln_b", RWKV_DIM), ("norm_ffn_g", D_MODEL), ("norm_final_g", D_MODEL))


LANES = 128


def _small_layout():
    out, off = {}, 0
    for n, size in _SMALL + (("loss", 1),):
        pieces, col = [], 0
        while col < size:
            row, lane = divmod(off + col, PACK_W)
            width = min(size - col, PACK_W - lane)
            pieces.append((row, lane, width, col))
            col += width
        out[n] = pieces
        off += -(-size // LANES) * LANES
    return out, -(-off // PACK_W)


def _pack_small(d, loss):
    layout, rows = _small_layout()
    parts, used = [], 0
    for n, size in _SMALL + (("loss", 1),):
        item = loss if n == "loss" else d[n]
        fill = -size % LANES
        parts += [item.reshape(-1).astype(f32), jnp.zeros((fill,), f32)]
        used += size + fill
    parts.append(jnp.zeros((rows * PACK_W - used,), f32))
    return jnp.concatenate(parts).reshape(rows, PACK_W)


def _adamw_small(packed, first_row, ws, ms, vs, meta, *, name):
    layout, _ = _small_layout()
    names = [n for n, _ in _SMALL]
    k = len(names)

    def body(*refs):
        packed_ref = refs[0]
        w_refs, m_refs, v_refs = refs[1:1 + k], refs[1 + k:1 + 2 * k], refs[1 + 2 * k:1 + 3 * k]
        meta_refs = refs[1 + 3 * k:5 + 3 * k]
        outs = refs[5 + 3 * k:]
        for idx, n in enumerate(names):
            for row, lane, width, col in layout[n]:
                gv = packed_ref[first_row + row:first_row + row + 1, lane:lane + width]
                at = (slice(None), slice(col, col + width))
                new = _adam_math(w_refs[idx][at], gv, m_refs[idx][at], v_refs[idx][at])
                for o_ref, val in zip(outs[4 * idx:4 * idx + 4], (gv,) + new):
                    o_ref[at] = val
        for o_ref, val in zip(outs[4 * k:], _adam_math(*(r[...] for r in meta_refs))):
            o_ref[...] = val

    ins = [packed] + [d[n] for d in (ws, ms, vs) for n in names] + list(meta)
    shapes = [jax.ShapeDtypeStruct(ws[n].shape, f32) for n in names for _ in range(4)]
    shapes += [jax.ShapeDtypeStruct(meta[0].shape, f32)] * 3
    res = pl.pallas_call(
        body, name=name, grid=(1,), in_specs=[_full(a.shape) for a in ins],
        out_specs=[_full(s.shape) for s in shapes], out_shape=shapes,
        compiler_params=_params(("arbitrary",)),
    )(*ins)
    return {n: res[4 * i:4 * i + 4] for i, n in enumerate(names)}, res[4 * k:]


def kernel(x, meta_tokens, norm_mix_g, w_in, b_in, attn_sinks, rwkv_mix, rwkv_w0, rwkv_w2, rwkv_a0, rwkv_a2, rwkv_g2, rwkv_k_k, rwkv_k_a, rwkv_r_k, rwkv_ln_w, rwkv_ln_b, w_br_attn, w_br_rwkv, w_o, norm_ffn_g, w_ffn_gate, w_ffn_up, w_ffn_down, norm_final_g, loss_target, m_meta_tokens, m_norm_mix_g, m_w_in, m_b_in, m_attn_sinks, m_rwkv_mix, m_rwkv_w0, m_rwkv_w2, m_rwkv_a0, m_rwkv_a2, m_rwkv_g2, m_rwkv_k_k, m_rwkv_k_a, m_rwkv_r_k, m_rwkv_ln_w, m_rwkv_ln_b, m_w_br_attn, m_w_br_rwkv, m_w_o, m_norm_ffn_g, m_w_ffn_gate, m_w_ffn_up, m_w_ffn_down, m_norm_final_g, v_meta_tokens, v_norm_mix_g, v_w_in, v_b_in, v_attn_sinks, v_rwkv_mix, v_rwkv_w0, v_rwkv_w2, v_rwkv_a0, v_rwkv_a2, v_rwkv_g2, v_rwkv_k_k, v_rwkv_k_a, v_rwkv_r_k, v_rwkv_ln_w, v_rwkv_ln_b, v_w_br_attn, v_w_br_rwkv, v_w_o, v_norm_ffn_g, v_w_ffn_gate, v_w_ffn_up, v_w_ffn_down, v_norm_final_g):
    names = ("meta_tokens", "norm_mix_g", "w_in", "b_in", "attn_sinks", "rwkv_mix", "rwkv_w0", "rwkv_w2", "rwkv_a0",
             "rwkv_a2", "rwkv_g2", "rwkv_k_k", "rwkv_k_a", "rwkv_r_k", "rwkv_ln_w", "rwkv_ln_b", "w_br_attn",
             "w_br_rwkv", "w_o", "norm_ffn_g", "w_ffn_gate", "w_ffn_up", "w_ffn_down", "norm_final_g")
    w_all = dict(zip(names, (meta_tokens, norm_mix_g, w_in, b_in, attn_sinks, rwkv_mix, rwkv_w0, rwkv_w2, rwkv_a0,
                             rwkv_a2, rwkv_g2, rwkv_k_k, rwkv_k_a, rwkv_r_k, rwkv_ln_w, rwkv_ln_b, w_br_attn,
                             w_br_rwkv, w_o, norm_ffn_g, w_ffn_gate, w_ffn_up, w_ffn_down, norm_final_g)))
    m_all = dict(zip(names, (m_meta_tokens, m_norm_mix_g, m_w_in, m_b_in, m_attn_sinks, m_rwkv_mix, m_rwkv_w0,
                             m_rwkv_w2, m_rwkv_a0, m_rwkv_a2, m_rwkv_g2, m_rwkv_k_k, m_rwkv_k_a, m_rwkv_r_k,
                             m_rwkv_ln_w, m_rwkv_ln_b, m_w_br_attn, m_w_br_rwkv, m_w_o, m_norm_ffn_g, m_w_ffn_gate,
                             m_w_ffn_up, m_w_ffn_down, m_norm_final_g)))
    v_all = dict(zip(names, (v_meta_tokens, v_norm_mix_g, v_w_in, v_b_in, v_attn_sinks, v_rwkv_mix, v_rwkv_w0,
                             v_rwkv_w2, v_rwkv_a0, v_rwkv_a2, v_rwkv_g2, v_rwkv_k_k, v_rwkv_k_a, v_rwkv_r_k,
                             v_rwkv_ln_w, v_rwkv_ln_b, v_w_br_attn, v_w_br_rwkv, v_w_o, v_norm_ffn_g, v_w_ffn_gate,
                             v_w_ffn_up, v_w_ffn_down, v_norm_final_g)))
    cx, cy, _ = _position()
    chip = 2 * cx + cy

    t_of = dict(w_in_t="w_in", w_gate_t="w_ffn_gate", w_up_t="w_ffn_up", w_br_attn_t="w_br_attn",
                w_br_rwkv_t="w_br_rwkv", g2_t="rwkv_g2", w2_t="rwkv_w2", a2_t="rwkv_a2")
    plain_of = dict(w_down="w_ffn_down", w_o="w_o")
    meta_cols = meta_tokens.shape[1]

    def shard(k):
        return (w_all[t_of[k]][0].T if k in t_of else w_all[plain_of[k]][0]).astype(bf16)

    def whole(zone, own):
        return lax.dynamic_update_slice_in_dim(zone, own[None], chip, axis=0).reshape(-1, own.shape[-1])

    tiny = ("g2_t", "w2_t", "a2_t")
    late = ("w_gate_t", "w_up_t", "w_down", "w_o", "w_br_attn_t", "w_br_rwkv_t")
    w_in_own = shard("w_in_t")
    w_in_rows, w_in_cols = w_in_own.shape
    tiny_h, w_in_h, late_h = _exchange_start_groups(
        [([shard(k) for k in tiny] + [meta_tokens], "whole"), ([w_in_own], "half"), ([shard(k) for k in late], "whole")],
        name="gather_start")
    own, zones = _exchange_wait(tiny_h, late_h[4], kind="whole", name="gather_tiny_wait")
    got = {k: whole(z, o) for k, z, o in zip(tiny, zones, own)}
    meta_full = whole(zones[-1], own[-1]).reshape(N_CHIPS, N_META, meta_cols).transpose(1, 0, 2).reshape(N_META, -1)
    p = dict(
        g2=got["g2_t"].T.astype(f32), w2=got["w2_t"].T.astype(f32), a2=got["a2_t"].T.astype(f32),
        b_in=b_in, sinks=attn_sinks, mix=rwkv_mix, w0=rwkv_w0, a0=rwkv_a0, k_k=rwkv_k_k, k_a=rwkv_k_a,
        r_k=rwkv_r_k.reshape(1, RWKV_DIM), ln_w=rwkv_ln_w, ln_b=rwkv_ln_b, norm_mix_g=norm_mix_g,
        norm_ffn_g=norm_ffn_g, norm_final_g=norm_final_g.reshape(1, D_MODEL),
    )

    def early_weights(after):
        own_h, zones_h = _exchange_wait(w_in_h, after, kind="half", name="gather_w_in_wait")
        zone = _swap_halves(zones_h[0], name="swap_w_in_halves")
        own_halves = own_h[0].reshape(w_in_rows, 2, w_in_cols // 2).transpose(1, 0, 2)[:, None]
        zone = lax.dynamic_update_slice(zone, own_halves, (0, chip, 0, 0))
        return dict(w_in_lr=zone.reshape(2, N_CHIPS * w_in_rows, w_in_cols // 2))

    def late_weights(after):
        own_l, zones_l = _exchange_wait(late_h, after, kind="whole", name="gather_late_wait")
        return {k: whole(z, o) for k, z, o in zip(late, zones_l, own_l)}

    started = {}

    def partial_sums(groups, after):
        parts = {}
        for group in groups:
            keys, handle = started[group]
            slabs, lands = _exchange_wait(handle, after, kind="slab", name="scatter_" + group + "_wait")
            parts.update({k: _sum_own_and_received(s, l, name="sum_chips_" + k) for k, s, l in zip(keys, slabs, lands)})
        return parts

    def emit(group, grads_):
        keys = list(grads_)
        slabs = []
        for k in keys:
            a = grads_[k].T if k in ("g2", "w2", "a2") else grads_[k]
            slabs.append(a.reshape(N_CHIPS, a.shape[0] // N_CHIPS, a.shape[1]))
        started[group] = (keys, _exchange_start(slabs, kind="slab", name="scatter_" + group + "_start"))
        zero = started[group][1][4]
        if group == "input":
            started["parts_a"] = partial_sums(("ffn", "branch"), zero)
            started["swap_a"] = _exchange_start(list(started["parts_a"].values()), kind="sibling",
                                                name="swap_cores_a_start")
            zero = started["swap_a"][4]
        return zero[0, 0]

    loss, dx, g = _local_step(x[0], loss_target[0], meta_full, p, early_weights, late_weights, emit)

    grads, delta, new_m, new_v = {}, {}, {}, {}
    in_grad_layout = ("w_in_t", "w_gate_t", "w_up_t")
    weight_of = {**t_of, **plain_of}

    def update(keys, mine, theirs, after=None):
        for k, part, other in zip(keys, mine, theirs):
            both = [part, other]
            k = k + "_t" if k in ("g2", "w2", "a2") else k
            n = weight_of[k]
            shape2 = w_all[n].shape[1:]
            w_, m_, v_ = (a.reshape(shape2) for a in (w_all[n], m_all[n], v_all[n]))
            if k in in_grad_layout:
                raw = _adamw(w_.T, both, m_.T, v_.T, name="adamw_" + n, after=after)
                res = [t.T for t in raw]
            else:
                res = raw = _adamw(w_, both, m_, v_, name="adamw_" + n, transposed=k in t_of, after=after)
            grads[n], delta[n], new_m[n], new_v[n] = (t.reshape(w_all[n].shape) for t in res)
        return raw[1]

    small = _pack_small(g, loss)
    small_rows8 = -(-(small.shape[0] + N_META) // 8) * 8
    small_h = _exchange_start([_pad_rows(jnp.concatenate([g["meta"], small], axis=0), small_rows8)], kind="all",
                              name="reduce_small_start")
    keys_a = list(started["parts_a"])
    mine_a, theirs_a = _exchange_wait(started["swap_a"], small_h[4], kind="sibling", name="swap_cores_a_wait")
    cut = len(started["ffn"][0])
    done = update(keys_a[:cut], mine_a[:cut], theirs_a[:cut])
    parts_b = partial_sums(("input",), done)
    swap_b = _exchange_start(list(parts_b.values()), kind="sibling", name="swap_cores_b_start")
    done = update(keys_a[cut:], mine_a[cut:], theirs_a[cut:], after=swap_b[4])
    small_own, small_got = _exchange_wait(small_h, done, kind="all", name="reduce_small_wait")
    reduced = _sum_all_devices(small_own[0], small_got[0], name="reduce_small_sum")
    update(list(parts_b), *_exchange_wait(swap_b, reduced, kind="sibling", name="swap_cores_b_wait"))
    g_meta = lax.dynamic_slice_in_dim(reduced[:N_META], chip * meta_cols, meta_cols, axis=1)
    (loss_row, loss_lane, _, _), = _small_layout()[0]["loss"]
    loss_total = reduced[N_META + loss_row, loss_lane]

    small_of = dict(norm_mix_g="norm_mix_g", b_in="b_in", attn_sinks="sinks", rwkv_mix="mix", rwkv_w0="w0",
                    rwkv_a0="a0", rwkv_k_k="k_k", rwkv_k_a="k_a", rwkv_r_k="r_k", rwkv_ln_w="ln_w",
                    rwkv_ln_b="ln_b", norm_ffn_g="norm_ffn_g", norm_final_g="norm_final_g")
    as_rows = [{k: src[n].reshape(1, -1) for n, k in small_of.items()} for src in (w_all, m_all, v_all)]
    meta_in = (meta_tokens, g_meta, m_meta_tokens, v_meta_tokens)
    small_out, meta_out = _adamw_small(reduced, N_META, *as_rows, meta_in, name="adamw_small")
    grads["meta_tokens"] = g_meta
    delta["meta_tokens"], new_m["meta_tokens"], new_v["meta_tokens"] = meta_out
    for n, k in small_of.items():
        grads[n], delta[n], new_m[n], new_v[n] = (t.reshape(w_all[n].shape) for t in small_out[k])

    return (loss_total, dx.reshape(x.shape), *[grads[n] for n in names], *[delta[n] for n in names],
            *[new_m[n] for n in names], *[new_v[n] for n in names])
```

```python
import math

import jax
import jax.numpy as jnp
import numpy as np
from jax import lax
from jax.experimental import pallas as pl
from jax.experimental.pallas import tpu as pltpu

f32 = jnp.float32
bf16 = jnp.bfloat16

D_MODEL = 1024
N_META = 16
HEAD_DIM = 64
Q_HEADS = 8
KV_HEADS = 2
GROUP = Q_HEADS // KV_HEADS
WINDOW = 128
BLOCK = 128
ROPE_THETA = 500000.0
ROPE_DIM = HEAD_DIM // 4
RWKV_HEADS = 8
RWKV_HEAD = 64
RWKV_DIM = RWKV_HEADS * RWKV_HEAD
DECAY_LORA = 64
AAA_LORA = 64
GATE_LORA = 160
LORA_W = DECAY_LORA + AAA_LORA + GATE_LORA
RWKV_LN_EPS = 64e-5
D_FF = 2816
Q_W = Q_HEADS * HEAD_DIM
KV_W = KV_HEADS * HEAD_DIM
ATTN_PROJ = Q_W + 2 * KV_W
RKV_W = 3 * RWKV_DIM
RWKV_PROJ = RKV_W + LORA_W
D_IN = ATTN_PROJ + RWKV_PROJ + 2 * D_MODEL
RMS_EPS = 1e-6
NEG_INF = -1e30
PAD = BLOCK - N_META
FRONT = PAD + N_META

ADAM_LR = 0.001
ADAM_B1 = 0.9
ADAM_B2 = 0.999
ADAM_EPS = 1e-08
ADAM_WD = 0.01
ADAM_STEP = 10

N_CHIPS = 4
N_DEV = 8
CHUNK = 128
VMEM_LIMIT = 56 * 1024 * 1024
MM_ROWS = 704
PACK_W = 1024
MESH = pl.DeviceIdType.MESH


def _tile(m, pref=384):
    for step in (16, 8):
        for t in range(min(m, pref) // step * step, 0, -step):
            if m % t == 0:
                return t
    return m


def _params(sem=None):
    return pltpu.CompilerParams(dimension_semantics=sem, vmem_limit_bytes=VMEM_LIMIT)


def _full(shape):
    nd = len(shape)
    return pl.BlockSpec(shape, lambda *_: (0,) * nd)


def _dot(a, b, dims="nn"):
    dn = {"nn": (((1,), (0,)), ((), ())), "nt": (((1,), (1,)), ((), ())), "tn": (((0,), (0,)), ((), ()))}[dims]
    return lax.dot_general(a.astype(bf16), b.astype(bf16), dn, preferred_element_type=f32)


def _two_pass(x, m, dims="nn"):
    x_hi = x.astype(bf16)
    x_lo = (x - x_hi.astype(f32)).astype(bf16)
    return _dot(x_hi, m, dims) + _dot(x_lo, m, dims)


@jax.custom_vjp
def _dot_const(x, m):
    return _two_pass(x, m)


def _dot_const_fwd(x, m):
    return _two_pass(x, m), m


def _dot_const_bwd(m, ct):
    return _two_pass(ct, m, "nt"), jnp.zeros_like(m)


_dot_const.defvjp(_dot_const_fwd, _dot_const_bwd)


def _two_pass_left(m, x, dims):
    x_hi = x.astype(bf16)
    x_lo = (x - x_hi.astype(f32)).astype(bf16)
    return _dot(m, x_hi, dims) + _dot(m, x_lo, dims)


@jax.custom_vjp
def _const_dot(m, x):
    return _two_pass_left(m, x, "nn")


def _const_dot_fwd(m, x):
    return _two_pass_left(m, x, "nn"), m


def _const_dot_bwd(m, ct):
    return jnp.zeros_like(m), _two_pass_left(m, ct, "tn")


_const_dot.defvjp(_const_dot_fwd, _const_dot_bwd)


def _mm(a, b, mode, *, name, out_dtype=f32, bias=None, add=None, zero_rows_below=0):
    m, _ = a.shape
    n = b.shape[1] if mode == "nn" else b.shape[0]
    tm = _tile(m, MM_ROWS)
    has_bias, has_add = bias is not None, add is not None

    def body(*refs):
        a_ref, b_ref = refs[0], refs[1]
        o_ref = refs[-1]
        acc = _dot(a_ref[...], b_ref[...], mode)
        k = 2
        if has_bias:
            acc = acc + refs[k][...]
            k += 1
        if zero_rows_below:
            rows = pl.program_id(0) * tm + lax.broadcasted_iota(jnp.int32, acc.shape, 0)
            acc = jnp.where(rows >= zero_rows_below, acc, 0.0)
        if has_add:
            acc = acc + refs[k][...].astype(f32)
        o_ref[...] = acc.astype(out_dtype)

    ins = [a, b]
    in_specs = [pl.BlockSpec((tm, a.shape[1]), lambda i: (i, 0)), _full(b.shape)]
    if has_bias:
        ins.append(bias)
        in_specs.append(_full(bias.shape))
    if has_add:
        ins.append(add)
        in_specs.append(pl.BlockSpec((tm, n), lambda i: (i, 0)))
    return pl.pallas_call(
        body, name=name, grid=(m // tm,), in_specs=in_specs,
        out_specs=pl.BlockSpec((tm, n), lambda i: (i, 0)),
        out_shape=jax.ShapeDtypeStruct((m, n), out_dtype),
        compiler_params=_params(("parallel",)),
    )(*ins)


def _pieces(widths):
    out, off = [], 0
    for w in widths:
        out.append((off, w))
        off += w
    return out


def _proj_in(a, w_lr, bias, widths, rope, *, name, zero_rows_below=0):
    m, kdim = a.shape
    half = kdim // 2
    tm = _tile(m, MM_ROWS)
    cos_t, sin_t, swap = rope
    out_widths = [Q_W, KV_W, KV_W] + list(widths[1:])

    def body(a_ref, w_ref, b_ref, cos_ref, sin_ref, swap_ref, *outs):
        a_l, a_r = a_ref[:, :half], a_ref[:, half:]
        for j, (off, width) in enumerate(_pieces(widths)):
            acc = _dot(a_l, w_ref[0, off:off + width, :], "nt") + _dot(a_r, w_ref[1, off:off + width, :], "nt")
            acc = acc + b_ref[:, off:off + width]
            if zero_rows_below:
                rows = pl.program_id(0) * tm + lax.broadcasted_iota(jnp.int32, acc.shape, 0)
                acc = jnp.where(rows >= zero_rows_below, acc, 0.0)
            if j == 0:
                qkv = acc.astype(bf16).astype(f32)
                for o_ref, val in zip(outs[:3], _attn_prep(qkv, cos_ref[...], sin_ref[...], swap_ref[...])):
                    o_ref[...] = val.astype(o_ref.dtype)
            else:
                outs[2 + j][...] = acc.astype(outs[2 + j].dtype)

    table = pl.BlockSpec((tm, HEAD_DIM), lambda i: (i, 0))
    return pl.pallas_call(
        body, name=name, grid=(m // tm,),
        in_specs=[pl.BlockSpec((tm, kdim), lambda i: (i, 0)), _full(w_lr.shape), _full(bias.shape), table, table,
                  _full(swap.shape)],
        out_specs=[pl.BlockSpec((tm, w), lambda i: (i, 0)) for w in out_widths],
        out_shape=[jax.ShapeDtypeStruct((m, w), bf16) for w in out_widths],
        compiler_params=_params(("parallel",)),
    )(a, w_lr, bias, cos_t, sin_t, swap)


def _proj_in_bwd(d_list, w_lr, *, name):
    m = d_list[0].shape[0]
    half = w_lr.shape[2]
    widths = [d.shape[1] for d in d_list]
    tm = _tile(m, MM_ROWS)

    def body(*refs):
        w_ref, o_ref = refs[-2], refs[-1]
        for side in range(2):
            acc = None
            for (off, width), d_ref in zip(_pieces(widths), refs):
                term = _dot(d_ref[...], w_ref[side, off:off + width, :])
                acc = term if acc is None else acc + term
            o_ref[:, side * half:(side + 1) * half] = acc.astype(o_ref.dtype)

    return pl.pallas_call(
        body, name=name, grid=(m // tm,),
        in_specs=[pl.BlockSpec((tm, w), lambda i: (i, 0)) for w in widths] + [_full(w_lr.shape)],
        out_specs=pl.BlockSpec((tm, 2 * half), lambda i: (i, 0)),
        out_shape=jax.ShapeDtypeStruct((m, 2 * half), bf16),
        compiler_params=_params(("parallel",)),
    )(*d_list, w_lr)


def _residual_norm(a, w, res, g, *, name):
    m, d = res.shape
    tm = _tile(m, MM_ROWS)

    def body(a_ref, w_ref, r_ref, g_ref, h_ref, n_ref):
        h = _dot(a_ref[...], w_ref[...]) + r_ref[...]
        h_ref[...] = h
        n_ref[...] = _rms(h, g_ref[...]).astype(n_ref.dtype)

    tile = pl.BlockSpec((tm, d), lambda i: (i, 0))
    return pl.pallas_call(
        body, name=name, grid=(m // tm,),
        in_specs=[pl.BlockSpec((tm, a.shape[1]), lambda i: (i, 0)), _full(w.shape), tile, _full(g.shape)],
        out_specs=[tile, tile],
        out_shape=[jax.ShapeDtypeStruct((m, d), f32), jax.ShapeDtypeStruct((m, d), bf16)],
        compiler_params=_params(("parallel",)),
    )(a, w, res, g)


def _residual_norm_bwd(d_list, w_list, h, g, dh_out, *, name):
    m, d = h.shape
    k = len(d_list)
    tm = _tile(m)

    def body(*refs):
        h_ref, g_ref, dho_ref, dh_ref, dg_ref = refs[2 * k:]
        dn = _dot(refs[0][...], refs[k][...])
        for i in range(1, k):
            dn = dn + _dot(refs[i][...], refs[k + i][...])
        _, vjp = jax.vjp(lambda hv, gv: (_rms(hv, gv), hv), h_ref[...], g_ref[...])
        dh, dg = vjp((dn, dho_ref[...]))
        dh_ref[...] = dh

        @pl.when(pl.program_id(0) == 0)
        def _():
            dg_ref[...] = jnp.zeros_like(dg_ref)

        dg_ref[...] += dg

    tile = pl.BlockSpec((tm, d), lambda i: (i, 0))
    return pl.pallas_call(
        body, name=name, grid=(m // tm,),
        in_specs=[pl.BlockSpec((tm, a.shape[1]), lambda i: (i, 0)) for a in d_list] + [_full(w.shape) for w in w_list]
        + [tile, _full(g.shape), tile],
        out_specs=[tile, _full(g.shape)],
        out_shape=[jax.ShapeDtypeStruct((m, d), f32), jax.ShapeDtypeStruct(g.shape, f32)],
        compiler_params=_params(("arbitrary",)),
    )(*d_list, *w_list, h, g, dh_out)


def _mm_tn(a, b, *, name, colsum=False, out_dtype=bf16, into=None):
    r, m = a.shape
    n = b.shape[1]
    tr = _tile(r, 1408)
    tmo = m
    for cand in (1408, 1024, 768, 512):
        if m > 1024 and m % cand == 0:
            tmo = cand
            break
    steps = r // tr

    rows, offset, target = into or (m, 0, None)

    def body(a_ref, b_ref, *rest):
        o_ref, rest = (rest[0], rest[1:]) if target is None else (rest[1], rest[2:])
        acc = rest[-1]
        i = pl.program_id(1)

        @pl.when(i == 0)
        def _():
            acc[...] = jnp.zeros_like(acc)
            if colsum:
                rest[0][...] = jnp.zeros_like(rest[0])

        acc[...] += _dot(a_ref[...], b_ref[...], "tn")
        if colsum:
            rest[0][...] += jnp.sum(a_ref[...].astype(f32), axis=0, keepdims=True)

        @pl.when(i == steps - 1)
        def _():
            o_ref[...] = acc[...].astype(out_dtype)

    out_shape = [jax.ShapeDtypeStruct((rows, n), out_dtype)]
    if offset % tmo == 0:
        out_specs = [pl.BlockSpec((tmo, n), lambda j, i: (offset // tmo + j, 0))]
    else:
        out_specs = [pl.BlockSpec((pl.Element(tmo), pl.Element(n)), lambda j, i: (pl.multiple_of(offset + j * tmo, math.gcd(offset, tmo)), 0))]
    if colsum:
        out_shape.append(jax.ShapeDtypeStruct((1, m), f32))
        out_specs.append(pl.BlockSpec((1, tmo), lambda j, i: (0, j)))
    in_specs = [pl.BlockSpec((tr, tmo), lambda j, i: (i, j)), pl.BlockSpec((tr, n), lambda j, i: (i, 0))]
    res = pl.pallas_call(
        body, name=name, grid=(m // tmo, steps),
        in_specs=in_specs + ([] if target is None else [pl.BlockSpec(memory_space=pl.ANY)]),
        out_specs=out_specs, out_shape=out_shape,
        scratch_shapes=[pltpu.VMEM((tmo, n), f32)],
        input_output_aliases={} if target is None else {2: 0},
        compiler_params=_params(("parallel", "arbitrary")),
    )(a, b, *([] if target is None else [target]))
    return res if colsum else res[0]


def _rowwise(fn, rows, params, outs, *, name, tm=None):
    m = rows[0].shape[0]
    tm = tm or _tile(m, MM_ROWS)
    nr, npar = len(rows), len(params)

    def body(*refs):
        vals = [r[...] for r in refs[:nr + npar]]
        res = fn(*vals)
        for o_ref, v in zip(refs[nr + npar:], res):
            o_ref[...] = v.astype(o_ref.dtype)

    return pl.pallas_call(
        body, name=name, grid=(m // tm,),
        in_specs=[pl.BlockSpec((tm, r.shape[1]), lambda i: (i, 0)) for r in rows] + [_full(p.shape) for p in params],
        out_specs=[pl.BlockSpec((tm, w), lambda i: (i, 0)) for w, _ in outs],
        out_shape=[jax.ShapeDtypeStruct((m, w), dt) for w, dt in outs],
        compiler_params=_params(("parallel",)),
    )(*rows, *params)


def _rowwise_bwd(fn, rows, params, cts, *, name, diff_rows, diff_params, tm=None, zero_rows_below=0, out_dtypes=None):
    m = rows[0].shape[0]
    tm = tm or _tile(m)
    nr, npar = len(rows), len(params)
    d_idx = [i for i in range(nr) if diff_rows[i]]
    p_idx = [i for i in range(npar) if diff_params[i]]
    out_dtypes = out_dtypes or [f32] * len(d_idx)
    flat_cts = [c for group in cts for c in group]
    n_ct = len(flat_cts)

    def body(*refs):
        vals = [r[...] for r in refs[:nr + npar]]
        ct_refs = refs[nr + npar:nr + npar + n_ct]
        out_refs = refs[nr + npar + n_ct:]
        ct_vals, k = [], 0
        for group in cts:
            acc = ct_refs[k][...].astype(f32)
            for extra in range(1, len(group)):
                acc = acc + ct_refs[k + extra][...].astype(f32)
            k += len(group)
            if zero_rows_below:
                rr = pl.program_id(0) * tm + lax.broadcasted_iota(jnp.int32, acc.shape, 0)
                acc = jnp.where(rr >= zero_rows_below, acc, 0.0)
            ct_vals.append(acc)

        def g(*dargs):
            full = list(vals)
            for pos, i in enumerate(d_idx):
                full[i] = dargs[pos]
            for pos, i in enumerate(p_idx):
                full[nr + i] = dargs[len(d_idx) + pos]
            return tuple(fn(*full))

        _, vjp = jax.vjp(g, *[vals[i].astype(f32) for i in d_idx], *[vals[nr + i] for i in p_idx])
        grads = vjp(tuple(ct_vals))
        for pos in range(len(d_idx)):
            out_refs[pos][...] = grads[pos].astype(out_refs[pos].dtype)
        first = pl.program_id(0) == 0
        for pos in range(len(p_idx)):
            o_ref = out_refs[len(d_idx) + pos]

            @pl.when(first)
            def _(o_ref=o_ref):
                o_ref[...] = jnp.zeros_like(o_ref)

            o_ref[...] += grads[len(d_idx) + pos]

    return pl.pallas_call(
        body, name=name, grid=(m // tm,),
        in_specs=[pl.BlockSpec((tm, r.shape[1]), lambda i: (i, 0)) for r in rows] + [_full(p.shape) for p in params]
        + [pl.BlockSpec((tm, c.shape[1]), lambda i: (i, 0)) for c in flat_cts],
        out_specs=[pl.BlockSpec((tm, rows[i].shape[1]), lambda i_: (i_, 0)) for i in d_idx]
        + [_full(params[i].shape) for i in p_idx],
        out_shape=[jax.ShapeDtypeStruct(rows[i].shape, dt) for i, dt in zip(d_idx, out_dtypes)]
        + [jax.ShapeDtypeStruct(params[i].shape, f32) for i in p_idx],
        compiler_params=_params(("arbitrary",)),
    )(*rows, *params, *flat_cts)


def _rms(x, g):
    return x * lax.rsqrt(jnp.mean(x * x, axis=-1, keepdims=True) + RMS_EPS) * g


def _head_sum_matrix(width, head):
    idx = jnp.arange(width) // head
    return (idx[:, None] == idx[None, :]).astype(f32)


def _rope_tables(lp):
    half = ROPE_DIM // 2
    pos = (np.arange(lp) - PAD).astype(np.float32)
    inv_freq = np.power(np.float32(ROPE_THETA), -np.arange(half, dtype=np.float32) * np.float32(2.0 / ROPE_DIM))
    ang = pos[:, None] * inv_freq[None, :].astype(np.float32)
    cos, sin = np.cos(ang), np.sin(ang)
    ones = np.ones((lp, HEAD_DIM - ROPE_DIM), np.float32)
    cos_t = np.concatenate([cos, cos, ones], axis=1)
    sin_t = np.concatenate([-sin, sin, 0.0 * ones], axis=1)
    i = np.arange(HEAD_DIM)
    src = np.where(i < half, i + half, np.where(i < ROPE_DIM, i - half, i))
    swap = ((i[:, None] == src[None, :]) & (i[None, :] < ROPE_DIM)).astype(np.float32)
    return jnp.asarray(cos_t, f32), jnp.asarray(sin_t, f32), jnp.asarray(swap, f32)


def _attn_prep(qkv, cos_t, sin_t, swap):
    outs = []
    for h in range(Q_HEADS + KV_HEADS):
        t = qkv[:, h * HEAD_DIM:(h + 1) * HEAD_DIM]
        outs.append(t * cos_t + _dot_const(t, swap) * sin_t)
    q = jnp.concatenate(outs[:Q_HEADS], axis=1)
    k = jnp.concatenate(outs[Q_HEADS:], axis=1)
    return q, k, qkv[:, Q_W + KV_W:]


def _attn_prep_transposed(dq, dk, dv, cos_t, sin_t, swap, dk_meta, dv_meta):
    first = pl.program_id(0) == 0

    def with_meta(d, d_meta):
        rest = jnp.zeros((d.shape[0] - BLOCK, KV_W), f32)
        return d.astype(f32) + jnp.where(first, jnp.concatenate([d_meta, rest], axis=0), 0.0)

    parts = []
    for d, heads in ((dq.astype(f32), Q_HEADS), (with_meta(dk, dk_meta), KV_HEADS)):
        for h in range(heads):
            t = d[:, h * HEAD_DIM:(h + 1) * HEAD_DIM]
            parts.append(t * cos_t + _two_pass(t * sin_t, swap, "nt"))
    return (jnp.concatenate(parts + [with_meta(dv, dv_meta)], axis=1),)


def _softplus(z):
    return jnp.maximum(z, 0.0) + jnp.log1p(jnp.exp(-jnp.abs(z)))


def _rwkv_prep(rkv, lora, w0, w2, a0, a2, g2, k_k, k_a, hsum):
    r = rkv[:, :RWKV_DIM]
    k = rkv[:, RWKV_DIM:2 * RWKV_DIM]
    v = rkv[:, 2 * RWKV_DIM:]
    dw = lora[:, :DECAY_LORA]
    da = lora[:, DECAY_LORA:DECAY_LORA + AAA_LORA]
    dg = lora[:, DECAY_LORA + AAA_LORA:]
    w = -_softplus(-(w0 + _dot(jnp.tanh(dw), w2))) - 0.5
    a = jax.nn.sigmoid(a0 + _dot(da, a2))
    g = _dot(jax.nn.sigmoid(dg), g2)
    kk = k * k_k
    kk = kk * lax.rsqrt(jnp.maximum(_dot_const(kk * kk, hsum), 1e-24))
    k = k * (1.0 + (a - 1.0) * k_a)
    log_decay = -jnp.exp(w)
    return r, log_decay, k, v, -kk, kk * a, g


def _rwkv_post(y, r, k, v, g, ln_w, ln_b, r_k, hmean):
    hsum = hmean * RWKV_HEAD
    mean = _dot_const(y, hmean)
    yc = y - mean
    var = _dot_const(yc * yc, hmean)
    yn = yc * lax.rsqrt(var + RWKV_LN_EPS) * ln_w + ln_b
    bonus = _dot_const(r * k * r_k, hsum) * v
    return ((yn + bonus) * g,)


def _merge(gates, br_a, br_r):
    sg = jax.nn.sigmoid(gates)
    return (sg[:, :D_MODEL] * br_a + sg[:, D_MODEL:] * br_r,)


def _swiglu(gate, up):
    return (jax.nn.silu(gate) * up,)


def _ffn_in(f, w_gate_t, w_up_t, *, name):
    m, d = f.shape
    n = w_gate_t.shape[0]
    tm = _tile(m)

    def body(f_ref, wg_ref, wu_ref, g_ref, u_ref, a_ref):
        g = _dot(f_ref[...], wg_ref[...], "nt")
        u = _dot(f_ref[...], wu_ref[...], "nt")
        g_ref[...] = g.astype(g_ref.dtype)
        u_ref[...] = u.astype(u_ref.dtype)
        a_ref[...] = _swiglu(g, u)[0].astype(a_ref.dtype)

    spec = pl.BlockSpec((tm, n), lambda i: (i, 0))
    return pl.pallas_call(
        body, name=name, grid=(m // tm,),
        in_specs=[pl.BlockSpec((tm, d), lambda i: (i, 0)), _full(w_gate_t.shape), _full(w_up_t.shape)],
        out_specs=[spec] * 3, out_shape=[jax.ShapeDtypeStruct((m, n), bf16)] * 3,
        compiler_params=_params(("parallel",)),
    )(f, w_gate_t, w_up_t)


def _branch_merge(y_attn, y_rwkv, w_attn_t, w_rwkv_t, gates, *, name):
    m = y_attn.shape[0]
    tm = _tile(m, MM_ROWS)

    def body(ya_ref, yr_ref, wa_ref, wr_ref, g_ref, a_ref, r_ref, o_ref):
        br_a = _dot(ya_ref[...], wa_ref[...], "nt")
        br_r = _dot(yr_ref[...], wr_ref[...], "nt")
        a_ref[...] = br_a.astype(a_ref.dtype)
        r_ref[...] = br_r.astype(r_ref.dtype)
        o_ref[...] = _merge(g_ref[...].astype(f32), br_a, br_r)[0].astype(o_ref.dtype)

    rows = lambda a: pl.BlockSpec((tm, a.shape[1]), lambda i: (i, 0))
    spec = pl.BlockSpec((tm, D_MODEL), lambda i: (i, 0))
    return pl.pallas_call(
        body, name=name, grid=(m // tm,),
        in_specs=[rows(y_attn), rows(y_rwkv), _full(w_attn_t.shape), _full(w_rwkv_t.shape), rows(gates)],
        out_specs=[spec] * 3, out_shape=[jax.ShapeDtypeStruct((m, D_MODEL), bf16)] * 3,
        compiler_params=_params(("parallel",)),
    )(y_attn, y_rwkv, w_attn_t, w_rwkv_t, gates)


def _branch_merge_bwd(dh, w_o, gates, br_a, br_r, w_attn_t, w_rwkv_t, *, name):
    m = dh.shape[0]
    tm = _tile(m, MM_ROWS)

    def body(dh_ref, w_ref, g_ref, a_ref, r_ref, wa_ref, wr_ref, dg_ref, da_ref, dr_ref, dya_ref, dyr_ref):
        dmerged = _dot(dh_ref[...], w_ref[...], "nt")
        _, vjp = jax.vjp(lambda g, a, r: _merge(g, a, r)[0], g_ref[...].astype(f32), a_ref[...].astype(f32),
                         r_ref[...].astype(f32))
        dg, da, dr = vjp(dmerged)
        dg_ref[...] = dg.astype(dg_ref.dtype)
        da_ref[...] = da.astype(da_ref.dtype)
        dr_ref[...] = dr.astype(dr_ref.dtype)
        dya_ref[...] = _dot(da, wa_ref[...])
        dyr_ref[...] = _dot(dr, wr_ref[...])

    rows = lambda a: pl.BlockSpec((tm, a.shape[1]), lambda i: (i, 0))
    mixer = pl.BlockSpec((tm, w_attn_t.shape[1]), lambda i: (i, 0))
    return pl.pallas_call(
        body, name=name, grid=(m // tm,),
        in_specs=[rows(dh), _full(w_o.shape), rows(gates), rows(br_a), rows(br_r), _full(w_attn_t.shape),
                  _full(w_rwkv_t.shape)],
        out_specs=[rows(gates), rows(br_a), rows(br_r), mixer, mixer],
        out_shape=[jax.ShapeDtypeStruct(gates.shape, bf16), jax.ShapeDtypeStruct(br_a.shape, bf16),
                   jax.ShapeDtypeStruct(br_r.shape, bf16), jax.ShapeDtypeStruct((m, w_attn_t.shape[1]), f32),
                   jax.ShapeDtypeStruct((m, w_rwkv_t.shape[1]), f32)],
        compiler_params=_params(("parallel",)),
    )(dh, w_o, gates, br_a, br_r, w_attn_t, w_rwkv_t)


def _ffn_in_bwd(dh, w_down, gate, up, *, name):
    m, d = dh.shape
    n = w_down.shape[0]
    tm = _tile(m)

    def body(dh_ref, w_ref, g_ref, u_ref, dg_ref, du_ref):
        dact = _dot(dh_ref[...], w_ref[...], "nt")
        _, vjp = jax.vjp(lambda a, b: _swiglu(a, b)[0], g_ref[...].astype(f32), u_ref[...].astype(f32))
        dg, du = vjp(dact)
        dg_ref[...] = dg.astype(dg_ref.dtype)
        du_ref[...] = du.astype(du_ref.dtype)

    spec = pl.BlockSpec((tm, n), lambda i: (i, 0))
    return pl.pallas_call(
        body, name=name, grid=(m // tm,),
        in_specs=[pl.BlockSpec((tm, d), lambda i: (i, 0)), _full(w_down.shape), spec, spec],
        out_specs=[spec] * 2, out_shape=[jax.ShapeDtypeStruct((m, n), bf16)] * 2,
        compiler_params=_params(("parallel",)),
    )(dh, w_down, gate, up)


def _ffn_bwd(dh, w_down, gate, up, w_gate_t, w_up_t, h, g, *, name):
    m, d = dh.shape
    n = w_down.shape[0]
    tm = _tile(m)
    halves = [(c * (n // 2), (c + 1) * (n // 2)) for c in range(2)]

    def body(dh_ref, wd_ref, g_ref, u_ref, wg_ref, wu_ref, h_ref, gn_ref, dg_ref, du_ref, dh1_ref, dgn_ref):
        dn = jnp.zeros((tm, d), f32)
        for lo, hi in halves:
            dact = _dot(dh_ref[...], wd_ref[lo:hi, :], "nt")
            _, vjp = jax.vjp(lambda a, b: _swiglu(a, b)[0], g_ref[:, lo:hi].astype(f32), u_ref[:, lo:hi].astype(f32))
            dg, du = (t.astype(bf16) for t in vjp(dact))
            dg_ref[:, lo:hi] = dg
            du_ref[:, lo:hi] = du
            dn = dn + _dot(dg, wg_ref[lo:hi, :]) + _dot(du, wu_ref[lo:hi, :])
        _, vjp = jax.vjp(lambda hv, gv: (_rms(hv, gv), hv), h_ref[...], gn_ref[...])
        dh1, dgn = vjp((dn, dh_ref[...]))
        dh1_ref[...] = dh1

        @pl.when(pl.program_id(0) == 0)
        def _():
            dgn_ref[...] = jnp.zeros_like(dgn_ref)

        dgn_ref[...] += dgn

    wide = pl.BlockSpec((tm, n), lambda i: (i, 0))
    tile = pl.BlockSpec((tm, d), lambda i: (i, 0))
    return pl.pallas_call(
        body, name=name, grid=(m // tm,),
        in_specs=[tile, _full(w_down.shape), wide, wide, _full(w_gate_t.shape), _full(w_up_t.shape), tile,
                  _full(g.shape)],
        out_specs=[wide, wide, tile, _full(g.shape)],
        out_shape=[jax.ShapeDtypeStruct((m, n), bf16)] * 2 + [jax.ShapeDtypeStruct((m, d), f32),
                                                               jax.ShapeDtypeStruct(g.shape, f32)],
        compiler_params=_params(("arbitrary",)),
    )(dh, w_down, gate, up, w_gate_t, w_up_t, h, g)


HALO = 16


def _previous_rows(x, before_ref, first_tile):
    rows = lax.broadcasted_iota(jnp.int32, x.shape, 0)
    last = jnp.where(first_tile, 0.0, before_ref[HALO - 1:HALO, :].astype(f32))
    return jnp.where(rows == 0, last, pltpu.roll(x, 1, axis=0))


def _mixer_inputs(ps, mixes, params, *, name):
    m = ps[0].shape[0]
    tm = _tile(m)
    sub = tm // HALO
    n_par = len(params)

    def body(*refs):
        first = pl.program_id(0) == 0
        pf = []
        for k in range(2):
            x = refs[k][...].astype(f32)
            pf.append(x + (_previous_rows(x, refs[2 + k], first) - x) * refs[4 + k][...])
        res = _rwkv_prep(*pf, *[ref[...] for ref in refs[6:6 + n_par]])
        for o_ref, val in zip(refs[6 + n_par:], res):
            o_ref[...] = val

    tile = lambda a: pl.BlockSpec((tm, a.shape[1]), lambda i: (i, 0))
    before = lambda a: pl.BlockSpec((HALO, a.shape[1]), lambda i: (jnp.maximum(i * sub - 1, 0), 0))
    out = pl.BlockSpec((tm, RWKV_DIM), lambda i: (i, 0))
    return pl.pallas_call(
        body, name=name, grid=(m // tm,),
        in_specs=[tile(a) for a in ps] + [before(a) for a in ps] + [_full(a.shape) for a in mixes + params],
        out_specs=[out] * 7, out_shape=[jax.ShapeDtypeStruct((m, RWKV_DIM), f32)] * 7,
        compiler_params=_params(("parallel",)),
    )(*ps, *ps, *mixes, *params)


def _mixer_inputs_bwd(ps, mixes, params, cts, *, name):
    m = ps[0].shape[0]
    tm = _tile(m)
    sub = tm // HALO
    nt = m // tm
    n_par = len(params)
    flat_cts = [c for group in cts for c in group]
    n_ct = len(flat_cts)

    def body(*refs):
        i = pl.program_id(0)
        tile_index = nt - 1 - i
        ct_refs = refs[6 + n_par:6 + n_par + n_ct]
        dp_refs = refs[6 + n_par + n_ct:8 + n_par + n_ct]
        dmix_refs = refs[8 + n_par + n_ct:10 + n_par + n_ct]
        dpar_refs = refs[10 + n_par + n_ct:9 + 2 * n_par + n_ct]
        carries = refs[9 + 2 * n_par + n_ct:]
        rows1 = tile_index * tm + lax.broadcasted_iota(jnp.int32, (tm, 1), 0)
        live = rows1 >= PAD

        @pl.when(i == 0)
        def _():
            for ref in (*dmix_refs, *dpar_refs, *carries):
                ref[...] = jnp.zeros_like(ref)

        xs, prevs, pf = [], [], []
        for k in range(2):
            x = refs[k][...].astype(f32)
            xp = _previous_rows(x, refs[2 + k], tile_index == 0)
            xs.append(x)
            prevs.append(xp)
            pf.append(x + (xp - x) * refs[4 + k][...])
        ct_vals, pos = [], 0
        for group in cts:
            acc = ct_refs[pos][...].astype(f32)
            for extra in range(1, len(group)):
                acc = acc + ct_refs[pos + extra][...].astype(f32)
            pos += len(group)
            ct_vals.append(jnp.where(live, acc, 0.0))
        par_vals = [ref[...] for ref in refs[6:6 + n_par]]
        _, vjp = jax.vjp(lambda *args: _rwkv_prep(*args, par_vals[-1]), *pf, *par_vals[:-1])
        g = vjp(tuple(ct_vals))
        for k in range(2):
            dpf = g[k]
            mixv = refs[4 + k][...]
            dm = dpf * mixv
            rows = lax.broadcasted_iota(jnp.int32, dm.shape, 0)
            dm_next = jnp.where(rows == tm - 1, carries[k][...], pltpu.roll(dm, tm - 1, axis=0))
            dp_refs[k][...] = jnp.where(live, dpf - dm + dm_next, 0.0).astype(dp_refs[k].dtype)
            carries[k][...] = dm[0:1, :]
            dmix_refs[k][...] += jnp.sum(dpf * (prevs[k] - xs[k]), axis=0, keepdims=True)
        for ref, val in zip(dpar_refs, g[2:]):
            ref[...] += val

    tile = lambda a: pl.BlockSpec((tm, a.shape[1]), lambda i: (nt - 1 - i, 0))
    before = lambda a: pl.BlockSpec((HALO, a.shape[1]), lambda i: (jnp.maximum((nt - 1 - i) * sub - 1, 0), 0))
    return pl.pallas_call(
        body, name=name, grid=(nt,),
        in_specs=[tile(a) for a in ps] + [before(a) for a in ps] + [_full(a.shape) for a in mixes + params]
        + [tile(c) for c in flat_cts],
        out_specs=[tile(a) for a in ps] + [_full(a.shape) for a in mixes + params[:-1]],
        out_shape=[jax.ShapeDtypeStruct(a.shape, bf16) for a in ps]
        + [jax.ShapeDtypeStruct(a.shape, f32) for a in mixes + params[:-1]],
        scratch_shapes=[pltpu.VMEM((1, a.shape[1]), f32) for a in ps],
        compiler_params=_params(("arbitrary",)),
    )(*ps, *ps, *mixes, *params, *flat_cts)


def _attn_masks(blk):
    qi = lax.broadcasted_iota(jnp.int32, (BLOCK, BLOCK), 0)
    ki = lax.broadcasted_iota(jnp.int32, (BLOCK, BLOCK), 1)
    qpos = blk * BLOCK + qi - PAD
    kpos_c = blk * BLOCK + ki - PAD
    kpos_p = kpos_c - BLOCK
    kpos_m = ki - PAD

    def band(kpos):
        return (kpos >= N_META) & (kpos <= qpos) & (qpos - kpos < WINDOW)

    return band(kpos_p), band(kpos_c), (kpos_m >= 0) & (kpos_m <= qpos)


def _attn_probs(qs, k3s, sink, oks):
    s = [[jnp.where(ok, _dot(qh, kx, "nt"), NEG_INF) for kx, ok in zip(k3, oks)] for qh, k3 in zip(qs, k3s)]
    mx = [jnp.maximum(jnp.maximum(jnp.max(t[0], -1, keepdims=True), jnp.max(t[1], -1, keepdims=True)),
                      jnp.maximum(jnp.max(t[2], -1, keepdims=True), sk)) for t, sk in zip(s, sink)]
    e = [[jnp.exp(tx - m) for tx in t] for t, m in zip(s, mx)]
    e_sink = [jnp.exp(sk - m) for sk, m in zip(sink, mx)]
    inv = [1.0 / (jnp.sum(t[0], -1, keepdims=True) + jnp.sum(t[1], -1, keepdims=True)
                  + jnp.sum(t[2], -1, keepdims=True) + es) for t, es in zip(e, e_sink)]
    return [[tx * i for tx in t] for t, i in zip(e, inv)], [es * i for es, i in zip(e_sink, inv)]


def _head_cols(i):
    return slice(i * HEAD_DIM, (i + 1) * HEAD_DIM)


def _attn_operands(refs):
    q_ref, kp_ref, kc_ref, km_ref, vp_ref, vc_ref, vm_ref, s_ref = refs
    qs = [q_ref[:, _head_cols(i)] * (HEAD_DIM ** -0.5) for i in range(Q_HEADS)]
    k3 = [[ref[:, _head_cols(h)] for ref in (kp_ref, kc_ref, km_ref)] for h in range(KV_HEADS)]
    v3 = [[ref[:, _head_cols(h)] for ref in (vp_ref, vc_ref, vm_ref)] for h in range(KV_HEADS)]
    return (qs, [k3[i // GROUP] for i in range(Q_HEADS)], [v3[i // GROUP] for i in range(Q_HEADS)],
            [s_ref[:, i:i + 1] for i in range(Q_HEADS)])


def _attention(q, k, v, sinks, *, name):
    lp = q.shape[0]
    nb = lp // BLOCK
    prev = lambda i: (jnp.maximum(i - 1, 0), 0)
    cur = lambda i: (i, 0)
    meta = lambda i: (0, 0)
    kv = lambda index: pl.BlockSpec((BLOCK, KV_W), index)

    def body(*refs):
        o_ref = refs[-1]
        qs, k3s, v3s, sink = _attn_operands(refs[:-1])
        p, _ = _attn_probs(qs, k3s, sink, _attn_masks(pl.program_id(0)))
        out = [_dot(ph[0], v3[0]) + _dot(ph[1], v3[1]) + _dot(ph[2], v3[2]) for ph, v3 in zip(p, v3s)]
        for i in range(Q_HEADS):
            o_ref[:, _head_cols(i)] = out[i].astype(o_ref.dtype)

    return pl.pallas_call(
        body, name=name, grid=(nb,),
        in_specs=[pl.BlockSpec((BLOCK, Q_W), cur), kv(prev), kv(cur), kv(meta), kv(prev), kv(cur), kv(meta),
                  _full((1, Q_HEADS))],
        out_specs=pl.BlockSpec((BLOCK, Q_W), cur),
        out_shape=jax.ShapeDtypeStruct((lp, Q_W), bf16),
        compiler_params=_params(("parallel",)),
    )(q, k, k, k, v, v, v, sinks)


def _attention_bwd(q, k, v, sinks, out, do, *, name):
    lp = q.shape[0]
    nb = lp // BLOCK
    cur = lambda n: (jnp.minimum(n, nb - 1), 0)
    prev = lambda n: (jnp.maximum(jnp.minimum(n, nb - 1) - 1, 0), 0)
    behind = lambda n: (jnp.maximum(n - 1, 0), 0)
    meta = lambda n: (0, 0)
    kv = lambda index: pl.BlockSpec((BLOCK, KV_W), index)
    scale = HEAD_DIM ** -0.5

    def body(*refs):
        ins, fwd_ref, do_ref = refs[:8], refs[8], refs[9]
        dq_ref, dk_ref, dv_ref, dkm_ref, dvm_ref, ds_ref, carry_k, carry_v = refs[10:]
        n = pl.program_id(0)

        @pl.when(n == 0)
        def _():
            for ref in (dkm_ref, dvm_ref, ds_ref, carry_k, carry_v):
                ref[...] = jnp.zeros_like(ref)

        @pl.when(n < nb)
        def _():
            qs, k3s, v3s, sink = _attn_operands(ins)
            do = [do_ref[:, _head_cols(i)] for i in range(Q_HEADS)]
            p, p_sink = _attn_probs(qs, k3s, sink, _attn_masks(n))
            delta = [jnp.sum(d * fwd_ref[:, _head_cols(i)].astype(f32), -1, keepdims=True) for i, d in enumerate(do)]
            dp = [[_dot(d, vx, "nt") for vx in v3] for d, v3 in zip(do, v3s)]
            ds = [[px * (dx - dl) for px, dx in zip(ph, dh)] for ph, dh, dl in zip(p, dp, delta)]
            dq = [_dot(dsh[0], k3[0]) + _dot(dsh[1], k3[1]) + _dot(dsh[2], k3[2]) for dsh, k3 in zip(ds, k3s)]
            for i in range(Q_HEADS):
                dq_ref[:, _head_cols(i)] = dq[i] * scale
                ds_ref[:, i:i + 1] -= jnp.sum(p_sink[i] * delta[i], axis=0, keepdims=True)
            for h in range(KV_HEADS):
                group = slice(h * GROUP, (h + 1) * GROUP)
                q_all = jnp.concatenate(qs[group], axis=0)
                do_all = jnp.concatenate(do[group], axis=0)
                dk3 = [_dot(jnp.concatenate([dsh[x] for dsh in ds[group]], axis=0), q_all, "tn") for x in range(3)]
                dv3 = [_dot(jnp.concatenate([ph[x] for ph in p[group]], axis=0), do_all, "tn") for x in range(3)]
                hs = _head_cols(h)
                for out_ref, carry, meta_ref, d3 in ((dk_ref, carry_k, dkm_ref, dk3),
                                                     (dv_ref, carry_v, dvm_ref, dv3)):
                    out_ref[:, hs] = carry[:, hs] + d3[0]
                    carry[:, hs] = d3[1]
                    meta_ref[:, hs] += d3[2]

        @pl.when(n == nb)
        def _():
            dk_ref[...] = carry_k[...]
            dv_ref[...] = carry_v[...]

    kv_shape = jax.ShapeDtypeStruct((lp, KV_W), f32)
    one_shape = jax.ShapeDtypeStruct((BLOCK, KV_W), f32)
    return pl.pallas_call(
        body, name=name, grid=(nb + 1,),
        in_specs=[pl.BlockSpec((BLOCK, Q_W), cur), kv(prev), kv(cur), kv(meta), kv(prev), kv(cur), kv(meta),
                  _full((1, Q_HEADS)), pl.BlockSpec((BLOCK, Q_W), cur), pl.BlockSpec((BLOCK, Q_W), cur)],
        out_specs=[pl.BlockSpec((BLOCK, Q_W), cur), kv(behind), kv(behind), kv(meta), kv(meta),
                   _full((1, Q_HEADS))],
        out_shape=[jax.ShapeDtypeStruct((lp, Q_W), f32), kv_shape, kv_shape, one_shape, one_shape,
                   jax.ShapeDtypeStruct((1, Q_HEADS), f32)],
        scratch_shapes=[pltpu.VMEM((BLOCK, KV_W), f32), pltpu.VMEM((BLOCK, KV_W), f32)],
        compiler_params=_params(("arbitrary",)),
    )(q, k, k, k, v, v, v, sinks, out, do)


@jax.custom_vjp
def _known_inverse(l, x):
    return x


def _known_inverse_fwd(l, x):
    return x, x


def _known_inverse_bwd(x, ct):
    return _dot(_dot(x, ct, "tn"), x, "nt"), jnp.zeros_like(x)


_known_inverse.defvjp(_known_inverse_fwd, _known_inverse_bwd)


@jax.custom_vjp
def _decayed(x, c):
    return (x * jnp.exp(c)).astype(bf16).astype(f32)


def _decayed_fwd(x, c):
    e = jnp.exp(c)
    out = (x * e).astype(bf16).astype(f32)
    return out, (e, out)


def _decayed_bwd(res, ct):
    e, out = res
    return ct * e, ct * out


_decayed.defvjp(_decayed_fwd, _decayed_bwd)


@jax.custom_vjp
def _pair(x, y):
    return _dot(x, y, "nt")


def _pair_fwd(x, y):
    return _dot(x, y, "nt"), (x, y)


def _pair_bwd(res, ct):
    x, y = res
    hi = ct.astype(bf16)
    lo = (ct - hi.astype(f32)).astype(bf16)
    return _dot(hi, y) + _dot(lo, y), _dot(hi, x, "tn") + _dot(lo, x, "tn")


_pair.defvjp(_pair_fwd, _pair_bwd)


def _scan_chunk(s0, r, lw, k, v, a, b, inv=None):
    t = r[0].shape[0]
    ii = lax.broadcasted_iota(jnp.int32, (t, t), 0)
    jj = lax.broadcasted_iota(jnp.int32, (t, t), 1)
    incl = jj <= ii
    strict = jj < ii
    tri = incl.astype(f32)
    eye = jnp.where(ii == jj, 1.0, 0.0)
    cl = [_const_dot(tri, x) for x in lw]
    mid = [c[t // 2 - 1:t // 2, :] for c in cl]
    s0 = [s * jnp.exp(m) for s, m in zip(s0, mid)]
    cl = [c - m for c, m in zip(cl, mid)]
    rt = [_decayed(x, c) for x, c in zip(r, cl)]
    at = [_decayed(x, c - l) for x, c, l in zip(a, cl, lw)]
    bt = [_decayed(x, -c) for x, c in zip(b, cl)]
    kt = [_decayed(x, -c) for x, c in zip(k, cl)]
    l_ab = [jnp.where(strict, _pair(x, y), 0.0) for x, y in zip(at, bt)]
    l_ak = [jnp.where(strict, _pair(x, y), 0.0) for x, y in zip(at, kt)]
    r_b = [jnp.where(incl, _pair(x, y), 0.0) for x, y in zip(rt, bt)]
    r_k = [jnp.where(incl, _pair(x, y), 0.0) for x, y in zip(rt, kt)]
    if inv is None:
        inv = [eye + x for x in l_ab]
        pw = l_ab
        for _ in range(int(math.log2(t)) - 1):
            pw = [_dot(x, x) for x in pw]
            inv = [x + _dot(x, y) for x, y in zip(inv, pw)]
    else:
        inv = [_known_inverse(x, y) for x, y in zip(l_ab, inv)]
    rhs = [_dot(x, s, "nt") + _dot(m, y) for x, s, m, y in zip(at, s0, l_ak, v)]
    u = [_dot(x, y) for x, y in zip(inv, rhs)]
    y_s = [_dot(x, s, "nt") for x, s in zip(rt, s0)]
    y = [ys + _dot(m, uu) + _dot(n, vv) for ys, m, uu, n, vv in zip(y_s, r_b, u, r_k, v)]
    grow = [s + _dot(uu, x, "tn") + _dot(vv, z, "tn") for s, uu, x, vv, z in zip(s0, u, bt, v, kt)]
    s1 = [g * jnp.exp(c[t - 1:t, :]) for g, c in zip(grow, cl)]
    return y, s1, inv


def _head_rows(h):
    return slice(h * RWKV_HEAD, (h + 1) * RWKV_HEAD)


def _per_head(ref):
    return [ref[:, _head_rows(h)] for h in range(RWKV_HEADS)]


def _scan(r, lw, k, v, a, b, *, name):
    lp = r.shape[0]
    nc = lp // CHUNK
    row = pl.BlockSpec((CHUNK, RWKV_DIM), lambda c: (c, 0))

    def body(r_ref, lw_ref, k_ref, v_ref, a_ref, b_ref, y_ref, s_ref, inv_ref, state):
        @pl.when(pl.program_id(0) == 0)
        def _():
            state[...] = jnp.zeros_like(state)

        s_ref[...] = state[...]
        s0 = [state[_head_rows(h), :] for h in range(RWKV_HEADS)]
        y, s1, inv = _scan_chunk(s0, *[_per_head(ref) for ref in (r_ref, lw_ref, k_ref, v_ref, a_ref, b_ref)])
        for h in range(RWKV_HEADS):
            y_ref[:, _head_rows(h)] = y[h]
            state[_head_rows(h), :] = s1[h]
            inv_ref[h * CHUNK:(h + 1) * CHUNK, :] = inv[h].astype(inv_ref.dtype)

    return pl.pallas_call(
        body, name=name, grid=(nc,), in_specs=[row] * 6,
        out_specs=[row, pl.BlockSpec((RWKV_DIM, RWKV_HEAD), lambda c: (c, 0)),
                   pl.BlockSpec((RWKV_HEADS * CHUNK, CHUNK), lambda c: (c, 0))],
        out_shape=[jax.ShapeDtypeStruct((lp, RWKV_DIM), f32), jax.ShapeDtypeStruct((nc * RWKV_DIM, RWKV_HEAD), f32),
                   jax.ShapeDtypeStruct((nc * RWKV_HEADS * CHUNK, CHUNK), bf16)],
        scratch_shapes=[pltpu.VMEM((RWKV_DIM, RWKV_HEAD), f32)],
        compiler_params=_params(("arbitrary",)),
    )(r, lw, k, v, a, b)


def _scan_bwd(r, lw, k, v, a, b, states, inverses, dy, *, name):
    lp = r.shape[0]
    nc = lp // CHUNK
    back = lambda c: (nc - 1 - c, 0)
    row = pl.BlockSpec((CHUNK, RWKV_DIM), back)

    def body(r_ref, lw_ref, k_ref, v_ref, a_ref, b_ref, s_ref, inv_ref, dy_ref,
             dr_ref, dlw_ref, dk_ref, dv_ref, da_ref, db_ref, dstate):
        @pl.when(pl.program_id(0) == 0)
        def _():
            dstate[...] = jnp.zeros_like(dstate)

        outs = (dr_ref, dlw_ref, dk_ref, dv_ref, da_ref, db_ref)
        s0 = [s_ref[_head_rows(h), :] for h in range(RWKV_HEADS)]
        inv = [inv_ref[h * CHUNK:(h + 1) * CHUNK, :].astype(f32) for h in range(RWKV_HEADS)]
        _, vjp = jax.vjp(lambda *args: _scan_chunk(*args, inv=inv)[:2], s0,
                         *[_per_head(ref) for ref in (r_ref, lw_ref, k_ref, v_ref, a_ref, b_ref)])
        g = vjp((_per_head(dy_ref), [dstate[_head_rows(h), :] for h in range(RWKV_HEADS)]))
        for h in range(RWKV_HEADS):
            dstate[_head_rows(h), :] = g[0][h]
            for o_ref, gv in zip(outs, g[1:]):
                o_ref[:, _head_rows(h)] = gv[h]

    shape = jax.ShapeDtypeStruct((lp, RWKV_DIM), f32)
    return pl.pallas_call(
        body, name=name, grid=(nc,),
        in_specs=[row] * 6 + [pl.BlockSpec((RWKV_DIM, RWKV_HEAD), back),
                              pl.BlockSpec((RWKV_HEADS * CHUNK, CHUNK), back), row],
        out_specs=[row] * 6, out_shape=[shape] * 6,
        scratch_shapes=[pltpu.VMEM((RWKV_DIM, RWKV_HEAD), f32)],
        compiler_params=_params(("arbitrary",)),
    )(r, lw, k, v, a, b, states, inverses, dy)


def _loss_head(act, w_down, h1, target, g_final, *, name):
    lp = h1.shape[0]
    per_tile = 3
    tm = per_tile * BLOCK
    last_block = (lp - FRONT) // BLOCK - 1

    def body(a_ref, w_ref, h_ref, t0_ref, t1_ref, t2_ref, g_ref, loss_ref, dh_ref, dg_ref):
        i = pl.program_id(0)
        loss, dg = 0.0, jnp.zeros(dg_ref.shape, f32)
        for j, t_ref in enumerate((t0_ref, t1_ref, t2_ref)):
            rows = slice(j * BLOCK, (j + 1) * BLOCK)
            real = i * tm + j * BLOCK + lax.broadcasted_iota(jnp.int32, (BLOCK, 1), 0) >= FRONT

            def block_loss(hv, gv, t_ref=t_ref, real=real):
                err = _rms(hv, gv) - t_ref[...]
                return 0.5 * jnp.sum(jnp.where(real, jnp.mean(err * err, axis=-1, keepdims=True), 0.0))

            h2 = _dot(a_ref[rows, :], w_ref[...], "nn") + h_ref[rows, :]
            loss_j, (dh, dg_j) = jax.value_and_grad(block_loss, argnums=(0, 1))(h2, g_ref[...])
            dh_ref[rows, :] = dh
            loss, dg = loss + loss_j, dg + dg_j

        @pl.when(i == 0)
        def _():
            loss_ref[...] = jnp.zeros_like(loss_ref)
            dg_ref[...] = jnp.zeros_like(dg_ref)

        loss_ref[...] += jnp.full(loss_ref.shape, loss, f32)
        dg_ref[...] += dg

    def target_block(j):
        return pl.BlockSpec((BLOCK, D_MODEL),
                            lambda i: (jnp.clip(per_tile * i + j - FRONT // BLOCK, 0, last_block), 0))

    return pl.pallas_call(
        body, name=name, grid=(lp // tm,),
        in_specs=[pl.BlockSpec((tm, act.shape[1]), lambda i: (i, 0)), _full(w_down.shape),
                  pl.BlockSpec((tm, D_MODEL), lambda i: (i, 0)), target_block(0), target_block(1), target_block(2),
                  _full(g_final.shape)],
        out_specs=[_full((8, 128)), pl.BlockSpec((tm, D_MODEL), lambda i: (i, 0)), _full(g_final.shape)],
        out_shape=[jax.ShapeDtypeStruct((8, 128), f32), jax.ShapeDtypeStruct((lp, D_MODEL), f32),
                   jax.ShapeDtypeStruct(g_final.shape, f32)],
        compiler_params=_params(("arbitrary",)),
    )(act, w_down, h1, target, target, target, g_final)


def _embed_norm(x, meta, g, *, name):
    seq = x.shape[0]
    lp = seq + FRONT
    per_tile = 3
    tm = per_tile * BLOCK
    last_block = seq // BLOCK - 1

    def body(x0_ref, x1_ref, x2_ref, meta_ref, g_ref, h_ref, u_ref):
        front = jnp.concatenate([jnp.zeros((PAD, D_MODEL), f32), meta_ref[...]], axis=0)
        first = jnp.where(pl.program_id(0) == 0, front, x0_ref[...])
        h = jnp.concatenate([first, x1_ref[...], x2_ref[...]], axis=0)
        h_ref[...] = h
        u_ref[...] = _rms(h, g_ref[...]).astype(u_ref.dtype)

    def x_block(j):
        return pl.BlockSpec((BLOCK, D_MODEL),
                            lambda i: (jnp.clip(per_tile * i + j - FRONT // BLOCK, 0, last_block), 0))

    tile = pl.BlockSpec((tm, D_MODEL), lambda i: (i, 0))
    return pl.pallas_call(
        body, name=name, grid=(lp // tm,),
        in_specs=[x_block(0), x_block(1), x_block(2), _full(meta.shape), _full(g.shape)],
        out_specs=[tile, tile],
        out_shape=[jax.ShapeDtypeStruct((lp, D_MODEL), f32), jax.ShapeDtypeStruct((lp, D_MODEL), bf16)],
        compiler_params=_params(("parallel",)),
    )(x, x, x, meta, g)


def _input_norm_bwd(h0, g, du, dh1, *, name):
    lp = h0.shape[0]
    blocks = (lp - FRONT) // FRONT
    per_tile = max(n for n in (4, 3, 2, 1) if blocks % n == 0)
    ins = (h0, du, dh1)

    def body(*refs):
        tiles = [refs[k * per_tile:(k + 1) * per_tile] for k in range(len(ins))]
        front_refs = refs[len(ins) * per_tile:len(ins) * (per_tile + 1)]
        g_ref, dx_ref, front_ref, dg_ref = refs[len(ins) * (per_tile + 1):]

        def cotangents(h_ref, du_ref, dh1_ref):
            _, vjp = jax.vjp(lambda hv, gv: (_rms(hv, gv), hv), h_ref[...], g_ref[...])
            return vjp((du_ref[...].astype(f32), dh1_ref[...]))

        @pl.when(pl.program_id(0) == 0)
        def _():
            front_ref[...], dg_ref[...] = cotangents(*front_refs)

        dg = jnp.zeros(dg_ref.shape, f32)
        for j in range(per_tile):
            dh, dg_j = cotangents(*(t[j] for t in tiles))
            dx_ref[j * FRONT:(j + 1) * FRONT, :] = dh
            dg = dg + dg_j
        dg_ref[...] += dg

    def block(j):
        return pl.BlockSpec((FRONT, D_MODEL), lambda i: (per_tile * i + j + 1, 0))

    first = pl.BlockSpec((FRONT, D_MODEL), lambda i: (0, 0))
    return pl.pallas_call(
        body, name=name, grid=(blocks // per_tile,),
        in_specs=[block(j) for _ in ins for j in range(per_tile)] + [first] * len(ins) + [_full(g.shape)],
        out_specs=[pl.BlockSpec((per_tile * FRONT, D_MODEL), lambda i: (i, 0)), _full((FRONT, D_MODEL)),
                   _full(g.shape)],
        out_shape=[jax.ShapeDtypeStruct((lp - FRONT, D_MODEL), f32), jax.ShapeDtypeStruct((FRONT, D_MODEL), f32),
                   jax.ShapeDtypeStruct(g.shape, f32)],
        compiler_params=_params(("arbitrary",)),
    )(*(a for a in ins for _ in range(per_tile)), *ins, g)


def _local_step(x, target, meta, p, early_weights=None, late_weights=None, emit=None):
    emit = emit or (lambda group, grads: 0.0)
    seq = x.shape[0]
    lp = seq + FRONT
    cos_t, sin_t, swap = _rope_tables(lp)
    hsum = _head_sum_matrix(RWKV_DIM, RWKV_HEAD)
    hmean = hsum / RWKV_HEAD
    post_params = [p["ln_w"], p["ln_b"], p["r_k"], hmean]

    h0, u = _embed_norm(x, meta, p["norm_mix_g"], name="norm_mix")
    if early_weights is not None:
        p = {**p, **early_weights(u)}
    prep_params = [p["w0"], p["w2"], p["a0"], p["a2"], p["g2"], p["k_k"], p["k_a"], hsum]
    in_widths = [ATTN_PROJ, RKV_W, LORA_W, 2 * D_MODEL]
    q, k, v, p_rkv, p_lora, gates = _proj_in(u, p["w_in_lr"], p["b_in"], in_widths,
                                             (cos_t, sin_t, swap), name="proj_in", zero_rows_below=PAD)
    y_attn = _attention(q, k, v, p["sinks"], name="attention")

    mix_rkv, mix_lora = p["mix"][:, :RKV_W], p["mix"][:, RKV_W:]
    r_, lw_, k_, v_, a_, b_, g_ = _mixer_inputs([p_rkv, p_lora], [mix_rkv, mix_lora], prep_params,
                                                name="mixer_inputs")
    y_scan, states, inverses = _scan(r_, lw_, k_, v_, a_, b_, name="wkv_scan")
    (y_rwkv,) = _rowwise(_rwkv_post, [y_scan, r_, k_, v_, g_], post_params, [(RWKV_DIM, bf16)], name="rwkv_post")

    if late_weights is not None:
        p = {**p, **late_weights(y_rwkv)}
    br_a, br_r, merged = _branch_merge(y_attn, y_rwkv, p["w_br_attn_t"], p["w_br_rwkv_t"], gates, name="branch_merge")
    h1, f = _residual_norm(merged, p["w_o"], h0, p["norm_ffn_g"], name="out_proj")
    gate, up, act = _ffn_in(f, p["w_gate_t"], p["w_up_t"], name="ffn_in")

    loss8, dh2, d_final_g = _loss_head(act, p["w_down"], h1, target, p["norm_final_g"], name="loss_head")
    dgate, dup, dh1, d_ffn_g = _ffn_bwd(dh2, p["w_down"], gate, up, p["w_gate_t"], p["w_up_t"], h1, p["norm_ffn_g"],
                                        name="ffn_bwd")
    d_w_down = _mm_tn(act, dh2, name="dw_down")
    d_w_gate_t = _mm_tn(dgate, f, name="dw_gate")
    d_w_up_t = _mm_tn(dup, f, name="dw_up")
    zero = emit("ffn", dict(w_down=d_w_down, w_gate_t=d_w_gate_t, w_up_t=d_w_up_t))
    dgates, dbr_a, dbr_r, dy_attn, dy_rwkv = _branch_merge_bwd(
        dh1, p["w_o"], gates, br_a, br_r, p["w_br_attn_t"], p["w_br_rwkv_t"], name="branch_merge_bwd")
    d_w_o = _mm_tn(merged, dh1, name="dw_o")
    d_w_br_attn_t = _mm_tn(dbr_a, y_attn, name="dw_br_attn")
    d_w_br_rwkv_t = _mm_tn(dbr_r, y_rwkv, name="dw_br_rwkv")
    zero = zero + emit("branch", dict(w_o=d_w_o, w_br_attn_t=d_w_br_attn_t, w_br_rwkv_t=d_w_br_rwkv_t))

    post_params = [p["ln_w"] + zero, p["ln_b"], p["r_k"], hmean]
    res = _rowwise_bwd(_rwkv_post, [y_scan, r_, k_, v_, g_], post_params, [[dy_rwkv]], name="rwkv_post_bwd",
                       diff_rows=[True] * 5, diff_params=[True, True, True, False])
    dy_scan, dr_p, dk_p, dv_p, dg_p, d_ln_w, d_ln_b, d_r_k = res
    dr_s, dlw_s, dk_s, dv_s, da_s, db_s = _scan_bwd(r_, lw_, k_, v_, a_, b_, states, inverses, dy_scan,
                                                    name="wkv_scan_bwd")
    res = _mixer_inputs_bwd([p_rkv, p_lora], [mix_rkv, mix_lora], prep_params,
                            [[dr_s, dr_p], [dlw_s], [dk_s, dk_p], [dv_s, dv_p], [da_s], [db_s], [dg_p]],
                            name="mixer_inputs_bwd")
    dp_rkv, dp_lora, d_mix_rkv, d_mix_lora, d_w0, d_w2, d_a0, d_a2, d_g2, d_k_k, d_k_a = res

    dq, dk, dv, dkm, dvm, d_sinks = _attention_bwd(q, k, v, p["sinks"], y_attn, dy_attn, name="attention_bwd")
    (dqkv,) = _rowwise(_attn_prep_transposed, [dq, dk, dv, cos_t, sin_t], [swap, dkm, dvm], [(ATTN_PROJ, bf16)],
                       name="attn_prep_bwd")

    in_rows, (at_qkv, at_rkv, at_lora, at_gates) = p["w_in_lr"].shape[1], [off for off, _ in _pieces(in_widths)]
    d_w_in_t, db_gates = _mm_tn(dgates, u, name="dw_gates", colsum=True, into=(in_rows, at_gates, None))
    d_w_in_t, db_rkv = _mm_tn(dp_rkv, u, name="dw_rkv", colsum=True, into=(in_rows, at_rkv, d_w_in_t))
    d_w_in_t, db_lora = _mm_tn(dp_lora, u, name="dw_lora", colsum=True, into=(in_rows, at_lora, d_w_in_t))
    d_w_in_t, db_qkv = _mm_tn(dqkv, u, name="dw_qkv", colsum=True, into=(in_rows, at_qkv, d_w_in_t))
    zero = emit("input", dict(w_in_t=d_w_in_t, g2=d_g2, w2=d_w2, a2=d_a2))
    du = _proj_in_bwd([dqkv, dp_rkv, dp_lora, dgates], p["w_in_lr"], name="d_u")
    dx, d_front, d_mix_g = _input_norm_bwd(h0, p["norm_mix_g"] + zero, du, dh1, name="norm_mix_bwd")

    grads = dict(
        w_in_t=d_w_in_t,
        b_in=jnp.concatenate([db_qkv, db_rkv, db_lora, db_gates], axis=1),
        mix=jnp.concatenate([d_mix_rkv, d_mix_lora], axis=1),
        norm_mix_g=d_mix_g, sinks=d_sinks, w0=d_w0, w2=d_w2, a0=d_a0, a2=d_a2, g2=d_g2, k_k=d_k_k, k_a=d_k_a,
        r_k=d_r_k, ln_w=d_ln_w, ln_b=d_ln_b, w_br_attn_t=d_w_br_attn_t, w_br_rwkv_t=d_w_br_rwkv_t, w_o=d_w_o,
        norm_ffn_g=d_ffn_g, w_gate_t=d_w_gate_t, w_up_t=d_w_up_t, w_down=d_w_down, norm_final_g=d_final_g,
        meta=d_front[PAD:],
    )
    return loss8[0, 0], dx, grads


def _position():
    return lax.axis_index("x"), lax.axis_index("y"), lax.axis_index("c")


def _other_chips(x, y):
    return [(1 - x, y), (x, 1 - y), (1 - x, 1 - y)]


_HBM = pl.BlockSpec(memory_space=pltpu.HBM)
_SEM = pl.BlockSpec(memory_space=pltpu.SEMAPHORE)
_EFFECT = pltpu.SideEffectType.DATAFLOW_SIDE_EFFECTING


def _landing_zone(src, kind):
    shape = {"whole": (N_CHIPS,) + src.shape, "half": (2, N_CHIPS, src.shape[0], src.shape[1] // 2),
             "slab": (3,) + src.shape[1:], "sibling": src.shape, "all": (N_DEV - 1,) + src.shape}[kind]
    return lax.empty(shape, src.dtype)


def _copies_per_source(kind):
    return {"sibling": 1, "all": N_DEV - 1}.get(kind, 3)


def _chip_copies(src_refs, land_refs, send_sems, recv_sems, kind):
    x, y, c = _position()
    if kind == "all":
        copies = []
        for a, (src, land) in enumerate(zip(src_refs, land_refs)):
            for rel in range(1, N_DEV):
                peer = ((1 - x) if rel & 4 else x, (1 - y) if rel & 2 else y, (1 - c) if rel & 1 else c)
                k = (N_DEV - 1) * a + rel - 1
                copies.append(pltpu.make_async_remote_copy(
                    src_ref=src, dst_ref=land.at[rel - 1], send_sem=send_sems.at[k], recv_sem=recv_sems.at[k],
                    device_id=peer, device_id_type=MESH))
        return copies
    if kind == "sibling":
        return [pltpu.make_async_remote_copy(
            src_ref=src, dst_ref=land, send_sem=send_sems.at[a], recv_sem=recv_sems.at[a],
            device_id=(x, y, 1 - c), device_id_type=MESH) for a, (src, land) in enumerate(zip(src_refs, land_refs))]
    copies = []
    for a, (src, land) in enumerate(zip(src_refs, land_refs)):
        for j, (px, py) in enumerate(_other_chips(x, y)):
            if kind == "whole":
                src_ref, dst_ref = src, land.at[2 * x + y]
            elif kind == "half":
                half = src.shape[1] // 2
                src_ref, dst_ref = src.at[:, pl.ds(pl.multiple_of(c * half, half), half)], land.at[c, 2 * x + y]
            else:
                src_ref, dst_ref = src.at[2 * px + py], land.at[j]
            copies.append(pltpu.make_async_remote_copy(
                src_ref=src_ref, dst_ref=dst_ref, send_sem=send_sems.at[3 * a + j], recv_sem=recv_sems.at[3 * a + j],
                device_id=(px, py, c), device_id_type=MESH))
    return copies


def _exchange_start(srcs, *, kind, name):
    n = len(srcs)
    lands = [_landing_zone(s, kind) for s in srcs]

    def body(*refs):
        for cp in _chip_copies(refs[:n], refs[n:2 * n], refs[2 * n], refs[2 * n + 1], kind):
            cp.start()
        refs[-1][...] = jnp.zeros_like(refs[-1])

    res = pl.pallas_call(
        body, name=name,
        out_shape=(pltpu.SemaphoreType.DMA((_copies_per_source(kind) * n,)),
                   pltpu.SemaphoreType.DMA((_copies_per_source(kind) * n,)),
                   *[pltpu.HBM(a.shape, a.dtype) for a in srcs + lands], jax.ShapeDtypeStruct((8, 128), f32)),
        in_specs=[_HBM] * (2 * n),
        out_specs=(_SEM, _SEM, *[_HBM] * (2 * n), pl.BlockSpec(memory_space=pltpu.VMEM)),
        input_output_aliases={i: 2 + i for i in range(2 * n)},
        compiler_params=pltpu.CompilerParams(has_side_effects=_EFFECT),
    )(*[pltpu.with_memory_space_constraint(a, pltpu.HBM) for a in srcs + lands])
    return res[0], res[1], list(res[2:2 + n]), list(res[2 + n:2 + 2 * n]), res[-1]


def _exchange_wait(handle, after, *, kind, name):
    send_sems, recv_sems, srcs, lands, _ = handle
    n = len(srcs)

    def body(*refs):
        for cp in _chip_copies(refs[:n], refs[n:2 * n], refs[2 * n], refs[2 * n + 1], kind):
            cp.wait_send()
            cp.wait_recv()

    res = pl.pallas_call(
        body, name=name,
        out_shape=tuple(pltpu.HBM(a.shape, a.dtype) for a in srcs + lands),
        in_specs=[_HBM] * (2 * n) + [_SEM, _SEM, pl.BlockSpec(memory_space=pl.ANY)],
        out_specs=tuple([_HBM] * (2 * n)),
        input_output_aliases={i: i for i in range(2 * n)},
        compiler_params=pltpu.CompilerParams(has_side_effects=_EFFECT),
    )(*srcs, *lands, send_sems, recv_sems, after)
    return list(res[:n]), list(res[n:])


def _sum_own_and_received(g, recv, *, name):
    _, r, w = g.shape
    tm = _tile(r)
    if g.dtype == bf16 and tm % 16:
        tm = r
    x, y, _ = _position()
    me = jnp.reshape(2 * x + y, (1,)).astype(jnp.int32)

    def body(me_ref, g_ref, r_ref, o_ref):
        o_ref[...] = (g_ref[0].astype(f32) + r_ref[0].astype(f32)) + (r_ref[1].astype(f32) + r_ref[2].astype(f32))

    return pl.pallas_call(
        body, name=name,
        grid_spec=pltpu.PrefetchScalarGridSpec(
            num_scalar_prefetch=1, grid=(r // tm,),
            in_specs=[pl.BlockSpec((1, tm, w), lambda i, me_ref: (me_ref[0], i, 0)),
                      pl.BlockSpec((3, tm, w), lambda i, me_ref: (0, i, 0))],
            out_specs=pl.BlockSpec((tm, w), lambda i, me_ref: (i, 0))),
        out_shape=jax.ShapeDtypeStruct((r, w), f32),
        compiler_params=_params(("parallel",)),
    )(me, g, recv)


def _swap_halves(zone, *, name):
    def body(z_ref, o_ref, send_sems, recv_sems):
        x, y, c = _position()
        mine = [pltpu.make_async_remote_copy(
            src_ref=o_ref.at[c, 2 * px + py], dst_ref=o_ref.at[c, 2 * px + py], send_sem=send_sems.at[j],
            recv_sem=recv_sems.at[j], device_id=(x, y, 1 - c), device_id_type=MESH)
            for j, (px, py) in enumerate(_other_chips(x, y))]
        for cp in mine:
            cp.start()
        for j, (px, py) in enumerate(_other_chips(x, y)):
            pltpu.make_async_remote_copy(
                src_ref=o_ref.at[c, 2 * px + py], dst_ref=o_ref.at[1 - c, 2 * px + py], send_sem=send_sems.at[j],
                recv_sem=recv_sems.at[j], device_id=(x, y, 1 - c), device_id_type=MESH).wait_recv()
        for cp in mine:
            cp.wait_send()

    return pl.pallas_call(
        body, name=name,
        in_specs=[pl.BlockSpec(memory_space=pl.ANY)], out_specs=pl.BlockSpec(memory_space=pl.ANY),
        out_shape=jax.ShapeDtypeStruct(zone.shape, zone.dtype), input_output_aliases={0: 0},
        scratch_shapes=[pltpu.SemaphoreType.DMA((3,)), pltpu.SemaphoreType.DMA((3,))],
    )(zone)


def _sum_all_devices(own, received, *, name):
    def body(own_ref, got_ref, o_ref):
        x, y, c = _position()
        me = 4 * x + 2 * y + c
        acc = None
        for d in range(N_DEV):
            rel = jnp.bitwise_xor(me, d)
            block = jnp.where(rel == 0, own_ref[...], got_ref[jnp.maximum(rel, 1) - 1])
            acc = block if acc is None else acc + block
        o_ref[...] = acc

    return pl.pallas_call(
        body, name=name,
        in_specs=[pl.BlockSpec(memory_space=pltpu.VMEM)] * 2, out_specs=pl.BlockSpec(memory_space=pltpu.VMEM),
        out_shape=jax.ShapeDtypeStruct(own.shape, f32),
    )(own, received)


def _adam_math(w, g, m, v):
    nm = ADAM_B1 * m + (1.0 - ADAM_B1) * g
    nv = ADAM_B2 * v + (1.0 - ADAM_B2) * (g * g)
    m_hat = nm / (1.0 - ADAM_B1 ** ADAM_STEP)
    v_hat = nv / (1.0 - ADAM_B2 ** ADAM_STEP)
    return -ADAM_LR * (m_hat / (jnp.sqrt(v_hat) + ADAM_EPS) + ADAM_WD * w), nm, nv


def _adamw(w, g_parts, m, v, *, name, transposed=False, after=None):
    rows, cols = w.shape
    ordered = after is not None
    if transposed:
        tm = 256 if rows % 256 == 0 else rows
        g_spec = pl.BlockSpec((cols, tm), lambda i: (0, i))
    else:
        tm = _tile(rows, 256)
        g_spec = pl.BlockSpec((tm, cols), lambda i: (i, 0))
    n = len(g_parts)

    def body(*refs):
        refs = refs[1:] if ordered else refs
        w_ref, m_ref, v_ref = refs[0], refs[1 + n], refs[2 + n]
        g_ref, d_ref, nm_ref, nv_ref = refs[3 + n:]
        gv = refs[1][...]
        for part in refs[2:1 + n]:
            gv = gv + part[...]
        if transposed:
            gv = gv.T
        g_ref[...] = gv
        d_ref[...], nm_ref[...], nv_ref[...] = _adam_math(w_ref[...], gv, m_ref[...], v_ref[...])

    spec = pl.BlockSpec((tm, cols), lambda i: (i, 0))
    shape = jax.ShapeDtypeStruct((rows, cols), f32)
    return pl.pallas_call(
        body, name=name, grid=(rows // tm,),
        in_specs=[pl.BlockSpec(memory_space=pl.ANY)] * ordered + [spec] + [g_spec] * n + [spec] * 2,
        out_specs=[spec] * 4, out_shape=[shape] * 4,
        compiler_params=_params(("parallel",)),
    )(*([after] if ordered else []), w, *g_parts, m, v)


def _pad_rows(a, rows):
    return jnp.concatenate([a, jnp.zeros((rows - a.shape[0], a.shape[1]), a.dtype)], axis=0) if rows > a.shape[0] else a


_SMALL = (("norm_mix_g", D_MODEL), ("b_in", D_IN), ("sinks", Q_HEADS), ("mix", RWKV_PROJ), ("w0", RWKV_DIM),
          ("a0", RWKV_DIM), ("k_k", RWKV_DIM), ("k_a", RWKV_DIM), ("r_k", RWKV_DIM), ("ln_w", RWKV_DIM),
          ("ln_b", RWKV_DIM), ("norm_ffn_g", D_MODEL), ("norm_final_g", D_MODEL))


LANES = 128


def _small_layout():
    out, off = {}, 0
    for n, size in _SMALL + (("loss", 1),):
        pieces, col = [], 0
        while col < size:
            row, lane = divmod(off + col, PACK_W)
            width = min(size - col, PACK_W - lane)
            pieces.append((row, lane, width, col))
            col += width
        out[n] = pieces
        off += -(-size // LANES) * LANES
    return out, -(-off // PACK_W)


def _pack_small(d, loss):
    layout, rows = _small_layout()
    parts, used = [], 0
    for n, size in _SMALL + (("loss", 1),):
        item = loss if n == "loss" else d[n]
        fill = -size % LANES
        parts += [item.reshape(-1).astype(f32), jnp.zeros((fill,), f32)]
        used += size + fill
    parts.append(jnp.zeros((rows * PACK_W - used,), f32))
    return jnp.concatenate(parts).reshape(rows, PACK_W)


def _adamw_small(packed, first_row, ws, ms, vs, meta, *, name):
    layout, _ = _small_layout()
    names = [n for n, _ in _SMALL]
    k = len(names)

    def body(*refs):
        packed_ref = refs[0]
        w_refs, m_refs, v_refs = refs[1:1 + k], refs[1 + k:1 + 2 * k], refs[1 + 2 * k:1 + 3 * k]
        meta_refs = refs[1 + 3 * k:5 + 3 * k]
        outs = refs[5 + 3 * k:]
        for idx, n in enumerate(names):
            for row, lane, width, col in layout[n]:
                gv = packed_ref[first_row + row:first_row + row + 1, lane:lane + width]
                at = (slice(None), slice(col, col + width))
                new = _adam_math(w_refs[idx][at], gv, m_refs[idx][at], v_refs[idx][at])
                for o_ref, val in zip(outs[4 * idx:4 * idx + 4], (gv,) + new):
                    o_ref[at] = val
        for o_ref, val in zip(outs[4 * k:], _adam_math(*(r[...] for r in meta_refs))):
            o_ref[...] = val

    ins = [packed] + [d[n] for d in (ws, ms, vs) for n in names] + list(meta)
    shapes = [jax.ShapeDtypeStruct(ws[n].shape, f32) for n in names for _ in range(4)]
    shapes += [jax.ShapeDtypeStruct(meta[0].shape, f32)] * 3
    res = pl.pallas_call(
        body, name=name, grid=(1,), in_specs=[_full(a.shape) for a in ins],
        out_specs=[_full(s.shape) for s in shapes], out_shape=shapes,
        compiler_params=_params(("arbitrary",)),
    )(*ins)
    return {n: res[4 * i:4 * i + 4] for i, n in enumerate(names)}, res[4 * k:]


def kernel(x, meta_tokens, norm_mix_g, w_in, b_in, attn_sinks, rwkv_mix, rwkv_w0, rwkv_w2, rwkv_a0, rwkv_a2, rwkv_g2, rwkv_k_k, rwkv_k_a, rwkv_r_k, rwkv_ln_w, rwkv_ln_b, w_br_attn, w_br_rwkv, w_o, norm_ffn_g, w_ffn_gate, w_ffn_up, w_ffn_down, norm_final_g, loss_target, m_meta_tokens, m_norm_mix_g, m_w_in, m_b_in, m_attn_sinks, m_rwkv_mix, m_rwkv_w0, m_rwkv_w2, m_rwkv_a0, m_rwkv_a2, m_rwkv_g2, m_rwkv_k_k, m_rwkv_k_a, m_rwkv_r_k, m_rwkv_ln_w, m_rwkv_ln_b, m_w_br_attn, m_w_br_rwkv, m_w_o, m_norm_ffn_g, m_w_ffn_gate, m_w_ffn_up, m_w_ffn_down, m_norm_final_g, v_meta_tokens, v_norm_mix_g, v_w_in, v_b_in, v_attn_sinks, v_rwkv_mix, v_rwkv_w0, v_rwkv_w2, v_rwkv_a0, v_rwkv_a2, v_rwkv_g2, v_rwkv_k_k, v_rwkv_k_a, v_rwkv_r_k, v_rwkv_ln_w, v_rwkv_ln_b, v_w_br_attn, v_w_br_rwkv, v_w_o, v_norm_ffn_g, v_w_ffn_gate, v_w_ffn_up, v_w_ffn_down, v_norm_final_g):
    names = ("meta_tokens", "norm_mix_g", "w_in", "b_in", "attn_sinks", "rwkv_mix", "rwkv_w0", "rwkv_w2", "rwkv_a0",
             "rwkv_a2", "rwkv_g2", "rwkv_k_k", "rwkv_k_a", "rwkv_r_k", "rwkv_ln_w", "rwkv_ln_b", "w_br_attn",
             "w_br_rwkv", "w_o", "norm_ffn_g", "w_ffn_gate", "w_ffn_up", "w_ffn_down", "norm_final_g")
    w_all = dict(zip(names, (meta_tokens, norm_mix_g, w_in, b_in, attn_sinks, rwkv_mix, rwkv_w0, rwkv_w2, rwkv_a0,
                             rwkv_a2, rwkv_g2, rwkv_k_k, rwkv_k_a, rwkv_r_k, rwkv_ln_w, rwkv_ln_b, w_br_attn,
                             w_br_rwkv, w_o, norm_ffn_g, w_ffn_gate, w_ffn_up, w_ffn_down, norm_final_g)))
    m_all = dict(zip(names, (m_meta_tokens, m_norm_mix_g, m_w_in, m_b_in, m_attn_sinks, m_rwkv_mix, m_rwkv_w0,
                             m_rwkv_w2, m_rwkv_a0, m_rwkv_a2, m_rwkv_g2, m_rwkv_k_k, m_rwkv_k_a, m_rwkv_r_k,
                             m_rwkv_ln_w, m_rwkv_ln_b, m_w_br_attn, m_w_br_rwkv, m_w_o, m_norm_ffn_g, m_w_ffn_gate,
                             m_w_ffn_up, m_w_ffn_down, m_norm_final_g)))
    v_all = dict(zip(names, (v_meta_tokens, v_norm_mix_g, v_w_in, v_b_in, v_attn_sinks, v_rwkv_mix, v_rwkv_w0,
                             v_rwkv_w2, v_rwkv_a0, v_rwkv_a2, v_rwkv_g2, v_rwkv_k_k, v_rwkv_k_a, v_rwkv_r_k,
                             v_rwkv_ln_w, v_rwkv_ln_b, v_w_br_attn, v_w_br_rwkv, v_w_o, v_norm_ffn_g, v_w_ffn_gate,
                             v_w_ffn_up, v_w_ffn_down, v_norm_final_g)))
    cx, cy, _ = _position()
    chip = 2 * cx + cy

    t_of = dict(w_in_t="w_in", w_gate_t="w_ffn_gate", w_up_t="w_ffn_up", w_br_attn_t="w_br_attn",
                w_br_rwkv_t="w_br_rwkv", g2_t="rwkv_g2", w2_t="rwkv_w2", a2_t="rwkv_a2")
    plain_of = dict(w_down="w_ffn_down", w_o="w_o")
    meta_cols = meta_tokens.shape[1]

    def shard(k):
        return (w_all[t_of[k]][0].T if k in t_of else w_all[plain_of[k]][0]).astype(bf16)

    def whole(zone, own):
        return lax.dynamic_update_slice_in_dim(zone, own[None], chip, axis=0).reshape(-1, own.shape[-1])

    tiny = ("g2_t", "w2_t", "a2_t")
    late = ("w_gate_t", "w_up_t", "w_down", "w_o", "w_br_attn_t", "w_br_rwkv_t")
    w_in_own = shard("w_in_t")
    w_in_rows, w_in_cols = w_in_own.shape
    tiny_h = _exchange_start([shard(k) for k in tiny] + [meta_tokens], kind="whole", name="gather_tiny_start")
    w_in_h = _exchange_start([w_in_own + tiny_h[4][0, 0].astype(bf16)], kind="half", name="gather_w_in_start")
    behind = w_in_h[4][0, 0].astype(bf16)
    late_h = _exchange_start([shard(k) + behind for k in late], kind="whole", name="gather_late_start")
    own, zones = _exchange_wait(tiny_h, late_h[4], kind="whole", name="gather_tiny_wait")
    got = {k: whole(z, o) for k, z, o in zip(tiny, zones, own)}
    meta_full = whole(zones[-1], own[-1]).reshape(N_CHIPS, N_META, meta_cols).transpose(1, 0, 2).reshape(N_META, -1)
    p = dict(
        g2=got["g2_t"].T.astype(f32), w2=got["w2_t"].T.astype(f32), a2=got["a2_t"].T.astype(f32),
        b_in=b_in, sinks=attn_sinks, mix=rwkv_mix, w0=rwkv_w0, a0=rwkv_a0, k_k=rwkv_k_k, k_a=rwkv_k_a,
        r_k=rwkv_r_k.reshape(1, RWKV_DIM), ln_w=rwkv_ln_w, ln_b=rwkv_ln_b, norm_mix_g=norm_mix_g,
        norm_ffn_g=norm_ffn_g, norm_final_g=norm_final_g.reshape(1, D_MODEL),
    )

    def early_weights(after):
        own_h, zones_h = _exchange_wait(w_in_h, after, kind="half", name="gather_w_in_wait")
        zone = _swap_halves(zones_h[0], name="swap_w_in_halves")
        own_halves = own_h[0].reshape(w_in_rows, 2, w_in_cols // 2).transpose(1, 0, 2)[:, None]
        zone = lax.dynamic_update_slice(zone, own_halves, (0, chip, 0, 0))
        return dict(w_in_lr=zone.reshape(2, N_CHIPS * w_in_rows, w_in_cols // 2))

    def late_weights(after):
        own_l, zones_l = _exchange_wait(late_h, after, kind="whole", name="gather_late_wait")
        return {k: whole(z, o) for k, z, o in zip(late, zones_l, own_l)}

    started = {}

    def partial_sums(groups, after):
        parts = {}
        for group in groups:
            keys, handle = started[group]
            slabs, lands = _exchange_wait(handle, after, kind="slab", name="scatter_" + group + "_wait")
            parts.update({k: _sum_own_and_received(s, l, name="sum_chips_" + k) for k, s, l in zip(keys, slabs, lands)})
        return parts

    def emit(group, grads_):
        keys = list(grads_)
        slabs = []
        for k in keys:
            a = grads_[k].T if k in ("g2", "w2", "a2") else grads_[k]
            slabs.append(a.reshape(N_CHIPS, a.shape[0] // N_CHIPS, a.shape[1]))
        started[group] = (keys, _exchange_start(slabs, kind="slab", name="scatter_" + group + "_start"))
        zero = started[group][1][4]
        if group == "input":
            started["parts_a"] = partial_sums(("ffn", "branch"), zero)
            started["swap_a"] = _exchange_start(list(started["parts_a"].values()), kind="sibling",
                                                name="swap_cores_a_start")
            zero = started["swap_a"][4]
        return zero[0, 0]

    loss, dx, g = _local_step(x[0], loss_target[0], meta_full, p, early_weights, late_weights, emit)

    grads, delta, new_m, new_v = {}, {}, {}, {}
    in_grad_layout = ("w_in_t", "w_gate_t", "w_up_t")
    weight_of = {**t_of, **plain_of}

    def update(keys, mine, theirs, after=None):
        for k, part, other in zip(keys, mine, theirs):
            both = [part, other]
            k = k + "_t" if k in ("g2", "w2", "a2") else k
            n = weight_of[k]
            shape2 = w_all[n].shape[1:]
            w_, m_, v_ = (a.reshape(shape2) for a in (w_all[n], m_all[n], v_all[n]))
            if k in in_grad_layout:
                raw = _adamw(w_.T, both, m_.T, v_.T, name="adamw_" + n, after=after)
                res = [t.T for t in raw]
            else:
                res = raw = _adamw(w_, both, m_, v_, name="adamw_" + n, transposed=k in t_of, after=after)
            grads[n], delta[n], new_m[n], new_v[n] = (t.reshape(w_all[n].shape) for t in res)
        return raw[1]

    small = _pack_small(g, loss)
    small_rows8 = -(-(small.shape[0] + N_META) // 8) * 8
    small_h = _exchange_start([_pad_rows(jnp.concatenate([g["meta"], small], axis=0), small_rows8)], kind="all",
                              name="reduce_small_start")
    keys_a = list(started["parts_a"])
    mine_a, theirs_a = _exchange_wait(started["swap_a"], small_h[4], kind="sibling", name="swap_cores_a_wait")
    cut = len(started["ffn"][0])
    done = update(keys_a[:cut], mine_a[:cut], theirs_a[:cut])
    parts_b = partial_sums(("input",), done)
    swap_b = _exchange_start(list(parts_b.values()), kind="sibling", name="swap_cores_b_start")
    done = update(keys_a[cut:], mine_a[cut:], theirs_a[cut:], after=swap_b[4])
    small_own, small_got = _exchange_wait(small_h, done, kind="all", name="reduce_small_wait")
    reduced = _sum_all_devices(small_own[0], small_got[0], name="reduce_small_sum")
    update(list(parts_b), *_exchange_wait(swap_b, reduced, kind="sibling", name="swap_cores_b_wait"))
    g_meta = lax.dynamic_slice_in_dim(reduced[:N_META], chip * meta_cols, meta_cols, axis=1)
    (loss_row, loss_lane, _, _), = _small_layout()[0]["loss"]
    loss_total = reduced[N_META + loss_row, loss_lane]

    small_of = dict(norm_mix_g="norm_mix_g", b_in="b_in", attn_sinks="sinks", rwkv_mix="mix", rwkv_w0="w0",
                    rwkv_a0="a0", rwkv_k_k="k_k", rwkv_k_a="k_a", rwkv_r_k="r_k", rwkv_ln_w="ln_w",
                    rwkv_ln_b="ln_b", norm_ffn_g="norm_ffn_g", norm_final_g="norm_final_g")
    as_rows = [{k: src[n].reshape(1, -1) for n, k in small_of.items()} for src in (w_all, m_all, v_all)]
    meta_in = (meta_tokens, g_meta, m_meta_tokens, v_meta_tokens)
    small_out, meta_out = _adamw_small(reduced, N_META, *as_rows, meta_in, name="adamw_small")
    grads["meta_tokens"] = g_meta
    delta["meta_tokens"], new_m["meta_tokens"], new_v["meta_tokens"] = meta_out
    for n, k in small_of.items():
        grads[n], delta[n], new_m[n], new_v[n] = (t.reshape(w_all[n].shape) for t in small_out[k])

    return (loss_total, dx.reshape(x.shape), *[grads[n] for n in names], *[delta[n] for n in names],
            *[new_m[n] for n in names], *[new_v[n] for n in names])
```

```python
import math

import jax
import jax.numpy as jnp
import numpy as np
from jax import lax
from jax.experimental import pallas as pl
from jax.experimental.pallas import tpu as pltpu

f32 = jnp.float32
bf16 = jnp.bfloat16

D_MODEL = 1024
N_META = 16
HEAD_DIM = 64
Q_HEADS = 8
KV_HEADS = 2
GROUP = Q_HEADS // KV_HEADS
WINDOW = 128
BLOCK = 128
ROPE_THETA = 500000.0
ROPE_DIM = HEAD_DIM // 4
RWKV_HEADS = 8
RWKV_HEAD = 64
RWKV_DIM = RWKV_HEADS * RWKV_HEAD
DECAY_LORA = 64
AAA_LORA = 64
GATE_LORA = 160
LORA_W = DECAY_LORA + AAA_LORA + GATE_LORA
RWKV_LN_EPS = 64e-5
D_FF = 2816
Q_W = Q_HEADS * HEAD_DIM
KV_W = KV_HEADS * HEAD_DIM
ATTN_PROJ = Q_W + 2 * KV_W
RKV_W = 3 * RWKV_DIM
RWKV_PROJ = RKV_W + LORA_W
D_IN = ATTN_PROJ + RWKV_PROJ + 2 * D_MODEL
RMS_EPS = 1e-6
NEG_INF = -1e30
PAD = BLOCK - N_META
FRONT = PAD + N_META

ADAM_LR = 0.001
ADAM_B1 = 0.9
ADAM_B2 = 0.999
ADAM_EPS = 1e-08
ADAM_WD = 0.01
ADAM_STEP = 10

N_CHIPS = 4
N_DEV = 8
CHUNK = 128
VMEM_LIMIT = 56 * 1024 * 1024
MM_ROWS = 704
PACK_W = 1024
MESH = pl.DeviceIdType.MESH


def _tile(m, pref=384):
    for step in (16, 8):
        for t in range(min(m, pref) // step * step, 0, -step):
            if m % t == 0:
                return t
    return m


def _params(sem=None):
    return pltpu.CompilerParams(dimension_semantics=sem, vmem_limit_bytes=VMEM_LIMIT)


def _full(shape):
    nd = len(shape)
    return pl.BlockSpec(shape, lambda *_: (0,) * nd)


def _dot(a, b, dims="nn"):
    dn = {"nn": (((1,), (0,)), ((), ())), "nt": (((1,), (1,)), ((), ())), "tn": (((0,), (0,)), ((), ()))}[dims]
    return lax.dot_general(a.astype(bf16), b.astype(bf16), dn, preferred_element_type=f32)


def _two_pass(x, m, dims="nn"):
    x_hi = x.astype(bf16)
    x_lo = (x - x_hi.astype(f32)).astype(bf16)
    return _dot(x_hi, m, dims) + _dot(x_lo, m, dims)


@jax.custom_vjp
def _dot_const(x, m):
    return _two_pass(x, m)


def _dot_const_fwd(x, m):
    return _two_pass(x, m), m


def _dot_const_bwd(m, ct):
    return _two_pass(ct, m, "nt"), jnp.zeros_like(m)


_dot_const.defvjp(_dot_const_fwd, _dot_const_bwd)


def _two_pass_left(m, x, dims):
    x_hi = x.astype(bf16)
    x_lo = (x - x_hi.astype(f32)).astype(bf16)
    return _dot(m, x_hi, dims) + _dot(m, x_lo, dims)


@jax.custom_vjp
def _const_dot(m, x):
    return _two_pass_left(m, x, "nn")


def _const_dot_fwd(m, x):
    return _two_pass_left(m, x, "nn"), m


def _const_dot_bwd(m, ct):
    return jnp.zeros_like(m), _two_pass_left(m, ct, "tn")


_const_dot.defvjp(_const_dot_fwd, _const_dot_bwd)


def _mm(a, b, mode, *, name, out_dtype=f32, bias=None, add=None, zero_rows_below=0):
    m, _ = a.shape
    n = b.shape[1] if mode == "nn" else b.shape[0]
    tm = _tile(m, MM_ROWS)
    has_bias, has_add = bias is not None, add is not None

    def body(*refs):
        a_ref, b_ref = refs[0], refs[1]
        o_ref = refs[-1]
        acc = _dot(a_ref[...], b_ref[...], mode)
        k = 2
        if has_bias:
            acc = acc + refs[k][...]
            k += 1
        if zero_rows_below:
            rows = pl.program_id(0) * tm + lax.broadcasted_iota(jnp.int32, acc.shape, 0)
            acc = jnp.where(rows >= zero_rows_below, acc, 0.0)
        if has_add:
            acc = acc + refs[k][...].astype(f32)
        o_ref[...] = acc.astype(out_dtype)

    ins = [a, b]
    in_specs = [pl.BlockSpec((tm, a.shape[1]), lambda i: (i, 0)), _full(b.shape)]
    if has_bias:
        ins.append(bias)
        in_specs.append(_full(bias.shape))
    if has_add:
        ins.append(add)
        in_specs.append(pl.BlockSpec((tm, n), lambda i: (i, 0)))
    return pl.pallas_call(
        body, name=name, grid=(m // tm,), in_specs=in_specs,
        out_specs=pl.BlockSpec((tm, n), lambda i: (i, 0)),
        out_shape=jax.ShapeDtypeStruct((m, n), out_dtype),
        compiler_params=_params(("parallel",)),
    )(*ins)


def _pieces(widths):
    out, off = [], 0
    for w in widths:
        out.append((off, w))
        off += w
    return out


def _proj_in(a, w_lr, bias, widths, rope, *, name, zero_rows_below=0):
    m, kdim = a.shape
    half = kdim // 2
    tm = _tile(m, MM_ROWS)
    cos_t, sin_t, swap = rope
    out_widths = [Q_W, KV_W, KV_W] + list(widths[1:])

    def body(a_ref, w_ref, b_ref, cos_ref, sin_ref, swap_ref, *outs):
        a_l, a_r = a_ref[:, :half], a_ref[:, half:]
        for j, (off, width) in enumerate(_pieces(widths)):
            acc = _dot(a_l, w_ref[0, off:off + width, :], "nt") + _dot(a_r, w_ref[1, off:off + width, :], "nt")
            acc = acc + b_ref[:, off:off + width]
            if zero_rows_below:
                rows = pl.program_id(0) * tm + lax.broadcasted_iota(jnp.int32, acc.shape, 0)
                acc = jnp.where(rows >= zero_rows_below, acc, 0.0)
            if j == 0:
                qkv = acc.astype(bf16).astype(f32)
                for o_ref, val in zip(outs[:3], _attn_prep(qkv, cos_ref[...], sin_ref[...], swap_ref[...])):
                    o_ref[...] = val.astype(o_ref.dtype)
            else:
                outs[2 + j][...] = acc.astype(outs[2 + j].dtype)

    table = pl.BlockSpec((tm, HEAD_DIM), lambda i: (i, 0))
    return pl.pallas_call(
        body, name=name, grid=(m // tm,),
        in_specs=[pl.BlockSpec((tm, kdim), lambda i: (i, 0)), _full(w_lr.shape), _full(bias.shape), table, table,
                  _full(swap.shape)],
        out_specs=[pl.BlockSpec((tm, w), lambda i: (i, 0)) for w in out_widths],
        out_shape=[jax.ShapeDtypeStruct((m, w), bf16) for w in out_widths],
        compiler_params=_params(("parallel",)),
    )(a, w_lr, bias, cos_t, sin_t, swap)


def _proj_in_bwd(d_list, w_lr, *, name):
    m = d_list[0].shape[0]
    half = w_lr.shape[2]
    widths = [d.shape[1] for d in d_list]
    tm = _tile(m, MM_ROWS)

    def body(*refs):
        w_ref, o_ref = refs[-2], refs[-1]
        for side in range(2):
            acc = None
            for (off, width), d_ref in zip(_pieces(widths), refs):
                term = _dot(d_ref[...], w_ref[side, off:off + width, :])
                acc = term if acc is None else acc + term
            o_ref[:, side * half:(side + 1) * half] = acc.astype(o_ref.dtype)

    return pl.pallas_call(
        body, name=name, grid=(m // tm,),
        in_specs=[pl.BlockSpec((tm, w), lambda i: (i, 0)) for w in widths] + [_full(w_lr.shape)],
        out_specs=pl.BlockSpec((tm, 2 * half), lambda i: (i, 0)),
        out_shape=jax.ShapeDtypeStruct((m, 2 * half), bf16),
        compiler_params=_params(("parallel",)),
    )(*d_list, w_lr)


def _residual_norm(a, w, res, g, *, name):
    m, d = res.shape
    tm = _tile(m, MM_ROWS)

    def body(a_ref, w_ref, r_ref, g_ref, h_ref, n_ref):
        h = _dot(a_ref[...], w_ref[...]) + r_ref[...]
        h_ref[...] = h
        n_ref[...] = _rms(h, g_ref[...]).astype(n_ref.dtype)

    tile = pl.BlockSpec((tm, d), lambda i: (i, 0))
    return pl.pallas_call(
        body, name=name, grid=(m // tm,),
        in_specs=[pl.BlockSpec((tm, a.shape[1]), lambda i: (i, 0)), _full(w.shape), tile, _full(g.shape)],
        out_specs=[tile, tile],
        out_shape=[jax.ShapeDtypeStruct((m, d), f32), jax.ShapeDtypeStruct((m, d), bf16)],
        compiler_params=_params(("parallel",)),
    )(a, w, res, g)


def _residual_norm_bwd(d_list, w_list, h, g, dh_out, *, name):
    m, d = h.shape
    k = len(d_list)
    tm = _tile(m)

    def body(*refs):
        h_ref, g_ref, dho_ref, dh_ref, dg_ref = refs[2 * k:]
        dn = _dot(refs[0][...], refs[k][...])
        for i in range(1, k):
            dn = dn + _dot(refs[i][...], refs[k + i][...])
        _, vjp = jax.vjp(lambda hv, gv: (_rms(hv, gv), hv), h_ref[...], g_ref[...])
        dh, dg = vjp((dn, dho_ref[...]))
        dh_ref[...] = dh

        @pl.when(pl.program_id(0) == 0)
        def _():
            dg_ref[...] = jnp.zeros_like(dg_ref)

        dg_ref[...] += dg

    tile = pl.BlockSpec((tm, d), lambda i: (i, 0))
    return pl.pallas_call(
        body, name=name, grid=(m // tm,),
        in_specs=[pl.BlockSpec((tm, a.shape[1]), lambda i: (i, 0)) for a in d_list] + [_full(w.shape) for w in w_list]
        + [tile, _full(g.shape), tile],
        out_specs=[tile, _full(g.shape)],
        out_shape=[jax.ShapeDtypeStruct((m, d), f32), jax.ShapeDtypeStruct(g.shape, f32)],
        compiler_params=_params(("arbitrary",)),
    )(*d_list, *w_list, h, g, dh_out)


def _mm_tn(a, b, *, name, colsum=False, out_dtype=bf16, into=None):
    r, m = a.shape
    n = b.shape[1]
    tr = _tile(r, 1408)
    tmo = m
    for cand in (1408, 1024, 768, 512):
        if m > 1024 and m % cand == 0:
            tmo = cand
            break
    steps = r // tr

    rows, offset, target = into or (m, 0, None)

    def body(a_ref, b_ref, *rest):
        o_ref, rest = (rest[0], rest[1:]) if target is None else (rest[1], rest[2:])
        acc = rest[-1]
        i = pl.program_id(1)

        @pl.when(i == 0)
        def _():
            acc[...] = jnp.zeros_like(acc)
            if colsum:
                rest[0][...] = jnp.zeros_like(rest[0])

        acc[...] += _dot(a_ref[...], b_ref[...], "tn")
        if colsum:
            rest[0][...] += jnp.sum(a_ref[...].astype(f32), axis=0, keepdims=True)

        @pl.when(i == steps - 1)
        def _():
            o_ref[...] = acc[...].astype(out_dtype)

    out_shape = [jax.ShapeDtypeStruct((rows, n), out_dtype)]
    if offset % tmo == 0:
        out_specs = [pl.BlockSpec((tmo, n), lambda j, i: (offset // tmo + j, 0))]
    else:
        out_specs = [pl.BlockSpec((pl.Element(tmo), pl.Element(n)), lambda j, i: (pl.multiple_of(offset + j * tmo, math.gcd(offset, tmo)), 0))]
    if colsum:
        out_shape.append(jax.ShapeDtypeStruct((1, m), f32))
        out_specs.append(pl.BlockSpec((1, tmo), lambda j, i: (0, j)))
    in_specs = [pl.BlockSpec((tr, tmo), lambda j, i: (i, j)), pl.BlockSpec((tr, n), lambda j, i: (i, 0))]
    res = pl.pallas_call(
        body, name=name, grid=(m // tmo, steps),
        in_specs=in_specs + ([] if target is None else [pl.BlockSpec(memory_space=pl.ANY)]),
        out_specs=out_specs, out_shape=out_shape,
        scratch_shapes=[pltpu.VMEM((tmo, n), f32)],
        input_output_aliases={} if target is None else {2: 0},
        compiler_params=_params(("parallel", "arbitrary")),
    )(a, b, *([] if target is None else [target]))
    return res if colsum else res[0]


def _rowwise(fn, rows, params, outs, *, name, tm=None):
    m = rows[0].shape[0]
    tm = tm or _tile(m, MM_ROWS)
    nr, npar = len(rows), len(params)

    def body(*refs):
        vals = [r[...] for r in refs[:nr + npar]]
        res = fn(*vals)
        for o_ref, v in zip(refs[nr + npar:], res):
            o_ref[...] = v.astype(o_ref.dtype)

    return pl.pallas_call(
        body, name=name, grid=(m // tm,),
        in_specs=[pl.BlockSpec((tm, r.shape[1]), lambda i: (i, 0)) for r in rows] + [_full(p.shape) for p in params],
        out_specs=[pl.BlockSpec((tm, w), lambda i: (i, 0)) for w, _ in outs],
        out_shape=[jax.ShapeDtypeStruct((m, w), dt) for w, dt in outs],
        compiler_params=_params(("parallel",)),
    )(*rows, *params)


def _rowwise_bwd(fn, rows, params, cts, *, name, diff_rows, diff_params, tm=None, zero_rows_below=0, out_dtypes=None):
    m = rows[0].shape[0]
    tm = tm or _tile(m)
    nr, npar = len(rows), len(params)
    d_idx = [i for i in range(nr) if diff_rows[i]]
    p_idx = [i for i in range(npar) if diff_params[i]]
    out_dtypes = out_dtypes or [f32] * len(d_idx)
    flat_cts = [c for group in cts for c in group]
    n_ct = len(flat_cts)

    def body(*refs):
        vals = [r[...] for r in refs[:nr + npar]]
        ct_refs = refs[nr + npar:nr + npar + n_ct]
        out_refs = refs[nr + npar + n_ct:]
        ct_vals, k = [], 0
        for group in cts:
            acc = ct_refs[k][...].astype(f32)
            for extra in range(1, len(group)):
                acc = acc + ct_refs[k + extra][...].astype(f32)
            k += len(group)
            if zero_rows_below:
                rr = pl.program_id(0) * tm + lax.broadcasted_iota(jnp.int32, acc.shape, 0)
                acc = jnp.where(rr >= zero_rows_below, acc, 0.0)
            ct_vals.append(acc)

        def g(*dargs):
            full = list(vals)
            for pos, i in enumerate(d_idx):
                full[i] = dargs[pos]
            for pos, i in enumerate(p_idx):
                full[nr + i] = dargs[len(d_idx) + pos]
            return tuple(fn(*full))

        _, vjp = jax.vjp(g, *[vals[i].astype(f32) for i in d_idx], *[vals[nr + i] for i in p_idx])
        grads = vjp(tuple(ct_vals))
        for pos in range(len(d_idx)):
            out_refs[pos][...] = grads[pos].astype(out_refs[pos].dtype)
        first = pl.program_id(0) == 0
        for pos in range(len(p_idx)):
            o_ref = out_refs[len(d_idx) + pos]

            @pl.when(first)
            def _(o_ref=o_ref):
                o_ref[...] = jnp.zeros_like(o_ref)

            o_ref[...] += grads[len(d_idx) + pos]

    return pl.pallas_call(
        body, name=name, grid=(m // tm,),
        in_specs=[pl.BlockSpec((tm, r.shape[1]), lambda i: (i, 0)) for r in rows] + [_full(p.shape) for p in params]
        + [pl.BlockSpec((tm, c.shape[1]), lambda i: (i, 0)) for c in flat_cts],
        out_specs=[pl.BlockSpec((tm, rows[i].shape[1]), lambda i_: (i_, 0)) for i in d_idx]
        + [_full(params[i].shape) for i in p_idx],
        out_shape=[jax.ShapeDtypeStruct(rows[i].shape, dt) for i, dt in zip(d_idx, out_dtypes)]
        + [jax.ShapeDtypeStruct(params[i].shape, f32) for i in p_idx],
        compiler_params=_params(("arbitrary",)),
    )(*rows, *params, *flat_cts)


def _rms(x, g):
    return x * lax.rsqrt(jnp.mean(x * x, axis=-1, keepdims=True) + RMS_EPS) * g


def _head_sum_matrix(width, head):
    idx = jnp.arange(width) // head
    return (idx[:, None] == idx[None, :]).astype(f32)


def _rope_tables(lp):
    half = ROPE_DIM // 2
    pos = (np.arange(lp) - PAD).astype(np.float32)
    inv_freq = np.power(np.float32(ROPE_THETA), -np.arange(half, dtype=np.float32) * np.float32(2.0 / ROPE_DIM))
    ang = pos[:, None] * inv_freq[None, :].astype(np.float32)
    cos, sin = np.cos(ang), np.sin(ang)
    ones = np.ones((lp, HEAD_DIM - ROPE_DIM), np.float32)
    cos_t = np.concatenate([cos, cos, ones], axis=1)
    sin_t = np.concatenate([-sin, sin, 0.0 * ones], axis=1)
    i = np.arange(HEAD_DIM)
    src = np.where(i < half, i + half, np.where(i < ROPE_DIM, i - half, i))
    swap = ((i[:, None] == src[None, :]) & (i[None, :] < ROPE_DIM)).astype(np.float32)
    return jnp.asarray(cos_t, f32), jnp.asarray(sin_t, f32), jnp.asarray(swap, f32)


def _attn_prep(qkv, cos_t, sin_t, swap):
    outs = []
    for h in range(Q_HEADS + KV_HEADS):
        t = qkv[:, h * HEAD_DIM:(h + 1) * HEAD_DIM]
        outs.append(t * cos_t + _dot_const(t, swap) * sin_t)
    q = jnp.concatenate(outs[:Q_HEADS], axis=1)
    k = jnp.concatenate(outs[Q_HEADS:], axis=1)
    return q, k, qkv[:, Q_W + KV_W:]


def _attn_prep_transposed(dq, dk, dv, cos_t, sin_t, swap, dk_meta, dv_meta):
    first = pl.program_id(0) == 0

    def with_meta(d, d_meta):
        rest = jnp.zeros((d.shape[0] - BLOCK, KV_W), f32)
        return d.astype(f32) + jnp.where(first, jnp.concatenate([d_meta, rest], axis=0), 0.0)

    parts = []
    for d, heads in ((dq.astype(f32), Q_HEADS), (with_meta(dk, dk_meta), KV_HEADS)):
        for h in range(heads):
            t = d[:, h * HEAD_DIM:(h + 1) * HEAD_DIM]
            parts.append(t * cos_t + _two_pass(t * sin_t, swap, "nt"))
    return (jnp.concatenate(parts + [with_meta(dv, dv_meta)], axis=1),)


def _softplus(z):
    return jnp.maximum(z, 0.0) + jnp.log1p(jnp.exp(-jnp.abs(z)))


def _rwkv_prep(rkv, lora, w0, w2, a0, a2, g2, k_k, k_a, hsum):
    r = rkv[:, :RWKV_DIM]
    k = rkv[:, RWKV_DIM:2 * RWKV_DIM]
    v = rkv[:, 2 * RWKV_DIM:]
    dw = lora[:, :DECAY_LORA]
    da = lora[:, DECAY_LORA:DECAY_LORA + AAA_LORA]
    dg = lora[:, DECAY_LORA + AAA_LORA:]
    w = -_softplus(-(w0 + _dot(jnp.tanh(dw), w2))) - 0.5
    a = jax.nn.sigmoid(a0 + _dot(da, a2))
    g = _dot(jax.nn.sigmoid(dg), g2)
    kk = k * k_k
    kk = kk * lax.rsqrt(jnp.maximum(_dot_const(kk * kk, hsum), 1e-24))
    k = k * (1.0 + (a - 1.0) * k_a)
    log_decay = -jnp.exp(w)
    return r, log_decay, k, v, -kk, kk * a, g


def _rwkv_post(y, r, k, v, g, ln_w, ln_b, r_k, hmean):
    hsum = hmean * RWKV_HEAD
    mean = _dot_const(y, hmean)
    yc = y - mean
    var = _dot_const(yc * yc, hmean)
    yn = yc * lax.rsqrt(var + RWKV_LN_EPS) * ln_w + ln_b
    bonus = _dot_const(r * k * r_k, hsum) * v
    return ((yn + bonus) * g,)


def _merge(gates, br_a, br_r):
    sg = jax.nn.sigmoid(gates)
    return (sg[:, :D_MODEL] * br_a + sg[:, D_MODEL:] * br_r,)


def _swiglu(gate, up):
    return (jax.nn.silu(gate) * up,)


def _ffn_in(f, w_gate_t, w_up_t, *, name):
    m, d = f.shape
    n = w_gate_t.shape[0]
    tm = _tile(m)

    def body(f_ref, wg_ref, wu_ref, g_ref, u_ref, a_ref):
        g = _dot(f_ref[...], wg_ref[...], "nt")
        u = _dot(f_ref[...], wu_ref[...], "nt")
        g_ref[...] = g.astype(g_ref.dtype)
        u_ref[...] = u.astype(u_ref.dtype)
        a_ref[...] = _swiglu(g, u)[0].astype(a_ref.dtype)

    spec = pl.BlockSpec((tm, n), lambda i: (i, 0))
    return pl.pallas_call(
        body, name=name, grid=(m // tm,),
        in_specs=[pl.BlockSpec((tm, d), lambda i: (i, 0)), _full(w_gate_t.shape), _full(w_up_t.shape)],
        out_specs=[spec] * 3, out_shape=[jax.ShapeDtypeStruct((m, n), bf16)] * 3,
        compiler_params=_params(("parallel",)),
    )(f, w_gate_t, w_up_t)


def _branch_merge(y_attn, y_rwkv, w_attn_t, w_rwkv_t, gates, *, name):
    m = y_attn.shape[0]
    tm = _tile(m, MM_ROWS)

    def body(ya_ref, yr_ref, wa_ref, wr_ref, g_ref, a_ref, r_ref, o_ref):
        br_a = _dot(ya_ref[...], wa_ref[...], "nt")
        br_r = _dot(yr_ref[...], wr_ref[...], "nt")
        a_ref[...] = br_a.astype(a_ref.dtype)
        r_ref[...] = br_r.astype(r_ref.dtype)
        o_ref[...] = _merge(g_ref[...].astype(f32), br_a, br_r)[0].astype(o_ref.dtype)

    rows = lambda a: pl.BlockSpec((tm, a.shape[1]), lambda i: (i, 0))
    spec = pl.BlockSpec((tm, D_MODEL), lambda i: (i, 0))
    return pl.pallas_call(
        body, name=name, grid=(m // tm,),
        in_specs=[rows(y_attn), rows(y_rwkv), _full(w_attn_t.shape), _full(w_rwkv_t.shape), rows(gates)],
        out_specs=[spec] * 3, out_shape=[jax.ShapeDtypeStruct((m, D_MODEL), bf16)] * 3,
        compiler_params=_params(("parallel",)),
    )(y_attn, y_rwkv, w_attn_t, w_rwkv_t, gates)


def _branch_merge_bwd(dh, w_o, gates, br_a, br_r, w_attn_t, w_rwkv_t, *, name):
    m = dh.shape[0]
    tm = _tile(m, MM_ROWS)

    def body(dh_ref, w_ref, g_ref, a_ref, r_ref, wa_ref, wr_ref, dg_ref, da_ref, dr_ref, dya_ref, dyr_ref):
        dmerged = _dot(dh_ref[...], w_ref[...], "nt")
        _, vjp = jax.vjp(lambda g, a, r: _merge(g, a, r)[0], g_ref[...].astype(f32), a_ref[...].astype(f32),
                         r_ref[...].astype(f32))
        dg, da, dr = vjp(dmerged)
        dg_ref[...] = dg.astype(dg_ref.dtype)
        da_ref[...] = da.astype(da_ref.dtype)
        dr_ref[...] = dr.astype(dr_ref.dtype)
        dya_ref[...] = _dot(da, wa_ref[...])
        dyr_ref[...] = _dot(dr, wr_ref[...])

    rows = lambda a: pl.BlockSpec((tm, a.shape[1]), lambda i: (i, 0))
    mixer = pl.BlockSpec((tm, w_attn_t.shape[1]), lambda i: (i, 0))
    return pl.pallas_call(
        body, name=name, grid=(m // tm,),
        in_specs=[rows(dh), _full(w_o.shape), rows(gates), rows(br_a), rows(br_r), _full(w_attn_t.shape),
                  _full(w_rwkv_t.shape)],
        out_specs=[rows(gates), rows(br_a), rows(br_r), mixer, mixer],
        out_shape=[jax.ShapeDtypeStruct(gates.shape, bf16), jax.ShapeDtypeStruct(br_a.shape, bf16),
                   jax.ShapeDtypeStruct(br_r.shape, bf16), jax.ShapeDtypeStruct((m, w_attn_t.shape[1]), f32),
                   jax.ShapeDtypeStruct((m, w_rwkv_t.shape[1]), f32)],
        compiler_params=_params(("parallel",)),
    )(dh, w_o, gates, br_a, br_r, w_attn_t, w_rwkv_t)


def _ffn_in_bwd(dh, w_down, gate, up, *, name):
    m, d = dh.shape
    n = w_down.shape[0]
    tm = _tile(m)

    def body(dh_ref, w_ref, g_ref, u_ref, dg_ref, du_ref):
        dact = _dot(dh_ref[...], w_ref[...], "nt")
        _, vjp = jax.vjp(lambda a, b: _swiglu(a, b)[0], g_ref[...].astype(f32), u_ref[...].astype(f32))
        dg, du = vjp(dact)
        dg_ref[...] = dg.astype(dg_ref.dtype)
        du_ref[...] = du.astype(du_ref.dtype)

    spec = pl.BlockSpec((tm, n), lambda i: (i, 0))
    return pl.pallas_call(
        body, name=name, grid=(m // tm,),
        in_specs=[pl.BlockSpec((tm, d), lambda i: (i, 0)), _full(w_down.shape), spec, spec],
        out_specs=[spec] * 2, out_shape=[jax.ShapeDtypeStruct((m, n), bf16)] * 2,
        compiler_params=_params(("parallel",)),
    )(dh, w_down, gate, up)


def _ffn_bwd(dh, w_down, gate, up, w_gate_t, w_up_t, h, g, *, name):
    m, d = dh.shape
    n = w_down.shape[0]
    tm = _tile(m)
    halves = [(c * (n // 2), (c + 1) * (n // 2)) for c in range(2)]

    def body(dh_ref, wd_ref, g_ref, u_ref, wg_hbm, wu_hbm, h_ref, gn_ref, dg_ref, du_ref, dh1_ref, dgn_ref,
             wg_ref, wu_ref, sems):
        first = pl.program_id(0) == 0
        fetches = [pltpu.make_async_copy(wg_hbm, wg_ref, sems.at[0]), pltpu.make_async_copy(wu_hbm, wu_ref, sems.at[1])]

        @pl.when(first)
        def _():
            for cp in fetches:
                cp.start()

        dn = jnp.zeros((tm, d), f32)
        for at, (lo, hi) in enumerate(halves):
            dact = _dot(dh_ref[...], wd_ref[lo:hi, :], "nt")
            _, vjp = jax.vjp(lambda a, b: _swiglu(a, b)[0], g_ref[:, lo:hi].astype(f32), u_ref[:, lo:hi].astype(f32))
            dg, du = (t.astype(bf16) for t in vjp(dact))
            dg_ref[:, lo:hi] = dg
            du_ref[:, lo:hi] = du
            if at == 0:
                @pl.when(first)
                def _():
                    for cp in fetches:
                        cp.wait()
            dn = dn + _dot(dg, wg_ref[lo:hi, :]) + _dot(du, wu_ref[lo:hi, :])
        _, vjp = jax.vjp(lambda hv, gv: (_rms(hv, gv), hv), h_ref[...], gn_ref[...])
        dh1, dgn = vjp((dn, dh_ref[...]))
        dh1_ref[...] = dh1

        @pl.when(pl.program_id(0) == 0)
        def _():
            dgn_ref[...] = jnp.zeros_like(dgn_ref)

        dgn_ref[...] += dgn

    wide = pl.BlockSpec((tm, n), lambda i: (i, 0))
    tile = pl.BlockSpec((tm, d), lambda i: (i, 0))
    return pl.pallas_call(
        body, name=name, grid=(m // tm,),
        in_specs=[tile, _full(w_down.shape), wide, wide, pl.BlockSpec(memory_space=pl.ANY),
                  pl.BlockSpec(memory_space=pl.ANY), tile, _full(g.shape)],
        out_specs=[wide, wide, tile, _full(g.shape)],
        out_shape=[jax.ShapeDtypeStruct((m, n), bf16)] * 2 + [jax.ShapeDtypeStruct((m, d), f32),
                                                               jax.ShapeDtypeStruct(g.shape, f32)],
        scratch_shapes=[pltpu.VMEM(w_gate_t.shape, w_gate_t.dtype), pltpu.VMEM(w_up_t.shape, w_up_t.dtype),
                        pltpu.SemaphoreType.DMA((2,))],
        compiler_params=_params(("arbitrary",)),
    )(dh, w_down, gate, up, w_gate_t, w_up_t, h, g)


HALO = 16


def _previous_rows(x, before_ref, first_tile):
    rows = lax.broadcasted_iota(jnp.int32, x.shape, 0)
    last = jnp.where(first_tile, 0.0, before_ref[HALO - 1:HALO, :].astype(f32))
    return jnp.where(rows == 0, last, pltpu.roll(x, 1, axis=0))


def _mixer_inputs(ps, mixes, params, *, name):
    m = ps[0].shape[0]
    tm = _tile(m)
    sub = tm // HALO
    n_par = len(params)

    def body(*refs):
        first = pl.program_id(0) == 0
        pf = []
        for k in range(2):
            x = refs[k][...].astype(f32)
            pf.append(x + (_previous_rows(x, refs[2 + k], first) - x) * refs[4 + k][...])
        res = _rwkv_prep(*pf, *[ref[...] for ref in refs[6:6 + n_par]])
        for o_ref, val in zip(refs[6 + n_par:], res):
            o_ref[...] = val

    tile = lambda a: pl.BlockSpec((tm, a.shape[1]), lambda i: (i, 0))
    before = lambda a: pl.BlockSpec((HALO, a.shape[1]), lambda i: (jnp.maximum(i * sub - 1, 0), 0))
    out = pl.BlockSpec((tm, RWKV_DIM), lambda i: (i, 0))
    return pl.pallas_call(
        body, name=name, grid=(m // tm,),
        in_specs=[tile(a) for a in ps] + [before(a) for a in ps] + [_full(a.shape) for a in mixes + params],
        out_specs=[out] * 7, out_shape=[jax.ShapeDtypeStruct((m, RWKV_DIM), f32)] * 7,
        compiler_params=_params(("parallel",)),
    )(*ps, *ps, *mixes, *params)


def _mixer_inputs_bwd(ps, mixes, params, cts, *, name):
    m = ps[0].shape[0]
    tm = _tile(m)
    sub = tm // HALO
    nt = m // tm
    n_par = len(params)
    flat_cts = [c for group in cts for c in group]
    n_ct = len(flat_cts)

    def body(*refs):
        i = pl.program_id(0)
        tile_index = nt - 1 - i
        ct_refs = refs[6 + n_par:6 + n_par + n_ct]
        dp_refs = refs[6 + n_par + n_ct:8 + n_par + n_ct]
        dmix_refs = refs[8 + n_par + n_ct:10 + n_par + n_ct]
        dpar_refs = refs[10 + n_par + n_ct:9 + 2 * n_par + n_ct]
        carries = refs[9 + 2 * n_par + n_ct:]
        rows1 = tile_index * tm + lax.broadcasted_iota(jnp.int32, (tm, 1), 0)
        live = rows1 >= PAD

        @pl.when(i == 0)
        def _():
            for ref in (*dmix_refs, *dpar_refs, *carries):
                ref[...] = jnp.zeros_like(ref)

        xs, prevs, pf = [], [], []
        for k in range(2):
            x = refs[k][...].astype(f32)
            xp = _previous_rows(x, refs[2 + k], tile_index == 0)
            xs.append(x)
            prevs.append(xp)
            pf.append(x + (xp - x) * refs[4 + k][...])
        ct_vals, pos = [], 0
        for group in cts:
            acc = ct_refs[pos][...].astype(f32)
            for extra in range(1, len(group)):
                acc = acc + ct_refs[pos + extra][...].astype(f32)
            pos += len(group)
            ct_vals.append(jnp.where(live, acc, 0.0))
        par_vals = [ref[...] for ref in refs[6:6 + n_par]]
        _, vjp = jax.vjp(lambda *args: _rwkv_prep(*args, par_vals[-1]), *pf, *par_vals[:-1])
        g = vjp(tuple(ct_vals))
        for k in range(2):
            dpf = g[k]
            mixv = refs[4 + k][...]
            dm = dpf * mixv
            rows = lax.broadcasted_iota(jnp.int32, dm.shape, 0)
            dm_next = jnp.where(rows == tm - 1, carries[k][...], pltpu.roll(dm, tm - 1, axis=0))
            dp_refs[k][...] = jnp.where(live, dpf - dm + dm_next, 0.0).astype(dp_refs[k].dtype)
            carries[k][...] = dm[0:1, :]
            dmix_refs[k][...] += jnp.sum(dpf * (prevs[k] - xs[k]), axis=0, keepdims=True)
        for ref, val in zip(dpar_refs, g[2:]):
            ref[...] += val

    tile = lambda a: pl.BlockSpec((tm, a.shape[1]), lambda i: (nt - 1 - i, 0))
    before = lambda a: pl.BlockSpec((HALO, a.shape[1]), lambda i: (jnp.maximum((nt - 1 - i) * sub - 1, 0), 0))
    return pl.pallas_call(
        body, name=name, grid=(nt,),
        in_specs=[tile(a) for a in ps] + [before(a) for a in ps] + [_full(a.shape) for a in mixes + params]
        + [tile(c) for c in flat_cts],
        out_specs=[tile(a) for a in ps] + [_full(a.shape) for a in mixes + params[:-1]],
        out_shape=[jax.ShapeDtypeStruct(a.shape, bf16) for a in ps]
        + [jax.ShapeDtypeStruct(a.shape, f32) for a in mixes + params[:-1]],
        scratch_shapes=[pltpu.VMEM((1, a.shape[1]), f32) for a in ps],
        compiler_params=_params(("arbitrary",)),
    )(*ps, *ps, *mixes, *params, *flat_cts)


def _attn_masks(blk):
    qi = lax.broadcasted_iota(jnp.int32, (BLOCK, BLOCK), 0)
    ki = lax.broadcasted_iota(jnp.int32, (BLOCK, BLOCK), 1)
    qpos = blk * BLOCK + qi - PAD
    kpos_c = blk * BLOCK + ki - PAD
    kpos_p = kpos_c - BLOCK
    kpos_m = ki - PAD

    def band(kpos):
        return (kpos >= N_META) & (kpos <= qpos) & (qpos - kpos < WINDOW)

    return band(kpos_p), band(kpos_c), (kpos_m >= 0) & (kpos_m <= qpos)


def _attn_probs(qs, k3s, sink, oks):
    s = [[jnp.where(ok, _dot(qh, kx, "nt"), NEG_INF) for kx, ok in zip(k3, oks)] for qh, k3 in zip(qs, k3s)]
    mx = [jnp.maximum(jnp.maximum(jnp.max(t[0], -1, keepdims=True), jnp.max(t[1], -1, keepdims=True)),
                      jnp.maximum(jnp.max(t[2], -1, keepdims=True), sk)) for t, sk in zip(s, sink)]
    e = [[jnp.exp(tx - m) for tx in t] for t, m in zip(s, mx)]
    e_sink = [jnp.exp(sk - m) for sk, m in zip(sink, mx)]
    inv = [1.0 / (jnp.sum(t[0], -1, keepdims=True) + jnp.sum(t[1], -1, keepdims=True)
                  + jnp.sum(t[2], -1, keepdims=True) + es) for t, es in zip(e, e_sink)]
    return [[tx * i for tx in t] for t, i in zip(e, inv)], [es * i for es, i in zip(e_sink, inv)]


def _head_cols(i):
    return slice(i * HEAD_DIM, (i + 1) * HEAD_DIM)


def _attn_operands(refs):
    q_ref, kp_ref, kc_ref, km_ref, vp_ref, vc_ref, vm_ref, s_ref = refs
    qs = [q_ref[:, _head_cols(i)] * (HEAD_DIM ** -0.5) for i in range(Q_HEADS)]
    k3 = [[ref[:, _head_cols(h)] for ref in (kp_ref, kc_ref, km_ref)] for h in range(KV_HEADS)]
    v3 = [[ref[:, _head_cols(h)] for ref in (vp_ref, vc_ref, vm_ref)] for h in range(KV_HEADS)]
    return (qs, [k3[i // GROUP] for i in range(Q_HEADS)], [v3[i // GROUP] for i in range(Q_HEADS)],
            [s_ref[:, i:i + 1] for i in range(Q_HEADS)])


def _attention(q, k, v, sinks, *, name):
    lp = q.shape[0]
    nb = lp // BLOCK
    prev = lambda i: (jnp.maximum(i - 1, 0), 0)
    cur = lambda i: (i, 0)
    meta = lambda i: (0, 0)
    kv = lambda index: pl.BlockSpec((BLOCK, KV_W), index)

    def body(*refs):
        o_ref = refs[-1]
        qs, k3s, v3s, sink = _attn_operands(refs[:-1])
        p, _ = _attn_probs(qs, k3s, sink, _attn_masks(pl.program_id(0)))
        out = [_dot(ph[0], v3[0]) + _dot(ph[1], v3[1]) + _dot(ph[2], v3[2]) for ph, v3 in zip(p, v3s)]
        for i in range(Q_HEADS):
            o_ref[:, _head_cols(i)] = out[i].astype(o_ref.dtype)

    return pl.pallas_call(
        body, name=name, grid=(nb,),
        in_specs=[pl.BlockSpec((BLOCK, Q_W), cur), kv(prev), kv(cur), kv(meta), kv(prev), kv(cur), kv(meta),
                  _full((1, Q_HEADS))],
        out_specs=pl.BlockSpec((BLOCK, Q_W), cur),
        out_shape=jax.ShapeDtypeStruct((lp, Q_W), bf16),
        compiler_params=_params(("parallel",)),
    )(q, k, k, k, v, v, v, sinks)


def _attention_bwd(q, k, v, sinks, out, do, *, name):
    lp = q.shape[0]
    nb = lp // BLOCK
    cur = lambda n: (jnp.minimum(n, nb - 1), 0)
    prev = lambda n: (jnp.maximum(jnp.minimum(n, nb - 1) - 1, 0), 0)
    behind = lambda n: (jnp.maximum(n - 1, 0), 0)
    meta = lambda n: (0, 0)
    kv = lambda index: pl.BlockSpec((BLOCK, KV_W), index)
    scale = HEAD_DIM ** -0.5

    def body(*refs):
        ins, fwd_ref, do_ref = refs[:8], refs[8], refs[9]
        dq_ref, dk_ref, dv_ref, dkm_ref, dvm_ref, ds_ref, carry_k, carry_v = refs[10:]
        n = pl.program_id(0)

        @pl.when(n == 0)
        def _():
            for ref in (dkm_ref, dvm_ref, ds_ref, carry_k, carry_v):
                ref[...] = jnp.zeros_like(ref)

        @pl.when(n < nb)
        def _():
            qs, k3s, v3s, sink = _attn_operands(ins)
            do = [do_ref[:, _head_cols(i)] for i in range(Q_HEADS)]
            p, p_sink = _attn_probs(qs, k3s, sink, _attn_masks(n))
            delta = [jnp.sum(d * fwd_ref[:, _head_cols(i)].astype(f32), -1, keepdims=True) for i, d in enumerate(do)]
            dp = [[_dot(d, vx, "nt") for vx in v3] for d, v3 in zip(do, v3s)]
            ds = [[px * (dx - dl) for px, dx in zip(ph, dh)] for ph, dh, dl in zip(p, dp, delta)]
            dq = [_dot(dsh[0], k3[0]) + _dot(dsh[1], k3[1]) + _dot(dsh[2], k3[2]) for dsh, k3 in zip(ds, k3s)]
            for i in range(Q_HEADS):
                dq_ref[:, _head_cols(i)] = dq[i] * scale
                ds_ref[:, i:i + 1] -= jnp.sum(p_sink[i] * delta[i], axis=0, keepdims=True)
            for h in range(KV_HEADS):
                group = slice(h * GROUP, (h + 1) * GROUP)
                q_all = jnp.concatenate(qs[group], axis=0)
                do_all = jnp.concatenate(do[group], axis=0)
                dk3 = [_dot(jnp.concatenate([dsh[x] for dsh in ds[group]], axis=0), q_all, "tn") for x in range(3)]
                dv3 = [_dot(jnp.concatenate([ph[x] for ph in p[group]], axis=0), do_all, "tn") for x in range(3)]
                hs = _head_cols(h)
                for out_ref, carry, meta_ref, d3 in ((dk_ref, carry_k, dkm_ref, dk3),
                                                     (dv_ref, carry_v, dvm_ref, dv3)):
                    out_ref[:, hs] = carry[:, hs] + d3[0]
                    carry[:, hs] = d3[1]
                    meta_ref[:, hs] += d3[2]

        @pl.when(n == nb)
        def _():
            dk_ref[...] = carry_k[...]
            dv_ref[...] = carry_v[...]

    kv_shape = jax.ShapeDtypeStruct((lp, KV_W), f32)
    one_shape = jax.ShapeDtypeStruct((BLOCK, KV_W), f32)
    return pl.pallas_call(
        body, name=name, grid=(nb + 1,),
        in_specs=[pl.BlockSpec((BLOCK, Q_W), cur), kv(prev), kv(cur), kv(meta), kv(prev), kv(cur), kv(meta),
                  _full((1, Q_HEADS)), pl.BlockSpec((BLOCK, Q_W), cur), pl.BlockSpec((BLOCK, Q_W), cur)],
        out_specs=[pl.BlockSpec((BLOCK, Q_W), cur), kv(behind), kv(behind), kv(meta), kv(meta),
                   _full((1, Q_HEADS))],
        out_shape=[jax.ShapeDtypeStruct((lp, Q_W), f32), kv_shape, kv_shape, one_shape, one_shape,
                   jax.ShapeDtypeStruct((1, Q_HEADS), f32)],
        scratch_shapes=[pltpu.VMEM((BLOCK, KV_W), f32), pltpu.VMEM((BLOCK, KV_W), f32)],
        compiler_params=_params(("arbitrary",)),
    )(q, k, k, k, v, v, v, sinks, out, do)


@jax.custom_vjp
def _known_inverse(l, x):
    return x


def _known_inverse_fwd(l, x):
    return x, x


def _known_inverse_bwd(x, ct):
    return _dot(_dot(x, ct, "tn"), x, "nt"), jnp.zeros_like(x)


_known_inverse.defvjp(_known_inverse_fwd, _known_inverse_bwd)


@jax.custom_vjp
def _decayed(x, c):
    return (x * jnp.exp(c)).astype(bf16).astype(f32)


def _decayed_fwd(x, c):
    e = jnp.exp(c)
    out = (x * e).astype(bf16).astype(f32)
    return out, (e, out)


def _decayed_bwd(res, ct):
    e, out = res
    return ct * e, ct * out


_decayed.defvjp(_decayed_fwd, _decayed_bwd)


@jax.custom_vjp
def _pair(x, y):
    return _dot(x, y, "nt")


def _pair_fwd(x, y):
    return _dot(x, y, "nt"), (x, y)


def _pair_bwd(res, ct):
    x, y = res
    hi = ct.astype(bf16)
    lo = (ct - hi.astype(f32)).astype(bf16)
    return _dot(hi, y) + _dot(lo, y), _dot(hi, x, "tn") + _dot(lo, x, "tn")


_pair.defvjp(_pair_fwd, _pair_bwd)


def _scan_chunk(s0, r, lw, k, v, a, b, inv=None):
    t = r[0].shape[0]
    ii = lax.broadcasted_iota(jnp.int32, (t, t), 0)
    jj = lax.broadcasted_iota(jnp.int32, (t, t), 1)
    incl = jj <= ii
    strict = jj < ii
    tri = incl.astype(f32)
    eye = jnp.where(ii == jj, 1.0, 0.0)
    cl = [_const_dot(tri, x) for x in lw]
    mid = [c[t // 2 - 1:t // 2, :] for c in cl]
    s0 = [s * jnp.exp(m) for s, m in zip(s0, mid)]
    cl = [c - m for c, m in zip(cl, mid)]
    rt = [_decayed(x, c) for x, c in zip(r, cl)]
    at = [_decayed(x, c - l) for x, c, l in zip(a, cl, lw)]
    bt = [_decayed(x, -c) for x, c in zip(b, cl)]
    kt = [_decayed(x, -c) for x, c in zip(k, cl)]
    l_ab = [jnp.where(strict, _pair(x, y), 0.0) for x, y in zip(at, bt)]
    l_ak = [jnp.where(strict, _pair(x, y), 0.0) for x, y in zip(at, kt)]
    r_b = [jnp.where(incl, _pair(x, y), 0.0) for x, y in zip(rt, bt)]
    r_k = [jnp.where(incl, _pair(x, y), 0.0) for x, y in zip(rt, kt)]
    if inv is None:
        inv = [eye + x for x in l_ab]
        pw = l_ab
        for _ in range(int(math.log2(t)) - 1):
            pw = [_dot(x, x) for x in pw]
            inv = [x + _dot(x, y) for x, y in zip(inv, pw)]
    else:
        inv = [_known_inverse(x, y) for x, y in zip(l_ab, inv)]
    rhs = [_dot(x, s, "nt") + _dot(m, y) for x, s, m, y in zip(at, s0, l_ak, v)]
    u = [_dot(x, y) for x, y in zip(inv, rhs)]
    y_s = [_dot(x, s, "nt") for x, s in zip(rt, s0)]
    y = [ys + _dot(m, uu) + _dot(n, vv) for ys, m, uu, n, vv in zip(y_s, r_b, u, r_k, v)]
    grow = [s + _dot(uu, x, "tn") + _dot(vv, z, "tn") for s, uu, x, vv, z in zip(s0, u, bt, v, kt)]
    s1 = [g * jnp.exp(c[t - 1:t, :]) for g, c in zip(grow, cl)]
    return y, s1, inv


def _head_rows(h):
    return slice(h * RWKV_HEAD, (h + 1) * RWKV_HEAD)


def _per_head(ref):
    return [ref[:, _head_rows(h)] for h in range(RWKV_HEADS)]


def _scan(r, lw, k, v, a, b, *, name):
    lp = r.shape[0]
    nc = lp // CHUNK
    row = pl.BlockSpec((CHUNK, RWKV_DIM), lambda c: (c, 0))

    def body(r_ref, lw_ref, k_ref, v_ref, a_ref, b_ref, y_ref, s_ref, inv_ref, state):
        @pl.when(pl.program_id(0) == 0)
        def _():
            state[...] = jnp.zeros_like(state)

        s_ref[...] = state[...]
        s0 = [state[_head_rows(h), :] for h in range(RWKV_HEADS)]
        y, s1, inv = _scan_chunk(s0, *[_per_head(ref) for ref in (r_ref, lw_ref, k_ref, v_ref, a_ref, b_ref)])
        for h in range(RWKV_HEADS):
            y_ref[:, _head_rows(h)] = y[h]
            state[_head_rows(h), :] = s1[h]
            inv_ref[h * CHUNK:(h + 1) * CHUNK, :] = inv[h].astype(inv_ref.dtype)

    return pl.pallas_call(
        body, name=name, grid=(nc,), in_specs=[row] * 6,
        out_specs=[row, pl.BlockSpec((RWKV_DIM, RWKV_HEAD), lambda c: (c, 0)),
                   pl.BlockSpec((RWKV_HEADS * CHUNK, CHUNK), lambda c: (c, 0))],
        out_shape=[jax.ShapeDtypeStruct((lp, RWKV_DIM), f32), jax.ShapeDtypeStruct((nc * RWKV_DIM, RWKV_HEAD), f32),
                   jax.ShapeDtypeStruct((nc * RWKV_HEADS * CHUNK, CHUNK), bf16)],
        scratch_shapes=[pltpu.VMEM((RWKV_DIM, RWKV_HEAD), f32)],
        compiler_params=_params(("arbitrary",)),
    )(r, lw, k, v, a, b)


def _scan_bwd(r, lw, k, v, a, b, states, inverses, dy, *, name):
    lp = r.shape[0]
    nc = lp // CHUNK
    back = lambda c: (nc - 1 - c, 0)
    row = pl.BlockSpec((CHUNK, RWKV_DIM), back)

    def body(r_ref, lw_ref, k_ref, v_ref, a_ref, b_ref, s_ref, inv_ref, dy_ref,
             dr_ref, dlw_ref, dk_ref, dv_ref, da_ref, db_ref, dstate):
        @pl.when(pl.program_id(0) == 0)
        def _():
            dstate[...] = jnp.zeros_like(dstate)

        outs = (dr_ref, dlw_ref, dk_ref, dv_ref, da_ref, db_ref)
        s0 = [s_ref[_head_rows(h), :] for h in range(RWKV_HEADS)]
        inv = [inv_ref[h * CHUNK:(h + 1) * CHUNK, :].astype(f32) for h in range(RWKV_HEADS)]
        _, vjp = jax.vjp(lambda *args: _scan_chunk(*args, inv=inv)[:2], s0,
                         *[_per_head(ref) for ref in (r_ref, lw_ref, k_ref, v_ref, a_ref, b_ref)])
        g = vjp((_per_head(dy_ref), [dstate[_head_rows(h), :] for h in range(RWKV_HEADS)]))
        for h in range(RWKV_HEADS):
            dstate[_head_rows(h), :] = g[0][h]
            for o_ref, gv in zip(outs, g[1:]):
                o_ref[:, _head_rows(h)] = gv[h]

    shape = jax.ShapeDtypeStruct((lp, RWKV_DIM), f32)
    return pl.pallas_call(
        body, name=name, grid=(nc,),
        in_specs=[row] * 6 + [pl.BlockSpec((RWKV_DIM, RWKV_HEAD), back),
                              pl.BlockSpec((RWKV_HEADS * CHUNK, CHUNK), back), row],
        out_specs=[row] * 6, out_shape=[shape] * 6,
        scratch_shapes=[pltpu.VMEM((RWKV_DIM, RWKV_HEAD), f32)],
        compiler_params=_params(("arbitrary",)),
    )(r, lw, k, v, a, b, states, inverses, dy)


def _loss_head(act, w_down, h1, target, g_final, *, name):
    lp = h1.shape[0]
    per_tile = 3
    tm = per_tile * BLOCK
    last_block = (lp - FRONT) // BLOCK - 1

    def body(a_ref, w_ref, h_ref, t0_ref, t1_ref, t2_ref, g_ref, loss_ref, dh_ref, dg_ref):
        i = pl.program_id(0)
        loss, dg = 0.0, jnp.zeros(dg_ref.shape, f32)
        for j, t_ref in enumerate((t0_ref, t1_ref, t2_ref)):
            rows = slice(j * BLOCK, (j + 1) * BLOCK)
            real = i * tm + j * BLOCK + lax.broadcasted_iota(jnp.int32, (BLOCK, 1), 0) >= FRONT

            def block_loss(hv, gv, t_ref=t_ref, real=real):
                err = _rms(hv, gv) - t_ref[...]
                return 0.5 * jnp.sum(jnp.where(real, jnp.mean(err * err, axis=-1, keepdims=True), 0.0))

            h2 = _dot(a_ref[rows, :], w_ref[...], "nn") + h_ref[rows, :]
            loss_j, (dh, dg_j) = jax.value_and_grad(block_loss, argnums=(0, 1))(h2, g_ref[...])
            dh_ref[rows, :] = dh
            loss, dg = loss + loss_j, dg + dg_j

        @pl.when(i == 0)
        def _():
            loss_ref[...] = jnp.zeros_like(loss_ref)
            dg_ref[...] = jnp.zeros_like(dg_ref)

        loss_ref[...] += jnp.full(loss_ref.shape, loss, f32)
        dg_ref[...] += dg

    def target_block(j):
        return pl.BlockSpec((BLOCK, D_MODEL),
                            lambda i: (jnp.clip(per_tile * i + j - FRONT // BLOCK, 0, last_block), 0))

    return pl.pallas_call(
        body, name=name, grid=(lp // tm,),
        in_specs=[pl.BlockSpec((tm, act.shape[1]), lambda i: (i, 0)), _full(w_down.shape),
                  pl.BlockSpec((tm, D_MODEL), lambda i: (i, 0)), target_block(0), target_block(1), target_block(2),
                  _full(g_final.shape)],
        out_specs=[_full((8, 128)), pl.BlockSpec((tm, D_MODEL), lambda i: (i, 0)), _full(g_final.shape)],
        out_shape=[jax.ShapeDtypeStruct((8, 128), f32), jax.ShapeDtypeStruct((lp, D_MODEL), f32),
                   jax.ShapeDtypeStruct(g_final.shape, f32)],
        compiler_params=_params(("arbitrary",)),
    )(act, w_down, h1, target, target, target, g_final)


def _embed_norm(x, meta, g, *, name):
    seq = x.shape[0]
    lp = seq + FRONT
    per_tile = 3
    tm = per_tile * BLOCK
    last_block = seq // BLOCK - 1

    def body(x0_ref, x1_ref, x2_ref, meta_ref, g_ref, h_ref, u_ref):
        front = jnp.concatenate([jnp.zeros((PAD, D_MODEL), f32), meta_ref[...]], axis=0)
        first = jnp.where(pl.program_id(0) == 0, front, x0_ref[...])
        h = jnp.concatenate([first, x1_ref[...], x2_ref[...]], axis=0)
        h_ref[...] = h
        u_ref[...] = _rms(h, g_ref[...]).astype(u_ref.dtype)

    def x_block(j):
        return pl.BlockSpec((BLOCK, D_MODEL),
                            lambda i: (jnp.clip(per_tile * i + j - FRONT // BLOCK, 0, last_block), 0))

    tile = pl.BlockSpec((tm, D_MODEL), lambda i: (i, 0))
    return pl.pallas_call(
        body, name=name, grid=(lp // tm,),
        in_specs=[x_block(0), x_block(1), x_block(2), _full(meta.shape), _full(g.shape)],
        out_specs=[tile, tile],
        out_shape=[jax.ShapeDtypeStruct((lp, D_MODEL), f32), jax.ShapeDtypeStruct((lp, D_MODEL), bf16)],
        compiler_params=_params(("parallel",)),
    )(x, x, x, meta, g)


def _input_norm_bwd(h0, g, du, dh1, *, name):
    lp = h0.shape[0]
    blocks = (lp - FRONT) // FRONT
    per_tile = max(n for n in (4, 3, 2, 1) if blocks % n == 0)
    ins = (h0, du, dh1)

    def body(*refs):
        tiles = [refs[k * per_tile:(k + 1) * per_tile] for k in range(len(ins))]
        front_refs = refs[len(ins) * per_tile:len(ins) * (per_tile + 1)]
        g_ref, dx_ref, front_ref, dg_ref = refs[len(ins) * (per_tile + 1):]

        def cotangents(h_ref, du_ref, dh1_ref):
            _, vjp = jax.vjp(lambda hv, gv: (_rms(hv, gv), hv), h_ref[...], g_ref[...])
            return vjp((du_ref[...].astype(f32), dh1_ref[...]))

        @pl.when(pl.program_id(0) == 0)
        def _():
            front_ref[...], dg_ref[...] = cotangents(*front_refs)

        dg = jnp.zeros(dg_ref.shape, f32)
        for j in range(per_tile):
            dh, dg_j = cotangents(*(t[j] for t in tiles))
            dx_ref[j * FRONT:(j + 1) * FRONT, :] = dh
            dg = dg + dg_j
        dg_ref[...] += dg

    def block(j):
        return pl.BlockSpec((FRONT, D_MODEL), lambda i: (per_tile * i + j + 1, 0))

    first = pl.BlockSpec((FRONT, D_MODEL), lambda i: (0, 0))
    return pl.pallas_call(
        body, name=name, grid=(blocks // per_tile,),
        in_specs=[block(j) for _ in ins for j in range(per_tile)] + [first] * len(ins) + [_full(g.shape)],
        out_specs=[pl.BlockSpec((per_tile * FRONT, D_MODEL), lambda i: (i, 0)), _full((FRONT, D_MODEL)),
                   _full(g.shape)],
        out_shape=[jax.ShapeDtypeStruct((lp - FRONT, D_MODEL), f32), jax.ShapeDtypeStruct((FRONT, D_MODEL), f32),
                   jax.ShapeDtypeStruct(g.shape, f32)],
        compiler_params=_params(("arbitrary",)),
    )(*(a for a in ins for _ in range(per_tile)), *ins, g)


def _local_step(x, target, meta, p, early_weights=None, late_weights=None, emit=None):
    emit = emit or (lambda group, grads: 0.0)
    seq = x.shape[0]
    lp = seq + FRONT
    cos_t, sin_t, swap = _rope_tables(lp)
    hsum = _head_sum_matrix(RWKV_DIM, RWKV_HEAD)
    hmean = hsum / RWKV_HEAD
    post_params = [p["ln_w"], p["ln_b"], p["r_k"], hmean]

    h0, u = _embed_norm(x, meta, p["norm_mix_g"], name="norm_mix")
    if early_weights is not None:
        p = {**p, **early_weights(u)}
    prep_params = [p["w0"], p["w2"], p["a0"], p["a2"], p["g2"], p["k_k"], p["k_a"], hsum]
    in_widths = [ATTN_PROJ, RKV_W, LORA_W, 2 * D_MODEL]
    q, k, v, p_rkv, p_lora, gates = _proj_in(u, p["w_in_lr"], p["b_in"], in_widths,
                                             (cos_t, sin_t, swap), name="proj_in", zero_rows_below=PAD)
    y_attn = _attention(q, k, v, p["sinks"], name="attention")

    mix_rkv, mix_lora = p["mix"][:, :RKV_W], p["mix"][:, RKV_W:]
    r_, lw_, k_, v_, a_, b_, g_ = _mixer_inputs([p_rkv, p_lora], [mix_rkv, mix_lora], prep_params,
                                                name="mixer_inputs")
    y_scan, states, inverses = _scan(r_, lw_, k_, v_, a_, b_, name="wkv_scan")
    (y_rwkv,) = _rowwise(_rwkv_post, [y_scan, r_, k_, v_, g_], post_params, [(RWKV_DIM, bf16)], name="rwkv_post")

    if late_weights is not None:
        p = {**p, **late_weights(y_rwkv)}
    br_a, br_r, merged = _branch_merge(y_attn, y_rwkv, p["w_br_attn_t"], p["w_br_rwkv_t"], gates, name="branch_merge")
    h1, f = _residual_norm(merged, p["w_o"], h0, p["norm_ffn_g"], name="out_proj")
    gate, up, act = _ffn_in(f, p["w_gate_t"], p["w_up_t"], name="ffn_in")

    loss8, dh2, d_final_g = _loss_head(act, p["w_down"], h1, target, p["norm_final_g"], name="loss_head")
    dgate, dup, dh1, d_ffn_g = _ffn_bwd(dh2, p["w_down"], gate, up, p["w_gate_t"], p["w_up_t"], h1, p["norm_ffn_g"],
                                        name="ffn_bwd")
    d_w_down = _mm_tn(act, dh2, name="dw_down")
    d_w_gate_t = _mm_tn(dgate, f, name="dw_gate")
    d_w_up_t = _mm_tn(dup, f, name="dw_up")
    zero = emit("ffn", dict(w_down=d_w_down, w_gate_t=d_w_gate_t, w_up_t=d_w_up_t))
    dgates, dbr_a, dbr_r, dy_attn, dy_rwkv = _branch_merge_bwd(
        dh1, p["w_o"], gates, br_a, br_r, p["w_br_attn_t"], p["w_br_rwkv_t"], name="branch_merge_bwd")
    d_w_o = _mm_tn(merged, dh1, name="dw_o")
    d_w_br_attn_t = _mm_tn(dbr_a, y_attn, name="dw_br_attn")
    d_w_br_rwkv_t = _mm_tn(dbr_r, y_rwkv, name="dw_br_rwkv")
    zero = zero + emit("branch", dict(w_o=d_w_o, w_br_attn_t=d_w_br_attn_t, w_br_rwkv_t=d_w_br_rwkv_t))

    post_params = [p["ln_w"] + zero, p["ln_b"], p["r_k"], hmean]
    res = _rowwise_bwd(_rwkv_post, [y_scan, r_, k_, v_, g_], post_params, [[dy_rwkv]], name="rwkv_post_bwd",
                       diff_rows=[True] * 5, diff_params=[True, True, True, False])
    dy_scan, dr_p, dk_p, dv_p, dg_p, d_ln_w, d_ln_b, d_r_k = res
    dr_s, dlw_s, dk_s, dv_s, da_s, db_s = _scan_bwd(r_, lw_, k_, v_, a_, b_, states, inverses, dy_scan,
                                                    name="wkv_scan_bwd")
    res = _mixer_inputs_bwd([p_rkv, p_lora], [mix_rkv, mix_lora], prep_params,
                            [[dr_s, dr_p], [dlw_s], [dk_s, dk_p], [dv_s, dv_p], [da_s], [db_s], [dg_p]],
                            name="mixer_inputs_bwd")
    dp_rkv, dp_lora, d_mix_rkv, d_mix_lora, d_w0, d_w2, d_a0, d_a2, d_g2, d_k_k, d_k_a = res

    dq, dk, dv, dkm, dvm, d_sinks = _attention_bwd(q, k, v, p["sinks"], y_attn, dy_attn, name="attention_bwd")
    (dqkv,) = _rowwise(_attn_prep_transposed, [dq, dk, dv, cos_t, sin_t], [swap, dkm, dvm], [(ATTN_PROJ, bf16)],
                       name="attn_prep_bwd")

    in_rows, (at_qkv, at_rkv, at_lora, at_gates) = p["w_in_lr"].shape[1], [off for off, _ in _pieces(in_widths)]
    d_w_in_t, db_gates = _mm_tn(dgates, u, name="dw_gates", colsum=True, into=(in_rows, at_gates, None))
    d_w_in_t, db_rkv = _mm_tn(dp_rkv, u, name="dw_rkv", colsum=True, into=(in_rows, at_rkv, d_w_in_t))
    d_w_in_t, db_lora = _mm_tn(dp_lora, u, name="dw_lora", colsum=True, into=(in_rows, at_lora, d_w_in_t))
    d_w_in_t, db_qkv = _mm_tn(dqkv, u, name="dw_qkv", colsum=True, into=(in_rows, at_qkv, d_w_in_t))
    zero = emit("input", dict(w_in_t=d_w_in_t, g2=d_g2, w2=d_w2, a2=d_a2))
    du = _proj_in_bwd([dqkv, dp_rkv, dp_lora, dgates], p["w_in_lr"], name="d_u")
    dx, d_front, d_mix_g = _input_norm_bwd(h0, p["norm_mix_g"] + zero, du, dh1, name="norm_mix_bwd")

    grads = dict(
        w_in_t=d_w_in_t,
        b_in=jnp.concatenate([db_qkv, db_rkv, db_lora, db_gates], axis=1),
        mix=jnp.concatenate([d_mix_rkv, d_mix_lora], axis=1),
        norm_mix_g=d_mix_g, sinks=d_sinks, w0=d_w0, w2=d_w2, a0=d_a0, a2=d_a2, g2=d_g2, k_k=d_k_k, k_a=d_k_a,
        r_k=d_r_k, ln_w=d_ln_w, ln_b=d_ln_b, w_br_attn_t=d_w_br_attn_t, w_br_rwkv_t=d_w_br_rwkv_t, w_o=d_w_o,
        norm_ffn_g=d_ffn_g, w_gate_t=d_w_gate_t, w_up_t=d_w_up_t, w_down=d_w_down, norm_final_g=d_final_g,
        meta=d_front[PAD:],
    )
    return loss8[0, 0], dx, grads


def _position():
    return lax.axis_index("x"), lax.axis_index("y"), lax.axis_index("c")


def _other_chips(x, y):
    return [(1 - x, y), (x, 1 - y), (1 - x, 1 - y)]


_HBM = pl.BlockSpec(memory_space=pltpu.HBM)
_SEM = pl.BlockSpec(memory_space=pltpu.SEMAPHORE)
_EFFECT = pltpu.SideEffectType.DATAFLOW_SIDE_EFFECTING


def _landing_zone(src, kind):
    shape = {"whole": (N_CHIPS,) + src.shape, "half": (2, N_CHIPS, src.shape[0], src.shape[1] // 2),
             "slab": (3,) + src.shape[1:], "sibling": src.shape, "all": (N_DEV - 1,) + src.shape}[kind]
    return lax.empty(shape, src.dtype)


def _copies_per_source(kind):
    return {"sibling": 1, "all": N_DEV - 1}.get(kind, 3)


def _chip_copies(src_refs, land_refs, send_sems, recv_sems, kind):
    x, y, c = _position()
    if kind == "all":
        copies = []
        for a, (src, land) in enumerate(zip(src_refs, land_refs)):
            for rel in range(1, N_DEV):
                peer = ((1 - x) if rel & 4 else x, (1 - y) if rel & 2 else y, (1 - c) if rel & 1 else c)
                k = (N_DEV - 1) * a + rel - 1
                copies.append(pltpu.make_async_remote_copy(
                    src_ref=src, dst_ref=land.at[rel - 1], send_sem=send_sems.at[k], recv_sem=recv_sems.at[k],
                    device_id=peer, device_id_type=MESH))
        return copies
    if kind == "sibling":
        return [pltpu.make_async_remote_copy(
            src_ref=src, dst_ref=land, send_sem=send_sems.at[a], recv_sem=recv_sems.at[a],
            device_id=(x, y, 1 - c), device_id_type=MESH) for a, (src, land) in enumerate(zip(src_refs, land_refs))]
    copies = []
    for a, (src, land) in enumerate(zip(src_refs, land_refs)):
        for j, (px, py) in enumerate(_other_chips(x, y)):
            if kind == "whole":
                src_ref, dst_ref = src, land.at[2 * x + y]
            elif kind == "half":
                half = src.shape[1] // 2
                src_ref, dst_ref = src.at[:, pl.ds(pl.multiple_of(c * half, half), half)], land.at[c, 2 * x + y]
            else:
                src_ref, dst_ref = src.at[2 * px + py], land.at[j]
            copies.append(pltpu.make_async_remote_copy(
                src_ref=src_ref, dst_ref=dst_ref, send_sem=send_sems.at[3 * a + j], recv_sem=recv_sems.at[3 * a + j],
                device_id=(px, py, c), device_id_type=MESH))
    return copies


def _exchange_start(srcs, *, kind, name):
    n = len(srcs)
    lands = [_landing_zone(s, kind) for s in srcs]

    def body(*refs):
        for cp in _chip_copies(refs[:n], refs[n:2 * n], refs[2 * n], refs[2 * n + 1], kind):
            cp.start()
        refs[-1][...] = jnp.zeros_like(refs[-1])

    res = pl.pallas_call(
        body, name=name,
        out_shape=(pltpu.SemaphoreType.DMA((_copies_per_source(kind) * n,)),
                   pltpu.SemaphoreType.DMA((_copies_per_source(kind) * n,)),
                   *[pltpu.HBM(a.shape, a.dtype) for a in srcs + lands], jax.ShapeDtypeStruct((8, 128), f32)),
        in_specs=[_HBM] * (2 * n),
        out_specs=(_SEM, _SEM, *[_HBM] * (2 * n), pl.BlockSpec(memory_space=pltpu.VMEM)),
        input_output_aliases={i: 2 + i for i in range(2 * n)},
        compiler_params=pltpu.CompilerParams(has_side_effects=_EFFECT),
    )(*[pltpu.with_memory_space_constraint(a, pltpu.HBM) for a in srcs + lands])
    return res[0], res[1], list(res[2:2 + n]), list(res[2 + n:2 + 2 * n]), res[-1]


def _exchange_wait(handle, after, *, kind, name):
    send_sems, recv_sems, srcs, lands, _ = handle
    n = len(srcs)

    def body(*refs):
        for cp in _chip_copies(refs[:n], refs[n:2 * n], refs[2 * n], refs[2 * n + 1], kind):
            cp.wait_send()
            cp.wait_recv()

    res = pl.pallas_call(
        body, name=name,
        out_shape=tuple(pltpu.HBM(a.shape, a.dtype) for a in srcs + lands),
        in_specs=[_HBM] * (2 * n) + [_SEM, _SEM, pl.BlockSpec(memory_space=pl.ANY)],
        out_specs=tuple([_HBM] * (2 * n)),
        input_output_aliases={i: i for i in range(2 * n)},
        compiler_params=pltpu.CompilerParams(has_side_effects=_EFFECT),
    )(*srcs, *lands, send_sems, recv_sems, after)
    return list(res[:n]), list(res[n:])


def _sum_own_and_received(g, recv, *, name):
    _, r, w = g.shape
    tm = _tile(r)
    if g.dtype == bf16 and tm % 16:
        tm = r
    x, y, _ = _position()
    me = jnp.reshape(2 * x + y, (1,)).astype(jnp.int32)

    def body(me_ref, g_ref, r_ref, o_ref):
        o_ref[...] = (g_ref[0].astype(f32) + r_ref[0].astype(f32)) + (r_ref[1].astype(f32) + r_ref[2].astype(f32))

    return pl.pallas_call(
        body, name=name,
        grid_spec=pltpu.PrefetchScalarGridSpec(
            num_scalar_prefetch=1, grid=(r // tm,),
            in_specs=[pl.BlockSpec((1, tm, w), lambda i, me_ref: (me_ref[0], i, 0)),
                      pl.BlockSpec((3, tm, w), lambda i, me_ref: (0, i, 0))],
            out_specs=pl.BlockSpec((tm, w), lambda i, me_ref: (i, 0))),
        out_shape=jax.ShapeDtypeStruct((r, w), f32),
        compiler_params=_params(("parallel",)),
    )(me, g, recv)


def _swap_halves(zone, *, name):
    def body(z_ref, o_ref, send_sems, recv_sems):
        x, y, c = _position()
        mine = [pltpu.make_async_remote_copy(
            src_ref=o_ref.at[c, 2 * px + py], dst_ref=o_ref.at[c, 2 * px + py], send_sem=send_sems.at[j],
            recv_sem=recv_sems.at[j], device_id=(x, y, 1 - c), device_id_type=MESH)
            for j, (px, py) in enumerate(_other_chips(x, y))]
        for cp in mine:
            cp.start()
        for j, (px, py) in enumerate(_other_chips(x, y)):
            pltpu.make_async_remote_copy(
                src_ref=o_ref.at[c, 2 * px + py], dst_ref=o_ref.at[1 - c, 2 * px + py], send_sem=send_sems.at[j],
                recv_sem=recv_sems.at[j], device_id=(x, y, 1 - c), device_id_type=MESH).wait_recv()
        for cp in mine:
            cp.wait_send()

    return pl.pallas_call(
        body, name=name,
        in_specs=[pl.BlockSpec(memory_space=pl.ANY)], out_specs=pl.BlockSpec(memory_space=pl.ANY),
        out_shape=jax.ShapeDtypeStruct(zone.shape, zone.dtype), input_output_aliases={0: 0},
        scratch_shapes=[pltpu.SemaphoreType.DMA((3,)), pltpu.SemaphoreType.DMA((3,))],
    )(zone)


def _sum_all_devices(own, received, *, name):
    def body(own_ref, got_ref, o_ref):
        x, y, c = _position()
        me = 4 * x + 2 * y + c
        acc = None
        for d in range(N_DEV):
            rel = jnp.bitwise_xor(me, d)
            block = jnp.where(rel == 0, own_ref[...], got_ref[jnp.maximum(rel, 1) - 1])
            acc = block if acc is None else acc + block
        o_ref[...] = acc

    return pl.pallas_call(
        body, name=name,
        in_specs=[pl.BlockSpec(memory_space=pltpu.VMEM)] * 2, out_specs=pl.BlockSpec(memory_space=pltpu.VMEM),
        out_shape=jax.ShapeDtypeStruct(own.shape, f32),
    )(own, received)


def _adam_math(w, g, m, v):
    nm = ADAM_B1 * m + (1.0 - ADAM_B1) * g
    nv = ADAM_B2 * v + (1.0 - ADAM_B2) * (g * g)
    m_hat = nm / (1.0 - ADAM_B1 ** ADAM_STEP)
    v_hat = nv / (1.0 - ADAM_B2 ** ADAM_STEP)
    return -ADAM_LR * (m_hat / (jnp.sqrt(v_hat) + ADAM_EPS) + ADAM_WD * w), nm, nv


def _adamw(w, g_parts, m, v, *, name, transposed=False, after=None):
    rows, cols = w.shape
    ordered = after is not None
    if transposed:
        tm = 256 if rows % 256 == 0 else rows
        g_spec = pl.BlockSpec((cols, tm), lambda i: (0, i))
    else:
        tm = _tile(rows, 256)
        g_spec = pl.BlockSpec((tm, cols), lambda i: (i, 0))
    n = len(g_parts)

    def body(*refs):
        refs = refs[1:] if ordered else refs
        w_ref, m_ref, v_ref = refs[0], refs[1 + n], refs[2 + n]
        g_ref, d_ref, nm_ref, nv_ref = refs[3 + n:]
        gv = refs[1][...]
        for part in refs[2:1 + n]:
            gv = gv + part[...]
        if transposed:
            gv = gv.T
        g_ref[...] = gv
        d_ref[...], nm_ref[...], nv_ref[...] = _adam_math(w_ref[...], gv, m_ref[...], v_ref[...])

    spec = pl.BlockSpec((tm, cols), lambda i: (i, 0))
    shape = jax.ShapeDtypeStruct((rows, cols), f32)
    return pl.pallas_call(
        body, name=name, grid=(rows // tm,),
        in_specs=[pl.BlockSpec(memory_space=pl.ANY)] * ordered + [spec] + [g_spec] * n + [spec] * 2,
        out_specs=[spec] * 4, out_shape=[shape] * 4,
        compiler_params=_params(("parallel",)),
    )(*([after] if ordered else []), w, *g_parts, m, v)


def _pad_rows(a, rows):
    return jnp.concatenate([a, jnp.zeros((rows - a.shape[0], a.shape[1]), a.dtype)], axis=0) if rows > a.shape[0] else a


_SMALL = (("norm_mix_g", D_MODEL), ("b_in", D_IN), ("sinks", Q_HEADS), ("mix", RWKV_PROJ), ("w0", RWKV_DIM),
          ("a0", RWKV_DIM), ("k_k", RWKV_DIM), ("k_a", RWKV_DIM), ("r_k", RWKV_DIM), ("ln_w", RWKV_DIM),
          ("ln_b", RWKV_DIM), ("norm_ffn_g", D_MODEL), ("norm_final_g", D_MODEL))


LANES = 128


def _small_layout():
    out, off = {}, 0
    for n, size in _SMALL + (("loss", 1),):
        pieces, col = [], 0
        while col < size:
            row, lane = divmod(off + col, PACK_W)
            width = min(size - col, PACK_W - lane)
            pieces.append((row, lane, width, col))
            col += width
        out[n] = pieces
        off += -(-size // LANES) * LANES
    return out, -(-off // PACK_W)


def _pack_small(d, loss):
    layout, rows = _small_layout()
    parts, used = [], 0
    for n, size in _SMALL + (("loss", 1),):
        item = loss if n == "loss" else d[n]
        fill = -size % LANES
        parts += [item.reshape(-1).astype(f32), jnp.zeros((fill,), f32)]
        used += size + fill
    parts.append(jnp.zeros((rows * PACK_W - used,), f32))
    return jnp.concatenate(parts).reshape(rows, PACK_W)


def _adamw_small(packed, first_row, ws, ms, vs, meta, *, name):
    layout, _ = _small_layout()
    names = [n for n, _ in _SMALL]
    k = len(names)

    def body(*refs):
        packed_ref = refs[0]
        w_refs, m_refs, v_refs = refs[1:1 + k], refs[1 + k:1 + 2 * k], refs[1 + 2 * k:1 + 3 * k]
        meta_refs = refs[1 + 3 * k:5 + 3 * k]
        outs = refs[5 + 3 * k:]
        for idx, n in enumerate(names):
            for row, lane, width, col in layout[n]:
                gv = packed_ref[first_row + row:first_row + row + 1, lane:lane + width]
                at = (slice(None), slice(col, col + width))
                new = _adam_math(w_refs[idx][at], gv, m_refs[idx][at], v_refs[idx][at])
                for o_ref, val in zip(outs[4 * idx:4 * idx + 4], (gv,) + new):
                    o_ref[at] = val
        for o_ref, val in zip(outs[4 * k:], _adam_math(*(r[...] for r in meta_refs))):
            o_ref[...] = val

    ins = [packed] + [d[n] for d in (ws, ms, vs) for n in names] + list(meta)
    shapes = [jax.ShapeDtypeStruct(ws[n].shape, f32) for n in names for _ in range(4)]
    shapes += [jax.ShapeDtypeStruct(meta[0].shape, f32)] * 3
    res = pl.pallas_call(
        body, name=name, grid=(1,), in_specs=[_full(a.shape) for a in ins],
        out_specs=[_full(s.shape) for s in shapes], out_shape=shapes,
        compiler_params=_params(("arbitrary",)),
    )(*ins)
    return {n: res[4 * i:4 * i + 4] for i, n in enumerate(names)}, res[4 * k:]


def kernel(x, meta_tokens, norm_mix_g, w_in, b_in, attn_sinks, rwkv_mix, rwkv_w0, rwkv_w2, rwkv_a0, rwkv_a2, rwkv_g2, rwkv_k_k, rwkv_k_a, rwkv_r_k, rwkv_ln_w, rwkv_ln_b, w_br_attn, w_br_rwkv, w_o, norm_ffn_g, w_ffn_gate, w_ffn_up, w_ffn_down, norm_final_g, loss_target, m_meta_tokens, m_norm_mix_g, m_w_in, m_b_in, m_attn_sinks, m_rwkv_mix, m_rwkv_w0, m_rwkv_w2, m_rwkv_a0, m_rwkv_a2, m_rwkv_g2, m_rwkv_k_k, m_rwkv_k_a, m_rwkv_r_k, m_rwkv_ln_w, m_rwkv_ln_b, m_w_br_attn, m_w_br_rwkv, m_w_o, m_norm_ffn_g, m_w_ffn_gate, m_w_ffn_up, m_w_ffn_down, m_norm_final_g, v_meta_tokens, v_norm_mix_g, v_w_in, v_b_in, v_attn_sinks, v_rwkv_mix, v_rwkv_w0, v_rwkv_w2, v_rwkv_a0, v_rwkv_a2, v_rwkv_g2, v_rwkv_k_k, v_rwkv_k_a, v_rwkv_r_k, v_rwkv_ln_w, v_rwkv_ln_b, v_w_br_attn, v_w_br_rwkv, v_w_o, v_norm_ffn_g, v_w_ffn_gate, v_w_ffn_up, v_w_ffn_down, v_norm_final_g):
    names = ("meta_tokens", "norm_mix_g", "w_in", "b_in", "attn_sinks", "rwkv_mix", "rwkv_w0", "rwkv_w2", "rwkv_a0",
             "rwkv_a2", "rwkv_g2", "rwkv_k_k", "rwkv_k_a", "rwkv_r_k", "rwkv_ln_w", "rwkv_ln_b", "w_br_attn",
             "w_br_rwkv", "w_o", "norm_ffn_g", "w_ffn_gate", "w_ffn_up", "w_ffn_down", "norm_final_g")
    w_all = dict(zip(names, (meta_tokens, norm_mix_g, w_in, b_in, attn_sinks, rwkv_mix, rwkv_w0, rwkv_w2, rwkv_a0,
                             rwkv_a2, rwkv_g2, rwkv_k_k, rwkv_k_a, rwkv_r_k, rwkv_ln_w, rwkv_ln_b, w_br_attn,
                             w_br_rwkv, w_o, norm_ffn_g, w_ffn_gate, w_ffn_up, w_ffn_down, norm_final_g)))
    m_all = dict(zip(names, (m_meta_tokens, m_norm_mix_g, m_w_in, m_b_in, m_attn_sinks, m_rwkv_mix, m_rwkv_w0,
                             m_rwkv_w2, m_rwkv_a0, m_rwkv_a2, m_rwkv_g2, m_rwkv_k_k, m_rwkv_k_a, m_rwkv_r_k,
                             m_rwkv_ln_w, m_rwkv_ln_b, m_w_br_attn, m_w_br_rwkv, m_w_o, m_norm_ffn_g, m_w_ffn_gate,
                             m_w_ffn_up, m_w_ffn_down, m_norm_final_g)))
    v_all = dict(zip(names, (v_meta_tokens, v_norm_mix_g, v_w_in, v_b_in, v_attn_sinks, v_rwkv_mix, v_rwkv_w0,
                             v_rwkv_w2, v_rwkv_a0, v_rwkv_a2, v_rwkv_g2, v_rwkv_k_k, v_rwkv_k_a, v_rwkv_r_k,
                             v_rwkv_ln_w, v_rwkv_ln_b, v_w_br_attn, v_w_br_rwkv, v_w_o, v_norm_ffn_g, v_w_ffn_gate,
                             v_w_ffn_up, v_w_ffn_down, v_norm_final_g)))
    cx, cy, _ = _position()
    chip = 2 * cx + cy

    t_of = dict(w_in_t="w_in", w_gate_t="w_ffn_gate", w_up_t="w_ffn_up", w_br_attn_t="w_br_attn",
                w_br_rwkv_t="w_br_rwkv", g2_t="rwkv_g2", w2_t="rwkv_w2", a2_t="rwkv_a2")
    plain_of = dict(w_down="w_ffn_down", w_o="w_o")
    meta_cols = meta_tokens.shape[1]

    def shard(k):
        return (w_all[t_of[k]][0].T if k in t_of else w_all[plain_of[k]][0]).astype(bf16)

    def whole(zone, own):
        return lax.dynamic_update_slice_in_dim(zone, own[None], chip, axis=0).reshape(-1, own.shape[-1])

    tiny = ("g2_t", "w2_t", "a2_t")
    late = ("w_gate_t", "w_up_t", "w_down", "w_o", "w_br_attn_t", "w_br_rwkv_t")
    w_in_own = shard("w_in_t")
    w_in_rows, w_in_cols = w_in_own.shape
    tiny_h = _exchange_start([shard(k) for k in tiny] + [meta_tokens], kind="whole", name="gather_tiny_start")
    w_in_h = _exchange_start([w_in_own + tiny_h[4][0, 0].astype(bf16)], kind="half", name="gather_w_in_start")
    behind = w_in_h[4][0, 0].astype(bf16)
    late_h = _exchange_start([shard(k) + behind for k in late], kind="whole", name="gather_late_start")
    own, zones = _exchange_wait(tiny_h, late_h[4], kind="whole", name="gather_tiny_wait")
    got = {k: whole(z, o) for k, z, o in zip(tiny, zones, own)}
    meta_full = whole(zones[-1], own[-1]).reshape(N_CHIPS, N_META, meta_cols).transpose(1, 0, 2).reshape(N_META, -1)
    p = dict(
        g2=got["g2_t"].T.astype(f32), w2=got["w2_t"].T.astype(f32), a2=got["a2_t"].T.astype(f32),
        b_in=b_in, sinks=attn_sinks, mix=rwkv_mix, w0=rwkv_w0, a0=rwkv_a0, k_k=rwkv_k_k, k_a=rwkv_k_a,
        r_k=rwkv_r_k.reshape(1, RWKV_DIM), ln_w=rwkv_ln_w, ln_b=rwkv_ln_b, norm_mix_g=norm_mix_g,
        norm_ffn_g=norm_ffn_g, norm_final_g=norm_final_g.reshape(1, D_MODEL),
    )

    def early_weights(after):
        own_h, zones_h = _exchange_wait(w_in_h, after, kind="half", name="gather_w_in_wait")
        zone = _swap_halves(zones_h[0], name="swap_w_in_halves")
        own_halves = own_h[0].reshape(w_in_rows, 2, w_in_cols // 2).transpose(1, 0, 2)[:, None]
        zone = lax.dynamic_update_slice(zone, own_halves, (0, chip, 0, 0))
        return dict(w_in_lr=zone.reshape(2, N_CHIPS * w_in_rows, w_in_cols // 2))

    def late_weights(after):
        own_l, zones_l = _exchange_wait(late_h, after, kind="whole", name="gather_late_wait")
        return {k: whole(z, o) for k, z, o in zip(late, zones_l, own_l)}

    started = {}

    def partial_sums(groups, after):
        parts = {}
        for group in groups:
            keys, handle = started[group]
            slabs, lands = _exchange_wait(handle, after, kind="slab", name="scatter_" + group + "_wait")
            parts.update({k: _sum_own_and_received(s, l, name="sum_chips_" + k) for k, s, l in zip(keys, slabs, lands)})
        return parts

    def emit(group, grads_):
        keys = list(grads_)
        slabs = []
        for k in keys:
            a = grads_[k].T if k in ("g2", "w2", "a2") else grads_[k]
            slabs.append(a.reshape(N_CHIPS, a.shape[0] // N_CHIPS, a.shape[1]))
        started[group] = (keys, _exchange_start(slabs, kind="slab", name="scatter_" + group + "_start"))
        zero = started[group][1][4]
        if group == "input":
            started["parts_a"] = partial_sums(("ffn", "branch"), zero)
            started["swap_a"] = _exchange_start(list(started["parts_a"].values()), kind="sibling",
                                                name="swap_cores_a_start")
            zero = started["swap_a"][4]
        return zero[0, 0]

    loss, dx, g = _local_step(x[0], loss_target[0], meta_full, p, early_weights, late_weights, emit)

    grads, delta, new_m, new_v = {}, {}, {}, {}
    in_grad_layout = ("w_in_t", "w_gate_t", "w_up_t")
    weight_of = {**t_of, **plain_of}

    def update(keys, mine, theirs, after=None):
        for k, part, other in zip(keys, mine, theirs):
            both = [part, other]
            k = k + "_t" if k in ("g2", "w2", "a2") else k
            n = weight_of[k]
            shape2 = w_all[n].shape[1:]
            w_, m_, v_ = (a.reshape(shape2) for a in (w_all[n], m_all[n], v_all[n]))
            if k in in_grad_layout:
                raw = _adamw(w_.T, both, m_.T, v_.T, name="adamw_" + n, after=after)
                res = [t.T for t in raw]
            else:
                res = raw = _adamw(w_, both, m_, v_, name="adamw_" + n, transposed=k in t_of, after=after)
            grads[n], delta[n], new_m[n], new_v[n] = (t.reshape(w_all[n].shape) for t in res)
        return raw[1]

    small = _pack_small(g, loss)
    small_rows8 = -(-(small.shape[0] + N_META) // 8) * 8
    small_h = _exchange_start([_pad_rows(jnp.concatenate([g["meta"], small], axis=0), small_rows8)], kind="all",
                              name="reduce_small_start")
    keys_a = list(started["parts_a"])
    mine_a, theirs_a = _exchange_wait(started["swap_a"], small_h[4], kind="sibling", name="swap_cores_a_wait")
    cut = len(started["ffn"][0])
    done = update(keys_a[:cut], mine_a[:cut], theirs_a[:cut])
    parts_b = partial_sums(("input",), done)
    swap_b = _exchange_start(list(parts_b.values()), kind="sibling", name="swap_cores_b_start")
    done = update(keys_a[cut:], mine_a[cut:], theirs_a[cut:], after=swap_b[4])
    small_own, small_got = _exchange_wait(small_h, done, kind="all", name="reduce_small_wait")
    reduced = _sum_all_devices(small_own[0], small_got[0], name="reduce_small_sum")
    update(list(parts_b), *_exchange_wait(swap_b, reduced, kind="sibling", name="swap_cores_b_wait"))
    g_meta = lax.dynamic_slice_in_dim(reduced[:N_META], chip * meta_cols, meta_cols, axis=1)
    (loss_row, loss_lane, _, _), = _small_layout()[0]["loss"]
    loss_total = reduced[N_META + loss_row, loss_lane]

    small_of = dict(norm_mix_g="norm_mix_g", b_in="b_in", attn_sinks="sinks", rwkv_mix="mix", rwkv_w0="w0",
                    rwkv_a0="a0", rwkv_k_k="k_k", rwkv_k_a="k_a", rwkv_r_k="r_k", rwkv_ln_w="ln_w",
                    rwkv_ln_b="ln_b", norm_ffn_g="norm_ffn_g", norm_final_g="norm_final_g")
    as_rows = [{k: src[n].reshape(1, -1) for n, k in small_of.items()} for src in (w_all, m_all, v_all)]
    meta_in = (meta_tokens, g_meta, m_meta_tokens, v_meta_tokens)
    small_out, meta_out = _adamw_small(reduced, N_META, *as_rows, meta_in, name="adamw_small")
    grads["meta_tokens"] = g_meta
    delta["meta_tokens"], new_m["meta_tokens"], new_v["meta_tokens"] = meta_out
    for n, k in small_of.items():
        grads[n], delta[n], new_m[n], new_v[n] = (t.reshape(w_all[n].shape) for t in small_out[k])

    return (loss_total, dx.reshape(x.shape), *[grads[n] for n in names], *[delta[n] for n in names],
            *[new_m[n] for n in names], *[new_v[n] for n in names])
```
